```python
import math
import jax, jax.numpy as jnp
from jax import lax
import numpy as np

D_MODEL = 1024
BATCH = 16
SEQ = 4096
DEPTH = 1

N_META = 16
S5_WIDTH = D_MODEL // 2
S5_GROUP = 16
S5_GROUPS = S5_WIDTH // S5_GROUP
S5_STATE = 64
DT_MIN = 1e-3
DT_MAX = 1e-1
M_HEADS = 4
M_DK = D_MODEL // 8
M_DV = D_MODEL // 4
M_QK_WIDTH = M_HEADS * M_DK
M_V_WIDTH = M_HEADS * M_DV
M_CHUNK = 64
CONV_WIDTH = 4
D_FF = 4 * D_MODEL
ALPHA = (2.0 * DEPTH) ** 0.25
BETA = (8.0 * DEPTH) ** -0.25
LN_EPS = 1e-5
IN_SIZES = (S5_WIDTH, M_QK_WIDTH, M_QK_WIDTH, M_V_WIDTH, M_V_WIDTH, M_HEADS, M_HEADS, D_MODEL, D_MODEL)
IN_WIDTH = sum(IN_SIZES)
F_GATE_OFFSET = S5_WIDTH + 2 * M_QK_WIDTH + 2 * M_V_WIDTH + M_HEADS

kernel_name = "hybrid_s5_mlstm_gated_block"


def _layer_norm(x, g, b):
    xf = x.astype(jnp.float32)
    mu = jnp.mean(xf, axis=-1, keepdims=True)
    var = jnp.mean(jnp.square(xf - mu), axis=-1, keepdims=True)
    y = (xf - mu) * lax.rsqrt(var + LN_EPS)
    return (y * g.astype(jnp.float32) + b.astype(jnp.float32)).astype(x.dtype)


def _head_norm(h, g):
    hf = h.astype(jnp.float32)
    mu = jnp.mean(hf, axis=-1, keepdims=True)
    var = jnp.mean(jnp.square(hf - mu), axis=-1, keepdims=True)
    return (hf - mu) * lax.rsqrt(var + LN_EPS) * g.astype(jnp.float32).reshape(M_HEADS, M_DV)


def _split_columns(p):
    parts, start = [], 0
    for size in IN_SIZES:
        parts.append(p[..., start:start + size])
        start += size
    return parts


def _causal_depthwise_conv(x, w, b):
    c = x.shape[-1]
    y = lax.conv_general_dilated(
        x, w[:, None, :].astype(x.dtype), window_strides=(1,),
        padding=((CONV_WIDTH - 1, 0),), dimension_numbers=("NWC", "WIO", "NWC"),
        feature_group_count=c)
    return y + b


def _linear_recurrence_combine(left, right):
    a_l, b_l = left
    a_r, b_r = right
    return a_l * a_r, a_r * b_l + b_r


def _s5_mixer(u, lam_re, lam_im, log_dt, b_re, b_im, c_re, c_im, d_skip):
    f32 = jnp.float32
    bsz, length, _ = u.shape
    uf = u.astype(f32).reshape(bsz, length, S5_GROUPS, S5_GROUP)
    lam = lax.complex(lam_re.astype(f32), lam_im.astype(f32))
    dt = jnp.exp(log_dt.astype(f32))[:, None]
    lam_bar = jnp.exp(lam * dt)
    b_mat = lax.complex(b_re.astype(f32), b_im.astype(f32))
    b_bar = ((lam_bar - 1.0) / lam)[..., None] * b_mat
    bu = jnp.einsum("gph,blgh->blgp", b_bar, uf.astype(jnp.complex64))
    a = jnp.broadcast_to(lam_bar, (1, length, S5_GROUPS, S5_STATE))
    _, state = lax.associative_scan(_linear_recurrence_combine, (a, bu), axis=1)
    c_mat = lax.complex(c_re.astype(f32), c_im.astype(f32))
    y = jnp.real(jnp.einsum("ghp,blgp->blgh", c_mat, state))
    y = y + d_skip.astype(f32).reshape(S5_GROUPS, S5_GROUP) * uf
    return y.reshape(bsz, length, S5_WIDTH)


def _mlstm_mixer(q, k, v, i_pre, f_pre):
    f32 = jnp.float32
    bsz, length = q.shape[:2]
    n_pad = M_CHUNK - N_META
    n_chunks = (length + n_pad) // M_CHUNK

    def to_chunks(t, fill):
        t = t.astype(f32)
        t = jnp.pad(t, ((0, 0), (n_pad, 0)) + ((0, 0),) * (t.ndim - 2), constant_values=fill)
        t = t.reshape((bsz, n_chunks, M_CHUNK) + t.shape[2:])
        return jnp.moveaxis(t, (1, 3), (0, 2))

    qc = to_chunks(q, 0.0)
    kc = to_chunks(k * (M_DK ** -0.5), 0.0)
    vc = to_chunks(v, 0.0)
    log_i = to_chunks(i_pre, -jnp.inf)
    log_f = to_chunks(jax.nn.log_sigmoid(f_pre.astype(f32)), 0.0)
    causal = jnp.tril(jnp.ones((M_CHUNK, M_CHUNK), dtype=bool))

    def chunk_step(carry, inp):
        c_st, n_st, m_st = carry
        q_c, k_c, v_c, li_c, lf_c = inp
        b = jnp.cumsum(lf_c, axis=-1)
        d_mat = jnp.where(causal, b[..., :, None] - b[..., None, :] + li_c[..., None, :], -jnp.inf)
        m_inter = b + m_st[..., None]
        m_row = jnp.maximum(m_inter, jnp.max(d_mat, axis=-1))
        w_intra = jnp.exp(d_mat - m_row[..., None])
        w_inter = jnp.exp(m_inter - m_row)
        s = jnp.einsum("bhsd,bhjd->bhsj", q_c, k_c) * w_intra
        num = (jnp.einsum("bhsj,bhje->bhse", s, v_c)
               + w_inter[..., None] * jnp.einsum("bhsd,bhde->bhse", q_c, c_st))
        den = jnp.sum(s, axis=-1) + w_inter * jnp.einsum("bhsd,bhd->bhs", q_c, n_st)
        h_c = num / jnp.maximum(jnp.abs(den), jnp.exp(-m_row))[..., None]
        b_last = b[..., -1]
        g_log = b_last[..., None] - b + li_c
        m_new = jnp.maximum(b_last + m_st, jnp.max(g_log, axis=-1))
        w_k = jnp.exp(g_log - m_new[..., None])
        decay = jnp.exp(b_last + m_st - m_new)
        c_new = decay[..., None, None] * c_st + jnp.einsum("bhj,bhjd,bhje->bhde", w_k, k_c, v_c)
        n_new = decay[..., None] * n_st + jnp.einsum("bhj,bhjd->bhd", w_k, k_c)
        return (c_new, n_new, m_new), h_c

    init = (jnp.zeros((bsz, M_HEADS, M_DK, M_DV), f32),
            jnp.zeros((bsz, M_HEADS, M_DK), f32),
            jnp.zeros((bsz, M_HEADS), f32))
    _, h = lax.scan(chunk_step, init, (qc, kc, vc, log_i, log_f))
    h = jnp.moveaxis(h, (0, 2), (1, 3)).reshape(bsz, n_chunks * M_CHUNK, M_HEADS, M_DV)
    return h[:, n_pad:]


def _fwd_setup_inputs(seed: int = 0) -> dict:
    key = jax.random.key(seed)
    ks = jax.random.split(key, 27)
    f32 = jnp.float32

    def nrm(k, shape, scale):
        return scale * jax.random.normal(k, shape, f32)

    b_in = nrm(ks[5], (DEPTH, IN_WIDTH), 0.02)
    b_in = b_in.at[:, F_GATE_OFFSET:F_GATE_OFFSET + M_HEADS].add(jnp.linspace(3.0, 6.0, M_HEADS))
    s5_lambda_re = -0.5 + nrm(ks[8], (DEPTH, S5_GROUPS, S5_STATE), 0.01)
    s5_lambda_im = jnp.pi * jnp.arange(S5_STATE, dtype=f32) + nrm(ks[9], (DEPTH, S5_GROUPS, S5_STATE), 0.01)
    s5_log_dt = jax.random.uniform(ks[10], (DEPTH, S5_GROUPS), f32, math.log(DT_MIN), math.log(DT_MAX))
    return {
        "x": nrm(ks[0], (BATCH, SEQ, D_MODEL), 1.0),
        "meta_tokens": nrm(ks[1], (N_META, D_MODEL), 1.0),
        "ln0_g": 1.0 + nrm(ks[2], (D_MODEL,), 0.05),
        "ln0_b": nrm(ks[3], (D_MODEL,), 0.02),
        "w_in": nrm(ks[4], (DEPTH, D_MODEL, IN_WIDTH), D_MODEL ** -0.5),
        "b_in": b_in,
        "qk_conv_w": nrm(ks[6], (DEPTH, CONV_WIDTH, 2 * M_QK_WIDTH), CONV_WIDTH ** -0.5),
        "qk_conv_b": nrm(ks[7], (DEPTH, 2 * M_QK_WIDTH), 0.02),
        "s5_lambda_re": s5_lambda_re,
        "s5_lambda_im": s5_lambda_im,
        "s5_log_dt": s5_log_dt,
        "s5_b_re": nrm(ks[11], (DEPTH, S5_GROUPS, S5_STATE, S5_GROUP), (2.0 * S5_GROUP) ** -0.5),
        "s5_b_im": nrm(ks[12], (DEPTH, S5_GROUPS, S5_STATE, S5_GROUP), (2.0 * S5_GROUP) ** -0.5),
        "s5_c_re": nrm(ks[13], (DEPTH, S5_GROUPS, S5_GROUP, S5_STATE), S5_STATE ** -0.5),
        "s5_c_im": nrm(ks[14], (DEPTH, S5_GROUPS, S5_GROUP, S5_STATE), S5_STATE ** -0.5),
        "s5_d": nrm(ks[15], (DEPTH, S5_WIDTH), 1.0),
        "s5_w_glu": nrm(ks[16], (DEPTH, S5_WIDTH, 2 * D_MODEL), S5_WIDTH ** -0.5),
        "m_norm_g": 1.0 + nrm(ks[17], (DEPTH, M_V_WIDTH), 0.05),
        "m_w_out": nrm(ks[18], (DEPTH, M_V_WIDTH, D_MODEL), M_V_WIDTH ** -0.5),
        "w_o": nrm(ks[19], (DEPTH, D_MODEL, D_MODEL), BETA * D_MODEL ** -0.5),
        "ln1_g": 1.0 + nrm(ks[20], (DEPTH, D_MODEL), 0.05),
        "ln1_b": nrm(ks[21], (DEPTH, D_MODEL), 0.02),
        "w_up": nrm(ks[22], (DEPTH, D_MODEL, D_FF), D_MODEL ** -0.5),
        "b_up": nrm(ks[23], (DEPTH, D_FF), 0.02),
        "w_down": nrm(ks[24], (DEPTH, D_FF, D_MODEL), BETA * D_FF ** -0.5),
        "ln2_g": 1.0 + nrm(ks[25], (DEPTH, D_MODEL), 0.05),
        "ln2_b": nrm(ks[26], (DEPTH, D_MODEL), 0.02),
    }


def _fwd_reference(x, meta_tokens, ln0_g, ln0_b, w_in, b_in, qk_conv_w, qk_conv_b,
              s5_lambda_re, s5_lambda_im, s5_log_dt, s5_b_re, s5_b_im, s5_c_re, s5_c_im,
              s5_d, s5_w_glu, m_norm_g, m_w_out, w_o, ln1_g, ln1_b, w_up, b_up, w_down,
              ln2_g, ln2_b):
    bsz = x.shape[0]
    meta = jnp.broadcast_to(meta_tokens[None].astype(x.dtype), (bsz, N_META, D_MODEL))
    h = _layer_norm(jnp.concatenate([meta, x], axis=1), ln0_g, ln0_b)
    length = h.shape[1]
    for layer in range(DEPTH):
        p = h @ w_in[layer] + b_in[layer]
        u_s5, q, k, v, o_pre, i_pre, f_pre, g_s5, g_m = _split_columns(p)
        y_s5 = _s5_mixer(u_s5, s5_lambda_re[layer], s5_lambda_im[layer], s5_log_dt[layer],
                         s5_b_re[layer], s5_b_im[layer], s5_c_re[layer], s5_c_im[layer], s5_d[layer])
        z = jax.nn.gelu(y_s5).astype(h.dtype) @ s5_w_glu[layer]
        y_s5 = z[..., :D_MODEL] * jax.nn.sigmoid(z[..., D_MODEL:])
        qk = jax.nn.silu(_causal_depthwise_conv(jnp.concatenate([q, k], axis=-1),
                                                qk_conv_w[layer], qk_conv_b[layer]))
        q_h = qk[..., :M_QK_WIDTH].reshape(bsz, length, M_HEADS, M_DK)
        k_h = qk[..., M_QK_WIDTH:].reshape(bsz, length, M_HEADS, M_DK)
        v_h = v.reshape(bsz, length, M_HEADS, M_DV)
        hm = _mlstm_mixer(q_h, k_h, v_h, i_pre, f_pre)
        hm = _head_norm(hm, m_norm_g[layer]).reshape(bsz, length, M_V_WIDTH).astype(h.dtype)
        y_m = (jax.nn.sigmoid(o_pre) * hm) @ m_w_out[layer]
        mix = jax.nn.sigmoid(g_s5) * y_s5 + jax.nn.sigmoid(g_m) * y_m
        h = _layer_norm(ALPHA * h + mix @ w_o[layer], ln1_g[layer], ln1_b[layer])
        ff = jnp.square(jax.nn.relu(h @ w_up[layer] + b_up[layer])) @ w_down[layer]
        h = _layer_norm(ALPHA * h + ff, ln2_g[layer], ln2_b[layer])
    return h[:, N_META:]


import jax as _jax
import jax.numpy as _jnp

TWIN_FORMAT = 'train_step'
FWD_PARAMS = ['x', 'meta_tokens', 'ln0_g', 'ln0_b', 'w_in', 'b_in', 'qk_conv_w', 'qk_conv_b', 's5_lambda_re', 's5_lambda_im', 's5_log_dt', 's5_b_re', 's5_b_im', 's5_c_re', 's5_c_im', 's5_d', 's5_w_glu', 'm_norm_g', 'm_w_out', 'w_o', 'ln1_g', 'ln1_b', 'w_up', 'b_up', 'w_down', 'ln2_g', 'ln2_b']
TWIN_WEIGHTS = ['meta_tokens', 'ln0_g', 'ln0_b', 'w_in', 'b_in', 'qk_conv_w', 'qk_conv_b', 's5_lambda_re', 's5_lambda_im', 's5_log_dt', 's5_b_re', 's5_b_im', 's5_c_re', 's5_c_im', 's5_d', 's5_w_glu', 'm_norm_g', 'm_w_out', 'w_o', 'ln1_g', 'ln1_b', 'w_up', 'b_up', 'w_down', 'ln2_g', 'ln2_b']
TWIN_DIFF_INPUT = 'x'
TWIN_INPUTS = ['x', 'meta_tokens', 'ln0_g', 'ln0_b', 'w_in', 'b_in', 'qk_conv_w', 'qk_conv_b', 's5_lambda_re', 's5_lambda_im', 's5_log_dt', 's5_b_re', 's5_b_im', 's5_c_re', 's5_c_im', 's5_d', 's5_w_glu', 'm_norm_g', 'm_w_out', 'w_o', 'ln1_g', 'ln1_b', 'w_up', 'b_up', 'w_down', 'ln2_g', 'ln2_b', 'loss_target', 'm_meta_tokens', 'm_ln0_g', 'm_ln0_b', 'm_w_in', 'm_b_in', 'm_qk_conv_w', 'm_qk_conv_b', 'm_s5_lambda_re', 'm_s5_lambda_im', 'm_s5_log_dt', 'm_s5_b_re', 'm_s5_b_im', 'm_s5_c_re', 'm_s5_c_im', 'm_s5_d', 'm_s5_w_glu', 'm_m_norm_g', 'm_m_w_out', 'm_w_o', 'm_ln1_g', 'm_ln1_b', 'm_w_up', 'm_b_up', 'm_w_down', 'm_ln2_g', 'm_ln2_b', 'v_meta_tokens', 'v_ln0_g', 'v_ln0_b', 'v_w_in', 'v_b_in', 'v_qk_conv_w', 'v_qk_conv_b', 'v_s5_lambda_re', 'v_s5_lambda_im', 'v_s5_log_dt', 'v_s5_b_re', 'v_s5_b_im', 'v_s5_c_re', 'v_s5_c_im', 'v_s5_d', 'v_s5_w_glu', 'v_m_norm_g', 'v_m_w_out', 'v_w_o', 'v_ln1_g', 'v_ln1_b', 'v_w_up', 'v_b_up', 'v_w_down', 'v_ln2_g', 'v_ln2_b']
TWIN_OUTPUTS = ['loss', 'grad_x', 'grad_meta_tokens', 'grad_ln0_g', 'grad_ln0_b', 'grad_w_in', 'grad_b_in', 'grad_qk_conv_w', 'grad_qk_conv_b', 'grad_s5_lambda_re', 'grad_s5_lambda_im', 'grad_s5_log_dt', 'grad_s5_b_re', 'grad_s5_b_im', 'grad_s5_c_re', 'grad_s5_c_im', 'grad_s5_d', 'grad_s5_w_glu', 'grad_m_norm_g', 'grad_m_w_out', 'grad_w_o', 'grad_ln1_g', 'grad_ln1_b', 'grad_w_up', 'grad_b_up', 'grad_w_down', 'grad_ln2_g', 'grad_ln2_b', 'delta_meta_tokens', 'delta_ln0_g', 'delta_ln0_b', 'delta_w_in', 'delta_b_in', 'delta_qk_conv_w', 'delta_qk_conv_b', 'delta_s5_lambda_re', 'delta_s5_lambda_im', 'delta_s5_log_dt', 'delta_s5_b_re', 'delta_s5_b_im', 'delta_s5_c_re', 'delta_s5_c_im', 'delta_s5_d', 'delta_s5_w_glu', 'delta_m_norm_g', 'delta_m_w_out', 'delta_w_o', 'delta_ln1_g', 'delta_ln1_b', 'delta_w_up', 'delta_b_up', 'delta_w_down', 'delta_ln2_g', 'delta_ln2_b', 'new_m_meta_tokens', 'new_m_ln0_g', 'new_m_ln0_b', 'new_m_w_in', 'new_m_b_in', 'new_m_qk_conv_w', 'new_m_qk_conv_b', 'new_m_s5_lambda_re', 'new_m_s5_lambda_im', 'new_m_s5_log_dt', 'new_m_s5_b_re', 'new_m_s5_b_im', 'new_m_s5_c_re', 'new_m_s5_c_im', 'new_m_s5_d', 'new_m_s5_w_glu', 'new_m_m_norm_g', 'new_m_m_w_out', 'new_m_w_o', 'new_m_ln1_g', 'new_m_ln1_b', 'new_m_w_up', 'new_m_b_up', 'new_m_w_down', 'new_m_ln2_g', 'new_m_ln2_b', 'new_v_meta_tokens', 'new_v_ln0_g', 'new_v_ln0_b', 'new_v_w_in', 'new_v_b_in', 'new_v_qk_conv_w', 'new_v_qk_conv_b', 'new_v_s5_lambda_re', 'new_v_s5_lambda_im', 'new_v_s5_log_dt', 'new_v_s5_b_re', 'new_v_s5_b_im', 'new_v_s5_c_re', 'new_v_s5_c_im', 'new_v_s5_d', 'new_v_s5_w_glu', 'new_v_m_norm_g', 'new_v_m_w_out', 'new_v_w_o', 'new_v_ln1_g', 'new_v_ln1_b', 'new_v_w_up', 'new_v_b_up', 'new_v_w_down', 'new_v_ln2_g', 'new_v_ln2_b']
TWIN_LEAF_KINDS = {'loss': 'loss', 'grad_x': 'grad_x', 'grad_meta_tokens': 'grad_w', 'grad_ln0_g': 'grad_w', 'grad_ln0_b': 'grad_w', 'grad_w_in': 'grad_w', 'grad_b_in': 'grad_w', 'grad_qk_conv_w': 'grad_w', 'grad_qk_conv_b': 'grad_w', 'grad_s5_lambda_re': 'grad_w', 'grad_s5_lambda_im': 'grad_w', 'grad_s5_log_dt': 'grad_w', 'grad_s5_b_re': 'grad_w', 'grad_s5_b_im': 'grad_w', 'grad_s5_c_re': 'grad_w', 'grad_s5_c_im': 'grad_w', 'grad_s5_d': 'grad_w', 'grad_s5_w_glu': 'grad_w', 'grad_m_norm_g': 'grad_w', 'grad_m_w_out': 'grad_w', 'grad_w_o': 'grad_w', 'grad_ln1_g': 'grad_w', 'grad_ln1_b': 'grad_w', 'grad_w_up': 'grad_w', 'grad_b_up': 'grad_w', 'grad_w_down': 'grad_w', 'grad_ln2_g': 'grad_w', 'grad_ln2_b': 'grad_w', 'delta_meta_tokens': 'delta_w', 'delta_ln0_g': 'delta_w', 'delta_ln0_b': 'delta_w', 'delta_w_in': 'delta_w', 'delta_b_in': 'delta_w', 'delta_qk_conv_w': 'delta_w', 'delta_qk_conv_b': 'delta_w', 'delta_s5_lambda_re': 'delta_w', 'delta_s5_lambda_im': 'delta_w', 'delta_s5_log_dt': 'delta_w', 'delta_s5_b_re': 'delta_w', 'delta_s5_b_im': 'delta_w', 'delta_s5_c_re': 'delta_w', 'delta_s5_c_im': 'delta_w', 'delta_s5_d': 'delta_w', 'delta_s5_w_glu': 'delta_w', 'delta_m_norm_g': 'delta_w', 'delta_m_w_out': 'delta_w', 'delta_w_o': 'delta_w', 'delta_ln1_g': 'delta_w', 'delta_ln1_b': 'delta_w', 'delta_w_up': 'delta_w', 'delta_b_up': 'delta_w', 'delta_w_down': 'delta_w', 'delta_ln2_g': 'delta_w', 'delta_ln2_b': 'delta_w', 'new_m_meta_tokens': 'new_m', 'new_m_ln0_g': 'new_m', 'new_m_ln0_b': 'new_m', 'new_m_w_in': 'new_m', 'new_m_b_in': 'new_m', 'new_m_qk_conv_w': 'new_m', 'new_m_qk_conv_b': 'new_m', 'new_m_s5_lambda_re': 'new_m', 'new_m_s5_lambda_im': 'new_m', 'new_m_s5_log_dt': 'new_m', 'new_m_s5_b_re': 'new_m', 'new_m_s5_b_im': 'new_m', 'new_m_s5_c_re': 'new_m', 'new_m_s5_c_im': 'new_m', 'new_m_s5_d': 'new_m', 'new_m_s5_w_glu': 'new_m', 'new_m_m_norm_g': 'new_m', 'new_m_m_w_out': 'new_m', 'new_m_w_o': 'new_m', 'new_m_ln1_g': 'new_m', 'new_m_ln1_b': 'new_m', 'new_m_w_up': 'new_m', 'new_m_b_up': 'new_m', 'new_m_w_down': 'new_m', 'new_m_ln2_g': 'new_m', 'new_m_ln2_b': 'new_m', 'new_v_meta_tokens': 'new_v', 'new_v_ln0_g': 'new_v', 'new_v_ln0_b': 'new_v', 'new_v_w_in': 'new_v', 'new_v_b_in': 'new_v', 'new_v_qk_conv_w': 'new_v', 'new_v_qk_conv_b': 'new_v', 'new_v_s5_lambda_re': 'new_v', 'new_v_s5_lambda_im': 'new_v', 'new_v_s5_log_dt': 'new_v', 'new_v_s5_b_re': 'new_v', 'new_v_s5_b_im': 'new_v', 'new_v_s5_c_re': 'new_v', 'new_v_s5_c_im': 'new_v', 'new_v_s5_d': 'new_v', 'new_v_s5_w_glu': 'new_v', 'new_v_m_norm_g': 'new_v', 'new_v_m_w_out': 'new_v', 'new_v_w_o': 'new_v', 'new_v_ln1_g': 'new_v', 'new_v_ln1_b': 'new_v', 'new_v_w_up': 'new_v', 'new_v_b_up': 'new_v', 'new_v_w_down': 'new_v', 'new_v_ln2_g': 'new_v', 'new_v_ln2_b': 'new_v'}


def _forward(args):
    return _fwd_reference(*[args[k] for k in FWD_PARAMS])


def _output_shape():
    out = _jax.eval_shape(lambda: _forward(_fwd_setup_inputs(0)))
    return out.shape, out.dtype

N_MICROBATCH = 1
ADAM_LR = 0.001
ADAM_B1 = 0.9
ADAM_B2 = 0.999
ADAM_EPS = 1e-08
ADAM_WD = 0.01
ADAM_STEP = 10
PER_EXAMPLE_BATCH_AXIS = {'x': 0, 'loss_target': 0}
SHARED_INPUTS = []
_WEIGHT_DTYPES = {'meta_tokens': _jnp.float32, 'ln0_g': _jnp.float32, 'ln0_b': _jnp.float32, 'w_in': _jnp.float32, 'b_in': _jnp.float32, 'qk_conv_w': _jnp.float32, 'qk_conv_b': _jnp.float32, 's5_lambda_re': _jnp.float32, 's5_lambda_im': _jnp.float32, 's5_log_dt': _jnp.float32, 's5_b_re': _jnp.float32, 's5_b_im': _jnp.float32, 's5_c_re': _jnp.float32, 's5_c_im': _jnp.float32, 's5_d': _jnp.float32, 's5_w_glu': _jnp.float32, 'm_norm_g': _jnp.float32, 'm_w_out': _jnp.float32, 'w_o': _jnp.float32, 'ln1_g': _jnp.float32, 'ln1_b': _jnp.float32, 'w_up': _jnp.float32, 'b_up': _jnp.float32, 'w_down': _jnp.float32, 'ln2_g': _jnp.float32, 'ln2_b': _jnp.float32}
MOMENT_SCALE = {'meta_tokens': 3.799121e-03, 'ln0_g': 4.082517e+00, 'ln0_b': 1.624819e+00, 'w_in': 3.326355e-02, 'b_in': 5.165029e-01, 'qk_conv_w': 3.918657e-02, 'qk_conv_b': 3.921863e-02, 's5_lambda_re': 2.973472e-03, 's5_lambda_im': 2.821989e-03, 's5_log_dt': 2.011336e+00, 's5_b_re': 1.636552e-03, 's5_b_im': 1.625707e-03, 's5_c_re': 2.485235e-03, 's5_c_im': 2.438368e-03, 's5_d': 5.973862e-02, 's5_w_glu': 3.052815e-02, 'm_norm_g': 4.323031e-02, 'm_w_out': 4.533721e-02, 'w_o': 9.929637e-02, 'ln1_g': 4.334055e+00, 'ln1_b': 1.133536e+00, 'w_up': 7.763356e-02, 'b_up': 1.969439e-01, 'w_down': 3.980822e-01, 'ln2_g': 6.442747e+01, 'ln2_b': 1.365232e+01}


def _to_microbatches(a, axis):
    t = _jnp.moveaxis(a, axis, 0)
    t = t.reshape((N_MICROBATCH, t.shape[0] // N_MICROBATCH) + t.shape[1:])
    return _jnp.moveaxis(t, 1, axis + 1)


def setup_inputs(seed: int = 0) -> dict:
    inp = _fwd_setup_inputs(seed)
    key = _jax.random.fold_in(_jax.random.key(seed), 7919)
    shape, _ = _output_shape()
    out = dict(inp)
    out["loss_target"] = _jax.random.normal(_jax.random.fold_in(key, 0), shape, _jnp.float32)
    for i, name in enumerate(TWIN_WEIGHTS):
        w = inp[name].astype(_jnp.float32)
        if MOMENT_SCALE is None:
            s = _jnp.sqrt(_jnp.mean(_jnp.square(w)) + 1e-30)
        else:
            s = MOMENT_SCALE[name]
        km, kv = _jax.random.split(_jax.random.fold_in(key, i + 1))
        out[name] = w
        out["m_" + name] = s * _jax.random.normal(km, w.shape, _jnp.float32)
        out["v_" + name] = (s * s) * _jax.random.uniform(kv, w.shape, _jnp.float32, 0.5, 1.5)
    if N_MICROBATCH > 1:
        for name, axis in PER_EXAMPLE_BATCH_AXIS.items():
            out[name] = _to_microbatches(out[name], axis)
    return {'x': out['x'], 'meta_tokens': out['meta_tokens'], 'ln0_g': out['ln0_g'], 'ln0_b': out['ln0_b'], 'w_in': out['w_in'], 'b_in': out['b_in'], 'qk_conv_w': out['qk_conv_w'], 'qk_conv_b': out['qk_conv_b'], 's5_lambda_re': out['s5_lambda_re'], 's5_lambda_im': out['s5_lambda_im'], 's5_log_dt': out['s5_log_dt'], 's5_b_re': out['s5_b_re'], 's5_b_im': out['s5_b_im'], 's5_c_re': out['s5_c_re'], 's5_c_im': out['s5_c_im'], 's5_d': out['s5_d'], 's5_w_glu': out['s5_w_glu'], 'm_norm_g': out['m_norm_g'], 'm_w_out': out['m_w_out'], 'w_o': out['w_o'], 'ln1_g': out['ln1_g'], 'ln1_b': out['ln1_b'], 'w_up': out['w_up'], 'b_up': out['b_up'], 'w_down': out['w_down'], 'ln2_g': out['ln2_g'], 'ln2_b': out['ln2_b'], 'loss_target': out['loss_target'], 'm_meta_tokens': out['m_meta_tokens'], 'm_ln0_g': out['m_ln0_g'], 'm_ln0_b': out['m_ln0_b'], 'm_w_in': out['m_w_in'], 'm_b_in': out['m_b_in'], 'm_qk_conv_w': out['m_qk_conv_w'], 'm_qk_conv_b': out['m_qk_conv_b'], 'm_s5_lambda_re': out['m_s5_lambda_re'], 'm_s5_lambda_im': out['m_s5_lambda_im'], 'm_s5_log_dt': out['m_s5_log_dt'], 'm_s5_b_re': out['m_s5_b_re'], 'm_s5_b_im': out['m_s5_b_im'], 'm_s5_c_re': out['m_s5_c_re'], 'm_s5_c_im': out['m_s5_c_im'], 'm_s5_d': out['m_s5_d'], 'm_s5_w_glu': out['m_s5_w_glu'], 'm_m_norm_g': out['m_m_norm_g'], 'm_m_w_out': out['m_m_w_out'], 'm_w_o': out['m_w_o'], 'm_ln1_g': out['m_ln1_g'], 'm_ln1_b': out['m_ln1_b'], 'm_w_up': out['m_w_up'], 'm_b_up': out['m_b_up'], 'm_w_down': out['m_w_down'], 'm_ln2_g': out['m_ln2_g'], 'm_ln2_b': out['m_ln2_b'], 'v_meta_tokens': out['v_meta_tokens'], 'v_ln0_g': out['v_ln0_g'], 'v_ln0_b': out['v_ln0_b'], 'v_w_in': out['v_w_in'], 'v_b_in': out['v_b_in'], 'v_qk_conv_w': out['v_qk_conv_w'], 'v_qk_conv_b': out['v_qk_conv_b'], 'v_s5_lambda_re': out['v_s5_lambda_re'], 'v_s5_lambda_im': out['v_s5_lambda_im'], 'v_s5_log_dt': out['v_s5_log_dt'], 'v_s5_b_re': out['v_s5_b_re'], 'v_s5_b_im': out['v_s5_b_im'], 'v_s5_c_re': out['v_s5_c_re'], 'v_s5_c_im': out['v_s5_c_im'], 'v_s5_d': out['v_s5_d'], 'v_s5_w_glu': out['v_s5_w_glu'], 'v_m_norm_g': out['v_m_norm_g'], 'v_m_w_out': out['v_m_w_out'], 'v_w_o': out['v_w_o'], 'v_ln1_g': out['v_ln1_g'], 'v_ln1_b': out['v_ln1_b'], 'v_w_up': out['v_w_up'], 'v_b_up': out['v_b_up'], 'v_w_down': out['v_w_down'], 'v_ln2_g': out['v_ln2_g'], 'v_ln2_b': out['v_ln2_b']}


def _loss(weights, diff, rest, loss_target):
    with _jax.named_scope("forward"):
        args = {**rest, TWIN_DIFF_INPUT: diff, **{k: w.astype(_WEIGHT_DTYPES[k]) for k, w in weights.items()}}
        y = _forward(args)
    with _jax.named_scope("loss_head"):
        err = _jnp.square(y.astype(_jnp.float32) - loss_target)
        return 0.5 * _jnp.sum(_jnp.mean(err, axis=-1)) if err.ndim else 0.5 * err


def _adamw(w, g, m, v):
    m = ADAM_B1 * m + (1.0 - ADAM_B1) * g
    v = ADAM_B2 * v + (1.0 - ADAM_B2) * _jnp.square(g)
    m_hat = m / (1.0 - ADAM_B1 ** ADAM_STEP)
    v_hat = v / (1.0 - ADAM_B2 ** ADAM_STEP)
    delta = -ADAM_LR * (m_hat / (_jnp.sqrt(v_hat) + ADAM_EPS) + ADAM_WD * w)
    return delta, m, v


def reference(x, meta_tokens, ln0_g, ln0_b, w_in, b_in, qk_conv_w, qk_conv_b, s5_lambda_re, s5_lambda_im, s5_log_dt, s5_b_re, s5_b_im, s5_c_re, s5_c_im, s5_d, s5_w_glu, m_norm_g, m_w_out, w_o, ln1_g, ln1_b, w_up, b_up, w_down, ln2_g, ln2_b, loss_target, m_meta_tokens, m_ln0_g, m_ln0_b, m_w_in, m_b_in, m_qk_conv_w, m_qk_conv_b, m_s5_lambda_re, m_s5_lambda_im, m_s5_log_dt, m_s5_b_re, m_s5_b_im, m_s5_c_re, m_s5_c_im, m_s5_d, m_s5_w_glu, m_m_norm_g, m_m_w_out, m_w_o, m_ln1_g, m_ln1_b, m_w_up, m_b_up, m_w_down, m_ln2_g, m_ln2_b, v_meta_tokens, v_ln0_g, v_ln0_b, v_w_in, v_b_in, v_qk_conv_w, v_qk_conv_b, v_s5_lambda_re, v_s5_lambda_im, v_s5_log_dt, v_s5_b_re, v_s5_b_im, v_s5_c_re, v_s5_c_im, v_s5_d, v_s5_w_glu, v_m_norm_g, v_m_w_out, v_w_o, v_ln1_g, v_ln1_b, v_w_up, v_b_up, v_w_down, v_ln2_g, v_ln2_b):
    given = dict(x=x, meta_tokens=meta_tokens, ln0_g=ln0_g, ln0_b=ln0_b, w_in=w_in, b_in=b_in, qk_conv_w=qk_conv_w, qk_conv_b=qk_conv_b, s5_lambda_re=s5_lambda_re, s5_lambda_im=s5_lambda_im, s5_log_dt=s5_log_dt, s5_b_re=s5_b_re, s5_b_im=s5_b_im, s5_c_re=s5_c_re, s5_c_im=s5_c_im, s5_d=s5_d, s5_w_glu=s5_w_glu, m_norm_g=m_norm_g, m_w_out=m_w_out, w_o=w_o, ln1_g=ln1_g, ln1_b=ln1_b, w_up=w_up, b_up=b_up, w_down=w_down, ln2_g=ln2_g, ln2_b=ln2_b, loss_target=loss_target, m_meta_tokens=m_meta_tokens, m_ln0_g=m_ln0_g, m_ln0_b=m_ln0_b, m_w_in=m_w_in, m_b_in=m_b_in, m_qk_conv_w=m_qk_conv_w, m_qk_conv_b=m_qk_conv_b, m_s5_lambda_re=m_s5_lambda_re, m_s5_lambda_im=m_s5_lambda_im, m_s5_log_dt=m_s5_log_dt, m_s5_b_re=m_s5_b_re, m_s5_b_im=m_s5_b_im, m_s5_c_re=m_s5_c_re, m_s5_c_im=m_s5_c_im, m_s5_d=m_s5_d, m_s5_w_glu=m_s5_w_glu, m_m_norm_g=m_m_norm_g, m_m_w_out=m_m_w_out, m_w_o=m_w_o, m_ln1_g=m_ln1_g, m_ln1_b=m_ln1_b, m_w_up=m_w_up, m_b_up=m_b_up, m_w_down=m_w_down, m_ln2_g=m_ln2_g, m_ln2_b=m_ln2_b, v_meta_tokens=v_meta_tokens, v_ln0_g=v_ln0_g, v_ln0_b=v_ln0_b, v_w_in=v_w_in, v_b_in=v_b_in, v_qk_conv_w=v_qk_conv_w, v_qk_conv_b=v_qk_conv_b, v_s5_lambda_re=v_s5_lambda_re, v_s5_lambda_im=v_s5_lambda_im, v_s5_log_dt=v_s5_log_dt, v_s5_b_re=v_s5_b_re, v_s5_b_im=v_s5_b_im, v_s5_c_re=v_s5_c_re, v_s5_c_im=v_s5_c_im, v_s5_d=v_s5_d, v_s5_w_glu=v_s5_w_glu, v_m_norm_g=v_m_norm_g, v_m_w_out=v_m_w_out, v_w_o=v_w_o, v_ln1_g=v_ln1_g, v_ln1_b=v_ln1_b, v_w_up=v_w_up, v_b_up=v_b_up, v_w_down=v_w_down, v_ln2_g=v_ln2_g, v_ln2_b=v_ln2_b)
    weights = {n: given[n] for n in TWIN_WEIGHTS}
    shared = {n: given[n] for n in SHARED_INPUTS}
    per_example = {n: given[n] for n in ['x']}
    grad_fn = _jax.value_and_grad(_loss, argnums=(0, 1))

    def one_microbatch(ex, loss_target):
        ex = dict(ex)
        diff = ex.pop(TWIN_DIFF_INPUT)
        return grad_fn(weights, diff, {**shared, **ex}, loss_target)

    if N_MICROBATCH == 1:
        loss, (grad_w, grad_x) = one_microbatch(per_example, given["loss_target"])
    else:
        def body(carry, xs):
            loss_sum, grad_sum = carry
            l_k, (gw_k, gx_k) = one_microbatch(xs[0], xs[1])
            with _jax.named_scope("update"):
                return (loss_sum + l_k, _jax.tree.map(_jnp.add, grad_sum, gw_k)), gx_k

        init = (_jnp.zeros((), _jnp.float32), _jax.tree.map(_jnp.zeros_like, weights))
        (loss, grad_w), grad_x = _jax.lax.scan(body, init, (per_example, given["loss_target"]))
    with _jax.named_scope("update"):
        delta_w, new_m, new_v = {}, {}, {}
        for n in TWIN_WEIGHTS:
            delta_w[n], new_m[n], new_v[n] = _adamw(weights[n], grad_w[n], given["m_" + n], given["v_" + n])
    return (loss, grad_x, *[grad_w[n] for n in TWIN_WEIGHTS], *[delta_w[n] for n in TWIN_WEIGHTS],
            *[new_m[n] for n in TWIN_WEIGHTS], *[new_v[n] for n in TWIN_WEIGHTS])
```

```python
import functools
import math

import jax
import jax.numpy as jnp
from jax import lax
from jax.experimental import pallas as pl
from jax.experimental.pallas import tpu as pltpu

F32 = jnp.float32
BF16 = jnp.bfloat16
HI = lax.Precision.HIGHEST

N_META = 16
M_HEADS = 4
M_CHUNK = 64
PAD = M_CHUNK - N_META
CONV_W = 4
S5_GROUP = 16
S5_STATE = 64
S5_KCH = 4
LN_EPS = 1e-5
ALPHA = 2.0 ** 0.25
NEG = -1e30
ADAM_LR, ADAM_B1, ADAM_B2, ADAM_EPS, ADAM_WD, ADAM_STEP = 0.001, 0.9, 0.999, 1e-08, 0.01, 10

O_OFF, GS_OFF, GM_OFF, V_OFF, Q_OFF, K_OFF, U_OFF, G_OFF, NP = 0, 1024, 2048, 3072, 4096, 4608, 5120, 5632, 5760

NN = ((1,), (0,))
NT = ((1,), (1,))
TN = ((0,), (0,))


def _dot(a, b, dims=NN, prec=None):
    return lax.dot_general(a, b, (dims, ((), ())), preferred_element_type=F32, precision=prec)


def _bf(x):
    return x.astype(BF16)


def _sig(x):
    return 1.0 / (1.0 + jnp.exp(-x))


def _pcall(body, **kw):
    return pl.pallas_call(body, **kw)


def _cp(sem=None, vmem_mb=None):
    kw = {}
    if sem is not None:
        kw["dimension_semantics"] = sem
    if vmem_mb is not None:
        kw["vmem_limit_bytes"] = vmem_mb << 20
    return pltpu.CompilerParams(**kw)


def _row_tile(n, want, mult=16):
    best = None
    for t in range(mult, want + 1, mult):
        if n % t == 0:
            best = t
    assert best is not None, (n, want)
    return best


def _resident(shape):
    nd = len(shape)
    return pl.BlockSpec(shape, lambda *_: (0,) * nd, pipeline_mode=pl.Buffered(1))


def _const(shape):
    nd = len(shape)
    return pl.BlockSpec(shape, lambda *_: (0,) * nd)


def _ln_fwd(x, g, b):
    mu = jnp.mean(x, axis=-1, keepdims=True)
    xc = x - mu
    var = jnp.mean(xc * xc, axis=-1, keepdims=True)
    rstd = lax.rsqrt(var + LN_EPS)
    xhat = xc * rstd
    return xhat * g + b, xhat, rstd


def _ln_bwd(dy, xhat, rstd, g):
    dxh = dy * g
    m1 = jnp.mean(dxh, axis=-1, keepdims=True)
    m2 = jnp.mean(dxh * xhat, axis=-1, keepdims=True)
    return rstd * (dxh - m1 - xhat * m2)


def _colsum(x):
    return jnp.sum(x, axis=0, keepdims=True)


def _to_pad_cols(w):
    u, q, k, v, o, gi, gf, gs, gm = (w[..., 0:512], w[..., 512:1024], w[..., 1024:1536], w[..., 1536:2560],
                                     w[..., 2560:3584], w[..., 3584:3588], w[..., 3588:3592], w[..., 3592:4616],
                                     w[..., 4616:5640])
    z = jnp.zeros(w.shape[:-1] + (NP - G_OFF - 8,), w.dtype)
    return jnp.concatenate([o, gs, gm, v, q, k, u, gi, gf, z], axis=-1)


def _from_pad_cols(w):
    o, gs, gm, v, q, k, u = (w[..., O_OFF:GS_OFF], w[..., GS_OFF:GM_OFF], w[..., GM_OFF:V_OFF], w[..., V_OFF:Q_OFF],
                             w[..., Q_OFF:K_OFF], w[..., K_OFF:U_OFF], w[..., U_OFF:G_OFF])
    gi, gf = w[..., G_OFF:G_OFF + 4], w[..., G_OFF + 4:G_OFF + 8]
    return jnp.concatenate([u, q, k, v, o, gi, gf, gs, gm], axis=-1)


def _ln0_fwd(hin, g, b, lp):
    r, d = hin.shape
    tm = _row_tile(lp, 416)

    def body(x_ref, g_ref, b_ref, o_ref):
        y, _, _ = _ln_fwd(x_ref[...], g_ref[...], b_ref[...])
        o_ref[...] = y

    return _pcall(
        body, name="ln0_fwd", grid=(r // tm,),
        in_specs=[pl.BlockSpec((tm, d), lambda i: (i, 0)), _const((1, d)), _const((1, d))],
        out_specs=pl.BlockSpec((tm, d), lambda i: (i, 0)),
        out_shape=jax.ShapeDtypeStruct((r, d), F32),
        compiler_params=_cp(("parallel",)),
    )(hin, g, b)


def _ln0_bwd(hin, dr1, dpw, g, lp):
    r, d = hin.shape
    tm = _row_tile(lp, 416)
    tps = lp // tm
    assert tm >= PAD + N_META

    def body(x_ref, a_ref, c_ref, g_ref, o_ref, dg_ref, db_ref, dm_ref):
        i = pl.program_id(0)

        @pl.when(i == 0)
        def _():
            dg_ref[...] = jnp.zeros_like(dg_ref)
            db_ref[...] = jnp.zeros_like(db_ref)
            dm_ref[...] = jnp.zeros_like(dm_ref)

        dy = ALPHA * a_ref[...] + c_ref[...]
        _, xhat, rstd = _ln_fwd(x_ref[...], g_ref[...], 0.0)
        dx = _ln_bwd(dy, xhat, rstd, g_ref[...])
        o_ref[...] = dx
        dg_ref[...] += _colsum(dy * xhat)
        db_ref[...] += _colsum(dy)

        @pl.when(i % tps == 0)
        def _():
            dm_ref[...] += dx[PAD:PAD + N_META, :]

    return _pcall(
        body, name="ln0_bwd", grid=(r // tm,),
        in_specs=[pl.BlockSpec((tm, d), lambda i: (i, 0))] * 3 + [_const((1, d))],
        out_specs=[pl.BlockSpec((tm, d), lambda i: (i, 0)), _const((1, d)), _const((1, d)), _const((N_META, d))],
        out_shape=[jax.ShapeDtypeStruct((r, d), F32), jax.ShapeDtypeStruct((1, d), F32),
                   jax.ShapeDtypeStruct((1, d), F32), jax.ShapeDtypeStruct((N_META, d), F32)],
        compiler_params=_cp(("arbitrary",)),
    )(hin, dr1, dpw, g)


def _inproj(h0, w_bf, bias, lp):
    r, d = h0.shape
    n = w_bf.shape[1]
    tm = _row_tile(lp, 832)
    tn = 1152
    tps = lp // tm

    def body(a_ref, w_ref, b_ref, o_ref):
        i = pl.program_id(0)
        acc = _dot(_bf(a_ref[...]), w_ref[...]) + b_ref[...]
        t = (i % tps) * tm + lax.broadcasted_iota(jnp.int32, (tm, 1), 0)
        o_ref[...] = jnp.where(t >= PAD, acc, 0.0)

    return _pcall(
        body, name="inproj", grid=(r // tm, n // tn),
        in_specs=[pl.BlockSpec((tm, d), lambda i, j: (i, 0)), pl.BlockSpec((d, tn), lambda i, j: (0, j)),
                  pl.BlockSpec((1, tn), lambda i, j: (0, j))],
        out_specs=pl.BlockSpec((tm, tn), lambda i, j: (i, j)),
        out_shape=jax.ShapeDtypeStruct((r, n), F32),
        compiler_params=_cp(("parallel", "parallel"), 48),
    )(h0, w_bf, bias)


def _mm_tn(a, b, *, name, split=1, colsum=False, tk_want=832):
    r, m = a.shape
    n = b.shape[1]
    tk = _row_tile(r, tk_want)
    tm = min(m, 1024)
    ns = n // split
    tn = ns
    for cand in (1024, 1152, 640, 512, 128):
        if ns % cand == 0 and cand <= ns:
            tn = cand
            break
    nb = ns // tn
    nk = r // tk

    def body(a_ref, b_ref, o_ref, *rest):
        acc = rest[-1]
        k = pl.program_id(2)

        @pl.when(k == 0)
        def _():
            acc[...] = jnp.zeros_like(acc)

        bt = b_ref[...]
        acc[...] += _dot(_bf(a_ref[...]), _bf(bt), TN)

        @pl.when(k == nk - 1)
        def _():
            o_ref[...] = acc[...]

        if colsum:
            cs_ref = rest[0]

            @pl.when(k == 0)
            def _():
                cs_ref[...] = jnp.zeros_like(cs_ref)

            cs_ref[...] += _colsum(bt.astype(F32))

    out_specs = [pl.BlockSpec((None, tm, tn), lambda i, j, k: (j // nb, i, j % nb))]
    out_shape = [jax.ShapeDtypeStruct((split, m, ns), F32)]
    if colsum:
        assert m == tm
        out_specs.append(pl.BlockSpec((1, tn), lambda i, j, k: (0, j)))
        out_shape.append(jax.ShapeDtypeStruct((1, n), F32))
    res = _pcall(
        body, name=name, grid=(m // tm, n // tn, nk),
        in_specs=[pl.BlockSpec((tk, tm), lambda i, j, k: (k, i)), pl.BlockSpec((tk, tn), lambda i, j, k: (k, j))],
        out_specs=out_specs, out_shape=out_shape,
        scratch_shapes=[pltpu.VMEM((tm, tn), F32)],
        compiler_params=_cp(("parallel", "parallel", "arbitrary"), 48),
    )(a, b)
    return res if colsum else res[0]


def _mm_nt(a, w_bf, lp, *, name):
    r, kdim = a.shape
    n = w_bf.shape[0]
    tm = _row_tile(lp, 832)
    tk = 1152
    nk = kdim // tk

    def body(a_ref, w_ref, o_ref, acc):
        k = pl.program_id(1)

        @pl.when(k == 0)
        def _():
            acc[...] = jnp.zeros_like(acc)

        acc[...] += _dot(_bf(a_ref[...]), w_ref[...], NT)

        @pl.when(k == nk - 1)
        def _():
            o_ref[...] = acc[...]

    return _pcall(
        body, name=name, grid=(r // tm, nk),
        in_specs=[pl.BlockSpec((tm, tk), lambda i, k: (i, k)), pl.BlockSpec((n, tk), lambda i, k: (0, k))],
        out_specs=pl.BlockSpec((tm, n), lambda i, k: (i, 0)),
        out_shape=jax.ShapeDtypeStruct((r, n), F32),
        scratch_shapes=[pltpu.VMEM((tm, n), F32)],
        compiler_params=_cp(("parallel", "arbitrary"), 48),
    )(a, w_bf)


def _s5_prep(lam_re, lam_im, log_dt, b_re_t, b_im_t):
    g, p = lam_re.shape
    h = b_re_t.shape[0]

    def body(lr_ref, li_ref, ldt_ref, br_ref, bi_ref, pr_ref, pi_ref, bbr_ref, bbi_ref):
        lr, li = lr_ref[...], li_ref[...]
        dt = jnp.exp(ldt_ref[...])
        e = jnp.exp(lr * dt)
        ar, ai = e * jnp.cos(li * dt), e * jnp.sin(li * dt)
        den = lr * lr + li * li
        cr = ((ar - 1.0) * lr + ai * li) / den
        ci = (ai * lr - (ar - 1.0) * li) / den
        br, bi = br_ref[...], bi_ref[...]
        bbr_ref[...] = cr[None] * br - ci[None] * bi
        bbi_ref[...] = cr[None] * bi + ci[None] * br
        xr, xi = ar, ai
        pr_ref[0] = xr
        pi_ref[0] = xi
        for t in range(1, 8):
            xr, xi = xr * ar - xi * ai, xr * ai + xi * ar
            pr_ref[t] = xr
            pi_ref[t] = xi

    sd = jax.ShapeDtypeStruct
    return _pcall(body, name="s5_prep",
                  out_shape=[sd((8, g, p), F32), sd((8, g, p), F32), sd((h, g, p), F32), sd((h, g, p), F32)])(
        lam_re, lam_im, log_dt, b_re_t, b_im_t)


def _s5_prep_bwd(lam_re, lam_im, log_dt, b_re_t, b_im_t, da_re, da_im, dbb_re_t, dbb_im_t):
    g, p = lam_re.shape
    h = b_re_t.shape[0]

    def body(lr_ref, li_ref, ldt_ref, br_ref, bi_ref, dar_ref, dai_ref, dbr_ref, dbi_ref,
             glr_ref, gli_ref, gdt_ref, gbr_ref, gbi_ref):
        lr, li = lr_ref[...], li_ref[...]
        dt = jnp.exp(ldt_ref[...])
        e = jnp.exp(lr * dt)
        ar, ai = e * jnp.cos(li * dt), e * jnp.sin(li * dt)
        den = lr * lr + li * li
        cr = ((ar - 1.0) * lr + ai * li) / den
        ci = (ai * lr - (ar - 1.0) * li) / den
        br, bi = br_ref[...], bi_ref[...]
        gr, gi = dbr_ref[...], dbi_ref[...]
        gbr_ref[...] = gr * cr[None] + gi * ci[None]
        gbi_ref[...] = gi * cr[None] - gr * ci[None]
        gcr = jnp.sum(gr * br + gi * bi, axis=0)
        gci = jnp.sum(gi * br - gr * bi, axis=0)
        ilr, ili = lr / den, -li / den
        gar = dar_ref[...] + gcr * ilr + gci * ili
        gai = dai_ref[...] + gci * ilr - gcr * ili
        qr, qi = cr * ilr - ci * ili, cr * ili + ci * ilr
        glr = -(gcr * qr + gci * qi)
        gli = -(gci * qr - gcr * qi)
        gzr = gar * ar + gai * ai
        gzi = gai * ar - gar * ai
        glr_ref[...] = glr + gzr * dt
        gli_ref[...] = gli + gzi * dt
        gdt_ref[...] = jnp.sum(gzr * lr + gzi * li, axis=1, keepdims=True) * dt

    sd = jax.ShapeDtypeStruct
    return _pcall(body, name="s5_prep_bwd",
                  out_shape=[sd((g, p), F32), sd((g, p), F32), sd((g, 1), F32), sd((h, g, p), F32), sd((h, g, p), F32)])(
        lam_re, lam_im, log_dt, b_re_t, b_im_t, da_re, da_im, dbb_re_t, dbb_im_t)


def _cmul(xr, xi, yr, yi):
    return xr * yr - xi * yi, xr * yi + xi * yr


def _s5_fwd(p3, bk, cre, cim, apow, dskip):
    bsz, lp, _ = p3.shape
    tt = _row_tile(lp, 520, 8)
    nt = lp // tt
    nblk = tt // 8
    hw = 512

    def body(u_ref, bk_ref, cre_ref, cim_ref, ap_ref, d_ref, y_ref, xs_ref, car_ref):
        t = pl.program_id(2)

        @pl.when(t == 0)
        def _():
            car_ref[...] = jnp.zeros_like(car_ref)

        u = u_ref[...]
        xs_ref[...] = _dot(u, bk_ref[...], NN, HI)
        ap = ap_ref[...]
        apr, api = ap[:, :hw], ap[:, hw:]
        rows = lax.broadcasted_iota(jnp.int32, (8, hw), 0)

        def blk(i, carry):
            cr, ci = carry
            off = pl.multiple_of(i * 8, 8)
            x = xs_ref[pl.ds(off, 8), :]
            xr, xi = x[:, :hw], x[:, hw:]
            for d in (1, 2, 4):
                sr = jnp.where(rows < d, 0.0, pltpu.roll(xr, d, 0))
                si = jnp.where(rows < d, 0.0, pltpu.roll(xi, d, 0))
                mr, mi = _cmul(sr, si, apr[d - 1:d, :], api[d - 1:d, :])
                xr, xi = xr + mr, xi + mi
            mr, mi = _cmul(apr, api, cr, ci)
            xr, xi = xr + mr, xi + mi
            xs_ref[pl.ds(off, 8), :] = jnp.concatenate([xr, xi], axis=1)
            return xr[7:8, :], xi[7:8, :]

        c0 = car_ref[...]
        cr, ci = lax.fori_loop(0, nblk, blk, (c0[0:1, :hw], c0[0:1, hw:]))
        car_ref[...] = jnp.broadcast_to(jnp.concatenate([cr, ci], axis=1), car_ref.shape)
        xs = xs_ref[...]
        y_ref[...] = (_dot(xs[:, :hw], cre_ref[...], NN, HI) - _dot(xs[:, hw:], cim_ref[...], NN, HI)
                      + d_ref[...] * u)

    ub = U_OFF // 128
    return _pcall(
        body, name="s5_fwd", grid=(S5_KCH, bsz, nt),
        in_specs=[pl.BlockSpec((None, tt, 128), lambda k, b, t: (b, t, ub + k)),
                  pl.BlockSpec((None, 128, 2 * hw), lambda k, b, t: (k, 0, 0)),
                  pl.BlockSpec((None, hw, 128), lambda k, b, t: (k, 0, 0)),
                  pl.BlockSpec((None, hw, 128), lambda k, b, t: (k, 0, 0)),
                  pl.BlockSpec((None, 8, 2 * hw), lambda k, b, t: (k, 0, 0)),
                  pl.BlockSpec((1, 128), lambda k, b, t: (0, k))],
        out_specs=[pl.BlockSpec((None, tt, 128), lambda k, b, t: (b, t, k)),
                   pl.BlockSpec((None, None, tt, 2 * hw), lambda k, b, t: (b, k, t, 0))],
        out_shape=[jax.ShapeDtypeStruct((bsz, lp, S5_KCH * 128), F32),
                   jax.ShapeDtypeStruct((bsz, S5_KCH, lp, 2 * hw), F32)],
        scratch_shapes=[pltpu.VMEM((8, 2 * hw), F32)],
        compiler_params=_cp(("parallel", "parallel", "arbitrary"), 40),
    )(p3, bk, cre, cim, apow, dskip)


def _s5_bwd(dp3, p3, dy3, xs, bk, cre, cim, apow_rev, dskip):
    bsz, lp, _ = p3.shape
    tt = _row_tile(lp, 520, 8)
    nt = lp // tt
    nblk = tt // 8
    hw = 512
    tb = tt // 8

    def body(dp_any, u_ref, dy_ref, xs_ref, halo_ref, bk_ref, cre_ref, cim_ref, ap_ref, d_ref,
             du_ref, dbk_ref, dcre_ref, dcim_ref, da_ref, dd_ref, g_ref, ext_ref, car_ref):
        b = pl.program_id(1)
        t = pl.program_id(2)
        tidx = nt - 1 - t

        @pl.when(t == 0)
        def _():
            car_ref[...] = jnp.zeros_like(car_ref)

        @pl.when((b == 0) & (t == 0))
        def _():
            dbk_ref[...] = jnp.zeros_like(dbk_ref)
            dcre_ref[...] = jnp.zeros_like(dcre_ref)
            dcim_ref[...] = jnp.zeros_like(dcim_ref)
            da_ref[...] = jnp.zeros_like(da_ref)
            dd_ref[...] = jnp.zeros_like(dd_ref)

        u = u_ref[...]
        dy = dy_ref[...]
        g_ref[:, :hw] = _dot(dy, cre_ref[...], NT, HI)
        g_ref[:, hw:] = -_dot(dy, cim_ref[...], NT, HI)
        ap = ap_ref[...]
        apr, api = ap[:, :hw], -ap[:, hw:]
        rows = lax.broadcasted_iota(jnp.int32, (8, hw), 0)

        def blk(i, carry):
            cr, ci = carry
            off = pl.multiple_of((nblk - 1 - i) * 8, 8)
            x = g_ref[pl.ds(off, 8), :]
            xr, xi = x[:, :hw], x[:, hw:]
            for d in (1, 2, 4):
                sr = jnp.where(rows >= 8 - d, 0.0, pltpu.roll(xr, 8 - d, 0))
                si = jnp.where(rows >= 8 - d, 0.0, pltpu.roll(xi, 8 - d, 0))
                mr, mi = _cmul(sr, si, apr[8 - d:9 - d, :], api[8 - d:9 - d, :])
                xr, xi = xr + mr, xi + mi
            mr, mi = _cmul(apr, api, cr, ci)
            xr, xi = xr + mr, xi + mi
            g_ref[pl.ds(off, 8), :] = jnp.concatenate([xr, xi], axis=1)
            return xr[0:1, :], xi[0:1, :]

        c0 = car_ref[...]
        cr, ci = lax.fori_loop(0, nblk, blk, (c0[0:1, :hw], c0[0:1, hw:]))
        car_ref[...] = jnp.broadcast_to(jnp.concatenate([cr, ci], axis=1), car_ref.shape)

        gg = g_ref[...]
        du = _dot(gg, bk_ref[...], NT, HI) + d_ref[...] * dy
        trow = tidx * tt + lax.broadcasted_iota(jnp.int32, (tt, 1), 0)
        du_ref[...] = jnp.where(trow >= PAD, du, 0.0)
        dbk_ref[...] += _dot(u, gg, TN, HI)
        xsv = xs_ref[...]
        dcre_ref[...] += _dot(xsv[:, :hw], dy, TN, HI)
        dcim_ref[...] -= _dot(xsv[:, hw:], dy, TN, HI)
        dd_ref[...] += _colsum(dy * u)
        ext_ref[0:8, :] = jnp.where(tidx == 0, 0.0, halo_ref[...])
        ext_ref[8:, :] = xsv
        xp = ext_ref[pl.ds(7, tt), :]
        gr, gi, pr, pi = gg[:, :hw], gg[:, hw:], xp[:, :hw], xp[:, hw:]
        da_ref[:, :hw] += _colsum(gr * pr + gi * pi)
        da_ref[:, hw:] += _colsum(gi * pr - gr * pi)

    ub = U_OFF // 128
    sd = jax.ShapeDtypeStruct
    rt = lambda t: nt - 1 - t
    res = _pcall(
        body, name="s5_bwd", grid=(S5_KCH, bsz, nt),
        in_specs=[pl.BlockSpec(memory_space=pl.ANY),
                  pl.BlockSpec((None, tt, 128), lambda k, b, t: (b, rt(t), ub + k)),
                  pl.BlockSpec((None, tt, 128), lambda k, b, t: (b, rt(t), k)),
                  pl.BlockSpec((None, None, tt, 2 * hw), lambda k, b, t: (b, k, rt(t), 0)),
                  pl.BlockSpec((None, None, 8, 2 * hw), lambda k, b, t: (b, k, jnp.maximum(rt(t) * tb - 1, 0), 0)),
                  pl.BlockSpec((None, 128, 2 * hw), lambda k, b, t: (k, 0, 0)),
                  pl.BlockSpec((None, hw, 128), lambda k, b, t: (k, 0, 0)),
                  pl.BlockSpec((None, hw, 128), lambda k, b, t: (k, 0, 0)),
                  pl.BlockSpec((None, 8, 2 * hw), lambda k, b, t: (k, 0, 0)),
                  pl.BlockSpec((1, 128), lambda k, b, t: (0, k))],
        out_specs=[pl.BlockSpec((None, tt, 128), lambda k, b, t: (b, rt(t), ub + k)),
                   pl.BlockSpec((None, 128, 2 * hw), lambda k, b, t: (k, 0, 0)),
                   pl.BlockSpec((None, hw, 128), lambda k, b, t: (k, 0, 0)),
                   pl.BlockSpec((None, hw, 128), lambda k, b, t: (k, 0, 0)),
                   pl.BlockSpec((None, 1, 2 * hw), lambda k, b, t: (k, 0, 0)),
                   pl.BlockSpec((1, 128), lambda k, b, t: (0, k))],
        out_shape=[sd(dp3.shape, F32), sd((S5_KCH, 128, 2 * hw), F32), sd((S5_KCH, hw, 128), F32),
                   sd((S5_KCH, hw, 128), F32), sd((S5_KCH, 1, 2 * hw), F32), sd((1, S5_KCH * 128), F32)],
        scratch_shapes=[pltpu.VMEM((tt, 2 * hw), F32), pltpu.VMEM((tt + 8, 2 * hw), F32), pltpu.VMEM((8, 2 * hw), F32)],
        input_output_aliases={0: 0},
        compiler_params=_cp(("arbitrary", "arbitrary", "arbitrary"), 48),
    )(dp3, p3, dy3, xs, xs, bk, cre, cim, apow_rev, dskip)
    return res


_G0 = math.sqrt(2.0 / math.pi)
_G1 = 0.044715


def _gelu(y):
    return 0.5 * y * (1.0 + jnp.tanh(_G0 * (y + _G1 * y * y * y)))


def _gelu_grad(y):
    th = jnp.tanh(_G0 * (y + _G1 * y * y * y))
    return 0.5 * (1.0 + th) + 0.5 * y * (1.0 - th * th) * _G0 * (1.0 + 3.0 * _G1 * y * y)


def _glu_fwd(y_s5, wglu_g, lp):
    r, w = y_s5.shape
    tm = _row_tile(lp, 416)
    cw = wglu_g.shape[2]

    def body(y_ref, w_ref, gy_ref, z_ref, o_ref):
        gy = _bf(_gelu(y_ref[...]))
        gy_ref[...] = gy
        zs = [_dot(gy, w_ref[s]) for s in range(4)]
        for s in range(4):
            z_ref[:, s * cw:(s + 1) * cw] = zs[s]
        o_ref[:, :cw] = zs[0] * _sig(zs[2])
        o_ref[:, cw:] = zs[1] * _sig(zs[3])

    sd = jax.ShapeDtypeStruct
    return _pcall(
        body, name="glu_fwd", grid=(r // tm,),
        in_specs=[pl.BlockSpec((tm, w), lambda i: (i, 0)), _resident(wglu_g.shape)],
        out_specs=[pl.BlockSpec((tm, w), lambda i: (i, 0)), pl.BlockSpec((tm, 4 * cw), lambda i: (i, 0)),
                   pl.BlockSpec((tm, 2 * cw), lambda i: (i, 0))],
        out_shape=[sd((r, w), BF16), sd((r, 4 * cw), F32), sd((r, 2 * cw), F32)],
        compiler_params=_cp(("parallel",), 40),
    )(y_s5, wglu_g)


def _glu_bwd(dyg, z, y_s5, wglu_g, lp):
    r, w = y_s5.shape
    tm = _row_tile(lp, 416)
    cw = wglu_g.shape[2]

    def body(d_ref, z_ref, y_ref, w_ref, dz_ref, dy_ref):
        d = d_ref[...]
        zz = z_ref[...]
        acc = jnp.zeros((tm, w), F32)
        for s in range(2):
            z1 = zz[:, s * cw:(s + 1) * cw]
            sg = _sig(zz[:, (2 + s) * cw:(3 + s) * cw])
            dd = d[:, s * cw:(s + 1) * cw]
            dz1 = _bf(dd * sg)
            dz2 = _bf(dd * z1 * sg * (1.0 - sg))
            dz_ref[:, s * cw:(s + 1) * cw] = dz1
            dz_ref[:, (2 + s) * cw:(3 + s) * cw] = dz2
            acc += _dot(dz1, w_ref[s], NT) + _dot(dz2, w_ref[2 + s], NT)
        dy_ref[...] = acc * _gelu_grad(y_ref[...])

    sd = jax.ShapeDtypeStruct
    return _pcall(
        body, name="glu_bwd", grid=(r // tm,),
        in_specs=[pl.BlockSpec((tm, 2 * cw), lambda i: (i, 0)), pl.BlockSpec((tm, 4 * cw), lambda i: (i, 0)),
                  pl.BlockSpec((tm, w), lambda i: (i, 0)), _resident(wglu_g.shape)],
        out_specs=[pl.BlockSpec((tm, 4 * cw), lambda i: (i, 0)), pl.BlockSpec((tm, w), lambda i: (i, 0))],
        out_shape=[sd((r, 4 * cw), BF16), sd((r, w), F32)],
        compiler_params=_cp(("parallel",), 40),
    )(dyg, z, y_s5, wglu_g)


def _conv_fwd(p3, cw, cb):
    bsz, lp, _ = p3.shape
    tt = _row_tile(lp, 416)
    nt = lp // tt
    tb = tt // 8
    c = cw.shape[1]
    qb = Q_OFF // c

    def body(x_ref, halo_ref, w_ref, b_ref, pre_ref, act_ref, ext_ref):
        t = pl.program_id(1)
        ext_ref[0:8, :] = jnp.where(t == 0, 0.0, halo_ref[...])
        ext_ref[8:, :] = x_ref[...]
        w = w_ref[...]
        acc = b_ref[...] + w[0:1, :] * ext_ref[pl.ds(5, tt), :]
        for j in range(1, CONV_W):
            acc = acc + w[j:j + 1, :] * ext_ref[pl.ds(5 + j, tt), :]
        pre_ref[...] = acc
        act_ref[...] = acc * _sig(acc)

    sd = jax.ShapeDtypeStruct
    return _pcall(
        body, name="conv_fwd", grid=(bsz, nt),
        in_specs=[pl.BlockSpec((None, tt, c), lambda b, t: (b, t, qb)),
                  pl.BlockSpec((None, 8, c), lambda b, t: (b, jnp.maximum(t * tb - 1, 0), qb)),
                  _const((CONV_W, c)), _const((1, c))],
        out_specs=[pl.BlockSpec((None, tt, c), lambda b, t: (b, t, 0))] * 2,
        out_shape=[sd((bsz, lp, c), F32)] * 2,
        scratch_shapes=[pltpu.VMEM((tt + 8, c), F32)],
        compiler_params=_cp(("parallel", "parallel")),
    )(p3, p3, cw, cb)


def _conv_bwd(dp3, p3, dact3, pre3, cw):
    bsz, lp, _ = p3.shape
    tt = _row_tile(lp, 416)
    nt = lp // tt
    tb = tt // 8
    c = cw.shape[1]
    qb = Q_OFF // c

    def silu_grad(x):
        s = _sig(x)
        return s * (1.0 + x * (1.0 - s))

    def body(dp_any, x_ref, xh_ref, d_ref, dh_ref, pre_ref, preh_ref, w_ref, o_ref, dw_ref, db_ref, ext_ref, dext_ref):
        b = pl.program_id(0)
        t = pl.program_id(1)

        @pl.when((b == 0) & (t == 0))
        def _():
            dw_ref[...] = jnp.zeros_like(dw_ref)
            db_ref[...] = jnp.zeros_like(db_ref)

        dc = d_ref[...] * silu_grad(pre_ref[...])
        dch = jnp.where(t == nt - 1, 0.0, dh_ref[...] * silu_grad(preh_ref[...]))
        dext_ref[0:tt, :] = dc
        dext_ref[tt:, :] = dch
        ext_ref[0:8, :] = jnp.where(t == 0, 0.0, xh_ref[...])
        ext_ref[8:, :] = x_ref[...]
        w = w_ref[...]
        acc = w[CONV_W - 1:CONV_W, :] * dc
        for j in range(CONV_W - 1):
            acc = acc + w[j:j + 1, :] * dext_ref[pl.ds(CONV_W - 1 - j, tt), :]
        trow = t * tt + lax.broadcasted_iota(jnp.int32, (tt, 1), 0)
        o_ref[...] = jnp.where(trow >= PAD, acc, 0.0)
        db_ref[...] += _colsum(dc)
        for j in range(CONV_W):
            dw_ref[j:j + 1, :] += _colsum(dc * ext_ref[pl.ds(5 + j, tt), :])

    sd = jax.ShapeDtypeStruct
    nxt = lambda t: jnp.minimum((t + 1) * tb, lp // 8 - 1)
    return _pcall(
        body, name="conv_bwd", grid=(bsz, nt),
        in_specs=[pl.BlockSpec(memory_space=pl.ANY),
                  pl.BlockSpec((None, tt, c), lambda b, t: (b, t, qb)),
                  pl.BlockSpec((None, 8, c), lambda b, t: (b, jnp.maximum(t * tb - 1, 0), qb)),
                  pl.BlockSpec((None, tt, c), lambda b, t: (b, t, 0)),
                  pl.BlockSpec((None, 8, c), lambda b, t: (b, nxt(t), 0)),
                  pl.BlockSpec((None, tt, c), lambda b, t: (b, t, 0)),
                  pl.BlockSpec((None, 8, c), lambda b, t: (b, nxt(t), 0)),
                  _const((CONV_W, c))],
        out_specs=[pl.BlockSpec((None, tt, c), lambda b, t: (b, t, qb)), _const((CONV_W, c)), _const((1, c))],
        out_shape=[sd(dp3.shape, F32), sd((CONV_W, c), F32), sd((1, c), F32)],
        scratch_shapes=[pltpu.VMEM((tt + 8, c), F32), pltpu.VMEM((tt + 8, c), F32)],
        input_output_aliases={0: 0},
        compiler_params=_cp(("arbitrary", "arbitrary")),
    )(dp3, p3, p3, dact3, dact3, pre3, pre3, cw)


def _mlstm_gates(g, h_idx, c_idx, lc):
    lane = lax.broadcasted_iota(jnp.int32, g.shape, 1)
    i_col = jnp.sum(jnp.where(lane == h_idx, g, 0.0), axis=1, keepdims=True)
    f_col = jnp.sum(jnp.where(lane == M_HEADS + h_idx, g, 0.0), axis=1, keepdims=True)
    row = lax.broadcasted_iota(jnp.int32, (lc, 1), 0)
    valid = (c_idx * lc + row) >= PAD
    li = jnp.where(valid, i_col, NEG)
    lf = jnp.where(valid, jnp.minimum(f_col, 0.0) - jnp.log(1.0 + jnp.exp(-jnp.abs(f_col))), 0.0)
    r2 = lax.broadcasted_iota(jnp.int32, (lc, lc), 0)
    c2 = lax.broadcasted_iota(jnp.int32, (lc, lc), 1)
    eye = r2 == c2
    tril = r2 >= c2
    to_row = lambda col: jnp.sum(jnp.where(eye, col, 0.0), axis=0, keepdims=True)
    lf_row = to_row(lf)
    b_col = jnp.sum(jnp.where(tril, lf_row, 0.0), axis=1, keepdims=True)
    b_row = to_row(b_col)
    li_row = to_row(li)
    d_mat = jnp.where(tril, b_col - b_row + li_row, NEG)
    return dict(f_col=f_col, valid=valid, li=li, b_col=b_col, d_mat=d_mat, eye=eye, r2=r2, c2=c2, row=row,
                to_row=to_row)


def _mlstm_chunk(q, ks, v, gq, c_st, n_st, m_st, lc):
    b_col, d_mat = gq["b_col"], gq["d_mat"]
    m_inter = b_col + m_st
    m_row = jnp.maximum(m_inter, jnp.max(d_mat, axis=1, keepdims=True))
    w_intra = jnp.exp(d_mat - m_row)
    w_inter = jnp.exp(m_inter - m_row)
    qb, kb, vb, cb = _bf(q), _bf(ks), _bf(v), _bf(c_st)
    s = _dot(qb, kb, NT) * w_intra
    qc = _dot(qb, cb)
    num = _dot(_bf(s), vb) + w_inter * qc
    qn = jnp.sum(q * n_st, axis=1, keepdims=True)
    den = jnp.sum(s, axis=1, keepdims=True) + w_inter * qn
    e = jnp.exp(-m_row)
    nn = jnp.maximum(jnp.abs(den), e)
    b_last = b_col[lc - 1:lc, :]
    g_log = b_last - b_col + gq["li"]
    m_new = jnp.maximum(b_last + m_st, jnp.max(g_log, axis=0, keepdims=True))
    w_k = jnp.exp(g_log - m_new)
    decay = jnp.exp(b_last + m_st - m_new)
    return dict(w_intra=w_intra, w_inter=w_inter, qb=qb, kb=kb, vb=vb, cb=cb, s=s, qc=qc, num=num, qn=qn, den=den,
                e=e, nn=nn, m_new=m_new, w_k=w_k, decay=decay)


def _mlstm_fwd(qk3, p3):
    bsz, lp, _ = p3.shape
    lc = M_CHUNK
    nc = lp // lc
    dk, dv = 128, 256
    scale = dk ** -0.5

    def body(q_ref, k_ref, v_ref, g_ref, h_ref, cs_ref, ns_ref, ms_ref, c_sc, n_sc, m_sc):
        hh = pl.program_id(1)
        c = pl.program_id(2)

        @pl.when(c == 0)
        def _():
            c_sc[...] = jnp.zeros_like(c_sc)
            n_sc[...] = jnp.zeros_like(n_sc)
            m_sc[...] = jnp.zeros_like(m_sc)

        c_st, n_st, m_all = c_sc[...], n_sc[...], m_sc[...]
        cs_ref[...] = c_st
        ns_ref[...] = n_st
        ms_ref[...] = m_all
        m_st = m_all[:, 0:1]
        q = q_ref[...]
        ks = k_ref[...] * scale
        v = v_ref[...]
        gq = _mlstm_gates(g_ref[...], hh, c, lc)
        f = _mlstm_chunk(q, ks, v, gq, c_st, n_st, m_st, lc)
        h_ref[...] = f["num"] / f["nn"]
        kw = ks * f["w_k"]
        c_sc[...] = f["decay"] * c_st + _dot(_bf(kw), f["vb"], TN)
        n_sc[...] = f["decay"] * n_st + _colsum(kw)
        m_sc[...] = jnp.broadcast_to(f["m_new"], m_sc.shape)

    sd = jax.ShapeDtypeStruct
    qb, kb, vb, gb = 0, M_HEADS, V_OFF // dv, G_OFF // 128
    return _pcall(
        body, name="mlstm_fwd", grid=(bsz, M_HEADS, nc),
        in_specs=[pl.BlockSpec((None, lc, dk), lambda b, h, c: (b, c, qb + h)),
                  pl.BlockSpec((None, lc, dk), lambda b, h, c: (b, c, kb + h)),
                  pl.BlockSpec((None, lc, dv), lambda b, h, c: (b, c, vb + h)),
                  pl.BlockSpec((None, lc, 128), lambda b, h, c: (b, c, gb))],
        out_specs=[pl.BlockSpec((None, lc, dv), lambda b, h, c: (b, c, h)),
                   pl.BlockSpec((None, None, None, dk, dv), lambda b, h, c: (b, h, c, 0, 0)),
                   pl.BlockSpec((None, None, None, 1, dk), lambda b, h, c: (b, h, c, 0, 0)),
                   pl.BlockSpec((None, None, None, 1, 128), lambda b, h, c: (b, h, c, 0, 0))],
        out_shape=[sd((bsz, lp, M_HEADS * dv), F32), sd((bsz, M_HEADS, nc, dk, dv), F32),
                   sd((bsz, M_HEADS, nc, 1, dk), F32), sd((bsz, M_HEADS, nc, 1, 128), F32)],
        scratch_shapes=[pltpu.VMEM((dk, dv), F32), pltpu.VMEM((1, dk), F32), pltpu.VMEM((1, 128), F32)],
        compiler_params=_cp(("parallel", "parallel", "arbitrary")),
    )(qk3, qk3, p3, p3)


def _mlstm_bwd(dp3, qk3, p3, dh3, cs, ns, ms):
    bsz, lp, _ = p3.shape
    lc = M_CHUNK
    nc = lp // lc
    dk, dv = 128, 256
    scale = dk ** -0.5

    def body(dp_any, q_ref, k_ref, v_ref, g_ref, dh_ref, cs_ref, ns_ref, ms_ref,
             dv_ref, dq_ref, dkk_ref, di_ref, df_ref, dc_sc, dn_sc):
        hh = pl.program_id(1)
        t = pl.program_id(2)
        c = nc - 1 - t

        @pl.when(t == 0)
        def _():
            dc_sc[...] = jnp.zeros_like(dc_sc)
            dn_sc[...] = jnp.zeros_like(dn_sc)

        c_st, n_st = cs_ref[...], ns_ref[...]
        m_st = ms_ref[...][:, 0:1]
        q = q_ref[...]
        ks = k_ref[...] * scale
        v = v_ref[...]
        dh = dh_ref[...]
        gq = _mlstm_gates(g_ref[...], hh, c, lc)
        f = _mlstm_chunk(q, ks, v, gq, c_st, n_st, m_st, lc)
        eye, r2, c2, row, valid = gq["eye"], gq["r2"], gq["c2"], gq["row"], gq["valid"]
        w_intra, w_inter, s, nn, den = f["w_intra"], f["w_inter"], f["s"], f["nn"], f["den"]
        qb, kb, vb, cb, w_k, decay = f["qb"], f["kb"], f["vb"], f["cb"], f["w_k"], f["decay"]
        d_c, d_n = dc_sc[...], dn_sc[...]
        d_cb = _bf(d_c)

        hout = f["num"] / nn
        dnum = dh / nn
        d_nn = -jnp.sum(dh * hout, axis=1, keepdims=True) / nn
        dden = jnp.where(jnp.abs(den) > f["e"], d_nn * jnp.sign(den), 0.0)
        wdnum = w_inter * dnum
        wdden = w_inter * dden
        ds = _dot(_bf(dnum), vb, NT) + dden
        dsw = _bf(ds * w_intra)
        dq = _dot(dsw, kb) + _dot(_bf(wdnum), cb, NT) + wdden * n_st
        dkw = _dot(vb, d_cb, NT) + d_n
        dks = _dot(dsw, qb, TN) + dkw * w_k
        kw = ks * w_k
        dvv = _dot(_bf(s), _bf(dnum), TN) + _dot(_bf(kw), d_cb)
        dd = ds * s
        rs = jnp.sum(dd, axis=1, keepdims=True)
        cs_col = jnp.sum(jnp.where(eye, jnp.sum(dd, axis=0, keepdims=True), 0.0), axis=1, keepdims=True)
        dwi = jnp.sum(dnum * f["qc"], axis=1, keepdims=True) + dden * f["qn"]
        db = rs - cs_col + dwi * w_inter
        dli = cs_col
        ddecay = jnp.sum(jnp.sum(d_c * c_st, axis=1, keepdims=True), axis=0, keepdims=True) \
            + jnp.sum(d_n * n_st, axis=1, keepdims=True)
        dgl = jnp.sum(dkw * ks, axis=1, keepdims=True) * w_k
        dblast = ddecay * decay + jnp.sum(dgl, axis=0, keepdims=True)
        db = db - dgl + jnp.where(row == lc - 1, dblast, 0.0)
        dli = dli + dgl
        db_row = gq["to_row"](db)
        dlf = jnp.sum(jnp.where(c2 >= r2, db_row, 0.0), axis=1, keepdims=True)
        dlf = jnp.where(valid, dlf, 0.0)
        di_ref[...] = jnp.where(valid, dli, 0.0)
        df_ref[...] = dlf * _sig(-gq["f_col"])
        dq_ref[...] = dq
        dkk_ref[...] = dks * scale
        dv_ref[...] = dvv
        dc_sc[...] = decay * d_c + _dot(qb, _bf(wdnum), TN)
        dn_sc[...] = decay * d_n + _colsum(q * wdden)

    sd = jax.ShapeDtypeStruct
    qb_, kb_, vb_, gb_ = 0, M_HEADS, V_OFF // dv, G_OFF // 128
    rc = lambda c: nc - 1 - c
    return _pcall(
        body, name="mlstm_bwd", grid=(bsz, M_HEADS, nc),
        in_specs=[pl.BlockSpec(memory_space=pl.ANY),
                  pl.BlockSpec((None, lc, dk), lambda b, h, c: (b, rc(c), qb_ + h)),
                  pl.BlockSpec((None, lc, dk), lambda b, h, c: (b, rc(c), kb_ + h)),
                  pl.BlockSpec((None, lc, dv), lambda b, h, c: (b, rc(c), vb_ + h)),
                  pl.BlockSpec((None, lc, 128), lambda b, h, c: (b, rc(c), gb_)),
                  pl.BlockSpec((None, lc, dv), lambda b, h, c: (b, rc(c), h)),
                  pl.BlockSpec((None, None, None, dk, dv), lambda b, h, c: (b, h, rc(c), 0, 0)),
                  pl.BlockSpec((None, None, None, 1, dk), lambda b, h, c: (b, h, rc(c), 0, 0)),
                  pl.BlockSpec((None, None, None, 1, 128), lambda b, h, c: (b, h, rc(c), 0, 0))],
        out_specs=[pl.BlockSpec((None, lc, dv), lambda b, h, c: (b, rc(c), vb_ + h)),
                   pl.BlockSpec((None, lc, dk), lambda b, h, c: (b, rc(c), h)),
                   pl.BlockSpec((None, lc, dk), lambda b, h, c: (b, rc(c), h)),
                   pl.BlockSpec((None, None, lc, 1), lambda b, h, c: (b, h, rc(c), 0)),
                   pl.BlockSpec((None, None, lc, 1), lambda b, h, c: (b, h, rc(c), 0))],
        out_shape=[sd(dp3.shape, F32), sd((bsz, lp, M_HEADS * dk), F32), sd((bsz, lp, M_HEADS * dk), F32),
                   sd((bsz, M_HEADS, lp, 1), F32), sd((bsz, M_HEADS, lp, 1), F32)],
        scratch_shapes=[pltpu.VMEM((dk, dv), F32), pltpu.VMEM((1, dk), F32)],
        input_output_aliases={0: 0},
        compiler_params=_cp(("arbitrary", "arbitrary", "arbitrary")),
    )(dp3, qk3, qk3, p3, p3, dh3, cs, ns, ms)


def _headnorm(x):
    dv = x.shape[1] // M_HEADS
    xh, rs = [], []
    for h in range(M_HEADS):
        xx = x[:, h * dv:(h + 1) * dv]
        mu = jnp.mean(xx, axis=-1, keepdims=True)
        xc = xx - mu
        rstd = lax.rsqrt(jnp.mean(xc * xc, axis=-1, keepdims=True) + LN_EPS)
        xh.append(xc * rstd)
        rs.append(rstd)
    return jnp.concatenate(xh, axis=1), rs


def _mix_fwd(hm, p, ys5g, h0, gn, wmo_bf, wo_bf, g1, b1, lp):
    r, d = hm.shape
    tm = _row_tile(lp, 208)

    def body(hm_ref, o_ref, gs_ref, gm_ref, ys_ref, h0_ref, gn_ref, wmo_ref, wo_ref, g1_ref, b1_ref,
             ymin_ref, ym_ref, mix_ref, r1_ref, h1_ref):
        xhat, _ = _headnorm(hm_ref[...])
        ymin = _bf(_sig(o_ref[...]) * (xhat * gn_ref[...]))
        ymin_ref[...] = ymin
        ym = _dot(ymin, wmo_ref[...])
        ym_ref[...] = ym
        mix = _bf(_sig(gs_ref[...]) * ys_ref[...] + _sig(gm_ref[...]) * ym)
        mix_ref[...] = mix
        r1 = ALPHA * h0_ref[...] + _dot(mix, wo_ref[...])
        r1_ref[...] = r1
        h1, _, _ = _ln_fwd(r1, g1_ref[...], b1_ref[...])
        h1_ref[...] = h1

    sd = jax.ShapeDtypeStruct
    row = pl.BlockSpec((tm, d), lambda i: (i, 0))
    return _pcall(
        body, name="mix_fwd", grid=(r // tm,),
        in_specs=[row, pl.BlockSpec((tm, d), lambda i: (i, O_OFF // d)), pl.BlockSpec((tm, d), lambda i: (i, GS_OFF // d)),
                  pl.BlockSpec((tm, d), lambda i: (i, GM_OFF // d)), row, row, _const((1, d)),
                  _resident((d, d)), _resident((d, d)), _const((1, d)), _const((1, d))],
        out_specs=[row] * 5,
        out_shape=[sd((r, d), BF16), sd((r, d), F32), sd((r, d), BF16), sd((r, d), F32), sd((r, d), F32)],
        compiler_params=_cp(("parallel",), 48),
    )(hm, p, p, p, ys5g, h0, gn, wmo_bf, wo_bf, g1, b1)


def _mix_bwd(dh1, r1, g1, wo_bf, wmo_bf, p, ys5g, ym, hm, gn, lp):
    r, d = hm.shape
    tm = _row_tile(lp, 208)
    dv = d // M_HEADS

    def body(dh1_ref, r1_ref, g1_ref, wo_ref, wmo_ref, o_ref, gs_ref, gm_ref, ys_ref, ym_ref, hm_ref, gn_ref,
             dr1_ref, dp_ref, dys_ref, dym_ref, dhm_ref, dg1_ref, db1_ref, dgn_ref):
        i = pl.program_id(0)

        @pl.when(i == 0)
        def _():
            dg1_ref[...] = jnp.zeros_like(dg1_ref)
            db1_ref[...] = jnp.zeros_like(db1_ref)
            dgn_ref[...] = jnp.zeros_like(dgn_ref)

        dh1 = dh1_ref[...]
        _, xhat1, rstd1 = _ln_fwd(r1_ref[...], g1_ref[...], 0.0)
        dr1 = _ln_bwd(dh1, xhat1, rstd1, g1_ref[...])
        dr1_ref[...] = dr1
        dg1_ref[...] += _colsum(dh1 * xhat1)
        db1_ref[...] += _colsum(dh1)
        dmix = _dot(_bf(dr1), wo_ref[...], NT)
        sgs, sgm, so = _sig(gs_ref[...]), _sig(gm_ref[...]), _sig(o_ref[...])
        dys_ref[...] = dmix * sgs
        dp_ref[:, d:2 * d] = dmix * ys_ref[...] * sgs * (1.0 - sgs)
        dym = dmix * sgm
        dym_ref[...] = _bf(dym)
        dp_ref[:, 2 * d:3 * d] = dmix * ym_ref[...] * sgm * (1.0 - sgm)
        dymin = _dot(_bf(dym), wmo_ref[...], NT)
        xhat, rs = _headnorm(hm_ref[...])
        gn_ = gn_ref[...]
        dp_ref[:, 0:d] = dymin * (xhat * gn_) * so * (1.0 - so)
        dhn = dymin * so
        dgn_ref[...] += _colsum(dhn * xhat)
        dxh = dhn * gn_
        for h in range(M_HEADS):
            sl = slice(h * dv, (h + 1) * dv)
            a, xh = dxh[:, sl], xhat[:, sl]
            m1 = jnp.mean(a, axis=-1, keepdims=True)
            m2 = jnp.mean(a * xh, axis=-1, keepdims=True)
            dhm_ref[:, sl] = rs[h] * (a - m1 - xh * m2)

    sd = jax.ShapeDtypeStruct
    row = pl.BlockSpec((tm, d), lambda i: (i, 0))
    vec = _const((1, d))
    return _pcall(
        body, name="mix_bwd", grid=(r // tm,),
        in_specs=[row, row, vec, _resident((d, d)), _resident((d, d)),
                  pl.BlockSpec((tm, d), lambda i: (i, O_OFF // d)), pl.BlockSpec((tm, d), lambda i: (i, GS_OFF // d)),
                  pl.BlockSpec((tm, d), lambda i: (i, GM_OFF // d)), row, row, row, vec],
        out_specs=[row, pl.BlockSpec((tm, 3 * d), lambda i: (i, 0)), row, row, row, vec, vec, vec],
        out_shape=[sd((r, d), F32), sd((r, NP), F32), sd((r, d), F32), sd((r, d), BF16), sd((r, d), F32),
                   sd((1, d), F32), sd((1, d), F32), sd((1, d), F32)],
        compiler_params=_cp(("arbitrary",), 48),
    )(dh1, r1, g1, wo_bf, wmo_bf, p, p, p, ys5g, ym, hm, gn)


def _mlp_fwd(h1, tgt, wup_g, wdn_bf, bup, g2, b2, lp):
    r, d = h1.shape
    tm = _row_tile(lp, 320)
    tps = lp // tm
    nf = wup_g.shape[0]

    def body(h1_ref, t_ref, wup_ref, wdn_ref, bup_ref, g2_ref, b2_ref, dr2_ref, act_ref, loss_ref, dg2_ref, db2_ref):
        i = pl.program_id(0)

        @pl.when(i == 0)
        def _():
            loss_ref[...] = jnp.zeros_like(loss_ref)
            dg2_ref[...] = jnp.zeros_like(dg2_ref)
            db2_ref[...] = jnp.zeros_like(db2_ref)

        h1 = h1_ref[...]
        h1b = _bf(h1)
        ff = jnp.zeros((tm, d), F32)
        for s in range(nf):
            up = _dot(h1b, wup_ref[s]) + bup_ref[:, s * d:(s + 1) * d]
            a = jnp.maximum(up, 0.0)
            a = _bf(a * a)
            act_ref[:, s * d:(s + 1) * d] = a
            ff = ff + _dot(a, wdn_ref[s * d:(s + 1) * d, :])
        r2 = ALPHA * h1 + ff
        g2 = g2_ref[...]
        y, xhat, rstd = _ln_fwd(r2, g2, b2_ref[...])
        t = (i % tps) * tm + lax.broadcasted_iota(jnp.int32, (tm, 1), 0)
        diff = jnp.where(t >= PAD + N_META, y - t_ref[...], 0.0)
        loss_ref[...] += 0.5 / d * jnp.sum(jnp.sum(diff * diff, axis=1, keepdims=True), axis=0, keepdims=True)
        dy = diff * (1.0 / d)
        dg2_ref[...] += _colsum(dy * xhat)
        db2_ref[...] += _colsum(dy)
        dr2_ref[...] = _ln_bwd(dy, xhat, rstd, g2)

    sd = jax.ShapeDtypeStruct
    row = pl.BlockSpec((tm, d), lambda i: (i, 0))
    vec = _const((1, d))
    return _pcall(
        body, name="mlp_fwd", grid=(r // tm,),
        in_specs=[row, row, _resident(wup_g.shape), _resident(wdn_bf.shape), _const((1, nf * d)), vec, vec],
        out_specs=[row, pl.BlockSpec((tm, nf * d), lambda i: (i, 0)), _const((1, 128)), vec, vec],
        out_shape=[sd((r, d), F32), sd((r, nf * d), BF16), sd((1, 128), F32), sd((1, d), F32), sd((1, d), F32)],
        compiler_params=_cp(("arbitrary",), 56),
    )(h1, tgt, wup_g, wdn_bf, bup, g2, b2)


def _mlp_bwd(h1, dr2, wup_g, wdn_bf, bup, lp):
    r, d = h1.shape
    tm = _row_tile(lp, 320)
    nf = wup_g.shape[0]

    def body(h1_ref, dr2_ref, wup_ref, wdn_ref, bup_ref, dh1_ref, dup_ref, dbup_ref):
        i = pl.program_id(0)

        @pl.when(i == 0)
        def _():
            dbup_ref[...] = jnp.zeros_like(dbup_ref)

        h1b = _bf(h1_ref[...])
        dr2 = dr2_ref[...]
        dr2b = _bf(dr2)
        acc = ALPHA * dr2
        for s in range(nf):
            up = _dot(h1b, wup_ref[s]) + bup_ref[:, s * d:(s + 1) * d]
            dact = _dot(dr2b, wdn_ref[s * d:(s + 1) * d, :], NT)
            dup = dact * (2.0 * jnp.maximum(up, 0.0))
            dbup_ref[:, s * d:(s + 1) * d] += _colsum(dup)
            dupb = _bf(dup)
            dup_ref[:, s * d:(s + 1) * d] = dupb
            acc = acc + _dot(dupb, wup_ref[s], NT)
        dh1_ref[...] = acc

    sd = jax.ShapeDtypeStruct
    row = pl.BlockSpec((tm, d), lambda i: (i, 0))
    return _pcall(
        body, name="mlp_bwd", grid=(r // tm,),
        in_specs=[row, row, _resident(wup_g.shape), _resident(wdn_bf.shape), _const((1, nf * d))],
        out_specs=[row, pl.BlockSpec((tm, nf * d), lambda i: (i, 0)), _const((1, nf * d))],
        out_shape=[sd((r, d), F32), sd((r, nf * d), BF16), sd((1, nf * d), F32)],
        compiler_params=_cp(("arbitrary",), 56),
    )(h1, dr2, wup_g, wdn_bf, bup)


def _s5_block_mats(bb_re_t, bb_im_t, c_re, c_im, ap_re, ap_im):
    ng = c_re.shape[0]
    gl = ng // S5_KCH
    eye = jnp.eye(gl, dtype=F32)

    def bmat(bt):
        bb = jnp.transpose(bt, (1, 0, 2)).reshape(S5_KCH, gl, S5_GROUP, S5_STATE)
        return jnp.einsum("kghp,gj->kghjp", bb, eye).reshape(S5_KCH, gl * S5_GROUP, gl * S5_STATE)

    def cmat(c):
        cc = c.reshape(S5_KCH, gl, S5_GROUP, S5_STATE)
        return jnp.einsum("kghp,gj->kjpgh", cc, eye).reshape(S5_KCH, gl * S5_STATE, gl * S5_GROUP)

    def pw(a):
        return jnp.transpose(a.reshape(8, S5_KCH, gl * S5_STATE), (1, 0, 2))

    bk = jnp.concatenate([bmat(bb_re_t), bmat(bb_im_t)], axis=-1)
    apow = jnp.concatenate([pw(ap_re), pw(ap_im)], axis=-1)
    return bk, cmat(c_re), cmat(c_im), apow


def _s5_block_grads(dbk, dcre, dcim, da):
    gl = dbk.shape[1] // S5_GROUP
    ng = gl * S5_KCH
    eye = jnp.eye(gl, dtype=F32)
    hw = gl * S5_STATE

    def bpart(x):
        x = x.reshape(S5_KCH, gl, S5_GROUP, gl, S5_STATE)
        x = jnp.einsum("kghjp,gj->kghp", x, eye).reshape(ng, S5_GROUP, S5_STATE)
        return jnp.transpose(x, (1, 0, 2))

    def cpart(x):
        x = x.reshape(S5_KCH, gl, S5_STATE, gl, S5_GROUP)
        return jnp.einsum("kjpgh,gj->kghp", x, eye).reshape(ng, S5_GROUP, S5_STATE)

    return (bpart(dbk[..., :hw]), bpart(dbk[..., hw:]), cpart(dcre), cpart(dcim),
            da[:, 0, :hw].reshape(ng, S5_STATE), da[:, 0, hw:].reshape(ng, S5_STATE))


def _local_step(x, tgt, w):
    bsz, seq, d = x.shape
    lp = PAD + N_META + seq
    r = bsz * lp
    meta = jnp.broadcast_to(w["meta_tokens"][None], (bsz, N_META, d))
    hin = jnp.concatenate([jnp.zeros((bsz, PAD, d), F32), meta, x], axis=1).reshape(r, d)
    tgtp = jnp.concatenate([jnp.zeros((bsz, PAD + N_META, d), F32), tgt], axis=1).reshape(r, d)

    h0 = _ln0_fwd(hin, w["ln0_g"], w["ln0_b"], lp)
    p = _inproj(h0, w["w_in"], w["b_in"], lp)
    p3 = p.reshape(bsz, lp, NP)

    b_re_t = jnp.transpose(w["s5_b_re"], (2, 0, 1))
    b_im_t = jnp.transpose(w["s5_b_im"], (2, 0, 1))
    ap_re, ap_im, bb_re_t, bb_im_t = _s5_prep(w["s5_lambda_re"], w["s5_lambda_im"], w["s5_log_dt"], b_re_t, b_im_t)
    bk, cre, cim, apow = _s5_block_mats(bb_re_t, bb_im_t, w["s5_c_re"], w["s5_c_im"], ap_re, ap_im)
    y_s5, xs = _s5_fwd(p3, bk, cre, cim, apow, w["s5_d"])
    sw = y_s5.shape[-1]
    gy, z, ys5g = _glu_fwd(y_s5.reshape(r, sw), w["s5_w_glu"], lp)

    pre3, qk3 = _conv_fwd(p3, w["qk_conv_w"], w["qk_conv_b"])
    hm3, cs, ns, ms = _mlstm_fwd(qk3, p3)
    hm = hm3.reshape(r, d)
    ymin, ym, mix, r1, h1 = _mix_fwd(hm, p, ys5g, h0, w["m_norm_g"], w["m_w_out"], w["w_o"], w["ln1_g"], w["ln1_b"], lp)
    dr2, act, loss, dg2, db2 = _mlp_fwd(h1, tgtp, w["w_up"], w["w_down"], w["b_up"], w["ln2_g"], w["ln2_b"], lp)

    g = {"ln2_g": dg2, "ln2_b": db2}
    dh1, dup, g["b_up"] = _mlp_bwd(h1, dr2, w["w_up"], w["w_down"], w["b_up"], lp)
    g["w_down"] = _mm_tn(act, dr2, name="dw_down")
    g["w_up"] = _mm_tn(h1, dup, name="dw_up", split=w["w_up"].shape[0])
    dr1, dp, dys5g, dym, dhm, g["ln1_g"], g["ln1_b"], g["m_norm_g"] = _mix_bwd(
        dh1, r1, w["ln1_g"], w["w_o"], w["m_w_out"], p, ys5g, ym, hm, w["m_norm_g"], lp)
    g["w_o"] = _mm_tn(mix, dr1, name="dw_o")
    g["m_w_out"] = _mm_tn(ymin, dym, name="dw_mout")

    dp3 = dp.reshape(bsz, lp, NP)
    dp3, dq3, dk3, di, df = _mlstm_bwd(dp3, qk3, p3, dhm.reshape(bsz, lp, d), cs, ns, ms)
    dqk3 = jnp.concatenate([dq3, dk3], axis=-1)
    dp3, g["qk_conv_w"], g["qk_conv_b"] = _conv_bwd(dp3, p3, dqk3, pre3, w["qk_conv_w"])
    dz, dys5 = _glu_bwd(dys5g, z, y_s5.reshape(r, sw), w["s5_w_glu"], lp)
    g["s5_w_glu"] = _mm_tn(gy, dz, name="dw_glu", split=w["s5_w_glu"].shape[0])
    apow_rev = jnp.flip(apow, axis=1)
    dp3, dbk, dcre, dcim, da, g["s5_d"] = _s5_bwd(dp3, p3, dys5.reshape(bsz, lp, sw), xs, bk, cre, cim, apow_rev, w["s5_d"])
    dbb_re_t, dbb_im_t, g["s5_c_re"], g["s5_c_im"], da_re, da_im = _s5_block_grads(dbk, dcre, dcim, da)
    g["s5_lambda_re"], g["s5_lambda_im"], g["s5_log_dt"], gb_re_t, gb_im_t = _s5_prep_bwd(
        w["s5_lambda_re"], w["s5_lambda_im"], w["s5_log_dt"], b_re_t, b_im_t, da_re, da_im, dbb_re_t, dbb_im_t)
    g["s5_b_re"] = jnp.transpose(gb_re_t, (1, 2, 0))
    g["s5_b_im"] = jnp.transpose(gb_im_t, (1, 2, 0))

    dgate = jnp.concatenate([jnp.transpose(di[..., 0], (0, 2, 1)), jnp.transpose(df[..., 0], (0, 2, 1)),
                             jnp.zeros((bsz, lp, NP - G_OFF - 2 * M_HEADS), F32)], axis=-1)
    dp3 = lax.dynamic_update_slice(dp3, dgate, (0, 0, G_OFF))
    dp = dp3.reshape(r, NP)
    g["w_in"], g["b_in"] = _mm_tn(h0, dp, name="dw_in", colsum=True)
    dpw = _mm_nt(dp, w["w_in"], lp, name="dh0")
    dhin, g["ln0_g"], g["ln0_b"], g["meta_tokens"] = _ln0_bwd(hin, dr1, dpw, w["ln0_g"], lp)
    grad_x = dhin.reshape(bsz, lp, d)[:, PAD + N_META:]
    return loss, grad_x, g


_ANY = pl.BlockSpec(memory_space=pl.ANY)
_MESH = pl.DeviceIdType.MESH


def _place():
    return lax.axis_index("x"), lax.axis_index("y"), lax.axis_index("c")


def _gather_chips(shards):
    n = len(shards)

    def body(*refs):
        ins, outs = refs[:n], refs[n:2 * n]
        send, recv, loc = refs[2 * n:]
        x, y, c = _place()
        me = 2 * x + y
        peers = [(1 - x, y), (x, 1 - y), (1 - x, 1 - y)]

        def rc(a, k, slot):
            px, py = peers[k]
            return pltpu.make_async_remote_copy(src_ref=ins[a], dst_ref=outs[a].at[slot], send_sem=send.at[a, k],
                                                recv_sem=recv.at[a, k], device_id=(px, py, c), device_id_type=_MESH)

        own = [pltpu.make_async_copy(ins[a], outs[a].at[me], loc.at[a]) for a in range(n)]
        for cp in own:
            cp.start()
        out = [rc(a, k, me) for a in range(n) for k in range(3)]
        for cp in out:
            cp.start()
        for a in range(n):
            for k in range(3):
                rc(a, k, 2 * peers[k][0] + peers[k][1]).wait_recv()
        for cp in out:
            cp.wait_send()
        for cp in own:
            cp.wait()

    return _pcall(
        body, name="gather_chips", in_specs=[_ANY] * n, out_specs=[_ANY] * n,
        out_shape=[jax.ShapeDtypeStruct((4,) + s.shape, s.dtype) for s in shards],
        scratch_shapes=[pltpu.SemaphoreType.DMA((n, 3)), pltpu.SemaphoreType.DMA((n, 3)), pltpu.SemaphoreType.DMA((n,))],
    )(*shards)


def _scatter_chips(parts):
    n = len(parts)

    def body(*refs):
        ins, outs = refs[:n], refs[n:2 * n]
        send, recv, loc = refs[2 * n:]
        x, y, c = _place()
        me = 2 * x + y
        peers = [(1 - x, y), (x, 1 - y), (1 - x, 1 - y)]
        pchip = [2 * px + py for px, py in peers]

        def rc(a, k, src_slot, dst_slot):
            px, py = peers[k]
            return pltpu.make_async_remote_copy(src_ref=ins[a].at[src_slot], dst_ref=outs[a].at[dst_slot],
                                                send_sem=send.at[a, k], recv_sem=recv.at[a, k],
                                                device_id=(px, py, c), device_id_type=_MESH)

        own = [pltpu.make_async_copy(ins[a].at[me], outs[a].at[me], loc.at[a]) for a in range(n)]
        for cp in own:
            cp.start()
        out = [rc(a, k, pchip[k], me) for a in range(n) for k in range(3)]
        for cp in out:
            cp.start()
        for a in range(n):
            for k in range(3):
                rc(a, k, me, pchip[k]).wait_recv()
        for cp in out:
            cp.wait_send()
        for cp in own:
            cp.wait()

    return _pcall(
        body, name="scatter_chips", in_specs=[_ANY] * n, out_specs=[_ANY] * n,
        out_shape=[jax.ShapeDtypeStruct(s.shape, s.dtype) for s in parts],
        scratch_shapes=[pltpu.SemaphoreType.DMA((n, 3)), pltpu.SemaphoreType.DMA((n, 3)), pltpu.SemaphoreType.DMA((n,))],
    )(*parts)


def _swap_cores(arrs):
    n = len(arrs)

    def body(*refs):
        ins, outs = refs[:n], refs[n:2 * n]
        send, recv = refs[2 * n:]
        x, y, c = _place()
        cps = [pltpu.make_async_remote_copy(src_ref=ins[a], dst_ref=outs[a], send_sem=send.at[a], recv_sem=recv.at[a],
                                            device_id=(x, y, 1 - c), device_id_type=_MESH) for a in range(n)]
        for cp in cps:
            cp.start()
        for cp in cps:
            cp.wait_recv()
        for cp in cps:
            cp.wait_send()

    return _pcall(
        body, name="swap_cores", in_specs=[_ANY] * n, out_specs=[_ANY] * n,
        out_shape=[jax.ShapeDtypeStruct(s.shape, s.dtype) for s in arrs],
        scratch_shapes=[pltpu.SemaphoreType.DMA((n,)), pltpu.SemaphoreType.DMA((n,))],
    )(*arrs)


def _allreduce_small(v):
    rows = v.shape[0]

    def body(v_ref, sum_ref, all_ref, send, recv):
        x, y, c = _place()
        me = 4 * x + 2 * y + c
        flips = [(k >> 2 & 1, k >> 1 & 1, k & 1) for k in range(1, 8)]

        def peer(f):
            return tuple(1 - q if b else q for q, b in zip((x, y, c), f))

        all_ref[me] = v_ref[...]
        cps = [pltpu.make_async_remote_copy(src_ref=v_ref, dst_ref=all_ref.at[me], send_sem=send.at[k], recv_sem=recv.at[k],
                                            device_id=peer(f), device_id_type=_MESH) for k, f in enumerate(flips)]
        for cp in cps:
            cp.start()
        for cp in cps:
            cp.wait_recv()
        acc = all_ref[0]
        for j in range(1, 8):
            acc = acc + all_ref[j]
        sum_ref[...] = acc
        for cp in cps:
            cp.wait_send()

    vm = pl.BlockSpec(memory_space=pltpu.VMEM)
    return _pcall(
        body, name="allreduce_small", in_specs=[vm], out_specs=vm,
        out_shape=jax.ShapeDtypeStruct((rows, 128), F32),
        scratch_shapes=[pltpu.VMEM((8, rows, 128), F32), pltpu.SemaphoreType.DMA((7,)), pltpu.SemaphoreType.DMA((7,))],
        compiler_params=_cp(None, 40),
    )(v)


def _sum_slots(a):
    _, rows, cols = a.shape
    tm = _row_tile(rows, 256, 8)

    def body(a_ref, o_ref):
        o_ref[...] = ((a_ref[0] + a_ref[1]) + a_ref[2]) + a_ref[3]

    return _pcall(
        body, name="sum_slots", grid=(rows // tm,),
        in_specs=[pl.BlockSpec((4, tm, cols), lambda i: (0, i, 0))],
        out_specs=pl.BlockSpec((tm, cols), lambda i: (i, 0)),
        out_shape=jax.ShapeDtypeStruct((rows, cols), F32),
        compiler_params=_cp(("parallel",), 40),
    )(a)


def _adamw(w, m, v, g0, g1=None):
    rows, cols = w.shape
    tm = _row_tile(rows, 256, 8)
    c1 = 1.0 - ADAM_B1 ** ADAM_STEP
    c2 = 1.0 - ADAM_B2 ** ADAM_STEP
    two = g1 is not None

    def body(*refs):
        w_ref, m_ref, v_ref, g0_ref = refs[:4]
        g_ref, d_ref, nm_ref, nv_ref = refs[-4:]
        g = g0_ref[...]
        if two:
            g = g + refs[4][...]
        nm = ADAM_B1 * m_ref[...] + (1.0 - ADAM_B1) * g
        nv = ADAM_B2 * v_ref[...] + (1.0 - ADAM_B2) * (g * g)
        g_ref[...] = g
        nm_ref[...] = nm
        nv_ref[...] = nv
        d_ref[...] = -ADAM_LR * ((nm / c1) / (jnp.sqrt(nv / c2) + ADAM_EPS) + ADAM_WD * w_ref[...])

    blk = pl.BlockSpec((tm, cols), lambda i: (i, 0))
    ins = [w, m, v, g0] + ([g1] if two else [])
    return _pcall(
        body, name="adamw", grid=(rows // tm,), in_specs=[blk] * len(ins), out_specs=[blk] * 4,
        out_shape=[jax.ShapeDtypeStruct((rows, cols), F32)] * 4,
        compiler_params=_cp(("parallel",), 40),
    )(*ins)


_BIG = ("w_in", "s5_w_glu", "m_w_out", "w_o", "w_up", "w_down")
_SMALL = ("ln0_g", "ln0_b", "b_in", "qk_conv_b", "s5_lambda_re", "s5_lambda_im", "s5_log_dt", "s5_b_re", "s5_b_im",
          "s5_c_re", "s5_c_im", "s5_d", "m_norm_g", "ln1_g", "ln1_b", "b_up", "ln2_g", "ln2_b")
_SMALL_SHARDED = ("meta_tokens", "qk_conv_w")
_ORDER = ("meta_tokens", "ln0_g", "ln0_b", "w_in", "b_in", "qk_conv_w", "qk_conv_b", "s5_lambda_re", "s5_lambda_im",
          "s5_log_dt", "s5_b_re", "s5_b_im", "s5_c_re", "s5_c_im", "s5_d", "s5_w_glu", "m_norm_g", "m_w_out", "w_o",
          "ln1_g", "ln1_b", "w_up", "b_up", "w_down", "ln2_g", "ln2_b")


def _pack(arrs):
    flat = jnp.concatenate([a.reshape(-1) for a in arrs])
    n = flat.shape[0]
    rows = -(-n // 1024) * 8
    return jnp.pad(flat, (0, rows * 128 - n)).reshape(rows, 128)


def _unpack(packed, shapes):
    flat = packed.reshape(-1)
    out, off = [], 0
    for s in shapes:
        n = math.prod(s)
        out.append(flat[off:off + n].reshape(s))
        off += n
    return out


def kernel(x, meta_tokens, ln0_g, ln0_b, w_in, b_in, qk_conv_w, qk_conv_b, s5_lambda_re, s5_lambda_im, s5_log_dt, s5_b_re, s5_b_im, s5_c_re, s5_c_im, s5_d, s5_w_glu, m_norm_g, m_w_out, w_o, ln1_g, ln1_b, w_up, b_up, w_down, ln2_g, ln2_b, loss_target, m_meta_tokens, m_ln0_g, m_ln0_b, m_w_in, m_b_in, m_qk_conv_w, m_qk_conv_b, m_s5_lambda_re, m_s5_lambda_im, m_s5_log_dt, m_s5_b_re, m_s5_b_im, m_s5_c_re, m_s5_c_im, m_s5_d, m_s5_w_glu, m_m_norm_g, m_m_w_out, m_w_o, m_ln1_g, m_ln1_b, m_w_up, m_b_up, m_w_down, m_ln2_g, m_ln2_b, v_meta_tokens, v_ln0_g, v_ln0_b, v_w_in, v_b_in, v_qk_conv_w, v_qk_conv_b, v_s5_lambda_re, v_s5_lambda_im, v_s5_log_dt, v_s5_b_re, v_s5_b_im, v_s5_c_re, v_s5_c_im, v_s5_d, v_s5_w_glu, v_m_norm_g, v_m_w_out, v_w_o, v_ln1_g, v_ln1_b, v_w_up, v_b_up, v_w_down, v_ln2_g, v_ln2_b):
    wts = dict(meta_tokens=meta_tokens, ln0_g=ln0_g, ln0_b=ln0_b, w_in=w_in, b_in=b_in, qk_conv_w=qk_conv_w,
               qk_conv_b=qk_conv_b, s5_lambda_re=s5_lambda_re, s5_lambda_im=s5_lambda_im, s5_log_dt=s5_log_dt,
               s5_b_re=s5_b_re, s5_b_im=s5_b_im, s5_c_re=s5_c_re, s5_c_im=s5_c_im, s5_d=s5_d, s5_w_glu=s5_w_glu,
               m_norm_g=m_norm_g, m_w_out=m_w_out, w_o=w_o, ln1_g=ln1_g, ln1_b=ln1_b, w_up=w_up, b_up=b_up,
               w_down=w_down, ln2_g=ln2_g, ln2_b=ln2_b)
    mom = dict(meta_tokens=m_meta_tokens, ln0_g=m_ln0_g, ln0_b=m_ln0_b, w_in=m_w_in, b_in=m_b_in, qk_conv_w=m_qk_conv_w,
               qk_conv_b=m_qk_conv_b, s5_lambda_re=m_s5_lambda_re, s5_lambda_im=m_s5_lambda_im, s5_log_dt=m_s5_log_dt,
               s5_b_re=m_s5_b_re, s5_b_im=m_s5_b_im, s5_c_re=m_s5_c_re, s5_c_im=m_s5_c_im, s5_d=m_s5_d,
               s5_w_glu=m_s5_w_glu, m_norm_g=m_m_norm_g, m_w_out=m_m_w_out, w_o=m_w_o, ln1_g=m_ln1_g, ln1_b=m_ln1_b,
               w_up=m_w_up, b_up=m_b_up, w_down=m_w_down, ln2_g=m_ln2_g, ln2_b=m_ln2_b)
    var = dict(meta_tokens=v_meta_tokens, ln0_g=v_ln0_g, ln0_b=v_ln0_b, w_in=v_w_in, b_in=v_b_in, qk_conv_w=v_qk_conv_w,
               qk_conv_b=v_qk_conv_b, s5_lambda_re=v_s5_lambda_re, s5_lambda_im=v_s5_lambda_im, s5_log_dt=v_s5_log_dt,
               s5_b_re=v_s5_b_re, s5_b_im=v_s5_b_im, s5_c_re=v_s5_c_re, s5_c_im=v_s5_c_im, s5_d=v_s5_d,
               s5_w_glu=v_s5_w_glu, m_norm_g=v_m_norm_g, m_w_out=v_m_w_out, w_o=v_w_o, ln1_g=v_ln1_g, ln1_b=v_ln1_b,
               w_up=v_w_up, b_up=v_b_up, w_down=v_w_down, ln2_g=v_ln2_g, ln2_b=v_ln2_b)
    d = x.shape[-1]
    chip = 2 * lax.axis_index("x") + lax.axis_index("y")

    gathered = _gather_chips([_bf(wts[n][0]) for n in _BIG] + [meta_tokens, qk_conv_w[0]])
    gw = dict(zip(_BIG + _SMALL_SHARDED, gathered))
    cat = lambda a: jnp.transpose(a, (1, 0, 2)).reshape(a.shape[1], 4 * a.shape[2])
    w = dict(
        meta_tokens=cat(gw["meta_tokens"]), ln0_g=ln0_g[None], ln0_b=ln0_b[None],
        w_in=_to_pad_cols(cat(gw["w_in"])), b_in=_to_pad_cols(b_in),
        qk_conv_w=cat(gw["qk_conv_w"]), qk_conv_b=qk_conv_b,
        s5_lambda_re=s5_lambda_re[0], s5_lambda_im=s5_lambda_im[0], s5_log_dt=s5_log_dt[0][:, None],
        s5_b_re=s5_b_re[0], s5_b_im=s5_b_im[0], s5_c_re=s5_c_re[0], s5_c_im=s5_c_im[0], s5_d=s5_d,
        s5_w_glu=gw["s5_w_glu"], m_norm_g=m_norm_g, m_w_out=gw["m_w_out"].reshape(d, d), w_o=gw["w_o"].reshape(d, d),
        ln1_g=ln1_g, ln1_b=ln1_b, w_up=gw["w_up"], b_up=b_up, w_down=gw["w_down"].reshape(4 * d, d),
        ln2_g=ln2_g, ln2_b=ln2_b)

    loss, grad_x, g = _local_step(x, loss_target, w)
    g["b_in"] = _from_pad_cols(g["b_in"])

    gin = _from_pad_cols(g["w_in"][0])
    parts = dict(
        w_in=jnp.transpose(gin.reshape(d, 4, gin.shape[1] // 4), (1, 0, 2)), s5_w_glu=g["s5_w_glu"],
        m_w_out=g["m_w_out"].reshape(4, d // 4, d), w_o=g["w_o"].reshape(4, d // 4, d), w_up=g["w_up"],
        w_down=g["w_down"].reshape(4, d, d))
    got = _scatter_chips([parts[n] for n in _BIG])
    mine = [_sum_slots(a) for a in got]
    theirs = _swap_cores(mine)

    small_shapes = [(1, 128)] + [wts[n].shape for n in _SMALL] + [g[n].shape for n in _SMALL_SHARDED]
    packed = _pack([loss] + [g[n] for n in _SMALL] + [g[n] for n in _SMALL_SHARDED])
    tot = _unpack(_allreduce_small(packed), small_shapes)
    loss_out = tot[0][0, 0]
    gsm = dict(zip(_SMALL + _SMALL_SHARDED, tot[1:]))
    for n in _SMALL_SHARDED:
        cols = wts[n].shape[-1]
        gsm[n] = lax.dynamic_slice_in_dim(gsm[n], chip * cols, cols, axis=1).reshape(wts[n].shape)

    res = {}
    for i, n in enumerate(_BIG):
        sh = wts[n].shape
        two = lambda a: a.reshape(-1, sh[-1])
        res[n] = [r.reshape(sh) for r in _adamw(two(wts[n]), two(mom[n]), two(var[n]), mine[i], theirs[i])]
    names = _SMALL + _SMALL_SHARDED
    shapes = [wts[n].shape for n in names]
    pk = lambda dct: _pack([dct[n] for n in names])
    small_res = [_unpack(r, shapes) for r in _adamw(pk(wts), pk(mom), pk(var), pk(gsm))]
    for j, n in enumerate(names):
        res[n] = [small_res[q][j] for q in range(4)]

    return (loss_out, grad_x, *[res[n][0] for n in _ORDER], *[res[n][1] for n in _ORDER],
            *[res[n][2] for n in _ORDER], *[res[n][3] for n in _ORDER])
```

```python
import functools
import math

import jax
import jax.numpy as jnp
from jax import lax
from jax.experimental import pallas as pl
from jax.experimental.pallas import tpu as pltpu

F32 = jnp.float32
BF16 = jnp.bfloat16
HI = lax.Precision.HIGHEST

N_META = 16
M_HEADS = 4
M_CHUNK = 64
PAD = M_CHUNK - N_META
CONV_W = 4
S5_GROUP = 16
S5_STATE = 64
S5_KCH = 4
LN_EPS = 1e-5
ALPHA = 2.0 ** 0.25
NEG = -1e30
ADAM_LR, ADAM_B1, ADAM_B2, ADAM_EPS, ADAM_WD, ADAM_STEP = 0.001, 0.9, 0.999, 1e-08, 0.01, 10

O_OFF, GS_OFF, GM_OFF, V_OFF, Q_OFF, K_OFF, U_OFF, G_OFF, NP = 0, 1024, 2048, 3072, 4096, 4608, 5120, 5632, 5760

NN = ((1,), (0,))
NT = ((1,), (1,))
TN = ((0,), (0,))


def _dot(a, b, dims=NN, prec=None):
    return lax.dot_general(a, b, (dims, ((), ())), preferred_element_type=F32, precision=prec)


def _bf(x):
    return x.astype(BF16)


def _sig(x):
    return 1.0 / (1.0 + jnp.exp(-x))


def _pcall(body, **kw):
    return pl.pallas_call(body, **kw)


def _cp(sem=None, vmem_mb=None):
    kw = {}
    if sem is not None:
        kw["dimension_semantics"] = sem
    if vmem_mb is not None:
        kw["vmem_limit_bytes"] = vmem_mb << 20
    return pltpu.CompilerParams(**kw)


def _row_tile(n, want, mult=16):
    best = None
    for t in range(mult, want + 1, mult):
        if n % t == 0:
            best = t
    assert best is not None, (n, want)
    return best


def _resident(shape):
    nd = len(shape)
    return pl.BlockSpec(shape, lambda *_: (0,) * nd, pipeline_mode=pl.Buffered(1))


def _const(shape):
    nd = len(shape)
    return pl.BlockSpec(shape, lambda *_: (0,) * nd)


def _ln_fwd(x, g, b):
    mu = jnp.mean(x, axis=-1, keepdims=True)
    xc = x - mu
    var = jnp.mean(xc * xc, axis=-1, keepdims=True)
    rstd = lax.rsqrt(var + LN_EPS)
    xhat = xc * rstd
    return xhat * g + b, xhat, rstd


def _ln_bwd(dy, xhat, rstd, g):
    dxh = dy * g
    m1 = jnp.mean(dxh, axis=-1, keepdims=True)
    m2 = jnp.mean(dxh * xhat, axis=-1, keepdims=True)
    return rstd * (dxh - m1 - xhat * m2)


def _colsum(x):
    return jnp.sum(x, axis=0, keepdims=True)


def _to_pad_cols(w):
    u, q, k, v, o, gi, gf, gs, gm = (w[..., 0:512], w[..., 512:1024], w[..., 1024:1536], w[..., 1536:2560],
                                     w[..., 2560:3584], w[..., 3584:3588], w[..., 3588:3592], w[..., 3592:4616],
                                     w[..., 4616:5640])
    z = jnp.zeros(w.shape[:-1] + (NP - G_OFF - 8,), w.dtype)
    return jnp.concatenate([o, gs, gm, v, q, k, u, gi, gf, z], axis=-1)


def _from_pad_cols(w):
    o, gs, gm, v, q, k, u = (w[..., O_OFF:GS_OFF], w[..., GS_OFF:GM_OFF], w[..., GM_OFF:V_OFF], w[..., V_OFF:Q_OFF],
                             w[..., Q_OFF:K_OFF], w[..., K_OFF:U_OFF], w[..., U_OFF:G_OFF])
    gi, gf = w[..., G_OFF:G_OFF + 4], w[..., G_OFF + 4:G_OFF + 8]
    return jnp.concatenate([u, q, k, v, o, gi, gf, gs, gm], axis=-1)


_IN_REF = (("u", 512), ("q", 512), ("k", 512), ("v", 1024), ("o", 1024), ("i", 4), ("f", 4), ("gs", 1024), ("gm", 1024))
_IN_PAD = (("o", O_OFF), ("gs", GS_OFF), ("gm", GM_OFF), ("v", V_OFF), ("q", Q_OFF), ("k", K_OFF), ("u", U_OFF),
           ("i", G_OFF), ("f", G_OFF + 4))


def _in_ref_ranges():
    out, off = {}, 0
    for n, s in _IN_REF:
        out[n] = (off, off + s)
        off += s
    return out, off


def _w_in_from_slots(g):
    rng, total = _in_ref_ranges()
    width = total // g.shape[0]
    cols = []
    for n, _ in _IN_PAD:
        a, b = rng[n]
        while a < b:
            s = a // width
            e = min(b, (s + 1) * width)
            cols.append(g[s][:, a - s * width:e - s * width])
            a = e
    cols.append(jnp.zeros((g.shape[1], NP - G_OFF - 8), g.dtype))
    return jnp.concatenate(cols, axis=1)


def _slots_from_w_in(wp, nslot=4):
    rng, total = _in_ref_ranges()
    width = total // nslot
    pad_off = dict(_IN_PAD)
    slots = []
    for s in range(nslot):
        lo, hi = s * width, (s + 1) * width
        cols = []
        for n, _ in _IN_REF:
            a, b = rng[n]
            x0, x1 = max(a, lo), min(b, hi)
            if x0 < x1:
                cols.append(wp[:, pad_off[n] + x0 - a:pad_off[n] + x1 - a])
        slots.append(jnp.concatenate(cols, axis=1))
    return jnp.stack(slots, axis=0)


def _ln0_fwd(hin, g, b, lp):
    r, d = hin.shape
    tm = _row_tile(lp, 416)

    def body(x_ref, g_ref, b_ref, o_ref):
        y, _, _ = _ln_fwd(x_ref[...], g_ref[...], b_ref[...])
        o_ref[...] = y

    return _pcall(
        body, name="ln0_fwd", grid=(r // tm,),
        in_specs=[pl.BlockSpec((tm, d), lambda i: (i, 0)), _const((1, d)), _const((1, d))],
        out_specs=pl.BlockSpec((tm, d), lambda i: (i, 0)),
        out_shape=jax.ShapeDtypeStruct((r, d), F32),
        compiler_params=_cp(("parallel",)),
    )(hin, g, b)


def _ln0_bwd(hin, dr1, dpw, g, lp):
    r, d = hin.shape
    tm = _row_tile(lp, 416)
    tps = lp // tm
    assert tm >= PAD + N_META

    def body(x_ref, a_ref, c_ref, g_ref, o_ref, dg_ref, db_ref, dm_ref):
        i = pl.program_id(0)

        @pl.when(i == 0)
        def _():
            dg_ref[...] = jnp.zeros_like(dg_ref)
            db_ref[...] = jnp.zeros_like(db_ref)
            dm_ref[...] = jnp.zeros_like(dm_ref)

        dy = ALPHA * a_ref[...] + c_ref[...]
        _, xhat, rstd = _ln_fwd(x_ref[...], g_ref[...], 0.0)
        dx = _ln_bwd(dy, xhat, rstd, g_ref[...])
        o_ref[...] = dx
        dg_ref[...] += _colsum(dy * xhat)
        db_ref[...] += _colsum(dy)

        @pl.when(i % tps == 0)
        def _():
            dm_ref[...] += dx[PAD:PAD + N_META, :]

    return _pcall(
        body, name="ln0_bwd", grid=(r // tm,),
        in_specs=[pl.BlockSpec((tm, d), lambda i: (i, 0))] * 3 + [_const((1, d))],
        out_specs=[pl.BlockSpec((tm, d), lambda i: (i, 0)), _const((1, d)), _const((1, d)), _const((N_META, d))],
        out_shape=[jax.ShapeDtypeStruct((r, d), F32), jax.ShapeDtypeStruct((1, d), F32),
                   jax.ShapeDtypeStruct((1, d), F32), jax.ShapeDtypeStruct((N_META, d), F32)],
        compiler_params=_cp(("arbitrary",)),
    )(hin, dr1, dpw, g)


def _inproj(h0, w_bf, bias, lp):
    r, d = h0.shape
    n = w_bf.shape[1]
    tm = _row_tile(lp, 832)
    tn = 1152
    tps = lp // tm

    def body(a_ref, w_ref, b_ref, o_ref):
        i = pl.program_id(0)
        acc = _dot(_bf(a_ref[...]), w_ref[...]) + b_ref[...]
        t = (i % tps) * tm + lax.broadcasted_iota(jnp.int32, (tm, 1), 0)
        o_ref[...] = jnp.where(t >= PAD, acc, 0.0)

    return _pcall(
        body, name="inproj", grid=(r // tm, n // tn),
        in_specs=[pl.BlockSpec((tm, d), lambda i, j: (i, 0)), pl.BlockSpec((d, tn), lambda i, j: (0, j)),
                  pl.BlockSpec((1, tn), lambda i, j: (0, j))],
        out_specs=pl.BlockSpec((tm, tn), lambda i, j: (i, j)),
        out_shape=jax.ShapeDtypeStruct((r, n), F32),
        compiler_params=_cp(("parallel", "parallel"), 48),
    )(h0, w_bf, bias)


def _mm_tn(a, b, *, name, split=1, colsum=False, tk_want=832):
    r, m = a.shape
    n = b.shape[1]
    tk = _row_tile(r, tk_want)
    tm = min(m, 1024)
    ns = n // split
    tn = ns
    for cand in (1024, 1152, 640, 512, 128):
        if ns % cand == 0 and cand <= ns:
            tn = cand
            break
    nb = ns // tn
    nk = r // tk

    def body(a_ref, b_ref, o_ref, *rest):
        acc = rest[-1]
        k = pl.program_id(2)

        @pl.when(k == 0)
        def _():
            acc[...] = jnp.zeros_like(acc)

        bt = b_ref[...]
        acc[...] += _dot(_bf(a_ref[...]), _bf(bt), TN)

        @pl.when(k == nk - 1)
        def _():
            o_ref[...] = acc[...]

        if colsum:
            cs_ref = rest[0]

            @pl.when(k == 0)
            def _():
                cs_ref[...] = jnp.zeros_like(cs_ref)

            cs_ref[...] += _colsum(bt.astype(F32))

    out_specs = [pl.BlockSpec((None, tm, tn), lambda i, j, k: (j // nb, i, j % nb))]
    out_shape = [jax.ShapeDtypeStruct((split, m, ns), F32)]
    if colsum:
        assert m == tm
        out_specs.append(pl.BlockSpec((1, tn), lambda i, j, k: (0, j)))
        out_shape.append(jax.ShapeDtypeStruct((1, n), F32))
    res = _pcall(
        body, name=name, grid=(m // tm, n // tn, nk),
        in_specs=[pl.BlockSpec((tk, tm), lambda i, j, k: (k, i)), pl.BlockSpec((tk, tn), lambda i, j, k: (k, j))],
        out_specs=out_specs, out_shape=out_shape,
        scratch_shapes=[pltpu.VMEM((tm, tn), F32)],
        compiler_params=_cp(("parallel", "parallel", "arbitrary"), 48),
    )(a, b)
    return res if colsum else res[0]


def _mm_nt(a, w_bf, lp, *, name):
    r, kdim = a.shape
    n = w_bf.shape[0]
    tm = _row_tile(lp, 832)
    tk = 1152
    nk = kdim // tk

    def body(a_ref, w_ref, o_ref, acc):
        k = pl.program_id(1)

        @pl.when(k == 0)
        def _():
            acc[...] = jnp.zeros_like(acc)

        acc[...] += _dot(_bf(a_ref[...]), w_ref[...], NT)

        @pl.when(k == nk - 1)
        def _():
            o_ref[...] = acc[...]

    return _pcall(
        body, name=name, grid=(r // tm, nk),
        in_specs=[pl.BlockSpec((tm, tk), lambda i, k: (i, k)), pl.BlockSpec((n, tk), lambda i, k: (0, k))],
        out_specs=pl.BlockSpec((tm, n), lambda i, k: (i, 0)),
        out_shape=jax.ShapeDtypeStruct((r, n), F32),
        scratch_shapes=[pltpu.VMEM((tm, n), F32)],
        compiler_params=_cp(("parallel", "arbitrary"), 48),
    )(a, w_bf)


def _s5_prep(lam_re, lam_im, log_dt, b_re_t, b_im_t):
    g, p = lam_re.shape
    h = b_re_t.shape[0]

    def body(lr_ref, li_ref, ldt_ref, br_ref, bi_ref, pr_ref, pi_ref, bbr_ref, bbi_ref):
        lr, li = lr_ref[...], li_ref[...]
        dt = jnp.exp(ldt_ref[...])
        e = jnp.exp(lr * dt)
        ar, ai = e * jnp.cos(li * dt), e * jnp.sin(li * dt)
        den = lr * lr + li * li
        cr = ((ar - 1.0) * lr + ai * li) / den
        ci = (ai * lr - (ar - 1.0) * li) / den
        br, bi = br_ref[...], bi_ref[...]
        bbr_ref[...] = cr[None] * br - ci[None] * bi
        bbi_ref[...] = cr[None] * bi + ci[None] * br
        xr, xi = ar, ai
        pr_ref[0] = xr
        pi_ref[0] = xi
        for t in range(1, 8):
            xr, xi = xr * ar - xi * ai, xr * ai + xi * ar
            pr_ref[t] = xr
            pi_ref[t] = xi

    sd = jax.ShapeDtypeStruct
    return _pcall(body, name="s5_prep",
                  out_shape=[sd((8, g, p), F32), sd((8, g, p), F32), sd((h, g, p), F32), sd((h, g, p), F32)])(
        lam_re, lam_im, log_dt, b_re_t, b_im_t)


def _s5_prep_bwd(lam_re, lam_im, log_dt, b_re_t, b_im_t, da_re, da_im, dbb_re_t, dbb_im_t):
    g, p = lam_re.shape
    h = b_re_t.shape[0]

    def body(lr_ref, li_ref, ldt_ref, br_ref, bi_ref, dar_ref, dai_ref, dbr_ref, dbi_ref,
             glr_ref, gli_ref, gdt_ref, gbr_ref, gbi_ref):
        lr, li = lr_ref[...], li_ref[...]
        dt = jnp.exp(ldt_ref[...])
        e = jnp.exp(lr * dt)
        ar, ai = e * jnp.cos(li * dt), e * jnp.sin(li * dt)
        den = lr * lr + li * li
        cr = ((ar - 1.0) * lr + ai * li) / den
        ci = (ai * lr - (ar - 1.0) * li) / den
        br, bi = br_ref[...], bi_ref[...]
        gr, gi = dbr_ref[...], dbi_ref[...]
        gbr_ref[...] = gr * cr[None] + gi * ci[None]
        gbi_ref[...] = gi * cr[None] - gr * ci[None]
        gcr = jnp.sum(gr * br + gi * bi, axis=0)
        gci = jnp.sum(gi * br - gr * bi, axis=0)
        ilr, ili = lr / den, -li / den
        gar = dar_ref[...] + gcr * ilr + gci * ili
        gai = dai_ref[...] + gci * ilr - gcr * ili
        qr, qi = cr * ilr - ci * ili, cr * ili + ci * ilr
        glr = -(gcr * qr + gci * qi)
        gli = -(gci * qr - gcr * qi)
        gzr = gar * ar + gai * ai
        gzi = gai * ar - gar * ai
        glr_ref[...] = glr + gzr * dt
        gli_ref[...] = gli + gzi * dt
        gdt_ref[...] = jnp.sum(gzr * lr + gzi * li, axis=1, keepdims=True) * dt

    sd = jax.ShapeDtypeStruct
    return _pcall(body, name="s5_prep_bwd",
                  out_shape=[sd((g, p), F32), sd((g, p), F32), sd((g, 1), F32), sd((h, g, p), F32), sd((h, g, p), F32)])(
        lam_re, lam_im, log_dt, b_re_t, b_im_t, da_re, da_im, dbb_re_t, dbb_im_t)


def _cmul(xr, xi, yr, yi):
    return xr * yr - xi * yi, xr * yi + xi * yr


def _dot5(a, b, dims=NN):
    return _dot(_bf(a), _bf(b), dims)


def _s5_fwd(p3, bk, cre, cim, apow, dskip):
    bsz, lp, _ = p3.shape
    tt = _row_tile(lp, 520, 8)
    nt = lp // tt
    nblk = tt // 8
    hw = 512

    def body(u_ref, bk_ref, cre_ref, cim_ref, ap_ref, d_ref, y_ref, xs_ref, car_ref):
        t = pl.program_id(2)

        @pl.when(t == 0)
        def _():
            car_ref[...] = jnp.zeros_like(car_ref)

        u = u_ref[...]
        xs_ref[...] = _dot5(u, bk_ref[...])
        ap = ap_ref[...]
        apr, api = ap[:, :hw], ap[:, hw:]
        rows = lax.broadcasted_iota(jnp.int32, (8, hw), 0)

        def blk(i, carry):
            cr, ci = carry
            off = pl.multiple_of(i * 8, 8)
            x = xs_ref[pl.ds(off, 8), :]
            xr, xi = x[:, :hw], x[:, hw:]
            for d in (1, 2, 4):
                sr = jnp.where(rows < d, 0.0, pltpu.roll(xr, d, 0))
                si = jnp.where(rows < d, 0.0, pltpu.roll(xi, d, 0))
                mr, mi = _cmul(sr, si, apr[d - 1:d, :], api[d - 1:d, :])
                xr, xi = xr + mr, xi + mi
            mr, mi = _cmul(apr, api, cr, ci)
            xr, xi = xr + mr, xi + mi
            xs_ref[pl.ds(off, 8), :] = jnp.concatenate([xr, xi], axis=1)
            return xr[7:8, :], xi[7:8, :]

        c0 = car_ref[...]
        cr, ci = lax.fori_loop(0, nblk, blk, (c0[0:1, :hw], c0[0:1, hw:]))
        car_ref[...] = jnp.broadcast_to(jnp.concatenate([cr, ci], axis=1), car_ref.shape)
        xs = xs_ref[...]
        y_ref[...] = (_dot5(xs[:, :hw], cre_ref[...]) - _dot5(xs[:, hw:], cim_ref[...])
                      + d_ref[...] * u)

    ub = U_OFF // 128
    return _pcall(
        body, name="s5_fwd", grid=(S5_KCH, bsz, nt),
        in_specs=[pl.BlockSpec((None, tt, 128), lambda k, b, t: (b, t, ub + k)),
                  pl.BlockSpec((None, 128, 2 * hw), lambda k, b, t: (k, 0, 0)),
                  pl.BlockSpec((None, hw, 128), lambda k, b, t: (k, 0, 0)),
                  pl.BlockSpec((None, hw, 128), lambda k, b, t: (k, 0, 0)),
                  pl.BlockSpec((None, 8, 2 * hw), lambda k, b, t: (k, 0, 0)),
                  pl.BlockSpec((1, 128), lambda k, b, t: (0, k))],
        out_specs=[pl.BlockSpec((None, tt, 128), lambda k, b, t: (b, t, k)),
                   pl.BlockSpec((None, None, tt, 2 * hw), lambda k, b, t: (b, k, t, 0))],
        out_shape=[jax.ShapeDtypeStruct((bsz, lp, S5_KCH * 128), F32),
                   jax.ShapeDtypeStruct((bsz, S5_KCH, lp, 2 * hw), F32)],
        scratch_shapes=[pltpu.VMEM((8, 2 * hw), F32)],
        compiler_params=_cp(("parallel", "parallel", "arbitrary"), 40),
    )(p3, bk, cre, cim, apow, dskip)


def _s5_bwd(dp3, p3, dy3, xs, bk, cre, cim, apow_rev, dskip):
    bsz, lp, _ = p3.shape
    tt = _row_tile(lp, 520, 8)
    nt = lp // tt
    nblk = tt // 8
    hw = 512
    tb = tt // 8

    def body(dp_any, u_ref, dy_ref, xs_ref, halo_ref, bk_ref, cre_ref, cim_ref, ap_ref, d_ref,
             du_ref, dbk_ref, dcre_ref, dcim_ref, da_ref, dd_ref, g_ref, ext_ref, car_ref):
        b = pl.program_id(1)
        t = pl.program_id(2)
        tidx = nt - 1 - t

        @pl.when(t == 0)
        def _():
            car_ref[...] = jnp.zeros_like(car_ref)

        @pl.when((b == 0) & (t == 0))
        def _():
            dbk_ref[...] = jnp.zeros_like(dbk_ref)
            dcre_ref[...] = jnp.zeros_like(dcre_ref)
            dcim_ref[...] = jnp.zeros_like(dcim_ref)
            da_ref[...] = jnp.zeros_like(da_ref)
            dd_ref[...] = jnp.zeros_like(dd_ref)

        u = u_ref[...]
        dy = dy_ref[...]
        g_ref[:, :hw] = _dot5(dy, cre_ref[...], NT)
        g_ref[:, hw:] = -_dot5(dy, cim_ref[...], NT)
        ap = ap_ref[...]
        apr, api = ap[:, :hw], -ap[:, hw:]
        rows = lax.broadcasted_iota(jnp.int32, (8, hw), 0)

        def blk(i, carry):
            cr, ci = carry
            off = pl.multiple_of((nblk - 1 - i) * 8, 8)
            x = g_ref[pl.ds(off, 8), :]
            xr, xi = x[:, :hw], x[:, hw:]
            for d in (1, 2, 4):
                sr = jnp.where(rows >= 8 - d, 0.0, pltpu.roll(xr, 8 - d, 0))
                si = jnp.where(rows >= 8 - d, 0.0, pltpu.roll(xi, 8 - d, 0))
                mr, mi = _cmul(sr, si, apr[8 - d:9 - d, :], api[8 - d:9 - d, :])
                xr, xi = xr + mr, xi + mi
            mr, mi = _cmul(apr, api, cr, ci)
            xr, xi = xr + mr, xi + mi
            g_ref[pl.ds(off, 8), :] = jnp.concatenate([xr, xi], axis=1)
            return xr[0:1, :], xi[0:1, :]

        c0 = car_ref[...]
        cr, ci = lax.fori_loop(0, nblk, blk, (c0[0:1, :hw], c0[0:1, hw:]))
        car_ref[...] = jnp.broadcast_to(jnp.concatenate([cr, ci], axis=1), car_ref.shape)

        gg = g_ref[...]
        du = _dot5(gg, bk_ref[...], NT) + d_ref[...] * dy
        trow = tidx * tt + lax.broadcasted_iota(jnp.int32, (tt, 1), 0)
        du_ref[...] = jnp.where(trow >= PAD, du, 0.0)
        dbk_ref[...] += _dot5(u, gg, TN)
        xsv = xs_ref[...]
        dcre_ref[...] += _dot5(xsv[:, :hw], dy, TN)
        dcim_ref[...] -= _dot5(xsv[:, hw:], dy, TN)
        dd_ref[...] += _colsum(dy * u)
        ext_ref[0:8, :] = jnp.where(tidx == 0, 0.0, halo_ref[...])
        ext_ref[8:, :] = xsv
        xp = ext_ref[pl.ds(7, tt), :]
        gr, gi, pr, pi = gg[:, :hw], gg[:, hw:], xp[:, :hw], xp[:, hw:]
        da_ref[:, :hw] += _colsum(gr * pr + gi * pi)
        da_ref[:, hw:] += _colsum(gi * pr - gr * pi)

    ub = U_OFF // 128
    sd = jax.ShapeDtypeStruct
    rt = lambda t: nt - 1 - t
    res = _pcall(
        body, name="s5_bwd", grid=(S5_KCH, bsz, nt),
        in_specs=[pl.BlockSpec(memory_space=pl.ANY),
                  pl.BlockSpec((None, tt, 128), lambda k, b, t: (b, rt(t), ub + k)),
                  pl.BlockSpec((None, tt, 128), lambda k, b, t: (b, rt(t), k)),
                  pl.BlockSpec((None, None, tt, 2 * hw), lambda k, b, t: (b, k, rt(t), 0)),
                  pl.BlockSpec((None, None, 8, 2 * hw), lambda k, b, t: (b, k, jnp.maximum(rt(t) * tb - 1, 0), 0)),
                  pl.BlockSpec((None, 128, 2 * hw), lambda k, b, t: (k, 0, 0)),
                  pl.BlockSpec((None, hw, 128), lambda k, b, t: (k, 0, 0)),
                  pl.BlockSpec((None, hw, 128), lambda k, b, t: (k, 0, 0)),
                  pl.BlockSpec((None, 8, 2 * hw), lambda k, b, t: (k, 0, 0)),
                  pl.BlockSpec((1, 128), lambda k, b, t: (0, k))],
        out_specs=[pl.BlockSpec((None, tt, 128), lambda k, b, t: (b, rt(t), ub + k)),
                   pl.BlockSpec((None, 128, 2 * hw), lambda k, b, t: (k, 0, 0)),
                   pl.BlockSpec((None, hw, 128), lambda k, b, t: (k, 0, 0)),
                   pl.BlockSpec((None, hw, 128), lambda k, b, t: (k, 0, 0)),
                   pl.BlockSpec((None, 1, 2 * hw), lambda k, b, t: (k, 0, 0)),
                   pl.BlockSpec((1, 128), lambda k, b, t: (0, k))],
        out_shape=[sd(dp3.shape, F32), sd((S5_KCH, 128, 2 * hw), F32), sd((S5_KCH, hw, 128), F32),
                   sd((S5_KCH, hw, 128), F32), sd((S5_KCH, 1, 2 * hw), F32), sd((1, S5_KCH * 128), F32)],
        scratch_shapes=[pltpu.VMEM((tt, 2 * hw), F32), pltpu.VMEM((tt + 8, 2 * hw), F32), pltpu.VMEM((8, 2 * hw), F32)],
        input_output_aliases={0: 0},
        compiler_params=_cp(("arbitrary", "arbitrary", "arbitrary"), 48),
    )(dp3, p3, dy3, xs, xs, bk, cre, cim, apow_rev, dskip)
    return res


_G0 = math.sqrt(2.0 / math.pi)
_G1 = 0.044715


def _gelu(y):
    return 0.5 * y * (1.0 + jnp.tanh(_G0 * (y + _G1 * y * y * y)))


def _gelu_grad(y):
    th = jnp.tanh(_G0 * (y + _G1 * y * y * y))
    return 0.5 * (1.0 + th) + 0.5 * y * (1.0 - th * th) * _G0 * (1.0 + 3.0 * _G1 * y * y)


def _glu_fwd(y_s5, wglu_g, lp):
    r, w = y_s5.shape
    tm = _row_tile(lp, 416)
    cw = wglu_g.shape[2]

    def body(y_ref, w_ref, gy_ref, z_ref, o_ref):
        gy = _bf(_gelu(y_ref[...]))
        gy_ref[...] = gy
        zs = [_dot(gy, w_ref[s]) for s in range(4)]
        for s in range(4):
            z_ref[:, s * cw:(s + 1) * cw] = zs[s]
        o_ref[:, :cw] = zs[0] * _sig(zs[2])
        o_ref[:, cw:] = zs[1] * _sig(zs[3])

    sd = jax.ShapeDtypeStruct
    return _pcall(
        body, name="glu_fwd", grid=(r // tm,),
        in_specs=[pl.BlockSpec((tm, w), lambda i: (i, 0)), _resident(wglu_g.shape)],
        out_specs=[pl.BlockSpec((tm, w), lambda i: (i, 0)), pl.BlockSpec((tm, 4 * cw), lambda i: (i, 0)),
                   pl.BlockSpec((tm, 2 * cw), lambda i: (i, 0))],
        out_shape=[sd((r, w), BF16), sd((r, 4 * cw), F32), sd((r, 2 * cw), F32)],
        compiler_params=_cp(("parallel",), 40),
    )(y_s5, wglu_g)


def _glu_bwd(dyg, z, y_s5, wglu_g, lp):
    r, w = y_s5.shape
    tm = _row_tile(lp, 416)
    cw = wglu_g.shape[2]

    def body(d_ref, z_ref, y_ref, w_ref, dz_ref, dy_ref):
        d = d_ref[...]
        zz = z_ref[...]
        acc = jnp.zeros((tm, w), F32)
        for s in range(2):
            z1 = zz[:, s * cw:(s + 1) * cw]
            sg = _sig(zz[:, (2 + s) * cw:(3 + s) * cw])
            dd = d[:, s * cw:(s + 1) * cw]
            dz1 = _bf(dd * sg)
            dz2 = _bf(dd * z1 * sg * (1.0 - sg))
            dz_ref[:, s * cw:(s + 1) * cw] = dz1
            dz_ref[:, (2 + s) * cw:(3 + s) * cw] = dz2
            acc += _dot(dz1, w_ref[s], NT) + _dot(dz2, w_ref[2 + s], NT)
        dy_ref[...] = acc * _gelu_grad(y_ref[...])

    sd = jax.ShapeDtypeStruct
    return _pcall(
        body, name="glu_bwd", grid=(r // tm,),
        in_specs=[pl.BlockSpec((tm, 2 * cw), lambda i: (i, 0)), pl.BlockSpec((tm, 4 * cw), lambda i: (i, 0)),
                  pl.BlockSpec((tm, w), lambda i: (i, 0)), _resident(wglu_g.shape)],
        out_specs=[pl.BlockSpec((tm, 4 * cw), lambda i: (i, 0)), pl.BlockSpec((tm, w), lambda i: (i, 0))],
        out_shape=[sd((r, 4 * cw), BF16), sd((r, w), F32)],
        compiler_params=_cp(("parallel",), 40),
    )(dyg, z, y_s5, wglu_g)


def _conv_fwd(p3, cw, cb):
    bsz, lp, _ = p3.shape
    tt = _row_tile(lp, 416)
    nt = lp // tt
    tb = tt // 8
    c = cw.shape[1]
    qb = Q_OFF // c

    def body(x_ref, halo_ref, w_ref, b_ref, pre_ref, act_ref, ext_ref):
        t = pl.program_id(1)
        ext_ref[0:8, :] = jnp.where(t == 0, 0.0, halo_ref[...])
        ext_ref[8:, :] = x_ref[...]
        w = w_ref[...]
        acc = b_ref[...] + w[0:1, :] * ext_ref[pl.ds(5, tt), :]
        for j in range(1, CONV_W):
            acc = acc + w[j:j + 1, :] * ext_ref[pl.ds(5 + j, tt), :]
        pre_ref[...] = acc
        act_ref[...] = acc * _sig(acc)

    sd = jax.ShapeDtypeStruct
    return _pcall(
        body, name="conv_fwd", grid=(bsz, nt),
        in_specs=[pl.BlockSpec((None, tt, c), lambda b, t: (b, t, qb)),
                  pl.BlockSpec((None, 8, c), lambda b, t: (b, jnp.maximum(t * tb - 1, 0), qb)),
                  _const((CONV_W, c)), _const((1, c))],
        out_specs=[pl.BlockSpec((None, tt, c), lambda b, t: (b, t, 0))] * 2,
        out_shape=[sd((bsz, lp, c), F32)] * 2,
        scratch_shapes=[pltpu.VMEM((tt + 8, c), F32)],
        compiler_params=_cp(("parallel", "parallel")),
    )(p3, p3, cw, cb)


def _conv_bwd(dp3, p3, dact3, pre3, cw):
    bsz, lp, _ = p3.shape
    tt = _row_tile(lp, 416)
    nt = lp // tt
    tb = tt // 8
    c = cw.shape[1]
    qb = Q_OFF // c

    def silu_grad(x):
        s = _sig(x)
        return s * (1.0 + x * (1.0 - s))

    def body(dp_any, x_ref, xh_ref, d_ref, dh_ref, pre_ref, preh_ref, w_ref, o_ref, dw_ref, db_ref, ext_ref, dext_ref):
        b = pl.program_id(0)
        t = pl.program_id(1)

        @pl.when((b == 0) & (t == 0))
        def _():
            dw_ref[...] = jnp.zeros_like(dw_ref)
            db_ref[...] = jnp.zeros_like(db_ref)

        dc = d_ref[...] * silu_grad(pre_ref[...])
        dch = jnp.where(t == nt - 1, 0.0, dh_ref[...] * silu_grad(preh_ref[...]))
        dext_ref[0:tt, :] = dc
        dext_ref[tt:, :] = dch
        ext_ref[0:8, :] = jnp.where(t == 0, 0.0, xh_ref[...])
        ext_ref[8:, :] = x_ref[...]
        w = w_ref[...]
        acc = w[CONV_W - 1:CONV_W, :] * dc
        for j in range(CONV_W - 1):
            acc = acc + w[j:j + 1, :] * dext_ref[pl.ds(CONV_W - 1 - j, tt), :]
        trow = t * tt + lax.broadcasted_iota(jnp.int32, (tt, 1), 0)
        o_ref[...] = jnp.where(trow >= PAD, acc, 0.0)
        db_ref[...] += _colsum(dc)
        for j in range(CONV_W):
            dw_ref[j:j + 1, :] += _colsum(dc * ext_ref[pl.ds(5 + j, tt), :])

    sd = jax.ShapeDtypeStruct
    nxt = lambda t: jnp.minimum((t + 1) * tb, lp // 8 - 1)
    return _pcall(
        body, name="conv_bwd", grid=(bsz, nt),
        in_specs=[pl.BlockSpec(memory_space=pl.ANY),
                  pl.BlockSpec((None, tt, c), lambda b, t: (b, t, qb)),
                  pl.BlockSpec((None, 8, c), lambda b, t: (b, jnp.maximum(t * tb - 1, 0), qb)),
                  pl.BlockSpec((None, tt, c), lambda b, t: (b, t, 0)),
                  pl.BlockSpec((None, 8, c), lambda b, t: (b, nxt(t), 0)),
                  pl.BlockSpec((None, tt, c), lambda b, t: (b, t, 0)),
                  pl.BlockSpec((None, 8, c), lambda b, t: (b, nxt(t), 0)),
                  _const((CONV_W, c))],
        out_specs=[pl.BlockSpec((None, tt, c), lambda b, t: (b, t, qb)), _const((CONV_W, c)), _const((1, c))],
        out_shape=[sd(dp3.shape, F32), sd((CONV_W, c), F32), sd((1, c), F32)],
        scratch_shapes=[pltpu.VMEM((tt + 8, c), F32), pltpu.VMEM((tt + 8, c), F32)],
        input_output_aliases={0: 0},
        compiler_params=_cp(("arbitrary", "arbitrary")),
    )(dp3, p3, p3, dact3, dact3, pre3, pre3, cw)


def _mlstm_gates(g, h_idx, c_idx, lc):
    lane = lax.broadcasted_iota(jnp.int32, g.shape, 1)
    i_col = jnp.sum(jnp.where(lane == h_idx, g, 0.0), axis=1, keepdims=True)
    f_col = jnp.sum(jnp.where(lane == M_HEADS + h_idx, g, 0.0), axis=1, keepdims=True)
    row = lax.broadcasted_iota(jnp.int32, (lc, 1), 0)
    valid = (c_idx * lc + row) >= PAD
    li = jnp.where(valid, i_col, NEG)
    lf = jnp.where(valid, jnp.minimum(f_col, 0.0) - jnp.log(1.0 + jnp.exp(-jnp.abs(f_col))), 0.0)
    r2 = lax.broadcasted_iota(jnp.int32, (lc, lc), 0)
    c2 = lax.broadcasted_iota(jnp.int32, (lc, lc), 1)
    eye = r2 == c2
    tril = r2 >= c2
    to_row = lambda col: jnp.sum(jnp.where(eye, col, 0.0), axis=0, keepdims=True)
    lf_row = to_row(lf)
    b_col = jnp.sum(jnp.where(tril, lf_row, 0.0), axis=1, keepdims=True)
    b_row = to_row(b_col)
    li_row = to_row(li)
    d_mat = jnp.where(tril, b_col - b_row + li_row, NEG)
    return dict(f_col=f_col, valid=valid, li=li, b_col=b_col, d_mat=d_mat, eye=eye, r2=r2, c2=c2, row=row,
                to_row=to_row)


def _mlstm_chunk(q, ks, v, gq, c_st, n_st, m_st, lc):
    b_col, d_mat = gq["b_col"], gq["d_mat"]
    m_inter = b_col + m_st
    m_row = jnp.maximum(m_inter, jnp.max(d_mat, axis=1, keepdims=True))
    w_intra = jnp.exp(d_mat - m_row)
    w_inter = jnp.exp(m_inter - m_row)
    qb, kb, vb, cb = _bf(q), _bf(ks), _bf(v), _bf(c_st)
    s = _dot(qb, kb, NT) * w_intra
    qc = _dot(qb, cb)
    num = _dot(_bf(s), vb) + w_inter * qc
    qn = jnp.sum(q * n_st, axis=1, keepdims=True)
    den = jnp.sum(s, axis=1, keepdims=True) + w_inter * qn
    e = jnp.exp(-m_row)
    nn = jnp.maximum(jnp.abs(den), e)
    b_last = b_col[lc - 1:lc, :]
    g_log = b_last - b_col + gq["li"]
    m_new = jnp.maximum(b_last + m_st, jnp.max(g_log, axis=0, keepdims=True))
    w_k = jnp.exp(g_log - m_new)
    decay = jnp.exp(b_last + m_st - m_new)
    return dict(w_intra=w_intra, w_inter=w_inter, qb=qb, kb=kb, vb=vb, cb=cb, s=s, qc=qc, num=num, qn=qn, den=den,
                e=e, nn=nn, m_new=m_new, w_k=w_k, decay=decay)


def _mlstm_fwd(qk3, p3):
    bsz, lp, _ = p3.shape
    lc = M_CHUNK
    nc = lp // lc
    dk, dv = 128, 256
    scale = dk ** -0.5

    def body(q_ref, k_ref, v_ref, g_ref, h_ref, cs_ref, ns_ref, ms_ref, c_sc, n_sc, m_sc):
        c = pl.program_id(1)

        @pl.when(c == 0)
        def _():
            c_sc[...] = jnp.zeros_like(c_sc)
            n_sc[...] = jnp.zeros_like(n_sc)
            m_sc[...] = jnp.zeros_like(m_sc)

        g = g_ref[...]
        for hh in range(M_HEADS):
            c_st, n_st, m_all = c_sc[hh], n_sc[hh], m_sc[hh]
            cs_ref[hh] = c_st
            ns_ref[hh] = n_st
            ms_ref[hh] = m_all
            m_st = m_all[:, 0:1]
            q = q_ref[:, hh * dk:(hh + 1) * dk]
            ks = k_ref[:, hh * dk:(hh + 1) * dk] * scale
            v = v_ref[:, hh * dv:(hh + 1) * dv]
            gq = _mlstm_gates(g, hh, c, lc)
            f = _mlstm_chunk(q, ks, v, gq, c_st, n_st, m_st, lc)
            h_ref[:, hh * dv:(hh + 1) * dv] = f["num"] / f["nn"]
            kw = ks * f["w_k"]
            c_sc[hh] = f["decay"] * c_st + _dot(_bf(kw), f["vb"], TN)
            n_sc[hh] = f["decay"] * n_st + _colsum(kw)
            m_sc[hh] = jnp.broadcast_to(f["m_new"], (1, 128))

    sd = jax.ShapeDtypeStruct
    nh = M_HEADS
    return _pcall(
        body, name="mlstm_fwd", grid=(bsz, nc),
        in_specs=[pl.BlockSpec((None, lc, nh * dk), lambda b, c: (b, c, 0)),
                  pl.BlockSpec((None, lc, nh * dk), lambda b, c: (b, c, 1)),
                  pl.BlockSpec((None, lc, nh * dv), lambda b, c: (b, c, V_OFF // (nh * dv))),
                  pl.BlockSpec((None, lc, 128), lambda b, c: (b, c, G_OFF // 128))],
        out_specs=[pl.BlockSpec((None, lc, nh * dv), lambda b, c: (b, c, 0)),
                   pl.BlockSpec((None, nh, None, dk, dv), lambda b, c: (b, 0, c, 0, 0)),
                   pl.BlockSpec((None, nh, None, 1, dk), lambda b, c: (b, 0, c, 0, 0)),
                   pl.BlockSpec((None, nh, None, 1, 128), lambda b, c: (b, 0, c, 0, 0))],
        out_shape=[sd((bsz, lp, nh * dv), F32), sd((bsz, nh, nc, dk, dv), F32),
                   sd((bsz, nh, nc, 1, dk), F32), sd((bsz, nh, nc, 1, 128), F32)],
        scratch_shapes=[pltpu.VMEM((nh, dk, dv), F32), pltpu.VMEM((nh, 1, dk), F32), pltpu.VMEM((nh, 1, 128), F32)],
        compiler_params=_cp(("parallel", "arbitrary")),
    )(qk3, qk3, p3, p3)


def _mlstm_bwd(dp3, qk3, p3, dh3, cs, ns, ms):
    bsz, lp, _ = p3.shape
    lc = M_CHUNK
    nc = lp // lc
    dk, dv = 128, 256
    scale = dk ** -0.5

    def body(dp_any, q_ref, k_ref, v_ref, g_ref, dh_ref, cs_ref, ns_ref, ms_ref,
             dv_ref, dqk_ref, dg_ref, dc_sc, dn_sc):
        t = pl.program_id(1)
        c = nc - 1 - t

        @pl.when(t == 0)
        def _():
            dc_sc[...] = jnp.zeros_like(dc_sc)
            dn_sc[...] = jnp.zeros_like(dn_sc)

        g = g_ref[...]
        lane = lax.broadcasted_iota(jnp.int32, (lc, 128), 1)
        dgate = jnp.zeros((lc, 128), F32)
        for hh in range(M_HEADS):
            dgate = head(hh, c, g, lane, dgate, q_ref, k_ref, v_ref, dh_ref, cs_ref, ns_ref, ms_ref,
                         dv_ref, dqk_ref, dc_sc, dn_sc)
        dg_ref[...] = dgate

    def head(hh, c, g, lane, dgate, q_ref, k_ref, v_ref, dh_ref, cs_ref, ns_ref, ms_ref, dv_ref, dqk_ref, dc_sc, dn_sc):
        c_st, n_st = cs_ref[hh], ns_ref[hh]
        m_st = ms_ref[hh][:, 0:1]
        q = q_ref[:, hh * dk:(hh + 1) * dk]
        ks = k_ref[:, hh * dk:(hh + 1) * dk] * scale
        v = v_ref[:, hh * dv:(hh + 1) * dv]
        dh = dh_ref[:, hh * dv:(hh + 1) * dv]
        gq = _mlstm_gates(g, hh, c, lc)
        f = _mlstm_chunk(q, ks, v, gq, c_st, n_st, m_st, lc)
        eye, r2, c2, row, valid = gq["eye"], gq["r2"], gq["c2"], gq["row"], gq["valid"]
        w_intra, w_inter, s, nn, den = f["w_intra"], f["w_inter"], f["s"], f["nn"], f["den"]
        qb, kb, vb, cb, w_k, decay = f["qb"], f["kb"], f["vb"], f["cb"], f["w_k"], f["decay"]
        d_c, d_n = dc_sc[hh], dn_sc[hh]
        d_cb = _bf(d_c)

        hout = f["num"] / nn
        dnum = dh / nn
        d_nn = -jnp.sum(dh * hout, axis=1, keepdims=True) / nn
        dden = jnp.where(jnp.abs(den) > f["e"], d_nn * jnp.sign(den), 0.0)
        wdnum = w_inter * dnum
        wdden = w_inter * dden
        ds = _dot(_bf(dnum), vb, NT) + dden
        dsw = _bf(ds * w_intra)
        dq = _dot(dsw, kb) + _dot(_bf(wdnum), cb, NT) + wdden * n_st
        dkw = _dot(vb, d_cb, NT) + d_n
        dks = _dot(dsw, qb, TN) + dkw * w_k
        kw = ks * w_k
        dvv = _dot(_bf(s), _bf(dnum), TN) + _dot(_bf(kw), d_cb)
        dd = ds * s
        rs = jnp.sum(dd, axis=1, keepdims=True)
        cs_col = jnp.sum(jnp.where(eye, jnp.sum(dd, axis=0, keepdims=True), 0.0), axis=1, keepdims=True)
        dwi = jnp.sum(dnum * f["qc"], axis=1, keepdims=True) + dden * f["qn"]
        db = rs - cs_col + dwi * w_inter
        dli = cs_col
        ddecay = jnp.sum(jnp.sum(d_c * c_st, axis=1, keepdims=True), axis=0, keepdims=True) \
            + jnp.sum(d_n * n_st, axis=1, keepdims=True)
        dgl = jnp.sum(dkw * ks, axis=1, keepdims=True) * w_k
        dblast = ddecay * decay + jnp.sum(dgl, axis=0, keepdims=True)
        db = db - dgl + jnp.where(row == lc - 1, dblast, 0.0)
        dli = dli + dgl
        db_row = gq["to_row"](db)
        dlf = jnp.sum(jnp.where(c2 >= r2, db_row, 0.0), axis=1, keepdims=True)
        dlf = jnp.where(valid, dlf, 0.0)
        dgate = jnp.where(lane == hh, jnp.where(valid, dli, 0.0), dgate)
        dgate = jnp.where(lane == M_HEADS + hh, dlf * _sig(-gq["f_col"]), dgate)
        dqk_ref[:, hh * dk:(hh + 1) * dk] = dq
        dqk_ref[:, (M_HEADS + hh) * dk:(M_HEADS + hh + 1) * dk] = dks * scale
        dv_ref[:, hh * dv:(hh + 1) * dv] = dvv
        dc_sc[hh] = decay * d_c + _dot(qb, _bf(wdnum), TN)
        dn_sc[hh] = decay * d_n + _colsum(q * wdden)
        return dgate

    sd = jax.ShapeDtypeStruct
    nh = M_HEADS
    rc = lambda c: nc - 1 - c
    return _pcall(
        body, name="mlstm_bwd", grid=(bsz, nc),
        in_specs=[pl.BlockSpec(memory_space=pl.ANY),
                  pl.BlockSpec((None, lc, nh * dk), lambda b, c: (b, rc(c), 0)),
                  pl.BlockSpec((None, lc, nh * dk), lambda b, c: (b, rc(c), 1)),
                  pl.BlockSpec((None, lc, nh * dv), lambda b, c: (b, rc(c), V_OFF // (nh * dv))),
                  pl.BlockSpec((None, lc, 128), lambda b, c: (b, rc(c), G_OFF // 128)),
                  pl.BlockSpec((None, lc, nh * dv), lambda b, c: (b, rc(c), 0)),
                  pl.BlockSpec((None, nh, None, dk, dv), lambda b, c: (b, 0, rc(c), 0, 0)),
                  pl.BlockSpec((None, nh, None, 1, dk), lambda b, c: (b, 0, rc(c), 0, 0)),
                  pl.BlockSpec((None, nh, None, 1, 128), lambda b, c: (b, 0, rc(c), 0, 0))],
        out_specs=[pl.BlockSpec((None, lc, nh * dv), lambda b, c: (b, rc(c), V_OFF // (nh * dv))),
                   pl.BlockSpec((None, lc, 2 * nh * dk), lambda b, c: (b, rc(c), 0)),
                   pl.BlockSpec((None, lc, 128), lambda b, c: (b, rc(c), 0))],
        out_shape=[sd(dp3.shape, F32), sd((bsz, lp, 2 * nh * dk), F32), sd((bsz, lp, 128), F32)],
        scratch_shapes=[pltpu.VMEM((nh, dk, dv), F32), pltpu.VMEM((nh, 1, dk), F32)],
        input_output_aliases={0: 0},
        compiler_params=_cp(("arbitrary", "arbitrary")),
    )(dp3, qk3, qk3, p3, p3, dh3, cs, ns, ms)


def _headnorm(x):
    dv = x.shape[1] // M_HEADS
    xh, rs = [], []
    for h in range(M_HEADS):
        xx = x[:, h * dv:(h + 1) * dv]
        mu = jnp.mean(xx, axis=-1, keepdims=True)
        xc = xx - mu
        rstd = lax.rsqrt(jnp.mean(xc * xc, axis=-1, keepdims=True) + LN_EPS)
        xh.append(xc * rstd)
        rs.append(rstd)
    return jnp.concatenate(xh, axis=1), rs


def _mix_fwd(hm, p, ys5g, h0, gn, wmo_bf, wo_bf, g1, b1, lp):
    r, d = hm.shape
    tm = _row_tile(lp, 208)

    def body(hm_ref, o_ref, gs_ref, gm_ref, ys_ref, h0_ref, gn_ref, wmo_ref, wo_ref, g1_ref, b1_ref,
             ymin_ref, ym_ref, mix_ref, r1_ref, h1_ref):
        xhat, _ = _headnorm(hm_ref[...])
        ymin = _bf(_sig(o_ref[...]) * (xhat * gn_ref[...]))
        ymin_ref[...] = ymin
        ym = _dot(ymin, wmo_ref[...])
        ym_ref[...] = ym
        mix = _bf(_sig(gs_ref[...]) * ys_ref[...] + _sig(gm_ref[...]) * ym)
        mix_ref[...] = mix
        r1 = ALPHA * h0_ref[...] + _dot(mix, wo_ref[...])
        r1_ref[...] = r1
        h1, _, _ = _ln_fwd(r1, g1_ref[...], b1_ref[...])
        h1_ref[...] = h1

    sd = jax.ShapeDtypeStruct
    row = pl.BlockSpec((tm, d), lambda i: (i, 0))
    return _pcall(
        body, name="mix_fwd", grid=(r // tm,),
        in_specs=[row, pl.BlockSpec((tm, d), lambda i: (i, O_OFF // d)), pl.BlockSpec((tm, d), lambda i: (i, GS_OFF // d)),
                  pl.BlockSpec((tm, d), lambda i: (i, GM_OFF // d)), row, row, _const((1, d)),
                  _resident((d, d)), _resident((d, d)), _const((1, d)), _const((1, d))],
        out_specs=[row] * 5,
        out_shape=[sd((r, d), BF16), sd((r, d), F32), sd((r, d), BF16), sd((r, d), F32), sd((r, d), F32)],
        compiler_params=_cp(("parallel",), 48),
    )(hm, p, p, p, ys5g, h0, gn, wmo_bf, wo_bf, g1, b1)


def _mix_bwd(dh1, r1, g1, wo_bf, wmo_bf, p, ys5g, ym, hm, gn, lp):
    r, d = hm.shape
    tm = _row_tile(lp, 208)
    dv = d // M_HEADS

    def body(dh1_ref, r1_ref, g1_ref, wo_ref, wmo_ref, o_ref, gs_ref, gm_ref, ys_ref, ym_ref, hm_ref, gn_ref,
             dr1_ref, dp_ref, dys_ref, dym_ref, dhm_ref, dg1_ref, db1_ref, dgn_ref):
        i = pl.program_id(0)

        @pl.when(i == 0)
        def _():
            dg1_ref[...] = jnp.zeros_like(dg1_ref)
            db1_ref[...] = jnp.zeros_like(db1_ref)
            dgn_ref[...] = jnp.zeros_like(dgn_ref)

        dh1 = dh1_ref[...]
        _, xhat1, rstd1 = _ln_fwd(r1_ref[...], g1_ref[...], 0.0)
        dr1 = _ln_bwd(dh1, xhat1, rstd1, g1_ref[...])
        dr1_ref[...] = dr1
        dg1_ref[...] += _colsum(dh1 * xhat1)
        db1_ref[...] += _colsum(dh1)
        dmix = _dot(_bf(dr1), wo_ref[...], NT)
        sgs, sgm, so = _sig(gs_ref[...]), _sig(gm_ref[...]), _sig(o_ref[...])
        dys_ref[...] = dmix * sgs
        dp_ref[:, d:2 * d] = dmix * ys_ref[...] * sgs * (1.0 - sgs)
        dym = dmix * sgm
        dym_ref[...] = _bf(dym)
        dp_ref[:, 2 * d:3 * d] = dmix * ym_ref[...] * sgm * (1.0 - sgm)
        dymin = _dot(_bf(dym), wmo_ref[...], NT)
        xhat, rs = _headnorm(hm_ref[...])
        gn_ = gn_ref[...]
        dp_ref[:, 0:d] = dymin * (xhat * gn_) * so * (1.0 - so)
        dhn = dymin * so
        dgn_ref[...] += _colsum(dhn * xhat)
        dxh = dhn * gn_
        for h in range(M_HEADS):
            sl = slice(h * dv, (h + 1) * dv)
            a, xh = dxh[:, sl], xhat[:, sl]
            m1 = jnp.mean(a, axis=-1, keepdims=True)
            m2 = jnp.mean(a * xh, axis=-1, keepdims=True)
            dhm_ref[:, sl] = rs[h] * (a - m1 - xh * m2)

    sd = jax.ShapeDtypeStruct
    row = pl.BlockSpec((tm, d), lambda i: (i, 0))
    vec = _const((1, d))
    return _pcall(
        body, name="mix_bwd", grid=(r // tm,),
        in_specs=[row, row, vec, _resident((d, d)), _resident((d, d)),
                  pl.BlockSpec((tm, d), lambda i: (i, O_OFF // d)), pl.BlockSpec((tm, d), lambda i: (i, GS_OFF // d)),
                  pl.BlockSpec((tm, d), lambda i: (i, GM_OFF // d)), row, row, row, vec],
        out_specs=[row, pl.BlockSpec((tm, 3 * d), lambda i: (i, 0)), row, row, row, vec, vec, vec],
        out_shape=[sd((r, d), F32), sd((r, NP), F32), sd((r, d), F32), sd((r, d), BF16), sd((r, d), F32),
                   sd((1, d), F32), sd((1, d), F32), sd((1, d), F32)],
        compiler_params=_cp(("arbitrary",), 48),
    )(dh1, r1, g1, wo_bf, wmo_bf, p, p, p, ys5g, ym, hm, gn)


def _mlp_fwd(h1, tgt, wup_g, wdn_bf, bup, g2, b2, lp):
    r, d = h1.shape
    tm = _row_tile(lp, 320)
    tps = lp // tm
    nf = wup_g.shape[0]

    def body(h1_ref, t_ref, wup_ref, wdn_ref, bup_ref, g2_ref, b2_ref, dr2_ref, act_ref, loss_ref, dg2_ref, db2_ref):
        i = pl.program_id(0)

        @pl.when(i == 0)
        def _():
            loss_ref[...] = jnp.zeros_like(loss_ref)
            dg2_ref[...] = jnp.zeros_like(dg2_ref)
            db2_ref[...] = jnp.zeros_like(db2_ref)

        h1 = h1_ref[...]
        h1b = _bf(h1)
        ff = jnp.zeros((tm, d), F32)
        for s in range(nf):
            up = _dot(h1b, wup_ref[s]) + bup_ref[:, s * d:(s + 1) * d]
            a = jnp.maximum(up, 0.0)
            a = _bf(a * a)
            act_ref[:, s * d:(s + 1) * d] = a
            ff = ff + _dot(a, wdn_ref[s * d:(s + 1) * d, :])
        r2 = ALPHA * h1 + ff
        g2 = g2_ref[...]
        y, xhat, rstd = _ln_fwd(r2, g2, b2_ref[...])
        t = (i % tps) * tm + lax.broadcasted_iota(jnp.int32, (tm, 1), 0)
        diff = jnp.where(t >= PAD + N_META, y - t_ref[...], 0.0)
        loss_ref[...] += 0.5 / d * jnp.sum(jnp.sum(diff * diff, axis=1, keepdims=True), axis=0, keepdims=True)
        dy = diff * (1.0 / d)
        dg2_ref[...] += _colsum(dy * xhat)
        db2_ref[...] += _colsum(dy)
        dr2_ref[...] = _ln_bwd(dy, xhat, rstd, g2)

    sd = jax.ShapeDtypeStruct
    row = pl.BlockSpec((tm, d), lambda i: (i, 0))
    vec = _const((1, d))
    return _pcall(
        body, name="mlp_fwd", grid=(r // tm,),
        in_specs=[row, row, _resident(wup_g.shape), _resident(wdn_bf.shape), _const((1, nf * d)), vec, vec],
        out_specs=[row, pl.BlockSpec((tm, nf * d), lambda i: (i, 0)), _const((1, 128)), vec, vec],
        out_shape=[sd((r, d), F32), sd((r, nf * d), BF16), sd((1, 128), F32), sd((1, d), F32), sd((1, d), F32)],
        compiler_params=_cp(("arbitrary",), 56),
    )(h1, tgt, wup_g, wdn_bf, bup, g2, b2)


def _mlp_bwd(h1, dr2, wup_g, wdn_bf, bup, lp):
    r, d = h1.shape
    tm = _row_tile(lp, 320)
    nf = wup_g.shape[0]

    def body(h1_ref, dr2_ref, wup_ref, wdn_ref, bup_ref, dh1_ref, dup_ref, dbup_ref):
        i = pl.program_id(0)

        @pl.when(i == 0)
        def _():
            dbup_ref[...] = jnp.zeros_like(dbup_ref)

        h1b = _bf(h1_ref[...])
        dr2 = dr2_ref[...]
        dr2b = _bf(dr2)
        acc = ALPHA * dr2
        for s in range(nf):
            up = _dot(h1b, wup_ref[s]) + bup_ref[:, s * d:(s + 1) * d]
            dact = _dot(dr2b, wdn_ref[s * d:(s + 1) * d, :], NT)
            dup = dact * (2.0 * jnp.maximum(up, 0.0))
            dbup_ref[:, s * d:(s + 1) * d] += _colsum(dup)
            dupb = _bf(dup)
            dup_ref[:, s * d:(s + 1) * d] = dupb
            acc = acc + _dot(dupb, wup_ref[s], NT)
        dh1_ref[...] = acc

    sd = jax.ShapeDtypeStruct
    row = pl.BlockSpec((tm, d), lambda i: (i, 0))
    return _pcall(
        body, name="mlp_bwd", grid=(r // tm,),
        in_specs=[row, row, _resident(wup_g.shape), _resident(wdn_bf.shape), _const((1, nf * d))],
        out_specs=[row, pl.BlockSpec((tm, nf * d), lambda i: (i, 0)), _const((1, nf * d))],
        out_shape=[sd((r, d), F32), sd((r, nf * d), BF16), sd((1, nf * d), F32)],
        compiler_params=_cp(("arbitrary",), 56),
    )(h1, dr2, wup_g, wdn_bf, bup)


def _s5_block_mats(bb_re_t, bb_im_t, c_re, c_im, ap_re, ap_im):
    ng = c_re.shape[0]
    gl = ng // S5_KCH
    eye = jnp.eye(gl, dtype=F32)

    def bmat(bt):
        bb = jnp.transpose(bt, (1, 0, 2)).reshape(S5_KCH, gl, S5_GROUP, S5_STATE)
        return jnp.einsum("kghp,gj->kghjp", bb, eye).reshape(S5_KCH, gl * S5_GROUP, gl * S5_STATE)

    def cmat(c):
        cc = c.reshape(S5_KCH, gl, S5_GROUP, S5_STATE)
        return jnp.einsum("kghp,gj->kjpgh", cc, eye).reshape(S5_KCH, gl * S5_STATE, gl * S5_GROUP)

    def pw(a):
        return jnp.transpose(a.reshape(8, S5_KCH, gl * S5_STATE), (1, 0, 2))

    bk = jnp.concatenate([bmat(bb_re_t), bmat(bb_im_t)], axis=-1)
    apow = jnp.concatenate([pw(ap_re), pw(ap_im)], axis=-1)
    return _bf(bk), _bf(cmat(c_re)), _bf(cmat(c_im)), apow


def _s5_block_grads(dbk, dcre, dcim, da):
    gl = dbk.shape[1] // S5_GROUP
    ng = gl * S5_KCH
    eye = jnp.eye(gl, dtype=F32)
    hw = gl * S5_STATE

    def bpart(x):
        x = x.reshape(S5_KCH, gl, S5_GROUP, gl, S5_STATE)
        x = jnp.einsum("kghjp,gj->kghp", x, eye).reshape(ng, S5_GROUP, S5_STATE)
        return jnp.transpose(x, (1, 0, 2))

    def cpart(x):
        x = x.reshape(S5_KCH, gl, S5_STATE, gl, S5_GROUP)
        return jnp.einsum("kjpgh,gj->kghp", x, eye).reshape(ng, S5_GROUP, S5_STATE)

    return (bpart(dbk[..., :hw]), bpart(dbk[..., hw:]), cpart(dcre), cpart(dcim),
            da[:, 0, :hw].reshape(ng, S5_STATE), da[:, 0, hw:].reshape(ng, S5_STATE))


def _local_step(x, tgt, w):
    bsz, seq, d = x.shape
    lp = PAD + N_META + seq
    r = bsz * lp
    meta = jnp.broadcast_to(w["meta_tokens"][None], (bsz, N_META, d))
    hin = jnp.concatenate([jnp.zeros((bsz, PAD, d), F32), meta, x], axis=1).reshape(r, d)
    tgtp = jnp.concatenate([jnp.zeros((bsz, PAD + N_META, d), F32), tgt], axis=1).reshape(r, d)

    h0 = _ln0_fwd(hin, w["ln0_g"], w["ln0_b"], lp)
    p = _inproj(h0, w["w_in"], w["b_in"], lp)
    p3 = p.reshape(bsz, lp, NP)

    b_re_t = jnp.transpose(w["s5_b_re"], (2, 0, 1))
    b_im_t = jnp.transpose(w["s5_b_im"], (2, 0, 1))
    ap_re, ap_im, bb_re_t, bb_im_t = _s5_prep(w["s5_lambda_re"], w["s5_lambda_im"], w["s5_log_dt"], b_re_t, b_im_t)
    bk, cre, cim, apow = _s5_block_mats(bb_re_t, bb_im_t, w["s5_c_re"], w["s5_c_im"], ap_re, ap_im)
    y_s5, xs = _s5_fwd(p3, bk, cre, cim, apow, w["s5_d"])
    sw = y_s5.shape[-1]
    gy, z, ys5g = _glu_fwd(y_s5.reshape(r, sw), w["s5_w_glu"], lp)

    pre3, qk3 = _conv_fwd(p3, w["qk_conv_w"], w["qk_conv_b"])
    hm3, cs, ns, ms = _mlstm_fwd(qk3, p3)
    hm = hm3.reshape(r, d)
    ymin, ym, mix, r1, h1 = _mix_fwd(hm, p, ys5g, h0, w["m_norm_g"], w["m_w_out"], w["w_o"], w["ln1_g"], w["ln1_b"], lp)
    dr2, act, loss, dg2, db2 = _mlp_fwd(h1, tgtp, w["w_up"], w["w_down"], w["b_up"], w["ln2_g"], w["ln2_b"], lp)

    g = {"ln2_g": dg2, "ln2_b": db2}
    dh1, dup, g["b_up"] = _mlp_bwd(h1, dr2, w["w_up"], w["w_down"], w["b_up"], lp)
    g["w_down"] = _mm_tn(act, dr2, name="dw_down")
    g["w_up"] = _mm_tn(h1, dup, name="dw_up", split=w["w_up"].shape[0])
    dr1, dp, dys5g, dym, dhm, g["ln1_g"], g["ln1_b"], g["m_norm_g"] = _mix_bwd(
        dh1, r1, w["ln1_g"], w["w_o"], w["m_w_out"], p, ys5g, ym, hm, w["m_norm_g"], lp)
    g["w_o"] = _mm_tn(mix, dr1, name="dw_o")
    g["m_w_out"] = _mm_tn(ymin, dym, name="dw_mout")

    dp3 = dp.reshape(bsz, lp, NP)
    dp3, dqk3, dgate = _mlstm_bwd(dp3, qk3, p3, dhm.reshape(bsz, lp, d), cs, ns, ms)
    dp3, g["qk_conv_w"], g["qk_conv_b"] = _conv_bwd(dp3, p3, dqk3, pre3, w["qk_conv_w"])
    dz, dys5 = _glu_bwd(dys5g, z, y_s5.reshape(r, sw), w["s5_w_glu"], lp)
    g["s5_w_glu"] = _mm_tn(gy, dz, name="dw_glu", split=w["s5_w_glu"].shape[0])
    apow_rev = jnp.flip(apow, axis=1)
    dp3, dbk, dcre, dcim, da, g["s5_d"] = _s5_bwd(dp3, p3, dys5.reshape(bsz, lp, sw), xs, bk, cre, cim, apow_rev, w["s5_d"])
    dbb_re_t, dbb_im_t, g["s5_c_re"], g["s5_c_im"], da_re, da_im = _s5_block_grads(dbk, dcre, dcim, da)
    g["s5_lambda_re"], g["s5_lambda_im"], g["s5_log_dt"], gb_re_t, gb_im_t = _s5_prep_bwd(
        w["s5_lambda_re"], w["s5_lambda_im"], w["s5_log_dt"], b_re_t, b_im_t, da_re, da_im, dbb_re_t, dbb_im_t)
    g["s5_b_re"] = jnp.transpose(gb_re_t, (1, 2, 0))
    g["s5_b_im"] = jnp.transpose(gb_im_t, (1, 2, 0))

    dp3 = lax.dynamic_update_slice(dp3, dgate, (0, 0, G_OFF))
    dp = dp3.reshape(r, NP)
    g["w_in"], g["b_in"] = _mm_tn(h0, dp, name="dw_in", colsum=True)
    dpw = _mm_nt(dp, w["w_in"], lp, name="dh0")
    dhin, g["ln0_g"], g["ln0_b"], g["meta_tokens"] = _ln0_bwd(hin, dr1, dpw, w["ln0_g"], lp)
    grad_x = dhin.reshape(bsz, lp, d)[:, PAD + N_META:]
    return loss, grad_x, g


_ANY = pl.BlockSpec(memory_space=pl.ANY)
_MESH = pl.DeviceIdType.MESH


def _place():
    return lax.axis_index("x"), lax.axis_index("y"), lax.axis_index("c")


def _gather_chips(shards):
    n = len(shards)

    def body(*refs):
        ins, outs = refs[:n], refs[n:2 * n]
        send, recv, loc = refs[2 * n:]
        x, y, c = _place()
        me = 2 * x + y
        peers = [(1 - x, y), (x, 1 - y), (1 - x, 1 - y)]

        def rc(a, k, slot):
            px, py = peers[k]
            return pltpu.make_async_remote_copy(src_ref=ins[a], dst_ref=outs[a].at[slot], send_sem=send.at[a, k],
                                                recv_sem=recv.at[a, k], device_id=(px, py, c), device_id_type=_MESH)

        own = [pltpu.make_async_copy(ins[a], outs[a].at[me], loc.at[a]) for a in range(n)]
        for cp in own:
            cp.start()
        out = [rc(a, k, me) for a in range(n) for k in range(3)]
        for cp in out:
            cp.start()
        for a in range(n):
            for k in range(3):
                rc(a, k, 2 * peers[k][0] + peers[k][1]).wait_recv()
        for cp in out:
            cp.wait_send()
        for cp in own:
            cp.wait()

    return _pcall(
        body, name="gather_chips", in_specs=[_ANY] * n, out_specs=[_ANY] * n,
        out_shape=[jax.ShapeDtypeStruct((4,) + s.shape, s.dtype) for s in shards],
        scratch_shapes=[pltpu.SemaphoreType.DMA((n, 3)), pltpu.SemaphoreType.DMA((n, 3)), pltpu.SemaphoreType.DMA((n,))],
    )(*shards)


def _scatter_chips(parts):
    n = len(parts)

    def body(*refs):
        ins, outs = refs[:n], refs[n:2 * n]
        send, recv, loc = refs[2 * n:]
        x, y, c = _place()
        me = 2 * x + y
        peers = [(1 - x, y), (x, 1 - y), (1 - x, 1 - y)]
        pchip = [2 * px + py for px, py in peers]

        def rc(a, k, src_slot, dst_slot):
            px, py = peers[k]
            return pltpu.make_async_remote_copy(src_ref=ins[a].at[src_slot], dst_ref=outs[a].at[dst_slot],
                                                send_sem=send.at[a, k], recv_sem=recv.at[a, k],
                                                device_id=(px, py, c), device_id_type=_MESH)

        own = [pltpu.make_async_copy(ins[a].at[me], outs[a].at[me], loc.at[a]) for a in range(n)]
        for cp in own:
            cp.start()
        out = [rc(a, k, pchip[k], me) for a in range(n) for k in range(3)]
        for cp in out:
            cp.start()
        for a in range(n):
            for k in range(3):
                rc(a, k, me, pchip[k]).wait_recv()
        for cp in out:
            cp.wait_send()
        for cp in own:
            cp.wait()

    return _pcall(
        body, name="scatter_chips", in_specs=[_ANY] * n, out_specs=[_ANY] * n,
        out_shape=[jax.ShapeDtypeStruct(s.shape, s.dtype) for s in parts],
        scratch_shapes=[pltpu.SemaphoreType.DMA((n, 3)), pltpu.SemaphoreType.DMA((n, 3)), pltpu.SemaphoreType.DMA((n,))],
    )(*parts)


def _swap_cores(arrs):
    n = len(arrs)

    def body(*refs):
        ins, outs = refs[:n], refs[n:2 * n]
        send, recv = refs[2 * n:]
        x, y, c = _place()
        cps = [pltpu.make_async_remote_copy(src_ref=ins[a], dst_ref=outs[a], send_sem=send.at[a], recv_sem=recv.at[a],
                                            device_id=(x, y, 1 - c), device_id_type=_MESH) for a in range(n)]
        for cp in cps:
            cp.start()
        for cp in cps:
            cp.wait_recv()
        for cp in cps:
            cp.wait_send()

    return _pcall(
        body, name="swap_cores", in_specs=[_ANY] * n, out_specs=[_ANY] * n,
        out_shape=[jax.ShapeDtypeStruct(s.shape, s.dtype) for s in arrs],
        scratch_shapes=[pltpu.SemaphoreType.DMA((n,)), pltpu.SemaphoreType.DMA((n,))],
    )(*arrs)


def _allreduce_small(v):
    rows = v.shape[0]

    def body(v_ref, sum_ref, all_ref, send, recv):
        x, y, c = _place()
        me = 4 * x + 2 * y + c
        flips = [(k >> 2 & 1, k >> 1 & 1, k & 1) for k in range(1, 8)]

        def peer(f):
            return tuple(1 - q if b else q for q, b in zip((x, y, c), f))

        all_ref[me] = v_ref[...]
        cps = [pltpu.make_async_remote_copy(src_ref=v_ref, dst_ref=all_ref.at[me], send_sem=send.at[k], recv_sem=recv.at[k],
                                            device_id=peer(f), device_id_type=_MESH) for k, f in enumerate(flips)]
        for cp in cps:
            cp.start()
        for cp in cps:
            cp.wait_recv()
        acc = all_ref[0]
        for j in range(1, 8):
            acc = acc + all_ref[j]
        sum_ref[...] = acc
        for cp in cps:
            cp.wait_send()

    vm = pl.BlockSpec(memory_space=pltpu.VMEM)
    return _pcall(
        body, name="allreduce_small", in_specs=[vm], out_specs=vm,
        out_shape=jax.ShapeDtypeStruct((rows, 128), F32),
        scratch_shapes=[pltpu.VMEM((8, rows, 128), F32), pltpu.SemaphoreType.DMA((7,)), pltpu.SemaphoreType.DMA((7,))],
        compiler_params=_cp(None, 40),
    )(v)


def _sum_slots(a):
    _, rows, cols = a.shape
    tm = _row_tile(rows, 256, 8)

    def body(a_ref, o_ref):
        o_ref[...] = ((a_ref[0] + a_ref[1]) + a_ref[2]) + a_ref[3]

    return _pcall(
        body, name="sum_slots", grid=(rows // tm,),
        in_specs=[pl.BlockSpec((4, tm, cols), lambda i: (0, i, 0))],
        out_specs=pl.BlockSpec((tm, cols), lambda i: (i, 0)),
        out_shape=jax.ShapeDtypeStruct((rows, cols), F32),
        compiler_params=_cp(("parallel",), 40),
    )(a)


def _adamw(w, m, v, g0, g1=None):
    rows, cols = w.shape[-2:]
    lead = w.ndim == 3
    tm = _row_tile(rows, 256, 8)
    c1 = 1.0 - ADAM_B1 ** ADAM_STEP
    c2 = 1.0 - ADAM_B2 ** ADAM_STEP
    two = g1 is not None

    def body(*refs):
        w_ref, m_ref, v_ref, g0_ref = refs[:4]
        g_ref, d_ref, nm_ref, nv_ref = refs[-4:]
        g = g0_ref[...]
        if two:
            g = g + refs[4][...]
        nm = ADAM_B1 * m_ref[...] + (1.0 - ADAM_B1) * g
        nv = ADAM_B2 * v_ref[...] + (1.0 - ADAM_B2) * (g * g)
        g_ref[...] = g
        nm_ref[...] = nm
        nv_ref[...] = nv
        d_ref[...] = -ADAM_LR * ((nm / c1) / (jnp.sqrt(nv / c2) + ADAM_EPS) + ADAM_WD * w_ref[...])

    blk = pl.BlockSpec((tm, cols), lambda i: (i, 0))
    wblk = pl.BlockSpec((None, tm, cols), lambda i: (0, i, 0)) if lead else blk
    ins = [w, m, v, g0] + ([g1] if two else [])
    return _pcall(
        body, name="adamw", grid=(rows // tm,), in_specs=[wblk] * 3 + [blk] * (len(ins) - 3), out_specs=[wblk] * 4,
        out_shape=[jax.ShapeDtypeStruct(w.shape, F32)] * 4,
        compiler_params=_cp(("parallel",), 40),
    )(*ins)


_BIG = ("w_in", "s5_w_glu", "m_w_out", "w_o", "w_up", "w_down")
_SMALL = ("ln0_g", "ln0_b", "b_in", "qk_conv_b", "s5_lambda_re", "s5_lambda_im", "s5_log_dt", "s5_b_re", "s5_b_im",
          "s5_c_re", "s5_c_im", "s5_d", "m_norm_g", "ln1_g", "ln1_b", "b_up", "ln2_g", "ln2_b")
_SMALL_SHARDED = ("meta_tokens", "qk_conv_w")
_ORDER = ("meta_tokens", "ln0_g", "ln0_b", "w_in", "b_in", "qk_conv_w", "qk_conv_b", "s5_lambda_re", "s5_lambda_im",
          "s5_log_dt", "s5_b_re", "s5_b_im", "s5_c_re", "s5_c_im", "s5_d", "s5_w_glu", "m_norm_g", "m_w_out", "w_o",
          "ln1_g", "ln1_b", "w_up", "b_up", "w_down", "ln2_g", "ln2_b")


def _pack(arrs):
    flat = jnp.concatenate([a.reshape(-1) for a in arrs])
    n = flat.shape[0]
    rows = -(-n // 1024) * 8
    return jnp.pad(flat, (0, rows * 128 - n)).reshape(rows, 128)


def _unpack(packed, shapes):
    flat = packed.reshape(-1)
    out, off = [], 0
    for s in shapes:
        n = math.prod(s)
        out.append(flat[off:off + n].reshape(s))
        off += n
    return out


def kernel(x, meta_tokens, ln0_g, ln0_b, w_in, b_in, qk_conv_w, qk_conv_b, s5_lambda_re, s5_lambda_im, s5_log_dt, s5_b_re, s5_b_im, s5_c_re, s5_c_im, s5_d, s5_w_glu, m_norm_g, m_w_out, w_o, ln1_g, ln1_b, w_up, b_up, w_down, ln2_g, ln2_b, loss_target, m_meta_tokens, m_ln0_g, m_ln0_b, m_w_in, m_b_in, m_qk_conv_w, m_qk_conv_b, m_s5_lambda_re, m_s5_lambda_im, m_s5_log_dt, m_s5_b_re, m_s5_b_im, m_s5_c_re, m_s5_c_im, m_s5_d, m_s5_w_glu, m_m_norm_g, m_m_w_out, m_w_o, m_ln1_g, m_ln1_b, m_w_up, m_b_up, m_w_down, m_ln2_g, m_ln2_b, v_meta_tokens, v_ln0_g, v_ln0_b, v_w_in, v_b_in, v_qk_conv_w, v_qk_conv_b, v_s5_lambda_re, v_s5_lambda_im, v_s5_log_dt, v_s5_b_re, v_s5_b_im, v_s5_c_re, v_s5_c_im, v_s5_d, v_s5_w_glu, v_m_norm_g, v_m_w_out, v_w_o, v_ln1_g, v_ln1_b, v_w_up, v_b_up, v_w_down, v_ln2_g, v_ln2_b):
    wts = dict(meta_tokens=meta_tokens, ln0_g=ln0_g, ln0_b=ln0_b, w_in=w_in, b_in=b_in, qk_conv_w=qk_conv_w,
               qk_conv_b=qk_conv_b, s5_lambda_re=s5_lambda_re, s5_lambda_im=s5_lambda_im, s5_log_dt=s5_log_dt,
               s5_b_re=s5_b_re, s5_b_im=s5_b_im, s5_c_re=s5_c_re, s5_c_im=s5_c_im, s5_d=s5_d, s5_w_glu=s5_w_glu,
               m_norm_g=m_norm_g, m_w_out=m_w_out, w_o=w_o, ln1_g=ln1_g, ln1_b=ln1_b, w_up=w_up, b_up=b_up,
               w_down=w_down, ln2_g=ln2_g, ln2_b=ln2_b)
    mom = dict(meta_tokens=m_meta_tokens, ln0_g=m_ln0_g, ln0_b=m_ln0_b, w_in=m_w_in, b_in=m_b_in, qk_conv_w=m_qk_conv_w,
               qk_conv_b=m_qk_conv_b, s5_lambda_re=m_s5_lambda_re, s5_lambda_im=m_s5_lambda_im, s5_log_dt=m_s5_log_dt,
               s5_b_re=m_s5_b_re, s5_b_im=m_s5_b_im, s5_c_re=m_s5_c_re, s5_c_im=m_s5_c_im, s5_d=m_s5_d,
               s5_w_glu=m_s5_w_glu, m_norm_g=m_m_norm_g, m_w_out=m_m_w_out, w_o=m_w_o, ln1_g=m_ln1_g, ln1_b=m_ln1_b,
               w_up=m_w_up, b_up=m_b_up, w_down=m_w_down, ln2_g=m_ln2_g, ln2_b=m_ln2_b)
    var = dict(meta_tokens=v_meta_tokens, ln0_g=v_ln0_g, ln0_b=v_ln0_b, w_in=v_w_in, b_in=v_b_in, qk_conv_w=v_qk_conv_w,
               qk_conv_b=v_qk_conv_b, s5_lambda_re=v_s5_lambda_re, s5_lambda_im=v_s5_lambda_im, s5_log_dt=v_s5_log_dt,
               s5_b_re=v_s5_b_re, s5_b_im=v_s5_b_im, s5_c_re=v_s5_c_re, s5_c_im=v_s5_c_im, s5_d=v_s5_d,
               s5_w_glu=v_s5_w_glu, m_norm_g=v_m_norm_g, m_w_out=v_m_w_out, w_o=v_w_o, ln1_g=v_ln1_g, ln1_b=v_ln1_b,
               w_up=v_w_up, b_up=v_b_up, w_down=v_w_down, ln2_g=v_ln2_g, ln2_b=v_ln2_b)
    d = x.shape[-1]
    chip = 2 * lax.axis_index("x") + lax.axis_index("y")

    gathered = _gather_chips([_bf(wts[n][0]) for n in _BIG] + [meta_tokens, qk_conv_w[0]])
    gw = dict(zip(_BIG + _SMALL_SHARDED, gathered))
    cat = lambda a: jnp.transpose(a, (1, 0, 2)).reshape(a.shape[1], 4 * a.shape[2])
    w = dict(
        meta_tokens=cat(gw["meta_tokens"]), ln0_g=ln0_g[None], ln0_b=ln0_b[None],
        w_in=_w_in_from_slots(gw["w_in"]), b_in=_to_pad_cols(b_in),
        qk_conv_w=cat(gw["qk_conv_w"]), qk_conv_b=qk_conv_b,
        s5_lambda_re=s5_lambda_re[0], s5_lambda_im=s5_lambda_im[0], s5_log_dt=s5_log_dt[0][:, None],
        s5_b_re=s5_b_re[0], s5_b_im=s5_b_im[0], s5_c_re=s5_c_re[0], s5_c_im=s5_c_im[0], s5_d=s5_d,
        s5_w_glu=gw["s5_w_glu"], m_norm_g=m_norm_g, m_w_out=gw["m_w_out"].reshape(d, d), w_o=gw["w_o"].reshape(d, d),
        ln1_g=ln1_g, ln1_b=ln1_b, w_up=gw["w_up"], b_up=b_up, w_down=gw["w_down"].reshape(4 * d, d),
        ln2_g=ln2_g, ln2_b=ln2_b)

    loss, grad_x, g = _local_step(x, loss_target, w)
    g["b_in"] = _from_pad_cols(g["b_in"])

    parts = dict(
        w_in=_slots_from_w_in(g["w_in"][0]), s5_w_glu=g["s5_w_glu"],
        m_w_out=g["m_w_out"].reshape(4, d // 4, d), w_o=g["w_o"].reshape(4, d // 4, d), w_up=g["w_up"],
        w_down=g["w_down"].reshape(4, d, d))
    got = _scatter_chips([parts[n] for n in _BIG])
    mine = [_sum_slots(a) for a in got]
    theirs = _swap_cores(mine)

    small_shapes = [(1, 128)] + [wts[n].shape for n in _SMALL] + [g[n].shape for n in _SMALL_SHARDED]
    packed = _pack([loss] + [g[n] for n in _SMALL] + [g[n] for n in _SMALL_SHARDED])
    tot = _unpack(_allreduce_small(packed), small_shapes)
    loss_out = tot[0][0, 0]
    gsm = dict(zip(_SMALL + _SMALL_SHARDED, tot[1:]))
    for n in _SMALL_SHARDED:
        cols = wts[n].shape[-1]
        gsm[n] = lax.dynamic_slice_in_dim(gsm[n], chip * cols, cols, axis=1).reshape(wts[n].shape)

    res = {}
    for i, n in enumerate(_BIG):
        res[n] = _adamw(wts[n], mom[n], var[n], mine[i], theirs[i])
    names = _SMALL + _SMALL_SHARDED
    shapes = [wts[n].shape for n in names]
    pk = lambda dct: _pack([dct[n] for n in names])
    small_res = [_unpack(r, shapes) for r in _adamw(pk(wts), pk(mom), pk(var), pk(gsm))]
    for j, n in enumerate(names):
        res[n] = [small_res[q][j] for q in range(4)]

    return (loss_out, grad_x, *[res[n][0] for n in _ORDER], *[res[n][1] for n in _ORDER],
            *[res[n][2] for n in _ORDER], *[res[n][3] for n in _ORDER])
```

```python
import functools
import math

import jax
import jax.numpy as jnp
from jax import lax
from jax.experimental import pallas as pl
from jax.experimental.pallas import tpu as pltpu

F32 = jnp.float32
BF16 = jnp.bfloat16
HI = lax.Precision.HIGHEST

N_META = 16
M_HEADS = 4
M_CHUNK = 64
PAD = M_CHUNK - N_META
CONV_W = 4
S5_GROUP = 16
S5_STATE = 64
S5_KCH = 4
LN_EPS = 1e-5
ALPHA = 2.0 ** 0.25
NEG = -1e30
ADAM_LR, ADAM_B1, ADAM_B2, ADAM_EPS, ADAM_WD, ADAM_STEP = 0.001, 0.9, 0.999, 1e-08, 0.01, 10

O_OFF, GS_OFF, GM_OFF, V_OFF, Q_OFF, K_OFF, U_OFF, G_OFF, NP = 0, 1024, 2048, 3072, 4096, 4608, 5120, 5632, 5760

NN = ((1,), (0,))
NT = ((1,), (1,))
TN = ((0,), (0,))


def _dot(a, b, dims=NN, prec=None):
    return lax.dot_general(a, b, (dims, ((), ())), preferred_element_type=F32, precision=prec)


def _bf(x):
    return x.astype(BF16)


def _sig(x):
    return 1.0 / (1.0 + jnp.exp(-x))


def _pcall(body, **kw):
    return pl.pallas_call(body, **kw)


def _cp(sem=None, vmem_mb=None):
    kw = {}
    if sem is not None:
        kw["dimension_semantics"] = sem
    if vmem_mb is not None:
        kw["vmem_limit_bytes"] = vmem_mb << 20
    return pltpu.CompilerParams(**kw)


def _row_tile(n, want, mult=16):
    best = None
    for t in range(mult, want + 1, mult):
        if n % t == 0:
            best = t
    assert best is not None, (n, want)
    return best


def _resident(shape):
    nd = len(shape)
    return pl.BlockSpec(shape, lambda *_: (0,) * nd, pipeline_mode=pl.Buffered(1))


def _const(shape):
    nd = len(shape)
    return pl.BlockSpec(shape, lambda *_: (0,) * nd)


def _ln_fwd(x, g, b):
    mu = jnp.mean(x, axis=-1, keepdims=True)
    xc = x - mu
    var = jnp.mean(xc * xc, axis=-1, keepdims=True)
    rstd = lax.rsqrt(var + LN_EPS)
    xhat = xc * rstd
    return xhat * g + b, xhat, rstd


def _ln_bwd(dy, xhat, rstd, g):
    dxh = dy * g
    m1 = jnp.mean(dxh, axis=-1, keepdims=True)
    m2 = jnp.mean(dxh * xhat, axis=-1, keepdims=True)
    return rstd * (dxh - m1 - xhat * m2)


def _colsum(x):
    return jnp.sum(x, axis=0, keepdims=True)


def _to_pad_cols(w):
    u, q, k, v, o, gi, gf, gs, gm = (w[..., 0:512], w[..., 512:1024], w[..., 1024:1536], w[..., 1536:2560],
                                     w[..., 2560:3584], w[..., 3584:3588], w[..., 3588:3592], w[..., 3592:4616],
                                     w[..., 4616:5640])
    z = jnp.zeros(w.shape[:-1] + (NP - G_OFF - 8,), w.dtype)
    return jnp.concatenate([o, gs, gm, v, q, k, u, gi, gf, z], axis=-1)


def _from_pad_cols(w):
    o, gs, gm, v, q, k, u = (w[..., O_OFF:GS_OFF], w[..., GS_OFF:GM_OFF], w[..., GM_OFF:V_OFF], w[..., V_OFF:Q_OFF],
                             w[..., Q_OFF:K_OFF], w[..., K_OFF:U_OFF], w[..., U_OFF:G_OFF])
    gi, gf = w[..., G_OFF:G_OFF + 4], w[..., G_OFF + 4:G_OFF + 8]
    return jnp.concatenate([u, q, k, v, o, gi, gf, gs, gm], axis=-1)


_IN_REF = (("u", 512), ("q", 512), ("k", 512), ("v", 1024), ("o", 1024), ("i", 4), ("f", 4), ("gs", 1024), ("gm", 1024))
_IN_PAD = (("o", O_OFF), ("gs", GS_OFF), ("gm", GM_OFF), ("v", V_OFF), ("q", Q_OFF), ("k", K_OFF), ("u", U_OFF),
           ("i", G_OFF), ("f", G_OFF + 4))


def _in_ref_ranges():
    out, off = {}, 0
    for n, s in _IN_REF:
        out[n] = (off, off + s)
        off += s
    return out, off


def _w_in_from_slots(g):
    rng, total = _in_ref_ranges()
    width = total // g.shape[0]
    cols = []
    for n, _ in _IN_PAD:
        a, b = rng[n]
        while a < b:
            s = a // width
            e = min(b, (s + 1) * width)
            cols.append(g[s][:, a - s * width:e - s * width])
            a = e
    cols.append(jnp.zeros((g.shape[1], NP - G_OFF - 8), g.dtype))
    return jnp.concatenate(cols, axis=1)


def _slots_from_w_in(wp, nslot=4):
    rng, total = _in_ref_ranges()
    width = total // nslot
    pad_off = dict(_IN_PAD)
    slots = []
    for s in range(nslot):
        lo, hi = s * width, (s + 1) * width
        cols = []
        for n, _ in _IN_REF:
            a, b = rng[n]
            x0, x1 = max(a, lo), min(b, hi)
            if x0 < x1:
                cols.append(wp[:, pad_off[n] + x0 - a:pad_off[n] + x1 - a])
        slots.append(jnp.concatenate(cols, axis=1))
    return jnp.stack(slots, axis=0)


def _ln0_fwd(hin, g, b, lp):
    r, d = hin.shape
    tm = _row_tile(lp, 416)

    def body(x_ref, g_ref, b_ref, o_ref):
        y, _, _ = _ln_fwd(x_ref[...], g_ref[...], b_ref[...])
        o_ref[...] = y

    return _pcall(
        body, name="ln0_fwd", grid=(r // tm,),
        in_specs=[pl.BlockSpec((tm, d), lambda i: (i, 0)), _const((1, d)), _const((1, d))],
        out_specs=pl.BlockSpec((tm, d), lambda i: (i, 0)),
        out_shape=jax.ShapeDtypeStruct((r, d), F32),
        compiler_params=_cp(("parallel",)),
    )(hin, g, b)


def _ln0_bwd(hin, dr1, dpw, g, lp):
    r, d = hin.shape
    tm = _row_tile(lp, 416)
    tps = lp // tm
    assert tm >= PAD + N_META

    def body(x_ref, a_ref, c_ref, g_ref, o_ref, dg_ref, db_ref, dm_ref):
        i = pl.program_id(0)

        @pl.when(i == 0)
        def _():
            dg_ref[...] = jnp.zeros_like(dg_ref)
            db_ref[...] = jnp.zeros_like(db_ref)
            dm_ref[...] = jnp.zeros_like(dm_ref)

        dy = ALPHA * a_ref[...] + c_ref[...]
        _, xhat, rstd = _ln_fwd(x_ref[...], g_ref[...], 0.0)
        dx = _ln_bwd(dy, xhat, rstd, g_ref[...])
        o_ref[...] = dx
        dg_ref[...] += _colsum(dy * xhat)
        db_ref[...] += _colsum(dy)

        @pl.when(i % tps == 0)
        def _():
            dm_ref[...] += dx[PAD:PAD + N_META, :]

    return _pcall(
        body, name="ln0_bwd", grid=(r // tm,),
        in_specs=[pl.BlockSpec((tm, d), lambda i: (i, 0))] * 3 + [_const((1, d))],
        out_specs=[pl.BlockSpec((tm, d), lambda i: (i, 0)), _const((1, d)), _const((1, d)), _const((N_META, d))],
        out_shape=[jax.ShapeDtypeStruct((r, d), F32), jax.ShapeDtypeStruct((1, d), F32),
                   jax.ShapeDtypeStruct((1, d), F32), jax.ShapeDtypeStruct((N_META, d), F32)],
        compiler_params=_cp(("arbitrary",)),
    )(hin, dr1, dpw, g)


def _inproj(h0, w_bf, bias, lp):
    r, d = h0.shape
    n = w_bf.shape[1]
    tm = _row_tile(lp, 832)
    tn = 1152
    tps = lp // tm

    def body(a_ref, w_ref, b_ref, o_ref):
        i = pl.program_id(0)
        acc = _dot(_bf(a_ref[...]), w_ref[...]) + b_ref[...]
        t = (i % tps) * tm + lax.broadcasted_iota(jnp.int32, (tm, 1), 0)
        o_ref[...] = jnp.where(t >= PAD, acc, 0.0)

    return _pcall(
        body, name="inproj", grid=(r // tm, n // tn),
        in_specs=[pl.BlockSpec((tm, d), lambda i, j: (i, 0)), pl.BlockSpec((d, tn), lambda i, j: (0, j)),
                  pl.BlockSpec((1, tn), lambda i, j: (0, j))],
        out_specs=pl.BlockSpec((tm, tn), lambda i, j: (i, j)),
        out_shape=jax.ShapeDtypeStruct((r, n), F32),
        compiler_params=_cp(("parallel", "parallel"), 48),
    )(h0, w_bf, bias)


def _mm_tn(a, b, *, name, split=1, colsum=False, tk_want=832):
    r, m = a.shape
    n = b.shape[1]
    tk = _row_tile(r, tk_want)
    tm = min(m, 1024)
    ns = n // split
    tn = ns
    for cand in (1024, 1152, 640, 512, 128):
        if ns % cand == 0 and cand <= ns:
            tn = cand
            break
    nb = ns // tn
    nk = r // tk

    def body(a_ref, b_ref, o_ref, *rest):
        acc = rest[-1]
        k = pl.program_id(2)

        @pl.when(k == 0)
        def _():
            acc[...] = jnp.zeros_like(acc)

        bt = b_ref[...]
        acc[...] += _dot(_bf(a_ref[...]), _bf(bt), TN)

        @pl.when(k == nk - 1)
        def _():
            o_ref[...] = acc[...]

        if colsum:
            cs_ref = rest[0]

            @pl.when(k == 0)
            def _():
                cs_ref[...] = jnp.zeros_like(cs_ref)

            cs_ref[...] += _colsum(bt.astype(F32))

    out_specs = [pl.BlockSpec((None, tm, tn), lambda i, j, k: (j // nb, i, j % nb))]
    out_shape = [jax.ShapeDtypeStruct((split, m, ns), F32)]
    if colsum:
        assert m == tm
        out_specs.append(pl.BlockSpec((1, tn), lambda i, j, k: (0, j)))
        out_shape.append(jax.ShapeDtypeStruct((1, n), F32))
    res = _pcall(
        body, name=name, grid=(m // tm, n // tn, nk),
        in_specs=[pl.BlockSpec((tk, tm), lambda i, j, k: (k, i)), pl.BlockSpec((tk, tn), lambda i, j, k: (k, j))],
        out_specs=out_specs, out_shape=out_shape,
        scratch_shapes=[pltpu.VMEM((tm, tn), F32)],
        compiler_params=_cp(("parallel", "parallel", "arbitrary"), 48),
    )(a, b)
    return res if colsum else res[0]


def _mm_nt(a, w_bf, lp, *, name):
    r, kdim = a.shape
    n = w_bf.shape[0]
    tm = _row_tile(lp, 832)
    tk = 1152
    nk = kdim // tk

    def body(a_ref, w_ref, o_ref, acc):
        k = pl.program_id(1)

        @pl.when(k == 0)
        def _():
            acc[...] = jnp.zeros_like(acc)

        acc[...] += _dot(_bf(a_ref[...]), w_ref[...], NT)

        @pl.when(k == nk - 1)
        def _():
            o_ref[...] = acc[...]

    return _pcall(
        body, name=name, grid=(r // tm, nk),
        in_specs=[pl.BlockSpec((tm, tk), lambda i, k: (i, k)), pl.BlockSpec((n, tk), lambda i, k: (0, k))],
        out_specs=pl.BlockSpec((tm, n), lambda i, k: (i, 0)),
        out_shape=jax.ShapeDtypeStruct((r, n), F32),
        scratch_shapes=[pltpu.VMEM((tm, n), F32)],
        compiler_params=_cp(("parallel", "arbitrary"), 48),
    )(a, w_bf)


def _s5_prep(lam_re, lam_im, log_dt, b_re_t, b_im_t):
    g, p = lam_re.shape
    h = b_re_t.shape[0]

    def body(lr_ref, li_ref, ldt_ref, br_ref, bi_ref, pr_ref, pi_ref, bbr_ref, bbi_ref):
        lr, li = lr_ref[...], li_ref[...]
        dt = jnp.exp(ldt_ref[...])
        e = jnp.exp(lr * dt)
        ar, ai = e * jnp.cos(li * dt), e * jnp.sin(li * dt)
        den = lr * lr + li * li
        cr = ((ar - 1.0) * lr + ai * li) / den
        ci = (ai * lr - (ar - 1.0) * li) / den
        br, bi = br_ref[...], bi_ref[...]
        bbr_ref[...] = cr[None] * br - ci[None] * bi
        bbi_ref[...] = cr[None] * bi + ci[None] * br
        xr, xi = ar, ai
        pr_ref[0] = xr
        pi_ref[0] = xi
        for t in range(1, 8):
            xr, xi = xr * ar - xi * ai, xr * ai + xi * ar
            pr_ref[t] = xr
            pi_ref[t] = xi

    sd = jax.ShapeDtypeStruct
    return _pcall(body, name="s5_prep",
                  out_shape=[sd((8, g, p), F32), sd((8, g, p), F32), sd((h, g, p), F32), sd((h, g, p), F32)])(
        lam_re, lam_im, log_dt, b_re_t, b_im_t)


def _s5_prep_bwd(lam_re, lam_im, log_dt, b_re_t, b_im_t, da_re, da_im, dbb_re_t, dbb_im_t):
    g, p = lam_re.shape
    h = b_re_t.shape[0]

    def body(lr_ref, li_ref, ldt_ref, br_ref, bi_ref, dar_ref, dai_ref, dbr_ref, dbi_ref,
             glr_ref, gli_ref, gdt_ref, gbr_ref, gbi_ref):
        lr, li = lr_ref[...], li_ref[...]
        dt = jnp.exp(ldt_ref[...])
        e = jnp.exp(lr * dt)
        ar, ai = e * jnp.cos(li * dt), e * jnp.sin(li * dt)
        den = lr * lr + li * li
        cr = ((ar - 1.0) * lr + ai * li) / den
        ci = (ai * lr - (ar - 1.0) * li) / den
        br, bi = br_ref[...], bi_ref[...]
        gr, gi = dbr_ref[...], dbi_ref[...]
        gbr_ref[...] = gr * cr[None] + gi * ci[None]
        gbi_ref[...] = gi * cr[None] - gr * ci[None]
        gcr = jnp.sum(gr * br + gi * bi, axis=0)
        gci = jnp.sum(gi * br - gr * bi, axis=0)
        ilr, ili = lr / den, -li / den
        gar = dar_ref[...] + gcr * ilr + gci * ili
        gai = dai_ref[...] + gci * ilr - gcr * ili
        qr, qi = cr * ilr - ci * ili, cr * ili + ci * ilr
        glr = -(gcr * qr + gci * qi)
        gli = -(gci * qr - gcr * qi)
        gzr = gar * ar + gai * ai
        gzi = gai * ar - gar * ai
        glr_ref[...] = glr + gzr * dt
        gli_ref[...] = gli + gzi * dt
        gdt_ref[...] = jnp.sum(gzr * lr + gzi * li, axis=1, keepdims=True) * dt

    sd = jax.ShapeDtypeStruct
    return _pcall(body, name="s5_prep_bwd",
                  out_shape=[sd((g, p), F32), sd((g, p), F32), sd((g, 1), F32), sd((h, g, p), F32), sd((h, g, p), F32)])(
        lam_re, lam_im, log_dt, b_re_t, b_im_t, da_re, da_im, dbb_re_t, dbb_im_t)


def _cmul(xr, xi, yr, yi):
    return xr * yr - xi * yi, xr * yi + xi * yr


def _dot5(a, b, dims=NN):
    return _dot(_bf(a), _bf(b), dims)


def _s5_fwd(p3, bk, cre, cim, apow, dskip):
    bsz, lp, _ = p3.shape
    tt = _row_tile(lp, 520, 8)
    nt = lp // tt
    nblk = tt // 8
    hw = 512

    def body(u_ref, bk_ref, cre_ref, cim_ref, ap_ref, d_ref, y_ref, xs_ref, car_ref):
        t = pl.program_id(2)

        @pl.when(t == 0)
        def _():
            car_ref[...] = jnp.zeros_like(car_ref)

        u = u_ref[...]
        xs_ref[...] = _dot5(u, bk_ref[...])
        ap = ap_ref[...]
        apr, api = ap[:, :hw], ap[:, hw:]
        rows = lax.broadcasted_iota(jnp.int32, (8, hw), 0)

        def blk(i, carry):
            cr, ci = carry
            off = pl.multiple_of(i * 8, 8)
            x = xs_ref[pl.ds(off, 8), :]
            xr, xi = x[:, :hw], x[:, hw:]
            for d in (1, 2, 4):
                sr = jnp.where(rows < d, 0.0, pltpu.roll(xr, d, 0))
                si = jnp.where(rows < d, 0.0, pltpu.roll(xi, d, 0))
                mr, mi = _cmul(sr, si, apr[d - 1:d, :], api[d - 1:d, :])
                xr, xi = xr + mr, xi + mi
            mr, mi = _cmul(apr, api, cr, ci)
            xr, xi = xr + mr, xi + mi
            xs_ref[pl.ds(off, 8), :] = jnp.concatenate([xr, xi], axis=1)
            return xr[7:8, :], xi[7:8, :]

        c0 = car_ref[...]
        cr, ci = lax.fori_loop(0, nblk, blk, (c0[0:1, :hw], c0[0:1, hw:]))
        car_ref[...] = jnp.broadcast_to(jnp.concatenate([cr, ci], axis=1), car_ref.shape)
        xs = xs_ref[...]
        y_ref[...] = (_dot5(xs[:, :hw], cre_ref[...]) - _dot5(xs[:, hw:], cim_ref[...])
                      + d_ref[...] * u)

    ub = U_OFF // 128
    return _pcall(
        body, name="s5_fwd", grid=(S5_KCH, bsz, nt),
        in_specs=[pl.BlockSpec((None, tt, 128), lambda k, b, t: (b, t, ub + k)),
                  pl.BlockSpec((None, 128, 2 * hw), lambda k, b, t: (k, 0, 0)),
                  pl.BlockSpec((None, hw, 128), lambda k, b, t: (k, 0, 0)),
                  pl.BlockSpec((None, hw, 128), lambda k, b, t: (k, 0, 0)),
                  pl.BlockSpec((None, 8, 2 * hw), lambda k, b, t: (k, 0, 0)),
                  pl.BlockSpec((1, 128), lambda k, b, t: (0, k))],
        out_specs=[pl.BlockSpec((None, tt, 128), lambda k, b, t: (b, t, k)),
                   pl.BlockSpec((None, None, tt, 2 * hw), lambda k, b, t: (b, k, t, 0))],
        out_shape=[jax.ShapeDtypeStruct((bsz, lp, S5_KCH * 128), F32),
                   jax.ShapeDtypeStruct((bsz, S5_KCH, lp, 2 * hw), F32)],
        scratch_shapes=[pltpu.VMEM((8, 2 * hw), F32)],
        compiler_params=_cp(("parallel", "parallel", "arbitrary"), 40),
    )(p3, bk, cre, cim, apow, dskip)


def _s5_bwd(dp3, p3, dy3, xs, bk, cre, cim, apow_rev, dskip):
    bsz, lp, _ = p3.shape
    tt = _row_tile(lp, 520, 8)
    nt = lp // tt
    nblk = tt // 8
    hw = 512
    tb = tt // 8

    def body(dp_any, u_ref, dy_ref, xs_ref, halo_ref, bk_ref, cre_ref, cim_ref, ap_ref, d_ref,
             du_ref, dbk_ref, dcre_ref, dcim_ref, da_ref, dd_ref, g_ref, ext_ref, car_ref):
        b = pl.program_id(1)
        t = pl.program_id(2)
        tidx = nt - 1 - t

        @pl.when(t == 0)
        def _():
            car_ref[...] = jnp.zeros_like(car_ref)

        @pl.when((b == 0) & (t == 0))
        def _():
            dbk_ref[...] = jnp.zeros_like(dbk_ref)
            dcre_ref[...] = jnp.zeros_like(dcre_ref)
            dcim_ref[...] = jnp.zeros_like(dcim_ref)
            da_ref[...] = jnp.zeros_like(da_ref)
            dd_ref[...] = jnp.zeros_like(dd_ref)

        u = u_ref[...]
        dy = dy_ref[...]
        g_ref[:, :hw] = _dot5(dy, cre_ref[...], NT)
        g_ref[:, hw:] = -_dot5(dy, cim_ref[...], NT)
        ap = ap_ref[...]
        apr, api = ap[:, :hw], -ap[:, hw:]
        rows = lax.broadcasted_iota(jnp.int32, (8, hw), 0)

        def blk(i, carry):
            cr, ci = carry
            off = pl.multiple_of((nblk - 1 - i) * 8, 8)
            x = g_ref[pl.ds(off, 8), :]
            xr, xi = x[:, :hw], x[:, hw:]
            for d in (1, 2, 4):
                sr = jnp.where(rows >= 8 - d, 0.0, pltpu.roll(xr, 8 - d, 0))
                si = jnp.where(rows >= 8 - d, 0.0, pltpu.roll(xi, 8 - d, 0))
                mr, mi = _cmul(sr, si, apr[8 - d:9 - d, :], api[8 - d:9 - d, :])
                xr, xi = xr + mr, xi + mi
            mr, mi = _cmul(apr, api, cr, ci)
            xr, xi = xr + mr, xi + mi
            g_ref[pl.ds(off, 8), :] = jnp.concatenate([xr, xi], axis=1)
            return xr[0:1, :], xi[0:1, :]

        c0 = car_ref[...]
        cr, ci = lax.fori_loop(0, nblk, blk, (c0[0:1, :hw], c0[0:1, hw:]))
        car_ref[...] = jnp.broadcast_to(jnp.concatenate([cr, ci], axis=1), car_ref.shape)

        gg = g_ref[...]
        du = _dot5(gg, bk_ref[...], NT) + d_ref[...] * dy
        trow = tidx * tt + lax.broadcasted_iota(jnp.int32, (tt, 1), 0)
        du_ref[...] = jnp.where(trow >= PAD, du, 0.0)
        dbk_ref[...] += _dot5(u, gg, TN)
        xsv = xs_ref[...]
        dcre_ref[...] += _dot5(xsv[:, :hw], dy, TN)
        dcim_ref[...] -= _dot5(xsv[:, hw:], dy, TN)
        dd_ref[...] += _colsum(dy * u)
        ext_ref[0:8, :] = jnp.where(tidx == 0, 0.0, halo_ref[...])
        ext_ref[8:, :] = xsv
        xp = ext_ref[pl.ds(7, tt), :]
        gr, gi, pr, pi = gg[:, :hw], gg[:, hw:], xp[:, :hw], xp[:, hw:]
        da_ref[:, :hw] += _colsum(gr * pr + gi * pi)
        da_ref[:, hw:] += _colsum(gi * pr - gr * pi)

    ub = U_OFF // 128
    sd = jax.ShapeDtypeStruct
    rt = lambda t: nt - 1 - t
    res = _pcall(
        body, name="s5_bwd", grid=(S5_KCH, bsz, nt),
        in_specs=[pl.BlockSpec(memory_space=pl.ANY),
                  pl.BlockSpec((None, tt, 128), lambda k, b, t: (b, rt(t), ub + k)),
                  pl.BlockSpec((None, tt, 128), lambda k, b, t: (b, rt(t), k)),
                  pl.BlockSpec((None, None, tt, 2 * hw), lambda k, b, t: (b, k, rt(t), 0)),
                  pl.BlockSpec((None, None, 8, 2 * hw), lambda k, b, t: (b, k, jnp.maximum(rt(t) * tb - 1, 0), 0)),
                  pl.BlockSpec((None, 128, 2 * hw), lambda k, b, t: (k, 0, 0)),
                  pl.BlockSpec((None, hw, 128), lambda k, b, t: (k, 0, 0)),
                  pl.BlockSpec((None, hw, 128), lambda k, b, t: (k, 0, 0)),
                  pl.BlockSpec((None, 8, 2 * hw), lambda k, b, t: (k, 0, 0)),
                  pl.BlockSpec((1, 128), lambda k, b, t: (0, k))],
        out_specs=[pl.BlockSpec((None, tt, 128), lambda k, b, t: (b, rt(t), ub + k)),
                   pl.BlockSpec((None, 128, 2 * hw), lambda k, b, t: (k, 0, 0)),
                   pl.BlockSpec((None, hw, 128), lambda k, b, t: (k, 0, 0)),
                   pl.BlockSpec((None, hw, 128), lambda k, b, t: (k, 0, 0)),
                   pl.BlockSpec((None, 1, 2 * hw), lambda k, b, t: (k, 0, 0)),
                   pl.BlockSpec((1, 128), lambda k, b, t: (0, k))],
        out_shape=[sd(dp3.shape, F32), sd((S5_KCH, 128, 2 * hw), F32), sd((S5_KCH, hw, 128), F32),
                   sd((S5_KCH, hw, 128), F32), sd((S5_KCH, 1, 2 * hw), F32), sd((1, S5_KCH * 128), F32)],
        scratch_shapes=[pltpu.VMEM((tt, 2 * hw), F32), pltpu.VMEM((tt + 8, 2 * hw), F32), pltpu.VMEM((8, 2 * hw), F32)],
        input_output_aliases={0: 0},
        compiler_params=_cp(("arbitrary", "arbitrary", "arbitrary"), 48),
    )(dp3, p3, dy3, xs, xs, bk, cre, cim, apow_rev, dskip)
    return res


_G0 = math.sqrt(2.0 / math.pi)
_G1 = 0.044715


def _gelu(y):
    return 0.5 * y * (1.0 + jnp.tanh(_G0 * (y + _G1 * y * y * y)))


def _gelu_grad(y):
    th = jnp.tanh(_G0 * (y + _G1 * y * y * y))
    return 0.5 * (1.0 + th) + 0.5 * y * (1.0 - th * th) * _G0 * (1.0 + 3.0 * _G1 * y * y)


def _glu_fwd(y_s5, wglu_g, lp):
    r, w = y_s5.shape
    tm = _row_tile(lp, 416)
    cw = wglu_g.shape[2]

    def body(y_ref, w_ref, gy_ref, z_ref, o_ref):
        gy = _bf(_gelu(y_ref[...]))
        gy_ref[...] = gy
        zs = [_dot(gy, w_ref[s]) for s in range(4)]
        for s in range(4):
            z_ref[:, s * cw:(s + 1) * cw] = zs[s]
        o_ref[:, :cw] = zs[0] * _sig(zs[2])
        o_ref[:, cw:] = zs[1] * _sig(zs[3])

    sd = jax.ShapeDtypeStruct
    return _pcall(
        body, name="glu_fwd", grid=(r // tm,),
        in_specs=[pl.BlockSpec((tm, w), lambda i: (i, 0)), _resident(wglu_g.shape)],
        out_specs=[pl.BlockSpec((tm, w), lambda i: (i, 0)), pl.BlockSpec((tm, 4 * cw), lambda i: (i, 0)),
                   pl.BlockSpec((tm, 2 * cw), lambda i: (i, 0))],
        out_shape=[sd((r, w), BF16), sd((r, 4 * cw), F32), sd((r, 2 * cw), F32)],
        compiler_params=_cp(("parallel",), 40),
    )(y_s5, wglu_g)


def _glu_bwd(dyg, z, y_s5, wglu_g, lp):
    r, w = y_s5.shape
    tm = _row_tile(lp, 416)
    cw = wglu_g.shape[2]

    def body(d_ref, z_ref, y_ref, w_ref, dz_ref, dy_ref):
        d = d_ref[...]
        zz = z_ref[...]
        acc = jnp.zeros((tm, w), F32)
        for s in range(2):
            z1 = zz[:, s * cw:(s + 1) * cw]
            sg = _sig(zz[:, (2 + s) * cw:(3 + s) * cw])
            dd = d[:, s * cw:(s + 1) * cw]
            dz1 = _bf(dd * sg)
            dz2 = _bf(dd * z1 * sg * (1.0 - sg))
            dz_ref[:, s * cw:(s + 1) * cw] = dz1
            dz_ref[:, (2 + s) * cw:(3 + s) * cw] = dz2
            acc += _dot(dz1, w_ref[s], NT) + _dot(dz2, w_ref[2 + s], NT)
        dy_ref[...] = acc * _gelu_grad(y_ref[...])

    sd = jax.ShapeDtypeStruct
    return _pcall(
        body, name="glu_bwd", grid=(r // tm,),
        in_specs=[pl.BlockSpec((tm, 2 * cw), lambda i: (i, 0)), pl.BlockSpec((tm, 4 * cw), lambda i: (i, 0)),
                  pl.BlockSpec((tm, w), lambda i: (i, 0)), _resident(wglu_g.shape)],
        out_specs=[pl.BlockSpec((tm, 4 * cw), lambda i: (i, 0)), pl.BlockSpec((tm, w), lambda i: (i, 0))],
        out_shape=[sd((r, 4 * cw), BF16), sd((r, w), F32)],
        compiler_params=_cp(("parallel",), 40),
    )(dyg, z, y_s5, wglu_g)


def _conv_fwd(p3, cw, cb):
    bsz, lp, _ = p3.shape
    tt = _row_tile(lp, 416)
    nt = lp // tt
    tb = tt // 8
    c = cw.shape[1]
    qb = Q_OFF // c

    def body(x_ref, halo_ref, w_ref, b_ref, pre_ref, act_ref, ext_ref):
        t = pl.program_id(1)
        ext_ref[0:8, :] = jnp.where(t == 0, 0.0, halo_ref[...])
        ext_ref[8:, :] = x_ref[...]
        w = w_ref[...]
        acc = b_ref[...] + w[0:1, :] * ext_ref[pl.ds(5, tt), :]
        for j in range(1, CONV_W):
            acc = acc + w[j:j + 1, :] * ext_ref[pl.ds(5 + j, tt), :]
        pre_ref[...] = acc
        act_ref[...] = acc * _sig(acc)

    sd = jax.ShapeDtypeStruct
    return _pcall(
        body, name="conv_fwd", grid=(bsz, nt),
        in_specs=[pl.BlockSpec((None, tt, c), lambda b, t: (b, t, qb)),
                  pl.BlockSpec((None, 8, c), lambda b, t: (b, jnp.maximum(t * tb - 1, 0), qb)),
                  _const((CONV_W, c)), _const((1, c))],
        out_specs=[pl.BlockSpec((None, tt, c), lambda b, t: (b, t, 0))] * 2,
        out_shape=[sd((bsz, lp, c), F32)] * 2,
        scratch_shapes=[pltpu.VMEM((tt + 8, c), F32)],
        compiler_params=_cp(("parallel", "parallel")),
    )(p3, p3, cw, cb)


def _conv_bwd(dp3, p3, dact3, pre3, cw):
    bsz, lp, _ = p3.shape
    tt = _row_tile(lp, 416)
    nt = lp // tt
    tb = tt // 8
    c = cw.shape[1]
    qb = Q_OFF // c

    def silu_grad(x):
        s = _sig(x)
        return s * (1.0 + x * (1.0 - s))

    def body(dp_any, x_ref, xh_ref, d_ref, dh_ref, pre_ref, preh_ref, w_ref, o_ref, dw_ref, db_ref, ext_ref, dext_ref):
        b = pl.program_id(0)
        t = pl.program_id(1)

        @pl.when((b == 0) & (t == 0))
        def _():
            dw_ref[...] = jnp.zeros_like(dw_ref)
            db_ref[...] = jnp.zeros_like(db_ref)

        dc = d_ref[...] * silu_grad(pre_ref[...])
        dch = jnp.where(t == nt - 1, 0.0, dh_ref[...] * silu_grad(preh_ref[...]))
        dext_ref[0:tt, :] = dc
        dext_ref[tt:, :] = dch
        ext_ref[0:8, :] = jnp.where(t == 0, 0.0, xh_ref[...])
        ext_ref[8:, :] = x_ref[...]
        w = w_ref[...]
        acc = w[CONV_W - 1:CONV_W, :] * dc
        for j in range(CONV_W - 1):
            acc = acc + w[j:j + 1, :] * dext_ref[pl.ds(CONV_W - 1 - j, tt), :]
        trow = t * tt + lax.broadcasted_iota(jnp.int32, (tt, 1), 0)
        o_ref[...] = jnp.where(trow >= PAD, acc, 0.0)
        db_ref[...] += _colsum(dc)
        for j in range(CONV_W):
            dw_ref[j:j + 1, :] += _colsum(dc * ext_ref[pl.ds(5 + j, tt), :])

    sd = jax.ShapeDtypeStruct
    nxt = lambda t: jnp.minimum((t + 1) * tb, lp // 8 - 1)
    return _pcall(
        body, name="conv_bwd", grid=(bsz, nt),
        in_specs=[pl.BlockSpec(memory_space=pl.ANY),
                  pl.BlockSpec((None, tt, c), lambda b, t: (b, t, qb)),
                  pl.BlockSpec((None, 8, c), lambda b, t: (b, jnp.maximum(t * tb - 1, 0), qb)),
                  pl.BlockSpec((None, tt, c), lambda b, t: (b, t, 0)),
                  pl.BlockSpec((None, 8, c), lambda b, t: (b, nxt(t), 0)),
                  pl.BlockSpec((None, tt, c), lambda b, t: (b, t, 0)),
                  pl.BlockSpec((None, 8, c), lambda b, t: (b, nxt(t), 0)),
                  _const((CONV_W, c))],
        out_specs=[pl.BlockSpec((None, tt, c), lambda b, t: (b, t, qb)), _const((CONV_W, c)), _const((1, c))],
        out_shape=[sd(dp3.shape, F32), sd((CONV_W, c), F32), sd((1, c), F32)],
        scratch_shapes=[pltpu.VMEM((tt + 8, c), F32), pltpu.VMEM((tt + 8, c), F32)],
        input_output_aliases={0: 0},
        compiler_params=_cp(("arbitrary", "arbitrary")),
    )(dp3, p3, p3, dact3, dact3, pre3, pre3, cw)


def _mlstm_gates(g, h_idx, c_idx, lc):
    lane = lax.broadcasted_iota(jnp.int32, g.shape, 1)
    i_col = jnp.sum(jnp.where(lane == h_idx, g, 0.0), axis=1, keepdims=True)
    f_col = jnp.sum(jnp.where(lane == M_HEADS + h_idx, g, 0.0), axis=1, keepdims=True)
    row = lax.broadcasted_iota(jnp.int32, (lc, 1), 0)
    valid = (c_idx * lc + row) >= PAD
    li = jnp.where(valid, i_col, NEG)
    lf = jnp.where(valid, jnp.minimum(f_col, 0.0) - jnp.log(1.0 + jnp.exp(-jnp.abs(f_col))), 0.0)
    r2 = lax.broadcasted_iota(jnp.int32, (lc, lc), 0)
    c2 = lax.broadcasted_iota(jnp.int32, (lc, lc), 1)
    eye = r2 == c2
    tril = r2 >= c2
    to_row = lambda col: jnp.sum(jnp.where(eye, col, 0.0), axis=0, keepdims=True)
    lf_row = to_row(lf)
    b_col = jnp.sum(jnp.where(tril, lf_row, 0.0), axis=1, keepdims=True)
    b_row = to_row(b_col)
    li_row = to_row(li)
    d_mat = jnp.where(tril, b_col - b_row + li_row, NEG)
    return dict(f_col=f_col, valid=valid, li=li, b_col=b_col, d_mat=d_mat, eye=eye, r2=r2, c2=c2, row=row,
                to_row=to_row)


def _mlstm_chunk(q, ks, v, gq, c_st, n_st, m_st, lc):
    b_col, d_mat = gq["b_col"], gq["d_mat"]
    m_inter = b_col + m_st
    m_row = jnp.maximum(m_inter, jnp.max(d_mat, axis=1, keepdims=True))
    w_intra = jnp.exp(d_mat - m_row)
    w_inter = jnp.exp(m_inter - m_row)
    qb, kb, vb, cb = _bf(q), _bf(ks), _bf(v), _bf(c_st)
    s = _dot(qb, kb, NT) * w_intra
    qc = _dot(qb, cb)
    num = _dot(_bf(s), vb) + w_inter * qc
    qn = jnp.sum(q * n_st, axis=1, keepdims=True)
    den = jnp.sum(s, axis=1, keepdims=True) + w_inter * qn
    e = jnp.exp(-m_row)
    nn = jnp.maximum(jnp.abs(den), e)
    b_last = b_col[lc - 1:lc, :]
    g_log = b_last - b_col + gq["li"]
    m_new = jnp.maximum(b_last + m_st, jnp.max(g_log, axis=0, keepdims=True))
    w_k = jnp.exp(g_log - m_new)
    decay = jnp.exp(b_last + m_st - m_new)
    return dict(w_intra=w_intra, w_inter=w_inter, qb=qb, kb=kb, vb=vb, cb=cb, s=s, qc=qc, num=num, qn=qn, den=den,
                e=e, nn=nn, m_new=m_new, w_k=w_k, decay=decay)


def _mlstm_fwd(qk3, p3):
    bsz, lp, _ = p3.shape
    lc = M_CHUNK
    nc = lp // lc
    dk, dv = 128, 256
    scale = dk ** -0.5

    def body(q_ref, k_ref, v_ref, g_ref, h_ref, cs_ref, ns_ref, ms_ref, c_sc, n_sc, m_sc):
        c = pl.program_id(1)

        @pl.when(c == 0)
        def _():
            c_sc[...] = jnp.zeros_like(c_sc)
            n_sc[...] = jnp.zeros_like(n_sc)
            m_sc[...] = jnp.zeros_like(m_sc)

        g = g_ref[...]
        for hh in range(M_HEADS):
            c_st, n_st, m_all = c_sc[hh], n_sc[hh], m_sc[hh]
            cs_ref[hh] = c_st
            ns_ref[hh] = n_st
            ms_ref[hh] = m_all
            m_st = m_all[:, 0:1]
            q = q_ref[:, hh * dk:(hh + 1) * dk]
            ks = k_ref[:, hh * dk:(hh + 1) * dk] * scale
            v = v_ref[:, hh * dv:(hh + 1) * dv]
            gq = _mlstm_gates(g, hh, c, lc)
            f = _mlstm_chunk(q, ks, v, gq, c_st, n_st, m_st, lc)
            h_ref[:, hh * dv:(hh + 1) * dv] = f["num"] / f["nn"]
            kw = ks * f["w_k"]
            c_sc[hh] = f["decay"] * c_st + _dot(_bf(kw), f["vb"], TN)
            n_sc[hh] = f["decay"] * n_st + _colsum(kw)
            m_sc[hh] = jnp.broadcast_to(f["m_new"], (1, 128))

    sd = jax.ShapeDtypeStruct
    nh = M_HEADS
    return _pcall(
        body, name="mlstm_fwd", grid=(bsz, nc),
        in_specs=[pl.BlockSpec((None, lc, nh * dk), lambda b, c: (b, c, 0)),
                  pl.BlockSpec((None, lc, nh * dk), lambda b, c: (b, c, 1)),
                  pl.BlockSpec((None, lc, nh * dv), lambda b, c: (b, c, V_OFF // (nh * dv))),
                  pl.BlockSpec((None, lc, 128), lambda b, c: (b, c, G_OFF // 128))],
        out_specs=[pl.BlockSpec((None, lc, nh * dv), lambda b, c: (b, c, 0)),
                   pl.BlockSpec((None, nh, None, dk, dv), lambda b, c: (b, 0, c, 0, 0)),
                   pl.BlockSpec((None, nh, None, 1, dk), lambda b, c: (b, 0, c, 0, 0)),
                   pl.BlockSpec((None, nh, None, 1, 128), lambda b, c: (b, 0, c, 0, 0))],
        out_shape=[sd((bsz, lp, nh * dv), F32), sd((bsz, nh, nc, dk, dv), F32),
                   sd((bsz, nh, nc, 1, dk), F32), sd((bsz, nh, nc, 1, 128), F32)],
        scratch_shapes=[pltpu.VMEM((nh, dk, dv), F32), pltpu.VMEM((nh, 1, dk), F32), pltpu.VMEM((nh, 1, 128), F32)],
        compiler_params=_cp(("parallel", "arbitrary")),
    )(qk3, qk3, p3, p3)


def _mlstm_bwd(dp3, qk3, p3, dh3, cs, ns, ms):
    bsz, lp, _ = p3.shape
    lc = M_CHUNK
    nc = lp // lc
    dk, dv = 128, 256
    scale = dk ** -0.5

    def body(dp_any, q_ref, k_ref, v_ref, g_ref, dh_ref, cs_ref, ns_ref, ms_ref,
             dv_ref, dqk_ref, dg_ref, dc_sc, dn_sc):
        t = pl.program_id(1)
        c = nc - 1 - t

        @pl.when(t == 0)
        def _():
            dc_sc[...] = jnp.zeros_like(dc_sc)
            dn_sc[...] = jnp.zeros_like(dn_sc)

        g = g_ref[...]
        lane = lax.broadcasted_iota(jnp.int32, (lc, 128), 1)
        dgate = jnp.zeros((lc, 128), F32)
        for hh in range(M_HEADS):
            dgate = head(hh, c, g, lane, dgate, q_ref, k_ref, v_ref, dh_ref, cs_ref, ns_ref, ms_ref,
                         dv_ref, dqk_ref, dc_sc, dn_sc)
        dg_ref[...] = dgate

    def head(hh, c, g, lane, dgate, q_ref, k_ref, v_ref, dh_ref, cs_ref, ns_ref, ms_ref, dv_ref, dqk_ref, dc_sc, dn_sc):
        c_st, n_st = cs_ref[hh], ns_ref[hh]
        m_st = ms_ref[hh][:, 0:1]
        q = q_ref[:, hh * dk:(hh + 1) * dk]
        ks = k_ref[:, hh * dk:(hh + 1) * dk] * scale
        v = v_ref[:, hh * dv:(hh + 1) * dv]
        dh = dh_ref[:, hh * dv:(hh + 1) * dv]
        gq = _mlstm_gates(g, hh, c, lc)
        f = _mlstm_chunk(q, ks, v, gq, c_st, n_st, m_st, lc)
        eye, r2, c2, row, valid = gq["eye"], gq["r2"], gq["c2"], gq["row"], gq["valid"]
        w_intra, w_inter, s, nn, den = f["w_intra"], f["w_inter"], f["s"], f["nn"], f["den"]
        qb, kb, vb, cb, w_k, decay = f["qb"], f["kb"], f["vb"], f["cb"], f["w_k"], f["decay"]
        d_c, d_n = dc_sc[hh], dn_sc[hh]
        d_cb = _bf(d_c)

        hout = f["num"] / nn
        dnum = dh / nn
        d_nn = -jnp.sum(dh * hout, axis=1, keepdims=True) / nn
        dden = jnp.where(jnp.abs(den) > f["e"], d_nn * jnp.sign(den), 0.0)
        wdnum = w_inter * dnum
        wdden = w_inter * dden
        ds = _dot(_bf(dnum), vb, NT) + dden
        dsw = _bf(ds * w_intra)
        dq = _dot(dsw, kb) + _dot(_bf(wdnum), cb, NT) + wdden * n_st
        dkw = _dot(vb, d_cb, NT) + d_n
        dks = _dot(dsw, qb, TN) + dkw * w_k
        kw = ks * w_k
        dvv = _dot(_bf(s), _bf(dnum), TN) + _dot(_bf(kw), d_cb)
        dd = ds * s
        rs = jnp.sum(dd, axis=1, keepdims=True)
        cs_col = jnp.sum(jnp.where(eye, jnp.sum(dd, axis=0, keepdims=True), 0.0), axis=1, keepdims=True)
        dwi = jnp.sum(dnum * f["qc"], axis=1, keepdims=True) + dden * f["qn"]
        db = rs - cs_col + dwi * w_inter
        dli = cs_col
        ddecay = jnp.sum(jnp.sum(d_c * c_st, axis=1, keepdims=True), axis=0, keepdims=True) \
            + jnp.sum(d_n * n_st, axis=1, keepdims=True)
        dgl = jnp.sum(dkw * ks, axis=1, keepdims=True) * w_k
        dblast = ddecay * decay + jnp.sum(dgl, axis=0, keepdims=True)
        db = db - dgl + jnp.where(row == lc - 1, dblast, 0.0)
        dli = dli + dgl
        db_row = gq["to_row"](db)
        dlf = jnp.sum(jnp.where(c2 >= r2, db_row, 0.0), axis=1, keepdims=True)
        dlf = jnp.where(valid, dlf, 0.0)
        dgate = jnp.where(lane == hh, jnp.where(valid, dli, 0.0), dgate)
        dgate = jnp.where(lane == M_HEADS + hh, dlf * _sig(-gq["f_col"]), dgate)
        dqk_ref[:, hh * dk:(hh + 1) * dk] = dq
        dqk_ref[:, (M_HEADS + hh) * dk:(M_HEADS + hh + 1) * dk] = dks * scale
        dv_ref[:, hh * dv:(hh + 1) * dv] = dvv
        dc_sc[hh] = decay * d_c + _dot(qb, _bf(wdnum), TN)
        dn_sc[hh] = decay * d_n + _colsum(q * wdden)
        return dgate

    sd = jax.ShapeDtypeStruct
    nh = M_HEADS
    rc = lambda c: nc - 1 - c
    return _pcall(
        body, name="mlstm_bwd", grid=(bsz, nc),
        in_specs=[pl.BlockSpec(memory_space=pl.ANY),
                  pl.BlockSpec((None, lc, nh * dk), lambda b, c: (b, rc(c), 0)),
                  pl.BlockSpec((None, lc, nh * dk), lambda b, c: (b, rc(c), 1)),
                  pl.BlockSpec((None, lc, nh * dv), lambda b, c: (b, rc(c), V_OFF // (nh * dv))),
                  pl.BlockSpec((None, lc, 128), lambda b, c: (b, rc(c), G_OFF // 128)),
                  pl.BlockSpec((None, lc, nh * dv), lambda b, c: (b, rc(c), 0)),
                  pl.BlockSpec((None, nh, None, dk, dv), lambda b, c: (b, 0, rc(c), 0, 0)),
                  pl.BlockSpec((None, nh, None, 1, dk), lambda b, c: (b, 0, rc(c), 0, 0)),
                  pl.BlockSpec((None, nh, None, 1, 128), lambda b, c: (b, 0, rc(c), 0, 0))],
        out_specs=[pl.BlockSpec((None, lc, nh * dv), lambda b, c: (b, rc(c), V_OFF // (nh * dv))),
                   pl.BlockSpec((None, lc, 2 * nh * dk), lambda b, c: (b, rc(c), 0)),
                   pl.BlockSpec((None, lc, 128), lambda b, c: (b, rc(c), 0))],
        out_shape=[sd(dp3.shape, F32), sd((bsz, lp, 2 * nh * dk), F32), sd((bsz, lp, 128), F32)],
        scratch_shapes=[pltpu.VMEM((nh, dk, dv), F32), pltpu.VMEM((nh, 1, dk), F32)],
        input_output_aliases={0: 0},
        compiler_params=_cp(("arbitrary", "arbitrary")),
    )(dp3, qk3, qk3, p3, p3, dh3, cs, ns, ms)


def _headnorm(x):
    dv = x.shape[1] // M_HEADS
    xh, rs = [], []
    for h in range(M_HEADS):
        xx = x[:, h * dv:(h + 1) * dv]
        mu = jnp.mean(xx, axis=-1, keepdims=True)
        xc = xx - mu
        rstd = lax.rsqrt(jnp.mean(xc * xc, axis=-1, keepdims=True) + LN_EPS)
        xh.append(xc * rstd)
        rs.append(rstd)
    return jnp.concatenate(xh, axis=1), rs


def _mix_fwd(hm, p, ys5g, h0, gn, wmo_bf, wo_bf, g1, b1, lp):
    r, d = hm.shape
    tm = _row_tile(lp, 208)

    def body(hm_ref, o_ref, gs_ref, gm_ref, ys_ref, h0_ref, gn_ref, wmo_ref, wo_ref, g1_ref, b1_ref,
             ymin_ref, ym_ref, mix_ref, r1_ref, h1_ref):
        xhat, _ = _headnorm(hm_ref[...])
        ymin = _bf(_sig(o_ref[...]) * (xhat * gn_ref[...]))
        ymin_ref[...] = ymin
        ym = _dot(ymin, wmo_ref[...])
        ym_ref[...] = ym
        mix = _bf(_sig(gs_ref[...]) * ys_ref[...] + _sig(gm_ref[...]) * ym)
        mix_ref[...] = mix
        r1 = ALPHA * h0_ref[...] + _dot(mix, wo_ref[...])
        r1_ref[...] = r1
        h1, _, _ = _ln_fwd(r1, g1_ref[...], b1_ref[...])
        h1_ref[...] = h1

    sd = jax.ShapeDtypeStruct
    row = pl.BlockSpec((tm, d), lambda i: (i, 0))
    return _pcall(
        body, name="mix_fwd", grid=(r // tm,),
        in_specs=[row, pl.BlockSpec((tm, d), lambda i: (i, O_OFF // d)), pl.BlockSpec((tm, d), lambda i: (i, GS_OFF // d)),
                  pl.BlockSpec((tm, d), lambda i: (i, GM_OFF // d)), row, row, _const((1, d)),
                  _resident((d, d)), _resident((d, d)), _const((1, d)), _const((1, d))],
        out_specs=[row] * 5,
        out_shape=[sd((r, d), BF16), sd((r, d), F32), sd((r, d), BF16), sd((r, d), F32), sd((r, d), F32)],
        compiler_params=_cp(("parallel",), 48),
    )(hm, p, p, p, ys5g, h0, gn, wmo_bf, wo_bf, g1, b1)


def _mix_bwd(dh1, r1, g1, wo_bf, wmo_bf, p, ys5g, ym, hm, gn, lp):
    r, d = hm.shape
    tm = _row_tile(lp, 208)
    dv = d // M_HEADS

    def body(dh1_ref, r1_ref, g1_ref, wo_ref, wmo_ref, o_ref, gs_ref, gm_ref, ys_ref, ym_ref, hm_ref, gn_ref,
             dr1_ref, dp_ref, dys_ref, dym_ref, dhm_ref, dg1_ref, db1_ref, dgn_ref):
        i = pl.program_id(0)

        @pl.when(i == 0)
        def _():
            dg1_ref[...] = jnp.zeros_like(dg1_ref)
            db1_ref[...] = jnp.zeros_like(db1_ref)
            dgn_ref[...] = jnp.zeros_like(dgn_ref)

        dh1 = dh1_ref[...]
        _, xhat1, rstd1 = _ln_fwd(r1_ref[...], g1_ref[...], 0.0)
        dr1 = _ln_bwd(dh1, xhat1, rstd1, g1_ref[...])
        dr1_ref[...] = dr1
        dg1_ref[...] += _colsum(dh1 * xhat1)
        db1_ref[...] += _colsum(dh1)
        dmix = _dot(_bf(dr1), wo_ref[...], NT)
        sgs, sgm, so = _sig(gs_ref[...]), _sig(gm_ref[...]), _sig(o_ref[...])
        dys_ref[...] = dmix * sgs
        dp_ref[:, d:2 * d] = dmix * ys_ref[...] * sgs * (1.0 - sgs)
        dym = dmix * sgm
        dym_ref[...] = _bf(dym)
        dp_ref[:, 2 * d:3 * d] = dmix * ym_ref[...] * sgm * (1.0 - sgm)
        dymin = _dot(_bf(dym), wmo_ref[...], NT)
        xhat, rs = _headnorm(hm_ref[...])
        gn_ = gn_ref[...]
        dp_ref[:, 0:d] = dymin * (xhat * gn_) * so * (1.0 - so)
        dhn = dymin * so
        dgn_ref[...] += _colsum(dhn * xhat)
        dxh = dhn * gn_
        for h in range(M_HEADS):
            sl = slice(h * dv, (h + 1) * dv)
            a, xh = dxh[:, sl], xhat[:, sl]
            m1 = jnp.mean(a, axis=-1, keepdims=True)
            m2 = jnp.mean(a * xh, axis=-1, keepdims=True)
            dhm_ref[:, sl] = rs[h] * (a - m1 - xh * m2)

    sd = jax.ShapeDtypeStruct
    row = pl.BlockSpec((tm, d), lambda i: (i, 0))
    vec = _const((1, d))
    return _pcall(
        body, name="mix_bwd", grid=(r // tm,),
        in_specs=[row, row, vec, _resident((d, d)), _resident((d, d)),
                  pl.BlockSpec((tm, d), lambda i: (i, O_OFF // d)), pl.BlockSpec((tm, d), lambda i: (i, GS_OFF // d)),
                  pl.BlockSpec((tm, d), lambda i: (i, GM_OFF // d)), row, row, row, vec],
        out_specs=[row, pl.BlockSpec((tm, 3 * d), lambda i: (i, 0)), row, row, row, vec, vec, vec],
        out_shape=[sd((r, d), F32), sd((r, NP), F32), sd((r, d), F32), sd((r, d), BF16), sd((r, d), F32),
                   sd((1, d), F32), sd((1, d), F32), sd((1, d), F32)],
        compiler_params=_cp(("arbitrary",), 48),
    )(dh1, r1, g1, wo_bf, wmo_bf, p, p, p, ys5g, ym, hm, gn)


def _mlp_fwd(h1, tgt, wup_g, wdn_bf, bup, g2, b2, lp):
    r, d = h1.shape
    tm = _row_tile(lp, 320)
    tps = lp // tm
    nf = wup_g.shape[0]

    def body(h1_ref, t_ref, wup_ref, wdn_ref, bup_ref, g2_ref, b2_ref, dr2_ref, act_ref, loss_ref, dg2_ref, db2_ref):
        i = pl.program_id(0)

        @pl.when(i == 0)
        def _():
            loss_ref[...] = jnp.zeros_like(loss_ref)
            dg2_ref[...] = jnp.zeros_like(dg2_ref)
            db2_ref[...] = jnp.zeros_like(db2_ref)

        h1 = h1_ref[...]
        h1b = _bf(h1)
        ff = jnp.zeros((tm, d), F32)
        for s in range(nf):
            up = _dot(h1b, wup_ref[s]) + bup_ref[:, s * d:(s + 1) * d]
            a = jnp.maximum(up, 0.0)
            a = _bf(a * a)
            act_ref[:, s * d:(s + 1) * d] = a
            ff = ff + _dot(a, wdn_ref[s * d:(s + 1) * d, :])
        r2 = ALPHA * h1 + ff
        g2 = g2_ref[...]
        y, xhat, rstd = _ln_fwd(r2, g2, b2_ref[...])
        t = (i % tps) * tm + lax.broadcasted_iota(jnp.int32, (tm, 1), 0)
        diff = jnp.where(t >= PAD + N_META, y - t_ref[...], 0.0)
        loss_ref[...] += 0.5 / d * jnp.sum(jnp.sum(diff * diff, axis=1, keepdims=True), axis=0, keepdims=True)
        dy = diff * (1.0 / d)
        dg2_ref[...] += _colsum(dy * xhat)
        db2_ref[...] += _colsum(dy)
        dr2_ref[...] = _ln_bwd(dy, xhat, rstd, g2)

    sd = jax.ShapeDtypeStruct
    row = pl.BlockSpec((tm, d), lambda i: (i, 0))
    vec = _const((1, d))
    return _pcall(
        body, name="mlp_fwd", grid=(r // tm,),
        in_specs=[row, row, _resident(wup_g.shape), _resident(wdn_bf.shape), _const((1, nf * d)), vec, vec],
        out_specs=[row, pl.BlockSpec((tm, nf * d), lambda i: (i, 0)), _const((1, 128)), vec, vec],
        out_shape=[sd((r, d), F32), sd((r, nf * d), BF16), sd((1, 128), F32), sd((1, d), F32), sd((1, d), F32)],
        compiler_params=_cp(("arbitrary",), 56),
    )(h1, tgt, wup_g, wdn_bf, bup, g2, b2)


def _mlp_bwd(h1, dr2, wup_g, wdn_bf, bup, lp):
    r, d = h1.shape
    tm = _row_tile(lp, 320)
    nf = wup_g.shape[0]

    def body(h1_ref, dr2_ref, wup_ref, wdn_ref, bup_ref, dh1_ref, dup_ref, dbup_ref):
        i = pl.program_id(0)

        @pl.when(i == 0)
        def _():
            dbup_ref[...] = jnp.zeros_like(dbup_ref)

        h1b = _bf(h1_ref[...])
        dr2 = dr2_ref[...]
        dr2b = _bf(dr2)
        acc = ALPHA * dr2
        for s in range(nf):
            up = _dot(h1b, wup_ref[s]) + bup_ref[:, s * d:(s + 1) * d]
            dact = _dot(dr2b, wdn_ref[s * d:(s + 1) * d, :], NT)
            dup = dact * (2.0 * jnp.maximum(up, 0.0))
            dbup_ref[:, s * d:(s + 1) * d] += _colsum(dup)
            dupb = _bf(dup)
            dup_ref[:, s * d:(s + 1) * d] = dupb
            acc = acc + _dot(dupb, wup_ref[s], NT)
        dh1_ref[...] = acc

    sd = jax.ShapeDtypeStruct
    row = pl.BlockSpec((tm, d), lambda i: (i, 0))
    return _pcall(
        body, name="mlp_bwd", grid=(r // tm,),
        in_specs=[row, row, _resident(wup_g.shape), _resident(wdn_bf.shape), _const((1, nf * d))],
        out_specs=[row, pl.BlockSpec((tm, nf * d), lambda i: (i, 0)), _const((1, nf * d))],
        out_shape=[sd((r, d), F32), sd((r, nf * d), BF16), sd((1, nf * d), F32)],
        compiler_params=_cp(("arbitrary",), 56),
    )(h1, dr2, wup_g, wdn_bf, bup)


def _s5_block_mats(bb_re_t, bb_im_t, c_re, c_im, ap_re, ap_im):
    ng = c_re.shape[0]
    gl = ng // S5_KCH
    eye = jnp.eye(gl, dtype=F32)

    def bmat(bt):
        bb = jnp.transpose(bt, (1, 0, 2)).reshape(S5_KCH, gl, S5_GROUP, S5_STATE)
        return jnp.einsum("kghp,gj->kghjp", bb, eye).reshape(S5_KCH, gl * S5_GROUP, gl * S5_STATE)

    def cmat(c):
        cc = c.reshape(S5_KCH, gl, S5_GROUP, S5_STATE)
        return jnp.einsum("kghp,gj->kjpgh", cc, eye).reshape(S5_KCH, gl * S5_STATE, gl * S5_GROUP)

    def pw(a):
        return jnp.transpose(a.reshape(8, S5_KCH, gl * S5_STATE), (1, 0, 2))

    bk = jnp.concatenate([bmat(bb_re_t), bmat(bb_im_t)], axis=-1)
    apow = jnp.concatenate([pw(ap_re), pw(ap_im)], axis=-1)
    return _bf(bk), _bf(cmat(c_re)), _bf(cmat(c_im)), apow


def _s5_block_grads(dbk, dcre, dcim, da):
    gl = dbk.shape[1] // S5_GROUP
    ng = gl * S5_KCH
    eye = jnp.eye(gl, dtype=F32)
    hw = gl * S5_STATE

    def bpart(x):
        x = x.reshape(S5_KCH, gl, S5_GROUP, gl, S5_STATE)
        x = jnp.einsum("kghjp,gj->kghp", x, eye).reshape(ng, S5_GROUP, S5_STATE)
        return jnp.transpose(x, (1, 0, 2))

    def cpart(x):
        x = x.reshape(S5_KCH, gl, S5_STATE, gl, S5_GROUP)
        return jnp.einsum("kjpgh,gj->kghp", x, eye).reshape(ng, S5_GROUP, S5_STATE)

    return (bpart(dbk[..., :hw]), bpart(dbk[..., hw:]), cpart(dcre), cpart(dcim),
            da[:, 0, :hw].reshape(ng, S5_STATE), da[:, 0, hw:].reshape(ng, S5_STATE))


def _tie(a, tok):
    return a if tok is None else lax.optimization_barrier((a, tok))[0]


def _local_step(x, tgt, w, late=None, ready=None):
    ready = ready or (lambda names, g: None)
    bsz, seq, d = x.shape
    lp = PAD + N_META + seq
    r = bsz * lp
    meta = jnp.broadcast_to(w["meta_tokens"][None], (bsz, N_META, d))
    hin = jnp.concatenate([jnp.zeros((bsz, PAD, d), F32), meta, x], axis=1).reshape(r, d)
    tgtp = jnp.concatenate([jnp.zeros((bsz, PAD + N_META, d), F32), tgt], axis=1).reshape(r, d)

    h0 = _ln0_fwd(hin, w["ln0_g"], w["ln0_b"], lp)
    p = _inproj(h0, w["w_in"], w["b_in"], lp)
    p3 = p.reshape(bsz, lp, NP)

    b_re_t = jnp.transpose(w["s5_b_re"], (2, 0, 1))
    b_im_t = jnp.transpose(w["s5_b_im"], (2, 0, 1))
    ap_re, ap_im, bb_re_t, bb_im_t = _s5_prep(w["s5_lambda_re"], w["s5_lambda_im"], w["s5_log_dt"], b_re_t, b_im_t)
    bk, cre, cim, apow = _s5_block_mats(bb_re_t, bb_im_t, w["s5_c_re"], w["s5_c_im"], ap_re, ap_im)
    y_s5, xs = _s5_fwd(p3, bk, cre, cim, apow, w["s5_d"])
    sw = y_s5.shape[-1]
    if late is not None:
        w = {**w, **late(y_s5)}
    gy, z, ys5g = _glu_fwd(y_s5.reshape(r, sw), w["s5_w_glu"], lp)

    pre3, qk3 = _conv_fwd(p3, w["qk_conv_w"], w["qk_conv_b"])
    hm3, cs, ns, ms = _mlstm_fwd(qk3, p3)
    hm = hm3.reshape(r, d)
    ymin, ym, mix, r1, h1 = _mix_fwd(hm, p, ys5g, h0, w["m_norm_g"], w["m_w_out"], w["w_o"], w["ln1_g"], w["ln1_b"], lp)
    dr2, act, loss, dg2, db2 = _mlp_fwd(h1, tgtp, w["w_up"], w["w_down"], w["b_up"], w["ln2_g"], w["ln2_b"], lp)

    g = {"ln2_g": dg2, "ln2_b": db2}
    dh1, dup, g["b_up"] = _mlp_bwd(h1, dr2, w["w_up"], w["w_down"], w["b_up"], lp)
    g["w_down"] = _mm_tn(act, dr2, name="dw_down")
    g["w_up"] = _mm_tn(h1, dup, name="dw_up", split=w["w_up"].shape[0])
    tok = ready(("w_down", "w_up"), g)
    dr1, dp, dys5g, dym, dhm, g["ln1_g"], g["ln1_b"], g["m_norm_g"] = _mix_bwd(
        _tie(dh1, tok), r1, w["ln1_g"], w["w_o"], w["m_w_out"], p, ys5g, ym, hm, w["m_norm_g"], lp)
    g["w_o"] = _mm_tn(mix, dr1, name="dw_o")
    g["m_w_out"] = _mm_tn(ymin, dym, name="dw_mout")

    dp3 = dp.reshape(bsz, lp, NP)
    dp3, dqk3, dgate = _mlstm_bwd(dp3, qk3, p3, dhm.reshape(bsz, lp, d), cs, ns, ms)
    dp3, g["qk_conv_w"], g["qk_conv_b"] = _conv_bwd(dp3, p3, dqk3, pre3, w["qk_conv_w"])
    dz, dys5 = _glu_bwd(dys5g, z, y_s5.reshape(r, sw), w["s5_w_glu"], lp)
    g["s5_w_glu"] = _mm_tn(gy, dz, name="dw_glu", split=w["s5_w_glu"].shape[0])
    tok = ready(("s5_w_glu", "m_w_out", "w_o"), g)
    apow_rev = jnp.flip(apow, axis=1)
    dp3, dbk, dcre, dcim, da, g["s5_d"] = _s5_bwd(dp3, p3, _tie(dys5, tok).reshape(bsz, lp, sw), xs, bk, cre, cim, apow_rev, w["s5_d"])
    dbb_re_t, dbb_im_t, g["s5_c_re"], g["s5_c_im"], da_re, da_im = _s5_block_grads(dbk, dcre, dcim, da)
    g["s5_lambda_re"], g["s5_lambda_im"], g["s5_log_dt"], gb_re_t, gb_im_t = _s5_prep_bwd(
        w["s5_lambda_re"], w["s5_lambda_im"], w["s5_log_dt"], b_re_t, b_im_t, da_re, da_im, dbb_re_t, dbb_im_t)
    g["s5_b_re"] = jnp.transpose(gb_re_t, (1, 2, 0))
    g["s5_b_im"] = jnp.transpose(gb_im_t, (1, 2, 0))

    dp3 = lax.dynamic_update_slice(dp3, dgate, (0, 0, G_OFF))
    dp = dp3.reshape(r, NP)
    g["w_in"], g["b_in"] = _mm_tn(h0, dp, name="dw_in", colsum=True)
    tok = ready(("w_in",), g)
    dpw = _mm_nt(_tie(dp, tok), w["w_in"], lp, name="dh0")
    dhin, g["ln0_g"], g["ln0_b"], g["meta_tokens"] = _ln0_bwd(hin, dr1, dpw, w["ln0_g"], lp)
    grad_x = dhin.reshape(bsz, lp, d)[:, PAD + N_META:]
    return loss, grad_x, g


_ANY = pl.BlockSpec(memory_space=pl.ANY)
_MESH = pl.DeviceIdType.MESH


def _place():
    return lax.axis_index("x"), lax.axis_index("y"), lax.axis_index("c")


def _gather_chips(shards):
    n = len(shards)

    def body(*refs):
        ins, outs = refs[:n], refs[n:2 * n]
        send, recv, loc = refs[2 * n:]
        x, y, c = _place()
        me = 2 * x + y
        peers = [(1 - x, y), (x, 1 - y), (1 - x, 1 - y)]

        def rc(a, k, slot):
            px, py = peers[k]
            return pltpu.make_async_remote_copy(src_ref=ins[a], dst_ref=outs[a].at[slot], send_sem=send.at[a, k],
                                                recv_sem=recv.at[a, k], device_id=(px, py, c), device_id_type=_MESH)

        own = [pltpu.make_async_copy(ins[a], outs[a].at[me], loc.at[a]) for a in range(n)]
        for cp in own:
            cp.start()
        out = [rc(a, k, me) for a in range(n) for k in range(3)]
        for cp in out:
            cp.start()
        for a in range(n):
            for k in range(3):
                rc(a, k, 2 * peers[k][0] + peers[k][1]).wait_recv()
        for cp in out:
            cp.wait_send()
        for cp in own:
            cp.wait()

    return _pcall(
        body, name="gather_chips", in_specs=[_ANY] * n, out_specs=[_ANY] * n,
        out_shape=[jax.ShapeDtypeStruct((4,) + s.shape, s.dtype) for s in shards],
        scratch_shapes=[pltpu.SemaphoreType.DMA((n, 3)), pltpu.SemaphoreType.DMA((n, 3)), pltpu.SemaphoreType.DMA((n,))],
    )(*shards)


def _scatter_chips(parts):
    n = len(parts)

    def body(*refs):
        ins, outs = refs[:n], refs[n:2 * n]
        send, recv, loc = refs[2 * n:]
        x, y, c = _place()
        me = 2 * x + y
        peers = [(1 - x, y), (x, 1 - y), (1 - x, 1 - y)]
        pchip = [2 * px + py for px, py in peers]

        def rc(a, k, src_slot, dst_slot):
            px, py = peers[k]
            return pltpu.make_async_remote_copy(src_ref=ins[a].at[src_slot], dst_ref=outs[a].at[dst_slot],
                                                send_sem=send.at[a, k], recv_sem=recv.at[a, k],
                                                device_id=(px, py, c), device_id_type=_MESH)

        own = [pltpu.make_async_copy(ins[a].at[me], outs[a].at[me], loc.at[a]) for a in range(n)]
        for cp in own:
            cp.start()
        out = [rc(a, k, pchip[k], me) for a in range(n) for k in range(3)]
        for cp in out:
            cp.start()
        for a in range(n):
            for k in range(3):
                rc(a, k, me, pchip[k]).wait_recv()
        for cp in out:
            cp.wait_send()
        for cp in own:
            cp.wait()

    return _pcall(
        body, name="scatter_chips", in_specs=[_ANY] * n, out_specs=[_ANY] * n,
        out_shape=[jax.ShapeDtypeStruct(s.shape, s.dtype) for s in parts],
        scratch_shapes=[pltpu.SemaphoreType.DMA((n, 3)), pltpu.SemaphoreType.DMA((n, 3)), pltpu.SemaphoreType.DMA((n,))],
    )(*parts)


_HBM = pl.BlockSpec(memory_space=pltpu.HBM)
_SEM = pl.BlockSpec(memory_space=pltpu.SEMAPHORE)
_EFFECT = pltpu.SideEffectType.DATAFLOW_SIDE_EFFECTING


def _xchg_copies(srcs, lands, send, recv, scatter):
    x, y, c = _place()
    me = 2 * x + y
    peers = [(1 - x, y), (x, 1 - y), (1 - x, 1 - y)]
    out = []
    for a in range(len(srcs)):
        for k, (px, py) in enumerate(peers):
            src = srcs[a].at[2 * px + py] if scatter else srcs[a]
            dst = lands[a].at[k] if scatter else lands[a].at[me]
            out.append(pltpu.make_async_remote_copy(src_ref=src, dst_ref=dst, send_sem=send.at[3 * a + k],
                                                    recv_sem=recv.at[3 * a + k], device_id=(px, py, c),
                                                    device_id_type=_MESH))
    return out


def _xchg_start(srcs, lands, *, name, scatter):
    n = len(srcs)

    def body(*refs):
        send, recv = refs[2 * n], refs[2 * n + 1]
        for cp in _xchg_copies(refs[:n], refs[n:2 * n], send, recv, scatter):
            cp.start()
        refs[-1][...] = jnp.zeros_like(refs[-1])

    hbm = lambda a: pltpu.HBM(a.shape, a.dtype)
    con = lambda a: pltpu.with_memory_space_constraint(a, pltpu.HBM)
    res = _pcall(
        body, name=name, in_specs=[_HBM] * (2 * n),
        out_specs=[_SEM, _SEM] + [_HBM] * (2 * n) + [pl.BlockSpec(memory_space=pltpu.VMEM)],
        out_shape=[pltpu.SemaphoreType.DMA((3 * n,)), pltpu.SemaphoreType.DMA((3 * n,))]
        + [hbm(a) for a in srcs] + [hbm(a) for a in lands] + [jax.ShapeDtypeStruct((8, 128), F32)],
        input_output_aliases={i: 2 + i for i in range(2 * n)},
        compiler_params=pltpu.CompilerParams(has_side_effects=_EFFECT),
    )(*[con(a) for a in srcs], *[con(a) for a in lands])
    return res[0], res[1], list(res[2:2 + n]), list(res[2 + n:2 + 2 * n]), res[-1]


def _xchg_wait(send, recv, srcs, lands, after, *, name, scatter):
    n = len(srcs)

    def body(*refs):
        s_ref, r_ref = refs[2 * n], refs[2 * n + 1]
        for cp in _xchg_copies(refs[:n], refs[n:2 * n], s_ref, r_ref, scatter):
            cp.wait_send()
            cp.wait_recv()

    hbm = lambda a: pltpu.HBM(a.shape, a.dtype)
    res = _pcall(
        body, name=name, in_specs=[_HBM] * (2 * n) + [_SEM, _SEM, _ANY],
        out_specs=[_HBM] * (2 * n),
        out_shape=[hbm(a) for a in srcs] + [hbm(a) for a in lands],
        input_output_aliases={i: i for i in range(2 * n)},
        compiler_params=pltpu.CompilerParams(has_side_effects=_EFFECT),
    )(*srcs, *lands, send, recv, after)
    return list(res[:n]), list(res[n:])


def _swap_cores(arrs):
    n = len(arrs)

    def body(*refs):
        ins, outs = refs[:n], refs[n:2 * n]
        send, recv = refs[2 * n:]
        x, y, c = _place()
        cps = [pltpu.make_async_remote_copy(src_ref=ins[a], dst_ref=outs[a], send_sem=send.at[a], recv_sem=recv.at[a],
                                            device_id=(x, y, 1 - c), device_id_type=_MESH) for a in range(n)]
        for cp in cps:
            cp.start()
        for cp in cps:
            cp.wait_recv()
        for cp in cps:
            cp.wait_send()

    return _pcall(
        body, name="swap_cores", in_specs=[_ANY] * n, out_specs=[_ANY] * n,
        out_shape=[jax.ShapeDtypeStruct(s.shape, s.dtype) for s in arrs],
        scratch_shapes=[pltpu.SemaphoreType.DMA((n,)), pltpu.SemaphoreType.DMA((n,))],
    )(*arrs)


def _allreduce_small(v):
    rows = v.shape[0]

    def body(v_ref, sum_ref, all_ref, send, recv):
        x, y, c = _place()
        me = 4 * x + 2 * y + c
        flips = [(k >> 2 & 1, k >> 1 & 1, k & 1) for k in range(1, 8)]

        def peer(f):
            return tuple(1 - q if b else q for q, b in zip((x, y, c), f))

        all_ref[me] = v_ref[...]
        cps = [pltpu.make_async_remote_copy(src_ref=v_ref, dst_ref=all_ref.at[me], send_sem=send.at[k], recv_sem=recv.at[k],
                                            device_id=peer(f), device_id_type=_MESH) for k, f in enumerate(flips)]
        for cp in cps:
            cp.start()
        for cp in cps:
            cp.wait_recv()
        acc = all_ref[0]
        for j in range(1, 8):
            acc = acc + all_ref[j]
        sum_ref[...] = acc
        for cp in cps:
            cp.wait_send()

    vm = pl.BlockSpec(memory_space=pltpu.VMEM)
    return _pcall(
        body, name="allreduce_small", in_specs=[vm], out_specs=vm,
        out_shape=jax.ShapeDtypeStruct((rows, 128), F32),
        scratch_shapes=[pltpu.VMEM((8, rows, 128), F32), pltpu.SemaphoreType.DMA((7,)), pltpu.SemaphoreType.DMA((7,))],
        compiler_params=_cp(None, 40),
    )(v)


def _sum_slots(own, land):
    ns, rows, cols = land.shape
    tm = _row_tile(rows, 256, 8)

    def body(own_ref, a_ref, o_ref):
        o_ref[...] = ((own_ref[...] + a_ref[0]) + a_ref[1]) + a_ref[2]

    return _pcall(
        body, name="sum_slots", grid=(rows // tm,),
        in_specs=[pl.BlockSpec((tm, cols), lambda i: (i, 0)), pl.BlockSpec((ns, tm, cols), lambda i: (0, i, 0))],
        out_specs=pl.BlockSpec((tm, cols), lambda i: (i, 0)),
        out_shape=jax.ShapeDtypeStruct((rows, cols), F32),
        compiler_params=_cp(("parallel",), 40),
    )(own, land)


def _adamw(w, m, v, g0, g1=None):
    rows, cols = w.shape[-2:]
    lead = w.ndim == 3
    tm = _row_tile(rows, 256, 8)
    c1 = 1.0 - ADAM_B1 ** ADAM_STEP
    c2 = 1.0 - ADAM_B2 ** ADAM_STEP
    two = g1 is not None

    def body(*refs):
        w_ref, m_ref, v_ref, g0_ref = refs[:4]
        g_ref, d_ref, nm_ref, nv_ref = refs[-4:]
        g = g0_ref[...]
        if two:
            g = g + refs[4][...]
        nm = ADAM_B1 * m_ref[...] + (1.0 - ADAM_B1) * g
        nv = ADAM_B2 * v_ref[...] + (1.0 - ADAM_B2) * (g * g)
        g_ref[...] = g
        nm_ref[...] = nm
        nv_ref[...] = nv
        d_ref[...] = -ADAM_LR * ((nm / c1) / (jnp.sqrt(nv / c2) + ADAM_EPS) + ADAM_WD * w_ref[...])

    blk = pl.BlockSpec((tm, cols), lambda i: (i, 0))
    wblk = pl.BlockSpec((None, tm, cols), lambda i: (0, i, 0)) if lead else blk
    ins = [w, m, v, g0] + ([g1] if two else [])
    return _pcall(
        body, name="adamw", grid=(rows // tm,), in_specs=[wblk] * 3 + [blk] * (len(ins) - 3), out_specs=[wblk] * 4,
        out_shape=[jax.ShapeDtypeStruct(w.shape, F32)] * 4,
        compiler_params=_cp(("parallel",), 40),
    )(*ins)


_BIG = ("w_in", "s5_w_glu", "m_w_out", "w_o", "w_up", "w_down")
_SMALL = ("ln0_g", "ln0_b", "b_in", "qk_conv_b", "s5_lambda_re", "s5_lambda_im", "s5_log_dt", "s5_b_re", "s5_b_im",
          "s5_c_re", "s5_c_im", "s5_d", "m_norm_g", "ln1_g", "ln1_b", "b_up", "ln2_g", "ln2_b")
_SMALL_SHARDED = ("meta_tokens", "qk_conv_w")
_ORDER = ("meta_tokens", "ln0_g", "ln0_b", "w_in", "b_in", "qk_conv_w", "qk_conv_b", "s5_lambda_re", "s5_lambda_im",
          "s5_log_dt", "s5_b_re", "s5_b_im", "s5_c_re", "s5_c_im", "s5_d", "s5_w_glu", "m_norm_g", "m_w_out", "w_o",
          "ln1_g", "ln1_b", "w_up", "b_up", "w_down", "ln2_g", "ln2_b")


def _pack(arrs):
    flat = jnp.concatenate([a.reshape(-1) for a in arrs])
    n = flat.shape[0]
    rows = -(-n // 1024) * 8
    return jnp.pad(flat, (0, rows * 128 - n)).reshape(rows, 128)


def _unpack(packed, shapes):
    flat = packed.reshape(-1)
    out, off = [], 0
    for s in shapes:
        n = math.prod(s)
        out.append(flat[off:off + n].reshape(s))
        off += n
    return out


def kernel(x, meta_tokens, ln0_g, ln0_b, w_in, b_in, qk_conv_w, qk_conv_b, s5_lambda_re, s5_lambda_im, s5_log_dt, s5_b_re, s5_b_im, s5_c_re, s5_c_im, s5_d, s5_w_glu, m_norm_g, m_w_out, w_o, ln1_g, ln1_b, w_up, b_up, w_down, ln2_g, ln2_b, loss_target, m_meta_tokens, m_ln0_g, m_ln0_b, m_w_in, m_b_in, m_qk_conv_w, m_qk_conv_b, m_s5_lambda_re, m_s5_lambda_im, m_s5_log_dt, m_s5_b_re, m_s5_b_im, m_s5_c_re, m_s5_c_im, m_s5_d, m_s5_w_glu, m_m_norm_g, m_m_w_out, m_w_o, m_ln1_g, m_ln1_b, m_w_up, m_b_up, m_w_down, m_ln2_g, m_ln2_b, v_meta_tokens, v_ln0_g, v_ln0_b, v_w_in, v_b_in, v_qk_conv_w, v_qk_conv_b, v_s5_lambda_re, v_s5_lambda_im, v_s5_log_dt, v_s5_b_re, v_s5_b_im, v_s5_c_re, v_s5_c_im, v_s5_d, v_s5_w_glu, v_m_norm_g, v_m_w_out, v_w_o, v_ln1_g, v_ln1_b, v_w_up, v_b_up, v_w_down, v_ln2_g, v_ln2_b):
    wts = dict(meta_tokens=meta_tokens, ln0_g=ln0_g, ln0_b=ln0_b, w_in=w_in, b_in=b_in, qk_conv_w=qk_conv_w,
               qk_conv_b=qk_conv_b, s5_lambda_re=s5_lambda_re, s5_lambda_im=s5_lambda_im, s5_log_dt=s5_log_dt,
               s5_b_re=s5_b_re, s5_b_im=s5_b_im, s5_c_re=s5_c_re, s5_c_im=s5_c_im, s5_d=s5_d, s5_w_glu=s5_w_glu,
               m_norm_g=m_norm_g, m_w_out=m_w_out, w_o=w_o, ln1_g=ln1_g, ln1_b=ln1_b, w_up=w_up, b_up=b_up,
               w_down=w_down, ln2_g=ln2_g, ln2_b=ln2_b)
    mom = dict(meta_tokens=m_meta_tokens, ln0_g=m_ln0_g, ln0_b=m_ln0_b, w_in=m_w_in, b_in=m_b_in, qk_conv_w=m_qk_conv_w,
               qk_conv_b=m_qk_conv_b, s5_lambda_re=m_s5_lambda_re, s5_lambda_im=m_s5_lambda_im, s5_log_dt=m_s5_log_dt,
               s5_b_re=m_s5_b_re, s5_b_im=m_s5_b_im, s5_c_re=m_s5_c_re, s5_c_im=m_s5_c_im, s5_d=m_s5_d,
               s5_w_glu=m_s5_w_glu, m_norm_g=m_m_norm_g, m_w_out=m_m_w_out, w_o=m_w_o, ln1_g=m_ln1_g, ln1_b=m_ln1_b,
               w_up=m_w_up, b_up=m_b_up, w_down=m_w_down, ln2_g=m_ln2_g, ln2_b=m_ln2_b)
    var = dict(meta_tokens=v_meta_tokens, ln0_g=v_ln0_g, ln0_b=v_ln0_b, w_in=v_w_in, b_in=v_b_in, qk_conv_w=v_qk_conv_w,
               qk_conv_b=v_qk_conv_b, s5_lambda_re=v_s5_lambda_re, s5_lambda_im=v_s5_lambda_im, s5_log_dt=v_s5_log_dt,
               s5_b_re=v_s5_b_re, s5_b_im=v_s5_b_im, s5_c_re=v_s5_c_re, s5_c_im=v_s5_c_im, s5_d=v_s5_d,
               s5_w_glu=v_s5_w_glu, m_norm_g=v_m_norm_g, m_w_out=v_m_w_out, w_o=v_w_o, ln1_g=v_ln1_g, ln1_b=v_ln1_b,
               w_up=v_w_up, b_up=v_b_up, w_down=v_w_down, ln2_g=v_ln2_g, ln2_b=v_ln2_b)
    d = x.shape[-1]
    chip = 2 * lax.axis_index("x") + lax.axis_index("y")

    first = ("w_in",) + _SMALL_SHARDED
    gw = dict(zip(first, _gather_chips([_bf(w_in[0]), meta_tokens, qk_conv_w[0]])))
    late_names = tuple(n for n in _BIG if n != "w_in")
    late_src = [_bf(wts[n][0]) for n in late_names]
    lsend, lrecv, late_src, late_land, ltok = _xchg_start(
        late_src, [lax.empty((4,) + a.shape, a.dtype) for a in late_src], name="gather_late_start", scatter=False)
    cat = lambda a: jnp.transpose(a, (1, 0, 2)).reshape(a.shape[1], 4 * a.shape[2])
    w = dict(
        meta_tokens=cat(gw["meta_tokens"]), ln0_g=_tie(ln0_g[None], ltok), ln0_b=ln0_b[None],
        w_in=_w_in_from_slots(gw["w_in"]), b_in=_to_pad_cols(b_in),
        qk_conv_w=cat(gw["qk_conv_w"]), qk_conv_b=qk_conv_b,
        s5_lambda_re=s5_lambda_re[0], s5_lambda_im=s5_lambda_im[0], s5_log_dt=s5_log_dt[0][:, None],
        s5_b_re=s5_b_re[0], s5_b_im=s5_b_im[0], s5_c_re=s5_c_re[0], s5_c_im=s5_c_im[0], s5_d=s5_d,
        m_norm_g=m_norm_g, ln1_g=ln1_g, ln1_b=ln1_b, b_up=b_up, ln2_g=ln2_g, ln2_b=ln2_b)

    def late(after):
        src, land = _xchg_wait(lsend, lrecv, late_src, late_land, after, name="gather_late_wait", scatter=False)
        full = {n: lax.dynamic_update_slice(ld, s[None], (chip,) + (0,) * s.ndim)
                for n, s, ld in zip(late_names, src, land)}
        return dict(s5_w_glu=full["s5_w_glu"], m_w_out=full["m_w_out"].reshape(d, d), w_o=full["w_o"].reshape(d, d),
                    w_up=full["w_up"], w_down=full["w_down"].reshape(4 * d, d))

    flying = []

    def ready(names, g):
        parts = dict(
            w_in=lambda: _slots_from_w_in(g["w_in"][0]), s5_w_glu=lambda: g["s5_w_glu"],
            m_w_out=lambda: g["m_w_out"].reshape(4, d // 4, d), w_o=lambda: g["w_o"].reshape(4, d // 4, d),
            w_up=lambda: g["w_up"], w_down=lambda: g["w_down"].reshape(4, d, d))
        src = [parts[n]() for n in names]
        land = [lax.empty((3,) + a.shape[1:], a.dtype) for a in src]
        st = _xchg_start(src, land, name="scatter_" + names[0] + "_start", scatter=True)
        flying.append((names,) + st[:4])
        return st[4]

    loss, grad_x, g = _local_step(x, loss_target, w, late, ready)
    g["b_in"] = _from_pad_cols(g["b_in"])

    mine = {}
    for names, send, recv, src, land in flying:
        src, land = _xchg_wait(send, recv, src, land, g["ln0_g"], name="scatter_" + names[0] + "_wait", scatter=True)
        for n, s, ld in zip(names, src, land):
            own = lax.dynamic_index_in_dim(s, chip, 0, keepdims=False)
            mine[n] = _sum_slots(own, ld)
    mine = [mine[n] for n in _BIG]
    theirs = _swap_cores(mine)

    small_shapes = [(1, 128)] + [wts[n].shape for n in _SMALL] + [g[n].shape for n in _SMALL_SHARDED]
    packed = _pack([loss] + [g[n] for n in _SMALL] + [g[n] for n in _SMALL_SHARDED])
    tot = _unpack(_allreduce_small(packed), small_shapes)
    loss_out = tot[0][0, 0]
    gsm = dict(zip(_SMALL + _SMALL_SHARDED, tot[1:]))
    for n in _SMALL_SHARDED:
        cols = wts[n].shape[-1]
        gsm[n] = lax.dynamic_slice_in_dim(gsm[n], chip * cols, cols, axis=1).reshape(wts[n].shape)

    res = {}
    for i, n in enumerate(_BIG):
        res[n] = _adamw(wts[n], mom[n], var[n], mine[i], theirs[i])
    names = _SMALL + _SMALL_SHARDED
    shapes = [wts[n].shape for n in names]
    pk = lambda dct: _pack([dct[n] for n in names])
    small_res = [_unpack(r, shapes) for r in _adamw(pk(wts), pk(mom), pk(var), pk(gsm))]
    for j, n in enumerate(names):
        res[n] = [small_res[q][j] for q in range(4)]

    return (loss_out, grad_x, *[res[n][0] for n in _ORDER], *[res[n][1] for n in _ORDER],
            *[res[n][2] for n in _ORDER], *[res[n][3] for n in _ORDER])
```

```python
import functools
import math

import jax
import jax.numpy as jnp
from jax import lax
from jax.experimental import pallas as pl
from jax.experimental.pallas import tpu as pltpu

F32 = jnp.float32
BF16 = jnp.bfloat16
HI = lax.Precision.HIGHEST

N_META = 16
M_HEADS = 4
M_CHUNK = 64
PAD = M_CHUNK - N_META
CONV_W = 4
S5_GROUP = 16
S5_STATE = 64
S5_KCH = 4
LN_EPS = 1e-5
ALPHA = 2.0 ** 0.25
NEG = -1e30
ADAM_LR, ADAM_B1, ADAM_B2, ADAM_EPS, ADAM_WD, ADAM_STEP = 0.001, 0.9, 0.999, 1e-08, 0.01, 10

O_OFF, GS_OFF, GM_OFF, V_OFF, Q_OFF, K_OFF, U_OFF, G_OFF, NP = 0, 1024, 2048, 3072, 4096, 4608, 5120, 5632, 5760

NN = ((1,), (0,))
NT = ((1,), (1,))
TN = ((0,), (0,))


def _dot(a, b, dims=NN, prec=None):
    return lax.dot_general(a, b, (dims, ((), ())), preferred_element_type=F32, precision=prec)


def _bf(x):
    return x.astype(BF16)


def _sig(x):
    return 1.0 / (1.0 + jnp.exp(-x))


def _pcall(body, **kw):
    return pl.pallas_call(body, **kw)


def _cp(sem=None, vmem_mb=None):
    kw = {}
    if sem is not None:
        kw["dimension_semantics"] = sem
    if vmem_mb is not None:
        kw["vmem_limit_bytes"] = vmem_mb << 20
    return pltpu.CompilerParams(**kw)


def _row_tile(n, want, mult=16):
    best = None
    for t in range(mult, want + 1, mult):
        if n % t == 0:
            best = t
    assert best is not None, (n, want)
    return best


def _resident(shape):
    nd = len(shape)
    return pl.BlockSpec(shape, lambda *_: (0,) * nd, pipeline_mode=pl.Buffered(1))


def _const(shape):
    nd = len(shape)
    return pl.BlockSpec(shape, lambda *_: (0,) * nd)


def _ln_fwd(x, g, b):
    mu = jnp.mean(x, axis=-1, keepdims=True)
    xc = x - mu
    var = jnp.mean(xc * xc, axis=-1, keepdims=True)
    rstd = lax.rsqrt(var + LN_EPS)
    xhat = xc * rstd
    return xhat * g + b, xhat, rstd


def _ln_bwd(dy, xhat, rstd, g):
    dxh = dy * g
    m1 = jnp.mean(dxh, axis=-1, keepdims=True)
    m2 = jnp.mean(dxh * xhat, axis=-1, keepdims=True)
    return rstd * (dxh - m1 - xhat * m2)


def _colsum(x):
    return jnp.sum(x, axis=0, keepdims=True)


def _to_pad_cols(w):
    u, q, k, v, o, gi, gf, gs, gm = (w[..., 0:512], w[..., 512:1024], w[..., 1024:1536], w[..., 1536:2560],
                                     w[..., 2560:3584], w[..., 3584:3588], w[..., 3588:3592], w[..., 3592:4616],
                                     w[..., 4616:5640])
    z = jnp.zeros(w.shape[:-1] + (NP - G_OFF - 8,), w.dtype)
    return jnp.concatenate([o, gs, gm, v, q, k, u, gi, gf, z], axis=-1)


def _from_pad_cols(w):
    o, gs, gm, v, q, k, u = (w[..., O_OFF:GS_OFF], w[..., GS_OFF:GM_OFF], w[..., GM_OFF:V_OFF], w[..., V_OFF:Q_OFF],
                             w[..., Q_OFF:K_OFF], w[..., K_OFF:U_OFF], w[..., U_OFF:G_OFF])
    gi, gf = w[..., G_OFF:G_OFF + 4], w[..., G_OFF + 4:G_OFF + 8]
    return jnp.concatenate([u, q, k, v, o, gi, gf, gs, gm], axis=-1)


_IN_REF = (("u", 512), ("q", 512), ("k", 512), ("v", 1024), ("o", 1024), ("i", 4), ("f", 4), ("gs", 1024), ("gm", 1024))
_IN_PAD = (("o", O_OFF), ("gs", GS_OFF), ("gm", GM_OFF), ("v", V_OFF), ("q", Q_OFF), ("k", K_OFF), ("u", U_OFF),
           ("i", G_OFF), ("f", G_OFF + 4))


def _in_ref_ranges():
    out, off = {}, 0
    for n, s in _IN_REF:
        out[n] = (off, off + s)
        off += s
    return out, off


def _w_in_from_slots(g):
    rng, total = _in_ref_ranges()
    width = total // g.shape[0]
    cols = []
    for n, _ in _IN_PAD:
        a, b = rng[n]
        while a < b:
            s = a // width
            e = min(b, (s + 1) * width)
            cols.append(g[s][:, a - s * width:e - s * width])
            a = e
    cols.append(jnp.zeros((g.shape[1], NP - G_OFF - 8), g.dtype))
    return jnp.concatenate(cols, axis=1)


def _slots_from_w_in(wp, nslot=4):
    rng, total = _in_ref_ranges()
    width = total // nslot
    pad_off = dict(_IN_PAD)
    slots = []
    for s in range(nslot):
        lo, hi = s * width, (s + 1) * width
        cols = []
        for n, _ in _IN_REF:
            a, b = rng[n]
            x0, x1 = max(a, lo), min(b, hi)
            if x0 < x1:
                cols.append(wp[:, pad_off[n] + x0 - a:pad_off[n] + x1 - a])
        slots.append(jnp.concatenate(cols, axis=1))
    return jnp.stack(slots, axis=0)


def _ln0_fwd(hin, g, b, lp):
    r, d = hin.shape
    tm = _row_tile(lp, 416)

    def body(x_ref, g_ref, b_ref, o_ref):
        y, _, _ = _ln_fwd(x_ref[...], g_ref[...], b_ref[...])
        o_ref[...] = y

    return _pcall(
        body, name="ln0_fwd", grid=(r // tm,),
        in_specs=[pl.BlockSpec((tm, d), lambda i: (i, 0)), _const((1, d)), _const((1, d))],
        out_specs=pl.BlockSpec((tm, d), lambda i: (i, 0)),
        out_shape=jax.ShapeDtypeStruct((r, d), F32),
        compiler_params=_cp(("parallel",)),
    )(hin, g, b)


def _ln0_bwd(hin, dr1, dpw, g, lp):
    r, d = hin.shape
    tm = _row_tile(lp, 416)
    tps = lp // tm
    assert tm >= PAD + N_META

    def body(x_ref, a_ref, c_ref, g_ref, o_ref, dg_ref, db_ref, dm_ref):
        i = pl.program_id(0)

        @pl.when(i == 0)
        def _():
            dg_ref[...] = jnp.zeros_like(dg_ref)
            db_ref[...] = jnp.zeros_like(db_ref)
            dm_ref[...] = jnp.zeros_like(dm_ref)

        dy = ALPHA * a_ref[...] + c_ref[...]
        _, xhat, rstd = _ln_fwd(x_ref[...], g_ref[...], 0.0)
        dx = _ln_bwd(dy, xhat, rstd, g_ref[...])
        o_ref[...] = dx
        dg_ref[...] += _colsum(dy * xhat)
        db_ref[...] += _colsum(dy)

        @pl.when(i % tps == 0)
        def _():
            dm_ref[...] += dx[PAD:PAD + N_META, :]

    return _pcall(
        body, name="ln0_bwd", grid=(r // tm,),
        in_specs=[pl.BlockSpec((tm, d), lambda i: (i, 0))] * 3 + [_const((1, d))],
        out_specs=[pl.BlockSpec((tm, d), lambda i: (i, 0)), _const((1, d)), _const((1, d)), _const((N_META, d))],
        out_shape=[jax.ShapeDtypeStruct((r, d), F32), jax.ShapeDtypeStruct((1, d), F32),
                   jax.ShapeDtypeStruct((1, d), F32), jax.ShapeDtypeStruct((N_META, d), F32)],
        compiler_params=_cp(("arbitrary",)),
    )(hin, dr1, dpw, g)


def _inproj(h0, w_bf, bias, lp):
    r, d = h0.shape
    n = w_bf.shape[1]
    tm = _row_tile(lp, 832)
    tn = 1152
    tps = lp // tm

    def body(a_ref, w_ref, b_ref, o_ref):
        i = pl.program_id(0)
        acc = _dot(_bf(a_ref[...]), w_ref[...]) + b_ref[...]
        t = (i % tps) * tm + lax.broadcasted_iota(jnp.int32, (tm, 1), 0)
        o_ref[...] = jnp.where(t >= PAD, acc, 0.0)

    return _pcall(
        body, name="inproj", grid=(r // tm, n // tn),
        in_specs=[pl.BlockSpec((tm, d), lambda i, j: (i, 0)), pl.BlockSpec((d, tn), lambda i, j: (0, j)),
                  pl.BlockSpec((1, tn), lambda i, j: (0, j))],
        out_specs=pl.BlockSpec((tm, tn), lambda i, j: (i, j)),
        out_shape=jax.ShapeDtypeStruct((r, n), F32),
        compiler_params=_cp(("parallel", "parallel"), 48),
    )(h0, w_bf, bias)


def _mm_tn(a, b, *, name, split=1, colsum=False, tk_want=832):
    r, m = a.shape
    n = b.shape[1]
    tk = _row_tile(r, tk_want)
    tm = min(m, 1024)
    ns = n // split
    tn = ns
    for cand in (1024, 1152, 640, 512, 128):
        if ns % cand == 0 and cand <= ns:
            tn = cand
            break
    nb = ns // tn
    nk = r // tk

    def body(a_ref, b_ref, o_ref, *rest):
        acc = rest[-1]
        k = pl.program_id(2)

        @pl.when(k == 0)
        def _():
            acc[...] = jnp.zeros_like(acc)

        bt = b_ref[...]
        acc[...] += _dot(_bf(a_ref[...]), _bf(bt), TN)

        @pl.when(k == nk - 1)
        def _():
            o_ref[...] = acc[...]

        if colsum:
            cs_ref = rest[0]

            @pl.when(k == 0)
            def _():
                cs_ref[...] = jnp.zeros_like(cs_ref)

            cs_ref[...] += _colsum(bt.astype(F32))

    out_specs = [pl.BlockSpec((None, tm, tn), lambda i, j, k: (j // nb, i, j % nb))]
    out_shape = [jax.ShapeDtypeStruct((split, m, ns), F32)]
    if colsum:
        assert m == tm
        out_specs.append(pl.BlockSpec((1, tn), lambda i, j, k: (0, j)))
        out_shape.append(jax.ShapeDtypeStruct((1, n), F32))
    res = _pcall(
        body, name=name, grid=(m // tm, n // tn, nk),
        in_specs=[pl.BlockSpec((tk, tm), lambda i, j, k: (k, i)), pl.BlockSpec((tk, tn), lambda i, j, k: (k, j))],
        out_specs=out_specs, out_shape=out_shape,
        scratch_shapes=[pltpu.VMEM((tm, tn), F32)],
        compiler_params=_cp(("parallel", "parallel", "arbitrary"), 48),
    )(a, b)
    return res if colsum else res[0]


def _mm_nt(a, w_bf, lp, *, name, dep=None):
    r, kdim = a.shape
    n = w_bf.shape[0]
    tm = _row_tile(lp, 832)
    tk = 1152
    nk = kdim // tk
    deps = [] if dep is None else [dep]

    def body(a_ref, w_ref, *rest):
        o_ref, acc = rest[-2:]
        k = pl.program_id(1)

        @pl.when(k == 0)
        def _():
            acc[...] = jnp.zeros_like(acc)

        acc[...] += _dot(_bf(a_ref[...]), w_ref[...], NT)

        @pl.when(k == nk - 1)
        def _():
            o_ref[...] = acc[...]

    return _pcall(
        body, name=name, grid=(r // tm, nk),
        in_specs=[pl.BlockSpec((tm, tk), lambda i, k: (i, k)), pl.BlockSpec((n, tk), lambda i, k: (0, k))]
        + [_const(dp_.shape) for dp_ in deps],
        out_specs=pl.BlockSpec((tm, n), lambda i, k: (i, 0)),
        out_shape=jax.ShapeDtypeStruct((r, n), F32),
        scratch_shapes=[pltpu.VMEM((tm, n), F32)],
        compiler_params=_cp(("parallel", "arbitrary"), 48),
    )(a, w_bf, *deps)


def _s5_prep(lam_re, lam_im, log_dt, b_re_t, b_im_t):
    g, p = lam_re.shape
    h = b_re_t.shape[0]

    def body(lr_ref, li_ref, ldt_ref, br_ref, bi_ref, pr_ref, pi_ref, bbr_ref, bbi_ref):
        lr, li = lr_ref[...], li_ref[...]
        dt = jnp.exp(ldt_ref[...])
        e = jnp.exp(lr * dt)
        ar, ai = e * jnp.cos(li * dt), e * jnp.sin(li * dt)
        den = lr * lr + li * li
        cr = ((ar - 1.0) * lr + ai * li) / den
        ci = (ai * lr - (ar - 1.0) * li) / den
        br, bi = br_ref[...], bi_ref[...]
        bbr_ref[...] = cr[None] * br - ci[None] * bi
        bbi_ref[...] = cr[None] * bi + ci[None] * br
        xr, xi = ar, ai
        pr_ref[0] = xr
        pi_ref[0] = xi
        for t in range(1, 8):
            xr, xi = xr * ar - xi * ai, xr * ai + xi * ar
            pr_ref[t] = xr
            pi_ref[t] = xi

    sd = jax.ShapeDtypeStruct
    return _pcall(body, name="s5_prep",
                  out_shape=[sd((8, g, p), F32), sd((8, g, p), F32), sd((h, g, p), F32), sd((h, g, p), F32)])(
        lam_re, lam_im, log_dt, b_re_t, b_im_t)


def _s5_prep_bwd(lam_re, lam_im, log_dt, b_re_t, b_im_t, da_re, da_im, dbb_re_t, dbb_im_t):
    g, p = lam_re.shape
    h = b_re_t.shape[0]

    def body(lr_ref, li_ref, ldt_ref, br_ref, bi_ref, dar_ref, dai_ref, dbr_ref, dbi_ref,
             glr_ref, gli_ref, gdt_ref, gbr_ref, gbi_ref):
        lr, li = lr_ref[...], li_ref[...]
        dt = jnp.exp(ldt_ref[...])
        e = jnp.exp(lr * dt)
        ar, ai = e * jnp.cos(li * dt), e * jnp.sin(li * dt)
        den = lr * lr + li * li
        cr = ((ar - 1.0) * lr + ai * li) / den
        ci = (ai * lr - (ar - 1.0) * li) / den
        br, bi = br_ref[...], bi_ref[...]
        gr, gi = dbr_ref[...], dbi_ref[...]
        gbr_ref[...] = gr * cr[None] + gi * ci[None]
        gbi_ref[...] = gi * cr[None] - gr * ci[None]
        gcr = jnp.sum(gr * br + gi * bi, axis=0)
        gci = jnp.sum(gi * br - gr * bi, axis=0)
        ilr, ili = lr / den, -li / den
        gar = dar_ref[...] + gcr * ilr + gci * ili
        gai = dai_ref[...] + gci * ilr - gcr * ili
        qr, qi = cr * ilr - ci * ili, cr * ili + ci * ilr
        glr = -(gcr * qr + gci * qi)
        gli = -(gci * qr - gcr * qi)
        gzr = gar * ar + gai * ai
        gzi = gai * ar - gar * ai
        glr_ref[...] = glr + gzr * dt
        gli_ref[...] = gli + gzi * dt
        gdt_ref[...] = jnp.sum(gzr * lr + gzi * li, axis=1, keepdims=True) * dt

    sd = jax.ShapeDtypeStruct
    return _pcall(body, name="s5_prep_bwd",
                  out_shape=[sd((g, p), F32), sd((g, p), F32), sd((g, 1), F32), sd((h, g, p), F32), sd((h, g, p), F32)])(
        lam_re, lam_im, log_dt, b_re_t, b_im_t, da_re, da_im, dbb_re_t, dbb_im_t)


def _cmul(xr, xi, yr, yi):
    return xr * yr - xi * yi, xr * yi + xi * yr


def _dot5(a, b, dims=NN):
    return _dot(_bf(a), _bf(b), dims)


def _s5_fwd(p3, bk, cre, cim, apow, dskip):
    bsz, lp, _ = p3.shape
    tt = _row_tile(lp, 520, 8)
    nt = lp // tt
    nblk = tt // 8
    hw = 512

    def body(u_ref, bk_ref, cre_ref, cim_ref, ap_ref, d_ref, y_ref, xs_ref, car_ref):
        t = pl.program_id(2)

        @pl.when(t == 0)
        def _():
            car_ref[...] = jnp.zeros_like(car_ref)

        u = u_ref[...]
        xs_ref[...] = _dot5(u, bk_ref[...])
        ap = ap_ref[...]
        apr, api = ap[:, :hw], ap[:, hw:]
        rows = lax.broadcasted_iota(jnp.int32, (8, hw), 0)

        def blk(i, carry):
            cr, ci = carry
            off = pl.multiple_of(i * 8, 8)
            x = xs_ref[pl.ds(off, 8), :]
            xr, xi = x[:, :hw], x[:, hw:]
            for d in (1, 2, 4):
                sr = jnp.where(rows < d, 0.0, pltpu.roll(xr, d, 0))
                si = jnp.where(rows < d, 0.0, pltpu.roll(xi, d, 0))
                mr, mi = _cmul(sr, si, apr[d - 1:d, :], api[d - 1:d, :])
                xr, xi = xr + mr, xi + mi
            mr, mi = _cmul(apr, api, cr, ci)
            xr, xi = xr + mr, xi + mi
            xs_ref[pl.ds(off, 8), :] = jnp.concatenate([xr, xi], axis=1)
            return xr[7:8, :], xi[7:8, :]

        c0 = car_ref[...]
        cr, ci = lax.fori_loop(0, nblk, blk, (c0[0:1, :hw], c0[0:1, hw:]))
        car_ref[...] = jnp.broadcast_to(jnp.concatenate([cr, ci], axis=1), car_ref.shape)
        xs = xs_ref[...]
        y_ref[...] = (_dot5(xs[:, :hw], cre_ref[...]) - _dot5(xs[:, hw:], cim_ref[...])
                      + d_ref[...] * u)

    ub = U_OFF // 128
    return _pcall(
        body, name="s5_fwd", grid=(S5_KCH, bsz, nt),
        in_specs=[pl.BlockSpec((None, tt, 128), lambda k, b, t: (b, t, ub + k)),
                  pl.BlockSpec((None, 128, 2 * hw), lambda k, b, t: (k, 0, 0)),
                  pl.BlockSpec((None, hw, 128), lambda k, b, t: (k, 0, 0)),
                  pl.BlockSpec((None, hw, 128), lambda k, b, t: (k, 0, 0)),
                  pl.BlockSpec((None, 8, 2 * hw), lambda k, b, t: (k, 0, 0)),
                  pl.BlockSpec((1, 128), lambda k, b, t: (0, k))],
        out_specs=[pl.BlockSpec((None, tt, 128), lambda k, b, t: (b, t, k)),
                   pl.BlockSpec((None, None, tt, 2 * hw), lambda k, b, t: (b, k, t, 0))],
        out_shape=[jax.ShapeDtypeStruct((bsz, lp, S5_KCH * 128), F32),
                   jax.ShapeDtypeStruct((bsz, S5_KCH, lp, 2 * hw), F32)],
        scratch_shapes=[pltpu.VMEM((8, 2 * hw), F32)],
        compiler_params=_cp(("parallel", "parallel", "arbitrary"), 40),
    )(p3, bk, cre, cim, apow, dskip)


def _s5_bwd(dp3, p3, dy3, xs, bk, cre, cim, apow_rev, dskip):
    bsz, lp, _ = p3.shape
    tt = _row_tile(lp, 520, 8)
    nt = lp // tt
    nblk = tt // 8
    hw = 512
    tb = tt // 8

    def body(dp_any, u_ref, dy_ref, xs_ref, halo_ref, bk_ref, cre_ref, cim_ref, ap_ref, d_ref,
             du_ref, dbk_ref, dcre_ref, dcim_ref, da_ref, dd_ref, g_ref, ext_ref, car_ref):
        b = pl.program_id(1)
        t = pl.program_id(2)
        tidx = nt - 1 - t

        @pl.when(t == 0)
        def _():
            car_ref[...] = jnp.zeros_like(car_ref)

        @pl.when((b == 0) & (t == 0))
        def _():
            dbk_ref[...] = jnp.zeros_like(dbk_ref)
            dcre_ref[...] = jnp.zeros_like(dcre_ref)
            dcim_ref[...] = jnp.zeros_like(dcim_ref)
            da_ref[...] = jnp.zeros_like(da_ref)
            dd_ref[...] = jnp.zeros_like(dd_ref)

        u = u_ref[...]
        dy = dy_ref[...]
        g_ref[:, :hw] = _dot5(dy, cre_ref[...], NT)
        g_ref[:, hw:] = -_dot5(dy, cim_ref[...], NT)
        ap = ap_ref[...]
        apr, api = ap[:, :hw], -ap[:, hw:]
        rows = lax.broadcasted_iota(jnp.int32, (8, hw), 0)

        def blk(i, carry):
            cr, ci = carry
            off = pl.multiple_of((nblk - 1 - i) * 8, 8)
            x = g_ref[pl.ds(off, 8), :]
            xr, xi = x[:, :hw], x[:, hw:]
            for d in (1, 2, 4):
                sr = jnp.where(rows >= 8 - d, 0.0, pltpu.roll(xr, 8 - d, 0))
                si = jnp.where(rows >= 8 - d, 0.0, pltpu.roll(xi, 8 - d, 0))
                mr, mi = _cmul(sr, si, apr[8 - d:9 - d, :], api[8 - d:9 - d, :])
                xr, xi = xr + mr, xi + mi
            mr, mi = _cmul(apr, api, cr, ci)
            xr, xi = xr + mr, xi + mi
            g_ref[pl.ds(off, 8), :] = jnp.concatenate([xr, xi], axis=1)
            return xr[0:1, :], xi[0:1, :]

        c0 = car_ref[...]
        cr, ci = lax.fori_loop(0, nblk, blk, (c0[0:1, :hw], c0[0:1, hw:]))
        car_ref[...] = jnp.broadcast_to(jnp.concatenate([cr, ci], axis=1), car_ref.shape)

        gg = g_ref[...]
        du = _dot5(gg, bk_ref[...], NT) + d_ref[...] * dy
        trow = tidx * tt + lax.broadcasted_iota(jnp.int32, (tt, 1), 0)
        du_ref[...] = jnp.where(trow >= PAD, du, 0.0)
        dbk_ref[...] += _dot5(u, gg, TN)
        xsv = xs_ref[...]
        dcre_ref[...] += _dot5(xsv[:, :hw], dy, TN)
        dcim_ref[...] -= _dot5(xsv[:, hw:], dy, TN)
        dd_ref[...] += _colsum(dy * u)
        ext_ref[0:8, :] = jnp.where(tidx == 0, 0.0, halo_ref[...])
        ext_ref[8:, :] = xsv
        xp = ext_ref[pl.ds(7, tt), :]
        gr, gi, pr, pi = gg[:, :hw], gg[:, hw:], xp[:, :hw], xp[:, hw:]
        da_ref[:, :hw] += _colsum(gr * pr + gi * pi)
        da_ref[:, hw:] += _colsum(gi * pr - gr * pi)

    ub = U_OFF // 128
    sd = jax.ShapeDtypeStruct
    rt = lambda t: nt - 1 - t
    res = _pcall(
        body, name="s5_bwd", grid=(S5_KCH, bsz, nt),
        in_specs=[pl.BlockSpec(memory_space=pl.ANY),
                  pl.BlockSpec((None, tt, 128), lambda k, b, t: (b, rt(t), ub + k)),
                  pl.BlockSpec((None, tt, 128), lambda k, b, t: (b, rt(t), k)),
                  pl.BlockSpec((None, None, tt, 2 * hw), lambda k, b, t: (b, k, rt(t), 0)),
                  pl.BlockSpec((None, None, 8, 2 * hw), lambda k, b, t: (b, k, jnp.maximum(rt(t) * tb - 1, 0), 0)),
                  pl.BlockSpec((None, 128, 2 * hw), lambda k, b, t: (k, 0, 0)),
                  pl.BlockSpec((None, hw, 128), lambda k, b, t: (k, 0, 0)),
                  pl.BlockSpec((None, hw, 128), lambda k, b, t: (k, 0, 0)),
                  pl.BlockSpec((None, 8, 2 * hw), lambda k, b, t: (k, 0, 0)),
                  pl.BlockSpec((1, 128), lambda k, b, t: (0, k))],
        out_specs=[pl.BlockSpec((None, tt, 128), lambda k, b, t: (b, rt(t), ub + k)),
                   pl.BlockSpec((None, 128, 2 * hw), lambda k, b, t: (k, 0, 0)),
                   pl.BlockSpec((None, hw, 128), lambda k, b, t: (k, 0, 0)),
                   pl.BlockSpec((None, hw, 128), lambda k, b, t: (k, 0, 0)),
                   pl.BlockSpec((None, 1, 2 * hw), lambda k, b, t: (k, 0, 0)),
                   pl.BlockSpec((1, 128), lambda k, b, t: (0, k))],
        out_shape=[sd(dp3.shape, F32), sd((S5_KCH, 128, 2 * hw), F32), sd((S5_KCH, hw, 128), F32),
                   sd((S5_KCH, hw, 128), F32), sd((S5_KCH, 1, 2 * hw), F32), sd((1, S5_KCH * 128), F32)],
        scratch_shapes=[pltpu.VMEM((tt, 2 * hw), F32), pltpu.VMEM((tt + 8, 2 * hw), F32), pltpu.VMEM((8, 2 * hw), F32)],
        input_output_aliases={0: 0},
        compiler_params=_cp(("arbitrary", "arbitrary", "arbitrary"), 48),
    )(dp3, p3, dy3, xs, xs, bk, cre, cim, apow_rev, dskip)
    return res


_G0 = math.sqrt(2.0 / math.pi)
_G1 = 0.044715


def _gelu(y):
    return 0.5 * y * (1.0 + jnp.tanh(_G0 * (y + _G1 * y * y * y)))


def _gelu_grad(y):
    th = jnp.tanh(_G0 * (y + _G1 * y * y * y))
    return 0.5 * (1.0 + th) + 0.5 * y * (1.0 - th * th) * _G0 * (1.0 + 3.0 * _G1 * y * y)


def _glu_fwd(y_s5, wglu_g, lp):
    r, w = y_s5.shape
    tm = _row_tile(lp, 416)
    cw = wglu_g.shape[2]

    def body(y_ref, w_ref, gy_ref, z_ref, o_ref):
        gy = _bf(_gelu(y_ref[...]))
        gy_ref[...] = gy
        zs = [_dot(gy, w_ref[s]) for s in range(4)]
        for s in range(4):
            z_ref[:, s * cw:(s + 1) * cw] = zs[s]
        o_ref[:, :cw] = zs[0] * _sig(zs[2])
        o_ref[:, cw:] = zs[1] * _sig(zs[3])

    sd = jax.ShapeDtypeStruct
    return _pcall(
        body, name="glu_fwd", grid=(r // tm,),
        in_specs=[pl.BlockSpec((tm, w), lambda i: (i, 0)), _resident(wglu_g.shape)],
        out_specs=[pl.BlockSpec((tm, w), lambda i: (i, 0)), pl.BlockSpec((tm, 4 * cw), lambda i: (i, 0)),
                   pl.BlockSpec((tm, 2 * cw), lambda i: (i, 0))],
        out_shape=[sd((r, w), BF16), sd((r, 4 * cw), F32), sd((r, 2 * cw), F32)],
        compiler_params=_cp(("parallel",), 40),
    )(y_s5, wglu_g)


def _glu_bwd(dyg, z, y_s5, wglu_g, lp):
    r, w = y_s5.shape
    tm = _row_tile(lp, 416)
    cw = wglu_g.shape[2]

    def body(d_ref, z_ref, y_ref, w_ref, dz_ref, dy_ref):
        d = d_ref[...]
        zz = z_ref[...]
        acc = jnp.zeros((tm, w), F32)
        for s in range(2):
            z1 = zz[:, s * cw:(s + 1) * cw]
            sg = _sig(zz[:, (2 + s) * cw:(3 + s) * cw])
            dd = d[:, s * cw:(s + 1) * cw]
            dz1 = _bf(dd * sg)
            dz2 = _bf(dd * z1 * sg * (1.0 - sg))
            dz_ref[:, s * cw:(s + 1) * cw] = dz1
            dz_ref[:, (2 + s) * cw:(3 + s) * cw] = dz2
            acc += _dot(dz1, w_ref[s], NT) + _dot(dz2, w_ref[2 + s], NT)
        dy_ref[...] = acc * _gelu_grad(y_ref[...])

    sd = jax.ShapeDtypeStruct
    return _pcall(
        body, name="glu_bwd", grid=(r // tm,),
        in_specs=[pl.BlockSpec((tm, 2 * cw), lambda i: (i, 0)), pl.BlockSpec((tm, 4 * cw), lambda i: (i, 0)),
                  pl.BlockSpec((tm, w), lambda i: (i, 0)), _resident(wglu_g.shape)],
        out_specs=[pl.BlockSpec((tm, 4 * cw), lambda i: (i, 0)), pl.BlockSpec((tm, w), lambda i: (i, 0))],
        out_shape=[sd((r, 4 * cw), BF16), sd((r, w), F32)],
        compiler_params=_cp(("parallel",), 40),
    )(dyg, z, y_s5, wglu_g)


def _conv_fwd(p3, cw, cb):
    bsz, lp, _ = p3.shape
    tt = _row_tile(lp, 416)
    nt = lp // tt
    tb = tt // 8
    c = cw.shape[1]
    qb = Q_OFF // c

    def body(x_ref, halo_ref, w_ref, b_ref, pre_ref, act_ref, ext_ref):
        t = pl.program_id(1)
        ext_ref[0:8, :] = jnp.where(t == 0, 0.0, halo_ref[...])
        ext_ref[8:, :] = x_ref[...]
        w = w_ref[...]
        acc = b_ref[...] + w[0:1, :] * ext_ref[pl.ds(5, tt), :]
        for j in range(1, CONV_W):
            acc = acc + w[j:j + 1, :] * ext_ref[pl.ds(5 + j, tt), :]
        pre_ref[...] = acc
        act_ref[...] = acc * _sig(acc)

    sd = jax.ShapeDtypeStruct
    return _pcall(
        body, name="conv_fwd", grid=(bsz, nt),
        in_specs=[pl.BlockSpec((None, tt, c), lambda b, t: (b, t, qb)),
                  pl.BlockSpec((None, 8, c), lambda b, t: (b, jnp.maximum(t * tb - 1, 0), qb)),
                  _const((CONV_W, c)), _const((1, c))],
        out_specs=[pl.BlockSpec((None, tt, c), lambda b, t: (b, t, 0))] * 2,
        out_shape=[sd((bsz, lp, c), F32)] * 2,
        scratch_shapes=[pltpu.VMEM((tt + 8, c), F32)],
        compiler_params=_cp(("parallel", "parallel")),
    )(p3, p3, cw, cb)


def _conv_bwd(dp3, p3, dact3, pre3, cw):
    bsz, lp, _ = p3.shape
    tt = _row_tile(lp, 416)
    nt = lp // tt
    tb = tt // 8
    c = cw.shape[1]
    qb = Q_OFF // c

    def silu_grad(x):
        s = _sig(x)
        return s * (1.0 + x * (1.0 - s))

    def body(dp_any, x_ref, xh_ref, d_ref, dh_ref, pre_ref, preh_ref, w_ref, o_ref, dw_ref, db_ref, ext_ref, dext_ref):
        b = pl.program_id(0)
        t = pl.program_id(1)

        @pl.when((b == 0) & (t == 0))
        def _():
            dw_ref[...] = jnp.zeros_like(dw_ref)
            db_ref[...] = jnp.zeros_like(db_ref)

        dc = d_ref[...] * silu_grad(pre_ref[...])
        dch = jnp.where(t == nt - 1, 0.0, dh_ref[...] * silu_grad(preh_ref[...]))
        dext_ref[0:tt, :] = dc
        dext_ref[tt:, :] = dch
        ext_ref[0:8, :] = jnp.where(t == 0, 0.0, xh_ref[...])
        ext_ref[8:, :] = x_ref[...]
        w = w_ref[...]
        acc = w[CONV_W - 1:CONV_W, :] * dc
        for j in range(CONV_W - 1):
            acc = acc + w[j:j + 1, :] * dext_ref[pl.ds(CONV_W - 1 - j, tt), :]
        trow = t * tt + lax.broadcasted_iota(jnp.int32, (tt, 1), 0)
        o_ref[...] = jnp.where(trow >= PAD, acc, 0.0)
        db_ref[...] += _colsum(dc)
        for j in range(CONV_W):
            dw_ref[j:j + 1, :] += _colsum(dc * ext_ref[pl.ds(5 + j, tt), :])

    sd = jax.ShapeDtypeStruct
    nxt = lambda t: jnp.minimum((t + 1) * tb, lp // 8 - 1)
    return _pcall(
        body, name="conv_bwd", grid=(bsz, nt),
        in_specs=[pl.BlockSpec(memory_space=pl.ANY),
                  pl.BlockSpec((None, tt, c), lambda b, t: (b, t, qb)),
                  pl.BlockSpec((None, 8, c), lambda b, t: (b, jnp.maximum(t * tb - 1, 0), qb)),
                  pl.BlockSpec((None, tt, c), lambda b, t: (b, t, 0)),
                  pl.BlockSpec((None, 8, c), lambda b, t: (b, nxt(t), 0)),
                  pl.BlockSpec((None, tt, c), lambda b, t: (b, t, 0)),
                  pl.BlockSpec((None, 8, c), lambda b, t: (b, nxt(t), 0)),
                  _const((CONV_W, c))],
        out_specs=[pl.BlockSpec((None, tt, c), lambda b, t: (b, t, qb)), _const((CONV_W, c)), _const((1, c))],
        out_shape=[sd(dp3.shape, F32), sd((CONV_W, c), F32), sd((1, c), F32)],
        scratch_shapes=[pltpu.VMEM((tt + 8, c), F32), pltpu.VMEM((tt + 8, c), F32)],
        input_output_aliases={0: 0},
        compiler_params=_cp(("arbitrary", "arbitrary")),
    )(dp3, p3, p3, dact3, dact3, pre3, pre3, cw)


def _mlstm_gates(g, h_idx, c_idx, lc):
    lane = lax.broadcasted_iota(jnp.int32, g.shape, 1)
    i_col = jnp.sum(jnp.where(lane == h_idx, g, 0.0), axis=1, keepdims=True)
    f_col = jnp.sum(jnp.where(lane == M_HEADS + h_idx, g, 0.0), axis=1, keepdims=True)
    row = lax.broadcasted_iota(jnp.int32, (lc, 1), 0)
    valid = (c_idx * lc + row) >= PAD
    li = jnp.where(valid, i_col, NEG)
    lf = jnp.where(valid, jnp.minimum(f_col, 0.0) - jnp.log(1.0 + jnp.exp(-jnp.abs(f_col))), 0.0)
    r2 = lax.broadcasted_iota(jnp.int32, (lc, lc), 0)
    c2 = lax.broadcasted_iota(jnp.int32, (lc, lc), 1)
    eye = r2 == c2
    tril = r2 >= c2
    to_row = lambda col: jnp.sum(jnp.where(eye, col, 0.0), axis=0, keepdims=True)
    lf_row = to_row(lf)
    b_col = jnp.sum(jnp.where(tril, lf_row, 0.0), axis=1, keepdims=True)
    b_row = to_row(b_col)
    li_row = to_row(li)
    d_mat = jnp.where(tril, b_col - b_row + li_row, NEG)
    return dict(f_col=f_col, valid=valid, li=li, b_col=b_col, d_mat=d_mat, eye=eye, r2=r2, c2=c2, row=row,
                to_row=to_row)


def _mlstm_chunk(q, ks, v, gq, c_st, n_st, m_st, lc):
    b_col, d_mat = gq["b_col"], gq["d_mat"]
    m_inter = b_col + m_st
    m_row = jnp.maximum(m_inter, jnp.max(d_mat, axis=1, keepdims=True))
    w_intra = jnp.exp(d_mat - m_row)
    w_inter = jnp.exp(m_inter - m_row)
    qb, kb, vb, cb = _bf(q), _bf(ks), _bf(v), _bf(c_st)
    s = _dot(qb, kb, NT) * w_intra
    qc = _dot(qb, cb)
    num = _dot(_bf(s), vb) + w_inter * qc
    qn = jnp.sum(q * n_st, axis=1, keepdims=True)
    den = jnp.sum(s, axis=1, keepdims=True) + w_inter * qn
    e = jnp.exp(-m_row)
    nn = jnp.maximum(jnp.abs(den), e)
    b_last = b_col[lc - 1:lc, :]
    g_log = b_last - b_col + gq["li"]
    m_new = jnp.maximum(b_last + m_st, jnp.max(g_log, axis=0, keepdims=True))
    w_k = jnp.exp(g_log - m_new)
    decay = jnp.exp(b_last + m_st - m_new)
    return dict(w_intra=w_intra, w_inter=w_inter, qb=qb, kb=kb, vb=vb, cb=cb, s=s, qc=qc, num=num, qn=qn, den=den,
                e=e, nn=nn, m_new=m_new, w_k=w_k, decay=decay)


def _mlstm_fwd(qk3, p3):
    bsz, lp, _ = p3.shape
    lc = M_CHUNK
    nc = lp // lc
    dk, dv = 128, 256
    scale = dk ** -0.5

    def body(q_ref, k_ref, v_ref, g_ref, h_ref, cs_ref, ns_ref, ms_ref, c_sc, n_sc, m_sc):
        c = pl.program_id(1)

        @pl.when(c == 0)
        def _():
            c_sc[...] = jnp.zeros_like(c_sc)
            n_sc[...] = jnp.zeros_like(n_sc)
            m_sc[...] = jnp.zeros_like(m_sc)

        g = g_ref[...]
        for hh in range(M_HEADS):
            c_st, n_st, m_all = c_sc[hh], n_sc[hh], m_sc[hh]
            cs_ref[hh] = c_st
            ns_ref[hh] = n_st
            ms_ref[hh] = m_all
            m_st = m_all[:, 0:1]
            q = q_ref[:, hh * dk:(hh + 1) * dk]
            ks = k_ref[:, hh * dk:(hh + 1) * dk] * scale
            v = v_ref[:, hh * dv:(hh + 1) * dv]
            gq = _mlstm_gates(g, hh, c, lc)
            f = _mlstm_chunk(q, ks, v, gq, c_st, n_st, m_st, lc)
            h_ref[:, hh * dv:(hh + 1) * dv] = f["num"] / f["nn"]
            kw = ks * f["w_k"]
            c_sc[hh] = f["decay"] * c_st + _dot(_bf(kw), f["vb"], TN)
            n_sc[hh] = f["decay"] * n_st + _colsum(kw)
            m_sc[hh] = jnp.broadcast_to(f["m_new"], (1, 128))

    sd = jax.ShapeDtypeStruct
    nh = M_HEADS
    return _pcall(
        body, name="mlstm_fwd", grid=(bsz, nc),
        in_specs=[pl.BlockSpec((None, lc, nh * dk), lambda b, c: (b, c, 0)),
                  pl.BlockSpec((None, lc, nh * dk), lambda b, c: (b, c, 1)),
                  pl.BlockSpec((None, lc, nh * dv), lambda b, c: (b, c, V_OFF // (nh * dv))),
                  pl.BlockSpec((None, lc, 128), lambda b, c: (b, c, G_OFF // 128))],
        out_specs=[pl.BlockSpec((None, lc, nh * dv), lambda b, c: (b, c, 0)),
                   pl.BlockSpec((None, nh, None, dk, dv), lambda b, c: (b, 0, c, 0, 0)),
                   pl.BlockSpec((None, nh, None, 1, dk), lambda b, c: (b, 0, c, 0, 0)),
                   pl.BlockSpec((None, nh, None, 1, 128), lambda b, c: (b, 0, c, 0, 0))],
        out_shape=[sd((bsz, lp, nh * dv), F32), sd((bsz, nh, nc, dk, dv), F32),
                   sd((bsz, nh, nc, 1, dk), F32), sd((bsz, nh, nc, 1, 128), F32)],
        scratch_shapes=[pltpu.VMEM((nh, dk, dv), F32), pltpu.VMEM((nh, 1, dk), F32), pltpu.VMEM((nh, 1, 128), F32)],
        compiler_params=_cp(("parallel", "arbitrary")),
    )(qk3, qk3, p3, p3)


def _mlstm_bwd(dp3, qk3, p3, dh3, cs, ns, ms):
    bsz, lp, _ = p3.shape
    lc = M_CHUNK
    nc = lp // lc
    dk, dv = 128, 256
    scale = dk ** -0.5

    def body(dp_any, q_ref, k_ref, v_ref, g_ref, dh_ref, cs_ref, ns_ref, ms_ref,
             dv_ref, dqk_ref, dg_ref, dc_sc, dn_sc):
        t = pl.program_id(1)
        c = nc - 1 - t

        @pl.when(t == 0)
        def _():
            dc_sc[...] = jnp.zeros_like(dc_sc)
            dn_sc[...] = jnp.zeros_like(dn_sc)

        g = g_ref[...]
        lane = lax.broadcasted_iota(jnp.int32, (lc, 128), 1)
        dgate = jnp.zeros((lc, 128), F32)
        for hh in range(M_HEADS):
            dgate = head(hh, c, g, lane, dgate, q_ref, k_ref, v_ref, dh_ref, cs_ref, ns_ref, ms_ref,
                         dv_ref, dqk_ref, dc_sc, dn_sc)
        dg_ref[...] = dgate

    def head(hh, c, g, lane, dgate, q_ref, k_ref, v_ref, dh_ref, cs_ref, ns_ref, ms_ref, dv_ref, dqk_ref, dc_sc, dn_sc):
        c_st, n_st = cs_ref[hh], ns_ref[hh]
        m_st = ms_ref[hh][:, 0:1]
        q = q_ref[:, hh * dk:(hh + 1) * dk]
        ks = k_ref[:, hh * dk:(hh + 1) * dk] * scale
        v = v_ref[:, hh * dv:(hh + 1) * dv]
        dh = dh_ref[:, hh * dv:(hh + 1) * dv]
        gq = _mlstm_gates(g, hh, c, lc)
        f = _mlstm_chunk(q, ks, v, gq, c_st, n_st, m_st, lc)
        eye, r2, c2, row, valid = gq["eye"], gq["r2"], gq["c2"], gq["row"], gq["valid"]
        w_intra, w_inter, s, nn, den = f["w_intra"], f["w_inter"], f["s"], f["nn"], f["den"]
        qb, kb, vb, cb, w_k, decay = f["qb"], f["kb"], f["vb"], f["cb"], f["w_k"], f["decay"]
        d_c, d_n = dc_sc[hh], dn_sc[hh]
        d_cb = _bf(d_c)

        hout = f["num"] / nn
        dnum = dh / nn
        d_nn = -jnp.sum(dh * hout, axis=1, keepdims=True) / nn
        dden = jnp.where(jnp.abs(den) > f["e"], d_nn * jnp.sign(den), 0.0)
        wdnum = w_inter * dnum
        wdden = w_inter * dden
        ds = _dot(_bf(dnum), vb, NT) + dden
        dsw = _bf(ds * w_intra)
        dq = _dot(dsw, kb) + _dot(_bf(wdnum), cb, NT) + wdden * n_st
        dkw = _dot(vb, d_cb, NT) + d_n
        dks = _dot(dsw, qb, TN) + dkw * w_k
        kw = ks * w_k
        dvv = _dot(_bf(s), _bf(dnum), TN) + _dot(_bf(kw), d_cb)
        dd = ds * s
        rs = jnp.sum(dd, axis=1, keepdims=True)
        cs_col = jnp.sum(jnp.where(eye, jnp.sum(dd, axis=0, keepdims=True), 0.0), axis=1, keepdims=True)
        dwi = jnp.sum(dnum * f["qc"], axis=1, keepdims=True) + dden * f["qn"]
        db = rs - cs_col + dwi * w_inter
        dli = cs_col
        ddecay = jnp.sum(jnp.sum(d_c * c_st, axis=1, keepdims=True), axis=0, keepdims=True) \
            + jnp.sum(d_n * n_st, axis=1, keepdims=True)
        dgl = jnp.sum(dkw * ks, axis=1, keepdims=True) * w_k
        dblast = ddecay * decay + jnp.sum(dgl, axis=0, keepdims=True)
        db = db - dgl + jnp.where(row == lc - 1, dblast, 0.0)
        dli = dli + dgl
        db_row = gq["to_row"](db)
        dlf = jnp.sum(jnp.where(c2 >= r2, db_row, 0.0), axis=1, keepdims=True)
        dlf = jnp.where(valid, dlf, 0.0)
        dgate = jnp.where(lane == hh, jnp.where(valid, dli, 0.0), dgate)
        dgate = jnp.where(lane == M_HEADS + hh, dlf * _sig(-gq["f_col"]), dgate)
        dqk_ref[:, hh * dk:(hh + 1) * dk] = dq
        dqk_ref[:, (M_HEADS + hh) * dk:(M_HEADS + hh + 1) * dk] = dks * scale
        dv_ref[:, hh * dv:(hh + 1) * dv] = dvv
        dc_sc[hh] = decay * d_c + _dot(qb, _bf(wdnum), TN)
        dn_sc[hh] = decay * d_n + _colsum(q * wdden)
        return dgate

    sd = jax.ShapeDtypeStruct
    nh = M_HEADS
    rc = lambda c: nc - 1 - c
    return _pcall(
        body, name="mlstm_bwd", grid=(bsz, nc),
        in_specs=[pl.BlockSpec(memory_space=pl.ANY),
                  pl.BlockSpec((None, lc, nh * dk), lambda b, c: (b, rc(c), 0)),
                  pl.BlockSpec((None, lc, nh * dk), lambda b, c: (b, rc(c), 1)),
                  pl.BlockSpec((None, lc, nh * dv), lambda b, c: (b, rc(c), V_OFF // (nh * dv))),
                  pl.BlockSpec((None, lc, 128), lambda b, c: (b, rc(c), G_OFF // 128)),
                  pl.BlockSpec((None, lc, nh * dv), lambda b, c: (b, rc(c), 0)),
                  pl.BlockSpec((None, nh, None, dk, dv), lambda b, c: (b, 0, rc(c), 0, 0)),
                  pl.BlockSpec((None, nh, None, 1, dk), lambda b, c: (b, 0, rc(c), 0, 0)),
                  pl.BlockSpec((None, nh, None, 1, 128), lambda b, c: (b, 0, rc(c), 0, 0))],
        out_specs=[pl.BlockSpec((None, lc, nh * dv), lambda b, c: (b, rc(c), V_OFF // (nh * dv))),
                   pl.BlockSpec((None, lc, 2 * nh * dk), lambda b, c: (b, rc(c), 0)),
                   pl.BlockSpec((None, lc, 128), lambda b, c: (b, rc(c), 0))],
        out_shape=[sd(dp3.shape, F32), sd((bsz, lp, 2 * nh * dk), F32), sd((bsz, lp, 128), F32)],
        scratch_shapes=[pltpu.VMEM((nh, dk, dv), F32), pltpu.VMEM((nh, 1, dk), F32)],
        input_output_aliases={0: 0},
        compiler_params=_cp(("arbitrary", "arbitrary")),
    )(dp3, qk3, qk3, p3, p3, dh3, cs, ns, ms)


def _headnorm(x):
    dv = x.shape[1] // M_HEADS
    xh, rs = [], []
    for h in range(M_HEADS):
        xx = x[:, h * dv:(h + 1) * dv]
        mu = jnp.mean(xx, axis=-1, keepdims=True)
        xc = xx - mu
        rstd = lax.rsqrt(jnp.mean(xc * xc, axis=-1, keepdims=True) + LN_EPS)
        xh.append(xc * rstd)
        rs.append(rstd)
    return jnp.concatenate(xh, axis=1), rs


def _mix_fwd(hm, p, ys5g, h0, gn, wmo_bf, wo_bf, g1, b1, lp):
    r, d = hm.shape
    tm = _row_tile(lp, 208)

    def body(hm_ref, o_ref, gs_ref, gm_ref, ys_ref, h0_ref, gn_ref, wmo_ref, wo_ref, g1_ref, b1_ref,
             ymin_ref, ym_ref, mix_ref, r1_ref, h1_ref):
        xhat, _ = _headnorm(hm_ref[...])
        ymin = _bf(_sig(o_ref[...]) * (xhat * gn_ref[...]))
        ymin_ref[...] = ymin
        ym = _dot(ymin, wmo_ref[...])
        ym_ref[...] = ym
        mix = _bf(_sig(gs_ref[...]) * ys_ref[...] + _sig(gm_ref[...]) * ym)
        mix_ref[...] = mix
        r1 = ALPHA * h0_ref[...] + _dot(mix, wo_ref[...])
        r1_ref[...] = r1
        h1, _, _ = _ln_fwd(r1, g1_ref[...], b1_ref[...])
        h1_ref[...] = h1

    sd = jax.ShapeDtypeStruct
    row = pl.BlockSpec((tm, d), lambda i: (i, 0))
    return _pcall(
        body, name="mix_fwd", grid=(r // tm,),
        in_specs=[row, pl.BlockSpec((tm, d), lambda i: (i, O_OFF // d)), pl.BlockSpec((tm, d), lambda i: (i, GS_OFF // d)),
                  pl.BlockSpec((tm, d), lambda i: (i, GM_OFF // d)), row, row, _const((1, d)),
                  _resident((d, d)), _resident((d, d)), _const((1, d)), _const((1, d))],
        out_specs=[row] * 5,
        out_shape=[sd((r, d), BF16), sd((r, d), F32), sd((r, d), BF16), sd((r, d), F32), sd((r, d), F32)],
        compiler_params=_cp(("parallel",), 48),
    )(hm, p, p, p, ys5g, h0, gn, wmo_bf, wo_bf, g1, b1)


def _mix_bwd(dh1, r1, g1, wo_bf, wmo_bf, p, ys5g, ym, hm, gn, lp):
    r, d = hm.shape
    tm = _row_tile(lp, 208)
    dv = d // M_HEADS

    def body(dh1_ref, r1_ref, g1_ref, wo_ref, wmo_ref, o_ref, gs_ref, gm_ref, ys_ref, ym_ref, hm_ref, gn_ref,
             dr1_ref, dp_ref, dys_ref, dym_ref, dhm_ref, dg1_ref, db1_ref, dgn_ref):
        i = pl.program_id(0)

        @pl.when(i == 0)
        def _():
            dg1_ref[...] = jnp.zeros_like(dg1_ref)
            db1_ref[...] = jnp.zeros_like(db1_ref)
            dgn_ref[...] = jnp.zeros_like(dgn_ref)

        dh1 = dh1_ref[...]
        _, xhat1, rstd1 = _ln_fwd(r1_ref[...], g1_ref[...], 0.0)
        dr1 = _ln_bwd(dh1, xhat1, rstd1, g1_ref[...])
        dr1_ref[...] = dr1
        dg1_ref[...] += _colsum(dh1 * xhat1)
        db1_ref[...] += _colsum(dh1)
        dmix = _dot(_bf(dr1), wo_ref[...], NT)
        sgs, sgm, so = _sig(gs_ref[...]), _sig(gm_ref[...]), _sig(o_ref[...])
        dys_ref[...] = dmix * sgs
        dp_ref[:, d:2 * d] = dmix * ys_ref[...] * sgs * (1.0 - sgs)
        dym = dmix * sgm
        dym_ref[...] = _bf(dym)
        dp_ref[:, 2 * d:3 * d] = dmix * ym_ref[...] * sgm * (1.0 - sgm)
        dymin = _dot(_bf(dym), wmo_ref[...], NT)
        xhat, rs = _headnorm(hm_ref[...])
        gn_ = gn_ref[...]
        dp_ref[:, 0:d] = dymin * (xhat * gn_) * so * (1.0 - so)
        dhn = dymin * so
        dgn_ref[...] += _colsum(dhn * xhat)
        dxh = dhn * gn_
        for h in range(M_HEADS):
            sl = slice(h * dv, (h + 1) * dv)
            a, xh = dxh[:, sl], xhat[:, sl]
            m1 = jnp.mean(a, axis=-1, keepdims=True)
            m2 = jnp.mean(a * xh, axis=-1, keepdims=True)
            dhm_ref[:, sl] = rs[h] * (a - m1 - xh * m2)

    sd = jax.ShapeDtypeStruct
    row = pl.BlockSpec((tm, d), lambda i: (i, 0))
    vec = _const((1, d))
    return _pcall(
        body, name="mix_bwd", grid=(r // tm,),
        in_specs=[row, row, vec, _resident((d, d)), _resident((d, d)),
                  pl.BlockSpec((tm, d), lambda i: (i, O_OFF // d)), pl.BlockSpec((tm, d), lambda i: (i, GS_OFF // d)),
                  pl.BlockSpec((tm, d), lambda i: (i, GM_OFF // d)), row, row, row, vec],
        out_specs=[row, pl.BlockSpec((tm, 3 * d), lambda i: (i, 0)), row, row, row, vec, vec, vec],
        out_shape=[sd((r, d), F32), sd((r, NP), F32), sd((r, d), F32), sd((r, d), BF16), sd((r, d), F32),
                   sd((1, d), F32), sd((1, d), F32), sd((1, d), F32)],
        compiler_params=_cp(("arbitrary",), 48),
    )(dh1, r1, g1, wo_bf, wmo_bf, p, p, p, ys5g, ym, hm, gn)


def _mlp_fwd(h1, tgt, wup_g, wdn_bf, bup, g2, b2, lp):
    r, d = h1.shape
    tm = _row_tile(lp, 320)
    tps = lp // tm
    nf = wup_g.shape[0]

    def body(h1_ref, t_ref, wup_ref, wdn_ref, bup_ref, g2_ref, b2_ref, dr2_ref, act_ref, loss_ref, dg2_ref, db2_ref):
        i = pl.program_id(0)

        @pl.when(i == 0)
        def _():
            loss_ref[...] = jnp.zeros_like(loss_ref)
            dg2_ref[...] = jnp.zeros_like(dg2_ref)
            db2_ref[...] = jnp.zeros_like(db2_ref)

        h1 = h1_ref[...]
        h1b = _bf(h1)
        ff = jnp.zeros((tm, d), F32)
        for s in range(nf):
            up = _dot(h1b, wup_ref[s]) + bup_ref[:, s * d:(s + 1) * d]
            a = jnp.maximum(up, 0.0)
            a = _bf(a * a)
            act_ref[:, s * d:(s + 1) * d] = a
            ff = ff + _dot(a, wdn_ref[s * d:(s + 1) * d, :])
        r2 = ALPHA * h1 + ff
        g2 = g2_ref[...]
        y, xhat, rstd = _ln_fwd(r2, g2, b2_ref[...])
        t = (i % tps) * tm + lax.broadcasted_iota(jnp.int32, (tm, 1), 0)
        diff = jnp.where(t >= PAD + N_META, y - t_ref[...], 0.0)
        loss_ref[...] += 0.5 / d * jnp.sum(jnp.sum(diff * diff, axis=1, keepdims=True), axis=0, keepdims=True)
        dy = diff * (1.0 / d)
        dg2_ref[...] += _colsum(dy * xhat)
        db2_ref[...] += _colsum(dy)
        dr2_ref[...] = _ln_bwd(dy, xhat, rstd, g2)

    sd = jax.ShapeDtypeStruct
    row = pl.BlockSpec((tm, d), lambda i: (i, 0))
    vec = _const((1, d))
    return _pcall(
        body, name="mlp_fwd", grid=(r // tm,),
        in_specs=[row, row, _resident(wup_g.shape), _resident(wdn_bf.shape), _const((1, nf * d)), vec, vec],
        out_specs=[row, pl.BlockSpec((tm, nf * d), lambda i: (i, 0)), _const((1, 128)), vec, vec],
        out_shape=[sd((r, d), F32), sd((r, nf * d), BF16), sd((1, 128), F32), sd((1, d), F32), sd((1, d), F32)],
        compiler_params=_cp(("arbitrary",), 56),
    )(h1, tgt, wup_g, wdn_bf, bup, g2, b2)


def _mlp_bwd(h1, dr2, wup_g, wdn_bf, bup, lp):
    r, d = h1.shape
    tm = _row_tile(lp, 320)
    nf = wup_g.shape[0]

    def body(h1_ref, dr2_ref, wup_ref, wdn_ref, bup_ref, dh1_ref, dup_ref, dbup_ref):
        i = pl.program_id(0)

        @pl.when(i == 0)
        def _():
            dbup_ref[...] = jnp.zeros_like(dbup_ref)

        h1b = _bf(h1_ref[...])
        dr2 = dr2_ref[...]
        dr2b = _bf(dr2)
        acc = ALPHA * dr2
        for s in range(nf):
            up = _dot(h1b, wup_ref[s]) + bup_ref[:, s * d:(s + 1) * d]
            dact = _dot(dr2b, wdn_ref[s * d:(s + 1) * d, :], NT)
            dup = dact * (2.0 * jnp.maximum(up, 0.0))
            dbup_ref[:, s * d:(s + 1) * d] += _colsum(dup)
            dupb = _bf(dup)
            dup_ref[:, s * d:(s + 1) * d] = dupb
            acc = acc + _dot(dupb, wup_ref[s], NT)
        dh1_ref[...] = acc

    sd = jax.ShapeDtypeStruct
    row = pl.BlockSpec((tm, d), lambda i: (i, 0))
    return _pcall(
        body, name="mlp_bwd", grid=(r // tm,),
        in_specs=[row, row, _resident(wup_g.shape), _resident(wdn_bf.shape), _const((1, nf * d))],
        out_specs=[row, pl.BlockSpec((tm, nf * d), lambda i: (i, 0)), _const((1, nf * d))],
        out_shape=[sd((r, d), F32), sd((r, nf * d), BF16), sd((1, nf * d), F32)],
        compiler_params=_cp(("arbitrary",), 56),
    )(h1, dr2, wup_g, wdn_bf, bup)


def _s5_block_mats(bb_re_t, bb_im_t, c_re, c_im, ap_re, ap_im):
    ng = c_re.shape[0]
    gl = ng // S5_KCH
    eye = jnp.eye(gl, dtype=F32)

    def bmat(bt):
        bb = jnp.transpose(bt, (1, 0, 2)).reshape(S5_KCH, gl, S5_GROUP, S5_STATE)
        return jnp.einsum("kghp,gj->kghjp", bb, eye).reshape(S5_KCH, gl * S5_GROUP, gl * S5_STATE)

    def cmat(c):
        cc = c.reshape(S5_KCH, gl, S5_GROUP, S5_STATE)
        return jnp.einsum("kghp,gj->kjpgh", cc, eye).reshape(S5_KCH, gl * S5_STATE, gl * S5_GROUP)

    def pw(a):
        return jnp.transpose(a.reshape(8, S5_KCH, gl * S5_STATE), (1, 0, 2))

    bk = jnp.concatenate([bmat(bb_re_t), bmat(bb_im_t)], axis=-1)
    apow = jnp.concatenate([pw(ap_re), pw(ap_im)], axis=-1)
    return _bf(bk), _bf(cmat(c_re)), _bf(cmat(c_im)), apow


def _s5_block_grads(dbk, dcre, dcim, da):
    gl = dbk.shape[1] // S5_GROUP
    ng = gl * S5_KCH
    eye = jnp.eye(gl, dtype=F32)
    hw = gl * S5_STATE

    def bpart(x):
        x = x.reshape(S5_KCH, gl, S5_GROUP, gl, S5_STATE)
        x = jnp.einsum("kghjp,gj->kghp", x, eye).reshape(ng, S5_GROUP, S5_STATE)
        return jnp.transpose(x, (1, 0, 2))

    def cpart(x):
        x = x.reshape(S5_KCH, gl, S5_STATE, gl, S5_GROUP)
        return jnp.einsum("kjpgh,gj->kghp", x, eye).reshape(ng, S5_GROUP, S5_STATE)

    return (bpart(dbk[..., :hw]), bpart(dbk[..., hw:]), cpart(dcre), cpart(dcim),
            da[:, 0, :hw].reshape(ng, S5_STATE), da[:, 0, hw:].reshape(ng, S5_STATE))


def _tie(a, tok):
    return a if tok is None else a + tok[0, 0]


def _local_step(x, tgt, w, late=None, ready=None):
    ready = ready or (lambda names, g: None)
    bsz, seq, d = x.shape
    lp = PAD + N_META + seq
    r = bsz * lp
    meta = jnp.broadcast_to(w["meta_tokens"][None], (bsz, N_META, d))
    hin = jnp.concatenate([jnp.zeros((bsz, PAD, d), F32), meta, x], axis=1).reshape(r, d)
    tgtp = jnp.concatenate([jnp.zeros((bsz, PAD + N_META, d), F32), tgt], axis=1).reshape(r, d)

    h0 = _ln0_fwd(hin, w["ln0_g"], w["ln0_b"], lp)
    p = _inproj(h0, w["w_in"], w["b_in"], lp)
    p3 = p.reshape(bsz, lp, NP)

    b_re_t = jnp.transpose(w["s5_b_re"], (2, 0, 1))
    b_im_t = jnp.transpose(w["s5_b_im"], (2, 0, 1))
    ap_re, ap_im, bb_re_t, bb_im_t = _s5_prep(w["s5_lambda_re"], w["s5_lambda_im"], w["s5_log_dt"], b_re_t, b_im_t)
    bk, cre, cim, apow = _s5_block_mats(bb_re_t, bb_im_t, w["s5_c_re"], w["s5_c_im"], ap_re, ap_im)
    y_s5, xs = _s5_fwd(p3, bk, cre, cim, apow, w["s5_d"])
    sw = y_s5.shape[-1]
    if late is not None:
        w = {**w, **late(y_s5)}
    gy, z, ys5g = _glu_fwd(y_s5.reshape(r, sw), w["s5_w_glu"], lp)

    pre3, qk3 = _conv_fwd(p3, w["qk_conv_w"], w["qk_conv_b"])
    hm3, cs, ns, ms = _mlstm_fwd(qk3, p3)
    hm = hm3.reshape(r, d)
    ymin, ym, mix, r1, h1 = _mix_fwd(hm, p, ys5g, h0, w["m_norm_g"], w["m_w_out"], w["w_o"], w["ln1_g"], w["ln1_b"], lp)
    dr2, act, loss, dg2, db2 = _mlp_fwd(h1, tgtp, w["w_up"], w["w_down"], w["b_up"], w["ln2_g"], w["ln2_b"], lp)

    g = {"ln2_g": dg2, "ln2_b": db2}
    dh1, dup, g["b_up"] = _mlp_bwd(h1, dr2, w["w_up"], w["w_down"], w["b_up"], lp)
    g["w_down"] = _mm_tn(act, dr2, name="dw_down")
    g["w_up"] = _mm_tn(h1, dup, name="dw_up", split=w["w_up"].shape[0])
    tok = ready(("w_down", "w_up"), g)
    dr1, dp, dys5g, dym, dhm, g["ln1_g"], g["ln1_b"], g["m_norm_g"] = _mix_bwd(
        dh1, r1, _tie(w["ln1_g"], tok), w["w_o"], w["m_w_out"], p, ys5g, ym, hm, w["m_norm_g"], lp)
    g["w_o"] = _mm_tn(mix, dr1, name="dw_o")
    g["m_w_out"] = _mm_tn(ymin, dym, name="dw_mout")

    dp3 = dp.reshape(bsz, lp, NP)
    dp3, dqk3, dgate = _mlstm_bwd(dp3, qk3, p3, dhm.reshape(bsz, lp, d), cs, ns, ms)
    dp3, g["qk_conv_w"], g["qk_conv_b"] = _conv_bwd(dp3, p3, dqk3, pre3, w["qk_conv_w"])
    dz, dys5 = _glu_bwd(dys5g, z, y_s5.reshape(r, sw), w["s5_w_glu"], lp)
    g["s5_w_glu"] = _mm_tn(gy, dz, name="dw_glu", split=w["s5_w_glu"].shape[0])
    tok = ready(("s5_w_glu", "m_w_out", "w_o"), g)
    apow_rev = jnp.flip(apow, axis=1)
    dp3, dbk, dcre, dcim, da, g["s5_d"] = _s5_bwd(dp3, p3, dys5.reshape(bsz, lp, sw), xs, bk, cre, cim, apow_rev,
                                                 _tie(w["s5_d"], tok))
    dbb_re_t, dbb_im_t, g["s5_c_re"], g["s5_c_im"], da_re, da_im = _s5_block_grads(dbk, dcre, dcim, da)
    g["s5_lambda_re"], g["s5_lambda_im"], g["s5_log_dt"], gb_re_t, gb_im_t = _s5_prep_bwd(
        w["s5_lambda_re"], w["s5_lambda_im"], w["s5_log_dt"], b_re_t, b_im_t, da_re, da_im, dbb_re_t, dbb_im_t)
    g["s5_b_re"] = jnp.transpose(gb_re_t, (1, 2, 0))
    g["s5_b_im"] = jnp.transpose(gb_im_t, (1, 2, 0))

    dp3 = lax.dynamic_update_slice(dp3, dgate, (0, 0, G_OFF))
    dp = dp3.reshape(r, NP)
    g["w_in"], g["b_in"] = _mm_tn(h0, dp, name="dw_in", colsum=True)
    tok = ready(("w_in",), g)
    dpw = _mm_nt(dp, w["w_in"], lp, name="dh0", dep=tok)
    dhin, g["ln0_g"], g["ln0_b"], g["meta_tokens"] = _ln0_bwd(hin, dr1, dpw, w["ln0_g"], lp)
    grad_x = dhin.reshape(bsz, lp, d)[:, PAD + N_META:]
    return loss, grad_x, g


_ANY = pl.BlockSpec(memory_space=pl.ANY)
_MESH = pl.DeviceIdType.MESH


def _place():
    return lax.axis_index("x"), lax.axis_index("y"), lax.axis_index("c")


def _gather_chips(shards):
    n = len(shards)

    def body(*refs):
        ins, outs = refs[:n], refs[n:2 * n]
        send, recv, loc = refs[2 * n:]
        x, y, c = _place()
        me = 2 * x + y
        peers = [(1 - x, y), (x, 1 - y), (1 - x, 1 - y)]

        def rc(a, k, slot):
            px, py = peers[k]
            return pltpu.make_async_remote_copy(src_ref=ins[a], dst_ref=outs[a].at[slot], send_sem=send.at[a, k],
                                                recv_sem=recv.at[a, k], device_id=(px, py, c), device_id_type=_MESH)

        own = [pltpu.make_async_copy(ins[a], outs[a].at[me], loc.at[a]) for a in range(n)]
        for cp in own:
            cp.start()
        out = [rc(a, k, me) for a in range(n) for k in range(3)]
        for cp in out:
            cp.start()
        for a in range(n):
            for k in range(3):
                rc(a, k, 2 * peers[k][0] + peers[k][1]).wait_recv()
        for cp in out:
            cp.wait_send()
        for cp in own:
            cp.wait()

    return _pcall(
        body, name="gather_chips", in_specs=[_ANY] * n, out_specs=[_ANY] * n,
        out_shape=[jax.ShapeDtypeStruct((4,) + s.shape, s.dtype) for s in shards],
        scratch_shapes=[pltpu.SemaphoreType.DMA((n, 3)), pltpu.SemaphoreType.DMA((n, 3)), pltpu.SemaphoreType.DMA((n,))],
    )(*shards)


_HBM = pl.BlockSpec(memory_space=pltpu.HBM)
_SEM = pl.BlockSpec(memory_space=pltpu.SEMAPHORE)
_EFFECT = pltpu.SideEffectType.DATAFLOW_SIDE_EFFECTING


def _xchg_copies(srcs, lands, send, recv, scatter):
    x, y, c = _place()
    me = 2 * x + y
    peers = [(1 - x, y), (x, 1 - y), (1 - x, 1 - y)]
    out = []
    for a in range(len(srcs)):
        for k, (px, py) in enumerate(peers):
            src = srcs[a].at[2 * px + py] if scatter else srcs[a]
            dst = lands[a].at[k] if scatter else lands[a].at[me]
            out.append(pltpu.make_async_remote_copy(src_ref=src, dst_ref=dst, send_sem=send.at[3 * a + k],
                                                    recv_sem=recv.at[3 * a + k], device_id=(px, py, c),
                                                    device_id_type=_MESH))
    return out


def _xchg_start(srcs, lands, *, name, scatter):
    n = len(srcs)

    def body(*refs):
        send, recv = refs[2 * n], refs[2 * n + 1]
        for cp in _xchg_copies(refs[:n], refs[n:2 * n], send, recv, scatter):
            cp.start()
        refs[-1][...] = jnp.zeros_like(refs[-1])

    hbm = lambda a: pltpu.HBM(a.shape, a.dtype)
    con = lambda a: pltpu.with_memory_space_constraint(a, pltpu.HBM)
    res = _pcall(
        body, name=name, in_specs=[_HBM] * (2 * n),
        out_specs=[_SEM, _SEM] + [_HBM] * (2 * n) + [pl.BlockSpec(memory_space=pltpu.VMEM)],
        out_shape=[pltpu.SemaphoreType.DMA((3 * n,)), pltpu.SemaphoreType.DMA((3 * n,))]
        + [hbm(a) for a in srcs] + [hbm(a) for a in lands] + [jax.ShapeDtypeStruct((8, 128), F32)],
        input_output_aliases={i: 2 + i for i in range(2 * n)},
        compiler_params=pltpu.CompilerParams(has_side_effects=_EFFECT),
    )(*[con(a) for a in srcs], *[con(a) for a in lands])
    return res[0], res[1], list(res[2:2 + n]), list(res[2 + n:2 + 2 * n]), res[-1]


def _xchg_wait(send, recv, srcs, lands, after, *, name, scatter):
    n = len(srcs)

    def body(*refs):
        s_ref, r_ref = refs[2 * n], refs[2 * n + 1]
        for cp in _xchg_copies(refs[:n], refs[n:2 * n], s_ref, r_ref, scatter):
            cp.wait_send()
            cp.wait_recv()

    hbm = lambda a: pltpu.HBM(a.shape, a.dtype)
    res = _pcall(
        body, name=name, in_specs=[_HBM] * (2 * n) + [_SEM, _SEM, _ANY],
        out_specs=[_HBM] * (2 * n),
        out_shape=[hbm(a) for a in srcs] + [hbm(a) for a in lands],
        input_output_aliases={i: i for i in range(2 * n)},
        compiler_params=pltpu.CompilerParams(has_side_effects=_EFFECT),
    )(*srcs, *lands, send, recv, after)
    return list(res[:n]), list(res[n:])


def _swap_cores(arrs, name="swap_cores"):
    n = len(arrs)

    def body(*refs):
        ins, outs = refs[:n], refs[n:2 * n]
        send, recv = refs[2 * n:]
        x, y, c = _place()
        cps = [pltpu.make_async_remote_copy(src_ref=ins[a], dst_ref=outs[a], send_sem=send.at[a], recv_sem=recv.at[a],
                                            device_id=(x, y, 1 - c), device_id_type=_MESH) for a in range(n)]
        for cp in cps:
            cp.start()
        for cp in cps:
            cp.wait_recv()
        for cp in cps:
            cp.wait_send()

    return _pcall(
        body, name=name, in_specs=[_ANY] * n, out_specs=[_ANY] * n,
        out_shape=[jax.ShapeDtypeStruct(s.shape, s.dtype) for s in arrs],
        scratch_shapes=[pltpu.SemaphoreType.DMA((n,)), pltpu.SemaphoreType.DMA((n,))],
    )(*arrs)


def _allreduce_small(v):
    rows = v.shape[0]

    def body(v_ref, sum_ref, all_ref, send, recv):
        x, y, c = _place()
        me = 4 * x + 2 * y + c
        flips = [(k >> 2 & 1, k >> 1 & 1, k & 1) for k in range(1, 8)]

        def peer(f):
            return tuple(1 - q if b else q for q, b in zip((x, y, c), f))

        all_ref[me] = v_ref[...]
        cps = [pltpu.make_async_remote_copy(src_ref=v_ref, dst_ref=all_ref.at[me], send_sem=send.at[k], recv_sem=recv.at[k],
                                            device_id=peer(f), device_id_type=_MESH) for k, f in enumerate(flips)]
        for cp in cps:
            cp.start()
        for cp in cps:
            cp.wait_recv()
        acc = all_ref[0]
        for j in range(1, 8):
            acc = acc + all_ref[j]
        sum_ref[...] = acc
        for cp in cps:
            cp.wait_send()

    vm = pl.BlockSpec(memory_space=pltpu.VMEM)
    return _pcall(
        body, name="allreduce_small", in_specs=[vm], out_specs=vm,
        out_shape=jax.ShapeDtypeStruct((rows, 128), F32),
        scratch_shapes=[pltpu.VMEM((8, rows, 128), F32), pltpu.SemaphoreType.DMA((7,)), pltpu.SemaphoreType.DMA((7,))],
        compiler_params=_cp(None, 40),
    )(v)


def _sum_slots(own, land):
    ns, rows, cols = land.shape
    tm = _row_tile(rows, 256, 8)

    def body(own_ref, a_ref, o_ref):
        o_ref[...] = ((own_ref[...] + a_ref[0]) + a_ref[1]) + a_ref[2]

    return _pcall(
        body, name="sum_slots", grid=(rows // tm,),
        in_specs=[pl.BlockSpec((tm, cols), lambda i: (i, 0)), pl.BlockSpec((ns, tm, cols), lambda i: (0, i, 0))],
        out_specs=pl.BlockSpec((tm, cols), lambda i: (i, 0)),
        out_shape=jax.ShapeDtypeStruct((rows, cols), F32),
        compiler_params=_cp(("parallel",), 40),
    )(own, land)


def _adamw(w, m, v, g0, g1=None):
    rows, cols = w.shape[-2:]
    lead = w.ndim == 3
    tm = _row_tile(rows, 256, 8)
    c1 = 1.0 - ADAM_B1 ** ADAM_STEP
    c2 = 1.0 - ADAM_B2 ** ADAM_STEP
    two = g1 is not None

    def body(*refs):
        w_ref, m_ref, v_ref, g0_ref = refs[:4]
        g_ref, d_ref, nm_ref, nv_ref = refs[-4:]
        g = g0_ref[...]
        if two:
            g = g + refs[4][...]
        nm = ADAM_B1 * m_ref[...] + (1.0 - ADAM_B1) * g
        nv = ADAM_B2 * v_ref[...] + (1.0 - ADAM_B2) * (g * g)
        g_ref[...] = g
        nm_ref[...] = nm
        nv_ref[...] = nv
        d_ref[...] = -ADAM_LR * ((nm / c1) / (jnp.sqrt(nv / c2) + ADAM_EPS) + ADAM_WD * w_ref[...])

    blk = pl.BlockSpec((tm, cols), lambda i: (i, 0))
    wblk = pl.BlockSpec((None, tm, cols), lambda i: (0, i, 0)) if lead else blk
    ins = [w, m, v, g0] + ([g1] if two else [])
    return _pcall(
        body, name="adamw", grid=(rows // tm,), in_specs=[wblk] * 3 + [blk] * (len(ins) - 3), out_specs=[wblk] * 4,
        out_shape=[jax.ShapeDtypeStruct(w.shape, F32)] * 4,
        compiler_params=_cp(("parallel",), 40),
    )(*ins)


_BIG = ("w_in", "s5_w_glu", "m_w_out", "w_o", "w_up", "w_down")
_SMALL = ("ln0_g", "ln0_b", "b_in", "qk_conv_b", "s5_lambda_re", "s5_lambda_im", "s5_log_dt", "s5_b_re", "s5_b_im",
          "s5_c_re", "s5_c_im", "s5_d", "m_norm_g", "ln1_g", "ln1_b", "b_up", "ln2_g", "ln2_b")
_SMALL_SHARDED = ("meta_tokens", "qk_conv_w")
_ORDER = ("meta_tokens", "ln0_g", "ln0_b", "w_in", "b_in", "qk_conv_w", "qk_conv_b", "s5_lambda_re", "s5_lambda_im",
          "s5_log_dt", "s5_b_re", "s5_b_im", "s5_c_re", "s5_c_im", "s5_d", "s5_w_glu", "m_norm_g", "m_w_out", "w_o",
          "ln1_g", "ln1_b", "w_up", "b_up", "w_down", "ln2_g", "ln2_b")


def _pack(arrs):
    flat = jnp.concatenate([a.reshape(-1) for a in arrs])
    n = flat.shape[0]
    rows = -(-n // 1024) * 8
    return jnp.pad(flat, (0, rows * 128 - n)).reshape(rows, 128)


def _unpack(packed, shapes):
    flat = packed.reshape(-1)
    out, off = [], 0
    for s in shapes:
        n = math.prod(s)
        out.append(flat[off:off + n].reshape(s))
        off += n
    return out


def kernel(x, meta_tokens, ln0_g, ln0_b, w_in, b_in, qk_conv_w, qk_conv_b, s5_lambda_re, s5_lambda_im, s5_log_dt, s5_b_re, s5_b_im, s5_c_re, s5_c_im, s5_d, s5_w_glu, m_norm_g, m_w_out, w_o, ln1_g, ln1_b, w_up, b_up, w_down, ln2_g, ln2_b, loss_target, m_meta_tokens, m_ln0_g, m_ln0_b, m_w_in, m_b_in, m_qk_conv_w, m_qk_conv_b, m_s5_lambda_re, m_s5_lambda_im, m_s5_log_dt, m_s5_b_re, m_s5_b_im, m_s5_c_re, m_s5_c_im, m_s5_d, m_s5_w_glu, m_m_norm_g, m_m_w_out, m_w_o, m_ln1_g, m_ln1_b, m_w_up, m_b_up, m_w_down, m_ln2_g, m_ln2_b, v_meta_tokens, v_ln0_g, v_ln0_b, v_w_in, v_b_in, v_qk_conv_w, v_qk_conv_b, v_s5_lambda_re, v_s5_lambda_im, v_s5_log_dt, v_s5_b_re, v_s5_b_im, v_s5_c_re, v_s5_c_im, v_s5_d, v_s5_w_glu, v_m_norm_g, v_m_w_out, v_w_o, v_ln1_g, v_ln1_b, v_w_up, v_b_up, v_w_down, v_ln2_g, v_ln2_b):
    wts = dict(meta_tokens=meta_tokens, ln0_g=ln0_g, ln0_b=ln0_b, w_in=w_in, b_in=b_in, qk_conv_w=qk_conv_w,
               qk_conv_b=qk_conv_b, s5_lambda_re=s5_lambda_re, s5_lambda_im=s5_lambda_im, s5_log_dt=s5_log_dt,
               s5_b_re=s5_b_re, s5_b_im=s5_b_im, s5_c_re=s5_c_re, s5_c_im=s5_c_im, s5_d=s5_d, s5_w_glu=s5_w_glu,
               m_norm_g=m_norm_g, m_w_out=m_w_out, w_o=w_o, ln1_g=ln1_g, ln1_b=ln1_b, w_up=w_up, b_up=b_up,
               w_down=w_down, ln2_g=ln2_g, ln2_b=ln2_b)
    mom = dict(meta_tokens=m_meta_tokens, ln0_g=m_ln0_g, ln0_b=m_ln0_b, w_in=m_w_in, b_in=m_b_in, qk_conv_w=m_qk_conv_w,
               qk_conv_b=m_qk_conv_b, s5_lambda_re=m_s5_lambda_re, s5_lambda_im=m_s5_lambda_im, s5_log_dt=m_s5_log_dt,
               s5_b_re=m_s5_b_re, s5_b_im=m_s5_b_im, s5_c_re=m_s5_c_re, s5_c_im=m_s5_c_im, s5_d=m_s5_d,
               s5_w_glu=m_s5_w_glu, m_norm_g=m_m_norm_g, m_w_out=m_m_w_out, w_o=m_w_o, ln1_g=m_ln1_g, ln1_b=m_ln1_b,
               w_up=m_w_up, b_up=m_b_up, w_down=m_w_down, ln2_g=m_ln2_g, ln2_b=m_ln2_b)
    var = dict(meta_tokens=v_meta_tokens, ln0_g=v_ln0_g, ln0_b=v_ln0_b, w_in=v_w_in, b_in=v_b_in, qk_conv_w=v_qk_conv_w,
               qk_conv_b=v_qk_conv_b, s5_lambda_re=v_s5_lambda_re, s5_lambda_im=v_s5_lambda_im, s5_log_dt=v_s5_log_dt,
               s5_b_re=v_s5_b_re, s5_b_im=v_s5_b_im, s5_c_re=v_s5_c_re, s5_c_im=v_s5_c_im, s5_d=v_s5_d,
               s5_w_glu=v_s5_w_glu, m_norm_g=v_m_norm_g, m_w_out=v_m_w_out, w_o=v_w_o, ln1_g=v_ln1_g, ln1_b=v_ln1_b,
               w_up=v_w_up, b_up=v_b_up, w_down=v_w_down, ln2_g=v_ln2_g, ln2_b=v_ln2_b)
    d = x.shape[-1]
    chip = 2 * lax.axis_index("x") + lax.axis_index("y")

    first = ("w_in",) + _SMALL_SHARDED
    gw = dict(zip(first, _gather_chips([_bf(w_in[0]), meta_tokens, qk_conv_w[0]])))
    late_names = tuple(n for n in _BIG if n != "w_in")
    late_src = [_bf(wts[n][0]) for n in late_names]
    lsend, lrecv, late_src, late_land, ltok = _xchg_start(
        late_src, [lax.empty((4,) + a.shape, a.dtype) for a in late_src], name="gather_late_start", scatter=False)
    cat = lambda a: jnp.transpose(a, (1, 0, 2)).reshape(a.shape[1], 4 * a.shape[2])
    w = dict(
        meta_tokens=cat(gw["meta_tokens"]), ln0_g=ln0_g[None], ln0_b=_tie(ln0_b[None], ltok),
        w_in=_w_in_from_slots(gw["w_in"]), b_in=_to_pad_cols(b_in),
        qk_conv_w=cat(gw["qk_conv_w"]), qk_conv_b=qk_conv_b,
        s5_lambda_re=s5_lambda_re[0], s5_lambda_im=s5_lambda_im[0], s5_log_dt=s5_log_dt[0][:, None],
        s5_b_re=s5_b_re[0], s5_b_im=s5_b_im[0], s5_c_re=s5_c_re[0], s5_c_im=s5_c_im[0], s5_d=s5_d,
        m_norm_g=m_norm_g, ln1_g=ln1_g, ln1_b=ln1_b, b_up=b_up, ln2_g=ln2_g, ln2_b=ln2_b)

    def late(after):
        src, land = _xchg_wait(lsend, lrecv, late_src, late_land, after, name="gather_late_wait", scatter=False)
        full = {n: lax.dynamic_update_slice(ld, s[None], (chip,) + (0,) * s.ndim)
                for n, s, ld in zip(late_names, src, land)}
        return dict(s5_w_glu=full["s5_w_glu"], m_w_out=full["m_w_out"].reshape(d, d), w_o=full["w_o"].reshape(d, d),
                    w_up=full["w_up"], w_down=full["w_down"].reshape(4 * d, d))

    flying = []

    def ready(names, g):
        parts = dict(
            w_in=lambda: _slots_from_w_in(g["w_in"][0]), s5_w_glu=lambda: g["s5_w_glu"],
            m_w_out=lambda: g["m_w_out"].reshape(4, d // 4, d), w_o=lambda: g["w_o"].reshape(4, d // 4, d),
            w_up=lambda: g["w_up"], w_down=lambda: g["w_down"].reshape(4, d, d))
        src = [parts[n]() for n in names]
        land = [lax.empty((3,) + a.shape[1:], a.dtype) for a in src]
        st = _xchg_start(src, land, name="scatter_" + names[0] + "_start", scatter=True)
        flying.append((names,) + st[:4])
        return st[4]

    loss, grad_x, g = _local_step(x, loss_target, w, late, ready)
    g["b_in"] = _from_pad_cols(g["b_in"])

    res = {}

    def finish(groups, after, tag):
        mine = {}
        for names, send, recv, src, land in groups:
            src, land = _xchg_wait(send, recv, src, land, after, name="scatter_" + names[0] + "_wait", scatter=True)
            for n, s, ld in zip(names, src, land):
                mine[n] = _sum_slots(lax.dynamic_index_in_dim(s, chip, 0, keepdims=False), ld)
        theirs = _swap_cores(list(mine.values()), name="swap_cores_" + tag)
        for n, t in zip(mine, theirs):
            res[n] = _adamw(wts[n], mom[n], var[n], mine[n], t)

    finish(flying[:-1], g["ln0_g"], "a")

    small_shapes = [(1, 128)] + [wts[n].shape for n in _SMALL] + [g[n].shape for n in _SMALL_SHARDED]
    packed = _pack([loss] + [g[n] for n in _SMALL] + [g[n] for n in _SMALL_SHARDED])
    tot = _unpack(_allreduce_small(packed), small_shapes)
    loss_out = tot[0][0, 0]
    gsm = dict(zip(_SMALL + _SMALL_SHARDED, tot[1:]))
    for n in _SMALL_SHARDED:
        cols = wts[n].shape[-1]
        gsm[n] = lax.dynamic_slice_in_dim(gsm[n], chip * cols, cols, axis=1).reshape(wts[n].shape)

    names = _SMALL + _SMALL_SHARDED
    shapes = [wts[n].shape for n in names]
    pk = lambda dct: _pack([dct[n] for n in names])
    small_out = _adamw(pk(wts), pk(mom), pk(var), pk(gsm))
    small_res = [_unpack(r, shapes) for r in small_out]
    for j, n in enumerate(names):
        res[n] = [small_res[q][j] for q in range(4)]
    finish(flying[-1:], small_out[0], "b")

    return (loss_out, grad_x, *[res[n][0] for n in _ORDER], *[res[n][1] for n in _ORDER],
            *[res[n][2] for n in _ORDER], *[res[n][3] for n in _ORDER])
```

```python
import functools
import math

import jax
import jax.numpy as jnp
from jax import lax
from jax.experimental import pallas as pl
from jax.experimental.pallas import tpu as pltpu

F32 = jnp.float32
BF16 = jnp.bfloat16
HI = lax.Precision.HIGHEST

N_META = 16
M_HEADS = 4
M_CHUNK = 64
PAD = M_CHUNK - N_META
CONV_W = 4
S5_GROUP = 16
S5_STATE = 64
S5_KCH = 4
LN_EPS = 1e-5
ALPHA = 2.0 ** 0.25
NEG = -1e30
ADAM_LR, ADAM_B1, ADAM_B2, ADAM_EPS, ADAM_WD, ADAM_STEP = 0.001, 0.9, 0.999, 1e-08, 0.01, 10

O_OFF, GS_OFF, GM_OFF, V_OFF, Q_OFF, K_OFF, U_OFF, G_OFF, NP = 0, 1024, 2048, 3072, 4096, 4608, 5120, 5632, 5760

NN = ((1,), (0,))
NT = ((1,), (1,))
TN = ((0,), (0,))


def _dot(a, b, dims=NN, prec=None):
    return lax.dot_general(a, b, (dims, ((), ())), preferred_element_type=F32, precision=prec)


def _bf(x):
    return x.astype(BF16)


def _sig(x):
    return 1.0 / (1.0 + jnp.exp(-x))


def _pcall(body, **kw):
    return pl.pallas_call(body, **kw)


def _cp(sem=None, vmem_mb=None):
    kw = {}
    if sem is not None:
        kw["dimension_semantics"] = sem
    if vmem_mb is not None:
        kw["vmem_limit_bytes"] = vmem_mb << 20
    return pltpu.CompilerParams(**kw)


def _row_tile(n, want, mult=16):
    best = None
    for t in range(mult, want + 1, mult):
        if n % t == 0:
            best = t
    assert best is not None, (n, want)
    return best


def _resident(shape):
    nd = len(shape)
    return pl.BlockSpec(shape, lambda *_: (0,) * nd, pipeline_mode=pl.Buffered(1))


def _const(shape):
    nd = len(shape)
    return pl.BlockSpec(shape, lambda *_: (0,) * nd)


def _ln_fwd(x, g, b):
    mu = jnp.mean(x, axis=-1, keepdims=True)
    xc = x - mu
    var = jnp.mean(xc * xc, axis=-1, keepdims=True)
    rstd = lax.rsqrt(var + LN_EPS)
    xhat = xc * rstd
    return xhat * g + b, xhat, rstd


def _ln_bwd(dy, xhat, rstd, g):
    dxh = dy * g
    m1 = jnp.mean(dxh, axis=-1, keepdims=True)
    m2 = jnp.mean(dxh * xhat, axis=-1, keepdims=True)
    return rstd * (dxh - m1 - xhat * m2)


def _colsum(x):
    return jnp.sum(x, axis=0, keepdims=True)


def _to_pad_cols(w):
    u, q, k, v, o, gi, gf, gs, gm = (w[..., 0:512], w[..., 512:1024], w[..., 1024:1536], w[..., 1536:2560],
                                     w[..., 2560:3584], w[..., 3584:3588], w[..., 3588:3592], w[..., 3592:4616],
                                     w[..., 4616:5640])
    z = jnp.zeros(w.shape[:-1] + (NP - G_OFF - 8,), w.dtype)
    return jnp.concatenate([o, gs, gm, v, q, k, u, gi, gf, z], axis=-1)


def _from_pad_cols(w):
    o, gs, gm, v, q, k, u = (w[..., O_OFF:GS_OFF], w[..., GS_OFF:GM_OFF], w[..., GM_OFF:V_OFF], w[..., V_OFF:Q_OFF],
                             w[..., Q_OFF:K_OFF], w[..., K_OFF:U_OFF], w[..., U_OFF:G_OFF])
    gi, gf = w[..., G_OFF:G_OFF + 4], w[..., G_OFF + 4:G_OFF + 8]
    return jnp.concatenate([u, q, k, v, o, gi, gf, gs, gm], axis=-1)


_IN_REF = (("u", 512), ("q", 512), ("k", 512), ("v", 1024), ("o", 1024), ("i", 4), ("f", 4), ("gs", 1024), ("gm", 1024))
_IN_PAD = (("o", O_OFF), ("gs", GS_OFF), ("gm", GM_OFF), ("v", V_OFF), ("q", Q_OFF), ("k", K_OFF), ("u", U_OFF),
           ("i", G_OFF), ("f", G_OFF + 4))


def _in_ref_ranges():
    out, off = {}, 0
    for n, s in _IN_REF:
        out[n] = (off, off + s)
        off += s
    return out, off


def _w_in_from_slots(g):
    rng, total = _in_ref_ranges()
    width = total // g.shape[0]
    cols = []
    for n, _ in _IN_PAD:
        a, b = rng[n]
        while a < b:
            s = a // width
            e = min(b, (s + 1) * width)
            cols.append(g[s][:, a - s * width:e - s * width])
            a = e
    cols.append(jnp.zeros((g.shape[1], NP - G_OFF - 8), g.dtype))
    return jnp.concatenate(cols, axis=1)


def _slots_from_w_in(wp, nslot=4):
    rng, total = _in_ref_ranges()
    width = total // nslot
    pad_off = dict(_IN_PAD)
    slots = []
    for s in range(nslot):
        lo, hi = s * width, (s + 1) * width
        cols = []
        for n, _ in _IN_REF:
            a, b = rng[n]
            x0, x1 = max(a, lo), min(b, hi)
            if x0 < x1:
                cols.append(wp[:, pad_off[n] + x0 - a:pad_off[n] + x1 - a])
        slots.append(jnp.concatenate(cols, axis=1))
    return jnp.stack(slots, axis=0)


def _ln0_fwd(hin, g, b, lp):
    r, d = hin.shape
    tm = _row_tile(lp, 416)

    def body(x_ref, g_ref, b_ref, o_ref):
        y, _, _ = _ln_fwd(x_ref[...], g_ref[...], b_ref[...])
        o_ref[...] = y

    return _pcall(
        body, name="ln0_fwd", grid=(r // tm,),
        in_specs=[pl.BlockSpec((tm, d), lambda i: (i, 0)), _const((1, d)), _const((1, d))],
        out_specs=pl.BlockSpec((tm, d), lambda i: (i, 0)),
        out_shape=jax.ShapeDtypeStruct((r, d), F32),
        compiler_params=_cp(("parallel",)),
    )(hin, g, b)


def _ln0_bwd(hin, dr1, dpw, g, lp):
    r, d = hin.shape
    tm = _row_tile(lp, 416)
    tps = lp // tm
    assert tm >= PAD + N_META

    def body(x_ref, a_ref, c_ref, g_ref, o_ref, dg_ref, db_ref, dm_ref):
        i = pl.program_id(0)

        @pl.when(i == 0)
        def _():
            dg_ref[...] = jnp.zeros_like(dg_ref)
            db_ref[...] = jnp.zeros_like(db_ref)
            dm_ref[...] = jnp.zeros_like(dm_ref)

        dy = ALPHA * a_ref[...] + c_ref[...]
        _, xhat, rstd = _ln_fwd(x_ref[...], g_ref[...], 0.0)
        dx = _ln_bwd(dy, xhat, rstd, g_ref[...])
        o_ref[...] = dx
        dg_ref[...] += _colsum(dy * xhat)
        db_ref[...] += _colsum(dy)

        @pl.when(i % tps == 0)
        def _():
            dm_ref[...] += dx[PAD:PAD + N_META, :]

    return _pcall(
        body, name="ln0_bwd", grid=(r // tm,),
        in_specs=[pl.BlockSpec((tm, d), lambda i: (i, 0))] * 3 + [_const((1, d))],
        out_specs=[pl.BlockSpec((tm, d), lambda i: (i, 0)), _const((1, d)), _const((1, d)), _const((N_META, d))],
        out_shape=[jax.ShapeDtypeStruct((r, d), F32), jax.ShapeDtypeStruct((1, d), F32),
                   jax.ShapeDtypeStruct((1, d), F32), jax.ShapeDtypeStruct((N_META, d), F32)],
        compiler_params=_cp(("arbitrary",)),
    )(hin, dr1, dpw, g)


def _inproj(h0, w_bf, bias, lp):
    r, d = h0.shape
    n = w_bf.shape[1]
    tm = _row_tile(lp, 832)
    tn = 1152
    tps = lp // tm

    def body(a_ref, w_ref, b_ref, o_ref):
        i = pl.program_id(0)
        acc = _dot(_bf(a_ref[...]), w_ref[...]) + b_ref[...]
        t = (i % tps) * tm + lax.broadcasted_iota(jnp.int32, (tm, 1), 0)
        o_ref[...] = jnp.where(t >= PAD, acc, 0.0)

    return _pcall(
        body, name="inproj", grid=(r // tm, n // tn),
        in_specs=[pl.BlockSpec((tm, d), lambda i, j: (i, 0)), pl.BlockSpec((d, tn), lambda i, j: (0, j)),
                  pl.BlockSpec((1, tn), lambda i, j: (0, j))],
        out_specs=pl.BlockSpec((tm, tn), lambda i, j: (i, j)),
        out_shape=jax.ShapeDtypeStruct((r, n), F32),
        compiler_params=_cp(("parallel", "parallel"), 48),
    )(h0, w_bf, bias)


def _mm_tn(a, b, *, name, split=1, colsum=False, tk_want=832):
    r, m = a.shape
    n = b.shape[1]
    tk = _row_tile(r, tk_want)
    tm = min(m, 1024)
    ns = n // split
    tn = ns
    for cand in (1024, 1152, 640, 512, 128):
        if ns % cand == 0 and cand <= ns:
            tn = cand
            break
    nb = ns // tn
    nk = r // tk

    def body(a_ref, b_ref, o_ref, *rest):
        acc = rest[-1]
        k = pl.program_id(2)

        @pl.when(k == 0)
        def _():
            acc[...] = jnp.zeros_like(acc)

        bt = b_ref[...]
        acc[...] += _dot(_bf(a_ref[...]), _bf(bt), TN)

        @pl.when(k == nk - 1)
        def _():
            o_ref[...] = acc[...]

        if colsum:
            cs_ref = rest[0]

            @pl.when(k == 0)
            def _():
                cs_ref[...] = jnp.zeros_like(cs_ref)

            cs_ref[...] += _colsum(bt.astype(F32))

    out_specs = [pl.BlockSpec((None, tm, tn), lambda i, j, k: (j // nb, i, j % nb))]
    out_shape = [jax.ShapeDtypeStruct((split, m, ns), F32)]
    if colsum:
        assert m == tm
        out_specs.append(pl.BlockSpec((1, tn), lambda i, j, k: (0, j)))
        out_shape.append(jax.ShapeDtypeStruct((1, n), F32))
    res = _pcall(
        body, name=name, grid=(m // tm, n // tn, nk),
        in_specs=[pl.BlockSpec((tk, tm), lambda i, j, k: (k, i)), pl.BlockSpec((tk, tn), lambda i, j, k: (k, j))],
        out_specs=out_specs, out_shape=out_shape,
        scratch_shapes=[pltpu.VMEM((tm, tn), F32)],
        compiler_params=_cp(("parallel", "parallel", "arbitrary"), 48),
    )(a, b)
    return res if colsum else res[0]


def _mm_nt(a, w_bf, lp, *, name, dep=None):
    r, kdim = a.shape
    n = w_bf.shape[0]
    tm = _row_tile(lp, 832)
    tk = 1152
    nk = kdim // tk
    deps = [] if dep is None else [dep]

    def body(a_ref, w_ref, *rest):
        o_ref, acc = rest[-2:]
        k = pl.program_id(1)

        @pl.when(k == 0)
        def _():
            acc[...] = jnp.zeros_like(acc)

        acc[...] += _dot(_bf(a_ref[...]), w_ref[...], NT)

        @pl.when(k == nk - 1)
        def _():
            o_ref[...] = acc[...]

    return _pcall(
        body, name=name, grid=(r // tm, nk),
        in_specs=[pl.BlockSpec((tm, tk), lambda i, k: (i, k)), pl.BlockSpec((n, tk), lambda i, k: (0, k))]
        + [_const(dp_.shape) for dp_ in deps],
        out_specs=pl.BlockSpec((tm, n), lambda i, k: (i, 0)),
        out_shape=jax.ShapeDtypeStruct((r, n), F32),
        scratch_shapes=[pltpu.VMEM((tm, n), F32)],
        compiler_params=_cp(("parallel", "arbitrary"), 48),
    )(a, w_bf, *deps)


def _s5_prep(lam_re, lam_im, log_dt, b_re_t, b_im_t):
    g, p = lam_re.shape
    h = b_re_t.shape[0]

    def body(lr_ref, li_ref, ldt_ref, br_ref, bi_ref, pr_ref, pi_ref, bbr_ref, bbi_ref):
        lr, li = lr_ref[...], li_ref[...]
        dt = jnp.exp(ldt_ref[...])
        e = jnp.exp(lr * dt)
        ar, ai = e * jnp.cos(li * dt), e * jnp.sin(li * dt)
        den = lr * lr + li * li
        cr = ((ar - 1.0) * lr + ai * li) / den
        ci = (ai * lr - (ar - 1.0) * li) / den
        br, bi = br_ref[...], bi_ref[...]
        bbr_ref[...] = cr[None] * br - ci[None] * bi
        bbi_ref[...] = cr[None] * bi + ci[None] * br
        xr, xi = ar, ai
        pr_ref[0] = xr
        pi_ref[0] = xi
        for t in range(1, 8):
            xr, xi = xr * ar - xi * ai, xr * ai + xi * ar
            pr_ref[t] = xr
            pi_ref[t] = xi

    sd = jax.ShapeDtypeStruct
    return _pcall(body, name="s5_prep",
                  out_shape=[sd((8, g, p), F32), sd((8, g, p), F32), sd((h, g, p), F32), sd((h, g, p), F32)])(
        lam_re, lam_im, log_dt, b_re_t, b_im_t)


def _s5_prep_bwd(lam_re, lam_im, log_dt, b_re_t, b_im_t, da_re, da_im, dbb_re_t, dbb_im_t):
    g, p = lam_re.shape
    h = b_re_t.shape[0]

    def body(lr_ref, li_ref, ldt_ref, br_ref, bi_ref, dar_ref, dai_ref, dbr_ref, dbi_ref,
             glr_ref, gli_ref, gdt_ref, gbr_ref, gbi_ref):
        lr, li = lr_ref[...], li_ref[...]
        dt = jnp.exp(ldt_ref[...])
        e = jnp.exp(lr * dt)
        ar, ai = e * jnp.cos(li * dt), e * jnp.sin(li * dt)
        den = lr * lr + li * li
        cr = ((ar - 1.0) * lr + ai * li) / den
        ci = (ai * lr - (ar - 1.0) * li) / den
        br, bi = br_ref[...], bi_ref[...]
        gr, gi = dbr_ref[...], dbi_ref[...]
        gbr_ref[...] = gr * cr[None] + gi * ci[None]
        gbi_ref[...] = gi * cr[None] - gr * ci[None]
        gcr = jnp.sum(gr * br + gi * bi, axis=0)
        gci = jnp.sum(gi * br - gr * bi, axis=0)
        ilr, ili = lr / den, -li / den
        gar = dar_ref[...] + gcr * ilr + gci * ili
        gai = dai_ref[...] + gci * ilr - gcr * ili
        qr, qi = cr * ilr - ci * ili, cr * ili + ci * ilr
        glr = -(gcr * qr + gci * qi)
        gli = -(gci * qr - gcr * qi)
        gzr = gar * ar + gai * ai
        gzi = gai * ar - gar * ai
        glr_ref[...] = glr + gzr * dt
        gli_ref[...] = gli + gzi * dt
        gdt_ref[...] = jnp.sum(gzr * lr + gzi * li, axis=1, keepdims=True) * dt

    sd = jax.ShapeDtypeStruct
    return _pcall(body, name="s5_prep_bwd",
                  out_shape=[sd((g, p), F32), sd((g, p), F32), sd((g, 1), F32), sd((h, g, p), F32), sd((h, g, p), F32)])(
        lam_re, lam_im, log_dt, b_re_t, b_im_t, da_re, da_im, dbb_re_t, dbb_im_t)


def _cmul(xr, xi, yr, yi):
    return xr * yr - xi * yi, xr * yi + xi * yr


def _dot5(a, b, dims=NN):
    return _dot(_bf(a), _bf(b), dims)


def _s5_fwd(p3, bk, cre, cim, apow, dskip):
    bsz, lp, _ = p3.shape
    tt = _row_tile(lp, 520, 8)
    nt = lp // tt
    nblk = tt // 8
    hw = 512

    def body(u_ref, bk_ref, cre_ref, cim_ref, ap_ref, d_ref, y_ref, xs_ref, car_ref):
        t = pl.program_id(2)

        @pl.when(t == 0)
        def _():
            car_ref[...] = jnp.zeros_like(car_ref)

        u = u_ref[...]
        xs_ref[...] = _dot5(u, bk_ref[...])
        ap = ap_ref[...]
        apr, api = ap[:, :hw], ap[:, hw:]
        rows = lax.broadcasted_iota(jnp.int32, (8, hw), 0)

        def blk(i, carry):
            cr, ci = carry
            off = pl.multiple_of(i * 8, 8)
            x = xs_ref[pl.ds(off, 8), :]
            xr, xi = x[:, :hw], x[:, hw:]
            for d in (1, 2, 4):
                sr = jnp.where(rows < d, 0.0, pltpu.roll(xr, d, 0))
                si = jnp.where(rows < d, 0.0, pltpu.roll(xi, d, 0))
                mr, mi = _cmul(sr, si, apr[d - 1:d, :], api[d - 1:d, :])
                xr, xi = xr + mr, xi + mi
            mr, mi = _cmul(apr, api, cr, ci)
            xr, xi = xr + mr, xi + mi
            xs_ref[pl.ds(off, 8), :] = jnp.concatenate([xr, xi], axis=1)
            return xr[7:8, :], xi[7:8, :]

        c0 = car_ref[...]
        cr, ci = lax.fori_loop(0, nblk, blk, (c0[0:1, :hw], c0[0:1, hw:]))
        car_ref[...] = jnp.broadcast_to(jnp.concatenate([cr, ci], axis=1), car_ref.shape)
        xs = xs_ref[...]
        y_ref[...] = (_dot5(xs[:, :hw], cre_ref[...]) - _dot5(xs[:, hw:], cim_ref[...])
                      + d_ref[...] * u)

    ub = U_OFF // 128
    return _pcall(
        body, name="s5_fwd", grid=(S5_KCH, bsz, nt),
        in_specs=[pl.BlockSpec((None, tt, 128), lambda k, b, t: (b, t, ub + k)),
                  pl.BlockSpec((None, 128, 2 * hw), lambda k, b, t: (k, 0, 0)),
                  pl.BlockSpec((None, hw, 128), lambda k, b, t: (k, 0, 0)),
                  pl.BlockSpec((None, hw, 128), lambda k, b, t: (k, 0, 0)),
                  pl.BlockSpec((None, 8, 2 * hw), lambda k, b, t: (k, 0, 0)),
                  pl.BlockSpec((1, 128), lambda k, b, t: (0, k))],
        out_specs=[pl.BlockSpec((None, tt, 128), lambda k, b, t: (b, t, k)),
                   pl.BlockSpec((None, None, tt, 2 * hw), lambda k, b, t: (b, k, t, 0))],
        out_shape=[jax.ShapeDtypeStruct((bsz, lp, S5_KCH * 128), F32),
                   jax.ShapeDtypeStruct((bsz, S5_KCH, lp, 2 * hw), F32)],
        scratch_shapes=[pltpu.VMEM((8, 2 * hw), F32)],
        compiler_params=_cp(("parallel", "parallel", "arbitrary"), 40),
    )(p3, bk, cre, cim, apow, dskip)


def _s5_bwd(dp3, p3, dy3, xs, bk, cre, cim, apow_rev, dskip):
    bsz, lp, _ = p3.shape
    tt = _row_tile(lp, 520, 8)
    nt = lp // tt
    nblk = tt // 8
    hw = 512
    tb = tt // 8

    def body(dp_any, u_ref, dy_ref, xs_ref, halo_ref, bk_ref, cre_ref, cim_ref, ap_ref, d_ref,
             du_ref, dbk_ref, dcre_ref, dcim_ref, da_ref, dd_ref, g_ref, ext_ref, car_ref):
        b = pl.program_id(1)
        t = pl.program_id(2)
        tidx = nt - 1 - t

        @pl.when(t == 0)
        def _():
            car_ref[...] = jnp.zeros_like(car_ref)

        @pl.when((b == 0) & (t == 0))
        def _():
            dbk_ref[...] = jnp.zeros_like(dbk_ref)
            dcre_ref[...] = jnp.zeros_like(dcre_ref)
            dcim_ref[...] = jnp.zeros_like(dcim_ref)
            da_ref[...] = jnp.zeros_like(da_ref)
            dd_ref[...] = jnp.zeros_like(dd_ref)

        u = u_ref[...]
        dy = dy_ref[...]
        g_ref[:, :hw] = _dot5(dy, cre_ref[...], NT)
        g_ref[:, hw:] = -_dot5(dy, cim_ref[...], NT)
        ap = ap_ref[...]
        apr, api = ap[:, :hw], -ap[:, hw:]
        rows = lax.broadcasted_iota(jnp.int32, (8, hw), 0)

        def blk(i, carry):
            cr, ci = carry
            off = pl.multiple_of((nblk - 1 - i) * 8, 8)
            x = g_ref[pl.ds(off, 8), :]
            xr, xi = x[:, :hw], x[:, hw:]
            for d in (1, 2, 4):
                sr = jnp.where(rows >= 8 - d, 0.0, pltpu.roll(xr, 8 - d, 0))
                si = jnp.where(rows >= 8 - d, 0.0, pltpu.roll(xi, 8 - d, 0))
                mr, mi = _cmul(sr, si, apr[8 - d:9 - d, :], api[8 - d:9 - d, :])
                xr, xi = xr + mr, xi + mi
            mr, mi = _cmul(apr, api, cr, ci)
            xr, xi = xr + mr, xi + mi
            g_ref[pl.ds(off, 8), :] = jnp.concatenate([xr, xi], axis=1)
            return xr[0:1, :], xi[0:1, :]

        c0 = car_ref[...]
        cr, ci = lax.fori_loop(0, nblk, blk, (c0[0:1, :hw], c0[0:1, hw:]))
        car_ref[...] = jnp.broadcast_to(jnp.concatenate([cr, ci], axis=1), car_ref.shape)

        gg = g_ref[...]
        du = _dot5(gg, bk_ref[...], NT) + d_ref[...] * dy
        trow = tidx * tt + lax.broadcasted_iota(jnp.int32, (tt, 1), 0)
        du_ref[...] = jnp.where(trow >= PAD, du, 0.0)
        dbk_ref[...] += _dot5(u, gg, TN)
        xsv = xs_ref[...]
        dcre_ref[...] += _dot5(xsv[:, :hw], dy, TN)
        dcim_ref[...] -= _dot5(xsv[:, hw:], dy, TN)
        dd_ref[...] += _colsum(dy * u)
        ext_ref[0:8, :] = jnp.where(tidx == 0, 0.0, halo_ref[...])
        ext_ref[8:, :] = xsv
        xp = ext_ref[pl.ds(7, tt), :]
        gr, gi, pr, pi = gg[:, :hw], gg[:, hw:], xp[:, :hw], xp[:, hw:]
        da_ref[:, :hw] += _colsum(gr * pr + gi * pi)
        da_ref[:, hw:] += _colsum(gi * pr - gr * pi)

    ub = U_OFF // 128
    sd = jax.ShapeDtypeStruct
    rt = lambda t: nt - 1 - t
    res = _pcall(
        body, name="s5_bwd", grid=(S5_KCH, bsz, nt),
        in_specs=[pl.BlockSpec(memory_space=pl.ANY),
                  pl.BlockSpec((None, tt, 128), lambda k, b, t: (b, rt(t), ub + k)),
                  pl.BlockSpec((None, tt, 128), lambda k, b, t: (b, rt(t), k)),
                  pl.BlockSpec((None, None, tt, 2 * hw), lambda k, b, t: (b, k, rt(t), 0)),
                  pl.BlockSpec((None, None, 8, 2 * hw), lambda k, b, t: (b, k, jnp.maximum(rt(t) * tb - 1, 0), 0)),
                  pl.BlockSpec((None, 128, 2 * hw), lambda k, b, t: (k, 0, 0)),
                  pl.BlockSpec((None, hw, 128), lambda k, b, t: (k, 0, 0)),
                  pl.BlockSpec((None, hw, 128), lambda k, b, t: (k, 0, 0)),
                  pl.BlockSpec((None, 8, 2 * hw), lambda k, b, t: (k, 0, 0)),
                  pl.BlockSpec((1, 128), lambda k, b, t: (0, k))],
        out_specs=[pl.BlockSpec((None, tt, 128), lambda k, b, t: (b, rt(t), ub + k)),
                   pl.BlockSpec((None, 128, 2 * hw), lambda k, b, t: (k, 0, 0)),
                   pl.BlockSpec((None, hw, 128), lambda k, b, t: (k, 0, 0)),
                   pl.BlockSpec((None, hw, 128), lambda k, b, t: (k, 0, 0)),
                   pl.BlockSpec((None, 1, 2 * hw), lambda k, b, t: (k, 0, 0)),
                   pl.BlockSpec((1, 128), lambda k, b, t: (0, k))],
        out_shape=[sd(dp3.shape, F32), sd((S5_KCH, 128, 2 * hw), F32), sd((S5_KCH, hw, 128), F32),
                   sd((S5_KCH, hw, 128), F32), sd((S5_KCH, 1, 2 * hw), F32), sd((1, S5_KCH * 128), F32)],
        scratch_shapes=[pltpu.VMEM((tt, 2 * hw), F32), pltpu.VMEM((tt + 8, 2 * hw), F32), pltpu.VMEM((8, 2 * hw), F32)],
        input_output_aliases={0: 0},
        compiler_params=_cp(("arbitrary", "arbitrary", "arbitrary"), 48),
    )(dp3, p3, dy3, xs, xs, bk, cre, cim, apow_rev, dskip)
    return res


_G0 = math.sqrt(2.0 / math.pi)
_G1 = 0.044715


def _gelu(y):
    return 0.5 * y * (1.0 + jnp.tanh(_G0 * (y + _G1 * y * y * y)))


def _gelu_grad(y):
    th = jnp.tanh(_G0 * (y + _G1 * y * y * y))
    return 0.5 * (1.0 + th) + 0.5 * y * (1.0 - th * th) * _G0 * (1.0 + 3.0 * _G1 * y * y)


def _glu_fwd(y_s5, wglu_g, lp):
    r, w = y_s5.shape
    tm = _row_tile(lp, 416)
    cw = wglu_g.shape[2]

    def body(y_ref, w_ref, gy_ref, z_ref, o_ref):
        gy = _bf(_gelu(y_ref[...]))
        gy_ref[...] = gy
        zs = [_dot(gy, w_ref[s]) for s in range(4)]
        for s in range(4):
            z_ref[:, s * cw:(s + 1) * cw] = zs[s]
        o_ref[:, :cw] = zs[0] * _sig(zs[2])
        o_ref[:, cw:] = zs[1] * _sig(zs[3])

    sd = jax.ShapeDtypeStruct
    return _pcall(
        body, name="glu_fwd", grid=(r // tm,),
        in_specs=[pl.BlockSpec((tm, w), lambda i: (i, 0)), _resident(wglu_g.shape)],
        out_specs=[pl.BlockSpec((tm, w), lambda i: (i, 0)), pl.BlockSpec((tm, 4 * cw), lambda i: (i, 0)),
                   pl.BlockSpec((tm, 2 * cw), lambda i: (i, 0))],
        out_shape=[sd((r, w), BF16), sd((r, 4 * cw), F32), sd((r, 2 * cw), F32)],
        compiler_params=_cp(("parallel",), 40),
    )(y_s5, wglu_g)


def _glu_bwd(dyg, z, y_s5, wglu_g, lp):
    r, w = y_s5.shape
    tm = _row_tile(lp, 416)
    cw = wglu_g.shape[2]

    def body(d_ref, z_ref, y_ref, w_ref, dz_ref, dy_ref):
        d = d_ref[...]
        zz = z_ref[...]
        acc = jnp.zeros((tm, w), F32)
        for s in range(2):
            z1 = zz[:, s * cw:(s + 1) * cw]
            sg = _sig(zz[:, (2 + s) * cw:(3 + s) * cw])
            dd = d[:, s * cw:(s + 1) * cw]
            dz1 = _bf(dd * sg)
            dz2 = _bf(dd * z1 * sg * (1.0 - sg))
            dz_ref[:, s * cw:(s + 1) * cw] = dz1
            dz_ref[:, (2 + s) * cw:(3 + s) * cw] = dz2
            acc += _dot(dz1, w_ref[s], NT) + _dot(dz2, w_ref[2 + s], NT)
        dy_ref[...] = acc * _gelu_grad(y_ref[...])

    sd = jax.ShapeDtypeStruct
    return _pcall(
        body, name="glu_bwd", grid=(r // tm,),
        in_specs=[pl.BlockSpec((tm, 2 * cw), lambda i: (i, 0)), pl.BlockSpec((tm, 4 * cw), lambda i: (i, 0)),
                  pl.BlockSpec((tm, w), lambda i: (i, 0)), _resident(wglu_g.shape)],
        out_specs=[pl.BlockSpec((tm, 4 * cw), lambda i: (i, 0)), pl.BlockSpec((tm, w), lambda i: (i, 0))],
        out_shape=[sd((r, 4 * cw), BF16), sd((r, w), F32)],
        compiler_params=_cp(("parallel",), 40),
    )(dyg, z, y_s5, wglu_g)


def _conv_fwd(p3, cw, cb):
    bsz, lp, _ = p3.shape
    tt = _row_tile(lp, 416)
    nt = lp // tt
    tb = tt // 8
    c = cw.shape[1]
    qb = Q_OFF // c

    def body(x_ref, halo_ref, w_ref, b_ref, pre_ref, act_ref, ext_ref):
        t = pl.program_id(1)
        ext_ref[0:8, :] = jnp.where(t == 0, 0.0, halo_ref[...])
        ext_ref[8:, :] = x_ref[...]
        w = w_ref[...]
        acc = b_ref[...] + w[0:1, :] * ext_ref[pl.ds(5, tt), :]
        for j in range(1, CONV_W):
            acc = acc + w[j:j + 1, :] * ext_ref[pl.ds(5 + j, tt), :]
        pre_ref[...] = acc
        act_ref[...] = acc * _sig(acc)

    sd = jax.ShapeDtypeStruct
    return _pcall(
        body, name="conv_fwd", grid=(bsz, nt),
        in_specs=[pl.BlockSpec((None, tt, c), lambda b, t: (b, t, qb)),
                  pl.BlockSpec((None, 8, c), lambda b, t: (b, jnp.maximum(t * tb - 1, 0), qb)),
                  _const((CONV_W, c)), _const((1, c))],
        out_specs=[pl.BlockSpec((None, tt, c), lambda b, t: (b, t, 0))] * 2,
        out_shape=[sd((bsz, lp, c), F32)] * 2,
        scratch_shapes=[pltpu.VMEM((tt + 8, c), F32)],
        compiler_params=_cp(("parallel", "parallel")),
    )(p3, p3, cw, cb)


def _conv_bwd(dp3, p3, dact3, pre3, cw):
    bsz, lp, _ = p3.shape
    tt = _row_tile(lp, 416)
    nt = lp // tt
    tb = tt // 8
    c = cw.shape[1]
    qb = Q_OFF // c

    def silu_grad(x):
        s = _sig(x)
        return s * (1.0 + x * (1.0 - s))

    def body(dp_any, x_ref, xh_ref, d_ref, dh_ref, pre_ref, preh_ref, w_ref, o_ref, dw_ref, db_ref, ext_ref, dext_ref):
        b = pl.program_id(0)
        t = pl.program_id(1)

        @pl.when((b == 0) & (t == 0))
        def _():
            dw_ref[...] = jnp.zeros_like(dw_ref)
            db_ref[...] = jnp.zeros_like(db_ref)

        dc = d_ref[...] * silu_grad(pre_ref[...])
        dch = jnp.where(t == nt - 1, 0.0, dh_ref[...] * silu_grad(preh_ref[...]))
        dext_ref[0:tt, :] = dc
        dext_ref[tt:, :] = dch
        ext_ref[0:8, :] = jnp.where(t == 0, 0.0, xh_ref[...])
        ext_ref[8:, :] = x_ref[...]
        w = w_ref[...]
        acc = w[CONV_W - 1:CONV_W, :] * dc
        for j in range(CONV_W - 1):
            acc = acc + w[j:j + 1, :] * dext_ref[pl.ds(CONV_W - 1 - j, tt), :]
        trow = t * tt + lax.broadcasted_iota(jnp.int32, (tt, 1), 0)
        o_ref[...] = jnp.where(trow >= PAD, acc, 0.0)
        db_ref[...] += _colsum(dc)
        for j in range(CONV_W):
            dw_ref[j:j + 1, :] += _colsum(dc * ext_ref[pl.ds(5 + j, tt), :])

    sd = jax.ShapeDtypeStruct
    nxt = lambda t: jnp.minimum((t + 1) * tb, lp // 8 - 1)
    return _pcall(
        body, name="conv_bwd", grid=(bsz, nt),
        in_specs=[pl.BlockSpec(memory_space=pl.ANY),
                  pl.BlockSpec((None, tt, c), lambda b, t: (b, t, qb)),
                  pl.BlockSpec((None, 8, c), lambda b, t: (b, jnp.maximum(t * tb - 1, 0), qb)),
                  pl.BlockSpec((None, tt, c), lambda b, t: (b, t, 0)),
                  pl.BlockSpec((None, 8, c), lambda b, t: (b, nxt(t), 0)),
                  pl.BlockSpec((None, tt, c), lambda b, t: (b, t, 0)),
                  pl.BlockSpec((None, 8, c), lambda b, t: (b, nxt(t), 0)),
                  _const((CONV_W, c))],
        out_specs=[pl.BlockSpec((None, tt, c), lambda b, t: (b, t, qb)), _const((CONV_W, c)), _const((1, c))],
        out_shape=[sd(dp3.shape, F32), sd((CONV_W, c), F32), sd((1, c), F32)],
        scratch_shapes=[pltpu.VMEM((tt + 8, c), F32), pltpu.VMEM((tt + 8, c), F32)],
        input_output_aliases={0: 0},
        compiler_params=_cp(("arbitrary", "arbitrary")),
    )(dp3, p3, p3, dact3, dact3, pre3, pre3, cw)


def _mlstm_gates(g, h_idx, c_idx, lc):
    lane = lax.broadcasted_iota(jnp.int32, g.shape, 1)
    i_col = jnp.sum(jnp.where(lane == h_idx, g, 0.0), axis=1, keepdims=True)
    f_col = jnp.sum(jnp.where(lane == M_HEADS + h_idx, g, 0.0), axis=1, keepdims=True)
    row = lax.broadcasted_iota(jnp.int32, (lc, 1), 0)
    valid = (c_idx * lc + row) >= PAD
    li = jnp.where(valid, i_col, NEG)
    lf = jnp.where(valid, jnp.minimum(f_col, 0.0) - jnp.log(1.0 + jnp.exp(-jnp.abs(f_col))), 0.0)
    r2 = lax.broadcasted_iota(jnp.int32, (lc, lc), 0)
    c2 = lax.broadcasted_iota(jnp.int32, (lc, lc), 1)
    eye = r2 == c2
    tril = r2 >= c2
    to_row = lambda col: jnp.sum(jnp.where(eye, col, 0.0), axis=0, keepdims=True)
    lf_row = to_row(lf)
    b_col = jnp.sum(jnp.where(tril, lf_row, 0.0), axis=1, keepdims=True)
    b_row = to_row(b_col)
    li_row = to_row(li)
    d_mat = jnp.where(tril, b_col - b_row + li_row, NEG)
    return dict(f_col=f_col, valid=valid, li=li, b_col=b_col, d_mat=d_mat, eye=eye, r2=r2, c2=c2, row=row,
                to_row=to_row)


def _mlstm_chunk(q, ks, v, gq, c_st, n_st, m_st, lc):
    b_col, d_mat = gq["b_col"], gq["d_mat"]
    m_inter = b_col + m_st
    m_row = jnp.maximum(m_inter, jnp.max(d_mat, axis=1, keepdims=True))
    w_intra = jnp.exp(d_mat - m_row)
    w_inter = jnp.exp(m_inter - m_row)
    qb, kb, vb, cb = _bf(q), _bf(ks), _bf(v), _bf(c_st)
    s = _dot(qb, kb, NT) * w_intra
    qc = _dot(qb, cb)
    num = _dot(_bf(s), vb) + w_inter * qc
    qn = jnp.sum(q * n_st, axis=1, keepdims=True)
    den = jnp.sum(s, axis=1, keepdims=True) + w_inter * qn
    e = jnp.exp(-m_row)
    nn = jnp.maximum(jnp.abs(den), e)
    b_last = b_col[lc - 1:lc, :]
    g_log = b_last - b_col + gq["li"]
    m_new = jnp.maximum(b_last + m_st, jnp.max(g_log, axis=0, keepdims=True))
    w_k = jnp.exp(g_log - m_new)
    decay = jnp.exp(b_last + m_st - m_new)
    return dict(w_intra=w_intra, w_inter=w_inter, qb=qb, kb=kb, vb=vb, cb=cb, s=s, qc=qc, num=num, qn=qn, den=den,
                e=e, nn=nn, m_new=m_new, w_k=w_k, decay=decay)


def _chunks_per_step(nc):
    return max(c for c in (5, 4, 2, 1) if nc % c == 0)


def _mlstm_fwd(qk3, p3):
    bsz, lp, _ = p3.shape
    lc = M_CHUNK
    nc = lp // lc
    dk, dv = 128, 256
    scale = dk ** -0.5

    cps = _chunks_per_step(nc)
    rows = cps * lc

    def body(q_ref, k_ref, v_ref, g_ref, h_ref, cs_ref, ns_ref, ms_ref, c_sc, n_sc, m_sc):
        st = pl.program_id(1)

        @pl.when(st == 0)
        def _():
            c_sc[...] = jnp.zeros_like(c_sc)
            n_sc[...] = jnp.zeros_like(n_sc)
            m_sc[...] = jnp.zeros_like(m_sc)

        for j in range(cps):
            rs = slice(j * lc, (j + 1) * lc)
            g = g_ref[rs, :]
            for hh in range(M_HEADS):
                c_st, n_st, m_all = c_sc[hh], n_sc[hh], m_sc[hh]
                cs_ref[hh, j] = c_st
                ns_ref[hh, j] = n_st
                ms_ref[hh, j] = m_all
                m_st = m_all[:, 0:1]
                q = q_ref[rs, hh * dk:(hh + 1) * dk]
                ks = k_ref[rs, hh * dk:(hh + 1) * dk] * scale
                v = v_ref[rs, hh * dv:(hh + 1) * dv]
                gq = _mlstm_gates(g, hh, st * cps + j, lc)
                f = _mlstm_chunk(q, ks, v, gq, c_st, n_st, m_st, lc)
                h_ref[rs, hh * dv:(hh + 1) * dv] = f["num"] / f["nn"]
                kw = ks * f["w_k"]
                c_sc[hh] = f["decay"] * c_st + _dot(_bf(kw), f["vb"], TN)
                n_sc[hh] = f["decay"] * n_st + _colsum(kw)
                m_sc[hh] = jnp.broadcast_to(f["m_new"], (1, 128))

    sd = jax.ShapeDtypeStruct
    nh = M_HEADS
    return _pcall(
        body, name="mlstm_fwd", grid=(bsz, nc // cps),
        in_specs=[pl.BlockSpec((None, rows, nh * dk), lambda b, c: (b, c, 0)),
                  pl.BlockSpec((None, rows, nh * dk), lambda b, c: (b, c, 1)),
                  pl.BlockSpec((None, rows, nh * dv), lambda b, c: (b, c, V_OFF // (nh * dv))),
                  pl.BlockSpec((None, rows, 128), lambda b, c: (b, c, G_OFF // 128))],
        out_specs=[pl.BlockSpec((None, rows, nh * dv), lambda b, c: (b, c, 0)),
                   pl.BlockSpec((None, nh, cps, dk, dv), lambda b, c: (b, 0, c, 0, 0)),
                   pl.BlockSpec((None, nh, cps, 1, dk), lambda b, c: (b, 0, c, 0, 0)),
                   pl.BlockSpec((None, nh, cps, 1, 128), lambda b, c: (b, 0, c, 0, 0))],
        out_shape=[sd((bsz, lp, nh * dv), F32), sd((bsz, nh, nc, dk, dv), F32),
                   sd((bsz, nh, nc, 1, dk), F32), sd((bsz, nh, nc, 1, 128), F32)],
        scratch_shapes=[pltpu.VMEM((nh, dk, dv), F32), pltpu.VMEM((nh, 1, dk), F32), pltpu.VMEM((nh, 1, 128), F32)],
        compiler_params=_cp(("parallel", "arbitrary")),
    )(qk3, qk3, p3, p3)


def _mlstm_bwd(dp3, qk3, p3, dh3, cs, ns, ms):
    bsz, lp, _ = p3.shape
    lc = M_CHUNK
    nc = lp // lc
    dk, dv = 128, 256
    scale = dk ** -0.5

    cps = _chunks_per_step(nc)
    nst = nc // cps
    rows = cps * lc

    def body(dp_any, q_ref, k_ref, v_ref, g_ref, dh_ref, cs_ref, ns_ref, ms_ref,
             dv_ref, dqk_ref, dg_ref, dc_sc, dn_sc):
        t = pl.program_id(1)
        st = nst - 1 - t

        @pl.when(t == 0)
        def _():
            dc_sc[...] = jnp.zeros_like(dc_sc)
            dn_sc[...] = jnp.zeros_like(dn_sc)

        lane = lax.broadcasted_iota(jnp.int32, (lc, 128), 1)
        for j in reversed(range(cps)):
            rs = slice(j * lc, (j + 1) * lc)
            g = g_ref[rs, :]
            dgate = jnp.zeros((lc, 128), F32)
            for hh in range(M_HEADS):
                dgate = head(hh, j, rs, st * cps + j, g, lane, dgate, q_ref, k_ref, v_ref, dh_ref, cs_ref, ns_ref,
                             ms_ref, dv_ref, dqk_ref, dc_sc, dn_sc)
            dg_ref[rs, :] = dgate

    def head(hh, j, sl, c, g, lane, dgate, q_ref, k_ref, v_ref, dh_ref, cs_ref, ns_ref, ms_ref, dv_ref, dqk_ref,
             dc_sc, dn_sc):
        c_st, n_st = cs_ref[hh, j], ns_ref[hh, j]
        m_st = ms_ref[hh, j][:, 0:1]
        q = q_ref[sl, hh * dk:(hh + 1) * dk]
        ks = k_ref[sl, hh * dk:(hh + 1) * dk] * scale
        v = v_ref[sl, hh * dv:(hh + 1) * dv]
        dh = dh_ref[sl, hh * dv:(hh + 1) * dv]
        gq = _mlstm_gates(g, hh, c, lc)
        f = _mlstm_chunk(q, ks, v, gq, c_st, n_st, m_st, lc)
        eye, r2, c2, row, valid = gq["eye"], gq["r2"], gq["c2"], gq["row"], gq["valid"]
        w_intra, w_inter, s, nn, den = f["w_intra"], f["w_inter"], f["s"], f["nn"], f["den"]
        qb, kb, vb, cb, w_k, decay = f["qb"], f["kb"], f["vb"], f["cb"], f["w_k"], f["decay"]
        d_c, d_n = dc_sc[hh], dn_sc[hh]
        d_cb = _bf(d_c)

        hout = f["num"] / nn
        dnum = dh / nn
        d_nn = -jnp.sum(dh * hout, axis=1, keepdims=True) / nn
        dden = jnp.where(jnp.abs(den) > f["e"], d_nn * jnp.sign(den), 0.0)
        wdnum = w_inter * dnum
        wdden = w_inter * dden
        ds = _dot(_bf(dnum), vb, NT) + dden
        dsw = _bf(ds * w_intra)
        dq = _dot(dsw, kb) + _dot(_bf(wdnum), cb, NT) + wdden * n_st
        dkw = _dot(vb, d_cb, NT) + d_n
        dks = _dot(dsw, qb, TN) + dkw * w_k
        kw = ks * w_k
        dvv = _dot(_bf(s), _bf(dnum), TN) + _dot(_bf(kw), d_cb)
        dd = ds * s
        rs = jnp.sum(dd, axis=1, keepdims=True)
        cs_col = jnp.sum(jnp.where(eye, jnp.sum(dd, axis=0, keepdims=True), 0.0), axis=1, keepdims=True)
        dwi = jnp.sum(dnum * f["qc"], axis=1, keepdims=True) + dden * f["qn"]
        db = rs - cs_col + dwi * w_inter
        dli = cs_col
        ddecay = jnp.sum(jnp.sum(d_c * c_st, axis=1, keepdims=True), axis=0, keepdims=True) \
            + jnp.sum(d_n * n_st, axis=1, keepdims=True)
        dgl = jnp.sum(dkw * ks, axis=1, keepdims=True) * w_k
        dblast = ddecay * decay + jnp.sum(dgl, axis=0, keepdims=True)
        db = db - dgl + jnp.where(row == lc - 1, dblast, 0.0)
        dli = dli + dgl
        db_row = gq["to_row"](db)
        dlf = jnp.sum(jnp.where(c2 >= r2, db_row, 0.0), axis=1, keepdims=True)
        dlf = jnp.where(valid, dlf, 0.0)
        dgate = jnp.where(lane == hh, jnp.where(valid, dli, 0.0), dgate)
        dgate = jnp.where(lane == M_HEADS + hh, dlf * _sig(-gq["f_col"]), dgate)
        dqk_ref[sl, hh * dk:(hh + 1) * dk] = dq
        dqk_ref[sl, (M_HEADS + hh) * dk:(M_HEADS + hh + 1) * dk] = dks * scale
        dv_ref[sl, hh * dv:(hh + 1) * dv] = dvv
        dc_sc[hh] = decay * d_c + _dot(qb, _bf(wdnum), TN)
        dn_sc[hh] = decay * d_n + _colsum(q * wdden)
        return dgate

    sd = jax.ShapeDtypeStruct
    nh = M_HEADS
    rc = lambda c: nst - 1 - c
    return _pcall(
        body, name="mlstm_bwd", grid=(bsz, nst),
        in_specs=[pl.BlockSpec(memory_space=pl.ANY),
                  pl.BlockSpec((None, rows, nh * dk), lambda b, c: (b, rc(c), 0)),
                  pl.BlockSpec((None, rows, nh * dk), lambda b, c: (b, rc(c), 1)),
                  pl.BlockSpec((None, rows, nh * dv), lambda b, c: (b, rc(c), V_OFF // (nh * dv))),
                  pl.BlockSpec((None, rows, 128), lambda b, c: (b, rc(c), G_OFF // 128)),
                  pl.BlockSpec((None, rows, nh * dv), lambda b, c: (b, rc(c), 0)),
                  pl.BlockSpec((None, nh, cps, dk, dv), lambda b, c: (b, 0, rc(c), 0, 0)),
                  pl.BlockSpec((None, nh, cps, 1, dk), lambda b, c: (b, 0, rc(c), 0, 0)),
                  pl.BlockSpec((None, nh, cps, 1, 128), lambda b, c: (b, 0, rc(c), 0, 0))],
        out_specs=[pl.BlockSpec((None, rows, nh * dv), lambda b, c: (b, rc(c), V_OFF // (nh * dv))),
                   pl.BlockSpec((None, rows, 2 * nh * dk), lambda b, c: (b, rc(c), 0)),
                   pl.BlockSpec((None, rows, 128), lambda b, c: (b, rc(c), 0))],
        out_shape=[sd(dp3.shape, F32), sd((bsz, lp, 2 * nh * dk), F32), sd((bsz, lp, 128), F32)],
        scratch_shapes=[pltpu.VMEM((nh, dk, dv), F32), pltpu.VMEM((nh, 1, dk), F32)],
        input_output_aliases={0: 0},
        compiler_params=_cp(("arbitrary", "arbitrary")),
    )(dp3, qk3, qk3, p3, p3, dh3, cs, ns, ms)


def _headnorm(x):
    dv = x.shape[1] // M_HEADS
    xh, rs = [], []
    for h in range(M_HEADS):
        xx = x[:, h * dv:(h + 1) * dv]
        mu = jnp.mean(xx, axis=-1, keepdims=True)
        xc = xx - mu
        rstd = lax.rsqrt(jnp.mean(xc * xc, axis=-1, keepdims=True) + LN_EPS)
        xh.append(xc * rstd)
        rs.append(rstd)
    return jnp.concatenate(xh, axis=1), rs


def _mix_fwd(hm, p, ys5g, h0, gn, wmo_bf, wo_bf, g1, b1, lp):
    r, d = hm.shape
    tm = _row_tile(lp, 208)

    def body(hm_ref, o_ref, gs_ref, gm_ref, ys_ref, h0_ref, gn_ref, wmo_ref, wo_ref, g1_ref, b1_ref,
             ymin_ref, ym_ref, mix_ref, r1_ref, h1_ref):
        xhat, _ = _headnorm(hm_ref[...])
        ymin = _bf(_sig(o_ref[...]) * (xhat * gn_ref[...]))
        ymin_ref[...] = ymin
        ym = _dot(ymin, wmo_ref[...])
        ym_ref[...] = ym
        mix = _bf(_sig(gs_ref[...]) * ys_ref[...] + _sig(gm_ref[...]) * ym)
        mix_ref[...] = mix
        r1 = ALPHA * h0_ref[...] + _dot(mix, wo_ref[...])
        r1_ref[...] = r1
        h1, _, _ = _ln_fwd(r1, g1_ref[...], b1_ref[...])
        h1_ref[...] = h1

    sd = jax.ShapeDtypeStruct
    row = pl.BlockSpec((tm, d), lambda i: (i, 0))
    return _pcall(
        body, name="mix_fwd", grid=(r // tm,),
        in_specs=[row, pl.BlockSpec((tm, d), lambda i: (i, O_OFF // d)), pl.BlockSpec((tm, d), lambda i: (i, GS_OFF // d)),
                  pl.BlockSpec((tm, d), lambda i: (i, GM_OFF // d)), row, row, _const((1, d)),
                  _resident((d, d)), _resident((d, d)), _const((1, d)), _const((1, d))],
        out_specs=[row] * 5,
        out_shape=[sd((r, d), BF16), sd((r, d), F32), sd((r, d), BF16), sd((r, d), F32), sd((r, d), F32)],
        compiler_params=_cp(("parallel",), 48),
    )(hm, p, p, p, ys5g, h0, gn, wmo_bf, wo_bf, g1, b1)


def _mix_bwd(dh1, r1, g1, wo_bf, wmo_bf, p, ys5g, ym, hm, gn, lp):
    r, d = hm.shape
    tm = _row_tile(lp, 208)
    dv = d // M_HEADS

    def body(dh1_ref, r1_ref, g1_ref, wo_ref, wmo_ref, o_ref, gs_ref, gm_ref, ys_ref, ym_ref, hm_ref, gn_ref,
             dr1_ref, dp_ref, dys_ref, dym_ref, dhm_ref, dg1_ref, db1_ref, dgn_ref):
        i = pl.program_id(0)

        @pl.when(i == 0)
        def _():
            dg1_ref[...] = jnp.zeros_like(dg1_ref)
            db1_ref[...] = jnp.zeros_like(db1_ref)
            dgn_ref[...] = jnp.zeros_like(dgn_ref)

        dh1 = dh1_ref[...]
        _, xhat1, rstd1 = _ln_fwd(r1_ref[...], g1_ref[...], 0.0)
        dr1 = _ln_bwd(dh1, xhat1, rstd1, g1_ref[...])
        dr1_ref[...] = dr1
        dg1_ref[...] += _colsum(dh1 * xhat1)
        db1_ref[...] += _colsum(dh1)
        dmix = _dot(_bf(dr1), wo_ref[...], NT)
        sgs, sgm, so = _sig(gs_ref[...]), _sig(gm_ref[...]), _sig(o_ref[...])
        dys_ref[...] = dmix * sgs
        dp_ref[:, d:2 * d] = dmix * ys_ref[...] * sgs * (1.0 - sgs)
        dym = dmix * sgm
        dym_ref[...] = _bf(dym)
        dp_ref[:, 2 * d:3 * d] = dmix * ym_ref[...] * sgm * (1.0 - sgm)
        dymin = _dot(_bf(dym), wmo_ref[...], NT)
        xhat, rs = _headnorm(hm_ref[...])
        gn_ = gn_ref[...]
        dp_ref[:, 0:d] = dymin * (xhat * gn_) * so * (1.0 - so)
        dhn = dymin * so
        dgn_ref[...] += _colsum(dhn * xhat)
        dxh = dhn * gn_
        for h in range(M_HEADS):
            sl = slice(h * dv, (h + 1) * dv)
            a, xh = dxh[:, sl], xhat[:, sl]
            m1 = jnp.mean(a, axis=-1, keepdims=True)
            m2 = jnp.mean(a * xh, axis=-1, keepdims=True)
            dhm_ref[:, sl] = rs[h] * (a - m1 - xh * m2)

    sd = jax.ShapeDtypeStruct
    row = pl.BlockSpec((tm, d), lambda i: (i, 0))
    vec = _const((1, d))
    return _pcall(
        body, name="mix_bwd", grid=(r // tm,),
        in_specs=[row, row, vec, _resident((d, d)), _resident((d, d)),
                  pl.BlockSpec((tm, d), lambda i: (i, O_OFF // d)), pl.BlockSpec((tm, d), lambda i: (i, GS_OFF // d)),
                  pl.BlockSpec((tm, d), lambda i: (i, GM_OFF // d)), row, row, row, vec],
        out_specs=[row, pl.BlockSpec((tm, 3 * d), lambda i: (i, 0)), row, row, row, vec, vec, vec],
        out_shape=[sd((r, d), F32), sd((r, NP), F32), sd((r, d), F32), sd((r, d), BF16), sd((r, d), F32),
                   sd((1, d), F32), sd((1, d), F32), sd((1, d), F32)],
        compiler_params=_cp(("arbitrary",), 48),
    )(dh1, r1, g1, wo_bf, wmo_bf, p, p, p, ys5g, ym, hm, gn)


def _mlp_fwd(h1, tgt, wup_g, wdn_bf, bup, g2, b2, lp):
    r, d = h1.shape
    tm = _row_tile(lp, 320)
    tps = lp // tm
    nf = wup_g.shape[0]

    def body(h1_ref, t_ref, wup_ref, wdn_ref, bup_ref, g2_ref, b2_ref, dr2_ref, act_ref, loss_ref, dg2_ref, db2_ref):
        i = pl.program_id(0)

        @pl.when(i == 0)
        def _():
            loss_ref[...] = jnp.zeros_like(loss_ref)
            dg2_ref[...] = jnp.zeros_like(dg2_ref)
            db2_ref[...] = jnp.zeros_like(db2_ref)

        h1 = h1_ref[...]
        h1b = _bf(h1)
        ff = jnp.zeros((tm, d), F32)
        for s in range(nf):
            up = _dot(h1b, wup_ref[s]) + bup_ref[:, s * d:(s + 1) * d]
            a = jnp.maximum(up, 0.0)
            a = _bf(a * a)
            act_ref[:, s * d:(s + 1) * d] = a
            ff = ff + _dot(a, wdn_ref[s * d:(s + 1) * d, :])
        r2 = ALPHA * h1 + ff
        g2 = g2_ref[...]
        y, xhat, rstd = _ln_fwd(r2, g2, b2_ref[...])
        t = (i % tps) * tm + lax.broadcasted_iota(jnp.int32, (tm, 1), 0)
        diff = jnp.where(t >= PAD + N_META, y - t_ref[...], 0.0)
        loss_ref[...] += 0.5 / d * jnp.sum(jnp.sum(diff * diff, axis=1, keepdims=True), axis=0, keepdims=True)
        dy = diff * (1.0 / d)
        dg2_ref[...] += _colsum(dy * xhat)
        db2_ref[...] += _colsum(dy)
        dr2_ref[...] = _ln_bwd(dy, xhat, rstd, g2)

    sd = jax.ShapeDtypeStruct
    row = pl.BlockSpec((tm, d), lambda i: (i, 0))
    vec = _const((1, d))
    return _pcall(
        body, name="mlp_fwd", grid=(r // tm,),
        in_specs=[row, row, _resident(wup_g.shape), _resident(wdn_bf.shape), _const((1, nf * d)), vec, vec],
        out_specs=[row, pl.BlockSpec((tm, nf * d), lambda i: (i, 0)), _const((1, 128)), vec, vec],
        out_shape=[sd((r, d), F32), sd((r, nf * d), BF16), sd((1, 128), F32), sd((1, d), F32), sd((1, d), F32)],
        compiler_params=_cp(("arbitrary",), 56),
    )(h1, tgt, wup_g, wdn_bf, bup, g2, b2)


def _mlp_bwd(h1, dr2, wup_g, wdn_bf, bup, lp):
    r, d = h1.shape
    tm = _row_tile(lp, 320)
    nf = wup_g.shape[0]

    def body(h1_ref, dr2_ref, wup_ref, wdn_ref, bup_ref, dh1_ref, dup_ref, dbup_ref):
        i = pl.program_id(0)

        @pl.when(i == 0)
        def _():
            dbup_ref[...] = jnp.zeros_like(dbup_ref)

        h1b = _bf(h1_ref[...])
        dr2 = dr2_ref[...]
        dr2b = _bf(dr2)
        acc = ALPHA * dr2
        for s in range(nf):
            up = _dot(h1b, wup_ref[s]) + bup_ref[:, s * d:(s + 1) * d]
            dact = _dot(dr2b, wdn_ref[s * d:(s + 1) * d, :], NT)
            dup = dact * (2.0 * jnp.maximum(up, 0.0))
            dbup_ref[:, s * d:(s + 1) * d] += _colsum(dup)
            dupb = _bf(dup)
            dup_ref[:, s * d:(s + 1) * d] = dupb
            acc = acc + _dot(dupb, wup_ref[s], NT)
        dh1_ref[...] = acc

    sd = jax.ShapeDtypeStruct
    row = pl.BlockSpec((tm, d), lambda i: (i, 0))
    return _pcall(
        body, name="mlp_bwd", grid=(r // tm,),
        in_specs=[row, row, _resident(wup_g.shape), _resident(wdn_bf.shape), _const((1, nf * d))],
        out_specs=[row, pl.BlockSpec((tm, nf * d), lambda i: (i, 0)), _const((1, nf * d))],
        out_shape=[sd((r, d), F32), sd((r, nf * d), BF16), sd((1, nf * d), F32)],
        compiler_params=_cp(("arbitrary",), 56),
    )(h1, dr2, wup_g, wdn_bf, bup)


def _s5_block_mats(bb_re_t, bb_im_t, c_re, c_im, ap_re, ap_im):
    ng = c_re.shape[0]
    gl = ng // S5_KCH
    eye = jnp.eye(gl, dtype=F32)

    def bmat(bt):
        bb = jnp.transpose(bt, (1, 0, 2)).reshape(S5_KCH, gl, S5_GROUP, S5_STATE)
        return jnp.einsum("kghp,gj->kghjp", bb, eye).reshape(S5_KCH, gl * S5_GROUP, gl * S5_STATE)

    def cmat(c):
        cc = c.reshape(S5_KCH, gl, S5_GROUP, S5_STATE)
        return jnp.einsum("kghp,gj->kjpgh", cc, eye).reshape(S5_KCH, gl * S5_STATE, gl * S5_GROUP)

    def pw(a):
        return jnp.transpose(a.reshape(8, S5_KCH, gl * S5_STATE), (1, 0, 2))

    bk = jnp.concatenate([bmat(bb_re_t), bmat(bb_im_t)], axis=-1)
    apow = jnp.concatenate([pw(ap_re), pw(ap_im)], axis=-1)
    return _bf(bk), _bf(cmat(c_re)), _bf(cmat(c_im)), apow


def _s5_block_grads(dbk, dcre, dcim, da):
    gl = dbk.shape[1] // S5_GROUP
    ng = gl * S5_KCH
    eye = jnp.eye(gl, dtype=F32)
    hw = gl * S5_STATE

    def bpart(x):
        x = x.reshape(S5_KCH, gl, S5_GROUP, gl, S5_STATE)
        x = jnp.einsum("kghjp,gj->kghp", x, eye).reshape(ng, S5_GROUP, S5_STATE)
        return jnp.transpose(x, (1, 0, 2))

    def cpart(x):
        x = x.reshape(S5_KCH, gl, S5_STATE, gl, S5_GROUP)
        return jnp.einsum("kjpgh,gj->kghp", x, eye).reshape(ng, S5_GROUP, S5_STATE)

    return (bpart(dbk[..., :hw]), bpart(dbk[..., hw:]), cpart(dcre), cpart(dcim),
            da[:, 0, :hw].reshape(ng, S5_STATE), da[:, 0, hw:].reshape(ng, S5_STATE))


def _tie(a, tok):
    return a if tok is None else a + tok[0, 0]


def _local_step(x, tgt, w, early=None, late=None, ready=None):
    ready = ready or (lambda names, g: None)
    bsz, seq, d = x.shape
    lp = PAD + N_META + seq
    r = bsz * lp
    meta = jnp.broadcast_to(w["meta_tokens"][None], (bsz, N_META, d))
    hin = jnp.concatenate([jnp.zeros((bsz, PAD, d), F32), meta, x], axis=1).reshape(r, d)
    tgtp = jnp.concatenate([jnp.zeros((bsz, PAD + N_META, d), F32), tgt], axis=1).reshape(r, d)

    h0 = _ln0_fwd(hin, w["ln0_g"], w["ln0_b"], lp)
    if early is not None:
        w = {**w, **early(h0)}
    p = _inproj(h0, w["w_in"], w["b_in"], lp)
    p3 = p.reshape(bsz, lp, NP)

    b_re_t = jnp.transpose(w["s5_b_re"], (2, 0, 1))
    b_im_t = jnp.transpose(w["s5_b_im"], (2, 0, 1))
    ap_re, ap_im, bb_re_t, bb_im_t = _s5_prep(w["s5_lambda_re"], w["s5_lambda_im"], w["s5_log_dt"], b_re_t, b_im_t)
    bk, cre, cim, apow = _s5_block_mats(bb_re_t, bb_im_t, w["s5_c_re"], w["s5_c_im"], ap_re, ap_im)
    y_s5, xs = _s5_fwd(p3, bk, cre, cim, apow, w["s5_d"])
    sw = y_s5.shape[-1]
    if late is not None:
        w = {**w, **late(y_s5)}
    gy, z, ys5g = _glu_fwd(y_s5.reshape(r, sw), w["s5_w_glu"], lp)

    pre3, qk3 = _conv_fwd(p3, w["qk_conv_w"], w["qk_conv_b"])
    hm3, cs, ns, ms = _mlstm_fwd(qk3, p3)
    hm = hm3.reshape(r, d)
    ymin, ym, mix, r1, h1 = _mix_fwd(hm, p, ys5g, h0, w["m_norm_g"], w["m_w_out"], w["w_o"], w["ln1_g"], w["ln1_b"], lp)
    dr2, act, loss, dg2, db2 = _mlp_fwd(h1, tgtp, w["w_up"], w["w_down"], w["b_up"], w["ln2_g"], w["ln2_b"], lp)

    g = {"ln2_g": dg2, "ln2_b": db2}
    dh1, dup, g["b_up"] = _mlp_bwd(h1, dr2, w["w_up"], w["w_down"], w["b_up"], lp)
    g["w_down"] = _mm_tn(act, dr2, name="dw_down")
    g["w_up"] = _mm_tn(h1, dup, name="dw_up", split=w["w_up"].shape[0])
    tok = ready(("w_down", "w_up"), g)
    dr1, dp, dys5g, dym, dhm, g["ln1_g"], g["ln1_b"], g["m_norm_g"] = _mix_bwd(
        dh1, r1, _tie(w["ln1_g"], tok), w["w_o"], w["m_w_out"], p, ys5g, ym, hm, w["m_norm_g"], lp)
    g["w_o"] = _mm_tn(mix, dr1, name="dw_o")
    g["m_w_out"] = _mm_tn(ymin, dym, name="dw_mout")

    dp3 = dp.reshape(bsz, lp, NP)
    dp3, dqk3, dgate = _mlstm_bwd(dp3, qk3, p3, dhm.reshape(bsz, lp, d), cs, ns, ms)
    dp3, g["qk_conv_w"], g["qk_conv_b"] = _conv_bwd(dp3, p3, dqk3, pre3, w["qk_conv_w"])
    dz, dys5 = _glu_bwd(dys5g, z, y_s5.reshape(r, sw), w["s5_w_glu"], lp)
    g["s5_w_glu"] = _mm_tn(gy, dz, name="dw_glu", split=w["s5_w_glu"].shape[0])
    tok = ready(("s5_w_glu", "m_w_out", "w_o"), g)
    apow_rev = jnp.flip(apow, axis=1)
    dp3, dbk, dcre, dcim, da, g["s5_d"] = _s5_bwd(dp3, p3, dys5.reshape(bsz, lp, sw), xs, bk, cre, cim, apow_rev,
                                                 _tie(w["s5_d"], tok))
    dbb_re_t, dbb_im_t, g["s5_c_re"], g["s5_c_im"], da_re, da_im = _s5_block_grads(dbk, dcre, dcim, da)
    g["s5_lambda_re"], g["s5_lambda_im"], g["s5_log_dt"], gb_re_t, gb_im_t = _s5_prep_bwd(
        w["s5_lambda_re"], w["s5_lambda_im"], w["s5_log_dt"], b_re_t, b_im_t, da_re, da_im, dbb_re_t, dbb_im_t)
    g["s5_b_re"] = jnp.transpose(gb_re_t, (1, 2, 0))
    g["s5_b_im"] = jnp.transpose(gb_im_t, (1, 2, 0))

    dp3 = lax.dynamic_update_slice(dp3, dgate, (0, 0, G_OFF))
    dp = dp3.reshape(r, NP)
    g["w_in"], g["b_in"] = _mm_tn(h0, dp, name="dw_in", colsum=True)
    tok = ready(("w_in",), g)
    dpw = _mm_nt(dp, w["w_in"], lp, name="dh0", dep=tok)
    dhin, g["ln0_g"], g["ln0_b"], g["meta_tokens"] = _ln0_bwd(hin, dr1, dpw, w["ln0_g"], lp)
    grad_x = dhin.reshape(bsz, lp, d)[:, PAD + N_META:]
    return loss, grad_x, g


_ANY = pl.BlockSpec(memory_space=pl.ANY)
_MESH = pl.DeviceIdType.MESH


def _place():
    return lax.axis_index("x"), lax.axis_index("y"), lax.axis_index("c")


def _gather_chips(shards):
    n = len(shards)

    def body(*refs):
        ins, outs = refs[:n], refs[n:2 * n]
        send, recv, loc = refs[2 * n:]
        x, y, c = _place()
        me = 2 * x + y
        peers = [(1 - x, y), (x, 1 - y), (1 - x, 1 - y)]

        def rc(a, k, slot):
            px, py = peers[k]
            return pltpu.make_async_remote_copy(src_ref=ins[a], dst_ref=outs[a].at[slot], send_sem=send.at[a, k],
                                                recv_sem=recv.at[a, k], device_id=(px, py, c), device_id_type=_MESH)

        own = [pltpu.make_async_copy(ins[a], outs[a].at[me], loc.at[a]) for a in range(n)]
        for cp in own:
            cp.start()
        out = [rc(a, k, me) for a in range(n) for k in range(3)]
        for cp in out:
            cp.start()
        for a in range(n):
            for k in range(3):
                rc(a, k, 2 * peers[k][0] + peers[k][1]).wait_recv()
        for cp in out:
            cp.wait_send()
        for cp in own:
            cp.wait()

    return _pcall(
        body, name="gather_chips", in_specs=[_ANY] * n, out_specs=[_ANY] * n,
        out_shape=[jax.ShapeDtypeStruct((4,) + s.shape, s.dtype) for s in shards],
        scratch_shapes=[pltpu.SemaphoreType.DMA((n, 3)), pltpu.SemaphoreType.DMA((n, 3)), pltpu.SemaphoreType.DMA((n,))],
    )(*shards)


_HBM = pl.BlockSpec(memory_space=pltpu.HBM)
_SEM = pl.BlockSpec(memory_space=pltpu.SEMAPHORE)
_EFFECT = pltpu.SideEffectType.DATAFLOW_SIDE_EFFECTING


def _xchg_copies(srcs, lands, send, recv, scatter):
    x, y, c = _place()
    me = 2 * x + y
    peers = [(1 - x, y), (x, 1 - y), (1 - x, 1 - y)]
    out = []
    for a in range(len(srcs)):
        for k, (px, py) in enumerate(peers):
            src = srcs[a].at[2 * px + py] if scatter else srcs[a]
            dst = lands[a].at[k] if scatter else lands[a].at[me]
            out.append(pltpu.make_async_remote_copy(src_ref=src, dst_ref=dst, send_sem=send.at[3 * a + k],
                                                    recv_sem=recv.at[3 * a + k], device_id=(px, py, c),
                                                    device_id_type=_MESH))
    return out


def _xchg_start(srcs, lands, *, name, scatter, dep=None):
    n = len(srcs)
    deps = [] if dep is None else [dep]
    nd = len(deps)

    def body(*refs):
        send, recv = refs[2 * n + nd], refs[2 * n + nd + 1]
        for cp in _xchg_copies(refs[:n], refs[n:2 * n], send, recv, scatter):
            cp.start()
        refs[-1][...] = jnp.zeros_like(refs[-1])

    hbm = lambda a: pltpu.HBM(a.shape, a.dtype)
    con = lambda a: pltpu.with_memory_space_constraint(a, pltpu.HBM)
    res = _pcall(
        body, name=name, in_specs=[_HBM] * (2 * n) + [_ANY] * nd,
        out_specs=[_SEM, _SEM] + [_HBM] * (2 * n) + [pl.BlockSpec(memory_space=pltpu.VMEM)],
        out_shape=[pltpu.SemaphoreType.DMA((3 * n,)), pltpu.SemaphoreType.DMA((3 * n,))]
        + [hbm(a) for a in srcs] + [hbm(a) for a in lands] + [jax.ShapeDtypeStruct((8, 128), F32)],
        input_output_aliases={i: 2 + i for i in range(2 * n)},
        compiler_params=pltpu.CompilerParams(has_side_effects=_EFFECT),
    )(*[con(a) for a in srcs], *[con(a) for a in lands], *deps)
    return res[0], res[1], list(res[2:2 + n]), list(res[2 + n:2 + 2 * n]), res[-1]


def _xchg_wait(send, recv, srcs, lands, after, *, name, scatter):
    n = len(srcs)

    def body(*refs):
        s_ref, r_ref = refs[2 * n], refs[2 * n + 1]
        for cp in _xchg_copies(refs[:n], refs[n:2 * n], s_ref, r_ref, scatter):
            cp.wait_send()
            cp.wait_recv()

    hbm = lambda a: pltpu.HBM(a.shape, a.dtype)
    res = _pcall(
        body, name=name, in_specs=[_HBM] * (2 * n) + [_SEM, _SEM, _ANY],
        out_specs=[_HBM] * (2 * n),
        out_shape=[hbm(a) for a in srcs] + [hbm(a) for a in lands],
        input_output_aliases={i: i for i in range(2 * n)},
        compiler_params=pltpu.CompilerParams(has_side_effects=_EFFECT),
    )(*srcs, *lands, send, recv, after)
    return list(res[:n]), list(res[n:])


def _swap_cores(arrs, name="swap_cores"):
    n = len(arrs)

    def body(*refs):
        ins, outs = refs[:n], refs[n:2 * n]
        send, recv = refs[2 * n:]
        x, y, c = _place()
        cps = [pltpu.make_async_remote_copy(src_ref=ins[a], dst_ref=outs[a], send_sem=send.at[a], recv_sem=recv.at[a],
                                            device_id=(x, y, 1 - c), device_id_type=_MESH) for a in range(n)]
        for cp in cps:
            cp.start()
        for cp in cps:
            cp.wait_recv()
        for cp in cps:
            cp.wait_send()

    return _pcall(
        body, name=name, in_specs=[_ANY] * n, out_specs=[_ANY] * n,
        out_shape=[jax.ShapeDtypeStruct(s.shape, s.dtype) for s in arrs],
        scratch_shapes=[pltpu.SemaphoreType.DMA((n,)), pltpu.SemaphoreType.DMA((n,))],
    )(*arrs)


def _allreduce_small(v):
    rows = v.shape[0]

    def body(v_ref, sum_ref, all_ref, send, recv):
        x, y, c = _place()
        me = 4 * x + 2 * y + c
        flips = [(k >> 2 & 1, k >> 1 & 1, k & 1) for k in range(1, 8)]

        def peer(f):
            return tuple(1 - q if b else q for q, b in zip((x, y, c), f))

        all_ref[me] = v_ref[...]
        cps = [pltpu.make_async_remote_copy(src_ref=v_ref, dst_ref=all_ref.at[me], send_sem=send.at[k], recv_sem=recv.at[k],
                                            device_id=peer(f), device_id_type=_MESH) for k, f in enumerate(flips)]
        for cp in cps:
            cp.start()
        for cp in cps:
            cp.wait_recv()
        acc = all_ref[0]
        for j in range(1, 8):
            acc = acc + all_ref[j]
        sum_ref[...] = acc
        for cp in cps:
            cp.wait_send()

    vm = pl.BlockSpec(memory_space=pltpu.VMEM)
    return _pcall(
        body, name="allreduce_small", in_specs=[vm], out_specs=vm,
        out_shape=jax.ShapeDtypeStruct((rows, 128), F32),
        scratch_shapes=[pltpu.VMEM((8, rows, 128), F32), pltpu.SemaphoreType.DMA((7,)), pltpu.SemaphoreType.DMA((7,))],
        compiler_params=_cp(None, 40),
    )(v)


def _sum_slots(own, land):
    ns, rows, cols = land.shape
    tm = _row_tile(rows, 256, 8)

    def body(own_ref, a_ref, o_ref):
        o_ref[...] = ((own_ref[...] + a_ref[0]) + a_ref[1]) + a_ref[2]

    return _pcall(
        body, name="sum_slots", grid=(rows // tm,),
        in_specs=[pl.BlockSpec((tm, cols), lambda i: (i, 0)), pl.BlockSpec((ns, tm, cols), lambda i: (0, i, 0))],
        out_specs=pl.BlockSpec((tm, cols), lambda i: (i, 0)),
        out_shape=jax.ShapeDtypeStruct((rows, cols), F32),
        compiler_params=_cp(("parallel",), 40),
    )(own, land)


def _adamw(w, m, v, g0, g1=None):
    rows, cols = w.shape[-2:]
    lead = w.ndim == 3
    tm = _row_tile(rows, 256, 8)
    c1 = 1.0 - ADAM_B1 ** ADAM_STEP
    c2 = 1.0 - ADAM_B2 ** ADAM_STEP
    two = g1 is not None

    def body(*refs):
        w_ref, m_ref, v_ref, g0_ref = refs[:4]
        g_ref, d_ref, nm_ref, nv_ref = refs[-4:]
        g = g0_ref[...]
        if two:
            g = g + refs[4][...]
        nm = ADAM_B1 * m_ref[...] + (1.0 - ADAM_B1) * g
        nv = ADAM_B2 * v_ref[...] + (1.0 - ADAM_B2) * (g * g)
        g_ref[...] = g
        nm_ref[...] = nm
        nv_ref[...] = nv
        d_ref[...] = -ADAM_LR * ((nm / c1) / (jnp.sqrt(nv / c2) + ADAM_EPS) + ADAM_WD * w_ref[...])

    blk = pl.BlockSpec((tm, cols), lambda i: (i, 0))
    wblk = pl.BlockSpec((None, tm, cols), lambda i: (0, i, 0)) if lead else blk
    ins = [w, m, v, g0] + ([g1] if two else [])
    return _pcall(
        body, name="adamw", grid=(rows // tm,), in_specs=[wblk] * 3 + [blk] * (len(ins) - 3), out_specs=[wblk] * 4,
        out_shape=[jax.ShapeDtypeStruct(w.shape, F32)] * 4,
        compiler_params=_cp(("parallel",), 40),
    )(*ins)


_BIG = ("w_in", "s5_w_glu", "m_w_out", "w_o", "w_up", "w_down")
_SMALL = ("ln0_g", "ln0_b", "b_in", "qk_conv_b", "s5_lambda_re", "s5_lambda_im", "s5_log_dt", "s5_b_re", "s5_b_im",
          "s5_c_re", "s5_c_im", "s5_d", "m_norm_g", "ln1_g", "ln1_b", "b_up", "ln2_g", "ln2_b")
_SMALL_SHARDED = ("meta_tokens", "qk_conv_w")
_ORDER = ("meta_tokens", "ln0_g", "ln0_b", "w_in", "b_in", "qk_conv_w", "qk_conv_b", "s5_lambda_re", "s5_lambda_im",
          "s5_log_dt", "s5_b_re", "s5_b_im", "s5_c_re", "s5_c_im", "s5_d", "s5_w_glu", "m_norm_g", "m_w_out", "w_o",
          "ln1_g", "ln1_b", "w_up", "b_up", "w_down", "ln2_g", "ln2_b")


def _pack(arrs):
    flat = jnp.concatenate([a.reshape(-1) for a in arrs])
    n = flat.shape[0]
    rows = -(-n // 1024) * 8
    return jnp.pad(flat, (0, rows * 128 - n)).reshape(rows, 128)


def _unpack(packed, shapes):
    flat = packed.reshape(-1)
    out, off = [], 0
    for s in shapes:
        n = math.prod(s)
        out.append(flat[off:off + n].reshape(s))
        off += n
    return out


def kernel(x, meta_tokens, ln0_g, ln0_b, w_in, b_in, qk_conv_w, qk_conv_b, s5_lambda_re, s5_lambda_im, s5_log_dt, s5_b_re, s5_b_im, s5_c_re, s5_c_im, s5_d, s5_w_glu, m_norm_g, m_w_out, w_o, ln1_g, ln1_b, w_up, b_up, w_down, ln2_g, ln2_b, loss_target, m_meta_tokens, m_ln0_g, m_ln0_b, m_w_in, m_b_in, m_qk_conv_w, m_qk_conv_b, m_s5_lambda_re, m_s5_lambda_im, m_s5_log_dt, m_s5_b_re, m_s5_b_im, m_s5_c_re, m_s5_c_im, m_s5_d, m_s5_w_glu, m_m_norm_g, m_m_w_out, m_w_o, m_ln1_g, m_ln1_b, m_w_up, m_b_up, m_w_down, m_ln2_g, m_ln2_b, v_meta_tokens, v_ln0_g, v_ln0_b, v_w_in, v_b_in, v_qk_conv_w, v_qk_conv_b, v_s5_lambda_re, v_s5_lambda_im, v_s5_log_dt, v_s5_b_re, v_s5_b_im, v_s5_c_re, v_s5_c_im, v_s5_d, v_s5_w_glu, v_m_norm_g, v_m_w_out, v_w_o, v_ln1_g, v_ln1_b, v_w_up, v_b_up, v_w_down, v_ln2_g, v_ln2_b):
    wts = dict(meta_tokens=meta_tokens, ln0_g=ln0_g, ln0_b=ln0_b, w_in=w_in, b_in=b_in, qk_conv_w=qk_conv_w,
               qk_conv_b=qk_conv_b, s5_lambda_re=s5_lambda_re, s5_lambda_im=s5_lambda_im, s5_log_dt=s5_log_dt,
               s5_b_re=s5_b_re, s5_b_im=s5_b_im, s5_c_re=s5_c_re, s5_c_im=s5_c_im, s5_d=s5_d, s5_w_glu=s5_w_glu,
               m_norm_g=m_norm_g, m_w_out=m_w_out, w_o=w_o, ln1_g=ln1_g, ln1_b=ln1_b, w_up=w_up, b_up=b_up,
               w_down=w_down, ln2_g=ln2_g, ln2_b=ln2_b)
    mom = dict(meta_tokens=m_meta_tokens, ln0_g=m_ln0_g, ln0_b=m_ln0_b, w_in=m_w_in, b_in=m_b_in, qk_conv_w=m_qk_conv_w,
               qk_conv_b=m_qk_conv_b, s5_lambda_re=m_s5_lambda_re, s5_lambda_im=m_s5_lambda_im, s5_log_dt=m_s5_log_dt,
               s5_b_re=m_s5_b_re, s5_b_im=m_s5_b_im, s5_c_re=m_s5_c_re, s5_c_im=m_s5_c_im, s5_d=m_s5_d,
               s5_w_glu=m_s5_w_glu, m_norm_g=m_m_norm_g, m_w_out=m_m_w_out, w_o=m_w_o, ln1_g=m_ln1_g, ln1_b=m_ln1_b,
               w_up=m_w_up, b_up=m_b_up, w_down=m_w_down, ln2_g=m_ln2_g, ln2_b=m_ln2_b)
    var = dict(meta_tokens=v_meta_tokens, ln0_g=v_ln0_g, ln0_b=v_ln0_b, w_in=v_w_in, b_in=v_b_in, qk_conv_w=v_qk_conv_w,
               qk_conv_b=v_qk_conv_b, s5_lambda_re=v_s5_lambda_re, s5_lambda_im=v_s5_lambda_im, s5_log_dt=v_s5_log_dt,
               s5_b_re=v_s5_b_re, s5_b_im=v_s5_b_im, s5_c_re=v_s5_c_re, s5_c_im=v_s5_c_im, s5_d=v_s5_d,
               s5_w_glu=v_s5_w_glu, m_norm_g=v_m_norm_g, m_w_out=v_m_w_out, w_o=v_w_o, ln1_g=v_ln1_g, ln1_b=v_ln1_b,
               w_up=v_w_up, b_up=v_b_up, w_down=v_w_down, ln2_g=v_ln2_g, ln2_b=v_ln2_b)
    d = x.shape[-1]
    chip = 2 * lax.axis_index("x") + lax.axis_index("y")

    gw = dict(zip(_SMALL_SHARDED, _gather_chips([meta_tokens, qk_conv_w[0]])))
    own_w_in = _bf(w_in[0])
    fsend, frecv, fsrc, fland, ftok = _xchg_start([own_w_in], [lax.empty((4,) + own_w_in.shape, BF16)],
                                                  name="gather_w_in_start", scatter=False)
    late_names = tuple(n for n in _BIG if n != "w_in")
    cat = lambda a: jnp.transpose(a, (1, 0, 2)).reshape(a.shape[1], 4 * a.shape[2])
    w = dict(
        meta_tokens=cat(gw["meta_tokens"]), ln0_g=ln0_g[None], ln0_b=_tie(ln0_b[None], ftok),
        qk_conv_w=cat(gw["qk_conv_w"]), qk_conv_b=qk_conv_b,
        s5_lambda_re=s5_lambda_re[0], s5_lambda_im=s5_lambda_im[0], s5_log_dt=s5_log_dt[0][:, None],
        s5_b_re=s5_b_re[0], s5_b_im=s5_b_im[0], s5_c_re=s5_c_re[0], s5_c_im=s5_c_im[0], s5_d=s5_d,
        m_norm_g=m_norm_g, ln1_g=ln1_g, ln1_b=ln1_b, b_up=b_up, ln2_g=ln2_g, ln2_b=ln2_b)
    in_flight = {}

    def place_own(src, land):
        return lax.dynamic_update_slice(land, src[None], (chip,) + (0,) * src.ndim)

    def early(after):
        src, land = _xchg_wait(fsend, frecv, fsrc, fland, after, name="gather_w_in_wait", scatter=False)
        late_src = [_bf(wts[n][0]) for n in late_names]
        st = _xchg_start(late_src, [lax.empty((4,) + a.shape, a.dtype) for a in late_src], name="gather_late_start",
                         scatter=False, dep=src[0])
        in_flight["late"] = st[:4]
        return dict(w_in=_w_in_from_slots(place_own(src[0], land[0])), b_in=_tie(_to_pad_cols(b_in), st[4]))

    def late(after):
        src, land = _xchg_wait(*in_flight["late"], after, name="gather_late_wait", scatter=False)
        full = {n: place_own(s, ld) for n, s, ld in zip(late_names, src, land)}
        return dict(s5_w_glu=full["s5_w_glu"], m_w_out=full["m_w_out"].reshape(d, d), w_o=full["w_o"].reshape(d, d),
                    w_up=full["w_up"], w_down=full["w_down"].reshape(4 * d, d))

    flying = []

    def ready(names, g):
        parts = dict(
            w_in=lambda: _slots_from_w_in(g["w_in"][0]), s5_w_glu=lambda: g["s5_w_glu"],
            m_w_out=lambda: g["m_w_out"].reshape(4, d // 4, d), w_o=lambda: g["w_o"].reshape(4, d // 4, d),
            w_up=lambda: g["w_up"], w_down=lambda: g["w_down"].reshape(4, d, d))
        src = [parts[n]() for n in names]
        land = [lax.empty((3,) + a.shape[1:], a.dtype) for a in src]
        st = _xchg_start(src, land, name="scatter_" + names[0] + "_start", scatter=True)
        flying.append((names,) + st[:4])
        return st[4]

    loss, grad_x, g = _local_step(x, loss_target, w, early, late, ready)
    g["b_in"] = _from_pad_cols(g["b_in"])

    res = {}

    def finish(groups, after, tag):
        mine = {}
        for names, send, recv, src, land in groups:
            src, land = _xchg_wait(send, recv, src, land, after, name="scatter_" + names[0] + "_wait", scatter=True)
            for n, s, ld in zip(names, src, land):
                mine[n] = _sum_slots(lax.dynamic_index_in_dim(s, chip, 0, keepdims=False), ld)
        theirs = _swap_cores(list(mine.values()), name="swap_cores_" + tag)
        for n, t in zip(mine, theirs):
            res[n] = _adamw(wts[n], mom[n], var[n], mine[n], t)

    finish(flying[:-1], g["ln0_g"], "a")

    small_shapes = [(1, 128)] + [wts[n].shape for n in _SMALL] + [g[n].shape for n in _SMALL_SHARDED]
    packed = _pack([loss] + [g[n] for n in _SMALL] + [g[n] for n in _SMALL_SHARDED])
    tot = _unpack(_allreduce_small(packed), small_shapes)
    loss_out = tot[0][0, 0]
    gsm = dict(zip(_SMALL + _SMALL_SHARDED, tot[1:]))
    for n in _SMALL_SHARDED:
        cols = wts[n].shape[-1]
        gsm[n] = lax.dynamic_slice_in_dim(gsm[n], chip * cols, cols, axis=1).reshape(wts[n].shape)

    names = _SMALL + _SMALL_SHARDED
    shapes = [wts[n].shape for n in names]
    pk = lambda dct: _pack([dct[n] for n in names])
    small_out = _adamw(pk(wts), pk(mom), pk(var), pk(gsm))
    small_res = [_unpack(r, shapes) for r in small_out]
    for j, n in enumerate(names):
        res[n] = [small_res[q][j] for q in range(4)]
    finish(flying[-1:], small_out[0], "b")

    return (loss_out, grad_x, *[res[n][0] for n in _ORDER], *[res[n][1] for n in _ORDER],
            *[res[n][2] for n in _ORDER], *[res[n][3] for n in _ORDER])
```

```python
import functools
import math

import jax
import jax.numpy as jnp
from jax import lax
from jax.experimental import pallas as pl
from jax.experimental.pallas import tpu as pltpu

F32 = jnp.float32
BF16 = jnp.bfloat16
HI = lax.Precision.HIGHEST

N_META = 16
M_HEADS = 4
M_CHUNK = 64
PAD = M_CHUNK - N_META
CONV_W = 4
S5_GROUP = 16
S5_STATE = 64
S5_KCH = 4
LN_EPS = 1e-5
ALPHA = 2.0 ** 0.25
NEG = -1e30
ADAM_LR, ADAM_B1, ADAM_B2, ADAM_EPS, ADAM_WD, ADAM_STEP = 0.001, 0.9, 0.999, 1e-08, 0.01, 10

O_OFF, GS_OFF, GM_OFF, V_OFF, Q_OFF, K_OFF, U_OFF, G_OFF, NP = 0, 1024, 2048, 3072, 4096, 4608, 5120, 5632, 5760

NN = ((1,), (0,))
NT = ((1,), (1,))
TN = ((0,), (0,))


def _dot(a, b, dims=NN, prec=None):
    return lax.dot_general(a, b, (dims, ((), ())), preferred_element_type=F32, precision=prec)


def _bf(x):
    return x.astype(BF16)


def _sig(x):
    return 1.0 / (1.0 + jnp.exp(-x))


def _pcall(body, **kw):
    return pl.pallas_call(body, **kw)


def _cp(sem=None, vmem_mb=None):
    kw = {}
    if sem is not None:
        kw["dimension_semantics"] = sem
    if vmem_mb is not None:
        kw["vmem_limit_bytes"] = vmem_mb << 20
    return pltpu.CompilerParams(**kw)


def _row_tile(n, want, mult=16):
    best = None
    for t in range(mult, want + 1, mult):
        if n % t == 0:
            best = t
    assert best is not None, (n, want)
    return best


def _resident(shape):
    nd = len(shape)
    return pl.BlockSpec(shape, lambda *_: (0,) * nd, pipeline_mode=pl.Buffered(1))


def _const(shape):
    nd = len(shape)
    return pl.BlockSpec(shape, lambda *_: (0,) * nd)


def _ln_fwd(x, g, b):
    mu = jnp.mean(x, axis=-1, keepdims=True)
    xc = x - mu
    var = jnp.mean(xc * xc, axis=-1, keepdims=True)
    rstd = lax.rsqrt(var + LN_EPS)
    xhat = xc * rstd
    return xhat * g + b, xhat, rstd


def _ln_bwd(dy, xhat, rstd, g):
    dxh = dy * g
    m1 = jnp.mean(dxh, axis=-1, keepdims=True)
    m2 = jnp.mean(dxh * xhat, axis=-1, keepdims=True)
    return rstd * (dxh - m1 - xhat * m2)


def _colsum(x):
    return jnp.sum(x, axis=0, keepdims=True)


def _to_pad_cols(w):
    u, q, k, v, o, gi, gf, gs, gm = (w[..., 0:512], w[..., 512:1024], w[..., 1024:1536], w[..., 1536:2560],
                                     w[..., 2560:3584], w[..., 3584:3588], w[..., 3588:3592], w[..., 3592:4616],
                                     w[..., 4616:5640])
    z = jnp.zeros(w.shape[:-1] + (NP - G_OFF - 8,), w.dtype)
    return jnp.concatenate([o, gs, gm, v, q, k, u, gi, gf, z], axis=-1)


def _from_pad_cols(w):
    o, gs, gm, v, q, k, u = (w[..., O_OFF:GS_OFF], w[..., GS_OFF:GM_OFF], w[..., GM_OFF:V_OFF], w[..., V_OFF:Q_OFF],
                             w[..., Q_OFF:K_OFF], w[..., K_OFF:U_OFF], w[..., U_OFF:G_OFF])
    gi, gf = w[..., G_OFF:G_OFF + 4], w[..., G_OFF + 4:G_OFF + 8]
    return jnp.concatenate([u, q, k, v, o, gi, gf, gs, gm], axis=-1)


_IN_REF = (("u", 512), ("q", 512), ("k", 512), ("v", 1024), ("o", 1024), ("i", 4), ("f", 4), ("gs", 1024), ("gm", 1024))
_IN_PAD = (("o", O_OFF), ("gs", GS_OFF), ("gm", GM_OFF), ("v", V_OFF), ("q", Q_OFF), ("k", K_OFF), ("u", U_OFF),
           ("i", G_OFF), ("f", G_OFF + 4))


def _in_ref_ranges():
    out, off = {}, 0
    for n, s in _IN_REF:
        out[n] = (off, off + s)
        off += s
    return out, off


def _w_in_from_slots(g):
    rng, total = _in_ref_ranges()
    width = total // g.shape[0]
    cols = []
    for n, _ in _IN_PAD:
        a, b = rng[n]
        while a < b:
            s = a // width
            e = min(b, (s + 1) * width)
            cols.append(g[s][:, a - s * width:e - s * width])
            a = e
    cols.append(jnp.zeros((g.shape[1], NP - G_OFF - 8), g.dtype))
    return jnp.concatenate(cols, axis=1)


def _slots_from_w_in(wp, nslot=4):
    rng, total = _in_ref_ranges()
    width = total // nslot
    pad_off = dict(_IN_PAD)
    slots = []
    for s in range(nslot):
        lo, hi = s * width, (s + 1) * width
        cols = []
        for n, _ in _IN_REF:
            a, b = rng[n]
            x0, x1 = max(a, lo), min(b, hi)
            if x0 < x1:
                cols.append(wp[:, pad_off[n] + x0 - a:pad_off[n] + x1 - a])
        slots.append(jnp.concatenate(cols, axis=1))
    return jnp.stack(slots, axis=0)


def _ln0_fwd(hin, g, b, lp):
    r, d = hin.shape
    tm = _row_tile(lp, 416)

    def body(x_ref, g_ref, b_ref, o_ref):
        y, _, _ = _ln_fwd(x_ref[...], g_ref[...], b_ref[...])
        o_ref[...] = y

    return _pcall(
        body, name="ln0_fwd", grid=(r // tm,),
        in_specs=[pl.BlockSpec((tm, d), lambda i: (i, 0)), _const((1, d)), _const((1, d))],
        out_specs=pl.BlockSpec((tm, d), lambda i: (i, 0)),
        out_shape=jax.ShapeDtypeStruct((r, d), F32),
        compiler_params=_cp(("parallel",)),
    )(hin, g, b)


def _ln0_bwd(hin, dr1, dpw, g, lp):
    r, d = hin.shape
    tm = _row_tile(lp, 416)
    tps = lp // tm
    assert tm >= PAD + N_META

    def body(x_ref, a_ref, c_ref, g_ref, o_ref, dg_ref, db_ref, dm_ref):
        i = pl.program_id(0)

        @pl.when(i == 0)
        def _():
            dg_ref[...] = jnp.zeros_like(dg_ref)
            db_ref[...] = jnp.zeros_like(db_ref)
            dm_ref[...] = jnp.zeros_like(dm_ref)

        dy = ALPHA * a_ref[...] + c_ref[...]
        _, xhat, rstd = _ln_fwd(x_ref[...], g_ref[...], 0.0)
        dx = _ln_bwd(dy, xhat, rstd, g_ref[...])
        o_ref[...] = dx
        dg_ref[...] += _colsum(dy * xhat)
        db_ref[...] += _colsum(dy)

        @pl.when(i % tps == 0)
        def _():
            dm_ref[...] += dx[PAD:PAD + N_META, :]

    return _pcall(
        body, name="ln0_bwd", grid=(r // tm,),
        in_specs=[pl.BlockSpec((tm, d), lambda i: (i, 0))] * 3 + [_const((1, d))],
        out_specs=[pl.BlockSpec((tm, d), lambda i: (i, 0)), _const((1, d)), _const((1, d)), _const((N_META, d))],
        out_shape=[jax.ShapeDtypeStruct((r, d), F32), jax.ShapeDtypeStruct((1, d), F32),
                   jax.ShapeDtypeStruct((1, d), F32), jax.ShapeDtypeStruct((N_META, d), F32)],
        compiler_params=_cp(("arbitrary",)),
    )(hin, dr1, dpw, g)


def _inproj(h0, w_bf, bias, lp):
    r, d = h0.shape
    n = w_bf.shape[1]
    tm = _row_tile(lp, 832)
    tn = 1152
    tps = lp // tm

    def body(a_ref, w_ref, b_ref, o_ref):
        i = pl.program_id(0)
        acc = _dot(_bf(a_ref[...]), w_ref[...]) + b_ref[...]
        t = (i % tps) * tm + lax.broadcasted_iota(jnp.int32, (tm, 1), 0)
        o_ref[...] = jnp.where(t >= PAD, acc, 0.0)

    return _pcall(
        body, name="inproj", grid=(r // tm, n // tn),
        in_specs=[pl.BlockSpec((tm, d), lambda i, j: (i, 0)), pl.BlockSpec((d, tn), lambda i, j: (0, j)),
                  pl.BlockSpec((1, tn), lambda i, j: (0, j))],
        out_specs=pl.BlockSpec((tm, tn), lambda i, j: (i, j)),
        out_shape=jax.ShapeDtypeStruct((r, n), F32),
        compiler_params=_cp(("parallel", "parallel"), 48),
    )(h0, w_bf, bias)


def _mm_tn(a, b, *, name, split=1, colsum=False, tk_want=832):
    r, m = a.shape
    n = b.shape[1]
    tk = _row_tile(r, tk_want)
    tm = min(m, 1024)
    ns = n // split
    tn = ns
    for cand in (1024, 1152, 640, 512, 128):
        if ns % cand == 0 and cand <= ns:
            tn = cand
            break
    nb = ns // tn
    nk = r // tk

    def body(a_ref, b_ref, o_ref, *rest):
        acc = rest[-1]
        k = pl.program_id(2)

        @pl.when(k == 0)
        def _():
            acc[...] = jnp.zeros_like(acc)

        bt = b_ref[...]
        acc[...] += _dot(_bf(a_ref[...]), _bf(bt), TN)

        @pl.when(k == nk - 1)
        def _():
            o_ref[...] = acc[...]

        if colsum:
            cs_ref = rest[0]

            @pl.when(k == 0)
            def _():
                cs_ref[...] = jnp.zeros_like(cs_ref)

            cs_ref[...] += _colsum(bt.astype(F32))

    out_specs = [pl.BlockSpec((None, tm, tn), lambda i, j, k: (j // nb, i, j % nb))]
    out_shape = [jax.ShapeDtypeStruct((split, m, ns), F32)]
    if colsum:
        assert m == tm
        out_specs.append(pl.BlockSpec((1, tn), lambda i, j, k: (0, j)))
        out_shape.append(jax.ShapeDtypeStruct((1, n), F32))
    res = _pcall(
        body, name=name, grid=(m // tm, n // tn, nk),
        in_specs=[pl.BlockSpec((tk, tm), lambda i, j, k: (k, i)), pl.BlockSpec((tk, tn), lambda i, j, k: (k, j))],
        out_specs=out_specs, out_shape=out_shape,
        scratch_shapes=[pltpu.VMEM((tm, tn), F32)],
        compiler_params=_cp(("parallel", "parallel", "arbitrary"), 48),
    )(a, b)
    return res if colsum else res[0]


def _mm_nt(a, w_bf, lp, *, name, dep=None):
    r, kdim = a.shape
    n = w_bf.shape[0]
    tm = _row_tile(lp, 832)
    tk = 1152
    nk = kdim // tk
    deps = [] if dep is None else [dep]

    def body(a_ref, w_ref, *rest):
        o_ref, acc = rest[-2:]
        k = pl.program_id(1)

        @pl.when(k == 0)
        def _():
            acc[...] = jnp.zeros_like(acc)

        acc[...] += _dot(_bf(a_ref[...]), w_ref[...], NT)

        @pl.when(k == nk - 1)
        def _():
            o_ref[...] = acc[...]

    return _pcall(
        body, name=name, grid=(r // tm, nk),
        in_specs=[pl.BlockSpec((tm, tk), lambda i, k: (i, k)), pl.BlockSpec((n, tk), lambda i, k: (0, k))]
        + [_const(dp_.shape) for dp_ in deps],
        out_specs=pl.BlockSpec((tm, n), lambda i, k: (i, 0)),
        out_shape=jax.ShapeDtypeStruct((r, n), F32),
        scratch_shapes=[pltpu.VMEM((tm, n), F32)],
        compiler_params=_cp(("parallel", "arbitrary"), 48),
    )(a, w_bf, *deps)


def _s5_prep(lam_re, lam_im, log_dt, b_re_t, b_im_t):
    g, p = lam_re.shape
    h = b_re_t.shape[0]

    def body(lr_ref, li_ref, ldt_ref, br_ref, bi_ref, pr_ref, pi_ref, bbr_ref, bbi_ref):
        lr, li = lr_ref[...], li_ref[...]
        dt = jnp.exp(ldt_ref[...])
        e = jnp.exp(lr * dt)
        ar, ai = e * jnp.cos(li * dt), e * jnp.sin(li * dt)
        den = lr * lr + li * li
        cr = ((ar - 1.0) * lr + ai * li) / den
        ci = (ai * lr - (ar - 1.0) * li) / den
        br, bi = br_ref[...], bi_ref[...]
        bbr_ref[...] = cr[None] * br - ci[None] * bi
        bbi_ref[...] = cr[None] * bi + ci[None] * br
        xr, xi = ar, ai
        pr_ref[0] = xr
        pi_ref[0] = xi
        for t in range(1, 8):
            xr, xi = xr * ar - xi * ai, xr * ai + xi * ar
            pr_ref[t] = xr
            pi_ref[t] = xi

    sd = jax.ShapeDtypeStruct
    return _pcall(body, name="s5_prep",
                  out_shape=[sd((8, g, p), F32), sd((8, g, p), F32), sd((h, g, p), F32), sd((h, g, p), F32)])(
        lam_re, lam_im, log_dt, b_re_t, b_im_t)


def _s5_prep_bwd(lam_re, lam_im, log_dt, b_re_t, b_im_t, da_re, da_im, dbb_re_t, dbb_im_t):
    g, p = lam_re.shape
    h = b_re_t.shape[0]

    def body(lr_ref, li_ref, ldt_ref, br_ref, bi_ref, dar_ref, dai_ref, dbr_ref, dbi_ref,
             glr_ref, gli_ref, gdt_ref, gbr_ref, gbi_ref):
        lr, li = lr_ref[...], li_ref[...]
        dt = jnp.exp(ldt_ref[...])
        e = jnp.exp(lr * dt)
        ar, ai = e * jnp.cos(li * dt), e * jnp.sin(li * dt)
        den = lr * lr + li * li
        cr = ((ar - 1.0) * lr + ai * li) / den
        ci = (ai * lr - (ar - 1.0) * li) / den
        br, bi = br_ref[...], bi_ref[...]
        gr, gi = dbr_ref[...], dbi_ref[...]
        gbr_ref[...] = gr * cr[None] + gi * ci[None]
        gbi_ref[...] = gi * cr[None] - gr * ci[None]
        gcr = jnp.sum(gr * br + gi * bi, axis=0)
        gci = jnp.sum(gi * br - gr * bi, axis=0)
        ilr, ili = lr / den, -li / den
        gar = dar_ref[...] + gcr * ilr + gci * ili
        gai = dai_ref[...] + gci * ilr - gcr * ili
        qr, qi = cr * ilr - ci * ili, cr * ili + ci * ilr
        glr = -(gcr * qr + gci * qi)
        gli = -(gci * qr - gcr * qi)
        gzr = gar * ar + gai * ai
        gzi = gai * ar - gar * ai
        glr_ref[...] = glr + gzr * dt
        gli_ref[...] = gli + gzi * dt
        gdt_ref[...] = jnp.sum(gzr * lr + gzi * li, axis=1, keepdims=True) * dt

    sd = jax.ShapeDtypeStruct
    return _pcall(body, name="s5_prep_bwd",
                  out_shape=[sd((g, p), F32), sd((g, p), F32), sd((g, 1), F32), sd((h, g, p), F32), sd((h, g, p), F32)])(
        lam_re, lam_im, log_dt, b_re_t, b_im_t, da_re, da_im, dbb_re_t, dbb_im_t)


def _cmul(xr, xi, yr, yi):
    return xr * yr - xi * yi, xr * yi + xi * yr


def _dot5(a, b, dims=NN):
    return _dot(_bf(a), _bf(b), dims)


def _s5_fwd(p3, bk, cre, cim, apow, dskip):
    bsz, lp, _ = p3.shape
    tt = _row_tile(lp, 520, 8)
    nt = lp // tt
    nblk = tt // 8
    hw = 512

    def body(u_ref, bk_ref, cre_ref, cim_ref, ap_ref, d_ref, y_ref, xs_ref, car_ref):
        t = pl.program_id(2)

        @pl.when(t == 0)
        def _():
            car_ref[...] = jnp.zeros_like(car_ref)

        u = u_ref[...]
        xs_ref[...] = _dot5(u, bk_ref[...])
        ap = ap_ref[...]
        apr, api = ap[:, :hw], ap[:, hw:]
        rows = lax.broadcasted_iota(jnp.int32, (8, hw), 0)

        def blk(i, carry):
            cr, ci = carry
            off = pl.multiple_of(i * 8, 8)
            x = xs_ref[pl.ds(off, 8), :]
            xr, xi = x[:, :hw], x[:, hw:]
            for d in (1, 2, 4):
                sr = jnp.where(rows < d, 0.0, pltpu.roll(xr, d, 0))
                si = jnp.where(rows < d, 0.0, pltpu.roll(xi, d, 0))
                mr, mi = _cmul(sr, si, apr[d - 1:d, :], api[d - 1:d, :])
                xr, xi = xr + mr, xi + mi
            mr, mi = _cmul(apr, api, cr, ci)
            xr, xi = xr + mr, xi + mi
            xs_ref[pl.ds(off, 8), :] = jnp.concatenate([xr, xi], axis=1)
            return xr[7:8, :], xi[7:8, :]

        c0 = car_ref[...]
        cr, ci = lax.fori_loop(0, nblk, blk, (c0[0:1, :hw], c0[0:1, hw:]))
        car_ref[...] = jnp.broadcast_to(jnp.concatenate([cr, ci], axis=1), car_ref.shape)
        xs = xs_ref[...]
        y_ref[...] = (_dot5(xs[:, :hw], cre_ref[...]) - _dot5(xs[:, hw:], cim_ref[...])
                      + d_ref[...] * u)

    ub = U_OFF // 128
    return _pcall(
        body, name="s5_fwd", grid=(S5_KCH, bsz, nt),
        in_specs=[pl.BlockSpec((None, tt, 128), lambda k, b, t: (b, t, ub + k)),
                  pl.BlockSpec((None, 128, 2 * hw), lambda k, b, t: (k, 0, 0)),
                  pl.BlockSpec((None, hw, 128), lambda k, b, t: (k, 0, 0)),
                  pl.BlockSpec((None, hw, 128), lambda k, b, t: (k, 0, 0)),
                  pl.BlockSpec((None, 8, 2 * hw), lambda k, b, t: (k, 0, 0)),
                  pl.BlockSpec((1, 128), lambda k, b, t: (0, k))],
        out_specs=[pl.BlockSpec((None, tt, 128), lambda k, b, t: (b, t, k)),
                   pl.BlockSpec((None, None, tt, 2 * hw), lambda k, b, t: (b, k, t, 0))],
        out_shape=[jax.ShapeDtypeStruct((bsz, lp, S5_KCH * 128), F32),
                   jax.ShapeDtypeStruct((bsz, S5_KCH, lp, 2 * hw), F32)],
        scratch_shapes=[pltpu.VMEM((8, 2 * hw), F32)],
        compiler_params=_cp(("parallel", "parallel", "arbitrary"), 40),
    )(p3, bk, cre, cim, apow, dskip)


def _s5_bwd(dp3, p3, dy3, xs, bk, cre, cim, apow_rev, dskip):
    bsz, lp, _ = p3.shape
    tt = _row_tile(lp, 520, 8)
    nt = lp // tt
    nblk = tt // 8
    hw = 512
    tb = tt // 8

    def body(dp_any, u_ref, dy_ref, xs_ref, halo_ref, bk_ref, cre_ref, cim_ref, ap_ref, d_ref,
             du_ref, dbk_ref, dcre_ref, dcim_ref, da_ref, dd_ref, g_ref, ext_ref, car_ref):
        b = pl.program_id(1)
        t = pl.program_id(2)
        tidx = nt - 1 - t

        @pl.when(t == 0)
        def _():
            car_ref[...] = jnp.zeros_like(car_ref)

        @pl.when((b == 0) & (t == 0))
        def _():
            dbk_ref[...] = jnp.zeros_like(dbk_ref)
            dcre_ref[...] = jnp.zeros_like(dcre_ref)
            dcim_ref[...] = jnp.zeros_like(dcim_ref)
            da_ref[...] = jnp.zeros_like(da_ref)
            dd_ref[...] = jnp.zeros_like(dd_ref)

        u = u_ref[...]
        dy = dy_ref[...]
        g_ref[:, :hw] = _dot5(dy, cre_ref[...], NT)
        g_ref[:, hw:] = -_dot5(dy, cim_ref[...], NT)
        ap = ap_ref[...]
        apr, api = ap[:, :hw], -ap[:, hw:]
        rows = lax.broadcasted_iota(jnp.int32, (8, hw), 0)

        def blk(i, carry):
            cr, ci = carry
            off = pl.multiple_of((nblk - 1 - i) * 8, 8)
            x = g_ref[pl.ds(off, 8), :]
            xr, xi = x[:, :hw], x[:, hw:]
            for d in (1, 2, 4):
                sr = jnp.where(rows >= 8 - d, 0.0, pltpu.roll(xr, 8 - d, 0))
                si = jnp.where(rows >= 8 - d, 0.0, pltpu.roll(xi, 8 - d, 0))
                mr, mi = _cmul(sr, si, apr[8 - d:9 - d, :], api[8 - d:9 - d, :])
                xr, xi = xr + mr, xi + mi
            mr, mi = _cmul(apr, api, cr, ci)
            xr, xi = xr + mr, xi + mi
            g_ref[pl.ds(off, 8), :] = jnp.concatenate([xr, xi], axis=1)
            return xr[0:1, :], xi[0:1, :]

        c0 = car_ref[...]
        cr, ci = lax.fori_loop(0, nblk, blk, (c0[0:1, :hw], c0[0:1, hw:]))
        car_ref[...] = jnp.broadcast_to(jnp.concatenate([cr, ci], axis=1), car_ref.shape)

        gg = g_ref[...]
        du = _dot5(gg, bk_ref[...], NT) + d_ref[...] * dy
        trow = tidx * tt + lax.broadcasted_iota(jnp.int32, (tt, 1), 0)
        du_ref[...] = jnp.where(trow >= PAD, du, 0.0)
        dbk_ref[...] += _dot5(u, gg, TN)
        xsv = xs_ref[...]
        dcre_ref[...] += _dot5(xsv[:, :hw], dy, TN)
        dcim_ref[...] -= _dot5(xsv[:, hw:], dy, TN)
        dd_ref[...] += _colsum(dy * u)
        ext_ref[0:8, :] = jnp.where(tidx == 0, 0.0, halo_ref[...])
        ext_ref[8:, :] = xsv
        xp = ext_ref[pl.ds(7, tt), :]
        gr, gi, pr, pi = gg[:, :hw], gg[:, hw:], xp[:, :hw], xp[:, hw:]
        da_ref[:, :hw] += _colsum(gr * pr + gi * pi)
        da_ref[:, hw:] += _colsum(gi * pr - gr * pi)

    ub = U_OFF // 128
    sd = jax.ShapeDtypeStruct
    rt = lambda t: nt - 1 - t
    res = _pcall(
        body, name="s5_bwd", grid=(S5_KCH, bsz, nt),
        in_specs=[pl.BlockSpec(memory_space=pl.ANY),
                  pl.BlockSpec((None, tt, 128), lambda k, b, t: (b, rt(t), ub + k)),
                  pl.BlockSpec((None, tt, 128), lambda k, b, t: (b, rt(t), k)),
                  pl.BlockSpec((None, None, tt, 2 * hw), lambda k, b, t: (b, k, rt(t), 0)),
                  pl.BlockSpec((None, None, 8, 2 * hw), lambda k, b, t: (b, k, jnp.maximum(rt(t) * tb - 1, 0), 0)),
                  pl.BlockSpec((None, 128, 2 * hw), lambda k, b, t: (k, 0, 0)),
                  pl.BlockSpec((None, hw, 128), lambda k, b, t: (k, 0, 0)),
                  pl.BlockSpec((None, hw, 128), lambda k, b, t: (k, 0, 0)),
                  pl.BlockSpec((None, 8, 2 * hw), lambda k, b, t: (k, 0, 0)),
                  pl.BlockSpec((1, 128), lambda k, b, t: (0, k))],
        out_specs=[pl.BlockSpec((None, tt, 128), lambda k, b, t: (b, rt(t), ub + k)),
                   pl.BlockSpec((None, 128, 2 * hw), lambda k, b, t: (k, 0, 0)),
                   pl.BlockSpec((None, hw, 128), lambda k, b, t: (k, 0, 0)),
                   pl.BlockSpec((None, hw, 128), lambda k, b, t: (k, 0, 0)),
                   pl.BlockSpec((None, 1, 2 * hw), lambda k, b, t: (k, 0, 0)),
                   pl.BlockSpec((1, 128), lambda k, b, t: (0, k))],
        out_shape=[sd(dp3.shape, F32), sd((S5_KCH, 128, 2 * hw), F32), sd((S5_KCH, hw, 128), F32),
                   sd((S5_KCH, hw, 128), F32), sd((S5_KCH, 1, 2 * hw), F32), sd((1, S5_KCH * 128), F32)],
        scratch_shapes=[pltpu.VMEM((tt, 2 * hw), F32), pltpu.VMEM((tt + 8, 2 * hw), F32), pltpu.VMEM((8, 2 * hw), F32)],
        input_output_aliases={0: 0},
        compiler_params=_cp(("arbitrary", "arbitrary", "arbitrary"), 48),
    )(dp3, p3, dy3, xs, xs, bk, cre, cim, apow_rev, dskip)
    return res


_G0 = math.sqrt(2.0 / math.pi)
_G1 = 0.044715


def _gelu(y):
    return 0.5 * y * (1.0 + jnp.tanh(_G0 * (y + _G1 * y * y * y)))


def _gelu_grad(y):
    th = jnp.tanh(_G0 * (y + _G1 * y * y * y))
    return 0.5 * (1.0 + th) + 0.5 * y * (1.0 - th * th) * _G0 * (1.0 + 3.0 * _G1 * y * y)


def _glu_fwd(y_s5, wglu_g, lp):
    r, w = y_s5.shape
    tm = _row_tile(lp, 416)
    cw = wglu_g.shape[2]

    def body(y_ref, w_ref, gy_ref, z_ref, o_ref):
        gy = _bf(_gelu(y_ref[...]))
        gy_ref[...] = gy
        zs = [_dot(gy, w_ref[s]) for s in range(4)]
        for s in range(4):
            z_ref[:, s * cw:(s + 1) * cw] = zs[s]
        o_ref[:, :cw] = zs[0] * _sig(zs[2])
        o_ref[:, cw:] = zs[1] * _sig(zs[3])

    sd = jax.ShapeDtypeStruct
    return _pcall(
        body, name="glu_fwd", grid=(r // tm,),
        in_specs=[pl.BlockSpec((tm, w), lambda i: (i, 0)), _resident(wglu_g.shape)],
        out_specs=[pl.BlockSpec((tm, w), lambda i: (i, 0)), pl.BlockSpec((tm, 4 * cw), lambda i: (i, 0)),
                   pl.BlockSpec((tm, 2 * cw), lambda i: (i, 0))],
        out_shape=[sd((r, w), BF16), sd((r, 4 * cw), F32), sd((r, 2 * cw), F32)],
        compiler_params=_cp(("parallel",), 40),
    )(y_s5, wglu_g)


def _glu_bwd(dyg, z, y_s5, wglu_g, lp):
    r, w = y_s5.shape
    tm = _row_tile(lp, 416)
    cw = wglu_g.shape[2]

    def body(d_ref, z_ref, y_ref, w_ref, dz_ref, dy_ref):
        d = d_ref[...]
        zz = z_ref[...]
        acc = jnp.zeros((tm, w), F32)
        for s in range(2):
            z1 = zz[:, s * cw:(s + 1) * cw]
            sg = _sig(zz[:, (2 + s) * cw:(3 + s) * cw])
            dd = d[:, s * cw:(s + 1) * cw]
            dz1 = _bf(dd * sg)
            dz2 = _bf(dd * z1 * sg * (1.0 - sg))
            dz_ref[:, s * cw:(s + 1) * cw] = dz1
            dz_ref[:, (2 + s) * cw:(3 + s) * cw] = dz2
            acc += _dot(dz1, w_ref[s], NT) + _dot(dz2, w_ref[2 + s], NT)
        dy_ref[...] = acc * _gelu_grad(y_ref[...])

    sd = jax.ShapeDtypeStruct
    return _pcall(
        body, name="glu_bwd", grid=(r // tm,),
        in_specs=[pl.BlockSpec((tm, 2 * cw), lambda i: (i, 0)), pl.BlockSpec((tm, 4 * cw), lambda i: (i, 0)),
                  pl.BlockSpec((tm, w), lambda i: (i, 0)), _resident(wglu_g.shape)],
        out_specs=[pl.BlockSpec((tm, 4 * cw), lambda i: (i, 0)), pl.BlockSpec((tm, w), lambda i: (i, 0))],
        out_shape=[sd((r, 4 * cw), BF16), sd((r, w), F32)],
        compiler_params=_cp(("parallel",), 40),
    )(dyg, z, y_s5, wglu_g)


def _conv_fwd(p3, cw, cb):
    bsz, lp, _ = p3.shape
    tt = _row_tile(lp, 416)
    nt = lp // tt
    tb = tt // 8
    c = cw.shape[1]
    qb = Q_OFF // c

    def body(x_ref, halo_ref, w_ref, b_ref, pre_ref, act_ref, ext_ref):
        t = pl.program_id(1)
        ext_ref[0:8, :] = jnp.where(t == 0, 0.0, halo_ref[...])
        ext_ref[8:, :] = x_ref[...]
        w = w_ref[...]
        acc = b_ref[...] + w[0:1, :] * ext_ref[pl.ds(5, tt), :]
        for j in range(1, CONV_W):
            acc = acc + w[j:j + 1, :] * ext_ref[pl.ds(5 + j, tt), :]
        pre_ref[...] = acc
        act_ref[...] = acc * _sig(acc)

    sd = jax.ShapeDtypeStruct
    return _pcall(
        body, name="conv_fwd", grid=(bsz, nt),
        in_specs=[pl.BlockSpec((None, tt, c), lambda b, t: (b, t, qb)),
                  pl.BlockSpec((None, 8, c), lambda b, t: (b, jnp.maximum(t * tb - 1, 0), qb)),
                  _const((CONV_W, c)), _const((1, c))],
        out_specs=[pl.BlockSpec((None, tt, c), lambda b, t: (b, t, 0))] * 2,
        out_shape=[sd((bsz, lp, c), F32)] * 2,
        scratch_shapes=[pltpu.VMEM((tt + 8, c), F32)],
        compiler_params=_cp(("parallel", "parallel")),
    )(p3, p3, cw, cb)


def _conv_bwd(dp3, p3, dact3, pre3, cw):
    bsz, lp, _ = p3.shape
    tt = _row_tile(lp, 416)
    nt = lp // tt
    tb = tt // 8
    c = cw.shape[1]
    qb = Q_OFF // c

    def silu_grad(x):
        s = _sig(x)
        return s * (1.0 + x * (1.0 - s))

    def body(dp_any, x_ref, xh_ref, d_ref, dh_ref, pre_ref, preh_ref, w_ref, o_ref, dw_ref, db_ref, ext_ref, dext_ref):
        b = pl.program_id(0)
        t = pl.program_id(1)

        @pl.when((b == 0) & (t == 0))
        def _():
            dw_ref[...] = jnp.zeros_like(dw_ref)
            db_ref[...] = jnp.zeros_like(db_ref)

        dc = d_ref[...] * silu_grad(pre_ref[...])
        dch = jnp.where(t == nt - 1, 0.0, dh_ref[...] * silu_grad(preh_ref[...]))
        dext_ref[0:tt, :] = dc
        dext_ref[tt:, :] = dch
        ext_ref[0:8, :] = jnp.where(t == 0, 0.0, xh_ref[...])
        ext_ref[8:, :] = x_ref[...]
        w = w_ref[...]
        acc = w[CONV_W - 1:CONV_W, :] * dc
        for j in range(CONV_W - 1):
            acc = acc + w[j:j + 1, :] * dext_ref[pl.ds(CONV_W - 1 - j, tt), :]
        trow = t * tt + lax.broadcasted_iota(jnp.int32, (tt, 1), 0)
        o_ref[...] = jnp.where(trow >= PAD, acc, 0.0)
        db_ref[...] += _colsum(dc)
        for j in range(CONV_W):
            dw_ref[j:j + 1, :] += _colsum(dc * ext_ref[pl.ds(5 + j, tt), :])

    sd = jax.ShapeDtypeStruct
    nxt = lambda t: jnp.minimum((t + 1) * tb, lp // 8 - 1)
    return _pcall(
        body, name="conv_bwd", grid=(bsz, nt),
        in_specs=[pl.BlockSpec(memory_space=pl.ANY),
                  pl.BlockSpec((None, tt, c), lambda b, t: (b, t, qb)),
                  pl.BlockSpec((None, 8, c), lambda b, t: (b, jnp.maximum(t * tb - 1, 0), qb)),
                  pl.BlockSpec((None, tt, c), lambda b, t: (b, t, 0)),
                  pl.BlockSpec((None, 8, c), lambda b, t: (b, nxt(t), 0)),
                  pl.BlockSpec((None, tt, c), lambda b, t: (b, t, 0)),
                  pl.BlockSpec((None, 8, c), lambda b, t: (b, nxt(t), 0)),
                  _const((CONV_W, c))],
        out_specs=[pl.BlockSpec((None, tt, c), lambda b, t: (b, t, qb)), _const((CONV_W, c)), _const((1, c))],
        out_shape=[sd(dp3.shape, F32), sd((CONV_W, c), F32), sd((1, c), F32)],
        scratch_shapes=[pltpu.VMEM((tt + 8, c), F32), pltpu.VMEM((tt + 8, c), F32)],
        input_output_aliases={0: 0},
        compiler_params=_cp(("arbitrary", "arbitrary")),
    )(dp3, p3, p3, dact3, dact3, pre3, pre3, cw)


def _mlstm_gates(g, h_idx, c_idx, lc):
    lane = lax.broadcasted_iota(jnp.int32, g.shape, 1)
    i_col = jnp.sum(jnp.where(lane == h_idx, g, 0.0), axis=1, keepdims=True)
    f_col = jnp.sum(jnp.where(lane == M_HEADS + h_idx, g, 0.0), axis=1, keepdims=True)
    row = lax.broadcasted_iota(jnp.int32, (lc, 1), 0)
    valid = (c_idx * lc + row) >= PAD
    li = jnp.where(valid, i_col, NEG)
    lf = jnp.where(valid, jnp.minimum(f_col, 0.0) - jnp.log(1.0 + jnp.exp(-jnp.abs(f_col))), 0.0)
    r2 = lax.broadcasted_iota(jnp.int32, (lc, lc), 0)
    c2 = lax.broadcasted_iota(jnp.int32, (lc, lc), 1)
    eye = r2 == c2
    tril = r2 >= c2
    to_row = lambda col: jnp.sum(jnp.where(eye, col, 0.0), axis=0, keepdims=True)
    lf_row = to_row(lf)
    b_col = jnp.sum(jnp.where(tril, lf_row, 0.0), axis=1, keepdims=True)
    b_row = to_row(b_col)
    li_row = to_row(li)
    d_mat = jnp.where(tril, b_col - b_row + li_row, NEG)
    return dict(f_col=f_col, valid=valid, li=li, b_col=b_col, d_mat=d_mat, eye=eye, r2=r2, c2=c2, row=row,
                to_row=to_row)


def _mlstm_chunk(q, ks, v, gq, c_st, n_st, m_st, lc):
    b_col, d_mat = gq["b_col"], gq["d_mat"]
    m_inter = b_col + m_st
    m_row = jnp.maximum(m_inter, jnp.max(d_mat, axis=1, keepdims=True))
    w_intra = jnp.exp(d_mat - m_row)
    w_inter = jnp.exp(m_inter - m_row)
    qb, kb, vb, cb = _bf(q), _bf(ks), _bf(v), _bf(c_st)
    s = _dot(qb, kb, NT) * w_intra
    qc = _dot(qb, cb)
    num = _dot(_bf(s), vb) + w_inter * qc
    qn = jnp.sum(q * n_st, axis=1, keepdims=True)
    den = jnp.sum(s, axis=1, keepdims=True) + w_inter * qn
    e = jnp.exp(-m_row)
    nn = jnp.maximum(jnp.abs(den), e)
    b_last = b_col[lc - 1:lc, :]
    g_log = b_last - b_col + gq["li"]
    m_new = jnp.maximum(b_last + m_st, jnp.max(g_log, axis=0, keepdims=True))
    w_k = jnp.exp(g_log - m_new)
    decay = jnp.exp(b_last + m_st - m_new)
    return dict(w_intra=w_intra, w_inter=w_inter, qb=qb, kb=kb, vb=vb, cb=cb, s=s, qc=qc, num=num, qn=qn, den=den,
                e=e, nn=nn, m_new=m_new, w_k=w_k, decay=decay)


def _chunks_per_step(nc):
    return max(c for c in (5, 4, 2, 1) if nc % c == 0)


def _mlstm_fwd(qk3, p3):
    bsz, lp, _ = p3.shape
    lc = M_CHUNK
    nc = lp // lc
    dk, dv = 128, 256
    scale = dk ** -0.5

    cps = _chunks_per_step(nc)
    rows = cps * lc

    def body(q_ref, k_ref, v_ref, g_ref, h_ref, cs_ref, ns_ref, ms_ref, c_sc, n_sc, m_sc):
        st = pl.program_id(1)

        @pl.when(st == 0)
        def _():
            c_sc[...] = jnp.zeros_like(c_sc)
            n_sc[...] = jnp.zeros_like(n_sc)
            m_sc[...] = jnp.zeros_like(m_sc)

        for j in range(cps):
            rs = slice(j * lc, (j + 1) * lc)
            g = g_ref[rs, :]
            for hh in range(M_HEADS):
                c_st, n_st, m_all = c_sc[hh], n_sc[hh], m_sc[hh]
                cs_ref[hh, j] = c_st
                ns_ref[hh, j] = n_st
                ms_ref[hh, j] = m_all
                m_st = m_all[:, 0:1]
                q = q_ref[rs, hh * dk:(hh + 1) * dk]
                ks = k_ref[rs, hh * dk:(hh + 1) * dk] * scale
                v = v_ref[rs, hh * dv:(hh + 1) * dv]
                gq = _mlstm_gates(g, hh, st * cps + j, lc)
                f = _mlstm_chunk(q, ks, v, gq, c_st, n_st, m_st, lc)
                h_ref[rs, hh * dv:(hh + 1) * dv] = f["num"] / f["nn"]
                kw = ks * f["w_k"]
                c_sc[hh] = f["decay"] * c_st + _dot(_bf(kw), f["vb"], TN)
                n_sc[hh] = f["decay"] * n_st + _colsum(kw)
                m_sc[hh] = jnp.broadcast_to(f["m_new"], (1, 128))

    sd = jax.ShapeDtypeStruct
    nh = M_HEADS
    return _pcall(
        body, name="mlstm_fwd", grid=(bsz, nc // cps),
        in_specs=[pl.BlockSpec((None, rows, nh * dk), lambda b, c: (b, c, 0)),
                  pl.BlockSpec((None, rows, nh * dk), lambda b, c: (b, c, 1)),
                  pl.BlockSpec((None, rows, nh * dv), lambda b, c: (b, c, V_OFF // (nh * dv))),
                  pl.BlockSpec((None, rows, 128), lambda b, c: (b, c, G_OFF // 128))],
        out_specs=[pl.BlockSpec((None, rows, nh * dv), lambda b, c: (b, c, 0)),
                   pl.BlockSpec((None, nh, cps, dk, dv), lambda b, c: (b, 0, c, 0, 0)),
                   pl.BlockSpec((None, nh, cps, 1, dk), lambda b, c: (b, 0, c, 0, 0)),
                   pl.BlockSpec((None, nh, cps, 1, 128), lambda b, c: (b, 0, c, 0, 0))],
        out_shape=[sd((bsz, lp, nh * dv), F32), sd((bsz, nh, nc, dk, dv), F32),
                   sd((bsz, nh, nc, 1, dk), F32), sd((bsz, nh, nc, 1, 128), F32)],
        scratch_shapes=[pltpu.VMEM((nh, dk, dv), F32), pltpu.VMEM((nh, 1, dk), F32), pltpu.VMEM((nh, 1, 128), F32)],
        compiler_params=_cp(("parallel", "arbitrary")),
    )(qk3, qk3, p3, p3)


def _mlstm_bwd(dp3, qk3, p3, dh3, cs, ns, ms):
    bsz, lp, _ = p3.shape
    lc = M_CHUNK
    nc = lp // lc
    dk, dv = 128, 256
    scale = dk ** -0.5

    cps = _chunks_per_step(nc)
    nst = nc // cps
    rows = cps * lc

    def body(dp_any, q_ref, k_ref, v_ref, g_ref, dh_ref, cs_ref, ns_ref, ms_ref,
             dv_ref, dqk_ref, dg_ref, dc_sc, dn_sc):
        t = pl.program_id(1)
        st = nst - 1 - t

        @pl.when(t == 0)
        def _():
            dc_sc[...] = jnp.zeros_like(dc_sc)
            dn_sc[...] = jnp.zeros_like(dn_sc)

        lane = lax.broadcasted_iota(jnp.int32, (lc, 128), 1)
        for j in reversed(range(cps)):
            rs = slice(j * lc, (j + 1) * lc)
            g = g_ref[rs, :]
            dgate = jnp.zeros((lc, 128), F32)
            for hh in range(M_HEADS):
                dgate = head(hh, j, rs, st * cps + j, g, lane, dgate, q_ref, k_ref, v_ref, dh_ref, cs_ref, ns_ref,
                             ms_ref, dv_ref, dqk_ref, dc_sc, dn_sc)
            dg_ref[rs, :] = dgate

    def head(hh, j, sl, c, g, lane, dgate, q_ref, k_ref, v_ref, dh_ref, cs_ref, ns_ref, ms_ref, dv_ref, dqk_ref,
             dc_sc, dn_sc):
        c_st, n_st = cs_ref[hh, j], ns_ref[hh, j]
        m_st = ms_ref[hh, j][:, 0:1]
        q = q_ref[sl, hh * dk:(hh + 1) * dk]
        ks = k_ref[sl, hh * dk:(hh + 1) * dk] * scale
        v = v_ref[sl, hh * dv:(hh + 1) * dv]
        dh = dh_ref[sl, hh * dv:(hh + 1) * dv]
        gq = _mlstm_gates(g, hh, c, lc)
        f = _mlstm_chunk(q, ks, v, gq, c_st, n_st, m_st, lc)
        eye, r2, c2, row, valid = gq["eye"], gq["r2"], gq["c2"], gq["row"], gq["valid"]
        w_intra, w_inter, s, nn, den = f["w_intra"], f["w_inter"], f["s"], f["nn"], f["den"]
        qb, kb, vb, cb, w_k, decay = f["qb"], f["kb"], f["vb"], f["cb"], f["w_k"], f["decay"]
        d_c, d_n = dc_sc[hh], dn_sc[hh]
        d_cb = _bf(d_c)

        hout = f["num"] / nn
        dnum = dh / nn
        d_nn = -jnp.sum(dh * hout, axis=1, keepdims=True) / nn
        dden = jnp.where(jnp.abs(den) > f["e"], d_nn * jnp.sign(den), 0.0)
        wdnum = w_inter * dnum
        wdden = w_inter * dden
        ds = _dot(_bf(dnum), vb, NT) + dden
        dsw = _bf(ds * w_intra)
        dq = _dot(dsw, kb) + _dot(_bf(wdnum), cb, NT) + wdden * n_st
        dkw = _dot(vb, d_cb, NT) + d_n
        dks = _dot(dsw, qb, TN) + dkw * w_k
        kw = ks * w_k
        dvv = _dot(_bf(s), _bf(dnum), TN) + _dot(_bf(kw), d_cb)
        dd = ds * s
        rs = jnp.sum(dd, axis=1, keepdims=True)
        cs_col = jnp.sum(jnp.where(eye, jnp.sum(dd, axis=0, keepdims=True), 0.0), axis=1, keepdims=True)
        dwi = jnp.sum(dnum * f["qc"], axis=1, keepdims=True) + dden * f["qn"]
        db = rs - cs_col + dwi * w_inter
        dli = cs_col
        ddecay = jnp.sum(jnp.sum(d_c * c_st, axis=1, keepdims=True), axis=0, keepdims=True) \
            + jnp.sum(d_n * n_st, axis=1, keepdims=True)
        dgl = jnp.sum(dkw * ks, axis=1, keepdims=True) * w_k
        dblast = ddecay * decay + jnp.sum(dgl, axis=0, keepdims=True)
        db = db - dgl + jnp.where(row == lc - 1, dblast, 0.0)
        dli = dli + dgl
        db_row = gq["to_row"](db)
        dlf = jnp.sum(jnp.where(c2 >= r2, db_row, 0.0), axis=1, keepdims=True)
        dlf = jnp.where(valid, dlf, 0.0)
        dgate = jnp.where(lane == hh, jnp.where(valid, dli, 0.0), dgate)
        dgate = jnp.where(lane == M_HEADS + hh, dlf * _sig(-gq["f_col"]), dgate)
        dqk_ref[sl, hh * dk:(hh + 1) * dk] = dq
        dqk_ref[sl, (M_HEADS + hh) * dk:(M_HEADS + hh + 1) * dk] = dks * scale
        dv_ref[sl, hh * dv:(hh + 1) * dv] = dvv
        dc_sc[hh] = decay * d_c + _dot(qb, _bf(wdnum), TN)
        dn_sc[hh] = decay * d_n + _colsum(q * wdden)
        return dgate

    sd = jax.ShapeDtypeStruct
    nh = M_HEADS
    rc = lambda c: nst - 1 - c
    return _pcall(
        body, name="mlstm_bwd", grid=(bsz, nst),
        in_specs=[pl.BlockSpec(memory_space=pl.ANY),
                  pl.BlockSpec((None, rows, nh * dk), lambda b, c: (b, rc(c), 0)),
                  pl.BlockSpec((None, rows, nh * dk), lambda b, c: (b, rc(c), 1)),
                  pl.BlockSpec((None, rows, nh * dv), lambda b, c: (b, rc(c), V_OFF // (nh * dv))),
                  pl.BlockSpec((None, rows, 128), lambda b, c: (b, rc(c), G_OFF // 128)),
                  pl.BlockSpec((None, rows, nh * dv), lambda b, c: (b, rc(c), 0)),
                  pl.BlockSpec((None, nh, cps, dk, dv), lambda b, c: (b, 0, rc(c), 0, 0)),
                  pl.BlockSpec((None, nh, cps, 1, dk), lambda b, c: (b, 0, rc(c), 0, 0)),
                  pl.BlockSpec((None, nh, cps, 1, 128), lambda b, c: (b, 0, rc(c), 0, 0))],
        out_specs=[pl.BlockSpec((None, rows, nh * dv), lambda b, c: (b, rc(c), V_OFF // (nh * dv))),
                   pl.BlockSpec((None, rows, 2 * nh * dk), lambda b, c: (b, rc(c), 0)),
                   pl.BlockSpec((None, rows, 128), lambda b, c: (b, rc(c), 0))],
        out_shape=[sd(dp3.shape, F32), sd((bsz, lp, 2 * nh * dk), F32), sd((bsz, lp, 128), F32)],
        scratch_shapes=[pltpu.VMEM((nh, dk, dv), F32), pltpu.VMEM((nh, 1, dk), F32)],
        input_output_aliases={0: 0},
        compiler_params=_cp(("arbitrary", "arbitrary")),
    )(dp3, qk3, qk3, p3, p3, dh3, cs, ns, ms)


def _headnorm(x):
    dv = x.shape[1] // M_HEADS
    xh, rs = [], []
    for h in range(M_HEADS):
        xx = x[:, h * dv:(h + 1) * dv]
        mu = jnp.mean(xx, axis=-1, keepdims=True)
        xc = xx - mu
        rstd = lax.rsqrt(jnp.mean(xc * xc, axis=-1, keepdims=True) + LN_EPS)
        xh.append(xc * rstd)
        rs.append(rstd)
    return jnp.concatenate(xh, axis=1), rs


def _mix_fwd(hm, p, ys5g, h0, gn, wmo_bf, wo_bf, g1, b1, lp):
    r, d = hm.shape
    tm = _row_tile(lp, 208)

    def body(hm_ref, o_ref, gs_ref, gm_ref, ys_ref, h0_ref, gn_ref, wmo_ref, wo_ref, g1_ref, b1_ref,
             ymin_ref, ym_ref, mix_ref, r1_ref, h1_ref):
        xhat, _ = _headnorm(hm_ref[...])
        ymin = _bf(_sig(o_ref[...]) * (xhat * gn_ref[...]))
        ymin_ref[...] = ymin
        ym = _dot(ymin, wmo_ref[...])
        ym_ref[...] = ym
        mix = _bf(_sig(gs_ref[...]) * ys_ref[...] + _sig(gm_ref[...]) * ym)
        mix_ref[...] = mix
        r1 = ALPHA * h0_ref[...] + _dot(mix, wo_ref[...])
        r1_ref[...] = r1
        h1, _, _ = _ln_fwd(r1, g1_ref[...], b1_ref[...])
        h1_ref[...] = h1

    sd = jax.ShapeDtypeStruct
    row = pl.BlockSpec((tm, d), lambda i: (i, 0))
    return _pcall(
        body, name="mix_fwd", grid=(r // tm,),
        in_specs=[row, pl.BlockSpec((tm, d), lambda i: (i, O_OFF // d)), pl.BlockSpec((tm, d), lambda i: (i, GS_OFF // d)),
                  pl.BlockSpec((tm, d), lambda i: (i, GM_OFF // d)), row, row, _const((1, d)),
                  _resident((d, d)), _resident((d, d)), _const((1, d)), _const((1, d))],
        out_specs=[row] * 5,
        out_shape=[sd((r, d), BF16), sd((r, d), F32), sd((r, d), BF16), sd((r, d), F32), sd((r, d), F32)],
        compiler_params=_cp(("parallel",), 48),
    )(hm, p, p, p, ys5g, h0, gn, wmo_bf, wo_bf, g1, b1)


def _mix_bwd(dh1, r1, g1, wo_bf, wmo_bf, p, ys5g, ym, hm, gn, lp):
    r, d = hm.shape
    tm = _row_tile(lp, 208)
    dv = d // M_HEADS

    def body(dh1_ref, r1_ref, g1_ref, wo_ref, wmo_ref, o_ref, gs_ref, gm_ref, ys_ref, ym_ref, hm_ref, gn_ref,
             dr1_ref, dp_ref, dys_ref, dym_ref, dhm_ref, dg1_ref, db1_ref, dgn_ref):
        i = pl.program_id(0)

        @pl.when(i == 0)
        def _():
            dg1_ref[...] = jnp.zeros_like(dg1_ref)
            db1_ref[...] = jnp.zeros_like(db1_ref)
            dgn_ref[...] = jnp.zeros_like(dgn_ref)

        dh1 = dh1_ref[...]
        _, xhat1, rstd1 = _ln_fwd(r1_ref[...], g1_ref[...], 0.0)
        dr1 = _ln_bwd(dh1, xhat1, rstd1, g1_ref[...])
        dr1_ref[...] = dr1
        dg1_ref[...] += _colsum(dh1 * xhat1)
        db1_ref[...] += _colsum(dh1)
        dmix = _dot(_bf(dr1), wo_ref[...], NT)
        sgs, sgm, so = _sig(gs_ref[...]), _sig(gm_ref[...]), _sig(o_ref[...])
        dys_ref[...] = dmix * sgs
        dp_ref[:, d:2 * d] = dmix * ys_ref[...] * sgs * (1.0 - sgs)
        dym = dmix * sgm
        dym_ref[...] = _bf(dym)
        dp_ref[:, 2 * d:3 * d] = dmix * ym_ref[...] * sgm * (1.0 - sgm)
        dymin = _dot(_bf(dym), wmo_ref[...], NT)
        xhat, rs = _headnorm(hm_ref[...])
        gn_ = gn_ref[...]
        dp_ref[:, 0:d] = dymin * (xhat * gn_) * so * (1.0 - so)
        dhn = dymin * so
        dgn_ref[...] += _colsum(dhn * xhat)
        dxh = dhn * gn_
        for h in range(M_HEADS):
            sl = slice(h * dv, (h + 1) * dv)
            a, xh = dxh[:, sl], xhat[:, sl]
            m1 = jnp.mean(a, axis=-1, keepdims=True)
            m2 = jnp.mean(a * xh, axis=-1, keepdims=True)
            dhm_ref[:, sl] = rs[h] * (a - m1 - xh * m2)

    sd = jax.ShapeDtypeStruct
    row = pl.BlockSpec((tm, d), lambda i: (i, 0))
    vec = _const((1, d))
    return _pcall(
        body, name="mix_bwd", grid=(r // tm,),
        in_specs=[row, row, vec, _resident((d, d)), _resident((d, d)),
                  pl.BlockSpec((tm, d), lambda i: (i, O_OFF // d)), pl.BlockSpec((tm, d), lambda i: (i, GS_OFF // d)),
                  pl.BlockSpec((tm, d), lambda i: (i, GM_OFF // d)), row, row, row, vec],
        out_specs=[row, pl.BlockSpec((tm, 3 * d), lambda i: (i, 0)), row, row, row, vec, vec, vec],
        out_shape=[sd((r, d), F32), sd((r, NP), F32), sd((r, d), F32), sd((r, d), BF16), sd((r, d), F32),
                   sd((1, d), F32), sd((1, d), F32), sd((1, d), F32)],
        compiler_params=_cp(("arbitrary",), 48),
    )(dh1, r1, g1, wo_bf, wmo_bf, p, p, p, ys5g, ym, hm, gn)


def _mlp_fwd(h1, tgt, wup_g, wdn_bf, bup, g2, b2, lp):
    r, d = h1.shape
    tm = _row_tile(lp, 320)
    tps = lp // tm
    nf = wup_g.shape[0]

    def body(h1_ref, t_ref, wup_ref, wdn_ref, bup_ref, g2_ref, b2_ref, dr2_ref, act_ref, loss_ref, dg2_ref, db2_ref):
        i = pl.program_id(0)

        @pl.when(i == 0)
        def _():
            loss_ref[...] = jnp.zeros_like(loss_ref)
            dg2_ref[...] = jnp.zeros_like(dg2_ref)
            db2_ref[...] = jnp.zeros_like(db2_ref)

        h1 = h1_ref[...]
        h1b = _bf(h1)
        ff = jnp.zeros((tm, d), F32)
        for s in range(nf):
            up = _dot(h1b, wup_ref[s]) + bup_ref[:, s * d:(s + 1) * d]
            a = jnp.maximum(up, 0.0)
            a = _bf(a * a)
            act_ref[:, s * d:(s + 1) * d] = a
            ff = ff + _dot(a, wdn_ref[s * d:(s + 1) * d, :])
        r2 = ALPHA * h1 + ff
        g2 = g2_ref[...]
        y, xhat, rstd = _ln_fwd(r2, g2, b2_ref[...])
        t = (i % tps) * tm + lax.broadcasted_iota(jnp.int32, (tm, 1), 0)
        diff = jnp.where(t >= PAD + N_META, y - t_ref[...], 0.0)
        loss_ref[...] += 0.5 / d * jnp.sum(jnp.sum(diff * diff, axis=1, keepdims=True), axis=0, keepdims=True)
        dy = diff * (1.0 / d)
        dg2_ref[...] += _colsum(dy * xhat)
        db2_ref[...] += _colsum(dy)
        dr2_ref[...] = _ln_bwd(dy, xhat, rstd, g2)

    sd = jax.ShapeDtypeStruct
    row = pl.BlockSpec((tm, d), lambda i: (i, 0))
    vec = _const((1, d))
    return _pcall(
        body, name="mlp_fwd", grid=(r // tm,),
        in_specs=[row, row, _resident(wup_g.shape), _resident(wdn_bf.shape), _const((1, nf * d)), vec, vec],
        out_specs=[row, pl.BlockSpec((tm, nf * d), lambda i: (i, 0)), _const((1, 128)), vec, vec],
        out_shape=[sd((r, d), F32), sd((r, nf * d), BF16), sd((1, 128), F32), sd((1, d), F32), sd((1, d), F32)],
        compiler_params=_cp(("arbitrary",), 56),
    )(h1, tgt, wup_g, wdn_bf, bup, g2, b2)


def _mlp_bwd(h1, dr2, wup_g, wdn_bf, bup, lp):
    r, d = h1.shape
    tm = _row_tile(lp, 320)
    nf = wup_g.shape[0]

    def body(h1_ref, dr2_ref, wup_ref, wdn_ref, bup_ref, dh1_ref, dup_ref, dbup_ref):
        i = pl.program_id(0)

        @pl.when(i == 0)
        def _():
            dbup_ref[...] = jnp.zeros_like(dbup_ref)

        h1b = _bf(h1_ref[...])
        dr2 = dr2_ref[...]
        dr2b = _bf(dr2)
        acc = ALPHA * dr2
        for s in range(nf):
            up = _dot(h1b, wup_ref[s]) + bup_ref[:, s * d:(s + 1) * d]
            dact = _dot(dr2b, wdn_ref[s * d:(s + 1) * d, :], NT)
            dup = dact * (2.0 * jnp.maximum(up, 0.0))
            dbup_ref[:, s * d:(s + 1) * d] += _colsum(dup)
            dupb = _bf(dup)
            dup_ref[:, s * d:(s + 1) * d] = dupb
            acc = acc + _dot(dupb, wup_ref[s], NT)
        dh1_ref[...] = acc

    sd = jax.ShapeDtypeStruct
    row = pl.BlockSpec((tm, d), lambda i: (i, 0))
    return _pcall(
        body, name="mlp_bwd", grid=(r // tm,),
        in_specs=[row, row, _resident(wup_g.shape), _resident(wdn_bf.shape), _const((1, nf * d))],
        out_specs=[row, pl.BlockSpec((tm, nf * d), lambda i: (i, 0)), _const((1, nf * d))],
        out_shape=[sd((r, d), F32), sd((r, nf * d), BF16), sd((1, nf * d), F32)],
        compiler_params=_cp(("arbitrary",), 56),
    )(h1, dr2, wup_g, wdn_bf, bup)


def _s5_block_mats(bb_re_t, bb_im_t, c_re, c_im, ap_re, ap_im):
    ng = c_re.shape[0]
    gl = ng // S5_KCH
    eye = jnp.eye(gl, dtype=F32)

    def bmat(bt):
        bb = jnp.transpose(bt, (1, 0, 2)).reshape(S5_KCH, gl, S5_GROUP, S5_STATE)
        return jnp.einsum("kghp,gj->kghjp", bb, eye).reshape(S5_KCH, gl * S5_GROUP, gl * S5_STATE)

    def cmat(c):
        cc = c.reshape(S5_KCH, gl, S5_GROUP, S5_STATE)
        return jnp.einsum("kghp,gj->kjpgh", cc, eye).reshape(S5_KCH, gl * S5_STATE, gl * S5_GROUP)

    def pw(a):
        return jnp.transpose(a.reshape(8, S5_KCH, gl * S5_STATE), (1, 0, 2))

    bk = jnp.concatenate([bmat(bb_re_t), bmat(bb_im_t)], axis=-1)
    apow = jnp.concatenate([pw(ap_re), pw(ap_im)], axis=-1)
    return _bf(bk), _bf(cmat(c_re)), _bf(cmat(c_im)), apow


def _s5_block_grads(dbk, dcre, dcim, da):
    gl = dbk.shape[1] // S5_GROUP
    ng = gl * S5_KCH
    eye = jnp.eye(gl, dtype=F32)
    hw = gl * S5_STATE

    def bpart(x):
        x = x.reshape(S5_KCH, gl, S5_GROUP, gl, S5_STATE)
        x = jnp.einsum("kghjp,gj->kghp", x, eye).reshape(ng, S5_GROUP, S5_STATE)
        return jnp.transpose(x, (1, 0, 2))

    def cpart(x):
        x = x.reshape(S5_KCH, gl, S5_STATE, gl, S5_GROUP)
        return jnp.einsum("kjpgh,gj->kghp", x, eye).reshape(ng, S5_GROUP, S5_STATE)

    return (bpart(dbk[..., :hw]), bpart(dbk[..., hw:]), cpart(dcre), cpart(dcim),
            da[:, 0, :hw].reshape(ng, S5_STATE), da[:, 0, hw:].reshape(ng, S5_STATE))


def _tie(a, tok):
    return a if tok is None else a + tok[0, 0]


def _local_step(x, tgt, w, early=None, late=None, ready=None):
    ready = ready or (lambda names, g: None)
    bsz, seq, d = x.shape
    lp = PAD + N_META + seq
    r = bsz * lp
    meta = jnp.broadcast_to(w["meta_tokens"][None], (bsz, N_META, d))
    hin = jnp.concatenate([jnp.zeros((bsz, PAD, d), F32), meta, x], axis=1).reshape(r, d)
    tgtp = jnp.concatenate([jnp.zeros((bsz, PAD + N_META, d), F32), tgt], axis=1).reshape(r, d)

    h0 = _ln0_fwd(hin, w["ln0_g"], w["ln0_b"], lp)
    if early is not None:
        w = {**w, **early(h0)}
    p = _inproj(h0, w["w_in"], w["b_in"], lp)
    p3 = p.reshape(bsz, lp, NP)

    b_re_t = jnp.transpose(w["s5_b_re"], (2, 0, 1))
    b_im_t = jnp.transpose(w["s5_b_im"], (2, 0, 1))
    ap_re, ap_im, bb_re_t, bb_im_t = _s5_prep(w["s5_lambda_re"], w["s5_lambda_im"], w["s5_log_dt"], b_re_t, b_im_t)
    bk, cre, cim, apow = _s5_block_mats(bb_re_t, bb_im_t, w["s5_c_re"], w["s5_c_im"], ap_re, ap_im)
    y_s5, xs = _s5_fwd(p3, bk, cre, cim, apow, w["s5_d"])
    sw = y_s5.shape[-1]
    if late is not None:
        w = {**w, **late(y_s5)}
    gy, z, ys5g = _glu_fwd(y_s5.reshape(r, sw), w["s5_w_glu"], lp)

    pre3, qk3 = _conv_fwd(p3, w["qk_conv_w"], w["qk_conv_b"])
    hm3, cs, ns, ms = _mlstm_fwd(qk3, p3)
    hm = hm3.reshape(r, d)
    ymin, ym, mix, r1, h1 = _mix_fwd(hm, p, ys5g, h0, w["m_norm_g"], w["m_w_out"], w["w_o"], w["ln1_g"], w["ln1_b"], lp)
    dr2, act, loss, dg2, db2 = _mlp_fwd(h1, tgtp, w["w_up"], w["w_down"], w["b_up"], w["ln2_g"], w["ln2_b"], lp)

    g = {"ln2_g": dg2, "ln2_b": db2}
    dh1, dup, g["b_up"] = _mlp_bwd(h1, dr2, w["w_up"], w["w_down"], w["b_up"], lp)
    g["w_down"] = _mm_tn(act, dr2, name="dw_down")
    g["w_up"] = _mm_tn(h1, dup, name="dw_up", split=w["w_up"].shape[0])
    tok = ready(("w_down", "w_up"), g)
    dr1, dp, dys5g, dym, dhm, g["ln1_g"], g["ln1_b"], g["m_norm_g"] = _mix_bwd(
        dh1, r1, _tie(w["ln1_g"], tok), w["w_o"], w["m_w_out"], p, ys5g, ym, hm, w["m_norm_g"], lp)
    g["w_o"] = _mm_tn(mix, dr1, name="dw_o")
    g["m_w_out"] = _mm_tn(ymin, dym, name="dw_mout")

    dp3 = dp.reshape(bsz, lp, NP)
    dp3, dqk3, dgate = _mlstm_bwd(dp3, qk3, p3, dhm.reshape(bsz, lp, d), cs, ns, ms)
    dp3, g["qk_conv_w"], g["qk_conv_b"] = _conv_bwd(dp3, p3, dqk3, pre3, w["qk_conv_w"])
    dz, dys5 = _glu_bwd(dys5g, z, y_s5.reshape(r, sw), w["s5_w_glu"], lp)
    g["s5_w_glu"] = _mm_tn(gy, dz, name="dw_glu", split=w["s5_w_glu"].shape[0])
    tok = ready(("s5_w_glu", "m_w_out", "w_o"), g)
    apow_rev = jnp.flip(apow, axis=1)
    dp3, dbk, dcre, dcim, da, g["s5_d"] = _s5_bwd(dp3, p3, dys5.reshape(bsz, lp, sw), xs, bk, cre, cim, apow_rev,
                                                 _tie(w["s5_d"], tok))
    dbb_re_t, dbb_im_t, g["s5_c_re"], g["s5_c_im"], da_re, da_im = _s5_block_grads(dbk, dcre, dcim, da)
    g["s5_lambda_re"], g["s5_lambda_im"], g["s5_log_dt"], gb_re_t, gb_im_t = _s5_prep_bwd(
        w["s5_lambda_re"], w["s5_lambda_im"], w["s5_log_dt"], b_re_t, b_im_t, da_re, da_im, dbb_re_t, dbb_im_t)
    g["s5_b_re"] = jnp.transpose(gb_re_t, (1, 2, 0))
    g["s5_b_im"] = jnp.transpose(gb_im_t, (1, 2, 0))

    dp3 = lax.dynamic_update_slice(dp3, dgate, (0, 0, G_OFF))
    dp = dp3.reshape(r, NP)
    g["w_in"], g["b_in"] = _mm_tn(h0, dp, name="dw_in", colsum=True)
    tok = ready(("w_in",), g)
    dpw = _mm_nt(dp, w["w_in"], lp, name="dh0", dep=tok)
    dhin, g["ln0_g"], g["ln0_b"], g["meta_tokens"] = _ln0_bwd(hin, dr1, dpw, w["ln0_g"], lp)
    grad_x = dhin.reshape(bsz, lp, d)[:, PAD + N_META:]
    return loss, grad_x, g


_ANY = pl.BlockSpec(memory_space=pl.ANY)
_MESH = pl.DeviceIdType.MESH


def _place():
    return lax.axis_index("x"), lax.axis_index("y"), lax.axis_index("c")


def _gather_chips(shards):
    n = len(shards)

    def body(*refs):
        ins, outs = refs[:n], refs[n:2 * n]
        send, recv, loc = refs[2 * n:]
        x, y, c = _place()
        me = 2 * x + y
        peers = [(1 - x, y), (x, 1 - y), (1 - x, 1 - y)]

        def rc(a, k, slot):
            px, py = peers[k]
            return pltpu.make_async_remote_copy(src_ref=ins[a], dst_ref=outs[a].at[slot], send_sem=send.at[a, k],
                                                recv_sem=recv.at[a, k], device_id=(px, py, c), device_id_type=_MESH)

        own = [pltpu.make_async_copy(ins[a], outs[a].at[me], loc.at[a]) for a in range(n)]
        for cp in own:
            cp.start()
        out = [rc(a, k, me) for a in range(n) for k in range(3)]
        for cp in out:
            cp.start()
        for a in range(n):
            for k in range(3):
                rc(a, k, 2 * peers[k][0] + peers[k][1]).wait_recv()
        for cp in out:
            cp.wait_send()
        for cp in own:
            cp.wait()

    return _pcall(
        body, name="gather_chips", in_specs=[_ANY] * n, out_specs=[_ANY] * n,
        out_shape=[jax.ShapeDtypeStruct((4,) + s.shape, s.dtype) for s in shards],
        scratch_shapes=[pltpu.SemaphoreType.DMA((n, 3)), pltpu.SemaphoreType.DMA((n, 3)), pltpu.SemaphoreType.DMA((n,))],
    )(*shards)


_HBM = pl.BlockSpec(memory_space=pltpu.HBM)
_SEM = pl.BlockSpec(memory_space=pltpu.SEMAPHORE)
_EFFECT = pltpu.SideEffectType.DATAFLOW_SIDE_EFFECTING


def _xchg_copies(srcs, lands, send, recv, scatter):
    x, y, c = _place()
    me = 2 * x + y
    peers = [(1 - x, y), (x, 1 - y), (1 - x, 1 - y)]
    out = []
    for a in range(len(srcs)):
        for k, (px, py) in enumerate(peers):
            src = srcs[a].at[2 * px + py] if scatter else srcs[a]
            dst = lands[a].at[k] if scatter else lands[a].at[me]
            out.append(pltpu.make_async_remote_copy(src_ref=src, dst_ref=dst, send_sem=send.at[3 * a + k],
                                                    recv_sem=recv.at[3 * a + k], device_id=(px, py, c),
                                                    device_id_type=_MESH))
    return out


def _xchg_start(srcs, lands, *, name, scatter, dep=None):
    n = len(srcs)
    deps = [] if dep is None else [dep]
    nd = len(deps)

    def body(*refs):
        send, recv = refs[2 * n + nd], refs[2 * n + nd + 1]
        for cp in _xchg_copies(refs[:n], refs[n:2 * n], send, recv, scatter):
            cp.start()
        refs[-1][...] = jnp.zeros_like(refs[-1])

    hbm = lambda a: pltpu.HBM(a.shape, a.dtype)
    con = lambda a: pltpu.with_memory_space_constraint(a, pltpu.HBM)
    res = _pcall(
        body, name=name, in_specs=[_HBM] * (2 * n) + [_ANY] * nd,
        out_specs=[_SEM, _SEM] + [_HBM] * (2 * n) + [pl.BlockSpec(memory_space=pltpu.VMEM)],
        out_shape=[pltpu.SemaphoreType.DMA((3 * n,)), pltpu.SemaphoreType.DMA((3 * n,))]
        + [hbm(a) for a in srcs] + [hbm(a) for a in lands] + [jax.ShapeDtypeStruct((8, 128), F32)],
        input_output_aliases={i: 2 + i for i in range(2 * n)},
        compiler_params=pltpu.CompilerParams(has_side_effects=_EFFECT),
    )(*[con(a) for a in srcs], *[con(a) for a in lands], *deps)
    return res[0], res[1], list(res[2:2 + n]), list(res[2 + n:2 + 2 * n]), res[-1]


def _xchg_wait(send, recv, srcs, lands, after, *, name, scatter):
    n = len(srcs)

    def body(*refs):
        s_ref, r_ref = refs[2 * n], refs[2 * n + 1]
        for cp in _xchg_copies(refs[:n], refs[n:2 * n], s_ref, r_ref, scatter):
            cp.wait_send()
            cp.wait_recv()

    hbm = lambda a: pltpu.HBM(a.shape, a.dtype)
    res = _pcall(
        body, name=name, in_specs=[_HBM] * (2 * n) + [_SEM, _SEM, _ANY],
        out_specs=[_HBM] * (2 * n),
        out_shape=[hbm(a) for a in srcs] + [hbm(a) for a in lands],
        input_output_aliases={i: i for i in range(2 * n)},
        compiler_params=pltpu.CompilerParams(has_side_effects=_EFFECT),
    )(*srcs, *lands, send, recv, after)
    return list(res[:n]), list(res[n:])


def _swap_cores(arrs, name="swap_cores"):
    n = len(arrs)

    def body(*refs):
        ins, outs = refs[:n], refs[n:2 * n]
        send, recv = refs[2 * n:]
        x, y, c = _place()
        cps = [pltpu.make_async_remote_copy(src_ref=ins[a], dst_ref=outs[a], send_sem=send.at[a], recv_sem=recv.at[a],
                                            device_id=(x, y, 1 - c), device_id_type=_MESH) for a in range(n)]
        for cp in cps:
            cp.start()
        for cp in cps:
            cp.wait_recv()
        for cp in cps:
            cp.wait_send()

    return _pcall(
        body, name=name, in_specs=[_ANY] * n, out_specs=[_ANY] * n,
        out_shape=[jax.ShapeDtypeStruct(s.shape, s.dtype) for s in arrs],
        scratch_shapes=[pltpu.SemaphoreType.DMA((n,)), pltpu.SemaphoreType.DMA((n,))],
    )(*arrs)


def _allreduce_small(v):
    rows = v.shape[0]
    half = rows // 2
    assert half % 8 == 0 and 2 * half == rows

    def body(v_ref, out_ref, sib_ref, pair_ref, slots_ref, send, recv):
        x, y, c = _place()
        chip = 2 * x + y
        sibling = (x, y, 1 - c)
        peers = [(1 - x, y), (x, 1 - y), (1 - x, 1 - y)]
        mine = pl.ds(pl.multiple_of(c * half, 8), half)

        first = pltpu.make_async_remote_copy(src_ref=v_ref, dst_ref=sib_ref, send_sem=send.at[0], recv_sem=recv.at[0],
                                             device_id=sibling, device_id_type=_MESH)
        first.start()
        first.wait_recv()
        pair_ref[...] = v_ref[...] + sib_ref[...]
        slots_ref[chip] = pair_ref[mine, :]
        cross = [pltpu.make_async_remote_copy(src_ref=pair_ref.at[mine], dst_ref=slots_ref.at[chip],
                                              send_sem=send.at[1 + k], recv_sem=recv.at[1 + k],
                                              device_id=(px, py, c), device_id_type=_MESH)
                 for k, (px, py) in enumerate(peers)]
        for cp in cross:
            cp.start()
        for cp in cross:
            cp.wait_recv()
        out_ref[mine, :] = ((slots_ref[0] + slots_ref[1]) + slots_ref[2]) + slots_ref[3]
        last = pltpu.make_async_remote_copy(src_ref=out_ref.at[mine], dst_ref=out_ref.at[mine], send_sem=send.at[4],
                                            recv_sem=recv.at[4], device_id=sibling, device_id_type=_MESH)
        last.start()
        last.wait_recv()
        first.wait_send()
        for cp in cross:
            cp.wait_send()
        last.wait_send()

    vm = pl.BlockSpec(memory_space=pltpu.VMEM)
    return _pcall(
        body, name="allreduce_small", in_specs=[vm], out_specs=vm,
        out_shape=jax.ShapeDtypeStruct((rows, 128), F32),
        scratch_shapes=[pltpu.VMEM((rows, 128), F32), pltpu.VMEM((rows, 128), F32), pltpu.VMEM((4, half, 128), F32),
                        pltpu.SemaphoreType.DMA((5,)), pltpu.SemaphoreType.DMA((5,))],
        compiler_params=_cp(None, 40),
    )(v)


def _sum_slots(own, land):
    ns, rows, cols = land.shape
    tm = _row_tile(rows, 256, 8)

    def body(own_ref, a_ref, o_ref):
        o_ref[...] = ((own_ref[...] + a_ref[0]) + a_ref[1]) + a_ref[2]

    return _pcall(
        body, name="sum_slots", grid=(rows // tm,),
        in_specs=[pl.BlockSpec((tm, cols), lambda i: (i, 0)), pl.BlockSpec((ns, tm, cols), lambda i: (0, i, 0))],
        out_specs=pl.BlockSpec((tm, cols), lambda i: (i, 0)),
        out_shape=jax.ShapeDtypeStruct((rows, cols), F32),
        compiler_params=_cp(("parallel",), 40),
    )(own, land)


def _adamw(w, m, v, g0, g1=None):
    rows, cols = w.shape[-2:]
    lead = w.ndim == 3
    tm = _row_tile(rows, 256, 8)
    c1 = 1.0 - ADAM_B1 ** ADAM_STEP
    c2 = 1.0 - ADAM_B2 ** ADAM_STEP
    two = g1 is not None

    def body(*refs):
        w_ref, m_ref, v_ref, g0_ref = refs[:4]
        g_ref, d_ref, nm_ref, nv_ref = refs[-4:]
        g = g0_ref[...]
        if two:
            g = g + refs[4][...]
        nm = ADAM_B1 * m_ref[...] + (1.0 - ADAM_B1) * g
        nv = ADAM_B2 * v_ref[...] + (1.0 - ADAM_B2) * (g * g)
        g_ref[...] = g
        nm_ref[...] = nm
        nv_ref[...] = nv
        d_ref[...] = -ADAM_LR * ((nm / c1) / (jnp.sqrt(nv / c2) + ADAM_EPS) + ADAM_WD * w_ref[...])

    blk = pl.BlockSpec((tm, cols), lambda i: (i, 0))
    wblk = pl.BlockSpec((None, tm, cols), lambda i: (0, i, 0)) if lead else blk
    ins = [w, m, v, g0] + ([g1] if two else [])
    return _pcall(
        body, name="adamw", grid=(rows // tm,), in_specs=[wblk] * 3 + [blk] * (len(ins) - 3), out_specs=[wblk] * 4,
        out_shape=[jax.ShapeDtypeStruct(w.shape, F32)] * 4,
        compiler_params=_cp(("parallel",), 40),
    )(*ins)


_BIG = ("w_in", "s5_w_glu", "m_w_out", "w_o", "w_up", "w_down")
_SMALL = ("ln0_g", "ln0_b", "b_in", "qk_conv_b", "s5_lambda_re", "s5_lambda_im", "s5_log_dt", "s5_b_re", "s5_b_im",
          "s5_c_re", "s5_c_im", "s5_d", "m_norm_g", "ln1_g", "ln1_b", "b_up", "ln2_g", "ln2_b")
_SMALL_SHARDED = ("meta_tokens", "qk_conv_w")
_ORDER = ("meta_tokens", "ln0_g", "ln0_b", "w_in", "b_in", "qk_conv_w", "qk_conv_b", "s5_lambda_re", "s5_lambda_im",
          "s5_log_dt", "s5_b_re", "s5_b_im", "s5_c_re", "s5_c_im", "s5_d", "s5_w_glu", "m_norm_g", "m_w_out", "w_o",
          "ln1_g", "ln1_b", "w_up", "b_up", "w_down", "ln2_g", "ln2_b")


def _pack(arrs):
    flat = jnp.concatenate([a.reshape(-1) for a in arrs])
    n = flat.shape[0]
    rows = -(-n // 2048) * 16
    return jnp.pad(flat, (0, rows * 128 - n)).reshape(rows, 128)


def _unpack(packed, shapes):
    flat = packed.reshape(-1)
    out, off = [], 0
    for s in shapes:
        n = math.prod(s)
        out.append(flat[off:off + n].reshape(s))
        off += n
    return out


def kernel(x, meta_tokens, ln0_g, ln0_b, w_in, b_in, qk_conv_w, qk_conv_b, s5_lambda_re, s5_lambda_im, s5_log_dt, s5_b_re, s5_b_im, s5_c_re, s5_c_im, s5_d, s5_w_glu, m_norm_g, m_w_out, w_o, ln1_g, ln1_b, w_up, b_up, w_down, ln2_g, ln2_b, loss_target, m_meta_tokens, m_ln0_g, m_ln0_b, m_w_in, m_b_in, m_qk_conv_w, m_qk_conv_b, m_s5_lambda_re, m_s5_lambda_im, m_s5_log_dt, m_s5_b_re, m_s5_b_im, m_s5_c_re, m_s5_c_im, m_s5_d, m_s5_w_glu, m_m_norm_g, m_m_w_out, m_w_o, m_ln1_g, m_ln1_b, m_w_up, m_b_up, m_w_down, m_ln2_g, m_ln2_b, v_meta_tokens, v_ln0_g, v_ln0_b, v_w_in, v_b_in, v_qk_conv_w, v_qk_conv_b, v_s5_lambda_re, v_s5_lambda_im, v_s5_log_dt, v_s5_b_re, v_s5_b_im, v_s5_c_re, v_s5_c_im, v_s5_d, v_s5_w_glu, v_m_norm_g, v_m_w_out, v_w_o, v_ln1_g, v_ln1_b, v_w_up, v_b_up, v_w_down, v_ln2_g, v_ln2_b):
    wts = dict(meta_tokens=meta_tokens, ln0_g=ln0_g, ln0_b=ln0_b, w_in=w_in, b_in=b_in, qk_conv_w=qk_conv_w,
               qk_conv_b=qk_conv_b, s5_lambda_re=s5_lambda_re, s5_lambda_im=s5_lambda_im, s5_log_dt=s5_log_dt,
               s5_b_re=s5_b_re, s5_b_im=s5_b_im, s5_c_re=s5_c_re, s5_c_im=s5_c_im, s5_d=s5_d, s5_w_glu=s5_w_glu,
               m_norm_g=m_norm_g, m_w_out=m_w_out, w_o=w_o, ln1_g=ln1_g, ln1_b=ln1_b, w_up=w_up, b_up=b_up,
               w_down=w_down, ln2_g=ln2_g, ln2_b=ln2_b)
    mom = dict(meta_tokens=m_meta_tokens, ln0_g=m_ln0_g, ln0_b=m_ln0_b, w_in=m_w_in, b_in=m_b_in, qk_conv_w=m_qk_conv_w,
               qk_conv_b=m_qk_conv_b, s5_lambda_re=m_s5_lambda_re, s5_lambda_im=m_s5_lambda_im, s5_log_dt=m_s5_log_dt,
               s5_b_re=m_s5_b_re, s5_b_im=m_s5_b_im, s5_c_re=m_s5_c_re, s5_c_im=m_s5_c_im, s5_d=m_s5_d,
               s5_w_glu=m_s5_w_glu, m_norm_g=m_m_norm_g, m_w_out=m_m_w_out, w_o=m_w_o, ln1_g=m_ln1_g, ln1_b=m_ln1_b,
               w_up=m_w_up, b_up=m_b_up, w_down=m_w_down, ln2_g=m_ln2_g, ln2_b=m_ln2_b)
    var = dict(meta_tokens=v_meta_tokens, ln0_g=v_ln0_g, ln0_b=v_ln0_b, w_in=v_w_in, b_in=v_b_in, qk_conv_w=v_qk_conv_w,
               qk_conv_b=v_qk_conv_b, s5_lambda_re=v_s5_lambda_re, s5_lambda_im=v_s5_lambda_im, s5_log_dt=v_s5_log_dt,
               s5_b_re=v_s5_b_re, s5_b_im=v_s5_b_im, s5_c_re=v_s5_c_re, s5_c_im=v_s5_c_im, s5_d=v_s5_d,
               s5_w_glu=v_s5_w_glu, m_norm_g=v_m_norm_g, m_w_out=v_m_w_out, w_o=v_w_o, ln1_g=v_ln1_g, ln1_b=v_ln1_b,
               w_up=v_w_up, b_up=v_b_up, w_down=v_w_down, ln2_g=v_ln2_g, ln2_b=v_ln2_b)
    d = x.shape[-1]
    chip = 2 * lax.axis_index("x") + lax.axis_index("y")

    gw = dict(zip(_SMALL_SHARDED, _gather_chips([meta_tokens, qk_conv_w[0]])))
    own_w_in = _bf(w_in[0])
    fsend, frecv, fsrc, fland, ftok = _xchg_start([own_w_in], [lax.empty((4,) + own_w_in.shape, BF16)],
                                                  name="gather_w_in_start", scatter=False, dep=gw["qk_conv_w"])
    late_names = tuple(n for n in _BIG if n != "w_in")
    cat = lambda a: jnp.transpose(a, (1, 0, 2)).reshape(a.shape[1], 4 * a.shape[2])
    w = dict(
        meta_tokens=cat(gw["meta_tokens"]), ln0_g=ln0_g[None], ln0_b=_tie(ln0_b[None], ftok),
        qk_conv_w=cat(gw["qk_conv_w"]), qk_conv_b=qk_conv_b,
        s5_lambda_re=s5_lambda_re[0], s5_lambda_im=s5_lambda_im[0], s5_log_dt=s5_log_dt[0][:, None],
        s5_b_re=s5_b_re[0], s5_b_im=s5_b_im[0], s5_c_re=s5_c_re[0], s5_c_im=s5_c_im[0], s5_d=s5_d,
        m_norm_g=m_norm_g, ln1_g=ln1_g, ln1_b=ln1_b, b_up=b_up, ln2_g=ln2_g, ln2_b=ln2_b)
    in_flight = {}

    def place_own(src, land):
        return lax.dynamic_update_slice(land, src[None], (chip,) + (0,) * src.ndim)

    def early(after):
        src, land = _xchg_wait(fsend, frecv, fsrc, fland, after, name="gather_w_in_wait", scatter=False)
        late_src = [_bf(wts[n][0]) for n in late_names]
        st = _xchg_start(late_src, [lax.empty((4,) + a.shape, a.dtype) for a in late_src], name="gather_late_start",
                         scatter=False, dep=src[0])
        in_flight["late"] = st[:4]
        return dict(w_in=_w_in_from_slots(place_own(src[0], land[0])), b_in=_tie(_to_pad_cols(b_in), st[4]))

    def late(after):
        src, land = _xchg_wait(*in_flight["late"], after, name="gather_late_wait", scatter=False)
        full = {n: place_own(s, ld) for n, s, ld in zip(late_names, src, land)}
        return dict(s5_w_glu=full["s5_w_glu"], m_w_out=full["m_w_out"].reshape(d, d), w_o=full["w_o"].reshape(d, d),
                    w_up=full["w_up"], w_down=full["w_down"].reshape(4 * d, d))

    flying = []

    def flat(a):
        return jnp.swapaxes(a, -1, -2).reshape(a.shape[:-2] + (-1, 128))

    def unflat(y, shape):
        return jnp.swapaxes(y.reshape(shape[:-2] + (shape[-1], shape[-2])), -1, -2)

    def ready(names, g):
        parts = dict(
            w_in=lambda: flat(_slots_from_w_in(g["w_in"][0])), s5_w_glu=lambda: g["s5_w_glu"],
            m_w_out=lambda: g["m_w_out"].reshape(4, d // 4, d), w_o=lambda: g["w_o"].reshape(4, d // 4, d),
            w_up=lambda: g["w_up"], w_down=lambda: g["w_down"].reshape(4, d, d))
        src = [parts[n]() for n in names]
        land = [lax.empty((3,) + a.shape[1:], a.dtype) for a in src]
        st = _xchg_start(src, land, name="scatter_" + names[0] + "_start", scatter=True)
        flying.append((names,) + st[:4])
        return st[4]

    loss, grad_x, g = _local_step(x, loss_target, w, early, late, ready)
    g["b_in"] = _from_pad_cols(g["b_in"])

    res = {}

    def finish(groups, after, tag):
        mine = {}
        for names, send, recv, src, land in groups:
            src, land = _xchg_wait(send, recv, src, land, after, name="scatter_" + names[0] + "_wait", scatter=True)
            for n, s, ld in zip(names, src, land):
                mine[n] = _sum_slots(lax.dynamic_index_in_dim(s, chip, 0, keepdims=False), ld)
        theirs = _swap_cores(list(mine.values()), name="swap_cores_" + tag)
        for n, t in zip(mine, theirs):
            if n == "w_in":
                res[n] = [unflat(r, wts[n].shape) for r in _adamw(flat(wts[n]), flat(mom[n]), flat(var[n]), mine[n], t)]
            else:
                res[n] = _adamw(wts[n], mom[n], var[n], mine[n], t)

    finish(flying[:-1], g["ln0_g"], "a")

    small_shapes = [(1, 128)] + [wts[n].shape for n in _SMALL] + [g[n].shape for n in _SMALL_SHARDED]
    packed = _pack([loss] + [g[n] for n in _SMALL] + [g[n] for n in _SMALL_SHARDED])
    tot = _unpack(_allreduce_small(packed), small_shapes)
    loss_out = tot[0][0, 0]
    gsm = dict(zip(_SMALL + _SMALL_SHARDED, tot[1:]))
    for n in _SMALL_SHARDED:
        cols = wts[n].shape[-1]
        gsm[n] = lax.dynamic_slice_in_dim(gsm[n], chip * cols, cols, axis=1).reshape(wts[n].shape)

    names = _SMALL + _SMALL_SHARDED
    shapes = [wts[n].shape for n in names]
    pk = lambda dct: _pack([dct[n] for n in names])
    small_out = _adamw(pk(wts), pk(mom), pk(var), pk(gsm))
    small_res = [_unpack(r, shapes) for r in small_out]
    for j, n in enumerate(names):
        res[n] = [small_res[q][j] for q in range(4)]
    finish(flying[-1:], small_out[0], "b")

    return (loss_out, grad_x, *[res[n][0] for n in _ORDER], *[res[n][1] for n in _ORDER],
            *[res[n][2] for n in _ORDER], *[res[n][3] for n in _ORDER])
```

```python
import functools
import math

import jax
import jax.numpy as jnp
from jax import lax
from jax.experimental import pallas as pl
from jax.experimental.pallas import tpu as pltpu

F32 = jnp.float32
BF16 = jnp.bfloat16
HI = lax.Precision.HIGHEST

N_META = 16
M_HEADS = 4
M_CHUNK = 64
PAD = M_CHUNK - N_META
CONV_W = 4
S5_GROUP = 16
S5_STATE = 64
S5_KCH = 4
LN_EPS = 1e-5
ALPHA = 2.0 ** 0.25
NEG = -1e30
ADAM_LR, ADAM_B1, ADAM_B2, ADAM_EPS, ADAM_WD, ADAM_STEP = 0.001, 0.9, 0.999, 1e-08, 0.01, 10

O_OFF, GS_OFF, GM_OFF, V_OFF, Q_OFF, K_OFF, U_OFF, G_OFF, NP = 0, 1024, 2048, 3072, 4096, 4608, 5120, 5632, 5760

NN = ((1,), (0,))
NT = ((1,), (1,))
TN = ((0,), (0,))


def _dot(a, b, dims=NN, prec=None):
    return lax.dot_general(a, b, (dims, ((), ())), preferred_element_type=F32, precision=prec)


def _bf(x):
    return x.astype(BF16)


def _sig(x):
    return 0.5 * jnp.tanh(0.5 * x) + 0.5


def _pcall(body, **kw):
    return pl.pallas_call(body, **kw)


def _cp(sem=None, vmem_mb=None):
    kw = {}
    if sem is not None:
        kw["dimension_semantics"] = sem
    if vmem_mb is not None:
        kw["vmem_limit_bytes"] = vmem_mb << 20
    return pltpu.CompilerParams(**kw)


def _row_tile(n, want, mult=16):
    best = None
    for t in range(mult, want + 1, mult):
        if n % t == 0:
            best = t
    assert best is not None, (n, want)
    return best


def _resident(shape):
    nd = len(shape)
    return pl.BlockSpec(shape, lambda *_: (0,) * nd, pipeline_mode=pl.Buffered(1))


def _const(shape):
    nd = len(shape)
    return pl.BlockSpec(shape, lambda *_: (0,) * nd)


def _ln_fwd(x, g, b):
    mu = jnp.mean(x, axis=-1, keepdims=True)
    xc = x - mu
    var = jnp.mean(xc * xc, axis=-1, keepdims=True)
    rstd = lax.rsqrt(var + LN_EPS)
    xhat = xc * rstd
    return xhat * g + b, xhat, rstd


def _ln_bwd(dy, xhat, rstd, g):
    dxh = dy * g
    m1 = jnp.mean(dxh, axis=-1, keepdims=True)
    m2 = jnp.mean(dxh * xhat, axis=-1, keepdims=True)
    return rstd * (dxh - m1 - xhat * m2)


def _colsum(x):
    return jnp.sum(x, axis=0, keepdims=True)


def _to_pad_cols(w):
    u, q, k, v, o, gi, gf, gs, gm = (w[..., 0:512], w[..., 512:1024], w[..., 1024:1536], w[..., 1536:2560],
                                     w[..., 2560:3584], w[..., 3584:3588], w[..., 3588:3592], w[..., 3592:4616],
                                     w[..., 4616:5640])
    z = jnp.zeros(w.shape[:-1] + (NP - G_OFF - 8,), w.dtype)
    return jnp.concatenate([o, gs, gm, v, q, k, u, gi, gf, z], axis=-1)


def _from_pad_cols(w):
    o, gs, gm, v, q, k, u = (w[..., O_OFF:GS_OFF], w[..., GS_OFF:GM_OFF], w[..., GM_OFF:V_OFF], w[..., V_OFF:Q_OFF],
                             w[..., Q_OFF:K_OFF], w[..., K_OFF:U_OFF], w[..., U_OFF:G_OFF])
    gi, gf = w[..., G_OFF:G_OFF + 4], w[..., G_OFF + 4:G_OFF + 8]
    return jnp.concatenate([u, q, k, v, o, gi, gf, gs, gm], axis=-1)


_IN_REF = (("u", 512), ("q", 512), ("k", 512), ("v", 1024), ("o", 1024), ("i", 4), ("f", 4), ("gs", 1024), ("gm", 1024))
_IN_PAD = (("o", O_OFF), ("gs", GS_OFF), ("gm", GM_OFF), ("v", V_OFF), ("q", Q_OFF), ("k", K_OFF), ("u", U_OFF),
           ("i", G_OFF), ("f", G_OFF + 4))


def _in_ref_ranges():
    out, off = {}, 0
    for n, s in _IN_REF:
        out[n] = (off, off + s)
        off += s
    return out, off


def _w_in_from_slots(g):
    rng, total = _in_ref_ranges()
    width = total // g.shape[0]
    cols = []
    for n, _ in _IN_PAD:
        a, b = rng[n]
        while a < b:
            s = a // width
            e = min(b, (s + 1) * width)
            cols.append(g[s][:, a - s * width:e - s * width])
            a = e
    cols.append(jnp.zeros((g.shape[1], NP - G_OFF - 8), g.dtype))
    return jnp.concatenate(cols, axis=1)


def _slots_from_w_in(wp, nslot=4):
    rng, total = _in_ref_ranges()
    width = total // nslot
    pad_off = dict(_IN_PAD)
    slots = []
    for s in range(nslot):
        lo, hi = s * width, (s + 1) * width
        cols = []
        for n, _ in _IN_REF:
            a, b = rng[n]
            x0, x1 = max(a, lo), min(b, hi)
            if x0 < x1:
                cols.append(wp[:, pad_off[n] + x0 - a:pad_off[n] + x1 - a])
        slots.append(jnp.concatenate(cols, axis=1))
    return jnp.stack(slots, axis=0)


def _ln0_fwd(hin, g, b, lp):
    r, d = hin.shape
    tm = _row_tile(lp, 416)

    def body(x_ref, g_ref, b_ref, o_ref):
        y, _, _ = _ln_fwd(x_ref[...], g_ref[...], b_ref[...])
        o_ref[...] = y

    return _pcall(
        body, name="ln0_fwd", grid=(r // tm,),
        in_specs=[pl.BlockSpec((tm, d), lambda i: (i, 0)), _const((1, d)), _const((1, d))],
        out_specs=pl.BlockSpec((tm, d), lambda i: (i, 0)),
        out_shape=jax.ShapeDtypeStruct((r, d), F32),
        compiler_params=_cp(("parallel",)),
    )(hin, g, b)


def _ln0_bwd(hin, dr1, dpw, g, lp):
    r, d = hin.shape
    tm = _row_tile(lp, 416)
    tps = lp // tm
    assert tm >= PAD + N_META

    def body(x_ref, a_ref, c_ref, g_ref, o_ref, dg_ref, db_ref, dm_ref):
        i = pl.program_id(0)

        @pl.when(i == 0)
        def _():
            dg_ref[...] = jnp.zeros_like(dg_ref)
            db_ref[...] = jnp.zeros_like(db_ref)
            dm_ref[...] = jnp.zeros_like(dm_ref)

        dy = ALPHA * a_ref[...] + c_ref[...]
        _, xhat, rstd = _ln_fwd(x_ref[...], g_ref[...], 0.0)
        dx = _ln_bwd(dy, xhat, rstd, g_ref[...])
        o_ref[...] = dx
        dg_ref[...] += _colsum(dy * xhat)
        db_ref[...] += _colsum(dy)

        @pl.when(i % tps == 0)
        def _():
            dm_ref[...] += dx[PAD:PAD + N_META, :]

    return _pcall(
        body, name="ln0_bwd", grid=(r // tm,),
        in_specs=[pl.BlockSpec((tm, d), lambda i: (i, 0))] * 3 + [_const((1, d))],
        out_specs=[pl.BlockSpec((tm, d), lambda i: (i, 0)), _const((1, d)), _const((1, d)), _const((N_META, d))],
        out_shape=[jax.ShapeDtypeStruct((r, d), F32), jax.ShapeDtypeStruct((1, d), F32),
                   jax.ShapeDtypeStruct((1, d), F32), jax.ShapeDtypeStruct((N_META, d), F32)],
        compiler_params=_cp(("arbitrary",)),
    )(hin, dr1, dpw, g)


def _inproj(h0, w_bf, bias, lp):
    r, d = h0.shape
    n = w_bf.shape[1]
    tm = _row_tile(lp, 832)
    tn = 1152
    tps = lp // tm

    def body(a_ref, w_ref, b_ref, o_ref):
        i = pl.program_id(0)
        acc = _dot(_bf(a_ref[...]), w_ref[...]) + b_ref[...]
        t = (i % tps) * tm + lax.broadcasted_iota(jnp.int32, (tm, 1), 0)
        o_ref[...] = jnp.where(t >= PAD, acc, 0.0)

    return _pcall(
        body, name="inproj", grid=(r // tm, n // tn),
        in_specs=[pl.BlockSpec((tm, d), lambda i, j: (i, 0)), pl.BlockSpec((d, tn), lambda i, j: (0, j)),
                  pl.BlockSpec((1, tn), lambda i, j: (0, j))],
        out_specs=pl.BlockSpec((tm, tn), lambda i, j: (i, j)),
        out_shape=jax.ShapeDtypeStruct((r, n), F32),
        compiler_params=_cp(("parallel", "parallel"), 48),
    )(h0, w_bf, bias)


def _mm_tn(a, b, *, name, split=1, colsum=False, tk_want=832):
    r, m = a.shape
    n = b.shape[1]
    tk = _row_tile(r, tk_want)
    tm = min(m, 1024)
    ns = n // split
    tn = ns
    for cand in (1024, 1152, 640, 512, 128):
        if ns % cand == 0 and cand <= ns:
            tn = cand
            break
    nb = ns // tn
    nk = r // tk

    def body(a_ref, b_ref, o_ref, *rest):
        acc = rest[-1]
        k = pl.program_id(2)

        @pl.when(k == 0)
        def _():
            acc[...] = jnp.zeros_like(acc)

        bt = b_ref[...]
        acc[...] += _dot(_bf(a_ref[...]), _bf(bt), TN)

        @pl.when(k == nk - 1)
        def _():
            o_ref[...] = acc[...]

        if colsum:
            cs_ref = rest[0]

            @pl.when(k == 0)
            def _():
                cs_ref[...] = jnp.zeros_like(cs_ref)

            cs_ref[...] += _colsum(bt.astype(F32))

    out_specs = [pl.BlockSpec((None, tm, tn), lambda i, j, k: (j // nb, i, j % nb))]
    out_shape = [jax.ShapeDtypeStruct((split, m, ns), F32)]
    if colsum:
        assert m == tm
        out_specs.append(pl.BlockSpec((1, tn), lambda i, j, k: (0, j)))
        out_shape.append(jax.ShapeDtypeStruct((1, n), F32))
    res = _pcall(
        body, name=name, grid=(m // tm, n // tn, nk),
        in_specs=[pl.BlockSpec((tk, tm), lambda i, j, k: (k, i)), pl.BlockSpec((tk, tn), lambda i, j, k: (k, j))],
        out_specs=out_specs, out_shape=out_shape,
        scratch_shapes=[pltpu.VMEM((tm, tn), F32)],
        compiler_params=_cp(("parallel", "parallel", "arbitrary"), 48),
    )(a, b)
    return res if colsum else res[0]


def _mm_nt(a, w_bf, lp, *, name, dep=None):
    r, kdim = a.shape
    n = w_bf.shape[0]
    tm = _row_tile(lp, 832)
    tk = 1152
    nk = kdim // tk
    deps = [] if dep is None else [dep]

    def body(a_ref, w_ref, *rest):
        o_ref, acc = rest[-2:]
        k = pl.program_id(1)

        @pl.when(k == 0)
        def _():
            acc[...] = jnp.zeros_like(acc)

        acc[...] += _dot(_bf(a_ref[...]), w_ref[...], NT)

        @pl.when(k == nk - 1)
        def _():
            o_ref[...] = acc[...]

    return _pcall(
        body, name=name, grid=(r // tm, nk),
        in_specs=[pl.BlockSpec((tm, tk), lambda i, k: (i, k)), pl.BlockSpec((n, tk), lambda i, k: (0, k))]
        + [_const(dp_.shape) for dp_ in deps],
        out_specs=pl.BlockSpec((tm, n), lambda i, k: (i, 0)),
        out_shape=jax.ShapeDtypeStruct((r, n), F32),
        scratch_shapes=[pltpu.VMEM((tm, n), F32)],
        compiler_params=_cp(("parallel", "arbitrary"), 48),
    )(a, w_bf, *deps)


def _s5_prep(lam_re, lam_im, log_dt, b_re_t, b_im_t):
    g, p = lam_re.shape
    h = b_re_t.shape[0]

    def body(lr_ref, li_ref, ldt_ref, br_ref, bi_ref, pr_ref, pi_ref, bbr_ref, bbi_ref):
        lr, li = lr_ref[...], li_ref[...]
        dt = jnp.exp(ldt_ref[...])
        e = jnp.exp(lr * dt)
        ar, ai = e * jnp.cos(li * dt), e * jnp.sin(li * dt)
        den = lr * lr + li * li
        cr = ((ar - 1.0) * lr + ai * li) / den
        ci = (ai * lr - (ar - 1.0) * li) / den
        br, bi = br_ref[...], bi_ref[...]
        bbr_ref[...] = cr[None] * br - ci[None] * bi
        bbi_ref[...] = cr[None] * bi + ci[None] * br
        xr, xi = ar, ai
        pr_ref[0] = xr
        pi_ref[0] = xi
        for t in range(1, 8):
            xr, xi = xr * ar - xi * ai, xr * ai + xi * ar
            pr_ref[t] = xr
            pi_ref[t] = xi

    sd = jax.ShapeDtypeStruct
    return _pcall(body, name="s5_prep",
                  out_shape=[sd((8, g, p), F32), sd((8, g, p), F32), sd((h, g, p), F32), sd((h, g, p), F32)])(
        lam_re, lam_im, log_dt, b_re_t, b_im_t)


def _s5_prep_bwd(lam_re, lam_im, log_dt, b_re_t, b_im_t, da_re, da_im, dbb_re_t, dbb_im_t):
    g, p = lam_re.shape
    h = b_re_t.shape[0]

    def body(lr_ref, li_ref, ldt_ref, br_ref, bi_ref, dar_ref, dai_ref, dbr_ref, dbi_ref,
             glr_ref, gli_ref, gdt_ref, gbr_ref, gbi_ref):
        lr, li = lr_ref[...], li_ref[...]
        dt = jnp.exp(ldt_ref[...])
        e = jnp.exp(lr * dt)
        ar, ai = e * jnp.cos(li * dt), e * jnp.sin(li * dt)
        den = lr * lr + li * li
        cr = ((ar - 1.0) * lr + ai * li) / den
        ci = (ai * lr - (ar - 1.0) * li) / den
        br, bi = br_ref[...], bi_ref[...]
        gr, gi = dbr_ref[...], dbi_ref[...]
        gbr_ref[...] = gr * cr[None] + gi * ci[None]
        gbi_ref[...] = gi * cr[None] - gr * ci[None]
        gcr = jnp.sum(gr * br + gi * bi, axis=0)
        gci = jnp.sum(gi * br - gr * bi, axis=0)
        ilr, ili = lr / den, -li / den
        gar = dar_ref[...] + gcr * ilr + gci * ili
        gai = dai_ref[...] + gci * ilr - gcr * ili
        qr, qi = cr * ilr - ci * ili, cr * ili + ci * ilr
        glr = -(gcr * qr + gci * qi)
        gli = -(gci * qr - gcr * qi)
        gzr = gar * ar + gai * ai
        gzi = gai * ar - gar * ai
        glr_ref[...] = glr + gzr * dt
        gli_ref[...] = gli + gzi * dt
        gdt_ref[...] = jnp.sum(gzr * lr + gzi * li, axis=1, keepdims=True) * dt

    sd = jax.ShapeDtypeStruct
    return _pcall(body, name="s5_prep_bwd",
                  out_shape=[sd((g, p), F32), sd((g, p), F32), sd((g, 1), F32), sd((h, g, p), F32), sd((h, g, p), F32)])(
        lam_re, lam_im, log_dt, b_re_t, b_im_t, da_re, da_im, dbb_re_t, dbb_im_t)


def _cmul(xr, xi, yr, yi):
    return xr * yr - xi * yi, xr * yi + xi * yr


def _dot5(a, b, dims=NN):
    return _dot(_bf(a), _bf(b), dims)


def _s5_fwd(p3, bk, cre, cim, apow, dskip):
    bsz, lp, _ = p3.shape
    tt = _row_tile(lp, 520, 8)
    nt = lp // tt
    nblk = tt // 8
    hw = 512

    def body(u_ref, bk_ref, cre_ref, cim_ref, ap_ref, d_ref, y_ref, xs_ref, car_ref):
        t = pl.program_id(2)

        @pl.when(t == 0)
        def _():
            car_ref[...] = jnp.zeros_like(car_ref)

        u = u_ref[...]
        xs_ref[...] = _dot5(u, bk_ref[...])
        ap = ap_ref[...]
        apr, api = ap[:, :hw], ap[:, hw:]
        rows = lax.broadcasted_iota(jnp.int32, (8, hw), 0)
        lev = [(d, jnp.where(rows < d, 0.0, jnp.broadcast_to(apr[d - 1:d, :], (8, hw))),
                jnp.where(rows < d, 0.0, jnp.broadcast_to(api[d - 1:d, :], (8, hw)))) for d in (1, 2, 4)]

        def blk(i, carry):
            cr, ci = carry
            off = pl.multiple_of(i * 8, 8)
            x = xs_ref[pl.ds(off, 8), :]
            xr, xi = x[:, :hw], x[:, hw:]
            for d, lr, li in lev:
                mr, mi = _cmul(pltpu.roll(xr, d, 0), pltpu.roll(xi, d, 0), lr, li)
                xr, xi = xr + mr, xi + mi
            mr, mi = _cmul(apr, api, cr, ci)
            xr, xi = xr + mr, xi + mi
            xs_ref[pl.ds(off, 8), :] = jnp.concatenate([xr, xi], axis=1)
            return xr[7:8, :], xi[7:8, :]

        c0 = car_ref[...]
        cr, ci = lax.fori_loop(0, nblk, blk, (c0[0:1, :hw], c0[0:1, hw:]))
        car_ref[...] = jnp.broadcast_to(jnp.concatenate([cr, ci], axis=1), car_ref.shape)
        xs = xs_ref[...]
        y_ref[...] = (_dot5(xs[:, :hw], cre_ref[...]) - _dot5(xs[:, hw:], cim_ref[...])
                      + d_ref[...] * u)

    ub = U_OFF // 128
    return _pcall(
        body, name="s5_fwd", grid=(S5_KCH, bsz, nt),
        in_specs=[pl.BlockSpec((None, tt, 128), lambda k, b, t: (b, t, ub + k)),
                  pl.BlockSpec((None, 128, 2 * hw), lambda k, b, t: (k, 0, 0)),
                  pl.BlockSpec((None, hw, 128), lambda k, b, t: (k, 0, 0)),
                  pl.BlockSpec((None, hw, 128), lambda k, b, t: (k, 0, 0)),
                  pl.BlockSpec((None, 8, 2 * hw), lambda k, b, t: (k, 0, 0)),
                  pl.BlockSpec((1, 128), lambda k, b, t: (0, k))],
        out_specs=[pl.BlockSpec((None, tt, 128), lambda k, b, t: (b, t, k)),
                   pl.BlockSpec((None, None, tt, 2 * hw), lambda k, b, t: (b, k, t, 0))],
        out_shape=[jax.ShapeDtypeStruct((bsz, lp, S5_KCH * 128), F32),
                   jax.ShapeDtypeStruct((bsz, S5_KCH, lp, 2 * hw), F32)],
        scratch_shapes=[pltpu.VMEM((8, 2 * hw), F32)],
        compiler_params=_cp(("parallel", "parallel", "arbitrary"), 40),
    )(p3, bk, cre, cim, apow, dskip)


def _s5_bwd(dp3, p3, dy3, xs, bk, cre, cim, apow_rev, dskip):
    bsz, lp, _ = p3.shape
    tt = _row_tile(lp, 520, 8)
    nt = lp // tt
    nblk = tt // 8
    hw = 512
    tb = tt // 8

    def body(dp_any, u_ref, dy_ref, xs_ref, halo_ref, bk_ref, cre_ref, cim_ref, ap_ref, d_ref,
             du_ref, dbk_ref, dcre_ref, dcim_ref, da_ref, dd_ref, g_ref, ext_ref, car_ref):
        b = pl.program_id(1)
        t = pl.program_id(2)
        tidx = nt - 1 - t

        @pl.when(t == 0)
        def _():
            car_ref[...] = jnp.zeros_like(car_ref)

        @pl.when((b == 0) & (t == 0))
        def _():
            dbk_ref[...] = jnp.zeros_like(dbk_ref)
            dcre_ref[...] = jnp.zeros_like(dcre_ref)
            dcim_ref[...] = jnp.zeros_like(dcim_ref)
            da_ref[...] = jnp.zeros_like(da_ref)
            dd_ref[...] = jnp.zeros_like(dd_ref)

        u = u_ref[...]
        dy = dy_ref[...]
        g_ref[:, :hw] = _dot5(dy, cre_ref[...], NT)
        g_ref[:, hw:] = -_dot5(dy, cim_ref[...], NT)
        ap = ap_ref[...]
        apr, api = ap[:, :hw], -ap[:, hw:]
        rows = lax.broadcasted_iota(jnp.int32, (8, hw), 0)
        lev = [(d, jnp.where(rows >= 8 - d, 0.0, jnp.broadcast_to(apr[8 - d:9 - d, :], (8, hw))),
                jnp.where(rows >= 8 - d, 0.0, jnp.broadcast_to(api[8 - d:9 - d, :], (8, hw)))) for d in (1, 2, 4)]

        def blk(i, carry):
            cr, ci = carry
            off = pl.multiple_of((nblk - 1 - i) * 8, 8)
            x = g_ref[pl.ds(off, 8), :]
            xr, xi = x[:, :hw], x[:, hw:]
            for d, lr, li in lev:
                mr, mi = _cmul(pltpu.roll(xr, 8 - d, 0), pltpu.roll(xi, 8 - d, 0), lr, li)
                xr, xi = xr + mr, xi + mi
            mr, mi = _cmul(apr, api, cr, ci)
            xr, xi = xr + mr, xi + mi
            g_ref[pl.ds(off, 8), :] = jnp.concatenate([xr, xi], axis=1)
            return xr[0:1, :], xi[0:1, :]

        c0 = car_ref[...]
        cr, ci = lax.fori_loop(0, nblk, blk, (c0[0:1, :hw], c0[0:1, hw:]))
        car_ref[...] = jnp.broadcast_to(jnp.concatenate([cr, ci], axis=1), car_ref.shape)

        gg = g_ref[...]
        du = _dot5(gg, bk_ref[...], NT) + d_ref[...] * dy
        trow = tidx * tt + lax.broadcasted_iota(jnp.int32, (tt, 1), 0)
        du_ref[...] = jnp.where(trow >= PAD, du, 0.0)
        dbk_ref[...] += _dot5(u, gg, TN)
        xsv = xs_ref[...]
        dcre_ref[...] += _dot5(xsv[:, :hw], dy, TN)
        dcim_ref[...] -= _dot5(xsv[:, hw:], dy, TN)
        dd_ref[...] += _colsum(dy * u)
        ext_ref[0:8, :] = jnp.where(tidx == 0, 0.0, halo_ref[...])
        ext_ref[8:, :] = xsv
        xp = ext_ref[pl.ds(7, tt), :]
        gr, gi, pr, pi = gg[:, :hw], gg[:, hw:], xp[:, :hw], xp[:, hw:]
        da_ref[:, :hw] += _colsum(gr * pr + gi * pi)
        da_ref[:, hw:] += _colsum(gi * pr - gr * pi)

    ub = U_OFF // 128
    sd = jax.ShapeDtypeStruct
    rt = lambda t: nt - 1 - t
    res = _pcall(
        body, name="s5_bwd", grid=(S5_KCH, bsz, nt),
        in_specs=[pl.BlockSpec(memory_space=pl.ANY),
                  pl.BlockSpec((None, tt, 128), lambda k, b, t: (b, rt(t), ub + k)),
                  pl.BlockSpec((None, tt, 128), lambda k, b, t: (b, rt(t), k)),
                  pl.BlockSpec((None, None, tt, 2 * hw), lambda k, b, t: (b, k, rt(t), 0)),
                  pl.BlockSpec((None, None, 8, 2 * hw), lambda k, b, t: (b, k, jnp.maximum(rt(t) * tb - 1, 0), 0)),
                  pl.BlockSpec((None, 128, 2 * hw), lambda k, b, t: (k, 0, 0)),
                  pl.BlockSpec((None, hw, 128), lambda k, b, t: (k, 0, 0)),
                  pl.BlockSpec((None, hw, 128), lambda k, b, t: (k, 0, 0)),
                  pl.BlockSpec((None, 8, 2 * hw), lambda k, b, t: (k, 0, 0)),
                  pl.BlockSpec((1, 128), lambda k, b, t: (0, k))],
        out_specs=[pl.BlockSpec((None, tt, 128), lambda k, b, t: (b, rt(t), ub + k)),
                   pl.BlockSpec((None, 128, 2 * hw), lambda k, b, t: (k, 0, 0)),
                   pl.BlockSpec((None, hw, 128), lambda k, b, t: (k, 0, 0)),
                   pl.BlockSpec((None, hw, 128), lambda k, b, t: (k, 0, 0)),
                   pl.BlockSpec((None, 1, 2 * hw), lambda k, b, t: (k, 0, 0)),
                   pl.BlockSpec((1, 128), lambda k, b, t: (0, k))],
        out_shape=[sd(dp3.shape, F32), sd((S5_KCH, 128, 2 * hw), F32), sd((S5_KCH, hw, 128), F32),
                   sd((S5_KCH, hw, 128), F32), sd((S5_KCH, 1, 2 * hw), F32), sd((1, S5_KCH * 128), F32)],
        scratch_shapes=[pltpu.VMEM((tt, 2 * hw), F32), pltpu.VMEM((tt + 8, 2 * hw), F32), pltpu.VMEM((8, 2 * hw), F32)],
        input_output_aliases={0: 0},
        compiler_params=_cp(("arbitrary", "arbitrary", "arbitrary"), 48),
    )(dp3, p3, dy3, xs, xs, bk, cre, cim, apow_rev, dskip)
    return res


_G0 = math.sqrt(2.0 / math.pi)
_G1 = 0.044715


def _gelu(y):
    return 0.5 * y * (1.0 + jnp.tanh(_G0 * (y + _G1 * y * y * y)))


def _gelu_grad(y):
    th = jnp.tanh(_G0 * (y + _G1 * y * y * y))
    return 0.5 * (1.0 + th) + 0.5 * y * (1.0 - th * th) * _G0 * (1.0 + 3.0 * _G1 * y * y)


def _glu_fwd(y_s5, wglu_g, lp):
    r, w = y_s5.shape
    tm = _row_tile(lp, 416)
    cw = wglu_g.shape[2]

    def body(y_ref, w_ref, gy_ref, z_ref, o_ref):
        gy = _bf(_gelu(y_ref[...]))
        gy_ref[...] = gy
        zs = [_dot(gy, w_ref[s]) for s in range(4)]
        for s in range(4):
            z_ref[:, s * cw:(s + 1) * cw] = zs[s]
        o_ref[:, :cw] = zs[0] * _sig(zs[2])
        o_ref[:, cw:] = zs[1] * _sig(zs[3])

    sd = jax.ShapeDtypeStruct
    return _pcall(
        body, name="glu_fwd", grid=(r // tm,),
        in_specs=[pl.BlockSpec((tm, w), lambda i: (i, 0)), _resident(wglu_g.shape)],
        out_specs=[pl.BlockSpec((tm, w), lambda i: (i, 0)), pl.BlockSpec((tm, 4 * cw), lambda i: (i, 0)),
                   pl.BlockSpec((tm, 2 * cw), lambda i: (i, 0))],
        out_shape=[sd((r, w), BF16), sd((r, 4 * cw), F32), sd((r, 2 * cw), F32)],
        compiler_params=_cp(("parallel",), 40),
    )(y_s5, wglu_g)


def _glu_bwd(dyg, z, y_s5, wglu_g, lp):
    r, w = y_s5.shape
    tm = _row_tile(lp, 416)
    cw = wglu_g.shape[2]

    def body(d_ref, z_ref, y_ref, w_ref, dz_ref, dy_ref):
        d = d_ref[...]
        zz = z_ref[...]
        acc = jnp.zeros((tm, w), F32)
        for s in range(2):
            z1 = zz[:, s * cw:(s + 1) * cw]
            sg = _sig(zz[:, (2 + s) * cw:(3 + s) * cw])
            dd = d[:, s * cw:(s + 1) * cw]
            dz1 = _bf(dd * sg)
            dz2 = _bf(dd * z1 * sg * (1.0 - sg))
            dz_ref[:, s * cw:(s + 1) * cw] = dz1
            dz_ref[:, (2 + s) * cw:(3 + s) * cw] = dz2
            acc += _dot(dz1, w_ref[s], NT) + _dot(dz2, w_ref[2 + s], NT)
        dy_ref[...] = acc * _gelu_grad(y_ref[...])

    sd = jax.ShapeDtypeStruct
    return _pcall(
        body, name="glu_bwd", grid=(r // tm,),
        in_specs=[pl.BlockSpec((tm, 2 * cw), lambda i: (i, 0)), pl.BlockSpec((tm, 4 * cw), lambda i: (i, 0)),
                  pl.BlockSpec((tm, w), lambda i: (i, 0)), _resident(wglu_g.shape)],
        out_specs=[pl.BlockSpec((tm, 4 * cw), lambda i: (i, 0)), pl.BlockSpec((tm, w), lambda i: (i, 0))],
        out_shape=[sd((r, 4 * cw), BF16), sd((r, w), F32)],
        compiler_params=_cp(("parallel",), 40),
    )(dyg, z, y_s5, wglu_g)


def _conv_fwd(p3, cw, cb):
    bsz, lp, _ = p3.shape
    tt = _row_tile(lp, 416)
    nt = lp // tt
    tb = tt // 8
    c = cw.shape[1]
    qb = Q_OFF // c

    def body(x_ref, halo_ref, w_ref, b_ref, pre_ref, act_ref, ext_ref):
        t = pl.program_id(1)
        ext_ref[0:8, :] = jnp.where(t == 0, 0.0, halo_ref[...])
        ext_ref[8:, :] = x_ref[...]
        w = w_ref[...]
        acc = b_ref[...] + w[0:1, :] * ext_ref[pl.ds(5, tt), :]
        for j in range(1, CONV_W):
            acc = acc + w[j:j + 1, :] * ext_ref[pl.ds(5 + j, tt), :]
        pre_ref[...] = acc
        act_ref[...] = acc * _sig(acc)

    sd = jax.ShapeDtypeStruct
    return _pcall(
        body, name="conv_fwd", grid=(bsz, nt),
        in_specs=[pl.BlockSpec((None, tt, c), lambda b, t: (b, t, qb)),
                  pl.BlockSpec((None, 8, c), lambda b, t: (b, jnp.maximum(t * tb - 1, 0), qb)),
                  _const((CONV_W, c)), _const((1, c))],
        out_specs=[pl.BlockSpec((None, tt, c), lambda b, t: (b, t, 0))] * 2,
        out_shape=[sd((bsz, lp, c), F32)] * 2,
        scratch_shapes=[pltpu.VMEM((tt + 8, c), F32)],
        compiler_params=_cp(("parallel", "parallel")),
    )(p3, p3, cw, cb)


def _conv_bwd(dp3, p3, dact3, pre3, cw):
    bsz, lp, _ = p3.shape
    tt = _row_tile(lp, 416)
    nt = lp // tt
    tb = tt // 8
    c = cw.shape[1]
    qb = Q_OFF // c

    def silu_grad(x):
        s = _sig(x)
        return s * (1.0 + x * (1.0 - s))

    def body(dp_any, x_ref, xh_ref, d_ref, dh_ref, pre_ref, preh_ref, w_ref, o_ref, dw_ref, db_ref, ext_ref, dext_ref):
        b = pl.program_id(0)
        t = pl.program_id(1)

        @pl.when((b == 0) & (t == 0))
        def _():
            dw_ref[...] = jnp.zeros_like(dw_ref)
            db_ref[...] = jnp.zeros_like(db_ref)

        dc = d_ref[...] * silu_grad(pre_ref[...])
        dch = jnp.where(t == nt - 1, 0.0, dh_ref[...] * silu_grad(preh_ref[...]))
        dext_ref[0:tt, :] = dc
        dext_ref[tt:, :] = dch
        ext_ref[0:8, :] = jnp.where(t == 0, 0.0, xh_ref[...])
        ext_ref[8:, :] = x_ref[...]
        w = w_ref[...]
        acc = w[CONV_W - 1:CONV_W, :] * dc
        for j in range(CONV_W - 1):
            acc = acc + w[j:j + 1, :] * dext_ref[pl.ds(CONV_W - 1 - j, tt), :]
        trow = t * tt + lax.broadcasted_iota(jnp.int32, (tt, 1), 0)
        o_ref[...] = jnp.where(trow >= PAD, acc, 0.0)
        db_ref[...] += _colsum(dc)
        for j in range(CONV_W):
            dw_ref[j:j + 1, :] += _colsum(dc * ext_ref[pl.ds(5 + j, tt), :])

    sd = jax.ShapeDtypeStruct
    nxt = lambda t: jnp.minimum((t + 1) * tb, lp // 8 - 1)
    return _pcall(
        body, name="conv_bwd", grid=(bsz, nt),
        in_specs=[pl.BlockSpec(memory_space=pl.ANY),
                  pl.BlockSpec((None, tt, c), lambda b, t: (b, t, qb)),
                  pl.BlockSpec((None, 8, c), lambda b, t: (b, jnp.maximum(t * tb - 1, 0), qb)),
                  pl.BlockSpec((None, tt, c), lambda b, t: (b, t, 0)),
                  pl.BlockSpec((None, 8, c), lambda b, t: (b, nxt(t), 0)),
                  pl.BlockSpec((None, tt, c), lambda b, t: (b, t, 0)),
                  pl.BlockSpec((None, 8, c), lambda b, t: (b, nxt(t), 0)),
                  _const((CONV_W, c))],
        out_specs=[pl.BlockSpec((None, tt, c), lambda b, t: (b, t, qb)), _const((CONV_W, c)), _const((1, c))],
        out_shape=[sd(dp3.shape, F32), sd((CONV_W, c), F32), sd((1, c), F32)],
        scratch_shapes=[pltpu.VMEM((tt + 8, c), F32), pltpu.VMEM((tt + 8, c), F32)],
        input_output_aliases={0: 0},
        compiler_params=_cp(("arbitrary", "arbitrary")),
    )(dp3, p3, p3, dact3, dact3, pre3, pre3, cw)


def _mlstm_gates(g, h_idx, c_idx, lc):
    lane = lax.broadcasted_iota(jnp.int32, g.shape, 1)
    i_col = jnp.sum(jnp.where(lane == h_idx, g, 0.0), axis=1, keepdims=True)
    f_col = jnp.sum(jnp.where(lane == M_HEADS + h_idx, g, 0.0), axis=1, keepdims=True)
    row = lax.broadcasted_iota(jnp.int32, (lc, 1), 0)
    valid = (c_idx * lc + row) >= PAD
    li = jnp.where(valid, i_col, NEG)
    lf = jnp.where(valid, jnp.minimum(f_col, 0.0) - jnp.log(1.0 + jnp.exp(-jnp.abs(f_col))), 0.0)
    r2 = lax.broadcasted_iota(jnp.int32, (lc, lc), 0)
    c2 = lax.broadcasted_iota(jnp.int32, (lc, lc), 1)
    eye = r2 == c2
    tril = r2 >= c2
    to_row = lambda col: jnp.sum(jnp.where(eye, col, 0.0), axis=0, keepdims=True)
    lf_row = to_row(lf)
    b_col = jnp.sum(jnp.where(tril, lf_row, 0.0), axis=1, keepdims=True)
    b_row = to_row(b_col)
    li_row = to_row(li)
    d_mat = jnp.where(tril, b_col - b_row + li_row, NEG)
    return dict(f_col=f_col, valid=valid, li=li, b_col=b_col, d_mat=d_mat, eye=eye, r2=r2, c2=c2, row=row,
                to_row=to_row)


def _mlstm_chunk(q, ks, v, gq, c_st, n_st, m_st, lc):
    b_col, d_mat = gq["b_col"], gq["d_mat"]
    m_inter = b_col + m_st
    m_row = jnp.maximum(m_inter, jnp.max(d_mat, axis=1, keepdims=True))
    w_intra = jnp.exp(d_mat - m_row)
    w_inter = jnp.exp(m_inter - m_row)
    qb, kb, vb, cb = _bf(q), _bf(ks), _bf(v), _bf(c_st)
    s = _dot(qb, kb, NT) * w_intra
    qc = _dot(qb, cb)
    num = _dot(_bf(s), vb) + w_inter * qc
    qn = jnp.sum(q * n_st, axis=1, keepdims=True)
    den = jnp.sum(s, axis=1, keepdims=True) + w_inter * qn
    e = jnp.exp(-m_row)
    nn = jnp.maximum(jnp.abs(den), e)
    b_last = b_col[lc - 1:lc, :]
    g_log = b_last - b_col + gq["li"]
    m_new = jnp.maximum(b_last + m_st, jnp.max(g_log, axis=0, keepdims=True))
    w_k = jnp.exp(g_log - m_new)
    decay = jnp.exp(b_last + m_st - m_new)
    return dict(w_intra=w_intra, w_inter=w_inter, qb=qb, kb=kb, vb=vb, cb=cb, s=s, qc=qc, num=num, qn=qn, den=den,
                e=e, nn=nn, m_new=m_new, w_k=w_k, decay=decay)


def _chunks_per_step(nc):
    return max(c for c in (5, 4, 2, 1) if nc % c == 0)


def _mlstm_fwd(qk3, p3):
    bsz, lp, _ = p3.shape
    lc = M_CHUNK
    nc = lp // lc
    dk, dv = 128, 256
    scale = dk ** -0.5

    cps = _chunks_per_step(nc)
    rows = cps * lc

    def body(q_ref, k_ref, v_ref, g_ref, h_ref, cs_ref, ns_ref, ms_ref, c_sc, n_sc, m_sc):
        st = pl.program_id(1)

        @pl.when(st == 0)
        def _():
            c_sc[...] = jnp.zeros_like(c_sc)
            n_sc[...] = jnp.zeros_like(n_sc)
            m_sc[...] = jnp.zeros_like(m_sc)

        for j in range(cps):
            rs = slice(j * lc, (j + 1) * lc)
            g = g_ref[rs, :]
            for hh in range(M_HEADS):
                c_st, n_st, m_all = c_sc[hh], n_sc[hh], m_sc[hh]
                cs_ref[hh, j] = c_st
                ns_ref[hh, j] = n_st
                ms_ref[hh, j] = m_all
                m_st = m_all[:, 0:1]
                q = q_ref[rs, hh * dk:(hh + 1) * dk]
                ks = k_ref[rs, hh * dk:(hh + 1) * dk] * scale
                v = v_ref[rs, hh * dv:(hh + 1) * dv]
                gq = _mlstm_gates(g, hh, st * cps + j, lc)
                f = _mlstm_chunk(q, ks, v, gq, c_st, n_st, m_st, lc)
                h_ref[rs, hh * dv:(hh + 1) * dv] = f["num"] / f["nn"]
                kw = ks * f["w_k"]
                c_sc[hh] = f["decay"] * c_st + _dot(_bf(kw), f["vb"], TN)
                n_sc[hh] = f["decay"] * n_st + _colsum(kw)
                m_sc[hh] = jnp.broadcast_to(f["m_new"], (1, 128))

    sd = jax.ShapeDtypeStruct
    nh = M_HEADS
    return _pcall(
        body, name="mlstm_fwd", grid=(bsz, nc // cps),
        in_specs=[pl.BlockSpec((None, rows, nh * dk), lambda b, c: (b, c, 0)),
                  pl.BlockSpec((None, rows, nh * dk), lambda b, c: (b, c, 1)),
                  pl.BlockSpec((None, rows, nh * dv), lambda b, c: (b, c, V_OFF // (nh * dv))),
                  pl.BlockSpec((None, rows, 128), lambda b, c: (b, c, G_OFF // 128))],
        out_specs=[pl.BlockSpec((None, rows, nh * dv), lambda b, c: (b, c, 0)),
                   pl.BlockSpec((None, nh, cps, dk, dv), lambda b, c: (b, 0, c, 0, 0)),
                   pl.BlockSpec((None, nh, cps, 1, dk), lambda b, c: (b, 0, c, 0, 0)),
                   pl.BlockSpec((None, nh, cps, 1, 128), lambda b, c: (b, 0, c, 0, 0))],
        out_shape=[sd((bsz, lp, nh * dv), F32), sd((bsz, nh, nc, dk, dv), F32),
                   sd((bsz, nh, nc, 1, dk), F32), sd((bsz, nh, nc, 1, 128), F32)],
        scratch_shapes=[pltpu.VMEM((nh, dk, dv), F32), pltpu.VMEM((nh, 1, dk), F32), pltpu.VMEM((nh, 1, 128), F32)],
        compiler_params=_cp(("parallel", "arbitrary")),
    )(qk3, qk3, p3, p3)


def _mlstm_bwd(dp3, qk3, p3, dh3, cs, ns, ms):
    bsz, lp, _ = p3.shape
    lc = M_CHUNK
    nc = lp // lc
    dk, dv = 128, 256
    scale = dk ** -0.5

    cps = _chunks_per_step(nc)
    nst = nc // cps
    rows = cps * lc

    def body(dp_any, q_ref, k_ref, v_ref, g_ref, dh_ref, cs_ref, ns_ref, ms_ref,
             dv_ref, dqk_ref, dg_ref, dc_sc, dn_sc):
        t = pl.program_id(1)
        st = nst - 1 - t

        @pl.when(t == 0)
        def _():
            dc_sc[...] = jnp.zeros_like(dc_sc)
            dn_sc[...] = jnp.zeros_like(dn_sc)

        lane = lax.broadcasted_iota(jnp.int32, (lc, 128), 1)
        for j in reversed(range(cps)):
            rs = slice(j * lc, (j + 1) * lc)
            g = g_ref[rs, :]
            dgate = jnp.zeros((lc, 128), F32)
            for hh in range(M_HEADS):
                dgate = head(hh, j, rs, st * cps + j, g, lane, dgate, q_ref, k_ref, v_ref, dh_ref, cs_ref, ns_ref,
                             ms_ref, dv_ref, dqk_ref, dc_sc, dn_sc)
            dg_ref[rs, :] = dgate

    def head(hh, j, sl, c, g, lane, dgate, q_ref, k_ref, v_ref, dh_ref, cs_ref, ns_ref, ms_ref, dv_ref, dqk_ref,
             dc_sc, dn_sc):
        c_st, n_st = cs_ref[hh, j], ns_ref[hh, j]
        m_st = ms_ref[hh, j][:, 0:1]
        q = q_ref[sl, hh * dk:(hh + 1) * dk]
        ks = k_ref[sl, hh * dk:(hh + 1) * dk] * scale
        v = v_ref[sl, hh * dv:(hh + 1) * dv]
        dh = dh_ref[sl, hh * dv:(hh + 1) * dv]
        gq = _mlstm_gates(g, hh, c, lc)
        f = _mlstm_chunk(q, ks, v, gq, c_st, n_st, m_st, lc)
        eye, r2, c2, row, valid = gq["eye"], gq["r2"], gq["c2"], gq["row"], gq["valid"]
        w_intra, w_inter, s, nn, den = f["w_intra"], f["w_inter"], f["s"], f["nn"], f["den"]
        qb, kb, vb, cb, w_k, decay = f["qb"], f["kb"], f["vb"], f["cb"], f["w_k"], f["decay"]
        d_c, d_n = dc_sc[hh], dn_sc[hh]
        d_cb = _bf(d_c)

        hout = f["num"] / nn
        dnum = dh / nn
        d_nn = -jnp.sum(dh * hout, axis=1, keepdims=True) / nn
        dden = jnp.where(jnp.abs(den) > f["e"], d_nn * jnp.sign(den), 0.0)
        wdnum = w_inter * dnum
        wdden = w_inter * dden
        ds = _dot(_bf(dnum), vb, NT) + dden
        dsw = _bf(ds * w_intra)
        dq = _dot(dsw, kb) + _dot(_bf(wdnum), cb, NT) + wdden * n_st
        dkw = _dot(vb, d_cb, NT) + d_n
        dks = _dot(dsw, qb, TN) + dkw * w_k
        kw = ks * w_k
        dvv = _dot(_bf(s), _bf(dnum), TN) + _dot(_bf(kw), d_cb)
        dd = ds * s
        rs = jnp.sum(dd, axis=1, keepdims=True)
        cs_col = jnp.sum(jnp.where(eye, jnp.sum(dd, axis=0, keepdims=True), 0.0), axis=1, keepdims=True)
        dwi = jnp.sum(dnum * f["qc"], axis=1, keepdims=True) + dden * f["qn"]
        db = rs - cs_col + dwi * w_inter
        dli = cs_col
        ddecay = jnp.sum(jnp.sum(d_c * c_st, axis=1, keepdims=True), axis=0, keepdims=True) \
            + jnp.sum(d_n * n_st, axis=1, keepdims=True)
        dgl = jnp.sum(dkw * ks, axis=1, keepdims=True) * w_k
        dblast = ddecay * decay + jnp.sum(dgl, axis=0, keepdims=True)
        db = db - dgl + jnp.where(row == lc - 1, dblast, 0.0)
        dli = dli + dgl
        db_row = gq["to_row"](db)
        dlf = jnp.sum(jnp.where(c2 >= r2, db_row, 0.0), axis=1, keepdims=True)
        dlf = jnp.where(valid, dlf, 0.0)
        dgate = jnp.where(lane == hh, jnp.where(valid, dli, 0.0), dgate)
        dgate = jnp.where(lane == M_HEADS + hh, dlf / (1.0 + jnp.exp(gq["f_col"])), dgate)
        dqk_ref[sl, hh * dk:(hh + 1) * dk] = dq
        dqk_ref[sl, (M_HEADS + hh) * dk:(M_HEADS + hh + 1) * dk] = dks * scale
        dv_ref[sl, hh * dv:(hh + 1) * dv] = dvv
        dc_sc[hh] = decay * d_c + _dot(qb, _bf(wdnum), TN)
        dn_sc[hh] = decay * d_n + _colsum(q * wdden)
        return dgate

    sd = jax.ShapeDtypeStruct
    nh = M_HEADS
    rc = lambda c: nst - 1 - c
    return _pcall(
        body, name="mlstm_bwd", grid=(bsz, nst),
        in_specs=[pl.BlockSpec(memory_space=pl.ANY),
                  pl.BlockSpec((None, rows, nh * dk), lambda b, c: (b, rc(c), 0)),
                  pl.BlockSpec((None, rows, nh * dk), lambda b, c: (b, rc(c), 1)),
                  pl.BlockSpec((None, rows, nh * dv), lambda b, c: (b, rc(c), V_OFF // (nh * dv))),
                  pl.BlockSpec((None, rows, 128), lambda b, c: (b, rc(c), G_OFF // 128)),
                  pl.BlockSpec((None, rows, nh * dv), lambda b, c: (b, rc(c), 0)),
                  pl.BlockSpec((None, nh, cps, dk, dv), lambda b, c: (b, 0, rc(c), 0, 0)),
                  pl.BlockSpec((None, nh, cps, 1, dk), lambda b, c: (b, 0, rc(c), 0, 0)),
                  pl.BlockSpec((None, nh, cps, 1, 128), lambda b, c: (b, 0, rc(c), 0, 0))],
        out_specs=[pl.BlockSpec((None, rows, nh * dv), lambda b, c: (b, rc(c), V_OFF // (nh * dv))),
                   pl.BlockSpec((None, rows, 2 * nh * dk), lambda b, c: (b, rc(c), 0)),
                   pl.BlockSpec((None, rows, 128), lambda b, c: (b, rc(c), 0))],
        out_shape=[sd(dp3.shape, F32), sd((bsz, lp, 2 * nh * dk), F32), sd((bsz, lp, 128), F32)],
        scratch_shapes=[pltpu.VMEM((nh, dk, dv), F32), pltpu.VMEM((nh, 1, dk), F32)],
        input_output_aliases={0: 0},
        compiler_params=_cp(("arbitrary", "arbitrary")),
    )(dp3, qk3, qk3, p3, p3, dh3, cs, ns, ms)


def _headnorm(x):
    dv = x.shape[1] // M_HEADS
    xh, rs = [], []
    for h in range(M_HEADS):
        xx = x[:, h * dv:(h + 1) * dv]
        mu = jnp.mean(xx, axis=-1, keepdims=True)
        xc = xx - mu
        rstd = lax.rsqrt(jnp.mean(xc * xc, axis=-1, keepdims=True) + LN_EPS)
        xh.append(xc * rstd)
        rs.append(rstd)
    return jnp.concatenate(xh, axis=1), rs


def _mix_fwd(hm, p, ys5g, h0, gn, wmo_bf, wo_bf, g1, b1, lp):
    r, d = hm.shape
    tm = _row_tile(lp, 208)

    def body(hm_ref, o_ref, gs_ref, gm_ref, ys_ref, h0_ref, gn_ref, wmo_ref, wo_ref, g1_ref, b1_ref,
             ymin_ref, ym_ref, mix_ref, r1_ref, h1_ref):
        xhat, _ = _headnorm(hm_ref[...])
        ymin = _bf(_sig(o_ref[...]) * (xhat * gn_ref[...]))
        ymin_ref[...] = ymin
        ym = _dot(ymin, wmo_ref[...])
        ym_ref[...] = ym
        mix = _bf(_sig(gs_ref[...]) * ys_ref[...] + _sig(gm_ref[...]) * ym)
        mix_ref[...] = mix
        r1 = ALPHA * h0_ref[...] + _dot(mix, wo_ref[...])
        r1_ref[...] = r1
        h1, _, _ = _ln_fwd(r1, g1_ref[...], b1_ref[...])
        h1_ref[...] = h1

    sd = jax.ShapeDtypeStruct
    row = pl.BlockSpec((tm, d), lambda i: (i, 0))
    return _pcall(
        body, name="mix_fwd", grid=(r // tm,),
        in_specs=[row, pl.BlockSpec((tm, d), lambda i: (i, O_OFF // d)), pl.BlockSpec((tm, d), lambda i: (i, GS_OFF // d)),
                  pl.BlockSpec((tm, d), lambda i: (i, GM_OFF // d)), row, row, _const((1, d)),
                  _resident((d, d)), _resident((d, d)), _const((1, d)), _const((1, d))],
        out_specs=[row] * 5,
        out_shape=[sd((r, d), BF16), sd((r, d), F32), sd((r, d), BF16), sd((r, d), F32), sd((r, d), F32)],
        compiler_params=_cp(("parallel",), 48),
    )(hm, p, p, p, ys5g, h0, gn, wmo_bf, wo_bf, g1, b1)


def _mix_bwd(dh1, r1, g1, wo_bf, wmo_bf, p, ys5g, ym, hm, gn, lp):
    r, d = hm.shape
    tm = _row_tile(lp, 208)
    dv = d // M_HEADS

    def body(dh1_ref, r1_ref, g1_ref, wo_ref, wmo_ref, o_ref, gs_ref, gm_ref, ys_ref, ym_ref, hm_ref, gn_ref,
             dr1_ref, dp_ref, dys_ref, dym_ref, dhm_ref, dg1_ref, db1_ref, dgn_ref):
        i = pl.program_id(0)

        @pl.when(i == 0)
        def _():
            dg1_ref[...] = jnp.zeros_like(dg1_ref)
            db1_ref[...] = jnp.zeros_like(db1_ref)
            dgn_ref[...] = jnp.zeros_like(dgn_ref)

        dh1 = dh1_ref[...]
        _, xhat1, rstd1 = _ln_fwd(r1_ref[...], g1_ref[...], 0.0)
        dr1 = _ln_bwd(dh1, xhat1, rstd1, g1_ref[...])
        dr1_ref[...] = dr1
        dg1_ref[...] += _colsum(dh1 * xhat1)
        db1_ref[...] += _colsum(dh1)
        dmix = _dot(_bf(dr1), wo_ref[...], NT)
        sgs, sgm, so = _sig(gs_ref[...]), _sig(gm_ref[...]), _sig(o_ref[...])
        dys_ref[...] = dmix * sgs
        dp_ref[:, d:2 * d] = dmix * ys_ref[...] * sgs * (1.0 - sgs)
        dym = dmix * sgm
        dym_ref[...] = _bf(dym)
        dp_ref[:, 2 * d:3 * d] = dmix * ym_ref[...] * sgm * (1.0 - sgm)
        dymin = _dot(_bf(dym), wmo_ref[...], NT)
        xhat, rs = _headnorm(hm_ref[...])
        gn_ = gn_ref[...]
        dp_ref[:, 0:d] = dymin * (xhat * gn_) * so * (1.0 - so)
        dhn = dymin * so
        dgn_ref[...] += _colsum(dhn * xhat)
        dxh = dhn * gn_
        for h in range(M_HEADS):
            sl = slice(h * dv, (h + 1) * dv)
            a, xh = dxh[:, sl], xhat[:, sl]
            m1 = jnp.mean(a, axis=-1, keepdims=True)
            m2 = jnp.mean(a * xh, axis=-1, keepdims=True)
            dhm_ref[:, sl] = rs[h] * (a - m1 - xh * m2)

    sd = jax.ShapeDtypeStruct
    row = pl.BlockSpec((tm, d), lambda i: (i, 0))
    vec = _const((1, d))
    return _pcall(
        body, name="mix_bwd", grid=(r // tm,),
        in_specs=[row, row, vec, _resident((d, d)), _resident((d, d)),
                  pl.BlockSpec((tm, d), lambda i: (i, O_OFF // d)), pl.BlockSpec((tm, d), lambda i: (i, GS_OFF // d)),
                  pl.BlockSpec((tm, d), lambda i: (i, GM_OFF // d)), row, row, row, vec],
        out_specs=[row, pl.BlockSpec((tm, 3 * d), lambda i: (i, 0)), row, row, row, vec, vec, vec],
        out_shape=[sd((r, d), F32), sd((r, NP), F32), sd((r, d), F32), sd((r, d), BF16), sd((r, d), F32),
                   sd((1, d), F32), sd((1, d), F32), sd((1, d), F32)],
        compiler_params=_cp(("arbitrary",), 48),
    )(dh1, r1, g1, wo_bf, wmo_bf, p, p, p, ys5g, ym, hm, gn)


def _mlp_fwd(h1, tgt, wup_g, wdn_bf, bup, g2, b2, lp):
    r, d = h1.shape
    tm = _row_tile(lp, 320)
    tps = lp // tm
    nf = wup_g.shape[0]

    def body(h1_ref, t_ref, wup_ref, wdn_ref, bup_ref, g2_ref, b2_ref, dr2_ref, act_ref, loss_ref, dg2_ref, db2_ref):
        i = pl.program_id(0)

        @pl.when(i == 0)
        def _():
            loss_ref[...] = jnp.zeros_like(loss_ref)
            dg2_ref[...] = jnp.zeros_like(dg2_ref)
            db2_ref[...] = jnp.zeros_like(db2_ref)

        h1 = h1_ref[...]
        h1b = _bf(h1)
        ff = jnp.zeros((tm, d), F32)
        for s in range(nf):
            up = _dot(h1b, wup_ref[s]) + bup_ref[:, s * d:(s + 1) * d]
            a = jnp.maximum(up, 0.0)
            a = _bf(a * a)
            act_ref[:, s * d:(s + 1) * d] = a
            ff = ff + _dot(a, wdn_ref[s * d:(s + 1) * d, :])
        r2 = ALPHA * h1 + ff
        g2 = g2_ref[...]
        y, xhat, rstd = _ln_fwd(r2, g2, b2_ref[...])
        t = (i % tps) * tm + lax.broadcasted_iota(jnp.int32, (tm, 1), 0)
        diff = jnp.where(t >= PAD + N_META, y - t_ref[...], 0.0)
        loss_ref[...] += 0.5 / d * jnp.sum(jnp.sum(diff * diff, axis=1, keepdims=True), axis=0, keepdims=True)
        dy = diff * (1.0 / d)
        dg2_ref[...] += _colsum(dy * xhat)
        db2_ref[...] += _colsum(dy)
        dr2_ref[...] = _ln_bwd(dy, xhat, rstd, g2)

    sd = jax.ShapeDtypeStruct
    row = pl.BlockSpec((tm, d), lambda i: (i, 0))
    vec = _const((1, d))
    return _pcall(
        body, name="mlp_fwd", grid=(r // tm,),
        in_specs=[row, row, _resident(wup_g.shape), _resident(wdn_bf.shape), _const((1, nf * d)), vec, vec],
        out_specs=[row, pl.BlockSpec((tm, nf * d), lambda i: (i, 0)), _const((1, 128)), vec, vec],
        out_shape=[sd((r, d), F32), sd((r, nf * d), BF16), sd((1, 128), F32), sd((1, d), F32), sd((1, d), F32)],
        compiler_params=_cp(("arbitrary",), 56),
    )(h1, tgt, wup_g, wdn_bf, bup, g2, b2)


def _mlp_bwd(h1, dr2, wup_g, wdn_bf, bup, lp):
    r, d = h1.shape
    tm = _row_tile(lp, 320)
    nf = wup_g.shape[0]

    def body(h1_ref, dr2_ref, wup_ref, wdn_ref, bup_ref, dh1_ref, dup_ref, dbup_ref):
        i = pl.program_id(0)

        @pl.when(i == 0)
        def _():
            dbup_ref[...] = jnp.zeros_like(dbup_ref)

        h1b = _bf(h1_ref[...])
        dr2 = dr2_ref[...]
        dr2b = _bf(dr2)
        acc = ALPHA * dr2
        for s in range(nf):
            up = _dot(h1b, wup_ref[s]) + bup_ref[:, s * d:(s + 1) * d]
            dact = _dot(dr2b, wdn_ref[s * d:(s + 1) * d, :], NT)
            dup = dact * (2.0 * jnp.maximum(up, 0.0))
            dbup_ref[:, s * d:(s + 1) * d] += _colsum(dup)
            dupb = _bf(dup)
            dup_ref[:, s * d:(s + 1) * d] = dupb
            acc = acc + _dot(dupb, wup_ref[s], NT)
        dh1_ref[...] = acc

    sd = jax.ShapeDtypeStruct
    row = pl.BlockSpec((tm, d), lambda i: (i, 0))
    return _pcall(
        body, name="mlp_bwd", grid=(r // tm,),
        in_specs=[row, row, _resident(wup_g.shape), _resident(wdn_bf.shape), _const((1, nf * d))],
        out_specs=[row, pl.BlockSpec((tm, nf * d), lambda i: (i, 0)), _const((1, nf * d))],
        out_shape=[sd((r, d), F32), sd((r, nf * d), BF16), sd((1, nf * d), F32)],
        compiler_params=_cp(("arbitrary",), 56),
    )(h1, dr2, wup_g, wdn_bf, bup)


def _s5_block_mats(bb_re_t, bb_im_t, c_re, c_im, ap_re, ap_im):
    ng = c_re.shape[0]
    gl = ng // S5_KCH
    eye = jnp.eye(gl, dtype=F32)

    def bmat(bt):
        bb = jnp.transpose(bt, (1, 0, 2)).reshape(S5_KCH, gl, S5_GROUP, S5_STATE)
        return jnp.einsum("kghp,gj->kghjp", bb, eye).reshape(S5_KCH, gl * S5_GROUP, gl * S5_STATE)

    def cmat(c):
        cc = c.reshape(S5_KCH, gl, S5_GROUP, S5_STATE)
        return jnp.einsum("kghp,gj->kjpgh", cc, eye).reshape(S5_KCH, gl * S5_STATE, gl * S5_GROUP)

    def pw(a):
        return jnp.transpose(a.reshape(8, S5_KCH, gl * S5_STATE), (1, 0, 2))

    bk = jnp.concatenate([bmat(bb_re_t), bmat(bb_im_t)], axis=-1)
    apow = jnp.concatenate([pw(ap_re), pw(ap_im)], axis=-1)
    return _bf(bk), _bf(cmat(c_re)), _bf(cmat(c_im)), apow


def _s5_block_grads(dbk, dcre, dcim, da):
    gl = dbk.shape[1] // S5_GROUP
    ng = gl * S5_KCH
    eye = jnp.eye(gl, dtype=F32)
    hw = gl * S5_STATE

    def bpart(x):
        x = x.reshape(S5_KCH, gl, S5_GROUP, gl, S5_STATE)
        x = jnp.einsum("kghjp,gj->kghp", x, eye).reshape(ng, S5_GROUP, S5_STATE)
        return jnp.transpose(x, (1, 0, 2))

    def cpart(x):
        x = x.reshape(S5_KCH, gl, S5_STATE, gl, S5_GROUP)
        return jnp.einsum("kjpgh,gj->kghp", x, eye).reshape(ng, S5_GROUP, S5_STATE)

    return (bpart(dbk[..., :hw]), bpart(dbk[..., hw:]), cpart(dcre), cpart(dcim),
            da[:, 0, :hw].reshape(ng, S5_STATE), da[:, 0, hw:].reshape(ng, S5_STATE))


def _tie(a, tok):
    return a if tok is None else a + tok[0, 0]


def _local_step(x, tgt, w, early=None, late=None, ready=None):
    ready = ready or (lambda names, g: None)
    bsz, seq, d = x.shape
    lp = PAD + N_META + seq
    r = bsz * lp
    meta = jnp.broadcast_to(w["meta_tokens"][None], (bsz, N_META, d))
    hin = jnp.concatenate([jnp.zeros((bsz, PAD, d), F32), meta, x], axis=1).reshape(r, d)
    tgtp = jnp.concatenate([jnp.zeros((bsz, PAD + N_META, d), F32), tgt], axis=1).reshape(r, d)

    h0 = _ln0_fwd(hin, w["ln0_g"], w["ln0_b"], lp)
    if early is not None:
        w = {**w, **early(h0)}
    p = _inproj(h0, w["w_in"], w["b_in"], lp)
    p3 = p.reshape(bsz, lp, NP)

    b_re_t = jnp.transpose(w["s5_b_re"], (2, 0, 1))
    b_im_t = jnp.transpose(w["s5_b_im"], (2, 0, 1))
    ap_re, ap_im, bb_re_t, bb_im_t = _s5_prep(w["s5_lambda_re"], w["s5_lambda_im"], w["s5_log_dt"], b_re_t, b_im_t)
    bk, cre, cim, apow = _s5_block_mats(bb_re_t, bb_im_t, w["s5_c_re"], w["s5_c_im"], ap_re, ap_im)
    y_s5, xs = _s5_fwd(p3, bk, cre, cim, apow, w["s5_d"])
    sw = y_s5.shape[-1]
    if late is not None:
        w = {**w, **late(y_s5)}
    gy, z, ys5g = _glu_fwd(y_s5.reshape(r, sw), w["s5_w_glu"], lp)

    pre3, qk3 = _conv_fwd(p3, w["qk_conv_w"], w["qk_conv_b"])
    hm3, cs, ns, ms = _mlstm_fwd(qk3, p3)
    hm = hm3.reshape(r, d)
    ymin, ym, mix, r1, h1 = _mix_fwd(hm, p, ys5g, h0, w["m_norm_g"], w["m_w_out"], w["w_o"], w["ln1_g"], w["ln1_b"], lp)
    dr2, act, loss, dg2, db2 = _mlp_fwd(h1, tgtp, w["w_up"], w["w_down"], w["b_up"], w["ln2_g"], w["ln2_b"], lp)

    g = {"ln2_g": dg2, "ln2_b": db2}
    dh1, dup, g["b_up"] = _mlp_bwd(h1, dr2, w["w_up"], w["w_down"], w["b_up"], lp)
    g["w_down"] = _mm_tn(act, dr2, name="dw_down")
    g["w_up"] = _mm_tn(h1, dup, name="dw_up", split=w["w_up"].shape[0])
    tok = ready(("w_down", "w_up"), g)
    dr1, dp, dys5g, dym, dhm, g["ln1_g"], g["ln1_b"], g["m_norm_g"] = _mix_bwd(
        dh1, r1, _tie(w["ln1_g"], tok), w["w_o"], w["m_w_out"], p, ys5g, ym, hm, w["m_norm_g"], lp)
    g["w_o"] = _mm_tn(mix, dr1, name="dw_o")
    g["m_w_out"] = _mm_tn(ymin, dym, name="dw_mout")

    dp3 = dp.reshape(bsz, lp, NP)
    dp3, dqk3, dgate = _mlstm_bwd(dp3, qk3, p3, dhm.reshape(bsz, lp, d), cs, ns, ms)
    dp3, g["qk_conv_w"], g["qk_conv_b"] = _conv_bwd(dp3, p3, dqk3, pre3, w["qk_conv_w"])
    dz, dys5 = _glu_bwd(dys5g, z, y_s5.reshape(r, sw), w["s5_w_glu"], lp)
    g["s5_w_glu"] = _mm_tn(gy, dz, name="dw_glu", split=w["s5_w_glu"].shape[0])
    tok = ready(("s5_w_glu", "m_w_out", "w_o"), g)
    apow_rev = jnp.flip(apow, axis=1)
    dp3, dbk, dcre, dcim, da, g["s5_d"] = _s5_bwd(dp3, p3, dys5.reshape(bsz, lp, sw), xs, bk, cre, cim, apow_rev,
                                                 _tie(w["s5_d"], tok))
    dbb_re_t, dbb_im_t, g["s5_c_re"], g["s5_c_im"], da_re, da_im = _s5_block_grads(dbk, dcre, dcim, da)
    g["s5_lambda_re"], g["s5_lambda_im"], g["s5_log_dt"], gb_re_t, gb_im_t = _s5_prep_bwd(
        w["s5_lambda_re"], w["s5_lambda_im"], w["s5_log_dt"], b_re_t, b_im_t, da_re, da_im, dbb_re_t, dbb_im_t)
    g["s5_b_re"] = jnp.transpose(gb_re_t, (1, 2, 0))
    g["s5_b_im"] = jnp.transpose(gb_im_t, (1, 2, 0))

    dp3 = lax.dynamic_update_slice(dp3, dgate, (0, 0, G_OFF))
    dp = dp3.reshape(r, NP)
    g["w_in"], g["b_in"] = _mm_tn(h0, dp, name="dw_in", colsum=True)
    tok = ready(("w_in",), g)
    dpw = _mm_nt(dp, w["w_in"], lp, name="dh0", dep=tok)
    dhin, g["ln0_g"], g["ln0_b"], g["meta_tokens"] = _ln0_bwd(hin, dr1, dpw, w["ln0_g"], lp)
    grad_x = dhin.reshape(bsz, lp, d)[:, PAD + N_META:]
    return loss, grad_x, g


_ANY = pl.BlockSpec(memory_space=pl.ANY)
_MESH = pl.DeviceIdType.MESH


def _place():
    return lax.axis_index("x"), lax.axis_index("y"), lax.axis_index("c")


def _gather_chips(shards):
    n = len(shards)

    def body(*refs):
        ins, outs = refs[:n], refs[n:2 * n]
        send, recv, loc = refs[2 * n:]
        x, y, c = _place()
        me = 2 * x + y
        peers = [(1 - x, y), (x, 1 - y), (1 - x, 1 - y)]

        def rc(a, k, slot):
            px, py = peers[k]
            return pltpu.make_async_remote_copy(src_ref=ins[a], dst_ref=outs[a].at[slot], send_sem=send.at[a, k],
                                                recv_sem=recv.at[a, k], device_id=(px, py, c), device_id_type=_MESH)

        own = [pltpu.make_async_copy(ins[a], outs[a].at[me], loc.at[a]) for a in range(n)]
        for cp in own:
            cp.start()
        out = [rc(a, k, me) for a in range(n) for k in range(3)]
        for cp in out:
            cp.start()
        for a in range(n):
            for k in range(3):
                rc(a, k, 2 * peers[k][0] + peers[k][1]).wait_recv()
        for cp in out:
            cp.wait_send()
        for cp in own:
            cp.wait()

    return _pcall(
        body, name="gather_chips", in_specs=[_ANY] * n, out_specs=[_ANY] * n,
        out_shape=[jax.ShapeDtypeStruct((4,) + s.shape, s.dtype) for s in shards],
        scratch_shapes=[pltpu.SemaphoreType.DMA((n, 3)), pltpu.SemaphoreType.DMA((n, 3)), pltpu.SemaphoreType.DMA((n,))],
    )(*shards)


_HBM = pl.BlockSpec(memory_space=pltpu.HBM)
_SEM = pl.BlockSpec(memory_space=pltpu.SEMAPHORE)
_EFFECT = pltpu.SideEffectType.DATAFLOW_SIDE_EFFECTING


def _xchg_copies(srcs, lands, send, recv, scatter):
    x, y, c = _place()
    me = 2 * x + y
    peers = [(1 - x, y), (x, 1 - y), (1 - x, 1 - y)]
    out = []
    for a in range(len(srcs)):
        for k, (px, py) in enumerate(peers):
            src = srcs[a].at[2 * px + py] if scatter else srcs[a]
            dst = lands[a].at[k] if scatter else lands[a].at[me]
            out.append(pltpu.make_async_remote_copy(src_ref=src, dst_ref=dst, send_sem=send.at[3 * a + k],
                                                    recv_sem=recv.at[3 * a + k], device_id=(px, py, c),
                                                    device_id_type=_MESH))
    return out


def _xchg_start(srcs, lands, *, name, scatter, dep=None):
    n = len(srcs)
    deps = [] if dep is None else [dep]
    nd = len(deps)

    def body(*refs):
        send, recv = refs[2 * n + nd], refs[2 * n + nd + 1]
        for cp in _xchg_copies(refs[:n], refs[n:2 * n], send, recv, scatter):
            cp.start()
        refs[-1][...] = jnp.zeros_like(refs[-1])

    hbm = lambda a: pltpu.HBM(a.shape, a.dtype)
    con = lambda a: pltpu.with_memory_space_constraint(a, pltpu.HBM)
    res = _pcall(
        body, name=name, in_specs=[_HBM] * (2 * n) + [_ANY] * nd,
        out_specs=[_SEM, _SEM] + [_HBM] * (2 * n) + [pl.BlockSpec(memory_space=pltpu.VMEM)],
        out_shape=[pltpu.SemaphoreType.DMA((3 * n,)), pltpu.SemaphoreType.DMA((3 * n,))]
        + [hbm(a) for a in srcs] + [hbm(a) for a in lands] + [jax.ShapeDtypeStruct((8, 128), F32)],
        input_output_aliases={i: 2 + i for i in range(2 * n)},
        compiler_params=pltpu.CompilerParams(has_side_effects=_EFFECT),
    )(*[con(a) for a in srcs], *[con(a) for a in lands], *deps)
    return res[0], res[1], list(res[2:2 + n]), list(res[2 + n:2 + 2 * n]), res[-1]


def _xchg_wait(send, recv, srcs, lands, after, *, name, scatter):
    n = len(srcs)

    def body(*refs):
        s_ref, r_ref = refs[2 * n], refs[2 * n + 1]
        for cp in _xchg_copies(refs[:n], refs[n:2 * n], s_ref, r_ref, scatter):
            cp.wait_send()
            cp.wait_recv()

    hbm = lambda a: pltpu.HBM(a.shape, a.dtype)
    res = _pcall(
        body, name=name, in_specs=[_HBM] * (2 * n) + [_SEM, _SEM, _ANY],
        out_specs=[_HBM] * (2 * n),
        out_shape=[hbm(a) for a in srcs] + [hbm(a) for a in lands],
        input_output_aliases={i: i for i in range(2 * n)},
        compiler_params=pltpu.CompilerParams(has_side_effects=_EFFECT),
    )(*srcs, *lands, send, recv, after)
    return list(res[:n]), list(res[n:])


def _swap_cores(arrs, name="swap_cores"):
    n = len(arrs)

    def body(*refs):
        ins, outs = refs[:n], refs[n:2 * n]
        send, recv = refs[2 * n:]
        x, y, c = _place()
        cps = [pltpu.make_async_remote_copy(src_ref=ins[a], dst_ref=outs[a], send_sem=send.at[a], recv_sem=recv.at[a],
                                            device_id=(x, y, 1 - c), device_id_type=_MESH) for a in range(n)]
        for cp in cps:
            cp.start()
        for cp in cps:
            cp.wait_recv()
        for cp in cps:
            cp.wait_send()

    return _pcall(
        body, name=name, in_specs=[_ANY] * n, out_specs=[_ANY] * n,
        out_shape=[jax.ShapeDtypeStruct(s.shape, s.dtype) for s in arrs],
        scratch_shapes=[pltpu.SemaphoreType.DMA((n,)), pltpu.SemaphoreType.DMA((n,))],
    )(*arrs)


def _allreduce_small(v):
    rows = v.shape[0]
    half = rows // 2
    assert half % 8 == 0 and 2 * half == rows

    def body(v_ref, out_ref, sib_ref, pair_ref, slots_ref, send, recv):
        x, y, c = _place()
        chip = 2 * x + y
        sibling = (x, y, 1 - c)
        peers = [(1 - x, y), (x, 1 - y), (1 - x, 1 - y)]
        mine = pl.ds(pl.multiple_of(c * half, 8), half)

        first = pltpu.make_async_remote_copy(src_ref=v_ref, dst_ref=sib_ref, send_sem=send.at[0], recv_sem=recv.at[0],
                                             device_id=sibling, device_id_type=_MESH)
        first.start()
        first.wait_recv()
        pair_ref[...] = v_ref[...] + sib_ref[...]
        slots_ref[chip] = pair_ref[mine, :]
        cross = [pltpu.make_async_remote_copy(src_ref=pair_ref.at[mine], dst_ref=slots_ref.at[chip],
                                              send_sem=send.at[1 + k], recv_sem=recv.at[1 + k],
                                              device_id=(px, py, c), device_id_type=_MESH)
                 for k, (px, py) in enumerate(peers)]
        for cp in cross:
            cp.start()
        for cp in cross:
            cp.wait_recv()
        out_ref[mine, :] = ((slots_ref[0] + slots_ref[1]) + slots_ref[2]) + slots_ref[3]
        last = pltpu.make_async_remote_copy(src_ref=out_ref.at[mine], dst_ref=out_ref.at[mine], send_sem=send.at[4],
                                            recv_sem=recv.at[4], device_id=sibling, device_id_type=_MESH)
        last.start()
        last.wait_recv()
        first.wait_send()
        for cp in cross:
            cp.wait_send()
        last.wait_send()

    vm = pl.BlockSpec(memory_space=pltpu.VMEM)
    return _pcall(
        body, name="allreduce_small", in_specs=[vm], out_specs=vm,
        out_shape=jax.ShapeDtypeStruct((rows, 128), F32),
        scratch_shapes=[pltpu.VMEM((rows, 128), F32), pltpu.VMEM((rows, 128), F32), pltpu.VMEM((4, half, 128), F32),
                        pltpu.SemaphoreType.DMA((5,)), pltpu.SemaphoreType.DMA((5,))],
        compiler_params=_cp(None, 40),
    )(v)


def _sum_slots(own, land):
    ns, rows, cols = land.shape
    tm = _row_tile(rows, 256, 8)

    def body(own_ref, a_ref, o_ref):
        o_ref[...] = ((own_ref[...] + a_ref[0]) + a_ref[1]) + a_ref[2]

    return _pcall(
        body, name="sum_slots", grid=(rows // tm,),
        in_specs=[pl.BlockSpec((tm, cols), lambda i: (i, 0)), pl.BlockSpec((ns, tm, cols), lambda i: (0, i, 0))],
        out_specs=pl.BlockSpec((tm, cols), lambda i: (i, 0)),
        out_shape=jax.ShapeDtypeStruct((rows, cols), F32),
        compiler_params=_cp(("parallel",), 40),
    )(own, land)


def _adamw(w, m, v, g0, g1=None):
    rows, cols = w.shape[-2:]
    lead = w.ndim == 3
    tm = _row_tile(rows, 256, 8)
    c1 = 1.0 - ADAM_B1 ** ADAM_STEP
    c2 = 1.0 - ADAM_B2 ** ADAM_STEP
    two = g1 is not None

    def body(*refs):
        w_ref, m_ref, v_ref, g0_ref = refs[:4]
        g_ref, d_ref, nm_ref, nv_ref = refs[-4:]
        g = g0_ref[...]
        if two:
            g = g + refs[4][...]
        nm = ADAM_B1 * m_ref[...] + (1.0 - ADAM_B1) * g
        nv = ADAM_B2 * v_ref[...] + (1.0 - ADAM_B2) * (g * g)
        g_ref[...] = g
        nm_ref[...] = nm
        nv_ref[...] = nv
        d_ref[...] = -ADAM_LR * ((nm / c1) / (jnp.sqrt(nv / c2) + ADAM_EPS) + ADAM_WD * w_ref[...])

    blk = pl.BlockSpec((tm, cols), lambda i: (i, 0))
    wblk = pl.BlockSpec((None, tm, cols), lambda i: (0, i, 0)) if lead else blk
    ins = [w, m, v, g0] + ([g1] if two else [])
    return _pcall(
        body, name="adamw", grid=(rows // tm,), in_specs=[wblk] * 3 + [blk] * (len(ins) - 3), out_specs=[wblk] * 4,
        out_shape=[jax.ShapeDtypeStruct(w.shape, F32)] * 4,
        compiler_params=_cp(("parallel",), 40),
    )(*ins)


_BIG = ("w_in", "s5_w_glu", "m_w_out", "w_o", "w_up", "w_down")
_SMALL = ("ln0_g", "ln0_b", "b_in", "qk_conv_b", "s5_lambda_re", "s5_lambda_im", "s5_log_dt", "s5_b_re", "s5_b_im",
          "s5_c_re", "s5_c_im", "s5_d", "m_norm_g", "ln1_g", "ln1_b", "b_up", "ln2_g", "ln2_b")
_SMALL_SHARDED = ("meta_tokens", "qk_conv_w")
_ORDER = ("meta_tokens", "ln0_g", "ln0_b", "w_in", "b_in", "qk_conv_w", "qk_conv_b", "s5_lambda_re", "s5_lambda_im",
          "s5_log_dt", "s5_b_re", "s5_b_im", "s5_c_re", "s5_c_im", "s5_d", "s5_w_glu", "m_norm_g", "m_w_out", "w_o",
          "ln1_g", "ln1_b", "w_up", "b_up", "w_down", "ln2_g", "ln2_b")


def _pack(arrs):
    flat = jnp.concatenate([a.reshape(-1) for a in arrs])
    n = flat.shape[0]
    rows = -(-n // 2048) * 16
    return jnp.pad(flat, (0, rows * 128 - n)).reshape(rows, 128)


def _unpack(packed, shapes):
    flat = packed.reshape(-1)
    out, off = [], 0
    for s in shapes:
        n = math.prod(s)
        out.append(flat[off:off + n].reshape(s))
        off += n
    return out


def kernel(x, meta_tokens, ln0_g, ln0_b, w_in, b_in, qk_conv_w, qk_conv_b, s5_lambda_re, s5_lambda_im, s5_log_dt, s5_b_re, s5_b_im, s5_c_re, s5_c_im, s5_d, s5_w_glu, m_norm_g, m_w_out, w_o, ln1_g, ln1_b, w_up, b_up, w_down, ln2_g, ln2_b, loss_target, m_meta_tokens, m_ln0_g, m_ln0_b, m_w_in, m_b_in, m_qk_conv_w, m_qk_conv_b, m_s5_lambda_re, m_s5_lambda_im, m_s5_log_dt, m_s5_b_re, m_s5_b_im, m_s5_c_re, m_s5_c_im, m_s5_d, m_s5_w_glu, m_m_norm_g, m_m_w_out, m_w_o, m_ln1_g, m_ln1_b, m_w_up, m_b_up, m_w_down, m_ln2_g, m_ln2_b, v_meta_tokens, v_ln0_g, v_ln0_b, v_w_in, v_b_in, v_qk_conv_w, v_qk_conv_b, v_s5_lambda_re, v_s5_lambda_im, v_s5_log_dt, v_s5_b_re, v_s5_b_im, v_s5_c_re, v_s5_c_im, v_s5_d, v_s5_w_glu, v_m_norm_g, v_m_w_out, v_w_o, v_ln1_g, v_ln1_b, v_w_up, v_b_up, v_w_down, v_ln2_g, v_ln2_b):
    wts = dict(meta_tokens=meta_tokens, ln0_g=ln0_g, ln0_b=ln0_b, w_in=w_in, b_in=b_in, qk_conv_w=qk_conv_w,
               qk_conv_b=qk_conv_b, s5_lambda_re=s5_lambda_re, s5_lambda_im=s5_lambda_im, s5_log_dt=s5_log_dt,
               s5_b_re=s5_b_re, s5_b_im=s5_b_im, s5_c_re=s5_c_re, s5_c_im=s5_c_im, s5_d=s5_d, s5_w_glu=s5_w_glu,
               m_norm_g=m_norm_g, m_w_out=m_w_out, w_o=w_o, ln1_g=ln1_g, ln1_b=ln1_b, w_up=w_up, b_up=b_up,
               w_down=w_down, ln2_g=ln2_g, ln2_b=ln2_b)
    mom = dict(meta_tokens=m_meta_tokens, ln0_g=m_ln0_g, ln0_b=m_ln0_b, w_in=m_w_in, b_in=m_b_in, qk_conv_w=m_qk_conv_w,
               qk_conv_b=m_qk_conv_b, s5_lambda_re=m_s5_lambda_re, s5_lambda_im=m_s5_lambda_im, s5_log_dt=m_s5_log_dt,
               s5_b_re=m_s5_b_re, s5_b_im=m_s5_b_im, s5_c_re=m_s5_c_re, s5_c_im=m_s5_c_im, s5_d=m_s5_d,
               s5_w_glu=m_s5_w_glu, m_norm_g=m_m_norm_g, m_w_out=m_m_w_out, w_o=m_w_o, ln1_g=m_ln1_g, ln1_b=m_ln1_b,
               w_up=m_w_up, b_up=m_b_up, w_down=m_w_down, ln2_g=m_ln2_g, ln2_b=m_ln2_b)
    var = dict(meta_tokens=v_meta_tokens, ln0_g=v_ln0_g, ln0_b=v_ln0_b, w_in=v_w_in, b_in=v_b_in, qk_conv_w=v_qk_conv_w,
               qk_conv_b=v_qk_conv_b, s5_lambda_re=v_s5_lambda_re, s5_lambda_im=v_s5_lambda_im, s5_log_dt=v_s5_log_dt,
               s5_b_re=v_s5_b_re, s5_b_im=v_s5_b_im, s5_c_re=v_s5_c_re, s5_c_im=v_s5_c_im, s5_d=v_s5_d,
               s5_w_glu=v_s5_w_glu, m_norm_g=v_m_norm_g, m_w_out=v_m_w_out, w_o=v_w_o, ln1_g=v_ln1_g, ln1_b=v_ln1_b,
               w_up=v_w_up, b_up=v_b_up, w_down=v_w_down, ln2_g=v_ln2_g, ln2_b=v_ln2_b)
    d = x.shape[-1]
    chip = 2 * lax.axis_index("x") + lax.axis_index("y")

    gw = dict(zip(_SMALL_SHARDED, _gather_chips([meta_tokens, qk_conv_w[0]])))
    own_w_in = _bf(w_in[0])
    fsend, frecv, fsrc, fland, ftok = _xchg_start([own_w_in], [lax.empty((4,) + own_w_in.shape, BF16)],
                                                  name="gather_w_in_start", scatter=False, dep=gw["qk_conv_w"])
    late_names = tuple(n for n in _BIG if n != "w_in")
    cat = lambda a: jnp.transpose(a, (1, 0, 2)).reshape(a.shape[1], 4 * a.shape[2])
    w = dict(
        meta_tokens=cat(gw["meta_tokens"]), ln0_g=ln0_g[None], ln0_b=_tie(ln0_b[None], ftok),
        qk_conv_w=cat(gw["qk_conv_w"]), qk_conv_b=qk_conv_b,
        s5_lambda_re=s5_lambda_re[0], s5_lambda_im=s5_lambda_im[0], s5_log_dt=s5_log_dt[0][:, None],
        s5_b_re=s5_b_re[0], s5_b_im=s5_b_im[0], s5_c_re=s5_c_re[0], s5_c_im=s5_c_im[0], s5_d=s5_d,
        m_norm_g=m_norm_g, ln1_g=ln1_g, ln1_b=ln1_b, b_up=b_up, ln2_g=ln2_g, ln2_b=ln2_b)
    in_flight = {}

    def place_own(src, land):
        return lax.dynamic_update_slice(land, src[None], (chip,) + (0,) * src.ndim)

    def early(after):
        src, land = _xchg_wait(fsend, frecv, fsrc, fland, after, name="gather_w_in_wait", scatter=False)
        late_src = [_bf(wts[n][0]) for n in late_names]
        st = _xchg_start(late_src, [lax.empty((4,) + a.shape, a.dtype) for a in late_src], name="gather_late_start",
                         scatter=False, dep=src[0])
        in_flight["late"] = st[:4]
        return dict(w_in=_w_in_from_slots(place_own(src[0], land[0])), b_in=_tie(_to_pad_cols(b_in), st[4]))

    def late(after):
        src, land = _xchg_wait(*in_flight["late"], after, name="gather_late_wait", scatter=False)
        full = {n: place_own(s, ld) for n, s, ld in zip(late_names, src, land)}
        return dict(s5_w_glu=full["s5_w_glu"], m_w_out=full["m_w_out"].reshape(d, d), w_o=full["w_o"].reshape(d, d),
                    w_up=full["w_up"], w_down=full["w_down"].reshape(4 * d, d))

    flying = []

    def ready(names, g):
        parts = dict(
            w_in=lambda: _slots_from_w_in(g["w_in"][0]), s5_w_glu=lambda: g["s5_w_glu"],
            m_w_out=lambda: g["m_w_out"].reshape(4, d // 4, d), w_o=lambda: g["w_o"].reshape(4, d // 4, d),
            w_up=lambda: g["w_up"], w_down=lambda: g["w_down"].reshape(4, d, d))
        src = [parts[n]() for n in names]
        land = [lax.empty((3,) + a.shape[1:], a.dtype) for a in src]
        st = _xchg_start(src, land, name="scatter_" + names[0] + "_start", scatter=True)
        flying.append((names,) + st[:4])
        return st[4]

    loss, grad_x, g = _local_step(x, loss_target, w, early, late, ready)
    g["b_in"] = _from_pad_cols(g["b_in"])

    res = {}

    def finish(groups, after, tag):
        mine = {}
        for names, send, recv, src, land in groups:
            src, land = _xchg_wait(send, recv, src, land, after, name="scatter_" + names[0] + "_wait", scatter=True)
            for n, s, ld in zip(names, src, land):
                mine[n] = _sum_slots(lax.dynamic_index_in_dim(s, chip, 0, keepdims=False), ld)
        theirs = _swap_cores(list(mine.values()), name="swap_cores_" + tag)
        for n, t in zip(mine, theirs):
            res[n] = _adamw(wts[n], mom[n], var[n], mine[n], t)

    finish(flying[:-1], g["ln0_g"], "a")

    small_shapes = [(1, 128)] + [wts[n].shape for n in _SMALL] + [g[n].shape for n in _SMALL_SHARDED]
    packed = _pack([loss] + [g[n] for n in _SMALL] + [g[n] for n in _SMALL_SHARDED])
    tot = _unpack(_allreduce_small(packed), small_shapes)
    loss_out = tot[0][0, 0]
    gsm = dict(zip(_SMALL + _SMALL_SHARDED, tot[1:]))
    for n in _SMALL_SHARDED:
        cols = wts[n].shape[-1]
        gsm[n] = lax.dynamic_slice_in_dim(gsm[n], chip * cols, cols, axis=1).reshape(wts[n].shape)

    names = _SMALL + _SMALL_SHARDED
    shapes = [wts[n].shape for n in names]
    pk = lambda dct: _pack([dct[n] for n in names])
    small_out = _adamw(pk(wts), pk(mom), pk(var), pk(gsm))
    small_res = [_unpack(r, shapes) for r in small_out]
    for j, n in enumerate(names):
        res[n] = [small_res[q][j] for q in range(4)]
    finish(flying[-1:], small_out[0], "b")

    return (loss_out, grad_x, *[res[n][0] for n in _ORDER], *[res[n][1] for n in _ORDER],
            *[res[n][2] for n in _ORDER], *[res[n][3] for n in _ORDER])
```

```python
import functools
import math

import jax
import jax.numpy as jnp
from jax import lax
from jax.experimental import pallas as pl
from jax.experimental.pallas import tpu as pltpu

F32 = jnp.float32
BF16 = jnp.bfloat16
HI = lax.Precision.HIGHEST

N_META = 16
M_HEADS = 4
M_CHUNK = 128
PAD = M_CHUNK - N_META
CONV_W = 4
S5_GROUP = 16
S5_STATE = 64
S5_KCH = 4
LN_EPS = 1e-5
ALPHA = 2.0 ** 0.25
NEG = -1e30
ADAM_LR, ADAM_B1, ADAM_B2, ADAM_EPS, ADAM_WD, ADAM_STEP = 0.001, 0.9, 0.999, 1e-08, 0.01, 10

O_OFF, GS_OFF, GM_OFF, V_OFF, Q_OFF, K_OFF, U_OFF, G_OFF, NP = 0, 1024, 2048, 3072, 4096, 4608, 5120, 5632, 5760

NN = ((1,), (0,))
NT = ((1,), (1,))
TN = ((0,), (0,))


def _dot(a, b, dims=NN, prec=None):
    return lax.dot_general(a, b, (dims, ((), ())), preferred_element_type=F32, precision=prec)


def _bf(x):
    return x.astype(BF16)


def _sig(x):
    return 0.5 * jnp.tanh(0.5 * x) + 0.5


def _pcall(body, **kw):
    return pl.pallas_call(body, **kw)


def _cp(sem=None, vmem_mb=None):
    kw = {}
    if sem is not None:
        kw["dimension_semantics"] = sem
    if vmem_mb is not None:
        kw["vmem_limit_bytes"] = vmem_mb << 20
    return pltpu.CompilerParams(**kw)


def _row_tile(n, want, mult=16):
    best = None
    for t in range(mult, want + 1, mult):
        if n % t == 0:
            best = t
    assert best is not None, (n, want)
    return best


def _resident(shape):
    nd = len(shape)
    return pl.BlockSpec(shape, lambda *_: (0,) * nd, pipeline_mode=pl.Buffered(1))


def _const(shape):
    nd = len(shape)
    return pl.BlockSpec(shape, lambda *_: (0,) * nd)


def _ln_fwd(x, g, b):
    mu = jnp.mean(x, axis=-1, keepdims=True)
    xc = x - mu
    var = jnp.mean(xc * xc, axis=-1, keepdims=True)
    rstd = lax.rsqrt(var + LN_EPS)
    xhat = xc * rstd
    return xhat * g + b, xhat, rstd


def _ln_bwd(dy, xhat, rstd, g):
    dxh = dy * g
    m1 = jnp.mean(dxh, axis=-1, keepdims=True)
    m2 = jnp.mean(dxh * xhat, axis=-1, keepdims=True)
    return rstd * (dxh - m1 - xhat * m2)


def _colsum(x):
    return jnp.sum(x, axis=0, keepdims=True)


def _to_pad_cols(w):
    u, q, k, v, o, gi, gf, gs, gm = (w[..., 0:512], w[..., 512:1024], w[..., 1024:1536], w[..., 1536:2560],
                                     w[..., 2560:3584], w[..., 3584:3588], w[..., 3588:3592], w[..., 3592:4616],
                                     w[..., 4616:5640])
    z = jnp.zeros(w.shape[:-1] + (NP - G_OFF - 8,), w.dtype)
    return jnp.concatenate([o, gs, gm, v, q, k, u, gi, gf, z], axis=-1)


def _from_pad_cols(w):
    o, gs, gm, v, q, k, u = (w[..., O_OFF:GS_OFF], w[..., GS_OFF:GM_OFF], w[..., GM_OFF:V_OFF], w[..., V_OFF:Q_OFF],
                             w[..., Q_OFF:K_OFF], w[..., K_OFF:U_OFF], w[..., U_OFF:G_OFF])
    gi, gf = w[..., G_OFF:G_OFF + 4], w[..., G_OFF + 4:G_OFF + 8]
    return jnp.concatenate([u, q, k, v, o, gi, gf, gs, gm], axis=-1)


_IN_REF = (("u", 512), ("q", 512), ("k", 512), ("v", 1024), ("o", 1024), ("i", 4), ("f", 4), ("gs", 1024), ("gm", 1024))
_IN_PAD = (("o", O_OFF), ("gs", GS_OFF), ("gm", GM_OFF), ("v", V_OFF), ("q", Q_OFF), ("k", K_OFF), ("u", U_OFF),
           ("i", G_OFF), ("f", G_OFF + 4))


def _in_ref_ranges():
    out, off = {}, 0
    for n, s in _IN_REF:
        out[n] = (off, off + s)
        off += s
    return out, off


def _w_in_from_slots(g):
    rng, total = _in_ref_ranges()
    width = total // g.shape[0]
    cols = []
    for n, _ in _IN_PAD:
        a, b = rng[n]
        while a < b:
            s = a // width
            e = min(b, (s + 1) * width)
            cols.append(g[s][:, a - s * width:e - s * width])
            a = e
    cols.append(jnp.zeros((g.shape[1], NP - G_OFF - 8), g.dtype))
    return jnp.concatenate(cols, axis=1)


def _slots_from_w_in(wp, nslot=4):
    rng, total = _in_ref_ranges()
    width = total // nslot
    pad_off = dict(_IN_PAD)
    slots = []
    for s in range(nslot):
        lo, hi = s * width, (s + 1) * width
        cols = []
        for n, _ in _IN_REF:
            a, b = rng[n]
            x0, x1 = max(a, lo), min(b, hi)
            if x0 < x1:
                cols.append(wp[:, pad_off[n] + x0 - a:pad_off[n] + x1 - a])
        slots.append(jnp.concatenate(cols, axis=1))
    return jnp.stack(slots, axis=0)


def _ln0_fwd(hin, g, b, lp):
    r, d = hin.shape
    tm = _row_tile(lp, 416)

    def body(x_ref, g_ref, b_ref, o_ref):
        y, _, _ = _ln_fwd(x_ref[...], g_ref[...], b_ref[...])
        o_ref[...] = y

    return _pcall(
        body, name="ln0_fwd", grid=(r // tm,),
        in_specs=[pl.BlockSpec((tm, d), lambda i: (i, 0)), _const((1, d)), _const((1, d))],
        out_specs=pl.BlockSpec((tm, d), lambda i: (i, 0)),
        out_shape=jax.ShapeDtypeStruct((r, d), F32),
        compiler_params=_cp(("parallel",)),
    )(hin, g, b)


def _ln0_bwd(hin, dr1, dpw, g, lp):
    r, d = hin.shape
    tm = _row_tile(lp, 416)
    tps = lp // tm
    assert tm >= PAD + N_META

    def body(x_ref, a_ref, c_ref, g_ref, o_ref, dg_ref, db_ref, dm_ref):
        i = pl.program_id(0)

        @pl.when(i == 0)
        def _():
            dg_ref[...] = jnp.zeros_like(dg_ref)
            db_ref[...] = jnp.zeros_like(db_ref)
            dm_ref[...] = jnp.zeros_like(dm_ref)

        dy = ALPHA * a_ref[...] + c_ref[...]
        _, xhat, rstd = _ln_fwd(x_ref[...], g_ref[...], 0.0)
        dx = _ln_bwd(dy, xhat, rstd, g_ref[...])
        o_ref[...] = dx
        dg_ref[...] += _colsum(dy * xhat)
        db_ref[...] += _colsum(dy)

        @pl.when(i % tps == 0)
        def _():
            dm_ref[...] += dx[PAD:PAD + N_META, :]

    return _pcall(
        body, name="ln0_bwd", grid=(r // tm,),
        in_specs=[pl.BlockSpec((tm, d), lambda i: (i, 0))] * 3 + [_const((1, d))],
        out_specs=[pl.BlockSpec((tm, d), lambda i: (i, 0)), _const((1, d)), _const((1, d)), _const((N_META, d))],
        out_shape=[jax.ShapeDtypeStruct((r, d), F32), jax.ShapeDtypeStruct((1, d), F32),
                   jax.ShapeDtypeStruct((1, d), F32), jax.ShapeDtypeStruct((N_META, d), F32)],
        compiler_params=_cp(("arbitrary",)),
    )(hin, dr1, dpw, g)


def _inproj(h0, w_bf, bias, lp):
    r, d = h0.shape
    n = w_bf.shape[1]
    tm = _row_tile(lp, 832)
    tn = 1152
    tps = lp // tm

    def body(a_ref, w_ref, b_ref, o_ref):
        i = pl.program_id(0)
        acc = _dot(_bf(a_ref[...]), w_ref[...]) + b_ref[...]
        t = (i % tps) * tm + lax.broadcasted_iota(jnp.int32, (tm, 1), 0)
        o_ref[...] = jnp.where(t >= PAD, acc, 0.0)

    return _pcall(
        body, name="inproj", grid=(r // tm, n // tn),
        in_specs=[pl.BlockSpec((tm, d), lambda i, j: (i, 0)), pl.BlockSpec((d, tn), lambda i, j: (0, j)),
                  pl.BlockSpec((1, tn), lambda i, j: (0, j))],
        out_specs=pl.BlockSpec((tm, tn), lambda i, j: (i, j)),
        out_shape=jax.ShapeDtypeStruct((r, n), F32),
        compiler_params=_cp(("parallel", "parallel"), 48),
    )(h0, w_bf, bias)


def _mm_tn(a, b, *, name, split=1, colsum=False, tk_want=832):
    r, m = a.shape
    n = b.shape[1]
    tk = _row_tile(r, tk_want)
    tm = min(m, 1024)
    ns = n // split
    tn = ns
    for cand in (1024, 1152, 640, 512, 128):
        if ns % cand == 0 and cand <= ns:
            tn = cand
            break
    nb = ns // tn
    nk = r // tk

    def body(a_ref, b_ref, o_ref, *rest):
        acc = rest[-1]
        k = pl.program_id(2)

        @pl.when(k == 0)
        def _():
            acc[...] = jnp.zeros_like(acc)

        bt = b_ref[...]
        acc[...] += _dot(_bf(a_ref[...]), _bf(bt), TN)

        @pl.when(k == nk - 1)
        def _():
            o_ref[...] = acc[...]

        if colsum:
            cs_ref = rest[0]

            @pl.when(k == 0)
            def _():
                cs_ref[...] = jnp.zeros_like(cs_ref)

            cs_ref[...] += _colsum(bt.astype(F32))

    out_specs = [pl.BlockSpec((None, tm, tn), lambda i, j, k: (j // nb, i, j % nb))]
    out_shape = [jax.ShapeDtypeStruct((split, m, ns), F32)]
    if colsum:
        assert m == tm
        out_specs.append(pl.BlockSpec((1, tn), lambda i, j, k: (0, j)))
        out_shape.append(jax.ShapeDtypeStruct((1, n), F32))
    res = _pcall(
        body, name=name, grid=(m // tm, n // tn, nk),
        in_specs=[pl.BlockSpec((tk, tm), lambda i, j, k: (k, i)), pl.BlockSpec((tk, tn), lambda i, j, k: (k, j))],
        out_specs=out_specs, out_shape=out_shape,
        scratch_shapes=[pltpu.VMEM((tm, tn), F32)],
        compiler_params=_cp(("parallel", "parallel", "arbitrary"), 48),
    )(a, b)
    return res if colsum else res[0]


def _mm_nt(a, w_bf, lp, *, name, dep=None):
    r, kdim = a.shape
    n = w_bf.shape[0]
    tm = _row_tile(lp, 832)
    tk = 1152
    nk = kdim // tk
    deps = [] if dep is None else [dep]

    def body(a_ref, w_ref, *rest):
        o_ref, acc = rest[-2:]
        k = pl.program_id(1)

        @pl.when(k == 0)
        def _():
            acc[...] = jnp.zeros_like(acc)

        acc[...] += _dot(_bf(a_ref[...]), w_ref[...], NT)

        @pl.when(k == nk - 1)
        def _():
            o_ref[...] = acc[...]

    return _pcall(
        body, name=name, grid=(r // tm, nk),
        in_specs=[pl.BlockSpec((tm, tk), lambda i, k: (i, k)), pl.BlockSpec((n, tk), lambda i, k: (0, k))]
        + [_const(dp_.shape) for dp_ in deps],
        out_specs=pl.BlockSpec((tm, n), lambda i, k: (i, 0)),
        out_shape=jax.ShapeDtypeStruct((r, n), F32),
        scratch_shapes=[pltpu.VMEM((tm, n), F32)],
        compiler_params=_cp(("parallel", "arbitrary"), 48),
    )(a, w_bf, *deps)


def _s5_prep(lam_re, lam_im, log_dt, b_re_t, b_im_t):
    g, p = lam_re.shape
    h = b_re_t.shape[0]

    def body(lr_ref, li_ref, ldt_ref, br_ref, bi_ref, pr_ref, pi_ref, bbr_ref, bbi_ref):
        lr, li = lr_ref[...], li_ref[...]
        dt = jnp.exp(ldt_ref[...])
        e = jnp.exp(lr * dt)
        ar, ai = e * jnp.cos(li * dt), e * jnp.sin(li * dt)
        den = lr * lr + li * li
        cr = ((ar - 1.0) * lr + ai * li) / den
        ci = (ai * lr - (ar - 1.0) * li) / den
        br, bi = br_ref[...], bi_ref[...]
        bbr_ref[...] = cr[None] * br - ci[None] * bi
        bbi_ref[...] = cr[None] * bi + ci[None] * br
        xr, xi = ar, ai
        pr_ref[0] = xr
        pi_ref[0] = xi
        for t in range(1, 8):
            xr, xi = xr * ar - xi * ai, xr * ai + xi * ar
            pr_ref[t] = xr
            pi_ref[t] = xi

    sd = jax.ShapeDtypeStruct
    return _pcall(body, name="s5_prep",
                  out_shape=[sd((8, g, p), F32), sd((8, g, p), F32), sd((h, g, p), F32), sd((h, g, p), F32)])(
        lam_re, lam_im, log_dt, b_re_t, b_im_t)


def _s5_prep_bwd(lam_re, lam_im, log_dt, b_re_t, b_im_t, da_re, da_im, dbb_re_t, dbb_im_t):
    g, p = lam_re.shape
    h = b_re_t.shape[0]

    def body(lr_ref, li_ref, ldt_ref, br_ref, bi_ref, dar_ref, dai_ref, dbr_ref, dbi_ref,
             glr_ref, gli_ref, gdt_ref, gbr_ref, gbi_ref):
        lr, li = lr_ref[...], li_ref[...]
        dt = jnp.exp(ldt_ref[...])
        e = jnp.exp(lr * dt)
        ar, ai = e * jnp.cos(li * dt), e * jnp.sin(li * dt)
        den = lr * lr + li * li
        cr = ((ar - 1.0) * lr + ai * li) / den
        ci = (ai * lr - (ar - 1.0) * li) / den
        br, bi = br_ref[...], bi_ref[...]
        gr, gi = dbr_ref[...], dbi_ref[...]
        gbr_ref[...] = gr * cr[None] + gi * ci[None]
        gbi_ref[...] = gi * cr[None] - gr * ci[None]
        gcr = jnp.sum(gr * br + gi * bi, axis=0)
        gci = jnp.sum(gi * br - gr * bi, axis=0)
        ilr, ili = lr / den, -li / den
        gar = dar_ref[...] + gcr * ilr + gci * ili
        gai = dai_ref[...] + gci * ilr - gcr * ili
        qr, qi = cr * ilr - ci * ili, cr * ili + ci * ilr
        glr = -(gcr * qr + gci * qi)
        gli = -(gci * qr - gcr * qi)
        gzr = gar * ar + gai * ai
        gzi = gai * ar - gar * ai
        glr_ref[...] = glr + gzr * dt
        gli_ref[...] = gli + gzi * dt
        gdt_ref[...] = jnp.sum(gzr * lr + gzi * li, axis=1, keepdims=True) * dt

    sd = jax.ShapeDtypeStruct
    return _pcall(body, name="s5_prep_bwd",
                  out_shape=[sd((g, p), F32), sd((g, p), F32), sd((g, 1), F32), sd((h, g, p), F32), sd((h, g, p), F32)])(
        lam_re, lam_im, log_dt, b_re_t, b_im_t, da_re, da_im, dbb_re_t, dbb_im_t)


def _cmul(xr, xi, yr, yi):
    return xr * yr - xi * yi, xr * yi + xi * yr


def _dot5(a, b, dims=NN):
    return _dot(_bf(a), _bf(b), dims)


def _s5_fwd(p3, bk, cre, cim, apow, dskip):
    bsz, lp, _ = p3.shape
    tt = _row_tile(lp, 528, 8)
    nt = lp // tt
    nblk = tt // 8
    hw = 512

    def body(u_ref, bk_ref, cre_ref, cim_ref, ap_ref, d_ref, y_ref, xs_ref, car_ref):
        t = pl.program_id(2)

        @pl.when(t == 0)
        def _():
            car_ref[...] = jnp.zeros_like(car_ref)

        u = u_ref[...]
        xs_ref[...] = _dot5(u, bk_ref[...])
        ap = ap_ref[...]
        apr, api = ap[:, :hw], ap[:, hw:]
        rows = lax.broadcasted_iota(jnp.int32, (8, hw), 0)
        lev = [(d, jnp.where(rows < d, 0.0, jnp.broadcast_to(apr[d - 1:d, :], (8, hw))),
                jnp.where(rows < d, 0.0, jnp.broadcast_to(api[d - 1:d, :], (8, hw)))) for d in (1, 2, 4)]

        def blk(i, carry):
            cr, ci = carry
            off = pl.multiple_of(i * 8, 8)
            x = xs_ref[pl.ds(off, 8), :]
            xr, xi = x[:, :hw], x[:, hw:]
            for d, lr, li in lev:
                mr, mi = _cmul(pltpu.roll(xr, d, 0), pltpu.roll(xi, d, 0), lr, li)
                xr, xi = xr + mr, xi + mi
            mr, mi = _cmul(apr, api, cr, ci)
            xr, xi = xr + mr, xi + mi
            xs_ref[pl.ds(off, 8), :] = jnp.concatenate([xr, xi], axis=1)
            return xr[7:8, :], xi[7:8, :]

        c0 = car_ref[...]
        cr, ci = lax.fori_loop(0, nblk, blk, (c0[0:1, :hw], c0[0:1, hw:]))
        car_ref[...] = jnp.broadcast_to(jnp.concatenate([cr, ci], axis=1), car_ref.shape)
        xs = xs_ref[...]
        y_ref[...] = (_dot5(xs[:, :hw], cre_ref[...]) - _dot5(xs[:, hw:], cim_ref[...])
                      + d_ref[...] * u)

    ub = U_OFF // 128
    return _pcall(
        body, name="s5_fwd", grid=(S5_KCH, bsz, nt),
        in_specs=[pl.BlockSpec((None, tt, 128), lambda k, b, t: (b, t, ub + k)),
                  pl.BlockSpec((None, 128, 2 * hw), lambda k, b, t: (k, 0, 0)),
                  pl.BlockSpec((None, hw, 128), lambda k, b, t: (k, 0, 0)),
                  pl.BlockSpec((None, hw, 128), lambda k, b, t: (k, 0, 0)),
                  pl.BlockSpec((None, 8, 2 * hw), lambda k, b, t: (k, 0, 0)),
                  pl.BlockSpec((1, 128), lambda k, b, t: (0, k))],
        out_specs=[pl.BlockSpec((None, tt, 128), lambda k, b, t: (b, t, k)),
                   pl.BlockSpec((None, None, tt, 2 * hw), lambda k, b, t: (b, k, t, 0))],
        out_shape=[jax.ShapeDtypeStruct((bsz, lp, S5_KCH * 128), F32),
                   jax.ShapeDtypeStruct((bsz, S5_KCH, lp, 2 * hw), F32)],
        scratch_shapes=[pltpu.VMEM((8, 2 * hw), F32)],
        compiler_params=_cp(("parallel", "parallel", "arbitrary"), 40),
    )(p3, bk, cre, cim, apow, dskip)


def _s5_bwd(dp3, p3, dy3, xs, bk, cre, cim, apow_rev, dskip):
    bsz, lp, _ = p3.shape
    tt = _row_tile(lp, 528, 8)
    nt = lp // tt
    nblk = tt // 8
    hw = 512
    tb = tt // 8

    def body(dp_any, u_ref, dy_ref, xs_ref, halo_ref, bk_ref, cre_ref, cim_ref, ap_ref, d_ref,
             du_ref, dbk_ref, dcre_ref, dcim_ref, da_ref, dd_ref, g_ref, ext_ref, car_ref):
        b = pl.program_id(1)
        t = pl.program_id(2)
        tidx = nt - 1 - t

        @pl.when(t == 0)
        def _():
            car_ref[...] = jnp.zeros_like(car_ref)

        @pl.when((b == 0) & (t == 0))
        def _():
            dbk_ref[...] = jnp.zeros_like(dbk_ref)
            dcre_ref[...] = jnp.zeros_like(dcre_ref)
            dcim_ref[...] = jnp.zeros_like(dcim_ref)
            da_ref[...] = jnp.zeros_like(da_ref)
            dd_ref[...] = jnp.zeros_like(dd_ref)

        u = u_ref[...]
        dy = dy_ref[...]
        g_ref[:, :hw] = _dot5(dy, cre_ref[...], NT)
        g_ref[:, hw:] = -_dot5(dy, cim_ref[...], NT)
        ap = ap_ref[...]
        apr, api = ap[:, :hw], -ap[:, hw:]
        rows = lax.broadcasted_iota(jnp.int32, (8, hw), 0)
        lev = [(d, jnp.where(rows >= 8 - d, 0.0, jnp.broadcast_to(apr[8 - d:9 - d, :], (8, hw))),
                jnp.where(rows >= 8 - d, 0.0, jnp.broadcast_to(api[8 - d:9 - d, :], (8, hw)))) for d in (1, 2, 4)]

        def blk(i, carry):
            cr, ci = carry
            off = pl.multiple_of((nblk - 1 - i) * 8, 8)
            x = g_ref[pl.ds(off, 8), :]
            xr, xi = x[:, :hw], x[:, hw:]
            for d, lr, li in lev:
                mr, mi = _cmul(pltpu.roll(xr, 8 - d, 0), pltpu.roll(xi, 8 - d, 0), lr, li)
                xr, xi = xr + mr, xi + mi
            mr, mi = _cmul(apr, api, cr, ci)
            xr, xi = xr + mr, xi + mi
            g_ref[pl.ds(off, 8), :] = jnp.concatenate([xr, xi], axis=1)
            return xr[0:1, :], xi[0:1, :]

        c0 = car_ref[...]
        cr, ci = lax.fori_loop(0, nblk, blk, (c0[0:1, :hw], c0[0:1, hw:]))
        car_ref[...] = jnp.broadcast_to(jnp.concatenate([cr, ci], axis=1), car_ref.shape)

        gg = g_ref[...]
        du = _dot5(gg, bk_ref[...], NT) + d_ref[...] * dy
        trow = tidx * tt + lax.broadcasted_iota(jnp.int32, (tt, 1), 0)
        du_ref[...] = jnp.where(trow >= PAD, du, 0.0)
        dbk_ref[...] += _dot5(u, gg, TN)
        xsv = xs_ref[...]
        dcre_ref[...] += _dot5(xsv[:, :hw], dy, TN)
        dcim_ref[...] -= _dot5(xsv[:, hw:], dy, TN)
        dd_ref[...] += _colsum(dy * u)
        ext_ref[0:8, :] = jnp.where(tidx == 0, 0.0, halo_ref[...])
        ext_ref[8:, :] = xsv
        xp = ext_ref[pl.ds(7, tt), :]
        gr, gi, pr, pi = gg[:, :hw], gg[:, hw:], xp[:, :hw], xp[:, hw:]
        da_ref[:, :hw] += _colsum(gr * pr + gi * pi)
        da_ref[:, hw:] += _colsum(gi * pr - gr * pi)

    ub = U_OFF // 128
    sd = jax.ShapeDtypeStruct
    rt = lambda t: nt - 1 - t
    res = _pcall(
        body, name="s5_bwd", grid=(S5_KCH, bsz, nt),
        in_specs=[pl.BlockSpec(memory_space=pl.ANY),
                  pl.BlockSpec((None, tt, 128), lambda k, b, t: (b, rt(t), ub + k)),
                  pl.BlockSpec((None, tt, 128), lambda k, b, t: (b, rt(t), k)),
                  pl.BlockSpec((None, None, tt, 2 * hw), lambda k, b, t: (b, k, rt(t), 0)),
                  pl.BlockSpec((None, None, 8, 2 * hw), lambda k, b, t: (b, k, jnp.maximum(rt(t) * tb - 1, 0), 0)),
                  pl.BlockSpec((None, 128, 2 * hw), lambda k, b, t: (k, 0, 0)),
                  pl.BlockSpec((None, hw, 128), lambda k, b, t: (k, 0, 0)),
                  pl.BlockSpec((None, hw, 128), lambda k, b, t: (k, 0, 0)),
                  pl.BlockSpec((None, 8, 2 * hw), lambda k, b, t: (k, 0, 0)),
                  pl.BlockSpec((1, 128), lambda k, b, t: (0, k))],
        out_specs=[pl.BlockSpec((None, tt, 128), lambda k, b, t: (b, rt(t), ub + k)),
                   pl.BlockSpec((None, 128, 2 * hw), lambda k, b, t: (k, 0, 0)),
                   pl.BlockSpec((None, hw, 128), lambda k, b, t: (k, 0, 0)),
                   pl.BlockSpec((None, hw, 128), lambda k, b, t: (k, 0, 0)),
                   pl.BlockSpec((None, 1, 2 * hw), lambda k, b, t: (k, 0, 0)),
                   pl.BlockSpec((1, 128), lambda k, b, t: (0, k))],
        out_shape=[sd(dp3.shape, F32), sd((S5_KCH, 128, 2 * hw), F32), sd((S5_KCH, hw, 128), F32),
                   sd((S5_KCH, hw, 128), F32), sd((S5_KCH, 1, 2 * hw), F32), sd((1, S5_KCH * 128), F32)],
        scratch_shapes=[pltpu.VMEM((tt, 2 * hw), F32), pltpu.VMEM((tt + 8, 2 * hw), F32), pltpu.VMEM((8, 2 * hw), F32)],
        input_output_aliases={0: 0},
        compiler_params=_cp(("arbitrary", "arbitrary", "arbitrary"), 48),
    )(dp3, p3, dy3, xs, xs, bk, cre, cim, apow_rev, dskip)
    return res


_G0 = math.sqrt(2.0 / math.pi)
_G1 = 0.044715


def _gelu(y):
    return 0.5 * y * (1.0 + jnp.tanh(_G0 * (y + _G1 * y * y * y)))


def _gelu_grad(y):
    th = jnp.tanh(_G0 * (y + _G1 * y * y * y))
    return 0.5 * (1.0 + th) + 0.5 * y * (1.0 - th * th) * _G0 * (1.0 + 3.0 * _G1 * y * y)


def _glu_fwd(y_s5, wglu_g, lp):
    r, w = y_s5.shape
    tm = _row_tile(lp, 416)
    cw = wglu_g.shape[2]

    def body(y_ref, w_ref, gy_ref, z_ref, o_ref):
        gy = _bf(_gelu(y_ref[...]))
        gy_ref[...] = gy
        zs = [_dot(gy, w_ref[s]) for s in range(4)]
        for s in range(4):
            z_ref[:, s * cw:(s + 1) * cw] = zs[s]
        o_ref[:, :cw] = zs[0] * _sig(zs[2])
        o_ref[:, cw:] = zs[1] * _sig(zs[3])

    sd = jax.ShapeDtypeStruct
    return _pcall(
        body, name="glu_fwd", grid=(r // tm,),
        in_specs=[pl.BlockSpec((tm, w), lambda i: (i, 0)), _resident(wglu_g.shape)],
        out_specs=[pl.BlockSpec((tm, w), lambda i: (i, 0)), pl.BlockSpec((tm, 4 * cw), lambda i: (i, 0)),
                   pl.BlockSpec((tm, 2 * cw), lambda i: (i, 0))],
        out_shape=[sd((r, w), BF16), sd((r, 4 * cw), F32), sd((r, 2 * cw), F32)],
        compiler_params=_cp(("parallel",), 40),
    )(y_s5, wglu_g)


def _glu_bwd(dyg, z, y_s5, wglu_g, lp):
    r, w = y_s5.shape
    tm = _row_tile(lp, 416)
    cw = wglu_g.shape[2]

    def body(d_ref, z_ref, y_ref, w_ref, dz_ref, dy_ref):
        d = d_ref[...]
        zz = z_ref[...]
        acc = jnp.zeros((tm, w), F32)
        for s in range(2):
            z1 = zz[:, s * cw:(s + 1) * cw]
            sg = _sig(zz[:, (2 + s) * cw:(3 + s) * cw])
            dd = d[:, s * cw:(s + 1) * cw]
            dz1 = _bf(dd * sg)
            dz2 = _bf(dd * z1 * sg * (1.0 - sg))
            dz_ref[:, s * cw:(s + 1) * cw] = dz1
            dz_ref[:, (2 + s) * cw:(3 + s) * cw] = dz2
            acc += _dot(dz1, w_ref[s], NT) + _dot(dz2, w_ref[2 + s], NT)
        dy_ref[...] = acc * _gelu_grad(y_ref[...])

    sd = jax.ShapeDtypeStruct
    return _pcall(
        body, name="glu_bwd", grid=(r // tm,),
        in_specs=[pl.BlockSpec((tm, 2 * cw), lambda i: (i, 0)), pl.BlockSpec((tm, 4 * cw), lambda i: (i, 0)),
                  pl.BlockSpec((tm, w), lambda i: (i, 0)), _resident(wglu_g.shape)],
        out_specs=[pl.BlockSpec((tm, 4 * cw), lambda i: (i, 0)), pl.BlockSpec((tm, w), lambda i: (i, 0))],
        out_shape=[sd((r, 4 * cw), BF16), sd((r, w), F32)],
        compiler_params=_cp(("parallel",), 40),
    )(dyg, z, y_s5, wglu_g)


def _conv_fwd(p3, cw, cb):
    bsz, lp, _ = p3.shape
    tt = _row_tile(lp, 416)
    nt = lp // tt
    tb = tt // 8
    c = cw.shape[1]
    qb = Q_OFF // c

    def body(x_ref, halo_ref, w_ref, b_ref, pre_ref, act_ref, ext_ref):
        t = pl.program_id(1)
        ext_ref[0:8, :] = jnp.where(t == 0, 0.0, halo_ref[...])
        ext_ref[8:, :] = x_ref[...]
        w = w_ref[...]
        acc = b_ref[...] + w[0:1, :] * ext_ref[pl.ds(5, tt), :]
        for j in range(1, CONV_W):
            acc = acc + w[j:j + 1, :] * ext_ref[pl.ds(5 + j, tt), :]
        pre_ref[...] = acc
        act_ref[...] = acc * _sig(acc)

    sd = jax.ShapeDtypeStruct
    return _pcall(
        body, name="conv_fwd", grid=(bsz, nt),
        in_specs=[pl.BlockSpec((None, tt, c), lambda b, t: (b, t, qb)),
                  pl.BlockSpec((None, 8, c), lambda b, t: (b, jnp.maximum(t * tb - 1, 0), qb)),
                  _const((CONV_W, c)), _const((1, c))],
        out_specs=[pl.BlockSpec((None, tt, c), lambda b, t: (b, t, 0))] * 2,
        out_shape=[sd((bsz, lp, c), F32)] * 2,
        scratch_shapes=[pltpu.VMEM((tt + 8, c), F32)],
        compiler_params=_cp(("parallel", "parallel")),
    )(p3, p3, cw, cb)


def _conv_bwd(dp3, p3, dact3, pre3, cw):
    bsz, lp, _ = p3.shape
    tt = _row_tile(lp, 416)
    nt = lp // tt
    tb = tt // 8
    c = cw.shape[1]
    qb = Q_OFF // c

    def silu_grad(x):
        s = _sig(x)
        return s * (1.0 + x * (1.0 - s))

    def body(dp_any, x_ref, xh_ref, d_ref, dh_ref, pre_ref, preh_ref, w_ref, o_ref, dw_ref, db_ref, ext_ref, dext_ref):
        b = pl.program_id(0)
        t = pl.program_id(1)

        @pl.when((b == 0) & (t == 0))
        def _():
            dw_ref[...] = jnp.zeros_like(dw_ref)
            db_ref[...] = jnp.zeros_like(db_ref)

        dc = d_ref[...] * silu_grad(pre_ref[...])
        dch = jnp.where(t == nt - 1, 0.0, dh_ref[...] * silu_grad(preh_ref[...]))
        dext_ref[0:tt, :] = dc
        dext_ref[tt:, :] = dch
        ext_ref[0:8, :] = jnp.where(t == 0, 0.0, xh_ref[...])
        ext_ref[8:, :] = x_ref[...]
        w = w_ref[...]
        acc = w[CONV_W - 1:CONV_W, :] * dc
        for j in range(CONV_W - 1):
            acc = acc + w[j:j + 1, :] * dext_ref[pl.ds(CONV_W - 1 - j, tt), :]
        trow = t * tt + lax.broadcasted_iota(jnp.int32, (tt, 1), 0)
        o_ref[...] = jnp.where(trow >= PAD, acc, 0.0)
        db_ref[...] += _colsum(dc)
        for j in range(CONV_W):
            dw_ref[j:j + 1, :] += _colsum(dc * ext_ref[pl.ds(5 + j, tt), :])

    sd = jax.ShapeDtypeStruct
    nxt = lambda t: jnp.minimum((t + 1) * tb, lp // 8 - 1)
    return _pcall(
        body, name="conv_bwd", grid=(bsz, nt),
        in_specs=[pl.BlockSpec(memory_space=pl.ANY),
                  pl.BlockSpec((None, tt, c), lambda b, t: (b, t, qb)),
                  pl.BlockSpec((None, 8, c), lambda b, t: (b, jnp.maximum(t * tb - 1, 0), qb)),
                  pl.BlockSpec((None, tt, c), lambda b, t: (b, t, 0)),
                  pl.BlockSpec((None, 8, c), lambda b, t: (b, nxt(t), 0)),
                  pl.BlockSpec((None, tt, c), lambda b, t: (b, t, 0)),
                  pl.BlockSpec((None, 8, c), lambda b, t: (b, nxt(t), 0)),
                  _const((CONV_W, c))],
        out_specs=[pl.BlockSpec((None, tt, c), lambda b, t: (b, t, qb)), _const((CONV_W, c)), _const((1, c))],
        out_shape=[sd(dp3.shape, F32), sd((CONV_W, c), F32), sd((1, c), F32)],
        scratch_shapes=[pltpu.VMEM((tt + 8, c), F32), pltpu.VMEM((tt + 8, c), F32)],
        input_output_aliases={0: 0},
        compiler_params=_cp(("arbitrary", "arbitrary")),
    )(dp3, p3, p3, dact3, dact3, pre3, pre3, cw)


def _mlstm_gates(g, h_idx, c_idx, lc):
    lane = lax.broadcasted_iota(jnp.int32, g.shape, 1)
    i_col = jnp.sum(jnp.where(lane == h_idx, g, 0.0), axis=1, keepdims=True)
    f_col = jnp.sum(jnp.where(lane == M_HEADS + h_idx, g, 0.0), axis=1, keepdims=True)
    row = lax.broadcasted_iota(jnp.int32, (lc, 1), 0)
    valid = (c_idx * lc + row) >= PAD
    li = jnp.where(valid, i_col, NEG)
    lf = jnp.where(valid, jnp.minimum(f_col, 0.0) - jnp.log(1.0 + jnp.exp(-jnp.abs(f_col))), 0.0)
    r2 = lax.broadcasted_iota(jnp.int32, (lc, lc), 0)
    c2 = lax.broadcasted_iota(jnp.int32, (lc, lc), 1)
    eye = r2 == c2
    tril = r2 >= c2
    to_row = lambda col: jnp.sum(jnp.where(eye, col, 0.0), axis=0, keepdims=True)
    lf_row = to_row(lf)
    b_col = jnp.sum(jnp.where(tril, lf_row, 0.0), axis=1, keepdims=True)
    b_row = to_row(b_col)
    li_row = to_row(li)
    d_mat = jnp.where(tril, b_col - b_row + li_row, NEG)
    return dict(f_col=f_col, valid=valid, li=li, b_col=b_col, d_mat=d_mat, eye=eye, r2=r2, c2=c2, row=row,
                to_row=to_row)


def _mlstm_chunk(q, ks, v, gq, c_st, n_st, m_st, lc):
    b_col, d_mat = gq["b_col"], gq["d_mat"]
    m_inter = b_col + m_st
    m_row = jnp.maximum(m_inter, jnp.max(d_mat, axis=1, keepdims=True))
    w_intra = jnp.exp(d_mat - m_row)
    w_inter = jnp.exp(m_inter - m_row)
    qb, kb, vb, cb = _bf(q), _bf(ks), _bf(v), _bf(c_st)
    s = _dot(qb, kb, NT) * w_intra
    qc = _dot(qb, cb)
    num = _dot(_bf(s), vb) + w_inter * qc
    qn = jnp.sum(q * n_st, axis=1, keepdims=True)
    den = jnp.sum(s, axis=1, keepdims=True) + w_inter * qn
    e = jnp.exp(-m_row)
    nn = jnp.maximum(jnp.abs(den), e)
    b_last = b_col[lc - 1:lc, :]
    g_log = b_last - b_col + gq["li"]
    m_new = jnp.maximum(b_last + m_st, jnp.max(g_log, axis=0, keepdims=True))
    w_k = jnp.exp(g_log - m_new)
    decay = jnp.exp(b_last + m_st - m_new)
    return dict(w_intra=w_intra, w_inter=w_inter, qb=qb, kb=kb, vb=vb, cb=cb, s=s, qc=qc, num=num, qn=qn, den=den,
                e=e, nn=nn, m_new=m_new, w_k=w_k, decay=decay)


def _chunks_per_step(nc):
    return max(c for c in (3, 2, 1) if nc % c == 0)


def _mlstm_fwd(qk3, p3):
    bsz, lp, _ = p3.shape
    lc = M_CHUNK
    nc = lp // lc
    dk, dv = 128, 256
    scale = dk ** -0.5

    cps = _chunks_per_step(nc)
    rows = cps * lc

    def body(q_ref, k_ref, v_ref, g_ref, h_ref, cs_ref, ns_ref, ms_ref, c_sc, n_sc, m_sc):
        st = pl.program_id(1)

        @pl.when(st == 0)
        def _():
            c_sc[...] = jnp.zeros_like(c_sc)
            n_sc[...] = jnp.zeros_like(n_sc)
            m_sc[...] = jnp.zeros_like(m_sc)

        for j in range(cps):
            rs = slice(j * lc, (j + 1) * lc)
            g = g_ref[rs, :]
            for hh in range(M_HEADS):
                c_st, n_st, m_all = c_sc[hh], n_sc[hh], m_sc[hh]
                cs_ref[hh, j] = c_st
                ns_ref[hh, j] = n_st
                ms_ref[hh, j] = m_all
                m_st = m_all[:, 0:1]
                q = q_ref[rs, hh * dk:(hh + 1) * dk]
                ks = k_ref[rs, hh * dk:(hh + 1) * dk] * scale
                v = v_ref[rs, hh * dv:(hh + 1) * dv]
                gq = _mlstm_gates(g, hh, st * cps + j, lc)
                f = _mlstm_chunk(q, ks, v, gq, c_st, n_st, m_st, lc)
                h_ref[rs, hh * dv:(hh + 1) * dv] = f["num"] / f["nn"]
                kw = ks * f["w_k"]
                c_sc[hh] = f["decay"] * c_st + _dot(_bf(kw), f["vb"], TN)
                n_sc[hh] = f["decay"] * n_st + _colsum(kw)
                m_sc[hh] = jnp.broadcast_to(f["m_new"], (1, 128))

    sd = jax.ShapeDtypeStruct
    nh = M_HEADS
    return _pcall(
        body, name="mlstm_fwd", grid=(bsz, nc // cps),
        in_specs=[pl.BlockSpec((None, rows, nh * dk), lambda b, c: (b, c, 0)),
                  pl.BlockSpec((None, rows, nh * dk), lambda b, c: (b, c, 1)),
                  pl.BlockSpec((None, rows, nh * dv), lambda b, c: (b, c, V_OFF // (nh * dv))),
                  pl.BlockSpec((None, rows, 128), lambda b, c: (b, c, G_OFF // 128))],
        out_specs=[pl.BlockSpec((None, rows, nh * dv), lambda b, c: (b, c, 0)),
                   pl.BlockSpec((None, nh, cps, dk, dv), lambda b, c: (b, 0, c, 0, 0)),
                   pl.BlockSpec((None, nh, cps, 1, dk), lambda b, c: (b, 0, c, 0, 0)),
                   pl.BlockSpec((None, nh, cps, 1, 128), lambda b, c: (b, 0, c, 0, 0))],
        out_shape=[sd((bsz, lp, nh * dv), F32), sd((bsz, nh, nc, dk, dv), F32),
                   sd((bsz, nh, nc, 1, dk), F32), sd((bsz, nh, nc, 1, 128), F32)],
        scratch_shapes=[pltpu.VMEM((nh, dk, dv), F32), pltpu.VMEM((nh, 1, dk), F32), pltpu.VMEM((nh, 1, 128), F32)],
        compiler_params=_cp(("parallel", "arbitrary")),
    )(qk3, qk3, p3, p3)


def _mlstm_bwd(dp3, qk3, p3, dh3, cs, ns, ms):
    bsz, lp, _ = p3.shape
    lc = M_CHUNK
    nc = lp // lc
    dk, dv = 128, 256
    scale = dk ** -0.5

    cps = _chunks_per_step(nc)
    nst = nc // cps
    rows = cps * lc

    def body(dp_any, q_ref, k_ref, v_ref, g_ref, dh_ref, cs_ref, ns_ref, ms_ref,
             dv_ref, dqk_ref, dg_ref, dc_sc, dn_sc):
        t = pl.program_id(1)
        st = nst - 1 - t

        @pl.when(t == 0)
        def _():
            dc_sc[...] = jnp.zeros_like(dc_sc)
            dn_sc[...] = jnp.zeros_like(dn_sc)

        lane = lax.broadcasted_iota(jnp.int32, (lc, 128), 1)
        for j in reversed(range(cps)):
            rs = slice(j * lc, (j + 1) * lc)
            g = g_ref[rs, :]
            dgate = jnp.zeros((lc, 128), F32)
            for hh in range(M_HEADS):
                dgate = head(hh, j, rs, st * cps + j, g, lane, dgate, q_ref, k_ref, v_ref, dh_ref, cs_ref, ns_ref,
                             ms_ref, dv_ref, dqk_ref, dc_sc, dn_sc)
            dg_ref[rs, :] = dgate

    def head(hh, j, sl, c, g, lane, dgate, q_ref, k_ref, v_ref, dh_ref, cs_ref, ns_ref, ms_ref, dv_ref, dqk_ref,
             dc_sc, dn_sc):
        c_st, n_st = cs_ref[hh, j], ns_ref[hh, j]
        m_st = ms_ref[hh, j][:, 0:1]
        q = q_ref[sl, hh * dk:(hh + 1) * dk]
        ks = k_ref[sl, hh * dk:(hh + 1) * dk] * scale
        v = v_ref[sl, hh * dv:(hh + 1) * dv]
        dh = dh_ref[sl, hh * dv:(hh + 1) * dv]
        gq = _mlstm_gates(g, hh, c, lc)
        f = _mlstm_chunk(q, ks, v, gq, c_st, n_st, m_st, lc)
        eye, r2, c2, row, valid = gq["eye"], gq["r2"], gq["c2"], gq["row"], gq["valid"]
        w_intra, w_inter, s, nn, den = f["w_intra"], f["w_inter"], f["s"], f["nn"], f["den"]
        qb, kb, vb, cb, w_k, decay = f["qb"], f["kb"], f["vb"], f["cb"], f["w_k"], f["decay"]
        d_c, d_n = dc_sc[hh], dn_sc[hh]
        d_cb = _bf(d_c)

        hout = f["num"] / nn
        dnum = dh / nn
        d_nn = -jnp.sum(dh * hout, axis=1, keepdims=True) / nn
        dden = jnp.where(jnp.abs(den) > f["e"], d_nn * jnp.sign(den), 0.0)
        wdnum = w_inter * dnum
        wdden = w_inter * dden
        ds = _dot(_bf(dnum), vb, NT) + dden
        dsw = _bf(ds * w_intra)
        dq = _dot(dsw, kb) + _dot(_bf(wdnum), cb, NT) + wdden * n_st
        dkw = _dot(vb, d_cb, NT) + d_n
        dks = _dot(dsw, qb, TN) + dkw * w_k
        kw = ks * w_k
        dvv = _dot(_bf(s), _bf(dnum), TN) + _dot(_bf(kw), d_cb)
        dd = ds * s
        rs = jnp.sum(dd, axis=1, keepdims=True)
        cs_col = jnp.sum(jnp.where(eye, jnp.sum(dd, axis=0, keepdims=True), 0.0), axis=1, keepdims=True)
        dwi = jnp.sum(dnum * f["qc"], axis=1, keepdims=True) + dden * f["qn"]
        db = rs - cs_col + dwi * w_inter
        dli = cs_col
        ddecay = jnp.sum(jnp.sum(d_c * c_st, axis=1, keepdims=True), axis=0, keepdims=True) \
            + jnp.sum(d_n * n_st, axis=1, keepdims=True)
        dgl = jnp.sum(dkw * ks, axis=1, keepdims=True) * w_k
        dblast = ddecay * decay + jnp.sum(dgl, axis=0, keepdims=True)
        db = db - dgl + jnp.where(row == lc - 1, dblast, 0.0)
        dli = dli + dgl
        db_row = gq["to_row"](db)
        dlf = jnp.sum(jnp.where(c2 >= r2, db_row, 0.0), axis=1, keepdims=True)
        dlf = jnp.where(valid, dlf, 0.0)
        dgate = jnp.where(lane == hh, jnp.where(valid, dli, 0.0), dgate)
        dgate = jnp.where(lane == M_HEADS + hh, dlf / (1.0 + jnp.exp(gq["f_col"])), dgate)
        dqk_ref[sl, hh * dk:(hh + 1) * dk] = dq
        dqk_ref[sl, (M_HEADS + hh) * dk:(M_HEADS + hh + 1) * dk] = dks * scale
        dv_ref[sl, hh * dv:(hh + 1) * dv] = dvv
        dc_sc[hh] = decay * d_c + _dot(qb, _bf(wdnum), TN)
        dn_sc[hh] = decay * d_n + _colsum(q * wdden)
        return dgate

    sd = jax.ShapeDtypeStruct
    nh = M_HEADS
    rc = lambda c: nst - 1 - c
    return _pcall(
        body, name="mlstm_bwd", grid=(bsz, nst),
        in_specs=[pl.BlockSpec(memory_space=pl.ANY),
                  pl.BlockSpec((None, rows, nh * dk), lambda b, c: (b, rc(c), 0)),
                  pl.BlockSpec((None, rows, nh * dk), lambda b, c: (b, rc(c), 1)),
                  pl.BlockSpec((None, rows, nh * dv), lambda b, c: (b, rc(c), V_OFF // (nh * dv))),
                  pl.BlockSpec((None, rows, 128), lambda b, c: (b, rc(c), G_OFF // 128)),
                  pl.BlockSpec((None, rows, nh * dv), lambda b, c: (b, rc(c), 0)),
                  pl.BlockSpec((None, nh, cps, dk, dv), lambda b, c: (b, 0, rc(c), 0, 0)),
                  pl.BlockSpec((None, nh, cps, 1, dk), lambda b, c: (b, 0, rc(c), 0, 0)),
                  pl.BlockSpec((None, nh, cps, 1, 128), lambda b, c: (b, 0, rc(c), 0, 0))],
        out_specs=[pl.BlockSpec((None, rows, nh * dv), lambda b, c: (b, rc(c), V_OFF // (nh * dv))),
                   pl.BlockSpec((None, rows, 2 * nh * dk), lambda b, c: (b, rc(c), 0)),
                   pl.BlockSpec((None, rows, 128), lambda b, c: (b, rc(c), 0))],
        out_shape=[sd(dp3.shape, F32), sd((bsz, lp, 2 * nh * dk), F32), sd((bsz, lp, 128), F32)],
        scratch_shapes=[pltpu.VMEM((nh, dk, dv), F32), pltpu.VMEM((nh, 1, dk), F32)],
        input_output_aliases={0: 0},
        compiler_params=_cp(("arbitrary", "arbitrary")),
    )(dp3, qk3, qk3, p3, p3, dh3, cs, ns, ms)


def _headnorm(x):
    dv = x.shape[1] // M_HEADS
    xh, rs = [], []
    for h in range(M_HEADS):
        xx = x[:, h * dv:(h + 1) * dv]
        mu = jnp.mean(xx, axis=-1, keepdims=True)
        xc = xx - mu
        rstd = lax.rsqrt(jnp.mean(xc * xc, axis=-1, keepdims=True) + LN_EPS)
        xh.append(xc * rstd)
        rs.append(rstd)
    return jnp.concatenate(xh, axis=1), rs


def _mix_fwd(hm, p, ys5g, h0, gn, wmo_bf, wo_bf, g1, b1, lp):
    r, d = hm.shape
    tm = _row_tile(lp, 208)

    def body(hm_ref, o_ref, gs_ref, gm_ref, ys_ref, h0_ref, gn_ref, wmo_ref, wo_ref, g1_ref, b1_ref,
             ymin_ref, ym_ref, mix_ref, r1_ref, h1_ref):
        xhat, _ = _headnorm(hm_ref[...])
        ymin = _bf(_sig(o_ref[...]) * (xhat * gn_ref[...]))
        ymin_ref[...] = ymin
        ym = _dot(ymin, wmo_ref[...])
        ym_ref[...] = ym
        mix = _bf(_sig(gs_ref[...]) * ys_ref[...] + _sig(gm_ref[...]) * ym)
        mix_ref[...] = mix
        r1 = ALPHA * h0_ref[...] + _dot(mix, wo_ref[...])
        r1_ref[...] = r1
        h1, _, _ = _ln_fwd(r1, g1_ref[...], b1_ref[...])
        h1_ref[...] = h1

    sd = jax.ShapeDtypeStruct
    row = pl.BlockSpec((tm, d), lambda i: (i, 0))
    return _pcall(
        body, name="mix_fwd", grid=(r // tm,),
        in_specs=[row, pl.BlockSpec((tm, d), lambda i: (i, O_OFF // d)), pl.BlockSpec((tm, d), lambda i: (i, GS_OFF // d)),
                  pl.BlockSpec((tm, d), lambda i: (i, GM_OFF // d)), row, row, _const((1, d)),
                  _resident((d, d)), _resident((d, d)), _const((1, d)), _const((1, d))],
        out_specs=[row] * 5,
        out_shape=[sd((r, d), BF16), sd((r, d), F32), sd((r, d), BF16), sd((r, d), F32), sd((r, d), F32)],
        compiler_params=_cp(("parallel",), 48),
    )(hm, p, p, p, ys5g, h0, gn, wmo_bf, wo_bf, g1, b1)


def _mix_bwd(dh1, r1, g1, wo_bf, wmo_bf, p, ys5g, ym, hm, gn, lp):
    r, d = hm.shape
    tm = _row_tile(lp, 208)
    dv = d // M_HEADS

    def body(dh1_ref, r1_ref, g1_ref, wo_ref, wmo_ref, o_ref, gs_ref, gm_ref, ys_ref, ym_ref, hm_ref, gn_ref,
             dr1_ref, dp_ref, dys_ref, dym_ref, dhm_ref, dg1_ref, db1_ref, dgn_ref):
        i = pl.program_id(0)

        @pl.when(i == 0)
        def _():
            dg1_ref[...] = jnp.zeros_like(dg1_ref)
            db1_ref[...] = jnp.zeros_like(db1_ref)
            dgn_ref[...] = jnp.zeros_like(dgn_ref)

        dh1 = dh1_ref[...]
        _, xhat1, rstd1 = _ln_fwd(r1_ref[...], g1_ref[...], 0.0)
        dr1 = _ln_bwd(dh1, xhat1, rstd1, g1_ref[...])
        dr1_ref[...] = dr1
        dg1_ref[...] += _colsum(dh1 * xhat1)
        db1_ref[...] += _colsum(dh1)
        dmix = _dot(_bf(dr1), wo_ref[...], NT)
        sgs, sgm, so = _sig(gs_ref[...]), _sig(gm_ref[...]), _sig(o_ref[...])
        dys_ref[...] = dmix * sgs
        dp_ref[:, d:2 * d] = dmix * ys_ref[...] * sgs * (1.0 - sgs)
        dym = dmix * sgm
        dym_ref[...] = _bf(dym)
        dp_ref[:, 2 * d:3 * d] = dmix * ym_ref[...] * sgm * (1.0 - sgm)
        dymin = _dot(_bf(dym), wmo_ref[...], NT)
        xhat, rs = _headnorm(hm_ref[...])
        gn_ = gn_ref[...]
        dp_ref[:, 0:d] = dymin * (xhat * gn_) * so * (1.0 - so)
        dhn = dymin * so
        dgn_ref[...] += _colsum(dhn * xhat)
        dxh = dhn * gn_
        for h in range(M_HEADS):
            sl = slice(h * dv, (h + 1) * dv)
            a, xh = dxh[:, sl], xhat[:, sl]
            m1 = jnp.mean(a, axis=-1, keepdims=True)
            m2 = jnp.mean(a * xh, axis=-1, keepdims=True)
            dhm_ref[:, sl] = rs[h] * (a - m1 - xh * m2)

    sd = jax.ShapeDtypeStruct
    row = pl.BlockSpec((tm, d), lambda i: (i, 0))
    vec = _const((1, d))
    return _pcall(
        body, name="mix_bwd", grid=(r // tm,),
        in_specs=[row, row, vec, _resident((d, d)), _resident((d, d)),
                  pl.BlockSpec((tm, d), lambda i: (i, O_OFF // d)), pl.BlockSpec((tm, d), lambda i: (i, GS_OFF // d)),
                  pl.BlockSpec((tm, d), lambda i: (i, GM_OFF // d)), row, row, row, vec],
        out_specs=[row, pl.BlockSpec((tm, 3 * d), lambda i: (i, 0)), row, row, row, vec, vec, vec],
        out_shape=[sd((r, d), F32), sd((r, NP), F32), sd((r, d), F32), sd((r, d), BF16), sd((r, d), F32),
                   sd((1, d), F32), sd((1, d), F32), sd((1, d), F32)],
        compiler_params=_cp(("arbitrary",), 48),
    )(dh1, r1, g1, wo_bf, wmo_bf, p, p, p, ys5g, ym, hm, gn)


def _mlp_fwd(h1, tgt, wup_g, wdn_bf, bup, g2, b2, lp):
    r, d = h1.shape
    tm = _row_tile(lp, 352)
    tps = lp // tm
    nf = wup_g.shape[0]

    def body(h1_ref, t_ref, wup_ref, wdn_ref, bup_ref, g2_ref, b2_ref, dr2_ref, act_ref, loss_ref, dg2_ref, db2_ref):
        i = pl.program_id(0)

        @pl.when(i == 0)
        def _():
            loss_ref[...] = jnp.zeros_like(loss_ref)
            dg2_ref[...] = jnp.zeros_like(dg2_ref)
            db2_ref[...] = jnp.zeros_like(db2_ref)

        h1 = h1_ref[...]
        h1b = _bf(h1)
        ff = jnp.zeros((tm, d), F32)
        for s in range(nf):
            up = _dot(h1b, wup_ref[s]) + bup_ref[:, s * d:(s + 1) * d]
            a = jnp.maximum(up, 0.0)
            a = _bf(a * a)
            act_ref[:, s * d:(s + 1) * d] = a
            ff = ff + _dot(a, wdn_ref[s * d:(s + 1) * d, :])
        r2 = ALPHA * h1 + ff
        g2 = g2_ref[...]
        y, xhat, rstd = _ln_fwd(r2, g2, b2_ref[...])
        t = (i % tps) * tm + lax.broadcasted_iota(jnp.int32, (tm, 1), 0)
        diff = jnp.where(t >= PAD + N_META, y - t_ref[...], 0.0)
        loss_ref[...] += 0.5 / d * jnp.sum(jnp.sum(diff * diff, axis=1, keepdims=True), axis=0, keepdims=True)
        dy = diff * (1.0 / d)
        dg2_ref[...] += _colsum(dy * xhat)
        db2_ref[...] += _colsum(dy)
        dr2_ref[...] = _ln_bwd(dy, xhat, rstd, g2)

    sd = jax.ShapeDtypeStruct
    row = pl.BlockSpec((tm, d), lambda i: (i, 0))
    vec = _const((1, d))
    return _pcall(
        body, name="mlp_fwd", grid=(r // tm,),
        in_specs=[row, row, _resident(wup_g.shape), _resident(wdn_bf.shape), _const((1, nf * d)), vec, vec],
        out_specs=[row, pl.BlockSpec((tm, nf * d), lambda i: (i, 0)), _const((1, 128)), vec, vec],
        out_shape=[sd((r, d), F32), sd((r, nf * d), BF16), sd((1, 128), F32), sd((1, d), F32), sd((1, d), F32)],
        compiler_params=_cp(("arbitrary",), 56),
    )(h1, tgt, wup_g, wdn_bf, bup, g2, b2)


def _mlp_bwd(h1, dr2, wup_g, wdn_bf, bup, lp):
    r, d = h1.shape
    tm = _row_tile(lp, 352)
    nf = wup_g.shape[0]

    def body(h1_ref, dr2_ref, wup_ref, wdn_ref, bup_ref, dh1_ref, dup_ref, dbup_ref):
        i = pl.program_id(0)

        @pl.when(i == 0)
        def _():
            dbup_ref[...] = jnp.zeros_like(dbup_ref)

        h1b = _bf(h1_ref[...])
        dr2 = dr2_ref[...]
        dr2b = _bf(dr2)
        acc = ALPHA * dr2
        for s in range(nf):
            up = _dot(h1b, wup_ref[s]) + bup_ref[:, s * d:(s + 1) * d]
            dact = _dot(dr2b, wdn_ref[s * d:(s + 1) * d, :], NT)
            dup = dact * (2.0 * jnp.maximum(up, 0.0))
            dbup_ref[:, s * d:(s + 1) * d] += _colsum(dup)
            dupb = _bf(dup)
            dup_ref[:, s * d:(s + 1) * d] = dupb
            acc = acc + _dot(dupb, wup_ref[s], NT)
        dh1_ref[...] = acc

    sd = jax.ShapeDtypeStruct
    row = pl.BlockSpec((tm, d), lambda i: (i, 0))
    return _pcall(
        body, name="mlp_bwd", grid=(r // tm,),
        in_specs=[row, row, _resident(wup_g.shape), _resident(wdn_bf.shape), _const((1, nf * d))],
        out_specs=[row, pl.BlockSpec((tm, nf * d), lambda i: (i, 0)), _const((1, nf * d))],
        out_shape=[sd((r, d), F32), sd((r, nf * d), BF16), sd((1, nf * d), F32)],
        compiler_params=_cp(("arbitrary",), 56),
    )(h1, dr2, wup_g, wdn_bf, bup)


def _s5_block_mats(bb_re_t, bb_im_t, c_re, c_im, ap_re, ap_im):
    ng = c_re.shape[0]
    gl = ng // S5_KCH
    eye = jnp.eye(gl, dtype=F32)

    def bmat(bt):
        bb = jnp.transpose(bt, (1, 0, 2)).reshape(S5_KCH, gl, S5_GROUP, S5_STATE)
        return jnp.einsum("kghp,gj->kghjp", bb, eye).reshape(S5_KCH, gl * S5_GROUP, gl * S5_STATE)

    def cmat(c):
        cc = c.reshape(S5_KCH, gl, S5_GROUP, S5_STATE)
        return jnp.einsum("kghp,gj->kjpgh", cc, eye).reshape(S5_KCH, gl * S5_STATE, gl * S5_GROUP)

    def pw(a):
        return jnp.transpose(a.reshape(8, S5_KCH, gl * S5_STATE), (1, 0, 2))

    bk = jnp.concatenate([bmat(bb_re_t), bmat(bb_im_t)], axis=-1)
    apow = jnp.concatenate([pw(ap_re), pw(ap_im)], axis=-1)
    return _bf(bk), _bf(cmat(c_re)), _bf(cmat(c_im)), apow


def _s5_block_grads(dbk, dcre, dcim, da):
    gl = dbk.shape[1] // S5_GROUP
    ng = gl * S5_KCH
    eye = jnp.eye(gl, dtype=F32)
    hw = gl * S5_STATE

    def bpart(x):
        x = x.reshape(S5_KCH, gl, S5_GROUP, gl, S5_STATE)
        x = jnp.einsum("kghjp,gj->kghp", x, eye).reshape(ng, S5_GROUP, S5_STATE)
        return jnp.transpose(x, (1, 0, 2))

    def cpart(x):
        x = x.reshape(S5_KCH, gl, S5_STATE, gl, S5_GROUP)
        return jnp.einsum("kjpgh,gj->kghp", x, eye).reshape(ng, S5_GROUP, S5_STATE)

    return (bpart(dbk[..., :hw]), bpart(dbk[..., hw:]), cpart(dcre), cpart(dcim),
            da[:, 0, :hw].reshape(ng, S5_STATE), da[:, 0, hw:].reshape(ng, S5_STATE))


def _tie(a, tok):
    return a if tok is None else a + tok[0, 0]


def _local_step(x, tgt, w, early=None, late=None, ready=None):
    ready = ready or (lambda names, g: None)
    bsz, seq, d = x.shape
    lp = PAD + N_META + seq
    r = bsz * lp
    meta = jnp.broadcast_to(w["meta_tokens"][None], (bsz, N_META, d))
    hin = jnp.concatenate([jnp.zeros((bsz, PAD, d), F32), meta, x], axis=1).reshape(r, d)
    tgtp = jnp.concatenate([jnp.zeros((bsz, PAD + N_META, d), F32), tgt], axis=1).reshape(r, d)

    h0 = _ln0_fwd(hin, w["ln0_g"], w["ln0_b"], lp)
    if early is not None:
        w = {**w, **early(h0)}
    p = _inproj(h0, w["w_in"], w["b_in"], lp)
    p3 = p.reshape(bsz, lp, NP)

    b_re_t = jnp.transpose(w["s5_b_re"], (2, 0, 1))
    b_im_t = jnp.transpose(w["s5_b_im"], (2, 0, 1))
    ap_re, ap_im, bb_re_t, bb_im_t = _s5_prep(w["s5_lambda_re"], w["s5_lambda_im"], w["s5_log_dt"], b_re_t, b_im_t)
    bk, cre, cim, apow = _s5_block_mats(bb_re_t, bb_im_t, w["s5_c_re"], w["s5_c_im"], ap_re, ap_im)
    y_s5, xs = _s5_fwd(p3, bk, cre, cim, apow, w["s5_d"])
    sw = y_s5.shape[-1]
    if late is not None:
        w = {**w, **late(y_s5)}
    gy, z, ys5g = _glu_fwd(y_s5.reshape(r, sw), w["s5_w_glu"], lp)

    pre3, qk3 = _conv_fwd(p3, w["qk_conv_w"], w["qk_conv_b"])
    hm3, cs, ns, ms = _mlstm_fwd(qk3, p3)
    hm = hm3.reshape(r, d)
    ymin, ym, mix, r1, h1 = _mix_fwd(hm, p, ys5g, h0, w["m_norm_g"], w["m_w_out"], w["w_o"], w["ln1_g"], w["ln1_b"], lp)
    dr2, act, loss, dg2, db2 = _mlp_fwd(h1, tgtp, w["w_up"], w["w_down"], w["b_up"], w["ln2_g"], w["ln2_b"], lp)

    g = {"ln2_g": dg2, "ln2_b": db2}
    dh1, dup, g["b_up"] = _mlp_bwd(h1, dr2, w["w_up"], w["w_down"], w["b_up"], lp)
    g["w_down"] = _mm_tn(act, dr2, name="dw_down")
    g["w_up"] = _mm_tn(h1, dup, name="dw_up", split=w["w_up"].shape[0])
    tok = ready(("w_down", "w_up"), g)
    dr1, dp, dys5g, dym, dhm, g["ln1_g"], g["ln1_b"], g["m_norm_g"] = _mix_bwd(
        dh1, r1, _tie(w["ln1_g"], tok), w["w_o"], w["m_w_out"], p, ys5g, ym, hm, w["m_norm_g"], lp)
    g["w_o"] = _mm_tn(mix, dr1, name="dw_o")
    g["m_w_out"] = _mm_tn(ymin, dym, name="dw_mout")

    dp3 = dp.reshape(bsz, lp, NP)
    dp3, dqk3, dgate = _mlstm_bwd(dp3, qk3, p3, dhm.reshape(bsz, lp, d), cs, ns, ms)
    dp3, g["qk_conv_w"], g["qk_conv_b"] = _conv_bwd(dp3, p3, dqk3, pre3, w["qk_conv_w"])
    dz, dys5 = _glu_bwd(dys5g, z, y_s5.reshape(r, sw), w["s5_w_glu"], lp)
    g["s5_w_glu"] = _mm_tn(gy, dz, name="dw_glu", split=w["s5_w_glu"].shape[0])
    tok = ready(("s5_w_glu", "m_w_out", "w_o"), g)
    apow_rev = jnp.flip(apow, axis=1)
    dp3, dbk, dcre, dcim, da, g["s5_d"] = _s5_bwd(dp3, p3, dys5.reshape(bsz, lp, sw), xs, bk, cre, cim, apow_rev,
                                                 _tie(w["s5_d"], tok))
    dbb_re_t, dbb_im_t, g["s5_c_re"], g["s5_c_im"], da_re, da_im = _s5_block_grads(dbk, dcre, dcim, da)
    g["s5_lambda_re"], g["s5_lambda_im"], g["s5_log_dt"], gb_re_t, gb_im_t = _s5_prep_bwd(
        w["s5_lambda_re"], w["s5_lambda_im"], w["s5_log_dt"], b_re_t, b_im_t, da_re, da_im, dbb_re_t, dbb_im_t)
    g["s5_b_re"] = jnp.transpose(gb_re_t, (1, 2, 0))
    g["s5_b_im"] = jnp.transpose(gb_im_t, (1, 2, 0))

    dp3 = lax.dynamic_update_slice(dp3, dgate, (0, 0, G_OFF))
    dp = dp3.reshape(r, NP)
    g["w_in"], g["b_in"] = _mm_tn(h0, dp, name="dw_in", colsum=True)
    tok = ready(("w_in",), g)
    dpw = _mm_nt(dp, w["w_in"], lp, name="dh0", dep=tok)
    dhin, g["ln0_g"], g["ln0_b"], g["meta_tokens"] = _ln0_bwd(hin, dr1, dpw, w["ln0_g"], lp)
    grad_x = dhin.reshape(bsz, lp, d)[:, PAD + N_META:]
    return loss, grad_x, g


_ANY = pl.BlockSpec(memory_space=pl.ANY)
_MESH = pl.DeviceIdType.MESH


def _place():
    return lax.axis_index("x"), lax.axis_index("y"), lax.axis_index("c")


def _gather_chips(shards):
    n = len(shards)

    def body(*refs):
        ins, outs = refs[:n], refs[n:2 * n]
        send, recv, loc = refs[2 * n:]
        x, y, c = _place()
        me = 2 * x + y
        peers = [(1 - x, y), (x, 1 - y), (1 - x, 1 - y)]

        def rc(a, k, slot):
            px, py = peers[k]
            return pltpu.make_async_remote_copy(src_ref=ins[a], dst_ref=outs[a].at[slot], send_sem=send.at[a, k],
                                                recv_sem=recv.at[a, k], device_id=(px, py, c), device_id_type=_MESH)

        own = [pltpu.make_async_copy(ins[a], outs[a].at[me], loc.at[a]) for a in range(n)]
        for cp in own:
            cp.start()
        out = [rc(a, k, me) for a in range(n) for k in range(3)]
        for cp in out:
            cp.start()
        for a in range(n):
            for k in range(3):
                rc(a, k, 2 * peers[k][0] + peers[k][1]).wait_recv()
        for cp in out:
            cp.wait_send()
        for cp in own:
            cp.wait()

    return _pcall(
        body, name="gather_chips", in_specs=[_ANY] * n, out_specs=[_ANY] * n,
        out_shape=[jax.ShapeDtypeStruct((4,) + s.shape, s.dtype) for s in shards],
        scratch_shapes=[pltpu.SemaphoreType.DMA((n, 3)), pltpu.SemaphoreType.DMA((n, 3)), pltpu.SemaphoreType.DMA((n,))],
    )(*shards)


_HBM = pl.BlockSpec(memory_space=pltpu.HBM)
_SEM = pl.BlockSpec(memory_space=pltpu.SEMAPHORE)
_EFFECT = pltpu.SideEffectType.DATAFLOW_SIDE_EFFECTING


def _xchg_copies(srcs, lands, send, recv, scatter):
    x, y, c = _place()
    me = 2 * x + y
    peers = [(1 - x, y), (x, 1 - y), (1 - x, 1 - y)]
    out = []
    for a in range(len(srcs)):
        for k, (px, py) in enumerate(peers):
            src = srcs[a].at[2 * px + py] if scatter else srcs[a]
            dst = lands[a].at[k] if scatter else lands[a].at[me]
            out.append(pltpu.make_async_remote_copy(src_ref=src, dst_ref=dst, send_sem=send.at[3 * a + k],
                                                    recv_sem=recv.at[3 * a + k], device_id=(px, py, c),
                                                    device_id_type=_MESH))
    return out


def _xchg_start(srcs, lands, *, name, scatter, dep=None):
    n = len(srcs)
    deps = [] if dep is None else [dep]
    nd = len(deps)

    def body(*refs):
        send, recv = refs[2 * n + nd], refs[2 * n + nd + 1]
        for cp in _xchg_copies(refs[:n], refs[n:2 * n], send, recv, scatter):
            cp.start()
        refs[-1][...] = jnp.zeros_like(refs[-1])

    hbm = lambda a: pltpu.HBM(a.shape, a.dtype)
    con = lambda a: pltpu.with_memory_space_constraint(a, pltpu.HBM)
    res = _pcall(
        body, name=name, in_specs=[_HBM] * (2 * n) + [_ANY] * nd,
        out_specs=[_SEM, _SEM] + [_HBM] * (2 * n) + [pl.BlockSpec(memory_space=pltpu.VMEM)],
        out_shape=[pltpu.SemaphoreType.DMA((3 * n,)), pltpu.SemaphoreType.DMA((3 * n,))]
        + [hbm(a) for a in srcs] + [hbm(a) for a in lands] + [jax.ShapeDtypeStruct((8, 128), F32)],
        input_output_aliases={i: 2 + i for i in range(2 * n)},
        compiler_params=pltpu.CompilerParams(has_side_effects=_EFFECT),
    )(*[con(a) for a in srcs], *[con(a) for a in lands], *deps)
    return res[0], res[1], list(res[2:2 + n]), list(res[2 + n:2 + 2 * n]), res[-1]


def _xchg_wait(send, recv, srcs, lands, after, *, name, scatter):
    n = len(srcs)

    def body(*refs):
        s_ref, r_ref = refs[2 * n], refs[2 * n + 1]
        for cp in _xchg_copies(refs[:n], refs[n:2 * n], s_ref, r_ref, scatter):
            cp.wait_send()
            cp.wait_recv()

    hbm = lambda a: pltpu.HBM(a.shape, a.dtype)
    res = _pcall(
        body, name=name, in_specs=[_HBM] * (2 * n) + [_SEM, _SEM, _ANY],
        out_specs=[_HBM] * (2 * n),
        out_shape=[hbm(a) for a in srcs] + [hbm(a) for a in lands],
        input_output_aliases={i: i for i in range(2 * n)},
        compiler_params=pltpu.CompilerParams(has_side_effects=_EFFECT),
    )(*srcs, *lands, send, recv, after)
    return list(res[:n]), list(res[n:])


def _swap_cores(arrs, name="swap_cores"):
    n = len(arrs)

    def body(*refs):
        ins, outs = refs[:n], refs[n:2 * n]
        send, recv = refs[2 * n:]
        x, y, c = _place()
        cps = [pltpu.make_async_remote_copy(src_ref=ins[a], dst_ref=outs[a], send_sem=send.at[a], recv_sem=recv.at[a],
                                            device_id=(x, y, 1 - c), device_id_type=_MESH) for a in range(n)]
        for cp in cps:
            cp.start()
        for cp in cps:
            cp.wait_recv()
        for cp in cps:
            cp.wait_send()

    return _pcall(
        body, name=name, in_specs=[_ANY] * n, out_specs=[_ANY] * n,
        out_shape=[jax.ShapeDtypeStruct(s.shape, s.dtype) for s in arrs],
        scratch_shapes=[pltpu.SemaphoreType.DMA((n,)), pltpu.SemaphoreType.DMA((n,))],
    )(*arrs)


def _allreduce_small(v):
    rows = v.shape[0]
    half = rows // 2
    assert half % 8 == 0 and 2 * half == rows

    def body(v_ref, out_ref, sib_ref, pair_ref, slots_ref, send, recv):
        x, y, c = _place()
        chip = 2 * x + y
        sibling = (x, y, 1 - c)
        peers = [(1 - x, y), (x, 1 - y), (1 - x, 1 - y)]
        mine = pl.ds(pl.multiple_of(c * half, 8), half)

        first = pltpu.make_async_remote_copy(src_ref=v_ref, dst_ref=sib_ref, send_sem=send.at[0], recv_sem=recv.at[0],
                                             device_id=sibling, device_id_type=_MESH)
        first.start()
        first.wait_recv()
        pair_ref[...] = v_ref[...] + sib_ref[...]
        slots_ref[chip] = pair_ref[mine, :]
        cross = [pltpu.make_async_remote_copy(src_ref=pair_ref.at[mine], dst_ref=slots_ref.at[chip],
                                              send_sem=send.at[1 + k], recv_sem=recv.at[1 + k],
                                              device_id=(px, py, c), device_id_type=_MESH)
                 for k, (px, py) in enumerate(peers)]
        for cp in cross:
            cp.start()
        for cp in cross:
            cp.wait_recv()
        out_ref[mine, :] = ((slots_ref[0] + slots_ref[1]) + slots_ref[2]) + slots_ref[3]
        last = pltpu.make_async_remote_copy(src_ref=out_ref.at[mine], dst_ref=out_ref.at[mine], send_sem=send.at[4],
                                            recv_sem=recv.at[4], device_id=sibling, device_id_type=_MESH)
        last.start()
        last.wait_recv()
        first.wait_send()
        for cp in cross:
            cp.wait_send()
        last.wait_send()

    vm = pl.BlockSpec(memory_space=pltpu.VMEM)
    return _pcall(
        body, name="allreduce_small", in_specs=[vm], out_specs=vm,
        out_shape=jax.ShapeDtypeStruct((rows, 128), F32),
        scratch_shapes=[pltpu.VMEM((rows, 128), F32), pltpu.VMEM((rows, 128), F32), pltpu.VMEM((4, half, 128), F32),
                        pltpu.SemaphoreType.DMA((5,)), pltpu.SemaphoreType.DMA((5,))],
        compiler_params=_cp(None, 40),
    )(v)


def _sum_slots(own, land):
    ns, rows, cols = land.shape
    tm = _row_tile(rows, 256, 8)

    def body(own_ref, a_ref, o_ref):
        o_ref[...] = ((own_ref[...] + a_ref[0]) + a_ref[1]) + a_ref[2]

    return _pcall(
        body, name="sum_slots", grid=(rows // tm,),
        in_specs=[pl.BlockSpec((tm, cols), lambda i: (i, 0)), pl.BlockSpec((ns, tm, cols), lambda i: (0, i, 0))],
        out_specs=pl.BlockSpec((tm, cols), lambda i: (i, 0)),
        out_shape=jax.ShapeDtypeStruct((rows, cols), F32),
        compiler_params=_cp(("parallel",), 40),
    )(own, land)


def _adamw(w, m, v, g0, g1=None):
    rows, cols = w.shape[-2:]
    lead = w.ndim == 3
    tm = _row_tile(rows, 256, 8)
    c1 = 1.0 - ADAM_B1 ** ADAM_STEP
    c2 = 1.0 - ADAM_B2 ** ADAM_STEP
    two = g1 is not None

    def body(*refs):
        w_ref, m_ref, v_ref, g0_ref = refs[:4]
        g_ref, d_ref, nm_ref, nv_ref = refs[-4:]
        g = g0_ref[...]
        if two:
            g = g + refs[4][...]
        nm = ADAM_B1 * m_ref[...] + (1.0 - ADAM_B1) * g
        nv = ADAM_B2 * v_ref[...] + (1.0 - ADAM_B2) * (g * g)
        g_ref[...] = g
        nm_ref[...] = nm
        nv_ref[...] = nv
        d_ref[...] = -ADAM_LR * ((nm / c1) / (jnp.sqrt(nv / c2) + ADAM_EPS) + ADAM_WD * w_ref[...])

    blk = pl.BlockSpec((tm, cols), lambda i: (i, 0))
    wblk = pl.BlockSpec((None, tm, cols), lambda i: (0, i, 0)) if lead else blk
    ins = [w, m, v, g0] + ([g1] if two else [])
    return _pcall(
        body, name="adamw", grid=(rows // tm,), in_specs=[wblk] * 3 + [blk] * (len(ins) - 3), out_specs=[wblk] * 4,
        out_shape=[jax.ShapeDtypeStruct(w.shape, F32)] * 4,
        compiler_params=_cp(("parallel",), 40),
    )(*ins)


_BIG = ("w_in", "s5_w_glu", "m_w_out", "w_o", "w_up", "w_down")
_SMALL = ("ln0_g", "ln0_b", "b_in", "qk_conv_b", "s5_lambda_re", "s5_lambda_im", "s5_log_dt", "s5_b_re", "s5_b_im",
          "s5_c_re", "s5_c_im", "s5_d", "m_norm_g", "ln1_g", "ln1_b", "b_up", "ln2_g", "ln2_b")
_SMALL_SHARDED = ("meta_tokens", "qk_conv_w")
_ORDER = ("meta_tokens", "ln0_g", "ln0_b", "w_in", "b_in", "qk_conv_w", "qk_conv_b", "s5_lambda_re", "s5_lambda_im",
          "s5_log_dt", "s5_b_re", "s5_b_im", "s5_c_re", "s5_c_im", "s5_d", "s5_w_glu", "m_norm_g", "m_w_out", "w_o",
          "ln1_g", "ln1_b", "w_up", "b_up", "w_down", "ln2_g", "ln2_b")


def _pack(arrs):
    flat = jnp.concatenate([a.reshape(-1) for a in arrs])
    n = flat.shape[0]
    rows = -(-n // 2048) * 16
    return jnp.pad(flat, (0, rows * 128 - n)).reshape(rows, 128)


def _unpack(packed, shapes):
    flat = packed.reshape(-1)
    out, off = [], 0
    for s in shapes:
        n = math.prod(s)
        out.append(flat[off:off + n].reshape(s))
        off += n
    return out


def kernel(x, meta_tokens, ln0_g, ln0_b, w_in, b_in, qk_conv_w, qk_conv_b, s5_lambda_re, s5_lambda_im, s5_log_dt, s5_b_re, s5_b_im, s5_c_re, s5_c_im, s5_d, s5_w_glu, m_norm_g, m_w_out, w_o, ln1_g, ln1_b, w_up, b_up, w_down, ln2_g, ln2_b, loss_target, m_meta_tokens, m_ln0_g, m_ln0_b, m_w_in, m_b_in, m_qk_conv_w, m_qk_conv_b, m_s5_lambda_re, m_s5_lambda_im, m_s5_log_dt, m_s5_b_re, m_s5_b_im, m_s5_c_re, m_s5_c_im, m_s5_d, m_s5_w_glu, m_m_norm_g, m_m_w_out, m_w_o, m_ln1_g, m_ln1_b, m_w_up, m_b_up, m_w_down, m_ln2_g, m_ln2_b, v_meta_tokens, v_ln0_g, v_ln0_b, v_w_in, v_b_in, v_qk_conv_w, v_qk_conv_b, v_s5_lambda_re, v_s5_lambda_im, v_s5_log_dt, v_s5_b_re, v_s5_b_im, v_s5_c_re, v_s5_c_im, v_s5_d, v_s5_w_glu, v_m_norm_g, v_m_w_out, v_w_o, v_ln1_g, v_ln1_b, v_w_up, v_b_up, v_w_down, v_ln2_g, v_ln2_b):
    wts = dict(meta_tokens=meta_tokens, ln0_g=ln0_g, ln0_b=ln0_b, w_in=w_in, b_in=b_in, qk_conv_w=qk_conv_w,
               qk_conv_b=qk_conv_b, s5_lambda_re=s5_lambda_re, s5_lambda_im=s5_lambda_im, s5_log_dt=s5_log_dt,
               s5_b_re=s5_b_re, s5_b_im=s5_b_im, s5_c_re=s5_c_re, s5_c_im=s5_c_im, s5_d=s5_d, s5_w_glu=s5_w_glu,
               m_norm_g=m_norm_g, m_w_out=m_w_out, w_o=w_o, ln1_g=ln1_g, ln1_b=ln1_b, w_up=w_up, b_up=b_up,
               w_down=w_down, ln2_g=ln2_g, ln2_b=ln2_b)
    mom = dict(meta_tokens=m_meta_tokens, ln0_g=m_ln0_g, ln0_b=m_ln0_b, w_in=m_w_in, b_in=m_b_in, qk_conv_w=m_qk_conv_w,
               qk_conv_b=m_qk_conv_b, s5_lambda_re=m_s5_lambda_re, s5_lambda_im=m_s5_lambda_im, s5_log_dt=m_s5_log_dt,
               s5_b_re=m_s5_b_re, s5_b_im=m_s5_b_im, s5_c_re=m_s5_c_re, s5_c_im=m_s5_c_im, s5_d=m_s5_d,
               s5_w_glu=m_s5_w_glu, m_norm_g=m_m_norm_g, m_w_out=m_m_w_out, w_o=m_w_o, ln1_g=m_ln1_g, ln1_b=m_ln1_b,
               w_up=m_w_up, b_up=m_b_up, w_down=m_w_down, ln2_g=m_ln2_g, ln2_b=m_ln2_b)
    var = dict(meta_tokens=v_meta_tokens, ln0_g=v_ln0_g, ln0_b=v_ln0_b, w_in=v_w_in, b_in=v_b_in, qk_conv_w=v_qk_conv_w,
               qk_conv_b=v_qk_conv_b, s5_lambda_re=v_s5_lambda_re, s5_lambda_im=v_s5_lambda_im, s5_log_dt=v_s5_log_dt,
               s5_b_re=v_s5_b_re, s5_b_im=v_s5_b_im, s5_c_re=v_s5_c_re, s5_c_im=v_s5_c_im, s5_d=v_s5_d,
               s5_w_glu=v_s5_w_glu, m_norm_g=v_m_norm_g, m_w_out=v_m_w_out, w_o=v_w_o, ln1_g=v_ln1_g, ln1_b=v_ln1_b,
               w_up=v_w_up, b_up=v_b_up, w_down=v_w_down, ln2_g=v_ln2_g, ln2_b=v_ln2_b)
    d = x.shape[-1]
    chip = 2 * lax.axis_index("x") + lax.axis_index("y")

    gw = dict(zip(_SMALL_SHARDED, _gather_chips([meta_tokens, qk_conv_w[0]])))
    own_w_in = _bf(w_in[0])
    fsend, frecv, fsrc, fland, ftok = _xchg_start([own_w_in], [lax.empty((4,) + own_w_in.shape, BF16)],
                                                  name="gather_w_in_start", scatter=False, dep=gw["qk_conv_w"])
    late_names = tuple(n for n in _BIG if n != "w_in")
    cat = lambda a: jnp.transpose(a, (1, 0, 2)).reshape(a.shape[1], 4 * a.shape[2])
    w = dict(
        meta_tokens=cat(gw["meta_tokens"]), ln0_g=ln0_g[None], ln0_b=_tie(ln0_b[None], ftok),
        qk_conv_w=cat(gw["qk_conv_w"]), qk_conv_b=qk_conv_b,
        s5_lambda_re=s5_lambda_re[0], s5_lambda_im=s5_lambda_im[0], s5_log_dt=s5_log_dt[0][:, None],
        s5_b_re=s5_b_re[0], s5_b_im=s5_b_im[0], s5_c_re=s5_c_re[0], s5_c_im=s5_c_im[0], s5_d=s5_d,
        m_norm_g=m_norm_g, ln1_g=ln1_g, ln1_b=ln1_b, b_up=b_up, ln2_g=ln2_g, ln2_b=ln2_b)
    in_flight = {}

    def place_own(src, land):
        return lax.dynamic_update_slice(land, src[None], (chip,) + (0,) * src.ndim)

    def early(after):
        src, land = _xchg_wait(fsend, frecv, fsrc, fland, after, name="gather_w_in_wait", scatter=False)
        late_src = [_bf(wts[n][0]) for n in late_names]
        st = _xchg_start(late_src, [lax.empty((4,) + a.shape, a.dtype) for a in late_src], name="gather_late_start",
                         scatter=False, dep=src[0])
        in_flight["late"] = st[:4]
        return dict(w_in=_w_in_from_slots(place_own(src[0], land[0])), b_in=_tie(_to_pad_cols(b_in), st[4]))

    def late(after):
        src, land = _xchg_wait(*in_flight["late"], after, name="gather_late_wait", scatter=False)
        full = {n: place_own(s, ld) for n, s, ld in zip(late_names, src, land)}
        return dict(s5_w_glu=full["s5_w_glu"], m_w_out=full["m_w_out"].reshape(d, d), w_o=full["w_o"].reshape(d, d),
                    w_up=full["w_up"], w_down=full["w_down"].reshape(4 * d, d))

    flying = []

    def ready(names, g):
        parts = dict(
            w_in=lambda: _slots_from_w_in(g["w_in"][0]), s5_w_glu=lambda: g["s5_w_glu"],
            m_w_out=lambda: g["m_w_out"].reshape(4, d // 4, d), w_o=lambda: g["w_o"].reshape(4, d // 4, d),
            w_up=lambda: g["w_up"], w_down=lambda: g["w_down"].reshape(4, d, d))
        src = [parts[n]() for n in names]
        land = [lax.empty((3,) + a.shape[1:], a.dtype) for a in src]
        st = _xchg_start(src, land, name="scatter_" + names[0] + "_start", scatter=True)
        flying.append((names,) + st[:4])
        return st[4]

    loss, grad_x, g = _local_step(x, loss_target, w, early, late, ready)
    g["b_in"] = _from_pad_cols(g["b_in"])

    res = {}

    def finish(groups, after, tag):
        mine = {}
        for names, send, recv, src, land in groups:
            src, land = _xchg_wait(send, recv, src, land, after, name="scatter_" + names[0] + "_wait", scatter=True)
            for n, s, ld in zip(names, src, land):
                mine[n] = _sum_slots(lax.dynamic_index_in_dim(s, chip, 0, keepdims=False), ld)
        theirs = _swap_cores(list(mine.values()), name="swap_cores_" + tag)
        for n, t in zip(mine, theirs):
            res[n] = _adamw(wts[n], mom[n], var[n], mine[n], t)

    finish(flying[:-1], g["ln0_g"], "a")

    small_shapes = [(1, 128)] + [wts[n].shape for n in _SMALL] + [g[n].shape for n in _SMALL_SHARDED]
    packed = _pack([loss] + [g[n] for n in _SMALL] + [g[n] for n in _SMALL_SHARDED])
    tot = _unpack(_allreduce_small(packed), small_shapes)
    loss_out = tot[0][0, 0]
    gsm = dict(zip(_SMALL + _SMALL_SHARDED, tot[1:]))
    for n in _SMALL_SHARDED:
        cols = wts[n].shape[-1]
        gsm[n] = lax.dynamic_slice_in_dim(gsm[n], chip * cols, cols, axis=1).reshape(wts[n].shape)

    names = _SMALL + _SMALL_SHARDED
    shapes = [wts[n].shape for n in names]
    pk = lambda dct: _pack([dct[n] for n in names])
    small_out = _adamw(pk(wts), pk(mom), pk(var), pk(gsm))
    small_res = [_unpack(r, shapes) for r in small_out]
    for j, n in enumerate(names):
        res[n] = [small_res[q][j] for q in range(4)]
    finish(flying[-1:], small_out[0], "b")

    return (loss_out, grad_x, *[res[n][0] for n in _ORDER], *[res[n][1] for n in _ORDER],
            *[res[n][2] for n in _ORDER], *[res[n][3] for n in _ORDER])
```

```python
import functools
import math

import jax
import jax.numpy as jnp
from jax import lax
from jax.experimental import pallas as pl
from jax.experimental.pallas import tpu as pltpu

F32 = jnp.float32
BF16 = jnp.bfloat16
HI = lax.Precision.HIGHEST

N_META = 16
M_HEADS = 4
M_CHUNK = 128
PAD = M_CHUNK - N_META
CONV_W = 4
S5_GROUP = 16
S5_STATE = 64
S5_KCH = 4
LN_EPS = 1e-5
ALPHA = 2.0 ** 0.25
NEG = -1e30
ADAM_LR, ADAM_B1, ADAM_B2, ADAM_EPS, ADAM_WD, ADAM_STEP = 0.001, 0.9, 0.999, 1e-08, 0.01, 10

O_OFF, GS_OFF, GM_OFF, V_OFF, Q_OFF, K_OFF, U_OFF, G_OFF, NP = 0, 1024, 2048, 3072, 4096, 4608, 5120, 5632, 5760

NN = ((1,), (0,))
NT = ((1,), (1,))
TN = ((0,), (0,))


def _dot(a, b, dims=NN, prec=None):
    return lax.dot_general(a, b, (dims, ((), ())), preferred_element_type=F32, precision=prec)


def _bf(x):
    return x.astype(BF16)


def _sig(x):
    return 0.5 * jnp.tanh(0.5 * x) + 0.5


def _pcall(body, **kw):
    return pl.pallas_call(body, **kw)


def _cp(sem=None, vmem_mb=None):
    kw = {}
    if sem is not None:
        kw["dimension_semantics"] = sem
    if vmem_mb is not None:
        kw["vmem_limit_bytes"] = vmem_mb << 20
    return pltpu.CompilerParams(**kw)


def _row_tile(n, want, mult=16):
    best = None
    for t in range(mult, want + 1, mult):
        if n % t == 0:
            best = t
    assert best is not None, (n, want)
    return best


def _resident(shape):
    nd = len(shape)
    return pl.BlockSpec(shape, lambda *_: (0,) * nd, pipeline_mode=pl.Buffered(1))


def _const(shape):
    nd = len(shape)
    return pl.BlockSpec(shape, lambda *_: (0,) * nd)


def _ln_fwd(x, g, b):
    mu = jnp.mean(x, axis=-1, keepdims=True)
    xc = x - mu
    var = jnp.mean(xc * xc, axis=-1, keepdims=True)
    rstd = lax.rsqrt(var + LN_EPS)
    xhat = xc * rstd
    return xhat * g + b, xhat, rstd


def _ln_bwd(dy, xhat, rstd, g):
    dxh = dy * g
    m1 = jnp.mean(dxh, axis=-1, keepdims=True)
    m2 = jnp.mean(dxh * xhat, axis=-1, keepdims=True)
    return rstd * (dxh - m1 - xhat * m2)


def _colsum(x):
    return jnp.sum(x, axis=0, keepdims=True)


def _to_pad_cols(w):
    u, q, k, v, o, gi, gf, gs, gm = (w[..., 0:512], w[..., 512:1024], w[..., 1024:1536], w[..., 1536:2560],
                                     w[..., 2560:3584], w[..., 3584:3588], w[..., 3588:3592], w[..., 3592:4616],
                                     w[..., 4616:5640])
    z = jnp.zeros(w.shape[:-1] + (NP - G_OFF - 8,), w.dtype)
    return jnp.concatenate([o, gs, gm, v, q, k, u, gi, gf, z], axis=-1)


def _from_pad_cols(w):
    o, gs, gm, v, q, k, u = (w[..., O_OFF:GS_OFF], w[..., GS_OFF:GM_OFF], w[..., GM_OFF:V_OFF], w[..., V_OFF:Q_OFF],
                             w[..., Q_OFF:K_OFF], w[..., K_OFF:U_OFF], w[..., U_OFF:G_OFF])
    gi, gf = w[..., G_OFF:G_OFF + 4], w[..., G_OFF + 4:G_OFF + 8]
    return jnp.concatenate([u, q, k, v, o, gi, gf, gs, gm], axis=-1)


_IN_REF = (("u", 512), ("q", 512), ("k", 512), ("v", 1024), ("o", 1024), ("i", 4), ("f", 4), ("gs", 1024), ("gm", 1024))
_IN_PAD = (("o", O_OFF), ("gs", GS_OFF), ("gm", GM_OFF), ("v", V_OFF), ("q", Q_OFF), ("k", K_OFF), ("u", U_OFF),
           ("i", G_OFF), ("f", G_OFF + 4))


def _in_ref_ranges():
    out, off = {}, 0
    for n, s in _IN_REF:
        out[n] = (off, off + s)
        off += s
    return out, off


def _w_in_from_slots(g):
    rng, total = _in_ref_ranges()
    width = total // g.shape[0]
    cols = []
    for n, _ in _IN_PAD:
        a, b = rng[n]
        while a < b:
            s = a // width
            e = min(b, (s + 1) * width)
            cols.append(g[s][:, a - s * width:e - s * width])
            a = e
    cols.append(jnp.zeros((g.shape[1], NP - G_OFF - 8), g.dtype))
    return jnp.concatenate(cols, axis=1)


def _slots_from_w_in(wp, nslot=4):
    rng, total = _in_ref_ranges()
    width = total // nslot
    pad_off = dict(_IN_PAD)
    slots = []
    for s in range(nslot):
        lo, hi = s * width, (s + 1) * width
        cols = []
        for n, _ in _IN_REF:
            a, b = rng[n]
            x0, x1 = max(a, lo), min(b, hi)
            if x0 < x1:
                cols.append(wp[:, pad_off[n] + x0 - a:pad_off[n] + x1 - a])
        slots.append(jnp.concatenate(cols, axis=1))
    return jnp.stack(slots, axis=0)


def _ln0_fwd(hin, g, b, lp):
    r, d = hin.shape
    tm = _row_tile(lp, 416)

    def body(x_ref, g_ref, b_ref, o_ref, ob_ref):
        y, _, _ = _ln_fwd(x_ref[...], g_ref[...], b_ref[...])
        o_ref[...] = y
        ob_ref[...] = _bf(y)

    row = pl.BlockSpec((tm, d), lambda i: (i, 0))
    return _pcall(
        body, name="ln0_fwd", grid=(r // tm,),
        in_specs=[row, _const((1, d)), _const((1, d))],
        out_specs=[row, row],
        out_shape=[jax.ShapeDtypeStruct((r, d), F32), jax.ShapeDtypeStruct((r, d), BF16)],
        compiler_params=_cp(("parallel",)),
    )(hin, g, b)


def _ln0_bwd(hin, dr1, dpw, g, lp):
    r, d = hin.shape
    tm = _row_tile(lp, 416)
    tps = lp // tm
    assert tm >= PAD + N_META

    def body(x_ref, a_ref, c_ref, g_ref, o_ref, dg_ref, db_ref, dm_ref):
        i = pl.program_id(0)

        @pl.when(i == 0)
        def _():
            dg_ref[...] = jnp.zeros_like(dg_ref)
            db_ref[...] = jnp.zeros_like(db_ref)
            dm_ref[...] = jnp.zeros_like(dm_ref)

        dy = ALPHA * a_ref[...] + c_ref[...]
        _, xhat, rstd = _ln_fwd(x_ref[...], g_ref[...], 0.0)
        dx = _ln_bwd(dy, xhat, rstd, g_ref[...])
        o_ref[...] = dx
        dg_ref[...] += _colsum(dy * xhat)
        db_ref[...] += _colsum(dy)

        @pl.when(i % tps == 0)
        def _():
            dm_ref[...] += dx[PAD:PAD + N_META, :]

    return _pcall(
        body, name="ln0_bwd", grid=(r // tm,),
        in_specs=[pl.BlockSpec((tm, d), lambda i: (i, 0))] * 3 + [_const((1, d))],
        out_specs=[pl.BlockSpec((tm, d), lambda i: (i, 0)), _const((1, d)), _const((1, d)), _const((N_META, d))],
        out_shape=[jax.ShapeDtypeStruct((r, d), F32), jax.ShapeDtypeStruct((1, d), F32),
                   jax.ShapeDtypeStruct((1, d), F32), jax.ShapeDtypeStruct((N_META, d), F32)],
        compiler_params=_cp(("arbitrary",)),
    )(hin, dr1, dpw, g)


IN_CHUNK = 1152


def _chunk_cols(w):
    k, n = w.shape
    return jnp.transpose(w.reshape(k, n // IN_CHUNK, IN_CHUNK), (1, 0, 2))


def _inproj(h0b, w3, bias, lp):
    r, d = h0b.shape
    nj, _, tn = w3.shape
    tm = _row_tile(lp, 832)
    tps = lp // tm

    def body(a_ref, w_ref, b_ref, o_ref):
        i = pl.program_id(0)
        j = pl.program_id(1)
        acc = _dot(a_ref[...], w_ref[j]) + b_ref[...]
        t = (i % tps) * tm + lax.broadcasted_iota(jnp.int32, (tm, 1), 0)
        o_ref[...] = jnp.where(t >= PAD, acc, 0.0)

    return _pcall(
        body, name="inproj", grid=(r // tm, nj),
        in_specs=[pl.BlockSpec((tm, d), lambda i, j: (i, 0)), _resident(w3.shape),
                  pl.BlockSpec((1, tn), lambda i, j: (0, j))],
        out_specs=pl.BlockSpec((tm, tn), lambda i, j: (i, j)),
        out_shape=jax.ShapeDtypeStruct((r, nj * tn), F32),
        compiler_params=_cp(("parallel", "arbitrary"), 48),
    )(h0b, w3, bias)


def _mm_tn(a, b, *, name, split=1, colsum=False, tk_want=832):
    r, m = a.shape
    n = b.shape[1]
    tk = _row_tile(r, tk_want)
    tm = min(m, 1024)
    ns = n // split
    tn = ns
    for cand in (1024, 1152, 640, 512, 128):
        if ns % cand == 0 and cand <= ns:
            tn = cand
            break
    nb = ns // tn
    nk = r // tk

    def body(a_ref, b_ref, o_ref, *rest):
        acc = rest[-1]
        k = pl.program_id(2)

        @pl.when(k == 0)
        def _():
            acc[...] = jnp.zeros_like(acc)

        bt = b_ref[...]
        acc[...] += _dot(_bf(a_ref[...]), _bf(bt), TN)

        @pl.when(k == nk - 1)
        def _():
            o_ref[...] = acc[...]

        if colsum:
            cs_ref = rest[0]

            @pl.when(k == 0)
            def _():
                cs_ref[...] = jnp.zeros_like(cs_ref)

            cs_ref[...] += _colsum(bt.astype(F32))

    out_specs = [pl.BlockSpec((None, tm, tn), lambda i, j, k: (j // nb, i, j % nb))]
    out_shape = [jax.ShapeDtypeStruct((split, m, ns), F32)]
    if colsum:
        assert m == tm
        out_specs.append(pl.BlockSpec((1, tn), lambda i, j, k: (0, j)))
        out_shape.append(jax.ShapeDtypeStruct((1, n), F32))
    res = _pcall(
        body, name=name, grid=(m // tm, n // tn, nk),
        in_specs=[pl.BlockSpec((tk, tm), lambda i, j, k: (k, i)), pl.BlockSpec((tk, tn), lambda i, j, k: (k, j))],
        out_specs=out_specs, out_shape=out_shape,
        scratch_shapes=[pltpu.VMEM((tm, tn), F32)],
        compiler_params=_cp(("parallel", "parallel", "arbitrary"), 48),
    )(a, b)
    return res if colsum else res[0]


def _mm_nt(a, w3, lp, *, name, dep=None):
    r, kdim = a.shape
    nk, n, tk = w3.shape
    assert nk * tk == kdim
    tm = _row_tile(lp, 832)
    deps = [] if dep is None else [dep]

    def body(a_ref, w_ref, *rest):
        o_ref, acc = rest[-2:]
        k = pl.program_id(1)

        @pl.when(k == 0)
        def _():
            acc[...] = jnp.zeros_like(acc)

        acc[...] += _dot(_bf(a_ref[...]), w_ref[k], NT)

        @pl.when(k == nk - 1)
        def _():
            o_ref[...] = acc[...]

    return _pcall(
        body, name=name, grid=(r // tm, nk),
        in_specs=[pl.BlockSpec((tm, tk), lambda i, k: (i, k)), _resident(w3.shape)]
        + [_const(dp_.shape) for dp_ in deps],
        out_specs=pl.BlockSpec((tm, n), lambda i, k: (i, 0)),
        out_shape=jax.ShapeDtypeStruct((r, n), F32),
        scratch_shapes=[pltpu.VMEM((tm, n), F32)],
        compiler_params=_cp(("parallel", "arbitrary"), 48),
    )(a, w3, *deps)


def _s5_prep(lam_re, lam_im, log_dt, b_re_t, b_im_t):
    g, p = lam_re.shape
    h = b_re_t.shape[0]

    def body(lr_ref, li_ref, ldt_ref, br_ref, bi_ref, pr_ref, pi_ref, bbr_ref, bbi_ref):
        lr, li = lr_ref[...], li_ref[...]
        dt = jnp.exp(ldt_ref[...])
        e = jnp.exp(lr * dt)
        ar, ai = e * jnp.cos(li * dt), e * jnp.sin(li * dt)
        den = lr * lr + li * li
        cr = ((ar - 1.0) * lr + ai * li) / den
        ci = (ai * lr - (ar - 1.0) * li) / den
        br, bi = br_ref[...], bi_ref[...]
        bbr_ref[...] = cr[None] * br - ci[None] * bi
        bbi_ref[...] = cr[None] * bi + ci[None] * br
        xr, xi = ar, ai
        pr_ref[0] = xr
        pi_ref[0] = xi
        for t in range(1, 8):
            xr, xi = xr * ar - xi * ai, xr * ai + xi * ar
            pr_ref[t] = xr
            pi_ref[t] = xi

    sd = jax.ShapeDtypeStruct
    return _pcall(body, name="s5_prep",
                  out_shape=[sd((8, g, p), F32), sd((8, g, p), F32), sd((h, g, p), F32), sd((h, g, p), F32)])(
        lam_re, lam_im, log_dt, b_re_t, b_im_t)


def _s5_prep_bwd(lam_re, lam_im, log_dt, b_re_t, b_im_t, da_re, da_im, dbb_re_t, dbb_im_t):
    g, p = lam_re.shape
    h = b_re_t.shape[0]

    def body(lr_ref, li_ref, ldt_ref, br_ref, bi_ref, dar_ref, dai_ref, dbr_ref, dbi_ref,
             glr_ref, gli_ref, gdt_ref, gbr_ref, gbi_ref):
        lr, li = lr_ref[...], li_ref[...]
        dt = jnp.exp(ldt_ref[...])
        e = jnp.exp(lr * dt)
        ar, ai = e * jnp.cos(li * dt), e * jnp.sin(li * dt)
        den = lr * lr + li * li
        cr = ((ar - 1.0) * lr + ai * li) / den
        ci = (ai * lr - (ar - 1.0) * li) / den
        br, bi = br_ref[...], bi_ref[...]
        gr, gi = dbr_ref[...], dbi_ref[...]
        gbr_ref[...] = gr * cr[None] + gi * ci[None]
        gbi_ref[...] = gi * cr[None] - gr * ci[None]
        gcr = jnp.sum(gr * br + gi * bi, axis=0)
        gci = jnp.sum(gi * br - gr * bi, axis=0)
        ilr, ili = lr / den, -li / den
        gar = dar_ref[...] + gcr * ilr + gci * ili
        gai = dai_ref[...] + gci * ilr - gcr * ili
        qr, qi = cr * ilr - ci * ili, cr * ili + ci * ilr
        glr = -(gcr * qr + gci * qi)
        gli = -(gci * qr - gcr * qi)
        gzr = gar * ar + gai * ai
        gzi = gai * ar - gar * ai
        glr_ref[...] = glr + gzr * dt
        gli_ref[...] = gli + gzi * dt
        gdt_ref[...] = jnp.sum(gzr * lr + gzi * li, axis=1, keepdims=True) * dt

    sd = jax.ShapeDtypeStruct
    return _pcall(body, name="s5_prep_bwd",
                  out_shape=[sd((g, p), F32), sd((g, p), F32), sd((g, 1), F32), sd((h, g, p), F32), sd((h, g, p), F32)])(
        lam_re, lam_im, log_dt, b_re_t, b_im_t, da_re, da_im, dbb_re_t, dbb_im_t)


def _cmul(xr, xi, yr, yi):
    return xr * yr - xi * yi, xr * yi + xi * yr


def _dot5(a, b, dims=NN):
    return _dot(_bf(a), _bf(b), dims)


def _s5_fwd(p3, bk, cre, cim, apow, dskip):
    bsz, lp, _ = p3.shape
    tt = _row_tile(lp, 528, 8)
    nt = lp // tt
    nblk = tt // 8
    hw = 512

    def body(u_ref, bk_ref, cre_ref, cim_ref, ap_ref, d_ref, y_ref, xs_ref, car_ref):
        t = pl.program_id(2)

        @pl.when(t == 0)
        def _():
            car_ref[...] = jnp.zeros_like(car_ref)

        u = u_ref[...]
        xs_ref[...] = _dot5(u, bk_ref[...])
        ap = ap_ref[...]
        apr, api = ap[:, :hw], ap[:, hw:]
        rows = lax.broadcasted_iota(jnp.int32, (8, hw), 0)
        lev = [(d, jnp.where(rows < d, 0.0, jnp.broadcast_to(apr[d - 1:d, :], (8, hw))),
                jnp.where(rows < d, 0.0, jnp.broadcast_to(api[d - 1:d, :], (8, hw)))) for d in (1, 2, 4)]

        def blk(i, carry):
            cr, ci = carry
            off = pl.multiple_of(i * 8, 8)
            x = xs_ref[pl.ds(off, 8), :]
            xr, xi = x[:, :hw], x[:, hw:]
            for d, lr, li in lev:
                mr, mi = _cmul(pltpu.roll(xr, d, 0), pltpu.roll(xi, d, 0), lr, li)
                xr, xi = xr + mr, xi + mi
            mr, mi = _cmul(apr, api, cr, ci)
            xr, xi = xr + mr, xi + mi
            xs_ref[pl.ds(off, 8), :] = jnp.concatenate([xr, xi], axis=1)
            return xr[7:8, :], xi[7:8, :]

        c0 = car_ref[...]
        cr, ci = lax.fori_loop(0, nblk, blk, (c0[0:1, :hw], c0[0:1, hw:]))
        car_ref[...] = jnp.broadcast_to(jnp.concatenate([cr, ci], axis=1), car_ref.shape)
        xs = xs_ref[...]
        y_ref[...] = (_dot5(xs[:, :hw], cre_ref[...]) - _dot5(xs[:, hw:], cim_ref[...])
                      + d_ref[...] * u)

    ub = U_OFF // 128
    return _pcall(
        body, name="s5_fwd", grid=(S5_KCH, bsz, nt),
        in_specs=[pl.BlockSpec((None, tt, 128), lambda k, b, t: (b, t, ub + k)),
                  pl.BlockSpec((None, 128, 2 * hw), lambda k, b, t: (k, 0, 0)),
                  pl.BlockSpec((None, hw, 128), lambda k, b, t: (k, 0, 0)),
                  pl.BlockSpec((None, hw, 128), lambda k, b, t: (k, 0, 0)),
                  pl.BlockSpec((None, 8, 2 * hw), lambda k, b, t: (k, 0, 0)),
                  pl.BlockSpec((1, 128), lambda k, b, t: (0, k))],
        out_specs=[pl.BlockSpec((None, tt, 128), lambda k, b, t: (b, t, k)),
                   pl.BlockSpec((None, None, tt, 2 * hw), lambda k, b, t: (b, k, t, 0))],
        out_shape=[jax.ShapeDtypeStruct((bsz, lp, S5_KCH * 128), F32),
                   jax.ShapeDtypeStruct((bsz, S5_KCH, lp, 2 * hw), F32)],
        scratch_shapes=[pltpu.VMEM((8, 2 * hw), F32)],
        compiler_params=_cp(("parallel", "parallel", "arbitrary"), 40),
    )(p3, bk, cre, cim, apow, dskip)


def _s5_bwd(dp3, p3, dy3, xs, bk, cre, cim, apow_rev, dskip):
    bsz, lp, _ = p3.shape
    tt = _row_tile(lp, 528, 8)
    nt = lp // tt
    nblk = tt // 8
    hw = 512
    tb = tt // 8

    def body(dp_any, u_ref, dy_ref, xs_ref, halo_ref, bk_ref, cre_ref, cim_ref, ap_ref, d_ref,
             du_ref, dbk_ref, dcre_ref, dcim_ref, da_ref, dd_ref, g_ref, ext_ref, car_ref):
        b = pl.program_id(1)
        t = pl.program_id(2)
        tidx = nt - 1 - t

        @pl.when(t == 0)
        def _():
            car_ref[...] = jnp.zeros_like(car_ref)

        @pl.when((b == 0) & (t == 0))
        def _():
            dbk_ref[...] = jnp.zeros_like(dbk_ref)
            dcre_ref[...] = jnp.zeros_like(dcre_ref)
            dcim_ref[...] = jnp.zeros_like(dcim_ref)
            da_ref[...] = jnp.zeros_like(da_ref)
            dd_ref[...] = jnp.zeros_like(dd_ref)

        u = u_ref[...]
        dy = dy_ref[...]
        g_ref[:, :hw] = _dot5(dy, cre_ref[...], NT)
        g_ref[:, hw:] = -_dot5(dy, cim_ref[...], NT)
        ap = ap_ref[...]
        apr, api = ap[:, :hw], -ap[:, hw:]
        rows = lax.broadcasted_iota(jnp.int32, (8, hw), 0)
        lev = [(d, jnp.where(rows >= 8 - d, 0.0, jnp.broadcast_to(apr[8 - d:9 - d, :], (8, hw))),
                jnp.where(rows >= 8 - d, 0.0, jnp.broadcast_to(api[8 - d:9 - d, :], (8, hw)))) for d in (1, 2, 4)]

        def blk(i, carry):
            cr, ci = carry
            off = pl.multiple_of((nblk - 1 - i) * 8, 8)
            x = g_ref[pl.ds(off, 8), :]
            xr, xi = x[:, :hw], x[:, hw:]
            for d, lr, li in lev:
                mr, mi = _cmul(pltpu.roll(xr, 8 - d, 0), pltpu.roll(xi, 8 - d, 0), lr, li)
                xr, xi = xr + mr, xi + mi
            mr, mi = _cmul(apr, api, cr, ci)
            xr, xi = xr + mr, xi + mi
            g_ref[pl.ds(off, 8), :] = jnp.concatenate([xr, xi], axis=1)
            return xr[0:1, :], xi[0:1, :]

        c0 = car_ref[...]
        cr, ci = lax.fori_loop(0, nblk, blk, (c0[0:1, :hw], c0[0:1, hw:]))
        car_ref[...] = jnp.broadcast_to(jnp.concatenate([cr, ci], axis=1), car_ref.shape)

        gg = g_ref[...]
        du = _dot5(gg, bk_ref[...], NT) + d_ref[...] * dy
        trow = tidx * tt + lax.broadcasted_iota(jnp.int32, (tt, 1), 0)
        du_ref[...] = jnp.where(trow >= PAD, du, 0.0).astype(du_ref.dtype)
        dbk_ref[...] += _dot5(u, gg, TN)
        xsv = xs_ref[...]
        dcre_ref[...] += _dot5(xsv[:, :hw], dy, TN)
        dcim_ref[...] -= _dot5(xsv[:, hw:], dy, TN)
        dd_ref[...] += _colsum(dy * u)
        ext_ref[0:8, :] = jnp.where(tidx == 0, 0.0, halo_ref[...])
        ext_ref[8:, :] = xsv
        xp = ext_ref[pl.ds(7, tt), :]
        gr, gi, pr, pi = gg[:, :hw], gg[:, hw:], xp[:, :hw], xp[:, hw:]
        da_ref[:, :hw] += _colsum(gr * pr + gi * pi)
        da_ref[:, hw:] += _colsum(gi * pr - gr * pi)

    ub = U_OFF // 128
    sd = jax.ShapeDtypeStruct
    rt = lambda t: nt - 1 - t
    res = _pcall(
        body, name="s5_bwd", grid=(S5_KCH, bsz, nt),
        in_specs=[pl.BlockSpec(memory_space=pl.ANY),
                  pl.BlockSpec((None, tt, 128), lambda k, b, t: (b, rt(t), ub + k)),
                  pl.BlockSpec((None, tt, 128), lambda k, b, t: (b, rt(t), k)),
                  pl.BlockSpec((None, None, tt, 2 * hw), lambda k, b, t: (b, k, rt(t), 0)),
                  pl.BlockSpec((None, None, 8, 2 * hw), lambda k, b, t: (b, k, jnp.maximum(rt(t) * tb - 1, 0), 0)),
                  pl.BlockSpec((None, 128, 2 * hw), lambda k, b, t: (k, 0, 0)),
                  pl.BlockSpec((None, hw, 128), lambda k, b, t: (k, 0, 0)),
                  pl.BlockSpec((None, hw, 128), lambda k, b, t: (k, 0, 0)),
                  pl.BlockSpec((None, 8, 2 * hw), lambda k, b, t: (k, 0, 0)),
                  pl.BlockSpec((1, 128), lambda k, b, t: (0, k))],
        out_specs=[pl.BlockSpec((None, tt, 128), lambda k, b, t: (b, rt(t), ub + k)),
                   pl.BlockSpec((None, 128, 2 * hw), lambda k, b, t: (k, 0, 0)),
                   pl.BlockSpec((None, hw, 128), lambda k, b, t: (k, 0, 0)),
                   pl.BlockSpec((None, hw, 128), lambda k, b, t: (k, 0, 0)),
                   pl.BlockSpec((None, 1, 2 * hw), lambda k, b, t: (k, 0, 0)),
                   pl.BlockSpec((1, 128), lambda k, b, t: (0, k))],
        out_shape=[sd(dp3.shape, dp3.dtype), sd((S5_KCH, 128, 2 * hw), F32), sd((S5_KCH, hw, 128), F32),
                   sd((S5_KCH, hw, 128), F32), sd((S5_KCH, 1, 2 * hw), F32), sd((1, S5_KCH * 128), F32)],
        scratch_shapes=[pltpu.VMEM((tt, 2 * hw), F32), pltpu.VMEM((tt + 8, 2 * hw), F32), pltpu.VMEM((8, 2 * hw), F32)],
        input_output_aliases={0: 0},
        compiler_params=_cp(("arbitrary", "arbitrary", "arbitrary"), 48),
    )(dp3, p3, dy3, xs, xs, bk, cre, cim, apow_rev, dskip)
    return res


_G0 = math.sqrt(2.0 / math.pi)
_G1 = 0.044715


def _gelu(y):
    return 0.5 * y * (1.0 + jnp.tanh(_G0 * (y + _G1 * y * y * y)))


def _gelu_grad(y):
    th = jnp.tanh(_G0 * (y + _G1 * y * y * y))
    return 0.5 * (1.0 + th) + 0.5 * y * (1.0 - th * th) * _G0 * (1.0 + 3.0 * _G1 * y * y)


def _glu_fwd(y_s5, wglu_g, lp):
    r, w = y_s5.shape
    tm = _row_tile(lp, 416)
    cw = wglu_g.shape[2]

    def body(y_ref, w_ref, gy_ref, z_ref, o_ref):
        gy = _bf(_gelu(y_ref[...]))
        gy_ref[...] = gy
        zs = [_dot(gy, w_ref[s]) for s in range(4)]
        for s in range(4):
            z_ref[:, s * cw:(s + 1) * cw] = zs[s]
        o_ref[:, :cw] = zs[0] * _sig(zs[2])
        o_ref[:, cw:] = zs[1] * _sig(zs[3])

    sd = jax.ShapeDtypeStruct
    return _pcall(
        body, name="glu_fwd", grid=(r // tm,),
        in_specs=[pl.BlockSpec((tm, w), lambda i: (i, 0)), _resident(wglu_g.shape)],
        out_specs=[pl.BlockSpec((tm, w), lambda i: (i, 0)), pl.BlockSpec((tm, 4 * cw), lambda i: (i, 0)),
                   pl.BlockSpec((tm, 2 * cw), lambda i: (i, 0))],
        out_shape=[sd((r, w), BF16), sd((r, 4 * cw), F32), sd((r, 2 * cw), F32)],
        compiler_params=_cp(("parallel",), 40),
    )(y_s5, wglu_g)


def _glu_bwd(dyg, z, y_s5, wglu_g, lp):
    r, w = y_s5.shape
    tm = _row_tile(lp, 416)
    cw = wglu_g.shape[2]

    def body(d_ref, z_ref, y_ref, w_ref, dz_ref, dy_ref):
        d = d_ref[...]
        zz = z_ref[...]
        acc = jnp.zeros((tm, w), F32)
        for s in range(2):
            z1 = zz[:, s * cw:(s + 1) * cw]
            sg = _sig(zz[:, (2 + s) * cw:(3 + s) * cw])
            dd = d[:, s * cw:(s + 1) * cw]
            dz1 = _bf(dd * sg)
            dz2 = _bf(dd * z1 * sg * (1.0 - sg))
            dz_ref[:, s * cw:(s + 1) * cw] = dz1
            dz_ref[:, (2 + s) * cw:(3 + s) * cw] = dz2
            acc += _dot(dz1, w_ref[s], NT) + _dot(dz2, w_ref[2 + s], NT)
        dy_ref[...] = acc * _gelu_grad(y_ref[...])

    sd = jax.ShapeDtypeStruct
    return _pcall(
        body, name="glu_bwd", grid=(r // tm,),
        in_specs=[pl.BlockSpec((tm, 2 * cw), lambda i: (i, 0)), pl.BlockSpec((tm, 4 * cw), lambda i: (i, 0)),
                  pl.BlockSpec((tm, w), lambda i: (i, 0)), _resident(wglu_g.shape)],
        out_specs=[pl.BlockSpec((tm, 4 * cw), lambda i: (i, 0)), pl.BlockSpec((tm, w), lambda i: (i, 0))],
        out_shape=[sd((r, 4 * cw), BF16), sd((r, w), F32)],
        compiler_params=_cp(("parallel",), 40),
    )(dyg, z, y_s5, wglu_g)


def _conv_fwd(p3, cw, cb):
    bsz, lp, _ = p3.shape
    tt = _row_tile(lp, 416)
    nt = lp // tt
    tb = tt // 8
    c = cw.shape[1]
    qb = Q_OFF // c

    def body(x_ref, halo_ref, w_ref, b_ref, pre_ref, act_ref, ext_ref):
        t = pl.program_id(1)
        ext_ref[0:8, :] = jnp.where(t == 0, 0.0, halo_ref[...])
        ext_ref[8:, :] = x_ref[...]
        w = w_ref[...]
        acc = b_ref[...] + w[0:1, :] * ext_ref[pl.ds(5, tt), :]
        for j in range(1, CONV_W):
            acc = acc + w[j:j + 1, :] * ext_ref[pl.ds(5 + j, tt), :]
        pre_ref[...] = acc
        act_ref[...] = acc * _sig(acc)

    sd = jax.ShapeDtypeStruct
    return _pcall(
        body, name="conv_fwd", grid=(bsz, nt),
        in_specs=[pl.BlockSpec((None, tt, c), lambda b, t: (b, t, qb)),
                  pl.BlockSpec((None, 8, c), lambda b, t: (b, jnp.maximum(t * tb - 1, 0), qb)),
                  _const((CONV_W, c)), _const((1, c))],
        out_specs=[pl.BlockSpec((None, tt, c), lambda b, t: (b, t, 0))] * 2,
        out_shape=[sd((bsz, lp, c), F32)] * 2,
        scratch_shapes=[pltpu.VMEM((tt + 8, c), F32)],
        compiler_params=_cp(("parallel", "parallel")),
    )(p3, p3, cw, cb)


def _conv_bwd(dp3, p3, dact3, pre3, cw):
    bsz, lp, _ = p3.shape
    tt = _row_tile(lp, 416)
    nt = lp // tt
    tb = tt // 8
    c = cw.shape[1]
    qb = Q_OFF // c

    def silu_grad(x):
        s = _sig(x)
        return s * (1.0 + x * (1.0 - s))

    def body(dp_any, x_ref, xh_ref, d_ref, dh_ref, pre_ref, preh_ref, w_ref, o_ref, dw_ref, db_ref, ext_ref, dext_ref):
        b = pl.program_id(0)
        t = pl.program_id(1)

        @pl.when((b == 0) & (t == 0))
        def _():
            dw_ref[...] = jnp.zeros_like(dw_ref)
            db_ref[...] = jnp.zeros_like(db_ref)

        dc = d_ref[...] * silu_grad(pre_ref[...])
        dch = jnp.where(t == nt - 1, 0.0, dh_ref[...] * silu_grad(preh_ref[...]))
        dext_ref[0:tt, :] = dc
        dext_ref[tt:, :] = dch
        ext_ref[0:8, :] = jnp.where(t == 0, 0.0, xh_ref[...])
        ext_ref[8:, :] = x_ref[...]
        w = w_ref[...]
        acc = w[CONV_W - 1:CONV_W, :] * dc
        for j in range(CONV_W - 1):
            acc = acc + w[j:j + 1, :] * dext_ref[pl.ds(CONV_W - 1 - j, tt), :]
        trow = t * tt + lax.broadcasted_iota(jnp.int32, (tt, 1), 0)
        o_ref[...] = jnp.where(trow >= PAD, acc, 0.0).astype(o_ref.dtype)
        db_ref[...] += _colsum(dc)
        for j in range(CONV_W):
            dw_ref[j:j + 1, :] += _colsum(dc * ext_ref[pl.ds(5 + j, tt), :])

    sd = jax.ShapeDtypeStruct
    nxt = lambda t: jnp.minimum((t + 1) * tb, lp // 8 - 1)
    return _pcall(
        body, name="conv_bwd", grid=(bsz, nt),
        in_specs=[pl.BlockSpec(memory_space=pl.ANY),
                  pl.BlockSpec((None, tt, c), lambda b, t: (b, t, qb)),
                  pl.BlockSpec((None, 8, c), lambda b, t: (b, jnp.maximum(t * tb - 1, 0), qb)),
                  pl.BlockSpec((None, tt, c), lambda b, t: (b, t, 0)),
                  pl.BlockSpec((None, 8, c), lambda b, t: (b, nxt(t), 0)),
                  pl.BlockSpec((None, tt, c), lambda b, t: (b, t, 0)),
                  pl.BlockSpec((None, 8, c), lambda b, t: (b, nxt(t), 0)),
                  _const((CONV_W, c))],
        out_specs=[pl.BlockSpec((None, tt, c), lambda b, t: (b, t, qb)), _const((CONV_W, c)), _const((1, c))],
        out_shape=[sd(dp3.shape, dp3.dtype), sd((CONV_W, c), F32), sd((1, c), F32)],
        scratch_shapes=[pltpu.VMEM((tt + 8, c), F32), pltpu.VMEM((tt + 8, c), F32)],
        input_output_aliases={0: 0},
        compiler_params=_cp(("arbitrary", "arbitrary")),
    )(dp3, p3, p3, dact3, dact3, pre3, pre3, cw)


def _mlstm_gates(g, h_idx, c_idx, lc):
    lane = lax.broadcasted_iota(jnp.int32, g.shape, 1)
    i_col = jnp.sum(jnp.where(lane == h_idx, g, 0.0), axis=1, keepdims=True)
    f_col = jnp.sum(jnp.where(lane == M_HEADS + h_idx, g, 0.0), axis=1, keepdims=True)
    row = lax.broadcasted_iota(jnp.int32, (lc, 1), 0)
    valid = (c_idx * lc + row) >= PAD
    li = jnp.where(valid, i_col, NEG)
    lf = jnp.where(valid, jnp.minimum(f_col, 0.0) - jnp.log(1.0 + jnp.exp(-jnp.abs(f_col))), 0.0)
    r2 = lax.broadcasted_iota(jnp.int32, (lc, lc), 0)
    c2 = lax.broadcasted_iota(jnp.int32, (lc, lc), 1)
    eye = r2 == c2
    tril = r2 >= c2
    to_row = lambda col: jnp.sum(jnp.where(eye, col, 0.0), axis=0, keepdims=True)
    lf_row = to_row(lf)
    b_col = jnp.sum(jnp.where(tril, lf_row, 0.0), axis=1, keepdims=True)
    b_row = to_row(b_col)
    li_row = to_row(li)
    d_mat = jnp.where(tril, b_col - b_row + li_row, NEG)
    return dict(f_col=f_col, valid=valid, li=li, b_col=b_col, d_mat=d_mat, eye=eye, r2=r2, c2=c2, row=row,
                to_row=to_row)


def _mlstm_chunk(q, ks, v, gq, c_st, n_st, m_st, lc):
    b_col, d_mat = gq["b_col"], gq["d_mat"]
    m_inter = b_col + m_st
    m_row = jnp.maximum(m_inter, jnp.max(d_mat, axis=1, keepdims=True))
    w_intra = jnp.exp(d_mat - m_row)
    w_inter = jnp.exp(m_inter - m_row)
    qb, kb, vb, cb = _bf(q), _bf(ks), _bf(v), _bf(c_st)
    s = _dot(qb, kb, NT) * w_intra
    qc = _dot(qb, cb)
    num = _dot(_bf(s), vb) + w_inter * qc
    qn = jnp.sum(q * n_st, axis=1, keepdims=True)
    den = jnp.sum(s, axis=1, keepdims=True) + w_inter * qn
    e = jnp.exp(-m_row)
    nn = jnp.maximum(jnp.abs(den), e)
    b_last = b_col[lc - 1:lc, :]
    g_log = b_last - b_col + gq["li"]
    m_new = jnp.maximum(b_last + m_st, jnp.max(g_log, axis=0, keepdims=True))
    w_k = jnp.exp(g_log - m_new)
    decay = jnp.exp(b_last + m_st - m_new)
    return dict(w_intra=w_intra, w_inter=w_inter, qb=qb, kb=kb, vb=vb, cb=cb, s=s, qc=qc, num=num, qn=qn, den=den,
                e=e, nn=nn, m_new=m_new, w_k=w_k, decay=decay)


def _chunks_per_step(nc):
    return max(c for c in (3, 2, 1) if nc % c == 0)


def _mlstm_fwd(qk3, p3):
    bsz, lp, _ = p3.shape
    lc = M_CHUNK
    nc = lp // lc
    dk, dv = 128, 256
    scale = dk ** -0.5

    cps = _chunks_per_step(nc)
    rows = cps * lc

    def body(q_ref, k_ref, v_ref, g_ref, h_ref, cs_ref, ns_ref, ms_ref, c_sc, n_sc, m_sc):
        st = pl.program_id(1)

        @pl.when(st == 0)
        def _():
            c_sc[...] = jnp.zeros_like(c_sc)
            n_sc[...] = jnp.zeros_like(n_sc)
            m_sc[...] = jnp.zeros_like(m_sc)

        for j in range(cps):
            rs = slice(j * lc, (j + 1) * lc)
            g = g_ref[rs, :]
            for hh in range(M_HEADS):
                c_st, n_st, m_all = c_sc[hh], n_sc[hh], m_sc[hh]
                cs_ref[hh, j] = c_st
                ns_ref[hh, j] = n_st
                ms_ref[hh, j] = m_all
                m_st = m_all[:, 0:1]
                q = q_ref[rs, hh * dk:(hh + 1) * dk]
                ks = k_ref[rs, hh * dk:(hh + 1) * dk] * scale
                v = v_ref[rs, hh * dv:(hh + 1) * dv]
                gq = _mlstm_gates(g, hh, st * cps + j, lc)
                f = _mlstm_chunk(q, ks, v, gq, c_st, n_st, m_st, lc)
                h_ref[rs, hh * dv:(hh + 1) * dv] = f["num"] / f["nn"]
                kw = ks * f["w_k"]
                c_sc[hh] = f["decay"] * c_st + _dot(_bf(kw), f["vb"], TN)
                n_sc[hh] = f["decay"] * n_st + _colsum(kw)
                m_sc[hh] = jnp.broadcast_to(f["m_new"], (1, 128))

    sd = jax.ShapeDtypeStruct
    nh = M_HEADS
    return _pcall(
        body, name="mlstm_fwd", grid=(bsz, nc // cps),
        in_specs=[pl.BlockSpec((None, rows, nh * dk), lambda b, c: (b, c, 0)),
                  pl.BlockSpec((None, rows, nh * dk), lambda b, c: (b, c, 1)),
                  pl.BlockSpec((None, rows, nh * dv), lambda b, c: (b, c, V_OFF // (nh * dv))),
                  pl.BlockSpec((None, rows, 128), lambda b, c: (b, c, G_OFF // 128))],
        out_specs=[pl.BlockSpec((None, rows, nh * dv), lambda b, c: (b, c, 0)),
                   pl.BlockSpec((None, nh, cps, dk, dv), lambda b, c: (b, 0, c, 0, 0)),
                   pl.BlockSpec((None, nh, cps, 1, dk), lambda b, c: (b, 0, c, 0, 0)),
                   pl.BlockSpec((None, nh, cps, 1, 128), lambda b, c: (b, 0, c, 0, 0))],
        out_shape=[sd((bsz, lp, nh * dv), F32), sd((bsz, nh, nc, dk, dv), F32),
                   sd((bsz, nh, nc, 1, dk), F32), sd((bsz, nh, nc, 1, 128), F32)],
        scratch_shapes=[pltpu.VMEM((nh, dk, dv), F32), pltpu.VMEM((nh, 1, dk), F32), pltpu.VMEM((nh, 1, 128), F32)],
        compiler_params=_cp(("parallel", "arbitrary")),
    )(qk3, qk3, p3, p3)


def _mlstm_bwd(dp3, qk3, p3, dh3, cs, ns, ms):
    bsz, lp, _ = p3.shape
    lc = M_CHUNK
    nc = lp // lc
    dk, dv = 128, 256
    scale = dk ** -0.5

    cps = _chunks_per_step(nc)
    nst = nc // cps
    rows = cps * lc

    def body(dp_any, q_ref, k_ref, v_ref, g_ref, dh_ref, cs_ref, ns_ref, ms_ref,
             dv_ref, dqk_ref, dg_ref, dc_sc, dn_sc):
        t = pl.program_id(1)
        st = nst - 1 - t

        @pl.when(t == 0)
        def _():
            dc_sc[...] = jnp.zeros_like(dc_sc)
            dn_sc[...] = jnp.zeros_like(dn_sc)

        lane = lax.broadcasted_iota(jnp.int32, (lc, 128), 1)
        for j in reversed(range(cps)):
            rs = slice(j * lc, (j + 1) * lc)
            g = g_ref[rs, :]
            dgate = jnp.zeros((lc, 128), F32)
            for hh in range(M_HEADS):
                dgate = head(hh, j, rs, st * cps + j, g, lane, dgate, q_ref, k_ref, v_ref, dh_ref, cs_ref, ns_ref,
                             ms_ref, dv_ref, dqk_ref, dc_sc, dn_sc)
            dg_ref[rs, :] = dgate.astype(dg_ref.dtype)

    def head(hh, j, sl, c, g, lane, dgate, q_ref, k_ref, v_ref, dh_ref, cs_ref, ns_ref, ms_ref, dv_ref, dqk_ref,
             dc_sc, dn_sc):
        c_st, n_st = cs_ref[hh, j], ns_ref[hh, j]
        m_st = ms_ref[hh, j][:, 0:1]
        q = q_ref[sl, hh * dk:(hh + 1) * dk]
        ks = k_ref[sl, hh * dk:(hh + 1) * dk] * scale
        v = v_ref[sl, hh * dv:(hh + 1) * dv]
        dh = dh_ref[sl, hh * dv:(hh + 1) * dv]
        gq = _mlstm_gates(g, hh, c, lc)
        f = _mlstm_chunk(q, ks, v, gq, c_st, n_st, m_st, lc)
        eye, r2, c2, row, valid = gq["eye"], gq["r2"], gq["c2"], gq["row"], gq["valid"]
        w_intra, w_inter, s, nn, den = f["w_intra"], f["w_inter"], f["s"], f["nn"], f["den"]
        qb, kb, vb, cb, w_k, decay = f["qb"], f["kb"], f["vb"], f["cb"], f["w_k"], f["decay"]
        d_c, d_n = dc_sc[hh], dn_sc[hh]
        d_cb = _bf(d_c)

        hout = f["num"] / nn
        dnum = dh / nn
        d_nn = -jnp.sum(dh * hout, axis=1, keepdims=True) / nn
        dden = jnp.where(jnp.abs(den) > f["e"], d_nn * jnp.sign(den), 0.0)
        wdnum = w_inter * dnum
        wdden = w_inter * dden
        ds = _dot(_bf(dnum), vb, NT) + dden
        dsw = _bf(ds * w_intra)
        dq = _dot(dsw, kb) + _dot(_bf(wdnum), cb, NT) + wdden * n_st
        dkw = _dot(vb, d_cb, NT) + d_n
        dks = _dot(dsw, qb, TN) + dkw * w_k
        kw = ks * w_k
        dvv = _dot(_bf(s), _bf(dnum), TN) + _dot(_bf(kw), d_cb)
        dd = ds * s
        rs = jnp.sum(dd, axis=1, keepdims=True)
        cs_col = jnp.sum(jnp.where(eye, jnp.sum(dd, axis=0, keepdims=True), 0.0), axis=1, keepdims=True)
        dwi = jnp.sum(dnum * f["qc"], axis=1, keepdims=True) + dden * f["qn"]
        db = rs - cs_col + dwi * w_inter
        dli = cs_col
        ddecay = jnp.sum(jnp.sum(d_c * c_st, axis=1, keepdims=True), axis=0, keepdims=True) \
            + jnp.sum(d_n * n_st, axis=1, keepdims=True)
        dgl = jnp.sum(dkw * ks, axis=1, keepdims=True) * w_k
        dblast = ddecay * decay + jnp.sum(dgl, axis=0, keepdims=True)
        db = db - dgl + jnp.where(row == lc - 1, dblast, 0.0)
        dli = dli + dgl
        db_row = gq["to_row"](db)
        dlf = jnp.sum(jnp.where(c2 >= r2, db_row, 0.0), axis=1, keepdims=True)
        dlf = jnp.where(valid, dlf, 0.0)
        dgate = jnp.where(lane == hh, jnp.where(valid, dli, 0.0), dgate)
        dgate = jnp.where(lane == M_HEADS + hh, dlf / (1.0 + jnp.exp(gq["f_col"])), dgate)
        dqk_ref[sl, hh * dk:(hh + 1) * dk] = dq
        dqk_ref[sl, (M_HEADS + hh) * dk:(M_HEADS + hh + 1) * dk] = dks * scale
        dv_ref[sl, hh * dv:(hh + 1) * dv] = dvv.astype(dv_ref.dtype)
        dc_sc[hh] = decay * d_c + _dot(qb, _bf(wdnum), TN)
        dn_sc[hh] = decay * d_n + _colsum(q * wdden)
        return dgate

    sd = jax.ShapeDtypeStruct
    nh = M_HEADS
    rc = lambda c: nst - 1 - c
    return _pcall(
        body, name="mlstm_bwd", grid=(bsz, nst),
        in_specs=[pl.BlockSpec(memory_space=pl.ANY),
                  pl.BlockSpec((None, rows, nh * dk), lambda b, c: (b, rc(c), 0)),
                  pl.BlockSpec((None, rows, nh * dk), lambda b, c: (b, rc(c), 1)),
                  pl.BlockSpec((None, rows, nh * dv), lambda b, c: (b, rc(c), V_OFF // (nh * dv))),
                  pl.BlockSpec((None, rows, 128), lambda b, c: (b, rc(c), G_OFF // 128)),
                  pl.BlockSpec((None, rows, nh * dv), lambda b, c: (b, rc(c), 0)),
                  pl.BlockSpec((None, nh, cps, dk, dv), lambda b, c: (b, 0, rc(c), 0, 0)),
                  pl.BlockSpec((None, nh, cps, 1, dk), lambda b, c: (b, 0, rc(c), 0, 0)),
                  pl.BlockSpec((None, nh, cps, 1, 128), lambda b, c: (b, 0, rc(c), 0, 0))],
        out_specs=[pl.BlockSpec((None, rows, nh * dv), lambda b, c: (b, rc(c), V_OFF // (nh * dv))),
                   pl.BlockSpec((None, rows, 2 * nh * dk), lambda b, c: (b, rc(c), 0)),
                   pl.BlockSpec((None, rows, 128), lambda b, c: (b, rc(c), 0))],
        out_shape=[sd(dp3.shape, dp3.dtype), sd((bsz, lp, 2 * nh * dk), F32), sd((bsz, lp, 128), dp3.dtype)],
        scratch_shapes=[pltpu.VMEM((nh, dk, dv), F32), pltpu.VMEM((nh, 1, dk), F32)],
        input_output_aliases={0: 0},
        compiler_params=_cp(("arbitrary", "arbitrary")),
    )(dp3, qk3, qk3, p3, p3, dh3, cs, ns, ms)


def _headnorm(x):
    dv = x.shape[1] // M_HEADS
    xh, rs = [], []
    for h in range(M_HEADS):
        xx = x[:, h * dv:(h + 1) * dv]
        mu = jnp.mean(xx, axis=-1, keepdims=True)
        xc = xx - mu
        rstd = lax.rsqrt(jnp.mean(xc * xc, axis=-1, keepdims=True) + LN_EPS)
        xh.append(xc * rstd)
        rs.append(rstd)
    return jnp.concatenate(xh, axis=1), rs


def _mix_fwd(hm, p, ys5g, h0, gn, wmo_bf, wo_bf, g1, b1, lp):
    r, d = hm.shape
    tm = _row_tile(lp, 208)

    def body(hm_ref, o_ref, gs_ref, gm_ref, ys_ref, h0_ref, gn_ref, wmo_ref, wo_ref, g1_ref, b1_ref,
             ymin_ref, ym_ref, mix_ref, r1_ref, h1_ref):
        xhat, _ = _headnorm(hm_ref[...])
        ymin = _bf(_sig(o_ref[...]) * (xhat * gn_ref[...]))
        ymin_ref[...] = ymin
        ym = _dot(ymin, wmo_ref[...])
        ym_ref[...] = ym
        mix = _bf(_sig(gs_ref[...]) * ys_ref[...] + _sig(gm_ref[...]) * ym)
        mix_ref[...] = mix
        r1 = ALPHA * h0_ref[...] + _dot(mix, wo_ref[...])
        r1_ref[...] = r1
        h1, _, _ = _ln_fwd(r1, g1_ref[...], b1_ref[...])
        h1_ref[...] = h1

    sd = jax.ShapeDtypeStruct
    row = pl.BlockSpec((tm, d), lambda i: (i, 0))
    return _pcall(
        body, name="mix_fwd", grid=(r // tm,),
        in_specs=[row, pl.BlockSpec((tm, d), lambda i: (i, O_OFF // d)), pl.BlockSpec((tm, d), lambda i: (i, GS_OFF // d)),
                  pl.BlockSpec((tm, d), lambda i: (i, GM_OFF // d)), row, row, _const((1, d)),
                  _resident((d, d)), _resident((d, d)), _const((1, d)), _const((1, d))],
        out_specs=[row] * 5,
        out_shape=[sd((r, d), BF16), sd((r, d), F32), sd((r, d), BF16), sd((r, d), F32), sd((r, d), F32)],
        compiler_params=_cp(("parallel",), 48),
    )(hm, p, p, p, ys5g, h0, gn, wmo_bf, wo_bf, g1, b1)


def _mix_bwd(dh1, r1, g1, wo_bf, wmo_bf, p, ys5g, ym, hm, gn, lp):
    r, d = hm.shape
    tm = _row_tile(lp, 208)
    dv = d // M_HEADS

    def body(dh1_ref, r1_ref, g1_ref, wo_ref, wmo_ref, o_ref, gs_ref, gm_ref, ys_ref, ym_ref, hm_ref, gn_ref,
             dr1_ref, dp_ref, dys_ref, dym_ref, dhm_ref, dg1_ref, db1_ref, dgn_ref):
        i = pl.program_id(0)

        @pl.when(i == 0)
        def _():
            dg1_ref[...] = jnp.zeros_like(dg1_ref)
            db1_ref[...] = jnp.zeros_like(db1_ref)
            dgn_ref[...] = jnp.zeros_like(dgn_ref)

        dh1 = dh1_ref[...]
        _, xhat1, rstd1 = _ln_fwd(r1_ref[...], g1_ref[...], 0.0)
        dr1 = _ln_bwd(dh1, xhat1, rstd1, g1_ref[...])
        dr1_ref[...] = dr1
        dg1_ref[...] += _colsum(dh1 * xhat1)
        db1_ref[...] += _colsum(dh1)
        dmix = _dot(_bf(dr1), wo_ref[...], NT)
        sgs, sgm, so = _sig(gs_ref[...]), _sig(gm_ref[...]), _sig(o_ref[...])
        dys_ref[...] = dmix * sgs
        dp_ref[:, d:2 * d] = _bf(dmix * ys_ref[...] * sgs * (1.0 - sgs))
        dym = dmix * sgm
        dym_ref[...] = _bf(dym)
        dp_ref[:, 2 * d:3 * d] = _bf(dmix * ym_ref[...] * sgm * (1.0 - sgm))
        dymin = _dot(_bf(dym), wmo_ref[...], NT)
        xhat, rs = _headnorm(hm_ref[...])
        gn_ = gn_ref[...]
        dp_ref[:, 0:d] = _bf(dymin * (xhat * gn_) * so * (1.0 - so))
        dhn = dymin * so
        dgn_ref[...] += _colsum(dhn * xhat)
        dxh = dhn * gn_
        for h in range(M_HEADS):
            sl = slice(h * dv, (h + 1) * dv)
            a, xh = dxh[:, sl], xhat[:, sl]
            m1 = jnp.mean(a, axis=-1, keepdims=True)
            m2 = jnp.mean(a * xh, axis=-1, keepdims=True)
            dhm_ref[:, sl] = rs[h] * (a - m1 - xh * m2)

    sd = jax.ShapeDtypeStruct
    row = pl.BlockSpec((tm, d), lambda i: (i, 0))
    vec = _const((1, d))
    return _pcall(
        body, name="mix_bwd", grid=(r // tm,),
        in_specs=[row, row, vec, _resident((d, d)), _resident((d, d)),
                  pl.BlockSpec((tm, d), lambda i: (i, O_OFF // d)), pl.BlockSpec((tm, d), lambda i: (i, GS_OFF // d)),
                  pl.BlockSpec((tm, d), lambda i: (i, GM_OFF // d)), row, row, row, vec],
        out_specs=[row, pl.BlockSpec((tm, 3 * d), lambda i: (i, 0)), row, row, row, vec, vec, vec],
        out_shape=[sd((r, d), F32), sd((r, NP), BF16), sd((r, d), F32), sd((r, d), BF16), sd((r, d), F32),
                   sd((1, d), F32), sd((1, d), F32), sd((1, d), F32)],
        compiler_params=_cp(("arbitrary",), 48),
    )(dh1, r1, g1, wo_bf, wmo_bf, p, p, p, ys5g, ym, hm, gn)


def _mlp_fwd(h1, tgt, wup_g, wdn_bf, bup, g2, b2, lp):
    r, d = h1.shape
    tm = _row_tile(lp, 352)
    tps = lp // tm
    nf = wup_g.shape[0]

    def body(h1_ref, t_ref, wup_ref, wdn_ref, bup_ref, g2_ref, b2_ref, dr2_ref, act_ref, loss_ref, dg2_ref, db2_ref):
        i = pl.program_id(0)

        @pl.when(i == 0)
        def _():
            loss_ref[...] = jnp.zeros_like(loss_ref)
            dg2_ref[...] = jnp.zeros_like(dg2_ref)
            db2_ref[...] = jnp.zeros_like(db2_ref)

        h1 = h1_ref[...]
        h1b = _bf(h1)
        ff = jnp.zeros((tm, d), F32)
        for s in range(nf):
            up = _dot(h1b, wup_ref[s]) + bup_ref[:, s * d:(s + 1) * d]
            a = jnp.maximum(up, 0.0)
            a = _bf(a * a)
            act_ref[:, s * d:(s + 1) * d] = a
            ff = ff + _dot(a, wdn_ref[s * d:(s + 1) * d, :])
        r2 = ALPHA * h1 + ff
        g2 = g2_ref[...]
        y, xhat, rstd = _ln_fwd(r2, g2, b2_ref[...])
        t = (i % tps) * tm + lax.broadcasted_iota(jnp.int32, (tm, 1), 0)
        diff = jnp.where(t >= PAD + N_META, y - t_ref[...], 0.0)
        loss_ref[...] += 0.5 / d * jnp.sum(jnp.sum(diff * diff, axis=1, keepdims=True), axis=0, keepdims=True)
        dy = diff * (1.0 / d)
        dg2_ref[...] += _colsum(dy * xhat)
        db2_ref[...] += _colsum(dy)
        dr2_ref[...] = _ln_bwd(dy, xhat, rstd, g2)

    sd = jax.ShapeDtypeStruct
    row = pl.BlockSpec((tm, d), lambda i: (i, 0))
    vec = _const((1, d))
    return _pcall(
        body, name="mlp_fwd", grid=(r // tm,),
        in_specs=[row, row, _resident(wup_g.shape), _resident(wdn_bf.shape), _const((1, nf * d)), vec, vec],
        out_specs=[row, pl.BlockSpec((tm, nf * d), lambda i: (i, 0)), _const((1, 128)), vec, vec],
        out_shape=[sd((r, d), F32), sd((r, nf * d), BF16), sd((1, 128), F32), sd((1, d), F32), sd((1, d), F32)],
        compiler_params=_cp(("arbitrary",), 56),
    )(h1, tgt, wup_g, wdn_bf, bup, g2, b2)


def _mlp_bwd(h1, dr2, wup_g, wdn_bf, bup, lp):
    r, d = h1.shape
    tm = _row_tile(lp, 352)
    nf = wup_g.shape[0]

    def body(h1_ref, dr2_ref, wup_ref, wdn_ref, bup_ref, dh1_ref, dup_ref, dbup_ref):
        i = pl.program_id(0)

        @pl.when(i == 0)
        def _():
            dbup_ref[...] = jnp.zeros_like(dbup_ref)

        h1b = _bf(h1_ref[...])
        dr2 = dr2_ref[...]
        dr2b = _bf(dr2)
        acc = ALPHA * dr2
        for s in range(nf):
            up = _dot(h1b, wup_ref[s]) + bup_ref[:, s * d:(s + 1) * d]
            dact = _dot(dr2b, wdn_ref[s * d:(s + 1) * d, :], NT)
            dup = dact * (2.0 * jnp.maximum(up, 0.0))
            dbup_ref[:, s * d:(s + 1) * d] += _colsum(dup)
            dupb = _bf(dup)
            dup_ref[:, s * d:(s + 1) * d] = dupb
            acc = acc + _dot(dupb, wup_ref[s], NT)
        dh1_ref[...] = acc

    sd = jax.ShapeDtypeStruct
    row = pl.BlockSpec((tm, d), lambda i: (i, 0))
    return _pcall(
        body, name="mlp_bwd", grid=(r // tm,),
        in_specs=[row, row, _resident(wup_g.shape), _resident(wdn_bf.shape), _const((1, nf * d))],
        out_specs=[row, pl.BlockSpec((tm, nf * d), lambda i: (i, 0)), _const((1, nf * d))],
        out_shape=[sd((r, d), F32), sd((r, nf * d), BF16), sd((1, nf * d), F32)],
        compiler_params=_cp(("arbitrary",), 56),
    )(h1, dr2, wup_g, wdn_bf, bup)


def _s5_block_mats(bb_re_t, bb_im_t, c_re, c_im, ap_re, ap_im):
    ng = c_re.shape[0]
    gl = ng // S5_KCH
    eye = jnp.eye(gl, dtype=F32)

    def bmat(bt):
        bb = jnp.transpose(bt, (1, 0, 2)).reshape(S5_KCH, gl, S5_GROUP, S5_STATE)
        return jnp.einsum("kghp,gj->kghjp", bb, eye).reshape(S5_KCH, gl * S5_GROUP, gl * S5_STATE)

    def cmat(c):
        cc = c.reshape(S5_KCH, gl, S5_GROUP, S5_STATE)
        return jnp.einsum("kghp,gj->kjpgh", cc, eye).reshape(S5_KCH, gl * S5_STATE, gl * S5_GROUP)

    def pw(a):
        return jnp.transpose(a.reshape(8, S5_KCH, gl * S5_STATE), (1, 0, 2))

    bk = jnp.concatenate([bmat(bb_re_t), bmat(bb_im_t)], axis=-1)
    apow = jnp.concatenate([pw(ap_re), pw(ap_im)], axis=-1)
    return _bf(bk), _bf(cmat(c_re)), _bf(cmat(c_im)), apow


def _s5_block_grads(dbk, dcre, dcim, da):
    gl = dbk.shape[1] // S5_GROUP
    ng = gl * S5_KCH
    eye = jnp.eye(gl, dtype=F32)
    hw = gl * S5_STATE

    def bpart(x):
        x = x.reshape(S5_KCH, gl, S5_GROUP, gl, S5_STATE)
        x = jnp.einsum("kghjp,gj->kghp", x, eye).reshape(ng, S5_GROUP, S5_STATE)
        return jnp.transpose(x, (1, 0, 2))

    def cpart(x):
        x = x.reshape(S5_KCH, gl, S5_STATE, gl, S5_GROUP)
        return jnp.einsum("kjpgh,gj->kghp", x, eye).reshape(ng, S5_GROUP, S5_STATE)

    return (bpart(dbk[..., :hw]), bpart(dbk[..., hw:]), cpart(dcre), cpart(dcim),
            da[:, 0, :hw].reshape(ng, S5_STATE), da[:, 0, hw:].reshape(ng, S5_STATE))


def _tie(a, tok):
    return a if tok is None else a + tok[0, 0]


def _local_step(x, tgt, w, early=None, late=None, ready=None):
    ready = ready or (lambda names, g: None)
    bsz, seq, d = x.shape
    lp = PAD + N_META + seq
    r = bsz * lp
    meta = jnp.broadcast_to(w["meta_tokens"][None], (bsz, N_META, d))
    hin = jnp.concatenate([jnp.zeros((bsz, PAD, d), F32), meta, x], axis=1).reshape(r, d)
    tgtp = jnp.concatenate([jnp.zeros((bsz, PAD + N_META, d), F32), tgt], axis=1).reshape(r, d)

    h0, h0b = _ln0_fwd(hin, w["ln0_g"], w["ln0_b"], lp)
    if early is not None:
        w = {**w, **early(h0)}
    p = _inproj(h0b, w["w_in"], w["b_in"], lp)
    p3 = p.reshape(bsz, lp, NP)

    b_re_t = jnp.transpose(w["s5_b_re"], (2, 0, 1))
    b_im_t = jnp.transpose(w["s5_b_im"], (2, 0, 1))
    ap_re, ap_im, bb_re_t, bb_im_t = _s5_prep(w["s5_lambda_re"], w["s5_lambda_im"], w["s5_log_dt"], b_re_t, b_im_t)
    bk, cre, cim, apow = _s5_block_mats(bb_re_t, bb_im_t, w["s5_c_re"], w["s5_c_im"], ap_re, ap_im)
    y_s5, xs = _s5_fwd(p3, bk, cre, cim, apow, w["s5_d"])
    sw = y_s5.shape[-1]
    if late is not None:
        w = {**w, **late(y_s5)}
    gy, z, ys5g = _glu_fwd(y_s5.reshape(r, sw), w["s5_w_glu"], lp)

    pre3, qk3 = _conv_fwd(p3, w["qk_conv_w"], w["qk_conv_b"])
    hm3, cs, ns, ms = _mlstm_fwd(qk3, p3)
    hm = hm3.reshape(r, d)
    ymin, ym, mix, r1, h1 = _mix_fwd(hm, p, ys5g, h0, w["m_norm_g"], w["m_w_out"], w["w_o"], w["ln1_g"], w["ln1_b"], lp)
    dr2, act, loss, dg2, db2 = _mlp_fwd(h1, tgtp, w["w_up"], w["w_down"], w["b_up"], w["ln2_g"], w["ln2_b"], lp)

    g = {"ln2_g": dg2, "ln2_b": db2}
    dh1, dup, g["b_up"] = _mlp_bwd(h1, dr2, w["w_up"], w["w_down"], w["b_up"], lp)
    g["w_down"] = _mm_tn(act, dr2, name="dw_down")
    g["w_up"] = _mm_tn(h1, dup, name="dw_up", split=w["w_up"].shape[0])
    tok = ready(("w_down", "w_up"), g)
    dr1, dp, dys5g, dym, dhm, g["ln1_g"], g["ln1_b"], g["m_norm_g"] = _mix_bwd(
        dh1, r1, _tie(w["ln1_g"], tok), w["w_o"], w["m_w_out"], p, ys5g, ym, hm, w["m_norm_g"], lp)
    g["w_o"] = _mm_tn(mix, dr1, name="dw_o")
    g["m_w_out"] = _mm_tn(ymin, dym, name="dw_mout")

    dp3 = dp.reshape(bsz, lp, NP)
    dp3, dqk3, dgate = _mlstm_bwd(dp3, qk3, p3, dhm.reshape(bsz, lp, d), cs, ns, ms)
    dp3, g["qk_conv_w"], g["qk_conv_b"] = _conv_bwd(dp3, p3, dqk3, pre3, w["qk_conv_w"])
    dz, dys5 = _glu_bwd(dys5g, z, y_s5.reshape(r, sw), w["s5_w_glu"], lp)
    g["s5_w_glu"] = _mm_tn(gy, dz, name="dw_glu", split=w["s5_w_glu"].shape[0])
    tok = ready(("s5_w_glu", "m_w_out", "w_o"), g)
    apow_rev = jnp.flip(apow, axis=1)
    dp3, dbk, dcre, dcim, da, g["s5_d"] = _s5_bwd(dp3, p3, dys5.reshape(bsz, lp, sw), xs, bk, cre, cim, apow_rev,
                                                 _tie(w["s5_d"], tok))
    dbb_re_t, dbb_im_t, g["s5_c_re"], g["s5_c_im"], da_re, da_im = _s5_block_grads(dbk, dcre, dcim, da)
    g["s5_lambda_re"], g["s5_lambda_im"], g["s5_log_dt"], gb_re_t, gb_im_t = _s5_prep_bwd(
        w["s5_lambda_re"], w["s5_lambda_im"], w["s5_log_dt"], b_re_t, b_im_t, da_re, da_im, dbb_re_t, dbb_im_t)
    g["s5_b_re"] = jnp.transpose(gb_re_t, (1, 2, 0))
    g["s5_b_im"] = jnp.transpose(gb_im_t, (1, 2, 0))

    dp3 = lax.dynamic_update_slice(dp3, dgate, (0, 0, G_OFF))
    dp = dp3.reshape(r, NP)
    g["w_in"], g["b_in"] = _mm_tn(h0b, dp, name="dw_in", colsum=True)
    tok = ready(("w_in",), g)
    dpw = _mm_nt(dp, w["w_in"], lp, name="dh0", dep=tok)
    dhin, g["ln0_g"], g["ln0_b"], g["meta_tokens"] = _ln0_bwd(hin, dr1, dpw, w["ln0_g"], lp)
    grad_x = dhin.reshape(bsz, lp, d)[:, PAD + N_META:]
    return loss, grad_x, g


_ANY = pl.BlockSpec(memory_space=pl.ANY)
_MESH = pl.DeviceIdType.MESH


def _place():
    return lax.axis_index("x"), lax.axis_index("y"), lax.axis_index("c")


def _gather_chips(shards):
    n = len(shards)

    def body(*refs):
        ins, outs = refs[:n], refs[n:2 * n]
        send, recv, loc = refs[2 * n:]
        x, y, c = _place()
        me = 2 * x + y
        peers = [(1 - x, y), (x, 1 - y), (1 - x, 1 - y)]

        def rc(a, k, slot):
            px, py = peers[k]
            return pltpu.make_async_remote_copy(src_ref=ins[a], dst_ref=outs[a].at[slot], send_sem=send.at[a, k],
                                                recv_sem=recv.at[a, k], device_id=(px, py, c), device_id_type=_MESH)

        own = [pltpu.make_async_copy(ins[a], outs[a].at[me], loc.at[a]) for a in range(n)]
        for cp in own:
            cp.start()
        out = [rc(a, k, me) for a in range(n) for k in range(3)]
        for cp in out:
            cp.start()
        for a in range(n):
            for k in range(3):
                rc(a, k, 2 * peers[k][0] + peers[k][1]).wait_recv()
        for cp in out:
            cp.wait_send()
        for cp in own:
            cp.wait()

    return _pcall(
        body, name="gather_chips", in_specs=[_ANY] * n, out_specs=[_ANY] * n,
        out_shape=[jax.ShapeDtypeStruct((4,) + s.shape, s.dtype) for s in shards],
        scratch_shapes=[pltpu.SemaphoreType.DMA((n, 3)), pltpu.SemaphoreType.DMA((n, 3)), pltpu.SemaphoreType.DMA((n,))],
    )(*shards)


_HBM = pl.BlockSpec(memory_space=pltpu.HBM)
_SEM = pl.BlockSpec(memory_space=pltpu.SEMAPHORE)
_EFFECT = pltpu.SideEffectType.DATAFLOW_SIDE_EFFECTING


def _xchg_copies(srcs, lands, send, recv, scatter):
    x, y, c = _place()
    me = 2 * x + y
    peers = [(1 - x, y), (x, 1 - y), (1 - x, 1 - y)]
    out = []
    for a in range(len(srcs)):
        for k, (px, py) in enumerate(peers):
            src = srcs[a].at[2 * px + py] if scatter else srcs[a]
            dst = lands[a].at[k] if scatter else lands[a].at[me]
            out.append(pltpu.make_async_remote_copy(src_ref=src, dst_ref=dst, send_sem=send.at[3 * a + k],
                                                    recv_sem=recv.at[3 * a + k], device_id=(px, py, c),
                                                    device_id_type=_MESH))
    return out


def _xchg_start(srcs, lands, *, name, scatter, dep=None):
    n = len(srcs)
    deps = [] if dep is None else [dep]
    nd = len(deps)

    def body(*refs):
        send, recv = refs[2 * n + nd], refs[2 * n + nd + 1]
        for cp in _xchg_copies(refs[:n], refs[n:2 * n], send, recv, scatter):
            cp.start()
        refs[-1][...] = jnp.zeros_like(refs[-1])

    hbm = lambda a: pltpu.HBM(a.shape, a.dtype)
    con = lambda a: pltpu.with_memory_space_constraint(a, pltpu.HBM)
    res = _pcall(
        body, name=name, in_specs=[_HBM] * (2 * n) + [_ANY] * nd,
        out_specs=[_SEM, _SEM] + [_HBM] * (2 * n) + [pl.BlockSpec(memory_space=pltpu.VMEM)],
        out_shape=[pltpu.SemaphoreType.DMA((3 * n,)), pltpu.SemaphoreType.DMA((3 * n,))]
        + [hbm(a) for a in srcs] + [hbm(a) for a in lands] + [jax.ShapeDtypeStruct((8, 128), F32)],
        input_output_aliases={i: 2 + i for i in range(2 * n)},
        compiler_params=pltpu.CompilerParams(has_side_effects=_EFFECT),
    )(*[con(a) for a in srcs], *[con(a) for a in lands], *deps)
    return res[0], res[1], list(res[2:2 + n]), list(res[2 + n:2 + 2 * n]), res[-1]


def _xchg_wait(send, recv, srcs, lands, after, *, name, scatter):
    n = len(srcs)

    def body(*refs):
        s_ref, r_ref = refs[2 * n], refs[2 * n + 1]
        for cp in _xchg_copies(refs[:n], refs[n:2 * n], s_ref, r_ref, scatter):
            cp.wait_send()
            cp.wait_recv()

    hbm = lambda a: pltpu.HBM(a.shape, a.dtype)
    res = _pcall(
        body, name=name, in_specs=[_HBM] * (2 * n) + [_SEM, _SEM, _ANY],
        out_specs=[_HBM] * (2 * n),
        out_shape=[hbm(a) for a in srcs] + [hbm(a) for a in lands],
        input_output_aliases={i: i for i in range(2 * n)},
        compiler_params=pltpu.CompilerParams(has_side_effects=_EFFECT),
    )(*srcs, *lands, send, recv, after)
    return list(res[:n]), list(res[n:])


def _swap_cores(arrs, name="swap_cores"):
    n = len(arrs)

    def body(*refs):
        ins, outs = refs[:n], refs[n:2 * n]
        send, recv = refs[2 * n:]
        x, y, c = _place()
        cps = [pltpu.make_async_remote_copy(src_ref=ins[a], dst_ref=outs[a], send_sem=send.at[a], recv_sem=recv.at[a],
                                            device_id=(x, y, 1 - c), device_id_type=_MESH) for a in range(n)]
        for cp in cps:
            cp.start()
        for cp in cps:
            cp.wait_recv()
        for cp in cps:
            cp.wait_send()

    return _pcall(
        body, name=name, in_specs=[_ANY] * n, out_specs=[_ANY] * n,
        out_shape=[jax.ShapeDtypeStruct(s.shape, s.dtype) for s in arrs],
        scratch_shapes=[pltpu.SemaphoreType.DMA((n,)), pltpu.SemaphoreType.DMA((n,))],
    )(*arrs)


def _allreduce_small(v):
    rows = v.shape[0]
    half = rows // 2
    assert half % 8 == 0 and 2 * half == rows

    def body(v_ref, out_ref, sib_ref, pair_ref, slots_ref, send, recv):
        x, y, c = _place()
        chip = 2 * x + y
        sibling = (x, y, 1 - c)
        peers = [(1 - x, y), (x, 1 - y), (1 - x, 1 - y)]
        mine = pl.ds(pl.multiple_of(c * half, 8), half)

        first = pltpu.make_async_remote_copy(src_ref=v_ref, dst_ref=sib_ref, send_sem=send.at[0], recv_sem=recv.at[0],
                                             device_id=sibling, device_id_type=_MESH)
        first.start()
        first.wait_recv()
        pair_ref[...] = v_ref[...] + sib_ref[...]
        slots_ref[chip] = pair_ref[mine, :]
        cross = [pltpu.make_async_remote_copy(src_ref=pair_ref.at[mine], dst_ref=slots_ref.at[chip],
                                              send_sem=send.at[1 + k], recv_sem=recv.at[1 + k],
                                              device_id=(px, py, c), device_id_type=_MESH)
                 for k, (px, py) in enumerate(peers)]
        for cp in cross:
            cp.start()
        for cp in cross:
            cp.wait_recv()
        out_ref[mine, :] = ((slots_ref[0] + slots_ref[1]) + slots_ref[2]) + slots_ref[3]
        last = pltpu.make_async_remote_copy(src_ref=out_ref.at[mine], dst_ref=out_ref.at[mine], send_sem=send.at[4],
                                            recv_sem=recv.at[4], device_id=sibling, device_id_type=_MESH)
        last.start()
        last.wait_recv()
        first.wait_send()
        for cp in cross:
            cp.wait_send()
        last.wait_send()

    vm = pl.BlockSpec(memory_space=pltpu.VMEM)
    return _pcall(
        body, name="allreduce_small", in_specs=[vm], out_specs=vm,
        out_shape=jax.ShapeDtypeStruct((rows, 128), F32),
        scratch_shapes=[pltpu.VMEM((rows, 128), F32), pltpu.VMEM((rows, 128), F32), pltpu.VMEM((4, half, 128), F32),
                        pltpu.SemaphoreType.DMA((5,)), pltpu.SemaphoreType.DMA((5,))],
        compiler_params=_cp(None, 40),
    )(v)


def _sum_slots(own, land):
    ns, rows, cols = land.shape
    tm = _row_tile(rows, 256, 8)

    def body(own_ref, a_ref, o_ref):
        o_ref[...] = ((own_ref[...] + a_ref[0]) + a_ref[1]) + a_ref[2]

    return _pcall(
        body, name="sum_slots", grid=(rows // tm,),
        in_specs=[pl.BlockSpec((tm, cols), lambda i: (i, 0)), pl.BlockSpec((ns, tm, cols), lambda i: (0, i, 0))],
        out_specs=pl.BlockSpec((tm, cols), lambda i: (i, 0)),
        out_shape=jax.ShapeDtypeStruct((rows, cols), F32),
        compiler_params=_cp(("parallel",), 40),
    )(own, land)


def _adamw(w, m, v, g0, g1=None):
    rows, cols = w.shape[-2:]
    lead = w.ndim == 3
    tm = _row_tile(rows, 256, 8)
    c1 = 1.0 - ADAM_B1 ** ADAM_STEP
    c2 = 1.0 - ADAM_B2 ** ADAM_STEP
    two = g1 is not None

    def body(*refs):
        w_ref, m_ref, v_ref, g0_ref = refs[:4]
        g_ref, d_ref, nm_ref, nv_ref = refs[-4:]
        g = g0_ref[...]
        if two:
            g = g + refs[4][...]
        nm = ADAM_B1 * m_ref[...] + (1.0 - ADAM_B1) * g
        nv = ADAM_B2 * v_ref[...] + (1.0 - ADAM_B2) * (g * g)
        g_ref[...] = g
        nm_ref[...] = nm
        nv_ref[...] = nv
        d_ref[...] = -ADAM_LR * ((nm / c1) / (jnp.sqrt(nv / c2) + ADAM_EPS) + ADAM_WD * w_ref[...])

    blk = pl.BlockSpec((tm, cols), lambda i: (i, 0))
    wblk = pl.BlockSpec((None, tm, cols), lambda i: (0, i, 0)) if lead else blk
    ins = [w, m, v, g0] + ([g1] if two else [])
    return _pcall(
        body, name="adamw", grid=(rows // tm,), in_specs=[wblk] * 3 + [blk] * (len(ins) - 3), out_specs=[wblk] * 4,
        out_shape=[jax.ShapeDtypeStruct(w.shape, F32)] * 4,
        compiler_params=_cp(("parallel",), 40),
    )(*ins)


_BIG = ("w_in", "s5_w_glu", "m_w_out", "w_o", "w_up", "w_down")
_SMALL = ("ln0_g", "ln0_b", "b_in", "qk_conv_b", "s5_lambda_re", "s5_lambda_im", "s5_log_dt", "s5_b_re", "s5_b_im",
          "s5_c_re", "s5_c_im", "s5_d", "m_norm_g", "ln1_g", "ln1_b", "b_up", "ln2_g", "ln2_b")
_SMALL_SHARDED = ("meta_tokens", "qk_conv_w")
_ORDER = ("meta_tokens", "ln0_g", "ln0_b", "w_in", "b_in", "qk_conv_w", "qk_conv_b", "s5_lambda_re", "s5_lambda_im",
          "s5_log_dt", "s5_b_re", "s5_b_im", "s5_c_re", "s5_c_im", "s5_d", "s5_w_glu", "m_norm_g", "m_w_out", "w_o",
          "ln1_g", "ln1_b", "w_up", "b_up", "w_down", "ln2_g", "ln2_b")


def _pack(arrs):
    flat = jnp.concatenate([a.reshape(-1) for a in arrs])
    n = flat.shape[0]
    rows = -(-n // 2048) * 16
    return jnp.pad(flat, (0, rows * 128 - n)).reshape(rows, 128)


def _unpack(packed, shapes):
    flat = packed.reshape(-1)
    out, off = [], 0
    for s in shapes:
        n = math.prod(s)
        out.append(flat[off:off + n].reshape(s))
        off += n
    return out


def kernel(x, meta_tokens, ln0_g, ln0_b, w_in, b_in, qk_conv_w, qk_conv_b, s5_lambda_re, s5_lambda_im, s5_log_dt, s5_b_re, s5_b_im, s5_c_re, s5_c_im, s5_d, s5_w_glu, m_norm_g, m_w_out, w_o, ln1_g, ln1_b, w_up, b_up, w_down, ln2_g, ln2_b, loss_target, m_meta_tokens, m_ln0_g, m_ln0_b, m_w_in, m_b_in, m_qk_conv_w, m_qk_conv_b, m_s5_lambda_re, m_s5_lambda_im, m_s5_log_dt, m_s5_b_re, m_s5_b_im, m_s5_c_re, m_s5_c_im, m_s5_d, m_s5_w_glu, m_m_norm_g, m_m_w_out, m_w_o, m_ln1_g, m_ln1_b, m_w_up, m_b_up, m_w_down, m_ln2_g, m_ln2_b, v_meta_tokens, v_ln0_g, v_ln0_b, v_w_in, v_b_in, v_qk_conv_w, v_qk_conv_b, v_s5_lambda_re, v_s5_lambda_im, v_s5_log_dt, v_s5_b_re, v_s5_b_im, v_s5_c_re, v_s5_c_im, v_s5_d, v_s5_w_glu, v_m_norm_g, v_m_w_out, v_w_o, v_ln1_g, v_ln1_b, v_w_up, v_b_up, v_w_down, v_ln2_g, v_ln2_b):
    wts = dict(meta_tokens=meta_tokens, ln0_g=ln0_g, ln0_b=ln0_b, w_in=w_in, b_in=b_in, qk_conv_w=qk_conv_w,
               qk_conv_b=qk_conv_b, s5_lambda_re=s5_lambda_re, s5_lambda_im=s5_lambda_im, s5_log_dt=s5_log_dt,
               s5_b_re=s5_b_re, s5_b_im=s5_b_im, s5_c_re=s5_c_re, s5_c_im=s5_c_im, s5_d=s5_d, s5_w_glu=s5_w_glu,
               m_norm_g=m_norm_g, m_w_out=m_w_out, w_o=w_o, ln1_g=ln1_g, ln1_b=ln1_b, w_up=w_up, b_up=b_up,
               w_down=w_down, ln2_g=ln2_g, ln2_b=ln2_b)
    mom = dict(meta_tokens=m_meta_tokens, ln0_g=m_ln0_g, ln0_b=m_ln0_b, w_in=m_w_in, b_in=m_b_in, qk_conv_w=m_qk_conv_w,
               qk_conv_b=m_qk_conv_b, s5_lambda_re=m_s5_lambda_re, s5_lambda_im=m_s5_lambda_im, s5_log_dt=m_s5_log_dt,
               s5_b_re=m_s5_b_re, s5_b_im=m_s5_b_im, s5_c_re=m_s5_c_re, s5_c_im=m_s5_c_im, s5_d=m_s5_d,
               s5_w_glu=m_s5_w_glu, m_norm_g=m_m_norm_g, m_w_out=m_m_w_out, w_o=m_w_o, ln1_g=m_ln1_g, ln1_b=m_ln1_b,
               w_up=m_w_up, b_up=m_b_up, w_down=m_w_down, ln2_g=m_ln2_g, ln2_b=m_ln2_b)
    var = dict(meta_tokens=v_meta_tokens, ln0_g=v_ln0_g, ln0_b=v_ln0_b, w_in=v_w_in, b_in=v_b_in, qk_conv_w=v_qk_conv_w,
               qk_conv_b=v_qk_conv_b, s5_lambda_re=v_s5_lambda_re, s5_lambda_im=v_s5_lambda_im, s5_log_dt=v_s5_log_dt,
               s5_b_re=v_s5_b_re, s5_b_im=v_s5_b_im, s5_c_re=v_s5_c_re, s5_c_im=v_s5_c_im, s5_d=v_s5_d,
               s5_w_glu=v_s5_w_glu, m_norm_g=v_m_norm_g, m_w_out=v_m_w_out, w_o=v_w_o, ln1_g=v_ln1_g, ln1_b=v_ln1_b,
               w_up=v_w_up, b_up=v_b_up, w_down=v_w_down, ln2_g=v_ln2_g, ln2_b=v_ln2_b)
    d = x.shape[-1]
    chip = 2 * lax.axis_index("x") + lax.axis_index("y")

    gw = dict(zip(_SMALL_SHARDED, _gather_chips([meta_tokens, qk_conv_w[0]])))
    own_w_in = _bf(w_in[0])
    fsend, frecv, fsrc, fland, ftok = _xchg_start([own_w_in], [lax.empty((4,) + own_w_in.shape, BF16)],
                                                  name="gather_w_in_start", scatter=False, dep=gw["qk_conv_w"])
    late_names = tuple(n for n in _BIG if n != "w_in")
    cat = lambda a: jnp.transpose(a, (1, 0, 2)).reshape(a.shape[1], 4 * a.shape[2])
    w = dict(
        meta_tokens=cat(gw["meta_tokens"]), ln0_g=ln0_g[None], ln0_b=_tie(ln0_b[None], ftok),
        qk_conv_w=cat(gw["qk_conv_w"]), qk_conv_b=qk_conv_b,
        s5_lambda_re=s5_lambda_re[0], s5_lambda_im=s5_lambda_im[0], s5_log_dt=s5_log_dt[0][:, None],
        s5_b_re=s5_b_re[0], s5_b_im=s5_b_im[0], s5_c_re=s5_c_re[0], s5_c_im=s5_c_im[0], s5_d=s5_d,
        m_norm_g=m_norm_g, ln1_g=ln1_g, ln1_b=ln1_b, b_up=b_up, ln2_g=ln2_g, ln2_b=ln2_b)
    in_flight = {}

    def place_own(src, land):
        return lax.dynamic_update_slice(land, src[None], (chip,) + (0,) * src.ndim)

    def early(after):
        src, land = _xchg_wait(fsend, frecv, fsrc, fland, after, name="gather_w_in_wait", scatter=False)
        late_src = [_bf(wts[n][0]) for n in late_names]
        st = _xchg_start(late_src, [lax.empty((4,) + a.shape, a.dtype) for a in late_src], name="gather_late_start",
                         scatter=False, dep=src[0])
        in_flight["late"] = st[:4]
        return dict(w_in=_chunk_cols(_w_in_from_slots(place_own(src[0], land[0]))), b_in=_tie(_to_pad_cols(b_in), st[4]))

    def late(after):
        src, land = _xchg_wait(*in_flight["late"], after, name="gather_late_wait", scatter=False)
        full = {n: place_own(s, ld) for n, s, ld in zip(late_names, src, land)}
        return dict(s5_w_glu=full["s5_w_glu"], m_w_out=full["m_w_out"].reshape(d, d), w_o=full["w_o"].reshape(d, d),
                    w_up=full["w_up"], w_down=full["w_down"].reshape(4 * d, d))

    flying = []

    def ready(names, g):
        parts = dict(
            w_in=lambda: _slots_from_w_in(g["w_in"][0]), s5_w_glu=lambda: g["s5_w_glu"],
            m_w_out=lambda: g["m_w_out"].reshape(4, d // 4, d), w_o=lambda: g["w_o"].reshape(4, d // 4, d),
            w_up=lambda: g["w_up"], w_down=lambda: g["w_down"].reshape(4, d, d))
        src = [parts[n]() for n in names]
        land = [lax.empty((3,) + a.shape[1:], a.dtype) for a in src]
        st = _xchg_start(src, land, name="scatter_" + names[0] + "_start", scatter=True)
        flying.append((names,) + st[:4])
        return st[4]

    loss, grad_x, g = _local_step(x, loss_target, w, early, late, ready)
    g["b_in"] = _from_pad_cols(g["b_in"])

    res = {}

    def finish(groups, after, tag):
        mine = {}
        for names, send, recv, src, land in groups:
            src, land = _xchg_wait(send, recv, src, land, after, name="scatter_" + names[0] + "_wait", scatter=True)
            for n, s, ld in zip(names, src, land):
                mine[n] = _sum_slots(lax.dynamic_index_in_dim(s, chip, 0, keepdims=False), ld)
        theirs = _swap_cores(list(mine.values()), name="swap_cores_" + tag)
        for n, t in zip(mine, theirs):
            res[n] = _adamw(wts[n], mom[n], var[n], mine[n], t)

    finish(flying[:-1], g["ln0_g"], "a")

    small_shapes = [(1, 128)] + [wts[n].shape for n in _SMALL] + [g[n].shape for n in _SMALL_SHARDED]
    packed = _pack([loss] + [g[n] for n in _SMALL] + [g[n] for n in _SMALL_SHARDED])
    tot = _unpack(_allreduce_small(packed), small_shapes)
    loss_out = tot[0][0, 0]
    gsm = dict(zip(_SMALL + _SMALL_SHARDED, tot[1:]))
    for n in _SMALL_SHARDED:
        cols = wts[n].shape[-1]
        gsm[n] = lax.dynamic_slice_in_dim(gsm[n], chip * cols, cols, axis=1).reshape(wts[n].shape)

    names = _SMALL + _SMALL_SHARDED
    shapes = [wts[n].shape for n in names]
    pk = lambda dct: _pack([dct[n] for n in names])
    small_out = _adamw(pk(wts), pk(mom), pk(var), pk(gsm))
    small_res = [_unpack(r, shapes) for r in small_out]
    for j, n in enumerate(names):
        res[n] = [small_res[q][j] for q in range(4)]
    finish(flying[-1:], small_out[0], "b")

    return (loss_out, grad_x, *[res[n][0] for n in _ORDER], *[res[n][1] for n in _ORDER],
            *[res[n][2] for n in _ORDER], *[res[n][3] for n in _ORDER])
```

```python
import functools
import math

import jax
import jax.numpy as jnp
from jax import lax
from jax.experimental import pallas as pl
from jax.experimental.pallas import tpu as pltpu

F32 = jnp.float32
BF16 = jnp.bfloat16
HI = lax.Precision.HIGHEST

N_META = 16
M_HEADS = 4
M_CHUNK = 128
PAD = M_CHUNK - N_META
CONV_W = 4
S5_GROUP = 16
S5_STATE = 64
S5_KCH = 4
LN_EPS = 1e-5
ALPHA = 2.0 ** 0.25
NEG = -1e30
ADAM_LR, ADAM_B1, ADAM_B2, ADAM_EPS, ADAM_WD, ADAM_STEP = 0.001, 0.9, 0.999, 1e-08, 0.01, 10

O_OFF, GS_OFF, GM_OFF, V_OFF, Q_OFF, K_OFF, U_OFF, G_OFF, NP = 0, 1024, 2048, 3072, 4096, 4608, 5120, 5632, 5760

NN = ((1,), (0,))
NT = ((1,), (1,))
TN = ((0,), (0,))


def _dot(a, b, dims=NN, prec=None):
    return lax.dot_general(a, b, (dims, ((), ())), preferred_element_type=F32, precision=prec)


def _bf(x):
    return x.astype(BF16)


def _sig(x):
    return 0.5 * jnp.tanh(0.5 * x) + 0.5


def _pcall(body, **kw):
    return pl.pallas_call(body, **kw)


def _cp(sem=None, vmem_mb=None):
    kw = {}
    if sem is not None:
        kw["dimension_semantics"] = sem
    if vmem_mb is not None:
        kw["vmem_limit_bytes"] = vmem_mb << 20
    return pltpu.CompilerParams(**kw)


def _row_tile(n, want, mult=16):
    best = None
    for t in range(mult, want + 1, mult):
        if n % t == 0:
            best = t
    assert best is not None, (n, want)
    return best


def _resident(shape):
    nd = len(shape)
    return pl.BlockSpec(shape, lambda *_: (0,) * nd, pipeline_mode=pl.Buffered(1))


def _const(shape):
    nd = len(shape)
    return pl.BlockSpec(shape, lambda *_: (0,) * nd)


def _ln_fwd(x, g, b):
    mu = jnp.mean(x, axis=-1, keepdims=True)
    xc = x - mu
    var = jnp.mean(xc * xc, axis=-1, keepdims=True)
    rstd = lax.rsqrt(var + LN_EPS)
    xhat = xc * rstd
    return xhat * g + b, xhat, rstd


def _ln_bwd(dy, xhat, rstd, g):
    dxh = dy * g
    m1 = jnp.mean(dxh, axis=-1, keepdims=True)
    m2 = jnp.mean(dxh * xhat, axis=-1, keepdims=True)
    return rstd * (dxh - m1 - xhat * m2)


def _colsum(x):
    return jnp.sum(x, axis=0, keepdims=True)


def _to_pad_cols(w):
    u, q, k, v, o, gi, gf, gs, gm = (w[..., 0:512], w[..., 512:1024], w[..., 1024:1536], w[..., 1536:2560],
                                     w[..., 2560:3584], w[..., 3584:3588], w[..., 3588:3592], w[..., 3592:4616],
                                     w[..., 4616:5640])
    z = jnp.zeros(w.shape[:-1] + (NP - G_OFF - 8,), w.dtype)
    return jnp.concatenate([o, gs, gm, v, q, k, u, gi, gf, z], axis=-1)


def _from_pad_cols(w):
    o, gs, gm, v, q, k, u = (w[..., O_OFF:GS_OFF], w[..., GS_OFF:GM_OFF], w[..., GM_OFF:V_OFF], w[..., V_OFF:Q_OFF],
                             w[..., Q_OFF:K_OFF], w[..., K_OFF:U_OFF], w[..., U_OFF:G_OFF])
    gi, gf = w[..., G_OFF:G_OFF + 4], w[..., G_OFF + 4:G_OFF + 8]
    return jnp.concatenate([u, q, k, v, o, gi, gf, gs, gm], axis=-1)


_IN_REF = (("u", 512), ("q", 512), ("k", 512), ("v", 1024), ("o", 1024), ("i", 4), ("f", 4), ("gs", 1024), ("gm", 1024))
_IN_PAD = (("o", O_OFF), ("gs", GS_OFF), ("gm", GM_OFF), ("v", V_OFF), ("q", Q_OFF), ("k", K_OFF), ("u", U_OFF),
           ("i", G_OFF), ("f", G_OFF + 4))


def _in_ref_ranges():
    out, off = {}, 0
    for n, s in _IN_REF:
        out[n] = (off, off + s)
        off += s
    return out, off


def _w_in_from_slots(g, chunk=None):
    rng, total = _in_ref_ranges()
    width = total // g.shape[0]
    cols = []
    for n, _ in _IN_PAD:
        a, b = rng[n]
        while a < b:
            s = a // width
            e = min(b, (s + 1) * width)
            cols.append(g[s][:, a - s * width:e - s * width])
            a = e
    cols.append(jnp.zeros((g.shape[1], NP - G_OFF - 8), g.dtype))
    if chunk is None:
        return jnp.concatenate(cols, axis=1)
    chunks, cur, room = [], [], chunk
    for c in cols:
        while c.shape[1] > 0:
            take = min(room, c.shape[1])
            cur.append(c[:, :take])
            c, room = c[:, take:], room - take
            if room == 0:
                chunks.append(jnp.concatenate(cur, axis=1))
                cur, room = [], chunk
    assert not cur
    return jnp.stack(chunks, axis=0)


def _slots_from_w_in(wp, nslot=4):
    rng, total = _in_ref_ranges()
    width = total // nslot
    pad_off = dict(_IN_PAD)
    slots = []
    for s in range(nslot):
        lo, hi = s * width, (s + 1) * width
        cols = []
        for n, _ in _IN_REF:
            a, b = rng[n]
            x0, x1 = max(a, lo), min(b, hi)
            if x0 < x1:
                cols.append(wp[:, pad_off[n] + x0 - a:pad_off[n] + x1 - a])
        slots.append(jnp.concatenate(cols, axis=1))
    return jnp.stack(slots, axis=0)


def _ln0_fwd(hin, g, b, lp):
    r, d = hin.shape
    tm = _row_tile(lp, 416)

    def body(x_ref, g_ref, b_ref, o_ref, ob_ref):
        y, _, _ = _ln_fwd(x_ref[...], g_ref[...], b_ref[...])
        o_ref[...] = y
        ob_ref[...] = _bf(y)

    row = pl.BlockSpec((tm, d), lambda i: (i, 0))
    return _pcall(
        body, name="ln0_fwd", grid=(r // tm,),
        in_specs=[row, _const((1, d)), _const((1, d))],
        out_specs=[row, row],
        out_shape=[jax.ShapeDtypeStruct((r, d), F32), jax.ShapeDtypeStruct((r, d), BF16)],
        compiler_params=_cp(("parallel",)),
    )(hin, g, b)


def _ln0_bwd(hin, dr1, dpw, g, lp):
    r, d = hin.shape
    tm = _row_tile(lp, 416)
    tps = lp // tm
    assert tm >= PAD + N_META

    def body(x_ref, a_ref, c_ref, g_ref, o_ref, dg_ref, db_ref, dm_ref):
        i = pl.program_id(0)

        @pl.when(i == 0)
        def _():
            dg_ref[...] = jnp.zeros_like(dg_ref)
            db_ref[...] = jnp.zeros_like(db_ref)
            dm_ref[...] = jnp.zeros_like(dm_ref)

        dy = ALPHA * a_ref[...] + c_ref[...]
        _, xhat, rstd = _ln_fwd(x_ref[...], g_ref[...], 0.0)
        dx = _ln_bwd(dy, xhat, rstd, g_ref[...])
        o_ref[...] = dx
        dg_ref[...] += _colsum(dy * xhat)
        db_ref[...] += _colsum(dy)

        @pl.when(i % tps == 0)
        def _():
            dm_ref[...] += dx[PAD:PAD + N_META, :]

    return _pcall(
        body, name="ln0_bwd", grid=(r // tm,),
        in_specs=[pl.BlockSpec((tm, d), lambda i: (i, 0))] * 3 + [_const((1, d))],
        out_specs=[pl.BlockSpec((tm, d), lambda i: (i, 0)), _const((1, d)), _const((1, d)), _const((N_META, d))],
        out_shape=[jax.ShapeDtypeStruct((r, d), F32), jax.ShapeDtypeStruct((1, d), F32),
                   jax.ShapeDtypeStruct((1, d), F32), jax.ShapeDtypeStruct((N_META, d), F32)],
        compiler_params=_cp(("arbitrary",)),
    )(hin, dr1, dpw, g)


IN_CHUNK = 1152


def _chunk_cols(w):
    k, n = w.shape
    return jnp.transpose(w.reshape(k, n // IN_CHUNK, IN_CHUNK), (1, 0, 2))


def _inproj(h0b, w3, bias, lp):
    r, d = h0b.shape
    nj, _, tn = w3.shape
    tm = _row_tile(lp, 832)
    tps = lp // tm

    def body(a_ref, w_ref, b_ref, o_ref):
        i = pl.program_id(0)
        j = pl.program_id(1)
        acc = _dot(a_ref[...], w_ref[j]) + b_ref[...]
        t = (i % tps) * tm + lax.broadcasted_iota(jnp.int32, (tm, 1), 0)
        o_ref[...] = jnp.where(t >= PAD, acc, 0.0)

    return _pcall(
        body, name="inproj", grid=(r // tm, nj),
        in_specs=[pl.BlockSpec((tm, d), lambda i, j: (i, 0)), _resident(w3.shape),
                  pl.BlockSpec((1, tn), lambda i, j: (0, j))],
        out_specs=pl.BlockSpec((tm, tn), lambda i, j: (i, j)),
        out_shape=jax.ShapeDtypeStruct((r, nj * tn), F32),
        compiler_params=_cp(("parallel", "arbitrary"), 48),
    )(h0b, w3, bias)


def _mm_tn(a, b, *, name, split=1, colsum=False, tk_want=1408):
    r, m = a.shape
    n = b.shape[1]
    tk = _row_tile(r, tk_want)
    tm = min(m, 1024)
    ns = n // split
    tn = ns
    for cand in (1024, 1152, 640, 512, 128):
        if ns % cand == 0 and cand <= ns:
            tn = cand
            break
    nb = ns // tn
    nk = r // tk

    def body(a_ref, b_ref, o_ref, *rest):
        acc = rest[-1]
        k = pl.program_id(2)

        @pl.when(k == 0)
        def _():
            acc[...] = jnp.zeros_like(acc)

        bt = b_ref[...]
        acc[...] += _dot(_bf(a_ref[...]), _bf(bt), TN)

        @pl.when(k == nk - 1)
        def _():
            o_ref[...] = acc[...]

        if colsum:
            cs_ref = rest[0]

            @pl.when(k == 0)
            def _():
                cs_ref[...] = jnp.zeros_like(cs_ref)

            cs_ref[...] += _colsum(bt.astype(F32))

    out_specs = [pl.BlockSpec((None, tm, tn), lambda i, j, k: (j // nb, i, j % nb))]
    out_shape = [jax.ShapeDtypeStruct((split, m, ns), F32)]
    if colsum:
        assert m == tm
        out_specs.append(pl.BlockSpec((1, tn), lambda i, j, k: (0, j)))
        out_shape.append(jax.ShapeDtypeStruct((1, n), F32))
    res = _pcall(
        body, name=name, grid=(m // tm, n // tn, nk),
        in_specs=[pl.BlockSpec((tk, tm), lambda i, j, k: (k, i)), pl.BlockSpec((tk, tn), lambda i, j, k: (k, j))],
        out_specs=out_specs, out_shape=out_shape,
        scratch_shapes=[pltpu.VMEM((tm, tn), F32)],
        compiler_params=_cp(("parallel", "parallel", "arbitrary"), 56),
    )(a, b)
    return res if colsum else res[0]


def _mm_nt(a, w3, lp, *, name, dep=None):
    r, kdim = a.shape
    nk, n, tk = w3.shape
    assert nk * tk == kdim
    tm = _row_tile(lp, 832)
    deps = [] if dep is None else [dep]

    def body(a_ref, w_ref, *rest):
        o_ref, acc = rest[-2:]
        k = pl.program_id(1)

        @pl.when(k == 0)
        def _():
            acc[...] = jnp.zeros_like(acc)

        acc[...] += _dot(_bf(a_ref[...]), w_ref[k], NT)

        @pl.when(k == nk - 1)
        def _():
            o_ref[...] = acc[...]

    return _pcall(
        body, name=name, grid=(r // tm, nk),
        in_specs=[pl.BlockSpec((tm, tk), lambda i, k: (i, k)), _resident(w3.shape)]
        + [_const(dp_.shape) for dp_ in deps],
        out_specs=pl.BlockSpec((tm, n), lambda i, k: (i, 0)),
        out_shape=jax.ShapeDtypeStruct((r, n), F32),
        scratch_shapes=[pltpu.VMEM((tm, n), F32)],
        compiler_params=_cp(("parallel", "arbitrary"), 48),
    )(a, w3, *deps)


def _s5_prep(lam_re, lam_im, log_dt, b_re_t, b_im_t):
    g, p = lam_re.shape
    h = b_re_t.shape[0]

    def body(lr_ref, li_ref, ldt_ref, br_ref, bi_ref, pr_ref, pi_ref, bbr_ref, bbi_ref):
        lr, li = lr_ref[...], li_ref[...]
        dt = jnp.exp(ldt_ref[...])
        e = jnp.exp(lr * dt)
        ar, ai = e * jnp.cos(li * dt), e * jnp.sin(li * dt)
        den = lr * lr + li * li
        cr = ((ar - 1.0) * lr + ai * li) / den
        ci = (ai * lr - (ar - 1.0) * li) / den
        br, bi = br_ref[...], bi_ref[...]
        bbr_ref[...] = cr[None] * br - ci[None] * bi
        bbi_ref[...] = cr[None] * bi + ci[None] * br
        xr, xi = ar, ai
        pr_ref[0] = xr
        pi_ref[0] = xi
        for t in range(1, 8):
            xr, xi = xr * ar - xi * ai, xr * ai + xi * ar
            pr_ref[t] = xr
            pi_ref[t] = xi

    sd = jax.ShapeDtypeStruct
    return _pcall(body, name="s5_prep",
                  out_shape=[sd((8, g, p), F32), sd((8, g, p), F32), sd((h, g, p), F32), sd((h, g, p), F32)])(
        lam_re, lam_im, log_dt, b_re_t, b_im_t)


def _s5_prep_bwd(lam_re, lam_im, log_dt, b_re_t, b_im_t, da_re, da_im, dbb_re_t, dbb_im_t):
    g, p = lam_re.shape
    h = b_re_t.shape[0]

    def body(lr_ref, li_ref, ldt_ref, br_ref, bi_ref, dar_ref, dai_ref, dbr_ref, dbi_ref,
             glr_ref, gli_ref, gdt_ref, gbr_ref, gbi_ref):
        lr, li = lr_ref[...], li_ref[...]
        dt = jnp.exp(ldt_ref[...])
        e = jnp.exp(lr * dt)
        ar, ai = e * jnp.cos(li * dt), e * jnp.sin(li * dt)
        den = lr * lr + li * li
        cr = ((ar - 1.0) * lr + ai * li) / den
        ci = (ai * lr - (ar - 1.0) * li) / den
        br, bi = br_ref[...], bi_ref[...]
        gr, gi = dbr_ref[...], dbi_ref[...]
        gbr_ref[...] = gr * cr[None] + gi * ci[None]
        gbi_ref[...] = gi * cr[None] - gr * ci[None]
        gcr = jnp.sum(gr * br + gi * bi, axis=0)
        gci = jnp.sum(gi * br - gr * bi, axis=0)
        ilr, ili = lr / den, -li / den
        gar = dar_ref[...] + gcr * ilr + gci * ili
        gai = dai_ref[...] + gci * ilr - gcr * ili
        qr, qi = cr * ilr - ci * ili, cr * ili + ci * ilr
        glr = -(gcr * qr + gci * qi)
        gli = -(gci * qr - gcr * qi)
        gzr = gar * ar + gai * ai
        gzi = gai * ar - gar * ai
        glr_ref[...] = glr + gzr * dt
        gli_ref[...] = gli + gzi * dt
        gdt_ref[...] = jnp.sum(gzr * lr + gzi * li, axis=1, keepdims=True) * dt

    sd = jax.ShapeDtypeStruct
    return _pcall(body, name="s5_prep_bwd",
                  out_shape=[sd((g, p), F32), sd((g, p), F32), sd((g, 1), F32), sd((h, g, p), F32), sd((h, g, p), F32)])(
        lam_re, lam_im, log_dt, b_re_t, b_im_t, da_re, da_im, dbb_re_t, dbb_im_t)


def _cmul(xr, xi, yr, yi):
    return xr * yr - xi * yi, xr * yi + xi * yr


def _dot5(a, b, dims=NN):
    return _dot(_bf(a), _bf(b), dims)


def _s5_fwd(p3, bk, cre, cim, apow, dskip):
    bsz, lp, _ = p3.shape
    tt = _row_tile(lp, 528, 8)
    nt = lp // tt
    nblk = tt // 8
    hw = 512

    def body(u_ref, bk_ref, cre_ref, cim_ref, ap_ref, d_ref, y_ref, xs_ref, car_ref):
        t = pl.program_id(2)

        @pl.when(t == 0)
        def _():
            car_ref[...] = jnp.zeros_like(car_ref)

        u = u_ref[...]
        xs_ref[...] = _dot5(u, bk_ref[...])
        ap = ap_ref[...]
        apr, api = ap[:, :hw], ap[:, hw:]
        rows = lax.broadcasted_iota(jnp.int32, (8, hw), 0)
        lev = [(d, jnp.where(rows < d, 0.0, jnp.broadcast_to(apr[d - 1:d, :], (8, hw))),
                jnp.where(rows < d, 0.0, jnp.broadcast_to(api[d - 1:d, :], (8, hw)))) for d in (1, 2, 4)]

        def blk(i, carry):
            cr, ci = carry
            off = pl.multiple_of(i * 8, 8)
            x = xs_ref[pl.ds(off, 8), :]
            xr, xi = x[:, :hw], x[:, hw:]
            for d, lr, li in lev:
                mr, mi = _cmul(pltpu.roll(xr, d, 0), pltpu.roll(xi, d, 0), lr, li)
                xr, xi = xr + mr, xi + mi
            mr, mi = _cmul(apr, api, cr, ci)
            xr, xi = xr + mr, xi + mi
            xs_ref[pl.ds(off, 8), :] = jnp.concatenate([xr, xi], axis=1)
            return xr[7:8, :], xi[7:8, :]

        c0 = car_ref[...]
        cr, ci = lax.fori_loop(0, nblk, blk, (c0[0:1, :hw], c0[0:1, hw:]))
        car_ref[...] = jnp.broadcast_to(jnp.concatenate([cr, ci], axis=1), car_ref.shape)
        xs = xs_ref[...]
        y_ref[...] = (_dot5(xs[:, :hw], cre_ref[...]) - _dot5(xs[:, hw:], cim_ref[...])
                      + d_ref[...] * u)

    ub = U_OFF // 128
    return _pcall(
        body, name="s5_fwd", grid=(S5_KCH, bsz, nt),
        in_specs=[pl.BlockSpec((None, tt, 128), lambda k, b, t: (b, t, ub + k)),
                  pl.BlockSpec((None, 128, 2 * hw), lambda k, b, t: (k, 0, 0)),
                  pl.BlockSpec((None, hw, 128), lambda k, b, t: (k, 0, 0)),
                  pl.BlockSpec((None, hw, 128), lambda k, b, t: (k, 0, 0)),
                  pl.BlockSpec((None, 8, 2 * hw), lambda k, b, t: (k, 0, 0)),
                  pl.BlockSpec((1, 128), lambda k, b, t: (0, k))],
        out_specs=[pl.BlockSpec((None, tt, 128), lambda k, b, t: (b, t, k)),
                   pl.BlockSpec((None, None, tt, 2 * hw), lambda k, b, t: (b, k, t, 0))],
        out_shape=[jax.ShapeDtypeStruct((bsz, lp, S5_KCH * 128), F32),
                   jax.ShapeDtypeStruct((bsz, S5_KCH, lp, 2 * hw), F32)],
        scratch_shapes=[pltpu.VMEM((8, 2 * hw), F32)],
        compiler_params=_cp(("parallel", "parallel", "arbitrary"), 40),
    )(p3, bk, cre, cim, apow, dskip)


def _s5_bwd(dp3, p3, dy3, xs, bk, cre, cim, apow_rev, dskip):
    bsz, lp, _ = p3.shape
    tt = _row_tile(lp, 528, 8)
    nt = lp // tt
    nblk = tt // 8
    hw = 512
    tb = tt // 8

    def body(dp_any, u_ref, dy_ref, xs_ref, halo_ref, bk_ref, cre_ref, cim_ref, ap_ref, d_ref,
             du_ref, dbk_ref, dcre_ref, dcim_ref, da_ref, dd_ref, g_ref, ext_ref, car_ref):
        b = pl.program_id(1)
        t = pl.program_id(2)
        tidx = nt - 1 - t

        @pl.when(t == 0)
        def _():
            car_ref[...] = jnp.zeros_like(car_ref)

        @pl.when((b == 0) & (t == 0))
        def _():
            dbk_ref[...] = jnp.zeros_like(dbk_ref)
            dcre_ref[...] = jnp.zeros_like(dcre_ref)
            dcim_ref[...] = jnp.zeros_like(dcim_ref)
            da_ref[...] = jnp.zeros_like(da_ref)
            dd_ref[...] = jnp.zeros_like(dd_ref)

        u = u_ref[...]
        dy = dy_ref[...]
        g_ref[:, :hw] = _dot5(dy, cre_ref[...], NT)
        g_ref[:, hw:] = -_dot5(dy, cim_ref[...], NT)
        ap = ap_ref[...]
        apr, api = ap[:, :hw], -ap[:, hw:]
        rows = lax.broadcasted_iota(jnp.int32, (8, hw), 0)
        lev = [(d, jnp.where(rows >= 8 - d, 0.0, jnp.broadcast_to(apr[8 - d:9 - d, :], (8, hw))),
                jnp.where(rows >= 8 - d, 0.0, jnp.broadcast_to(api[8 - d:9 - d, :], (8, hw)))) for d in (1, 2, 4)]

        def blk(i, carry):
            cr, ci = carry
            off = pl.multiple_of((nblk - 1 - i) * 8, 8)
            x = g_ref[pl.ds(off, 8), :]
            xr, xi = x[:, :hw], x[:, hw:]
            for d, lr, li in lev:
                mr, mi = _cmul(pltpu.roll(xr, 8 - d, 0), pltpu.roll(xi, 8 - d, 0), lr, li)
                xr, xi = xr + mr, xi + mi
            mr, mi = _cmul(apr, api, cr, ci)
            xr, xi = xr + mr, xi + mi
            g_ref[pl.ds(off, 8), :] = jnp.concatenate([xr, xi], axis=1)
            return xr[0:1, :], xi[0:1, :]

        c0 = car_ref[...]
        cr, ci = lax.fori_loop(0, nblk, blk, (c0[0:1, :hw], c0[0:1, hw:]))
        car_ref[...] = jnp.broadcast_to(jnp.concatenate([cr, ci], axis=1), car_ref.shape)

        gg = g_ref[...]
        du = _dot5(gg, bk_ref[...], NT) + d_ref[...] * dy
        trow = tidx * tt + lax.broadcasted_iota(jnp.int32, (tt, 1), 0)
        du_ref[...] = jnp.where(trow >= PAD, du, 0.0).astype(du_ref.dtype)
        dbk_ref[...] += _dot5(u, gg, TN)
        xsv = xs_ref[...]
        dcre_ref[...] += _dot5(xsv[:, :hw], dy, TN)
        dcim_ref[...] -= _dot5(xsv[:, hw:], dy, TN)
        dd_ref[...] += _colsum(dy * u)
        ext_ref[0:8, :] = jnp.where(tidx == 0, 0.0, halo_ref[...])
        ext_ref[8:, :] = xsv
        xp = ext_ref[pl.ds(7, tt), :]
        gr, gi, pr, pi = gg[:, :hw], gg[:, hw:], xp[:, :hw], xp[:, hw:]
        da_ref[:, :hw] += _colsum(gr * pr + gi * pi)
        da_ref[:, hw:] += _colsum(gi * pr - gr * pi)

    ub = U_OFF // 128
    sd = jax.ShapeDtypeStruct
    rt = lambda t: nt - 1 - t
    res = _pcall(
        body, name="s5_bwd", grid=(S5_KCH, bsz, nt),
        in_specs=[pl.BlockSpec(memory_space=pl.ANY),
                  pl.BlockSpec((None, tt, 128), lambda k, b, t: (b, rt(t), ub + k)),
                  pl.BlockSpec((None, tt, 128), lambda k, b, t: (b, rt(t), k)),
                  pl.BlockSpec((None, None, tt, 2 * hw), lambda k, b, t: (b, k, rt(t), 0)),
                  pl.BlockSpec((None, None, 8, 2 * hw), lambda k, b, t: (b, k, jnp.maximum(rt(t) * tb - 1, 0), 0)),
                  pl.BlockSpec((None, 128, 2 * hw), lambda k, b, t: (k, 0, 0)),
                  pl.BlockSpec((None, hw, 128), lambda k, b, t: (k, 0, 0)),
                  pl.BlockSpec((None, hw, 128), lambda k, b, t: (k, 0, 0)),
                  pl.BlockSpec((None, 8, 2 * hw), lambda k, b, t: (k, 0, 0)),
                  pl.BlockSpec((1, 128), lambda k, b, t: (0, k))],
        out_specs=[pl.BlockSpec((None, tt, 128), lambda k, b, t: (b, rt(t), ub + k)),
                   pl.BlockSpec((None, 128, 2 * hw), lambda k, b, t: (k, 0, 0)),
                   pl.BlockSpec((None, hw, 128), lambda k, b, t: (k, 0, 0)),
                   pl.BlockSpec((None, hw, 128), lambda k, b, t: (k, 0, 0)),
                   pl.BlockSpec((None, 1, 2 * hw), lambda k, b, t: (k, 0, 0)),
                   pl.BlockSpec((1, 128), lambda k, b, t: (0, k))],
        out_shape=[sd(dp3.shape, dp3.dtype), sd((S5_KCH, 128, 2 * hw), F32), sd((S5_KCH, hw, 128), F32),
                   sd((S5_KCH, hw, 128), F32), sd((S5_KCH, 1, 2 * hw), F32), sd((1, S5_KCH * 128), F32)],
        scratch_shapes=[pltpu.VMEM((tt, 2 * hw), F32), pltpu.VMEM((tt + 8, 2 * hw), F32), pltpu.VMEM((8, 2 * hw), F32)],
        input_output_aliases={0: 0},
        compiler_params=_cp(("arbitrary", "arbitrary", "arbitrary"), 48),
    )(dp3, p3, dy3, xs, xs, bk, cre, cim, apow_rev, dskip)
    return res


_G0 = math.sqrt(2.0 / math.pi)
_G1 = 0.044715


def _gelu(y):
    return 0.5 * y * (1.0 + jnp.tanh(_G0 * (y + _G1 * y * y * y)))


def _gelu_grad(y):
    th = jnp.tanh(_G0 * (y + _G1 * y * y * y))
    return 0.5 * (1.0 + th) + 0.5 * y * (1.0 - th * th) * _G0 * (1.0 + 3.0 * _G1 * y * y)


def _glu_fwd(y_s5, wglu_g, lp):
    r, w = y_s5.shape
    tm = _row_tile(lp, 416)
    cw = wglu_g.shape[2]

    def body(y_ref, w_ref, gy_ref, z_ref, o_ref):
        gy = _bf(_gelu(y_ref[...]))
        gy_ref[...] = gy
        zs = [_dot(gy, w_ref[s]) for s in range(4)]
        for s in range(4):
            z_ref[:, s * cw:(s + 1) * cw] = zs[s]
        o_ref[:, :cw] = zs[0] * _sig(zs[2])
        o_ref[:, cw:] = zs[1] * _sig(zs[3])

    sd = jax.ShapeDtypeStruct
    return _pcall(
        body, name="glu_fwd", grid=(r // tm,),
        in_specs=[pl.BlockSpec((tm, w), lambda i: (i, 0)), _resident(wglu_g.shape)],
        out_specs=[pl.BlockSpec((tm, w), lambda i: (i, 0)), pl.BlockSpec((tm, 4 * cw), lambda i: (i, 0)),
                   pl.BlockSpec((tm, 2 * cw), lambda i: (i, 0))],
        out_shape=[sd((r, w), BF16), sd((r, 4 * cw), F32), sd((r, 2 * cw), F32)],
        compiler_params=_cp(("parallel",), 40),
    )(y_s5, wglu_g)


def _glu_bwd(dyg, z, y_s5, wglu_g, lp):
    r, w = y_s5.shape
    tm = _row_tile(lp, 416)
    cw = wglu_g.shape[2]

    def body(d_ref, z_ref, y_ref, w_ref, dz_ref, dy_ref):
        d = d_ref[...]
        zz = z_ref[...]
        acc = jnp.zeros((tm, w), F32)
        for s in range(2):
            z1 = zz[:, s * cw:(s + 1) * cw]
            sg = _sig(zz[:, (2 + s) * cw:(3 + s) * cw])
            dd = d[:, s * cw:(s + 1) * cw]
            dz1 = _bf(dd * sg)
            dz2 = _bf(dd * z1 * sg * (1.0 - sg))
            dz_ref[:, s * cw:(s + 1) * cw] = dz1
            dz_ref[:, (2 + s) * cw:(3 + s) * cw] = dz2
            acc += _dot(dz1, w_ref[s], NT) + _dot(dz2, w_ref[2 + s], NT)
        dy_ref[...] = acc * _gelu_grad(y_ref[...])

    sd = jax.ShapeDtypeStruct
    return _pcall(
        body, name="glu_bwd", grid=(r // tm,),
        in_specs=[pl.BlockSpec((tm, 2 * cw), lambda i: (i, 0)), pl.BlockSpec((tm, 4 * cw), lambda i: (i, 0)),
                  pl.BlockSpec((tm, w), lambda i: (i, 0)), _resident(wglu_g.shape)],
        out_specs=[pl.BlockSpec((tm, 4 * cw), lambda i: (i, 0)), pl.BlockSpec((tm, w), lambda i: (i, 0))],
        out_shape=[sd((r, 4 * cw), BF16), sd((r, w), F32)],
        compiler_params=_cp(("parallel",), 40),
    )(dyg, z, y_s5, wglu_g)


def _conv_fwd(p3, cw, cb):
    bsz, lp, _ = p3.shape
    tt = _row_tile(lp, 416)
    nt = lp // tt
    tb = tt // 8
    c = cw.shape[1]
    qb = Q_OFF // c

    def body(x_ref, halo_ref, w_ref, b_ref, pre_ref, act_ref, ext_ref):
        t = pl.program_id(1)
        ext_ref[0:8, :] = jnp.where(t == 0, 0.0, halo_ref[...])
        ext_ref[8:, :] = x_ref[...]
        w = w_ref[...]
        acc = b_ref[...] + w[0:1, :] * ext_ref[pl.ds(5, tt), :]
        for j in range(1, CONV_W):
            acc = acc + w[j:j + 1, :] * ext_ref[pl.ds(5 + j, tt), :]
        pre_ref[...] = acc
        act_ref[...] = acc * _sig(acc)

    sd = jax.ShapeDtypeStruct
    return _pcall(
        body, name="conv_fwd", grid=(bsz, nt),
        in_specs=[pl.BlockSpec((None, tt, c), lambda b, t: (b, t, qb)),
                  pl.BlockSpec((None, 8, c), lambda b, t: (b, jnp.maximum(t * tb - 1, 0), qb)),
                  _const((CONV_W, c)), _const((1, c))],
        out_specs=[pl.BlockSpec((None, tt, c), lambda b, t: (b, t, 0))] * 2,
        out_shape=[sd((bsz, lp, c), F32)] * 2,
        scratch_shapes=[pltpu.VMEM((tt + 8, c), F32)],
        compiler_params=_cp(("parallel", "parallel")),
    )(p3, p3, cw, cb)


def _conv_bwd(dp3, p3, dact3, pre3, cw):
    bsz, lp, _ = p3.shape
    tt = _row_tile(lp, 416)
    nt = lp // tt
    tb = tt // 8
    c = cw.shape[1]
    qb = Q_OFF // c

    def silu_grad(x):
        s = _sig(x)
        return s * (1.0 + x * (1.0 - s))

    def body(dp_any, x_ref, xh_ref, d_ref, dh_ref, pre_ref, preh_ref, w_ref, o_ref, dw_ref, db_ref, ext_ref, dext_ref):
        b = pl.program_id(0)
        t = pl.program_id(1)

        @pl.when((b == 0) & (t == 0))
        def _():
            dw_ref[...] = jnp.zeros_like(dw_ref)
            db_ref[...] = jnp.zeros_like(db_ref)

        dc = d_ref[...] * silu_grad(pre_ref[...])
        dch = jnp.where(t == nt - 1, 0.0, dh_ref[...] * silu_grad(preh_ref[...]))
        dext_ref[0:tt, :] = dc
        dext_ref[tt:, :] = dch
        ext_ref[0:8, :] = jnp.where(t == 0, 0.0, xh_ref[...])
        ext_ref[8:, :] = x_ref[...]
        w = w_ref[...]
        acc = w[CONV_W - 1:CONV_W, :] * dc
        for j in range(CONV_W - 1):
            acc = acc + w[j:j + 1, :] * dext_ref[pl.ds(CONV_W - 1 - j, tt), :]
        trow = t * tt + lax.broadcasted_iota(jnp.int32, (tt, 1), 0)
        o_ref[...] = jnp.where(trow >= PAD, acc, 0.0).astype(o_ref.dtype)
        db_ref[...] += _colsum(dc)
        for j in range(CONV_W):
            dw_ref[j:j + 1, :] += _colsum(dc * ext_ref[pl.ds(5 + j, tt), :])

    sd = jax.ShapeDtypeStruct
    nxt = lambda t: jnp.minimum((t + 1) * tb, lp // 8 - 1)
    return _pcall(
        body, name="conv_bwd", grid=(bsz, nt),
        in_specs=[pl.BlockSpec(memory_space=pl.ANY),
                  pl.BlockSpec((None, tt, c), lambda b, t: (b, t, qb)),
                  pl.BlockSpec((None, 8, c), lambda b, t: (b, jnp.maximum(t * tb - 1, 0), qb)),
                  pl.BlockSpec((None, tt, c), lambda b, t: (b, t, 0)),
                  pl.BlockSpec((None, 8, c), lambda b, t: (b, nxt(t), 0)),
                  pl.BlockSpec((None, tt, c), lambda b, t: (b, t, 0)),
                  pl.BlockSpec((None, 8, c), lambda b, t: (b, nxt(t), 0)),
                  _const((CONV_W, c))],
        out_specs=[pl.BlockSpec((None, tt, c), lambda b, t: (b, t, qb)), _const((CONV_W, c)), _const((1, c))],
        out_shape=[sd(dp3.shape, dp3.dtype), sd((CONV_W, c), F32), sd((1, c), F32)],
        scratch_shapes=[pltpu.VMEM((tt + 8, c), F32), pltpu.VMEM((tt + 8, c), F32)],
        input_output_aliases={0: 0},
        compiler_params=_cp(("arbitrary", "arbitrary")),
    )(dp3, p3, p3, dact3, dact3, pre3, pre3, cw)


def _mlstm_gates(g, h_idx, c_idx, lc):
    lane = lax.broadcasted_iota(jnp.int32, g.shape, 1)
    i_col = jnp.sum(jnp.where(lane == h_idx, g, 0.0), axis=1, keepdims=True)
    f_col = jnp.sum(jnp.where(lane == M_HEADS + h_idx, g, 0.0), axis=1, keepdims=True)
    row = lax.broadcasted_iota(jnp.int32, (lc, 1), 0)
    valid = (c_idx * lc + row) >= PAD
    li = jnp.where(valid, i_col, NEG)
    lf = jnp.where(valid, jnp.minimum(f_col, 0.0) - jnp.log(1.0 + jnp.exp(-jnp.abs(f_col))), 0.0)
    r2 = lax.broadcasted_iota(jnp.int32, (lc, lc), 0)
    c2 = lax.broadcasted_iota(jnp.int32, (lc, lc), 1)
    eye = r2 == c2
    tril = r2 >= c2
    to_row = lambda col: jnp.sum(jnp.where(eye, col, 0.0), axis=0, keepdims=True)
    lf_row = to_row(lf)
    b_col = jnp.sum(jnp.where(tril, lf_row, 0.0), axis=1, keepdims=True)
    b_row = to_row(b_col)
    li_row = to_row(li)
    d_mat = jnp.where(tril, b_col - b_row + li_row, NEG)
    return dict(f_col=f_col, valid=valid, li=li, b_col=b_col, d_mat=d_mat, eye=eye, r2=r2, c2=c2, row=row,
                to_row=to_row)


def _mlstm_chunk(q, ks, v, gq, c_st, n_st, m_st, lc):
    b_col, d_mat = gq["b_col"], gq["d_mat"]
    m_inter = b_col + m_st
    m_row = jnp.maximum(m_inter, jnp.max(d_mat, axis=1, keepdims=True))
    w_intra = jnp.exp(d_mat - m_row)
    w_inter = jnp.exp(m_inter - m_row)
    qb, kb, vb, cb = _bf(q), _bf(ks), _bf(v), _bf(c_st)
    s = _dot(qb, kb, NT) * w_intra
    qc = _dot(qb, cb)
    num = _dot(_bf(s), vb) + w_inter * qc
    qn = jnp.sum(q * n_st, axis=1, keepdims=True)
    den = jnp.sum(s, axis=1, keepdims=True) + w_inter * qn
    e = jnp.exp(-m_row)
    nn = jnp.maximum(jnp.abs(den), e)
    b_last = b_col[lc - 1:lc, :]
    g_log = b_last - b_col + gq["li"]
    m_new = jnp.maximum(b_last + m_st, jnp.max(g_log, axis=0, keepdims=True))
    w_k = jnp.exp(g_log - m_new)
    decay = jnp.exp(b_last + m_st - m_new)
    return dict(w_intra=w_intra, w_inter=w_inter, qb=qb, kb=kb, vb=vb, cb=cb, s=s, qc=qc, num=num, qn=qn, den=den,
                e=e, nn=nn, m_new=m_new, w_k=w_k, decay=decay)


def _chunks_per_step(nc):
    return max(c for c in (3, 2, 1) if nc % c == 0)


def _mlstm_fwd(qk3, p3):
    bsz, lp, _ = p3.shape
    lc = M_CHUNK
    nc = lp // lc
    dk, dv = 128, 256
    scale = dk ** -0.5

    cps = _chunks_per_step(nc)
    rows = cps * lc

    def body(q_ref, k_ref, v_ref, g_ref, h_ref, cs_ref, ns_ref, ms_ref, c_sc, n_sc, m_sc):
        st = pl.program_id(1)

        @pl.when(st == 0)
        def _():
            c_sc[...] = jnp.zeros_like(c_sc)
            n_sc[...] = jnp.zeros_like(n_sc)
            m_sc[...] = jnp.zeros_like(m_sc)

        for j in range(cps):
            rs = slice(j * lc, (j + 1) * lc)
            g = g_ref[rs, :]
            for hh in range(M_HEADS):
                c_st, n_st, m_all = c_sc[hh], n_sc[hh], m_sc[hh]
                cs_ref[hh, j] = c_st
                ns_ref[hh, j] = n_st
                ms_ref[hh, j] = m_all
                m_st = m_all[:, 0:1]
                q = q_ref[rs, hh * dk:(hh + 1) * dk]
                ks = k_ref[rs, hh * dk:(hh + 1) * dk] * scale
                v = v_ref[rs, hh * dv:(hh + 1) * dv]
                gq = _mlstm_gates(g, hh, st * cps + j, lc)
                f = _mlstm_chunk(q, ks, v, gq, c_st, n_st, m_st, lc)
                h_ref[rs, hh * dv:(hh + 1) * dv] = f["num"] / f["nn"]
                kw = ks * f["w_k"]
                c_sc[hh] = f["decay"] * c_st + _dot(_bf(kw), f["vb"], TN)
                n_sc[hh] = f["decay"] * n_st + _colsum(kw)
                m_sc[hh] = jnp.broadcast_to(f["m_new"], (1, 128))

    sd = jax.ShapeDtypeStruct
    nh = M_HEADS
    return _pcall(
        body, name="mlstm_fwd", grid=(bsz, nc // cps),
        in_specs=[pl.BlockSpec((None, rows, nh * dk), lambda b, c: (b, c, 0)),
                  pl.BlockSpec((None, rows, nh * dk), lambda b, c: (b, c, 1)),
                  pl.BlockSpec((None, rows, nh * dv), lambda b, c: (b, c, V_OFF // (nh * dv))),
                  pl.BlockSpec((None, rows, 128), lambda b, c: (b, c, G_OFF // 128))],
        out_specs=[pl.BlockSpec((None, rows, nh * dv), lambda b, c: (b, c, 0)),
                   pl.BlockSpec((None, nh, cps, dk, dv), lambda b, c: (b, 0, c, 0, 0)),
                   pl.BlockSpec((None, nh, cps, 1, dk), lambda b, c: (b, 0, c, 0, 0)),
                   pl.BlockSpec((None, nh, cps, 1, 128), lambda b, c: (b, 0, c, 0, 0))],
        out_shape=[sd((bsz, lp, nh * dv), F32), sd((bsz, nh, nc, dk, dv), F32),
                   sd((bsz, nh, nc, 1, dk), F32), sd((bsz, nh, nc, 1, 128), F32)],
        scratch_shapes=[pltpu.VMEM((nh, dk, dv), F32), pltpu.VMEM((nh, 1, dk), F32), pltpu.VMEM((nh, 1, 128), F32)],
        compiler_params=_cp(("parallel", "arbitrary")),
    )(qk3, qk3, p3, p3)


def _mlstm_bwd(dp3, qk3, p3, dh3, cs, ns, ms):
    bsz, lp, _ = p3.shape
    lc = M_CHUNK
    nc = lp // lc
    dk, dv = 128, 256
    scale = dk ** -0.5

    cps = _chunks_per_step(nc)
    nst = nc // cps
    rows = cps * lc

    def body(dp_any, q_ref, k_ref, v_ref, g_ref, dh_ref, cs_ref, ns_ref, ms_ref,
             dv_ref, dqk_ref, dg_ref, dc_sc, dn_sc):
        t = pl.program_id(1)
        st = nst - 1 - t

        @pl.when(t == 0)
        def _():
            dc_sc[...] = jnp.zeros_like(dc_sc)
            dn_sc[...] = jnp.zeros_like(dn_sc)

        lane = lax.broadcasted_iota(jnp.int32, (lc, 128), 1)
        for j in reversed(range(cps)):
            rs = slice(j * lc, (j + 1) * lc)
            g = g_ref[rs, :]
            dgate = jnp.zeros((lc, 128), F32)
            for hh in range(M_HEADS):
                dgate = head(hh, j, rs, st * cps + j, g, lane, dgate, q_ref, k_ref, v_ref, dh_ref, cs_ref, ns_ref,
                             ms_ref, dv_ref, dqk_ref, dc_sc, dn_sc)
            dg_ref[rs, :] = dgate.astype(dg_ref.dtype)

    def head(hh, j, sl, c, g, lane, dgate, q_ref, k_ref, v_ref, dh_ref, cs_ref, ns_ref, ms_ref, dv_ref, dqk_ref,
             dc_sc, dn_sc):
        c_st, n_st = cs_ref[hh, j], ns_ref[hh, j]
        m_st = ms_ref[hh, j][:, 0:1]
        q = q_ref[sl, hh * dk:(hh + 1) * dk]
        ks = k_ref[sl, hh * dk:(hh + 1) * dk] * scale
        v = v_ref[sl, hh * dv:(hh + 1) * dv]
        dh = dh_ref[sl, hh * dv:(hh + 1) * dv]
        gq = _mlstm_gates(g, hh, c, lc)
        f = _mlstm_chunk(q, ks, v, gq, c_st, n_st, m_st, lc)
        eye, r2, c2, row, valid = gq["eye"], gq["r2"], gq["c2"], gq["row"], gq["valid"]
        w_intra, w_inter, s, nn, den = f["w_intra"], f["w_inter"], f["s"], f["nn"], f["den"]
        qb, kb, vb, cb, w_k, decay = f["qb"], f["kb"], f["vb"], f["cb"], f["w_k"], f["decay"]
        d_c, d_n = dc_sc[hh], dn_sc[hh]
        d_cb = _bf(d_c)

        hout = f["num"] / nn
        dnum = dh / nn
        d_nn = -jnp.sum(dh * hout, axis=1, keepdims=True) / nn
        dden = jnp.where(jnp.abs(den) > f["e"], d_nn * jnp.sign(den), 0.0)
        wdnum = w_inter * dnum
        wdden = w_inter * dden
        ds = _dot(_bf(dnum), vb, NT) + dden
        dsw = _bf(ds * w_intra)
        dq = _dot(dsw, kb) + _dot(_bf(wdnum), cb, NT) + wdden * n_st
        dkw = _dot(vb, d_cb, NT) + d_n
        dks = _dot(dsw, qb, TN) + dkw * w_k
        kw = ks * w_k
        dvv = _dot(_bf(s), _bf(dnum), TN) + _dot(_bf(kw), d_cb)
        dd = ds * s
        rs = jnp.sum(dd, axis=1, keepdims=True)
        cs_col = jnp.sum(jnp.where(eye, jnp.sum(dd, axis=0, keepdims=True), 0.0), axis=1, keepdims=True)
        dwi = jnp.sum(dnum * f["qc"], axis=1, keepdims=True) + dden * f["qn"]
        db = rs - cs_col + dwi * w_inter
        dli = cs_col
        ddecay = jnp.sum(jnp.sum(d_c * c_st, axis=1, keepdims=True), axis=0, keepdims=True) \
            + jnp.sum(d_n * n_st, axis=1, keepdims=True)
        dgl = jnp.sum(dkw * ks, axis=1, keepdims=True) * w_k
        dblast = ddecay * decay + jnp.sum(dgl, axis=0, keepdims=True)
        db = db - dgl + jnp.where(row == lc - 1, dblast, 0.0)
        dli = dli + dgl
        db_row = gq["to_row"](db)
        dlf = jnp.sum(jnp.where(c2 >= r2, db_row, 0.0), axis=1, keepdims=True)
        dlf = jnp.where(valid, dlf, 0.0)
        dgate = jnp.where(lane == hh, jnp.where(valid, dli, 0.0), dgate)
        dgate = jnp.where(lane == M_HEADS + hh, dlf / (1.0 + jnp.exp(gq["f_col"])), dgate)
        dqk_ref[sl, hh * dk:(hh + 1) * dk] = dq
        dqk_ref[sl, (M_HEADS + hh) * dk:(M_HEADS + hh + 1) * dk] = dks * scale
        dv_ref[sl, hh * dv:(hh + 1) * dv] = dvv.astype(dv_ref.dtype)
        dc_sc[hh] = decay * d_c + _dot(qb, _bf(wdnum), TN)
        dn_sc[hh] = decay * d_n + _colsum(q * wdden)
        return dgate

    sd = jax.ShapeDtypeStruct
    nh = M_HEADS
    rc = lambda c: nst - 1 - c
    return _pcall(
        body, name="mlstm_bwd", grid=(bsz, nst),
        in_specs=[pl.BlockSpec(memory_space=pl.ANY),
                  pl.BlockSpec((None, rows, nh * dk), lambda b, c: (b, rc(c), 0)),
                  pl.BlockSpec((None, rows, nh * dk), lambda b, c: (b, rc(c), 1)),
                  pl.BlockSpec((None, rows, nh * dv), lambda b, c: (b, rc(c), V_OFF // (nh * dv))),
                  pl.BlockSpec((None, rows, 128), lambda b, c: (b, rc(c), G_OFF // 128)),
                  pl.BlockSpec((None, rows, nh * dv), lambda b, c: (b, rc(c), 0)),
                  pl.BlockSpec((None, nh, cps, dk, dv), lambda b, c: (b, 0, rc(c), 0, 0)),
                  pl.BlockSpec((None, nh, cps, 1, dk), lambda b, c: (b, 0, rc(c), 0, 0)),
                  pl.BlockSpec((None, nh, cps, 1, 128), lambda b, c: (b, 0, rc(c), 0, 0))],
        out_specs=[pl.BlockSpec((None, rows, nh * dv), lambda b, c: (b, rc(c), V_OFF // (nh * dv))),
                   pl.BlockSpec((None, rows, 2 * nh * dk), lambda b, c: (b, rc(c), 0)),
                   pl.BlockSpec((None, rows, 128), lambda b, c: (b, rc(c), 0))],
        out_shape=[sd(dp3.shape, dp3.dtype), sd((bsz, lp, 2 * nh * dk), F32), sd((bsz, lp, 128), dp3.dtype)],
        scratch_shapes=[pltpu.VMEM((nh, dk, dv), F32), pltpu.VMEM((nh, 1, dk), F32)],
        input_output_aliases={0: 0},
        compiler_params=_cp(("arbitrary", "arbitrary")),
    )(dp3, qk3, qk3, p3, p3, dh3, cs, ns, ms)


def _headnorm(x):
    dv = x.shape[1] // M_HEADS
    xh, rs = [], []
    for h in range(M_HEADS):
        xx = x[:, h * dv:(h + 1) * dv]
        mu = jnp.mean(xx, axis=-1, keepdims=True)
        xc = xx - mu
        rstd = lax.rsqrt(jnp.mean(xc * xc, axis=-1, keepdims=True) + LN_EPS)
        xh.append(xc * rstd)
        rs.append(rstd)
    return jnp.concatenate(xh, axis=1), rs


def _mix_fwd(hm, p, ys5g, h0, gn, wmo_bf, wo_bf, g1, b1, lp):
    r, d = hm.shape
    tm = _row_tile(lp, 208)

    def body(hm_ref, o_ref, gs_ref, gm_ref, ys_ref, h0_ref, gn_ref, wmo_ref, wo_ref, g1_ref, b1_ref,
             ymin_ref, ym_ref, mix_ref, r1_ref, h1_ref):
        xhat, _ = _headnorm(hm_ref[...])
        ymin = _bf(_sig(o_ref[...]) * (xhat * gn_ref[...]))
        ymin_ref[...] = ymin
        ym = _dot(ymin, wmo_ref[...])
        ym_ref[...] = ym
        mix = _bf(_sig(gs_ref[...]) * ys_ref[...] + _sig(gm_ref[...]) * ym)
        mix_ref[...] = mix
        r1 = ALPHA * h0_ref[...] + _dot(mix, wo_ref[...])
        r1_ref[...] = r1
        h1, _, _ = _ln_fwd(r1, g1_ref[...], b1_ref[...])
        h1_ref[...] = h1

    sd = jax.ShapeDtypeStruct
    row = pl.BlockSpec((tm, d), lambda i: (i, 0))
    return _pcall(
        body, name="mix_fwd", grid=(r // tm,),
        in_specs=[row, pl.BlockSpec((tm, d), lambda i: (i, O_OFF // d)), pl.BlockSpec((tm, d), lambda i: (i, GS_OFF // d)),
                  pl.BlockSpec((tm, d), lambda i: (i, GM_OFF // d)), row, row, _const((1, d)),
                  _resident((d, d)), _resident((d, d)), _const((1, d)), _const((1, d))],
        out_specs=[row] * 5,
        out_shape=[sd((r, d), BF16), sd((r, d), F32), sd((r, d), BF16), sd((r, d), F32), sd((r, d), F32)],
        compiler_params=_cp(("parallel",), 48),
    )(hm, p, p, p, ys5g, h0, gn, wmo_bf, wo_bf, g1, b1)


def _mix_bwd(dh1, r1, g1, wo_bf, wmo_bf, p, ys5g, ym, hm, gn, lp):
    r, d = hm.shape
    tm = _row_tile(lp, 208)
    dv = d // M_HEADS

    def body(dh1_ref, r1_ref, g1_ref, wo_ref, wmo_ref, o_ref, gs_ref, gm_ref, ys_ref, ym_ref, hm_ref, gn_ref,
             dr1_ref, dp_ref, dys_ref, dym_ref, dhm_ref, dg1_ref, db1_ref, dgn_ref):
        i = pl.program_id(0)

        @pl.when(i == 0)
        def _():
            dg1_ref[...] = jnp.zeros_like(dg1_ref)
            db1_ref[...] = jnp.zeros_like(db1_ref)
            dgn_ref[...] = jnp.zeros_like(dgn_ref)

        dh1 = dh1_ref[...]
        _, xhat1, rstd1 = _ln_fwd(r1_ref[...], g1_ref[...], 0.0)
        dr1 = _ln_bwd(dh1, xhat1, rstd1, g1_ref[...])
        dr1_ref[...] = dr1
        dg1_ref[...] += _colsum(dh1 * xhat1)
        db1_ref[...] += _colsum(dh1)
        dmix = _dot(_bf(dr1), wo_ref[...], NT)
        sgs, sgm, so = _sig(gs_ref[...]), _sig(gm_ref[...]), _sig(o_ref[...])
        dys_ref[...] = dmix * sgs
        dp_ref[:, d:2 * d] = _bf(dmix * ys_ref[...] * sgs * (1.0 - sgs))
        dym = dmix * sgm
        dym_ref[...] = _bf(dym)
        dp_ref[:, 2 * d:3 * d] = _bf(dmix * ym_ref[...] * sgm * (1.0 - sgm))
        dymin = _dot(_bf(dym), wmo_ref[...], NT)
        xhat, rs = _headnorm(hm_ref[...])
        gn_ = gn_ref[...]
        dp_ref[:, 0:d] = _bf(dymin * (xhat * gn_) * so * (1.0 - so))
        dhn = dymin * so
        dgn_ref[...] += _colsum(dhn * xhat)
        dxh = dhn * gn_
        for h in range(M_HEADS):
            sl = slice(h * dv, (h + 1) * dv)
            a, xh = dxh[:, sl], xhat[:, sl]
            m1 = jnp.mean(a, axis=-1, keepdims=True)
            m2 = jnp.mean(a * xh, axis=-1, keepdims=True)
            dhm_ref[:, sl] = rs[h] * (a - m1 - xh * m2)

    sd = jax.ShapeDtypeStruct
    row = pl.BlockSpec((tm, d), lambda i: (i, 0))
    vec = _const((1, d))
    return _pcall(
        body, name="mix_bwd", grid=(r // tm,),
        in_specs=[row, row, vec, _resident((d, d)), _resident((d, d)),
                  pl.BlockSpec((tm, d), lambda i: (i, O_OFF // d)), pl.BlockSpec((tm, d), lambda i: (i, GS_OFF // d)),
                  pl.BlockSpec((tm, d), lambda i: (i, GM_OFF // d)), row, row, row, vec],
        out_specs=[row, pl.BlockSpec((tm, 3 * d), lambda i: (i, 0)), row, row, row, vec, vec, vec],
        out_shape=[sd((r, d), F32), sd((r, NP), BF16), sd((r, d), F32), sd((r, d), BF16), sd((r, d), F32),
                   sd((1, d), F32), sd((1, d), F32), sd((1, d), F32)],
        compiler_params=_cp(("arbitrary",), 48),
    )(dh1, r1, g1, wo_bf, wmo_bf, p, p, p, ys5g, ym, hm, gn)


def _mlp_fwd(h1, tgt, wup_g, wdn_bf, bup, g2, b2, lp):
    r, d = h1.shape
    tm = _row_tile(lp, 352)
    tps = lp // tm
    nf = wup_g.shape[0]

    def body(h1_ref, t_ref, wup_ref, wdn_ref, bup_ref, g2_ref, b2_ref, dr2_ref, act_ref, loss_ref, dg2_ref, db2_ref):
        i = pl.program_id(0)

        @pl.when(i == 0)
        def _():
            loss_ref[...] = jnp.zeros_like(loss_ref)
            dg2_ref[...] = jnp.zeros_like(dg2_ref)
            db2_ref[...] = jnp.zeros_like(db2_ref)

        h1 = h1_ref[...]
        h1b = _bf(h1)
        ff = jnp.zeros((tm, d), F32)
        for s in range(nf):
            up = _dot(h1b, wup_ref[s]) + bup_ref[:, s * d:(s + 1) * d]
            a = jnp.maximum(up, 0.0)
            a = _bf(a * a)
            act_ref[:, s * d:(s + 1) * d] = a
            ff = ff + _dot(a, wdn_ref[s * d:(s + 1) * d, :])
        r2 = ALPHA * h1 + ff
        g2 = g2_ref[...]
        y, xhat, rstd = _ln_fwd(r2, g2, b2_ref[...])
        t = (i % tps) * tm + lax.broadcasted_iota(jnp.int32, (tm, 1), 0)
        diff = jnp.where(t >= PAD + N_META, y - t_ref[...], 0.0)
        loss_ref[...] += 0.5 / d * jnp.sum(jnp.sum(diff * diff, axis=1, keepdims=True), axis=0, keepdims=True)
        dy = diff * (1.0 / d)
        dg2_ref[...] += _colsum(dy * xhat)
        db2_ref[...] += _colsum(dy)
        dr2_ref[...] = _ln_bwd(dy, xhat, rstd, g2)

    sd = jax.ShapeDtypeStruct
    row = pl.BlockSpec((tm, d), lambda i: (i, 0))
    vec = _const((1, d))
    return _pcall(
        body, name="mlp_fwd", grid=(r // tm,),
        in_specs=[row, row, _resident(wup_g.shape), _resident(wdn_bf.shape), _const((1, nf * d)), vec, vec],
        out_specs=[row, pl.BlockSpec((tm, nf * d), lambda i: (i, 0)), _const((1, 128)), vec, vec],
        out_shape=[sd((r, d), F32), sd((r, nf * d), BF16), sd((1, 128), F32), sd((1, d), F32), sd((1, d), F32)],
        compiler_params=_cp(("arbitrary",), 56),
    )(h1, tgt, wup_g, wdn_bf, bup, g2, b2)


def _mlp_bwd(h1, dr2, wup_g, wdn_bf, bup, lp):
    r, d = h1.shape
    tm = _row_tile(lp, 352)
    nf = wup_g.shape[0]

    def body(h1_ref, dr2_ref, wup_ref, wdn_ref, bup_ref, dh1_ref, dup_ref, dbup_ref):
        i = pl.program_id(0)

        @pl.when(i == 0)
        def _():
            dbup_ref[...] = jnp.zeros_like(dbup_ref)

        h1b = _bf(h1_ref[...])
        dr2 = dr2_ref[...]
        dr2b = _bf(dr2)
        acc = ALPHA * dr2
        for s in range(nf):
            up = _dot(h1b, wup_ref[s]) + bup_ref[:, s * d:(s + 1) * d]
            dact = _dot(dr2b, wdn_ref[s * d:(s + 1) * d, :], NT)
            dup = dact * (2.0 * jnp.maximum(up, 0.0))
            dbup_ref[:, s * d:(s + 1) * d] += _colsum(dup)
            dupb = _bf(dup)
            dup_ref[:, s * d:(s + 1) * d] = dupb
            acc = acc + _dot(dupb, wup_ref[s], NT)
        dh1_ref[...] = acc

    sd = jax.ShapeDtypeStruct
    row = pl.BlockSpec((tm, d), lambda i: (i, 0))
    return _pcall(
        body, name="mlp_bwd", grid=(r // tm,),
        in_specs=[row, row, _resident(wup_g.shape), _resident(wdn_bf.shape), _const((1, nf * d))],
        out_specs=[row, pl.BlockSpec((tm, nf * d), lambda i: (i, 0)), _const((1, nf * d))],
        out_shape=[sd((r, d), F32), sd((r, nf * d), BF16), sd((1, nf * d), F32)],
        compiler_params=_cp(("arbitrary",), 56),
    )(h1, dr2, wup_g, wdn_bf, bup)


def _s5_block_mats(bb_re_t, bb_im_t, c_re, c_im, ap_re, ap_im):
    ng = c_re.shape[0]
    gl = ng // S5_KCH
    eye = jnp.eye(gl, dtype=F32)

    def bmat(bt):
        bb = jnp.transpose(bt, (1, 0, 2)).reshape(S5_KCH, gl, S5_GROUP, S5_STATE)
        return jnp.einsum("kghp,gj->kghjp", bb, eye).reshape(S5_KCH, gl * S5_GROUP, gl * S5_STATE)

    def cmat(c):
        cc = c.reshape(S5_KCH, gl, S5_GROUP, S5_STATE)
        return jnp.einsum("kghp,gj->kjpgh", cc, eye).reshape(S5_KCH, gl * S5_STATE, gl * S5_GROUP)

    def pw(a):
        return jnp.transpose(a.reshape(8, S5_KCH, gl * S5_STATE), (1, 0, 2))

    bk = jnp.concatenate([bmat(bb_re_t), bmat(bb_im_t)], axis=-1)
    apow = jnp.concatenate([pw(ap_re), pw(ap_im)], axis=-1)
    return _bf(bk), _bf(cmat(c_re)), _bf(cmat(c_im)), apow


def _s5_block_grads(dbk, dcre, dcim, da):
    gl = dbk.shape[1] // S5_GROUP
    ng = gl * S5_KCH
    eye = jnp.eye(gl, dtype=F32)
    hw = gl * S5_STATE

    def bpart(x):
        x = x.reshape(S5_KCH, gl, S5_GROUP, gl, S5_STATE)
        x = jnp.einsum("kghjp,gj->kghp", x, eye).reshape(ng, S5_GROUP, S5_STATE)
        return jnp.transpose(x, (1, 0, 2))

    def cpart(x):
        x = x.reshape(S5_KCH, gl, S5_STATE, gl, S5_GROUP)
        return jnp.einsum("kjpgh,gj->kghp", x, eye).reshape(ng, S5_GROUP, S5_STATE)

    return (bpart(dbk[..., :hw]), bpart(dbk[..., hw:]), cpart(dcre), cpart(dcim),
            da[:, 0, :hw].reshape(ng, S5_STATE), da[:, 0, hw:].reshape(ng, S5_STATE))


def _tie(a, tok):
    return a if tok is None else a + tok[0, 0]


def _local_step(x, tgt, w, early=None, late=None, ready=None):
    ready = ready or (lambda names, g: None)
    bsz, seq, d = x.shape
    lp = PAD + N_META + seq
    r = bsz * lp
    meta = jnp.broadcast_to(w["meta_tokens"][None], (bsz, N_META, d))
    hin = jnp.concatenate([jnp.zeros((bsz, PAD, d), F32), meta, x], axis=1).reshape(r, d)
    tgtp = jnp.concatenate([jnp.zeros((bsz, PAD + N_META, d), F32), tgt], axis=1).reshape(r, d)

    h0, h0b = _ln0_fwd(hin, w["ln0_g"], w["ln0_b"], lp)
    if early is not None:
        w = {**w, **early((h0, tgtp))}
    p = _inproj(h0b, w["w_in"], w["b_in"], lp)
    p3 = p.reshape(bsz, lp, NP)

    b_re_t = jnp.transpose(w["s5_b_re"], (2, 0, 1))
    b_im_t = jnp.transpose(w["s5_b_im"], (2, 0, 1))
    ap_re, ap_im, bb_re_t, bb_im_t = _s5_prep(w["s5_lambda_re"], w["s5_lambda_im"], w["s5_log_dt"], b_re_t, b_im_t)
    bk, cre, cim, apow = _s5_block_mats(bb_re_t, bb_im_t, w["s5_c_re"], w["s5_c_im"], ap_re, ap_im)
    y_s5, xs = _s5_fwd(p3, bk, cre, cim, apow, w["s5_d"])
    sw = y_s5.shape[-1]
    if late is not None:
        w = {**w, **late(y_s5)}
    gy, z, ys5g = _glu_fwd(y_s5.reshape(r, sw), w["s5_w_glu"], lp)

    pre3, qk3 = _conv_fwd(p3, w["qk_conv_w"], w["qk_conv_b"])
    hm3, cs, ns, ms = _mlstm_fwd(qk3, p3)
    hm = hm3.reshape(r, d)
    ymin, ym, mix, r1, h1 = _mix_fwd(hm, p, ys5g, h0, w["m_norm_g"], w["m_w_out"], w["w_o"], w["ln1_g"], w["ln1_b"], lp)
    dr2, act, loss, dg2, db2 = _mlp_fwd(h1, tgtp, w["w_up"], w["w_down"], w["b_up"], w["ln2_g"], w["ln2_b"], lp)

    g = {"ln2_g": dg2, "ln2_b": db2}
    dh1, dup, g["b_up"] = _mlp_bwd(h1, dr2, w["w_up"], w["w_down"], w["b_up"], lp)
    g["w_down"] = _mm_tn(act, dr2, name="dw_down")
    g["w_up"] = _mm_tn(h1, dup, name="dw_up", split=w["w_up"].shape[0])
    tok = ready(("w_down", "w_up"), g)
    dr1, dp, dys5g, dym, dhm, g["ln1_g"], g["ln1_b"], g["m_norm_g"] = _mix_bwd(
        dh1, r1, _tie(w["ln1_g"], tok), w["w_o"], w["m_w_out"], p, ys5g, ym, hm, w["m_norm_g"], lp)
    g["w_o"] = _mm_tn(mix, dr1, name="dw_o")
    g["m_w_out"] = _mm_tn(ymin, dym, name="dw_mout")

    dp3 = dp.reshape(bsz, lp, NP)
    dp3, dqk3, dgate = _mlstm_bwd(dp3, qk3, p3, dhm.reshape(bsz, lp, d), cs, ns, ms)
    dp3, g["qk_conv_w"], g["qk_conv_b"] = _conv_bwd(dp3, p3, dqk3, pre3, w["qk_conv_w"])
    dz, dys5 = _glu_bwd(dys5g, z, y_s5.reshape(r, sw), w["s5_w_glu"], lp)
    g["s5_w_glu"] = _mm_tn(gy, dz, name="dw_glu", split=w["s5_w_glu"].shape[0])
    tok = ready(("s5_w_glu", "m_w_out", "w_o"), g)
    apow_rev = jnp.flip(apow, axis=1)
    dp3, dbk, dcre, dcim, da, g["s5_d"] = _s5_bwd(dp3, p3, dys5.reshape(bsz, lp, sw), xs, bk, cre, cim, apow_rev,
                                                 _tie(w["s5_d"], tok))
    dbb_re_t, dbb_im_t, g["s5_c_re"], g["s5_c_im"], da_re, da_im = _s5_block_grads(dbk, dcre, dcim, da)
    g["s5_lambda_re"], g["s5_lambda_im"], g["s5_log_dt"], gb_re_t, gb_im_t = _s5_prep_bwd(
        w["s5_lambda_re"], w["s5_lambda_im"], w["s5_log_dt"], b_re_t, b_im_t, da_re, da_im, dbb_re_t, dbb_im_t)
    g["s5_b_re"] = jnp.transpose(gb_re_t, (1, 2, 0))
    g["s5_b_im"] = jnp.transpose(gb_im_t, (1, 2, 0))

    dp3 = lax.dynamic_update_slice(dp3, dgate, (0, 0, G_OFF))
    dp = dp3.reshape(r, NP)
    g["w_in"], g["b_in"] = _mm_tn(h0b, dp, name="dw_in", colsum=True)
    tok = ready(("w_in",), g)
    dpw = _mm_nt(dp, w["w_in"], lp, name="dh0", dep=tok)
    dhin, g["ln0_g"], g["ln0_b"], g["meta_tokens"] = _ln0_bwd(hin, dr1, dpw, w["ln0_g"], lp)
    grad_x = dhin.reshape(bsz, lp, d)[:, PAD + N_META:]
    return loss, grad_x, g


_ANY = pl.BlockSpec(memory_space=pl.ANY)
_MESH = pl.DeviceIdType.MESH


def _place():
    return lax.axis_index("x"), lax.axis_index("y"), lax.axis_index("c")


def _gather_chips(shards):
    n = len(shards)

    def body(*refs):
        ins, outs = refs[:n], refs[n:2 * n]
        send, recv, loc = refs[2 * n:]
        x, y, c = _place()
        me = 2 * x + y
        peers = [(1 - x, y), (x, 1 - y), (1 - x, 1 - y)]

        def rc(a, k, slot):
            px, py = peers[k]
            return pltpu.make_async_remote_copy(src_ref=ins[a], dst_ref=outs[a].at[slot], send_sem=send.at[a, k],
                                                recv_sem=recv.at[a, k], device_id=(px, py, c), device_id_type=_MESH)

        own = [pltpu.make_async_copy(ins[a], outs[a].at[me], loc.at[a]) for a in range(n)]
        for cp in own:
            cp.start()
        out = [rc(a, k, me) for a in range(n) for k in range(3)]
        for cp in out:
            cp.start()
        for a in range(n):
            for k in range(3):
                rc(a, k, 2 * peers[k][0] + peers[k][1]).wait_recv()
        for cp in out:
            cp.wait_send()
        for cp in own:
            cp.wait()

    return _pcall(
        body, name="gather_chips", in_specs=[_ANY] * n, out_specs=[_ANY] * n,
        out_shape=[jax.ShapeDtypeStruct((4,) + s.shape, s.dtype) for s in shards],
        scratch_shapes=[pltpu.SemaphoreType.DMA((n, 3)), pltpu.SemaphoreType.DMA((n, 3)), pltpu.SemaphoreType.DMA((n,))],
    )(*shards)


_HBM = pl.BlockSpec(memory_space=pltpu.HBM)
_SEM = pl.BlockSpec(memory_space=pltpu.SEMAPHORE)
_EFFECT = pltpu.SideEffectType.DATAFLOW_SIDE_EFFECTING


def _xchg_copies(srcs, lands, send, recv, scatter):
    x, y, c = _place()
    me = 2 * x + y
    peers = [(1 - x, y), (x, 1 - y), (1 - x, 1 - y)]
    out = []
    for a in range(len(srcs)):
        for k, (px, py) in enumerate(peers):
            src = srcs[a].at[2 * px + py] if scatter else srcs[a]
            dst = lands[a].at[k] if scatter else lands[a].at[me]
            out.append(pltpu.make_async_remote_copy(src_ref=src, dst_ref=dst, send_sem=send.at[3 * a + k],
                                                    recv_sem=recv.at[3 * a + k], device_id=(px, py, c),
                                                    device_id_type=_MESH))
    return out


def _xchg_start(srcs, lands, *, name, scatter, dep=None):
    n = len(srcs)
    deps = [] if dep is None else [dep]
    nd = len(deps)

    def body(*refs):
        send, recv = refs[2 * n + nd], refs[2 * n + nd + 1]
        for cp in _xchg_copies(refs[:n], refs[n:2 * n], send, recv, scatter):
            cp.start()
        refs[-1][...] = jnp.zeros_like(refs[-1])

    hbm = lambda a: pltpu.HBM(a.shape, a.dtype)
    con = lambda a: pltpu.with_memory_space_constraint(a, pltpu.HBM)
    res = _pcall(
        body, name=name, in_specs=[_HBM] * (2 * n) + [_ANY] * nd,
        out_specs=[_SEM, _SEM] + [_HBM] * (2 * n) + [pl.BlockSpec(memory_space=pltpu.VMEM)],
        out_shape=[pltpu.SemaphoreType.DMA((3 * n,)), pltpu.SemaphoreType.DMA((3 * n,))]
        + [hbm(a) for a in srcs] + [hbm(a) for a in lands] + [jax.ShapeDtypeStruct((8, 128), F32)],
        input_output_aliases={i: 2 + i for i in range(2 * n)},
        compiler_params=pltpu.CompilerParams(has_side_effects=_EFFECT),
    )(*[con(a) for a in srcs], *[con(a) for a in lands], *deps)
    return res[0], res[1], list(res[2:2 + n]), list(res[2 + n:2 + 2 * n]), res[-1]


def _xchg_wait(send, recv, srcs, lands, after, *, name, scatter):
    n = len(srcs)
    afters = list(after) if isinstance(after, (list, tuple)) else [after]

    def body(*refs):
        s_ref, r_ref = refs[2 * n], refs[2 * n + 1]
        for cp in _xchg_copies(refs[:n], refs[n:2 * n], s_ref, r_ref, scatter):
            cp.wait_send()
            cp.wait_recv()

    hbm = lambda a: pltpu.HBM(a.shape, a.dtype)
    res = _pcall(
        body, name=name, in_specs=[_HBM] * (2 * n) + [_SEM, _SEM] + [_ANY] * len(afters),
        out_specs=[_HBM] * (2 * n),
        out_shape=[hbm(a) for a in srcs] + [hbm(a) for a in lands],
        input_output_aliases={i: i for i in range(2 * n)},
        compiler_params=pltpu.CompilerParams(has_side_effects=_EFFECT),
    )(*srcs, *lands, send, recv, *afters)
    return list(res[:n]), list(res[n:])


def _swap_cores(arrs, name="swap_cores"):
    n = len(arrs)

    def body(*refs):
        ins, outs = refs[:n], refs[n:2 * n]
        send, recv = refs[2 * n:]
        x, y, c = _place()
        cps = [pltpu.make_async_remote_copy(src_ref=ins[a], dst_ref=outs[a], send_sem=send.at[a], recv_sem=recv.at[a],
                                            device_id=(x, y, 1 - c), device_id_type=_MESH) for a in range(n)]
        for cp in cps:
            cp.start()
        for cp in cps:
            cp.wait_recv()
        for cp in cps:
            cp.wait_send()

    return _pcall(
        body, name=name, in_specs=[_ANY] * n, out_specs=[_ANY] * n,
        out_shape=[jax.ShapeDtypeStruct(s.shape, s.dtype) for s in arrs],
        scratch_shapes=[pltpu.SemaphoreType.DMA((n,)), pltpu.SemaphoreType.DMA((n,))],
    )(*arrs)


def _allreduce_small(v):
    rows = v.shape[0]
    half = rows // 2
    assert half % 8 == 0 and 2 * half == rows

    def body(v_ref, out_ref, sib_ref, pair_ref, slots_ref, send, recv):
        x, y, c = _place()
        chip = 2 * x + y
        sibling = (x, y, 1 - c)
        peers = [(1 - x, y), (x, 1 - y), (1 - x, 1 - y)]
        mine = pl.ds(pl.multiple_of(c * half, 8), half)

        first = pltpu.make_async_remote_copy(src_ref=v_ref, dst_ref=sib_ref, send_sem=send.at[0], recv_sem=recv.at[0],
                                             device_id=sibling, device_id_type=_MESH)
        first.start()
        first.wait_recv()
        pair_ref[...] = v_ref[...] + sib_ref[...]
        slots_ref[chip] = pair_ref[mine, :]
        cross = [pltpu.make_async_remote_copy(src_ref=pair_ref.at[mine], dst_ref=slots_ref.at[chip],
                                              send_sem=send.at[1 + k], recv_sem=recv.at[1 + k],
                                              device_id=(px, py, c), device_id_type=_MESH)
                 for k, (px, py) in enumerate(peers)]
        for cp in cross:
            cp.start()
        for cp in cross:
            cp.wait_recv()
        out_ref[mine, :] = ((slots_ref[0] + slots_ref[1]) + slots_ref[2]) + slots_ref[3]
        last = pltpu.make_async_remote_copy(src_ref=out_ref.at[mine], dst_ref=out_ref.at[mine], send_sem=send.at[4],
                                            recv_sem=recv.at[4], device_id=sibling, device_id_type=_MESH)
        last.start()
        last.wait_recv()
        first.wait_send()
        for cp in cross:
            cp.wait_send()
        last.wait_send()

    vm = pl.BlockSpec(memory_space=pltpu.VMEM)
    return _pcall(
        body, name="allreduce_small", in_specs=[vm], out_specs=vm,
        out_shape=jax.ShapeDtypeStruct((rows, 128), F32),
        scratch_shapes=[pltpu.VMEM((rows, 128), F32), pltpu.VMEM((rows, 128), F32), pltpu.VMEM((4, half, 128), F32),
                        pltpu.SemaphoreType.DMA((5,)), pltpu.SemaphoreType.DMA((5,))],
        compiler_params=_cp(None, 40),
    )(v)


def _sum_slots(own, land):
    ns, rows, cols = land.shape
    tm = _row_tile(rows, 256, 8)

    def body(own_ref, a_ref, o_ref):
        o_ref[...] = ((own_ref[...] + a_ref[0]) + a_ref[1]) + a_ref[2]

    return _pcall(
        body, name="sum_slots", grid=(rows // tm,),
        in_specs=[pl.BlockSpec((tm, cols), lambda i: (i, 0)), pl.BlockSpec((ns, tm, cols), lambda i: (0, i, 0))],
        out_specs=pl.BlockSpec((tm, cols), lambda i: (i, 0)),
        out_shape=jax.ShapeDtypeStruct((rows, cols), F32),
        compiler_params=_cp(("parallel",), 40),
    )(own, land)


def _adamw(w, m, v, g0, g1=None):
    rows, cols = w.shape[-2:]
    lead = w.ndim == 3
    tm = _row_tile(rows, 256, 8)
    c1 = 1.0 - ADAM_B1 ** ADAM_STEP
    c2 = 1.0 - ADAM_B2 ** ADAM_STEP
    two = g1 is not None

    def body(*refs):
        w_ref, m_ref, v_ref, g0_ref = refs[:4]
        g_ref, d_ref, nm_ref, nv_ref = refs[-4:]
        g = g0_ref[...]
        if two:
            g = g + refs[4][...]
        nm = ADAM_B1 * m_ref[...] + (1.0 - ADAM_B1) * g
        nv = ADAM_B2 * v_ref[...] + (1.0 - ADAM_B2) * (g * g)
        g_ref[...] = g
        nm_ref[...] = nm
        nv_ref[...] = nv
        d_ref[...] = -ADAM_LR * ((nm / c1) / (jnp.sqrt(nv / c2) + ADAM_EPS) + ADAM_WD * w_ref[...])

    blk = pl.BlockSpec((tm, cols), lambda i: (i, 0))
    wblk = pl.BlockSpec((None, tm, cols), lambda i: (0, i, 0)) if lead else blk
    ins = [w, m, v, g0] + ([g1] if two else [])
    return _pcall(
        body, name="adamw", grid=(rows // tm,), in_specs=[wblk] * 3 + [blk] * (len(ins) - 3), out_specs=[wblk] * 4,
        out_shape=[jax.ShapeDtypeStruct(w.shape, F32)] * 4,
        compiler_params=_cp(("parallel",), 40),
    )(*ins)


_BIG = ("w_in", "s5_w_glu", "m_w_out", "w_o", "w_up", "w_down")
_SMALL = ("ln0_g", "ln0_b", "b_in", "qk_conv_b", "s5_lambda_re", "s5_lambda_im", "s5_log_dt", "s5_b_re", "s5_b_im",
          "s5_c_re", "s5_c_im", "s5_d", "m_norm_g", "ln1_g", "ln1_b", "b_up", "ln2_g", "ln2_b")
_SMALL_SHARDED = ("meta_tokens", "qk_conv_w")
_ORDER = ("meta_tokens", "ln0_g", "ln0_b", "w_in", "b_in", "qk_conv_w", "qk_conv_b", "s5_lambda_re", "s5_lambda_im",
          "s5_log_dt", "s5_b_re", "s5_b_im", "s5_c_re", "s5_c_im", "s5_d", "s5_w_glu", "m_norm_g", "m_w_out", "w_o",
          "ln1_g", "ln1_b", "w_up", "b_up", "w_down", "ln2_g", "ln2_b")


def _pack(arrs):
    flat = jnp.concatenate([a.reshape(-1) for a in arrs])
    n = flat.shape[0]
    rows = -(-n // 2048) * 16
    return jnp.pad(flat, (0, rows * 128 - n)).reshape(rows, 128)


def _unpack(packed, shapes):
    flat = packed.reshape(-1)
    out, off = [], 0
    for s in shapes:
        n = math.prod(s)
        out.append(flat[off:off + n].reshape(s))
        off += n
    return out


def kernel(x, meta_tokens, ln0_g, ln0_b, w_in, b_in, qk_conv_w, qk_conv_b, s5_lambda_re, s5_lambda_im, s5_log_dt, s5_b_re, s5_b_im, s5_c_re, s5_c_im, s5_d, s5_w_glu, m_norm_g, m_w_out, w_o, ln1_g, ln1_b, w_up, b_up, w_down, ln2_g, ln2_b, loss_target, m_meta_tokens, m_ln0_g, m_ln0_b, m_w_in, m_b_in, m_qk_conv_w, m_qk_conv_b, m_s5_lambda_re, m_s5_lambda_im, m_s5_log_dt, m_s5_b_re, m_s5_b_im, m_s5_c_re, m_s5_c_im, m_s5_d, m_s5_w_glu, m_m_norm_g, m_m_w_out, m_w_o, m_ln1_g, m_ln1_b, m_w_up, m_b_up, m_w_down, m_ln2_g, m_ln2_b, v_meta_tokens, v_ln0_g, v_ln0_b, v_w_in, v_b_in, v_qk_conv_w, v_qk_conv_b, v_s5_lambda_re, v_s5_lambda_im, v_s5_log_dt, v_s5_b_re, v_s5_b_im, v_s5_c_re, v_s5_c_im, v_s5_d, v_s5_w_glu, v_m_norm_g, v_m_w_out, v_w_o, v_ln1_g, v_ln1_b, v_w_up, v_b_up, v_w_down, v_ln2_g, v_ln2_b):
    wts = dict(meta_tokens=meta_tokens, ln0_g=ln0_g, ln0_b=ln0_b, w_in=w_in, b_in=b_in, qk_conv_w=qk_conv_w,
               qk_conv_b=qk_conv_b, s5_lambda_re=s5_lambda_re, s5_lambda_im=s5_lambda_im, s5_log_dt=s5_log_dt,
               s5_b_re=s5_b_re, s5_b_im=s5_b_im, s5_c_re=s5_c_re, s5_c_im=s5_c_im, s5_d=s5_d, s5_w_glu=s5_w_glu,
               m_norm_g=m_norm_g, m_w_out=m_w_out, w_o=w_o, ln1_g=ln1_g, ln1_b=ln1_b, w_up=w_up, b_up=b_up,
               w_down=w_down, ln2_g=ln2_g, ln2_b=ln2_b)
    mom = dict(meta_tokens=m_meta_tokens, ln0_g=m_ln0_g, ln0_b=m_ln0_b, w_in=m_w_in, b_in=m_b_in, qk_conv_w=m_qk_conv_w,
               qk_conv_b=m_qk_conv_b, s5_lambda_re=m_s5_lambda_re, s5_lambda_im=m_s5_lambda_im, s5_log_dt=m_s5_log_dt,
               s5_b_re=m_s5_b_re, s5_b_im=m_s5_b_im, s5_c_re=m_s5_c_re, s5_c_im=m_s5_c_im, s5_d=m_s5_d,
               s5_w_glu=m_s5_w_glu, m_norm_g=m_m_norm_g, m_w_out=m_m_w_out, w_o=m_w_o, ln1_g=m_ln1_g, ln1_b=m_ln1_b,
               w_up=m_w_up, b_up=m_b_up, w_down=m_w_down, ln2_g=m_ln2_g, ln2_b=m_ln2_b)
    var = dict(meta_tokens=v_meta_tokens, ln0_g=v_ln0_g, ln0_b=v_ln0_b, w_in=v_w_in, b_in=v_b_in, qk_conv_w=v_qk_conv_w,
               qk_conv_b=v_qk_conv_b, s5_lambda_re=v_s5_lambda_re, s5_lambda_im=v_s5_lambda_im, s5_log_dt=v_s5_log_dt,
               s5_b_re=v_s5_b_re, s5_b_im=v_s5_b_im, s5_c_re=v_s5_c_re, s5_c_im=v_s5_c_im, s5_d=v_s5_d,
               s5_w_glu=v_s5_w_glu, m_norm_g=v_m_norm_g, m_w_out=v_m_w_out, w_o=v_w_o, ln1_g=v_ln1_g, ln1_b=v_ln1_b,
               w_up=v_w_up, b_up=v_b_up, w_down=v_w_down, ln2_g=v_ln2_g, ln2_b=v_ln2_b)
    d = x.shape[-1]
    chip = 2 * lax.axis_index("x") + lax.axis_index("y")

    gw = dict(zip(_SMALL_SHARDED, _gather_chips([meta_tokens, qk_conv_w[0]])))
    own_w_in = _bf(w_in[0])
    fsend, frecv, fsrc, fland, ftok = _xchg_start([own_w_in], [lax.empty((4,) + own_w_in.shape, BF16)],
                                                  name="gather_w_in_start", scatter=False, dep=gw["qk_conv_w"])
    late_names = tuple(n for n in _BIG if n != "w_in")
    cat = lambda a: jnp.transpose(a, (1, 0, 2)).reshape(a.shape[1], 4 * a.shape[2])
    w = dict(
        meta_tokens=cat(gw["meta_tokens"]), ln0_g=ln0_g[None], ln0_b=_tie(ln0_b[None], ftok),
        qk_conv_w=cat(gw["qk_conv_w"]), qk_conv_b=qk_conv_b,
        s5_lambda_re=s5_lambda_re[0], s5_lambda_im=s5_lambda_im[0], s5_log_dt=s5_log_dt[0][:, None],
        s5_b_re=s5_b_re[0], s5_b_im=s5_b_im[0], s5_c_re=s5_c_re[0], s5_c_im=s5_c_im[0], s5_d=s5_d,
        m_norm_g=m_norm_g, ln1_g=ln1_g, ln1_b=ln1_b, b_up=b_up, ln2_g=ln2_g, ln2_b=ln2_b)
    in_flight = {}

    def place_own(src, land):
        return lax.dynamic_update_slice(land, src[None], (chip,) + (0,) * src.ndim)

    def early(after):
        src, land = _xchg_wait(fsend, frecv, fsrc, fland, after, name="gather_w_in_wait", scatter=False)
        late_src = [_bf(wts[n][0]) for n in late_names]
        st = _xchg_start(late_src, [lax.empty((4,) + a.shape, a.dtype) for a in late_src], name="gather_late_start",
                         scatter=False, dep=src[0])
        in_flight["late"] = st[:4]
        return dict(w_in=_w_in_from_slots(place_own(src[0], land[0]), IN_CHUNK), b_in=_tie(_to_pad_cols(b_in), st[4]))

    def late(after):
        src, land = _xchg_wait(*in_flight["late"], after, name="gather_late_wait", scatter=False)
        full = {n: place_own(s, ld) for n, s, ld in zip(late_names, src, land)}
        return dict(s5_w_glu=full["s5_w_glu"], m_w_out=full["m_w_out"].reshape(d, d), w_o=full["w_o"].reshape(d, d),
                    w_up=full["w_up"], w_down=full["w_down"].reshape(4 * d, d))

    flying = []

    def ready(names, g):
        parts = dict(
            w_in=lambda: _slots_from_w_in(g["w_in"][0]), s5_w_glu=lambda: g["s5_w_glu"],
            m_w_out=lambda: g["m_w_out"].reshape(4, d // 4, d), w_o=lambda: g["w_o"].reshape(4, d // 4, d),
            w_up=lambda: g["w_up"], w_down=lambda: g["w_down"].reshape(4, d, d))
        src = [parts[n]() for n in names]
        land = [lax.empty((3,) + a.shape[1:], a.dtype) for a in src]
        st = _xchg_start(src, land, name="scatter_" + names[0] + "_start", scatter=True)
        flying.append((names,) + st[:4])
        return st[4]

    loss, grad_x, g = _local_step(x, loss_target, w, early, late, ready)
    g["b_in"] = _from_pad_cols(g["b_in"])

    res = {}

    def finish(groups, after, tag):
        mine = {}
        for names, send, recv, src, land in groups:
            src, land = _xchg_wait(send, recv, src, land, after, name="scatter_" + names[0] + "_wait", scatter=True)
            for n, s, ld in zip(names, src, land):
                mine[n] = _sum_slots(lax.dynamic_index_in_dim(s, chip, 0, keepdims=False), ld)
        theirs = _swap_cores(list(mine.values()), name="swap_cores_" + tag)
        for n, t in zip(mine, theirs):
            res[n] = _adamw(wts[n], mom[n], var[n], mine[n], t)

    finish(flying[:-1], g["ln0_g"], "a")

    small_shapes = [(1, 128)] + [wts[n].shape for n in _SMALL] + [g[n].shape for n in _SMALL_SHARDED]
    packed = _pack([loss] + [g[n] for n in _SMALL] + [g[n] for n in _SMALL_SHARDED])
    tot = _unpack(_allreduce_small(packed), small_shapes)
    loss_out = tot[0][0, 0]
    gsm = dict(zip(_SMALL + _SMALL_SHARDED, tot[1:]))
    for n in _SMALL_SHARDED:
        cols = wts[n].shape[-1]
        gsm[n] = lax.dynamic_slice_in_dim(gsm[n], chip * cols, cols, axis=1).reshape(wts[n].shape)

    names = _SMALL + _SMALL_SHARDED
    shapes = [wts[n].shape for n in names]
    pk = lambda dct: _pack([dct[n] for n in names])
    small_out = _adamw(pk(wts), pk(mom), pk(var), pk(gsm))
    small_res = [_unpack(r, shapes) for r in small_out]
    for j, n in enumerate(names):
        res[n] = [small_res[q][j] for q in range(4)]
    finish(flying[-1:], small_out[0], "b")

    return (loss_out, grad_x, *[res[n][0] for n in _ORDER], *[res[n][1] for n in _ORDER],
            *[res[n][2] for n in _ORDER], *[res[n][3] for n in _ORDER])
```

```python
import functools
import math

import jax
import jax.numpy as jnp
from jax import lax
from jax.experimental import pallas as pl
from jax.experimental.pallas import tpu as pltpu

F32 = jnp.float32
BF16 = jnp.bfloat16
HI = lax.Precision.HIGHEST

N_META = 16
M_HEADS = 4
M_CHUNK = 128
PAD = M_CHUNK - N_META
CONV_W = 4
S5_GROUP = 16
S5_STATE = 64
S5_KCH = 4
LN_EPS = 1e-5
ALPHA = 2.0 ** 0.25
NEG = -1e30
ADAM_LR, ADAM_B1, ADAM_B2, ADAM_EPS, ADAM_WD, ADAM_STEP = 0.001, 0.9, 0.999, 1e-08, 0.01, 10

O_OFF, GS_OFF, GM_OFF, V_OFF, Q_OFF, K_OFF, U_OFF, G_OFF, NP = 0, 1024, 2048, 3072, 4096, 4608, 5120, 5632, 5760

NN = ((1,), (0,))
NT = ((1,), (1,))
TN = ((0,), (0,))


def _dot(a, b, dims=NN, prec=None):
    return lax.dot_general(a, b, (dims, ((), ())), preferred_element_type=F32, precision=prec)


def _bf(x):
    return x.astype(BF16)


def _sig(x):
    return 0.5 * jnp.tanh(0.5 * x) + 0.5


def _pcall(body, **kw):
    return pl.pallas_call(body, **kw)


def _cp(sem=None, vmem_mb=None):
    kw = {}
    if sem is not None:
        kw["dimension_semantics"] = sem
    if vmem_mb is not None:
        kw["vmem_limit_bytes"] = vmem_mb << 20
    return pltpu.CompilerParams(**kw)


def _row_tile(n, want, mult=16):
    best = None
    for t in range(mult, want + 1, mult):
        if n % t == 0:
            best = t
    assert best is not None, (n, want)
    return best


def _resident(shape):
    nd = len(shape)
    return pl.BlockSpec(shape, lambda *_: (0,) * nd, pipeline_mode=pl.Buffered(1))


def _const(shape):
    nd = len(shape)
    return pl.BlockSpec(shape, lambda *_: (0,) * nd)


def _ln_fwd(x, g, b):
    mu = jnp.mean(x, axis=-1, keepdims=True)
    xc = x - mu
    var = jnp.mean(xc * xc, axis=-1, keepdims=True)
    rstd = lax.rsqrt(var + LN_EPS)
    xhat = xc * rstd
    return xhat * g + b, xhat, rstd


def _ln_bwd(dy, xhat, rstd, g):
    dxh = dy * g
    m1 = jnp.mean(dxh, axis=-1, keepdims=True)
    m2 = jnp.mean(dxh * xhat, axis=-1, keepdims=True)
    return rstd * (dxh - m1 - xhat * m2)


def _colsum(x):
    return jnp.sum(x, axis=0, keepdims=True)


def _to_pad_cols(w):
    u, q, k, v, o, gi, gf, gs, gm = (w[..., 0:512], w[..., 512:1024], w[..., 1024:1536], w[..., 1536:2560],
                                     w[..., 2560:3584], w[..., 3584:3588], w[..., 3588:3592], w[..., 3592:4616],
                                     w[..., 4616:5640])
    z = jnp.zeros(w.shape[:-1] + (NP - G_OFF - 8,), w.dtype)
    return jnp.concatenate([o, gs, gm, v, q, k, u, gi, gf, z], axis=-1)


def _from_pad_cols(w):
    o, gs, gm, v, q, k, u = (w[..., O_OFF:GS_OFF], w[..., GS_OFF:GM_OFF], w[..., GM_OFF:V_OFF], w[..., V_OFF:Q_OFF],
                             w[..., Q_OFF:K_OFF], w[..., K_OFF:U_OFF], w[..., U_OFF:G_OFF])
    gi, gf = w[..., G_OFF:G_OFF + 4], w[..., G_OFF + 4:G_OFF + 8]
    return jnp.concatenate([u, q, k, v, o, gi, gf, gs, gm], axis=-1)


_IN_REF = (("u", 512), ("q", 512), ("k", 512), ("v", 1024), ("o", 1024), ("i", 4), ("f", 4), ("gs", 1024), ("gm", 1024))
_IN_PAD = (("o", O_OFF), ("gs", GS_OFF), ("gm", GM_OFF), ("v", V_OFF), ("q", Q_OFF), ("k", K_OFF), ("u", U_OFF),
           ("i", G_OFF), ("f", G_OFF + 4))


def _in_ref_ranges():
    out, off = {}, 0
    for n, s in _IN_REF:
        out[n] = (off, off + s)
        off += s
    return out, off


def _w_in_from_slots(g, chunk=None):
    rng, total = _in_ref_ranges()
    width = total // g.shape[0]
    cols = []
    for n, _ in _IN_PAD:
        a, b = rng[n]
        while a < b:
            s = a // width
            e = min(b, (s + 1) * width)
            cols.append(g[s][:, a - s * width:e - s * width])
            a = e
    cols.append(jnp.zeros((g.shape[1], NP - G_OFF - 8), g.dtype))
    if chunk is None:
        return jnp.concatenate(cols, axis=1)
    chunks, cur, room = [], [], chunk
    for c in cols:
        while c.shape[1] > 0:
            take = min(room, c.shape[1])
            cur.append(c[:, :take])
            c, room = c[:, take:], room - take
            if room == 0:
                chunks.append(jnp.concatenate(cur, axis=1))
                cur, room = [], chunk
    assert not cur
    return jnp.stack(chunks, axis=0)


def _slots_from_w_in(wp, nslot=4):
    rng, total = _in_ref_ranges()
    width = total // nslot
    pad_off = dict(_IN_PAD)
    slots = []
    for s in range(nslot):
        lo, hi = s * width, (s + 1) * width
        cols = []
        for n, _ in _IN_REF:
            a, b = rng[n]
            x0, x1 = max(a, lo), min(b, hi)
            if x0 < x1:
                cols.append(wp[:, pad_off[n] + x0 - a:pad_off[n] + x1 - a])
        slots.append(jnp.concatenate(cols, axis=1))
    return jnp.stack(slots, axis=0)


def _ln0_fwd(hin, g, b, lp):
    r, d = hin.shape
    tm = _row_tile(lp, 416)

    def body(x_ref, g_ref, b_ref, o_ref, ob_ref):
        y, _, _ = _ln_fwd(x_ref[...], g_ref[...], b_ref[...])
        o_ref[...] = y
        ob_ref[...] = _bf(y)

    row = pl.BlockSpec((tm, d), lambda i: (i, 0))
    return _pcall(
        body, name="ln0_fwd", grid=(r // tm,),
        in_specs=[row, _const((1, d)), _const((1, d))],
        out_specs=[row, row],
        out_shape=[jax.ShapeDtypeStruct((r, d), F32), jax.ShapeDtypeStruct((r, d), BF16)],
        compiler_params=_cp(("parallel",)),
    )(hin, g, b)


def _ln0_bwd(hin, dr1, dpw, g, lp):
    r, d = hin.shape
    tm = _row_tile(lp, 416)
    tps = lp // tm
    assert tm >= PAD + N_META

    def body(x_ref, a_ref, c_ref, g_ref, o_ref, dg_ref, db_ref, dm_ref):
        i = pl.program_id(0)

        @pl.when(i == 0)
        def _():
            dg_ref[...] = jnp.zeros_like(dg_ref)
            db_ref[...] = jnp.zeros_like(db_ref)
            dm_ref[...] = jnp.zeros_like(dm_ref)

        dy = ALPHA * a_ref[...] + c_ref[...]
        _, xhat, rstd = _ln_fwd(x_ref[...], g_ref[...], 0.0)
        dx = _ln_bwd(dy, xhat, rstd, g_ref[...])
        o_ref[...] = dx
        dg_ref[...] += _colsum(dy * xhat)
        db_ref[...] += _colsum(dy)

        @pl.when(i % tps == 0)
        def _():
            dm_ref[...] += dx[PAD:PAD + N_META, :]

    return _pcall(
        body, name="ln0_bwd", grid=(r // tm,),
        in_specs=[pl.BlockSpec((tm, d), lambda i: (i, 0))] * 3 + [_const((1, d))],
        out_specs=[pl.BlockSpec((tm, d), lambda i: (i, 0)), _const((1, d)), _const((1, d)), _const((N_META, d))],
        out_shape=[jax.ShapeDtypeStruct((r, d), F32), jax.ShapeDtypeStruct((1, d), F32),
                   jax.ShapeDtypeStruct((1, d), F32), jax.ShapeDtypeStruct((N_META, d), F32)],
        compiler_params=_cp(("arbitrary",)),
    )(hin, dr1, dpw, g)


IN_CHUNK = 1152


def _chunk_cols(w):
    k, n = w.shape
    return jnp.transpose(w.reshape(k, n // IN_CHUNK, IN_CHUNK), (1, 0, 2))


def _inproj(h0b, w3, bias, lp):
    r, d = h0b.shape
    nj, _, tn = w3.shape
    tm = _row_tile(lp, 832)
    tps = lp // tm

    def body(a_ref, w_ref, b_ref, o_ref):
        i = pl.program_id(0)
        j = pl.program_id(1)
        acc = _dot(a_ref[...], w_ref[j]) + b_ref[...]
        t = (i % tps) * tm + lax.broadcasted_iota(jnp.int32, (tm, 1), 0)
        o_ref[...] = jnp.where(t >= PAD, acc, 0.0)

    return _pcall(
        body, name="inproj", grid=(r // tm, nj),
        in_specs=[pl.BlockSpec((tm, d), lambda i, j: (i, 0)), _resident(w3.shape),
                  pl.BlockSpec((1, tn), lambda i, j: (0, j))],
        out_specs=pl.BlockSpec((tm, tn), lambda i, j: (i, j)),
        out_shape=jax.ShapeDtypeStruct((r, nj * tn), F32),
        compiler_params=_cp(("parallel", "arbitrary"), 48),
    )(h0b, w3, bias)


def _mm_tn(a, b, *, name, split=1, colsum=False, tk_want=1408):
    r, m = a.shape
    n = b.shape[1]
    tk = _row_tile(r, tk_want)
    tm = min(m, 1024)
    ns = n // split
    tn = ns
    for cand in (1024, 1152, 640, 512, 128):
        if ns % cand == 0 and cand <= ns:
            tn = cand
            break
    nb = ns // tn
    nk = r // tk

    def body(a_ref, b_ref, o_ref, *rest):
        acc = rest[-1]
        k = pl.program_id(2)

        @pl.when(k == 0)
        def _():
            acc[...] = jnp.zeros_like(acc)

        bt = b_ref[...]
        acc[...] += _dot(_bf(a_ref[...]), _bf(bt), TN)

        @pl.when(k == nk - 1)
        def _():
            o_ref[...] = acc[...]

        if colsum:
            cs_ref = rest[0]

            @pl.when(k == 0)
            def _():
                cs_ref[...] = jnp.zeros_like(cs_ref)

            cs_ref[...] += _colsum(bt.astype(F32))

    out_specs = [pl.BlockSpec((None, tm, tn), lambda i, j, k: (j // nb, i, j % nb))]
    out_shape = [jax.ShapeDtypeStruct((split, m, ns), F32)]
    if colsum:
        assert m == tm
        out_specs.append(pl.BlockSpec((1, tn), lambda i, j, k: (0, j)))
        out_shape.append(jax.ShapeDtypeStruct((1, n), F32))
    res = _pcall(
        body, name=name, grid=(m // tm, n // tn, nk),
        in_specs=[pl.BlockSpec((tk, tm), lambda i, j, k: (k, i)), pl.BlockSpec((tk, tn), lambda i, j, k: (k, j))],
        out_specs=out_specs, out_shape=out_shape,
        scratch_shapes=[pltpu.VMEM((tm, tn), F32)],
        compiler_params=_cp(("parallel", "parallel", "arbitrary"), 56),
    )(a, b)
    return res if colsum else res[0]


def _mm_nt(a, w3, lp, *, name, dep=None):
    r, kdim = a.shape
    nk, n, tk = w3.shape
    assert nk * tk == kdim
    tm = _row_tile(lp, 832)
    deps = [] if dep is None else [dep]

    def body(a_ref, w_ref, *rest):
        o_ref, acc = rest[-2:]
        k = pl.program_id(1)

        @pl.when(k == 0)
        def _():
            acc[...] = jnp.zeros_like(acc)

        acc[...] += _dot(_bf(a_ref[...]), w_ref[k], NT)

        @pl.when(k == nk - 1)
        def _():
            o_ref[...] = acc[...]

    return _pcall(
        body, name=name, grid=(r // tm, nk),
        in_specs=[pl.BlockSpec((tm, tk), lambda i, k: (i, k)), _resident(w3.shape)]
        + [_const(dp_.shape) for dp_ in deps],
        out_specs=pl.BlockSpec((tm, n), lambda i, k: (i, 0)),
        out_shape=jax.ShapeDtypeStruct((r, n), F32),
        scratch_shapes=[pltpu.VMEM((tm, n), F32)],
        compiler_params=_cp(("parallel", "arbitrary"), 48),
    )(a, w3, *deps)


def _s5_prep(lam_re, lam_im, log_dt, b_re_t, b_im_t):
    g, p = lam_re.shape
    h = b_re_t.shape[0]

    def body(lr_ref, li_ref, ldt_ref, br_ref, bi_ref, pr_ref, pi_ref, bbr_ref, bbi_ref):
        lr, li = lr_ref[...], li_ref[...]
        dt = jnp.exp(ldt_ref[...])
        e = jnp.exp(lr * dt)
        ar, ai = e * jnp.cos(li * dt), e * jnp.sin(li * dt)
        den = lr * lr + li * li
        cr = ((ar - 1.0) * lr + ai * li) / den
        ci = (ai * lr - (ar - 1.0) * li) / den
        br, bi = br_ref[...], bi_ref[...]
        bbr_ref[...] = cr[None] * br - ci[None] * bi
        bbi_ref[...] = cr[None] * bi + ci[None] * br
        xr, xi = ar, ai
        pr_ref[0] = xr
        pi_ref[0] = xi
        for t in range(1, 8):
            xr, xi = xr * ar - xi * ai, xr * ai + xi * ar
            pr_ref[t] = xr
            pi_ref[t] = xi

    sd = jax.ShapeDtypeStruct
    return _pcall(body, name="s5_prep",
                  out_shape=[sd((8, g, p), F32), sd((8, g, p), F32), sd((h, g, p), F32), sd((h, g, p), F32)])(
        lam_re, lam_im, log_dt, b_re_t, b_im_t)


def _s5_prep_bwd(lam_re, lam_im, log_dt, b_re_t, b_im_t, da_re, da_im, dbb_re_t, dbb_im_t):
    g, p = lam_re.shape
    h = b_re_t.shape[0]

    def body(lr_ref, li_ref, ldt_ref, br_ref, bi_ref, dar_ref, dai_ref, dbr_ref, dbi_ref,
             glr_ref, gli_ref, gdt_ref, gbr_ref, gbi_ref):
        lr, li = lr_ref[...], li_ref[...]
        dt = jnp.exp(ldt_ref[...])
        e = jnp.exp(lr * dt)
        ar, ai = e * jnp.cos(li * dt), e * jnp.sin(li * dt)
        den = lr * lr + li * li
        cr = ((ar - 1.0) * lr + ai * li) / den
        ci = (ai * lr - (ar - 1.0) * li) / den
        br, bi = br_ref[...], bi_ref[...]
        gr, gi = dbr_ref[...], dbi_ref[...]
        gbr_ref[...] = gr * cr[None] + gi * ci[None]
        gbi_ref[...] = gi * cr[None] - gr * ci[None]
        gcr = jnp.sum(gr * br + gi * bi, axis=0)
        gci = jnp.sum(gi * br - gr * bi, axis=0)
        ilr, ili = lr / den, -li / den
        gar = dar_ref[...] + gcr * ilr + gci * ili
        gai = dai_ref[...] + gci * ilr - gcr * ili
        qr, qi = cr * ilr - ci * ili, cr * ili + ci * ilr
        glr = -(gcr * qr + gci * qi)
        gli = -(gci * qr - gcr * qi)
        gzr = gar * ar + gai * ai
        gzi = gai * ar - gar * ai
        glr_ref[...] = glr + gzr * dt
        gli_ref[...] = gli + gzi * dt
        gdt_ref[...] = jnp.sum(gzr * lr + gzi * li, axis=1, keepdims=True) * dt

    sd = jax.ShapeDtypeStruct
    return _pcall(body, name="s5_prep_bwd",
                  out_shape=[sd((g, p), F32), sd((g, p), F32), sd((g, 1), F32), sd((h, g, p), F32), sd((h, g, p), F32)])(
        lam_re, lam_im, log_dt, b_re_t, b_im_t, da_re, da_im, dbb_re_t, dbb_im_t)


def _cmul(xr, xi, yr, yi):
    return xr * yr - xi * yi, xr * yi + xi * yr


def _dot5(a, b, dims=NN):
    return _dot(_bf(a), _bf(b), dims)


def _s5_fwd(p3, bk, cre, cim, apow, dskip):
    bsz, lp, _ = p3.shape
    tt = _row_tile(lp, 528, 8)
    nt = lp // tt
    nblk = tt // 8
    hw = 512

    def body(u_ref, bk_ref, cre_ref, cim_ref, ap_ref, d_ref, y_ref, xs_ref, car_ref):
        t = pl.program_id(2)

        @pl.when(t == 0)
        def _():
            car_ref[...] = jnp.zeros_like(car_ref)

        u = u_ref[...]
        xs_ref[...] = _dot5(u, bk_ref[...])
        ap = ap_ref[...]
        apr, api = ap[:, :hw], ap[:, hw:]
        rows = lax.broadcasted_iota(jnp.int32, (8, hw), 0)
        lev = [(d, jnp.where(rows < d, 0.0, jnp.broadcast_to(apr[d - 1:d, :], (8, hw))),
                jnp.where(rows < d, 0.0, jnp.broadcast_to(api[d - 1:d, :], (8, hw)))) for d in (1, 2, 4)]

        def blk(i, carry):
            cr, ci = carry
            off = pl.multiple_of(i * 8, 8)
            x = xs_ref[pl.ds(off, 8), :]
            xr, xi = x[:, :hw], x[:, hw:]
            for d, lr, li in lev:
                mr, mi = _cmul(pltpu.roll(xr, d, 0), pltpu.roll(xi, d, 0), lr, li)
                xr, xi = xr + mr, xi + mi
            mr, mi = _cmul(apr, api, cr, ci)
            xr, xi = xr + mr, xi + mi
            xs_ref[pl.ds(off, 8), :] = jnp.concatenate([xr, xi], axis=1)
            return xr[7:8, :], xi[7:8, :]

        c0 = car_ref[...]
        cr, ci = lax.fori_loop(0, nblk, blk, (c0[0:1, :hw], c0[0:1, hw:]))
        car_ref[...] = jnp.broadcast_to(jnp.concatenate([cr, ci], axis=1), car_ref.shape)
        xs = xs_ref[...]
        y_ref[...] = (_dot5(xs[:, :hw], cre_ref[...]) - _dot5(xs[:, hw:], cim_ref[...])
                      + d_ref[...] * u)

    ub = U_OFF // 128
    return _pcall(
        body, name="s5_fwd", grid=(S5_KCH, bsz, nt),
        in_specs=[pl.BlockSpec((None, tt, 128), lambda k, b, t: (b, t, ub + k)),
                  pl.BlockSpec((None, 128, 2 * hw), lambda k, b, t: (k, 0, 0)),
                  pl.BlockSpec((None, hw, 128), lambda k, b, t: (k, 0, 0)),
                  pl.BlockSpec((None, hw, 128), lambda k, b, t: (k, 0, 0)),
                  pl.BlockSpec((None, 8, 2 * hw), lambda k, b, t: (k, 0, 0)),
                  pl.BlockSpec((1, 128), lambda k, b, t: (0, k))],
        out_specs=[pl.BlockSpec((None, tt, 128), lambda k, b, t: (b, t, k)),
                   pl.BlockSpec((None, None, tt, 2 * hw), lambda k, b, t: (b, k, t, 0))],
        out_shape=[jax.ShapeDtypeStruct((bsz, lp, S5_KCH * 128), F32),
                   jax.ShapeDtypeStruct((bsz, S5_KCH, lp, 2 * hw), F32)],
        scratch_shapes=[pltpu.VMEM((8, 2 * hw), F32)],
        compiler_params=_cp(("parallel", "parallel", "arbitrary"), 40),
    )(p3, bk, cre, cim, apow, dskip)


def _s5_bwd(dp3, p3, dy3, xs, bk, cre, cim, apow_rev, dskip):
    bsz, lp, _ = p3.shape
    tt = _row_tile(lp, 528, 8)
    nt = lp // tt
    nblk = tt // 8
    hw = 512
    tb = tt // 8

    def body(dp_any, u_ref, dy_ref, xs_ref, halo_ref, bk_ref, cre_ref, cim_ref, ap_ref, d_ref,
             du_ref, dbk_ref, dcre_ref, dcim_ref, da_ref, dd_ref, g_ref, ext_ref, car_ref):
        b = pl.program_id(1)
        t = pl.program_id(2)
        tidx = nt - 1 - t

        @pl.when(t == 0)
        def _():
            car_ref[...] = jnp.zeros_like(car_ref)

        @pl.when((b == 0) & (t == 0))
        def _():
            dbk_ref[...] = jnp.zeros_like(dbk_ref)
            dcre_ref[...] = jnp.zeros_like(dcre_ref)
            dcim_ref[...] = jnp.zeros_like(dcim_ref)
            da_ref[...] = jnp.zeros_like(da_ref)
            dd_ref[...] = jnp.zeros_like(dd_ref)

        u = u_ref[...]
        dy = dy_ref[...]
        g_ref[:, :hw] = _dot5(dy, cre_ref[...], NT)
        g_ref[:, hw:] = -_dot5(dy, cim_ref[...], NT)
        ap = ap_ref[...]
        apr, api = ap[:, :hw], -ap[:, hw:]
        rows = lax.broadcasted_iota(jnp.int32, (8, hw), 0)
        lev = [(d, jnp.where(rows >= 8 - d, 0.0, jnp.broadcast_to(apr[8 - d:9 - d, :], (8, hw))),
                jnp.where(rows >= 8 - d, 0.0, jnp.broadcast_to(api[8 - d:9 - d, :], (8, hw)))) for d in (1, 2, 4)]

        def blk(i, carry):
            cr, ci = carry
            off = pl.multiple_of((nblk - 1 - i) * 8, 8)
            x = g_ref[pl.ds(off, 8), :]
            xr, xi = x[:, :hw], x[:, hw:]
            for d, lr, li in lev:
                mr, mi = _cmul(pltpu.roll(xr, 8 - d, 0), pltpu.roll(xi, 8 - d, 0), lr, li)
                xr, xi = xr + mr, xi + mi
            mr, mi = _cmul(apr, api, cr, ci)
            xr, xi = xr + mr, xi + mi
            g_ref[pl.ds(off, 8), :] = jnp.concatenate([xr, xi], axis=1)
            return xr[0:1, :], xi[0:1, :]

        c0 = car_ref[...]
        cr, ci = lax.fori_loop(0, nblk, blk, (c0[0:1, :hw], c0[0:1, hw:]))
        car_ref[...] = jnp.broadcast_to(jnp.concatenate([cr, ci], axis=1), car_ref.shape)

        gg = g_ref[...]
        du = _dot5(gg, bk_ref[...], NT) + d_ref[...] * dy
        trow = tidx * tt + lax.broadcasted_iota(jnp.int32, (tt, 1), 0)
        du_ref[...] = jnp.where(trow >= PAD, du, 0.0).astype(du_ref.dtype)
        dbk_ref[...] += _dot5(u, gg, TN)
        xsv = xs_ref[...]
        dcre_ref[...] += _dot5(xsv[:, :hw], dy, TN)
        dcim_ref[...] -= _dot5(xsv[:, hw:], dy, TN)
        dd_ref[...] += _colsum(dy * u)
        ext_ref[0:8, :] = jnp.where(tidx == 0, 0.0, halo_ref[...])
        ext_ref[8:, :] = xsv
        xp = ext_ref[pl.ds(7, tt), :]
        gr, gi, pr, pi = gg[:, :hw], gg[:, hw:], xp[:, :hw], xp[:, hw:]
        da_ref[:, :hw] += _colsum(gr * pr + gi * pi)
        da_ref[:, hw:] += _colsum(gi * pr - gr * pi)

    ub = U_OFF // 128
    sd = jax.ShapeDtypeStruct
    rt = lambda t: nt - 1 - t
    res = _pcall(
        body, name="s5_bwd", grid=(S5_KCH, bsz, nt),
        in_specs=[pl.BlockSpec(memory_space=pl.ANY),
                  pl.BlockSpec((None, tt, 128), lambda k, b, t: (b, rt(t), ub + k)),
                  pl.BlockSpec((None, tt, 128), lambda k, b, t: (b, rt(t), k)),
                  pl.BlockSpec((None, None, tt, 2 * hw), lambda k, b, t: (b, k, rt(t), 0)),
                  pl.BlockSpec((None, None, 8, 2 * hw), lambda k, b, t: (b, k, jnp.maximum(rt(t) * tb - 1, 0), 0)),
                  pl.BlockSpec((None, 128, 2 * hw), lambda k, b, t: (k, 0, 0)),
                  pl.BlockSpec((None, hw, 128), lambda k, b, t: (k, 0, 0)),
                  pl.BlockSpec((None, hw, 128), lambda k, b, t: (k, 0, 0)),
                  pl.BlockSpec((None, 8, 2 * hw), lambda k, b, t: (k, 0, 0)),
                  pl.BlockSpec((1, 128), lambda k, b, t: (0, k))],
        out_specs=[pl.BlockSpec((None, tt, 128), lambda k, b, t: (b, rt(t), ub + k)),
                   pl.BlockSpec((None, 128, 2 * hw), lambda k, b, t: (k, 0, 0)),
                   pl.BlockSpec((None, hw, 128), lambda k, b, t: (k, 0, 0)),
                   pl.BlockSpec((None, hw, 128), lambda k, b, t: (k, 0, 0)),
                   pl.BlockSpec((None, 1, 2 * hw), lambda k, b, t: (k, 0, 0)),
                   pl.BlockSpec((1, 128), lambda k, b, t: (0, k))],
        out_shape=[sd(dp3.shape, dp3.dtype), sd((S5_KCH, 128, 2 * hw), F32), sd((S5_KCH, hw, 128), F32),
                   sd((S5_KCH, hw, 128), F32), sd((S5_KCH, 1, 2 * hw), F32), sd((1, S5_KCH * 128), F32)],
        scratch_shapes=[pltpu.VMEM((tt, 2 * hw), F32), pltpu.VMEM((tt + 8, 2 * hw), F32), pltpu.VMEM((8, 2 * hw), F32)],
        input_output_aliases={0: 0},
        compiler_params=_cp(("arbitrary", "arbitrary", "arbitrary"), 48),
    )(dp3, p3, dy3, xs, xs, bk, cre, cim, apow_rev, dskip)
    return res


_G0 = math.sqrt(2.0 / math.pi)
_G1 = 0.044715


def _gelu(y):
    return 0.5 * y * (1.0 + jnp.tanh(_G0 * (y + _G1 * y * y * y)))


def _gelu_grad(y):
    th = jnp.tanh(_G0 * (y + _G1 * y * y * y))
    return 0.5 * (1.0 + th) + 0.5 * y * (1.0 - th * th) * _G0 * (1.0 + 3.0 * _G1 * y * y)


def _glu_fwd(y_s5, wglu_g, lp):
    r, w = y_s5.shape
    tm = _row_tile(lp, 416)
    cw = wglu_g.shape[2]

    def body(y_ref, w_ref, gy_ref, z_ref, o_ref):
        gy = _bf(_gelu(y_ref[...]))
        gy_ref[...] = gy
        zs = [_dot(gy, w_ref[s]) for s in range(4)]
        for s in range(4):
            z_ref[:, s * cw:(s + 1) * cw] = zs[s]
        o_ref[:, :cw] = zs[0] * _sig(zs[2])
        o_ref[:, cw:] = zs[1] * _sig(zs[3])

    sd = jax.ShapeDtypeStruct
    return _pcall(
        body, name="glu_fwd", grid=(r // tm,),
        in_specs=[pl.BlockSpec((tm, w), lambda i: (i, 0)), _resident(wglu_g.shape)],
        out_specs=[pl.BlockSpec((tm, w), lambda i: (i, 0)), pl.BlockSpec((tm, 4 * cw), lambda i: (i, 0)),
                   pl.BlockSpec((tm, 2 * cw), lambda i: (i, 0))],
        out_shape=[sd((r, w), BF16), sd((r, 4 * cw), F32), sd((r, 2 * cw), F32)],
        compiler_params=_cp(("parallel",), 40),
    )(y_s5, wglu_g)


def _glu_bwd(dyg, z, y_s5, wglu_g, lp):
    r, w = y_s5.shape
    tm = _row_tile(lp, 416)
    cw = wglu_g.shape[2]

    def body(d_ref, z_ref, y_ref, w_ref, dz_ref, dy_ref):
        d = d_ref[...]
        zz = z_ref[...]
        acc = jnp.zeros((tm, w), F32)
        for s in range(2):
            z1 = zz[:, s * cw:(s + 1) * cw]
            sg = _sig(zz[:, (2 + s) * cw:(3 + s) * cw])
            dd = d[:, s * cw:(s + 1) * cw]
            dz1 = _bf(dd * sg)
            dz2 = _bf(dd * z1 * sg * (1.0 - sg))
            dz_ref[:, s * cw:(s + 1) * cw] = dz1
            dz_ref[:, (2 + s) * cw:(3 + s) * cw] = dz2
            acc += _dot(dz1, w_ref[s], NT) + _dot(dz2, w_ref[2 + s], NT)
        dy_ref[...] = acc * _gelu_grad(y_ref[...])

    sd = jax.ShapeDtypeStruct
    return _pcall(
        body, name="glu_bwd", grid=(r // tm,),
        in_specs=[pl.BlockSpec((tm, 2 * cw), lambda i: (i, 0)), pl.BlockSpec((tm, 4 * cw), lambda i: (i, 0)),
                  pl.BlockSpec((tm, w), lambda i: (i, 0)), _resident(wglu_g.shape)],
        out_specs=[pl.BlockSpec((tm, 4 * cw), lambda i: (i, 0)), pl.BlockSpec((tm, w), lambda i: (i, 0))],
        out_shape=[sd((r, 4 * cw), BF16), sd((r, w), F32)],
        compiler_params=_cp(("parallel",), 40),
    )(dyg, z, y_s5, wglu_g)


def _conv_fwd(p3, cw, cb):
    bsz, lp, _ = p3.shape
    tt = _row_tile(lp, 416)
    nt = lp // tt
    tb = tt // 8
    c = cw.shape[1]
    qb = Q_OFF // c

    def body(x_ref, halo_ref, w_ref, b_ref, pre_ref, act_ref, ext_ref):
        t = pl.program_id(1)
        ext_ref[0:8, :] = jnp.where(t == 0, 0.0, halo_ref[...])
        ext_ref[8:, :] = x_ref[...]
        w = w_ref[...]
        acc = b_ref[...] + w[0:1, :] * ext_ref[pl.ds(5, tt), :]
        for j in range(1, CONV_W):
            acc = acc + w[j:j + 1, :] * ext_ref[pl.ds(5 + j, tt), :]
        pre_ref[...] = acc
        act_ref[...] = acc * _sig(acc)

    sd = jax.ShapeDtypeStruct
    return _pcall(
        body, name="conv_fwd", grid=(bsz, nt),
        in_specs=[pl.BlockSpec((None, tt, c), lambda b, t: (b, t, qb)),
                  pl.BlockSpec((None, 8, c), lambda b, t: (b, jnp.maximum(t * tb - 1, 0), qb)),
                  _const((CONV_W, c)), _const((1, c))],
        out_specs=[pl.BlockSpec((None, tt, c), lambda b, t: (b, t, 0))] * 2,
        out_shape=[sd((bsz, lp, c), F32)] * 2,
        scratch_shapes=[pltpu.VMEM((tt + 8, c), F32)],
        compiler_params=_cp(("parallel", "parallel")),
    )(p3, p3, cw, cb)


def _conv_bwd(dp3, p3, dact3, pre3, cw):
    bsz, lp, _ = p3.shape
    tt = _row_tile(lp, 416)
    nt = lp // tt
    tb = tt // 8
    c = cw.shape[1]
    qb = Q_OFF // c

    def silu_grad(x):
        s = _sig(x)
        return s * (1.0 + x * (1.0 - s))

    def body(dp_any, x_ref, xh_ref, d_ref, dh_ref, pre_ref, preh_ref, w_ref, o_ref, dw_ref, db_ref, ext_ref, dext_ref):
        b = pl.program_id(0)
        t = pl.program_id(1)

        @pl.when((b == 0) & (t == 0))
        def _():
            dw_ref[...] = jnp.zeros_like(dw_ref)
            db_ref[...] = jnp.zeros_like(db_ref)

        dc = d_ref[...] * silu_grad(pre_ref[...])
        dch = jnp.where(t == nt - 1, 0.0, dh_ref[...] * silu_grad(preh_ref[...]))
        dext_ref[0:tt, :] = dc
        dext_ref[tt:, :] = dch
        ext_ref[0:8, :] = jnp.where(t == 0, 0.0, xh_ref[...])
        ext_ref[8:, :] = x_ref[...]
        w = w_ref[...]
        acc = w[CONV_W - 1:CONV_W, :] * dc
        for j in range(CONV_W - 1):
            acc = acc + w[j:j + 1, :] * dext_ref[pl.ds(CONV_W - 1 - j, tt), :]
        trow = t * tt + lax.broadcasted_iota(jnp.int32, (tt, 1), 0)
        o_ref[...] = jnp.where(trow >= PAD, acc, 0.0).astype(o_ref.dtype)
        db_ref[...] += _colsum(dc)
        for j in range(CONV_W):
            dw_ref[j:j + 1, :] += _colsum(dc * ext_ref[pl.ds(5 + j, tt), :])

    sd = jax.ShapeDtypeStruct
    nxt = lambda t: jnp.minimum((t + 1) * tb, lp // 8 - 1)
    return _pcall(
        body, name="conv_bwd", grid=(bsz, nt),
        in_specs=[pl.BlockSpec(memory_space=pl.ANY),
                  pl.BlockSpec((None, tt, c), lambda b, t: (b, t, qb)),
                  pl.BlockSpec((None, 8, c), lambda b, t: (b, jnp.maximum(t * tb - 1, 0), qb)),
                  pl.BlockSpec((None, tt, c), lambda b, t: (b, t, 0)),
                  pl.BlockSpec((None, 8, c), lambda b, t: (b, nxt(t), 0)),
                  pl.BlockSpec((None, tt, c), lambda b, t: (b, t, 0)),
                  pl.BlockSpec((None, 8, c), lambda b, t: (b, nxt(t), 0)),
                  _const((CONV_W, c))],
        out_specs=[pl.BlockSpec((None, tt, c), lambda b, t: (b, t, qb)), _const((CONV_W, c)), _const((1, c))],
        out_shape=[sd(dp3.shape, dp3.dtype), sd((CONV_W, c), F32), sd((1, c), F32)],
        scratch_shapes=[pltpu.VMEM((tt + 8, c), F32), pltpu.VMEM((tt + 8, c), F32)],
        input_output_aliases={0: 0},
        compiler_params=_cp(("arbitrary", "arbitrary")),
    )(dp3, p3, p3, dact3, dact3, pre3, pre3, cw)


def _mlstm_gates(g, h_idx, c_idx, lc):
    lane = lax.broadcasted_iota(jnp.int32, g.shape, 1)
    i_col = jnp.sum(jnp.where(lane == h_idx, g, 0.0), axis=1, keepdims=True)
    f_col = jnp.sum(jnp.where(lane == M_HEADS + h_idx, g, 0.0), axis=1, keepdims=True)
    row = lax.broadcasted_iota(jnp.int32, (lc, 1), 0)
    valid = (c_idx * lc + row) >= PAD
    li = jnp.where(valid, i_col, NEG)
    lf = jnp.where(valid, jnp.minimum(f_col, 0.0) - jnp.log(1.0 + jnp.exp(-jnp.abs(f_col))), 0.0)
    r2 = lax.broadcasted_iota(jnp.int32, (lc, lc), 0)
    c2 = lax.broadcasted_iota(jnp.int32, (lc, lc), 1)
    eye = r2 == c2
    tril = r2 >= c2
    to_row = lambda col: jnp.sum(jnp.where(eye, col, 0.0), axis=0, keepdims=True)
    lf_row = to_row(lf)
    b_col = jnp.sum(jnp.where(tril, lf_row, 0.0), axis=1, keepdims=True)
    b_row = to_row(b_col)
    li_row = to_row(li)
    d_mat = jnp.where(tril, b_col - b_row + li_row, NEG)
    return dict(f_col=f_col, valid=valid, li=li, b_col=b_col, d_mat=d_mat, eye=eye, r2=r2, c2=c2, row=row,
                to_row=to_row)


def _mlstm_chunk(q, ks, v, gq, c_st, n_st, m_st, lc):
    b_col, d_mat = gq["b_col"], gq["d_mat"]
    m_inter = b_col + m_st
    m_row = jnp.maximum(m_inter, jnp.max(d_mat, axis=1, keepdims=True))
    w_intra = jnp.exp(d_mat - m_row)
    w_inter = jnp.exp(m_inter - m_row)
    qb, kb, vb, cb = _bf(q), _bf(ks), _bf(v), _bf(c_st)
    s = _dot(qb, kb, NT) * w_intra
    qc = _dot(qb, cb)
    num = _dot(_bf(s), vb) + w_inter * qc
    qn = jnp.sum(q * n_st, axis=1, keepdims=True)
    den = jnp.sum(s, axis=1, keepdims=True) + w_inter * qn
    e = jnp.exp(-m_row)
    nn = jnp.maximum(jnp.abs(den), e)
    b_last = b_col[lc - 1:lc, :]
    g_log = b_last - b_col + gq["li"]
    m_new = jnp.maximum(b_last + m_st, jnp.max(g_log, axis=0, keepdims=True))
    w_k = jnp.exp(g_log - m_new)
    decay = jnp.exp(b_last + m_st - m_new)
    return dict(w_intra=w_intra, w_inter=w_inter, qb=qb, kb=kb, vb=vb, cb=cb, s=s, qc=qc, num=num, qn=qn, den=den,
                e=e, nn=nn, m_new=m_new, w_k=w_k, decay=decay)


def _chunks_per_step(nc):
    return max(c for c in (3, 2, 1) if nc % c == 0)


def _mlstm_fwd(qk3, p3):
    bsz, lp, _ = p3.shape
    lc = M_CHUNK
    nc = lp // lc
    dk, dv = 128, 256
    scale = dk ** -0.5

    cps = _chunks_per_step(nc)
    rows = cps * lc

    def body(q_ref, k_ref, v_ref, g_ref, h_ref, cs_ref, ns_ref, ms_ref, c_sc, n_sc, m_sc):
        st = pl.program_id(1)

        @pl.when(st == 0)
        def _():
            c_sc[...] = jnp.zeros_like(c_sc)
            n_sc[...] = jnp.zeros_like(n_sc)
            m_sc[...] = jnp.zeros_like(m_sc)

        for j in range(cps):
            rs = slice(j * lc, (j + 1) * lc)
            g = g_ref[rs, :]
            for hh in range(M_HEADS):
                c_st, n_st, m_all = c_sc[hh], n_sc[hh], m_sc[hh]
                cs_ref[hh, j] = c_st
                ns_ref[hh, j] = n_st
                ms_ref[hh, j] = m_all
                m_st = m_all[:, 0:1]
                q = q_ref[rs, hh * dk:(hh + 1) * dk]
                ks = k_ref[rs, hh * dk:(hh + 1) * dk] * scale
                v = v_ref[rs, hh * dv:(hh + 1) * dv]
                gq = _mlstm_gates(g, hh, st * cps + j, lc)
                f = _mlstm_chunk(q, ks, v, gq, c_st, n_st, m_st, lc)
                h_ref[rs, hh * dv:(hh + 1) * dv] = f["num"] / f["nn"]
                kw = ks * f["w_k"]
                c_sc[hh] = f["decay"] * c_st + _dot(_bf(kw), f["vb"], TN)
                n_sc[hh] = f["decay"] * n_st + _colsum(kw)
                m_sc[hh] = jnp.broadcast_to(f["m_new"], (1, 128))

    sd = jax.ShapeDtypeStruct
    nh = M_HEADS
    return _pcall(
        body, name="mlstm_fwd", grid=(bsz, nc // cps),
        in_specs=[pl.BlockSpec((None, rows, nh * dk), lambda b, c: (b, c, 0)),
                  pl.BlockSpec((None, rows, nh * dk), lambda b, c: (b, c, 1)),
                  pl.BlockSpec((None, rows, nh * dv), lambda b, c: (b, c, V_OFF // (nh * dv))),
                  pl.BlockSpec((None, rows, 128), lambda b, c: (b, c, G_OFF // 128))],
        out_specs=[pl.BlockSpec((None, rows, nh * dv), lambda b, c: (b, c, 0)),
                   pl.BlockSpec((None, nh, cps, dk, dv), lambda b, c: (b, 0, c, 0, 0)),
                   pl.BlockSpec((None, nh, cps, 1, dk), lambda b, c: (b, 0, c, 0, 0)),
                   pl.BlockSpec((None, nh, cps, 1, 128), lambda b, c: (b, 0, c, 0, 0))],
        out_shape=[sd((bsz, lp, nh * dv), F32), sd((bsz, nh, nc, dk, dv), F32),
                   sd((bsz, nh, nc, 1, dk), F32), sd((bsz, nh, nc, 1, 128), F32)],
        scratch_shapes=[pltpu.VMEM((nh, dk, dv), F32), pltpu.VMEM((nh, 1, dk), F32), pltpu.VMEM((nh, 1, 128), F32)],
        compiler_params=_cp(("parallel", "arbitrary")),
    )(qk3, qk3, p3, p3)


def _mlstm_bwd(dp3, qk3, p3, dh3, cs, ns, ms):
    bsz, lp, _ = p3.shape
    lc = M_CHUNK
    nc = lp // lc
    dk, dv = 128, 256
    scale = dk ** -0.5

    cps = _chunks_per_step(nc)
    nst = nc // cps
    rows = cps * lc

    def body(dp_any, q_ref, k_ref, v_ref, g_ref, dh_ref, cs_ref, ns_ref, ms_ref,
             dv_ref, dqk_ref, dg_ref, dc_sc, dn_sc):
        t = pl.program_id(1)
        st = nst - 1 - t

        @pl.when(t == 0)
        def _():
            dc_sc[...] = jnp.zeros_like(dc_sc)
            dn_sc[...] = jnp.zeros_like(dn_sc)

        lane = lax.broadcasted_iota(jnp.int32, (lc, 128), 1)
        for j in reversed(range(cps)):
            rs = slice(j * lc, (j + 1) * lc)
            g = g_ref[rs, :]
            dgate = jnp.zeros((lc, 128), F32)
            for hh in range(M_HEADS):
                dgate = head(hh, j, rs, st * cps + j, g, lane, dgate, q_ref, k_ref, v_ref, dh_ref, cs_ref, ns_ref,
                             ms_ref, dv_ref, dqk_ref, dc_sc, dn_sc)
            dg_ref[rs, :] = dgate.astype(dg_ref.dtype)

    def head(hh, j, sl, c, g, lane, dgate, q_ref, k_ref, v_ref, dh_ref, cs_ref, ns_ref, ms_ref, dv_ref, dqk_ref,
             dc_sc, dn_sc):
        c_st, n_st = cs_ref[hh, j], ns_ref[hh, j]
        m_st = ms_ref[hh, j][:, 0:1]
        q = q_ref[sl, hh * dk:(hh + 1) * dk]
        ks = k_ref[sl, hh * dk:(hh + 1) * dk] * scale
        v = v_ref[sl, hh * dv:(hh + 1) * dv]
        dh = dh_ref[sl, hh * dv:(hh + 1) * dv]
        gq = _mlstm_gates(g, hh, c, lc)
        f = _mlstm_chunk(q, ks, v, gq, c_st, n_st, m_st, lc)
        eye, r2, c2, row, valid = gq["eye"], gq["r2"], gq["c2"], gq["row"], gq["valid"]
        w_intra, w_inter, s, nn, den = f["w_intra"], f["w_inter"], f["s"], f["nn"], f["den"]
        qb, kb, vb, cb, w_k, decay = f["qb"], f["kb"], f["vb"], f["cb"], f["w_k"], f["decay"]
        d_c, d_n = dc_sc[hh], dn_sc[hh]
        d_cb = _bf(d_c)

        hout = f["num"] / nn
        dnum = dh / nn
        d_nn = -jnp.sum(dh * hout, axis=1, keepdims=True) / nn
        dden = jnp.where(jnp.abs(den) > f["e"], d_nn * jnp.sign(den), 0.0)
        wdnum = w_inter * dnum
        wdden = w_inter * dden
        ds = _dot(_bf(dnum), vb, NT) + dden
        dsw = _bf(ds * w_intra)
        dq = _dot(dsw, kb) + _dot(_bf(wdnum), cb, NT) + wdden * n_st
        dkw = _dot(vb, d_cb, NT) + d_n
        dks = _dot(dsw, qb, TN) + dkw * w_k
        kw = ks * w_k
        dvv = _dot(_bf(s), _bf(dnum), TN) + _dot(_bf(kw), d_cb)
        dd = ds * s
        rs = jnp.sum(dd, axis=1, keepdims=True)
        cs_col = jnp.sum(jnp.where(eye, jnp.sum(dd, axis=0, keepdims=True), 0.0), axis=1, keepdims=True)
        dwi = jnp.sum(dnum * f["qc"], axis=1, keepdims=True) + dden * f["qn"]
        db = rs - cs_col + dwi * w_inter
        dli = cs_col
        ddecay = jnp.sum(jnp.sum(d_c * c_st, axis=1, keepdims=True), axis=0, keepdims=True) \
            + jnp.sum(d_n * n_st, axis=1, keepdims=True)
        dgl = jnp.sum(dkw * ks, axis=1, keepdims=True) * w_k
        dblast = ddecay * decay + jnp.sum(dgl, axis=0, keepdims=True)
        db = db - dgl + jnp.where(row == lc - 1, dblast, 0.0)
        dli = dli + dgl
        db_row = gq["to_row"](db)
        dlf = jnp.sum(jnp.where(c2 >= r2, db_row, 0.0), axis=1, keepdims=True)
        dlf = jnp.where(valid, dlf, 0.0)
        dgate = jnp.where(lane == hh, jnp.where(valid, dli, 0.0), dgate)
        dgate = jnp.where(lane == M_HEADS + hh, dlf / (1.0 + jnp.exp(gq["f_col"])), dgate)
        dqk_ref[sl, hh * dk:(hh + 1) * dk] = dq
        dqk_ref[sl, (M_HEADS + hh) * dk:(M_HEADS + hh + 1) * dk] = dks * scale
        dv_ref[sl, hh * dv:(hh + 1) * dv] = dvv.astype(dv_ref.dtype)
        dc_sc[hh] = decay * d_c + _dot(qb, _bf(wdnum), TN)
        dn_sc[hh] = decay * d_n + _colsum(q * wdden)
        return dgate

    sd = jax.ShapeDtypeStruct
    nh = M_HEADS
    rc = lambda c: nst - 1 - c
    return _pcall(
        body, name="mlstm_bwd", grid=(bsz, nst),
        in_specs=[pl.BlockSpec(memory_space=pl.ANY),
                  pl.BlockSpec((None, rows, nh * dk), lambda b, c: (b, rc(c), 0)),
                  pl.BlockSpec((None, rows, nh * dk), lambda b, c: (b, rc(c), 1)),
                  pl.BlockSpec((None, rows, nh * dv), lambda b, c: (b, rc(c), V_OFF // (nh * dv))),
                  pl.BlockSpec((None, rows, 128), lambda b, c: (b, rc(c), G_OFF // 128)),
                  pl.BlockSpec((None, rows, nh * dv), lambda b, c: (b, rc(c), 0)),
                  pl.BlockSpec((None, nh, cps, dk, dv), lambda b, c: (b, 0, rc(c), 0, 0)),
                  pl.BlockSpec((None, nh, cps, 1, dk), lambda b, c: (b, 0, rc(c), 0, 0)),
                  pl.BlockSpec((None, nh, cps, 1, 128), lambda b, c: (b, 0, rc(c), 0, 0))],
        out_specs=[pl.BlockSpec((None, rows, nh * dv), lambda b, c: (b, rc(c), V_OFF // (nh * dv))),
                   pl.BlockSpec((None, rows, 2 * nh * dk), lambda b, c: (b, rc(c), 0)),
                   pl.BlockSpec((None, rows, 128), lambda b, c: (b, rc(c), 0))],
        out_shape=[sd(dp3.shape, dp3.dtype), sd((bsz, lp, 2 * nh * dk), F32), sd((bsz, lp, 128), dp3.dtype)],
        scratch_shapes=[pltpu.VMEM((nh, dk, dv), F32), pltpu.VMEM((nh, 1, dk), F32)],
        input_output_aliases={0: 0},
        compiler_params=_cp(("arbitrary", "arbitrary")),
    )(dp3, qk3, qk3, p3, p3, dh3, cs, ns, ms)


def _headnorm(x):
    dv = x.shape[1] // M_HEADS
    xh, rs = [], []
    for h in range(M_HEADS):
        xx = x[:, h * dv:(h + 1) * dv]
        mu = jnp.mean(xx, axis=-1, keepdims=True)
        xc = xx - mu
        rstd = lax.rsqrt(jnp.mean(xc * xc, axis=-1, keepdims=True) + LN_EPS)
        xh.append(xc * rstd)
        rs.append(rstd)
    return jnp.concatenate(xh, axis=1), rs


def _mix_fwd(hm, p, ys5g, h0, gn, wmo_bf, wo_bf, g1, b1, lp):
    r, d = hm.shape
    tm = _row_tile(lp, 208)

    def body(hm_ref, o_ref, gs_ref, gm_ref, ys_ref, h0_ref, gn_ref, wmo_ref, wo_ref, g1_ref, b1_ref,
             ymin_ref, ym_ref, mix_ref, r1_ref, h1_ref):
        xhat, _ = _headnorm(hm_ref[...])
        ymin = _bf(_sig(o_ref[...]) * (xhat * gn_ref[...]))
        ymin_ref[...] = ymin
        ym = _dot(ymin, wmo_ref[...])
        ym_ref[...] = ym
        mix = _bf(_sig(gs_ref[...]) * ys_ref[...] + _sig(gm_ref[...]) * ym)
        mix_ref[...] = mix
        r1 = ALPHA * h0_ref[...] + _dot(mix, wo_ref[...])
        r1_ref[...] = r1
        h1, _, _ = _ln_fwd(r1, g1_ref[...], b1_ref[...])
        h1_ref[...] = h1

    sd = jax.ShapeDtypeStruct
    row = pl.BlockSpec((tm, d), lambda i: (i, 0))
    return _pcall(
        body, name="mix_fwd", grid=(r // tm,),
        in_specs=[row, pl.BlockSpec((tm, d), lambda i: (i, O_OFF // d)), pl.BlockSpec((tm, d), lambda i: (i, GS_OFF // d)),
                  pl.BlockSpec((tm, d), lambda i: (i, GM_OFF // d)), row, row, _const((1, d)),
                  _resident((d, d)), _resident((d, d)), _const((1, d)), _const((1, d))],
        out_specs=[row] * 5,
        out_shape=[sd((r, d), BF16), sd((r, d), F32), sd((r, d), BF16), sd((r, d), F32), sd((r, d), F32)],
        compiler_params=_cp(("parallel",), 48),
    )(hm, p, p, p, ys5g, h0, gn, wmo_bf, wo_bf, g1, b1)


def _mix_bwd(dh1, r1, g1, wo_bf, wmo_bf, p, ys5g, ym, hm, gn, lp):
    r, d = hm.shape
    tm = _row_tile(lp, 208)
    dv = d // M_HEADS

    def body(dh1_ref, r1_ref, g1_ref, wo_ref, wmo_ref, o_ref, gs_ref, gm_ref, ys_ref, ym_ref, hm_ref, gn_ref,
             dr1_ref, dp_ref, dys_ref, dym_ref, dhm_ref, dg1_ref, db1_ref, dgn_ref):
        i = pl.program_id(0)

        @pl.when(i == 0)
        def _():
            dg1_ref[...] = jnp.zeros_like(dg1_ref)
            db1_ref[...] = jnp.zeros_like(db1_ref)
            dgn_ref[...] = jnp.zeros_like(dgn_ref)

        dh1 = dh1_ref[...]
        _, xhat1, rstd1 = _ln_fwd(r1_ref[...], g1_ref[...], 0.0)
        dr1 = _ln_bwd(dh1, xhat1, rstd1, g1_ref[...])
        dr1_ref[...] = dr1
        dg1_ref[...] += _colsum(dh1 * xhat1)
        db1_ref[...] += _colsum(dh1)
        dmix = _dot(_bf(dr1), wo_ref[...], NT)
        sgs, sgm, so = _sig(gs_ref[...]), _sig(gm_ref[...]), _sig(o_ref[...])
        dys_ref[...] = dmix * sgs
        dp_ref[:, d:2 * d] = _bf(dmix * ys_ref[...] * sgs * (1.0 - sgs))
        dym = dmix * sgm
        dym_ref[...] = _bf(dym)
        dp_ref[:, 2 * d:3 * d] = _bf(dmix * ym_ref[...] * sgm * (1.0 - sgm))
        dymin = _dot(_bf(dym), wmo_ref[...], NT)
        xhat, rs = _headnorm(hm_ref[...])
        gn_ = gn_ref[...]
        dp_ref[:, 0:d] = _bf(dymin * (xhat * gn_) * so * (1.0 - so))
        dhn = dymin * so
        dgn_ref[...] += _colsum(dhn * xhat)
        dxh = dhn * gn_
        for h in range(M_HEADS):
            sl = slice(h * dv, (h + 1) * dv)
            a, xh = dxh[:, sl], xhat[:, sl]
            m1 = jnp.mean(a, axis=-1, keepdims=True)
            m2 = jnp.mean(a * xh, axis=-1, keepdims=True)
            dhm_ref[:, sl] = rs[h] * (a - m1 - xh * m2)

    sd = jax.ShapeDtypeStruct
    row = pl.BlockSpec((tm, d), lambda i: (i, 0))
    vec = _const((1, d))
    return _pcall(
        body, name="mix_bwd", grid=(r // tm,),
        in_specs=[row, row, vec, _resident((d, d)), _resident((d, d)),
                  pl.BlockSpec((tm, d), lambda i: (i, O_OFF // d)), pl.BlockSpec((tm, d), lambda i: (i, GS_OFF // d)),
                  pl.BlockSpec((tm, d), lambda i: (i, GM_OFF // d)), row, row, row, vec],
        out_specs=[row, pl.BlockSpec((tm, 3 * d), lambda i: (i, 0)), row, row, row, vec, vec, vec],
        out_shape=[sd((r, d), F32), sd((r, NP), BF16), sd((r, d), F32), sd((r, d), BF16), sd((r, d), F32),
                   sd((1, d), F32), sd((1, d), F32), sd((1, d), F32)],
        compiler_params=_cp(("arbitrary",), 48),
    )(dh1, r1, g1, wo_bf, wmo_bf, p, p, p, ys5g, ym, hm, gn)


def _mlp_fwd(h1, tgt, wup_g, wdn_bf, bup, g2, b2, lp):
    r, d = h1.shape
    tm = _row_tile(lp, 352)
    tps = lp // tm
    nf = wup_g.shape[0]

    def body(h1_ref, t_ref, wup_ref, wdn_ref, bup_ref, g2_ref, b2_ref, dr2_ref, act_ref, loss_ref, dg2_ref, db2_ref):
        i = pl.program_id(0)

        @pl.when(i == 0)
        def _():
            loss_ref[...] = jnp.zeros_like(loss_ref)
            dg2_ref[...] = jnp.zeros_like(dg2_ref)
            db2_ref[...] = jnp.zeros_like(db2_ref)

        h1 = h1_ref[...]
        h1b = _bf(h1)
        ff = jnp.zeros((tm, d), F32)
        for s in range(nf):
            up = _dot(h1b, wup_ref[s]) + bup_ref[:, s * d:(s + 1) * d]
            a = jnp.maximum(up, 0.0)
            a = _bf(a * a)
            act_ref[:, s * d:(s + 1) * d] = a
            ff = ff + _dot(a, wdn_ref[s * d:(s + 1) * d, :])
        r2 = ALPHA * h1 + ff
        g2 = g2_ref[...]
        y, xhat, rstd = _ln_fwd(r2, g2, b2_ref[...])
        t = (i % tps) * tm + lax.broadcasted_iota(jnp.int32, (tm, 1), 0)
        diff = jnp.where(t >= PAD + N_META, y - t_ref[...], 0.0)
        loss_ref[...] += 0.5 / d * jnp.sum(jnp.sum(diff * diff, axis=1, keepdims=True), axis=0, keepdims=True)
        dy = diff * (1.0 / d)
        dg2_ref[...] += _colsum(dy * xhat)
        db2_ref[...] += _colsum(dy)
        dr2_ref[...] = _ln_bwd(dy, xhat, rstd, g2)

    sd = jax.ShapeDtypeStruct
    row = pl.BlockSpec((tm, d), lambda i: (i, 0))
    vec = _const((1, d))
    return _pcall(
        body, name="mlp_fwd", grid=(r // tm,),
        in_specs=[row, row, _resident(wup_g.shape), _resident(wdn_bf.shape), _const((1, nf * d)), vec, vec],
        out_specs=[row, pl.BlockSpec((tm, nf * d), lambda i: (i, 0)), _const((1, 128)), vec, vec],
        out_shape=[sd((r, d), F32), sd((r, nf * d), BF16), sd((1, 128), F32), sd((1, d), F32), sd((1, d), F32)],
        compiler_params=_cp(("arbitrary",), 56),
    )(h1, tgt, wup_g, wdn_bf, bup, g2, b2)


def _mlp_bwd(h1, dr2, wup_g, wdn_bf, bup, lp):
    r, d = h1.shape
    tm = _row_tile(lp, 352)
    nf = wup_g.shape[0]

    def body(h1_ref, dr2_ref, wup_ref, wdn_ref, bup_ref, dh1_ref, dup_ref, dbup_ref):
        i = pl.program_id(0)

        @pl.when(i == 0)
        def _():
            dbup_ref[...] = jnp.zeros_like(dbup_ref)

        h1b = _bf(h1_ref[...])
        dr2 = dr2_ref[...]
        dr2b = _bf(dr2)
        acc = ALPHA * dr2
        for s in range(nf):
            up = _dot(h1b, wup_ref[s]) + bup_ref[:, s * d:(s + 1) * d]
            dact = _dot(dr2b, wdn_ref[s * d:(s + 1) * d, :], NT)
            dup = dact * (2.0 * jnp.maximum(up, 0.0))
            dbup_ref[:, s * d:(s + 1) * d] += _colsum(dup)
            dupb = _bf(dup)
            dup_ref[:, s * d:(s + 1) * d] = dupb
            acc = acc + _dot(dupb, wup_ref[s], NT)
        dh1_ref[...] = acc

    sd = jax.ShapeDtypeStruct
    row = pl.BlockSpec((tm, d), lambda i: (i, 0))
    return _pcall(
        body, name="mlp_bwd", grid=(r // tm,),
        in_specs=[row, row, _resident(wup_g.shape), _resident(wdn_bf.shape), _const((1, nf * d))],
        out_specs=[row, pl.BlockSpec((tm, nf * d), lambda i: (i, 0)), _const((1, nf * d))],
        out_shape=[sd((r, d), F32), sd((r, nf * d), BF16), sd((1, nf * d), F32)],
        compiler_params=_cp(("arbitrary",), 56),
    )(h1, dr2, wup_g, wdn_bf, bup)


def _s5_block_mats(bb_re_t, bb_im_t, c_re, c_im, ap_re, ap_im):
    ng = c_re.shape[0]
    gl = ng // S5_KCH
    eye = jnp.eye(gl, dtype=F32)

    def bmat(bt):
        bb = jnp.transpose(bt, (1, 0, 2)).reshape(S5_KCH, gl, S5_GROUP, S5_STATE)
        return jnp.einsum("kghp,gj->kghjp", bb, eye).reshape(S5_KCH, gl * S5_GROUP, gl * S5_STATE)

    def cmat(c):
        cc = c.reshape(S5_KCH, gl, S5_GROUP, S5_STATE)
        return jnp.einsum("kghp,gj->kjpgh", cc, eye).reshape(S5_KCH, gl * S5_STATE, gl * S5_GROUP)

    def pw(a):
        return jnp.transpose(a.reshape(8, S5_KCH, gl * S5_STATE), (1, 0, 2))

    bk = jnp.concatenate([bmat(bb_re_t), bmat(bb_im_t)], axis=-1)
    apow = jnp.concatenate([pw(ap_re), pw(ap_im)], axis=-1)
    return _bf(bk), _bf(cmat(c_re)), _bf(cmat(c_im)), apow


def _s5_block_grads(dbk, dcre, dcim, da):
    gl = dbk.shape[1] // S5_GROUP
    ng = gl * S5_KCH
    eye = jnp.eye(gl, dtype=F32)
    hw = gl * S5_STATE

    def bpart(x):
        x = x.reshape(S5_KCH, gl, S5_GROUP, gl, S5_STATE)
        x = jnp.einsum("kghjp,gj->kghp", x, eye).reshape(ng, S5_GROUP, S5_STATE)
        return jnp.transpose(x, (1, 0, 2))

    def cpart(x):
        x = x.reshape(S5_KCH, gl, S5_STATE, gl, S5_GROUP)
        return jnp.einsum("kjpgh,gj->kghp", x, eye).reshape(ng, S5_GROUP, S5_STATE)

    return (bpart(dbk[..., :hw]), bpart(dbk[..., hw:]), cpart(dcre), cpart(dcim),
            da[:, 0, :hw].reshape(ng, S5_STATE), da[:, 0, hw:].reshape(ng, S5_STATE))


def _tie(a, tok):
    return a if tok is None else a + tok[0, 0]


def _local_step(x, tgt, w, early=None, late=None, ready=None):
    ready = ready or (lambda names, g: None)
    bsz, seq, d = x.shape
    lp = PAD + N_META + seq
    r = bsz * lp
    meta = jnp.broadcast_to(w["meta_tokens"][None], (bsz, N_META, d))
    hin = jnp.concatenate([jnp.zeros((bsz, PAD, d), F32), meta, x], axis=1).reshape(r, d)
    tgtp = jnp.concatenate([jnp.zeros((bsz, PAD + N_META, d), F32), tgt], axis=1).reshape(r, d)

    h0, h0b = _ln0_fwd(hin, w["ln0_g"], w["ln0_b"], lp)
    if early is not None:
        w = {**w, **early((h0, tgtp))}
    p = _inproj(h0b, w["w_in"], w["b_in"], lp)
    p3 = p.reshape(bsz, lp, NP)

    b_re_t = jnp.transpose(w["s5_b_re"], (2, 0, 1))
    b_im_t = jnp.transpose(w["s5_b_im"], (2, 0, 1))
    ap_re, ap_im, bb_re_t, bb_im_t = _s5_prep(w["s5_lambda_re"], w["s5_lambda_im"], w["s5_log_dt"], b_re_t, b_im_t)
    bk, cre, cim, apow = _s5_block_mats(bb_re_t, bb_im_t, w["s5_c_re"], w["s5_c_im"], ap_re, ap_im)
    y_s5, xs = _s5_fwd(p3, bk, cre, cim, apow, w["s5_d"])
    sw = y_s5.shape[-1]
    if late is not None:
        w = {**w, **late(y_s5)}
    gy, z, ys5g = _glu_fwd(y_s5.reshape(r, sw), w["s5_w_glu"], lp)

    pre3, qk3 = _conv_fwd(p3, w["qk_conv_w"], w["qk_conv_b"])
    hm3, cs, ns, ms = _mlstm_fwd(qk3, p3)
    hm = hm3.reshape(r, d)
    ymin, ym, mix, r1, h1 = _mix_fwd(hm, p, ys5g, h0, w["m_norm_g"], w["m_w_out"], w["w_o"], w["ln1_g"], w["ln1_b"], lp)
    dr2, act, loss, dg2, db2 = _mlp_fwd(h1, tgtp, w["w_up"], w["w_down"], w["b_up"], w["ln2_g"], w["ln2_b"], lp)

    g = {"ln2_g": dg2, "ln2_b": db2}
    dh1, dup, g["b_up"] = _mlp_bwd(h1, dr2, w["w_up"], w["w_down"], w["b_up"], lp)
    g["w_down"] = _mm_tn(act, dr2, name="dw_down")
    g["w_up"] = _mm_tn(h1, dup, name="dw_up", split=w["w_up"].shape[0])
    tok = ready(("w_down", "w_up"), g)
    dr1, dp, dys5g, dym, dhm, g["ln1_g"], g["ln1_b"], g["m_norm_g"] = _mix_bwd(
        dh1, r1, _tie(w["ln1_g"], tok), w["w_o"], w["m_w_out"], p, ys5g, ym, hm, w["m_norm_g"], lp)
    g["w_o"] = _mm_tn(mix, dr1, name="dw_o")
    g["m_w_out"] = _mm_tn(ymin, dym, name="dw_mout")

    dp3 = dp.reshape(bsz, lp, NP)
    dp3, dqk3, dgate = _mlstm_bwd(dp3, qk3, p3, dhm.reshape(bsz, lp, d), cs, ns, ms)
    dp3, g["qk_conv_w"], g["qk_conv_b"] = _conv_bwd(dp3, p3, dqk3, pre3, w["qk_conv_w"])
    dz, dys5 = _glu_bwd(dys5g, z, y_s5.reshape(r, sw), w["s5_w_glu"], lp)
    g["s5_w_glu"] = _mm_tn(gy, dz, name="dw_glu", split=w["s5_w_glu"].shape[0])
    tok = ready(("s5_w_glu", "m_w_out", "w_o"), g)
    apow_rev = jnp.flip(apow, axis=1)
    dp3, dbk, dcre, dcim, da, g["s5_d"] = _s5_bwd(dp3, p3, dys5.reshape(bsz, lp, sw), xs, bk, cre, cim, apow_rev,
                                                 _tie(w["s5_d"], tok))
    dbb_re_t, dbb_im_t, g["s5_c_re"], g["s5_c_im"], da_re, da_im = _s5_block_grads(dbk, dcre, dcim, da)
    g["s5_lambda_re"], g["s5_lambda_im"], g["s5_log_dt"], gb_re_t, gb_im_t = _s5_prep_bwd(
        w["s5_lambda_re"], w["s5_lambda_im"], w["s5_log_dt"], b_re_t, b_im_t, da_re, da_im, dbb_re_t, dbb_im_t)
    g["s5_b_re"] = jnp.transpose(gb_re_t, (1, 2, 0))
    g["s5_b_im"] = jnp.transpose(gb_im_t, (1, 2, 0))

    dp3 = lax.dynamic_update_slice(dp3, dgate, (0, 0, G_OFF))
    dp = dp3.reshape(r, NP)
    g["w_in"], g["b_in"] = _mm_tn(h0b, dp, name="dw_in", colsum=True)
    tok = ready(("w_in",), g)
    dpw = _mm_nt(dp, w["w_in"], lp, name="dh0", dep=tok)
    dhin, g["ln0_g"], g["ln0_b"], g["meta_tokens"] = _ln0_bwd(hin, dr1, dpw, w["ln0_g"], lp)
    grad_x = dhin.reshape(bsz, lp, d)[:, PAD + N_META:]
    return loss, grad_x, g


_ANY = pl.BlockSpec(memory_space=pl.ANY)
_MESH = pl.DeviceIdType.MESH


def _place():
    return lax.axis_index("x"), lax.axis_index("y"), lax.axis_index("c")


def _gather_chips(shards):
    n = len(shards)

    def body(*refs):
        ins, outs = refs[:n], refs[n:2 * n]
        send, recv, loc = refs[2 * n:]
        x, y, c = _place()
        me = 2 * x + y
        peers = [(1 - x, y), (x, 1 - y), (1 - x, 1 - y)]

        def rc(a, k, slot):
            px, py = peers[k]
            return pltpu.make_async_remote_copy(src_ref=ins[a], dst_ref=outs[a].at[slot], send_sem=send.at[a, k],
                                                recv_sem=recv.at[a, k], device_id=(px, py, c), device_id_type=_MESH)

        own = [pltpu.make_async_copy(ins[a], outs[a].at[me], loc.at[a]) for a in range(n)]
        for cp in own:
            cp.start()
        out = [rc(a, k, me) for a in range(n) for k in range(3)]
        for cp in out:
            cp.start()
        for a in range(n):
            for k in range(3):
                rc(a, k, 2 * peers[k][0] + peers[k][1]).wait_recv()
        for cp in out:
            cp.wait_send()
        for cp in own:
            cp.wait()

    return _pcall(
        body, name="gather_chips", in_specs=[_ANY] * n, out_specs=[_ANY] * n,
        out_shape=[jax.ShapeDtypeStruct((4,) + s.shape, s.dtype) for s in shards],
        scratch_shapes=[pltpu.SemaphoreType.DMA((n, 3)), pltpu.SemaphoreType.DMA((n, 3)), pltpu.SemaphoreType.DMA((n,))],
    )(*shards)


_HBM = pl.BlockSpec(memory_space=pltpu.HBM)
_SEM = pl.BlockSpec(memory_space=pltpu.SEMAPHORE)
_EFFECT = pltpu.SideEffectType.DATAFLOW_SIDE_EFFECTING


def _xchg_copies(srcs, lands, send, recv, scatter):
    x, y, c = _place()
    me = 2 * x + y
    peers = [(1 - x, y), (x, 1 - y), (1 - x, 1 - y)]
    out = []
    for a in range(len(srcs)):
        for k, (px, py) in enumerate(peers):
            src = srcs[a].at[2 * px + py] if scatter else srcs[a]
            dst = lands[a].at[k] if scatter else lands[a].at[me]
            out.append(pltpu.make_async_remote_copy(src_ref=src, dst_ref=dst, send_sem=send.at[3 * a + k],
                                                    recv_sem=recv.at[3 * a + k], device_id=(px, py, c),
                                                    device_id_type=_MESH))
    return out


def _xchg_start(srcs, lands, *, name, scatter, dep=None):
    n = len(srcs)
    deps = [] if dep is None else [dep]
    nd = len(deps)

    def body(*refs):
        send, recv = refs[2 * n + nd], refs[2 * n + nd + 1]
        for cp in _xchg_copies(refs[:n], refs[n:2 * n], send, recv, scatter):
            cp.start()
        refs[-1][...] = jnp.zeros_like(refs[-1])

    hbm = lambda a: pltpu.HBM(a.shape, a.dtype)
    con = lambda a: pltpu.with_memory_space_constraint(a, pltpu.HBM)
    res = _pcall(
        body, name=name, in_specs=[_HBM] * (2 * n) + [_ANY] * nd,
        out_specs=[_SEM, _SEM] + [_HBM] * (2 * n) + [pl.BlockSpec(memory_space=pltpu.VMEM)],
        out_shape=[pltpu.SemaphoreType.DMA((3 * n,)), pltpu.SemaphoreType.DMA((3 * n,))]
        + [hbm(a) for a in srcs] + [hbm(a) for a in lands] + [jax.ShapeDtypeStruct((8, 128), F32)],
        input_output_aliases={i: 2 + i for i in range(2 * n)},
        compiler_params=pltpu.CompilerParams(has_side_effects=_EFFECT),
    )(*[con(a) for a in srcs], *[con(a) for a in lands], *deps)
    return res[0], res[1], list(res[2:2 + n]), list(res[2 + n:2 + 2 * n]), res[-1]


def _xchg_wait(send, recv, srcs, lands, after, *, name, scatter):
    n = len(srcs)
    afters = list(after) if isinstance(after, (list, tuple)) else [after]

    def body(*refs):
        s_ref, r_ref = refs[2 * n], refs[2 * n + 1]
        for cp in _xchg_copies(refs[:n], refs[n:2 * n], s_ref, r_ref, scatter):
            cp.wait_send()
            cp.wait_recv()

    hbm = lambda a: pltpu.HBM(a.shape, a.dtype)
    res = _pcall(
        body, name=name, in_specs=[_HBM] * (2 * n) + [_SEM, _SEM] + [_ANY] * len(afters),
        out_specs=[_HBM] * (2 * n),
        out_shape=[hbm(a) for a in srcs] + [hbm(a) for a in lands],
        input_output_aliases={i: i for i in range(2 * n)},
        compiler_params=pltpu.CompilerParams(has_side_effects=_EFFECT),
    )(*srcs, *lands, send, recv, *afters)
    return list(res[:n]), list(res[n:])


def _swap_cores(arrs, name="swap_cores"):
    n = len(arrs)

    def body(*refs):
        ins, outs = refs[:n], refs[n:2 * n]
        send, recv = refs[2 * n:]
        x, y, c = _place()
        cps = [pltpu.make_async_remote_copy(src_ref=ins[a], dst_ref=outs[a], send_sem=send.at[a], recv_sem=recv.at[a],
                                            device_id=(x, y, 1 - c), device_id_type=_MESH) for a in range(n)]
        for cp in cps:
            cp.start()
        for cp in cps:
            cp.wait_recv()
        for cp in cps:
            cp.wait_send()

    return _pcall(
        body, name=name, in_specs=[_ANY] * n, out_specs=[_ANY] * n,
        out_shape=[jax.ShapeDtypeStruct(s.shape, s.dtype) for s in arrs],
        scratch_shapes=[pltpu.SemaphoreType.DMA((n,)), pltpu.SemaphoreType.DMA((n,))],
    )(*arrs)


def _allreduce_small(v):
    rows = v.shape[0]
    half = rows // 2
    assert half % 8 == 0 and 2 * half == rows

    def body(v_ref, out_ref, sib_ref, pair_ref, slots_ref, send, recv):
        x, y, c = _place()
        chip = 2 * x + y
        sibling = (x, y, 1 - c)
        peers = [(1 - x, y), (x, 1 - y), (1 - x, 1 - y)]
        mine = pl.ds(pl.multiple_of(c * half, 8), half)

        first = pltpu.make_async_remote_copy(src_ref=v_ref, dst_ref=sib_ref, send_sem=send.at[0], recv_sem=recv.at[0],
                                             device_id=sibling, device_id_type=_MESH)
        first.start()
        first.wait_recv()
        pair_ref[...] = v_ref[...] + sib_ref[...]
        slots_ref[chip] = pair_ref[mine, :]
        cross = [pltpu.make_async_remote_copy(src_ref=pair_ref.at[mine], dst_ref=slots_ref.at[chip],
                                              send_sem=send.at[1 + k], recv_sem=recv.at[1 + k],
                                              device_id=(px, py, c), device_id_type=_MESH)
                 for k, (px, py) in enumerate(peers)]
        for cp in cross:
            cp.start()
        for cp in cross:
            cp.wait_recv()
        out_ref[mine, :] = ((slots_ref[0] + slots_ref[1]) + slots_ref[2]) + slots_ref[3]
        last = pltpu.make_async_remote_copy(src_ref=out_ref.at[mine], dst_ref=out_ref.at[mine], send_sem=send.at[4],
                                            recv_sem=recv.at[4], device_id=sibling, device_id_type=_MESH)
        last.start()
        last.wait_recv()
        first.wait_send()
        for cp in cross:
            cp.wait_send()
        last.wait_send()

    vm = pl.BlockSpec(memory_space=pltpu.VMEM)
    return _pcall(
        body, name="allreduce_small", in_specs=[vm], out_specs=vm,
        out_shape=jax.ShapeDtypeStruct((rows, 128), F32),
        scratch_shapes=[pltpu.VMEM((rows, 128), F32), pltpu.VMEM((rows, 128), F32), pltpu.VMEM((4, half, 128), F32),
                        pltpu.SemaphoreType.DMA((5,)), pltpu.SemaphoreType.DMA((5,))],
        compiler_params=_cp(None, 40),
    )(v)


def _sum_slots(own, land):
    ns, rows, cols = land.shape
    tm = _row_tile(rows, 256, 8)

    def body(own_ref, a_ref, o_ref):
        o_ref[...] = ((own_ref[...] + a_ref[0]) + a_ref[1]) + a_ref[2]

    return _pcall(
        body, name="sum_slots", grid=(rows // tm,),
        in_specs=[pl.BlockSpec((tm, cols), lambda i: (i, 0)), pl.BlockSpec((ns, tm, cols), lambda i: (0, i, 0))],
        out_specs=pl.BlockSpec((tm, cols), lambda i: (i, 0)),
        out_shape=jax.ShapeDtypeStruct((rows, cols), F32),
        compiler_params=_cp(("parallel",), 40),
    )(own, land)


def _adamw(w, m, v, g0, g1=None):
    rows, cols = w.shape[-2:]
    lead = w.ndim == 3
    tm = _row_tile(rows, max(8, (1 << 20) // (4 * cols)), 8)
    c1 = 1.0 - ADAM_B1 ** ADAM_STEP
    c2 = 1.0 - ADAM_B2 ** ADAM_STEP
    two = g1 is not None

    def body(*refs):
        w_ref, m_ref, v_ref, g0_ref = refs[:4]
        g_ref, d_ref, nm_ref, nv_ref = refs[-4:]
        g = g0_ref[...]
        if two:
            g = g + refs[4][...]
        nm = ADAM_B1 * m_ref[...] + (1.0 - ADAM_B1) * g
        nv = ADAM_B2 * v_ref[...] + (1.0 - ADAM_B2) * (g * g)
        g_ref[...] = g
        nm_ref[...] = nm
        nv_ref[...] = nv
        d_ref[...] = -ADAM_LR * ((nm / c1) / (jnp.sqrt(nv / c2) + ADAM_EPS) + ADAM_WD * w_ref[...])

    blk = pl.BlockSpec((tm, cols), lambda i: (i, 0))
    wblk = pl.BlockSpec((None, tm, cols), lambda i: (0, i, 0)) if lead else blk
    ins = [w, m, v, g0] + ([g1] if two else [])
    return _pcall(
        body, name="adamw", grid=(rows // tm,), in_specs=[wblk] * 3 + [blk] * (len(ins) - 3), out_specs=[wblk] * 4,
        out_shape=[jax.ShapeDtypeStruct(w.shape, F32)] * 4,
        compiler_params=_cp(("parallel",), 40),
    )(*ins)


_BIG = ("w_in", "s5_w_glu", "m_w_out", "w_o", "w_up", "w_down")
_SMALL = ("ln0_g", "ln0_b", "b_in", "qk_conv_b", "s5_lambda_re", "s5_lambda_im", "s5_log_dt", "s5_b_re", "s5_b_im",
          "s5_c_re", "s5_c_im", "s5_d", "m_norm_g", "ln1_g", "ln1_b", "b_up", "ln2_g", "ln2_b")
_SMALL_SHARDED = ("meta_tokens", "qk_conv_w")
_ORDER = ("meta_tokens", "ln0_g", "ln0_b", "w_in", "b_in", "qk_conv_w", "qk_conv_b", "s5_lambda_re", "s5_lambda_im",
          "s5_log_dt", "s5_b_re", "s5_b_im", "s5_c_re", "s5_c_im", "s5_d", "s5_w_glu", "m_norm_g", "m_w_out", "w_o",
          "ln1_g", "ln1_b", "w_up", "b_up", "w_down", "ln2_g", "ln2_b")


def _pack(arrs):
    flat = jnp.concatenate([a.reshape(-1) for a in arrs])
    n = flat.shape[0]
    rows = -(-n // 2048) * 16
    return jnp.pad(flat, (0, rows * 128 - n)).reshape(rows, 128)


def _unpack(packed, shapes):
    flat = packed.reshape(-1)
    out, off = [], 0
    for s in shapes:
        n = math.prod(s)
        out.append(flat[off:off + n].reshape(s))
        off += n
    return out


def kernel(x, meta_tokens, ln0_g, ln0_b, w_in, b_in, qk_conv_w, qk_conv_b, s5_lambda_re, s5_lambda_im, s5_log_dt, s5_b_re, s5_b_im, s5_c_re, s5_c_im, s5_d, s5_w_glu, m_norm_g, m_w_out, w_o, ln1_g, ln1_b, w_up, b_up, w_down, ln2_g, ln2_b, loss_target, m_meta_tokens, m_ln0_g, m_ln0_b, m_w_in, m_b_in, m_qk_conv_w, m_qk_conv_b, m_s5_lambda_re, m_s5_lambda_im, m_s5_log_dt, m_s5_b_re, m_s5_b_im, m_s5_c_re, m_s5_c_im, m_s5_d, m_s5_w_glu, m_m_norm_g, m_m_w_out, m_w_o, m_ln1_g, m_ln1_b, m_w_up, m_b_up, m_w_down, m_ln2_g, m_ln2_b, v_meta_tokens, v_ln0_g, v_ln0_b, v_w_in, v_b_in, v_qk_conv_w, v_qk_conv_b, v_s5_lambda_re, v_s5_lambda_im, v_s5_log_dt, v_s5_b_re, v_s5_b_im, v_s5_c_re, v_s5_c_im, v_s5_d, v_s5_w_glu, v_m_norm_g, v_m_w_out, v_w_o, v_ln1_g, v_ln1_b, v_w_up, v_b_up, v_w_down, v_ln2_g, v_ln2_b):
    wts = dict(meta_tokens=meta_tokens, ln0_g=ln0_g, ln0_b=ln0_b, w_in=w_in, b_in=b_in, qk_conv_w=qk_conv_w,
               qk_conv_b=qk_conv_b, s5_lambda_re=s5_lambda_re, s5_lambda_im=s5_lambda_im, s5_log_dt=s5_log_dt,
               s5_b_re=s5_b_re, s5_b_im=s5_b_im, s5_c_re=s5_c_re, s5_c_im=s5_c_im, s5_d=s5_d, s5_w_glu=s5_w_glu,
               m_norm_g=m_norm_g, m_w_out=m_w_out, w_o=w_o, ln1_g=ln1_g, ln1_b=ln1_b, w_up=w_up, b_up=b_up,
               w_down=w_down, ln2_g=ln2_g, ln2_b=ln2_b)
    mom = dict(meta_tokens=m_meta_tokens, ln0_g=m_ln0_g, ln0_b=m_ln0_b, w_in=m_w_in, b_in=m_b_in, qk_conv_w=m_qk_conv_w,
               qk_conv_b=m_qk_conv_b, s5_lambda_re=m_s5_lambda_re, s5_lambda_im=m_s5_lambda_im, s5_log_dt=m_s5_log_dt,
               s5_b_re=m_s5_b_re, s5_b_im=m_s5_b_im, s5_c_re=m_s5_c_re, s5_c_im=m_s5_c_im, s5_d=m_s5_d,
               s5_w_glu=m_s5_w_glu, m_norm_g=m_m_norm_g, m_w_out=m_m_w_out, w_o=m_w_o, ln1_g=m_ln1_g, ln1_b=m_ln1_b,
               w_up=m_w_up, b_up=m_b_up, w_down=m_w_down, ln2_g=m_ln2_g, ln2_b=m_ln2_b)
    var = dict(meta_tokens=v_meta_tokens, ln0_g=v_ln0_g, ln0_b=v_ln0_b, w_in=v_w_in, b_in=v_b_in, qk_conv_w=v_qk_conv_w,
               qk_conv_b=v_qk_conv_b, s5_lambda_re=v_s5_lambda_re, s5_lambda_im=v_s5_lambda_im, s5_log_dt=v_s5_log_dt,
               s5_b_re=v_s5_b_re, s5_b_im=v_s5_b_im, s5_c_re=v_s5_c_re, s5_c_im=v_s5_c_im, s5_d=v_s5_d,
               s5_w_glu=v_s5_w_glu, m_norm_g=v_m_norm_g, m_w_out=v_m_w_out, w_o=v_w_o, ln1_g=v_ln1_g, ln1_b=v_ln1_b,
               w_up=v_w_up, b_up=v_b_up, w_down=v_w_down, ln2_g=v_ln2_g, ln2_b=v_ln2_b)
    d = x.shape[-1]
    chip = 2 * lax.axis_index("x") + lax.axis_index("y")

    gw = dict(zip(_SMALL_SHARDED, _gather_chips([meta_tokens, qk_conv_w[0]])))
    own_w_in = _bf(w_in[0])
    fsend, frecv, fsrc, fland, ftok = _xchg_start([own_w_in], [lax.empty((4,) + own_w_in.shape, BF16)],
                                                  name="gather_w_in_start", scatter=False, dep=gw["qk_conv_w"])
    late_names = tuple(n for n in _BIG if n != "w_in")
    cat = lambda a: jnp.transpose(a, (1, 0, 2)).reshape(a.shape[1], 4 * a.shape[2])
    w = dict(
        meta_tokens=cat(gw["meta_tokens"]), ln0_g=ln0_g[None], ln0_b=_tie(ln0_b[None], ftok),
        qk_conv_w=cat(gw["qk_conv_w"]), qk_conv_b=qk_conv_b,
        s5_lambda_re=s5_lambda_re[0], s5_lambda_im=s5_lambda_im[0], s5_log_dt=s5_log_dt[0][:, None],
        s5_b_re=s5_b_re[0], s5_b_im=s5_b_im[0], s5_c_re=s5_c_re[0], s5_c_im=s5_c_im[0], s5_d=s5_d,
        m_norm_g=m_norm_g, ln1_g=ln1_g, ln1_b=ln1_b, b_up=b_up, ln2_g=ln2_g, ln2_b=ln2_b)
    in_flight = {}

    def place_own(src, land):
        return lax.dynamic_update_slice(land, src[None], (chip,) + (0,) * src.ndim)

    def early(after):
        src, land = _xchg_wait(fsend, frecv, fsrc, fland, after, name="gather_w_in_wait", scatter=False)
        late_src = [_bf(wts[n][0]) for n in late_names]
        st = _xchg_start(late_src, [lax.empty((4,) + a.shape, a.dtype) for a in late_src], name="gather_late_start",
                         scatter=False, dep=src[0])
        in_flight["late"] = st[:4]
        return dict(w_in=_w_in_from_slots(place_own(src[0], land[0]), IN_CHUNK), b_in=_tie(_to_pad_cols(b_in), st[4]))

    def late(after):
        src, land = _xchg_wait(*in_flight["late"], after, name="gather_late_wait", scatter=False)
        full = {n: place_own(s, ld) for n, s, ld in zip(late_names, src, land)}
        return dict(s5_w_glu=full["s5_w_glu"], m_w_out=full["m_w_out"].reshape(d, d), w_o=full["w_o"].reshape(d, d),
                    w_up=full["w_up"], w_down=full["w_down"].reshape(4 * d, d))

    flying = []

    def ready(names, g):
        parts = dict(
            w_in=lambda: _slots_from_w_in(g["w_in"][0]), s5_w_glu=lambda: g["s5_w_glu"],
            m_w_out=lambda: g["m_w_out"].reshape(4, d // 4, d), w_o=lambda: g["w_o"].reshape(4, d // 4, d),
            w_up=lambda: g["w_up"], w_down=lambda: g["w_down"].reshape(4, d, d))
        src = [parts[n]() for n in names]
        land = [lax.empty((3,) + a.shape[1:], a.dtype) for a in src]
        st = _xchg_start(src, land, name="scatter_" + names[0] + "_start", scatter=True)
        flying.append((names,) + st[:4])
        return st[4]

    loss, grad_x, g = _local_step(x, loss_target, w, early, late, ready)
    g["b_in"] = _from_pad_cols(g["b_in"])

    res = {}

    def flat(a):
        return jnp.swapaxes(a, -1, -2).reshape(a.shape[:-2] + (-1, 128))

    def unflat(y, shape):
        return jnp.swapaxes(y.reshape(shape[:-2] + (shape[-1], shape[-2])), -1, -2)

    def finish(groups, after, tag):
        mine = {}
        for names, send, recv, src, land in groups:
            src, land = _xchg_wait(send, recv, src, land, after, name="scatter_" + names[0] + "_wait", scatter=True)
            for n, s, ld in zip(names, src, land):
                mine[n] = _sum_slots(lax.dynamic_index_in_dim(s, chip, 0, keepdims=False), ld)
        theirs = _swap_cores(list(mine.values()), name="swap_cores_" + tag)
        for n, t in zip(mine, theirs):
            if n == "w_in":
                res[n] = [unflat(r, wts[n].shape) for r in _adamw(flat(wts[n]), flat(mom[n]), flat(var[n]),
                                                                  flat(mine[n]), flat(t))]
            else:
                res[n] = _adamw(wts[n], mom[n], var[n], mine[n], t)

    finish(flying[:-1], g["ln0_g"], "a")

    small_shapes = [(1, 128)] + [wts[n].shape for n in _SMALL] + [g[n].shape for n in _SMALL_SHARDED]
    packed = _pack([loss] + [g[n] for n in _SMALL] + [g[n] for n in _SMALL_SHARDED])
    tot = _unpack(_allreduce_small(packed), small_shapes)
    loss_out = tot[0][0, 0]
    gsm = dict(zip(_SMALL + _SMALL_SHARDED, tot[1:]))
    for n in _SMALL_SHARDED:
        cols = wts[n].shape[-1]
        gsm[n] = lax.dynamic_slice_in_dim(gsm[n], chip * cols, cols, axis=1).reshape(wts[n].shape)

    names = _SMALL + _SMALL_SHARDED
    shapes = [wts[n].shape for n in names]
    pk = lambda dct: _pack([dct[n] for n in names])
    small_out = _adamw(pk(wts), pk(mom), pk(var), pk(gsm))
    small_res = [_unpack(r, shapes) for r in small_out]
    for j, n in enumerate(names):
        res[n] = [small_res[q][j] for q in range(4)]
    finish(flying[-1:], small_out[0], "b")

    return (loss_out, grad_x, *[res[n][0] for n in _ORDER], *[res[n][1] for n in _ORDER],
            *[res[n][2] for n in _ORDER], *[res[n][3] for n in _ORDER])
```

```python
import functools
import math

import jax
import jax.numpy as jnp
from jax import lax
from jax.experimental import pallas as pl
from jax.experimental.pallas import tpu as pltpu

F32 = jnp.float32
BF16 = jnp.bfloat16
HI = lax.Precision.HIGHEST

N_META = 16
M_HEADS = 4
M_CHUNK = 128
PAD = M_CHUNK - N_META
CONV_W = 4
S5_GROUP = 16
S5_STATE = 64
S5_KCH = 4
LN_EPS = 1e-5
ALPHA = 2.0 ** 0.25
NEG = -1e30
ADAM_LR, ADAM_B1, ADAM_B2, ADAM_EPS, ADAM_WD, ADAM_STEP = 0.001, 0.9, 0.999, 1e-08, 0.01, 10

O_OFF, GS_OFF, GM_OFF, V_OFF, Q_OFF, K_OFF, U_OFF, G_OFF, NP = 0, 1024, 2048, 3072, 4096, 4608, 5120, 5632, 5760

NN = ((1,), (0,))
NT = ((1,), (1,))
TN = ((0,), (0,))


def _dot(a, b, dims=NN, prec=None):
    return lax.dot_general(a, b, (dims, ((), ())), preferred_element_type=F32, precision=prec)


def _bf(x):
    return x.astype(BF16)


def _sig(x):
    return 0.5 * jnp.tanh(0.5 * x) + 0.5


def _pcall(body, **kw):
    return pl.pallas_call(body, **kw)


def _cp(sem=None, vmem_mb=None):
    kw = {}
    if sem is not None:
        kw["dimension_semantics"] = sem
    if vmem_mb is not None:
        kw["vmem_limit_bytes"] = vmem_mb << 20
    return pltpu.CompilerParams(**kw)


def _row_tile(n, want, mult=16):
    best = None
    for t in range(mult, want + 1, mult):
        if n % t == 0:
            best = t
    assert best is not None, (n, want)
    return best


def _resident(shape):
    nd = len(shape)
    return pl.BlockSpec(shape, lambda *_: (0,) * nd, pipeline_mode=pl.Buffered(1))


def _const(shape):
    nd = len(shape)
    return pl.BlockSpec(shape, lambda *_: (0,) * nd)


def _ln_fwd(x, g, b):
    mu = jnp.mean(x, axis=-1, keepdims=True)
    xc = x - mu
    var = jnp.mean(xc * xc, axis=-1, keepdims=True)
    rstd = lax.rsqrt(var + LN_EPS)
    xhat = xc * rstd
    return xhat * g + b, xhat, rstd


def _ln_bwd(dy, xhat, rstd, g):
    dxh = dy * g
    m1 = jnp.mean(dxh, axis=-1, keepdims=True)
    m2 = jnp.mean(dxh * xhat, axis=-1, keepdims=True)
    return rstd * (dxh - m1 - xhat * m2)


def _colsum(x):
    return jnp.sum(x, axis=0, keepdims=True)


def _to_pad_cols(w):
    u, q, k, v, o, gi, gf, gs, gm = (w[..., 0:512], w[..., 512:1024], w[..., 1024:1536], w[..., 1536:2560],
                                     w[..., 2560:3584], w[..., 3584:3588], w[..., 3588:3592], w[..., 3592:4616],
                                     w[..., 4616:5640])
    z = jnp.zeros(w.shape[:-1] + (NP - G_OFF - 8,), w.dtype)
    return jnp.concatenate([o, gs, gm, v, q, k, u, gi, gf, z], axis=-1)


def _from_pad_cols(w):
    o, gs, gm, v, q, k, u = (w[..., O_OFF:GS_OFF], w[..., GS_OFF:GM_OFF], w[..., GM_OFF:V_OFF], w[..., V_OFF:Q_OFF],
                             w[..., Q_OFF:K_OFF], w[..., K_OFF:U_OFF], w[..., U_OFF:G_OFF])
    gi, gf = w[..., G_OFF:G_OFF + 4], w[..., G_OFF + 4:G_OFF + 8]
    return jnp.concatenate([u, q, k, v, o, gi, gf, gs, gm], axis=-1)


_IN_REF = (("u", 512), ("q", 512), ("k", 512), ("v", 1024), ("o", 1024), ("i", 4), ("f", 4), ("gs", 1024), ("gm", 1024))
_IN_PAD = (("o", O_OFF), ("gs", GS_OFF), ("gm", GM_OFF), ("v", V_OFF), ("q", Q_OFF), ("k", K_OFF), ("u", U_OFF),
           ("i", G_OFF), ("f", G_OFF + 4))


def _in_ref_ranges():
    out, off = {}, 0
    for n, s in _IN_REF:
        out[n] = (off, off + s)
        off += s
    return out, off


def _w_in_from_slots(g, chunk=None):
    rng, total = _in_ref_ranges()
    width = total // g.shape[0]
    cols = []
    for n, _ in _IN_PAD:
        a, b = rng[n]
        while a < b:
            s = a // width
            e = min(b, (s + 1) * width)
            cols.append(g[s][:, a - s * width:e - s * width])
            a = e
    cols.append(jnp.zeros((g.shape[1], NP - G_OFF - 8), g.dtype))
    if chunk is None:
        return jnp.concatenate(cols, axis=1)
    chunks, cur, room = [], [], chunk
    for c in cols:
        while c.shape[1] > 0:
            take = min(room, c.shape[1])
            cur.append(c[:, :take])
            c, room = c[:, take:], room - take
            if room == 0:
                chunks.append(jnp.concatenate(cur, axis=1))
                cur, room = [], chunk
    assert not cur
    return jnp.stack(chunks, axis=0)


def _slots_from_w_in(wp, nslot=4):
    rng, total = _in_ref_ranges()
    width = total // nslot
    pad_off = dict(_IN_PAD)
    slots = []
    for s in range(nslot):
        lo, hi = s * width, (s + 1) * width
        cols = []
        for n, _ in _IN_REF:
            a, b = rng[n]
            x0, x1 = max(a, lo), min(b, hi)
            if x0 < x1:
                cols.append(wp[:, pad_off[n] + x0 - a:pad_off[n] + x1 - a])
        slots.append(jnp.concatenate(cols, axis=1))
    return jnp.stack(slots, axis=0)


def _ln0_fwd(hin, g, b, lp):
    r, d = hin.shape
    tm = _row_tile(lp, 416)

    def body(x_ref, g_ref, b_ref, o_ref, ob_ref):
        y, _, _ = _ln_fwd(x_ref[...], g_ref[...], b_ref[...])
        o_ref[...] = y
        ob_ref[...] = _bf(y)

    row = pl.BlockSpec((tm, d), lambda i: (i, 0))
    return _pcall(
        body, name="ln0_fwd", grid=(r // tm,),
        in_specs=[row, _const((1, d)), _const((1, d))],
        out_specs=[row, row],
        out_shape=[jax.ShapeDtypeStruct((r, d), F32), jax.ShapeDtypeStruct((r, d), BF16)],
        compiler_params=_cp(("parallel",)),
    )(hin, g, b)


def _ln0_bwd(hin, dr1, dpw, g, lp):
    r, d = hin.shape
    tm = _row_tile(lp, 416)
    tps = lp // tm
    assert tm >= PAD + N_META

    def body(x_ref, a_ref, c_ref, g_ref, o_ref, dg_ref, db_ref, dm_ref):
        i = pl.program_id(0)

        @pl.when(i == 0)
        def _():
            dg_ref[...] = jnp.zeros_like(dg_ref)
            db_ref[...] = jnp.zeros_like(db_ref)
            dm_ref[...] = jnp.zeros_like(dm_ref)

        dy = ALPHA * a_ref[...] + c_ref[...]
        _, xhat, rstd = _ln_fwd(x_ref[...], g_ref[...], 0.0)
        dx = _ln_bwd(dy, xhat, rstd, g_ref[...])
        o_ref[...] = dx
        dg_ref[...] += _colsum(dy * xhat)
        db_ref[...] += _colsum(dy)

        @pl.when(i % tps == 0)
        def _():
            dm_ref[...] += dx[PAD:PAD + N_META, :]

    return _pcall(
        body, name="ln0_bwd", grid=(r // tm,),
        in_specs=[pl.BlockSpec((tm, d), lambda i: (i, 0))] * 3 + [_const((1, d))],
        out_specs=[pl.BlockSpec((tm, d), lambda i: (i, 0)), _const((1, d)), _const((1, d)), _const((N_META, d))],
        out_shape=[jax.ShapeDtypeStruct((r, d), F32), jax.ShapeDtypeStruct((1, d), F32),
                   jax.ShapeDtypeStruct((1, d), F32), jax.ShapeDtypeStruct((N_META, d), F32)],
        compiler_params=_cp(("arbitrary",)),
    )(hin, dr1, dpw, g)


IN_CHUNK = 1152


def _chunk_cols(w):
    k, n = w.shape
    return jnp.transpose(w.reshape(k, n // IN_CHUNK, IN_CHUNK), (1, 0, 2))


def _inproj(h0b, w3, bias, lp):
    r, d = h0b.shape
    nj, _, tn = w3.shape
    tm = _row_tile(lp, 832)
    tps = lp // tm

    def body(a_ref, w_ref, b_ref, o_ref):
        i = pl.program_id(0)
        j = pl.program_id(1)
        acc = _dot(a_ref[...], w_ref[j]) + b_ref[...]
        t = (i % tps) * tm + lax.broadcasted_iota(jnp.int32, (tm, 1), 0)
        o_ref[...] = jnp.where(t >= PAD, acc, 0.0)

    return _pcall(
        body, name="inproj", grid=(r // tm, nj),
        in_specs=[pl.BlockSpec((tm, d), lambda i, j: (i, 0)), _resident(w3.shape),
                  pl.BlockSpec((1, tn), lambda i, j: (0, j))],
        out_specs=pl.BlockSpec((tm, tn), lambda i, j: (i, j)),
        out_shape=jax.ShapeDtypeStruct((r, nj * tn), F32),
        compiler_params=_cp(("parallel", "arbitrary"), 48),
    )(h0b, w3, bias)


def _mm_tn(a, b, *, name, split=1, colsum=False, tk_want=1408):
    r, m = a.shape
    n = b.shape[1]
    tk = _row_tile(r, tk_want)
    tm = min(m, 1024)
    ns = n // split
    tn = ns
    for cand in (1024, 1152, 640, 512, 128):
        if ns % cand == 0 and cand <= ns:
            tn = cand
            break
    nb = ns // tn
    nk = r // tk

    def body(a_ref, b_ref, o_ref, *rest):
        acc = rest[-1]
        k = pl.program_id(2)

        @pl.when(k == 0)
        def _():
            acc[...] = jnp.zeros_like(acc)

        bt = b_ref[...]
        acc[...] += _dot(_bf(a_ref[...]), _bf(bt), TN)

        @pl.when(k == nk - 1)
        def _():
            o_ref[...] = acc[...]

        if colsum:
            cs_ref = rest[0]

            @pl.when(k == 0)
            def _():
                cs_ref[...] = jnp.zeros_like(cs_ref)

            cs_ref[...] += _colsum(bt.astype(F32))

    out_specs = [pl.BlockSpec((None, tm, tn), lambda i, j, k: (j // nb, i, j % nb))]
    out_shape = [jax.ShapeDtypeStruct((split, m, ns), F32)]
    if colsum:
        assert m == tm
        out_specs.append(pl.BlockSpec((1, tn), lambda i, j, k: (0, j)))
        out_shape.append(jax.ShapeDtypeStruct((1, n), F32))
    res = _pcall(
        body, name=name, grid=(m // tm, n // tn, nk),
        in_specs=[pl.BlockSpec((tk, tm), lambda i, j, k: (k, i)), pl.BlockSpec((tk, tn), lambda i, j, k: (k, j))],
        out_specs=out_specs, out_shape=out_shape,
        scratch_shapes=[pltpu.VMEM((tm, tn), F32)],
        compiler_params=_cp(("parallel", "parallel", "arbitrary"), 56),
    )(a, b)
    return res if colsum else res[0]


def _mm_nt(a, w3, lp, *, name, dep=None):
    r, kdim = a.shape
    nk, n, tk = w3.shape
    assert nk * tk == kdim
    tm = _row_tile(lp, 832)
    deps = [] if dep is None else [dep]

    def body(a_ref, w_ref, *rest):
        o_ref, acc = rest[-2:]
        k = pl.program_id(1)

        @pl.when(k == 0)
        def _():
            acc[...] = jnp.zeros_like(acc)

        acc[...] += _dot(_bf(a_ref[...]), w_ref[k], NT)

        @pl.when(k == nk - 1)
        def _():
            o_ref[...] = acc[...]

    return _pcall(
        body, name=name, grid=(r // tm, nk),
        in_specs=[pl.BlockSpec((tm, tk), lambda i, k: (i, k)), _resident(w3.shape)]
        + [_const(dp_.shape) for dp_ in deps],
        out_specs=pl.BlockSpec((tm, n), lambda i, k: (i, 0)),
        out_shape=jax.ShapeDtypeStruct((r, n), F32),
        scratch_shapes=[pltpu.VMEM((tm, n), F32)],
        compiler_params=_cp(("parallel", "arbitrary"), 48),
    )(a, w3, *deps)


def _s5_prep(lam_re, lam_im, log_dt, b_re_t, b_im_t):
    g, p = lam_re.shape
    h = b_re_t.shape[0]

    def body(lr_ref, li_ref, ldt_ref, br_ref, bi_ref, pr_ref, pi_ref, bbr_ref, bbi_ref):
        lr, li = lr_ref[...], li_ref[...]
        dt = jnp.exp(ldt_ref[...])
        e = jnp.exp(lr * dt)
        ar, ai = e * jnp.cos(li * dt), e * jnp.sin(li * dt)
        den = lr * lr + li * li
        cr = ((ar - 1.0) * lr + ai * li) / den
        ci = (ai * lr - (ar - 1.0) * li) / den
        br, bi = br_ref[...], bi_ref[...]
        bbr_ref[...] = cr[None] * br - ci[None] * bi
        bbi_ref[...] = cr[None] * bi + ci[None] * br
        xr, xi = ar, ai
        pr_ref[0] = xr
        pi_ref[0] = xi
        for t in range(1, 8):
            xr, xi = xr * ar - xi * ai, xr * ai + xi * ar
            pr_ref[t] = xr
            pi_ref[t] = xi

    sd = jax.ShapeDtypeStruct
    return _pcall(body, name="s5_prep",
                  out_shape=[sd((8, g, p), F32), sd((8, g, p), F32), sd((h, g, p), F32), sd((h, g, p), F32)])(
        lam_re, lam_im, log_dt, b_re_t, b_im_t)


def _s5_prep_bwd(lam_re, lam_im, log_dt, b_re_t, b_im_t, da_re, da_im, dbb_re_t, dbb_im_t):
    g, p = lam_re.shape
    h = b_re_t.shape[0]

    def body(lr_ref, li_ref, ldt_ref, br_ref, bi_ref, dar_ref, dai_ref, dbr_ref, dbi_ref,
             glr_ref, gli_ref, gdt_ref, gbr_ref, gbi_ref):
        lr, li = lr_ref[...], li_ref[...]
        dt = jnp.exp(ldt_ref[...])
        e = jnp.exp(lr * dt)
        ar, ai = e * jnp.cos(li * dt), e * jnp.sin(li * dt)
        den = lr * lr + li * li
        cr = ((ar - 1.0) * lr + ai * li) / den
        ci = (ai * lr - (ar - 1.0) * li) / den
        br, bi = br_ref[...], bi_ref[...]
        gr, gi = dbr_ref[...], dbi_ref[...]
        gbr_ref[...] = gr * cr[None] + gi * ci[None]
        gbi_ref[...] = gi * cr[None] - gr * ci[None]
        gcr = jnp.sum(gr * br + gi * bi, axis=0)
        gci = jnp.sum(gi * br - gr * bi, axis=0)
        ilr, ili = lr / den, -li / den
        gar = dar_ref[...] + gcr * ilr + gci * ili
        gai = dai_ref[...] + gci * ilr - gcr * ili
        qr, qi = cr * ilr - ci * ili, cr * ili + ci * ilr
        glr = -(gcr * qr + gci * qi)
        gli = -(gci * qr - gcr * qi)
        gzr = gar * ar + gai * ai
        gzi = gai * ar - gar * ai
        glr_ref[...] = glr + gzr * dt
        gli_ref[...] = gli + gzi * dt
        gdt_ref[...] = jnp.sum(gzr * lr + gzi * li, axis=1, keepdims=True) * dt

    sd = jax.ShapeDtypeStruct
    return _pcall(body, name="s5_prep_bwd",
                  out_shape=[sd((g, p), F32), sd((g, p), F32), sd((g, 1), F32), sd((h, g, p), F32), sd((h, g, p), F32)])(
        lam_re, lam_im, log_dt, b_re_t, b_im_t, da_re, da_im, dbb_re_t, dbb_im_t)


def _cmul(xr, xi, yr, yi):
    return xr * yr - xi * yi, xr * yi + xi * yr


def _dot5(a, b, dims=NN):
    return _dot(_bf(a), _bf(b), dims)


def _s5_fwd(p3, bk, cre, cim, apow, dskip):
    bsz, lp, _ = p3.shape
    tt = _row_tile(lp, 528, 8)
    nt = lp // tt
    nblk = tt // 8
    hw = 512

    def body(u_ref, bk_ref, cre_ref, cim_ref, ap_ref, d_ref, y_ref, xs_ref, car_ref):
        t = pl.program_id(2)

        @pl.when(t == 0)
        def _():
            car_ref[...] = jnp.zeros_like(car_ref)

        u = u_ref[...]
        xs_ref[...] = _dot5(u, bk_ref[...])
        ap = ap_ref[...]
        apr, api = ap[:, :hw], ap[:, hw:]
        rows = lax.broadcasted_iota(jnp.int32, (8, hw), 0)
        lev = [(d, jnp.where(rows < d, 0.0, jnp.broadcast_to(apr[d - 1:d, :], (8, hw))),
                jnp.where(rows < d, 0.0, jnp.broadcast_to(api[d - 1:d, :], (8, hw)))) for d in (1, 2, 4)]

        def blk(i, carry):
            cr, ci = carry
            off = pl.multiple_of(i * 8, 8)
            x = xs_ref[pl.ds(off, 8), :]
            xr, xi = x[:, :hw], x[:, hw:]
            for d, lr, li in lev:
                mr, mi = _cmul(pltpu.roll(xr, d, 0), pltpu.roll(xi, d, 0), lr, li)
                xr, xi = xr + mr, xi + mi
            mr, mi = _cmul(apr, api, cr, ci)
            xr, xi = xr + mr, xi + mi
            xs_ref[pl.ds(off, 8), :] = jnp.concatenate([xr, xi], axis=1)
            return xr[7:8, :], xi[7:8, :]

        c0 = car_ref[...]
        cr, ci = lax.fori_loop(0, nblk, blk, (c0[0:1, :hw], c0[0:1, hw:]))
        car_ref[...] = jnp.broadcast_to(jnp.concatenate([cr, ci], axis=1), car_ref.shape)
        xs = xs_ref[...]
        y_ref[...] = (_dot5(xs[:, :hw], cre_ref[...]) - _dot5(xs[:, hw:], cim_ref[...])
                      + d_ref[...] * u)

    ub = U_OFF // 128
    return _pcall(
        body, name="s5_fwd", grid=(S5_KCH, bsz, nt),
        in_specs=[pl.BlockSpec((None, tt, 128), lambda k, b, t: (b, t, ub + k)),
                  pl.BlockSpec((None, 128, 2 * hw), lambda k, b, t: (k, 0, 0)),
                  pl.BlockSpec((None, hw, 128), lambda k, b, t: (k, 0, 0)),
                  pl.BlockSpec((None, hw, 128), lambda k, b, t: (k, 0, 0)),
                  pl.BlockSpec((None, 8, 2 * hw), lambda k, b, t: (k, 0, 0)),
                  pl.BlockSpec((1, 128), lambda k, b, t: (0, k))],
        out_specs=[pl.BlockSpec((None, tt, 128), lambda k, b, t: (b, t, k)),
                   pl.BlockSpec((None, None, tt, 2 * hw), lambda k, b, t: (b, k, t, 0))],
        out_shape=[jax.ShapeDtypeStruct((bsz, lp, S5_KCH * 128), F32),
                   jax.ShapeDtypeStruct((bsz, S5_KCH, lp, 2 * hw), F32)],
        scratch_shapes=[pltpu.VMEM((8, 2 * hw), F32)],
        compiler_params=_cp(("parallel", "parallel", "arbitrary"), 40),
    )(p3, bk, cre, cim, apow, dskip)


def _s5_bwd(dp3, p3, dy3, xs, bk, cre, cim, apow_rev, dskip):
    bsz, lp, _ = p3.shape
    tt = _row_tile(lp, 528, 8)
    nt = lp // tt
    nblk = tt // 8
    hw = 512
    tb = tt // 8

    def body(dp_any, u_ref, dy_ref, xs_ref, halo_ref, bk_ref, cre_ref, cim_ref, ap_ref, d_ref,
             du_ref, dbk_ref, dcre_ref, dcim_ref, da_ref, dd_ref, g_ref, ext_ref, car_ref):
        b = pl.program_id(1)
        t = pl.program_id(2)
        tidx = nt - 1 - t

        @pl.when(t == 0)
        def _():
            car_ref[...] = jnp.zeros_like(car_ref)

        @pl.when((b == 0) & (t == 0))
        def _():
            dbk_ref[...] = jnp.zeros_like(dbk_ref)
            dcre_ref[...] = jnp.zeros_like(dcre_ref)
            dcim_ref[...] = jnp.zeros_like(dcim_ref)
            da_ref[...] = jnp.zeros_like(da_ref)
            dd_ref[...] = jnp.zeros_like(dd_ref)

        u = u_ref[...]
        dy = dy_ref[...]
        g_ref[:, :hw] = _dot5(dy, cre_ref[...], NT)
        g_ref[:, hw:] = -_dot5(dy, cim_ref[...], NT)
        ap = ap_ref[...]
        apr, api = ap[:, :hw], -ap[:, hw:]
        rows = lax.broadcasted_iota(jnp.int32, (8, hw), 0)
        lev = [(d, jnp.where(rows >= 8 - d, 0.0, jnp.broadcast_to(apr[8 - d:9 - d, :], (8, hw))),
                jnp.where(rows >= 8 - d, 0.0, jnp.broadcast_to(api[8 - d:9 - d, :], (8, hw)))) for d in (1, 2, 4)]

        def blk(i, carry):
            cr, ci = carry
            off = pl.multiple_of((nblk - 1 - i) * 8, 8)
            x = g_ref[pl.ds(off, 8), :]
            xr, xi = x[:, :hw], x[:, hw:]
            for d, lr, li in lev:
                mr, mi = _cmul(pltpu.roll(xr, 8 - d, 0), pltpu.roll(xi, 8 - d, 0), lr, li)
                xr, xi = xr + mr, xi + mi
            mr, mi = _cmul(apr, api, cr, ci)
            xr, xi = xr + mr, xi + mi
            g_ref[pl.ds(off, 8), :] = jnp.concatenate([xr, xi], axis=1)
            return xr[0:1, :], xi[0:1, :]

        c0 = car_ref[...]
        cr, ci = lax.fori_loop(0, nblk, blk, (c0[0:1, :hw], c0[0:1, hw:]))
        car_ref[...] = jnp.broadcast_to(jnp.concatenate([cr, ci], axis=1), car_ref.shape)

        gg = g_ref[...]
        du = _dot5(gg, bk_ref[...], NT) + d_ref[...] * dy
        trow = tidx * tt + lax.broadcasted_iota(jnp.int32, (tt, 1), 0)
        du_ref[...] = jnp.where(trow >= PAD, du, 0.0).astype(du_ref.dtype)
        dbk_ref[...] += _dot5(u, gg, TN)
        xsv = xs_ref[...]
        dcre_ref[...] += _dot5(xsv[:, :hw], dy, TN)
        dcim_ref[...] -= _dot5(xsv[:, hw:], dy, TN)
        dd_ref[...] += _colsum(dy * u)
        ext_ref[0:8, :] = jnp.where(tidx == 0, 0.0, halo_ref[...])
        ext_ref[8:, :] = xsv
        xp = ext_ref[pl.ds(7, tt), :]
        gr, gi, pr, pi = gg[:, :hw], gg[:, hw:], xp[:, :hw], xp[:, hw:]
        da_ref[:, :hw] += _colsum(gr * pr + gi * pi)
        da_ref[:, hw:] += _colsum(gi * pr - gr * pi)

    ub = U_OFF // 128
    sd = jax.ShapeDtypeStruct
    rt = lambda t: nt - 1 - t
    res = _pcall(
        body, name="s5_bwd", grid=(S5_KCH, bsz, nt),
        in_specs=[pl.BlockSpec(memory_space=pl.ANY),
                  pl.BlockSpec((None, tt, 128), lambda k, b, t: (b, rt(t), ub + k)),
                  pl.BlockSpec((None, tt, 128), lambda k, b, t: (b, rt(t), k)),
                  pl.BlockSpec((None, None, tt, 2 * hw), lambda k, b, t: (b, k, rt(t), 0)),
                  pl.BlockSpec((None, None, 8, 2 * hw), lambda k, b, t: (b, k, jnp.maximum(rt(t) * tb - 1, 0), 0)),
                  pl.BlockSpec((None, 128, 2 * hw), lambda k, b, t: (k, 0, 0)),
                  pl.BlockSpec((None, hw, 128), lambda k, b, t: (k, 0, 0)),
                  pl.BlockSpec((None, hw, 128), lambda k, b, t: (k, 0, 0)),
                  pl.BlockSpec((None, 8, 2 * hw), lambda k, b, t: (k, 0, 0)),
                  pl.BlockSpec((1, 128), lambda k, b, t: (0, k))],
        out_specs=[pl.BlockSpec((None, tt, 128), lambda k, b, t: (b, rt(t), ub + k)),
                   pl.BlockSpec((None, 128, 2 * hw), lambda k, b, t: (k, 0, 0)),
                   pl.BlockSpec((None, hw, 128), lambda k, b, t: (k, 0, 0)),
                   pl.BlockSpec((None, hw, 128), lambda k, b, t: (k, 0, 0)),
                   pl.BlockSpec((None, 1, 2 * hw), lambda k, b, t: (k, 0, 0)),
                   pl.BlockSpec((1, 128), lambda k, b, t: (0, k))],
        out_shape=[sd(dp3.shape, dp3.dtype), sd((S5_KCH, 128, 2 * hw), F32), sd((S5_KCH, hw, 128), F32),
                   sd((S5_KCH, hw, 128), F32), sd((S5_KCH, 1, 2 * hw), F32), sd((1, S5_KCH * 128), F32)],
        scratch_shapes=[pltpu.VMEM((tt, 2 * hw), F32), pltpu.VMEM((tt + 8, 2 * hw), F32), pltpu.VMEM((8, 2 * hw), F32)],
        input_output_aliases={0: 0},
        compiler_params=_cp(("arbitrary", "arbitrary", "arbitrary"), 48),
    )(dp3, p3, dy3, xs, xs, bk, cre, cim, apow_rev, dskip)
    return res


_G0 = math.sqrt(2.0 / math.pi)
_G1 = 0.044715


def _gelu(y):
    return 0.5 * y * (1.0 + jnp.tanh(_G0 * (y + _G1 * y * y * y)))


def _gelu_grad(y):
    th = jnp.tanh(_G0 * (y + _G1 * y * y * y))
    return 0.5 * (1.0 + th) + 0.5 * y * (1.0 - th * th) * _G0 * (1.0 + 3.0 * _G1 * y * y)


def _glu_fwd(y_s5, wglu_g, lp):
    r, w = y_s5.shape
    tm = _row_tile(lp, 416)
    cw = wglu_g.shape[2]

    def body(y_ref, w_ref, gy_ref, z_ref, o_ref):
        gy = _bf(_gelu(y_ref[...]))
        gy_ref[...] = gy
        zs = [_dot(gy, w_ref[s]) for s in range(4)]
        for s in range(4):
            z_ref[:, s * cw:(s + 1) * cw] = zs[s]
        o_ref[:, :cw] = zs[0] * _sig(zs[2])
        o_ref[:, cw:] = zs[1] * _sig(zs[3])

    sd = jax.ShapeDtypeStruct
    return _pcall(
        body, name="glu_fwd", grid=(r // tm,),
        in_specs=[pl.BlockSpec((tm, w), lambda i: (i, 0)), _resident(wglu_g.shape)],
        out_specs=[pl.BlockSpec((tm, w), lambda i: (i, 0)), pl.BlockSpec((tm, 4 * cw), lambda i: (i, 0)),
                   pl.BlockSpec((tm, 2 * cw), lambda i: (i, 0))],
        out_shape=[sd((r, w), BF16), sd((r, 4 * cw), F32), sd((r, 2 * cw), F32)],
        compiler_params=_cp(("parallel",), 40),
    )(y_s5, wglu_g)


def _glu_bwd(dyg, z, y_s5, wglu_g, lp):
    r, w = y_s5.shape
    tm = _row_tile(lp, 416)
    cw = wglu_g.shape[2]

    def body(d_ref, z_ref, y_ref, w_ref, dz_ref, dy_ref):
        d = d_ref[...]
        zz = z_ref[...]
        acc = jnp.zeros((tm, w), F32)
        for s in range(2):
            z1 = zz[:, s * cw:(s + 1) * cw]
            sg = _sig(zz[:, (2 + s) * cw:(3 + s) * cw])
            dd = d[:, s * cw:(s + 1) * cw]
            dz1 = _bf(dd * sg)
            dz2 = _bf(dd * z1 * sg * (1.0 - sg))
            dz_ref[:, s * cw:(s + 1) * cw] = dz1
            dz_ref[:, (2 + s) * cw:(3 + s) * cw] = dz2
            acc += _dot(dz1, w_ref[s], NT) + _dot(dz2, w_ref[2 + s], NT)
        dy_ref[...] = acc * _gelu_grad(y_ref[...])

    sd = jax.ShapeDtypeStruct
    return _pcall(
        body, name="glu_bwd", grid=(r // tm,),
        in_specs=[pl.BlockSpec((tm, 2 * cw), lambda i: (i, 0)), pl.BlockSpec((tm, 4 * cw), lambda i: (i, 0)),
                  pl.BlockSpec((tm, w), lambda i: (i, 0)), _resident(wglu_g.shape)],
        out_specs=[pl.BlockSpec((tm, 4 * cw), lambda i: (i, 0)), pl.BlockSpec((tm, w), lambda i: (i, 0))],
        out_shape=[sd((r, 4 * cw), BF16), sd((r, w), F32)],
        compiler_params=_cp(("parallel",), 40),
    )(dyg, z, y_s5, wglu_g)


def _conv_fwd(p3, cw, cb):
    bsz, lp, _ = p3.shape
    tt = _row_tile(lp, 416)
    nt = lp // tt
    tb = tt // 8
    c = cw.shape[1]
    qb = Q_OFF // c

    def body(x_ref, halo_ref, w_ref, b_ref, pre_ref, act_ref, ext_ref):
        t = pl.program_id(1)
        ext_ref[0:8, :] = jnp.where(t == 0, 0.0, halo_ref[...])
        ext_ref[8:, :] = x_ref[...]
        w = w_ref[...]
        acc = b_ref[...] + w[0:1, :] * ext_ref[pl.ds(5, tt), :]
        for j in range(1, CONV_W):
            acc = acc + w[j:j + 1, :] * ext_ref[pl.ds(5 + j, tt), :]
        pre_ref[...] = acc
        act_ref[...] = acc * _sig(acc)

    sd = jax.ShapeDtypeStruct
    return _pcall(
        body, name="conv_fwd", grid=(bsz, nt),
        in_specs=[pl.BlockSpec((None, tt, c), lambda b, t: (b, t, qb)),
                  pl.BlockSpec((None, 8, c), lambda b, t: (b, jnp.maximum(t * tb - 1, 0), qb)),
                  _const((CONV_W, c)), _const((1, c))],
        out_specs=[pl.BlockSpec((None, tt, c), lambda b, t: (b, t, 0))] * 2,
        out_shape=[sd((bsz, lp, c), F32)] * 2,
        scratch_shapes=[pltpu.VMEM((tt + 8, c), F32)],
        compiler_params=_cp(("parallel", "parallel")),
    )(p3, p3, cw, cb)


def _conv_bwd(dp3, p3, dact3, pre3, cw):
    bsz, lp, _ = p3.shape
    tt = _row_tile(lp, 416)
    nt = lp // tt
    tb = tt // 8
    c = cw.shape[1]
    qb = Q_OFF // c

    def silu_grad(x):
        s = _sig(x)
        return s * (1.0 + x * (1.0 - s))

    def body(dp_any, x_ref, xh_ref, d_ref, dh_ref, pre_ref, preh_ref, w_ref, o_ref, dw_ref, db_ref, ext_ref, dext_ref):
        b = pl.program_id(0)
        t = pl.program_id(1)

        @pl.when((b == 0) & (t == 0))
        def _():
            dw_ref[...] = jnp.zeros_like(dw_ref)
            db_ref[...] = jnp.zeros_like(db_ref)

        dc = d_ref[...] * silu_grad(pre_ref[...])
        dch = jnp.where(t == nt - 1, 0.0, dh_ref[...] * silu_grad(preh_ref[...]))
        dext_ref[0:tt, :] = dc
        dext_ref[tt:, :] = dch
        ext_ref[0:8, :] = jnp.where(t == 0, 0.0, xh_ref[...])
        ext_ref[8:, :] = x_ref[...]
        w = w_ref[...]
        acc = w[CONV_W - 1:CONV_W, :] * dc
        for j in range(CONV_W - 1):
            acc = acc + w[j:j + 1, :] * dext_ref[pl.ds(CONV_W - 1 - j, tt), :]
        trow = t * tt + lax.broadcasted_iota(jnp.int32, (tt, 1), 0)
        o_ref[...] = jnp.where(trow >= PAD, acc, 0.0).astype(o_ref.dtype)
        db_ref[...] += _colsum(dc)
        for j in range(CONV_W):
            dw_ref[j:j + 1, :] += _colsum(dc * ext_ref[pl.ds(5 + j, tt), :])

    sd = jax.ShapeDtypeStruct
    nxt = lambda t: jnp.minimum((t + 1) * tb, lp // 8 - 1)
    return _pcall(
        body, name="conv_bwd", grid=(bsz, nt),
        in_specs=[pl.BlockSpec(memory_space=pl.ANY),
                  pl.BlockSpec((None, tt, c), lambda b, t: (b, t, qb)),
                  pl.BlockSpec((None, 8, c), lambda b, t: (b, jnp.maximum(t * tb - 1, 0), qb)),
                  pl.BlockSpec((None, tt, c), lambda b, t: (b, t, 0)),
                  pl.BlockSpec((None, 8, c), lambda b, t: (b, nxt(t), 0)),
                  pl.BlockSpec((None, tt, c), lambda b, t: (b, t, 0)),
                  pl.BlockSpec((None, 8, c), lambda b, t: (b, nxt(t), 0)),
                  _const((CONV_W, c))],
        out_specs=[pl.BlockSpec((None, tt, c), lambda b, t: (b, t, qb)), _const((CONV_W, c)), _const((1, c))],
        out_shape=[sd(dp3.shape, dp3.dtype), sd((CONV_W, c), F32), sd((1, c), F32)],
        scratch_shapes=[pltpu.VMEM((tt + 8, c), F32), pltpu.VMEM((tt + 8, c), F32)],
        input_output_aliases={0: 0},
        compiler_params=_cp(("arbitrary", "arbitrary")),
    )(dp3, p3, p3, dact3, dact3, pre3, pre3, cw)


def _mlstm_gates(g, h_idx, c_idx, lc):
    lane = lax.broadcasted_iota(jnp.int32, g.shape, 1)
    i_col = jnp.sum(jnp.where(lane == h_idx, g, 0.0), axis=1, keepdims=True)
    f_col = jnp.sum(jnp.where(lane == M_HEADS + h_idx, g, 0.0), axis=1, keepdims=True)
    row = lax.broadcasted_iota(jnp.int32, (lc, 1), 0)
    valid = (c_idx * lc + row) >= PAD
    li = jnp.where(valid, i_col, NEG)
    lf = jnp.where(valid, jnp.minimum(f_col, 0.0) - jnp.log(1.0 + jnp.exp(-jnp.abs(f_col))), 0.0)
    r2 = lax.broadcasted_iota(jnp.int32, (lc, lc), 0)
    c2 = lax.broadcasted_iota(jnp.int32, (lc, lc), 1)
    eye = r2 == c2
    tril = r2 >= c2
    to_row = lambda col: jnp.sum(jnp.where(eye, col, 0.0), axis=0, keepdims=True)
    lf_row = to_row(lf)
    b_col = jnp.sum(jnp.where(tril, lf_row, 0.0), axis=1, keepdims=True)
    b_row = to_row(b_col)
    li_row = to_row(li)
    d_mat = jnp.where(tril, b_col - b_row + li_row, NEG)
    return dict(f_col=f_col, valid=valid, li=li, b_col=b_col, d_mat=d_mat, eye=eye, r2=r2, c2=c2, row=row,
                to_row=to_row)


def _mlstm_chunk(q, ks, v, gq, c_st, n_st, m_st, lc):
    b_col, d_mat = gq["b_col"], gq["d_mat"]
    m_inter = b_col + m_st
    m_row = jnp.maximum(m_inter, jnp.max(d_mat, axis=1, keepdims=True))
    w_intra = jnp.exp(d_mat - m_row)
    w_inter = jnp.exp(m_inter - m_row)
    qb, kb, vb, cb = _bf(q), _bf(ks), _bf(v), _bf(c_st)
    s = _dot(qb, kb, NT) * w_intra
    qc = _dot(qb, cb)
    num = _dot(_bf(s), vb) + w_inter * qc
    qn = jnp.sum(q * n_st, axis=1, keepdims=True)
    den = jnp.sum(s, axis=1, keepdims=True) + w_inter * qn
    e = jnp.exp(-m_row)
    nn = jnp.maximum(jnp.abs(den), e)
    b_last = b_col[lc - 1:lc, :]
    g_log = b_last - b_col + gq["li"]
    m_new = jnp.maximum(b_last + m_st, jnp.max(g_log, axis=0, keepdims=True))
    w_k = jnp.exp(g_log - m_new)
    decay = jnp.exp(b_last + m_st - m_new)
    return dict(w_intra=w_intra, w_inter=w_inter, qb=qb, kb=kb, vb=vb, cb=cb, s=s, qc=qc, num=num, qn=qn, den=den,
                e=e, nn=nn, m_new=m_new, w_k=w_k, decay=decay)


def _chunks_per_step(nc):
    return max(c for c in (3, 2, 1) if nc % c == 0)


def _mlstm_fwd(qk3, p3):
    bsz, lp, _ = p3.shape
    lc = M_CHUNK
    nc = lp // lc
    dk, dv = 128, 256
    scale = dk ** -0.5

    cps = _chunks_per_step(nc)
    rows = cps * lc

    def body(q_ref, k_ref, v_ref, g_ref, h_ref, cs_ref, ns_ref, ms_ref, c_sc, n_sc, m_sc):
        st = pl.program_id(1)

        @pl.when(st == 0)
        def _():
            c_sc[...] = jnp.zeros_like(c_sc)
            n_sc[...] = jnp.zeros_like(n_sc)
            m_sc[...] = jnp.zeros_like(m_sc)

        for j in range(cps):
            rs = slice(j * lc, (j + 1) * lc)
            g = g_ref[rs, :]
            for hh in range(M_HEADS):
                c_st, n_st, m_all = c_sc[hh], n_sc[hh], m_sc[hh]
                cs_ref[hh, j] = c_st
                ns_ref[hh, j] = n_st
                ms_ref[hh, j] = m_all
                m_st = m_all[:, 0:1]
                q = q_ref[rs, hh * dk:(hh + 1) * dk]
                ks = k_ref[rs, hh * dk:(hh + 1) * dk] * scale
                v = v_ref[rs, hh * dv:(hh + 1) * dv]
                gq = _mlstm_gates(g, hh, st * cps + j, lc)
                f = _mlstm_chunk(q, ks, v, gq, c_st, n_st, m_st, lc)
                h_ref[rs, hh * dv:(hh + 1) * dv] = f["num"] / f["nn"]
                kw = ks * f["w_k"]
                c_sc[hh] = f["decay"] * c_st + _dot(_bf(kw), f["vb"], TN)
                n_sc[hh] = f["decay"] * n_st + _colsum(kw)
                m_sc[hh] = jnp.broadcast_to(f["m_new"], (1, 128))

    sd = jax.ShapeDtypeStruct
    nh = M_HEADS
    return _pcall(
        body, name="mlstm_fwd", grid=(bsz, nc // cps),
        in_specs=[pl.BlockSpec((None, rows, nh * dk), lambda b, c: (b, c, 0)),
                  pl.BlockSpec((None, rows, nh * dk), lambda b, c: (b, c, 1)),
                  pl.BlockSpec((None, rows, nh * dv), lambda b, c: (b, c, V_OFF // (nh * dv))),
                  pl.BlockSpec((None, rows, 128), lambda b, c: (b, c, G_OFF // 128))],
        out_specs=[pl.BlockSpec((None, rows, nh * dv), lambda b, c: (b, c, 0)),
                   pl.BlockSpec((None, nh, cps, dk, dv), lambda b, c: (b, 0, c, 0, 0)),
                   pl.BlockSpec((None, nh, cps, 1, dk), lambda b, c: (b, 0, c, 0, 0)),
                   pl.BlockSpec((None, nh, cps, 1, 128), lambda b, c: (b, 0, c, 0, 0))],
        out_shape=[sd((bsz, lp, nh * dv), F32), sd((bsz, nh, nc, dk, dv), F32),
                   sd((bsz, nh, nc, 1, dk), F32), sd((bsz, nh, nc, 1, 128), F32)],
        scratch_shapes=[pltpu.VMEM((nh, dk, dv), F32), pltpu.VMEM((nh, 1, dk), F32), pltpu.VMEM((nh, 1, 128), F32)],
        compiler_params=_cp(("parallel", "arbitrary")),
    )(qk3, qk3, p3, p3)


def _mlstm_bwd(dp3, qk3, p3, dh3, cs, ns, ms):
    bsz, lp, _ = p3.shape
    lc = M_CHUNK
    nc = lp // lc
    dk, dv = 128, 256
    scale = dk ** -0.5

    cps = _chunks_per_step(nc)
    nst = nc // cps
    rows = cps * lc

    def body(dp_any, q_ref, k_ref, v_ref, g_ref, dh_ref, cs_ref, ns_ref, ms_ref,
             dv_ref, dqk_ref, dg_ref, dc_sc, dn_sc):
        t = pl.program_id(1)
        st = nst - 1 - t

        @pl.when(t == 0)
        def _():
            dc_sc[...] = jnp.zeros_like(dc_sc)
            dn_sc[...] = jnp.zeros_like(dn_sc)

        lane = lax.broadcasted_iota(jnp.int32, (lc, 128), 1)
        for j in reversed(range(cps)):
            rs = slice(j * lc, (j + 1) * lc)
            g = g_ref[rs, :]
            dgate = jnp.zeros((lc, 128), F32)
            for hh in range(M_HEADS):
                dgate = head(hh, j, rs, st * cps + j, g, lane, dgate, q_ref, k_ref, v_ref, dh_ref, cs_ref, ns_ref,
                             ms_ref, dv_ref, dqk_ref, dc_sc, dn_sc)
            dg_ref[rs, :] = dgate.astype(dg_ref.dtype)

    def head(hh, j, sl, c, g, lane, dgate, q_ref, k_ref, v_ref, dh_ref, cs_ref, ns_ref, ms_ref, dv_ref, dqk_ref,
             dc_sc, dn_sc):
        c_st, n_st = cs_ref[hh, j], ns_ref[hh, j]
        m_st = ms_ref[hh, j][:, 0:1]
        q = q_ref[sl, hh * dk:(hh + 1) * dk]
        ks = k_ref[sl, hh * dk:(hh + 1) * dk] * scale
        v = v_ref[sl, hh * dv:(hh + 1) * dv]
        dh = dh_ref[sl, hh * dv:(hh + 1) * dv]
        gq = _mlstm_gates(g, hh, c, lc)
        f = _mlstm_chunk(q, ks, v, gq, c_st, n_st, m_st, lc)
        eye, r2, c2, row, valid = gq["eye"], gq["r2"], gq["c2"], gq["row"], gq["valid"]
        w_intra, w_inter, s, nn, den = f["w_intra"], f["w_inter"], f["s"], f["nn"], f["den"]
        qb, kb, vb, cb, w_k, decay = f["qb"], f["kb"], f["vb"], f["cb"], f["w_k"], f["decay"]
        d_c, d_n = dc_sc[hh], dn_sc[hh]
        d_cb = _bf(d_c)

        hout = f["num"] / nn
        dnum = dh / nn
        d_nn = -jnp.sum(dh * hout, axis=1, keepdims=True) / nn
        dden = jnp.where(jnp.abs(den) > f["e"], d_nn * jnp.sign(den), 0.0)
        wdnum = w_inter * dnum
        wdden = w_inter * dden
        ds = _dot(_bf(dnum), vb, NT) + dden
        dsw = _bf(ds * w_intra)
        dq = _dot(dsw, kb) + _dot(_bf(wdnum), cb, NT) + wdden * n_st
        dkw = _dot(vb, d_cb, NT) + d_n
        dks = _dot(dsw, qb, TN) + dkw * w_k
        kw = ks * w_k
        dvv = _dot(_bf(s), _bf(dnum), TN) + _dot(_bf(kw), d_cb)
        dd = ds * s
        rs = jnp.sum(dd, axis=1, keepdims=True)
        cs_col = jnp.sum(jnp.where(eye, jnp.sum(dd, axis=0, keepdims=True), 0.0), axis=1, keepdims=True)
        dwi = jnp.sum(dnum * f["qc"], axis=1, keepdims=True) + dden * f["qn"]
        db = rs - cs_col + dwi * w_inter
        dli = cs_col
        ddecay = jnp.sum(jnp.sum(d_c * c_st, axis=1, keepdims=True), axis=0, keepdims=True) \
            + jnp.sum(d_n * n_st, axis=1, keepdims=True)
        dgl = jnp.sum(dkw * ks, axis=1, keepdims=True) * w_k
        dblast = ddecay * decay + jnp.sum(dgl, axis=0, keepdims=True)
        db = db - dgl + jnp.where(row == lc - 1, dblast, 0.0)
        dli = dli + dgl
        db_row = gq["to_row"](db)
        dlf = jnp.sum(jnp.where(c2 >= r2, db_row, 0.0), axis=1, keepdims=True)
        dlf = jnp.where(valid, dlf, 0.0)
        dgate = jnp.where(lane == hh, jnp.where(valid, dli, 0.0), dgate)
        dgate = jnp.where(lane == M_HEADS + hh, dlf / (1.0 + jnp.exp(gq["f_col"])), dgate)
        dqk_ref[sl, hh * dk:(hh + 1) * dk] = dq
        dqk_ref[sl, (M_HEADS + hh) * dk:(M_HEADS + hh + 1) * dk] = dks * scale
        dv_ref[sl, hh * dv:(hh + 1) * dv] = dvv.astype(dv_ref.dtype)
        dc_sc[hh] = decay * d_c + _dot(qb, _bf(wdnum), TN)
        dn_sc[hh] = decay * d_n + _colsum(q * wdden)
        return dgate

    sd = jax.ShapeDtypeStruct
    nh = M_HEADS
    rc = lambda c: nst - 1 - c
    return _pcall(
        body, name="mlstm_bwd", grid=(bsz, nst),
        in_specs=[pl.BlockSpec(memory_space=pl.ANY),
                  pl.BlockSpec((None, rows, nh * dk), lambda b, c: (b, rc(c), 0)),
                  pl.BlockSpec((None, rows, nh * dk), lambda b, c: (b, rc(c), 1)),
                  pl.BlockSpec((None, rows, nh * dv), lambda b, c: (b, rc(c), V_OFF // (nh * dv))),
                  pl.BlockSpec((None, rows, 128), lambda b, c: (b, rc(c), G_OFF // 128)),
                  pl.BlockSpec((None, rows, nh * dv), lambda b, c: (b, rc(c), 0)),
                  pl.BlockSpec((None, nh, cps, dk, dv), lambda b, c: (b, 0, rc(c), 0, 0)),
                  pl.BlockSpec((None, nh, cps, 1, dk), lambda b, c: (b, 0, rc(c), 0, 0)),
                  pl.BlockSpec((None, nh, cps, 1, 128), lambda b, c: (b, 0, rc(c), 0, 0))],
        out_specs=[pl.BlockSpec((None, rows, nh * dv), lambda b, c: (b, rc(c), V_OFF // (nh * dv))),
                   pl.BlockSpec((None, rows, 2 * nh * dk), lambda b, c: (b, rc(c), 0)),
                   pl.BlockSpec((None, rows, 128), lambda b, c: (b, rc(c), 0))],
        out_shape=[sd(dp3.shape, dp3.dtype), sd((bsz, lp, 2 * nh * dk), F32), sd((bsz, lp, 128), dp3.dtype)],
        scratch_shapes=[pltpu.VMEM((nh, dk, dv), F32), pltpu.VMEM((nh, 1, dk), F32)],
        input_output_aliases={0: 0},
        compiler_params=_cp(("arbitrary", "arbitrary")),
    )(dp3, qk3, qk3, p3, p3, dh3, cs, ns, ms)


def _headnorm(x):
    dv = x.shape[1] // M_HEADS
    xh, rs = [], []
    for h in range(M_HEADS):
        xx = x[:, h * dv:(h + 1) * dv]
        mu = jnp.mean(xx, axis=-1, keepdims=True)
        xc = xx - mu
        rstd = lax.rsqrt(jnp.mean(xc * xc, axis=-1, keepdims=True) + LN_EPS)
        xh.append(xc * rstd)
        rs.append(rstd)
    return jnp.concatenate(xh, axis=1), rs


def _mix_fwd(hm, p, ys5g, h0, gn, wmo_bf, wo_bf, g1, b1, lp):
    r, d = hm.shape
    tm = _row_tile(lp, 208)

    def body(hm_ref, o_ref, gs_ref, gm_ref, ys_ref, h0_ref, gn_ref, wmo_ref, wo_ref, g1_ref, b1_ref,
             ymin_ref, ym_ref, mix_ref, r1_ref, h1_ref):
        xhat, _ = _headnorm(hm_ref[...])
        ymin = _bf(_sig(o_ref[...]) * (xhat * gn_ref[...]))
        ymin_ref[...] = ymin
        ym = _dot(ymin, wmo_ref[...])
        ym_ref[...] = ym
        mix = _bf(_sig(gs_ref[...]) * ys_ref[...] + _sig(gm_ref[...]) * ym)
        mix_ref[...] = mix
        r1 = ALPHA * h0_ref[...] + _dot(mix, wo_ref[...])
        r1_ref[...] = r1
        h1, _, _ = _ln_fwd(r1, g1_ref[...], b1_ref[...])
        h1_ref[...] = h1

    sd = jax.ShapeDtypeStruct
    row = pl.BlockSpec((tm, d), lambda i: (i, 0))
    return _pcall(
        body, name="mix_fwd", grid=(r // tm,),
        in_specs=[row, pl.BlockSpec((tm, d), lambda i: (i, O_OFF // d)), pl.BlockSpec((tm, d), lambda i: (i, GS_OFF // d)),
                  pl.BlockSpec((tm, d), lambda i: (i, GM_OFF // d)), row, row, _const((1, d)),
                  _resident((d, d)), _resident((d, d)), _const((1, d)), _const((1, d))],
        out_specs=[row] * 5,
        out_shape=[sd((r, d), BF16), sd((r, d), F32), sd((r, d), BF16), sd((r, d), F32), sd((r, d), F32)],
        compiler_params=_cp(("parallel",), 48),
    )(hm, p, p, p, ys5g, h0, gn, wmo_bf, wo_bf, g1, b1)


def _mix_bwd(dh1, r1, g1, wo_bf, wmo_bf, p, ys5g, ym, hm, gn, lp):
    r, d = hm.shape
    tm = _row_tile(lp, 208)
    dv = d // M_HEADS

    def body(dh1_ref, r1_ref, g1_ref, wo_ref, wmo_ref, o_ref, gs_ref, gm_ref, ys_ref, ym_ref, hm_ref, gn_ref,
             dr1_ref, dp_ref, dys_ref, dym_ref, dhm_ref, dg1_ref, db1_ref, dgn_ref):
        i = pl.program_id(0)

        @pl.when(i == 0)
        def _():
            dg1_ref[...] = jnp.zeros_like(dg1_ref)
            db1_ref[...] = jnp.zeros_like(db1_ref)
            dgn_ref[...] = jnp.zeros_like(dgn_ref)

        dh1 = dh1_ref[...]
        _, xhat1, rstd1 = _ln_fwd(r1_ref[...], g1_ref[...], 0.0)
        dr1 = _ln_bwd(dh1, xhat1, rstd1, g1_ref[...])
        dr1_ref[...] = dr1
        dg1_ref[...] += _colsum(dh1 * xhat1)
        db1_ref[...] += _colsum(dh1)
        dmix = _dot(_bf(dr1), wo_ref[...], NT)
        sgs, sgm, so = _sig(gs_ref[...]), _sig(gm_ref[...]), _sig(o_ref[...])
        dys_ref[...] = dmix * sgs
        dp_ref[:, d:2 * d] = _bf(dmix * ys_ref[...] * sgs * (1.0 - sgs))
        dym = dmix * sgm
        dym_ref[...] = _bf(dym)
        dp_ref[:, 2 * d:3 * d] = _bf(dmix * ym_ref[...] * sgm * (1.0 - sgm))
        dymin = _dot(_bf(dym), wmo_ref[...], NT)
        xhat, rs = _headnorm(hm_ref[...])
        gn_ = gn_ref[...]
        dp_ref[:, 0:d] = _bf(dymin * (xhat * gn_) * so * (1.0 - so))
        dhn = dymin * so
        dgn_ref[...] += _colsum(dhn * xhat)
        dxh = dhn * gn_
        for h in range(M_HEADS):
            sl = slice(h * dv, (h + 1) * dv)
            a, xh = dxh[:, sl], xhat[:, sl]
            m1 = jnp.mean(a, axis=-1, keepdims=True)
            m2 = jnp.mean(a * xh, axis=-1, keepdims=True)
            dhm_ref[:, sl] = rs[h] * (a - m1 - xh * m2)

    sd = jax.ShapeDtypeStruct
    row = pl.BlockSpec((tm, d), lambda i: (i, 0))
    vec = _const((1, d))
    return _pcall(
        body, name="mix_bwd", grid=(r // tm,),
        in_specs=[row, row, vec, _resident((d, d)), _resident((d, d)),
                  pl.BlockSpec((tm, d), lambda i: (i, O_OFF // d)), pl.BlockSpec((tm, d), lambda i: (i, GS_OFF // d)),
                  pl.BlockSpec((tm, d), lambda i: (i, GM_OFF // d)), row, row, row, vec],
        out_specs=[row, pl.BlockSpec((tm, 3 * d), lambda i: (i, 0)), row, row, row, vec, vec, vec],
        out_shape=[sd((r, d), F32), sd((r, NP), BF16), sd((r, d), F32), sd((r, d), BF16), sd((r, d), F32),
                   sd((1, d), F32), sd((1, d), F32), sd((1, d), F32)],
        compiler_params=_cp(("arbitrary",), 48),
    )(dh1, r1, g1, wo_bf, wmo_bf, p, p, p, ys5g, ym, hm, gn)


def _mlp_fwd(h1, tgt, wup_g, wdn_bf, bup, g2, b2, lp):
    r, d = h1.shape
    tm = _row_tile(lp, 352)
    tps = lp // tm
    nf = wup_g.shape[0]

    def body(h1_ref, t_ref, wup_ref, wdn_ref, bup_ref, g2_ref, b2_ref, dr2_ref, act_ref, loss_ref, dg2_ref, db2_ref):
        i = pl.program_id(0)

        @pl.when(i == 0)
        def _():
            loss_ref[...] = jnp.zeros_like(loss_ref)
            dg2_ref[...] = jnp.zeros_like(dg2_ref)
            db2_ref[...] = jnp.zeros_like(db2_ref)

        h1 = h1_ref[...]
        h1b = _bf(h1)
        ff = jnp.zeros((tm, d), F32)
        for s in range(nf):
            up = _dot(h1b, wup_ref[s]) + bup_ref[:, s * d:(s + 1) * d]
            a = jnp.maximum(up, 0.0)
            a = _bf(a * a)
            act_ref[:, s * d:(s + 1) * d] = a
            ff = ff + _dot(a, wdn_ref[s * d:(s + 1) * d, :])
        r2 = ALPHA * h1 + ff
        g2 = g2_ref[...]
        y, xhat, rstd = _ln_fwd(r2, g2, b2_ref[...])
        t = (i % tps) * tm + lax.broadcasted_iota(jnp.int32, (tm, 1), 0)
        diff = jnp.where(t >= PAD + N_META, y - t_ref[...], 0.0)
        loss_ref[...] += 0.5 / d * jnp.sum(jnp.sum(diff * diff, axis=1, keepdims=True), axis=0, keepdims=True)
        dy = diff * (1.0 / d)
        dg2_ref[...] += _colsum(dy * xhat)
        db2_ref[...] += _colsum(dy)
        dr2_ref[...] = _ln_bwd(dy, xhat, rstd, g2)

    sd = jax.ShapeDtypeStruct
    row = pl.BlockSpec((tm, d), lambda i: (i, 0))
    vec = _const((1, d))
    return _pcall(
        body, name="mlp_fwd", grid=(r // tm,),
        in_specs=[row, row, _resident(wup_g.shape), _resident(wdn_bf.shape), _const((1, nf * d)), vec, vec],
        out_specs=[row, pl.BlockSpec((tm, nf * d), lambda i: (i, 0)), _const((1, 128)), vec, vec],
        out_shape=[sd((r, d), F32), sd((r, nf * d), BF16), sd((1, 128), F32), sd((1, d), F32), sd((1, d), F32)],
        compiler_params=_cp(("arbitrary",), 56),
    )(h1, tgt, wup_g, wdn_bf, bup, g2, b2)


def _mlp_bwd(h1, dr2, wup_g, wdn_bf, bup, lp):
    r, d = h1.shape
    tm = _row_tile(lp, 352)
    nf = wup_g.shape[0]

    def body(h1_ref, dr2_ref, wup_ref, wdn_ref, bup_ref, dh1_ref, dup_ref, dbup_ref):
        i = pl.program_id(0)

        @pl.when(i == 0)
        def _():
            dbup_ref[...] = jnp.zeros_like(dbup_ref)

        h1b = _bf(h1_ref[...])
        dr2 = dr2_ref[...]
        dr2b = _bf(dr2)
        acc = ALPHA * dr2
        for s in range(nf):
            up = _dot(h1b, wup_ref[s]) + bup_ref[:, s * d:(s + 1) * d]
            dact = _dot(dr2b, wdn_ref[s * d:(s + 1) * d, :], NT)
            dup = dact * (2.0 * jnp.maximum(up, 0.0))
            dbup_ref[:, s * d:(s + 1) * d] += _colsum(dup)
            dupb = _bf(dup)
            dup_ref[:, s * d:(s + 1) * d] = dupb
            acc = acc + _dot(dupb, wup_ref[s], NT)
        dh1_ref[...] = acc

    sd = jax.ShapeDtypeStruct
    row = pl.BlockSpec((tm, d), lambda i: (i, 0))
    return _pcall(
        body, name="mlp_bwd", grid=(r // tm,),
        in_specs=[row, row, _resident(wup_g.shape), _resident(wdn_bf.shape), _const((1, nf * d))],
        out_specs=[row, pl.BlockSpec((tm, nf * d), lambda i: (i, 0)), _const((1, nf * d))],
        out_shape=[sd((r, d), F32), sd((r, nf * d), BF16), sd((1, nf * d), F32)],
        compiler_params=_cp(("arbitrary",), 56),
    )(h1, dr2, wup_g, wdn_bf, bup)


def _s5_block_mats(bb_re_t, bb_im_t, c_re, c_im, ap_re, ap_im):
    ng = c_re.shape[0]
    gl = ng // S5_KCH
    eye = jnp.eye(gl, dtype=F32)

    def bmat(bt):
        bb = jnp.transpose(bt, (1, 0, 2)).reshape(S5_KCH, gl, S5_GROUP, S5_STATE)
        return jnp.einsum("kghp,gj->kghjp", bb, eye).reshape(S5_KCH, gl * S5_GROUP, gl * S5_STATE)

    def cmat(c):
        cc = c.reshape(S5_KCH, gl, S5_GROUP, S5_STATE)
        return jnp.einsum("kghp,gj->kjpgh", cc, eye).reshape(S5_KCH, gl * S5_STATE, gl * S5_GROUP)

    def pw(a):
        return jnp.transpose(a.reshape(8, S5_KCH, gl * S5_STATE), (1, 0, 2))

    bk = jnp.concatenate([bmat(bb_re_t), bmat(bb_im_t)], axis=-1)
    apow = jnp.concatenate([pw(ap_re), pw(ap_im)], axis=-1)
    return _bf(bk), _bf(cmat(c_re)), _bf(cmat(c_im)), apow


def _s5_block_grads(dbk, dcre, dcim, da):
    gl = dbk.shape[1] // S5_GROUP
    ng = gl * S5_KCH
    eye = jnp.eye(gl, dtype=F32)
    hw = gl * S5_STATE

    def bpart(x):
        x = x.reshape(S5_KCH, gl, S5_GROUP, gl, S5_STATE)
        x = jnp.einsum("kghjp,gj->kghp", x, eye).reshape(ng, S5_GROUP, S5_STATE)
        return jnp.transpose(x, (1, 0, 2))

    def cpart(x):
        x = x.reshape(S5_KCH, gl, S5_STATE, gl, S5_GROUP)
        return jnp.einsum("kjpgh,gj->kghp", x, eye).reshape(ng, S5_GROUP, S5_STATE)

    return (bpart(dbk[..., :hw]), bpart(dbk[..., hw:]), cpart(dcre), cpart(dcim),
            da[:, 0, :hw].reshape(ng, S5_STATE), da[:, 0, hw:].reshape(ng, S5_STATE))


def _tie(a, tok):
    return a if tok is None else a + tok[0, 0]


def _local_step(x, tgt, w, early=None, late=None, ready=None):
    ready = ready or (lambda names, g: None)
    bsz, seq, d = x.shape
    lp = PAD + N_META + seq
    r = bsz * lp
    meta = jnp.broadcast_to(w["meta_tokens"][None], (bsz, N_META, d))
    hin = jnp.concatenate([jnp.zeros((bsz, PAD, d), F32), meta, x], axis=1).reshape(r, d)
    tgtp = jnp.concatenate([jnp.zeros((bsz, PAD + N_META, d), F32), tgt], axis=1).reshape(r, d)

    h0, h0b = _ln0_fwd(hin, w["ln0_g"], w["ln0_b"], lp)
    if early is not None:
        w = {**w, **early((h0, tgtp))}
    p = _inproj(h0b, w["w_in"], w["b_in"], lp)
    p3 = p.reshape(bsz, lp, NP)

    b_re_t = jnp.transpose(w["s5_b_re"], (2, 0, 1))
    b_im_t = jnp.transpose(w["s5_b_im"], (2, 0, 1))
    ap_re, ap_im, bb_re_t, bb_im_t = _s5_prep(w["s5_lambda_re"], w["s5_lambda_im"], w["s5_log_dt"], b_re_t, b_im_t)
    bk, cre, cim, apow = _s5_block_mats(bb_re_t, bb_im_t, w["s5_c_re"], w["s5_c_im"], ap_re, ap_im)
    y_s5, xs = _s5_fwd(p3, bk, cre, cim, apow, w["s5_d"])
    sw = y_s5.shape[-1]
    if late is not None:
        w = {**w, **late(y_s5)}
    gy, z, ys5g = _glu_fwd(y_s5.reshape(r, sw), w["s5_w_glu"], lp)

    pre3, qk3 = _conv_fwd(p3, w["qk_conv_w"], w["qk_conv_b"])
    hm3, cs, ns, ms = _mlstm_fwd(qk3, p3)
    hm = hm3.reshape(r, d)
    ymin, ym, mix, r1, h1 = _mix_fwd(hm, p, ys5g, h0, w["m_norm_g"], w["m_w_out"], w["w_o"], w["ln1_g"], w["ln1_b"], lp)
    dr2, act, loss, dg2, db2 = _mlp_fwd(h1, tgtp, w["w_up"], w["w_down"], w["b_up"], w["ln2_g"], w["ln2_b"], lp)

    g = {"ln2_g": dg2, "ln2_b": db2}
    dh1, dup, g["b_up"] = _mlp_bwd(h1, dr2, w["w_up"], w["w_down"], w["b_up"], lp)
    g["w_down"] = _mm_tn(act, dr2, name="dw_down")
    g["w_up"] = _mm_tn(h1, dup, name="dw_up", split=w["w_up"].shape[0])
    tok = ready(("w_down", "w_up"), g)
    dr1, dp, dys5g, dym, dhm, g["ln1_g"], g["ln1_b"], g["m_norm_g"] = _mix_bwd(
        dh1, r1, _tie(w["ln1_g"], tok), w["w_o"], w["m_w_out"], p, ys5g, ym, hm, w["m_norm_g"], lp)
    g["w_o"] = _mm_tn(mix, dr1, name="dw_o")
    g["m_w_out"] = _mm_tn(ymin, dym, name="dw_mout")

    dp3 = dp.reshape(bsz, lp, NP)
    dp3, dqk3, dgate = _mlstm_bwd(dp3, qk3, p3, dhm.reshape(bsz, lp, d), cs, ns, ms)
    dp3, g["qk_conv_w"], g["qk_conv_b"] = _conv_bwd(dp3, p3, dqk3, pre3, w["qk_conv_w"])
    dz, dys5 = _glu_bwd(dys5g, z, y_s5.reshape(r, sw), w["s5_w_glu"], lp)
    g["s5_w_glu"] = _mm_tn(gy, dz, name="dw_glu", split=w["s5_w_glu"].shape[0])
    tok = ready(("s5_w_glu", "m_w_out", "w_o"), g)
    apow_rev = jnp.flip(apow, axis=1)
    dp3, dbk, dcre, dcim, da, g["s5_d"] = _s5_bwd(dp3, p3, dys5.reshape(bsz, lp, sw), xs, bk, cre, cim, apow_rev,
                                                 _tie(w["s5_d"], tok))
    dbb_re_t, dbb_im_t, g["s5_c_re"], g["s5_c_im"], da_re, da_im = _s5_block_grads(dbk, dcre, dcim, da)
    g["s5_lambda_re"], g["s5_lambda_im"], g["s5_log_dt"], gb_re_t, gb_im_t = _s5_prep_bwd(
        w["s5_lambda_re"], w["s5_lambda_im"], w["s5_log_dt"], b_re_t, b_im_t, da_re, da_im, dbb_re_t, dbb_im_t)
    g["s5_b_re"] = jnp.transpose(gb_re_t, (1, 2, 0))
    g["s5_b_im"] = jnp.transpose(gb_im_t, (1, 2, 0))

    dp3 = lax.dynamic_update_slice(dp3, dgate, (0, 0, G_OFF))
    dp = dp3.reshape(r, NP)
    g["w_in"], g["b_in"] = _mm_tn(h0b, dp, name="dw_in", colsum=True)
    tok = ready(("w_in",), g)
    dpw = _mm_nt(dp, w["w_in"], lp, name="dh0", dep=tok)
    dhin, g["ln0_g"], g["ln0_b"], g["meta_tokens"] = _ln0_bwd(hin, dr1, dpw, w["ln0_g"], lp)
    grad_x = dhin.reshape(bsz, lp, d)[:, PAD + N_META:]
    return loss, grad_x, g


_ANY = pl.BlockSpec(memory_space=pl.ANY)
_MESH = pl.DeviceIdType.MESH


def _place():
    return lax.axis_index("x"), lax.axis_index("y"), lax.axis_index("c")


def _gather_chips(shards):
    n = len(shards)

    def body(*refs):
        ins, outs = refs[:n], refs[n:2 * n]
        send, recv, loc = refs[2 * n:]
        x, y, c = _place()
        me = 2 * x + y
        peers = [(1 - x, y), (x, 1 - y), (1 - x, 1 - y)]

        def rc(a, k, slot):
            px, py = peers[k]
            return pltpu.make_async_remote_copy(src_ref=ins[a], dst_ref=outs[a].at[slot], send_sem=send.at[a, k],
                                                recv_sem=recv.at[a, k], device_id=(px, py, c), device_id_type=_MESH)

        own = [pltpu.make_async_copy(ins[a], outs[a].at[me], loc.at[a]) for a in range(n)]
        for cp in own:
            cp.start()
        out = [rc(a, k, me) for a in range(n) for k in range(3)]
        for cp in out:
            cp.start()
        for a in range(n):
            for k in range(3):
                rc(a, k, 2 * peers[k][0] + peers[k][1]).wait_recv()
        for cp in out:
            cp.wait_send()
        for cp in own:
            cp.wait()

    return _pcall(
        body, name="gather_chips", in_specs=[_ANY] * n, out_specs=[_ANY] * n,
        out_shape=[jax.ShapeDtypeStruct((4,) + s.shape, s.dtype) for s in shards],
        scratch_shapes=[pltpu.SemaphoreType.DMA((n, 3)), pltpu.SemaphoreType.DMA((n, 3)), pltpu.SemaphoreType.DMA((n,))],
    )(*shards)


_HBM = pl.BlockSpec(memory_space=pltpu.HBM)
_SEM = pl.BlockSpec(memory_space=pltpu.SEMAPHORE)
_EFFECT = pltpu.SideEffectType.DATAFLOW_SIDE_EFFECTING


def _xchg_copies(srcs, lands, send, recv, scatter):
    x, y, c = _place()
    me = 2 * x + y
    peers = [(1 - x, y), (x, 1 - y), (1 - x, 1 - y)]
    out = []
    for a in range(len(srcs)):
        for k, (px, py) in enumerate(peers):
            src = srcs[a].at[2 * px + py] if scatter else srcs[a]
            dst = lands[a].at[k] if scatter else lands[a].at[me]
            out.append(pltpu.make_async_remote_copy(src_ref=src, dst_ref=dst, send_sem=send.at[3 * a + k],
                                                    recv_sem=recv.at[3 * a + k], device_id=(px, py, c),
                                                    device_id_type=_MESH))
    return out


def _xchg_start(srcs, lands, *, name, scatter, dep=None):
    n = len(srcs)
    deps = [] if dep is None else [dep]
    nd = len(deps)

    def body(*refs):
        send, recv = refs[2 * n + nd], refs[2 * n + nd + 1]
        for cp in _xchg_copies(refs[:n], refs[n:2 * n], send, recv, scatter):
            cp.start()
        refs[-1][...] = jnp.zeros_like(refs[-1])

    hbm = lambda a: pltpu.HBM(a.shape, a.dtype)
    con = lambda a: pltpu.with_memory_space_constraint(a, pltpu.HBM)
    res = _pcall(
        body, name=name, in_specs=[_HBM] * (2 * n) + [_ANY] * nd,
        out_specs=[_SEM, _SEM] + [_HBM] * (2 * n) + [pl.BlockSpec(memory_space=pltpu.VMEM)],
        out_shape=[pltpu.SemaphoreType.DMA((3 * n,)), pltpu.SemaphoreType.DMA((3 * n,))]
        + [hbm(a) for a in srcs] + [hbm(a) for a in lands] + [jax.ShapeDtypeStruct((8, 128), F32)],
        input_output_aliases={i: 2 + i for i in range(2 * n)},
        compiler_params=pltpu.CompilerParams(has_side_effects=_EFFECT),
    )(*[con(a) for a in srcs], *[con(a) for a in lands], *deps)
    return res[0], res[1], list(res[2:2 + n]), list(res[2 + n:2 + 2 * n]), res[-1]


def _xchg_wait(send, recv, srcs, lands, after, *, name, scatter):
    n = len(srcs)
    afters = list(after) if isinstance(after, (list, tuple)) else [after]

    def body(*refs):
        s_ref, r_ref = refs[2 * n], refs[2 * n + 1]
        for cp in _xchg_copies(refs[:n], refs[n:2 * n], s_ref, r_ref, scatter):
            cp.wait_send()
            cp.wait_recv()

    hbm = lambda a: pltpu.HBM(a.shape, a.dtype)
    res = _pcall(
        body, name=name, in_specs=[_HBM] * (2 * n) + [_SEM, _SEM] + [_ANY] * len(afters),
        out_specs=[_HBM] * (2 * n),
        out_shape=[hbm(a) for a in srcs] + [hbm(a) for a in lands],
        input_output_aliases={i: i for i in range(2 * n)},
        compiler_params=pltpu.CompilerParams(has_side_effects=_EFFECT),
    )(*srcs, *lands, send, recv, *afters)
    return list(res[:n]), list(res[n:])


def _swap_cores(arrs, name="swap_cores"):
    n = len(arrs)

    def body(*refs):
        ins, outs = refs[:n], refs[n:2 * n]
        send, recv = refs[2 * n:]
        x, y, c = _place()
        cps = [pltpu.make_async_remote_copy(src_ref=ins[a], dst_ref=outs[a], send_sem=send.at[a], recv_sem=recv.at[a],
                                            device_id=(x, y, 1 - c), device_id_type=_MESH) for a in range(n)]
        for cp in cps:
            cp.start()
        for cp in cps:
            cp.wait_recv()
        for cp in cps:
            cp.wait_send()

    return _pcall(
        body, name=name, in_specs=[_ANY] * n, out_specs=[_ANY] * n,
        out_shape=[jax.ShapeDtypeStruct(s.shape, s.dtype) for s in arrs],
        scratch_shapes=[pltpu.SemaphoreType.DMA((n,)), pltpu.SemaphoreType.DMA((n,))],
    )(*arrs)


def _allreduce_small(v, dep=None):
    rows = v.shape[0]
    half = rows // 2
    assert half % 8 == 0 and 2 * half == rows
    deps = [] if dep is None else [dep]

    def body(v_ref, *rest):
        out_ref, sib_ref, pair_ref, slots_ref, send, recv = rest[len(deps):]
        x, y, c = _place()
        chip = 2 * x + y
        sibling = (x, y, 1 - c)
        peers = [(1 - x, y), (x, 1 - y), (1 - x, 1 - y)]
        mine = pl.ds(pl.multiple_of(c * half, 8), half)

        first = pltpu.make_async_remote_copy(src_ref=v_ref, dst_ref=sib_ref, send_sem=send.at[0], recv_sem=recv.at[0],
                                             device_id=sibling, device_id_type=_MESH)
        first.start()
        first.wait_recv()
        pair_ref[...] = v_ref[...] + sib_ref[...]
        slots_ref[chip] = pair_ref[mine, :]
        cross = [pltpu.make_async_remote_copy(src_ref=pair_ref.at[mine], dst_ref=slots_ref.at[chip],
                                              send_sem=send.at[1 + k], recv_sem=recv.at[1 + k],
                                              device_id=(px, py, c), device_id_type=_MESH)
                 for k, (px, py) in enumerate(peers)]
        for cp in cross:
            cp.start()
        for cp in cross:
            cp.wait_recv()
        out_ref[mine, :] = ((slots_ref[0] + slots_ref[1]) + slots_ref[2]) + slots_ref[3]
        last = pltpu.make_async_remote_copy(src_ref=out_ref.at[mine], dst_ref=out_ref.at[mine], send_sem=send.at[4],
                                            recv_sem=recv.at[4], device_id=sibling, device_id_type=_MESH)
        last.start()
        last.wait_recv()
        first.wait_send()
        for cp in cross:
            cp.wait_send()
        last.wait_send()

    vm = pl.BlockSpec(memory_space=pltpu.VMEM)
    return _pcall(
        body, name="allreduce_small", in_specs=[vm] + [_ANY] * len(deps), out_specs=vm,
        out_shape=jax.ShapeDtypeStruct((rows, 128), F32),
        scratch_shapes=[pltpu.VMEM((rows, 128), F32), pltpu.VMEM((rows, 128), F32), pltpu.VMEM((4, half, 128), F32),
                        pltpu.SemaphoreType.DMA((5,)), pltpu.SemaphoreType.DMA((5,))],
        compiler_params=_cp(None, 40),
    )(v, *deps)


def _sum_slots(own, land):
    ns, rows, cols = land.shape
    tm = _row_tile(rows, 256, 8)

    def body(own_ref, a_ref, o_ref):
        o_ref[...] = ((own_ref[...] + a_ref[0]) + a_ref[1]) + a_ref[2]

    return _pcall(
        body, name="sum_slots", grid=(rows // tm,),
        in_specs=[pl.BlockSpec((tm, cols), lambda i: (i, 0)), pl.BlockSpec((ns, tm, cols), lambda i: (0, i, 0))],
        out_specs=pl.BlockSpec((tm, cols), lambda i: (i, 0)),
        out_shape=jax.ShapeDtypeStruct((rows, cols), F32),
        compiler_params=_cp(("parallel",), 40),
    )(own, land)


def _adamw(w, m, v, g0, g1=None):
    rows, cols = w.shape[-2:]
    lead = w.ndim == 3
    tm = _row_tile(rows, max(8, (1 << 20) // (4 * cols)), 8)
    c1 = 1.0 - ADAM_B1 ** ADAM_STEP
    c2 = 1.0 - ADAM_B2 ** ADAM_STEP
    two = g1 is not None

    def body(*refs):
        w_ref, m_ref, v_ref, g0_ref = refs[:4]
        g_ref, d_ref, nm_ref, nv_ref = refs[-4:]
        g = g0_ref[...]
        if two:
            g = g + refs[4][...]
        nm = ADAM_B1 * m_ref[...] + (1.0 - ADAM_B1) * g
        nv = ADAM_B2 * v_ref[...] + (1.0 - ADAM_B2) * (g * g)
        g_ref[...] = g
        nm_ref[...] = nm
        nv_ref[...] = nv
        d_ref[...] = -ADAM_LR * ((nm / c1) / (jnp.sqrt(nv / c2) + ADAM_EPS) + ADAM_WD * w_ref[...])

    blk = pl.BlockSpec((tm, cols), lambda i: (i, 0))
    wblk = pl.BlockSpec((None, tm, cols), lambda i: (0, i, 0)) if lead else blk
    ins = [w, m, v, g0] + ([g1] if two else [])
    return _pcall(
        body, name="adamw", grid=(rows // tm,), in_specs=[wblk] * 3 + [blk] * (len(ins) - 3), out_specs=[wblk] * 4,
        out_shape=[jax.ShapeDtypeStruct(w.shape, F32)] * 4,
        compiler_params=_cp(("parallel",), 40),
    )(*ins)


_BIG = ("w_in", "s5_w_glu", "m_w_out", "w_o", "w_up", "w_down")
_SMALL = ("ln0_g", "ln0_b", "b_in", "qk_conv_b", "s5_lambda_re", "s5_lambda_im", "s5_log_dt", "s5_b_re", "s5_b_im",
          "s5_c_re", "s5_c_im", "s5_d", "m_norm_g", "ln1_g", "ln1_b", "b_up", "ln2_g", "ln2_b")
_SMALL_SHARDED = ("meta_tokens", "qk_conv_w")
_ORDER = ("meta_tokens", "ln0_g", "ln0_b", "w_in", "b_in", "qk_conv_w", "qk_conv_b", "s5_lambda_re", "s5_lambda_im",
          "s5_log_dt", "s5_b_re", "s5_b_im", "s5_c_re", "s5_c_im", "s5_d", "s5_w_glu", "m_norm_g", "m_w_out", "w_o",
          "ln1_g", "ln1_b", "w_up", "b_up", "w_down", "ln2_g", "ln2_b")


def _pack(arrs):
    flat = jnp.concatenate([a.reshape(-1) for a in arrs])
    n = flat.shape[0]
    rows = -(-n // 2048) * 16
    return jnp.pad(flat, (0, rows * 128 - n)).reshape(rows, 128)


def _unpack(packed, shapes):
    flat = packed.reshape(-1)
    out, off = [], 0
    for s in shapes:
        n = math.prod(s)
        out.append(flat[off:off + n].reshape(s))
        off += n
    return out


def kernel(x, meta_tokens, ln0_g, ln0_b, w_in, b_in, qk_conv_w, qk_conv_b, s5_lambda_re, s5_lambda_im, s5_log_dt, s5_b_re, s5_b_im, s5_c_re, s5_c_im, s5_d, s5_w_glu, m_norm_g, m_w_out, w_o, ln1_g, ln1_b, w_up, b_up, w_down, ln2_g, ln2_b, loss_target, m_meta_tokens, m_ln0_g, m_ln0_b, m_w_in, m_b_in, m_qk_conv_w, m_qk_conv_b, m_s5_lambda_re, m_s5_lambda_im, m_s5_log_dt, m_s5_b_re, m_s5_b_im, m_s5_c_re, m_s5_c_im, m_s5_d, m_s5_w_glu, m_m_norm_g, m_m_w_out, m_w_o, m_ln1_g, m_ln1_b, m_w_up, m_b_up, m_w_down, m_ln2_g, m_ln2_b, v_meta_tokens, v_ln0_g, v_ln0_b, v_w_in, v_b_in, v_qk_conv_w, v_qk_conv_b, v_s5_lambda_re, v_s5_lambda_im, v_s5_log_dt, v_s5_b_re, v_s5_b_im, v_s5_c_re, v_s5_c_im, v_s5_d, v_s5_w_glu, v_m_norm_g, v_m_w_out, v_w_o, v_ln1_g, v_ln1_b, v_w_up, v_b_up, v_w_down, v_ln2_g, v_ln2_b):
    wts = dict(meta_tokens=meta_tokens, ln0_g=ln0_g, ln0_b=ln0_b, w_in=w_in, b_in=b_in, qk_conv_w=qk_conv_w,
               qk_conv_b=qk_conv_b, s5_lambda_re=s5_lambda_re, s5_lambda_im=s5_lambda_im, s5_log_dt=s5_log_dt,
               s5_b_re=s5_b_re, s5_b_im=s5_b_im, s5_c_re=s5_c_re, s5_c_im=s5_c_im, s5_d=s5_d, s5_w_glu=s5_w_glu,
               m_norm_g=m_norm_g, m_w_out=m_w_out, w_o=w_o, ln1_g=ln1_g, ln1_b=ln1_b, w_up=w_up, b_up=b_up,
               w_down=w_down, ln2_g=ln2_g, ln2_b=ln2_b)
    mom = dict(meta_tokens=m_meta_tokens, ln0_g=m_ln0_g, ln0_b=m_ln0_b, w_in=m_w_in, b_in=m_b_in, qk_conv_w=m_qk_conv_w,
               qk_conv_b=m_qk_conv_b, s5_lambda_re=m_s5_lambda_re, s5_lambda_im=m_s5_lambda_im, s5_log_dt=m_s5_log_dt,
               s5_b_re=m_s5_b_re, s5_b_im=m_s5_b_im, s5_c_re=m_s5_c_re, s5_c_im=m_s5_c_im, s5_d=m_s5_d,
               s5_w_glu=m_s5_w_glu, m_norm_g=m_m_norm_g, m_w_out=m_m_w_out, w_o=m_w_o, ln1_g=m_ln1_g, ln1_b=m_ln1_b,
               w_up=m_w_up, b_up=m_b_up, w_down=m_w_down, ln2_g=m_ln2_g, ln2_b=m_ln2_b)
    var = dict(meta_tokens=v_meta_tokens, ln0_g=v_ln0_g, ln0_b=v_ln0_b, w_in=v_w_in, b_in=v_b_in, qk_conv_w=v_qk_conv_w,
               qk_conv_b=v_qk_conv_b, s5_lambda_re=v_s5_lambda_re, s5_lambda_im=v_s5_lambda_im, s5_log_dt=v_s5_log_dt,
               s5_b_re=v_s5_b_re, s5_b_im=v_s5_b_im, s5_c_re=v_s5_c_re, s5_c_im=v_s5_c_im, s5_d=v_s5_d,
               s5_w_glu=v_s5_w_glu, m_norm_g=v_m_norm_g, m_w_out=v_m_w_out, w_o=v_w_o, ln1_g=v_ln1_g, ln1_b=v_ln1_b,
               w_up=v_w_up, b_up=v_b_up, w_down=v_w_down, ln2_g=v_ln2_g, ln2_b=v_ln2_b)
    d = x.shape[-1]
    chip = 2 * lax.axis_index("x") + lax.axis_index("y")

    gw = dict(zip(_SMALL_SHARDED, _gather_chips([meta_tokens, qk_conv_w[0]])))
    own_w_in = _bf(w_in[0])
    fsend, frecv, fsrc, fland, ftok = _xchg_start([own_w_in], [lax.empty((4,) + own_w_in.shape, BF16)],
                                                  name="gather_w_in_start", scatter=False, dep=gw["qk_conv_w"])
    late_names = tuple(n for n in _BIG if n != "w_in")
    cat = lambda a: jnp.transpose(a, (1, 0, 2)).reshape(a.shape[1], 4 * a.shape[2])
    w = dict(
        meta_tokens=cat(gw["meta_tokens"]), ln0_g=ln0_g[None], ln0_b=_tie(ln0_b[None], ftok),
        qk_conv_w=cat(gw["qk_conv_w"]), qk_conv_b=qk_conv_b,
        s5_lambda_re=s5_lambda_re[0], s5_lambda_im=s5_lambda_im[0], s5_log_dt=s5_log_dt[0][:, None],
        s5_b_re=s5_b_re[0], s5_b_im=s5_b_im[0], s5_c_re=s5_c_re[0], s5_c_im=s5_c_im[0], s5_d=s5_d,
        m_norm_g=m_norm_g, ln1_g=ln1_g, ln1_b=ln1_b, b_up=b_up, ln2_g=ln2_g, ln2_b=ln2_b)
    in_flight = {}

    def place_own(src, land):
        return lax.dynamic_update_slice(land, src[None], (chip,) + (0,) * src.ndim)

    def early(after):
        src, land = _xchg_wait(fsend, frecv, fsrc, fland, after, name="gather_w_in_wait", scatter=False)
        late_src = [_bf(wts[n][0]) for n in late_names]
        st = _xchg_start(late_src, [lax.empty((4,) + a.shape, a.dtype) for a in late_src], name="gather_late_start",
                         scatter=False, dep=src[0])
        in_flight["late"] = st[:4]
        return dict(w_in=_w_in_from_slots(place_own(src[0], land[0]), IN_CHUNK), b_in=_tie(_to_pad_cols(b_in), st[4]))

    def late(after):
        src, land = _xchg_wait(*in_flight["late"], after, name="gather_late_wait", scatter=False)
        full = {n: place_own(s, ld) for n, s, ld in zip(late_names, src, land)}
        return dict(s5_w_glu=full["s5_w_glu"], m_w_out=full["m_w_out"].reshape(d, d), w_o=full["w_o"].reshape(d, d),
                    w_up=full["w_up"], w_down=full["w_down"].reshape(4 * d, d))

    flying = []

    def ready(names, g):
        parts = dict(
            w_in=lambda: _slots_from_w_in(g["w_in"][0]), s5_w_glu=lambda: g["s5_w_glu"],
            m_w_out=lambda: g["m_w_out"].reshape(4, d // 4, d), w_o=lambda: g["w_o"].reshape(4, d // 4, d),
            w_up=lambda: g["w_up"], w_down=lambda: g["w_down"].reshape(4, d, d))
        src = [parts[n]() for n in names]
        land = [lax.empty((3,) + a.shape[1:], a.dtype) for a in src]
        st = _xchg_start(src, land, name="scatter_" + names[0] + "_start", scatter=True)
        flying.append((names,) + st[:4])
        return st[4]

    loss, grad_x, g = _local_step(x, loss_target, w, early, late, ready)
    g["b_in"] = _from_pad_cols(g["b_in"])

    res = {}

    def flat(a):
        return jnp.swapaxes(a, -1, -2).reshape(a.shape[:-2] + (-1, 128))

    def unflat(y, shape):
        return jnp.swapaxes(y.reshape(shape[:-2] + (shape[-1], shape[-2])), -1, -2)

    def finish(groups, after, tag):
        mine = {}
        for names, send, recv, src, land in groups:
            src, land = _xchg_wait(send, recv, src, land, after, name="scatter_" + names[0] + "_wait", scatter=True)
            for n, s, ld in zip(names, src, land):
                mine[n] = _sum_slots(lax.dynamic_index_in_dim(s, chip, 0, keepdims=False), ld)
        theirs = _swap_cores(list(mine.values()), name="swap_cores_" + tag)
        for n, t in zip(mine, theirs):
            if n == "w_in":
                res[n] = [unflat(r, wts[n].shape) for r in _adamw(flat(wts[n]), flat(mom[n]), flat(var[n]),
                                                                  flat(mine[n]), flat(t))]
            else:
                res[n] = _adamw(wts[n], mom[n], var[n], mine[n], t)

    finish(flying[:-1], g["ln0_g"], "a")

    small_shapes = [(1, 128)] + [wts[n].shape for n in _SMALL] + [g[n].shape for n in _SMALL_SHARDED]
    packed = _pack([loss] + [g[n] for n in _SMALL] + [g[n] for n in _SMALL_SHARDED])
    tot = _unpack(_allreduce_small(packed, dep=res["w_o"][3]), small_shapes)
    loss_out = tot[0][0, 0]
    gsm = dict(zip(_SMALL + _SMALL_SHARDED, tot[1:]))
    for n in _SMALL_SHARDED:
        cols = wts[n].shape[-1]
        gsm[n] = lax.dynamic_slice_in_dim(gsm[n], chip * cols, cols, axis=1).reshape(wts[n].shape)

    names = _SMALL + _SMALL_SHARDED
    shapes = [wts[n].shape for n in names]
    pk = lambda dct: _pack([dct[n] for n in names])
    small_out = _adamw(pk(wts), pk(mom), pk(var), pk(gsm))
    small_res = [_unpack(r, shapes) for r in small_out]
    for j, n in enumerate(names):
        res[n] = [small_res[q][j] for q in range(4)]
    finish(flying[-1:], small_out[0], "b")

    return (loss_out, grad_x, *[res[n][0] for n in _ORDER], *[res[n][1] for n in _ORDER],
            *[res[n][2] for n in _ORDER], *[res[n][3] for n in _ORDER])
```

```python
import functools
import math

import jax
import jax.numpy as jnp
from jax import lax
from jax.experimental import pallas as pl
from jax.experimental.pallas import tpu as pltpu

F32 = jnp.float32
BF16 = jnp.bfloat16
HI = lax.Precision.HIGHEST

N_META = 16
M_HEADS = 4
M_CHUNK = 128
PAD = M_CHUNK - N_META
CONV_W = 4
HALO_ROWS = 16
S5_GROUP = 16
S5_STATE = 64
S5_KCH = 4
LN_EPS = 1e-5
ALPHA = 2.0 ** 0.25
NEG = -1e30
ADAM_LR, ADAM_B1, ADAM_B2, ADAM_EPS, ADAM_WD, ADAM_STEP = 0.001, 0.9, 0.999, 1e-08, 0.01, 10

O_OFF, GS_OFF, GM_OFF, V_OFF, Q_OFF, K_OFF, U_OFF, G_OFF, NP = 0, 1024, 2048, 3072, 4096, 4608, 5120, 5632, 5760

NN = ((1,), (0,))
NT = ((1,), (1,))
TN = ((0,), (0,))


def _dot(a, b, dims=NN, prec=None):
    return lax.dot_general(a, b, (dims, ((), ())), preferred_element_type=F32, precision=prec)


def _bf(x):
    return x.astype(BF16)


def _sig(x):
    return 0.5 * jnp.tanh(0.5 * x) + 0.5


def _pcall(body, **kw):
    return pl.pallas_call(body, **kw)


def _cp(sem=None, vmem_mb=None):
    kw = {}
    if sem is not None:
        kw["dimension_semantics"] = sem
    if vmem_mb is not None:
        kw["vmem_limit_bytes"] = vmem_mb << 20
    return pltpu.CompilerParams(**kw)


def _row_tile(n, want, mult=16):
    best = None
    for t in range(mult, want + 1, mult):
        if n % t == 0:
            best = t
    assert best is not None, (n, want)
    return best


def _resident(shape):
    nd = len(shape)
    return pl.BlockSpec(shape, lambda *_: (0,) * nd, pipeline_mode=pl.Buffered(1))


def _const(shape):
    nd = len(shape)
    return pl.BlockSpec(shape, lambda *_: (0,) * nd)


def _ln_fwd(x, g, b):
    mu = jnp.mean(x, axis=-1, keepdims=True)
    xc = x - mu
    var = jnp.mean(xc * xc, axis=-1, keepdims=True)
    rstd = lax.rsqrt(var + LN_EPS)
    xhat = xc * rstd
    return xhat * g + b, xhat, rstd


def _ln_bwd(dy, xhat, rstd, g):
    dxh = dy * g
    m1 = jnp.mean(dxh, axis=-1, keepdims=True)
    m2 = jnp.mean(dxh * xhat, axis=-1, keepdims=True)
    return rstd * (dxh - m1 - xhat * m2)


def _colsum(x):
    return jnp.sum(x, axis=0, keepdims=True)


def _to_pad_cols(w):
    u, q, k, v, o, gi, gf, gs, gm = (w[..., 0:512], w[..., 512:1024], w[..., 1024:1536], w[..., 1536:2560],
                                     w[..., 2560:3584], w[..., 3584:3588], w[..., 3588:3592], w[..., 3592:4616],
                                     w[..., 4616:5640])
    z = jnp.zeros(w.shape[:-1] + (NP - G_OFF - 8,), w.dtype)
    return jnp.concatenate([o, gs, gm, v, q, k, u, gi, gf, z], axis=-1)


def _from_pad_cols(w):
    o, gs, gm, v, q, k, u = (w[..., O_OFF:GS_OFF], w[..., GS_OFF:GM_OFF], w[..., GM_OFF:V_OFF], w[..., V_OFF:Q_OFF],
                             w[..., Q_OFF:K_OFF], w[..., K_OFF:U_OFF], w[..., U_OFF:G_OFF])
    gi, gf = w[..., G_OFF:G_OFF + 4], w[..., G_OFF + 4:G_OFF + 8]
    return jnp.concatenate([u, q, k, v, o, gi, gf, gs, gm], axis=-1)


_IN_REF = (("u", 512), ("q", 512), ("k", 512), ("v", 1024), ("o", 1024), ("i", 4), ("f", 4), ("gs", 1024), ("gm", 1024))
_IN_PAD = (("o", O_OFF), ("gs", GS_OFF), ("gm", GM_OFF), ("v", V_OFF), ("q", Q_OFF), ("k", K_OFF), ("u", U_OFF),
           ("i", G_OFF), ("f", G_OFF + 4))


def _in_ref_ranges():
    out, off = {}, 0
    for n, s in _IN_REF:
        out[n] = (off, off + s)
        off += s
    return out, off


def _w_in_from_slots(g, chunk=None):
    rng, total = _in_ref_ranges()
    width = total // g.shape[0]
    cols = []
    for n, _ in _IN_PAD:
        a, b = rng[n]
        while a < b:
            s = a // width
            e = min(b, (s + 1) * width)
            cols.append(g[s][:, a - s * width:e - s * width])
            a = e
    cols.append(jnp.zeros((g.shape[1], NP - G_OFF - 8), g.dtype))
    if chunk is None:
        return jnp.concatenate(cols, axis=1)
    chunks, cur, room = [], [], chunk
    for c in cols:
        while c.shape[1] > 0:
            take = min(room, c.shape[1])
            cur.append(c[:, :take])
            c, room = c[:, take:], room - take
            if room == 0:
                chunks.append(jnp.concatenate(cur, axis=1))
                cur, room = [], chunk
    assert not cur
    return jnp.stack(chunks, axis=0)


def _slots_from_w_in(wp, nslot=4):
    rng, total = _in_ref_ranges()
    width = total // nslot
    pad_off = dict(_IN_PAD)
    slots = []
    for s in range(nslot):
        lo, hi = s * width, (s + 1) * width
        cols = []
        for n, _ in _IN_REF:
            a, b = rng[n]
            x0, x1 = max(a, lo), min(b, hi)
            if x0 < x1:
                cols.append(wp[:, pad_off[n] + x0 - a:pad_off[n] + x1 - a])
        slots.append(jnp.concatenate(cols, axis=1))
    return jnp.stack(slots, axis=0)


def _ln0_fwd(hin, g, b, lp):
    r, d = hin.shape
    tm = _row_tile(lp, 416)

    def body(x_ref, g_ref, b_ref, o_ref, ob_ref):
        y, _, _ = _ln_fwd(x_ref[...], g_ref[...], b_ref[...])
        o_ref[...] = y
        ob_ref[...] = _bf(y)

    row = pl.BlockSpec((tm, d), lambda i: (i, 0))
    return _pcall(
        body, name="ln0_fwd", grid=(r // tm,),
        in_specs=[row, _const((1, d)), _const((1, d))],
        out_specs=[row, row],
        out_shape=[jax.ShapeDtypeStruct((r, d), F32), jax.ShapeDtypeStruct((r, d), BF16)],
        compiler_params=_cp(("parallel",)),
    )(hin, g, b)


def _ln0_bwd(hin, dr1, dpw, g, lp):
    r, d = hin.shape
    tm = _row_tile(lp, 416)
    tps = lp // tm
    assert tm >= PAD + N_META

    def body(x_ref, a_ref, c_ref, g_ref, o_ref, dg_ref, db_ref, dm_ref):
        i = pl.program_id(0)

        @pl.when(i == 0)
        def _():
            dg_ref[...] = jnp.zeros_like(dg_ref)
            db_ref[...] = jnp.zeros_like(db_ref)
            dm_ref[...] = jnp.zeros_like(dm_ref)

        dy = ALPHA * a_ref[...] + c_ref[...]
        _, xhat, rstd = _ln_fwd(x_ref[...], g_ref[...], 0.0)
        dx = _ln_bwd(dy, xhat, rstd, g_ref[...])
        o_ref[...] = dx
        dg_ref[...] += _colsum(dy * xhat)
        db_ref[...] += _colsum(dy)

        @pl.when(i % tps == 0)
        def _():
            dm_ref[...] += dx[PAD:PAD + N_META, :]

    return _pcall(
        body, name="ln0_bwd", grid=(r // tm,),
        in_specs=[pl.BlockSpec((tm, d), lambda i: (i, 0))] * 3 + [_const((1, d))],
        out_specs=[pl.BlockSpec((tm, d), lambda i: (i, 0)), _const((1, d)), _const((1, d)), _const((N_META, d))],
        out_shape=[jax.ShapeDtypeStruct((r, d), F32), jax.ShapeDtypeStruct((1, d), F32),
                   jax.ShapeDtypeStruct((1, d), F32), jax.ShapeDtypeStruct((N_META, d), F32)],
        compiler_params=_cp(("arbitrary",)),
    )(hin, dr1, dpw, g)


IN_CHUNK = 1152


def _chunk_cols(w):
    k, n = w.shape
    return jnp.transpose(w.reshape(k, n // IN_CHUNK, IN_CHUNK), (1, 0, 2))


def _inproj(h0b, w3, bias, lp):
    r, d = h0b.shape
    nj, _, tn = w3.shape
    tm = _row_tile(lp, 832)
    tps = lp // tm

    def body(a_ref, w_ref, b_ref, o_ref, gate_ref):
        i = pl.program_id(0)
        j = pl.program_id(1)
        acc = _dot(a_ref[...], w_ref[j]) + b_ref[...]
        t = (i % tps) * tm + lax.broadcasted_iota(jnp.int32, (tm, 1), 0)
        acc = jnp.where(t >= PAD, acc, 0.0)
        o_ref[...] = _bf(acc)

        @pl.when(j == nj - 1)
        def _():
            gate_ref[...] = acc[:, tn - 128:]

    return _pcall(
        body, name="inproj", grid=(r // tm, nj),
        in_specs=[pl.BlockSpec((tm, d), lambda i, j: (i, 0)), _resident(w3.shape),
                  pl.BlockSpec((1, tn), lambda i, j: (0, j))],
        out_specs=[pl.BlockSpec((tm, tn), lambda i, j: (i, j)), pl.BlockSpec((tm, 128), lambda i, j: (i, 0))],
        out_shape=[jax.ShapeDtypeStruct((r, nj * tn), BF16), jax.ShapeDtypeStruct((r, 128), F32)],
        compiler_params=_cp(("parallel", "arbitrary"), 48),
    )(h0b, w3, bias)


def _mm_tn(a, b, *, name, split=1, colsum=False, tk_want=1408):
    r, m = a.shape
    n = b.shape[1]
    tk = _row_tile(r, tk_want)
    tm = min(m, 1024)
    ns = n // split
    tn = ns
    for cand in (1024, 1152, 640, 512, 128):
        if ns % cand == 0 and cand <= ns:
            tn = cand
            break
    nb = ns // tn
    nk = r // tk

    def body(a_ref, b_ref, o_ref, *rest):
        acc = rest[-1]
        k = pl.program_id(2)

        @pl.when(k == 0)
        def _():
            acc[...] = jnp.zeros_like(acc)

        bt = b_ref[...]
        acc[...] += _dot(_bf(a_ref[...]), _bf(bt), TN)

        @pl.when(k == nk - 1)
        def _():
            o_ref[...] = acc[...]

        if colsum:
            cs_ref = rest[0]

            @pl.when(k == 0)
            def _():
                cs_ref[...] = jnp.zeros_like(cs_ref)

            cs_ref[...] += _colsum(bt.astype(F32))

    out_specs = [pl.BlockSpec((None, tm, tn), lambda i, j, k: (j // nb, i, j % nb))]
    out_shape = [jax.ShapeDtypeStruct((split, m, ns), F32)]
    if colsum:
        assert m == tm
        out_specs.append(pl.BlockSpec((1, tn), lambda i, j, k: (0, j)))
        out_shape.append(jax.ShapeDtypeStruct((1, n), F32))
    res = _pcall(
        body, name=name, grid=(m // tm, n // tn, nk),
        in_specs=[pl.BlockSpec((tk, tm), lambda i, j, k: (k, i)), pl.BlockSpec((tk, tn), lambda i, j, k: (k, j))],
        out_specs=out_specs, out_shape=out_shape,
        scratch_shapes=[pltpu.VMEM((tm, tn), F32)],
        compiler_params=_cp(("parallel", "parallel", "arbitrary"), 56),
    )(a, b)
    return res if colsum else res[0]


def _mm_nt(a, w3, lp, *, name, dep=None):
    r, kdim = a.shape
    nk, n, tk = w3.shape
    assert nk * tk == kdim
    tm = _row_tile(lp, 832)
    deps = [] if dep is None else [dep]

    def body(a_ref, w_ref, *rest):
        o_ref, acc = rest[-2:]
        k = pl.program_id(1)

        @pl.when(k == 0)
        def _():
            acc[...] = jnp.zeros_like(acc)

        acc[...] += _dot(_bf(a_ref[...]), w_ref[k], NT)

        @pl.when(k == nk - 1)
        def _():
            o_ref[...] = acc[...]

    return _pcall(
        body, name=name, grid=(r // tm, nk),
        in_specs=[pl.BlockSpec((tm, tk), lambda i, k: (i, k)), _resident(w3.shape)]
        + [_const(dp_.shape) for dp_ in deps],
        out_specs=pl.BlockSpec((tm, n), lambda i, k: (i, 0)),
        out_shape=jax.ShapeDtypeStruct((r, n), F32),
        scratch_shapes=[pltpu.VMEM((tm, n), F32)],
        compiler_params=_cp(("parallel", "arbitrary"), 48),
    )(a, w3, *deps)


def _s5_prep(lam_re, lam_im, log_dt, b_re_t, b_im_t):
    g, p = lam_re.shape
    h = b_re_t.shape[0]

    def body(lr_ref, li_ref, ldt_ref, br_ref, bi_ref, pr_ref, pi_ref, bbr_ref, bbi_ref):
        lr, li = lr_ref[...], li_ref[...]
        dt = jnp.exp(ldt_ref[...])
        e = jnp.exp(lr * dt)
        ar, ai = e * jnp.cos(li * dt), e * jnp.sin(li * dt)
        den = lr * lr + li * li
        cr = ((ar - 1.0) * lr + ai * li) / den
        ci = (ai * lr - (ar - 1.0) * li) / den
        br, bi = br_ref[...], bi_ref[...]
        bbr_ref[...] = cr[None] * br - ci[None] * bi
        bbi_ref[...] = cr[None] * bi + ci[None] * br
        xr, xi = ar, ai
        pr_ref[0] = xr
        pi_ref[0] = xi
        for t in range(1, 8):
            xr, xi = xr * ar - xi * ai, xr * ai + xi * ar
            pr_ref[t] = xr
            pi_ref[t] = xi

    sd = jax.ShapeDtypeStruct
    return _pcall(body, name="s5_prep",
                  out_shape=[sd((8, g, p), F32), sd((8, g, p), F32), sd((h, g, p), F32), sd((h, g, p), F32)])(
        lam_re, lam_im, log_dt, b_re_t, b_im_t)


def _s5_prep_bwd(lam_re, lam_im, log_dt, b_re_t, b_im_t, da_re, da_im, dbb_re_t, dbb_im_t):
    g, p = lam_re.shape
    h = b_re_t.shape[0]

    def body(lr_ref, li_ref, ldt_ref, br_ref, bi_ref, dar_ref, dai_ref, dbr_ref, dbi_ref,
             glr_ref, gli_ref, gdt_ref, gbr_ref, gbi_ref):
        lr, li = lr_ref[...], li_ref[...]
        dt = jnp.exp(ldt_ref[...])
        e = jnp.exp(lr * dt)
        ar, ai = e * jnp.cos(li * dt), e * jnp.sin(li * dt)
        den = lr * lr + li * li
        cr = ((ar - 1.0) * lr + ai * li) / den
        ci = (ai * lr - (ar - 1.0) * li) / den
        br, bi = br_ref[...], bi_ref[...]
        gr, gi = dbr_ref[...], dbi_ref[...]
        gbr_ref[...] = gr * cr[None] + gi * ci[None]
        gbi_ref[...] = gi * cr[None] - gr * ci[None]
        gcr = jnp.sum(gr * br + gi * bi, axis=0)
        gci = jnp.sum(gi * br - gr * bi, axis=0)
        ilr, ili = lr / den, -li / den
        gar = dar_ref[...] + gcr * ilr + gci * ili
        gai = dai_ref[...] + gci * ilr - gcr * ili
        qr, qi = cr * ilr - ci * ili, cr * ili + ci * ilr
        glr = -(gcr * qr + gci * qi)
        gli = -(gci * qr - gcr * qi)
        gzr = gar * ar + gai * ai
        gzi = gai * ar - gar * ai
        glr_ref[...] = glr + gzr * dt
        gli_ref[...] = gli + gzi * dt
        gdt_ref[...] = jnp.sum(gzr * lr + gzi * li, axis=1, keepdims=True) * dt

    sd = jax.ShapeDtypeStruct
    return _pcall(body, name="s5_prep_bwd",
                  out_shape=[sd((g, p), F32), sd((g, p), F32), sd((g, 1), F32), sd((h, g, p), F32), sd((h, g, p), F32)])(
        lam_re, lam_im, log_dt, b_re_t, b_im_t, da_re, da_im, dbb_re_t, dbb_im_t)


def _cmul(xr, xi, yr, yi):
    return xr * yr - xi * yi, xr * yi + xi * yr


def _dot5(a, b, dims=NN):
    return _dot(_bf(a), _bf(b), dims)


def _s5_fwd(p3, bk, cre, cim, apow, dskip):
    bsz, lp, _ = p3.shape
    tt = _row_tile(lp, 528, 8)
    nt = lp // tt
    nblk = tt // 8
    hw = 512

    def body(u_ref, bk_ref, cre_ref, cim_ref, ap_ref, d_ref, y_ref, xs_ref, car_ref):
        t = pl.program_id(2)

        @pl.when(t == 0)
        def _():
            car_ref[...] = jnp.zeros_like(car_ref)

        u = u_ref[...].astype(F32)
        xs_ref[...] = _dot5(u, bk_ref[...])
        ap = ap_ref[...]
        apr, api = ap[:, :hw], ap[:, hw:]
        rows = lax.broadcasted_iota(jnp.int32, (8, hw), 0)
        lev = [(d, jnp.where(rows < d, 0.0, jnp.broadcast_to(apr[d - 1:d, :], (8, hw))),
                jnp.where(rows < d, 0.0, jnp.broadcast_to(api[d - 1:d, :], (8, hw)))) for d in (1, 2, 4)]

        def blk(i, carry):
            cr, ci = carry
            off = pl.multiple_of(i * 8, 8)
            x = xs_ref[pl.ds(off, 8), :]
            xr, xi = x[:, :hw], x[:, hw:]
            for d, lr, li in lev:
                mr, mi = _cmul(pltpu.roll(xr, d, 0), pltpu.roll(xi, d, 0), lr, li)
                xr, xi = xr + mr, xi + mi
            mr, mi = _cmul(apr, api, cr, ci)
            xr, xi = xr + mr, xi + mi
            xs_ref[pl.ds(off, 8), :] = jnp.concatenate([xr, xi], axis=1)
            return xr[7:8, :], xi[7:8, :]

        c0 = car_ref[...]
        cr, ci = lax.fori_loop(0, nblk, blk, (c0[0:1, :hw], c0[0:1, hw:]))
        car_ref[...] = jnp.broadcast_to(jnp.concatenate([cr, ci], axis=1), car_ref.shape)
        xs = xs_ref[...]
        y_ref[...] = (_dot5(xs[:, :hw], cre_ref[...]) - _dot5(xs[:, hw:], cim_ref[...])
                      + d_ref[...] * u)

    ub = U_OFF // 128
    return _pcall(
        body, name="s5_fwd", grid=(S5_KCH, bsz, nt),
        in_specs=[pl.BlockSpec((None, tt, 128), lambda k, b, t: (b, t, ub + k)),
                  pl.BlockSpec((None, 128, 2 * hw), lambda k, b, t: (k, 0, 0)),
                  pl.BlockSpec((None, hw, 128), lambda k, b, t: (k, 0, 0)),
                  pl.BlockSpec((None, hw, 128), lambda k, b, t: (k, 0, 0)),
                  pl.BlockSpec((None, 8, 2 * hw), lambda k, b, t: (k, 0, 0)),
                  pl.BlockSpec((1, 128), lambda k, b, t: (0, k))],
        out_specs=[pl.BlockSpec((None, tt, 128), lambda k, b, t: (b, t, k)),
                   pl.BlockSpec((None, None, tt, 2 * hw), lambda k, b, t: (b, k, t, 0))],
        out_shape=[jax.ShapeDtypeStruct((bsz, lp, S5_KCH * 128), F32),
                   jax.ShapeDtypeStruct((bsz, S5_KCH, lp, 2 * hw), F32)],
        scratch_shapes=[pltpu.VMEM((8, 2 * hw), F32)],
        compiler_params=_cp(("parallel", "parallel", "arbitrary"), 40),
    )(p3, bk, cre, cim, apow, dskip)


def _s5_bwd(dp3, p3, dy3, xs, bk, cre, cim, apow_rev, dskip):
    bsz, lp, _ = p3.shape
    tt = _row_tile(lp, 528, 8)
    nt = lp // tt
    nblk = tt // 8
    hw = 512
    tb = tt // 8

    def body(dp_any, u_ref, dy_ref, xs_ref, halo_ref, bk_ref, cre_ref, cim_ref, ap_ref, d_ref,
             du_ref, dbk_ref, dcre_ref, dcim_ref, da_ref, dd_ref, g_ref, ext_ref, car_ref):
        b = pl.program_id(1)
        t = pl.program_id(2)
        tidx = nt - 1 - t

        @pl.when(t == 0)
        def _():
            car_ref[...] = jnp.zeros_like(car_ref)

        @pl.when((b == 0) & (t == 0))
        def _():
            dbk_ref[...] = jnp.zeros_like(dbk_ref)
            dcre_ref[...] = jnp.zeros_like(dcre_ref)
            dcim_ref[...] = jnp.zeros_like(dcim_ref)
            da_ref[...] = jnp.zeros_like(da_ref)
            dd_ref[...] = jnp.zeros_like(dd_ref)

        u = u_ref[...].astype(F32)
        dy = dy_ref[...]
        g_ref[:, :hw] = _dot5(dy, cre_ref[...], NT)
        g_ref[:, hw:] = -_dot5(dy, cim_ref[...], NT)
        ap = ap_ref[...]
        apr, api = ap[:, :hw], -ap[:, hw:]
        rows = lax.broadcasted_iota(jnp.int32, (8, hw), 0)
        lev = [(d, jnp.where(rows >= 8 - d, 0.0, jnp.broadcast_to(apr[8 - d:9 - d, :], (8, hw))),
                jnp.where(rows >= 8 - d, 0.0, jnp.broadcast_to(api[8 - d:9 - d, :], (8, hw)))) for d in (1, 2, 4)]

        def blk(i, carry):
            cr, ci = carry
            off = pl.multiple_of((nblk - 1 - i) * 8, 8)
            x = g_ref[pl.ds(off, 8), :]
            xr, xi = x[:, :hw], x[:, hw:]
            for d, lr, li in lev:
                mr, mi = _cmul(pltpu.roll(xr, 8 - d, 0), pltpu.roll(xi, 8 - d, 0), lr, li)
                xr, xi = xr + mr, xi + mi
            mr, mi = _cmul(apr, api, cr, ci)
            xr, xi = xr + mr, xi + mi
            g_ref[pl.ds(off, 8), :] = jnp.concatenate([xr, xi], axis=1)
            return xr[0:1, :], xi[0:1, :]

        c0 = car_ref[...]
        cr, ci = lax.fori_loop(0, nblk, blk, (c0[0:1, :hw], c0[0:1, hw:]))
        car_ref[...] = jnp.broadcast_to(jnp.concatenate([cr, ci], axis=1), car_ref.shape)

        gg = g_ref[...]
        du = _dot5(gg, bk_ref[...], NT) + d_ref[...] * dy
        trow = tidx * tt + lax.broadcasted_iota(jnp.int32, (tt, 1), 0)
        du_ref[...] = jnp.where(trow >= PAD, du, 0.0).astype(du_ref.dtype)
        dbk_ref[...] += _dot5(u, gg, TN)
        xsv = xs_ref[...]
        dcre_ref[...] += _dot5(xsv[:, :hw], dy, TN)
        dcim_ref[...] -= _dot5(xsv[:, hw:], dy, TN)
        dd_ref[...] += _colsum(dy * u)
        ext_ref[0:8, :] = jnp.where(tidx == 0, 0.0, halo_ref[...])
        ext_ref[8:, :] = xsv
        xp = ext_ref[pl.ds(7, tt), :]
        gr, gi, pr, pi = gg[:, :hw], gg[:, hw:], xp[:, :hw], xp[:, hw:]
        da_ref[:, :hw] += _colsum(gr * pr + gi * pi)
        da_ref[:, hw:] += _colsum(gi * pr - gr * pi)

    ub = U_OFF // 128
    sd = jax.ShapeDtypeStruct
    rt = lambda t: nt - 1 - t
    res = _pcall(
        body, name="s5_bwd", grid=(S5_KCH, bsz, nt),
        in_specs=[pl.BlockSpec(memory_space=pl.ANY),
                  pl.BlockSpec((None, tt, 128), lambda k, b, t: (b, rt(t), ub + k)),
                  pl.BlockSpec((None, tt, 128), lambda k, b, t: (b, rt(t), k)),
                  pl.BlockSpec((None, None, tt, 2 * hw), lambda k, b, t: (b, k, rt(t), 0)),
                  pl.BlockSpec((None, None, 8, 2 * hw), lambda k, b, t: (b, k, jnp.maximum(rt(t) * tb - 1, 0), 0)),
                  pl.BlockSpec((None, 128, 2 * hw), lambda k, b, t: (k, 0, 0)),
                  pl.BlockSpec((None, hw, 128), lambda k, b, t: (k, 0, 0)),
                  pl.BlockSpec((None, hw, 128), lambda k, b, t: (k, 0, 0)),
                  pl.BlockSpec((None, 8, 2 * hw), lambda k, b, t: (k, 0, 0)),
                  pl.BlockSpec((1, 128), lambda k, b, t: (0, k))],
        out_specs=[pl.BlockSpec((None, tt, 128), lambda k, b, t: (b, rt(t), ub + k)),
                   pl.BlockSpec((None, 128, 2 * hw), lambda k, b, t: (k, 0, 0)),
                   pl.BlockSpec((None, hw, 128), lambda k, b, t: (k, 0, 0)),
                   pl.BlockSpec((None, hw, 128), lambda k, b, t: (k, 0, 0)),
                   pl.BlockSpec((None, 1, 2 * hw), lambda k, b, t: (k, 0, 0)),
                   pl.BlockSpec((1, 128), lambda k, b, t: (0, k))],
        out_shape=[sd(dp3.shape, dp3.dtype), sd((S5_KCH, 128, 2 * hw), F32), sd((S5_KCH, hw, 128), F32),
                   sd((S5_KCH, hw, 128), F32), sd((S5_KCH, 1, 2 * hw), F32), sd((1, S5_KCH * 128), F32)],
        scratch_shapes=[pltpu.VMEM((tt, 2 * hw), F32), pltpu.VMEM((tt + 8, 2 * hw), F32), pltpu.VMEM((8, 2 * hw), F32)],
        input_output_aliases={0: 0},
        compiler_params=_cp(("arbitrary", "arbitrary", "arbitrary"), 48),
    )(dp3, p3, dy3, xs, xs, bk, cre, cim, apow_rev, dskip)
    return res


_G0 = math.sqrt(2.0 / math.pi)
_G1 = 0.044715


def _gelu(y):
    return 0.5 * y * (1.0 + jnp.tanh(_G0 * (y + _G1 * y * y * y)))


def _gelu_grad(y):
    th = jnp.tanh(_G0 * (y + _G1 * y * y * y))
    return 0.5 * (1.0 + th) + 0.5 * y * (1.0 - th * th) * _G0 * (1.0 + 3.0 * _G1 * y * y)


def _glu_fwd(y_s5, wglu_g, lp):
    r, w = y_s5.shape
    tm = _row_tile(lp, 416)
    cw = wglu_g.shape[2]

    def body(y_ref, w_ref, gy_ref, z_ref, o_ref):
        gy = _bf(_gelu(y_ref[...]))
        gy_ref[...] = gy
        zs = [_dot(gy, w_ref[s]) for s in range(4)]
        for s in range(4):
            z_ref[:, s * cw:(s + 1) * cw] = zs[s]
        o_ref[:, :cw] = zs[0] * _sig(zs[2])
        o_ref[:, cw:] = zs[1] * _sig(zs[3])

    sd = jax.ShapeDtypeStruct
    return _pcall(
        body, name="glu_fwd", grid=(r // tm,),
        in_specs=[pl.BlockSpec((tm, w), lambda i: (i, 0)), _resident(wglu_g.shape)],
        out_specs=[pl.BlockSpec((tm, w), lambda i: (i, 0)), pl.BlockSpec((tm, 4 * cw), lambda i: (i, 0)),
                   pl.BlockSpec((tm, 2 * cw), lambda i: (i, 0))],
        out_shape=[sd((r, w), BF16), sd((r, 4 * cw), F32), sd((r, 2 * cw), F32)],
        compiler_params=_cp(("parallel",), 40),
    )(y_s5, wglu_g)


def _glu_bwd(dyg, z, y_s5, wglu_g, lp):
    r, w = y_s5.shape
    tm = _row_tile(lp, 416)
    cw = wglu_g.shape[2]

    def body(d_ref, z_ref, y_ref, w_ref, dz_ref, dy_ref):
        d = d_ref[...]
        zz = z_ref[...]
        acc = jnp.zeros((tm, w), F32)
        for s in range(2):
            z1 = zz[:, s * cw:(s + 1) * cw]
            sg = _sig(zz[:, (2 + s) * cw:(3 + s) * cw])
            dd = d[:, s * cw:(s + 1) * cw]
            dz1 = _bf(dd * sg)
            dz2 = _bf(dd * z1 * sg * (1.0 - sg))
            dz_ref[:, s * cw:(s + 1) * cw] = dz1
            dz_ref[:, (2 + s) * cw:(3 + s) * cw] = dz2
            acc += _dot(dz1, w_ref[s], NT) + _dot(dz2, w_ref[2 + s], NT)
        dy_ref[...] = acc * _gelu_grad(y_ref[...])

    sd = jax.ShapeDtypeStruct
    return _pcall(
        body, name="glu_bwd", grid=(r // tm,),
        in_specs=[pl.BlockSpec((tm, 2 * cw), lambda i: (i, 0)), pl.BlockSpec((tm, 4 * cw), lambda i: (i, 0)),
                  pl.BlockSpec((tm, w), lambda i: (i, 0)), _resident(wglu_g.shape)],
        out_specs=[pl.BlockSpec((tm, 4 * cw), lambda i: (i, 0)), pl.BlockSpec((tm, w), lambda i: (i, 0))],
        out_shape=[sd((r, 4 * cw), BF16), sd((r, w), F32)],
        compiler_params=_cp(("parallel",), 40),
    )(dyg, z, y_s5, wglu_g)


def _conv_fwd(p3, cw, cb):
    bsz, lp, _ = p3.shape
    tt = _row_tile(lp, 416)
    nt = lp // tt
    tb = tt // 8
    c = cw.shape[1]
    qb = Q_OFF // c

    hr = HALO_ROWS
    off = hr - (CONV_W - 1)

    def body(x_ref, halo_ref, w_ref, b_ref, pre_ref, act_ref, ext_ref):
        t = pl.program_id(1)
        ext_ref[0:hr, :] = jnp.where(t == 0, 0.0, halo_ref[...].astype(F32))
        ext_ref[hr:, :] = x_ref[...].astype(F32)
        w = w_ref[...]
        acc = b_ref[...] + w[0:1, :] * ext_ref[pl.ds(off, tt), :]
        for j in range(1, CONV_W):
            acc = acc + w[j:j + 1, :] * ext_ref[pl.ds(off + j, tt), :]
        pre_ref[...] = acc
        act_ref[...] = acc * _sig(acc)

    sd = jax.ShapeDtypeStruct
    return _pcall(
        body, name="conv_fwd", grid=(bsz, nt),
        in_specs=[pl.BlockSpec((None, tt, c), lambda b, t: (b, t, qb)),
                  pl.BlockSpec((None, hr, c), lambda b, t: (b, jnp.maximum(t * (tt // hr) - 1, 0), qb)),
                  _const((CONV_W, c)), _const((1, c))],
        out_specs=[pl.BlockSpec((None, tt, c), lambda b, t: (b, t, 0))] * 2,
        out_shape=[sd((bsz, lp, c), F32)] * 2,
        scratch_shapes=[pltpu.VMEM((tt + hr, c), F32)],
        compiler_params=_cp(("parallel", "parallel")),
    )(p3, p3, cw, cb)


def _conv_bwd(dp3, p3, dact3, pre3, cw):
    bsz, lp, _ = p3.shape
    tt = _row_tile(lp, 416)
    nt = lp // tt
    tb = tt // 8
    c = cw.shape[1]
    qb = Q_OFF // c

    hr = HALO_ROWS
    off = hr - (CONV_W - 1)

    def silu_grad(x):
        s = _sig(x)
        return s * (1.0 + x * (1.0 - s))

    def body(dp_any, x_ref, xh_ref, d_ref, dh_ref, pre_ref, preh_ref, w_ref, o_ref, dw_ref, db_ref, ext_ref, dext_ref):
        b = pl.program_id(0)
        t = pl.program_id(1)

        @pl.when((b == 0) & (t == 0))
        def _():
            dw_ref[...] = jnp.zeros_like(dw_ref)
            db_ref[...] = jnp.zeros_like(db_ref)

        dc = d_ref[...] * silu_grad(pre_ref[...])
        dch = jnp.where(t == nt - 1, 0.0, dh_ref[...] * silu_grad(preh_ref[...]))
        dext_ref[0:tt, :] = dc
        dext_ref[tt:, :] = dch
        ext_ref[0:hr, :] = jnp.where(t == 0, 0.0, xh_ref[...].astype(F32))
        ext_ref[hr:, :] = x_ref[...].astype(F32)
        w = w_ref[...]
        acc = w[CONV_W - 1:CONV_W, :] * dc
        for j in range(CONV_W - 1):
            acc = acc + w[j:j + 1, :] * dext_ref[pl.ds(CONV_W - 1 - j, tt), :]
        trow = t * tt + lax.broadcasted_iota(jnp.int32, (tt, 1), 0)
        o_ref[...] = jnp.where(trow >= PAD, acc, 0.0).astype(o_ref.dtype)
        db_ref[...] += _colsum(dc)
        for j in range(CONV_W):
            dw_ref[j:j + 1, :] += _colsum(dc * ext_ref[pl.ds(off + j, tt), :])

    sd = jax.ShapeDtypeStruct
    nxt = lambda t: jnp.minimum((t + 1) * tb, lp // 8 - 1)
    return _pcall(
        body, name="conv_bwd", grid=(bsz, nt),
        in_specs=[pl.BlockSpec(memory_space=pl.ANY),
                  pl.BlockSpec((None, tt, c), lambda b, t: (b, t, qb)),
                  pl.BlockSpec((None, hr, c), lambda b, t: (b, jnp.maximum(t * (tt // hr) - 1, 0), qb)),
                  pl.BlockSpec((None, tt, c), lambda b, t: (b, t, 0)),
                  pl.BlockSpec((None, 8, c), lambda b, t: (b, nxt(t), 0)),
                  pl.BlockSpec((None, tt, c), lambda b, t: (b, t, 0)),
                  pl.BlockSpec((None, 8, c), lambda b, t: (b, nxt(t), 0)),
                  _const((CONV_W, c))],
        out_specs=[pl.BlockSpec((None, tt, c), lambda b, t: (b, t, qb)), _const((CONV_W, c)), _const((1, c))],
        out_shape=[sd(dp3.shape, dp3.dtype), sd((CONV_W, c), F32), sd((1, c), F32)],
        scratch_shapes=[pltpu.VMEM((tt + hr, c), F32), pltpu.VMEM((tt + 8, c), F32)],
        input_output_aliases={0: 0},
        compiler_params=_cp(("arbitrary", "arbitrary")),
    )(dp3, p3, p3, dact3, dact3, pre3, pre3, cw)


def _mlstm_gates(g, h_idx, c_idx, lc):
    lane = lax.broadcasted_iota(jnp.int32, g.shape, 1)
    i_col = jnp.sum(jnp.where(lane == h_idx, g, 0.0), axis=1, keepdims=True)
    f_col = jnp.sum(jnp.where(lane == M_HEADS + h_idx, g, 0.0), axis=1, keepdims=True)
    row = lax.broadcasted_iota(jnp.int32, (lc, 1), 0)
    valid = (c_idx * lc + row) >= PAD
    li = jnp.where(valid, i_col, NEG)
    lf = jnp.where(valid, jnp.minimum(f_col, 0.0) - jnp.log(1.0 + jnp.exp(-jnp.abs(f_col))), 0.0)
    r2 = lax.broadcasted_iota(jnp.int32, (lc, lc), 0)
    c2 = lax.broadcasted_iota(jnp.int32, (lc, lc), 1)
    eye = r2 == c2
    tril = r2 >= c2
    to_row = lambda col: jnp.sum(jnp.where(eye, col, 0.0), axis=0, keepdims=True)
    lf_row = to_row(lf)
    b_col = jnp.sum(jnp.where(tril, lf_row, 0.0), axis=1, keepdims=True)
    b_row = to_row(b_col)
    li_row = to_row(li)
    d_mat = jnp.where(tril, b_col - b_row + li_row, NEG)
    return dict(f_col=f_col, valid=valid, li=li, b_col=b_col, d_mat=d_mat, eye=eye, r2=r2, c2=c2, row=row,
                to_row=to_row)


def _mlstm_chunk(q, ks, v, gq, c_st, n_st, m_st, lc):
    b_col, d_mat = gq["b_col"], gq["d_mat"]
    m_inter = b_col + m_st
    m_row = jnp.maximum(m_inter, jnp.max(d_mat, axis=1, keepdims=True))
    w_intra = jnp.exp(d_mat - m_row)
    w_inter = jnp.exp(m_inter - m_row)
    qb, kb, vb, cb = _bf(q), _bf(ks), _bf(v), _bf(c_st)
    s = _dot(qb, kb, NT) * w_intra
    qc = _dot(qb, cb)
    num = _dot(_bf(s), vb) + w_inter * qc
    qn = jnp.sum(q * n_st, axis=1, keepdims=True)
    den = jnp.sum(s, axis=1, keepdims=True) + w_inter * qn
    e = jnp.exp(-m_row)
    nn = jnp.maximum(jnp.abs(den), e)
    b_last = b_col[lc - 1:lc, :]
    g_log = b_last - b_col + gq["li"]
    m_new = jnp.maximum(b_last + m_st, jnp.max(g_log, axis=0, keepdims=True))
    w_k = jnp.exp(g_log - m_new)
    decay = jnp.exp(b_last + m_st - m_new)
    return dict(w_intra=w_intra, w_inter=w_inter, qb=qb, kb=kb, vb=vb, cb=cb, s=s, qc=qc, num=num, qn=qn, den=den,
                e=e, nn=nn, m_new=m_new, w_k=w_k, decay=decay)


def _chunks_per_step(nc):
    return max(c for c in (3, 2, 1) if nc % c == 0)


def _mlstm_fwd(qk3, p3, pg3):
    bsz, lp, _ = p3.shape
    lc = M_CHUNK
    nc = lp // lc
    dk, dv = 128, 256
    scale = dk ** -0.5

    cps = _chunks_per_step(nc)
    rows = cps * lc

    def body(q_ref, k_ref, v_ref, g_ref, h_ref, cs_ref, ns_ref, ms_ref, c_sc, n_sc, m_sc):
        st = pl.program_id(1)

        @pl.when(st == 0)
        def _():
            c_sc[...] = jnp.zeros_like(c_sc)
            n_sc[...] = jnp.zeros_like(n_sc)
            m_sc[...] = jnp.zeros_like(m_sc)

        for j in range(cps):
            rs = slice(j * lc, (j + 1) * lc)
            g = g_ref[rs, :]
            for hh in range(M_HEADS):
                c_st, n_st, m_all = c_sc[hh], n_sc[hh], m_sc[hh]
                cs_ref[hh, j] = c_st
                ns_ref[hh, j] = n_st
                ms_ref[hh, j] = m_all
                m_st = m_all[:, 0:1]
                q = q_ref[rs, hh * dk:(hh + 1) * dk]
                ks = k_ref[rs, hh * dk:(hh + 1) * dk] * scale
                v = v_ref[rs, hh * dv:(hh + 1) * dv]
                gq = _mlstm_gates(g, hh, st * cps + j, lc)
                f = _mlstm_chunk(q, ks, v, gq, c_st, n_st, m_st, lc)
                h_ref[rs, hh * dv:(hh + 1) * dv] = f["num"] / f["nn"]
                kw = ks * f["w_k"]
                c_sc[hh] = f["decay"] * c_st + _dot(_bf(kw), f["vb"], TN)
                n_sc[hh] = f["decay"] * n_st + _colsum(kw)
                m_sc[hh] = jnp.broadcast_to(f["m_new"], (1, 128))

    sd = jax.ShapeDtypeStruct
    nh = M_HEADS
    return _pcall(
        body, name="mlstm_fwd", grid=(bsz, nc // cps),
        in_specs=[pl.BlockSpec((None, rows, nh * dk), lambda b, c: (b, c, 0)),
                  pl.BlockSpec((None, rows, nh * dk), lambda b, c: (b, c, 1)),
                  pl.BlockSpec((None, rows, nh * dv), lambda b, c: (b, c, V_OFF // (nh * dv))),
                  pl.BlockSpec((None, rows, 128), lambda b, c: (b, c, 0))],
        out_specs=[pl.BlockSpec((None, rows, nh * dv), lambda b, c: (b, c, 0)),
                   pl.BlockSpec((None, nh, cps, dk, dv), lambda b, c: (b, 0, c, 0, 0)),
                   pl.BlockSpec((None, nh, cps, 1, dk), lambda b, c: (b, 0, c, 0, 0)),
                   pl.BlockSpec((None, nh, cps, 1, 128), lambda b, c: (b, 0, c, 0, 0))],
        out_shape=[sd((bsz, lp, nh * dv), F32), sd((bsz, nh, nc, dk, dv), F32),
                   sd((bsz, nh, nc, 1, dk), F32), sd((bsz, nh, nc, 1, 128), F32)],
        scratch_shapes=[pltpu.VMEM((nh, dk, dv), F32), pltpu.VMEM((nh, 1, dk), F32), pltpu.VMEM((nh, 1, 128), F32)],
        compiler_params=_cp(("parallel", "arbitrary")),
    )(qk3, qk3, p3, pg3)


def _mlstm_bwd(dp3, qk3, p3, pg3, dh3, cs, ns, ms):
    bsz, lp, _ = p3.shape
    lc = M_CHUNK
    nc = lp // lc
    dk, dv = 128, 256
    scale = dk ** -0.5

    cps = _chunks_per_step(nc)
    nst = nc // cps
    rows = cps * lc

    def body(dp_any, q_ref, k_ref, v_ref, g_ref, dh_ref, cs_ref, ns_ref, ms_ref,
             dv_ref, dqk_ref, dg_ref, dc_sc, dn_sc):
        t = pl.program_id(1)
        st = nst - 1 - t

        @pl.when(t == 0)
        def _():
            dc_sc[...] = jnp.zeros_like(dc_sc)
            dn_sc[...] = jnp.zeros_like(dn_sc)

        lane = lax.broadcasted_iota(jnp.int32, (lc, 128), 1)
        for j in reversed(range(cps)):
            rs = slice(j * lc, (j + 1) * lc)
            g = g_ref[rs, :]
            dgate = jnp.zeros((lc, 128), F32)
            for hh in range(M_HEADS):
                dgate = head(hh, j, rs, st * cps + j, g, lane, dgate, q_ref, k_ref, v_ref, dh_ref, cs_ref, ns_ref,
                             ms_ref, dv_ref, dqk_ref, dc_sc, dn_sc)
            dg_ref[rs, :] = dgate.astype(dg_ref.dtype)

    def head(hh, j, sl, c, g, lane, dgate, q_ref, k_ref, v_ref, dh_ref, cs_ref, ns_ref, ms_ref, dv_ref, dqk_ref,
             dc_sc, dn_sc):
        c_st, n_st = cs_ref[hh, j], ns_ref[hh, j]
        m_st = ms_ref[hh, j][:, 0:1]
        q = q_ref[sl, hh * dk:(hh + 1) * dk]
        ks = k_ref[sl, hh * dk:(hh + 1) * dk] * scale
        v = v_ref[sl, hh * dv:(hh + 1) * dv]
        dh = dh_ref[sl, hh * dv:(hh + 1) * dv]
        gq = _mlstm_gates(g, hh, c, lc)
        f = _mlstm_chunk(q, ks, v, gq, c_st, n_st, m_st, lc)
        eye, r2, c2, row, valid = gq["eye"], gq["r2"], gq["c2"], gq["row"], gq["valid"]
        w_intra, w_inter, s, nn, den = f["w_intra"], f["w_inter"], f["s"], f["nn"], f["den"]
        qb, kb, vb, cb, w_k, decay = f["qb"], f["kb"], f["vb"], f["cb"], f["w_k"], f["decay"]
        d_c, d_n = dc_sc[hh], dn_sc[hh]
        d_cb = _bf(d_c)

        hout = f["num"] / nn
        dnum = dh / nn
        d_nn = -jnp.sum(dh * hout, axis=1, keepdims=True) / nn
        dden = jnp.where(jnp.abs(den) > f["e"], d_nn * jnp.sign(den), 0.0)
        wdnum = w_inter * dnum
        wdden = w_inter * dden
        ds = _dot(_bf(dnum), vb, NT) + dden
        dsw = _bf(ds * w_intra)
        dq = _dot(dsw, kb) + _dot(_bf(wdnum), cb, NT) + wdden * n_st
        dkw = _dot(vb, d_cb, NT) + d_n
        dks = _dot(dsw, qb, TN) + dkw * w_k
        kw = ks * w_k
        dvv = _dot(_bf(s), _bf(dnum), TN) + _dot(_bf(kw), d_cb)
        dd = ds * s
        rs = jnp.sum(dd, axis=1, keepdims=True)
        cs_col = jnp.sum(jnp.where(eye, jnp.sum(dd, axis=0, keepdims=True), 0.0), axis=1, keepdims=True)
        dwi = jnp.sum(dnum * f["qc"], axis=1, keepdims=True) + dden * f["qn"]
        db = rs - cs_col + dwi * w_inter
        dli = cs_col
        ddecay = jnp.sum(jnp.sum(d_c * c_st, axis=1, keepdims=True), axis=0, keepdims=True) \
            + jnp.sum(d_n * n_st, axis=1, keepdims=True)
        dgl = jnp.sum(dkw * ks, axis=1, keepdims=True) * w_k
        dblast = ddecay * decay + jnp.sum(dgl, axis=0, keepdims=True)
        db = db - dgl + jnp.where(row == lc - 1, dblast, 0.0)
        dli = dli + dgl
        db_row = gq["to_row"](db)
        dlf = jnp.sum(jnp.where(c2 >= r2, db_row, 0.0), axis=1, keepdims=True)
        dlf = jnp.where(valid, dlf, 0.0)
        dgate = jnp.where(lane == hh, jnp.where(valid, dli, 0.0), dgate)
        dgate = jnp.where(lane == M_HEADS + hh, dlf / (1.0 + jnp.exp(gq["f_col"])), dgate)
        dqk_ref[sl, hh * dk:(hh + 1) * dk] = dq
        dqk_ref[sl, (M_HEADS + hh) * dk:(M_HEADS + hh + 1) * dk] = dks * scale
        dv_ref[sl, hh * dv:(hh + 1) * dv] = dvv.astype(dv_ref.dtype)
        dc_sc[hh] = decay * d_c + _dot(qb, _bf(wdnum), TN)
        dn_sc[hh] = decay * d_n + _colsum(q * wdden)
        return dgate

    sd = jax.ShapeDtypeStruct
    nh = M_HEADS
    rc = lambda c: nst - 1 - c
    return _pcall(
        body, name="mlstm_bwd", grid=(bsz, nst),
        in_specs=[pl.BlockSpec(memory_space=pl.ANY),
                  pl.BlockSpec((None, rows, nh * dk), lambda b, c: (b, rc(c), 0)),
                  pl.BlockSpec((None, rows, nh * dk), lambda b, c: (b, rc(c), 1)),
                  pl.BlockSpec((None, rows, nh * dv), lambda b, c: (b, rc(c), V_OFF // (nh * dv))),
                  pl.BlockSpec((None, rows, 128), lambda b, c: (b, rc(c), 0)),
                  pl.BlockSpec((None, rows, nh * dv), lambda b, c: (b, rc(c), 0)),
                  pl.BlockSpec((None, nh, cps, dk, dv), lambda b, c: (b, 0, rc(c), 0, 0)),
                  pl.BlockSpec((None, nh, cps, 1, dk), lambda b, c: (b, 0, rc(c), 0, 0)),
                  pl.BlockSpec((None, nh, cps, 1, 128), lambda b, c: (b, 0, rc(c), 0, 0))],
        out_specs=[pl.BlockSpec((None, rows, nh * dv), lambda b, c: (b, rc(c), V_OFF // (nh * dv))),
                   pl.BlockSpec((None, rows, 2 * nh * dk), lambda b, c: (b, rc(c), 0)),
                   pl.BlockSpec((None, rows, 128), lambda b, c: (b, rc(c), 0))],
        out_shape=[sd(dp3.shape, dp3.dtype), sd((bsz, lp, 2 * nh * dk), F32), sd((bsz, lp, 128), dp3.dtype)],
        scratch_shapes=[pltpu.VMEM((nh, dk, dv), F32), pltpu.VMEM((nh, 1, dk), F32)],
        input_output_aliases={0: 0},
        compiler_params=_cp(("arbitrary", "arbitrary")),
    )(dp3, qk3, qk3, p3, pg3, dh3, cs, ns, ms)


def _headnorm(x):
    dv = x.shape[1] // M_HEADS
    xh, rs = [], []
    for h in range(M_HEADS):
        xx = x[:, h * dv:(h + 1) * dv]
        mu = jnp.mean(xx, axis=-1, keepdims=True)
        xc = xx - mu
        rstd = lax.rsqrt(jnp.mean(xc * xc, axis=-1, keepdims=True) + LN_EPS)
        xh.append(xc * rstd)
        rs.append(rstd)
    return jnp.concatenate(xh, axis=1), rs


def _mix_fwd(hm, p, ys5g, h0, gn, wmo_bf, wo_bf, g1, b1, lp):
    r, d = hm.shape
    tm = _row_tile(lp, 208)

    def body(hm_ref, o_ref, gs_ref, gm_ref, ys_ref, h0_ref, gn_ref, wmo_ref, wo_ref, g1_ref, b1_ref,
             ymin_ref, ym_ref, mix_ref, r1_ref, h1_ref):
        xhat, _ = _headnorm(hm_ref[...])
        ymin = _bf(_sig(o_ref[...].astype(F32)) * (xhat * gn_ref[...]))
        ymin_ref[...] = ymin
        ym = _dot(ymin, wmo_ref[...])
        ym_ref[...] = ym
        mix = _bf(_sig(gs_ref[...].astype(F32)) * ys_ref[...] + _sig(gm_ref[...].astype(F32)) * ym)
        mix_ref[...] = mix
        r1 = ALPHA * h0_ref[...] + _dot(mix, wo_ref[...])
        r1_ref[...] = r1
        h1, _, _ = _ln_fwd(r1, g1_ref[...], b1_ref[...])
        h1_ref[...] = h1

    sd = jax.ShapeDtypeStruct
    row = pl.BlockSpec((tm, d), lambda i: (i, 0))
    return _pcall(
        body, name="mix_fwd", grid=(r // tm,),
        in_specs=[row, pl.BlockSpec((tm, d), lambda i: (i, O_OFF // d)), pl.BlockSpec((tm, d), lambda i: (i, GS_OFF // d)),
                  pl.BlockSpec((tm, d), lambda i: (i, GM_OFF // d)), row, row, _const((1, d)),
                  _resident((d, d)), _resident((d, d)), _const((1, d)), _const((1, d))],
        out_specs=[row] * 5,
        out_shape=[sd((r, d), BF16), sd((r, d), F32), sd((r, d), BF16), sd((r, d), F32), sd((r, d), F32)],
        compiler_params=_cp(("parallel",), 48),
    )(hm, p, p, p, ys5g, h0, gn, wmo_bf, wo_bf, g1, b1)


def _mix_bwd(dh1, r1, g1, wo_bf, wmo_bf, p, ys5g, ym, hm, gn, lp):
    r, d = hm.shape
    tm = _row_tile(lp, 208)
    dv = d // M_HEADS

    def body(dh1_ref, r1_ref, g1_ref, wo_ref, wmo_ref, o_ref, gs_ref, gm_ref, ys_ref, ym_ref, hm_ref, gn_ref,
             dr1_ref, dp_ref, dys_ref, dym_ref, dhm_ref, dg1_ref, db1_ref, dgn_ref):
        i = pl.program_id(0)

        @pl.when(i == 0)
        def _():
            dg1_ref[...] = jnp.zeros_like(dg1_ref)
            db1_ref[...] = jnp.zeros_like(db1_ref)
            dgn_ref[...] = jnp.zeros_like(dgn_ref)

        dh1 = dh1_ref[...]
        _, xhat1, rstd1 = _ln_fwd(r1_ref[...], g1_ref[...], 0.0)
        dr1 = _ln_bwd(dh1, xhat1, rstd1, g1_ref[...])
        dr1_ref[...] = dr1
        dg1_ref[...] += _colsum(dh1 * xhat1)
        db1_ref[...] += _colsum(dh1)
        dmix = _dot(_bf(dr1), wo_ref[...], NT)
        sgs, sgm, so = (_sig(gs_ref[...].astype(F32)), _sig(gm_ref[...].astype(F32)), _sig(o_ref[...].astype(F32)))
        dys_ref[...] = dmix * sgs
        dp_ref[:, d:2 * d] = _bf(dmix * ys_ref[...] * sgs * (1.0 - sgs))
        dym = dmix * sgm
        dym_ref[...] = _bf(dym)
        dp_ref[:, 2 * d:3 * d] = _bf(dmix * ym_ref[...] * sgm * (1.0 - sgm))
        dymin = _dot(_bf(dym), wmo_ref[...], NT)
        xhat, rs = _headnorm(hm_ref[...])
        gn_ = gn_ref[...]
        dp_ref[:, 0:d] = _bf(dymin * (xhat * gn_) * so * (1.0 - so))
        dhn = dymin * so
        dgn_ref[...] += _colsum(dhn * xhat)
        dxh = dhn * gn_
        for h in range(M_HEADS):
            sl = slice(h * dv, (h + 1) * dv)
            a, xh = dxh[:, sl], xhat[:, sl]
            m1 = jnp.mean(a, axis=-1, keepdims=True)
            m2 = jnp.mean(a * xh, axis=-1, keepdims=True)
            dhm_ref[:, sl] = rs[h] * (a - m1 - xh * m2)

    sd = jax.ShapeDtypeStruct
    row = pl.BlockSpec((tm, d), lambda i: (i, 0))
    vec = _const((1, d))
    return _pcall(
        body, name="mix_bwd", grid=(r // tm,),
        in_specs=[row, row, vec, _resident((d, d)), _resident((d, d)),
                  pl.BlockSpec((tm, d), lambda i: (i, O_OFF // d)), pl.BlockSpec((tm, d), lambda i: (i, GS_OFF // d)),
                  pl.BlockSpec((tm, d), lambda i: (i, GM_OFF // d)), row, row, row, vec],
        out_specs=[row, pl.BlockSpec((tm, 3 * d), lambda i: (i, 0)), row, row, row, vec, vec, vec],
        out_shape=[sd((r, d), F32), sd((r, NP), BF16), sd((r, d), F32), sd((r, d), BF16), sd((r, d), F32),
                   sd((1, d), F32), sd((1, d), F32), sd((1, d), F32)],
        compiler_params=_cp(("arbitrary",), 48),
    )(dh1, r1, g1, wo_bf, wmo_bf, p, p, p, ys5g, ym, hm, gn)


def _mlp_fwd(h1, tgt, wup_g, wdn_bf, bup, g2, b2, lp):
    r, d = h1.shape
    tm = _row_tile(lp, 352)
    tps = lp // tm
    nf = wup_g.shape[0]

    def body(h1_ref, t_ref, wup_ref, wdn_ref, bup_ref, g2_ref, b2_ref, dr2_ref, act_ref, loss_ref, dg2_ref, db2_ref):
        i = pl.program_id(0)

        @pl.when(i == 0)
        def _():
            loss_ref[...] = jnp.zeros_like(loss_ref)
            dg2_ref[...] = jnp.zeros_like(dg2_ref)
            db2_ref[...] = jnp.zeros_like(db2_ref)

        h1 = h1_ref[...]
        h1b = _bf(h1)
        ff = jnp.zeros((tm, d), F32)
        for s in range(nf):
            up = _dot(h1b, wup_ref[s]) + bup_ref[:, s * d:(s + 1) * d]
            a = jnp.maximum(up, 0.0)
            a = _bf(a * a)
            act_ref[:, s * d:(s + 1) * d] = a
            ff = ff + _dot(a, wdn_ref[s * d:(s + 1) * d, :])
        r2 = ALPHA * h1 + ff
        g2 = g2_ref[...]
        y, xhat, rstd = _ln_fwd(r2, g2, b2_ref[...])
        t = (i % tps) * tm + lax.broadcasted_iota(jnp.int32, (tm, 1), 0)
        diff = jnp.where(t >= PAD + N_META, y - t_ref[...], 0.0)
        loss_ref[...] += 0.5 / d * jnp.sum(jnp.sum(diff * diff, axis=1, keepdims=True), axis=0, keepdims=True)
        dy = diff * (1.0 / d)
        dg2_ref[...] += _colsum(dy * xhat)
        db2_ref[...] += _colsum(dy)
        dr2_ref[...] = _ln_bwd(dy, xhat, rstd, g2)

    sd = jax.ShapeDtypeStruct
    row = pl.BlockSpec((tm, d), lambda i: (i, 0))
    vec = _const((1, d))
    return _pcall(
        body, name="mlp_fwd", grid=(r // tm,),
        in_specs=[row, row, _resident(wup_g.shape), _resident(wdn_bf.shape), _const((1, nf * d)), vec, vec],
        out_specs=[row, pl.BlockSpec((tm, nf * d), lambda i: (i, 0)), _const((1, 128)), vec, vec],
        out_shape=[sd((r, d), F32), sd((r, nf * d), BF16), sd((1, 128), F32), sd((1, d), F32), sd((1, d), F32)],
        compiler_params=_cp(("arbitrary",), 56),
    )(h1, tgt, wup_g, wdn_bf, bup, g2, b2)


def _mlp_bwd(h1, dr2, wup_g, wdn_bf, bup, lp):
    r, d = h1.shape
    tm = _row_tile(lp, 352)
    nf = wup_g.shape[0]

    def body(h1_ref, dr2_ref, wup_ref, wdn_ref, bup_ref, dh1_ref, dup_ref, dbup_ref):
        i = pl.program_id(0)

        @pl.when(i == 0)
        def _():
            dbup_ref[...] = jnp.zeros_like(dbup_ref)

        h1b = _bf(h1_ref[...])
        dr2 = dr2_ref[...]
        dr2b = _bf(dr2)
        acc = ALPHA * dr2
        for s in range(nf):
            up = _dot(h1b, wup_ref[s]) + bup_ref[:, s * d:(s + 1) * d]
            dact = _dot(dr2b, wdn_ref[s * d:(s + 1) * d, :], NT)
            dup = dact * (2.0 * jnp.maximum(up, 0.0))
            dbup_ref[:, s * d:(s + 1) * d] += _colsum(dup)
            dupb = _bf(dup)
            dup_ref[:, s * d:(s + 1) * d] = dupb
            acc = acc + _dot(dupb, wup_ref[s], NT)
        dh1_ref[...] = acc

    sd = jax.ShapeDtypeStruct
    row = pl.BlockSpec((tm, d), lambda i: (i, 0))
    return _pcall(
        body, name="mlp_bwd", grid=(r // tm,),
        in_specs=[row, row, _resident(wup_g.shape), _resident(wdn_bf.shape), _const((1, nf * d))],
        out_specs=[row, pl.BlockSpec((tm, nf * d), lambda i: (i, 0)), _const((1, nf * d))],
        out_shape=[sd((r, d), F32), sd((r, nf * d), BF16), sd((1, nf * d), F32)],
        compiler_params=_cp(("arbitrary",), 56),
    )(h1, dr2, wup_g, wdn_bf, bup)


def _s5_block_mats(bb_re_t, bb_im_t, c_re, c_im, ap_re, ap_im):
    ng = c_re.shape[0]
    gl = ng // S5_KCH
    eye = jnp.eye(gl, dtype=F32)

    def bmat(bt):
        bb = jnp.transpose(bt, (1, 0, 2)).reshape(S5_KCH, gl, S5_GROUP, S5_STATE)
        return jnp.einsum("kghp,gj->kghjp", bb, eye).reshape(S5_KCH, gl * S5_GROUP, gl * S5_STATE)

    def cmat(c):
        cc = c.reshape(S5_KCH, gl, S5_GROUP, S5_STATE)
        return jnp.einsum("kghp,gj->kjpgh", cc, eye).reshape(S5_KCH, gl * S5_STATE, gl * S5_GROUP)

    def pw(a):
        return jnp.transpose(a.reshape(8, S5_KCH, gl * S5_STATE), (1, 0, 2))

    bk = jnp.concatenate([bmat(bb_re_t), bmat(bb_im_t)], axis=-1)
    apow = jnp.concatenate([pw(ap_re), pw(ap_im)], axis=-1)
    return _bf(bk), _bf(cmat(c_re)), _bf(cmat(c_im)), apow


def _s5_block_grads(dbk, dcre, dcim, da):
    gl = dbk.shape[1] // S5_GROUP
    ng = gl * S5_KCH
    eye = jnp.eye(gl, dtype=F32)
    hw = gl * S5_STATE

    def bpart(x):
        x = x.reshape(S5_KCH, gl, S5_GROUP, gl, S5_STATE)
        x = jnp.einsum("kghjp,gj->kghp", x, eye).reshape(ng, S5_GROUP, S5_STATE)
        return jnp.transpose(x, (1, 0, 2))

    def cpart(x):
        x = x.reshape(S5_KCH, gl, S5_STATE, gl, S5_GROUP)
        return jnp.einsum("kjpgh,gj->kghp", x, eye).reshape(ng, S5_GROUP, S5_STATE)

    return (bpart(dbk[..., :hw]), bpart(dbk[..., hw:]), cpart(dcre), cpart(dcim),
            da[:, 0, :hw].reshape(ng, S5_STATE), da[:, 0, hw:].reshape(ng, S5_STATE))


def _tie(a, tok):
    return a if tok is None else a + tok[0, 0]


def _local_step(x, tgt, w, early=None, late=None, ready=None):
    ready = ready or (lambda names, g: None)
    bsz, seq, d = x.shape
    lp = PAD + N_META + seq
    r = bsz * lp
    meta = jnp.broadcast_to(w["meta_tokens"][None], (bsz, N_META, d))
    hin = jnp.concatenate([jnp.zeros((bsz, PAD, d), F32), meta, x], axis=1).reshape(r, d)
    tgtp = jnp.concatenate([jnp.zeros((bsz, PAD + N_META, d), F32), tgt], axis=1).reshape(r, d)

    h0, h0b = _ln0_fwd(hin, w["ln0_g"], w["ln0_b"], lp)
    if early is not None:
        w = {**w, **early((h0, tgtp))}
    p, pg = _inproj(h0b, w["w_in"], w["b_in"], lp)
    p3 = p.reshape(bsz, lp, NP)
    pg3 = pg.reshape(bsz, lp, 128)

    b_re_t = jnp.transpose(w["s5_b_re"], (2, 0, 1))
    b_im_t = jnp.transpose(w["s5_b_im"], (2, 0, 1))
    ap_re, ap_im, bb_re_t, bb_im_t = _s5_prep(w["s5_lambda_re"], w["s5_lambda_im"], w["s5_log_dt"], b_re_t, b_im_t)
    bk, cre, cim, apow = _s5_block_mats(bb_re_t, bb_im_t, w["s5_c_re"], w["s5_c_im"], ap_re, ap_im)
    y_s5, xs = _s5_fwd(p3, bk, cre, cim, apow, w["s5_d"])
    sw = y_s5.shape[-1]
    if late is not None:
        w = {**w, **late(y_s5)}
    gy, z, ys5g = _glu_fwd(y_s5.reshape(r, sw), w["s5_w_glu"], lp)

    pre3, qk3 = _conv_fwd(p3, w["qk_conv_w"], w["qk_conv_b"])
    hm3, cs, ns, ms = _mlstm_fwd(qk3, p3, pg3)
    hm = hm3.reshape(r, d)
    ymin, ym, mix, r1, h1 = _mix_fwd(hm, p, ys5g, h0, w["m_norm_g"], w["m_w_out"], w["w_o"], w["ln1_g"], w["ln1_b"], lp)
    dr2, act, loss, dg2, db2 = _mlp_fwd(h1, tgtp, w["w_up"], w["w_down"], w["b_up"], w["ln2_g"], w["ln2_b"], lp)

    g = {"ln2_g": dg2, "ln2_b": db2}
    dh1, dup, g["b_up"] = _mlp_bwd(h1, dr2, w["w_up"], w["w_down"], w["b_up"], lp)
    g["w_down"] = _mm_tn(act, dr2, name="dw_down")
    g["w_up"] = _mm_tn(h1, dup, name="dw_up", split=w["w_up"].shape[0])
    tok = ready(("w_down", "w_up"), g)
    dr1, dp, dys5g, dym, dhm, g["ln1_g"], g["ln1_b"], g["m_norm_g"] = _mix_bwd(
        dh1, r1, _tie(w["ln1_g"], tok), w["w_o"], w["m_w_out"], p, ys5g, ym, hm, w["m_norm_g"], lp)
    g["w_o"] = _mm_tn(mix, dr1, name="dw_o")
    g["m_w_out"] = _mm_tn(ymin, dym, name="dw_mout")

    dp3 = dp.reshape(bsz, lp, NP)
    dp3, dqk3, dgate = _mlstm_bwd(dp3, qk3, p3, pg3, dhm.reshape(bsz, lp, d), cs, ns, ms)
    dp3, g["qk_conv_w"], g["qk_conv_b"] = _conv_bwd(dp3, p3, dqk3, pre3, w["qk_conv_w"])
    dz, dys5 = _glu_bwd(dys5g, z, y_s5.reshape(r, sw), w["s5_w_glu"], lp)
    g["s5_w_glu"] = _mm_tn(gy, dz, name="dw_glu", split=w["s5_w_glu"].shape[0])
    tok = ready(("s5_w_glu", "m_w_out", "w_o"), g)
    apow_rev = jnp.flip(apow, axis=1)
    dp3, dbk, dcre, dcim, da, g["s5_d"] = _s5_bwd(dp3, p3, dys5.reshape(bsz, lp, sw), xs, bk, cre, cim, apow_rev,
                                                 _tie(w["s5_d"], tok))
    dbb_re_t, dbb_im_t, g["s5_c_re"], g["s5_c_im"], da_re, da_im = _s5_block_grads(dbk, dcre, dcim, da)
    g["s5_lambda_re"], g["s5_lambda_im"], g["s5_log_dt"], gb_re_t, gb_im_t = _s5_prep_bwd(
        w["s5_lambda_re"], w["s5_lambda_im"], w["s5_log_dt"], b_re_t, b_im_t, da_re, da_im, dbb_re_t, dbb_im_t)
    g["s5_b_re"] = jnp.transpose(gb_re_t, (1, 2, 0))
    g["s5_b_im"] = jnp.transpose(gb_im_t, (1, 2, 0))

    dp3 = lax.dynamic_update_slice(dp3, dgate, (0, 0, G_OFF))
    dp = dp3.reshape(r, NP)
    g["w_in"], g["b_in"] = _mm_tn(h0b, dp, name="dw_in", colsum=True)
    tok = ready(("w_in",), g)
    dpw = _mm_nt(dp, w["w_in"], lp, name="dh0", dep=tok)
    dhin, g["ln0_g"], g["ln0_b"], g["meta_tokens"] = _ln0_bwd(hin, dr1, dpw, w["ln0_g"], lp)
    grad_x = dhin.reshape(bsz, lp, d)[:, PAD + N_META:]
    return loss, grad_x, g


_ANY = pl.BlockSpec(memory_space=pl.ANY)
_MESH = pl.DeviceIdType.MESH


def _place():
    return lax.axis_index("x"), lax.axis_index("y"), lax.axis_index("c")


def _gather_chips(shards):
    n = len(shards)

    def body(*refs):
        ins, outs = refs[:n], refs[n:2 * n]
        send, recv, loc = refs[2 * n:]
        x, y, c = _place()
        me = 2 * x + y
        peers = [(1 - x, y), (x, 1 - y), (1 - x, 1 - y)]

        def rc(a, k, slot):
            px, py = peers[k]
            return pltpu.make_async_remote_copy(src_ref=ins[a], dst_ref=outs[a].at[slot], send_sem=send.at[a, k],
                                                recv_sem=recv.at[a, k], device_id=(px, py, c), device_id_type=_MESH)

        own = [pltpu.make_async_copy(ins[a], outs[a].at[me], loc.at[a]) for a in range(n)]
        for cp in own:
            cp.start()
        out = [rc(a, k, me) for a in range(n) for k in range(3)]
        for cp in out:
            cp.start()
        for a in range(n):
            for k in range(3):
                rc(a, k, 2 * peers[k][0] + peers[k][1]).wait_recv()
        for cp in out:
            cp.wait_send()
        for cp in own:
            cp.wait()

    return _pcall(
        body, name="gather_chips", in_specs=[_ANY] * n, out_specs=[_ANY] * n,
        out_shape=[jax.ShapeDtypeStruct((4,) + s.shape, s.dtype) for s in shards],
        scratch_shapes=[pltpu.SemaphoreType.DMA((n, 3)), pltpu.SemaphoreType.DMA((n, 3)), pltpu.SemaphoreType.DMA((n,))],
    )(*shards)


_HBM = pl.BlockSpec(memory_space=pltpu.HBM)
_SEM = pl.BlockSpec(memory_space=pltpu.SEMAPHORE)
_EFFECT = pltpu.SideEffectType.DATAFLOW_SIDE_EFFECTING


def _xchg_copies(srcs, lands, send, recv, scatter):
    x, y, c = _place()
    me = 2 * x + y
    peers = [(1 - x, y), (x, 1 - y), (1 - x, 1 - y)]
    out = []
    for a in range(len(srcs)):
        for k, (px, py) in enumerate(peers):
            src = srcs[a].at[2 * px + py] if scatter else srcs[a]
            dst = lands[a].at[k] if scatter else lands[a].at[me]
            out.append(pltpu.make_async_remote_copy(src_ref=src, dst_ref=dst, send_sem=send.at[3 * a + k],
                                                    recv_sem=recv.at[3 * a + k], device_id=(px, py, c),
                                                    device_id_type=_MESH))
    return out


def _xchg_start(srcs, lands, *, name, scatter, dep=None):
    n = len(srcs)
    deps = [] if dep is None else [dep]
    nd = len(deps)

    def body(*refs):
        send, recv = refs[2 * n + nd], refs[2 * n + nd + 1]
        for cp in _xchg_copies(refs[:n], refs[n:2 * n], send, recv, scatter):
            cp.start()
        refs[-1][...] = jnp.zeros_like(refs[-1])

    hbm = lambda a: pltpu.HBM(a.shape, a.dtype)
    con = lambda a: pltpu.with_memory_space_constraint(a, pltpu.HBM)
    res = _pcall(
        body, name=name, in_specs=[_HBM] * (2 * n) + [_ANY] * nd,
        out_specs=[_SEM, _SEM] + [_HBM] * (2 * n) + [pl.BlockSpec(memory_space=pltpu.VMEM)],
        out_shape=[pltpu.SemaphoreType.DMA((3 * n,)), pltpu.SemaphoreType.DMA((3 * n,))]
        + [hbm(a) for a in srcs] + [hbm(a) for a in lands] + [jax.ShapeDtypeStruct((8, 128), F32)],
        input_output_aliases={i: 2 + i for i in range(2 * n)},
        compiler_params=pltpu.CompilerParams(has_side_effects=_EFFECT),
    )(*[con(a) for a in srcs], *[con(a) for a in lands], *deps)
    return res[0], res[1], list(res[2:2 + n]), list(res[2 + n:2 + 2 * n]), res[-1]


def _xchg_wait(send, recv, srcs, lands, after, *, name, scatter):
    n = len(srcs)
    afters = list(after) if isinstance(after, (list, tuple)) else [after]

    def body(*refs):
        s_ref, r_ref = refs[2 * n], refs[2 * n + 1]
        for cp in _xchg_copies(refs[:n], refs[n:2 * n], s_ref, r_ref, scatter):
            cp.wait_send()
            cp.wait_recv()

    hbm = lambda a: pltpu.HBM(a.shape, a.dtype)
    res = _pcall(
        body, name=name, in_specs=[_HBM] * (2 * n) + [_SEM, _SEM] + [_ANY] * len(afters),
        out_specs=[_HBM] * (2 * n),
        out_shape=[hbm(a) for a in srcs] + [hbm(a) for a in lands],
        input_output_aliases={i: i for i in range(2 * n)},
        compiler_params=pltpu.CompilerParams(has_side_effects=_EFFECT),
    )(*srcs, *lands, send, recv, *afters)
    return list(res[:n]), list(res[n:])


def _swap_cores(arrs, name="swap_cores"):
    n = len(arrs)

    def body(*refs):
        ins, outs = refs[:n], refs[n:2 * n]
        send, recv = refs[2 * n:]
        x, y, c = _place()
        cps = [pltpu.make_async_remote_copy(src_ref=ins[a], dst_ref=outs[a], send_sem=send.at[a], recv_sem=recv.at[a],
                                            device_id=(x, y, 1 - c), device_id_type=_MESH) for a in range(n)]
        for cp in cps:
            cp.start()
        for cp in cps:
            cp.wait_recv()
        for cp in cps:
            cp.wait_send()

    return _pcall(
        body, name=name, in_specs=[_ANY] * n, out_specs=[_ANY] * n,
        out_shape=[jax.ShapeDtypeStruct(s.shape, s.dtype) for s in arrs],
        scratch_shapes=[pltpu.SemaphoreType.DMA((n,)), pltpu.SemaphoreType.DMA((n,))],
    )(*arrs)


def _allreduce_small(v, dep=None):
    rows = v.shape[0]
    half = rows // 2
    assert half % 8 == 0 and 2 * half == rows
    deps = [] if dep is None else [dep]

    def body(v_ref, *rest):
        out_ref, sib_ref, pair_ref, slots_ref, send, recv = rest[len(deps):]
        x, y, c = _place()
        chip = 2 * x + y
        sibling = (x, y, 1 - c)
        peers = [(1 - x, y), (x, 1 - y), (1 - x, 1 - y)]
        mine = pl.ds(pl.multiple_of(c * half, 8), half)

        first = pltpu.make_async_remote_copy(src_ref=v_ref, dst_ref=sib_ref, send_sem=send.at[0], recv_sem=recv.at[0],
                                             device_id=sibling, device_id_type=_MESH)
        first.start()
        first.wait_recv()
        pair_ref[...] = v_ref[...] + sib_ref[...]
        slots_ref[chip] = pair_ref[mine, :]
        cross = [pltpu.make_async_remote_copy(src_ref=pair_ref.at[mine], dst_ref=slots_ref.at[chip],
                                              send_sem=send.at[1 + k], recv_sem=recv.at[1 + k],
                                              device_id=(px, py, c), device_id_type=_MESH)
                 for k, (px, py) in enumerate(peers)]
        for cp in cross:
            cp.start()
        for cp in cross:
            cp.wait_recv()
        out_ref[mine, :] = ((slots_ref[0] + slots_ref[1]) + slots_ref[2]) + slots_ref[3]
        last = pltpu.make_async_remote_copy(src_ref=out_ref.at[mine], dst_ref=out_ref.at[mine], send_sem=send.at[4],
                                            recv_sem=recv.at[4], device_id=sibling, device_id_type=_MESH)
        last.start()
        last.wait_recv()
        first.wait_send()
        for cp in cross:
            cp.wait_send()
        last.wait_send()

    vm = pl.BlockSpec(memory_space=pltpu.VMEM)
    return _pcall(
        body, name="allreduce_small", in_specs=[vm] + [_ANY] * len(deps), out_specs=vm,
        out_shape=jax.ShapeDtypeStruct((rows, 128), F32),
        scratch_shapes=[pltpu.VMEM((rows, 128), F32), pltpu.VMEM((rows, 128), F32), pltpu.VMEM((4, half, 128), F32),
                        pltpu.SemaphoreType.DMA((5,)), pltpu.SemaphoreType.DMA((5,))],
        compiler_params=_cp(None, 40),
    )(v, *deps)


def _sum_slots(own, land):
    ns, rows, cols = land.shape
    tm = _row_tile(rows, 256, 8)

    def body(own_ref, a_ref, o_ref):
        o_ref[...] = ((own_ref[...] + a_ref[0]) + a_ref[1]) + a_ref[2]

    return _pcall(
        body, name="sum_slots", grid=(rows // tm,),
        in_specs=[pl.BlockSpec((tm, cols), lambda i: (i, 0)), pl.BlockSpec((ns, tm, cols), lambda i: (0, i, 0))],
        out_specs=pl.BlockSpec((tm, cols), lambda i: (i, 0)),
        out_shape=jax.ShapeDtypeStruct((rows, cols), F32),
        compiler_params=_cp(("parallel",), 40),
    )(own, land)


def _adamw(w, m, v, g0, g1=None):
    rows, cols = w.shape[-2:]
    lead = w.ndim == 3
    tm = _row_tile(rows, max(8, (1 << 20) // (4 * cols)), 8)
    c1 = 1.0 - ADAM_B1 ** ADAM_STEP
    c2 = 1.0 - ADAM_B2 ** ADAM_STEP
    two = g1 is not None

    def body(*refs):
        w_ref, m_ref, v_ref, g0_ref = refs[:4]
        g_ref, d_ref, nm_ref, nv_ref = refs[-4:]
        g = g0_ref[...]
        if two:
            g = g + refs[4][...]
        nm = ADAM_B1 * m_ref[...] + (1.0 - ADAM_B1) * g
        nv = ADAM_B2 * v_ref[...] + (1.0 - ADAM_B2) * (g * g)
        g_ref[...] = g
        nm_ref[...] = nm
        nv_ref[...] = nv
        d_ref[...] = -ADAM_LR * ((nm / c1) / (jnp.sqrt(nv / c2) + ADAM_EPS) + ADAM_WD * w_ref[...])

    blk = pl.BlockSpec((tm, cols), lambda i: (i, 0))
    wblk = pl.BlockSpec((None, tm, cols), lambda i: (0, i, 0)) if lead else blk
    ins = [w, m, v, g0] + ([g1] if two else [])
    return _pcall(
        body, name="adamw", grid=(rows // tm,), in_specs=[wblk] * 3 + [blk] * (len(ins) - 3), out_specs=[wblk] * 4,
        out_shape=[jax.ShapeDtypeStruct(w.shape, F32)] * 4,
        compiler_params=_cp(("parallel",), 40),
    )(*ins)


_BIG = ("w_in", "s5_w_glu", "m_w_out", "w_o", "w_up", "w_down")
_SMALL = ("ln0_g", "ln0_b", "b_in", "qk_conv_b", "s5_lambda_re", "s5_lambda_im", "s5_log_dt", "s5_b_re", "s5_b_im",
          "s5_c_re", "s5_c_im", "s5_d", "m_norm_g", "ln1_g", "ln1_b", "b_up", "ln2_g", "ln2_b")
_SMALL_SHARDED = ("meta_tokens", "qk_conv_w")
_ORDER = ("meta_tokens", "ln0_g", "ln0_b", "w_in", "b_in", "qk_conv_w", "qk_conv_b", "s5_lambda_re", "s5_lambda_im",
          "s5_log_dt", "s5_b_re", "s5_b_im", "s5_c_re", "s5_c_im", "s5_d", "s5_w_glu", "m_norm_g", "m_w_out", "w_o",
          "ln1_g", "ln1_b", "w_up", "b_up", "w_down", "ln2_g", "ln2_b")


def _pack(arrs):
    flat = jnp.concatenate([a.reshape(-1) for a in arrs])
    n = flat.shape[0]
    rows = -(-n // 2048) * 16
    return jnp.pad(flat, (0, rows * 128 - n)).reshape(rows, 128)


def _unpack(packed, shapes):
    flat = packed.reshape(-1)
    out, off = [], 0
    for s in shapes:
        n = math.prod(s)
        out.append(flat[off:off + n].reshape(s))
        off += n
    return out


def kernel(x, meta_tokens, ln0_g, ln0_b, w_in, b_in, qk_conv_w, qk_conv_b, s5_lambda_re, s5_lambda_im, s5_log_dt, s5_b_re, s5_b_im, s5_c_re, s5_c_im, s5_d, s5_w_glu, m_norm_g, m_w_out, w_o, ln1_g, ln1_b, w_up, b_up, w_down, ln2_g, ln2_b, loss_target, m_meta_tokens, m_ln0_g, m_ln0_b, m_w_in, m_b_in, m_qk_conv_w, m_qk_conv_b, m_s5_lambda_re, m_s5_lambda_im, m_s5_log_dt, m_s5_b_re, m_s5_b_im, m_s5_c_re, m_s5_c_im, m_s5_d, m_s5_w_glu, m_m_norm_g, m_m_w_out, m_w_o, m_ln1_g, m_ln1_b, m_w_up, m_b_up, m_w_down, m_ln2_g, m_ln2_b, v_meta_tokens, v_ln0_g, v_ln0_b, v_w_in, v_b_in, v_qk_conv_w, v_qk_conv_b, v_s5_lambda_re, v_s5_lambda_im, v_s5_log_dt, v_s5_b_re, v_s5_b_im, v_s5_c_re, v_s5_c_im, v_s5_d, v_s5_w_glu, v_m_norm_g, v_m_w_out, v_w_o, v_ln1_g, v_ln1_b, v_w_up, v_b_up, v_w_down, v_ln2_g, v_ln2_b):
    wts = dict(meta_tokens=meta_tokens, ln0_g=ln0_g, ln0_b=ln0_b, w_in=w_in, b_in=b_in, qk_conv_w=qk_conv_w,
               qk_conv_b=qk_conv_b, s5_lambda_re=s5_lambda_re, s5_lambda_im=s5_lambda_im, s5_log_dt=s5_log_dt,
               s5_b_re=s5_b_re, s5_b_im=s5_b_im, s5_c_re=s5_c_re, s5_c_im=s5_c_im, s5_d=s5_d, s5_w_glu=s5_w_glu,
               m_norm_g=m_norm_g, m_w_out=m_w_out, w_o=w_o, ln1_g=ln1_g, ln1_b=ln1_b, w_up=w_up, b_up=b_up,
               w_down=w_down, ln2_g=ln2_g, ln2_b=ln2_b)
    mom = dict(meta_tokens=m_meta_tokens, ln0_g=m_ln0_g, ln0_b=m_ln0_b, w_in=m_w_in, b_in=m_b_in, qk_conv_w=m_qk_conv_w,
               qk_conv_b=m_qk_conv_b, s5_lambda_re=m_s5_lambda_re, s5_lambda_im=m_s5_lambda_im, s5_log_dt=m_s5_log_dt,
               s5_b_re=m_s5_b_re, s5_b_im=m_s5_b_im, s5_c_re=m_s5_c_re, s5_c_im=m_s5_c_im, s5_d=m_s5_d,
               s5_w_glu=m_s5_w_glu, m_norm_g=m_m_norm_g, m_w_out=m_m_w_out, w_o=m_w_o, ln1_g=m_ln1_g, ln1_b=m_ln1_b,
               w_up=m_w_up, b_up=m_b_up, w_down=m_w_down, ln2_g=m_ln2_g, ln2_b=m_ln2_b)
    var = dict(meta_tokens=v_meta_tokens, ln0_g=v_ln0_g, ln0_b=v_ln0_b, w_in=v_w_in, b_in=v_b_in, qk_conv_w=v_qk_conv_w,
               qk_conv_b=v_qk_conv_b, s5_lambda_re=v_s5_lambda_re, s5_lambda_im=v_s5_lambda_im, s5_log_dt=v_s5_log_dt,
               s5_b_re=v_s5_b_re, s5_b_im=v_s5_b_im, s5_c_re=v_s5_c_re, s5_c_im=v_s5_c_im, s5_d=v_s5_d,
               s5_w_glu=v_s5_w_glu, m_norm_g=v_m_norm_g, m_w_out=v_m_w_out, w_o=v_w_o, ln1_g=v_ln1_g, ln1_b=v_ln1_b,
               w_up=v_w_up, b_up=v_b_up, w_down=v_w_down, ln2_g=v_ln2_g, ln2_b=v_ln2_b)
    d = x.shape[-1]
    chip = 2 * lax.axis_index("x") + lax.axis_index("y")

    gw = dict(zip(_SMALL_SHARDED, _gather_chips([meta_tokens, qk_conv_w[0]])))
    own_w_in = _bf(w_in[0])
    fsend, frecv, fsrc, fland, ftok = _xchg_start([own_w_in], [lax.empty((4,) + own_w_in.shape, BF16)],
                                                  name="gather_w_in_start", scatter=False, dep=gw["qk_conv_w"])
    late_names = tuple(n for n in _BIG if n != "w_in")
    cat = lambda a: jnp.transpose(a, (1, 0, 2)).reshape(a.shape[1], 4 * a.shape[2])
    w = dict(
        meta_tokens=cat(gw["meta_tokens"]), ln0_g=ln0_g[None], ln0_b=_tie(ln0_b[None], ftok),
        qk_conv_w=cat(gw["qk_conv_w"]), qk_conv_b=qk_conv_b,
        s5_lambda_re=s5_lambda_re[0], s5_lambda_im=s5_lambda_im[0], s5_log_dt=s5_log_dt[0][:, None],
        s5_b_re=s5_b_re[0], s5_b_im=s5_b_im[0], s5_c_re=s5_c_re[0], s5_c_im=s5_c_im[0], s5_d=s5_d,
        m_norm_g=m_norm_g, ln1_g=ln1_g, ln1_b=ln1_b, b_up=b_up, ln2_g=ln2_g, ln2_b=ln2_b)
    in_flight = {}

    def place_own(src, land):
        return lax.dynamic_update_slice(land, src[None], (chip,) + (0,) * src.ndim)

    def early(after):
        src, land = _xchg_wait(fsend, frecv, fsrc, fland, after, name="gather_w_in_wait", scatter=False)
        late_src = [_bf(wts[n][0]) for n in late_names]
        st = _xchg_start(late_src, [lax.empty((4,) + a.shape, a.dtype) for a in late_src], name="gather_late_start",
                         scatter=False, dep=src[0])
        in_flight["late"] = st[:4]
        return dict(w_in=_w_in_from_slots(place_own(src[0], land[0]), IN_CHUNK), b_in=_tie(_to_pad_cols(b_in), st[4]))

    def late(after):
        src, land = _xchg_wait(*in_flight["late"], after, name="gather_late_wait", scatter=False)
        full = {n: place_own(s, ld) for n, s, ld in zip(late_names, src, land)}
        return dict(s5_w_glu=full["s5_w_glu"], m_w_out=full["m_w_out"].reshape(d, d), w_o=full["w_o"].reshape(d, d),
                    w_up=full["w_up"], w_down=full["w_down"].reshape(4 * d, d))

    flying = []

    def ready(names, g):
        parts = dict(
            w_in=lambda: _slots_from_w_in(g["w_in"][0]), s5_w_glu=lambda: g["s5_w_glu"],
            m_w_out=lambda: g["m_w_out"].reshape(4, d // 4, d), w_o=lambda: g["w_o"].reshape(4, d // 4, d),
            w_up=lambda: g["w_up"], w_down=lambda: g["w_down"].reshape(4, d, d))
        src = [parts[n]() for n in names]
        land = [lax.empty((3,) + a.shape[1:], a.dtype) for a in src]
        st = _xchg_start(src, land, name="scatter_" + names[0] + "_start", scatter=True)
        flying.append((names,) + st[:4])
        return st[4]

    loss, grad_x, g = _local_step(x, loss_target, w, early, late, ready)
    g["b_in"] = _from_pad_cols(g["b_in"])

    res = {}

    def flat(a):
        return jnp.swapaxes(a, -1, -2).reshape(a.shape[:-2] + (-1, 128))

    def unflat(y, shape):
        return jnp.swapaxes(y.reshape(shape[:-2] + (shape[-1], shape[-2])), -1, -2)

    def finish(groups, after, tag):
        mine = {}
        for names, send, recv, src, land in groups:
            src, land = _xchg_wait(send, recv, src, land, after, name="scatter_" + names[0] + "_wait", scatter=True)
            for n, s, ld in zip(names, src, land):
                mine[n] = _sum_slots(lax.dynamic_index_in_dim(s, chip, 0, keepdims=False), ld)
        theirs = _swap_cores(list(mine.values()), name="swap_cores_" + tag)
        for n, t in zip(mine, theirs):
            if n == "w_in":
                res[n] = [unflat(r, wts[n].shape) for r in _adamw(flat(wts[n]), flat(mom[n]), flat(var[n]),
                                                                  flat(mine[n]), flat(t))]
            else:
                res[n] = _adamw(wts[n], mom[n], var[n], mine[n], t)

    finish(flying[:-1], g["ln0_g"], "a")

    small_shapes = [(1, 128)] + [wts[n].shape for n in _SMALL] + [g[n].shape for n in _SMALL_SHARDED]
    packed = _pack([loss] + [g[n] for n in _SMALL] + [g[n] for n in _SMALL_SHARDED])
    tot = _unpack(_allreduce_small(packed, dep=res["w_o"][3]), small_shapes)
    loss_out = tot[0][0, 0]
    gsm = dict(zip(_SMALL + _SMALL_SHARDED, tot[1:]))
    for n in _SMALL_SHARDED:
        cols = wts[n].shape[-1]
        gsm[n] = lax.dynamic_slice_in_dim(gsm[n], chip * cols, cols, axis=1).reshape(wts[n].shape)

    names = _SMALL + _SMALL_SHARDED
    shapes = [wts[n].shape for n in names]
    pk = lambda dct: _pack([dct[n] for n in names])
    small_out = _adamw(pk(wts), pk(mom), pk(var), pk(gsm))
    small_res = [_unpack(r, shapes) for r in small_out]
    for j, n in enumerate(names):
        res[n] = [small_res[q][j] for q in range(4)]
    finish(flying[-1:], small_out[0], "b")

    return (loss_out, grad_x, *[res[n][0] for n in _ORDER], *[res[n][1] for n in _ORDER],
            *[res[n][2] for n in _ORDER], *[res[n][3] for n in _ORDER])
```

```python
import functools
import math

import jax
import jax.numpy as jnp
from jax import lax
from jax.experimental import pallas as pl
from jax.experimental.pallas import tpu as pltpu

F32 = jnp.float32
BF16 = jnp.bfloat16
HI = lax.Precision.HIGHEST

N_META = 16
M_HEADS = 4
M_CHUNK = 128
PAD = M_CHUNK - N_META
CONV_W = 4
HALO_ROWS = 16
S5_GROUP = 16
S5_STATE = 64
S5_KCH = 4
LN_EPS = 1e-5
ALPHA = 2.0 ** 0.25
NEG = -1e30
ADAM_LR, ADAM_B1, ADAM_B2, ADAM_EPS, ADAM_WD, ADAM_STEP = 0.001, 0.9, 0.999, 1e-08, 0.01, 10

O_OFF, GS_OFF, GM_OFF, V_OFF, Q_OFF, K_OFF, U_OFF, G_OFF, NP = 0, 1024, 2048, 3072, 4096, 4608, 5120, 5632, 5760

NN = ((1,), (0,))
NT = ((1,), (1,))
TN = ((0,), (0,))


def _dot(a, b, dims=NN, prec=None):
    return lax.dot_general(a, b, (dims, ((), ())), preferred_element_type=F32, precision=prec)


def _bf(x):
    return x.astype(BF16)


def _sig(x):
    return 0.5 * jnp.tanh(0.5 * x) + 0.5


def _pcall(body, **kw):
    return pl.pallas_call(body, **kw)


def _cp(sem=None, vmem_mb=None):
    kw = {}
    if sem is not None:
        kw["dimension_semantics"] = sem
    if vmem_mb is not None:
        kw["vmem_limit_bytes"] = vmem_mb << 20
    return pltpu.CompilerParams(**kw)


def _row_tile(n, want, mult=16):
    best = None
    for t in range(mult, want + 1, mult):
        if n % t == 0:
            best = t
    assert best is not None, (n, want)
    return best


def _resident(shape):
    nd = len(shape)
    return pl.BlockSpec(shape, lambda *_: (0,) * nd, pipeline_mode=pl.Buffered(1))


def _const(shape):
    nd = len(shape)
    return pl.BlockSpec(shape, lambda *_: (0,) * nd)


def _ln_fwd(x, g, b):
    mu = jnp.mean(x, axis=-1, keepdims=True)
    xc = x - mu
    var = jnp.mean(xc * xc, axis=-1, keepdims=True)
    rstd = lax.rsqrt(var + LN_EPS)
    xhat = xc * rstd
    return xhat * g + b, xhat, rstd


def _ln_bwd(dy, xhat, rstd, g):
    dxh = dy * g
    m1 = jnp.mean(dxh, axis=-1, keepdims=True)
    m2 = jnp.mean(dxh * xhat, axis=-1, keepdims=True)
    return rstd * (dxh - m1 - xhat * m2)


def _colsum(x):
    return jnp.sum(x, axis=0, keepdims=True)


def _to_pad_cols(w):
    u, q, k, v, o, gi, gf, gs, gm = (w[..., 0:512], w[..., 512:1024], w[..., 1024:1536], w[..., 1536:2560],
                                     w[..., 2560:3584], w[..., 3584:3588], w[..., 3588:3592], w[..., 3592:4616],
                                     w[..., 4616:5640])
    z = jnp.zeros(w.shape[:-1] + (NP - G_OFF - 8,), w.dtype)
    return jnp.concatenate([o, gs, gm, v, q, k, u, gi, gf, z], axis=-1)


def _from_pad_cols(w):
    o, gs, gm, v, q, k, u = (w[..., O_OFF:GS_OFF], w[..., GS_OFF:GM_OFF], w[..., GM_OFF:V_OFF], w[..., V_OFF:Q_OFF],
                             w[..., Q_OFF:K_OFF], w[..., K_OFF:U_OFF], w[..., U_OFF:G_OFF])
    gi, gf = w[..., G_OFF:G_OFF + 4], w[..., G_OFF + 4:G_OFF + 8]
    return jnp.concatenate([u, q, k, v, o, gi, gf, gs, gm], axis=-1)


_IN_REF = (("u", 512), ("q", 512), ("k", 512), ("v", 1024), ("o", 1024), ("i", 4), ("f", 4), ("gs", 1024), ("gm", 1024))
_IN_PAD = (("o", O_OFF), ("gs", GS_OFF), ("gm", GM_OFF), ("v", V_OFF), ("q", Q_OFF), ("k", K_OFF), ("u", U_OFF),
           ("i", G_OFF), ("f", G_OFF + 4))


def _in_ref_ranges():
    out, off = {}, 0
    for n, s in _IN_REF:
        out[n] = (off, off + s)
        off += s
    return out, off


def _w_in_from_slots(g, chunk=None):
    rng, total = _in_ref_ranges()
    width = total // g.shape[0]
    cols = []
    for n, _ in _IN_PAD:
        a, b = rng[n]
        while a < b:
            s = a // width
            e = min(b, (s + 1) * width)
            cols.append(g[s][:, a - s * width:e - s * width])
            a = e
    cols.append(jnp.zeros((g.shape[1], NP - G_OFF - 8), g.dtype))
    if chunk is None:
        return jnp.concatenate(cols, axis=1)
    chunks, cur, room = [], [], chunk
    for c in cols:
        while c.shape[1] > 0:
            take = min(room, c.shape[1])
            cur.append(c[:, :take])
            c, room = c[:, take:], room - take
            if room == 0:
                chunks.append(jnp.concatenate(cur, axis=1))
                cur, room = [], chunk
    assert not cur
    return jnp.stack(chunks, axis=0)


def _slots_from_w_in(wp, nslot=4):
    rng, total = _in_ref_ranges()
    width = total // nslot
    pad_off = dict(_IN_PAD)
    slots = []
    for s in range(nslot):
        lo, hi = s * width, (s + 1) * width
        cols = []
        for n, _ in _IN_REF:
            a, b = rng[n]
            x0, x1 = max(a, lo), min(b, hi)
            if x0 < x1:
                cols.append(wp[:, pad_off[n] + x0 - a:pad_off[n] + x1 - a])
        slots.append(jnp.concatenate(cols, axis=1))
    return jnp.stack(slots, axis=0)


def _ln0_fwd(hin, g, b, lp):
    r, d = hin.shape
    tm = _row_tile(lp, 416)

    def body(x_ref, g_ref, b_ref, o_ref, ob_ref):
        y, _, _ = _ln_fwd(x_ref[...], g_ref[...], b_ref[...])
        o_ref[...] = y
        ob_ref[...] = _bf(y)

    row = pl.BlockSpec((tm, d), lambda i: (i, 0))
    return _pcall(
        body, name="ln0_fwd", grid=(r // tm,),
        in_specs=[row, _const((1, d)), _const((1, d))],
        out_specs=[row, row],
        out_shape=[jax.ShapeDtypeStruct((r, d), F32), jax.ShapeDtypeStruct((r, d), BF16)],
        compiler_params=_cp(("parallel",)),
    )(hin, g, b)


def _ln0_bwd(hin, dr1, dpw, g, lp):
    r, d = hin.shape
    tm = _row_tile(lp, 416)
    tps = lp // tm
    assert tm >= PAD + N_META

    def body(x_ref, a_ref, c_ref, g_ref, o_ref, dg_ref, db_ref, dm_ref):
        i = pl.program_id(0)

        @pl.when(i == 0)
        def _():
            dg_ref[...] = jnp.zeros_like(dg_ref)
            db_ref[...] = jnp.zeros_like(db_ref)
            dm_ref[...] = jnp.zeros_like(dm_ref)

        dy = ALPHA * a_ref[...] + c_ref[...]
        _, xhat, rstd = _ln_fwd(x_ref[...], g_ref[...], 0.0)
        dx = _ln_bwd(dy, xhat, rstd, g_ref[...])
        o_ref[...] = dx
        dg_ref[...] += _colsum(dy * xhat)
        db_ref[...] += _colsum(dy)

        @pl.when(i % tps == 0)
        def _():
            dm_ref[...] += dx[PAD:PAD + N_META, :]

    return _pcall(
        body, name="ln0_bwd", grid=(r // tm,),
        in_specs=[pl.BlockSpec((tm, d), lambda i: (i, 0))] * 3 + [_const((1, d))],
        out_specs=[pl.BlockSpec((tm, d), lambda i: (i, 0)), _const((1, d)), _const((1, d)), _const((N_META, d))],
        out_shape=[jax.ShapeDtypeStruct((r, d), F32), jax.ShapeDtypeStruct((1, d), F32),
                   jax.ShapeDtypeStruct((1, d), F32), jax.ShapeDtypeStruct((N_META, d), F32)],
        compiler_params=_cp(("arbitrary",)),
    )(hin, dr1, dpw, g)


IN_CHUNK = 1152


def _chunk_cols(w):
    k, n = w.shape
    return jnp.transpose(w.reshape(k, n // IN_CHUNK, IN_CHUNK), (1, 0, 2))


def _inproj(h0b, w3, bias, lp):
    r, d = h0b.shape
    nj, _, tn = w3.shape
    tm = _row_tile(lp, 832)
    tps = lp // tm

    def body(a_ref, w_ref, b_ref, o_ref, gate_ref):
        i = pl.program_id(0)
        j = pl.program_id(1)
        acc = _dot(a_ref[...], w_ref[j]) + b_ref[...]
        t = (i % tps) * tm + lax.broadcasted_iota(jnp.int32, (tm, 1), 0)
        acc = jnp.where(t >= PAD, acc, 0.0)
        o_ref[...] = _bf(acc)

        @pl.when(j == nj - 1)
        def _():
            gate_ref[...] = acc[:, tn - 128:]

    return _pcall(
        body, name="inproj", grid=(r // tm, nj),
        in_specs=[pl.BlockSpec((tm, d), lambda i, j: (i, 0)), _resident(w3.shape),
                  pl.BlockSpec((1, tn), lambda i, j: (0, j))],
        out_specs=[pl.BlockSpec((tm, tn), lambda i, j: (i, j)), pl.BlockSpec((tm, 128), lambda i, j: (i, 0))],
        out_shape=[jax.ShapeDtypeStruct((r, nj * tn), BF16), jax.ShapeDtypeStruct((r, 128), F32)],
        compiler_params=_cp(("parallel", "arbitrary"), 48),
    )(h0b, w3, bias)


def _mm_tn(a, b, *, name, split=1, colsum=False, tk_want=1408):
    r, m = a.shape
    n = b.shape[1]
    tk = _row_tile(r, tk_want)
    tm = min(m, 1024)
    ns = n // split
    tn = ns
    for cand in (1024, 1152, 640, 512, 128):
        if ns % cand == 0 and cand <= ns:
            tn = cand
            break
    nb = ns // tn
    nk = r // tk

    def body(a_ref, b_ref, o_ref, *rest):
        acc = rest[-1]
        k = pl.program_id(2)

        @pl.when(k == 0)
        def _():
            acc[...] = jnp.zeros_like(acc)

        bt = b_ref[...]
        acc[...] += _dot(_bf(a_ref[...]), _bf(bt), TN)

        @pl.when(k == nk - 1)
        def _():
            o_ref[...] = acc[...]

        if colsum:
            cs_ref = rest[0]

            @pl.when(k == 0)
            def _():
                cs_ref[...] = jnp.zeros_like(cs_ref)

            cs_ref[...] += _colsum(bt.astype(F32))

    out_specs = [pl.BlockSpec((None, tm, tn), lambda i, j, k: (j // nb, i, j % nb))]
    out_shape = [jax.ShapeDtypeStruct((split, m, ns), F32)]
    if colsum:
        assert m == tm
        out_specs.append(pl.BlockSpec((1, tn), lambda i, j, k: (0, j)))
        out_shape.append(jax.ShapeDtypeStruct((1, n), F32))
    res = _pcall(
        body, name=name, grid=(m // tm, n // tn, nk),
        in_specs=[pl.BlockSpec((tk, tm), lambda i, j, k: (k, i)), pl.BlockSpec((tk, tn), lambda i, j, k: (k, j))],
        out_specs=out_specs, out_shape=out_shape,
        scratch_shapes=[pltpu.VMEM((tm, tn), F32)],
        compiler_params=_cp(("parallel", "parallel", "arbitrary"), 56),
    )(a, b)
    return res if colsum else res[0]


def _mm_nt(a, w3, lp, *, name, dep=None):
    r, kdim = a.shape
    nk, n, tk = w3.shape
    assert nk * tk == kdim
    tm = _row_tile(lp, 832)
    deps = [] if dep is None else [dep]

    def body(a_ref, w_ref, *rest):
        o_ref, acc = rest[-2:]
        k = pl.program_id(1)

        @pl.when(k == 0)
        def _():
            acc[...] = jnp.zeros_like(acc)

        acc[...] += _dot(_bf(a_ref[...]), w_ref[k], NT)

        @pl.when(k == nk - 1)
        def _():
            o_ref[...] = acc[...]

    return _pcall(
        body, name=name, grid=(r // tm, nk),
        in_specs=[pl.BlockSpec((tm, tk), lambda i, k: (i, k)), _resident(w3.shape)]
        + [_const(dp_.shape) for dp_ in deps],
        out_specs=pl.BlockSpec((tm, n), lambda i, k: (i, 0)),
        out_shape=jax.ShapeDtypeStruct((r, n), F32),
        scratch_shapes=[pltpu.VMEM((tm, n), F32)],
        compiler_params=_cp(("parallel", "arbitrary"), 48),
    )(a, w3, *deps)


def _s5_prep(lam_re, lam_im, log_dt, b_re_t, b_im_t):
    g, p = lam_re.shape
    h = b_re_t.shape[0]

    def body(lr_ref, li_ref, ldt_ref, br_ref, bi_ref, pr_ref, pi_ref, bbr_ref, bbi_ref):
        lr, li = lr_ref[...], li_ref[...]
        dt = jnp.exp(ldt_ref[...])
        e = jnp.exp(lr * dt)
        ar, ai = e * jnp.cos(li * dt), e * jnp.sin(li * dt)
        den = lr * lr + li * li
        cr = ((ar - 1.0) * lr + ai * li) / den
        ci = (ai * lr - (ar - 1.0) * li) / den
        br, bi = br_ref[...], bi_ref[...]
        bbr_ref[...] = cr[None] * br - ci[None] * bi
        bbi_ref[...] = cr[None] * bi + ci[None] * br
        xr, xi = ar, ai
        pr_ref[0] = xr
        pi_ref[0] = xi
        for t in range(1, 8):
            xr, xi = xr * ar - xi * ai, xr * ai + xi * ar
            pr_ref[t] = xr
            pi_ref[t] = xi

    sd = jax.ShapeDtypeStruct
    return _pcall(body, name="s5_prep",
                  out_shape=[sd((8, g, p), F32), sd((8, g, p), F32), sd((h, g, p), F32), sd((h, g, p), F32)])(
        lam_re, lam_im, log_dt, b_re_t, b_im_t)


def _s5_prep_bwd(lam_re, lam_im, log_dt, b_re_t, b_im_t, da_re, da_im, dbb_re_t, dbb_im_t):
    g, p = lam_re.shape
    h = b_re_t.shape[0]

    def body(lr_ref, li_ref, ldt_ref, br_ref, bi_ref, dar_ref, dai_ref, dbr_ref, dbi_ref,
             glr_ref, gli_ref, gdt_ref, gbr_ref, gbi_ref):
        lr, li = lr_ref[...], li_ref[...]
        dt = jnp.exp(ldt_ref[...])
        e = jnp.exp(lr * dt)
        ar, ai = e * jnp.cos(li * dt), e * jnp.sin(li * dt)
        den = lr * lr + li * li
        cr = ((ar - 1.0) * lr + ai * li) / den
        ci = (ai * lr - (ar - 1.0) * li) / den
        br, bi = br_ref[...], bi_ref[...]
        gr, gi = dbr_ref[...], dbi_ref[...]
        gbr_ref[...] = gr * cr[None] + gi * ci[None]
        gbi_ref[...] = gi * cr[None] - gr * ci[None]
        gcr = jnp.sum(gr * br + gi * bi, axis=0)
        gci = jnp.sum(gi * br - gr * bi, axis=0)
        ilr, ili = lr / den, -li / den
        gar = dar_ref[...] + gcr * ilr + gci * ili
        gai = dai_ref[...] + gci * ilr - gcr * ili
        qr, qi = cr * ilr - ci * ili, cr * ili + ci * ilr
        glr = -(gcr * qr + gci * qi)
        gli = -(gci * qr - gcr * qi)
        gzr = gar * ar + gai * ai
        gzi = gai * ar - gar * ai
        glr_ref[...] = glr + gzr * dt
        gli_ref[...] = gli + gzi * dt
        gdt_ref[...] = jnp.sum(gzr * lr + gzi * li, axis=1, keepdims=True) * dt

    sd = jax.ShapeDtypeStruct
    return _pcall(body, name="s5_prep_bwd",
                  out_shape=[sd((g, p), F32), sd((g, p), F32), sd((g, 1), F32), sd((h, g, p), F32), sd((h, g, p), F32)])(
        lam_re, lam_im, log_dt, b_re_t, b_im_t, da_re, da_im, dbb_re_t, dbb_im_t)


def _cmul(xr, xi, yr, yi):
    return xr * yr - xi * yi, xr * yi + xi * yr


def _dot5(a, b, dims=NN):
    return _dot(_bf(a), _bf(b), dims)


def _s5_fwd(p3, bk, cre, cim, apow, dskip):
    bsz, lp, _ = p3.shape
    tt = _row_tile(lp, 528, 8)
    nt = lp // tt
    nblk = tt // 8
    hw = 512

    def body(u_ref, bk_ref, cre_ref, cim_ref, ap_ref, d_ref, y_ref, xs_ref, car_ref):
        t = pl.program_id(2)

        @pl.when(t == 0)
        def _():
            car_ref[...] = jnp.zeros_like(car_ref)

        u = u_ref[...].astype(F32)
        xs_ref[...] = _dot5(u, bk_ref[...])
        ap = ap_ref[...]
        apr, api = ap[:, :hw], ap[:, hw:]
        rows = lax.broadcasted_iota(jnp.int32, (8, hw), 0)
        lev = [(d, jnp.where(rows < d, 0.0, jnp.broadcast_to(apr[d - 1:d, :], (8, hw))),
                jnp.where(rows < d, 0.0, jnp.broadcast_to(api[d - 1:d, :], (8, hw)))) for d in (1, 2, 4)]

        def blk(i, carry):
            cr, ci = carry
            off = pl.multiple_of(i * 8, 8)
            x = xs_ref[pl.ds(off, 8), :]
            xr, xi = x[:, :hw], x[:, hw:]
            for d, lr, li in lev:
                mr, mi = _cmul(pltpu.roll(xr, d, 0), pltpu.roll(xi, d, 0), lr, li)
                xr, xi = xr + mr, xi + mi
            mr, mi = _cmul(apr, api, cr, ci)
            xr, xi = xr + mr, xi + mi
            xs_ref[pl.ds(off, 8), :] = jnp.concatenate([xr, xi], axis=1)
            return xr[7:8, :], xi[7:8, :]

        c0 = car_ref[...]
        cr, ci = lax.fori_loop(0, nblk, blk, (c0[0:1, :hw], c0[0:1, hw:]))
        car_ref[...] = jnp.broadcast_to(jnp.concatenate([cr, ci], axis=1), car_ref.shape)
        xs = xs_ref[...]
        y_ref[...] = (_dot5(xs[:, :hw], cre_ref[...]) - _dot5(xs[:, hw:], cim_ref[...])
                      + d_ref[...] * u)

    ub = U_OFF // 128
    return _pcall(
        body, name="s5_fwd", grid=(S5_KCH, bsz, nt),
        in_specs=[pl.BlockSpec((None, tt, 128), lambda k, b, t: (b, t, ub + k)),
                  pl.BlockSpec((None, 128, 2 * hw), lambda k, b, t: (k, 0, 0)),
                  pl.BlockSpec((None, hw, 128), lambda k, b, t: (k, 0, 0)),
                  pl.BlockSpec((None, hw, 128), lambda k, b, t: (k, 0, 0)),
                  pl.BlockSpec((None, 8, 2 * hw), lambda k, b, t: (k, 0, 0)),
                  pl.BlockSpec((1, 128), lambda k, b, t: (0, k))],
        out_specs=[pl.BlockSpec((None, tt, 128), lambda k, b, t: (b, t, k)),
                   pl.BlockSpec((None, None, tt, 2 * hw), lambda k, b, t: (b, k, t, 0))],
        out_shape=[jax.ShapeDtypeStruct((bsz, lp, S5_KCH * 128), F32),
                   jax.ShapeDtypeStruct((bsz, S5_KCH, lp, 2 * hw), F32)],
        scratch_shapes=[pltpu.VMEM((8, 2 * hw), F32)],
        compiler_params=_cp(("parallel", "parallel", "arbitrary"), 40),
    )(p3, bk, cre, cim, apow, dskip)


def _s5_bwd(dp3, p3, dy3, xs, bk, cre, cim, apow_rev, dskip):
    bsz, lp, _ = p3.shape
    tt = _row_tile(lp, 528, 8)
    nt = lp // tt
    nblk = tt // 8
    hw = 512
    tb = tt // 8

    def body(dp_any, u_ref, dy_ref, xs_ref, halo_ref, bkt_ref, cre_ref, cim_ref, ap_ref, d_ref,
             du_ref, dbk_ref, dcre_ref, dcim_ref, da_ref, dd_ref, g_ref, ext_ref, car_ref):
        b = pl.program_id(1)
        t = pl.program_id(2)
        tidx = nt - 1 - t

        @pl.when(t == 0)
        def _():
            car_ref[...] = jnp.zeros_like(car_ref)

        @pl.when((b == 0) & (t == 0))
        def _():
            dbk_ref[...] = jnp.zeros_like(dbk_ref)
            dcre_ref[...] = jnp.zeros_like(dcre_ref)
            dcim_ref[...] = jnp.zeros_like(dcim_ref)
            da_ref[...] = jnp.zeros_like(da_ref)
            dd_ref[...] = jnp.zeros_like(dd_ref)

        u = u_ref[...].astype(F32)
        dy = dy_ref[...]
        g_ref[:, :hw] = _dot5(dy, cre_ref[...])
        g_ref[:, hw:] = -_dot5(dy, cim_ref[...])
        ap = ap_ref[...]
        apr, api = ap[:, :hw], -ap[:, hw:]
        rows = lax.broadcasted_iota(jnp.int32, (8, hw), 0)
        lev = [(d, jnp.where(rows >= 8 - d, 0.0, jnp.broadcast_to(apr[8 - d:9 - d, :], (8, hw))),
                jnp.where(rows >= 8 - d, 0.0, jnp.broadcast_to(api[8 - d:9 - d, :], (8, hw)))) for d in (1, 2, 4)]

        def blk(i, carry):
            cr, ci = carry
            off = pl.multiple_of((nblk - 1 - i) * 8, 8)
            x = g_ref[pl.ds(off, 8), :]
            xr, xi = x[:, :hw], x[:, hw:]
            for d, lr, li in lev:
                mr, mi = _cmul(pltpu.roll(xr, 8 - d, 0), pltpu.roll(xi, 8 - d, 0), lr, li)
                xr, xi = xr + mr, xi + mi
            mr, mi = _cmul(apr, api, cr, ci)
            xr, xi = xr + mr, xi + mi
            g_ref[pl.ds(off, 8), :] = jnp.concatenate([xr, xi], axis=1)
            return xr[0:1, :], xi[0:1, :]

        c0 = car_ref[...]
        cr, ci = lax.fori_loop(0, nblk, blk, (c0[0:1, :hw], c0[0:1, hw:]))
        car_ref[...] = jnp.broadcast_to(jnp.concatenate([cr, ci], axis=1), car_ref.shape)

        gg = g_ref[...]
        du = _dot5(gg, bkt_ref[...]) + d_ref[...] * dy
        trow = tidx * tt + lax.broadcasted_iota(jnp.int32, (tt, 1), 0)
        du_ref[...] = jnp.where(trow >= PAD, du, 0.0).astype(du_ref.dtype)
        dbk_ref[...] += _dot5(u, gg, TN)
        xsv = xs_ref[...]
        dcre_ref[...] += _dot5(dy, xsv[:, :hw], TN)
        dcim_ref[...] -= _dot5(dy, xsv[:, hw:], TN)
        dd_ref[...] += _colsum(dy * u)
        ext_ref[0:8, :] = jnp.where(tidx == 0, 0.0, halo_ref[...])
        ext_ref[8:, :] = xsv
        xp = ext_ref[pl.ds(7, tt), :]
        gr, gi, pr, pi = gg[:, :hw], gg[:, hw:], xp[:, :hw], xp[:, hw:]
        da_ref[:, :hw] += _colsum(gr * pr + gi * pi)
        da_ref[:, hw:] += _colsum(gi * pr - gr * pi)

    ub = U_OFF // 128
    sd = jax.ShapeDtypeStruct
    rt = lambda t: nt - 1 - t
    tr = lambda a: jnp.swapaxes(a, 1, 2)
    res = _pcall(
        body, name="s5_bwd", grid=(S5_KCH, bsz, nt),
        in_specs=[pl.BlockSpec(memory_space=pl.ANY),
                  pl.BlockSpec((None, tt, 128), lambda k, b, t: (b, rt(t), ub + k)),
                  pl.BlockSpec((None, tt, 128), lambda k, b, t: (b, rt(t), k)),
                  pl.BlockSpec((None, None, tt, 2 * hw), lambda k, b, t: (b, k, rt(t), 0)),
                  pl.BlockSpec((None, None, 8, 2 * hw), lambda k, b, t: (b, k, jnp.maximum(rt(t) * tb - 1, 0), 0)),
                  pl.BlockSpec((None, 2 * hw, 128), lambda k, b, t: (k, 0, 0)),
                  pl.BlockSpec((None, 128, hw), lambda k, b, t: (k, 0, 0)),
                  pl.BlockSpec((None, 128, hw), lambda k, b, t: (k, 0, 0)),
                  pl.BlockSpec((None, 8, 2 * hw), lambda k, b, t: (k, 0, 0)),
                  pl.BlockSpec((1, 128), lambda k, b, t: (0, k))],
        out_specs=[pl.BlockSpec((None, tt, 128), lambda k, b, t: (b, rt(t), ub + k)),
                   pl.BlockSpec((None, 128, 2 * hw), lambda k, b, t: (k, 0, 0)),
                   pl.BlockSpec((None, 128, hw), lambda k, b, t: (k, 0, 0)),
                   pl.BlockSpec((None, 128, hw), lambda k, b, t: (k, 0, 0)),
                   pl.BlockSpec((None, 1, 2 * hw), lambda k, b, t: (k, 0, 0)),
                   pl.BlockSpec((1, 128), lambda k, b, t: (0, k))],
        out_shape=[sd(dp3.shape, dp3.dtype), sd((S5_KCH, 128, 2 * hw), F32), sd((S5_KCH, 128, hw), F32),
                   sd((S5_KCH, 128, hw), F32), sd((S5_KCH, 1, 2 * hw), F32), sd((1, S5_KCH * 128), F32)],
        scratch_shapes=[pltpu.VMEM((tt, 2 * hw), F32), pltpu.VMEM((tt + 8, 2 * hw), F32), pltpu.VMEM((8, 2 * hw), F32)],
        input_output_aliases={0: 0},
        compiler_params=_cp(("arbitrary", "arbitrary", "arbitrary"), 48),
    )(dp3, p3, dy3, xs, xs, tr(bk), tr(cre), tr(cim), apow_rev, dskip)
    return res[0], res[1], tr(res[2]), tr(res[3]), res[4], res[5]


_G0 = math.sqrt(2.0 / math.pi)
_G1 = 0.044715


def _gelu(y):
    return 0.5 * y * (1.0 + jnp.tanh(_G0 * (y + _G1 * y * y * y)))


def _gelu_grad(y):
    th = jnp.tanh(_G0 * (y + _G1 * y * y * y))
    return 0.5 * (1.0 + th) + 0.5 * y * (1.0 - th * th) * _G0 * (1.0 + 3.0 * _G1 * y * y)


def _glu_fwd(y_s5, wglu_g, lp):
    r, w = y_s5.shape
    tm = _row_tile(lp, 416)
    cw = wglu_g.shape[2]

    def body(y_ref, w_ref, gy_ref, z_ref, o_ref):
        gy = _bf(_gelu(y_ref[...]))
        gy_ref[...] = gy
        zs = [_dot(gy, w_ref[s]) for s in range(4)]
        for s in range(4):
            z_ref[:, s * cw:(s + 1) * cw] = zs[s]
        o_ref[:, :cw] = zs[0] * _sig(zs[2])
        o_ref[:, cw:] = zs[1] * _sig(zs[3])

    sd = jax.ShapeDtypeStruct
    return _pcall(
        body, name="glu_fwd", grid=(r // tm,),
        in_specs=[pl.BlockSpec((tm, w), lambda i: (i, 0)), _resident(wglu_g.shape)],
        out_specs=[pl.BlockSpec((tm, w), lambda i: (i, 0)), pl.BlockSpec((tm, 4 * cw), lambda i: (i, 0)),
                   pl.BlockSpec((tm, 2 * cw), lambda i: (i, 0))],
        out_shape=[sd((r, w), BF16), sd((r, 4 * cw), F32), sd((r, 2 * cw), F32)],
        compiler_params=_cp(("parallel",), 40),
    )(y_s5, wglu_g)


def _glu_bwd(dyg, z, y_s5, wglu_g, lp):
    r, w = y_s5.shape
    tm = _row_tile(lp, 416)
    cw = wglu_g.shape[2]

    def body(d_ref, z_ref, y_ref, w_ref, dz_ref, dy_ref):
        d = d_ref[...]
        zz = z_ref[...]
        acc = jnp.zeros((tm, w), F32)
        for s in range(2):
            z1 = zz[:, s * cw:(s + 1) * cw]
            sg = _sig(zz[:, (2 + s) * cw:(3 + s) * cw])
            dd = d[:, s * cw:(s + 1) * cw]
            dz1 = _bf(dd * sg)
            dz2 = _bf(dd * z1 * sg * (1.0 - sg))
            dz_ref[:, s * cw:(s + 1) * cw] = dz1
            dz_ref[:, (2 + s) * cw:(3 + s) * cw] = dz2
            acc += _dot(dz1, w_ref[s], NT) + _dot(dz2, w_ref[2 + s], NT)
        dy_ref[...] = acc * _gelu_grad(y_ref[...])

    sd = jax.ShapeDtypeStruct
    return _pcall(
        body, name="glu_bwd", grid=(r // tm,),
        in_specs=[pl.BlockSpec((tm, 2 * cw), lambda i: (i, 0)), pl.BlockSpec((tm, 4 * cw), lambda i: (i, 0)),
                  pl.BlockSpec((tm, w), lambda i: (i, 0)), _resident(wglu_g.shape)],
        out_specs=[pl.BlockSpec((tm, 4 * cw), lambda i: (i, 0)), pl.BlockSpec((tm, w), lambda i: (i, 0))],
        out_shape=[sd((r, 4 * cw), BF16), sd((r, w), F32)],
        compiler_params=_cp(("parallel",), 40),
    )(dyg, z, y_s5, wglu_g)


def _conv_fwd(p3, cw, cb):
    bsz, lp, _ = p3.shape
    tt = _row_tile(lp, 416)
    nt = lp // tt
    tb = tt // 8
    c = cw.shape[1]
    qb = Q_OFF // c

    hr = HALO_ROWS
    off = hr - (CONV_W - 1)

    def body(x_ref, halo_ref, w_ref, b_ref, pre_ref, act_ref, ext_ref):
        t = pl.program_id(1)
        ext_ref[0:hr, :] = jnp.where(t == 0, 0.0, halo_ref[...].astype(F32))
        ext_ref[hr:, :] = x_ref[...].astype(F32)
        w = w_ref[...]
        acc = b_ref[...] + w[0:1, :] * ext_ref[pl.ds(off, tt), :]
        for j in range(1, CONV_W):
            acc = acc + w[j:j + 1, :] * ext_ref[pl.ds(off + j, tt), :]
        pre_ref[...] = acc
        act_ref[...] = acc * _sig(acc)

    sd = jax.ShapeDtypeStruct
    return _pcall(
        body, name="conv_fwd", grid=(bsz, nt),
        in_specs=[pl.BlockSpec((None, tt, c), lambda b, t: (b, t, qb)),
                  pl.BlockSpec((None, hr, c), lambda b, t: (b, jnp.maximum(t * (tt // hr) - 1, 0), qb)),
                  _const((CONV_W, c)), _const((1, c))],
        out_specs=[pl.BlockSpec((None, tt, c), lambda b, t: (b, t, 0))] * 2,
        out_shape=[sd((bsz, lp, c), F32)] * 2,
        scratch_shapes=[pltpu.VMEM((tt + hr, c), F32)],
        compiler_params=_cp(("parallel", "parallel")),
    )(p3, p3, cw, cb)


def _conv_bwd(dp3, p3, dact3, pre3, cw):
    bsz, lp, _ = p3.shape
    tt = _row_tile(lp, 416)
    nt = lp // tt
    tb = tt // 8
    c = cw.shape[1]
    qb = Q_OFF // c

    hr = HALO_ROWS
    off = hr - (CONV_W - 1)

    def silu_grad(x):
        s = _sig(x)
        return s * (1.0 + x * (1.0 - s))

    def body(dp_any, x_ref, xh_ref, d_ref, dh_ref, pre_ref, preh_ref, w_ref, o_ref, dw_ref, db_ref, ext_ref, dext_ref):
        b = pl.program_id(0)
        t = pl.program_id(1)

        @pl.when((b == 0) & (t == 0))
        def _():
            dw_ref[...] = jnp.zeros_like(dw_ref)
            db_ref[...] = jnp.zeros_like(db_ref)

        dc = d_ref[...] * silu_grad(pre_ref[...])
        dch = jnp.where(t == nt - 1, 0.0, dh_ref[...] * silu_grad(preh_ref[...]))
        dext_ref[0:tt, :] = dc
        dext_ref[tt:, :] = dch
        ext_ref[0:hr, :] = jnp.where(t == 0, 0.0, xh_ref[...].astype(F32))
        ext_ref[hr:, :] = x_ref[...].astype(F32)
        w = w_ref[...]
        acc = w[CONV_W - 1:CONV_W, :] * dc
        for j in range(CONV_W - 1):
            acc = acc + w[j:j + 1, :] * dext_ref[pl.ds(CONV_W - 1 - j, tt), :]
        trow = t * tt + lax.broadcasted_iota(jnp.int32, (tt, 1), 0)
        o_ref[...] = jnp.where(trow >= PAD, acc, 0.0).astype(o_ref.dtype)
        db_ref[...] += _colsum(dc)
        for j in range(CONV_W):
            dw_ref[j:j + 1, :] += _colsum(dc * ext_ref[pl.ds(off + j, tt), :])

    sd = jax.ShapeDtypeStruct
    nxt = lambda t: jnp.minimum((t + 1) * tb, lp // 8 - 1)
    return _pcall(
        body, name="conv_bwd", grid=(bsz, nt),
        in_specs=[pl.BlockSpec(memory_space=pl.ANY),
                  pl.BlockSpec((None, tt, c), lambda b, t: (b, t, qb)),
                  pl.BlockSpec((None, hr, c), lambda b, t: (b, jnp.maximum(t * (tt // hr) - 1, 0), qb)),
                  pl.BlockSpec((None, tt, c), lambda b, t: (b, t, 0)),
                  pl.BlockSpec((None, 8, c), lambda b, t: (b, nxt(t), 0)),
                  pl.BlockSpec((None, tt, c), lambda b, t: (b, t, 0)),
                  pl.BlockSpec((None, 8, c), lambda b, t: (b, nxt(t), 0)),
                  _const((CONV_W, c))],
        out_specs=[pl.BlockSpec((None, tt, c), lambda b, t: (b, t, qb)), _const((CONV_W, c)), _const((1, c))],
        out_shape=[sd(dp3.shape, dp3.dtype), sd((CONV_W, c), F32), sd((1, c), F32)],
        scratch_shapes=[pltpu.VMEM((tt + hr, c), F32), pltpu.VMEM((tt + 8, c), F32)],
        input_output_aliases={0: 0},
        compiler_params=_cp(("arbitrary", "arbitrary")),
    )(dp3, p3, p3, dact3, dact3, pre3, pre3, cw)


def _mlstm_gates(g, h_idx, c_idx, lc):
    lane = lax.broadcasted_iota(jnp.int32, g.shape, 1)
    i_col = jnp.sum(jnp.where(lane == h_idx, g, 0.0), axis=1, keepdims=True)
    f_col = jnp.sum(jnp.where(lane == M_HEADS + h_idx, g, 0.0), axis=1, keepdims=True)
    row = lax.broadcasted_iota(jnp.int32, (lc, 1), 0)
    valid = (c_idx * lc + row) >= PAD
    li = jnp.where(valid, i_col, NEG)
    lf = jnp.where(valid, jnp.minimum(f_col, 0.0) - jnp.log(1.0 + jnp.exp(-jnp.abs(f_col))), 0.0)
    r2 = lax.broadcasted_iota(jnp.int32, (lc, lc), 0)
    c2 = lax.broadcasted_iota(jnp.int32, (lc, lc), 1)
    eye = r2 == c2
    tril = r2 >= c2
    to_row = lambda col: jnp.sum(jnp.where(eye, col, 0.0), axis=0, keepdims=True)
    lf_row = to_row(lf)
    b_col = jnp.sum(jnp.where(tril, lf_row, 0.0), axis=1, keepdims=True)
    b_row = to_row(b_col)
    li_row = to_row(li)
    d_mat = jnp.where(tril, b_col - b_row + li_row, NEG)
    return dict(f_col=f_col, valid=valid, li=li, b_col=b_col, d_mat=d_mat, eye=eye, r2=r2, c2=c2, row=row,
                to_row=to_row)


def _mlstm_chunk(q, ks, v, gq, c_st, n_st, m_st, lc):
    b_col, d_mat = gq["b_col"], gq["d_mat"]
    m_inter = b_col + m_st
    m_row = jnp.maximum(m_inter, jnp.max(d_mat, axis=1, keepdims=True))
    w_intra = jnp.exp(d_mat - m_row)
    w_inter = jnp.exp(m_inter - m_row)
    qb, kb, vb, cb = _bf(q), _bf(ks), _bf(v), _bf(c_st)
    s = _dot(qb, kb, NT) * w_intra
    qc = _dot(qb, cb)
    num = _dot(_bf(s), vb) + w_inter * qc
    qn = jnp.sum(q * n_st, axis=1, keepdims=True)
    den = jnp.sum(s, axis=1, keepdims=True) + w_inter * qn
    e = jnp.exp(-m_row)
    nn = jnp.maximum(jnp.abs(den), e)
    b_last = b_col[lc - 1:lc, :]
    g_log = b_last - b_col + gq["li"]
    m_new = jnp.maximum(b_last + m_st, jnp.max(g_log, axis=0, keepdims=True))
    w_k = jnp.exp(g_log - m_new)
    decay = jnp.exp(b_last + m_st - m_new)
    return dict(w_intra=w_intra, w_inter=w_inter, qb=qb, kb=kb, vb=vb, cb=cb, s=s, qc=qc, num=num, qn=qn, den=den,
                e=e, nn=nn, m_new=m_new, w_k=w_k, decay=decay)


def _chunks_per_step(nc):
    return max(c for c in (3, 2, 1) if nc % c == 0)


def _mlstm_fwd(qk3, p3, pg3):
    bsz, lp, _ = p3.shape
    lc = M_CHUNK
    nc = lp // lc
    dk, dv = 128, 256
    scale = dk ** -0.5

    cps = _chunks_per_step(nc)
    rows = cps * lc

    def body(q_ref, k_ref, v_ref, g_ref, h_ref, cs_ref, ns_ref, ms_ref, c_sc, n_sc, m_sc):
        st = pl.program_id(1)

        @pl.when(st == 0)
        def _():
            c_sc[...] = jnp.zeros_like(c_sc)
            n_sc[...] = jnp.zeros_like(n_sc)
            m_sc[...] = jnp.zeros_like(m_sc)

        for j in range(cps):
            rs = slice(j * lc, (j + 1) * lc)
            g = g_ref[rs, :]
            for hh in range(M_HEADS):
                c_st, n_st, m_all = c_sc[hh], n_sc[hh], m_sc[hh]
                cs_ref[hh, j] = c_st
                ns_ref[hh, j] = n_st
                ms_ref[hh, j] = m_all
                m_st = m_all[:, 0:1]
                q = q_ref[rs, hh * dk:(hh + 1) * dk]
                ks = k_ref[rs, hh * dk:(hh + 1) * dk] * scale
                v = v_ref[rs, hh * dv:(hh + 1) * dv]
                gq = _mlstm_gates(g, hh, st * cps + j, lc)
                f = _mlstm_chunk(q, ks, v, gq, c_st, n_st, m_st, lc)
                h_ref[rs, hh * dv:(hh + 1) * dv] = f["num"] / f["nn"]
                kw = ks * f["w_k"]
                c_sc[hh] = f["decay"] * c_st + _dot(_bf(kw), f["vb"], TN)
                n_sc[hh] = f["decay"] * n_st + _colsum(kw)
                m_sc[hh] = jnp.broadcast_to(f["m_new"], (1, 128))

    sd = jax.ShapeDtypeStruct
    nh = M_HEADS
    return _pcall(
        body, name="mlstm_fwd", grid=(bsz, nc // cps),
        in_specs=[pl.BlockSpec((None, rows, nh * dk), lambda b, c: (b, c, 0)),
                  pl.BlockSpec((None, rows, nh * dk), lambda b, c: (b, c, 1)),
                  pl.BlockSpec((None, rows, nh * dv), lambda b, c: (b, c, V_OFF // (nh * dv))),
                  pl.BlockSpec((None, rows, 128), lambda b, c: (b, c, 0))],
        out_specs=[pl.BlockSpec((None, rows, nh * dv), lambda b, c: (b, c, 0)),
                   pl.BlockSpec((None, nh, cps, dk, dv), lambda b, c: (b, 0, c, 0, 0)),
                   pl.BlockSpec((None, nh, cps, 1, dk), lambda b, c: (b, 0, c, 0, 0)),
                   pl.BlockSpec((None, nh, cps, 1, 128), lambda b, c: (b, 0, c, 0, 0))],
        out_shape=[sd((bsz, lp, nh * dv), F32), sd((bsz, nh, nc, dk, dv), F32),
                   sd((bsz, nh, nc, 1, dk), F32), sd((bsz, nh, nc, 1, 128), F32)],
        scratch_shapes=[pltpu.VMEM((nh, dk, dv), F32), pltpu.VMEM((nh, 1, dk), F32), pltpu.VMEM((nh, 1, 128), F32)],
        compiler_params=_cp(("parallel", "arbitrary")),
    )(qk3, qk3, p3, pg3)


def _mlstm_bwd(dp3, qk3, p3, pg3, dh3, cs, ns, ms):
    bsz, lp, _ = p3.shape
    lc = M_CHUNK
    nc = lp // lc
    dk, dv = 128, 256
    scale = dk ** -0.5

    cps = _chunks_per_step(nc)
    nst = nc // cps
    rows = cps * lc

    def body(dp_any, q_ref, k_ref, v_ref, g_ref, dh_ref, cs_ref, ns_ref, ms_ref,
             dv_ref, dqk_ref, dg_ref, dc_sc, dn_sc):
        t = pl.program_id(1)
        st = nst - 1 - t

        @pl.when(t == 0)
        def _():
            dc_sc[...] = jnp.zeros_like(dc_sc)
            dn_sc[...] = jnp.zeros_like(dn_sc)

        lane = lax.broadcasted_iota(jnp.int32, (lc, 128), 1)
        for j in reversed(range(cps)):
            rs = slice(j * lc, (j + 1) * lc)
            g = g_ref[rs, :]
            dgate = jnp.zeros((lc, 128), F32)
            for hh in range(M_HEADS):
                dgate = head(hh, j, rs, st * cps + j, g, lane, dgate, q_ref, k_ref, v_ref, dh_ref, cs_ref, ns_ref,
                             ms_ref, dv_ref, dqk_ref, dc_sc, dn_sc)
            dg_ref[rs, :] = dgate.astype(dg_ref.dtype)

    def head(hh, j, sl, c, g, lane, dgate, q_ref, k_ref, v_ref, dh_ref, cs_ref, ns_ref, ms_ref, dv_ref, dqk_ref,
             dc_sc, dn_sc):
        c_st, n_st = cs_ref[hh, j], ns_ref[hh, j]
        m_st = ms_ref[hh, j][:, 0:1]
        q = q_ref[sl, hh * dk:(hh + 1) * dk]
        ks = k_ref[sl, hh * dk:(hh + 1) * dk] * scale
        v = v_ref[sl, hh * dv:(hh + 1) * dv]
        dh = dh_ref[sl, hh * dv:(hh + 1) * dv]
        gq = _mlstm_gates(g, hh, c, lc)
        f = _mlstm_chunk(q, ks, v, gq, c_st, n_st, m_st, lc)
        eye, r2, c2, row, valid = gq["eye"], gq["r2"], gq["c2"], gq["row"], gq["valid"]
        w_intra, w_inter, s, nn, den = f["w_intra"], f["w_inter"], f["s"], f["nn"], f["den"]
        qb, kb, vb, cb, w_k, decay = f["qb"], f["kb"], f["vb"], f["cb"], f["w_k"], f["decay"]
        d_c, d_n = dc_sc[hh], dn_sc[hh]
        d_cb = _bf(d_c)

        hout = f["num"] / nn
        dnum = dh / nn
        d_nn = -jnp.sum(dh * hout, axis=1, keepdims=True) / nn
        dden = jnp.where(jnp.abs(den) > f["e"], d_nn * jnp.sign(den), 0.0)
        wdnum = w_inter * dnum
        wdden = w_inter * dden
        ds = _dot(_bf(dnum), vb, NT) + dden
        dsw = _bf(ds * w_intra)
        dq = _dot(dsw, kb) + _dot(_bf(wdnum), cb, NT) + wdden * n_st
        dkw = _dot(vb, d_cb, NT) + d_n
        dks = _dot(dsw, qb, TN) + dkw * w_k
        kw = ks * w_k
        dvv = _dot(_bf(s), _bf(dnum), TN) + _dot(_bf(kw), d_cb)
        dd = ds * s
        rs = jnp.sum(dd, axis=1, keepdims=True)
        cs_col = jnp.sum(jnp.where(eye, jnp.sum(dd, axis=0, keepdims=True), 0.0), axis=1, keepdims=True)
        dwi = jnp.sum(dnum * f["qc"], axis=1, keepdims=True) + dden * f["qn"]
        db = rs - cs_col + dwi * w_inter
        dli = cs_col
        ddecay = jnp.sum(jnp.sum(d_c * c_st, axis=1, keepdims=True), axis=0, keepdims=True) \
            + jnp.sum(d_n * n_st, axis=1, keepdims=True)
        dgl = jnp.sum(dkw * ks, axis=1, keepdims=True) * w_k
        dblast = ddecay * decay + jnp.sum(dgl, axis=0, keepdims=True)
        db = db - dgl + jnp.where(row == lc - 1, dblast, 0.0)
        dli = dli + dgl
        db_row = gq["to_row"](db)
        dlf = jnp.sum(jnp.where(c2 >= r2, db_row, 0.0), axis=1, keepdims=True)
        dlf = jnp.where(valid, dlf, 0.0)
        dgate = jnp.where(lane == hh, jnp.where(valid, dli, 0.0), dgate)
        dgate = jnp.where(lane == M_HEADS + hh, dlf / (1.0 + jnp.exp(gq["f_col"])), dgate)
        dqk_ref[sl, hh * dk:(hh + 1) * dk] = dq
        dqk_ref[sl, (M_HEADS + hh) * dk:(M_HEADS + hh + 1) * dk] = dks * scale
        dv_ref[sl, hh * dv:(hh + 1) * dv] = dvv.astype(dv_ref.dtype)
        dc_sc[hh] = decay * d_c + _dot(qb, _bf(wdnum), TN)
        dn_sc[hh] = decay * d_n + _colsum(q * wdden)
        return dgate

    sd = jax.ShapeDtypeStruct
    nh = M_HEADS
    rc = lambda c: nst - 1 - c
    return _pcall(
        body, name="mlstm_bwd", grid=(bsz, nst),
        in_specs=[pl.BlockSpec(memory_space=pl.ANY),
                  pl.BlockSpec((None, rows, nh * dk), lambda b, c: (b, rc(c), 0)),
                  pl.BlockSpec((None, rows, nh * dk), lambda b, c: (b, rc(c), 1)),
                  pl.BlockSpec((None, rows, nh * dv), lambda b, c: (b, rc(c), V_OFF // (nh * dv))),
                  pl.BlockSpec((None, rows, 128), lambda b, c: (b, rc(c), 0)),
                  pl.BlockSpec((None, rows, nh * dv), lambda b, c: (b, rc(c), 0)),
                  pl.BlockSpec((None, nh, cps, dk, dv), lambda b, c: (b, 0, rc(c), 0, 0)),
                  pl.BlockSpec((None, nh, cps, 1, dk), lambda b, c: (b, 0, rc(c), 0, 0)),
                  pl.BlockSpec((None, nh, cps, 1, 128), lambda b, c: (b, 0, rc(c), 0, 0))],
        out_specs=[pl.BlockSpec((None, rows, nh * dv), lambda b, c: (b, rc(c), V_OFF // (nh * dv))),
                   pl.BlockSpec((None, rows, 2 * nh * dk), lambda b, c: (b, rc(c), 0)),
                   pl.BlockSpec((None, rows, 128), lambda b, c: (b, rc(c), 0))],
        out_shape=[sd(dp3.shape, dp3.dtype), sd((bsz, lp, 2 * nh * dk), F32), sd((bsz, lp, 128), dp3.dtype)],
        scratch_shapes=[pltpu.VMEM((nh, dk, dv), F32), pltpu.VMEM((nh, 1, dk), F32)],
        input_output_aliases={0: 0},
        compiler_params=_cp(("arbitrary", "arbitrary")),
    )(dp3, qk3, qk3, p3, pg3, dh3, cs, ns, ms)


def _headnorm(x):
    dv = x.shape[1] // M_HEADS
    xh, rs = [], []
    for h in range(M_HEADS):
        xx = x[:, h * dv:(h + 1) * dv]
        mu = jnp.mean(xx, axis=-1, keepdims=True)
        xc = xx - mu
        rstd = lax.rsqrt(jnp.mean(xc * xc, axis=-1, keepdims=True) + LN_EPS)
        xh.append(xc * rstd)
        rs.append(rstd)
    return jnp.concatenate(xh, axis=1), rs


def _mix_fwd(hm, p, ys5g, h0, gn, wmo_bf, wo_bf, lp):
    r, d = hm.shape
    tm = _row_tile(lp, 208)

    def body(hm_ref, o_ref, gs_ref, gm_ref, ys_ref, h0_ref, gn_ref, wmo_ref, wo_ref,
             ymin_ref, ym_ref, mix_ref, r1_ref):
        xhat, _ = _headnorm(hm_ref[...])
        ymin = _bf(_sig(o_ref[...].astype(F32)) * (xhat * gn_ref[...]))
        ymin_ref[...] = ymin
        ym = _dot(ymin, wmo_ref[...])
        ym_ref[...] = ym
        mix = _bf(_sig(gs_ref[...].astype(F32)) * ys_ref[...] + _sig(gm_ref[...].astype(F32)) * ym)
        mix_ref[...] = mix
        r1_ref[...] = ALPHA * h0_ref[...] + _dot(mix, wo_ref[...])

    sd = jax.ShapeDtypeStruct
    row = pl.BlockSpec((tm, d), lambda i: (i, 0))
    return _pcall(
        body, name="mix_fwd", grid=(r // tm,),
        in_specs=[row, pl.BlockSpec((tm, d), lambda i: (i, O_OFF // d)), pl.BlockSpec((tm, d), lambda i: (i, GS_OFF // d)),
                  pl.BlockSpec((tm, d), lambda i: (i, GM_OFF // d)), row, row, _const((1, d)),
                  _resident((d, d)), _resident((d, d))],
        out_specs=[row] * 4,
        out_shape=[sd((r, d), BF16), sd((r, d), F32), sd((r, d), BF16), sd((r, d), F32)],
        compiler_params=_cp(("parallel",), 48),
    )(hm, p, p, p, ys5g, h0, gn, wmo_bf, wo_bf)


def _mix_bwd(dr1, wo_bf, wmo_bf, p, ys5g, ym, hm, gn, lp):
    r, d = hm.shape
    tm = _row_tile(lp, 208)
    dv = d // M_HEADS

    def body(dr1_ref, wo_ref, wmo_ref, o_ref, gs_ref, gm_ref, ys_ref, ym_ref, hm_ref, gn_ref,
             dp_ref, dys_ref, dym_ref, dhm_ref, dgn_ref):
        i = pl.program_id(0)

        @pl.when(i == 0)
        def _():
            dgn_ref[...] = jnp.zeros_like(dgn_ref)

        dmix = _dot(_bf(dr1_ref[...]), wo_ref[...], NT)
        sgs, sgm, so = (_sig(gs_ref[...].astype(F32)), _sig(gm_ref[...].astype(F32)), _sig(o_ref[...].astype(F32)))
        dys_ref[...] = dmix * sgs
        dp_ref[:, d:2 * d] = _bf(dmix * ys_ref[...] * sgs * (1.0 - sgs))
        dym = dmix * sgm
        dym_ref[...] = _bf(dym)
        dp_ref[:, 2 * d:3 * d] = _bf(dmix * ym_ref[...] * sgm * (1.0 - sgm))
        dymin = _dot(_bf(dym), wmo_ref[...], NT)
        xhat, rs = _headnorm(hm_ref[...])
        gn_ = gn_ref[...]
        dp_ref[:, 0:d] = _bf(dymin * (xhat * gn_) * so * (1.0 - so))
        dhn = dymin * so
        dgn_ref[...] += _colsum(dhn * xhat)
        dxh = dhn * gn_
        for h in range(M_HEADS):
            sl = slice(h * dv, (h + 1) * dv)
            a, xh = dxh[:, sl], xhat[:, sl]
            m1 = jnp.mean(a, axis=-1, keepdims=True)
            m2 = jnp.mean(a * xh, axis=-1, keepdims=True)
            dhm_ref[:, sl] = rs[h] * (a - m1 - xh * m2)

    sd = jax.ShapeDtypeStruct
    row = pl.BlockSpec((tm, d), lambda i: (i, 0))
    vec = _const((1, d))
    return _pcall(
        body, name="mix_bwd", grid=(r // tm,),
        in_specs=[row, _resident((d, d)), _resident((d, d)),
                  pl.BlockSpec((tm, d), lambda i: (i, O_OFF // d)), pl.BlockSpec((tm, d), lambda i: (i, GS_OFF // d)),
                  pl.BlockSpec((tm, d), lambda i: (i, GM_OFF // d)), row, row, row, vec],
        out_specs=[pl.BlockSpec((tm, 3 * d), lambda i: (i, 0)), row, row, row, vec],
        out_shape=[sd((r, NP), BF16), sd((r, d), F32), sd((r, d), BF16), sd((r, d), F32), sd((1, d), F32)],
        compiler_params=_cp(("arbitrary",), 48),
    )(dr1, wo_bf, wmo_bf, p, p, p, ys5g, ym, hm, gn)


def _mlp_fwd(r1, tgt, g1, b1, wup_g, wdn_bf, bup, g2, b2, lp):
    r, d = r1.shape
    tm = _row_tile(lp, 352)
    tps = lp // tm
    nf = wup_g.shape[0]

    def body(r1_ref, t_ref, g1_ref, b1_ref, wup_ref, wdn_ref, bup_ref, g2_ref, b2_ref,
             dr2_ref, h1b_ref, act_ref, loss_ref, dg2_ref, db2_ref):
        i = pl.program_id(0)

        @pl.when(i == 0)
        def _():
            loss_ref[...] = jnp.zeros_like(loss_ref)
            dg2_ref[...] = jnp.zeros_like(dg2_ref)
            db2_ref[...] = jnp.zeros_like(db2_ref)

        h1, _, _ = _ln_fwd(r1_ref[...], g1_ref[...], b1_ref[...])
        h1b = _bf(h1)
        h1b_ref[...] = h1b
        ff = jnp.zeros((tm, d), F32)
        for s in range(nf):
            up = _dot(h1b, wup_ref[s]) + bup_ref[:, s * d:(s + 1) * d]
            a = jnp.maximum(up, 0.0)
            a = _bf(a * a)
            act_ref[:, s * d:(s + 1) * d] = a
            ff = ff + _dot(a, wdn_ref[s * d:(s + 1) * d, :])
        r2 = ALPHA * h1 + ff
        g2 = g2_ref[...]
        y, xhat, rstd = _ln_fwd(r2, g2, b2_ref[...])
        t = (i % tps) * tm + lax.broadcasted_iota(jnp.int32, (tm, 1), 0)
        diff = jnp.where(t >= PAD + N_META, y - t_ref[...], 0.0)
        loss_ref[...] += 0.5 / d * jnp.sum(jnp.sum(diff * diff, axis=1, keepdims=True), axis=0, keepdims=True)
        dy = diff * (1.0 / d)
        dg2_ref[...] += _colsum(dy * xhat)
        db2_ref[...] += _colsum(dy)
        dr2_ref[...] = _ln_bwd(dy, xhat, rstd, g2)

    sd = jax.ShapeDtypeStruct
    row = pl.BlockSpec((tm, d), lambda i: (i, 0))
    vec = _const((1, d))
    return _pcall(
        body, name="mlp_fwd", grid=(r // tm,),
        in_specs=[row, row, vec, vec, _resident(wup_g.shape), _resident(wdn_bf.shape), _const((1, nf * d)), vec, vec],
        out_specs=[row, row, pl.BlockSpec((tm, nf * d), lambda i: (i, 0)), _const((1, 128)), vec, vec],
        out_shape=[sd((r, d), F32), sd((r, d), BF16), sd((r, nf * d), BF16), sd((1, 128), F32), sd((1, d), F32),
                   sd((1, d), F32)],
        compiler_params=_cp(("arbitrary",), 56),
    )(r1, tgt, g1, b1, wup_g, wdn_bf, bup, g2, b2)


def _mlp_bwd(h1b, dr2, r1, g1, wup_g, wdn_bf, bup, lp):
    r, d = h1b.shape
    tm = _row_tile(lp, 352)
    nf = wup_g.shape[0]

    def body(h1_ref, dr2_ref, r1_ref, g1_ref, wup_ref, wdn_ref, bup_ref, dr1_ref, dup_ref, dbup_ref, dg1_ref, db1_ref):
        i = pl.program_id(0)

        @pl.when(i == 0)
        def _():
            dbup_ref[...] = jnp.zeros_like(dbup_ref)
            dg1_ref[...] = jnp.zeros_like(dg1_ref)
            db1_ref[...] = jnp.zeros_like(db1_ref)

        h1b = h1_ref[...]
        dr2 = dr2_ref[...]
        dr2b = _bf(dr2)
        acc = ALPHA * dr2
        for s in range(nf):
            up = _dot(h1b, wup_ref[s]) + bup_ref[:, s * d:(s + 1) * d]
            dact = _dot(dr2b, wdn_ref[s * d:(s + 1) * d, :], NT)
            dup = dact * (2.0 * jnp.maximum(up, 0.0))
            dbup_ref[:, s * d:(s + 1) * d] += _colsum(dup)
            dupb = _bf(dup)
            dup_ref[:, s * d:(s + 1) * d] = dupb
            acc = acc + _dot(dupb, wup_ref[s], NT)
        g1 = g1_ref[...]
        _, xhat1, rstd1 = _ln_fwd(r1_ref[...], g1, 0.0)
        dr1_ref[...] = _ln_bwd(acc, xhat1, rstd1, g1)
        dg1_ref[...] += _colsum(acc * xhat1)
        db1_ref[...] += _colsum(acc)

    sd = jax.ShapeDtypeStruct
    row = pl.BlockSpec((tm, d), lambda i: (i, 0))
    vec = _const((1, d))
    return _pcall(
        body, name="mlp_bwd", grid=(r // tm,),
        in_specs=[row, row, row, vec, _resident(wup_g.shape), _resident(wdn_bf.shape), _const((1, nf * d))],
        out_specs=[row, pl.BlockSpec((tm, nf * d), lambda i: (i, 0)), _const((1, nf * d)), vec, vec],
        out_shape=[sd((r, d), F32), sd((r, nf * d), BF16), sd((1, nf * d), F32), sd((1, d), F32), sd((1, d), F32)],
        compiler_params=_cp(("arbitrary",), 56),
    )(h1b, dr2, r1, g1, wup_g, wdn_bf, bup)


def _s5_block_mats(bb_re_t, bb_im_t, c_re, c_im, ap_re, ap_im):
    ng = c_re.shape[0]
    gl = ng // S5_KCH
    eye = jnp.eye(gl, dtype=F32)

    def bmat(bt):
        bb = jnp.transpose(bt, (1, 0, 2)).reshape(S5_KCH, gl, S5_GROUP, S5_STATE)
        return jnp.einsum("kghp,gj->kghjp", bb, eye).reshape(S5_KCH, gl * S5_GROUP, gl * S5_STATE)

    def cmat(c):
        cc = c.reshape(S5_KCH, gl, S5_GROUP, S5_STATE)
        return jnp.einsum("kghp,gj->kjpgh", cc, eye).reshape(S5_KCH, gl * S5_STATE, gl * S5_GROUP)

    def pw(a):
        return jnp.transpose(a.reshape(8, S5_KCH, gl * S5_STATE), (1, 0, 2))

    bk = jnp.concatenate([bmat(bb_re_t), bmat(bb_im_t)], axis=-1)
    apow = jnp.concatenate([pw(ap_re), pw(ap_im)], axis=-1)
    return _bf(bk), _bf(cmat(c_re)), _bf(cmat(c_im)), apow


def _s5_block_grads(dbk, dcre, dcim, da):
    gl = dbk.shape[1] // S5_GROUP
    ng = gl * S5_KCH
    eye = jnp.eye(gl, dtype=F32)
    hw = gl * S5_STATE

    def bpart(x):
        x = x.reshape(S5_KCH, gl, S5_GROUP, gl, S5_STATE)
        x = jnp.einsum("kghjp,gj->kghp", x, eye).reshape(ng, S5_GROUP, S5_STATE)
        return jnp.transpose(x, (1, 0, 2))

    def cpart(x):
        x = x.reshape(S5_KCH, gl, S5_STATE, gl, S5_GROUP)
        return jnp.einsum("kjpgh,gj->kghp", x, eye).reshape(ng, S5_GROUP, S5_STATE)

    return (bpart(dbk[..., :hw]), bpart(dbk[..., hw:]), cpart(dcre), cpart(dcim),
            da[:, 0, :hw].reshape(ng, S5_STATE), da[:, 0, hw:].reshape(ng, S5_STATE))


def _tie(a, tok):
    return a if tok is None else a + tok[0, 0]


def _local_step(x, tgt, w, early=None, late=None, ready=None):
    ready = ready or (lambda names, g: None)
    bsz, seq, d = x.shape
    lp = PAD + N_META + seq
    r = bsz * lp
    meta = jnp.broadcast_to(w["meta_tokens"][None], (bsz, N_META, d))
    hin = jnp.concatenate([jnp.zeros((bsz, PAD, d), F32), meta, x], axis=1).reshape(r, d)
    tgtp = jnp.concatenate([jnp.zeros((bsz, PAD + N_META, d), F32), tgt], axis=1).reshape(r, d)

    h0, h0b = _ln0_fwd(hin, w["ln0_g"], w["ln0_b"], lp)
    if early is not None:
        w = {**w, **early((h0, tgtp))}
    p, pg = _inproj(h0b, w["w_in"], w["b_in"], lp)
    p3 = p.reshape(bsz, lp, NP)
    pg3 = pg.reshape(bsz, lp, 128)

    b_re_t = jnp.transpose(w["s5_b_re"], (2, 0, 1))
    b_im_t = jnp.transpose(w["s5_b_im"], (2, 0, 1))
    ap_re, ap_im, bb_re_t, bb_im_t = _s5_prep(w["s5_lambda_re"], w["s5_lambda_im"], w["s5_log_dt"], b_re_t, b_im_t)
    bk, cre, cim, apow = _s5_block_mats(bb_re_t, bb_im_t, w["s5_c_re"], w["s5_c_im"], ap_re, ap_im)
    y_s5, xs = _s5_fwd(p3, bk, cre, cim, apow, w["s5_d"])
    sw = y_s5.shape[-1]
    if late is not None:
        w = {**w, **late(y_s5)}
    gy, z, ys5g = _glu_fwd(y_s5.reshape(r, sw), w["s5_w_glu"], lp)

    pre3, qk3 = _conv_fwd(p3, w["qk_conv_w"], w["qk_conv_b"])
    hm3, cs, ns, ms = _mlstm_fwd(qk3, p3, pg3)
    hm = hm3.reshape(r, d)
    ymin, ym, mix, r1 = _mix_fwd(hm, p, ys5g, h0, w["m_norm_g"], w["m_w_out"], w["w_o"], lp)
    dr2, h1b, act, loss, dg2, db2 = _mlp_fwd(r1, tgtp, w["ln1_g"], w["ln1_b"], w["w_up"], w["w_down"], w["b_up"],
                                             w["ln2_g"], w["ln2_b"], lp)

    g = {"ln2_g": dg2, "ln2_b": db2}
    dr1, dup, g["b_up"], g["ln1_g"], g["ln1_b"] = _mlp_bwd(h1b, dr2, r1, w["ln1_g"], w["w_up"], w["w_down"], w["b_up"], lp)
    g["w_down"] = _mm_tn(act, dr2, name="dw_down")
    g["w_up"] = _mm_tn(h1b, dup, name="dw_up", split=w["w_up"].shape[0])
    tok = ready(("w_down", "w_up"), g)
    dp, dys5g, dym, dhm, g["m_norm_g"] = _mix_bwd(
        dr1, w["w_o"], w["m_w_out"], p, ys5g, ym, hm, _tie(w["m_norm_g"], tok), lp)
    g["w_o"] = _mm_tn(mix, dr1, name="dw_o")
    g["m_w_out"] = _mm_tn(ymin, dym, name="dw_mout")

    dp3 = dp.reshape(bsz, lp, NP)
    dp3, dqk3, dgate = _mlstm_bwd(dp3, qk3, p3, pg3, dhm.reshape(bsz, lp, d), cs, ns, ms)
    dp3, g["qk_conv_w"], g["qk_conv_b"] = _conv_bwd(dp3, p3, dqk3, pre3, w["qk_conv_w"])
    dz, dys5 = _glu_bwd(dys5g, z, y_s5.reshape(r, sw), w["s5_w_glu"], lp)
    g["s5_w_glu"] = _mm_tn(gy, dz, name="dw_glu", split=w["s5_w_glu"].shape[0])
    tok = ready(("s5_w_glu", "m_w_out", "w_o"), g)
    apow_rev = jnp.flip(apow, axis=1)
    dp3, dbk, dcre, dcim, da, g["s5_d"] = _s5_bwd(dp3, p3, dys5.reshape(bsz, lp, sw), xs, bk, cre, cim, apow_rev,
                                                 _tie(w["s5_d"], tok))
    dbb_re_t, dbb_im_t, g["s5_c_re"], g["s5_c_im"], da_re, da_im = _s5_block_grads(dbk, dcre, dcim, da)
    g["s5_lambda_re"], g["s5_lambda_im"], g["s5_log_dt"], gb_re_t, gb_im_t = _s5_prep_bwd(
        w["s5_lambda_re"], w["s5_lambda_im"], w["s5_log_dt"], b_re_t, b_im_t, da_re, da_im, dbb_re_t, dbb_im_t)
    g["s5_b_re"] = jnp.transpose(gb_re_t, (1, 2, 0))
    g["s5_b_im"] = jnp.transpose(gb_im_t, (1, 2, 0))

    dp3 = lax.dynamic_update_slice(dp3, dgate, (0, 0, G_OFF))
    dp = dp3.reshape(r, NP)
    g["w_in"], g["b_in"] = _mm_tn(h0b, dp, name="dw_in", colsum=True)
    tok = ready(("w_in",), g)
    dpw = _mm_nt(dp, w["w_in"], lp, name="dh0", dep=tok)
    dhin, g["ln0_g"], g["ln0_b"], g["meta_tokens"] = _ln0_bwd(hin, dr1, dpw, w["ln0_g"], lp)
    grad_x = dhin.reshape(bsz, lp, d)[:, PAD + N_META:]
    return loss, grad_x, g


_ANY = pl.BlockSpec(memory_space=pl.ANY)
_MESH = pl.DeviceIdType.MESH


def _place():
    return lax.axis_index("x"), lax.axis_index("y"), lax.axis_index("c")


def _gather_chips(shards):
    n = len(shards)

    def body(*refs):
        ins, outs = refs[:n], refs[n:2 * n]
        send, recv, loc = refs[2 * n:]
        x, y, c = _place()
        me = 2 * x + y
        peers = [(1 - x, y), (x, 1 - y), (1 - x, 1 - y)]

        def rc(a, k, slot):
            px, py = peers[k]
            return pltpu.make_async_remote_copy(src_ref=ins[a], dst_ref=outs[a].at[slot], send_sem=send.at[a, k],
                                                recv_sem=recv.at[a, k], device_id=(px, py, c), device_id_type=_MESH)

        own = [pltpu.make_async_copy(ins[a], outs[a].at[me], loc.at[a]) for a in range(n)]
        for cp in own:
            cp.start()
        out = [rc(a, k, me) for a in range(n) for k in range(3)]
        for cp in out:
            cp.start()
        for a in range(n):
            for k in range(3):
                rc(a, k, 2 * peers[k][0] + peers[k][1]).wait_recv()
        for cp in out:
            cp.wait_send()
        for cp in own:
            cp.wait()

    return _pcall(
        body, name="gather_chips", in_specs=[_ANY] * n, out_specs=[_ANY] * n,
        out_shape=[jax.ShapeDtypeStruct((4,) + s.shape, s.dtype) for s in shards],
        scratch_shapes=[pltpu.SemaphoreType.DMA((n, 3)), pltpu.SemaphoreType.DMA((n, 3)), pltpu.SemaphoreType.DMA((n,))],
    )(*shards)


_HBM = pl.BlockSpec(memory_space=pltpu.HBM)
_SEM = pl.BlockSpec(memory_space=pltpu.SEMAPHORE)
_EFFECT = pltpu.SideEffectType.DATAFLOW_SIDE_EFFECTING


def _xchg_copies(srcs, lands, send, recv, scatter):
    x, y, c = _place()
    me = 2 * x + y
    peers = [(1 - x, y), (x, 1 - y), (1 - x, 1 - y)]
    out = []
    for a in range(len(srcs)):
        for k, (px, py) in enumerate(peers):
            src = srcs[a].at[2 * px + py] if scatter else srcs[a]
            dst = lands[a].at[k] if scatter else lands[a].at[me]
            out.append(pltpu.make_async_remote_copy(src_ref=src, dst_ref=dst, send_sem=send.at[3 * a + k],
                                                    recv_sem=recv.at[3 * a + k], device_id=(px, py, c),
                                                    device_id_type=_MESH))
    return out


def _xchg_start(srcs, lands, *, name, scatter, dep=None):
    n = len(srcs)
    deps = [] if dep is None else [dep]
    nd = len(deps)

    def body(*refs):
        send, recv = refs[2 * n + nd], refs[2 * n + nd + 1]
        for cp in _xchg_copies(refs[:n], refs[n:2 * n], send, recv, scatter):
            cp.start()
        refs[-1][...] = jnp.zeros_like(refs[-1])

    hbm = lambda a: pltpu.HBM(a.shape, a.dtype)
    con = lambda a: pltpu.with_memory_space_constraint(a, pltpu.HBM)
    res = _pcall(
        body, name=name, in_specs=[_HBM] * (2 * n) + [_ANY] * nd,
        out_specs=[_SEM, _SEM] + [_HBM] * (2 * n) + [pl.BlockSpec(memory_space=pltpu.VMEM)],
        out_shape=[pltpu.SemaphoreType.DMA((3 * n,)), pltpu.SemaphoreType.DMA((3 * n,))]
        + [hbm(a) for a in srcs] + [hbm(a) for a in lands] + [jax.ShapeDtypeStruct((8, 128), F32)],
        input_output_aliases={i: 2 + i for i in range(2 * n)},
        compiler_params=pltpu.CompilerParams(has_side_effects=_EFFECT),
    )(*[con(a) for a in srcs], *[con(a) for a in lands], *deps)
    return res[0], res[1], list(res[2:2 + n]), list(res[2 + n:2 + 2 * n]), res[-1]


def _xchg_wait(send, recv, srcs, lands, after, *, name, scatter):
    n = len(srcs)
    afters = list(after) if isinstance(after, (list, tuple)) else [after]

    def body(*refs):
        s_ref, r_ref = refs[2 * n], refs[2 * n + 1]
        for cp in _xchg_copies(refs[:n], refs[n:2 * n], s_ref, r_ref, scatter):
            cp.wait_send()
            cp.wait_recv()

    hbm = lambda a: pltpu.HBM(a.shape, a.dtype)
    res = _pcall(
        body, name=name, in_specs=[_HBM] * (2 * n) + [_SEM, _SEM] + [_ANY] * len(afters),
        out_specs=[_HBM] * (2 * n),
        out_shape=[hbm(a) for a in srcs] + [hbm(a) for a in lands],
        input_output_aliases={i: i for i in range(2 * n)},
        compiler_params=pltpu.CompilerParams(has_side_effects=_EFFECT),
    )(*srcs, *lands, send, recv, *afters)
    return list(res[:n]), list(res[n:])


def _swap_cores(arrs, name="swap_cores"):
    n = len(arrs)

    def body(*refs):
        ins, outs = refs[:n], refs[n:2 * n]
        send, recv = refs[2 * n:]
        x, y, c = _place()
        cps = [pltpu.make_async_remote_copy(src_ref=ins[a], dst_ref=outs[a], send_sem=send.at[a], recv_sem=recv.at[a],
                                            device_id=(x, y, 1 - c), device_id_type=_MESH) for a in range(n)]
        for cp in cps:
            cp.start()
        for cp in cps:
            cp.wait_recv()
        for cp in cps:
            cp.wait_send()

    return _pcall(
        body, name=name, in_specs=[_ANY] * n, out_specs=[_ANY] * n,
        out_shape=[jax.ShapeDtypeStruct(s.shape, s.dtype) for s in arrs],
        scratch_shapes=[pltpu.SemaphoreType.DMA((n,)), pltpu.SemaphoreType.DMA((n,))],
    )(*arrs)


def _allreduce_small(v, dep=None):
    rows = v.shape[0]
    half = rows // 2
    assert half % 8 == 0 and 2 * half == rows
    deps = [] if dep is None else [dep]

    def body(v_ref, *rest):
        out_ref, sib_ref, pair_ref, slots_ref, send, recv = rest[len(deps):]
        x, y, c = _place()
        chip = 2 * x + y
        sibling = (x, y, 1 - c)
        peers = [(1 - x, y), (x, 1 - y), (1 - x, 1 - y)]
        mine = pl.ds(pl.multiple_of(c * half, 8), half)

        first = pltpu.make_async_remote_copy(src_ref=v_ref, dst_ref=sib_ref, send_sem=send.at[0], recv_sem=recv.at[0],
                                             device_id=sibling, device_id_type=_MESH)
        first.start()
        first.wait_recv()
        pair_ref[...] = v_ref[...] + sib_ref[...]
        slots_ref[chip] = pair_ref[mine, :]
        cross = [pltpu.make_async_remote_copy(src_ref=pair_ref.at[mine], dst_ref=slots_ref.at[chip],
                                              send_sem=send.at[1 + k], recv_sem=recv.at[1 + k],
                                              device_id=(px, py, c), device_id_type=_MESH)
                 for k, (px, py) in enumerate(peers)]
        for cp in cross:
            cp.start()
        for cp in cross:
            cp.wait_recv()
        out_ref[mine, :] = ((slots_ref[0] + slots_ref[1]) + slots_ref[2]) + slots_ref[3]
        last = pltpu.make_async_remote_copy(src_ref=out_ref.at[mine], dst_ref=out_ref.at[mine], send_sem=send.at[4],
                                            recv_sem=recv.at[4], device_id=sibling, device_id_type=_MESH)
        last.start()
        last.wait_recv()
        first.wait_send()
        for cp in cross:
            cp.wait_send()
        last.wait_send()

    vm = pl.BlockSpec(memory_space=pltpu.VMEM)
    return _pcall(
        body, name="allreduce_small", in_specs=[vm] + [_ANY] * len(deps), out_specs=vm,
        out_shape=jax.ShapeDtypeStruct((rows, 128), F32),
        scratch_shapes=[pltpu.VMEM((rows, 128), F32), pltpu.VMEM((rows, 128), F32), pltpu.VMEM((4, half, 128), F32),
                        pltpu.SemaphoreType.DMA((5,)), pltpu.SemaphoreType.DMA((5,))],
        compiler_params=_cp(None, 40),
    )(v, *deps)


def _sum_slots(own, land):
    ns, rows, cols = land.shape
    tm = _row_tile(rows, 256, 8)

    def body(own_ref, a_ref, o_ref):
        o_ref[...] = ((own_ref[...] + a_ref[0]) + a_ref[1]) + a_ref[2]

    return _pcall(
        body, name="sum_slots", grid=(rows // tm,),
        in_specs=[pl.BlockSpec((tm, cols), lambda i: (i, 0)), pl.BlockSpec((ns, tm, cols), lambda i: (0, i, 0))],
        out_specs=pl.BlockSpec((tm, cols), lambda i: (i, 0)),
        out_shape=jax.ShapeDtypeStruct((rows, cols), F32),
        compiler_params=_cp(("parallel",), 40),
    )(own, land)


def _adamw(w, m, v, g0, g1=None):
    rows, cols = w.shape[-2:]
    lead = w.ndim == 3
    tm = _row_tile(rows, max(8, (1 << 20) // (4 * cols)), 8)
    c1 = 1.0 - ADAM_B1 ** ADAM_STEP
    c2 = 1.0 - ADAM_B2 ** ADAM_STEP
    two = g1 is not None

    def body(*refs):
        w_ref, m_ref, v_ref, g0_ref = refs[:4]
        g_ref, d_ref, nm_ref, nv_ref = refs[-4:]
        g = g0_ref[...]
        if two:
            g = g + refs[4][...]
        nm = ADAM_B1 * m_ref[...] + (1.0 - ADAM_B1) * g
        nv = ADAM_B2 * v_ref[...] + (1.0 - ADAM_B2) * (g * g)
        g_ref[...] = g
        nm_ref[...] = nm
        nv_ref[...] = nv
        d_ref[...] = -ADAM_LR * ((nm / c1) / (jnp.sqrt(nv / c2) + ADAM_EPS) + ADAM_WD * w_ref[...])

    blk = pl.BlockSpec((tm, cols), lambda i: (i, 0))
    wblk = pl.BlockSpec((None, tm, cols), lambda i: (0, i, 0)) if lead else blk
    ins = [w, m, v, g0] + ([g1] if two else [])
    return _pcall(
        body, name="adamw", grid=(rows // tm,), in_specs=[wblk] * 3 + [blk] * (len(ins) - 3), out_specs=[wblk] * 4,
        out_shape=[jax.ShapeDtypeStruct(w.shape, F32)] * 4,
        compiler_params=_cp(("parallel",), 40),
    )(*ins)


_BIG = ("w_in", "s5_w_glu", "m_w_out", "w_o", "w_up", "w_down")
_SMALL = ("ln0_g", "ln0_b", "b_in", "qk_conv_b", "s5_lambda_re", "s5_lambda_im", "s5_log_dt", "s5_b_re", "s5_b_im",
          "s5_c_re", "s5_c_im", "s5_d", "m_norm_g", "ln1_g", "ln1_b", "b_up", "ln2_g", "ln2_b")
_SMALL_SHARDED = ("meta_tokens", "qk_conv_w")
_ORDER = ("meta_tokens", "ln0_g", "ln0_b", "w_in", "b_in", "qk_conv_w", "qk_conv_b", "s5_lambda_re", "s5_lambda_im",
          "s5_log_dt", "s5_b_re", "s5_b_im", "s5_c_re", "s5_c_im", "s5_d", "s5_w_glu", "m_norm_g", "m_w_out", "w_o",
          "ln1_g", "ln1_b", "w_up", "b_up", "w_down", "ln2_g", "ln2_b")


def _pack(arrs):
    flat = jnp.concatenate([a.reshape(-1) for a in arrs])
    n = flat.shape[0]
    rows = -(-n // 2048) * 16
    return jnp.pad(flat, (0, rows * 128 - n)).reshape(rows, 128)


def _unpack(packed, shapes):
    flat = packed.reshape(-1)
    out, off = [], 0
    for s in shapes:
        n = math.prod(s)
        out.append(flat[off:off + n].reshape(s))
        off += n
    return out


def kernel(x, meta_tokens, ln0_g, ln0_b, w_in, b_in, qk_conv_w, qk_conv_b, s5_lambda_re, s5_lambda_im, s5_log_dt, s5_b_re, s5_b_im, s5_c_re, s5_c_im, s5_d, s5_w_glu, m_norm_g, m_w_out, w_o, ln1_g, ln1_b, w_up, b_up, w_down, ln2_g, ln2_b, loss_target, m_meta_tokens, m_ln0_g, m_ln0_b, m_w_in, m_b_in, m_qk_conv_w, m_qk_conv_b, m_s5_lambda_re, m_s5_lambda_im, m_s5_log_dt, m_s5_b_re, m_s5_b_im, m_s5_c_re, m_s5_c_im, m_s5_d, m_s5_w_glu, m_m_norm_g, m_m_w_out, m_w_o, m_ln1_g, m_ln1_b, m_w_up, m_b_up, m_w_down, m_ln2_g, m_ln2_b, v_meta_tokens, v_ln0_g, v_ln0_b, v_w_in, v_b_in, v_qk_conv_w, v_qk_conv_b, v_s5_lambda_re, v_s5_lambda_im, v_s5_log_dt, v_s5_b_re, v_s5_b_im, v_s5_c_re, v_s5_c_im, v_s5_d, v_s5_w_glu, v_m_norm_g, v_m_w_out, v_w_o, v_ln1_g, v_ln1_b, v_w_up, v_b_up, v_w_down, v_ln2_g, v_ln2_b):
    wts = dict(meta_tokens=meta_tokens, ln0_g=ln0_g, ln0_b=ln0_b, w_in=w_in, b_in=b_in, qk_conv_w=qk_conv_w,
               qk_conv_b=qk_conv_b, s5_lambda_re=s5_lambda_re, s5_lambda_im=s5_lambda_im, s5_log_dt=s5_log_dt,
               s5_b_re=s5_b_re, s5_b_im=s5_b_im, s5_c_re=s5_c_re, s5_c_im=s5_c_im, s5_d=s5_d, s5_w_glu=s5_w_glu,
               m_norm_g=m_norm_g, m_w_out=m_w_out, w_o=w_o, ln1_g=ln1_g, ln1_b=ln1_b, w_up=w_up, b_up=b_up,
               w_down=w_down, ln2_g=ln2_g, ln2_b=ln2_b)
    mom = dict(meta_tokens=m_meta_tokens, ln0_g=m_ln0_g, ln0_b=m_ln0_b, w_in=m_w_in, b_in=m_b_in, qk_conv_w=m_qk_conv_w,
               qk_conv_b=m_qk_conv_b, s5_lambda_re=m_s5_lambda_re, s5_lambda_im=m_s5_lambda_im, s5_log_dt=m_s5_log_dt,
               s5_b_re=m_s5_b_re, s5_b_im=m_s5_b_im, s5_c_re=m_s5_c_re, s5_c_im=m_s5_c_im, s5_d=m_s5_d,
               s5_w_glu=m_s5_w_glu, m_norm_g=m_m_norm_g, m_w_out=m_m_w_out, w_o=m_w_o, ln1_g=m_ln1_g, ln1_b=m_ln1_b,
               w_up=m_w_up, b_up=m_b_up, w_down=m_w_down, ln2_g=m_ln2_g, ln2_b=m_ln2_b)
    var = dict(meta_tokens=v_meta_tokens, ln0_g=v_ln0_g, ln0_b=v_ln0_b, w_in=v_w_in, b_in=v_b_in, qk_conv_w=v_qk_conv_w,
               qk_conv_b=v_qk_conv_b, s5_lambda_re=v_s5_lambda_re, s5_lambda_im=v_s5_lambda_im, s5_log_dt=v_s5_log_dt,
               s5_b_re=v_s5_b_re, s5_b_im=v_s5_b_im, s5_c_re=v_s5_c_re, s5_c_im=v_s5_c_im, s5_d=v_s5_d,
               s5_w_glu=v_s5_w_glu, m_norm_g=v_m_norm_g, m_w_out=v_m_w_out, w_o=v_w_o, ln1_g=v_ln1_g, ln1_b=v_ln1_b,
               w_up=v_w_up, b_up=v_b_up, w_down=v_w_down, ln2_g=v_ln2_g, ln2_b=v_ln2_b)
    d = x.shape[-1]
    chip = 2 * lax.axis_index("x") + lax.axis_index("y")

    gw = dict(zip(_SMALL_SHARDED, _gather_chips([meta_tokens, qk_conv_w[0]])))
    own_w_in = _bf(w_in[0])
    fsend, frecv, fsrc, fland, ftok = _xchg_start([own_w_in], [lax.empty((4,) + own_w_in.shape, BF16)],
                                                  name="gather_w_in_start", scatter=False, dep=gw["qk_conv_w"])
    late_names = tuple(n for n in _BIG if n != "w_in")
    cat = lambda a: jnp.transpose(a, (1, 0, 2)).reshape(a.shape[1], 4 * a.shape[2])
    w = dict(
        meta_tokens=cat(gw["meta_tokens"]), ln0_g=ln0_g[None], ln0_b=_tie(ln0_b[None], ftok),
        qk_conv_w=cat(gw["qk_conv_w"]), qk_conv_b=qk_conv_b,
        s5_lambda_re=s5_lambda_re[0], s5_lambda_im=s5_lambda_im[0], s5_log_dt=s5_log_dt[0][:, None],
        s5_b_re=s5_b_re[0], s5_b_im=s5_b_im[0], s5_c_re=s5_c_re[0], s5_c_im=s5_c_im[0], s5_d=s5_d,
        m_norm_g=m_norm_g, ln1_g=ln1_g, ln1_b=ln1_b, b_up=b_up, ln2_g=ln2_g, ln2_b=ln2_b)
    in_flight = {}

    def place_own(src, land):
        return lax.dynamic_update_slice(land, src[None], (chip,) + (0,) * src.ndim)

    def early(after):
        src, land = _xchg_wait(fsend, frecv, fsrc, fland, after, name="gather_w_in_wait", scatter=False)
        late_src = [_bf(wts[n][0]) for n in late_names]
        st = _xchg_start(late_src, [lax.empty((4,) + a.shape, a.dtype) for a in late_src], name="gather_late_start",
                         scatter=False, dep=src[0])
        in_flight["late"] = st[:4]
        return dict(w_in=_w_in_from_slots(place_own(src[0], land[0]), IN_CHUNK), b_in=_tie(_to_pad_cols(b_in), st[4]))

    def late(after):
        src, land = _xchg_wait(*in_flight["late"], after, name="gather_late_wait", scatter=False)
        full = {n: place_own(s, ld) for n, s, ld in zip(late_names, src, land)}
        return dict(s5_w_glu=full["s5_w_glu"], m_w_out=full["m_w_out"].reshape(d, d), w_o=full["w_o"].reshape(d, d),
                    w_up=full["w_up"], w_down=full["w_down"].reshape(4 * d, d))

    flying = []

    def ready(names, g):
        parts = dict(
            w_in=lambda: _slots_from_w_in(g["w_in"][0]), s5_w_glu=lambda: g["s5_w_glu"],
            m_w_out=lambda: g["m_w_out"].reshape(4, d // 4, d), w_o=lambda: g["w_o"].reshape(4, d // 4, d),
            w_up=lambda: g["w_up"], w_down=lambda: g["w_down"].reshape(4, d, d))
        src = [parts[n]() for n in names]
        land = [lax.empty((3,) + a.shape[1:], a.dtype) for a in src]
        st = _xchg_start(src, land, name="scatter_" + names[0] + "_start", scatter=True)
        flying.append((names,) + st[:4])
        return st[4]

    loss, grad_x, g = _local_step(x, loss_target, w, early, late, ready)
    g["b_in"] = _from_pad_cols(g["b_in"])

    res = {}

    def flat(a):
        return jnp.swapaxes(a, -1, -2).reshape(a.shape[:-2] + (-1, 128))

    def unflat(y, shape):
        return jnp.swapaxes(y.reshape(shape[:-2] + (shape[-1], shape[-2])), -1, -2)

    def finish(groups, after, tag):
        mine = {}
        for names, send, recv, src, land in groups:
            src, land = _xchg_wait(send, recv, src, land, after, name="scatter_" + names[0] + "_wait", scatter=True)
            for n, s, ld in zip(names, src, land):
                mine[n] = _sum_slots(lax.dynamic_index_in_dim(s, chip, 0, keepdims=False), ld)
        theirs = _swap_cores(list(mine.values()), name="swap_cores_" + tag)
        for n, t in zip(mine, theirs):
            if n == "w_in":
                res[n] = [unflat(r, wts[n].shape) for r in _adamw(flat(wts[n]), flat(mom[n]), flat(var[n]),
                                                                  flat(mine[n]), flat(t))]
            else:
                res[n] = _adamw(wts[n], mom[n], var[n], mine[n], t)

    finish(flying[:-1], g["ln0_g"], "a")

    small_shapes = [(1, 128)] + [wts[n].shape for n in _SMALL] + [g[n].shape for n in _SMALL_SHARDED]
    packed = _pack([loss] + [g[n] for n in _SMALL] + [g[n] for n in _SMALL_SHARDED])
    tot = _unpack(_allreduce_small(packed, dep=res["w_o"][3]), small_shapes)
    loss_out = tot[0][0, 0]
    gsm = dict(zip(_SMALL + _SMALL_SHARDED, tot[1:]))
    for n in _SMALL_SHARDED:
        cols = wts[n].shape[-1]
        gsm[n] = lax.dynamic_slice_in_dim(gsm[n], chip * cols, cols, axis=1).reshape(wts[n].shape)

    names = _SMALL + _SMALL_SHARDED
    shapes = [wts[n].shape for n in names]
    pk = lambda dct: _pack([dct[n] for n in names])
    small_out = _adamw(pk(wts), pk(mom), pk(var), pk(gsm))
    small_res = [_unpack(r, shapes) for r in small_out]
    for j, n in enumerate(names):
        res[n] = [small_res[q][j] for q in range(4)]
    finish(flying[-1:], small_out[0], "b")

    return (loss_out, grad_x, *[res[n][0] for n in _ORDER], *[res[n][1] for n in _ORDER],
            *[res[n][2] for n in _ORDER], *[res[n][3] for n in _ORDER])
```

```python
import functools
import math

import jax
import jax.numpy as jnp
from jax import lax
from jax.experimental import pallas as pl
from jax.experimental.pallas import tpu as pltpu

F32 = jnp.float32
BF16 = jnp.bfloat16
HI = lax.Precision.HIGHEST

N_META = 16
M_HEADS = 4
M_CHUNK = 128
PAD = M_CHUNK - N_META
CONV_W = 4
HALO_ROWS = 16
S5_GROUP = 16
S5_STATE = 64
S5_KCH = 4
LN_EPS = 1e-5
ALPHA = 2.0 ** 0.25
NEG = -1e30
ADAM_LR, ADAM_B1, ADAM_B2, ADAM_EPS, ADAM_WD, ADAM_STEP = 0.001, 0.9, 0.999, 1e-08, 0.01, 10

O_OFF, GS_OFF, GM_OFF, V_OFF, Q_OFF, K_OFF, U_OFF, G_OFF, NP = 0, 1024, 2048, 3072, 4096, 4608, 5120, 5632, 5760

NN = ((1,), (0,))
NT = ((1,), (1,))
TN = ((0,), (0,))


def _dot(a, b, dims=NN, prec=None):
    return lax.dot_general(a, b, (dims, ((), ())), preferred_element_type=F32, precision=prec)


def _bf(x):
    return x.astype(BF16)


def _sig(x):
    return 0.5 * jnp.tanh(0.5 * x) + 0.5


def _pcall(body, **kw):
    return pl.pallas_call(body, **kw)


def _cp(sem=None, vmem_mb=None):
    kw = {}
    if sem is not None:
        kw["dimension_semantics"] = sem
    if vmem_mb is not None:
        kw["vmem_limit_bytes"] = vmem_mb << 20
    return pltpu.CompilerParams(**kw)


def _row_tile(n, want, mult=16):
    best = None
    for t in range(mult, want + 1, mult):
        if n % t == 0:
            best = t
    assert best is not None, (n, want)
    return best


def _resident(shape):
    nd = len(shape)
    return pl.BlockSpec(shape, lambda *_: (0,) * nd, pipeline_mode=pl.Buffered(1))


def _const(shape):
    nd = len(shape)
    return pl.BlockSpec(shape, lambda *_: (0,) * nd)


def _ln_fwd(x, g, b):
    mu = jnp.mean(x, axis=-1, keepdims=True)
    xc = x - mu
    var = jnp.mean(xc * xc, axis=-1, keepdims=True)
    rstd = lax.rsqrt(var + LN_EPS)
    xhat = xc * rstd
    return xhat * g + b, xhat, rstd


def _ln_bwd(dy, xhat, rstd, g):
    dxh = dy * g
    m1 = jnp.mean(dxh, axis=-1, keepdims=True)
    m2 = jnp.mean(dxh * xhat, axis=-1, keepdims=True)
    return rstd * (dxh - m1 - xhat * m2)


def _colsum(x):
    return jnp.sum(x, axis=0, keepdims=True)


def _to_pad_cols(w):
    u, q, k, v, o, gi, gf, gs, gm = (w[..., 0:512], w[..., 512:1024], w[..., 1024:1536], w[..., 1536:2560],
                                     w[..., 2560:3584], w[..., 3584:3588], w[..., 3588:3592], w[..., 3592:4616],
                                     w[..., 4616:5640])
    z = jnp.zeros(w.shape[:-1] + (NP - G_OFF - 8,), w.dtype)
    return jnp.concatenate([o, gs, gm, v, q, k, u, gi, gf, z], axis=-1)


def _from_pad_cols(w):
    o, gs, gm, v, q, k, u = (w[..., O_OFF:GS_OFF], w[..., GS_OFF:GM_OFF], w[..., GM_OFF:V_OFF], w[..., V_OFF:Q_OFF],
                             w[..., Q_OFF:K_OFF], w[..., K_OFF:U_OFF], w[..., U_OFF:G_OFF])
    gi, gf = w[..., G_OFF:G_OFF + 4], w[..., G_OFF + 4:G_OFF + 8]
    return jnp.concatenate([u, q, k, v, o, gi, gf, gs, gm], axis=-1)


_IN_REF = (("u", 512), ("q", 512), ("k", 512), ("v", 1024), ("o", 1024), ("i", 4), ("f", 4), ("gs", 1024), ("gm", 1024))
_IN_PAD = (("o", O_OFF), ("gs", GS_OFF), ("gm", GM_OFF), ("v", V_OFF), ("q", Q_OFF), ("k", K_OFF), ("u", U_OFF),
           ("i", G_OFF), ("f", G_OFF + 4))


def _in_ref_ranges():
    out, off = {}, 0
    for n, s in _IN_REF:
        out[n] = (off, off + s)
        off += s
    return out, off


def _w_in_from_slots(g, chunk=None):
    rng, total = _in_ref_ranges()
    width = total // g.shape[0]
    cols = []
    for n, _ in _IN_PAD:
        a, b = rng[n]
        while a < b:
            s = a // width
            e = min(b, (s + 1) * width)
            cols.append(g[s][:, a - s * width:e - s * width])
            a = e
    cols.append(jnp.zeros((g.shape[1], NP - G_OFF - 8), g.dtype))
    if chunk is None:
        return jnp.concatenate(cols, axis=1)
    chunks, cur, room = [], [], chunk
    for c in cols:
        while c.shape[1] > 0:
            take = min(room, c.shape[1])
            cur.append(c[:, :take])
            c, room = c[:, take:], room - take
            if room == 0:
                chunks.append(jnp.concatenate(cur, axis=1))
                cur, room = [], chunk
    assert not cur
    return jnp.stack(chunks, axis=0)


def _slots_from_w_in(wp, nslot=4):
    rng, total = _in_ref_ranges()
    width = total // nslot
    pad_off = dict(_IN_PAD)
    slots = []
    for s in range(nslot):
        lo, hi = s * width, (s + 1) * width
        cols = []
        for n, _ in _IN_REF:
            a, b = rng[n]
            x0, x1 = max(a, lo), min(b, hi)
            if x0 < x1:
                cols.append(wp[:, pad_off[n] + x0 - a:pad_off[n] + x1 - a])
        slots.append(jnp.concatenate(cols, axis=1))
    return jnp.stack(slots, axis=0)


HEAD = PAD + N_META


def _ln0_in(j, x_ref, meta_ref):
    first = jnp.concatenate([jnp.zeros((PAD, meta_ref.shape[1]), F32), meta_ref[...]], axis=0)
    return jnp.where(j == 0, first, x_ref[...])


def _ln0_fwd(x, meta, g, b):
    bsz, seq, d = x.shape
    nb = seq // HEAD + 1

    def body(x_ref, m_ref, g_ref, b_ref, o_ref, ob_ref):
        y, _, _ = _ln_fwd(_ln0_in(pl.program_id(1), x_ref, m_ref), g_ref[...], b_ref[...])
        o_ref[...] = y
        ob_ref[...] = _bf(y)

    row = pl.BlockSpec((HEAD, d), lambda bb, j: (bb * nb + j, 0))
    return _pcall(
        body, name="ln0_fwd", grid=(bsz, nb),
        in_specs=[pl.BlockSpec((None, HEAD, d), lambda bb, j: (bb, jnp.maximum(j - 1, 0), 0)), _const((N_META, d)),
                  _const((1, d)), _const((1, d))],
        out_specs=[row, row],
        out_shape=[jax.ShapeDtypeStruct((bsz * nb * HEAD, d), F32), jax.ShapeDtypeStruct((bsz * nb * HEAD, d), BF16)],
        compiler_params=_cp(("parallel", "arbitrary")),
    )(x, meta, g, b)


def _ln0_bwd(x, meta, dr1, dpw, g):
    bsz, seq, d = x.shape
    nb = seq // HEAD + 1

    def body(x_ref, m_ref, a_ref, c_ref, g_ref, o_ref, dg_ref, db_ref, dm_ref):
        bb = pl.program_id(0)
        j = pl.program_id(1)

        @pl.when((bb == 0) & (j == 0))
        def _():
            dg_ref[...] = jnp.zeros_like(dg_ref)
            db_ref[...] = jnp.zeros_like(db_ref)
            dm_ref[...] = jnp.zeros_like(dm_ref)

        dy = ALPHA * a_ref[...] + c_ref[...]
        _, xhat, rstd = _ln_fwd(_ln0_in(j, x_ref, m_ref), g_ref[...], 0.0)
        dx = _ln_bwd(dy, xhat, rstd, g_ref[...])
        o_ref[...] = dx
        dg_ref[...] += _colsum(dy * xhat)
        db_ref[...] += _colsum(dy)

        @pl.when(j == 0)
        def _():
            dm_ref[...] += dx[PAD:, :]

    row = pl.BlockSpec((HEAD, d), lambda bb, j: (bb * nb + j, 0))
    tok = pl.BlockSpec((None, HEAD, d), lambda bb, j: (bb, jnp.maximum(j - 1, 0), 0))
    return _pcall(
        body, name="ln0_bwd", grid=(bsz, nb),
        in_specs=[tok, _const((N_META, d)), row, row, _const((1, d))],
        out_specs=[tok, _const((1, d)), _const((1, d)), _const((N_META, d))],
        out_shape=[jax.ShapeDtypeStruct((bsz, seq, d), F32), jax.ShapeDtypeStruct((1, d), F32),
                   jax.ShapeDtypeStruct((1, d), F32), jax.ShapeDtypeStruct((N_META, d), F32)],
        compiler_params=_cp(("arbitrary", "arbitrary")),
    )(x, meta, dr1, dpw, g)


IN_CHUNK = 1152


def _chunk_cols(w):
    k, n = w.shape
    return jnp.transpose(w.reshape(k, n // IN_CHUNK, IN_CHUNK), (1, 0, 2))


def _inproj(h0b, w3, bias, lp):
    r, d = h0b.shape
    nj, _, tn = w3.shape
    tm = _row_tile(lp, 832)
    tps = lp // tm

    def body(a_ref, w_ref, b_ref, o_ref, gate_ref):
        i = pl.program_id(0)
        j = pl.program_id(1)
        acc = _dot(a_ref[...], w_ref[j]) + b_ref[...]
        t = (i % tps) * tm + lax.broadcasted_iota(jnp.int32, (tm, 1), 0)
        acc = jnp.where(t >= PAD, acc, 0.0)
        o_ref[...] = _bf(acc)

        @pl.when(j == nj - 1)
        def _():
            gate_ref[...] = acc[:, tn - 128:]

    return _pcall(
        body, name="inproj", grid=(r // tm, nj),
        in_specs=[pl.BlockSpec((tm, d), lambda i, j: (i, 0)), _resident(w3.shape),
                  pl.BlockSpec((1, tn), lambda i, j: (0, j))],
        out_specs=[pl.BlockSpec((tm, tn), lambda i, j: (i, j)), pl.BlockSpec((tm, 128), lambda i, j: (i, 0))],
        out_shape=[jax.ShapeDtypeStruct((r, nj * tn), BF16), jax.ShapeDtypeStruct((r, 128), F32)],
        compiler_params=_cp(("parallel", "arbitrary"), 48),
    )(h0b, w3, bias)


def _mm_tn(a, b, *, name, split=1, colsum=False, tk_want=1408):
    r, m = a.shape
    n = b.shape[1]
    tk = _row_tile(r, tk_want)
    tm = min(m, 1024)
    ns = n // split
    tn = ns
    for cand in (1024, 1152, 640, 512, 128):
        if ns % cand == 0 and cand <= ns:
            tn = cand
            break
    nb = ns // tn
    nk = r // tk

    def body(a_ref, b_ref, o_ref, *rest):
        acc = rest[-1]
        k = pl.program_id(2)

        @pl.when(k == 0)
        def _():
            acc[...] = jnp.zeros_like(acc)

        bt = b_ref[...]
        acc[...] += _dot(_bf(a_ref[...]), _bf(bt), TN)

        @pl.when(k == nk - 1)
        def _():
            o_ref[...] = acc[...]

        if colsum:
            cs_ref = rest[0]

            @pl.when(k == 0)
            def _():
                cs_ref[...] = jnp.zeros_like(cs_ref)

            cs_ref[...] += _colsum(bt.astype(F32))

    out_specs = [pl.BlockSpec((None, tm, tn), lambda i, j, k: (j // nb, i, j % nb))]
    out_shape = [jax.ShapeDtypeStruct((split, m, ns), F32)]
    if colsum:
        assert m == tm
        out_specs.append(pl.BlockSpec((1, tn), lambda i, j, k: (0, j)))
        out_shape.append(jax.ShapeDtypeStruct((1, n), F32))
    res = _pcall(
        body, name=name, grid=(m // tm, n // tn, nk),
        in_specs=[pl.BlockSpec((tk, tm), lambda i, j, k: (k, i)), pl.BlockSpec((tk, tn), lambda i, j, k: (k, j))],
        out_specs=out_specs, out_shape=out_shape,
        scratch_shapes=[pltpu.VMEM((tm, tn), F32)],
        compiler_params=_cp(("parallel", "parallel", "arbitrary"), 56),
    )(a, b)
    return res if colsum else res[0]


def _mm_nt(a, w3, lp, *, name, dep=None):
    r, kdim = a.shape
    nk, n, tk = w3.shape
    assert nk * tk == kdim
    tm = _row_tile(lp, 832)
    deps = [] if dep is None else [dep]

    def body(a_ref, w_ref, *rest):
        o_ref, acc = rest[-2:]
        k = pl.program_id(1)

        @pl.when(k == 0)
        def _():
            acc[...] = jnp.zeros_like(acc)

        acc[...] += _dot(_bf(a_ref[...]), w_ref[k], NT)

        @pl.when(k == nk - 1)
        def _():
            o_ref[...] = acc[...]

    return _pcall(
        body, name=name, grid=(r // tm, nk),
        in_specs=[pl.BlockSpec((tm, tk), lambda i, k: (i, k)), _resident(w3.shape)]
        + [_const(dp_.shape) for dp_ in deps],
        out_specs=pl.BlockSpec((tm, n), lambda i, k: (i, 0)),
        out_shape=jax.ShapeDtypeStruct((r, n), F32),
        scratch_shapes=[pltpu.VMEM((tm, n), F32)],
        compiler_params=_cp(("parallel", "arbitrary"), 48),
    )(a, w3, *deps)


def _s5_prep(lam_re, lam_im, log_dt, b_re_t, b_im_t):
    g, p = lam_re.shape
    h = b_re_t.shape[0]

    def body(lr_ref, li_ref, ldt_ref, br_ref, bi_ref, pr_ref, pi_ref, bbr_ref, bbi_ref):
        lr, li = lr_ref[...], li_ref[...]
        dt = jnp.exp(ldt_ref[...])
        e = jnp.exp(lr * dt)
        ar, ai = e * jnp.cos(li * dt), e * jnp.sin(li * dt)
        den = lr * lr + li * li
        cr = ((ar - 1.0) * lr + ai * li) / den
        ci = (ai * lr - (ar - 1.0) * li) / den
        br, bi = br_ref[...], bi_ref[...]
        bbr_ref[...] = cr[None] * br - ci[None] * bi
        bbi_ref[...] = cr[None] * bi + ci[None] * br
        xr, xi = ar, ai
        pr_ref[0] = xr
        pi_ref[0] = xi
        for t in range(1, 8):
            xr, xi = xr * ar - xi * ai, xr * ai + xi * ar
            pr_ref[t] = xr
            pi_ref[t] = xi

    sd = jax.ShapeDtypeStruct
    return _pcall(body, name="s5_prep",
                  out_shape=[sd((8, g, p), F32), sd((8, g, p), F32), sd((h, g, p), F32), sd((h, g, p), F32)])(
        lam_re, lam_im, log_dt, b_re_t, b_im_t)


def _s5_prep_bwd(lam_re, lam_im, log_dt, b_re_t, b_im_t, da_re, da_im, dbb_re_t, dbb_im_t):
    g, p = lam_re.shape
    h = b_re_t.shape[0]

    def body(lr_ref, li_ref, ldt_ref, br_ref, bi_ref, dar_ref, dai_ref, dbr_ref, dbi_ref,
             glr_ref, gli_ref, gdt_ref, gbr_ref, gbi_ref):
        lr, li = lr_ref[...], li_ref[...]
        dt = jnp.exp(ldt_ref[...])
        e = jnp.exp(lr * dt)
        ar, ai = e * jnp.cos(li * dt), e * jnp.sin(li * dt)
        den = lr * lr + li * li
        cr = ((ar - 1.0) * lr + ai * li) / den
        ci = (ai * lr - (ar - 1.0) * li) / den
        br, bi = br_ref[...], bi_ref[...]
        gr, gi = dbr_ref[...], dbi_ref[...]
        gbr_ref[...] = gr * cr[None] + gi * ci[None]
        gbi_ref[...] = gi * cr[None] - gr * ci[None]
        gcr = jnp.sum(gr * br + gi * bi, axis=0)
        gci = jnp.sum(gi * br - gr * bi, axis=0)
        ilr, ili = lr / den, -li / den
        gar = dar_ref[...] + gcr * ilr + gci * ili
        gai = dai_ref[...] + gci * ilr - gcr * ili
        qr, qi = cr * ilr - ci * ili, cr * ili + ci * ilr
        glr = -(gcr * qr + gci * qi)
        gli = -(gci * qr - gcr * qi)
        gzr = gar * ar + gai * ai
        gzi = gai * ar - gar * ai
        glr_ref[...] = glr + gzr * dt
        gli_ref[...] = gli + gzi * dt
        gdt_ref[...] = jnp.sum(gzr * lr + gzi * li, axis=1, keepdims=True) * dt

    sd = jax.ShapeDtypeStruct
    return _pcall(body, name="s5_prep_bwd",
                  out_shape=[sd((g, p), F32), sd((g, p), F32), sd((g, 1), F32), sd((h, g, p), F32), sd((h, g, p), F32)])(
        lam_re, lam_im, log_dt, b_re_t, b_im_t, da_re, da_im, dbb_re_t, dbb_im_t)


def _cmul(xr, xi, yr, yi):
    return xr * yr - xi * yi, xr * yi + xi * yr


def _dot5(a, b, dims=NN):
    return _dot(_bf(a), _bf(b), dims)


def _s5_fwd(p3, bk, cre, cim, apow, dskip):
    bsz, lp, _ = p3.shape
    tt = _row_tile(lp, 528, 8)
    nt = lp // tt
    nblk = tt // 8
    hw = 512

    def body(u_ref, bk_ref, cre_ref, cim_ref, ap_ref, d_ref, y_ref, xs_ref, car_ref):
        t = pl.program_id(2)

        @pl.when(t == 0)
        def _():
            car_ref[...] = jnp.zeros_like(car_ref)

        u = u_ref[...].astype(F32)
        xs_ref[...] = _dot5(u, bk_ref[...])
        ap = ap_ref[...]
        apr, api = ap[:, :hw], ap[:, hw:]
        rows = lax.broadcasted_iota(jnp.int32, (8, hw), 0)
        lev = [(d, jnp.where(rows < d, 0.0, jnp.broadcast_to(apr[d - 1:d, :], (8, hw))),
                jnp.where(rows < d, 0.0, jnp.broadcast_to(api[d - 1:d, :], (8, hw)))) for d in (1, 2, 4)]

        def blk(i, carry):
            cr, ci = carry
            off = pl.multiple_of(i * 8, 8)
            x = xs_ref[pl.ds(off, 8), :]
            xr, xi = x[:, :hw], x[:, hw:]
            for d, lr, li in lev:
                mr, mi = _cmul(pltpu.roll(xr, d, 0), pltpu.roll(xi, d, 0), lr, li)
                xr, xi = xr + mr, xi + mi
            mr, mi = _cmul(apr, api, cr, ci)
            xr, xi = xr + mr, xi + mi
            xs_ref[pl.ds(off, 8), :] = jnp.concatenate([xr, xi], axis=1)
            return xr[7:8, :], xi[7:8, :]

        c0 = car_ref[...]
        cr, ci = lax.fori_loop(0, nblk, blk, (c0[0:1, :hw], c0[0:1, hw:]))
        car_ref[...] = jnp.broadcast_to(jnp.concatenate([cr, ci], axis=1), car_ref.shape)
        xs = xs_ref[...]
        y_ref[...] = (_dot5(xs[:, :hw], cre_ref[...]) - _dot5(xs[:, hw:], cim_ref[...])
                      + d_ref[...] * u)

    ub = U_OFF // 128
    return _pcall(
        body, name="s5_fwd", grid=(S5_KCH, bsz, nt),
        in_specs=[pl.BlockSpec((None, tt, 128), lambda k, b, t: (b, t, ub + k)),
                  pl.BlockSpec((None, 128, 2 * hw), lambda k, b, t: (k, 0, 0)),
                  pl.BlockSpec((None, hw, 128), lambda k, b, t: (k, 0, 0)),
                  pl.BlockSpec((None, hw, 128), lambda k, b, t: (k, 0, 0)),
                  pl.BlockSpec((None, 8, 2 * hw), lambda k, b, t: (k, 0, 0)),
                  pl.BlockSpec((1, 128), lambda k, b, t: (0, k))],
        out_specs=[pl.BlockSpec((None, tt, 128), lambda k, b, t: (b, t, k)),
                   pl.BlockSpec((None, None, tt, 2 * hw), lambda k, b, t: (b, k, t, 0))],
        out_shape=[jax.ShapeDtypeStruct((bsz, lp, S5_KCH * 128), F32),
                   jax.ShapeDtypeStruct((bsz, S5_KCH, lp, 2 * hw), F32)],
        scratch_shapes=[pltpu.VMEM((8, 2 * hw), F32)],
        compiler_params=_cp(("parallel", "parallel", "arbitrary"), 40),
    )(p3, bk, cre, cim, apow, dskip)


def _s5_bwd(dp3, p3, dy3, xs, bk, cre, cim, apow_rev, dskip):
    bsz, lp, _ = p3.shape
    tt = _row_tile(lp, 528, 8)
    nt = lp // tt
    nblk = tt // 8
    hw = 512
    tb = tt // 8

    def body(dp_any, u_ref, dy_ref, xs_ref, halo_ref, bkt_ref, cre_ref, cim_ref, ap_ref, d_ref,
             du_ref, dbk_ref, dcre_ref, dcim_ref, da_ref, dd_ref, g_ref, ext_ref, car_ref):
        b = pl.program_id(1)
        t = pl.program_id(2)
        tidx = nt - 1 - t

        @pl.when(t == 0)
        def _():
            car_ref[...] = jnp.zeros_like(car_ref)

        @pl.when((b == 0) & (t == 0))
        def _():
            dbk_ref[...] = jnp.zeros_like(dbk_ref)
            dcre_ref[...] = jnp.zeros_like(dcre_ref)
            dcim_ref[...] = jnp.zeros_like(dcim_ref)
            da_ref[...] = jnp.zeros_like(da_ref)
            dd_ref[...] = jnp.zeros_like(dd_ref)

        u = u_ref[...].astype(F32)
        dy = dy_ref[...]
        g_ref[:, :hw] = _dot5(dy, cre_ref[...])
        g_ref[:, hw:] = -_dot5(dy, cim_ref[...])
        ap = ap_ref[...]
        apr, api = ap[:, :hw], -ap[:, hw:]
        rows = lax.broadcasted_iota(jnp.int32, (8, hw), 0)
        lev = [(d, jnp.where(rows >= 8 - d, 0.0, jnp.broadcast_to(apr[8 - d:9 - d, :], (8, hw))),
                jnp.where(rows >= 8 - d, 0.0, jnp.broadcast_to(api[8 - d:9 - d, :], (8, hw)))) for d in (1, 2, 4)]

        def blk(i, carry):
            cr, ci = carry
            off = pl.multiple_of((nblk - 1 - i) * 8, 8)
            x = g_ref[pl.ds(off, 8), :]
            xr, xi = x[:, :hw], x[:, hw:]
            for d, lr, li in lev:
                mr, mi = _cmul(pltpu.roll(xr, 8 - d, 0), pltpu.roll(xi, 8 - d, 0), lr, li)
                xr, xi = xr + mr, xi + mi
            mr, mi = _cmul(apr, api, cr, ci)
            xr, xi = xr + mr, xi + mi
            g_ref[pl.ds(off, 8), :] = jnp.concatenate([xr, xi], axis=1)
            return xr[0:1, :], xi[0:1, :]

        c0 = car_ref[...]
        cr, ci = lax.fori_loop(0, nblk, blk, (c0[0:1, :hw], c0[0:1, hw:]))
        car_ref[...] = jnp.broadcast_to(jnp.concatenate([cr, ci], axis=1), car_ref.shape)

        gg = g_ref[...]
        du = _dot5(gg, bkt_ref[...]) + d_ref[...] * dy
        trow = tidx * tt + lax.broadcasted_iota(jnp.int32, (tt, 1), 0)
        du_ref[...] = jnp.where(trow >= PAD, du, 0.0).astype(du_ref.dtype)
        dbk_ref[...] += _dot5(u, gg, TN)
        xsv = xs_ref[...]
        dcre_ref[...] += _dot5(dy, xsv[:, :hw], TN)
        dcim_ref[...] -= _dot5(dy, xsv[:, hw:], TN)
        dd_ref[...] += _colsum(dy * u)
        ext_ref[0:8, :] = jnp.where(tidx == 0, 0.0, halo_ref[...])
        ext_ref[8:, :] = xsv
        xp = ext_ref[pl.ds(7, tt), :]
        gr, gi, pr, pi = gg[:, :hw], gg[:, hw:], xp[:, :hw], xp[:, hw:]
        da_ref[:, :hw] += _colsum(gr * pr + gi * pi)
        da_ref[:, hw:] += _colsum(gi * pr - gr * pi)

    ub = U_OFF // 128
    sd = jax.ShapeDtypeStruct
    rt = lambda t: nt - 1 - t
    tr = lambda a: jnp.swapaxes(a, 1, 2)
    res = _pcall(
        body, name="s5_bwd", grid=(S5_KCH, bsz, nt),
        in_specs=[pl.BlockSpec(memory_space=pl.ANY),
                  pl.BlockSpec((None, tt, 128), lambda k, b, t: (b, rt(t), ub + k)),
                  pl.BlockSpec((None, tt, 128), lambda k, b, t: (b, rt(t), k)),
                  pl.BlockSpec((None, None, tt, 2 * hw), lambda k, b, t: (b, k, rt(t), 0)),
                  pl.BlockSpec((None, None, 8, 2 * hw), lambda k, b, t: (b, k, jnp.maximum(rt(t) * tb - 1, 0), 0)),
                  pl.BlockSpec((None, 2 * hw, 128), lambda k, b, t: (k, 0, 0)),
                  pl.BlockSpec((None, 128, hw), lambda k, b, t: (k, 0, 0)),
                  pl.BlockSpec((None, 128, hw), lambda k, b, t: (k, 0, 0)),
                  pl.BlockSpec((None, 8, 2 * hw), lambda k, b, t: (k, 0, 0)),
                  pl.BlockSpec((1, 128), lambda k, b, t: (0, k))],
        out_specs=[pl.BlockSpec((None, tt, 128), lambda k, b, t: (b, rt(t), ub + k)),
                   pl.BlockSpec((None, 128, 2 * hw), lambda k, b, t: (k, 0, 0)),
                   pl.BlockSpec((None, 128, hw), lambda k, b, t: (k, 0, 0)),
                   pl.BlockSpec((None, 128, hw), lambda k, b, t: (k, 0, 0)),
                   pl.BlockSpec((None, 1, 2 * hw), lambda k, b, t: (k, 0, 0)),
                   pl.BlockSpec((1, 128), lambda k, b, t: (0, k))],
        out_shape=[sd(dp3.shape, dp3.dtype), sd((S5_KCH, 128, 2 * hw), F32), sd((S5_KCH, 128, hw), F32),
                   sd((S5_KCH, 128, hw), F32), sd((S5_KCH, 1, 2 * hw), F32), sd((1, S5_KCH * 128), F32)],
        scratch_shapes=[pltpu.VMEM((tt, 2 * hw), F32), pltpu.VMEM((tt + 8, 2 * hw), F32), pltpu.VMEM((8, 2 * hw), F32)],
        input_output_aliases={0: 0},
        compiler_params=_cp(("arbitrary", "arbitrary", "arbitrary"), 48),
    )(dp3, p3, dy3, xs, xs, tr(bk), tr(cre), tr(cim), apow_rev, dskip)
    return res[0], res[1], tr(res[2]), tr(res[3]), res[4], res[5]


_G0 = math.sqrt(2.0 / math.pi)
_G1 = 0.044715


def _gelu(y):
    return 0.5 * y * (1.0 + jnp.tanh(_G0 * (y + _G1 * y * y * y)))


def _gelu_grad(y):
    th = jnp.tanh(_G0 * (y + _G1 * y * y * y))
    return 0.5 * (1.0 + th) + 0.5 * y * (1.0 - th * th) * _G0 * (1.0 + 3.0 * _G1 * y * y)


def _glu_fwd(y_s5, wglu_g, lp):
    r, w = y_s5.shape
    tm = _row_tile(lp, 416)
    cw = wglu_g.shape[2]

    def body(y_ref, w_ref, gy_ref, z_ref, o_ref):
        gy = _bf(_gelu(y_ref[...]))
        gy_ref[...] = gy
        zs = [_dot(gy, w_ref[s]) for s in range(4)]
        for s in range(4):
            z_ref[:, s * cw:(s + 1) * cw] = zs[s]
        o_ref[:, :cw] = zs[0] * _sig(zs[2])
        o_ref[:, cw:] = zs[1] * _sig(zs[3])

    sd = jax.ShapeDtypeStruct
    return _pcall(
        body, name="glu_fwd", grid=(r // tm,),
        in_specs=[pl.BlockSpec((tm, w), lambda i: (i, 0)), _resident(wglu_g.shape)],
        out_specs=[pl.BlockSpec((tm, w), lambda i: (i, 0)), pl.BlockSpec((tm, 4 * cw), lambda i: (i, 0)),
                   pl.BlockSpec((tm, 2 * cw), lambda i: (i, 0))],
        out_shape=[sd((r, w), BF16), sd((r, 4 * cw), F32), sd((r, 2 * cw), F32)],
        compiler_params=_cp(("parallel",), 40),
    )(y_s5, wglu_g)


def _glu_bwd(dyg, z, y_s5, wglu_g, lp):
    r, w = y_s5.shape
    tm = _row_tile(lp, 416)
    cw = wglu_g.shape[2]

    def body(d_ref, z_ref, y_ref, w_ref, dz_ref, dy_ref):
        d = d_ref[...]
        zz = z_ref[...]
        acc = jnp.zeros((tm, w), F32)
        for s in range(2):
            z1 = zz[:, s * cw:(s + 1) * cw]
            sg = _sig(zz[:, (2 + s) * cw:(3 + s) * cw])
            dd = d[:, s * cw:(s + 1) * cw]
            dz1 = _bf(dd * sg)
            dz2 = _bf(dd * z1 * sg * (1.0 - sg))
            dz_ref[:, s * cw:(s + 1) * cw] = dz1
            dz_ref[:, (2 + s) * cw:(3 + s) * cw] = dz2
            acc += _dot(dz1, w_ref[s], NT) + _dot(dz2, w_ref[2 + s], NT)
        dy_ref[...] = acc * _gelu_grad(y_ref[...])

    sd = jax.ShapeDtypeStruct
    return _pcall(
        body, name="glu_bwd", grid=(r // tm,),
        in_specs=[pl.BlockSpec((tm, 2 * cw), lambda i: (i, 0)), pl.BlockSpec((tm, 4 * cw), lambda i: (i, 0)),
                  pl.BlockSpec((tm, w), lambda i: (i, 0)), _resident(wglu_g.shape)],
        out_specs=[pl.BlockSpec((tm, 4 * cw), lambda i: (i, 0)), pl.BlockSpec((tm, w), lambda i: (i, 0))],
        out_shape=[sd((r, 4 * cw), BF16), sd((r, w), F32)],
        compiler_params=_cp(("parallel",), 40),
    )(dyg, z, y_s5, wglu_g)


def _conv_fwd(p3, cw, cb):
    bsz, lp, _ = p3.shape
    tt = _row_tile(lp, 416)
    nt = lp // tt
    tb = tt // 8
    c = cw.shape[1]
    qb = Q_OFF // c

    hr = HALO_ROWS
    off = hr - (CONV_W - 1)

    def body(x_ref, halo_ref, w_ref, b_ref, pre_ref, act_ref, ext_ref):
        t = pl.program_id(1)
        ext_ref[0:hr, :] = jnp.where(t == 0, 0.0, halo_ref[...].astype(F32))
        ext_ref[hr:, :] = x_ref[...].astype(F32)
        w = w_ref[...]
        acc = b_ref[...] + w[0:1, :] * ext_ref[pl.ds(off, tt), :]
        for j in range(1, CONV_W):
            acc = acc + w[j:j + 1, :] * ext_ref[pl.ds(off + j, tt), :]
        pre_ref[...] = acc
        act_ref[...] = acc * _sig(acc)

    sd = jax.ShapeDtypeStruct
    return _pcall(
        body, name="conv_fwd", grid=(bsz, nt),
        in_specs=[pl.BlockSpec((None, tt, c), lambda b, t: (b, t, qb)),
                  pl.BlockSpec((None, hr, c), lambda b, t: (b, jnp.maximum(t * (tt // hr) - 1, 0), qb)),
                  _const((CONV_W, c)), _const((1, c))],
        out_specs=[pl.BlockSpec((None, tt, c), lambda b, t: (b, t, 0))] * 2,
        out_shape=[sd((bsz, lp, c), F32)] * 2,
        scratch_shapes=[pltpu.VMEM((tt + hr, c), F32)],
        compiler_params=_cp(("parallel", "parallel")),
    )(p3, p3, cw, cb)


def _conv_bwd(dp3, p3, dact3, pre3, cw):
    bsz, lp, _ = p3.shape
    tt = _row_tile(lp, 416)
    nt = lp // tt
    tb = tt // 8
    c = cw.shape[1]
    qb = Q_OFF // c

    hr = HALO_ROWS
    off = hr - (CONV_W - 1)

    def silu_grad(x):
        s = _sig(x)
        return s * (1.0 + x * (1.0 - s))

    def body(dp_any, x_ref, xh_ref, d_ref, dh_ref, pre_ref, preh_ref, w_ref, o_ref, dw_ref, db_ref, ext_ref, dext_ref):
        b = pl.program_id(0)
        t = pl.program_id(1)

        @pl.when((b == 0) & (t == 0))
        def _():
            dw_ref[...] = jnp.zeros_like(dw_ref)
            db_ref[...] = jnp.zeros_like(db_ref)

        dc = d_ref[...] * silu_grad(pre_ref[...])
        dch = jnp.where(t == nt - 1, 0.0, dh_ref[...] * silu_grad(preh_ref[...]))
        dext_ref[0:tt, :] = dc
        dext_ref[tt:, :] = dch
        ext_ref[0:hr, :] = jnp.where(t == 0, 0.0, xh_ref[...].astype(F32))
        ext_ref[hr:, :] = x_ref[...].astype(F32)
        w = w_ref[...]
        acc = w[CONV_W - 1:CONV_W, :] * dc
        for j in range(CONV_W - 1):
            acc = acc + w[j:j + 1, :] * dext_ref[pl.ds(CONV_W - 1 - j, tt), :]
        trow = t * tt + lax.broadcasted_iota(jnp.int32, (tt, 1), 0)
        o_ref[...] = jnp.where(trow >= PAD, acc, 0.0).astype(o_ref.dtype)
        db_ref[...] += _colsum(dc)
        for j in range(CONV_W):
            dw_ref[j:j + 1, :] += _colsum(dc * ext_ref[pl.ds(off + j, tt), :])

    sd = jax.ShapeDtypeStruct
    nxt = lambda t: jnp.minimum((t + 1) * tb, lp // 8 - 1)
    return _pcall(
        body, name="conv_bwd", grid=(bsz, nt),
        in_specs=[pl.BlockSpec(memory_space=pl.ANY),
                  pl.BlockSpec((None, tt, c), lambda b, t: (b, t, qb)),
                  pl.BlockSpec((None, hr, c), lambda b, t: (b, jnp.maximum(t * (tt // hr) - 1, 0), qb)),
                  pl.BlockSpec((None, tt, c), lambda b, t: (b, t, 0)),
                  pl.BlockSpec((None, 8, c), lambda b, t: (b, nxt(t), 0)),
                  pl.BlockSpec((None, tt, c), lambda b, t: (b, t, 0)),
                  pl.BlockSpec((None, 8, c), lambda b, t: (b, nxt(t), 0)),
                  _const((CONV_W, c))],
        out_specs=[pl.BlockSpec((None, tt, c), lambda b, t: (b, t, qb)), _const((CONV_W, c)), _const((1, c))],
        out_shape=[sd(dp3.shape, dp3.dtype), sd((CONV_W, c), F32), sd((1, c), F32)],
        scratch_shapes=[pltpu.VMEM((tt + hr, c), F32), pltpu.VMEM((tt + 8, c), F32)],
        input_output_aliases={0: 0},
        compiler_params=_cp(("arbitrary", "arbitrary")),
    )(dp3, p3, p3, dact3, dact3, pre3, pre3, cw)


def _mlstm_gates(g, h_idx, c_idx, lc):
    lane = lax.broadcasted_iota(jnp.int32, g.shape, 1)
    i_col = jnp.sum(jnp.where(lane == h_idx, g, 0.0), axis=1, keepdims=True)
    f_col = jnp.sum(jnp.where(lane == M_HEADS + h_idx, g, 0.0), axis=1, keepdims=True)
    row = lax.broadcasted_iota(jnp.int32, (lc, 1), 0)
    valid = (c_idx * lc + row) >= PAD
    li = jnp.where(valid, i_col, NEG)
    lf = jnp.where(valid, jnp.minimum(f_col, 0.0) - jnp.log(1.0 + jnp.exp(-jnp.abs(f_col))), 0.0)
    r2 = lax.broadcasted_iota(jnp.int32, (lc, lc), 0)
    c2 = lax.broadcasted_iota(jnp.int32, (lc, lc), 1)
    eye = r2 == c2
    tril = r2 >= c2
    to_row = lambda col: jnp.sum(jnp.where(eye, col, 0.0), axis=0, keepdims=True)
    lf_row = to_row(lf)
    b_col = jnp.sum(jnp.where(tril, lf_row, 0.0), axis=1, keepdims=True)
    b_row = to_row(b_col)
    li_row = to_row(li)
    d_mat = jnp.where(tril, b_col - b_row + li_row, NEG)
    return dict(f_col=f_col, valid=valid, li=li, b_col=b_col, d_mat=d_mat, eye=eye, r2=r2, c2=c2, row=row,
                to_row=to_row)


def _mlstm_chunk(q, ks, v, gq, c_st, n_st, m_st, lc):
    b_col, d_mat = gq["b_col"], gq["d_mat"]
    m_inter = b_col + m_st
    m_row = jnp.maximum(m_inter, jnp.max(d_mat, axis=1, keepdims=True))
    w_intra = jnp.exp(d_mat - m_row)
    w_inter = jnp.exp(m_inter - m_row)
    qb, kb, vb, cb = _bf(q), _bf(ks), _bf(v), _bf(c_st)
    s = _dot(qb, kb, NT) * w_intra
    qc = _dot(qb, cb)
    num = _dot(_bf(s), vb) + w_inter * qc
    qn = jnp.sum(q * n_st, axis=1, keepdims=True)
    den = jnp.sum(s, axis=1, keepdims=True) + w_inter * qn
    e = jnp.exp(-m_row)
    nn = jnp.maximum(jnp.abs(den), e)
    b_last = b_col[lc - 1:lc, :]
    g_log = b_last - b_col + gq["li"]
    m_new = jnp.maximum(b_last + m_st, jnp.max(g_log, axis=0, keepdims=True))
    w_k = jnp.exp(g_log - m_new)
    decay = jnp.exp(b_last + m_st - m_new)
    return dict(w_intra=w_intra, w_inter=w_inter, qb=qb, kb=kb, vb=vb, cb=cb, s=s, qc=qc, num=num, qn=qn, den=den,
                e=e, nn=nn, m_new=m_new, w_k=w_k, decay=decay)


def _chunks_per_step(nc):
    return max(c for c in (3, 2, 1) if nc % c == 0)


def _mlstm_fwd(qk3, p3, pg3):
    bsz, lp, _ = p3.shape
    lc = M_CHUNK
    nc = lp // lc
    dk, dv = 128, 256
    scale = dk ** -0.5

    cps = _chunks_per_step(nc)
    rows = cps * lc

    def body(q_ref, k_ref, v_ref, g_ref, h_ref, cs_ref, ns_ref, ms_ref, c_sc, n_sc, m_sc):
        st = pl.program_id(1)

        @pl.when(st == 0)
        def _():
            c_sc[...] = jnp.zeros_like(c_sc)
            n_sc[...] = jnp.zeros_like(n_sc)
            m_sc[...] = jnp.zeros_like(m_sc)

        for j in range(cps):
            rs = slice(j * lc, (j + 1) * lc)
            g = g_ref[rs, :]
            for hh in range(M_HEADS):
                c_st, n_st, m_all = c_sc[hh], n_sc[hh], m_sc[hh]
                cs_ref[hh, j] = c_st
                ns_ref[hh, j] = n_st
                ms_ref[hh, j] = m_all
                m_st = m_all[:, 0:1]
                q = q_ref[rs, hh * dk:(hh + 1) * dk]
                ks = k_ref[rs, hh * dk:(hh + 1) * dk] * scale
                v = v_ref[rs, hh * dv:(hh + 1) * dv]
                gq = _mlstm_gates(g, hh, st * cps + j, lc)
                f = _mlstm_chunk(q, ks, v, gq, c_st, n_st, m_st, lc)
                h_ref[rs, hh * dv:(hh + 1) * dv] = f["num"] / f["nn"]
                kw = ks * f["w_k"]
                c_sc[hh] = f["decay"] * c_st + _dot(_bf(kw), f["vb"], TN)
                n_sc[hh] = f["decay"] * n_st + _colsum(kw)
                m_sc[hh] = jnp.broadcast_to(f["m_new"], (1, 128))

    sd = jax.ShapeDtypeStruct
    nh = M_HEADS
    return _pcall(
        body, name="mlstm_fwd", grid=(bsz, nc // cps),
        in_specs=[pl.BlockSpec((None, rows, nh * dk), lambda b, c: (b, c, 0)),
                  pl.BlockSpec((None, rows, nh * dk), lambda b, c: (b, c, 1)),
                  pl.BlockSpec((None, rows, nh * dv), lambda b, c: (b, c, V_OFF // (nh * dv))),
                  pl.BlockSpec((None, rows, 128), lambda b, c: (b, c, 0))],
        out_specs=[pl.BlockSpec((None, rows, nh * dv), lambda b, c: (b, c, 0)),
                   pl.BlockSpec((None, nh, cps, dk, dv), lambda b, c: (b, 0, c, 0, 0)),
                   pl.BlockSpec((None, nh, cps, 1, dk), lambda b, c: (b, 0, c, 0, 0)),
                   pl.BlockSpec((None, nh, cps, 1, 128), lambda b, c: (b, 0, c, 0, 0))],
        out_shape=[sd((bsz, lp, nh * dv), F32), sd((bsz, nh, nc, dk, dv), F32),
                   sd((bsz, nh, nc, 1, dk), F32), sd((bsz, nh, nc, 1, 128), F32)],
        scratch_shapes=[pltpu.VMEM((nh, dk, dv), F32), pltpu.VMEM((nh, 1, dk), F32), pltpu.VMEM((nh, 1, 128), F32)],
        compiler_params=_cp(("parallel", "arbitrary")),
    )(qk3, qk3, p3, pg3)


def _mlstm_bwd(dp3, qk3, p3, pg3, dh3, cs, ns, ms):
    bsz, lp, _ = p3.shape
    lc = M_CHUNK
    nc = lp // lc
    dk, dv = 128, 256
    scale = dk ** -0.5

    cps = _chunks_per_step(nc)
    nst = nc // cps
    rows = cps * lc

    def body(dp_any, q_ref, k_ref, v_ref, g_ref, dh_ref, cs_ref, ns_ref, ms_ref,
             dv_ref, dqk_ref, dg_ref, dc_sc, dn_sc):
        t = pl.program_id(1)
        st = nst - 1 - t

        @pl.when(t == 0)
        def _():
            dc_sc[...] = jnp.zeros_like(dc_sc)
            dn_sc[...] = jnp.zeros_like(dn_sc)

        lane = lax.broadcasted_iota(jnp.int32, (lc, 128), 1)
        for j in reversed(range(cps)):
            rs = slice(j * lc, (j + 1) * lc)
            g = g_ref[rs, :]
            dgate = jnp.zeros((lc, 128), F32)
            for hh in range(M_HEADS):
                dgate = head(hh, j, rs, st * cps + j, g, lane, dgate, q_ref, k_ref, v_ref, dh_ref, cs_ref, ns_ref,
                             ms_ref, dv_ref, dqk_ref, dc_sc, dn_sc)
            dg_ref[rs, :] = dgate.astype(dg_ref.dtype)

    def head(hh, j, sl, c, g, lane, dgate, q_ref, k_ref, v_ref, dh_ref, cs_ref, ns_ref, ms_ref, dv_ref, dqk_ref,
             dc_sc, dn_sc):
        c_st, n_st = cs_ref[hh, j], ns_ref[hh, j]
        m_st = ms_ref[hh, j][:, 0:1]
        q = q_ref[sl, hh * dk:(hh + 1) * dk]
        ks = k_ref[sl, hh * dk:(hh + 1) * dk] * scale
        v = v_ref[sl, hh * dv:(hh + 1) * dv]
        dh = dh_ref[sl, hh * dv:(hh + 1) * dv]
        gq = _mlstm_gates(g, hh, c, lc)
        f = _mlstm_chunk(q, ks, v, gq, c_st, n_st, m_st, lc)
        eye, r2, c2, row, valid = gq["eye"], gq["r2"], gq["c2"], gq["row"], gq["valid"]
        w_intra, w_inter, s, nn, den = f["w_intra"], f["w_inter"], f["s"], f["nn"], f["den"]
        qb, kb, vb, cb, w_k, decay = f["qb"], f["kb"], f["vb"], f["cb"], f["w_k"], f["decay"]
        d_c, d_n = dc_sc[hh], dn_sc[hh]
        d_cb = _bf(d_c)

        hout = f["num"] / nn
        dnum = dh / nn
        d_nn = -jnp.sum(dh * hout, axis=1, keepdims=True) / nn
        dden = jnp.where(jnp.abs(den) > f["e"], d_nn * jnp.sign(den), 0.0)
        wdnum = w_inter * dnum
        wdden = w_inter * dden
        ds = _dot(_bf(dnum), vb, NT) + dden
        dsw = _bf(ds * w_intra)
        dq = _dot(dsw, kb) + _dot(_bf(wdnum), cb, NT) + wdden * n_st
        dkw = _dot(vb, d_cb, NT) + d_n
        dks = _dot(dsw, qb, TN) + dkw * w_k
        kw = ks * w_k
        dvv = _dot(_bf(s), _bf(dnum), TN) + _dot(_bf(kw), d_cb)
        dd = ds * s
        rs = jnp.sum(dd, axis=1, keepdims=True)
        cs_col = jnp.sum(jnp.where(eye, jnp.sum(dd, axis=0, keepdims=True), 0.0), axis=1, keepdims=True)
        dwi = jnp.sum(dnum * f["qc"], axis=1, keepdims=True) + dden * f["qn"]
        db = rs - cs_col + dwi * w_inter
        dli = cs_col
        ddecay = jnp.sum(jnp.sum(d_c * c_st, axis=1, keepdims=True), axis=0, keepdims=True) \
            + jnp.sum(d_n * n_st, axis=1, keepdims=True)
        dgl = jnp.sum(dkw * ks, axis=1, keepdims=True) * w_k
        dblast = ddecay * decay + jnp.sum(dgl, axis=0, keepdims=True)
        db = db - dgl + jnp.where(row == lc - 1, dblast, 0.0)
        dli = dli + dgl
        db_row = gq["to_row"](db)
        dlf = jnp.sum(jnp.where(c2 >= r2, db_row, 0.0), axis=1, keepdims=True)
        dlf = jnp.where(valid, dlf, 0.0)
        dgate = jnp.where(lane == hh, jnp.where(valid, dli, 0.0), dgate)
        dgate = jnp.where(lane == M_HEADS + hh, dlf / (1.0 + jnp.exp(gq["f_col"])), dgate)
        dqk_ref[sl, hh * dk:(hh + 1) * dk] = dq
        dqk_ref[sl, (M_HEADS + hh) * dk:(M_HEADS + hh + 1) * dk] = dks * scale
        dv_ref[sl, hh * dv:(hh + 1) * dv] = dvv.astype(dv_ref.dtype)
        dc_sc[hh] = decay * d_c + _dot(qb, _bf(wdnum), TN)
        dn_sc[hh] = decay * d_n + _colsum(q * wdden)
        return dgate

    sd = jax.ShapeDtypeStruct
    nh = M_HEADS
    rc = lambda c: nst - 1 - c
    return _pcall(
        body, name="mlstm_bwd", grid=(bsz, nst),
        in_specs=[pl.BlockSpec(memory_space=pl.ANY),
                  pl.BlockSpec((None, rows, nh * dk), lambda b, c: (b, rc(c), 0)),
                  pl.BlockSpec((None, rows, nh * dk), lambda b, c: (b, rc(c), 1)),
                  pl.BlockSpec((None, rows, nh * dv), lambda b, c: (b, rc(c), V_OFF // (nh * dv))),
                  pl.BlockSpec((None, rows, 128), lambda b, c: (b, rc(c), 0)),
                  pl.BlockSpec((None, rows, nh * dv), lambda b, c: (b, rc(c), 0)),
                  pl.BlockSpec((None, nh, cps, dk, dv), lambda b, c: (b, 0, rc(c), 0, 0)),
                  pl.BlockSpec((None, nh, cps, 1, dk), lambda b, c: (b, 0, rc(c), 0, 0)),
                  pl.BlockSpec((None, nh, cps, 1, 128), lambda b, c: (b, 0, rc(c), 0, 0))],
        out_specs=[pl.BlockSpec((None, rows, nh * dv), lambda b, c: (b, rc(c), V_OFF // (nh * dv))),
                   pl.BlockSpec((None, rows, 2 * nh * dk), lambda b, c: (b, rc(c), 0)),
                   pl.BlockSpec((None, rows, 128), lambda b, c: (b, rc(c), 0))],
        out_shape=[sd(dp3.shape, dp3.dtype), sd((bsz, lp, 2 * nh * dk), F32), sd((bsz, lp, 128), dp3.dtype)],
        scratch_shapes=[pltpu.VMEM((nh, dk, dv), F32), pltpu.VMEM((nh, 1, dk), F32)],
        input_output_aliases={0: 0},
        compiler_params=_cp(("arbitrary", "arbitrary")),
    )(dp3, qk3, qk3, p3, pg3, dh3, cs, ns, ms)


def _headnorm(x):
    dv = x.shape[1] // M_HEADS
    xh, rs = [], []
    for h in range(M_HEADS):
        xx = x[:, h * dv:(h + 1) * dv]
        mu = jnp.mean(xx, axis=-1, keepdims=True)
        xc = xx - mu
        rstd = lax.rsqrt(jnp.mean(xc * xc, axis=-1, keepdims=True) + LN_EPS)
        xh.append(xc * rstd)
        rs.append(rstd)
    return jnp.concatenate(xh, axis=1), rs


def _mix_fwd(hm, p, ys5g, h0, gn, wmo_bf, wo_bf, lp):
    r, d = hm.shape
    tm = _row_tile(lp, 208)

    def body(hm_ref, o_ref, gs_ref, gm_ref, ys_ref, h0_ref, gn_ref, wmo_ref, wo_ref,
             ymin_ref, ym_ref, mix_ref, r1_ref):
        xhat, _ = _headnorm(hm_ref[...])
        ymin = _bf(_sig(o_ref[...].astype(F32)) * (xhat * gn_ref[...]))
        ymin_ref[...] = ymin
        ym = _dot(ymin, wmo_ref[...])
        ym_ref[...] = ym
        mix = _bf(_sig(gs_ref[...].astype(F32)) * ys_ref[...] + _sig(gm_ref[...].astype(F32)) * ym)
        mix_ref[...] = mix
        r1_ref[...] = ALPHA * h0_ref[...] + _dot(mix, wo_ref[...])

    sd = jax.ShapeDtypeStruct
    row = pl.BlockSpec((tm, d), lambda i: (i, 0))
    return _pcall(
        body, name="mix_fwd", grid=(r // tm,),
        in_specs=[row, pl.BlockSpec((tm, d), lambda i: (i, O_OFF // d)), pl.BlockSpec((tm, d), lambda i: (i, GS_OFF // d)),
                  pl.BlockSpec((tm, d), lambda i: (i, GM_OFF // d)), row, row, _const((1, d)),
                  _resident((d, d)), _resident((d, d))],
        out_specs=[row] * 4,
        out_shape=[sd((r, d), BF16), sd((r, d), F32), sd((r, d), BF16), sd((r, d), F32)],
        compiler_params=_cp(("parallel",), 48),
    )(hm, p, p, p, ys5g, h0, gn, wmo_bf, wo_bf)


def _mix_bwd(dr1, wo_bf, wmo_bf, p, ys5g, ym, hm, gn, lp):
    r, d = hm.shape
    tm = _row_tile(lp, 208)
    dv = d // M_HEADS

    def body(dr1_ref, wo_ref, wmo_ref, o_ref, gs_ref, gm_ref, ys_ref, ym_ref, hm_ref, gn_ref,
             dp_ref, dys_ref, dym_ref, dhm_ref, dgn_ref):
        i = pl.program_id(0)

        @pl.when(i == 0)
        def _():
            dgn_ref[...] = jnp.zeros_like(dgn_ref)

        dmix = _dot(_bf(dr1_ref[...]), wo_ref[...], NT)
        sgs, sgm, so = (_sig(gs_ref[...].astype(F32)), _sig(gm_ref[...].astype(F32)), _sig(o_ref[...].astype(F32)))
        dys_ref[...] = dmix * sgs
        dp_ref[:, d:2 * d] = _bf(dmix * ys_ref[...] * sgs * (1.0 - sgs))
        dym = dmix * sgm
        dym_ref[...] = _bf(dym)
        dp_ref[:, 2 * d:3 * d] = _bf(dmix * ym_ref[...] * sgm * (1.0 - sgm))
        dymin = _dot(_bf(dym), wmo_ref[...], NT)
        xhat, rs = _headnorm(hm_ref[...])
        gn_ = gn_ref[...]
        dp_ref[:, 0:d] = _bf(dymin * (xhat * gn_) * so * (1.0 - so))
        dhn = dymin * so
        dgn_ref[...] += _colsum(dhn * xhat)
        dxh = dhn * gn_
        for h in range(M_HEADS):
            sl = slice(h * dv, (h + 1) * dv)
            a, xh = dxh[:, sl], xhat[:, sl]
            m1 = jnp.mean(a, axis=-1, keepdims=True)
            m2 = jnp.mean(a * xh, axis=-1, keepdims=True)
            dhm_ref[:, sl] = rs[h] * (a - m1 - xh * m2)

    sd = jax.ShapeDtypeStruct
    row = pl.BlockSpec((tm, d), lambda i: (i, 0))
    vec = _const((1, d))
    return _pcall(
        body, name="mix_bwd", grid=(r // tm,),
        in_specs=[row, _resident((d, d)), _resident((d, d)),
                  pl.BlockSpec((tm, d), lambda i: (i, O_OFF // d)), pl.BlockSpec((tm, d), lambda i: (i, GS_OFF // d)),
                  pl.BlockSpec((tm, d), lambda i: (i, GM_OFF // d)), row, row, row, vec],
        out_specs=[pl.BlockSpec((tm, 3 * d), lambda i: (i, 0)), row, row, row, vec],
        out_shape=[sd((r, NP), BF16), sd((r, d), F32), sd((r, d), BF16), sd((r, d), F32), sd((1, d), F32)],
        compiler_params=_cp(("arbitrary",), 48),
    )(dr1, wo_bf, wmo_bf, p, p, p, ys5g, ym, hm, gn)


def _mlp_fwd(r1, tgt, g1, b1, wup_g, wdn_bf, bup, g2, b2, lp):
    r, d = r1.shape
    tm = _row_tile(lp, 352)
    tps = lp // tm
    nf = wup_g.shape[0]

    def body(r1_ref, t_ref, g1_ref, b1_ref, wup_ref, wdn_ref, bup_ref, g2_ref, b2_ref,
             dr2_ref, h1b_ref, act_ref, loss_ref, dg2_ref, db2_ref):
        i = pl.program_id(0)

        @pl.when(i == 0)
        def _():
            loss_ref[...] = jnp.zeros_like(loss_ref)
            dg2_ref[...] = jnp.zeros_like(dg2_ref)
            db2_ref[...] = jnp.zeros_like(db2_ref)

        h1, _, _ = _ln_fwd(r1_ref[...], g1_ref[...], b1_ref[...])
        h1b = _bf(h1)
        h1b_ref[...] = h1b
        ff = jnp.zeros((tm, d), F32)
        for s in range(nf):
            up = _dot(h1b, wup_ref[s]) + bup_ref[:, s * d:(s + 1) * d]
            a = jnp.maximum(up, 0.0)
            a = _bf(a * a)
            act_ref[:, s * d:(s + 1) * d] = a
            ff = ff + _dot(a, wdn_ref[s * d:(s + 1) * d, :])
        r2 = ALPHA * h1 + ff
        g2 = g2_ref[...]
        y, xhat, rstd = _ln_fwd(r2, g2, b2_ref[...])
        t = (i % tps) * tm + lax.broadcasted_iota(jnp.int32, (tm, 1), 0)
        diff = jnp.where(t >= PAD + N_META, y - t_ref[...], 0.0)
        loss_ref[...] += 0.5 / d * jnp.sum(jnp.sum(diff * diff, axis=1, keepdims=True), axis=0, keepdims=True)
        dy = diff * (1.0 / d)
        dg2_ref[...] += _colsum(dy * xhat)
        db2_ref[...] += _colsum(dy)
        dr2_ref[...] = _ln_bwd(dy, xhat, rstd, g2)

    sd = jax.ShapeDtypeStruct
    row = pl.BlockSpec((tm, d), lambda i: (i, 0))
    vec = _const((1, d))
    return _pcall(
        body, name="mlp_fwd", grid=(r // tm,),
        in_specs=[row, row, vec, vec, _resident(wup_g.shape), _resident(wdn_bf.shape), _const((1, nf * d)), vec, vec],
        out_specs=[row, row, pl.BlockSpec((tm, nf * d), lambda i: (i, 0)), _const((1, 128)), vec, vec],
        out_shape=[sd((r, d), F32), sd((r, d), BF16), sd((r, nf * d), BF16), sd((1, 128), F32), sd((1, d), F32),
                   sd((1, d), F32)],
        compiler_params=_cp(("arbitrary",), 56),
    )(r1, tgt, g1, b1, wup_g, wdn_bf, bup, g2, b2)


def _mlp_bwd(h1b, dr2, r1, g1, wup_g, wdn_bf, bup, lp):
    r, d = h1b.shape
    tm = _row_tile(lp, 352)
    nf = wup_g.shape[0]

    def body(h1_ref, dr2_ref, r1_ref, g1_ref, wup_ref, wdn_ref, bup_ref, dr1_ref, dup_ref, dbup_ref, dg1_ref, db1_ref):
        i = pl.program_id(0)

        @pl.when(i == 0)
        def _():
            dbup_ref[...] = jnp.zeros_like(dbup_ref)
            dg1_ref[...] = jnp.zeros_like(dg1_ref)
            db1_ref[...] = jnp.zeros_like(db1_ref)

        h1b = h1_ref[...]
        dr2 = dr2_ref[...]
        dr2b = _bf(dr2)
        acc = ALPHA * dr2
        for s in range(nf):
            up = _dot(h1b, wup_ref[s]) + bup_ref[:, s * d:(s + 1) * d]
            dact = _dot(dr2b, wdn_ref[s * d:(s + 1) * d, :], NT)
            dup = dact * (2.0 * jnp.maximum(up, 0.0))
            dbup_ref[:, s * d:(s + 1) * d] += _colsum(dup)
            dupb = _bf(dup)
            dup_ref[:, s * d:(s + 1) * d] = dupb
            acc = acc + _dot(dupb, wup_ref[s], NT)
        g1 = g1_ref[...]
        _, xhat1, rstd1 = _ln_fwd(r1_ref[...], g1, 0.0)
        dr1_ref[...] = _ln_bwd(acc, xhat1, rstd1, g1)
        dg1_ref[...] += _colsum(acc * xhat1)
        db1_ref[...] += _colsum(acc)

    sd = jax.ShapeDtypeStruct
    row = pl.BlockSpec((tm, d), lambda i: (i, 0))
    vec = _const((1, d))
    return _pcall(
        body, name="mlp_bwd", grid=(r // tm,),
        in_specs=[row, row, row, vec, _resident(wup_g.shape), _resident(wdn_bf.shape), _const((1, nf * d))],
        out_specs=[row, pl.BlockSpec((tm, nf * d), lambda i: (i, 0)), _const((1, nf * d)), vec, vec],
        out_shape=[sd((r, d), F32), sd((r, nf * d), BF16), sd((1, nf * d), F32), sd((1, d), F32), sd((1, d), F32)],
        compiler_params=_cp(("arbitrary",), 56),
    )(h1b, dr2, r1, g1, wup_g, wdn_bf, bup)


def _s5_block_mats(bb_re_t, bb_im_t, c_re, c_im, ap_re, ap_im):
    ng = c_re.shape[0]
    gl = ng // S5_KCH
    eye = jnp.eye(gl, dtype=F32)

    def bmat(bt):
        bb = jnp.transpose(bt, (1, 0, 2)).reshape(S5_KCH, gl, S5_GROUP, S5_STATE)
        return jnp.einsum("kghp,gj->kghjp", bb, eye).reshape(S5_KCH, gl * S5_GROUP, gl * S5_STATE)

    def cmat(c):
        cc = c.reshape(S5_KCH, gl, S5_GROUP, S5_STATE)
        return jnp.einsum("kghp,gj->kjpgh", cc, eye).reshape(S5_KCH, gl * S5_STATE, gl * S5_GROUP)

    def pw(a):
        return jnp.transpose(a.reshape(8, S5_KCH, gl * S5_STATE), (1, 0, 2))

    bk = jnp.concatenate([bmat(bb_re_t), bmat(bb_im_t)], axis=-1)
    apow = jnp.concatenate([pw(ap_re), pw(ap_im)], axis=-1)
    return _bf(bk), _bf(cmat(c_re)), _bf(cmat(c_im)), apow


def _s5_block_grads(dbk, dcre, dcim, da):
    gl = dbk.shape[1] // S5_GROUP
    ng = gl * S5_KCH
    eye = jnp.eye(gl, dtype=F32)
    hw = gl * S5_STATE

    def bpart(x):
        x = x.reshape(S5_KCH, gl, S5_GROUP, gl, S5_STATE)
        x = jnp.einsum("kghjp,gj->kghp", x, eye).reshape(ng, S5_GROUP, S5_STATE)
        return jnp.transpose(x, (1, 0, 2))

    def cpart(x):
        x = x.reshape(S5_KCH, gl, S5_STATE, gl, S5_GROUP)
        return jnp.einsum("kjpgh,gj->kghp", x, eye).reshape(ng, S5_GROUP, S5_STATE)

    return (bpart(dbk[..., :hw]), bpart(dbk[..., hw:]), cpart(dcre), cpart(dcim),
            da[:, 0, :hw].reshape(ng, S5_STATE), da[:, 0, hw:].reshape(ng, S5_STATE))


def _tie(a, tok):
    return a if tok is None else a + tok[0, 0]


def _local_step(x, tgt, w, early=None, late=None, ready=None):
    ready = ready or (lambda names, g: None)
    bsz, seq, d = x.shape
    lp = PAD + N_META + seq
    r = bsz * lp
    tgtp = jnp.concatenate([jnp.zeros((bsz, PAD + N_META, d), F32), tgt], axis=1).reshape(r, d)

    h0, h0b = _ln0_fwd(x, w["meta_tokens"], w["ln0_g"], w["ln0_b"])
    if early is not None:
        w = {**w, **early((h0, tgtp))}
    p, pg = _inproj(h0b, w["w_in"], w["b_in"], lp)
    p3 = p.reshape(bsz, lp, NP)
    pg3 = pg.reshape(bsz, lp, 128)

    b_re_t = jnp.transpose(w["s5_b_re"], (2, 0, 1))
    b_im_t = jnp.transpose(w["s5_b_im"], (2, 0, 1))
    ap_re, ap_im, bb_re_t, bb_im_t = _s5_prep(w["s5_lambda_re"], w["s5_lambda_im"], w["s5_log_dt"], b_re_t, b_im_t)
    bk, cre, cim, apow = _s5_block_mats(bb_re_t, bb_im_t, w["s5_c_re"], w["s5_c_im"], ap_re, ap_im)
    y_s5, xs = _s5_fwd(p3, bk, cre, cim, apow, w["s5_d"])
    sw = y_s5.shape[-1]
    if late is not None:
        w = {**w, **late(y_s5)}
    gy, z, ys5g = _glu_fwd(y_s5.reshape(r, sw), w["s5_w_glu"], lp)

    pre3, qk3 = _conv_fwd(p3, w["qk_conv_w"], w["qk_conv_b"])
    hm3, cs, ns, ms = _mlstm_fwd(qk3, p3, pg3)
    hm = hm3.reshape(r, d)
    ymin, ym, mix, r1 = _mix_fwd(hm, p, ys5g, h0, w["m_norm_g"], w["m_w_out"], w["w_o"], lp)
    dr2, h1b, act, loss, dg2, db2 = _mlp_fwd(r1, tgtp, w["ln1_g"], w["ln1_b"], w["w_up"], w["w_down"], w["b_up"],
                                             w["ln2_g"], w["ln2_b"], lp)

    g = {"ln2_g": dg2, "ln2_b": db2}
    dr1, dup, g["b_up"], g["ln1_g"], g["ln1_b"] = _mlp_bwd(h1b, dr2, r1, w["ln1_g"], w["w_up"], w["w_down"], w["b_up"], lp)
    g["w_down"] = _mm_tn(act, dr2, name="dw_down")
    g["w_up"] = _mm_tn(h1b, dup, name="dw_up", split=w["w_up"].shape[0])
    tok = ready(("w_down", "w_up"), g)
    dp, dys5g, dym, dhm, g["m_norm_g"] = _mix_bwd(
        dr1, w["w_o"], w["m_w_out"], p, ys5g, ym, hm, _tie(w["m_norm_g"], tok), lp)
    g["w_o"] = _mm_tn(mix, dr1, name="dw_o")
    g["m_w_out"] = _mm_tn(ymin, dym, name="dw_mout")

    dp3 = dp.reshape(bsz, lp, NP)
    dp3, dqk3, dgate = _mlstm_bwd(dp3, qk3, p3, pg3, dhm.reshape(bsz, lp, d), cs, ns, ms)
    dp3, g["qk_conv_w"], g["qk_conv_b"] = _conv_bwd(dp3, p3, dqk3, pre3, w["qk_conv_w"])
    dz, dys5 = _glu_bwd(dys5g, z, y_s5.reshape(r, sw), w["s5_w_glu"], lp)
    g["s5_w_glu"] = _mm_tn(gy, dz, name="dw_glu", split=w["s5_w_glu"].shape[0])
    tok = ready(("s5_w_glu", "m_w_out", "w_o"), g)
    apow_rev = jnp.flip(apow, axis=1)
    dp3, dbk, dcre, dcim, da, g["s5_d"] = _s5_bwd(dp3, p3, dys5.reshape(bsz, lp, sw), xs, bk, cre, cim, apow_rev,
                                                 _tie(w["s5_d"], tok))
    dbb_re_t, dbb_im_t, g["s5_c_re"], g["s5_c_im"], da_re, da_im = _s5_block_grads(dbk, dcre, dcim, da)
    g["s5_lambda_re"], g["s5_lambda_im"], g["s5_log_dt"], gb_re_t, gb_im_t = _s5_prep_bwd(
        w["s5_lambda_re"], w["s5_lambda_im"], w["s5_log_dt"], b_re_t, b_im_t, da_re, da_im, dbb_re_t, dbb_im_t)
    g["s5_b_re"] = jnp.transpose(gb_re_t, (1, 2, 0))
    g["s5_b_im"] = jnp.transpose(gb_im_t, (1, 2, 0))

    dp3 = lax.dynamic_update_slice(dp3, dgate, (0, 0, G_OFF))
    dp = dp3.reshape(r, NP)
    g["w_in"], g["b_in"] = _mm_tn(h0b, dp, name="dw_in", colsum=True)
    tok = ready(("w_in",), g)
    dpw = _mm_nt(dp, w["w_in"], lp, name="dh0", dep=tok)
    grad_x, g["ln0_g"], g["ln0_b"], g["meta_tokens"] = _ln0_bwd(x, w["meta_tokens"], dr1, dpw, w["ln0_g"])
    return loss, grad_x, g


_ANY = pl.BlockSpec(memory_space=pl.ANY)
_MESH = pl.DeviceIdType.MESH


def _place():
    return lax.axis_index("x"), lax.axis_index("y"), lax.axis_index("c")


def _gather_chips(shards):
    n = len(shards)

    def body(*refs):
        ins, outs = refs[:n], refs[n:2 * n]
        send, recv, loc = refs[2 * n:]
        x, y, c = _place()
        me = 2 * x + y
        peers = [(1 - x, y), (x, 1 - y), (1 - x, 1 - y)]

        def rc(a, k, slot):
            px, py = peers[k]
            return pltpu.make_async_remote_copy(src_ref=ins[a], dst_ref=outs[a].at[slot], send_sem=send.at[a, k],
                                                recv_sem=recv.at[a, k], device_id=(px, py, c), device_id_type=_MESH)

        own = [pltpu.make_async_copy(ins[a], outs[a].at[me], loc.at[a]) for a in range(n)]
        for cp in own:
            cp.start()
        out = [rc(a, k, me) for a in range(n) for k in range(3)]
        for cp in out:
            cp.start()
        for a in range(n):
            for k in range(3):
                rc(a, k, 2 * peers[k][0] + peers[k][1]).wait_recv()
        for cp in out:
            cp.wait_send()
        for cp in own:
            cp.wait()

    return _pcall(
        body, name="gather_chips", in_specs=[_ANY] * n, out_specs=[_ANY] * n,
        out_shape=[jax.ShapeDtypeStruct((4,) + s.shape, s.dtype) for s in shards],
        scratch_shapes=[pltpu.SemaphoreType.DMA((n, 3)), pltpu.SemaphoreType.DMA((n, 3)), pltpu.SemaphoreType.DMA((n,))],
    )(*shards)


_HBM = pl.BlockSpec(memory_space=pltpu.HBM)
_SEM = pl.BlockSpec(memory_space=pltpu.SEMAPHORE)
_EFFECT = pltpu.SideEffectType.DATAFLOW_SIDE_EFFECTING


def _xchg_copies(srcs, lands, send, recv, scatter):
    x, y, c = _place()
    me = 2 * x + y
    peers = [(1 - x, y), (x, 1 - y), (1 - x, 1 - y)]
    out = []
    for a in range(len(srcs)):
        for k, (px, py) in enumerate(peers):
            src = srcs[a].at[2 * px + py] if scatter else srcs[a]
            dst = lands[a].at[k] if scatter else lands[a].at[me]
            out.append(pltpu.make_async_remote_copy(src_ref=src, dst_ref=dst, send_sem=send.at[3 * a + k],
                                                    recv_sem=recv.at[3 * a + k], device_id=(px, py, c),
                                                    device_id_type=_MESH))
    return out


def _xchg_start(srcs, lands, *, name, scatter, dep=None):
    n = len(srcs)
    deps = [] if dep is None else [dep]
    nd = len(deps)

    def body(*refs):
        send, recv = refs[2 * n + nd], refs[2 * n + nd + 1]
        for cp in _xchg_copies(refs[:n], refs[n:2 * n], send, recv, scatter):
            cp.start()
        refs[-1][...] = jnp.zeros_like(refs[-1])

    hbm = lambda a: pltpu.HBM(a.shape, a.dtype)
    con = lambda a: pltpu.with_memory_space_constraint(a, pltpu.HBM)
    res = _pcall(
        body, name=name, in_specs=[_HBM] * (2 * n) + [_ANY] * nd,
        out_specs=[_SEM, _SEM] + [_HBM] * (2 * n) + [pl.BlockSpec(memory_space=pltpu.VMEM)],
        out_shape=[pltpu.SemaphoreType.DMA((3 * n,)), pltpu.SemaphoreType.DMA((3 * n,))]
        + [hbm(a) for a in srcs] + [hbm(a) for a in lands] + [jax.ShapeDtypeStruct((8, 128), F32)],
        input_output_aliases={i: 2 + i for i in range(2 * n)},
        compiler_params=pltpu.CompilerParams(has_side_effects=_EFFECT),
    )(*[con(a) for a in srcs], *[con(a) for a in lands], *deps)
    return res[0], res[1], list(res[2:2 + n]), list(res[2 + n:2 + 2 * n]), res[-1]


def _xchg_wait(send, recv, srcs, lands, after, *, name, scatter):
    n = len(srcs)
    afters = list(after) if isinstance(after, (list, tuple)) else [after]

    def body(*refs):
        s_ref, r_ref = refs[2 * n], refs[2 * n + 1]
        for cp in _xchg_copies(refs[:n], refs[n:2 * n], s_ref, r_ref, scatter):
            cp.wait_send()
            cp.wait_recv()

    hbm = lambda a: pltpu.HBM(a.shape, a.dtype)
    res = _pcall(
        body, name=name, in_specs=[_HBM] * (2 * n) + [_SEM, _SEM] + [_ANY] * len(afters),
        out_specs=[_HBM] * (2 * n),
        out_shape=[hbm(a) for a in srcs] + [hbm(a) for a in lands],
        input_output_aliases={i: i for i in range(2 * n)},
        compiler_params=pltpu.CompilerParams(has_side_effects=_EFFECT),
    )(*srcs, *lands, send, recv, *afters)
    return list(res[:n]), list(res[n:])


def _swap_cores(arrs, name="swap_cores"):
    n = len(arrs)

    def body(*refs):
        ins, outs = refs[:n], refs[n:2 * n]
        send, recv = refs[2 * n:]
        x, y, c = _place()
        cps = [pltpu.make_async_remote_copy(src_ref=ins[a], dst_ref=outs[a], send_sem=send.at[a], recv_sem=recv.at[a],
                                            device_id=(x, y, 1 - c), device_id_type=_MESH) for a in range(n)]
        for cp in cps:
            cp.start()
        for cp in cps:
            cp.wait_recv()
        for cp in cps:
            cp.wait_send()

    return _pcall(
        body, name=name, in_specs=[_ANY] * n, out_specs=[_ANY] * n,
        out_shape=[jax.ShapeDtypeStruct(s.shape, s.dtype) for s in arrs],
        scratch_shapes=[pltpu.SemaphoreType.DMA((n,)), pltpu.SemaphoreType.DMA((n,))],
    )(*arrs)


def _allreduce_small(v, dep=None):
    rows = v.shape[0]
    half = rows // 2
    assert half % 8 == 0 and 2 * half == rows
    deps = [] if dep is None else [dep]

    def body(v_ref, *rest):
        out_ref, sib_ref, pair_ref, slots_ref, send, recv = rest[len(deps):]
        x, y, c = _place()
        chip = 2 * x + y
        sibling = (x, y, 1 - c)
        peers = [(1 - x, y), (x, 1 - y), (1 - x, 1 - y)]
        mine = pl.ds(pl.multiple_of(c * half, 8), half)

        first = pltpu.make_async_remote_copy(src_ref=v_ref, dst_ref=sib_ref, send_sem=send.at[0], recv_sem=recv.at[0],
                                             device_id=sibling, device_id_type=_MESH)
        first.start()
        first.wait_recv()
        pair_ref[...] = v_ref[...] + sib_ref[...]
        slots_ref[chip] = pair_ref[mine, :]
        cross = [pltpu.make_async_remote_copy(src_ref=pair_ref.at[mine], dst_ref=slots_ref.at[chip],
                                              send_sem=send.at[1 + k], recv_sem=recv.at[1 + k],
                                              device_id=(px, py, c), device_id_type=_MESH)
                 for k, (px, py) in enumerate(peers)]
        for cp in cross:
            cp.start()
        for cp in cross:
            cp.wait_recv()
        out_ref[mine, :] = ((slots_ref[0] + slots_ref[1]) + slots_ref[2]) + slots_ref[3]
        last = pltpu.make_async_remote_copy(src_ref=out_ref.at[mine], dst_ref=out_ref.at[mine], send_sem=send.at[4],
                                            recv_sem=recv.at[4], device_id=sibling, device_id_type=_MESH)
        last.start()
        last.wait_recv()
        first.wait_send()
        for cp in cross:
            cp.wait_send()
        last.wait_send()

    vm = pl.BlockSpec(memory_space=pltpu.VMEM)
    return _pcall(
        body, name="allreduce_small", in_specs=[vm] + [_ANY] * len(deps), out_specs=vm,
        out_shape=jax.ShapeDtypeStruct((rows, 128), F32),
        scratch_shapes=[pltpu.VMEM((rows, 128), F32), pltpu.VMEM((rows, 128), F32), pltpu.VMEM((4, half, 128), F32),
                        pltpu.SemaphoreType.DMA((5,)), pltpu.SemaphoreType.DMA((5,))],
        compiler_params=_cp(None, 40),
    )(v, *deps)


def _sum_slots(own, land):
    ns, rows, cols = land.shape
    tm = _row_tile(rows, 256, 8)

    def body(own_ref, a_ref, o_ref):
        o_ref[...] = ((own_ref[...] + a_ref[0]) + a_ref[1]) + a_ref[2]

    return _pcall(
        body, name="sum_slots", grid=(rows // tm,),
        in_specs=[pl.BlockSpec((tm, cols), lambda i: (i, 0)), pl.BlockSpec((ns, tm, cols), lambda i: (0, i, 0))],
        out_specs=pl.BlockSpec((tm, cols), lambda i: (i, 0)),
        out_shape=jax.ShapeDtypeStruct((rows, cols), F32),
        compiler_params=_cp(("parallel",), 40),
    )(own, land)


def _adamw(w, m, v, g0, g1=None):
    rows, cols = w.shape[-2:]
    lead = w.ndim == 3
    tm = _row_tile(rows, max(8, (1 << 20) // (4 * cols)), 8)
    c1 = 1.0 - ADAM_B1 ** ADAM_STEP
    c2 = 1.0 - ADAM_B2 ** ADAM_STEP
    two = g1 is not None

    def body(*refs):
        w_ref, m_ref, v_ref, g0_ref = refs[:4]
        g_ref, d_ref, nm_ref, nv_ref = refs[-4:]
        g = g0_ref[...]
        if two:
            g = g + refs[4][...]
        nm = ADAM_B1 * m_ref[...] + (1.0 - ADAM_B1) * g
        nv = ADAM_B2 * v_ref[...] + (1.0 - ADAM_B2) * (g * g)
        g_ref[...] = g
        nm_ref[...] = nm
        nv_ref[...] = nv
        d_ref[...] = -ADAM_LR * ((nm / c1) / (jnp.sqrt(nv / c2) + ADAM_EPS) + ADAM_WD * w_ref[...])

    blk = pl.BlockSpec((tm, cols), lambda i: (i, 0))
    wblk = pl.BlockSpec((None, tm, cols), lambda i: (0, i, 0)) if lead else blk
    ins = [w, m, v, g0] + ([g1] if two else [])
    return _pcall(
        body, name="adamw", grid=(rows // tm,), in_specs=[wblk] * 3 + [blk] * (len(ins) - 3), out_specs=[wblk] * 4,
        out_shape=[jax.ShapeDtypeStruct(w.shape, F32)] * 4,
        compiler_params=_cp(("parallel",), 40),
    )(*ins)


_BIG = ("w_in", "s5_w_glu", "m_w_out", "w_o", "w_up", "w_down")
_SMALL = ("ln0_g", "ln0_b", "b_in", "qk_conv_b", "s5_lambda_re", "s5_lambda_im", "s5_log_dt", "s5_b_re", "s5_b_im",
          "s5_c_re", "s5_c_im", "s5_d", "m_norm_g", "ln1_g", "ln1_b", "b_up", "ln2_g", "ln2_b")
_SMALL_SHARDED = ("meta_tokens", "qk_conv_w")
_ORDER = ("meta_tokens", "ln0_g", "ln0_b", "w_in", "b_in", "qk_conv_w", "qk_conv_b", "s5_lambda_re", "s5_lambda_im",
          "s5_log_dt", "s5_b_re", "s5_b_im", "s5_c_re", "s5_c_im", "s5_d", "s5_w_glu", "m_norm_g", "m_w_out", "w_o",
          "ln1_g", "ln1_b", "w_up", "b_up", "w_down", "ln2_g", "ln2_b")


def _pack(arrs):
    flat = jnp.concatenate([a.reshape(-1) for a in arrs])
    n = flat.shape[0]
    rows = -(-n // 2048) * 16
    return jnp.pad(flat, (0, rows * 128 - n)).reshape(rows, 128)


def _unpack(packed, shapes):
    flat = packed.reshape(-1)
    out, off = [], 0
    for s in shapes:
        n = math.prod(s)
        out.append(flat[off:off + n].reshape(s))
        off += n
    return out


def kernel(x, meta_tokens, ln0_g, ln0_b, w_in, b_in, qk_conv_w, qk_conv_b, s5_lambda_re, s5_lambda_im, s5_log_dt, s5_b_re, s5_b_im, s5_c_re, s5_c_im, s5_d, s5_w_glu, m_norm_g, m_w_out, w_o, ln1_g, ln1_b, w_up, b_up, w_down, ln2_g, ln2_b, loss_target, m_meta_tokens, m_ln0_g, m_ln0_b, m_w_in, m_b_in, m_qk_conv_w, m_qk_conv_b, m_s5_lambda_re, m_s5_lambda_im, m_s5_log_dt, m_s5_b_re, m_s5_b_im, m_s5_c_re, m_s5_c_im, m_s5_d, m_s5_w_glu, m_m_norm_g, m_m_w_out, m_w_o, m_ln1_g, m_ln1_b, m_w_up, m_b_up, m_w_down, m_ln2_g, m_ln2_b, v_meta_tokens, v_ln0_g, v_ln0_b, v_w_in, v_b_in, v_qk_conv_w, v_qk_conv_b, v_s5_lambda_re, v_s5_lambda_im, v_s5_log_dt, v_s5_b_re, v_s5_b_im, v_s5_c_re, v_s5_c_im, v_s5_d, v_s5_w_glu, v_m_norm_g, v_m_w_out, v_w_o, v_ln1_g, v_ln1_b, v_w_up, v_b_up, v_w_down, v_ln2_g, v_ln2_b):
    wts = dict(meta_tokens=meta_tokens, ln0_g=ln0_g, ln0_b=ln0_b, w_in=w_in, b_in=b_in, qk_conv_w=qk_conv_w,
               qk_conv_b=qk_conv_b, s5_lambda_re=s5_lambda_re, s5_lambda_im=s5_lambda_im, s5_log_dt=s5_log_dt,
               s5_b_re=s5_b_re, s5_b_im=s5_b_im, s5_c_re=s5_c_re, s5_c_im=s5_c_im, s5_d=s5_d, s5_w_glu=s5_w_glu,
               m_norm_g=m_norm_g, m_w_out=m_w_out, w_o=w_o, ln1_g=ln1_g, ln1_b=ln1_b, w_up=w_up, b_up=b_up,
               w_down=w_down, ln2_g=ln2_g, ln2_b=ln2_b)
    mom = dict(meta_tokens=m_meta_tokens, ln0_g=m_ln0_g, ln0_b=m_ln0_b, w_in=m_w_in, b_in=m_b_in, qk_conv_w=m_qk_conv_w,
               qk_conv_b=m_qk_conv_b, s5_lambda_re=m_s5_lambda_re, s5_lambda_im=m_s5_lambda_im, s5_log_dt=m_s5_log_dt,
               s5_b_re=m_s5_b_re, s5_b_im=m_s5_b_im, s5_c_re=m_s5_c_re, s5_c_im=m_s5_c_im, s5_d=m_s5_d,
               s5_w_glu=m_s5_w_glu, m_norm_g=m_m_norm_g, m_w_out=m_m_w_out, w_o=m_w_o, ln1_g=m_ln1_g, ln1_b=m_ln1_b,
               w_up=m_w_up, b_up=m_b_up, w_down=m_w_down, ln2_g=m_ln2_g, ln2_b=m_ln2_b)
    var = dict(meta_tokens=v_meta_tokens, ln0_g=v_ln0_g, ln0_b=v_ln0_b, w_in=v_w_in, b_in=v_b_in, qk_conv_w=v_qk_conv_w,
               qk_conv_b=v_qk_conv_b, s5_lambda_re=v_s5_lambda_re, s5_lambda_im=v_s5_lambda_im, s5_log_dt=v_s5_log_dt,
               s5_b_re=v_s5_b_re, s5_b_im=v_s5_b_im, s5_c_re=v_s5_c_re, s5_c_im=v_s5_c_im, s5_d=v_s5_d,
               s5_w_glu=v_s5_w_glu, m_norm_g=v_m_norm_g, m_w_out=v_m_w_out, w_o=v_w_o, ln1_g=v_ln1_g, ln1_b=v_ln1_b,
               w_up=v_w_up, b_up=v_b_up, w_down=v_w_down, ln2_g=v_ln2_g, ln2_b=v_ln2_b)
    d = x.shape[-1]
    chip = 2 * lax.axis_index("x") + lax.axis_index("y")

    gw = dict(zip(_SMALL_SHARDED, _gather_chips([meta_tokens, qk_conv_w[0]])))
    own_w_in = _bf(w_in[0])
    fsend, frecv, fsrc, fland, ftok = _xchg_start([own_w_in], [lax.empty((4,) + own_w_in.shape, BF16)],
                                                  name="gather_w_in_start", scatter=False, dep=gw["qk_conv_w"])
    late_names = tuple(n for n in _BIG if n != "w_in")
    cat = lambda a: jnp.transpose(a, (1, 0, 2)).reshape(a.shape[1], 4 * a.shape[2])
    w = dict(
        meta_tokens=cat(gw["meta_tokens"]), ln0_g=ln0_g[None], ln0_b=_tie(ln0_b[None], ftok),
        qk_conv_w=cat(gw["qk_conv_w"]), qk_conv_b=qk_conv_b,
        s5_lambda_re=s5_lambda_re[0], s5_lambda_im=s5_lambda_im[0], s5_log_dt=s5_log_dt[0][:, None],
        s5_b_re=s5_b_re[0], s5_b_im=s5_b_im[0], s5_c_re=s5_c_re[0], s5_c_im=s5_c_im[0], s5_d=s5_d,
        m_norm_g=m_norm_g, ln1_g=ln1_g, ln1_b=ln1_b, b_up=b_up, ln2_g=ln2_g, ln2_b=ln2_b)
    in_flight = {}

    def place_own(src, land):
        return lax.dynamic_update_slice(land, src[None], (chip,) + (0,) * src.ndim)

    small_names = _SMALL + _SMALL_SHARDED
    small_wmv = [_pack([dct[n] for n in small_names]) for dct in (wts, mom, var)]

    def early(after):
        src, land = _xchg_wait(fsend, frecv, fsrc, fland, tuple(after) + tuple(small_wmv), name="gather_w_in_wait",
                               scatter=False)
        late_src = [_bf(wts[n][0]) for n in late_names]
        st = _xchg_start(late_src, [lax.empty((4,) + a.shape, a.dtype) for a in late_src], name="gather_late_start",
                         scatter=False, dep=src[0])
        in_flight["late"] = st[:4]
        return dict(w_in=_w_in_from_slots(place_own(src[0], land[0]), IN_CHUNK), b_in=_tie(_to_pad_cols(b_in), st[4]))

    def late(after):
        src, land = _xchg_wait(*in_flight["late"], after, name="gather_late_wait", scatter=False)
        full = {n: place_own(s, ld) for n, s, ld in zip(late_names, src, land)}
        return dict(s5_w_glu=full["s5_w_glu"], m_w_out=full["m_w_out"].reshape(d, d), w_o=full["w_o"].reshape(d, d),
                    w_up=full["w_up"], w_down=full["w_down"].reshape(4 * d, d))

    flying = []

    def ready(names, g):
        parts = dict(
            w_in=lambda: _slots_from_w_in(g["w_in"][0]), s5_w_glu=lambda: g["s5_w_glu"],
            m_w_out=lambda: g["m_w_out"].reshape(4, d // 4, d), w_o=lambda: g["w_o"].reshape(4, d // 4, d),
            w_up=lambda: g["w_up"], w_down=lambda: g["w_down"].reshape(4, d, d))
        src = [parts[n]() for n in names]
        land = [lax.empty((3,) + a.shape[1:], a.dtype) for a in src]
        st = _xchg_start(src, land, name="scatter_" + names[0] + "_start", scatter=True)
        flying.append((names,) + st[:4])
        return st[4]

    loss, grad_x, g = _local_step(x, loss_target, w, early, late, ready)
    g["b_in"] = _from_pad_cols(g["b_in"])

    res = {}

    def flat(a):
        return jnp.swapaxes(a, -1, -2).reshape(a.shape[:-2] + (-1, 128))

    def unflat(y, shape):
        return jnp.swapaxes(y.reshape(shape[:-2] + (shape[-1], shape[-2])), -1, -2)

    def finish(groups, after, tag):
        mine = {}
        for names, send, recv, src, land in groups:
            src, land = _xchg_wait(send, recv, src, land, after, name="scatter_" + names[0] + "_wait", scatter=True)
            for n, s, ld in zip(names, src, land):
                mine[n] = _sum_slots(lax.dynamic_index_in_dim(s, chip, 0, keepdims=False), ld)
        theirs = _swap_cores(list(mine.values()), name="swap_cores_" + tag)
        for n, t in zip(mine, theirs):
            if n == "w_in":
                res[n] = [unflat(r, wts[n].shape) for r in _adamw(flat(wts[n]), flat(mom[n]), flat(var[n]),
                                                                  flat(mine[n]), flat(t))]
            else:
                res[n] = _adamw(wts[n], mom[n], var[n], mine[n], t)

    finish(flying[:-1], g["ln0_g"], "a")

    small_shapes = [(1, 128)] + [wts[n].shape for n in _SMALL] + [g[n].shape for n in _SMALL_SHARDED]
    packed = _pack([loss] + [g[n] for n in _SMALL] + [g[n] for n in _SMALL_SHARDED])
    tot = _unpack(_allreduce_small(packed, dep=res["w_o"][3]), small_shapes)
    loss_out = tot[0][0, 0]
    gsm = dict(zip(_SMALL + _SMALL_SHARDED, tot[1:]))
    for n in _SMALL_SHARDED:
        cols = wts[n].shape[-1]
        gsm[n] = lax.dynamic_slice_in_dim(gsm[n], chip * cols, cols, axis=1).reshape(wts[n].shape)

    names = small_names
    shapes = [wts[n].shape for n in names]
    small_out = _adamw(*small_wmv, _pack([gsm[n] for n in names]))
    small_res = [_unpack(r, shapes) for r in small_out]
    for j, n in enumerate(names):
        res[n] = [small_res[q][j] for q in range(4)]
    finish(flying[-1:], small_out[0], "b")

    return (loss_out, grad_x, *[res[n][0] for n in _ORDER], *[res[n][1] for n in _ORDER],
            *[res[n][2] for n in _ORDER], *[res[n][3] for n in _ORDER])
```

```python
import functools
import math

import jax
import jax.numpy as jnp
from jax import lax
from jax.experimental import pallas as pl
from jax.experimental.pallas import tpu as pltpu

F32 = jnp.float32
BF16 = jnp.bfloat16
HI = lax.Precision.HIGHEST

N_META = 16
M_HEADS = 4
M_CHUNK = 128
PAD = M_CHUNK - N_META
CONV_W = 4
HALO_ROWS = 16
S5_GROUP = 16
S5_STATE = 64
S5_KCH = 4
LN_EPS = 1e-5
ALPHA = 2.0 ** 0.25
NEG = -1e30
ADAM_LR, ADAM_B1, ADAM_B2, ADAM_EPS, ADAM_WD, ADAM_STEP = 0.001, 0.9, 0.999, 1e-08, 0.01, 10

O_OFF, GS_OFF, GM_OFF, V_OFF, Q_OFF, K_OFF, U_OFF, G_OFF, NP = 0, 1024, 2048, 3072, 4096, 4608, 5120, 5632, 5760

NN = ((1,), (0,))
NT = ((1,), (1,))
TN = ((0,), (0,))


def _dot(a, b, dims=NN, prec=None):
    return lax.dot_general(a, b, (dims, ((), ())), preferred_element_type=F32, precision=prec)


def _bf(x):
    return x.astype(BF16)


def _sig(x):
    return 0.5 * jnp.tanh(0.5 * x) + 0.5


def _pcall(body, **kw):
    return pl.pallas_call(body, **kw)


def _cp(sem=None, vmem_mb=None):
    kw = {}
    if sem is not None:
        kw["dimension_semantics"] = sem
    if vmem_mb is not None:
        kw["vmem_limit_bytes"] = vmem_mb << 20
    return pltpu.CompilerParams(**kw)


def _row_tile(n, want, mult=16):
    best = None
    for t in range(mult, want + 1, mult):
        if n % t == 0:
            best = t
    assert best is not None, (n, want)
    return best


def _resident(shape):
    nd = len(shape)
    return pl.BlockSpec(shape, lambda *_: (0,) * nd, pipeline_mode=pl.Buffered(1))


def _const(shape):
    nd = len(shape)
    return pl.BlockSpec(shape, lambda *_: (0,) * nd)


def _ln_fwd(x, g, b):
    mu = jnp.mean(x, axis=-1, keepdims=True)
    xc = x - mu
    var = jnp.mean(xc * xc, axis=-1, keepdims=True)
    rstd = lax.rsqrt(var + LN_EPS)
    xhat = xc * rstd
    return xhat * g + b, xhat, rstd


def _ln_bwd(dy, xhat, rstd, g):
    dxh = dy * g
    m1 = jnp.mean(dxh, axis=-1, keepdims=True)
    m2 = jnp.mean(dxh * xhat, axis=-1, keepdims=True)
    return rstd * (dxh - m1 - xhat * m2)


def _colsum(x):
    return jnp.sum(x, axis=0, keepdims=True)


def _to_pad_cols(w):
    u, q, k, v, o, gi, gf, gs, gm = (w[..., 0:512], w[..., 512:1024], w[..., 1024:1536], w[..., 1536:2560],
                                     w[..., 2560:3584], w[..., 3584:3588], w[..., 3588:3592], w[..., 3592:4616],
                                     w[..., 4616:5640])
    z = jnp.zeros(w.shape[:-1] + (NP - G_OFF - 8,), w.dtype)
    return jnp.concatenate([o, gs, gm, v, q, k, u, gi, gf, z], axis=-1)


def _from_pad_cols(w):
    o, gs, gm, v, q, k, u = (w[..., O_OFF:GS_OFF], w[..., GS_OFF:GM_OFF], w[..., GM_OFF:V_OFF], w[..., V_OFF:Q_OFF],
                             w[..., Q_OFF:K_OFF], w[..., K_OFF:U_OFF], w[..., U_OFF:G_OFF])
    gi, gf = w[..., G_OFF:G_OFF + 4], w[..., G_OFF + 4:G_OFF + 8]
    return jnp.concatenate([u, q, k, v, o, gi, gf, gs, gm], axis=-1)


_IN_REF = (("u", 512), ("q", 512), ("k", 512), ("v", 1024), ("o", 1024), ("i", 4), ("f", 4), ("gs", 1024), ("gm", 1024))
_IN_PAD = (("o", O_OFF), ("gs", GS_OFF), ("gm", GM_OFF), ("v", V_OFF), ("q", Q_OFF), ("k", K_OFF), ("u", U_OFF),
           ("i", G_OFF), ("f", G_OFF + 4))


def _in_ref_ranges():
    out, off = {}, 0
    for n, s in _IN_REF:
        out[n] = (off, off + s)
        off += s
    return out, off


def _w_in_from_slots(g, chunk=None):
    rng, total = _in_ref_ranges()
    width = total // g.shape[0]
    cols = []
    for n, _ in _IN_PAD:
        a, b = rng[n]
        while a < b:
            s = a // width
            e = min(b, (s + 1) * width)
            cols.append(g[s][:, a - s * width:e - s * width])
            a = e
    cols.append(jnp.zeros((g.shape[1], NP - G_OFF - 8), g.dtype))
    if chunk is None:
        return jnp.concatenate(cols, axis=1)
    chunks, cur, room = [], [], chunk
    for c in cols:
        while c.shape[1] > 0:
            take = min(room, c.shape[1])
            cur.append(c[:, :take])
            c, room = c[:, take:], room - take
            if room == 0:
                chunks.append(jnp.concatenate(cur, axis=1))
                cur, room = [], chunk
    assert not cur
    return jnp.stack(chunks, axis=0)


def _slots_from_w_in(wp, nslot=4):
    rng, total = _in_ref_ranges()
    width = total // nslot
    pad_off = dict(_IN_PAD)
    slots = []
    for s in range(nslot):
        lo, hi = s * width, (s + 1) * width
        cols = []
        for n, _ in _IN_REF:
            a, b = rng[n]
            x0, x1 = max(a, lo), min(b, hi)
            if x0 < x1:
                cols.append(wp[:, pad_off[n] + x0 - a:pad_off[n] + x1 - a])
        slots.append(jnp.concatenate(cols, axis=1))
    return jnp.stack(slots, axis=0)


HEAD = PAD + N_META


def _ln0_in(j, x_ref, meta_ref):
    first = jnp.concatenate([jnp.zeros((PAD, meta_ref.shape[1]), F32), meta_ref[...]], axis=0)
    return jnp.where(j == 0, first, x_ref[...])


def _ln0_fwd(x, meta, g, b):
    bsz, seq, d = x.shape
    nb = seq // HEAD + 1

    def body(x_ref, m_ref, g_ref, b_ref, o_ref, ob_ref):
        y, _, _ = _ln_fwd(_ln0_in(pl.program_id(1), x_ref, m_ref), g_ref[...], b_ref[...])
        o_ref[...] = y
        ob_ref[...] = _bf(y)

    row = pl.BlockSpec((HEAD, d), lambda bb, j: (bb * nb + j, 0))
    return _pcall(
        body, name="ln0_fwd", grid=(bsz, nb),
        in_specs=[pl.BlockSpec((None, HEAD, d), lambda bb, j: (bb, jnp.maximum(j - 1, 0), 0)), _const((N_META, d)),
                  _const((1, d)), _const((1, d))],
        out_specs=[row, row],
        out_shape=[jax.ShapeDtypeStruct((bsz * nb * HEAD, d), F32), jax.ShapeDtypeStruct((bsz * nb * HEAD, d), BF16)],
        compiler_params=_cp(("parallel", "arbitrary")),
    )(x, meta, g, b)


def _ln0_bwd(x, meta, dr1, dpw, g):
    bsz, seq, d = x.shape
    nb = seq // HEAD + 1

    def body(x_ref, m_ref, a_ref, c_ref, g_ref, o_ref, dg_ref, db_ref, dm_ref):
        bb = pl.program_id(0)
        j = pl.program_id(1)

        @pl.when((bb == 0) & (j == 0))
        def _():
            dg_ref[...] = jnp.zeros_like(dg_ref)
            db_ref[...] = jnp.zeros_like(db_ref)
            dm_ref[...] = jnp.zeros_like(dm_ref)

        dy = ALPHA * a_ref[...] + c_ref[...]
        _, xhat, rstd = _ln_fwd(_ln0_in(j, x_ref, m_ref), g_ref[...], 0.0)
        dx = _ln_bwd(dy, xhat, rstd, g_ref[...])
        o_ref[...] = dx
        dg_ref[...] += _colsum(dy * xhat)
        db_ref[...] += _colsum(dy)

        @pl.when(j == 0)
        def _():
            dm_ref[...] += dx[PAD:, :]

    row = pl.BlockSpec((HEAD, d), lambda bb, j: (bb * nb + j, 0))
    tok = pl.BlockSpec((None, HEAD, d), lambda bb, j: (bb, jnp.maximum(j - 1, 0), 0))
    return _pcall(
        body, name="ln0_bwd", grid=(bsz, nb),
        in_specs=[tok, _const((N_META, d)), row, row, _const((1, d))],
        out_specs=[tok, _const((1, d)), _const((1, d)), _const((N_META, d))],
        out_shape=[jax.ShapeDtypeStruct((bsz, seq, d), F32), jax.ShapeDtypeStruct((1, d), F32),
                   jax.ShapeDtypeStruct((1, d), F32), jax.ShapeDtypeStruct((N_META, d), F32)],
        compiler_params=_cp(("arbitrary", "arbitrary")),
    )(x, meta, dr1, dpw, g)


IN_CHUNK = 1152


def _chunk_cols(w):
    k, n = w.shape
    return jnp.transpose(w.reshape(k, n // IN_CHUNK, IN_CHUNK), (1, 0, 2))


def _inproj(h0b, w3, bias, lp):
    r, d = h0b.shape
    nj, _, tn = w3.shape
    tm = _row_tile(lp, 1056)
    tps = lp // tm

    def body(a_ref, w_ref, b_ref, o_ref, gate_ref):
        i = pl.program_id(0)
        j = pl.program_id(1)
        acc = _dot(a_ref[...], w_ref[j]) + b_ref[...]
        t = (i % tps) * tm + lax.broadcasted_iota(jnp.int32, (tm, 1), 0)
        acc = jnp.where(t >= PAD, acc, 0.0)
        o_ref[...] = _bf(acc)

        @pl.when(j == nj - 1)
        def _():
            gate_ref[...] = acc[:, tn - 128:]

    return _pcall(
        body, name="inproj", grid=(r // tm, nj),
        in_specs=[pl.BlockSpec((tm, d), lambda i, j: (i, 0)), _resident(w3.shape),
                  pl.BlockSpec((1, tn), lambda i, j: (0, j))],
        out_specs=[pl.BlockSpec((tm, tn), lambda i, j: (i, j)), pl.BlockSpec((tm, 128), lambda i, j: (i, 0))],
        out_shape=[jax.ShapeDtypeStruct((r, nj * tn), BF16), jax.ShapeDtypeStruct((r, 128), F32)],
        compiler_params=_cp(("parallel", "arbitrary"), 48),
    )(h0b, w3, bias)


def _mm_tn(a, b, *, name, split=1, colsum=False, tk_want=2112):
    r, m = a.shape
    n = b.shape[1]
    tk = _row_tile(r, tk_want)
    tm = min(m, 1024)
    ns = n // split
    tn = ns
    for cand in (1024, 1152, 640, 512, 128):
        if ns % cand == 0 and cand <= ns:
            tn = cand
            break
    nb = ns // tn
    nk = r // tk

    def body(a_ref, b_ref, o_ref, *rest):
        acc = rest[-1]
        k = pl.program_id(2)

        @pl.when(k == 0)
        def _():
            acc[...] = jnp.zeros_like(acc)

        bt = b_ref[...]
        acc[...] += _dot(_bf(a_ref[...]), _bf(bt), TN)

        @pl.when(k == nk - 1)
        def _():
            o_ref[...] = acc[...]

        if colsum:
            cs_ref = rest[0]

            @pl.when(k == 0)
            def _():
                cs_ref[...] = jnp.zeros_like(cs_ref)

            cs_ref[...] += _colsum(bt.astype(F32))

    out_specs = [pl.BlockSpec((None, tm, tn), lambda i, j, k: (j // nb, i, j % nb))]
    out_shape = [jax.ShapeDtypeStruct((split, m, ns), F32)]
    if colsum:
        assert m == tm
        out_specs.append(pl.BlockSpec((1, tn), lambda i, j, k: (0, j)))
        out_shape.append(jax.ShapeDtypeStruct((1, n), F32))
    res = _pcall(
        body, name=name, grid=(m // tm, n // tn, nk),
        in_specs=[pl.BlockSpec((tk, tm), lambda i, j, k: (k, i)), pl.BlockSpec((tk, tn), lambda i, j, k: (k, j))],
        out_specs=out_specs, out_shape=out_shape,
        scratch_shapes=[pltpu.VMEM((tm, tn), F32)],
        compiler_params=_cp(("parallel", "parallel", "arbitrary"), 56),
    )(a, b)
    return res if colsum else res[0]


def _mm_nt(a, w3, lp, *, name, dep=None):
    r, kdim = a.shape
    nk, n, tk = w3.shape
    assert nk * tk == kdim
    tm = _row_tile(lp, 1056)
    deps = [] if dep is None else [dep]

    def body(a_ref, w_ref, *rest):
        o_ref, acc = rest[-2:]
        k = pl.program_id(1)

        @pl.when(k == 0)
        def _():
            acc[...] = jnp.zeros_like(acc)

        acc[...] += _dot(_bf(a_ref[...]), w_ref[k], NT)

        @pl.when(k == nk - 1)
        def _():
            o_ref[...] = acc[...]

    return _pcall(
        body, name=name, grid=(r // tm, nk),
        in_specs=[pl.BlockSpec((tm, tk), lambda i, k: (i, k)), _resident(w3.shape)]
        + [_const(dp_.shape) for dp_ in deps],
        out_specs=pl.BlockSpec((tm, n), lambda i, k: (i, 0)),
        out_shape=jax.ShapeDtypeStruct((r, n), F32),
        scratch_shapes=[pltpu.VMEM((tm, n), F32)],
        compiler_params=_cp(("parallel", "arbitrary"), 48),
    )(a, w3, *deps)


def _s5_prep(lam_re, lam_im, log_dt, b_re_t, b_im_t):
    g, p = lam_re.shape
    h = b_re_t.shape[0]

    def body(lr_ref, li_ref, ldt_ref, br_ref, bi_ref, pr_ref, pi_ref, bbr_ref, bbi_ref):
        lr, li = lr_ref[...], li_ref[...]
        dt = jnp.exp(ldt_ref[...])
        e = jnp.exp(lr * dt)
        ar, ai = e * jnp.cos(li * dt), e * jnp.sin(li * dt)
        den = lr * lr + li * li
        cr = ((ar - 1.0) * lr + ai * li) / den
        ci = (ai * lr - (ar - 1.0) * li) / den
        br, bi = br_ref[...], bi_ref[...]
        bbr_ref[...] = cr[None] * br - ci[None] * bi
        bbi_ref[...] = cr[None] * bi + ci[None] * br
        xr, xi = ar, ai
        pr_ref[0] = xr
        pi_ref[0] = xi
        for t in range(1, 8):
            xr, xi = xr * ar - xi * ai, xr * ai + xi * ar
            pr_ref[t] = xr
            pi_ref[t] = xi

    sd = jax.ShapeDtypeStruct
    return _pcall(body, name="s5_prep",
                  out_shape=[sd((8, g, p), F32), sd((8, g, p), F32), sd((h, g, p), F32), sd((h, g, p), F32)])(
        lam_re, lam_im, log_dt, b_re_t, b_im_t)


def _s5_prep_bwd(lam_re, lam_im, log_dt, b_re_t, b_im_t, da_re, da_im, dbb_re_t, dbb_im_t):
    g, p = lam_re.shape
    h = b_re_t.shape[0]

    def body(lr_ref, li_ref, ldt_ref, br_ref, bi_ref, dar_ref, dai_ref, dbr_ref, dbi_ref,
             glr_ref, gli_ref, gdt_ref, gbr_ref, gbi_ref):
        lr, li = lr_ref[...], li_ref[...]
        dt = jnp.exp(ldt_ref[...])
        e = jnp.exp(lr * dt)
        ar, ai = e * jnp.cos(li * dt), e * jnp.sin(li * dt)
        den = lr * lr + li * li
        cr = ((ar - 1.0) * lr + ai * li) / den
        ci = (ai * lr - (ar - 1.0) * li) / den
        br, bi = br_ref[...], bi_ref[...]
        gr, gi = dbr_ref[...], dbi_ref[...]
        gbr_ref[...] = gr * cr[None] + gi * ci[None]
        gbi_ref[...] = gi * cr[None] - gr * ci[None]
        gcr = jnp.sum(gr * br + gi * bi, axis=0)
        gci = jnp.sum(gi * br - gr * bi, axis=0)
        ilr, ili = lr / den, -li / den
        gar = dar_ref[...] + gcr * ilr + gci * ili
        gai = dai_ref[...] + gci * ilr - gcr * ili
        qr, qi = cr * ilr - ci * ili, cr * ili + ci * ilr
        glr = -(gcr * qr + gci * qi)
        gli = -(gci * qr - gcr * qi)
        gzr = gar * ar + gai * ai
        gzi = gai * ar - gar * ai
        glr_ref[...] = glr + gzr * dt
        gli_ref[...] = gli + gzi * dt
        gdt_ref[...] = jnp.sum(gzr * lr + gzi * li, axis=1, keepdims=True) * dt

    sd = jax.ShapeDtypeStruct
    return _pcall(body, name="s5_prep_bwd",
                  out_shape=[sd((g, p), F32), sd((g, p), F32), sd((g, 1), F32), sd((h, g, p), F32), sd((h, g, p), F32)])(
        lam_re, lam_im, log_dt, b_re_t, b_im_t, da_re, da_im, dbb_re_t, dbb_im_t)


def _cmul(xr, xi, yr, yi):
    return xr * yr - xi * yi, xr * yi + xi * yr


def _dot5(a, b, dims=NN):
    return _dot(_bf(a), _bf(b), dims)


def _s5_fwd(p3, bk, cre, cim, apow, dskip):
    bsz, lp, _ = p3.shape
    tt = _row_tile(lp, 528, 8)
    nt = lp // tt
    nblk = tt // 8
    hw = 512

    def body(u_ref, bk_ref, cre_ref, cim_ref, ap_ref, d_ref, y_ref, xs_ref, car_ref):
        t = pl.program_id(2)

        @pl.when(t == 0)
        def _():
            car_ref[...] = jnp.zeros_like(car_ref)

        u = u_ref[...].astype(F32)
        xs_ref[...] = _dot5(u, bk_ref[...])
        ap = ap_ref[...]
        apr, api = ap[:, :hw], ap[:, hw:]
        rows = lax.broadcasted_iota(jnp.int32, (8, hw), 0)
        lev = [(d, jnp.where(rows < d, 0.0, jnp.broadcast_to(apr[d - 1:d, :], (8, hw))),
                jnp.where(rows < d, 0.0, jnp.broadcast_to(api[d - 1:d, :], (8, hw)))) for d in (1, 2, 4)]

        def blk(i, carry):
            cr, ci = carry
            off = pl.multiple_of(i * 8, 8)
            x = xs_ref[pl.ds(off, 8), :]
            xr, xi = x[:, :hw], x[:, hw:]
            for d, lr, li in lev:
                mr, mi = _cmul(pltpu.roll(xr, d, 0), pltpu.roll(xi, d, 0), lr, li)
                xr, xi = xr + mr, xi + mi
            mr, mi = _cmul(apr, api, cr, ci)
            xr, xi = xr + mr, xi + mi
            xs_ref[pl.ds(off, 8), :] = jnp.concatenate([xr, xi], axis=1)
            return xr[7:8, :], xi[7:8, :]

        c0 = car_ref[...]
        cr, ci = lax.fori_loop(0, nblk, blk, (c0[0:1, :hw], c0[0:1, hw:]))
        car_ref[...] = jnp.broadcast_to(jnp.concatenate([cr, ci], axis=1), car_ref.shape)
        xs = xs_ref[...]
        y_ref[...] = (_dot5(xs[:, :hw], cre_ref[...]) - _dot5(xs[:, hw:], cim_ref[...])
                      + d_ref[...] * u)

    ub = U_OFF // 128
    return _pcall(
        body, name="s5_fwd", grid=(S5_KCH, bsz, nt),
        in_specs=[pl.BlockSpec((None, tt, 128), lambda k, b, t: (b, t, ub + k)),
                  pl.BlockSpec((None, 128, 2 * hw), lambda k, b, t: (k, 0, 0)),
                  pl.BlockSpec((None, hw, 128), lambda k, b, t: (k, 0, 0)),
                  pl.BlockSpec((None, hw, 128), lambda k, b, t: (k, 0, 0)),
                  pl.BlockSpec((None, 8, 2 * hw), lambda k, b, t: (k, 0, 0)),
                  pl.BlockSpec((1, 128), lambda k, b, t: (0, k))],
        out_specs=[pl.BlockSpec((None, tt, 128), lambda k, b, t: (b, t, k)),
                   pl.BlockSpec((None, None, tt, 2 * hw), lambda k, b, t: (b, k, t, 0))],
        out_shape=[jax.ShapeDtypeStruct((bsz, lp, S5_KCH * 128), F32),
                   jax.ShapeDtypeStruct((bsz, S5_KCH, lp, 2 * hw), F32)],
        scratch_shapes=[pltpu.VMEM((8, 2 * hw), F32)],
        compiler_params=_cp(("parallel", "parallel", "arbitrary"), 40),
    )(p3, bk, cre, cim, apow, dskip)


def _s5_bwd(dp3, p3, dy3, xs, bk, cre, cim, apow_rev, dskip):
    bsz, lp, _ = p3.shape
    tt = _row_tile(lp, 528, 8)
    nt = lp // tt
    nblk = tt // 8
    hw = 512
    tb = tt // 8

    def body(dp_any, u_ref, dy_ref, xs_ref, halo_ref, bkt_ref, cre_ref, cim_ref, ap_ref, d_ref,
             du_ref, dbk_ref, dcre_ref, dcim_ref, da_ref, dd_ref, g_ref, ext_ref, car_ref):
        b = pl.program_id(1)
        t = pl.program_id(2)
        tidx = nt - 1 - t

        @pl.when(t == 0)
        def _():
            car_ref[...] = jnp.zeros_like(car_ref)

        @pl.when((b == 0) & (t == 0))
        def _():
            dbk_ref[...] = jnp.zeros_like(dbk_ref)
            dcre_ref[...] = jnp.zeros_like(dcre_ref)
            dcim_ref[...] = jnp.zeros_like(dcim_ref)
            da_ref[...] = jnp.zeros_like(da_ref)
            dd_ref[...] = jnp.zeros_like(dd_ref)

        u = u_ref[...].astype(F32)
        dy = dy_ref[...]
        g_ref[:, :hw] = _dot5(dy, cre_ref[...])
        g_ref[:, hw:] = -_dot5(dy, cim_ref[...])
        ap = ap_ref[...]
        apr, api = ap[:, :hw], -ap[:, hw:]
        rows = lax.broadcasted_iota(jnp.int32, (8, hw), 0)
        lev = [(d, jnp.where(rows >= 8 - d, 0.0, jnp.broadcast_to(apr[8 - d:9 - d, :], (8, hw))),
                jnp.where(rows >= 8 - d, 0.0, jnp.broadcast_to(api[8 - d:9 - d, :], (8, hw)))) for d in (1, 2, 4)]

        def blk(i, carry):
            cr, ci = carry
            off = pl.multiple_of((nblk - 1 - i) * 8, 8)
            x = g_ref[pl.ds(off, 8), :]
            xr, xi = x[:, :hw], x[:, hw:]
            for d, lr, li in lev:
                mr, mi = _cmul(pltpu.roll(xr, 8 - d, 0), pltpu.roll(xi, 8 - d, 0), lr, li)
                xr, xi = xr + mr, xi + mi
            mr, mi = _cmul(apr, api, cr, ci)
            xr, xi = xr + mr, xi + mi
            g_ref[pl.ds(off, 8), :] = jnp.concatenate([xr, xi], axis=1)
            return xr[0:1, :], xi[0:1, :]

        c0 = car_ref[...]
        cr, ci = lax.fori_loop(0, nblk, blk, (c0[0:1, :hw], c0[0:1, hw:]))
        car_ref[...] = jnp.broadcast_to(jnp.concatenate([cr, ci], axis=1), car_ref.shape)

        gg = g_ref[...]
        du = _dot5(gg, bkt_ref[...]) + d_ref[...] * dy
        trow = tidx * tt + lax.broadcasted_iota(jnp.int32, (tt, 1), 0)
        du_ref[...] = jnp.where(trow >= PAD, du, 0.0).astype(du_ref.dtype)
        dbk_ref[...] += _dot5(u, gg, TN)
        xsv = xs_ref[...]
        dcre_ref[...] += _dot5(dy, xsv[:, :hw], TN)
        dcim_ref[...] -= _dot5(dy, xsv[:, hw:], TN)
        dd_ref[...] += _colsum(dy * u)
        ext_ref[0:8, :] = jnp.where(tidx == 0, 0.0, halo_ref[...])
        ext_ref[8:, :] = xsv
        xp = ext_ref[pl.ds(7, tt), :]
        gr, gi, pr, pi = gg[:, :hw], gg[:, hw:], xp[:, :hw], xp[:, hw:]
        da_ref[:, :hw] += _colsum(gr * pr + gi * pi)
        da_ref[:, hw:] += _colsum(gi * pr - gr * pi)

    ub = U_OFF // 128
    sd = jax.ShapeDtypeStruct
    rt = lambda t: nt - 1 - t
    tr = lambda a: jnp.swapaxes(a, 1, 2)
    res = _pcall(
        body, name="s5_bwd", grid=(S5_KCH, bsz, nt),
        in_specs=[pl.BlockSpec(memory_space=pl.ANY),
                  pl.BlockSpec((None, tt, 128), lambda k, b, t: (b, rt(t), ub + k)),
                  pl.BlockSpec((None, tt, 128), lambda k, b, t: (b, rt(t), k)),
                  pl.BlockSpec((None, None, tt, 2 * hw), lambda k, b, t: (b, k, rt(t), 0)),
                  pl.BlockSpec((None, None, 8, 2 * hw), lambda k, b, t: (b, k, jnp.maximum(rt(t) * tb - 1, 0), 0)),
                  pl.BlockSpec((None, 2 * hw, 128), lambda k, b, t: (k, 0, 0)),
                  pl.BlockSpec((None, 128, hw), lambda k, b, t: (k, 0, 0)),
                  pl.BlockSpec((None, 128, hw), lambda k, b, t: (k, 0, 0)),
                  pl.BlockSpec((None, 8, 2 * hw), lambda k, b, t: (k, 0, 0)),
                  pl.BlockSpec((1, 128), lambda k, b, t: (0, k))],
        out_specs=[pl.BlockSpec((None, tt, 128), lambda k, b, t: (b, rt(t), ub + k)),
                   pl.BlockSpec((None, 128, 2 * hw), lambda k, b, t: (k, 0, 0)),
                   pl.BlockSpec((None, 128, hw), lambda k, b, t: (k, 0, 0)),
                   pl.BlockSpec((None, 128, hw), lambda k, b, t: (k, 0, 0)),
                   pl.BlockSpec((None, 1, 2 * hw), lambda k, b, t: (k, 0, 0)),
                   pl.BlockSpec((1, 128), lambda k, b, t: (0, k))],
        out_shape=[sd(dp3.shape, dp3.dtype), sd((S5_KCH, 128, 2 * hw), F32), sd((S5_KCH, 128, hw), F32),
                   sd((S5_KCH, 128, hw), F32), sd((S5_KCH, 1, 2 * hw), F32), sd((1, S5_KCH * 128), F32)],
        scratch_shapes=[pltpu.VMEM((tt, 2 * hw), F32), pltpu.VMEM((tt + 8, 2 * hw), F32), pltpu.VMEM((8, 2 * hw), F32)],
        input_output_aliases={0: 0},
        compiler_params=_cp(("arbitrary", "arbitrary", "arbitrary"), 48),
    )(dp3, p3, dy3, xs, xs, tr(bk), tr(cre), tr(cim), apow_rev, dskip)
    return res[0], res[1], tr(res[2]), tr(res[3]), res[4], res[5]


_G0 = math.sqrt(2.0 / math.pi)
_G1 = 0.044715


def _gelu(y):
    return 0.5 * y * (1.0 + jnp.tanh(_G0 * (y + _G1 * y * y * y)))


def _gelu_grad(y):
    th = jnp.tanh(_G0 * (y + _G1 * y * y * y))
    return 0.5 * (1.0 + th) + 0.5 * y * (1.0 - th * th) * _G0 * (1.0 + 3.0 * _G1 * y * y)


def _glu_fwd(y_s5, wglu_g, lp):
    r, w = y_s5.shape
    tm = _row_tile(lp, 416)
    cw = wglu_g.shape[2]

    def body(y_ref, w_ref, gy_ref, z_ref, o_ref):
        gy = _bf(_gelu(y_ref[...]))
        gy_ref[...] = gy
        zs = [_dot(gy, w_ref[s]) for s in range(4)]
        for s in range(4):
            z_ref[:, s * cw:(s + 1) * cw] = zs[s]
        o_ref[:, :cw] = zs[0] * _sig(zs[2])
        o_ref[:, cw:] = zs[1] * _sig(zs[3])

    sd = jax.ShapeDtypeStruct
    return _pcall(
        body, name="glu_fwd", grid=(r // tm,),
        in_specs=[pl.BlockSpec((tm, w), lambda i: (i, 0)), _resident(wglu_g.shape)],
        out_specs=[pl.BlockSpec((tm, w), lambda i: (i, 0)), pl.BlockSpec((tm, 4 * cw), lambda i: (i, 0)),
                   pl.BlockSpec((tm, 2 * cw), lambda i: (i, 0))],
        out_shape=[sd((r, w), BF16), sd((r, 4 * cw), F32), sd((r, 2 * cw), F32)],
        compiler_params=_cp(("parallel",), 40),
    )(y_s5, wglu_g)


def _glu_bwd(dyg, z, y_s5, wglu_g, lp):
    r, w = y_s5.shape
    tm = _row_tile(lp, 416)
    cw = wglu_g.shape[2]

    def body(d_ref, z_ref, y_ref, w_ref, dz_ref, dy_ref):
        d = d_ref[...]
        zz = z_ref[...]
        acc = jnp.zeros((tm, w), F32)
        for s in range(2):
            z1 = zz[:, s * cw:(s + 1) * cw]
            sg = _sig(zz[:, (2 + s) * cw:(3 + s) * cw])
            dd = d[:, s * cw:(s + 1) * cw]
            dz1 = _bf(dd * sg)
            dz2 = _bf(dd * z1 * sg * (1.0 - sg))
            dz_ref[:, s * cw:(s + 1) * cw] = dz1
            dz_ref[:, (2 + s) * cw:(3 + s) * cw] = dz2
            acc += _dot(dz1, w_ref[s], NT) + _dot(dz2, w_ref[2 + s], NT)
        dy_ref[...] = acc * _gelu_grad(y_ref[...])

    sd = jax.ShapeDtypeStruct
    return _pcall(
        body, name="glu_bwd", grid=(r // tm,),
        in_specs=[pl.BlockSpec((tm, 2 * cw), lambda i: (i, 0)), pl.BlockSpec((tm, 4 * cw), lambda i: (i, 0)),
                  pl.BlockSpec((tm, w), lambda i: (i, 0)), _resident(wglu_g.shape)],
        out_specs=[pl.BlockSpec((tm, 4 * cw), lambda i: (i, 0)), pl.BlockSpec((tm, w), lambda i: (i, 0))],
        out_shape=[sd((r, 4 * cw), BF16), sd((r, w), F32)],
        compiler_params=_cp(("parallel",), 40),
    )(dyg, z, y_s5, wglu_g)


def _conv_fwd(p3, cw, cb):
    bsz, lp, _ = p3.shape
    tt = _row_tile(lp, 416)
    nt = lp // tt
    tb = tt // 8
    c = cw.shape[1]
    qb = Q_OFF // c

    hr = HALO_ROWS
    off = hr - (CONV_W - 1)

    def body(x_ref, halo_ref, w_ref, b_ref, pre_ref, act_ref, ext_ref):
        t = pl.program_id(1)
        ext_ref[0:hr, :] = jnp.where(t == 0, 0.0, halo_ref[...].astype(F32))
        ext_ref[hr:, :] = x_ref[...].astype(F32)
        w = w_ref[...]
        acc = b_ref[...] + w[0:1, :] * ext_ref[pl.ds(off, tt), :]
        for j in range(1, CONV_W):
            acc = acc + w[j:j + 1, :] * ext_ref[pl.ds(off + j, tt), :]
        pre_ref[...] = acc
        act_ref[...] = acc * _sig(acc)

    sd = jax.ShapeDtypeStruct
    return _pcall(
        body, name="conv_fwd", grid=(bsz, nt),
        in_specs=[pl.BlockSpec((None, tt, c), lambda b, t: (b, t, qb)),
                  pl.BlockSpec((None, hr, c), lambda b, t: (b, jnp.maximum(t * (tt // hr) - 1, 0), qb)),
                  _const((CONV_W, c)), _const((1, c))],
        out_specs=[pl.BlockSpec((None, tt, c), lambda b, t: (b, t, 0))] * 2,
        out_shape=[sd((bsz, lp, c), F32)] * 2,
        scratch_shapes=[pltpu.VMEM((tt + hr, c), F32)],
        compiler_params=_cp(("parallel", "parallel")),
    )(p3, p3, cw, cb)


def _conv_bwd(dp3, p3, dact3, pre3, cw):
    bsz, lp, _ = p3.shape
    tt = _row_tile(lp, 416)
    nt = lp // tt
    tb = tt // 8
    c = cw.shape[1]
    qb = Q_OFF // c

    hr = HALO_ROWS
    off = hr - (CONV_W - 1)

    def silu_grad(x):
        s = _sig(x)
        return s * (1.0 + x * (1.0 - s))

    def body(dp_any, x_ref, xh_ref, d_ref, dh_ref, pre_ref, preh_ref, w_ref, o_ref, dw_ref, db_ref, ext_ref, dext_ref):
        b = pl.program_id(0)
        t = pl.program_id(1)

        @pl.when((b == 0) & (t == 0))
        def _():
            dw_ref[...] = jnp.zeros_like(dw_ref)
            db_ref[...] = jnp.zeros_like(db_ref)

        dc = d_ref[...] * silu_grad(pre_ref[...])
        dch = jnp.where(t == nt - 1, 0.0, dh_ref[...] * silu_grad(preh_ref[...]))
        dext_ref[0:tt, :] = dc
        dext_ref[tt:, :] = dch
        ext_ref[0:hr, :] = jnp.where(t == 0, 0.0, xh_ref[...].astype(F32))
        ext_ref[hr:, :] = x_ref[...].astype(F32)
        w = w_ref[...]
        acc = w[CONV_W - 1:CONV_W, :] * dc
        for j in range(CONV_W - 1):
            acc = acc + w[j:j + 1, :] * dext_ref[pl.ds(CONV_W - 1 - j, tt), :]
        trow = t * tt + lax.broadcasted_iota(jnp.int32, (tt, 1), 0)
        o_ref[...] = jnp.where(trow >= PAD, acc, 0.0).astype(o_ref.dtype)
        db_ref[...] += _colsum(dc)
        for j in range(CONV_W):
            dw_ref[j:j + 1, :] += _colsum(dc * ext_ref[pl.ds(off + j, tt), :])

    sd = jax.ShapeDtypeStruct
    nxt = lambda t: jnp.minimum((t + 1) * tb, lp // 8 - 1)
    return _pcall(
        body, name="conv_bwd", grid=(bsz, nt),
        in_specs=[pl.BlockSpec(memory_space=pl.ANY),
                  pl.BlockSpec((None, tt, c), lambda b, t: (b, t, qb)),
                  pl.BlockSpec((None, hr, c), lambda b, t: (b, jnp.maximum(t * (tt // hr) - 1, 0), qb)),
                  pl.BlockSpec((None, tt, c), lambda b, t: (b, t, 0)),
                  pl.BlockSpec((None, 8, c), lambda b, t: (b, nxt(t), 0)),
                  pl.BlockSpec((None, tt, c), lambda b, t: (b, t, 0)),
                  pl.BlockSpec((None, 8, c), lambda b, t: (b, nxt(t), 0)),
                  _const((CONV_W, c))],
        out_specs=[pl.BlockSpec((None, tt, c), lambda b, t: (b, t, qb)), _const((CONV_W, c)), _const((1, c))],
        out_shape=[sd(dp3.shape, dp3.dtype), sd((CONV_W, c), F32), sd((1, c), F32)],
        scratch_shapes=[pltpu.VMEM((tt + hr, c), F32), pltpu.VMEM((tt + 8, c), F32)],
        input_output_aliases={0: 0},
        compiler_params=_cp(("arbitrary", "arbitrary")),
    )(dp3, p3, p3, dact3, dact3, pre3, pre3, cw)


def _mlstm_gates(g, h_idx, c_idx, lc):
    lane = lax.broadcasted_iota(jnp.int32, g.shape, 1)
    i_col = jnp.sum(jnp.where(lane == h_idx, g, 0.0), axis=1, keepdims=True)
    f_col = jnp.sum(jnp.where(lane == M_HEADS + h_idx, g, 0.0), axis=1, keepdims=True)
    row = lax.broadcasted_iota(jnp.int32, (lc, 1), 0)
    valid = (c_idx * lc + row) >= PAD
    li = jnp.where(valid, i_col, NEG)
    lf = jnp.where(valid, jnp.minimum(f_col, 0.0) - jnp.log(1.0 + jnp.exp(-jnp.abs(f_col))), 0.0)
    r2 = lax.broadcasted_iota(jnp.int32, (lc, lc), 0)
    c2 = lax.broadcasted_iota(jnp.int32, (lc, lc), 1)
    eye = r2 == c2
    tril = r2 >= c2
    to_row = lambda col: jnp.sum(jnp.where(eye, col, 0.0), axis=0, keepdims=True)
    lf_row = to_row(lf)
    b_col = jnp.sum(jnp.where(tril, lf_row, 0.0), axis=1, keepdims=True)
    b_row = to_row(b_col)
    li_row = to_row(li)
    d_mat = jnp.where(tril, b_col - b_row + li_row, NEG)
    return dict(f_col=f_col, valid=valid, li=li, b_col=b_col, d_mat=d_mat, eye=eye, r2=r2, c2=c2, row=row,
                to_row=to_row)


def _mlstm_chunk(q, ks, v, gq, c_st, n_st, m_st, lc):
    b_col, d_mat = gq["b_col"], gq["d_mat"]
    m_inter = b_col + m_st
    m_row = jnp.maximum(m_inter, jnp.max(d_mat, axis=1, keepdims=True))
    w_intra = jnp.exp(d_mat - m_row)
    w_inter = jnp.exp(m_inter - m_row)
    qb, kb, vb, cb = _bf(q), _bf(ks), _bf(v), _bf(c_st)
    s = _dot(qb, kb, NT) * w_intra
    qc = _dot(qb, cb)
    num = _dot(_bf(s), vb) + w_inter * qc
    qn = jnp.sum(q * n_st, axis=1, keepdims=True)
    den = jnp.sum(s, axis=1, keepdims=True) + w_inter * qn
    e = jnp.exp(-m_row)
    nn = jnp.maximum(jnp.abs(den), e)
    b_last = b_col[lc - 1:lc, :]
    g_log = b_last - b_col + gq["li"]
    m_new = jnp.maximum(b_last + m_st, jnp.max(g_log, axis=0, keepdims=True))
    w_k = jnp.exp(g_log - m_new)
    decay = jnp.exp(b_last + m_st - m_new)
    return dict(w_intra=w_intra, w_inter=w_inter, qb=qb, kb=kb, vb=vb, cb=cb, s=s, qc=qc, num=num, qn=qn, den=den,
                e=e, nn=nn, m_new=m_new, w_k=w_k, decay=decay)


def _chunks_per_step(nc):
    return max(c for c in (3, 2, 1) if nc % c == 0)


def _mlstm_fwd(qk3, p3, pg3):
    bsz, lp, _ = p3.shape
    lc = M_CHUNK
    nc = lp // lc
    dk, dv = 128, 256
    scale = dk ** -0.5

    cps = _chunks_per_step(nc)
    rows = cps * lc

    def body(q_ref, k_ref, v_ref, g_ref, h_ref, cs_ref, ns_ref, ms_ref, c_sc, n_sc, m_sc):
        st = pl.program_id(1)

        @pl.when(st == 0)
        def _():
            c_sc[...] = jnp.zeros_like(c_sc)
            n_sc[...] = jnp.zeros_like(n_sc)
            m_sc[...] = jnp.zeros_like(m_sc)

        for j in range(cps):
            rs = slice(j * lc, (j + 1) * lc)
            g = g_ref[rs, :]
            for hh in range(M_HEADS):
                c_st, n_st, m_all = c_sc[hh], n_sc[hh], m_sc[hh]
                cs_ref[hh, j] = c_st
                ns_ref[hh, j] = n_st
                ms_ref[hh, j] = m_all
                m_st = m_all[:, 0:1]
                q = q_ref[rs, hh * dk:(hh + 1) * dk]
                ks = k_ref[rs, hh * dk:(hh + 1) * dk] * scale
                v = v_ref[rs, hh * dv:(hh + 1) * dv]
                gq = _mlstm_gates(g, hh, st * cps + j, lc)
                f = _mlstm_chunk(q, ks, v, gq, c_st, n_st, m_st, lc)
                h_ref[rs, hh * dv:(hh + 1) * dv] = f["num"] / f["nn"]
                kw = ks * f["w_k"]
                c_sc[hh] = f["decay"] * c_st + _dot(_bf(kw), f["vb"], TN)
                n_sc[hh] = f["decay"] * n_st + _colsum(kw)
                m_sc[hh] = jnp.broadcast_to(f["m_new"], (1, 128))

    sd = jax.ShapeDtypeStruct
    nh = M_HEADS
    return _pcall(
        body, name="mlstm_fwd", grid=(bsz, nc // cps),
        in_specs=[pl.BlockSpec((None, rows, nh * dk), lambda b, c: (b, c, 0)),
                  pl.BlockSpec((None, rows, nh * dk), lambda b, c: (b, c, 1)),
                  pl.BlockSpec((None, rows, nh * dv), lambda b, c: (b, c, V_OFF // (nh * dv))),
                  pl.BlockSpec((None, rows, 128), lambda b, c: (b, c, 0))],
        out_specs=[pl.BlockSpec((None, rows, nh * dv), lambda b, c: (b, c, 0)),
                   pl.BlockSpec((None, nh, cps, dk, dv), lambda b, c: (b, 0, c, 0, 0)),
                   pl.BlockSpec((None, nh, cps, 1, dk), lambda b, c: (b, 0, c, 0, 0)),
                   pl.BlockSpec((None, nh, cps, 1, 128), lambda b, c: (b, 0, c, 0, 0))],
        out_shape=[sd((bsz, lp, nh * dv), F32), sd((bsz, nh, nc, dk, dv), F32),
                   sd((bsz, nh, nc, 1, dk), F32), sd((bsz, nh, nc, 1, 128), F32)],
        scratch_shapes=[pltpu.VMEM((nh, dk, dv), F32), pltpu.VMEM((nh, 1, dk), F32), pltpu.VMEM((nh, 1, 128), F32)],
        compiler_params=_cp(("parallel", "arbitrary")),
    )(qk3, qk3, p3, pg3)


def _mlstm_bwd(dp3, qk3, p3, pg3, dh3, cs, ns, ms):
    bsz, lp, _ = p3.shape
    lc = M_CHUNK
    nc = lp // lc
    dk, dv = 128, 256
    scale = dk ** -0.5

    cps = _chunks_per_step(nc)
    nst = nc // cps
    rows = cps * lc

    def body(dp_any, q_ref, k_ref, v_ref, g_ref, dh_ref, cs_ref, ns_ref, ms_ref,
             dv_ref, dqk_ref, dg_ref, dc_sc, dn_sc):
        t = pl.program_id(1)
        st = nst - 1 - t

        @pl.when(t == 0)
        def _():
            dc_sc[...] = jnp.zeros_like(dc_sc)
            dn_sc[...] = jnp.zeros_like(dn_sc)

        lane = lax.broadcasted_iota(jnp.int32, (lc, 128), 1)
        for j in reversed(range(cps)):
            rs = slice(j * lc, (j + 1) * lc)
            g = g_ref[rs, :]
            dgate = jnp.zeros((lc, 128), F32)
            for hh in range(M_HEADS):
                dgate = head(hh, j, rs, st * cps + j, g, lane, dgate, q_ref, k_ref, v_ref, dh_ref, cs_ref, ns_ref,
                             ms_ref, dv_ref, dqk_ref, dc_sc, dn_sc)
            dg_ref[rs, :] = dgate.astype(dg_ref.dtype)

    def head(hh, j, sl, c, g, lane, dgate, q_ref, k_ref, v_ref, dh_ref, cs_ref, ns_ref, ms_ref, dv_ref, dqk_ref,
             dc_sc, dn_sc):
        c_st, n_st = cs_ref[hh, j], ns_ref[hh, j]
        m_st = ms_ref[hh, j][:, 0:1]
        q = q_ref[sl, hh * dk:(hh + 1) * dk]
        ks = k_ref[sl, hh * dk:(hh + 1) * dk] * scale
        v = v_ref[sl, hh * dv:(hh + 1) * dv]
        dh = dh_ref[sl, hh * dv:(hh + 1) * dv]
        gq = _mlstm_gates(g, hh, c, lc)
        f = _mlstm_chunk(q, ks, v, gq, c_st, n_st, m_st, lc)
        eye, r2, c2, row, valid = gq["eye"], gq["r2"], gq["c2"], gq["row"], gq["valid"]
        w_intra, w_inter, s, nn, den = f["w_intra"], f["w_inter"], f["s"], f["nn"], f["den"]
        qb, kb, vb, cb, w_k, decay = f["qb"], f["kb"], f["vb"], f["cb"], f["w_k"], f["decay"]
        d_c, d_n = dc_sc[hh], dn_sc[hh]
        d_cb = _bf(d_c)

        hout = f["num"] / nn
        dnum = dh / nn
        d_nn = -jnp.sum(dh * hout, axis=1, keepdims=True) / nn
        dden = jnp.where(jnp.abs(den) > f["e"], d_nn * jnp.sign(den), 0.0)
        wdnum = w_inter * dnum
        wdden = w_inter * dden
        ds = _dot(_bf(dnum), vb, NT) + dden
        dsw = _bf(ds * w_intra)
        dq = _dot(dsw, kb) + _dot(_bf(wdnum), cb, NT) + wdden * n_st
        dkw = _dot(vb, d_cb, NT) + d_n
        dks = _dot(dsw, qb, TN) + dkw * w_k
        kw = ks * w_k
        dvv = _dot(_bf(s), _bf(dnum), TN) + _dot(_bf(kw), d_cb)
        dd = ds * s
        rs = jnp.sum(dd, axis=1, keepdims=True)
        cs_col = jnp.sum(jnp.where(eye, jnp.sum(dd, axis=0, keepdims=True), 0.0), axis=1, keepdims=True)
        dwi = jnp.sum(dnum * f["qc"], axis=1, keepdims=True) + dden * f["qn"]
        db = rs - cs_col + dwi * w_inter
        dli = cs_col
        ddecay = jnp.sum(jnp.sum(d_c * c_st, axis=1, keepdims=True), axis=0, keepdims=True) \
            + jnp.sum(d_n * n_st, axis=1, keepdims=True)
        dgl = jnp.sum(dkw * ks, axis=1, keepdims=True) * w_k
        dblast = ddecay * decay + jnp.sum(dgl, axis=0, keepdims=True)
        db = db - dgl + jnp.where(row == lc - 1, dblast, 0.0)
        dli = dli + dgl
        db_row = gq["to_row"](db)
        dlf = jnp.sum(jnp.where(c2 >= r2, db_row, 0.0), axis=1, keepdims=True)
        dlf = jnp.where(valid, dlf, 0.0)
        dgate = jnp.where(lane == hh, jnp.where(valid, dli, 0.0), dgate)
        dgate = jnp.where(lane == M_HEADS + hh, dlf / (1.0 + jnp.exp(gq["f_col"])), dgate)
        dqk_ref[sl, hh * dk:(hh + 1) * dk] = dq
        dqk_ref[sl, (M_HEADS + hh) * dk:(M_HEADS + hh + 1) * dk] = dks * scale
        dv_ref[sl, hh * dv:(hh + 1) * dv] = dvv.astype(dv_ref.dtype)
        dc_sc[hh] = decay * d_c + _dot(qb, _bf(wdnum), TN)
        dn_sc[hh] = decay * d_n + _colsum(q * wdden)
        return dgate

    sd = jax.ShapeDtypeStruct
    nh = M_HEADS
    rc = lambda c: nst - 1 - c
    return _pcall(
        body, name="mlstm_bwd", grid=(bsz, nst),
        in_specs=[pl.BlockSpec(memory_space=pl.ANY),
                  pl.BlockSpec((None, rows, nh * dk), lambda b, c: (b, rc(c), 0)),
                  pl.BlockSpec((None, rows, nh * dk), lambda b, c: (b, rc(c), 1)),
                  pl.BlockSpec((None, rows, nh * dv), lambda b, c: (b, rc(c), V_OFF // (nh * dv))),
                  pl.BlockSpec((None, rows, 128), lambda b, c: (b, rc(c), 0)),
                  pl.BlockSpec((None, rows, nh * dv), lambda b, c: (b, rc(c), 0)),
                  pl.BlockSpec((None, nh, cps, dk, dv), lambda b, c: (b, 0, rc(c), 0, 0)),
                  pl.BlockSpec((None, nh, cps, 1, dk), lambda b, c: (b, 0, rc(c), 0, 0)),
                  pl.BlockSpec((None, nh, cps, 1, 128), lambda b, c: (b, 0, rc(c), 0, 0))],
        out_specs=[pl.BlockSpec((None, rows, nh * dv), lambda b, c: (b, rc(c), V_OFF // (nh * dv))),
                   pl.BlockSpec((None, rows, 2 * nh * dk), lambda b, c: (b, rc(c), 0)),
                   pl.BlockSpec((None, rows, 128), lambda b, c: (b, rc(c), 0))],
        out_shape=[sd(dp3.shape, dp3.dtype), sd((bsz, lp, 2 * nh * dk), F32), sd((bsz, lp, 128), dp3.dtype)],
        scratch_shapes=[pltpu.VMEM((nh, dk, dv), F32), pltpu.VMEM((nh, 1, dk), F32)],
        input_output_aliases={0: 0},
        compiler_params=_cp(("arbitrary", "arbitrary")),
    )(dp3, qk3, qk3, p3, pg3, dh3, cs, ns, ms)


def _headnorm(x):
    dv = x.shape[1] // M_HEADS
    xh, rs = [], []
    for h in range(M_HEADS):
        xx = x[:, h * dv:(h + 1) * dv]
        mu = jnp.mean(xx, axis=-1, keepdims=True)
        xc = xx - mu
        rstd = lax.rsqrt(jnp.mean(xc * xc, axis=-1, keepdims=True) + LN_EPS)
        xh.append(xc * rstd)
        rs.append(rstd)
    return jnp.concatenate(xh, axis=1), rs


def _mix_fwd(hm, p, ys5g, h0, gn, wmo_bf, wo_bf, lp):
    r, d = hm.shape
    tm = _row_tile(lp, 384)

    def body(hm_ref, o_ref, gs_ref, gm_ref, ys_ref, h0_ref, gn_ref, wmo_ref, wo_ref,
             ymin_ref, ym_ref, mix_ref, r1_ref):
        xhat, _ = _headnorm(hm_ref[...])
        ymin = _bf(_sig(o_ref[...].astype(F32)) * (xhat * gn_ref[...]))
        ymin_ref[...] = ymin
        ym = _dot(ymin, wmo_ref[...])
        ym_ref[...] = ym
        mix = _bf(_sig(gs_ref[...].astype(F32)) * ys_ref[...] + _sig(gm_ref[...].astype(F32)) * ym)
        mix_ref[...] = mix
        r1_ref[...] = ALPHA * h0_ref[...] + _dot(mix, wo_ref[...])

    sd = jax.ShapeDtypeStruct
    row = pl.BlockSpec((tm, d), lambda i: (i, 0))
    return _pcall(
        body, name="mix_fwd", grid=(r // tm,),
        in_specs=[row, pl.BlockSpec((tm, d), lambda i: (i, O_OFF // d)), pl.BlockSpec((tm, d), lambda i: (i, GS_OFF // d)),
                  pl.BlockSpec((tm, d), lambda i: (i, GM_OFF // d)), row, row, _const((1, d)),
                  _resident((d, d)), _resident((d, d))],
        out_specs=[row] * 4,
        out_shape=[sd((r, d), BF16), sd((r, d), F32), sd((r, d), BF16), sd((r, d), F32)],
        compiler_params=_cp(("parallel",), 48),
    )(hm, p, p, p, ys5g, h0, gn, wmo_bf, wo_bf)


def _mix_bwd(dr1, wo_bf, wmo_bf, p, ys5g, ym, hm, gn, lp):
    r, d = hm.shape
    tm = _row_tile(lp, 208)
    dv = d // M_HEADS

    def body(dr1_ref, wo_ref, wmo_ref, o_ref, gs_ref, gm_ref, ys_ref, ym_ref, hm_ref, gn_ref,
             dp_ref, dys_ref, dym_ref, dhm_ref, dgn_ref):
        i = pl.program_id(0)

        @pl.when(i == 0)
        def _():
            dgn_ref[...] = jnp.zeros_like(dgn_ref)

        dmix = _dot(_bf(dr1_ref[...]), wo_ref[...], NT)
        sgs, sgm, so = (_sig(gs_ref[...].astype(F32)), _sig(gm_ref[...].astype(F32)), _sig(o_ref[...].astype(F32)))
        dys_ref[...] = dmix * sgs
        dp_ref[:, d:2 * d] = _bf(dmix * ys_ref[...] * sgs * (1.0 - sgs))
        dym = dmix * sgm
        dym_ref[...] = _bf(dym)
        dp_ref[:, 2 * d:3 * d] = _bf(dmix * ym_ref[...] * sgm * (1.0 - sgm))
        dymin = _dot(_bf(dym), wmo_ref[...], NT)
        xhat, rs = _headnorm(hm_ref[...])
        gn_ = gn_ref[...]
        dp_ref[:, 0:d] = _bf(dymin * (xhat * gn_) * so * (1.0 - so))
        dhn = dymin * so
        dgn_ref[...] += _colsum(dhn * xhat)
        dxh = dhn * gn_
        for h in range(M_HEADS):
            sl = slice(h * dv, (h + 1) * dv)
            a, xh = dxh[:, sl], xhat[:, sl]
            m1 = jnp.mean(a, axis=-1, keepdims=True)
            m2 = jnp.mean(a * xh, axis=-1, keepdims=True)
            dhm_ref[:, sl] = rs[h] * (a - m1 - xh * m2)

    sd = jax.ShapeDtypeStruct
    row = pl.BlockSpec((tm, d), lambda i: (i, 0))
    vec = _const((1, d))
    return _pcall(
        body, name="mix_bwd", grid=(r // tm,),
        in_specs=[row, _resident((d, d)), _resident((d, d)),
                  pl.BlockSpec((tm, d), lambda i: (i, O_OFF // d)), pl.BlockSpec((tm, d), lambda i: (i, GS_OFF // d)),
                  pl.BlockSpec((tm, d), lambda i: (i, GM_OFF // d)), row, row, row, vec],
        out_specs=[pl.BlockSpec((tm, 3 * d), lambda i: (i, 0)), row, row, row, vec],
        out_shape=[sd((r, NP), BF16), sd((r, d), F32), sd((r, d), BF16), sd((r, d), F32), sd((1, d), F32)],
        compiler_params=_cp(("arbitrary",), 48),
    )(dr1, wo_bf, wmo_bf, p, p, p, ys5g, ym, hm, gn)


def _mlp_fwd(r1, tgt, g1, b1, wup_g, wdn_bf, bup, g2, b2, lp):
    r, d = r1.shape
    tm = _row_tile(lp, 352)
    tps = lp // tm
    nf = wup_g.shape[0]

    def body(r1_ref, t_ref, g1_ref, b1_ref, wup_ref, wdn_ref, bup_ref, g2_ref, b2_ref,
             dr2_ref, h1b_ref, act_ref, loss_ref, dg2_ref, db2_ref):
        i = pl.program_id(0)

        @pl.when(i == 0)
        def _():
            loss_ref[...] = jnp.zeros_like(loss_ref)
            dg2_ref[...] = jnp.zeros_like(dg2_ref)
            db2_ref[...] = jnp.zeros_like(db2_ref)

        h1, _, _ = _ln_fwd(r1_ref[...], g1_ref[...], b1_ref[...])
        h1b = _bf(h1)
        h1b_ref[...] = h1b
        ff = jnp.zeros((tm, d), F32)
        for s in range(nf):
            up = _dot(h1b, wup_ref[s]) + bup_ref[:, s * d:(s + 1) * d]
            a = jnp.maximum(up, 0.0)
            a = _bf(a * a)
            act_ref[:, s * d:(s + 1) * d] = a
            ff = ff + _dot(a, wdn_ref[s * d:(s + 1) * d, :])
        r2 = ALPHA * h1 + ff
        g2 = g2_ref[...]
        y, xhat, rstd = _ln_fwd(r2, g2, b2_ref[...])
        t = (i % tps) * tm + lax.broadcasted_iota(jnp.int32, (tm, 1), 0)
        diff = jnp.where(t >= PAD + N_META, y - t_ref[...], 0.0)
        loss_ref[...] += 0.5 / d * jnp.sum(jnp.sum(diff * diff, axis=1, keepdims=True), axis=0, keepdims=True)
        dy = diff * (1.0 / d)
        dg2_ref[...] += _colsum(dy * xhat)
        db2_ref[...] += _colsum(dy)
        dr2_ref[...] = _ln_bwd(dy, xhat, rstd, g2)

    sd = jax.ShapeDtypeStruct
    row = pl.BlockSpec((tm, d), lambda i: (i, 0))
    vec = _const((1, d))
    return _pcall(
        body, name="mlp_fwd", grid=(r // tm,),
        in_specs=[row, row, vec, vec, _resident(wup_g.shape), _resident(wdn_bf.shape), _const((1, nf * d)), vec, vec],
        out_specs=[row, row, pl.BlockSpec((tm, nf * d), lambda i: (i, 0)), _const((1, 128)), vec, vec],
        out_shape=[sd((r, d), F32), sd((r, d), BF16), sd((r, nf * d), BF16), sd((1, 128), F32), sd((1, d), F32),
                   sd((1, d), F32)],
        compiler_params=_cp(("arbitrary",), 56),
    )(r1, tgt, g1, b1, wup_g, wdn_bf, bup, g2, b2)


def _mlp_bwd(h1b, dr2, r1, g1, wup_g, wdn_bf, bup, lp):
    r, d = h1b.shape
    tm = _row_tile(lp, 352)
    nf = wup_g.shape[0]

    def body(h1_ref, dr2_ref, r1_ref, g1_ref, wup_ref, wdn_ref, bup_ref, dr1_ref, dup_ref, dbup_ref, dg1_ref, db1_ref):
        i = pl.program_id(0)

        @pl.when(i == 0)
        def _():
            dbup_ref[...] = jnp.zeros_like(dbup_ref)
            dg1_ref[...] = jnp.zeros_like(dg1_ref)
            db1_ref[...] = jnp.zeros_like(db1_ref)

        h1b = h1_ref[...]
        dr2 = dr2_ref[...]
        dr2b = _bf(dr2)
        acc = ALPHA * dr2
        for s in range(nf):
            up = _dot(h1b, wup_ref[s]) + bup_ref[:, s * d:(s + 1) * d]
            dact = _dot(dr2b, wdn_ref[s * d:(s + 1) * d, :], NT)
            dup = dact * (2.0 * jnp.maximum(up, 0.0))
            dbup_ref[:, s * d:(s + 1) * d] += _colsum(dup)
            dupb = _bf(dup)
            dup_ref[:, s * d:(s + 1) * d] = dupb
            acc = acc + _dot(dupb, wup_ref[s], NT)
        g1 = g1_ref[...]
        _, xhat1, rstd1 = _ln_fwd(r1_ref[...], g1, 0.0)
        dr1_ref[...] = _ln_bwd(acc, xhat1, rstd1, g1)
        dg1_ref[...] += _colsum(acc * xhat1)
        db1_ref[...] += _colsum(acc)

    sd = jax.ShapeDtypeStruct
    row = pl.BlockSpec((tm, d), lambda i: (i, 0))
    vec = _const((1, d))
    return _pcall(
        body, name="mlp_bwd", grid=(r // tm,),
        in_specs=[row, row, row, vec, _resident(wup_g.shape), _resident(wdn_bf.shape), _const((1, nf * d))],
        out_specs=[row, pl.BlockSpec((tm, nf * d), lambda i: (i, 0)), _const((1, nf * d)), vec, vec],
        out_shape=[sd((r, d), F32), sd((r, nf * d), BF16), sd((1, nf * d), F32), sd((1, d), F32), sd((1, d), F32)],
        compiler_params=_cp(("arbitrary",), 56),
    )(h1b, dr2, r1, g1, wup_g, wdn_bf, bup)


def _s5_block_mats(bb_re_t, bb_im_t, c_re, c_im, ap_re, ap_im):
    ng = c_re.shape[0]
    gl = ng // S5_KCH
    eye = jnp.eye(gl, dtype=F32)

    def bmat(bt):
        bb = jnp.transpose(bt, (1, 0, 2)).reshape(S5_KCH, gl, S5_GROUP, S5_STATE)
        return jnp.einsum("kghp,gj->kghjp", bb, eye).reshape(S5_KCH, gl * S5_GROUP, gl * S5_STATE)

    def cmat(c):
        cc = c.reshape(S5_KCH, gl, S5_GROUP, S5_STATE)
        return jnp.einsum("kghp,gj->kjpgh", cc, eye).reshape(S5_KCH, gl * S5_STATE, gl * S5_GROUP)

    def pw(a):
        return jnp.transpose(a.reshape(8, S5_KCH, gl * S5_STATE), (1, 0, 2))

    bk = jnp.concatenate([bmat(bb_re_t), bmat(bb_im_t)], axis=-1)
    apow = jnp.concatenate([pw(ap_re), pw(ap_im)], axis=-1)
    return _bf(bk), _bf(cmat(c_re)), _bf(cmat(c_im)), apow


def _s5_block_grads(dbk, dcre, dcim, da):
    gl = dbk.shape[1] // S5_GROUP
    ng = gl * S5_KCH
    eye = jnp.eye(gl, dtype=F32)
    hw = gl * S5_STATE

    def bpart(x):
        x = x.reshape(S5_KCH, gl, S5_GROUP, gl, S5_STATE)
        x = jnp.einsum("kghjp,gj->kghp", x, eye).reshape(ng, S5_GROUP, S5_STATE)
        return jnp.transpose(x, (1, 0, 2))

    def cpart(x):
        x = x.reshape(S5_KCH, gl, S5_STATE, gl, S5_GROUP)
        return jnp.einsum("kjpgh,gj->kghp", x, eye).reshape(ng, S5_GROUP, S5_STATE)

    return (bpart(dbk[..., :hw]), bpart(dbk[..., hw:]), cpart(dcre), cpart(dcim),
            da[:, 0, :hw].reshape(ng, S5_STATE), da[:, 0, hw:].reshape(ng, S5_STATE))


def _tie(a, tok):
    return a if tok is None else a + tok[0, 0]


def _local_step(x, tgt, w, early=None, late=None, ready=None):
    ready = ready or (lambda names, g: None)
    bsz, seq, d = x.shape
    lp = PAD + N_META + seq
    r = bsz * lp
    tgtp = jnp.concatenate([jnp.zeros((bsz, PAD + N_META, d), F32), tgt], axis=1).reshape(r, d)

    h0, h0b = _ln0_fwd(x, w["meta_tokens"], w["ln0_g"], w["ln0_b"])
    if early is not None:
        w = {**w, **early((h0, tgtp))}
    p, pg = _inproj(h0b, w["w_in"], w["b_in"], lp)
    p3 = p.reshape(bsz, lp, NP)
    pg3 = pg.reshape(bsz, lp, 128)

    b_re_t = jnp.transpose(w["s5_b_re"], (2, 0, 1))
    b_im_t = jnp.transpose(w["s5_b_im"], (2, 0, 1))
    ap_re, ap_im, bb_re_t, bb_im_t = _s5_prep(w["s5_lambda_re"], w["s5_lambda_im"], w["s5_log_dt"], b_re_t, b_im_t)
    bk, cre, cim, apow = _s5_block_mats(bb_re_t, bb_im_t, w["s5_c_re"], w["s5_c_im"], ap_re, ap_im)
    y_s5, xs = _s5_fwd(p3, bk, cre, cim, apow, w["s5_d"])
    sw = y_s5.shape[-1]
    if late is not None:
        w = {**w, **late(y_s5)}
    gy, z, ys5g = _glu_fwd(y_s5.reshape(r, sw), w["s5_w_glu"], lp)

    pre3, qk3 = _conv_fwd(p3, w["qk_conv_w"], w["qk_conv_b"])
    hm3, cs, ns, ms = _mlstm_fwd(qk3, p3, pg3)
    hm = hm3.reshape(r, d)
    ymin, ym, mix, r1 = _mix_fwd(hm, p, ys5g, h0, w["m_norm_g"], w["m_w_out"], w["w_o"], lp)
    dr2, h1b, act, loss, dg2, db2 = _mlp_fwd(r1, tgtp, w["ln1_g"], w["ln1_b"], w["w_up"], w["w_down"], w["b_up"],
                                             w["ln2_g"], w["ln2_b"], lp)

    g = {"ln2_g": dg2, "ln2_b": db2}
    dr1, dup, g["b_up"], g["ln1_g"], g["ln1_b"] = _mlp_bwd(h1b, dr2, r1, w["ln1_g"], w["w_up"], w["w_down"], w["b_up"], lp)
    g["w_down"] = _mm_tn(act, dr2, name="dw_down")
    g["w_up"] = _mm_tn(h1b, dup, name="dw_up", split=w["w_up"].shape[0])
    tok = ready(("w_down", "w_up"), g)
    dp, dys5g, dym, dhm, g["m_norm_g"] = _mix_bwd(
        dr1, w["w_o"], w["m_w_out"], p, ys5g, ym, hm, _tie(w["m_norm_g"], tok), lp)
    g["w_o"] = _mm_tn(mix, dr1, name="dw_o")
    g["m_w_out"] = _mm_tn(ymin, dym, name="dw_mout")

    dp3 = dp.reshape(bsz, lp, NP)
    dp3, dqk3, dgate = _mlstm_bwd(dp3, qk3, p3, pg3, dhm.reshape(bsz, lp, d), cs, ns, ms)
    dp3, g["qk_conv_w"], g["qk_conv_b"] = _conv_bwd(dp3, p3, dqk3, pre3, w["qk_conv_w"])
    dz, dys5 = _glu_bwd(dys5g, z, y_s5.reshape(r, sw), w["s5_w_glu"], lp)
    g["s5_w_glu"] = _mm_tn(gy, dz, name="dw_glu", split=w["s5_w_glu"].shape[0])
    tok = ready(("s5_w_glu", "m_w_out", "w_o"), g)
    apow_rev = jnp.flip(apow, axis=1)
    dp3, dbk, dcre, dcim, da, g["s5_d"] = _s5_bwd(dp3, p3, dys5.reshape(bsz, lp, sw), xs, bk, cre, cim, apow_rev,
                                                 _tie(w["s5_d"], tok))
    dbb_re_t, dbb_im_t, g["s5_c_re"], g["s5_c_im"], da_re, da_im = _s5_block_grads(dbk, dcre, dcim, da)
    g["s5_lambda_re"], g["s5_lambda_im"], g["s5_log_dt"], gb_re_t, gb_im_t = _s5_prep_bwd(
        w["s5_lambda_re"], w["s5_lambda_im"], w["s5_log_dt"], b_re_t, b_im_t, da_re, da_im, dbb_re_t, dbb_im_t)
    g["s5_b_re"] = jnp.transpose(gb_re_t, (1, 2, 0))
    g["s5_b_im"] = jnp.transpose(gb_im_t, (1, 2, 0))

    dp3 = lax.dynamic_update_slice(dp3, dgate, (0, 0, G_OFF))
    dp = dp3.reshape(r, NP)
    g["w_in"], g["b_in"] = _mm_tn(h0b, dp, name="dw_in", colsum=True)
    tok = ready(("w_in",), g)
    dpw = _mm_nt(dp, w["w_in"], lp, name="dh0", dep=tok)
    grad_x, g["ln0_g"], g["ln0_b"], g["meta_tokens"] = _ln0_bwd(x, w["meta_tokens"], dr1, dpw, w["ln0_g"])
    return loss, grad_x, g


_ANY = pl.BlockSpec(memory_space=pl.ANY)
_MESH = pl.DeviceIdType.MESH


def _place():
    return lax.axis_index("x"), lax.axis_index("y"), lax.axis_index("c")


def _gather_chips(shards):
    n = len(shards)

    def body(*refs):
        ins, outs = refs[:n], refs[n:2 * n]
        send, recv, loc = refs[2 * n:]
        x, y, c = _place()
        me = 2 * x + y
        peers = [(1 - x, y), (x, 1 - y), (1 - x, 1 - y)]

        def rc(a, k, slot):
            px, py = peers[k]
            return pltpu.make_async_remote_copy(src_ref=ins[a], dst_ref=outs[a].at[slot], send_sem=send.at[a, k],
                                                recv_sem=recv.at[a, k], device_id=(px, py, c), device_id_type=_MESH)

        own = [pltpu.make_async_copy(ins[a], outs[a].at[me], loc.at[a]) for a in range(n)]
        for cp in own:
            cp.start()
        out = [rc(a, k, me) for a in range(n) for k in range(3)]
        for cp in out:
            cp.start()
        for a in range(n):
            for k in range(3):
                rc(a, k, 2 * peers[k][0] + peers[k][1]).wait_recv()
        for cp in out:
            cp.wait_send()
        for cp in own:
            cp.wait()

    return _pcall(
        body, name="gather_chips", in_specs=[_ANY] * n, out_specs=[_ANY] * n,
        out_shape=[jax.ShapeDtypeStruct((4,) + s.shape, s.dtype) for s in shards],
        scratch_shapes=[pltpu.SemaphoreType.DMA((n, 3)), pltpu.SemaphoreType.DMA((n, 3)), pltpu.SemaphoreType.DMA((n,))],
    )(*shards)


_HBM = pl.BlockSpec(memory_space=pltpu.HBM)
_SEM = pl.BlockSpec(memory_space=pltpu.SEMAPHORE)
_EFFECT = pltpu.SideEffectType.DATAFLOW_SIDE_EFFECTING


def _xchg_copies(srcs, lands, send, recv, scatter):
    x, y, c = _place()
    me = 2 * x + y
    peers = [(1 - x, y), (x, 1 - y), (1 - x, 1 - y)]
    out = []
    for a in range(len(srcs)):
        for k, (px, py) in enumerate(peers):
            src = srcs[a].at[2 * px + py] if scatter else srcs[a]
            dst = lands[a].at[k] if scatter else lands[a].at[me]
            out.append(pltpu.make_async_remote_copy(src_ref=src, dst_ref=dst, send_sem=send.at[3 * a + k],
                                                    recv_sem=recv.at[3 * a + k], device_id=(px, py, c),
                                                    device_id_type=_MESH))
    return out


def _xchg_start(srcs, lands, *, name, scatter, dep=None):
    n = len(srcs)
    deps = [] if dep is None else [dep]
    nd = len(deps)

    def body(*refs):
        send, recv = refs[2 * n + nd], refs[2 * n + nd + 1]
        for cp in _xchg_copies(refs[:n], refs[n:2 * n], send, recv, scatter):
            cp.start()
        refs[-1][...] = jnp.zeros_like(refs[-1])

    hbm = lambda a: pltpu.HBM(a.shape, a.dtype)
    con = lambda a: pltpu.with_memory_space_constraint(a, pltpu.HBM)
    res = _pcall(
        body, name=name, in_specs=[_HBM] * (2 * n) + [_ANY] * nd,
        out_specs=[_SEM, _SEM] + [_HBM] * (2 * n) + [pl.BlockSpec(memory_space=pltpu.VMEM)],
        out_shape=[pltpu.SemaphoreType.DMA((3 * n,)), pltpu.SemaphoreType.DMA((3 * n,))]
        + [hbm(a) for a in srcs] + [hbm(a) for a in lands] + [jax.ShapeDtypeStruct((8, 128), F32)],
        input_output_aliases={i: 2 + i for i in range(2 * n)},
        compiler_params=pltpu.CompilerParams(has_side_effects=_EFFECT),
    )(*[con(a) for a in srcs], *[con(a) for a in lands], *deps)
    return res[0], res[1], list(res[2:2 + n]), list(res[2 + n:2 + 2 * n]), res[-1]


def _xchg_wait(send, recv, srcs, lands, after, *, name, scatter):
    n = len(srcs)
    afters = list(after) if isinstance(after, (list, tuple)) else [after]

    def body(*refs):
        s_ref, r_ref = refs[2 * n], refs[2 * n + 1]
        for cp in _xchg_copies(refs[:n], refs[n:2 * n], s_ref, r_ref, scatter):
            cp.wait_send()
            cp.wait_recv()

    hbm = lambda a: pltpu.HBM(a.shape, a.dtype)
    res = _pcall(
        body, name=name, in_specs=[_HBM] * (2 * n) + [_SEM, _SEM] + [_ANY] * len(afters),
        out_specs=[_HBM] * (2 * n),
        out_shape=[hbm(a) for a in srcs] + [hbm(a) for a in lands],
        input_output_aliases={i: i for i in range(2 * n)},
        compiler_params=pltpu.CompilerParams(has_side_effects=_EFFECT),
    )(*srcs, *lands, send, recv, *afters)
    return list(res[:n]), list(res[n:])


def _swap_cores(arrs, name="swap_cores"):
    n = len(arrs)

    def body(*refs):
        ins, outs = refs[:n], refs[n:2 * n]
        send, recv = refs[2 * n:]
        x, y, c = _place()
        cps = [pltpu.make_async_remote_copy(src_ref=ins[a], dst_ref=outs[a], send_sem=send.at[a], recv_sem=recv.at[a],
                                            device_id=(x, y, 1 - c), device_id_type=_MESH) for a in range(n)]
        for cp in cps:
            cp.start()
        for cp in cps:
            cp.wait_recv()
        for cp in cps:
            cp.wait_send()

    return _pcall(
        body, name=name, in_specs=[_ANY] * n, out_specs=[_ANY] * n,
        out_shape=[jax.ShapeDtypeStruct(s.shape, s.dtype) for s in arrs],
        scratch_shapes=[pltpu.SemaphoreType.DMA((n,)), pltpu.SemaphoreType.DMA((n,))],
    )(*arrs)


def _allreduce_small(v, dep=None):
    rows = v.shape[0]
    half = rows // 2
    assert half % 8 == 0 and 2 * half == rows
    deps = [] if dep is None else [dep]

    def body(v_ref, *rest):
        out_ref, sib_ref, pair_ref, slots_ref, send, recv = rest[len(deps):]
        x, y, c = _place()
        chip = 2 * x + y
        sibling = (x, y, 1 - c)
        peers = [(1 - x, y), (x, 1 - y), (1 - x, 1 - y)]
        mine = pl.ds(pl.multiple_of(c * half, 8), half)

        first = pltpu.make_async_remote_copy(src_ref=v_ref, dst_ref=sib_ref, send_sem=send.at[0], recv_sem=recv.at[0],
                                             device_id=sibling, device_id_type=_MESH)
        first.start()
        first.wait_recv()
        pair_ref[...] = v_ref[...] + sib_ref[...]
        slots_ref[chip] = pair_ref[mine, :]
        cross = [pltpu.make_async_remote_copy(src_ref=pair_ref.at[mine], dst_ref=slots_ref.at[chip],
                                              send_sem=send.at[1 + k], recv_sem=recv.at[1 + k],
                                              device_id=(px, py, c), device_id_type=_MESH)
                 for k, (px, py) in enumerate(peers)]
        for cp in cross:
            cp.start()
        for cp in cross:
            cp.wait_recv()
        out_ref[mine, :] = ((slots_ref[0] + slots_ref[1]) + slots_ref[2]) + slots_ref[3]
        last = pltpu.make_async_remote_copy(src_ref=out_ref.at[mine], dst_ref=out_ref.at[mine], send_sem=send.at[4],
                                            recv_sem=recv.at[4], device_id=sibling, device_id_type=_MESH)
        last.start()
        last.wait_recv()
        first.wait_send()
        for cp in cross:
            cp.wait_send()
        last.wait_send()

    vm = pl.BlockSpec(memory_space=pltpu.VMEM)
    return _pcall(
        body, name="allreduce_small", in_specs=[vm] + [_ANY] * len(deps), out_specs=vm,
        out_shape=jax.ShapeDtypeStruct((rows, 128), F32),
        scratch_shapes=[pltpu.VMEM((rows, 128), F32), pltpu.VMEM((rows, 128), F32), pltpu.VMEM((4, half, 128), F32),
                        pltpu.SemaphoreType.DMA((5,)), pltpu.SemaphoreType.DMA((5,))],
        compiler_params=_cp(None, 40),
    )(v, *deps)


def _sum_slots(own, land):
    ns, rows, cols = land.shape
    tm = _row_tile(rows, 256, 8)

    def body(own_ref, a_ref, o_ref):
        o_ref[...] = ((own_ref[...] + a_ref[0]) + a_ref[1]) + a_ref[2]

    return _pcall(
        body, name="sum_slots", grid=(rows // tm,),
        in_specs=[pl.BlockSpec((tm, cols), lambda i: (i, 0)), pl.BlockSpec((ns, tm, cols), lambda i: (0, i, 0))],
        out_specs=pl.BlockSpec((tm, cols), lambda i: (i, 0)),
        out_shape=jax.ShapeDtypeStruct((rows, cols), F32),
        compiler_params=_cp(("parallel",), 40),
    )(own, land)


def _adamw(w, m, v, g0, g1=None):
    rows, cols = w.shape[-2:]
    lead = w.ndim == 3
    tm = _row_tile(rows, max(8, (1 << 20) // (4 * cols)), 8)
    c1 = 1.0 - ADAM_B1 ** ADAM_STEP
    c2 = 1.0 - ADAM_B2 ** ADAM_STEP
    two = g1 is not None

    def body(*refs):
        w_ref, m_ref, v_ref, g0_ref = refs[:4]
        g_ref, d_ref, nm_ref, nv_ref = refs[-4:]
        g = g0_ref[...]
        if two:
            g = g + refs[4][...]
        nm = ADAM_B1 * m_ref[...] + (1.0 - ADAM_B1) * g
        nv = ADAM_B2 * v_ref[...] + (1.0 - ADAM_B2) * (g * g)
        g_ref[...] = g
        nm_ref[...] = nm
        nv_ref[...] = nv
        d_ref[...] = -ADAM_LR * ((nm / c1) / (jnp.sqrt(nv / c2) + ADAM_EPS) + ADAM_WD * w_ref[...])

    blk = pl.BlockSpec((tm, cols), lambda i: (i, 0))
    wblk = pl.BlockSpec((None, tm, cols), lambda i: (0, i, 0)) if lead else blk
    ins = [w, m, v, g0] + ([g1] if two else [])
    return _pcall(
        body, name="adamw", grid=(rows // tm,), in_specs=[wblk] * 3 + [blk] * (len(ins) - 3), out_specs=[wblk] * 4,
        out_shape=[jax.ShapeDtypeStruct(w.shape, F32)] * 4,
        compiler_params=_cp(("parallel",), 40),
    )(*ins)


_BIG = ("w_in", "s5_w_glu", "m_w_out", "w_o", "w_up", "w_down")
_SMALL = ("ln0_g", "ln0_b", "b_in", "qk_conv_b", "s5_lambda_re", "s5_lambda_im", "s5_log_dt", "s5_b_re", "s5_b_im",
          "s5_c_re", "s5_c_im", "s5_d", "m_norm_g", "ln1_g", "ln1_b", "b_up", "ln2_g", "ln2_b")
_SMALL_SHARDED = ("meta_tokens", "qk_conv_w")
_ORDER = ("meta_tokens", "ln0_g", "ln0_b", "w_in", "b_in", "qk_conv_w", "qk_conv_b", "s5_lambda_re", "s5_lambda_im",
          "s5_log_dt", "s5_b_re", "s5_b_im", "s5_c_re", "s5_c_im", "s5_d", "s5_w_glu", "m_norm_g", "m_w_out", "w_o",
          "ln1_g", "ln1_b", "w_up", "b_up", "w_down", "ln2_g", "ln2_b")


def _pack(arrs):
    flat = jnp.concatenate([a.reshape(-1) for a in arrs])
    n = flat.shape[0]
    rows = -(-n // 2048) * 16
    return jnp.pad(flat, (0, rows * 128 - n)).reshape(rows, 128)


def _unpack(packed, shapes):
    flat = packed.reshape(-1)
    out, off = [], 0
    for s in shapes:
        n = math.prod(s)
        out.append(flat[off:off + n].reshape(s))
        off += n
    return out


def kernel(x, meta_tokens, ln0_g, ln0_b, w_in, b_in, qk_conv_w, qk_conv_b, s5_lambda_re, s5_lambda_im, s5_log_dt, s5_b_re, s5_b_im, s5_c_re, s5_c_im, s5_d, s5_w_glu, m_norm_g, m_w_out, w_o, ln1_g, ln1_b, w_up, b_up, w_down, ln2_g, ln2_b, loss_target, m_meta_tokens, m_ln0_g, m_ln0_b, m_w_in, m_b_in, m_qk_conv_w, m_qk_conv_b, m_s5_lambda_re, m_s5_lambda_im, m_s5_log_dt, m_s5_b_re, m_s5_b_im, m_s5_c_re, m_s5_c_im, m_s5_d, m_s5_w_glu, m_m_norm_g, m_m_w_out, m_w_o, m_ln1_g, m_ln1_b, m_w_up, m_b_up, m_w_down, m_ln2_g, m_ln2_b, v_meta_tokens, v_ln0_g, v_ln0_b, v_w_in, v_b_in, v_qk_conv_w, v_qk_conv_b, v_s5_lambda_re, v_s5_lambda_im, v_s5_log_dt, v_s5_b_re, v_s5_b_im, v_s5_c_re, v_s5_c_im, v_s5_d, v_s5_w_glu, v_m_norm_g, v_m_w_out, v_w_o, v_ln1_g, v_ln1_b, v_w_up, v_b_up, v_w_down, v_ln2_g, v_ln2_b):
    wts = dict(meta_tokens=meta_tokens, ln0_g=ln0_g, ln0_b=ln0_b, w_in=w_in, b_in=b_in, qk_conv_w=qk_conv_w,
               qk_conv_b=qk_conv_b, s5_lambda_re=s5_lambda_re, s5_lambda_im=s5_lambda_im, s5_log_dt=s5_log_dt,
               s5_b_re=s5_b_re, s5_b_im=s5_b_im, s5_c_re=s5_c_re, s5_c_im=s5_c_im, s5_d=s5_d, s5_w_glu=s5_w_glu,
               m_norm_g=m_norm_g, m_w_out=m_w_out, w_o=w_o, ln1_g=ln1_g, ln1_b=ln1_b, w_up=w_up, b_up=b_up,
               w_down=w_down, ln2_g=ln2_g, ln2_b=ln2_b)
    mom = dict(meta_tokens=m_meta_tokens, ln0_g=m_ln0_g, ln0_b=m_ln0_b, w_in=m_w_in, b_in=m_b_in, qk_conv_w=m_qk_conv_w,
               qk_conv_b=m_qk_conv_b, s5_lambda_re=m_s5_lambda_re, s5_lambda_im=m_s5_lambda_im, s5_log_dt=m_s5_log_dt,
               s5_b_re=m_s5_b_re, s5_b_im=m_s5_b_im, s5_c_re=m_s5_c_re, s5_c_im=m_s5_c_im, s5_d=m_s5_d,
               s5_w_glu=m_s5_w_glu, m_norm_g=m_m_norm_g, m_w_out=m_m_w_out, w_o=m_w_o, ln1_g=m_ln1_g, ln1_b=m_ln1_b,
               w_up=m_w_up, b_up=m_b_up, w_down=m_w_down, ln2_g=m_ln2_g, ln2_b=m_ln2_b)
    var = dict(meta_tokens=v_meta_tokens, ln0_g=v_ln0_g, ln0_b=v_ln0_b, w_in=v_w_in, b_in=v_b_in, qk_conv_w=v_qk_conv_w,
               qk_conv_b=v_qk_conv_b, s5_lambda_re=v_s5_lambda_re, s5_lambda_im=v_s5_lambda_im, s5_log_dt=v_s5_log_dt,
               s5_b_re=v_s5_b_re, s5_b_im=v_s5_b_im, s5_c_re=v_s5_c_re, s5_c_im=v_s5_c_im, s5_d=v_s5_d,
               s5_w_glu=v_s5_w_glu, m_norm_g=v_m_norm_g, m_w_out=v_m_w_out, w_o=v_w_o, ln1_g=v_ln1_g, ln1_b=v_ln1_b,
               w_up=v_w_up, b_up=v_b_up, w_down=v_w_down, ln2_g=v_ln2_g, ln2_b=v_ln2_b)
    d = x.shape[-1]
    chip = 2 * lax.axis_index("x") + lax.axis_index("y")

    gw = dict(zip(_SMALL_SHARDED, _gather_chips([meta_tokens, qk_conv_w[0]])))
    own_w_in = _bf(w_in[0])
    fsend, frecv, fsrc, fland, ftok = _xchg_start([own_w_in], [lax.empty((4,) + own_w_in.shape, BF16)],
                                                  name="gather_w_in_start", scatter=False, dep=gw["qk_conv_w"])
    late_names = tuple(n for n in _BIG if n != "w_in")
    cat = lambda a: jnp.transpose(a, (1, 0, 2)).reshape(a.shape[1], 4 * a.shape[2])
    w = dict(
        meta_tokens=cat(gw["meta_tokens"]), ln0_g=ln0_g[None], ln0_b=_tie(ln0_b[None], ftok),
        qk_conv_w=cat(gw["qk_conv_w"]), qk_conv_b=qk_conv_b,
        s5_lambda_re=s5_lambda_re[0], s5_lambda_im=s5_lambda_im[0], s5_log_dt=s5_log_dt[0][:, None],
        s5_b_re=s5_b_re[0], s5_b_im=s5_b_im[0], s5_c_re=s5_c_re[0], s5_c_im=s5_c_im[0], s5_d=s5_d,
        m_norm_g=m_norm_g, ln1_g=ln1_g, ln1_b=ln1_b, b_up=b_up, ln2_g=ln2_g, ln2_b=ln2_b)
    in_flight = {}

    def place_own(src, land):
        return lax.dynamic_update_slice(land, src[None], (chip,) + (0,) * src.ndim)

    small_names = _SMALL + _SMALL_SHARDED
    small_wmv = [_pack([dct[n] for n in small_names]) for dct in (wts, mom, var)]

    def early(after):
        src, land = _xchg_wait(fsend, frecv, fsrc, fland, tuple(after) + tuple(small_wmv), name="gather_w_in_wait",
                               scatter=False)
        late_src = [_bf(wts[n][0]) for n in late_names]
        st = _xchg_start(late_src, [lax.empty((4,) + a.shape, a.dtype) for a in late_src], name="gather_late_start",
                         scatter=False, dep=src[0])
        in_flight["late"] = st[:4]
        return dict(w_in=_w_in_from_slots(place_own(src[0], land[0]), IN_CHUNK), b_in=_tie(_to_pad_cols(b_in), st[4]))

    def late(after):
        src, land = _xchg_wait(*in_flight["late"], after, name="gather_late_wait", scatter=False)
        full = {n: place_own(s, ld) for n, s, ld in zip(late_names, src, land)}
        return dict(s5_w_glu=full["s5_w_glu"], m_w_out=full["m_w_out"].reshape(d, d), w_o=full["w_o"].reshape(d, d),
                    w_up=full["w_up"], w_down=full["w_down"].reshape(4 * d, d))

    flying = []

    def ready(names, g):
        parts = dict(
            w_in=lambda: _slots_from_w_in(g["w_in"][0]), s5_w_glu=lambda: g["s5_w_glu"],
            m_w_out=lambda: g["m_w_out"].reshape(4, d // 4, d), w_o=lambda: g["w_o"].reshape(4, d // 4, d),
            w_up=lambda: g["w_up"], w_down=lambda: g["w_down"].reshape(4, d, d))
        src = [parts[n]() for n in names]
        land = [lax.empty((3,) + a.shape[1:], a.dtype) for a in src]
        st = _xchg_start(src, land, name="scatter_" + names[0] + "_start", scatter=True)
        flying.append((names,) + st[:4])
        return st[4]

    loss, grad_x, g = _local_step(x, loss_target, w, early, late, ready)
    g["b_in"] = _from_pad_cols(g["b_in"])

    res = {}

    def flat(a):
        return jnp.swapaxes(a, -1, -2).reshape(a.shape[:-2] + (-1, 128))

    def unflat(y, shape):
        return jnp.swapaxes(y.reshape(shape[:-2] + (shape[-1], shape[-2])), -1, -2)

    def finish(groups, after, tag):
        mine = {}
        for names, send, recv, src, land in groups:
            src, land = _xchg_wait(send, recv, src, land, after, name="scatter_" + names[0] + "_wait", scatter=True)
            for n, s, ld in zip(names, src, land):
                mine[n] = _sum_slots(lax.dynamic_index_in_dim(s, chip, 0, keepdims=False), ld)
        theirs = _swap_cores(list(mine.values()), name="swap_cores_" + tag)
        for n, t in zip(mine, theirs):
            if n == "w_in":
                res[n] = [unflat(r, wts[n].shape) for r in _adamw(flat(wts[n]), flat(mom[n]), flat(var[n]),
                                                                  flat(mine[n]), flat(t))]
            else:
                res[n] = _adamw(wts[n], mom[n], var[n], mine[n], t)

    finish(flying[:-1], g["ln0_g"], "a")

    small_shapes = [(1, 128)] + [wts[n].shape for n in _SMALL] + [g[n].shape for n in _SMALL_SHARDED]
    packed = _pack([loss] + [g[n] for n in _SMALL] + [g[n] for n in _SMALL_SHARDED])
    tot = _unpack(_allreduce_small(packed, dep=res["w_o"][3]), small_shapes)
    loss_out = tot[0][0, 0]
    gsm = dict(zip(_SMALL + _SMALL_SHARDED, tot[1:]))
    for n in _SMALL_SHARDED:
        cols = wts[n].shape[-1]
        gsm[n] = lax.dynamic_slice_in_dim(gsm[n], chip * cols, cols, axis=1).reshape(wts[n].shape)

    names = small_names
    shapes = [wts[n].shape for n in names]
    small_out = _adamw(*small_wmv, _pack([gsm[n] for n in names]))
    small_res = [_unpack(r, shapes) for r in small_out]
    for j, n in enumerate(names):
        res[n] = [small_res[q][j] for q in range(4)]
    finish(flying[-1:], small_out[0], "b")

    return (loss_out, grad_x, *[res[n][0] for n in _ORDER], *[res[n][1] for n in _ORDER],
            *[res[n][2] for n in _ORDER], *[res[n][3] for n in _ORDER])
```

```python
import functools
import math

import jax
import jax.numpy as jnp
from jax import lax
from jax.experimental import pallas as pl
from jax.experimental.pallas import tpu as pltpu

F32 = jnp.float32
BF16 = jnp.bfloat16
HI = lax.Precision.HIGHEST

N_META = 16
M_HEADS = 4
M_CHUNK = 128
PAD = M_CHUNK - N_META
CONV_W = 4
HALO_ROWS = 16
S5_GROUP = 16
S5_STATE = 64
S5_KCH = 4
LN_EPS = 1e-5
ALPHA = 2.0 ** 0.25
NEG = -1e30
ADAM_LR, ADAM_B1, ADAM_B2, ADAM_EPS, ADAM_WD, ADAM_STEP = 0.001, 0.9, 0.999, 1e-08, 0.01, 10

O_OFF, GS_OFF, GM_OFF, V_OFF, Q_OFF, K_OFF, U_OFF, G_OFF, NP = 0, 1024, 2048, 3072, 4096, 4608, 5120, 5632, 5760

NN = ((1,), (0,))
NT = ((1,), (1,))
TN = ((0,), (0,))


def _dot(a, b, dims=NN, prec=None):
    return lax.dot_general(a, b, (dims, ((), ())), preferred_element_type=F32, precision=prec)


def _bf(x):
    return x.astype(BF16)


def _sig(x):
    return 0.5 * jnp.tanh(0.5 * x) + 0.5


def _pcall(body, **kw):
    return pl.pallas_call(body, **kw)


def _cp(sem=None, vmem_mb=None):
    kw = {}
    if sem is not None:
        kw["dimension_semantics"] = sem
    if vmem_mb is not None:
        kw["vmem_limit_bytes"] = vmem_mb << 20
    return pltpu.CompilerParams(**kw)


def _row_tile(n, want, mult=16):
    best = None
    for t in range(mult, want + 1, mult):
        if n % t == 0:
            best = t
    assert best is not None, (n, want)
    return best


def _resident(shape):
    nd = len(shape)
    return pl.BlockSpec(shape, lambda *_: (0,) * nd, pipeline_mode=pl.Buffered(1))


def _const(shape):
    nd = len(shape)
    return pl.BlockSpec(shape, lambda *_: (0,) * nd)


def _ln_fwd(x, g, b):
    mu = jnp.mean(x, axis=-1, keepdims=True)
    xc = x - mu
    var = jnp.mean(xc * xc, axis=-1, keepdims=True)
    rstd = lax.rsqrt(var + LN_EPS)
    xhat = xc * rstd
    return xhat * g + b, xhat, rstd


def _ln_bwd(dy, xhat, rstd, g):
    dxh = dy * g
    m1 = jnp.mean(dxh, axis=-1, keepdims=True)
    m2 = jnp.mean(dxh * xhat, axis=-1, keepdims=True)
    return rstd * (dxh - m1 - xhat * m2)


def _colsum(x):
    return jnp.sum(x, axis=0, keepdims=True)


def _to_pad_cols(w):
    u, q, k, v, o, gi, gf, gs, gm = (w[..., 0:512], w[..., 512:1024], w[..., 1024:1536], w[..., 1536:2560],
                                     w[..., 2560:3584], w[..., 3584:3588], w[..., 3588:3592], w[..., 3592:4616],
                                     w[..., 4616:5640])
    z = jnp.zeros(w.shape[:-1] + (NP - G_OFF - 8,), w.dtype)
    return jnp.concatenate([o, gs, gm, v, q, k, u, gi, gf, z], axis=-1)


def _from_pad_cols(w):
    o, gs, gm, v, q, k, u = (w[..., O_OFF:GS_OFF], w[..., GS_OFF:GM_OFF], w[..., GM_OFF:V_OFF], w[..., V_OFF:Q_OFF],
                             w[..., Q_OFF:K_OFF], w[..., K_OFF:U_OFF], w[..., U_OFF:G_OFF])
    gi, gf = w[..., G_OFF:G_OFF + 4], w[..., G_OFF + 4:G_OFF + 8]
    return jnp.concatenate([u, q, k, v, o, gi, gf, gs, gm], axis=-1)


_IN_REF = (("u", 512), ("q", 512), ("k", 512), ("v", 1024), ("o", 1024), ("i", 4), ("f", 4), ("gs", 1024), ("gm", 1024))
_IN_PAD = (("o", O_OFF), ("gs", GS_OFF), ("gm", GM_OFF), ("v", V_OFF), ("q", Q_OFF), ("k", K_OFF), ("u", U_OFF),
           ("i", G_OFF), ("f", G_OFF + 4))


def _in_ref_ranges():
    out, off = {}, 0
    for n, s in _IN_REF:
        out[n] = (off, off + s)
        off += s
    return out, off


def _w_in_from_slots(g, chunk=None):
    rng, total = _in_ref_ranges()
    width = total // g.shape[0]
    cols = []
    for n, _ in _IN_PAD:
        a, b = rng[n]
        while a < b:
            s = a // width
            e = min(b, (s + 1) * width)
            cols.append(g[s][:, a - s * width:e - s * width])
            a = e
    cols.append(jnp.zeros((g.shape[1], NP - G_OFF - 8), g.dtype))
    if chunk is None:
        return jnp.concatenate(cols, axis=1)
    chunks, cur, room = [], [], chunk
    for c in cols:
        while c.shape[1] > 0:
            take = min(room, c.shape[1])
            cur.append(c[:, :take])
            c, room = c[:, take:], room - take
            if room == 0:
                chunks.append(jnp.concatenate(cur, axis=1))
                cur, room = [], chunk
    assert not cur
    return jnp.stack(chunks, axis=0)


def _slots_from_w_in(wp, nslot=4):
    rng, total = _in_ref_ranges()
    width = total // nslot
    pad_off = dict(_IN_PAD)
    slots = []
    for s in range(nslot):
        lo, hi = s * width, (s + 1) * width
        cols = []
        for n, _ in _IN_REF:
            a, b = rng[n]
            x0, x1 = max(a, lo), min(b, hi)
            if x0 < x1:
                cols.append(wp[:, pad_off[n] + x0 - a:pad_off[n] + x1 - a])
        slots.append(jnp.concatenate(cols, axis=1))
    return jnp.stack(slots, axis=0)


HEAD = PAD + N_META


def _ln0_in(j, x_ref, meta_ref):
    first = jnp.concatenate([jnp.zeros((PAD, meta_ref.shape[1]), F32), meta_ref[...]], axis=0)
    return jnp.where(j == 0, first[None], x_ref[...])


def _ln0_fwd(x, meta, g, b):
    bsz, seq, d = x.shape
    nb = seq // HEAD + 1

    def body(x_ref, m_ref, g_ref, b_ref, o_ref, ob_ref):
        y, _, _ = _ln_fwd(_ln0_in(pl.program_id(0), x_ref, m_ref), g_ref[...], b_ref[...])
        o_ref[...] = y
        ob_ref[...] = _bf(y)

    row = pl.BlockSpec((bsz, HEAD, d), lambda j: (0, j, 0))
    h0, h0b = _pcall(
        body, name="ln0_fwd", grid=(nb,),
        in_specs=[pl.BlockSpec((bsz, HEAD, d), lambda j: (0, jnp.maximum(j - 1, 0), 0)), _const((N_META, d)),
                  _const((1, d)), _const((1, d))],
        out_specs=[row, row],
        out_shape=[jax.ShapeDtypeStruct((bsz, nb * HEAD, d), F32), jax.ShapeDtypeStruct((bsz, nb * HEAD, d), BF16)],
        compiler_params=_cp(("arbitrary",)),
    )(x, meta, g, b)
    return h0.reshape(-1, d), h0b.reshape(-1, d)


def _ln0_bwd(x, meta, dr1, dpw, g):
    bsz, seq, d = x.shape
    nb = seq // HEAD + 1

    def body(x_ref, m_ref, a_ref, c_ref, g_ref, o_ref, dg_ref, db_ref, dm_ref):
        j = pl.program_id(0)

        @pl.when(j == 0)
        def _():
            dg_ref[...] = jnp.zeros_like(dg_ref)
            db_ref[...] = jnp.zeros_like(db_ref)
            dm_ref[...] = jnp.zeros_like(dm_ref)

        dy = ALPHA * a_ref[...] + c_ref[...]
        _, xhat, rstd = _ln_fwd(_ln0_in(j, x_ref, m_ref), g_ref[...], 0.0)
        dx = _ln_bwd(dy, xhat, rstd, g_ref[...])
        o_ref[...] = dx
        dg_ref[...] += _colsum((dy * xhat).reshape(bsz * HEAD, d))
        db_ref[...] += _colsum(dy.reshape(bsz * HEAD, d))

        @pl.when(j == 0)
        def _():
            dm_ref[...] += jnp.sum(dx[:, PAD:, :], axis=0)

    row = pl.BlockSpec((bsz, HEAD, d), lambda j: (0, j, 0))
    tok = pl.BlockSpec((bsz, HEAD, d), lambda j: (0, jnp.maximum(j - 1, 0), 0))
    lp = nb * HEAD
    return _pcall(
        body, name="ln0_bwd", grid=(nb,),
        in_specs=[tok, _const((N_META, d)), row, row, _const((1, d))],
        out_specs=[tok, _const((1, d)), _const((1, d)), _const((N_META, d))],
        out_shape=[jax.ShapeDtypeStruct((bsz, seq, d), F32), jax.ShapeDtypeStruct((1, d), F32),
                   jax.ShapeDtypeStruct((1, d), F32), jax.ShapeDtypeStruct((N_META, d), F32)],
        compiler_params=_cp(("arbitrary",)),
    )(x, meta, dr1.reshape(bsz, lp, d), dpw.reshape(bsz, lp, d), g)


IN_CHUNK = 1152


def _chunk_cols(w):
    k, n = w.shape
    return jnp.transpose(w.reshape(k, n // IN_CHUNK, IN_CHUNK), (1, 0, 2))


def _inproj(h0b, w3, bias, lp):
    r, d = h0b.shape
    nj, _, tn = w3.shape
    tm = _row_tile(lp, 1056)
    tps = lp // tm

    def body(a_ref, w_ref, b_ref, o_ref, gate_ref):
        i = pl.program_id(0)
        j = pl.program_id(1)
        acc = _dot(a_ref[...], w_ref[j]) + b_ref[...]
        t = (i % tps) * tm + lax.broadcasted_iota(jnp.int32, (tm, 1), 0)
        acc = jnp.where(t >= PAD, acc, 0.0)
        o_ref[...] = _bf(acc)

        @pl.when(j == nj - 1)
        def _():
            gate_ref[...] = acc[:, tn - 128:]

    return _pcall(
        body, name="inproj", grid=(r // tm, nj),
        in_specs=[pl.BlockSpec((tm, d), lambda i, j: (i, 0)), _resident(w3.shape),
                  pl.BlockSpec((1, tn), lambda i, j: (0, j))],
        out_specs=[pl.BlockSpec((tm, tn), lambda i, j: (i, j)), pl.BlockSpec((tm, 128), lambda i, j: (i, 0))],
        out_shape=[jax.ShapeDtypeStruct((r, nj * tn), BF16), jax.ShapeDtypeStruct((r, 128), F32)],
        compiler_params=_cp(("parallel", "arbitrary"), 48),
    )(h0b, w3, bias)


def _mm_tn(a, b, *, name, split=1, colsum=False, tk_want=2112):
    r, m = a.shape
    n = b.shape[1]
    tk = _row_tile(r, tk_want)
    tm = min(m, 1024)
    ns = n // split
    tn = ns
    for cand in (1024, 1152, 640, 512, 128):
        if ns % cand == 0 and cand <= ns:
            tn = cand
            break
    nb = ns // tn
    nk = r // tk

    def body(a_ref, b_ref, o_ref, *rest):
        acc = rest[-1]
        k = pl.program_id(2)

        @pl.when(k == 0)
        def _():
            acc[...] = jnp.zeros_like(acc)

        bt = b_ref[...]
        acc[...] += _dot(_bf(a_ref[...]), _bf(bt), TN)

        @pl.when(k == nk - 1)
        def _():
            o_ref[...] = acc[...]

        if colsum:
            cs_ref = rest[0]

            @pl.when(k == 0)
            def _():
                cs_ref[...] = jnp.zeros_like(cs_ref)

            cs_ref[...] += _colsum(bt.astype(F32))

    out_specs = [pl.BlockSpec((None, tm, tn), lambda i, j, k: (j // nb, i, j % nb))]
    out_shape = [jax.ShapeDtypeStruct((split, m, ns), F32)]
    if colsum:
        assert m == tm
        out_specs.append(pl.BlockSpec((1, tn), lambda i, j, k: (0, j)))
        out_shape.append(jax.ShapeDtypeStruct((1, n), F32))
    res = _pcall(
        body, name=name, grid=(m // tm, n // tn, nk),
        in_specs=[pl.BlockSpec((tk, tm), lambda i, j, k: (k, i)), pl.BlockSpec((tk, tn), lambda i, j, k: (k, j))],
        out_specs=out_specs, out_shape=out_shape,
        scratch_shapes=[pltpu.VMEM((tm, tn), F32)],
        compiler_params=_cp(("parallel", "parallel", "arbitrary"), 56),
    )(a, b)
    return res if colsum else res[0]


def _mm_nt(a, w3, lp, *, name, dep=None):
    r, kdim = a.shape
    nk, n, tk = w3.shape
    assert nk * tk == kdim
    tm = _row_tile(lp, 1056)
    deps = [] if dep is None else [dep]

    def body(a_ref, w_ref, *rest):
        o_ref, acc = rest[-2:]
        k = pl.program_id(1)

        @pl.when(k == 0)
        def _():
            acc[...] = jnp.zeros_like(acc)

        acc[...] += _dot(_bf(a_ref[...]), w_ref[k], NT)

        @pl.when(k == nk - 1)
        def _():
            o_ref[...] = acc[...]

    return _pcall(
        body, name=name, grid=(r // tm, nk),
        in_specs=[pl.BlockSpec((tm, tk), lambda i, k: (i, k)), _resident(w3.shape)]
        + [_const(dp_.shape) for dp_ in deps],
        out_specs=pl.BlockSpec((tm, n), lambda i, k: (i, 0)),
        out_shape=jax.ShapeDtypeStruct((r, n), F32),
        scratch_shapes=[pltpu.VMEM((tm, n), F32)],
        compiler_params=_cp(("parallel", "arbitrary"), 48),
    )(a, w3, *deps)


def _s5_prep(lam_re, lam_im, log_dt, b_re_t, b_im_t):
    g, p = lam_re.shape
    h = b_re_t.shape[0]

    def body(lr_ref, li_ref, ldt_ref, br_ref, bi_ref, pr_ref, pi_ref, bbr_ref, bbi_ref):
        lr, li = lr_ref[...], li_ref[...]
        dt = jnp.exp(ldt_ref[...])
        e = jnp.exp(lr * dt)
        ar, ai = e * jnp.cos(li * dt), e * jnp.sin(li * dt)
        den = lr * lr + li * li
        cr = ((ar - 1.0) * lr + ai * li) / den
        ci = (ai * lr - (ar - 1.0) * li) / den
        br, bi = br_ref[...], bi_ref[...]
        bbr_ref[...] = cr[None] * br - ci[None] * bi
        bbi_ref[...] = cr[None] * bi + ci[None] * br
        xr, xi = ar, ai
        pr_ref[0] = xr
        pi_ref[0] = xi
        for t in range(1, 8):
            xr, xi = xr * ar - xi * ai, xr * ai + xi * ar
            pr_ref[t] = xr
            pi_ref[t] = xi

    sd = jax.ShapeDtypeStruct
    return _pcall(body, name="s5_prep",
                  out_shape=[sd((8, g, p), F32), sd((8, g, p), F32), sd((h, g, p), F32), sd((h, g, p), F32)])(
        lam_re, lam_im, log_dt, b_re_t, b_im_t)


def _s5_prep_bwd(lam_re, lam_im, log_dt, b_re_t, b_im_t, da_re, da_im, dbb_re_t, dbb_im_t):
    g, p = lam_re.shape
    h = b_re_t.shape[0]

    def body(lr_ref, li_ref, ldt_ref, br_ref, bi_ref, dar_ref, dai_ref, dbr_ref, dbi_ref,
             glr_ref, gli_ref, gdt_ref, gbr_ref, gbi_ref):
        lr, li = lr_ref[...], li_ref[...]
        dt = jnp.exp(ldt_ref[...])
        e = jnp.exp(lr * dt)
        ar, ai = e * jnp.cos(li * dt), e * jnp.sin(li * dt)
        den = lr * lr + li * li
        cr = ((ar - 1.0) * lr + ai * li) / den
        ci = (ai * lr - (ar - 1.0) * li) / den
        br, bi = br_ref[...], bi_ref[...]
        gr, gi = dbr_ref[...], dbi_ref[...]
        gbr_ref[...] = gr * cr[None] + gi * ci[None]
        gbi_ref[...] = gi * cr[None] - gr * ci[None]
        gcr = jnp.sum(gr * br + gi * bi, axis=0)
        gci = jnp.sum(gi * br - gr * bi, axis=0)
        ilr, ili = lr / den, -li / den
        gar = dar_ref[...] + gcr * ilr + gci * ili
        gai = dai_ref[...] + gci * ilr - gcr * ili
        qr, qi = cr * ilr - ci * ili, cr * ili + ci * ilr
        glr = -(gcr * qr + gci * qi)
        gli = -(gci * qr - gcr * qi)
        gzr = gar * ar + gai * ai
        gzi = gai * ar - gar * ai
        glr_ref[...] = glr + gzr * dt
        gli_ref[...] = gli + gzi * dt
        gdt_ref[...] = jnp.sum(gzr * lr + gzi * li, axis=1, keepdims=True) * dt

    sd = jax.ShapeDtypeStruct
    return _pcall(body, name="s5_prep_bwd",
                  out_shape=[sd((g, p), F32), sd((g, p), F32), sd((g, 1), F32), sd((h, g, p), F32), sd((h, g, p), F32)])(
        lam_re, lam_im, log_dt, b_re_t, b_im_t, da_re, da_im, dbb_re_t, dbb_im_t)


def _cmul(xr, xi, yr, yi):
    return xr * yr - xi * yi, xr * yi + xi * yr


def _dot5(a, b, dims=NN):
    return _dot(_bf(a), _bf(b), dims)


def _s5_fwd(p3, bk, cre, cim, apow, dskip):
    bsz, lp, _ = p3.shape
    tt = _row_tile(lp, 528, 8)
    nt = lp // tt
    nblk = tt // 8
    hw = 512

    def body(u_ref, bk_ref, cre_ref, cim_ref, ap_ref, d_ref, y_ref, xs_ref, car_ref):
        t = pl.program_id(2)

        @pl.when(t == 0)
        def _():
            car_ref[...] = jnp.zeros_like(car_ref)

        u = u_ref[...].astype(F32)
        xs_ref[...] = _dot5(u, bk_ref[...])
        ap = ap_ref[...]
        apr, api = ap[:, :hw], ap[:, hw:]
        rows = lax.broadcasted_iota(jnp.int32, (8, hw), 0)
        lev = [(d, jnp.where(rows < d, 0.0, jnp.broadcast_to(apr[d - 1:d, :], (8, hw))),
                jnp.where(rows < d, 0.0, jnp.broadcast_to(api[d - 1:d, :], (8, hw)))) for d in (1, 2, 4)]

        def blk(i, carry):
            cr, ci = carry
            off = pl.multiple_of(i * 8, 8)
            x = xs_ref[pl.ds(off, 8), :]
            xr, xi = x[:, :hw], x[:, hw:]
            for d, lr, li in lev:
                mr, mi = _cmul(pltpu.roll(xr, d, 0), pltpu.roll(xi, d, 0), lr, li)
                xr, xi = xr + mr, xi + mi
            mr, mi = _cmul(apr, api, cr, ci)
            xr, xi = xr + mr, xi + mi
            xs_ref[pl.ds(off, 8), :] = jnp.concatenate([xr, xi], axis=1)
            return xr[7:8, :], xi[7:8, :]

        c0 = car_ref[...]
        cr, ci = lax.fori_loop(0, nblk, blk, (c0[0:1, :hw], c0[0:1, hw:]))
        car_ref[...] = jnp.broadcast_to(jnp.concatenate([cr, ci], axis=1), car_ref.shape)
        xs = xs_ref[...]
        y_ref[...] = (_dot5(xs[:, :hw], cre_ref[...]) - _dot5(xs[:, hw:], cim_ref[...])
                      + d_ref[...] * u)

    ub = U_OFF // 128
    return _pcall(
        body, name="s5_fwd", grid=(S5_KCH, bsz, nt),
        in_specs=[pl.BlockSpec((None, tt, 128), lambda k, b, t: (b, t, ub + k)),
                  pl.BlockSpec((None, 128, 2 * hw), lambda k, b, t: (k, 0, 0)),
                  pl.BlockSpec((None, hw, 128), lambda k, b, t: (k, 0, 0)),
                  pl.BlockSpec((None, hw, 128), lambda k, b, t: (k, 0, 0)),
                  pl.BlockSpec((None, 8, 2 * hw), lambda k, b, t: (k, 0, 0)),
                  pl.BlockSpec((1, 128), lambda k, b, t: (0, k))],
        out_specs=[pl.BlockSpec((None, tt, 128), lambda k, b, t: (b, t, k)),
                   pl.BlockSpec((None, None, tt, 2 * hw), lambda k, b, t: (b, k, t, 0))],
        out_shape=[jax.ShapeDtypeStruct((bsz, lp, S5_KCH * 128), F32),
                   jax.ShapeDtypeStruct((bsz, S5_KCH, lp, 2 * hw), F32)],
        scratch_shapes=[pltpu.VMEM((8, 2 * hw), F32)],
        compiler_params=_cp(("parallel", "parallel", "arbitrary"), 40),
    )(p3, bk, cre, cim, apow, dskip)


def _s5_bwd(dp3, p3, dy3, xs, bk, cre, cim, apow_rev, dskip):
    bsz, lp, _ = p3.shape
    tt = _row_tile(lp, 528, 8)
    nt = lp // tt
    nblk = tt // 8
    hw = 512
    tb = tt // 8

    def body(dp_any, u_ref, dy_ref, xs_ref, halo_ref, bkt_ref, cre_ref, cim_ref, ap_ref, d_ref,
             du_ref, dbk_ref, dcre_ref, dcim_ref, da_ref, dd_ref, g_ref, ext_ref, car_ref):
        b = pl.program_id(1)
        t = pl.program_id(2)
        tidx = nt - 1 - t

        @pl.when(t == 0)
        def _():
            car_ref[...] = jnp.zeros_like(car_ref)

        @pl.when((b == 0) & (t == 0))
        def _():
            dbk_ref[...] = jnp.zeros_like(dbk_ref)
            dcre_ref[...] = jnp.zeros_like(dcre_ref)
            dcim_ref[...] = jnp.zeros_like(dcim_ref)
            da_ref[...] = jnp.zeros_like(da_ref)
            dd_ref[...] = jnp.zeros_like(dd_ref)

        u = u_ref[...].astype(F32)
        dy = dy_ref[...]
        g_ref[:, :hw] = _dot5(dy, cre_ref[...])
        g_ref[:, hw:] = -_dot5(dy, cim_ref[...])
        ap = ap_ref[...]
        apr, api = ap[:, :hw], -ap[:, hw:]
        rows = lax.broadcasted_iota(jnp.int32, (8, hw), 0)
        lev = [(d, jnp.where(rows >= 8 - d, 0.0, jnp.broadcast_to(apr[8 - d:9 - d, :], (8, hw))),
                jnp.where(rows >= 8 - d, 0.0, jnp.broadcast_to(api[8 - d:9 - d, :], (8, hw)))) for d in (1, 2, 4)]

        def blk(i, carry):
            cr, ci = carry
            off = pl.multiple_of((nblk - 1 - i) * 8, 8)
            x = g_ref[pl.ds(off, 8), :]
            xr, xi = x[:, :hw], x[:, hw:]
            for d, lr, li in lev:
                mr, mi = _cmul(pltpu.roll(xr, 8 - d, 0), pltpu.roll(xi, 8 - d, 0), lr, li)
                xr, xi = xr + mr, xi + mi
            mr, mi = _cmul(apr, api, cr, ci)
            xr, xi = xr + mr, xi + mi
            g_ref[pl.ds(off, 8), :] = jnp.concatenate([xr, xi], axis=1)
            return xr[0:1, :], xi[0:1, :]

        c0 = car_ref[...]
        cr, ci = lax.fori_loop(0, nblk, blk, (c0[0:1, :hw], c0[0:1, hw:]))
        car_ref[...] = jnp.broadcast_to(jnp.concatenate([cr, ci], axis=1), car_ref.shape)

        gg = g_ref[...]
        du = _dot5(gg, bkt_ref[...]) + d_ref[...] * dy
        trow = tidx * tt + lax.broadcasted_iota(jnp.int32, (tt, 1), 0)
        du_ref[...] = jnp.where(trow >= PAD, du, 0.0).astype(du_ref.dtype)
        dbk_ref[...] += _dot5(u, gg, TN)
        xsv = xs_ref[...]
        dcre_ref[...] += _dot5(dy, xsv[:, :hw], TN)
        dcim_ref[...] -= _dot5(dy, xsv[:, hw:], TN)
        dd_ref[...] += _colsum(dy * u)
        ext_ref[0:8, :] = jnp.where(tidx == 0, 0.0, halo_ref[...])
        ext_ref[8:, :] = xsv
        xp = ext_ref[pl.ds(7, tt), :]
        gr, gi, pr, pi = gg[:, :hw], gg[:, hw:], xp[:, :hw], xp[:, hw:]
        da_ref[:, :hw] += _colsum(gr * pr + gi * pi)
        da_ref[:, hw:] += _colsum(gi * pr - gr * pi)

    ub = U_OFF // 128
    sd = jax.ShapeDtypeStruct
    rt = lambda t: nt - 1 - t
    tr = lambda a: jnp.swapaxes(a, 1, 2)
    res = _pcall(
        body, name="s5_bwd", grid=(S5_KCH, bsz, nt),
        in_specs=[pl.BlockSpec(memory_space=pl.ANY),
                  pl.BlockSpec((None, tt, 128), lambda k, b, t: (b, rt(t), ub + k)),
                  pl.BlockSpec((None, tt, 128), lambda k, b, t: (b, rt(t), k)),
                  pl.BlockSpec((None, None, tt, 2 * hw), lambda k, b, t: (b, k, rt(t), 0)),
                  pl.BlockSpec((None, None, 8, 2 * hw), lambda k, b, t: (b, k, jnp.maximum(rt(t) * tb - 1, 0), 0)),
                  pl.BlockSpec((None, 2 * hw, 128), lambda k, b, t: (k, 0, 0)),
                  pl.BlockSpec((None, 128, hw), lambda k, b, t: (k, 0, 0)),
                  pl.BlockSpec((None, 128, hw), lambda k, b, t: (k, 0, 0)),
                  pl.BlockSpec((None, 8, 2 * hw), lambda k, b, t: (k, 0, 0)),
                  pl.BlockSpec((1, 128), lambda k, b, t: (0, k))],
        out_specs=[pl.BlockSpec((None, tt, 128), lambda k, b, t: (b, rt(t), ub + k)),
                   pl.BlockSpec((None, 128, 2 * hw), lambda k, b, t: (k, 0, 0)),
                   pl.BlockSpec((None, 128, hw), lambda k, b, t: (k, 0, 0)),
                   pl.BlockSpec((None, 128, hw), lambda k, b, t: (k, 0, 0)),
                   pl.BlockSpec((None, 1, 2 * hw), lambda k, b, t: (k, 0, 0)),
                   pl.BlockSpec((1, 128), lambda k, b, t: (0, k))],
        out_shape=[sd(dp3.shape, dp3.dtype), sd((S5_KCH, 128, 2 * hw), F32), sd((S5_KCH, 128, hw), F32),
                   sd((S5_KCH, 128, hw), F32), sd((S5_KCH, 1, 2 * hw), F32), sd((1, S5_KCH * 128), F32)],
        scratch_shapes=[pltpu.VMEM((tt, 2 * hw), F32), pltpu.VMEM((tt + 8, 2 * hw), F32), pltpu.VMEM((8, 2 * hw), F32)],
        input_output_aliases={0: 0},
        compiler_params=_cp(("arbitrary", "arbitrary", "arbitrary"), 48),
    )(dp3, p3, dy3, xs, xs, tr(bk), tr(cre), tr(cim), apow_rev, dskip)
    return res[0], res[1], tr(res[2]), tr(res[3]), res[4], res[5]


_G0 = math.sqrt(2.0 / math.pi)
_G1 = 0.044715


def _gelu(y):
    return 0.5 * y * (1.0 + jnp.tanh(_G0 * (y + _G1 * y * y * y)))


def _gelu_grad(y):
    th = jnp.tanh(_G0 * (y + _G1 * y * y * y))
    return 0.5 * (1.0 + th) + 0.5 * y * (1.0 - th * th) * _G0 * (1.0 + 3.0 * _G1 * y * y)


def _glu_fwd(y_s5, wglu_g, lp):
    r, w = y_s5.shape
    tm = _row_tile(lp, 416)
    cw = wglu_g.shape[2]

    def body(y_ref, w_ref, gy_ref, z_ref, o_ref):
        gy = _bf(_gelu(y_ref[...]))
        gy_ref[...] = gy
        zs = [_dot(gy, w_ref[s]) for s in range(4)]
        for s in range(4):
            z_ref[:, s * cw:(s + 1) * cw] = zs[s]
        o_ref[:, :cw] = zs[0] * _sig(zs[2])
        o_ref[:, cw:] = zs[1] * _sig(zs[3])

    sd = jax.ShapeDtypeStruct
    return _pcall(
        body, name="glu_fwd", grid=(r // tm,),
        in_specs=[pl.BlockSpec((tm, w), lambda i: (i, 0)), _resident(wglu_g.shape)],
        out_specs=[pl.BlockSpec((tm, w), lambda i: (i, 0)), pl.BlockSpec((tm, 4 * cw), lambda i: (i, 0)),
                   pl.BlockSpec((tm, 2 * cw), lambda i: (i, 0))],
        out_shape=[sd((r, w), BF16), sd((r, 4 * cw), F32), sd((r, 2 * cw), F32)],
        compiler_params=_cp(("parallel",), 40),
    )(y_s5, wglu_g)


def _glu_bwd(dyg, z, y_s5, wglu_g, lp):
    r, w = y_s5.shape
    tm = _row_tile(lp, 416)
    cw = wglu_g.shape[2]

    def body(d_ref, z_ref, y_ref, w_ref, dz_ref, dy_ref):
        d = d_ref[...]
        zz = z_ref[...]
        acc = jnp.zeros((tm, w), F32)
        for s in range(2):
            z1 = zz[:, s * cw:(s + 1) * cw]
            sg = _sig(zz[:, (2 + s) * cw:(3 + s) * cw])
            dd = d[:, s * cw:(s + 1) * cw]
            dz1 = _bf(dd * sg)
            dz2 = _bf(dd * z1 * sg * (1.0 - sg))
            dz_ref[:, s * cw:(s + 1) * cw] = dz1
            dz_ref[:, (2 + s) * cw:(3 + s) * cw] = dz2
            acc += _dot(dz1, w_ref[s], NT) + _dot(dz2, w_ref[2 + s], NT)
        dy_ref[...] = acc * _gelu_grad(y_ref[...])

    sd = jax.ShapeDtypeStruct
    return _pcall(
        body, name="glu_bwd", grid=(r // tm,),
        in_specs=[pl.BlockSpec((tm, 2 * cw), lambda i: (i, 0)), pl.BlockSpec((tm, 4 * cw), lambda i: (i, 0)),
                  pl.BlockSpec((tm, w), lambda i: (i, 0)), _resident(wglu_g.shape)],
        out_specs=[pl.BlockSpec((tm, 4 * cw), lambda i: (i, 0)), pl.BlockSpec((tm, w), lambda i: (i, 0))],
        out_shape=[sd((r, 4 * cw), BF16), sd((r, w), F32)],
        compiler_params=_cp(("parallel",), 40),
    )(dyg, z, y_s5, wglu_g)


def _conv_fwd(p3, cw, cb):
    bsz, lp, _ = p3.shape
    tt = _row_tile(lp, 416)
    nt = lp // tt
    tb = tt // 8
    c = cw.shape[1]
    qb = Q_OFF // c

    hr = HALO_ROWS
    off = hr - (CONV_W - 1)

    def body(x_ref, halo_ref, w_ref, b_ref, pre_ref, act_ref, ext_ref):
        t = pl.program_id(1)
        ext_ref[0:hr, :] = jnp.where(t == 0, 0.0, halo_ref[...].astype(F32))
        ext_ref[hr:, :] = x_ref[...].astype(F32)
        w = w_ref[...]
        acc = b_ref[...] + w[0:1, :] * ext_ref[pl.ds(off, tt), :]
        for j in range(1, CONV_W):
            acc = acc + w[j:j + 1, :] * ext_ref[pl.ds(off + j, tt), :]
        pre_ref[...] = acc
        act_ref[...] = acc * _sig(acc)

    sd = jax.ShapeDtypeStruct
    return _pcall(
        body, name="conv_fwd", grid=(bsz, nt),
        in_specs=[pl.BlockSpec((None, tt, c), lambda b, t: (b, t, qb)),
                  pl.BlockSpec((None, hr, c), lambda b, t: (b, jnp.maximum(t * (tt // hr) - 1, 0), qb)),
                  _const((CONV_W, c)), _const((1, c))],
        out_specs=[pl.BlockSpec((None, tt, c), lambda b, t: (b, t, 0))] * 2,
        out_shape=[sd((bsz, lp, c), F32)] * 2,
        scratch_shapes=[pltpu.VMEM((tt + hr, c), F32)],
        compiler_params=_cp(("parallel", "parallel")),
    )(p3, p3, cw, cb)


def _conv_bwd(dp3, p3, dact3, pre3, cw):
    bsz, lp, _ = p3.shape
    tt = _row_tile(lp, 416)
    nt = lp // tt
    tb = tt // 8
    c = cw.shape[1]
    qb = Q_OFF // c

    hr = HALO_ROWS
    off = hr - (CONV_W - 1)

    def silu_grad(x):
        s = _sig(x)
        return s * (1.0 + x * (1.0 - s))

    def body(dp_any, x_ref, xh_ref, d_ref, dh_ref, pre_ref, preh_ref, w_ref, o_ref, dw_ref, db_ref, ext_ref, dext_ref):
        b = pl.program_id(0)
        t = pl.program_id(1)

        @pl.when((b == 0) & (t == 0))
        def _():
            dw_ref[...] = jnp.zeros_like(dw_ref)
            db_ref[...] = jnp.zeros_like(db_ref)

        dc = d_ref[...] * silu_grad(pre_ref[...])
        dch = jnp.where(t == nt - 1, 0.0, dh_ref[...] * silu_grad(preh_ref[...]))
        dext_ref[0:tt, :] = dc
        dext_ref[tt:, :] = dch
        ext_ref[0:hr, :] = jnp.where(t == 0, 0.0, xh_ref[...].astype(F32))
        ext_ref[hr:, :] = x_ref[...].astype(F32)
        w = w_ref[...]
        acc = w[CONV_W - 1:CONV_W, :] * dc
        for j in range(CONV_W - 1):
            acc = acc + w[j:j + 1, :] * dext_ref[pl.ds(CONV_W - 1 - j, tt), :]
        trow = t * tt + lax.broadcasted_iota(jnp.int32, (tt, 1), 0)
        o_ref[...] = jnp.where(trow >= PAD, acc, 0.0).astype(o_ref.dtype)
        db_ref[...] += _colsum(dc)
        for j in range(CONV_W):
            dw_ref[j:j + 1, :] += _colsum(dc * ext_ref[pl.ds(off + j, tt), :])

    sd = jax.ShapeDtypeStruct
    nxt = lambda t: jnp.minimum((t + 1) * tb, lp // 8 - 1)
    return _pcall(
        body, name="conv_bwd", grid=(bsz, nt),
        in_specs=[pl.BlockSpec(memory_space=pl.ANY),
                  pl.BlockSpec((None, tt, c), lambda b, t: (b, t, qb)),
                  pl.BlockSpec((None, hr, c), lambda b, t: (b, jnp.maximum(t * (tt // hr) - 1, 0), qb)),
                  pl.BlockSpec((None, tt, c), lambda b, t: (b, t, 0)),
                  pl.BlockSpec((None, 8, c), lambda b, t: (b, nxt(t), 0)),
                  pl.BlockSpec((None, tt, c), lambda b, t: (b, t, 0)),
                  pl.BlockSpec((None, 8, c), lambda b, t: (b, nxt(t), 0)),
                  _const((CONV_W, c))],
        out_specs=[pl.BlockSpec((None, tt, c), lambda b, t: (b, t, qb)), _const((CONV_W, c)), _const((1, c))],
        out_shape=[sd(dp3.shape, dp3.dtype), sd((CONV_W, c), F32), sd((1, c), F32)],
        scratch_shapes=[pltpu.VMEM((tt + hr, c), F32), pltpu.VMEM((tt + 8, c), F32)],
        input_output_aliases={0: 0},
        compiler_params=_cp(("arbitrary", "arbitrary")),
    )(dp3, p3, p3, dact3, dact3, pre3, pre3, cw)


def _mlstm_gates(g, h_idx, c_idx, lc):
    lane = lax.broadcasted_iota(jnp.int32, g.shape, 1)
    i_col = jnp.sum(jnp.where(lane == h_idx, g, 0.0), axis=1, keepdims=True)
    f_col = jnp.sum(jnp.where(lane == M_HEADS + h_idx, g, 0.0), axis=1, keepdims=True)
    row = lax.broadcasted_iota(jnp.int32, (lc, 1), 0)
    valid = (c_idx * lc + row) >= PAD
    li = jnp.where(valid, i_col, NEG)
    lf = jnp.where(valid, jnp.minimum(f_col, 0.0) - jnp.log(1.0 + jnp.exp(-jnp.abs(f_col))), 0.0)
    r2 = lax.broadcasted_iota(jnp.int32, (lc, lc), 0)
    c2 = lax.broadcasted_iota(jnp.int32, (lc, lc), 1)
    eye = r2 == c2
    tril = r2 >= c2
    to_row = lambda col: jnp.sum(jnp.where(eye, col, 0.0), axis=0, keepdims=True)
    lf_row = to_row(lf)
    b_col = jnp.sum(jnp.where(tril, lf_row, 0.0), axis=1, keepdims=True)
    b_row = to_row(b_col)
    li_row = to_row(li)
    d_mat = jnp.where(tril, b_col - b_row + li_row, NEG)
    return dict(f_col=f_col, valid=valid, li=li, b_col=b_col, d_mat=d_mat, eye=eye, r2=r2, c2=c2, row=row,
                to_row=to_row)


def _mlstm_chunk(q, ks, v, gq, c_st, n_st, m_st, lc):
    b_col, d_mat = gq["b_col"], gq["d_mat"]
    m_inter = b_col + m_st
    m_row = jnp.maximum(m_inter, jnp.max(d_mat, axis=1, keepdims=True))
    w_intra = jnp.exp(d_mat - m_row)
    w_inter = jnp.exp(m_inter - m_row)
    qb, kb, vb, cb = _bf(q), _bf(ks), _bf(v), _bf(c_st)
    s = _dot(qb, kb, NT) * w_intra
    qc = _dot(qb, cb)
    num = _dot(_bf(s), vb) + w_inter * qc
    qn = jnp.sum(q * n_st, axis=1, keepdims=True)
    den = jnp.sum(s, axis=1, keepdims=True) + w_inter * qn
    e = jnp.exp(-m_row)
    nn = jnp.maximum(jnp.abs(den), e)
    b_last = b_col[lc - 1:lc, :]
    g_log = b_last - b_col + gq["li"]
    m_new = jnp.maximum(b_last + m_st, jnp.max(g_log, axis=0, keepdims=True))
    w_k = jnp.exp(g_log - m_new)
    decay = jnp.exp(b_last + m_st - m_new)
    return dict(w_intra=w_intra, w_inter=w_inter, qb=qb, kb=kb, vb=vb, cb=cb, s=s, qc=qc, num=num, qn=qn, den=den,
                e=e, nn=nn, m_new=m_new, w_k=w_k, decay=decay)


def _chunks_per_step(nc):
    return max(c for c in (3, 2, 1) if nc % c == 0)


def _mlstm_fwd(qk3, p3, pg3):
    bsz, lp, _ = p3.shape
    lc = M_CHUNK
    nc = lp // lc
    dk, dv = 128, 256
    scale = dk ** -0.5

    cps = _chunks_per_step(nc)
    rows = cps * lc

    def body(q_ref, k_ref, v_ref, g_ref, h_ref, cs_ref, ns_ref, ms_ref, c_sc, n_sc, m_sc):
        st = pl.program_id(1)

        @pl.when(st == 0)
        def _():
            c_sc[...] = jnp.zeros_like(c_sc)
            n_sc[...] = jnp.zeros_like(n_sc)
            m_sc[...] = jnp.zeros_like(m_sc)

        for j in range(cps):
            rs = slice(j * lc, (j + 1) * lc)
            g = g_ref[rs, :]
            for hh in range(M_HEADS):
                c_st, n_st, m_all = c_sc[hh], n_sc[hh], m_sc[hh]
                cs_ref[hh, j] = c_st
                ns_ref[hh, j] = n_st
                ms_ref[hh, j] = m_all
                m_st = m_all[:, 0:1]
                q = q_ref[rs, hh * dk:(hh + 1) * dk]
                ks = k_ref[rs, hh * dk:(hh + 1) * dk] * scale
                v = v_ref[rs, hh * dv:(hh + 1) * dv]
                gq = _mlstm_gates(g, hh, st * cps + j, lc)
                f = _mlstm_chunk(q, ks, v, gq, c_st, n_st, m_st, lc)
                h_ref[rs, hh * dv:(hh + 1) * dv] = f["num"] / f["nn"]
                kw = ks * f["w_k"]
                c_sc[hh] = f["decay"] * c_st + _dot(_bf(kw), f["vb"], TN)
                n_sc[hh] = f["decay"] * n_st + _colsum(kw)
                m_sc[hh] = jnp.broadcast_to(f["m_new"], (1, 128))

    sd = jax.ShapeDtypeStruct
    nh = M_HEADS
    return _pcall(
        body, name="mlstm_fwd", grid=(bsz, nc // cps),
        in_specs=[pl.BlockSpec((None, rows, nh * dk), lambda b, c: (b, c, 0)),
                  pl.BlockSpec((None, rows, nh * dk), lambda b, c: (b, c, 1)),
                  pl.BlockSpec((None, rows, nh * dv), lambda b, c: (b, c, V_OFF // (nh * dv))),
                  pl.BlockSpec((None, rows, 128), lambda b, c: (b, c, 0))],
        out_specs=[pl.BlockSpec((None, rows, nh * dv), lambda b, c: (b, c, 0)),
                   pl.BlockSpec((None, nh, cps, dk, dv), lambda b, c: (b, 0, c, 0, 0)),
                   pl.BlockSpec((None, nh, cps, 1, dk), lambda b, c: (b, 0, c, 0, 0)),
                   pl.BlockSpec((None, nh, cps, 1, 128), lambda b, c: (b, 0, c, 0, 0))],
        out_shape=[sd((bsz, lp, nh * dv), F32), sd((bsz, nh, nc, dk, dv), F32),
                   sd((bsz, nh, nc, 1, dk), F32), sd((bsz, nh, nc, 1, 128), F32)],
        scratch_shapes=[pltpu.VMEM((nh, dk, dv), F32), pltpu.VMEM((nh, 1, dk), F32), pltpu.VMEM((nh, 1, 128), F32)],
        compiler_params=_cp(("parallel", "arbitrary")),
    )(qk3, qk3, p3, pg3)


def _mlstm_bwd(dp3, qk3, p3, pg3, dh3, cs, ns, ms):
    bsz, lp, _ = p3.shape
    lc = M_CHUNK
    nc = lp // lc
    dk, dv = 128, 256
    scale = dk ** -0.5

    cps = _chunks_per_step(nc)
    nst = nc // cps
    rows = cps * lc

    def body(dp_any, q_ref, k_ref, v_ref, g_ref, dh_ref, cs_ref, ns_ref, ms_ref,
             dv_ref, dqk_ref, dg_ref, dc_sc, dn_sc):
        t = pl.program_id(1)
        st = nst - 1 - t

        @pl.when(t == 0)
        def _():
            dc_sc[...] = jnp.zeros_like(dc_sc)
            dn_sc[...] = jnp.zeros_like(dn_sc)

        lane = lax.broadcasted_iota(jnp.int32, (lc, 128), 1)
        for j in reversed(range(cps)):
            rs = slice(j * lc, (j + 1) * lc)
            g = g_ref[rs, :]
            dgate = jnp.zeros((lc, 128), F32)
            for hh in range(M_HEADS):
                dgate = head(hh, j, rs, st * cps + j, g, lane, dgate, q_ref, k_ref, v_ref, dh_ref, cs_ref, ns_ref,
                             ms_ref, dv_ref, dqk_ref, dc_sc, dn_sc)
            dg_ref[rs, :] = dgate.astype(dg_ref.dtype)

    def head(hh, j, sl, c, g, lane, dgate, q_ref, k_ref, v_ref, dh_ref, cs_ref, ns_ref, ms_ref, dv_ref, dqk_ref,
             dc_sc, dn_sc):
        c_st, n_st = cs_ref[hh, j], ns_ref[hh, j]
        m_st = ms_ref[hh, j][:, 0:1]
        q = q_ref[sl, hh * dk:(hh + 1) * dk]
        ks = k_ref[sl, hh * dk:(hh + 1) * dk] * scale
        v = v_ref[sl, hh * dv:(hh + 1) * dv]
        dh = dh_ref[sl, hh * dv:(hh + 1) * dv]
        gq = _mlstm_gates(g, hh, c, lc)
        f = _mlstm_chunk(q, ks, v, gq, c_st, n_st, m_st, lc)
        eye, r2, c2, row, valid = gq["eye"], gq["r2"], gq["c2"], gq["row"], gq["valid"]
        w_intra, w_inter, s, nn, den = f["w_intra"], f["w_inter"], f["s"], f["nn"], f["den"]
        qb, kb, vb, cb, w_k, decay = f["qb"], f["kb"], f["vb"], f["cb"], f["w_k"], f["decay"]
        d_c, d_n = dc_sc[hh], dn_sc[hh]
        d_cb = _bf(d_c)

        hout = f["num"] / nn
        dnum = dh / nn
        d_nn = -jnp.sum(dh * hout, axis=1, keepdims=True) / nn
        dden = jnp.where(jnp.abs(den) > f["e"], d_nn * jnp.sign(den), 0.0)
        wdnum = w_inter * dnum
        wdden = w_inter * dden
        ds = _dot(_bf(dnum), vb, NT) + dden
        dsw = _bf(ds * w_intra)
        dq = _dot(dsw, kb) + _dot(_bf(wdnum), cb, NT) + wdden * n_st
        dkw = _dot(vb, d_cb, NT) + d_n
        dks = _dot(dsw, qb, TN) + dkw * w_k
        kw = ks * w_k
        dvv = _dot(_bf(s), _bf(dnum), TN) + _dot(_bf(kw), d_cb)
        dd = ds * s
        rs = jnp.sum(dd, axis=1, keepdims=True)
        cs_col = jnp.sum(jnp.where(eye, jnp.sum(dd, axis=0, keepdims=True), 0.0), axis=1, keepdims=True)
        dwi = jnp.sum(dnum * f["qc"], axis=1, keepdims=True) + dden * f["qn"]
        db = rs - cs_col + dwi * w_inter
        dli = cs_col
        ddecay = jnp.sum(jnp.sum(d_c * c_st, axis=1, keepdims=True), axis=0, keepdims=True) \
            + jnp.sum(d_n * n_st, axis=1, keepdims=True)
        dgl = jnp.sum(dkw * ks, axis=1, keepdims=True) * w_k
        dblast = ddecay * decay + jnp.sum(dgl, axis=0, keepdims=True)
        db = db - dgl + jnp.where(row == lc - 1, dblast, 0.0)
        dli = dli + dgl
        db_row = gq["to_row"](db)
        dlf = jnp.sum(jnp.where(c2 >= r2, db_row, 0.0), axis=1, keepdims=True)
        dlf = jnp.where(valid, dlf, 0.0)
        dgate = jnp.where(lane == hh, jnp.where(valid, dli, 0.0), dgate)
        dgate = jnp.where(lane == M_HEADS + hh, dlf / (1.0 + jnp.exp(gq["f_col"])), dgate)
        dqk_ref[sl, hh * dk:(hh + 1) * dk] = dq
        dqk_ref[sl, (M_HEADS + hh) * dk:(M_HEADS + hh + 1) * dk] = dks * scale
        dv_ref[sl, hh * dv:(hh + 1) * dv] = dvv.astype(dv_ref.dtype)
        dc_sc[hh] = decay * d_c + _dot(qb, _bf(wdnum), TN)
        dn_sc[hh] = decay * d_n + _colsum(q * wdden)
        return dgate

    sd = jax.ShapeDtypeStruct
    nh = M_HEADS
    rc = lambda c: nst - 1 - c
    return _pcall(
        body, name="mlstm_bwd", grid=(bsz, nst),
        in_specs=[pl.BlockSpec(memory_space=pl.ANY),
                  pl.BlockSpec((None, rows, nh * dk), lambda b, c: (b, rc(c), 0)),
                  pl.BlockSpec((None, rows, nh * dk), lambda b, c: (b, rc(c), 1)),
                  pl.BlockSpec((None, rows, nh * dv), lambda b, c: (b, rc(c), V_OFF // (nh * dv))),
                  pl.BlockSpec((None, rows, 128), lambda b, c: (b, rc(c), 0)),
                  pl.BlockSpec((None, rows, nh * dv), lambda b, c: (b, rc(c), 0)),
                  pl.BlockSpec((None, nh, cps, dk, dv), lambda b, c: (b, 0, rc(c), 0, 0)),
                  pl.BlockSpec((None, nh, cps, 1, dk), lambda b, c: (b, 0, rc(c), 0, 0)),
                  pl.BlockSpec((None, nh, cps, 1, 128), lambda b, c: (b, 0, rc(c), 0, 0))],
        out_specs=[pl.BlockSpec((None, rows, nh * dv), lambda b, c: (b, rc(c), V_OFF // (nh * dv))),
                   pl.BlockSpec((None, rows, 2 * nh * dk), lambda b, c: (b, rc(c), 0)),
                   pl.BlockSpec((None, rows, 128), lambda b, c: (b, rc(c), 0))],
        out_shape=[sd(dp3.shape, dp3.dtype), sd((bsz, lp, 2 * nh * dk), F32), sd((bsz, lp, 128), dp3.dtype)],
        scratch_shapes=[pltpu.VMEM((nh, dk, dv), F32), pltpu.VMEM((nh, 1, dk), F32)],
        input_output_aliases={0: 0},
        compiler_params=_cp(("arbitrary", "arbitrary")),
    )(dp3, qk3, qk3, p3, pg3, dh3, cs, ns, ms)


def _headnorm(x):
    dv = x.shape[1] // M_HEADS
    xh, rs = [], []
    for h in range(M_HEADS):
        xx = x[:, h * dv:(h + 1) * dv]
        mu = jnp.mean(xx, axis=-1, keepdims=True)
        xc = xx - mu
        rstd = lax.rsqrt(jnp.mean(xc * xc, axis=-1, keepdims=True) + LN_EPS)
        xh.append(xc * rstd)
        rs.append(rstd)
    return jnp.concatenate(xh, axis=1), rs


def _mix_fwd(hm, p, ys5g, h0, gn, wmo_bf, wo_bf, lp):
    r, d = hm.shape
    tm = _row_tile(lp, 384)

    def body(hm_ref, o_ref, gs_ref, gm_ref, ys_ref, h0_ref, gn_ref, wmo_ref, wo_ref,
             ymin_ref, ym_ref, mix_ref, r1_ref):
        xhat, _ = _headnorm(hm_ref[...])
        ymin = _bf(_sig(o_ref[...].astype(F32)) * (xhat * gn_ref[...]))
        ymin_ref[...] = ymin
        ym = _dot(ymin, wmo_ref[...])
        ym_ref[...] = ym
        mix = _bf(_sig(gs_ref[...].astype(F32)) * ys_ref[...] + _sig(gm_ref[...].astype(F32)) * ym)
        mix_ref[...] = mix
        r1_ref[...] = ALPHA * h0_ref[...] + _dot(mix, wo_ref[...])

    sd = jax.ShapeDtypeStruct
    row = pl.BlockSpec((tm, d), lambda i: (i, 0))
    return _pcall(
        body, name="mix_fwd", grid=(r // tm,),
        in_specs=[row, pl.BlockSpec((tm, d), lambda i: (i, O_OFF // d)), pl.BlockSpec((tm, d), lambda i: (i, GS_OFF // d)),
                  pl.BlockSpec((tm, d), lambda i: (i, GM_OFF // d)), row, row, _const((1, d)),
                  _resident((d, d)), _resident((d, d))],
        out_specs=[row] * 4,
        out_shape=[sd((r, d), BF16), sd((r, d), F32), sd((r, d), BF16), sd((r, d), F32)],
        compiler_params=_cp(("parallel",), 48),
    )(hm, p, p, p, ys5g, h0, gn, wmo_bf, wo_bf)


def _mix_bwd(dr1, wo_bf, wmo_bf, p, ys5g, ym, hm, gn, lp):
    r, d = hm.shape
    tm = _row_tile(lp, 384)
    dv = d // M_HEADS

    def body(dr1_ref, wo_ref, wmo_ref, o_ref, gs_ref, gm_ref, ys_ref, ym_ref, hm_ref, gn_ref,
             dp_ref, dys_ref, dym_ref, dhm_ref, dgn_ref):
        i = pl.program_id(0)

        @pl.when(i == 0)
        def _():
            dgn_ref[...] = jnp.zeros_like(dgn_ref)

        dmix = _dot(_bf(dr1_ref[...]), wo_ref[...], NT)
        sgs, sgm, so = (_sig(gs_ref[...].astype(F32)), _sig(gm_ref[...].astype(F32)), _sig(o_ref[...].astype(F32)))
        dys_ref[...] = dmix * sgs
        dp_ref[:, d:2 * d] = _bf(dmix * ys_ref[...] * sgs * (1.0 - sgs))
        dym = dmix * sgm
        dym_ref[...] = _bf(dym)
        dp_ref[:, 2 * d:3 * d] = _bf(dmix * ym_ref[...] * sgm * (1.0 - sgm))
        dymin = _dot(_bf(dym), wmo_ref[...], NT)
        xhat, rs = _headnorm(hm_ref[...])
        gn_ = gn_ref[...]
        dp_ref[:, 0:d] = _bf(dymin * (xhat * gn_) * so * (1.0 - so))
        dhn = dymin * so
        dgn_ref[...] += _colsum(dhn * xhat)
        dxh = dhn * gn_
        for h in range(M_HEADS):
            sl = slice(h * dv, (h + 1) * dv)
            a, xh = dxh[:, sl], xhat[:, sl]
            m1 = jnp.mean(a, axis=-1, keepdims=True)
            m2 = jnp.mean(a * xh, axis=-1, keepdims=True)
            dhm_ref[:, sl] = rs[h] * (a - m1 - xh * m2)

    sd = jax.ShapeDtypeStruct
    row = pl.BlockSpec((tm, d), lambda i: (i, 0))
    vec = _const((1, d))
    return _pcall(
        body, name="mix_bwd", grid=(r // tm,),
        in_specs=[row, _resident((d, d)), _resident((d, d)),
                  pl.BlockSpec((tm, d), lambda i: (i, O_OFF // d)), pl.BlockSpec((tm, d), lambda i: (i, GS_OFF // d)),
                  pl.BlockSpec((tm, d), lambda i: (i, GM_OFF // d)), row, row, row, vec],
        out_specs=[pl.BlockSpec((tm, 3 * d), lambda i: (i, 0)), row, row, row, vec],
        out_shape=[sd((r, NP), BF16), sd((r, d), F32), sd((r, d), BF16), sd((r, d), F32), sd((1, d), F32)],
        compiler_params=_cp(("arbitrary",), 56),
    )(dr1, wo_bf, wmo_bf, p, p, p, ys5g, ym, hm, gn)


def _mlp_fwd(r1, tgt, g1, b1, wup_g, wdn_bf, bup, g2, b2, lp):
    r, d = r1.shape
    tm = _row_tile(lp, 352)
    tps = lp // tm
    nf = wup_g.shape[0]

    def body(r1_ref, t_ref, g1_ref, b1_ref, wup_ref, wdn_ref, bup_ref, g2_ref, b2_ref,
             dr2_ref, h1b_ref, act_ref, loss_ref, dg2_ref, db2_ref):
        i = pl.program_id(0)

        @pl.when(i == 0)
        def _():
            loss_ref[...] = jnp.zeros_like(loss_ref)
            dg2_ref[...] = jnp.zeros_like(dg2_ref)
            db2_ref[...] = jnp.zeros_like(db2_ref)

        h1, _, _ = _ln_fwd(r1_ref[...], g1_ref[...], b1_ref[...])
        h1b = _bf(h1)
        h1b_ref[...] = h1b
        ff = jnp.zeros((tm, d), F32)
        for s in range(nf):
            up = _dot(h1b, wup_ref[s]) + bup_ref[:, s * d:(s + 1) * d]
            a = jnp.maximum(up, 0.0)
            a = _bf(a * a)
            act_ref[:, s * d:(s + 1) * d] = a
            ff = ff + _dot(a, wdn_ref[s * d:(s + 1) * d, :])
        r2 = ALPHA * h1 + ff
        g2 = g2_ref[...]
        y, xhat, rstd = _ln_fwd(r2, g2, b2_ref[...])
        t = (i % tps) * tm + lax.broadcasted_iota(jnp.int32, (tm, 1), 0)
        diff = jnp.where(t >= PAD + N_META, y - t_ref[...], 0.0)
        loss_ref[...] += 0.5 / d * jnp.sum(jnp.sum(diff * diff, axis=1, keepdims=True), axis=0, keepdims=True)
        dy = diff * (1.0 / d)
        dg2_ref[...] += _colsum(dy * xhat)
        db2_ref[...] += _colsum(dy)
        dr2_ref[...] = _ln_bwd(dy, xhat, rstd, g2)

    sd = jax.ShapeDtypeStruct
    row = pl.BlockSpec((tm, d), lambda i: (i, 0))
    vec = _const((1, d))
    return _pcall(
        body, name="mlp_fwd", grid=(r // tm,),
        in_specs=[row, row, vec, vec, _resident(wup_g.shape), _resident(wdn_bf.shape), _const((1, nf * d)), vec, vec],
        out_specs=[row, row, pl.BlockSpec((tm, nf * d), lambda i: (i, 0)), _const((1, 128)), vec, vec],
        out_shape=[sd((r, d), F32), sd((r, d), BF16), sd((r, nf * d), BF16), sd((1, 128), F32), sd((1, d), F32),
                   sd((1, d), F32)],
        compiler_params=_cp(("arbitrary",), 56),
    )(r1, tgt, g1, b1, wup_g, wdn_bf, bup, g2, b2)


def _mlp_bwd(h1b, dr2, r1, g1, wup_g, wdn_bf, bup, lp):
    r, d = h1b.shape
    tm = _row_tile(lp, 352)
    nf = wup_g.shape[0]

    def body(h1_ref, dr2_ref, r1_ref, g1_ref, wup_ref, wdn_ref, bup_ref, dr1_ref, dup_ref, dbup_ref, dg1_ref, db1_ref):
        i = pl.program_id(0)

        @pl.when(i == 0)
        def _():
            dbup_ref[...] = jnp.zeros_like(dbup_ref)
            dg1_ref[...] = jnp.zeros_like(dg1_ref)
            db1_ref[...] = jnp.zeros_like(db1_ref)

        h1b = h1_ref[...]
        dr2 = dr2_ref[...]
        dr2b = _bf(dr2)
        acc = ALPHA * dr2
        for s in range(nf):
            up = _dot(h1b, wup_ref[s]) + bup_ref[:, s * d:(s + 1) * d]
            dact = _dot(dr2b, wdn_ref[s * d:(s + 1) * d, :], NT)
            dup = dact * (2.0 * jnp.maximum(up, 0.0))
            dbup_ref[:, s * d:(s + 1) * d] += _colsum(dup)
            dupb = _bf(dup)
            dup_ref[:, s * d:(s + 1) * d] = dupb
            acc = acc + _dot(dupb, wup_ref[s], NT)
        g1 = g1_ref[...]
        _, xhat1, rstd1 = _ln_fwd(r1_ref[...], g1, 0.0)
        dr1_ref[...] = _ln_bwd(acc, xhat1, rstd1, g1)
        dg1_ref[...] += _colsum(acc * xhat1)
        db1_ref[...] += _colsum(acc)

    sd = jax.ShapeDtypeStruct
    row = pl.BlockSpec((tm, d), lambda i: (i, 0))
    vec = _const((1, d))
    return _pcall(
        body, name="mlp_bwd", grid=(r // tm,),
        in_specs=[row, row, row, vec, _resident(wup_g.shape), _resident(wdn_bf.shape), _const((1, nf * d))],
        out_specs=[row, pl.BlockSpec((tm, nf * d), lambda i: (i, 0)), _const((1, nf * d)), vec, vec],
        out_shape=[sd((r, d), F32), sd((r, nf * d), BF16), sd((1, nf * d), F32), sd((1, d), F32), sd((1, d), F32)],
        compiler_params=_cp(("arbitrary",), 56),
    )(h1b, dr2, r1, g1, wup_g, wdn_bf, bup)


def _s5_block_mats(bb_re_t, bb_im_t, c_re, c_im, ap_re, ap_im):
    ng = c_re.shape[0]
    gl = ng // S5_KCH
    eye = jnp.eye(gl, dtype=F32)

    def bmat(bt):
        bb = jnp.transpose(bt, (1, 0, 2)).reshape(S5_KCH, gl, S5_GROUP, S5_STATE)
        return jnp.einsum("kghp,gj->kghjp", bb, eye).reshape(S5_KCH, gl * S5_GROUP, gl * S5_STATE)

    def cmat(c):
        cc = c.reshape(S5_KCH, gl, S5_GROUP, S5_STATE)
        return jnp.einsum("kghp,gj->kjpgh", cc, eye).reshape(S5_KCH, gl * S5_STATE, gl * S5_GROUP)

    def pw(a):
        return jnp.transpose(a.reshape(8, S5_KCH, gl * S5_STATE), (1, 0, 2))

    bk = jnp.concatenate([bmat(bb_re_t), bmat(bb_im_t)], axis=-1)
    apow = jnp.concatenate([pw(ap_re), pw(ap_im)], axis=-1)
    return _bf(bk), _bf(cmat(c_re)), _bf(cmat(c_im)), apow


def _s5_block_grads(dbk, dcre, dcim, da):
    gl = dbk.shape[1] // S5_GROUP
    ng = gl * S5_KCH
    eye = jnp.eye(gl, dtype=F32)
    hw = gl * S5_STATE

    def bpart(x):
        x = x.reshape(S5_KCH, gl, S5_GROUP, gl, S5_STATE)
        x = jnp.einsum("kghjp,gj->kghp", x, eye).reshape(ng, S5_GROUP, S5_STATE)
        return jnp.transpose(x, (1, 0, 2))

    def cpart(x):
        x = x.reshape(S5_KCH, gl, S5_STATE, gl, S5_GROUP)
        return jnp.einsum("kjpgh,gj->kghp", x, eye).reshape(ng, S5_GROUP, S5_STATE)

    return (bpart(dbk[..., :hw]), bpart(dbk[..., hw:]), cpart(dcre), cpart(dcim),
            da[:, 0, :hw].reshape(ng, S5_STATE), da[:, 0, hw:].reshape(ng, S5_STATE))


def _tie(a, tok):
    return a if tok is None else a + tok[0, 0]


def _local_step(x, tgt, w, early=None, late=None, ready=None):
    ready = ready or (lambda names, g: None)
    bsz, seq, d = x.shape
    lp = PAD + N_META + seq
    r = bsz * lp
    tgtp = jnp.concatenate([jnp.zeros((bsz, PAD + N_META, d), F32), tgt], axis=1).reshape(r, d)

    h0, h0b = _ln0_fwd(x, w["meta_tokens"], w["ln0_g"], w["ln0_b"])
    if early is not None:
        w = {**w, **early((h0, tgtp))}
    p, pg = _inproj(h0b, w["w_in"], w["b_in"], lp)
    p3 = p.reshape(bsz, lp, NP)
    pg3 = pg.reshape(bsz, lp, 128)

    b_re_t = jnp.transpose(w["s5_b_re"], (2, 0, 1))
    b_im_t = jnp.transpose(w["s5_b_im"], (2, 0, 1))
    ap_re, ap_im, bb_re_t, bb_im_t = _s5_prep(w["s5_lambda_re"], w["s5_lambda_im"], w["s5_log_dt"], b_re_t, b_im_t)
    bk, cre, cim, apow = _s5_block_mats(bb_re_t, bb_im_t, w["s5_c_re"], w["s5_c_im"], ap_re, ap_im)
    y_s5, xs = _s5_fwd(p3, bk, cre, cim, apow, w["s5_d"])
    sw = y_s5.shape[-1]
    if late is not None:
        w = {**w, **late(y_s5)}
    gy, z, ys5g = _glu_fwd(y_s5.reshape(r, sw), w["s5_w_glu"], lp)

    pre3, qk3 = _conv_fwd(p3, w["qk_conv_w"], w["qk_conv_b"])
    hm3, cs, ns, ms = _mlstm_fwd(qk3, p3, pg3)
    hm = hm3.reshape(r, d)
    ymin, ym, mix, r1 = _mix_fwd(hm, p, ys5g, h0, w["m_norm_g"], w["m_w_out"], w["w_o"], lp)
    dr2, h1b, act, loss, dg2, db2 = _mlp_fwd(r1, tgtp, w["ln1_g"], w["ln1_b"], w["w_up"], w["w_down"], w["b_up"],
                                             w["ln2_g"], w["ln2_b"], lp)

    g = {"ln2_g": dg2, "ln2_b": db2}
    dr1, dup, g["b_up"], g["ln1_g"], g["ln1_b"] = _mlp_bwd(h1b, dr2, r1, w["ln1_g"], w["w_up"], w["w_down"], w["b_up"], lp)
    g["w_down"] = _mm_tn(act, dr2, name="dw_down")
    g["w_up"] = _mm_tn(h1b, dup, name="dw_up", split=w["w_up"].shape[0])
    tok = ready(("w_down", "w_up"), g)
    dp, dys5g, dym, dhm, g["m_norm_g"] = _mix_bwd(
        dr1, w["w_o"], w["m_w_out"], p, ys5g, ym, hm, _tie(w["m_norm_g"], tok), lp)
    g["w_o"] = _mm_tn(mix, dr1, name="dw_o")
    g["m_w_out"] = _mm_tn(ymin, dym, name="dw_mout")

    dp3 = dp.reshape(bsz, lp, NP)
    dp3, dqk3, dgate = _mlstm_bwd(dp3, qk3, p3, pg3, dhm.reshape(bsz, lp, d), cs, ns, ms)
    dp3, g["qk_conv_w"], g["qk_conv_b"] = _conv_bwd(dp3, p3, dqk3, pre3, w["qk_conv_w"])
    dz, dys5 = _glu_bwd(dys5g, z, y_s5.reshape(r, sw), w["s5_w_glu"], lp)
    g["s5_w_glu"] = _mm_tn(gy, dz, name="dw_glu", split=w["s5_w_glu"].shape[0])
    tok = ready(("s5_w_glu", "m_w_out", "w_o"), g)
    apow_rev = jnp.flip(apow, axis=1)
    dp3, dbk, dcre, dcim, da, g["s5_d"] = _s5_bwd(dp3, p3, dys5.reshape(bsz, lp, sw), xs, bk, cre, cim, apow_rev,
                                                 _tie(w["s5_d"], tok))
    dbb_re_t, dbb_im_t, g["s5_c_re"], g["s5_c_im"], da_re, da_im = _s5_block_grads(dbk, dcre, dcim, da)
    g["s5_lambda_re"], g["s5_lambda_im"], g["s5_log_dt"], gb_re_t, gb_im_t = _s5_prep_bwd(
        w["s5_lambda_re"], w["s5_lambda_im"], w["s5_log_dt"], b_re_t, b_im_t, da_re, da_im, dbb_re_t, dbb_im_t)
    g["s5_b_re"] = jnp.transpose(gb_re_t, (1, 2, 0))
    g["s5_b_im"] = jnp.transpose(gb_im_t, (1, 2, 0))

    dp3 = lax.dynamic_update_slice(dp3, dgate, (0, 0, G_OFF))
    dp = dp3.reshape(r, NP)
    g["w_in"], g["b_in"] = _mm_tn(h0b, dp, name="dw_in", colsum=True)
    tok = ready(("w_in",), g)
    dpw = _mm_nt(dp, w["w_in"], lp, name="dh0", dep=tok)
    grad_x, g["ln0_g"], g["ln0_b"], g["meta_tokens"] = _ln0_bwd(x, w["meta_tokens"], dr1, dpw, w["ln0_g"])
    return loss, grad_x, g


_ANY = pl.BlockSpec(memory_space=pl.ANY)
_MESH = pl.DeviceIdType.MESH


def _place():
    return lax.axis_index("x"), lax.axis_index("y"), lax.axis_index("c")


def _gather_chips(shards):
    n = len(shards)

    def body(*refs):
        ins, outs = refs[:n], refs[n:2 * n]
        send, recv, loc = refs[2 * n:]
        x, y, c = _place()
        me = 2 * x + y
        peers = [(1 - x, y), (x, 1 - y), (1 - x, 1 - y)]

        def rc(a, k, slot):
            px, py = peers[k]
            return pltpu.make_async_remote_copy(src_ref=ins[a], dst_ref=outs[a].at[slot], send_sem=send.at[a, k],
                                                recv_sem=recv.at[a, k], device_id=(px, py, c), device_id_type=_MESH)

        own = [pltpu.make_async_copy(ins[a], outs[a].at[me], loc.at[a]) for a in range(n)]
        for cp in own:
            cp.start()
        out = [rc(a, k, me) for a in range(n) for k in range(3)]
        for cp in out:
            cp.start()
        for a in range(n):
            for k in range(3):
                rc(a, k, 2 * peers[k][0] + peers[k][1]).wait_recv()
        for cp in out:
            cp.wait_send()
        for cp in own:
            cp.wait()

    return _pcall(
        body, name="gather_chips", in_specs=[_ANY] * n, out_specs=[_ANY] * n,
        out_shape=[jax.ShapeDtypeStruct((4,) + s.shape, s.dtype) for s in shards],
        scratch_shapes=[pltpu.SemaphoreType.DMA((n, 3)), pltpu.SemaphoreType.DMA((n, 3)), pltpu.SemaphoreType.DMA((n,))],
    )(*shards)


_HBM = pl.BlockSpec(memory_space=pltpu.HBM)
_SEM = pl.BlockSpec(memory_space=pltpu.SEMAPHORE)
_EFFECT = pltpu.SideEffectType.DATAFLOW_SIDE_EFFECTING


def _xchg_copies(srcs, lands, send, recv, scatter):
    x, y, c = _place()
    me = 2 * x + y
    peers = [(1 - x, y), (x, 1 - y), (1 - x, 1 - y)]
    out = []
    for a in range(len(srcs)):
        for k, (px, py) in enumerate(peers):
            src = srcs[a].at[2 * px + py] if scatter else srcs[a]
            dst = lands[a].at[k] if scatter else lands[a].at[me]
            out.append(pltpu.make_async_remote_copy(src_ref=src, dst_ref=dst, send_sem=send.at[3 * a + k],
                                                    recv_sem=recv.at[3 * a + k], device_id=(px, py, c),
                                                    device_id_type=_MESH))
    return out


def _xchg_start(srcs, lands, *, name, scatter, dep=None):
    n = len(srcs)
    deps = [] if dep is None else [dep]
    nd = len(deps)

    def body(*refs):
        send, recv = refs[2 * n + nd], refs[2 * n + nd + 1]
        for cp in _xchg_copies(refs[:n], refs[n:2 * n], send, recv, scatter):
            cp.start()
        refs[-1][...] = jnp.zeros_like(refs[-1])

    hbm = lambda a: pltpu.HBM(a.shape, a.dtype)
    con = lambda a: pltpu.with_memory_space_constraint(a, pltpu.HBM)
    res = _pcall(
        body, name=name, in_specs=[_HBM] * (2 * n) + [_ANY] * nd,
        out_specs=[_SEM, _SEM] + [_HBM] * (2 * n) + [pl.BlockSpec(memory_space=pltpu.VMEM)],
        out_shape=[pltpu.SemaphoreType.DMA((3 * n,)), pltpu.SemaphoreType.DMA((3 * n,))]
        + [hbm(a) for a in srcs] + [hbm(a) for a in lands] + [jax.ShapeDtypeStruct((8, 128), F32)],
        input_output_aliases={i: 2 + i for i in range(2 * n)},
        compiler_params=pltpu.CompilerParams(has_side_effects=_EFFECT),
    )(*[con(a) for a in srcs], *[con(a) for a in lands], *deps)
    return res[0], res[1], list(res[2:2 + n]), list(res[2 + n:2 + 2 * n]), res[-1]


def _xchg_wait(send, recv, srcs, lands, after, *, name, scatter):
    n = len(srcs)
    afters = list(after) if isinstance(after, (list, tuple)) else [after]

    def body(*refs):
        s_ref, r_ref = refs[2 * n], refs[2 * n + 1]
        for cp in _xchg_copies(refs[:n], refs[n:2 * n], s_ref, r_ref, scatter):
            cp.wait_send()
            cp.wait_recv()

    hbm = lambda a: pltpu.HBM(a.shape, a.dtype)
    res = _pcall(
        body, name=name, in_specs=[_HBM] * (2 * n) + [_SEM, _SEM] + [_ANY] * len(afters),
        out_specs=[_HBM] * (2 * n),
        out_shape=[hbm(a) for a in srcs] + [hbm(a) for a in lands],
        input_output_aliases={i: i for i in range(2 * n)},
        compiler_params=pltpu.CompilerParams(has_side_effects=_EFFECT),
    )(*srcs, *lands, send, recv, *afters)
    return list(res[:n]), list(res[n:])


def _swap_cores(arrs, name="swap_cores"):
    n = len(arrs)

    def body(*refs):
        ins, outs = refs[:n], refs[n:2 * n]
        send, recv = refs[2 * n:]
        x, y, c = _place()
        cps = [pltpu.make_async_remote_copy(src_ref=ins[a], dst_ref=outs[a], send_sem=send.at[a], recv_sem=recv.at[a],
                                            device_id=(x, y, 1 - c), device_id_type=_MESH) for a in range(n)]
        for cp in cps:
            cp.start()
        for cp in cps:
            cp.wait_recv()
        for cp in cps:
            cp.wait_send()

    return _pcall(
        body, name=name, in_specs=[_ANY] * n, out_specs=[_ANY] * n,
        out_shape=[jax.ShapeDtypeStruct(s.shape, s.dtype) for s in arrs],
        scratch_shapes=[pltpu.SemaphoreType.DMA((n,)), pltpu.SemaphoreType.DMA((n,))],
    )(*arrs)


def _allreduce_small(v, dep=None):
    rows = v.shape[0]
    half = rows // 2
    assert half % 8 == 0 and 2 * half == rows
    deps = [] if dep is None else [dep]

    def body(v_ref, *rest):
        out_ref, sib_ref, pair_ref, slots_ref, send, recv = rest[len(deps):]
        x, y, c = _place()
        chip = 2 * x + y
        sibling = (x, y, 1 - c)
        peers = [(1 - x, y), (x, 1 - y), (1 - x, 1 - y)]
        mine = pl.ds(pl.multiple_of(c * half, 8), half)

        first = pltpu.make_async_remote_copy(src_ref=v_ref, dst_ref=sib_ref, send_sem=send.at[0], recv_sem=recv.at[0],
                                             device_id=sibling, device_id_type=_MESH)
        first.start()
        first.wait_recv()
        pair_ref[...] = v_ref[...] + sib_ref[...]
        slots_ref[chip] = pair_ref[mine, :]
        cross = [pltpu.make_async_remote_copy(src_ref=pair_ref.at[mine], dst_ref=slots_ref.at[chip],
                                              send_sem=send.at[1 + k], recv_sem=recv.at[1 + k],
                                              device_id=(px, py, c), device_id_type=_MESH)
                 for k, (px, py) in enumerate(peers)]
        for cp in cross:
            cp.start()
        for cp in cross:
            cp.wait_recv()
        out_ref[mine, :] = ((slots_ref[0] + slots_ref[1]) + slots_ref[2]) + slots_ref[3]
        last = pltpu.make_async_remote_copy(src_ref=out_ref.at[mine], dst_ref=out_ref.at[mine], send_sem=send.at[4],
                                            recv_sem=recv.at[4], device_id=sibling, device_id_type=_MESH)
        last.start()
        last.wait_recv()
        first.wait_send()
        for cp in cross:
            cp.wait_send()
        last.wait_send()

    vm = pl.BlockSpec(memory_space=pltpu.VMEM)
    return _pcall(
        body, name="allreduce_small", in_specs=[vm] + [_ANY] * len(deps), out_specs=vm,
        out_shape=jax.ShapeDtypeStruct((rows, 128), F32),
        scratch_shapes=[pltpu.VMEM((rows, 128), F32), pltpu.VMEM((rows, 128), F32), pltpu.VMEM((4, half, 128), F32),
                        pltpu.SemaphoreType.DMA((5,)), pltpu.SemaphoreType.DMA((5,))],
        compiler_params=_cp(None, 40),
    )(v, *deps)


def _sum_slots(own, land):
    ns, rows, cols = land.shape
    tm = _row_tile(rows, 256, 8)

    def body(own_ref, a_ref, o_ref):
        o_ref[...] = ((own_ref[...] + a_ref[0]) + a_ref[1]) + a_ref[2]

    return _pcall(
        body, name="sum_slots", grid=(rows // tm,),
        in_specs=[pl.BlockSpec((tm, cols), lambda i: (i, 0)), pl.BlockSpec((ns, tm, cols), lambda i: (0, i, 0))],
        out_specs=pl.BlockSpec((tm, cols), lambda i: (i, 0)),
        out_shape=jax.ShapeDtypeStruct((rows, cols), F32),
        compiler_params=_cp(("parallel",), 40),
    )(own, land)


def _adamw(w, m, v, g0, g1=None):
    rows, cols = w.shape[-2:]
    lead = w.ndim == 3
    tm = _row_tile(rows, max(8, (1 << 20) // (4 * cols)), 8)
    c1 = 1.0 - ADAM_B1 ** ADAM_STEP
    c2 = 1.0 - ADAM_B2 ** ADAM_STEP
    two = g1 is not None

    def body(*refs):
        w_ref, m_ref, v_ref, g0_ref = refs[:4]
        g_ref, d_ref, nm_ref, nv_ref = refs[-4:]
        g = g0_ref[...]
        if two:
            g = g + refs[4][...]
        nm = ADAM_B1 * m_ref[...] + (1.0 - ADAM_B1) * g
        nv = ADAM_B2 * v_ref[...] + (1.0 - ADAM_B2) * (g * g)
        g_ref[...] = g
        nm_ref[...] = nm
        nv_ref[...] = nv
        d_ref[...] = -ADAM_LR * ((nm / c1) / (jnp.sqrt(nv / c2) + ADAM_EPS) + ADAM_WD * w_ref[...])

    blk = pl.BlockSpec((tm, cols), lambda i: (i, 0))
    wblk = pl.BlockSpec((None, tm, cols), lambda i: (0, i, 0)) if lead else blk
    ins = [w, m, v, g0] + ([g1] if two else [])
    return _pcall(
        body, name="adamw", grid=(rows // tm,), in_specs=[wblk] * 3 + [blk] * (len(ins) - 3), out_specs=[wblk] * 4,
        out_shape=[jax.ShapeDtypeStruct(w.shape, F32)] * 4,
        compiler_params=_cp(("parallel",), 40),
    )(*ins)


_BIG = ("w_in", "s5_w_glu", "m_w_out", "w_o", "w_up", "w_down")
_SMALL = ("ln0_g", "ln0_b", "b_in", "qk_conv_b", "s5_lambda_re", "s5_lambda_im", "s5_log_dt", "s5_b_re", "s5_b_im",
          "s5_c_re", "s5_c_im", "s5_d", "m_norm_g", "ln1_g", "ln1_b", "b_up", "ln2_g", "ln2_b")
_SMALL_SHARDED = ("meta_tokens", "qk_conv_w")
_ORDER = ("meta_tokens", "ln0_g", "ln0_b", "w_in", "b_in", "qk_conv_w", "qk_conv_b", "s5_lambda_re", "s5_lambda_im",
          "s5_log_dt", "s5_b_re", "s5_b_im", "s5_c_re", "s5_c_im", "s5_d", "s5_w_glu", "m_norm_g", "m_w_out", "w_o",
          "ln1_g", "ln1_b", "w_up", "b_up", "w_down", "ln2_g", "ln2_b")


def _pack(arrs):
    flat = jnp.concatenate([a.reshape(-1) for a in arrs])
    n = flat.shape[0]
    rows = -(-n // 2048) * 16
    return jnp.pad(flat, (0, rows * 128 - n)).reshape(rows, 128)


def _unpack(packed, shapes):
    flat = packed.reshape(-1)
    out, off = [], 0
    for s in shapes:
        n = math.prod(s)
        out.append(flat[off:off + n].reshape(s))
        off += n
    return out


def kernel(x, meta_tokens, ln0_g, ln0_b, w_in, b_in, qk_conv_w, qk_conv_b, s5_lambda_re, s5_lambda_im, s5_log_dt, s5_b_re, s5_b_im, s5_c_re, s5_c_im, s5_d, s5_w_glu, m_norm_g, m_w_out, w_o, ln1_g, ln1_b, w_up, b_up, w_down, ln2_g, ln2_b, loss_target, m_meta_tokens, m_ln0_g, m_ln0_b, m_w_in, m_b_in, m_qk_conv_w, m_qk_conv_b, m_s5_lambda_re, m_s5_lambda_im, m_s5_log_dt, m_s5_b_re, m_s5_b_im, m_s5_c_re, m_s5_c_im, m_s5_d, m_s5_w_glu, m_m_norm_g, m_m_w_out, m_w_o, m_ln1_g, m_ln1_b, m_w_up, m_b_up, m_w_down, m_ln2_g, m_ln2_b, v_meta_tokens, v_ln0_g, v_ln0_b, v_w_in, v_b_in, v_qk_conv_w, v_qk_conv_b, v_s5_lambda_re, v_s5_lambda_im, v_s5_log_dt, v_s5_b_re, v_s5_b_im, v_s5_c_re, v_s5_c_im, v_s5_d, v_s5_w_glu, v_m_norm_g, v_m_w_out, v_w_o, v_ln1_g, v_ln1_b, v_w_up, v_b_up, v_w_down, v_ln2_g, v_ln2_b):
    wts = dict(meta_tokens=meta_tokens, ln0_g=ln0_g, ln0_b=ln0_b, w_in=w_in, b_in=b_in, qk_conv_w=qk_conv_w,
               qk_conv_b=qk_conv_b, s5_lambda_re=s5_lambda_re, s5_lambda_im=s5_lambda_im, s5_log_dt=s5_log_dt,
               s5_b_re=s5_b_re, s5_b_im=s5_b_im, s5_c_re=s5_c_re, s5_c_im=s5_c_im, s5_d=s5_d, s5_w_glu=s5_w_glu,
               m_norm_g=m_norm_g, m_w_out=m_w_out, w_o=w_o, ln1_g=ln1_g, ln1_b=ln1_b, w_up=w_up, b_up=b_up,
               w_down=w_down, ln2_g=ln2_g, ln2_b=ln2_b)
    mom = dict(meta_tokens=m_meta_tokens, ln0_g=m_ln0_g, ln0_b=m_ln0_b, w_in=m_w_in, b_in=m_b_in, qk_conv_w=m_qk_conv_w,
               qk_conv_b=m_qk_conv_b, s5_lambda_re=m_s5_lambda_re, s5_lambda_im=m_s5_lambda_im, s5_log_dt=m_s5_log_dt,
               s5_b_re=m_s5_b_re, s5_b_im=m_s5_b_im, s5_c_re=m_s5_c_re, s5_c_im=m_s5_c_im, s5_d=m_s5_d,
               s5_w_glu=m_s5_w_glu, m_norm_g=m_m_norm_g, m_w_out=m_m_w_out, w_o=m_w_o, ln1_g=m_ln1_g, ln1_b=m_ln1_b,
               w_up=m_w_up, b_up=m_b_up, w_down=m_w_down, ln2_g=m_ln2_g, ln2_b=m_ln2_b)
    var = dict(meta_tokens=v_meta_tokens, ln0_g=v_ln0_g, ln0_b=v_ln0_b, w_in=v_w_in, b_in=v_b_in, qk_conv_w=v_qk_conv_w,
               qk_conv_b=v_qk_conv_b, s5_lambda_re=v_s5_lambda_re, s5_lambda_im=v_s5_lambda_im, s5_log_dt=v_s5_log_dt,
               s5_b_re=v_s5_b_re, s5_b_im=v_s5_b_im, s5_c_re=v_s5_c_re, s5_c_im=v_s5_c_im, s5_d=v_s5_d,
               s5_w_glu=v_s5_w_glu, m_norm_g=v_m_norm_g, m_w_out=v_m_w_out, w_o=v_w_o, ln1_g=v_ln1_g, ln1_b=v_ln1_b,
               w_up=v_w_up, b_up=v_b_up, w_down=v_w_down, ln2_g=v_ln2_g, ln2_b=v_ln2_b)
    d = x.shape[-1]
    chip = 2 * lax.axis_index("x") + lax.axis_index("y")

    gw = dict(zip(_SMALL_SHARDED, _gather_chips([meta_tokens, qk_conv_w[0]])))
    own_w_in = _bf(w_in[0])
    fsend, frecv, fsrc, fland, ftok = _xchg_start([own_w_in], [lax.empty((4,) + own_w_in.shape, BF16)],
                                                  name="gather_w_in_start", scatter=False, dep=gw["qk_conv_w"])
    late_names = tuple(n for n in _BIG if n != "w_in")
    cat = lambda a: jnp.transpose(a, (1, 0, 2)).reshape(a.shape[1], 4 * a.shape[2])
    w = dict(
        meta_tokens=cat(gw["meta_tokens"]), ln0_g=ln0_g[None], ln0_b=_tie(ln0_b[None], ftok),
        qk_conv_w=cat(gw["qk_conv_w"]), qk_conv_b=qk_conv_b,
        s5_lambda_re=s5_lambda_re[0], s5_lambda_im=s5_lambda_im[0], s5_log_dt=s5_log_dt[0][:, None],
        s5_b_re=s5_b_re[0], s5_b_im=s5_b_im[0], s5_c_re=s5_c_re[0], s5_c_im=s5_c_im[0], s5_d=s5_d,
        m_norm_g=m_norm_g, ln1_g=ln1_g, ln1_b=ln1_b, b_up=b_up, ln2_g=ln2_g, ln2_b=ln2_b)
    in_flight = {}

    def place_own(src, land):
        return lax.dynamic_update_slice(land, src[None], (chip,) + (0,) * src.ndim)

    small_names = _SMALL + _SMALL_SHARDED

    def view(n, a):
        return jnp.swapaxes(a, -1, -2) if n in ("s5_b_re", "s5_b_im") else a

    small_wmv = [_pack([view(n, dct[n]) for n in small_names]) for dct in (wts, mom, var)]

    def early(after):
        src, land = _xchg_wait(fsend, frecv, fsrc, fland, tuple(after) + tuple(small_wmv), name="gather_w_in_wait",
                               scatter=False)
        late_src = [_bf(wts[n][0]) for n in late_names]
        st = _xchg_start(late_src, [lax.empty((4,) + a.shape, a.dtype) for a in late_src], name="gather_late_start",
                         scatter=False, dep=src[0])
        in_flight["late"] = st[:4]
        return dict(w_in=_w_in_from_slots(place_own(src[0], land[0]), IN_CHUNK), b_in=_tie(_to_pad_cols(b_in), st[4]))

    def late(after):
        src, land = _xchg_wait(*in_flight["late"], after, name="gather_late_wait", scatter=False)
        full = {n: place_own(s, ld) for n, s, ld in zip(late_names, src, land)}
        return dict(s5_w_glu=full["s5_w_glu"], m_w_out=full["m_w_out"].reshape(d, d), w_o=full["w_o"].reshape(d, d),
                    w_up=full["w_up"], w_down=full["w_down"].reshape(4 * d, d))

    flying = []

    def ready(names, g):
        parts = dict(
            w_in=lambda: _slots_from_w_in(g["w_in"][0]), s5_w_glu=lambda: g["s5_w_glu"],
            m_w_out=lambda: g["m_w_out"].reshape(4, d // 4, d), w_o=lambda: g["w_o"].reshape(4, d // 4, d),
            w_up=lambda: g["w_up"], w_down=lambda: g["w_down"].reshape(4, d, d))
        src = [parts[n]() for n in names]
        land = [lax.empty((3,) + a.shape[1:], a.dtype) for a in src]
        st = _xchg_start(src, land, name="scatter_" + names[0] + "_start", scatter=True)
        flying.append((names,) + st[:4])
        return st[4]

    loss, grad_x, g = _local_step(x, loss_target, w, early, late, ready)
    g["b_in"] = _from_pad_cols(g["b_in"])

    res = {}

    def flat(a):
        return jnp.swapaxes(a, -1, -2).reshape(a.shape[:-2] + (-1, 128))

    def unflat(y, shape):
        return jnp.swapaxes(y.reshape(shape[:-2] + (shape[-1], shape[-2])), -1, -2)

    def finish(groups, after, tag):
        mine = {}
        for names, send, recv, src, land in groups:
            src, land = _xchg_wait(send, recv, src, land, after, name="scatter_" + names[0] + "_wait", scatter=True)
            for n, s, ld in zip(names, src, land):
                mine[n] = _sum_slots(lax.dynamic_index_in_dim(s, chip, 0, keepdims=False), ld)
        theirs = _swap_cores(list(mine.values()), name="swap_cores_" + tag)
        for n, t in zip(mine, theirs):
            if n == "w_in":
                res[n] = [unflat(r, wts[n].shape) for r in _adamw(flat(wts[n]), flat(mom[n]), flat(var[n]),
                                                                  flat(mine[n]), flat(t))]
            else:
                res[n] = _adamw(wts[n], mom[n], var[n], mine[n], t)

    finish(flying[:-1], g["ln0_g"], "a")

    small_shapes = [(1, 128)] + [view(n, wts[n]).shape for n in _SMALL] + [g[n].shape for n in _SMALL_SHARDED]
    packed = _pack([loss] + [view(n, g[n]) for n in _SMALL] + [g[n] for n in _SMALL_SHARDED])
    tot = _unpack(_allreduce_small(packed, dep=res["w_o"][3]), small_shapes)
    loss_out = tot[0][0, 0]
    gsm = dict(zip(_SMALL + _SMALL_SHARDED, tot[1:]))
    for n in _SMALL_SHARDED:
        cols = wts[n].shape[-1]
        gsm[n] = lax.dynamic_slice_in_dim(gsm[n], chip * cols, cols, axis=1).reshape(wts[n].shape)

    names = small_names
    shapes = [view(n, wts[n]).shape for n in names]
    small_out = _adamw(*small_wmv, _pack([gsm[n] for n in names]))
    small_res = [_unpack(r, shapes) for r in small_out]
    for j, n in enumerate(names):
        res[n] = [view(n, small_res[q][j]) for q in range(4)]
    finish(flying[-1:], small_out[0], "b")

    return (loss_out, grad_x, *[res[n][0] for n in _ORDER], *[res[n][1] for n in _ORDER],
            *[res[n][2] for n in _ORDER], *[res[n][3] for n in _ORDER])
```

```python
import functools
import math

import jax
import jax.numpy as jnp
from jax import lax
from jax.experimental import pallas as pl
from jax.experimental.pallas import tpu as pltpu

F32 = jnp.float32
BF16 = jnp.bfloat16
HI = lax.Precision.HIGHEST

N_META = 16
M_HEADS = 4
M_CHUNK = 128
PAD = M_CHUNK - N_META
CONV_W = 4
HALO_ROWS = 16
S5_GROUP = 16
S5_STATE = 64
S5_KCH = 4
LN_EPS = 1e-5
ALPHA = 2.0 ** 0.25
NEG = -1e30
ADAM_LR, ADAM_B1, ADAM_B2, ADAM_EPS, ADAM_WD, ADAM_STEP = 0.001, 0.9, 0.999, 1e-08, 0.01, 10

O_OFF, GS_OFF, GM_OFF, V_OFF, Q_OFF, K_OFF, U_OFF, G_OFF, NP = 0, 1024, 2048, 3072, 4096, 4608, 5120, 5632, 5760

NN = ((1,), (0,))
NT = ((1,), (1,))
TN = ((0,), (0,))


def _dot(a, b, dims=NN, prec=None):
    return lax.dot_general(a, b, (dims, ((), ())), preferred_element_type=F32, precision=prec)


def _bf(x):
    return x.astype(BF16)


def _sig(x):
    return 0.5 * jnp.tanh(0.5 * x) + 0.5


def _pcall(body, **kw):
    return pl.pallas_call(body, **kw)


def _cp(sem=None, vmem_mb=None):
    kw = {}
    if sem is not None:
        kw["dimension_semantics"] = sem
    if vmem_mb is not None:
        kw["vmem_limit_bytes"] = vmem_mb << 20
    return pltpu.CompilerParams(**kw)


def _row_tile(n, want, mult=16):
    best = None
    for t in range(mult, want + 1, mult):
        if n % t == 0:
            best = t
    assert best is not None, (n, want)
    return best


def _resident(shape):
    nd = len(shape)
    return pl.BlockSpec(shape, lambda *_: (0,) * nd, pipeline_mode=pl.Buffered(1))


def _const(shape):
    nd = len(shape)
    return pl.BlockSpec(shape, lambda *_: (0,) * nd)


def _ln_fwd(x, g, b):
    mu = jnp.mean(x, axis=-1, keepdims=True)
    xc = x - mu
    var = jnp.mean(xc * xc, axis=-1, keepdims=True)
    rstd = lax.rsqrt(var + LN_EPS)
    xhat = xc * rstd
    return xhat * g + b, xhat, rstd


def _ln_bwd(dy, xhat, rstd, g):
    dxh = dy * g
    m1 = jnp.mean(dxh, axis=-1, keepdims=True)
    m2 = jnp.mean(dxh * xhat, axis=-1, keepdims=True)
    return rstd * (dxh - m1 - xhat * m2)


def _colsum(x):
    return jnp.sum(x, axis=0, keepdims=True)


def _to_pad_cols(w):
    u, q, k, v, o, gi, gf, gs, gm = (w[..., 0:512], w[..., 512:1024], w[..., 1024:1536], w[..., 1536:2560],
                                     w[..., 2560:3584], w[..., 3584:3588], w[..., 3588:3592], w[..., 3592:4616],
                                     w[..., 4616:5640])
    z = jnp.zeros(w.shape[:-1] + (NP - G_OFF - 8,), w.dtype)
    return jnp.concatenate([o, gs, gm, v, q, k, u, gi, gf, z], axis=-1)


def _from_pad_cols(w):
    o, gs, gm, v, q, k, u = (w[..., O_OFF:GS_OFF], w[..., GS_OFF:GM_OFF], w[..., GM_OFF:V_OFF], w[..., V_OFF:Q_OFF],
                             w[..., Q_OFF:K_OFF], w[..., K_OFF:U_OFF], w[..., U_OFF:G_OFF])
    gi, gf = w[..., G_OFF:G_OFF + 4], w[..., G_OFF + 4:G_OFF + 8]
    return jnp.concatenate([u, q, k, v, o, gi, gf, gs, gm], axis=-1)


_IN_REF = (("u", 512), ("q", 512), ("k", 512), ("v", 1024), ("o", 1024), ("i", 4), ("f", 4), ("gs", 1024), ("gm", 1024))
_IN_PAD = (("o", O_OFF), ("gs", GS_OFF), ("gm", GM_OFF), ("v", V_OFF), ("q", Q_OFF), ("k", K_OFF), ("u", U_OFF),
           ("i", G_OFF), ("f", G_OFF + 4))


def _in_ref_ranges():
    out, off = {}, 0
    for n, s in _IN_REF:
        out[n] = (off, off + s)
        off += s
    return out, off


def _w_in_from_slots(g, chunk=None):
    rng, total = _in_ref_ranges()
    width = total // g.shape[0]
    cols = []
    for n, _ in _IN_PAD:
        a, b = rng[n]
        while a < b:
            s = a // width
            e = min(b, (s + 1) * width)
            cols.append(g[s][:, a - s * width:e - s * width])
            a = e
    cols.append(jnp.zeros((g.shape[1], NP - G_OFF - 8), g.dtype))
    if chunk is None:
        return jnp.concatenate(cols, axis=1)
    chunks, cur, room = [], [], chunk
    for c in cols:
        while c.shape[1] > 0:
            take = min(room, c.shape[1])
            cur.append(c[:, :take])
            c, room = c[:, take:], room - take
            if room == 0:
                chunks.append(jnp.concatenate(cur, axis=1))
                cur, room = [], chunk
    assert not cur
    return jnp.stack(chunks, axis=0)


def _slots_from_w_in(wp, nslot=4):
    rng, total = _in_ref_ranges()
    width = total // nslot
    pad_off = dict(_IN_PAD)
    slots = []
    for s in range(nslot):
        lo, hi = s * width, (s + 1) * width
        cols = []
        for n, _ in _IN_REF:
            a, b = rng[n]
            x0, x1 = max(a, lo), min(b, hi)
            if x0 < x1:
                cols.append(wp[:, pad_off[n] + x0 - a:pad_off[n] + x1 - a])
        slots.append(jnp.concatenate(cols, axis=1))
    return jnp.stack(slots, axis=0)


HEAD = PAD + N_META


def _ln0_in(j, x_ref, meta_ref):
    first = jnp.concatenate([jnp.zeros((PAD, meta_ref.shape[1]), F32), meta_ref[...]], axis=0)
    return jnp.where(j == 0, first[None], x_ref[...])


def _ln0_fwd(x, meta, g, b):
    bsz, seq, d = x.shape
    nb = seq // HEAD + 1

    def body(x_ref, m_ref, g_ref, b_ref, o_ref, ob_ref):
        y, _, _ = _ln_fwd(_ln0_in(pl.program_id(0), x_ref, m_ref), g_ref[...], b_ref[...])
        o_ref[...] = y
        ob_ref[...] = _bf(y)

    row = pl.BlockSpec((bsz, HEAD, d), lambda j: (0, j, 0))
    h0, h0b = _pcall(
        body, name="ln0_fwd", grid=(nb,),
        in_specs=[pl.BlockSpec((bsz, HEAD, d), lambda j: (0, jnp.maximum(j - 1, 0), 0)), _const((N_META, d)),
                  _const((1, d)), _const((1, d))],
        out_specs=[row, row],
        out_shape=[jax.ShapeDtypeStruct((bsz, nb * HEAD, d), F32), jax.ShapeDtypeStruct((bsz, nb * HEAD, d), BF16)],
        compiler_params=_cp(("arbitrary",)),
    )(x, meta, g, b)
    return h0.reshape(-1, d), h0b.reshape(-1, d)


def _ln0_bwd(x, meta, dr1, dpw, g):
    bsz, seq, d = x.shape
    nb = seq // HEAD + 1

    def body(x_ref, m_ref, a_ref, c_ref, g_ref, o_ref, dg_ref, db_ref, dm_ref):
        j = pl.program_id(0)

        @pl.when(j == 0)
        def _():
            dg_ref[...] = jnp.zeros_like(dg_ref)
            db_ref[...] = jnp.zeros_like(db_ref)
            dm_ref[...] = jnp.zeros_like(dm_ref)

        dy = ALPHA * a_ref[...] + c_ref[...]
        _, xhat, rstd = _ln_fwd(_ln0_in(j, x_ref, m_ref), g_ref[...], 0.0)
        dx = _ln_bwd(dy, xhat, rstd, g_ref[...])
        o_ref[...] = dx
        dg_ref[...] += _colsum((dy * xhat).reshape(bsz * HEAD, d))
        db_ref[...] += _colsum(dy.reshape(bsz * HEAD, d))

        @pl.when(j == 0)
        def _():
            dm_ref[...] += jnp.sum(dx[:, PAD:, :], axis=0)

    row = pl.BlockSpec((bsz, HEAD, d), lambda j: (0, j, 0))
    tok = pl.BlockSpec((bsz, HEAD, d), lambda j: (0, jnp.maximum(j - 1, 0), 0))
    lp = nb * HEAD
    return _pcall(
        body, name="ln0_bwd", grid=(nb,),
        in_specs=[tok, _const((N_META, d)), row, row, _const((1, d))],
        out_specs=[tok, _const((1, d)), _const((1, d)), _const((N_META, d))],
        out_shape=[jax.ShapeDtypeStruct((bsz, seq, d), F32), jax.ShapeDtypeStruct((1, d), F32),
                   jax.ShapeDtypeStruct((1, d), F32), jax.ShapeDtypeStruct((N_META, d), F32)],
        compiler_params=_cp(("arbitrary",)),
    )(x, meta, dr1.reshape(bsz, lp, d), dpw.reshape(bsz, lp, d), g)


IN_CHUNK = 1152


def _chunk_cols(w):
    k, n = w.shape
    return jnp.transpose(w.reshape(k, n // IN_CHUNK, IN_CHUNK), (1, 0, 2))


def _inproj(h0b, w3, bias, lp):
    r, d = h0b.shape
    nj, _, tn = w3.shape
    tm = _row_tile(lp, 1056)
    tps = lp // tm

    def body(a_ref, w_ref, b_ref, o_ref, gate_ref):
        i = pl.program_id(0)
        j = pl.program_id(1)
        acc = _dot(a_ref[...], w_ref[j]) + b_ref[...]
        t = (i % tps) * tm + lax.broadcasted_iota(jnp.int32, (tm, 1), 0)
        acc = jnp.where(t >= PAD, acc, 0.0)
        o_ref[...] = _bf(acc)

        @pl.when(j == nj - 1)
        def _():
            gate_ref[...] = acc[:, tn - 128:]

    return _pcall(
        body, name="inproj", grid=(r // tm, nj),
        in_specs=[pl.BlockSpec((tm, d), lambda i, j: (i, 0)), _resident(w3.shape),
                  pl.BlockSpec((1, tn), lambda i, j: (0, j))],
        out_specs=[pl.BlockSpec((tm, tn), lambda i, j: (i, j)), pl.BlockSpec((tm, 128), lambda i, j: (i, 0))],
        out_shape=[jax.ShapeDtypeStruct((r, nj * tn), BF16), jax.ShapeDtypeStruct((r, 128), F32)],
        compiler_params=_cp(("parallel", "arbitrary"), 48),
    )(h0b, w3, bias)


def _mm_tn(a, b, *, name, split=1, colsum=False, tk_want=2112):
    r, m = a.shape
    n = b.shape[1]
    tk = _row_tile(r, tk_want)
    tm = min(m, 1024)
    ns = n // split
    tn = ns
    for cand in (1024, 1152, 640, 512, 128):
        if ns % cand == 0 and cand <= ns:
            tn = cand
            break
    nb = ns // tn
    nk = r // tk

    def body(a_ref, b_ref, o_ref, *rest):
        acc = rest[-1]
        k = pl.program_id(2)

        @pl.when(k == 0)
        def _():
            acc[...] = jnp.zeros_like(acc)

        bt = b_ref[...]
        acc[...] += _dot(_bf(a_ref[...]), _bf(bt), TN)

        @pl.when(k == nk - 1)
        def _():
            o_ref[...] = acc[...]

        if colsum:
            cs_ref = rest[0]

            @pl.when(k == 0)
            def _():
                cs_ref[...] = jnp.zeros_like(cs_ref)

            cs_ref[...] += _colsum(bt.astype(F32))

    out_specs = [pl.BlockSpec((None, tm, tn), lambda i, j, k: (j // nb, i, j % nb))]
    out_shape = [jax.ShapeDtypeStruct((split, m, ns), F32)]
    if colsum:
        assert m == tm
        out_specs.append(pl.BlockSpec((1, tn), lambda i, j, k: (0, j)))
        out_shape.append(jax.ShapeDtypeStruct((1, n), F32))
    res = _pcall(
        body, name=name, grid=(m // tm, n // tn, nk),
        in_specs=[pl.BlockSpec((tk, tm), lambda i, j, k: (k, i)), pl.BlockSpec((tk, tn), lambda i, j, k: (k, j))],
        out_specs=out_specs, out_shape=out_shape,
        scratch_shapes=[pltpu.VMEM((tm, tn), F32)],
        compiler_params=_cp(("parallel", "parallel", "arbitrary"), 56),
    )(a, b)
    return res if colsum else res[0]


def _mm_nt(a, w3, lp, *, name, dep=None):
    r, kdim = a.shape
    nk, n, tk = w3.shape
    assert nk * tk == kdim
    tm = _row_tile(lp, 1056)
    deps = [] if dep is None else [dep]

    def body(a_ref, w_ref, *rest):
        o_ref, acc = rest[-2:]
        k = pl.program_id(1)

        @pl.when(k == 0)
        def _():
            acc[...] = jnp.zeros_like(acc)

        acc[...] += _dot(_bf(a_ref[...]), w_ref[k], NT)

        @pl.when(k == nk - 1)
        def _():
            o_ref[...] = acc[...]

    return _pcall(
        body, name=name, grid=(r // tm, nk),
        in_specs=[pl.BlockSpec((tm, tk), lambda i, k: (i, k)), _resident(w3.shape)]
        + [_const(dp_.shape) for dp_ in deps],
        out_specs=pl.BlockSpec((tm, n), lambda i, k: (i, 0)),
        out_shape=jax.ShapeDtypeStruct((r, n), F32),
        scratch_shapes=[pltpu.VMEM((tm, n), F32)],
        compiler_params=_cp(("parallel", "arbitrary"), 48),
    )(a, w3, *deps)


def _s5_prep(lam_re, lam_im, log_dt, b_re_t, b_im_t):
    g, p = lam_re.shape
    h = b_re_t.shape[0]

    def body(lr_ref, li_ref, ldt_ref, br_ref, bi_ref, pr_ref, pi_ref, bbr_ref, bbi_ref):
        lr, li = lr_ref[...], li_ref[...]
        dt = jnp.exp(ldt_ref[...])
        e = jnp.exp(lr * dt)
        ar, ai = e * jnp.cos(li * dt), e * jnp.sin(li * dt)
        den = lr * lr + li * li
        cr = ((ar - 1.0) * lr + ai * li) / den
        ci = (ai * lr - (ar - 1.0) * li) / den
        br, bi = br_ref[...], bi_ref[...]
        bbr_ref[...] = cr[None] * br - ci[None] * bi
        bbi_ref[...] = cr[None] * bi + ci[None] * br
        xr, xi = ar, ai
        pr_ref[0] = xr
        pi_ref[0] = xi
        for t in range(1, 8):
            xr, xi = xr * ar - xi * ai, xr * ai + xi * ar
            pr_ref[t] = xr
            pi_ref[t] = xi

    sd = jax.ShapeDtypeStruct
    return _pcall(body, name="s5_prep",
                  out_shape=[sd((8, g, p), F32), sd((8, g, p), F32), sd((h, g, p), F32), sd((h, g, p), F32)])(
        lam_re, lam_im, log_dt, b_re_t, b_im_t)


def _s5_prep_bwd(lam_re, lam_im, log_dt, b_re_t, b_im_t, da_re, da_im, dbb_re_t, dbb_im_t):
    g, p = lam_re.shape
    h = b_re_t.shape[0]

    def body(lr_ref, li_ref, ldt_ref, br_ref, bi_ref, dar_ref, dai_ref, dbr_ref, dbi_ref,
             glr_ref, gli_ref, gdt_ref, gbr_ref, gbi_ref):
        lr, li = lr_ref[...], li_ref[...]
        dt = jnp.exp(ldt_ref[...])
        e = jnp.exp(lr * dt)
        ar, ai = e * jnp.cos(li * dt), e * jnp.sin(li * dt)
        den = lr * lr + li * li
        cr = ((ar - 1.0) * lr + ai * li) / den
        ci = (ai * lr - (ar - 1.0) * li) / den
        br, bi = br_ref[...], bi_ref[...]
        gr, gi = dbr_ref[...], dbi_ref[...]
        gbr_ref[...] = gr * cr[None] + gi * ci[None]
        gbi_ref[...] = gi * cr[None] - gr * ci[None]
        gcr = jnp.sum(gr * br + gi * bi, axis=0)
        gci = jnp.sum(gi * br - gr * bi, axis=0)
        ilr, ili = lr / den, -li / den
        gar = dar_ref[...] + gcr * ilr + gci * ili
        gai = dai_ref[...] + gci * ilr - gcr * ili
        qr, qi = cr * ilr - ci * ili, cr * ili + ci * ilr
        glr = -(gcr * qr + gci * qi)
        gli = -(gci * qr - gcr * qi)
        gzr = gar * ar + gai * ai
        gzi = gai * ar - gar * ai
        glr_ref[...] = glr + gzr * dt
        gli_ref[...] = gli + gzi * dt
        gdt_ref[...] = jnp.sum(gzr * lr + gzi * li, axis=1, keepdims=True) * dt

    sd = jax.ShapeDtypeStruct
    return _pcall(body, name="s5_prep_bwd",
                  out_shape=[sd((g, p), F32), sd((g, p), F32), sd((g, 1), F32), sd((h, g, p), F32), sd((h, g, p), F32)])(
        lam_re, lam_im, log_dt, b_re_t, b_im_t, da_re, da_im, dbb_re_t, dbb_im_t)


def _cmul(xr, xi, yr, yi):
    return xr * yr - xi * yi, xr * yi + xi * yr


def _dot5(a, b, dims=NN):
    return _dot(_bf(a), _bf(b), dims)


def _s5_fwd(p3, bk, cre, cim, apow, dskip):
    bsz, lp, _ = p3.shape
    tt = _row_tile(lp, 528, 8)
    nt = lp // tt
    nblk = tt // 8
    hw = 512

    def body(u_ref, bk_ref, cre_ref, cim_ref, ap_ref, d_ref, y_ref, xs_ref, car_ref):
        t = pl.program_id(2)

        @pl.when(t == 0)
        def _():
            car_ref[...] = jnp.zeros_like(car_ref)

        u = u_ref[...].astype(F32)
        xs_ref[...] = _dot5(u, bk_ref[...])
        ap = ap_ref[...]
        apr, api = ap[:, :hw], ap[:, hw:]
        rows = lax.broadcasted_iota(jnp.int32, (8, hw), 0)
        lev = [(d, jnp.where(rows < d, 0.0, jnp.broadcast_to(apr[d - 1:d, :], (8, hw))),
                jnp.where(rows < d, 0.0, jnp.broadcast_to(api[d - 1:d, :], (8, hw)))) for d in (1, 2, 4)]

        def blk(i, carry):
            cr, ci = carry
            off = pl.multiple_of(i * 8, 8)
            x = xs_ref[pl.ds(off, 8), :]
            xr, xi = x[:, :hw], x[:, hw:]
            for d, lr, li in lev:
                mr, mi = _cmul(pltpu.roll(xr, d, 0), pltpu.roll(xi, d, 0), lr, li)
                xr, xi = xr + mr, xi + mi
            mr, mi = _cmul(apr, api, cr, ci)
            xr, xi = xr + mr, xi + mi
            xs_ref[pl.ds(off, 8), :] = jnp.concatenate([xr, xi], axis=1)
            return xr[7:8, :], xi[7:8, :]

        c0 = car_ref[...]
        cr, ci = lax.fori_loop(0, nblk, blk, (c0[0:1, :hw], c0[0:1, hw:]))
        car_ref[...] = jnp.broadcast_to(jnp.concatenate([cr, ci], axis=1), car_ref.shape)
        xs = xs_ref[...]
        y_ref[...] = (_dot5(xs[:, :hw], cre_ref[...]) - _dot5(xs[:, hw:], cim_ref[...])
                      + d_ref[...] * u)

    ub = U_OFF // 128
    return _pcall(
        body, name="s5_fwd", grid=(S5_KCH, bsz, nt),
        in_specs=[pl.BlockSpec((None, tt, 128), lambda k, b, t: (b, t, ub + k)),
                  pl.BlockSpec((None, 128, 2 * hw), lambda k, b, t: (k, 0, 0)),
                  pl.BlockSpec((None, hw, 128), lambda k, b, t: (k, 0, 0)),
                  pl.BlockSpec((None, hw, 128), lambda k, b, t: (k, 0, 0)),
                  pl.BlockSpec((None, 8, 2 * hw), lambda k, b, t: (k, 0, 0)),
                  pl.BlockSpec((1, 128), lambda k, b, t: (0, k))],
        out_specs=[pl.BlockSpec((None, tt, 128), lambda k, b, t: (b, t, k)),
                   pl.BlockSpec((None, None, tt, 2 * hw), lambda k, b, t: (b, k, t, 0))],
        out_shape=[jax.ShapeDtypeStruct((bsz, lp, S5_KCH * 128), F32),
                   jax.ShapeDtypeStruct((bsz, S5_KCH, lp, 2 * hw), F32)],
        scratch_shapes=[pltpu.VMEM((8, 2 * hw), F32)],
        compiler_params=_cp(("parallel", "parallel", "arbitrary"), 40),
    )(p3, bk, cre, cim, apow, dskip)


def _s5_bwd(dp3, p3, dy3, xs, bk, cre, cim, apow_rev, dskip):
    bsz, lp, _ = p3.shape
    tt = _row_tile(lp, 528, 8)
    nt = lp // tt
    nblk = tt // 8
    hw = 512
    tb = tt // 8

    def body(dp_any, u_ref, dy_ref, xs_ref, halo_ref, bkt_ref, cre_ref, cim_ref, ap_ref, d_ref,
             du_ref, dbk_ref, dcre_ref, dcim_ref, da_ref, dd_ref, g_ref, ext_ref, car_ref):
        b = pl.program_id(1)
        t = pl.program_id(2)
        tidx = nt - 1 - t

        @pl.when(t == 0)
        def _():
            car_ref[...] = jnp.zeros_like(car_ref)

        @pl.when((b == 0) & (t == 0))
        def _():
            dbk_ref[...] = jnp.zeros_like(dbk_ref)
            dcre_ref[...] = jnp.zeros_like(dcre_ref)
            dcim_ref[...] = jnp.zeros_like(dcim_ref)
            da_ref[...] = jnp.zeros_like(da_ref)
            dd_ref[...] = jnp.zeros_like(dd_ref)

        u = u_ref[...].astype(F32)
        dy = dy_ref[...]
        g_ref[:, :hw] = _dot5(dy, cre_ref[...])
        g_ref[:, hw:] = -_dot5(dy, cim_ref[...])
        ap = ap_ref[...]
        apr, api = ap[:, :hw], -ap[:, hw:]
        rows = lax.broadcasted_iota(jnp.int32, (8, hw), 0)
        lev = [(d, jnp.where(rows >= 8 - d, 0.0, jnp.broadcast_to(apr[8 - d:9 - d, :], (8, hw))),
                jnp.where(rows >= 8 - d, 0.0, jnp.broadcast_to(api[8 - d:9 - d, :], (8, hw)))) for d in (1, 2, 4)]

        def blk(i, carry):
            cr, ci = carry
            off = pl.multiple_of((nblk - 1 - i) * 8, 8)
            x = g_ref[pl.ds(off, 8), :]
            xr, xi = x[:, :hw], x[:, hw:]
            for d, lr, li in lev:
                mr, mi = _cmul(pltpu.roll(xr, 8 - d, 0), pltpu.roll(xi, 8 - d, 0), lr, li)
                xr, xi = xr + mr, xi + mi
            mr, mi = _cmul(apr, api, cr, ci)
            xr, xi = xr + mr, xi + mi
            g_ref[pl.ds(off, 8), :] = jnp.concatenate([xr, xi], axis=1)
            return xr[0:1, :], xi[0:1, :]

        c0 = car_ref[...]
        cr, ci = lax.fori_loop(0, nblk, blk, (c0[0:1, :hw], c0[0:1, hw:]))
        car_ref[...] = jnp.broadcast_to(jnp.concatenate([cr, ci], axis=1), car_ref.shape)

        gg = g_ref[...]
        du = _dot5(gg, bkt_ref[...]) + d_ref[...] * dy
        trow = tidx * tt + lax.broadcasted_iota(jnp.int32, (tt, 1), 0)
        du_ref[...] = jnp.where(trow >= PAD, du, 0.0).astype(du_ref.dtype)
        dbk_ref[...] += _dot5(u, gg, TN)
        xsv = xs_ref[...]
        dcre_ref[...] += _dot5(dy, xsv[:, :hw], TN)
        dcim_ref[...] -= _dot5(dy, xsv[:, hw:], TN)
        dd_ref[...] += _colsum(dy * u)
        ext_ref[0:8, :] = jnp.where(tidx == 0, 0.0, halo_ref[...])
        ext_ref[8:, :] = xsv
        xp = ext_ref[pl.ds(7, tt), :]
        gr, gi, pr, pi = gg[:, :hw], gg[:, hw:], xp[:, :hw], xp[:, hw:]
        da_ref[:, :hw] += _colsum(gr * pr + gi * pi)
        da_ref[:, hw:] += _colsum(gi * pr - gr * pi)

    ub = U_OFF // 128
    sd = jax.ShapeDtypeStruct
    rt = lambda t: nt - 1 - t
    tr = lambda a: jnp.swapaxes(a, 1, 2)
    res = _pcall(
        body, name="s5_bwd", grid=(S5_KCH, bsz, nt),
        in_specs=[pl.BlockSpec(memory_space=pl.ANY),
                  pl.BlockSpec((None, tt, 128), lambda k, b, t: (b, rt(t), ub + k)),
                  pl.BlockSpec((None, tt, 128), lambda k, b, t: (b, rt(t), k)),
                  pl.BlockSpec((None, None, tt, 2 * hw), lambda k, b, t: (b, k, rt(t), 0)),
                  pl.BlockSpec((None, None, 8, 2 * hw), lambda k, b, t: (b, k, jnp.maximum(rt(t) * tb - 1, 0), 0)),
                  pl.BlockSpec((None, 2 * hw, 128), lambda k, b, t: (k, 0, 0)),
                  pl.BlockSpec((None, 128, hw), lambda k, b, t: (k, 0, 0)),
                  pl.BlockSpec((None, 128, hw), lambda k, b, t: (k, 0, 0)),
                  pl.BlockSpec((None, 8, 2 * hw), lambda k, b, t: (k, 0, 0)),
                  pl.BlockSpec((1, 128), lambda k, b, t: (0, k))],
        out_specs=[pl.BlockSpec((None, tt, 128), lambda k, b, t: (b, rt(t), ub + k)),
                   pl.BlockSpec((None, 128, 2 * hw), lambda k, b, t: (k, 0, 0)),
                   pl.BlockSpec((None, 128, hw), lambda k, b, t: (k, 0, 0)),
                   pl.BlockSpec((None, 128, hw), lambda k, b, t: (k, 0, 0)),
                   pl.BlockSpec((None, 1, 2 * hw), lambda k, b, t: (k, 0, 0)),
                   pl.BlockSpec((1, 128), lambda k, b, t: (0, k))],
        out_shape=[sd(dp3.shape, dp3.dtype), sd((S5_KCH, 128, 2 * hw), F32), sd((S5_KCH, 128, hw), F32),
                   sd((S5_KCH, 128, hw), F32), sd((S5_KCH, 1, 2 * hw), F32), sd((1, S5_KCH * 128), F32)],
        scratch_shapes=[pltpu.VMEM((tt, 2 * hw), F32), pltpu.VMEM((tt + 8, 2 * hw), F32), pltpu.VMEM((8, 2 * hw), F32)],
        input_output_aliases={0: 0},
        compiler_params=_cp(("arbitrary", "arbitrary", "arbitrary"), 48),
    )(dp3, p3, dy3, xs, xs, tr(bk), tr(cre), tr(cim), apow_rev, dskip)
    return res[0], res[1], tr(res[2]), tr(res[3]), res[4], res[5]


_G0 = math.sqrt(2.0 / math.pi)
_G1 = 0.044715


def _gelu(y):
    return 0.5 * y * (1.0 + jnp.tanh(_G0 * (y + _G1 * y * y * y)))


def _gelu_grad(y):
    th = jnp.tanh(_G0 * (y + _G1 * y * y * y))
    return 0.5 * (1.0 + th) + 0.5 * y * (1.0 - th * th) * _G0 * (1.0 + 3.0 * _G1 * y * y)


def _glu_fwd(y_s5, wglu_g, lp):
    r, w = y_s5.shape
    tm = _row_tile(lp, 416)
    cw = wglu_g.shape[2]

    def body(y_ref, w_ref, gy_ref, z_ref, o_ref):
        gy = _bf(_gelu(y_ref[...]))
        gy_ref[...] = gy
        zs = [_dot(gy, w_ref[s]) for s in range(4)]
        for s in range(4):
            z_ref[:, s * cw:(s + 1) * cw] = zs[s]
        o_ref[:, :cw] = _bf(zs[0] * _sig(zs[2]))
        o_ref[:, cw:] = _bf(zs[1] * _sig(zs[3]))

    sd = jax.ShapeDtypeStruct
    return _pcall(
        body, name="glu_fwd", grid=(r // tm,),
        in_specs=[pl.BlockSpec((tm, w), lambda i: (i, 0)), _resident(wglu_g.shape)],
        out_specs=[pl.BlockSpec((tm, w), lambda i: (i, 0)), pl.BlockSpec((tm, 4 * cw), lambda i: (i, 0)),
                   pl.BlockSpec((tm, 2 * cw), lambda i: (i, 0))],
        out_shape=[sd((r, w), BF16), sd((r, 4 * cw), F32), sd((r, 2 * cw), BF16)],
        compiler_params=_cp(("parallel",), 40),
    )(y_s5, wglu_g)


def _glu_bwd(dyg, z, y_s5, wglu_g, lp):
    r, w = y_s5.shape
    tm = _row_tile(lp, 416)
    cw = wglu_g.shape[2]

    def body(d_ref, z_ref, y_ref, w_ref, dz_ref, dy_ref):
        d = d_ref[...].astype(F32)
        zz = z_ref[...]
        acc = jnp.zeros((tm, w), F32)
        for s in range(2):
            z1 = zz[:, s * cw:(s + 1) * cw]
            sg = _sig(zz[:, (2 + s) * cw:(3 + s) * cw])
            dd = d[:, s * cw:(s + 1) * cw]
            dz1 = _bf(dd * sg)
            dz2 = _bf(dd * z1 * sg * (1.0 - sg))
            dz_ref[:, s * cw:(s + 1) * cw] = dz1
            dz_ref[:, (2 + s) * cw:(3 + s) * cw] = dz2
            acc += _dot(dz1, w_ref[s], NT) + _dot(dz2, w_ref[2 + s], NT)
        dy_ref[...] = acc * _gelu_grad(y_ref[...])

    sd = jax.ShapeDtypeStruct
    return _pcall(
        body, name="glu_bwd", grid=(r // tm,),
        in_specs=[pl.BlockSpec((tm, 2 * cw), lambda i: (i, 0)), pl.BlockSpec((tm, 4 * cw), lambda i: (i, 0)),
                  pl.BlockSpec((tm, w), lambda i: (i, 0)), _resident(wglu_g.shape)],
        out_specs=[pl.BlockSpec((tm, 4 * cw), lambda i: (i, 0)), pl.BlockSpec((tm, w), lambda i: (i, 0))],
        out_shape=[sd((r, 4 * cw), BF16), sd((r, w), F32)],
        compiler_params=_cp(("parallel",), 40),
    )(dyg, z, y_s5, wglu_g)


def _conv_fwd(p3, cw, cb):
    bsz, lp, _ = p3.shape
    tt = _row_tile(lp, 416)
    nt = lp // tt
    tb = tt // 8
    c = cw.shape[1]
    qb = Q_OFF // c

    hr = HALO_ROWS
    off = hr - (CONV_W - 1)

    def body(x_ref, halo_ref, w_ref, b_ref, pre_ref, act_ref, ext_ref):
        t = pl.program_id(1)
        ext_ref[0:hr, :] = jnp.where(t == 0, 0.0, halo_ref[...].astype(F32))
        ext_ref[hr:, :] = x_ref[...].astype(F32)
        w = w_ref[...]
        acc = b_ref[...] + w[0:1, :] * ext_ref[pl.ds(off, tt), :]
        for j in range(1, CONV_W):
            acc = acc + w[j:j + 1, :] * ext_ref[pl.ds(off + j, tt), :]
        pre_ref[...] = acc
        act_ref[...] = acc * _sig(acc)

    sd = jax.ShapeDtypeStruct
    return _pcall(
        body, name="conv_fwd", grid=(bsz, nt),
        in_specs=[pl.BlockSpec((None, tt, c), lambda b, t: (b, t, qb)),
                  pl.BlockSpec((None, hr, c), lambda b, t: (b, jnp.maximum(t * (tt // hr) - 1, 0), qb)),
                  _const((CONV_W, c)), _const((1, c))],
        out_specs=[pl.BlockSpec((None, tt, c), lambda b, t: (b, t, 0))] * 2,
        out_shape=[sd((bsz, lp, c), F32)] * 2,
        scratch_shapes=[pltpu.VMEM((tt + hr, c), F32)],
        compiler_params=_cp(("parallel", "parallel")),
    )(p3, p3, cw, cb)


def _conv_bwd(dp3, p3, dact3, pre3, cw):
    bsz, lp, _ = p3.shape
    tt = _row_tile(lp, 416)
    nt = lp // tt
    tb = tt // 8
    c = cw.shape[1]
    qb = Q_OFF // c

    hr = HALO_ROWS
    off = hr - (CONV_W - 1)

    def silu_grad(x):
        s = _sig(x)
        return s * (1.0 + x * (1.0 - s))

    def body(dp_any, x_ref, xh_ref, d_ref, dh_ref, pre_ref, preh_ref, w_ref, o_ref, dw_ref, db_ref, ext_ref, dext_ref):
        b = pl.program_id(0)
        t = pl.program_id(1)

        @pl.when((b == 0) & (t == 0))
        def _():
            dw_ref[...] = jnp.zeros_like(dw_ref)
            db_ref[...] = jnp.zeros_like(db_ref)

        dc = d_ref[...] * silu_grad(pre_ref[...])
        dch = jnp.where(t == nt - 1, 0.0, dh_ref[...] * silu_grad(preh_ref[...]))
        dext_ref[0:tt, :] = dc
        dext_ref[tt:, :] = dch
        ext_ref[0:hr, :] = jnp.where(t == 0, 0.0, xh_ref[...].astype(F32))
        ext_ref[hr:, :] = x_ref[...].astype(F32)
        w = w_ref[...]
        acc = w[CONV_W - 1:CONV_W, :] * dc
        for j in range(CONV_W - 1):
            acc = acc + w[j:j + 1, :] * dext_ref[pl.ds(CONV_W - 1 - j, tt), :]
        trow = t * tt + lax.broadcasted_iota(jnp.int32, (tt, 1), 0)
        o_ref[...] = jnp.where(trow >= PAD, acc, 0.0).astype(o_ref.dtype)
        db_ref[...] += _colsum(dc)
        for j in range(CONV_W):
            dw_ref[j:j + 1, :] += _colsum(dc * ext_ref[pl.ds(off + j, tt), :])

    sd = jax.ShapeDtypeStruct
    nxt = lambda t: jnp.minimum((t + 1) * tb, lp // 8 - 1)
    return _pcall(
        body, name="conv_bwd", grid=(bsz, nt),
        in_specs=[pl.BlockSpec(memory_space=pl.ANY),
                  pl.BlockSpec((None, tt, c), lambda b, t: (b, t, qb)),
                  pl.BlockSpec((None, hr, c), lambda b, t: (b, jnp.maximum(t * (tt // hr) - 1, 0), qb)),
                  pl.BlockSpec((None, tt, c), lambda b, t: (b, t, 0)),
                  pl.BlockSpec((None, 8, c), lambda b, t: (b, nxt(t), 0)),
                  pl.BlockSpec((None, tt, c), lambda b, t: (b, t, 0)),
                  pl.BlockSpec((None, 8, c), lambda b, t: (b, nxt(t), 0)),
                  _const((CONV_W, c))],
        out_specs=[pl.BlockSpec((None, tt, c), lambda b, t: (b, t, qb)), _const((CONV_W, c)), _const((1, c))],
        out_shape=[sd(dp3.shape, dp3.dtype), sd((CONV_W, c), F32), sd((1, c), F32)],
        scratch_shapes=[pltpu.VMEM((tt + hr, c), F32), pltpu.VMEM((tt + 8, c), F32)],
        input_output_aliases={0: 0},
        compiler_params=_cp(("arbitrary", "arbitrary")),
    )(dp3, p3, p3, dact3, dact3, pre3, pre3, cw)


def _mlstm_gates(g, h_idx, c_idx, lc):
    lane = lax.broadcasted_iota(jnp.int32, g.shape, 1)
    i_col = jnp.sum(jnp.where(lane == h_idx, g, 0.0), axis=1, keepdims=True)
    f_col = jnp.sum(jnp.where(lane == M_HEADS + h_idx, g, 0.0), axis=1, keepdims=True)
    row = lax.broadcasted_iota(jnp.int32, (lc, 1), 0)
    valid = (c_idx * lc + row) >= PAD
    li = jnp.where(valid, i_col, NEG)
    lf = jnp.where(valid, jnp.minimum(f_col, 0.0) - jnp.log(1.0 + jnp.exp(-jnp.abs(f_col))), 0.0)
    r2 = lax.broadcasted_iota(jnp.int32, (lc, lc), 0)
    c2 = lax.broadcasted_iota(jnp.int32, (lc, lc), 1)
    eye = r2 == c2
    tril = r2 >= c2
    to_row = lambda col: jnp.sum(jnp.where(eye, col, 0.0), axis=0, keepdims=True)
    lf_row = to_row(lf)
    b_col = jnp.sum(jnp.where(tril, lf_row, 0.0), axis=1, keepdims=True)
    b_row = to_row(b_col)
    li_row = to_row(li)
    d_mat = jnp.where(tril, b_col - b_row + li_row, NEG)
    return dict(f_col=f_col, valid=valid, li=li, b_col=b_col, d_mat=d_mat, eye=eye, r2=r2, c2=c2, row=row,
                to_row=to_row)


def _mlstm_chunk(q, ks, v, gq, c_st, n_st, m_st, lc):
    b_col, d_mat = gq["b_col"], gq["d_mat"]
    m_inter = b_col + m_st
    m_row = jnp.maximum(m_inter, jnp.max(d_mat, axis=1, keepdims=True))
    w_intra = jnp.exp(d_mat - m_row)
    w_inter = jnp.exp(m_inter - m_row)
    qb, kb, vb, cb = _bf(q), _bf(ks), _bf(v), _bf(c_st)
    s = _dot(qb, kb, NT) * w_intra
    qc = _dot(qb, cb)
    num = _dot(_bf(s), vb) + w_inter * qc
    qn = jnp.sum(q * n_st, axis=1, keepdims=True)
    den = jnp.sum(s, axis=1, keepdims=True) + w_inter * qn
    e = jnp.exp(-m_row)
    nn = jnp.maximum(jnp.abs(den), e)
    b_last = b_col[lc - 1:lc, :]
    g_log = b_last - b_col + gq["li"]
    m_new = jnp.maximum(b_last + m_st, jnp.max(g_log, axis=0, keepdims=True))
    w_k = jnp.exp(g_log - m_new)
    decay = jnp.exp(b_last + m_st - m_new)
    return dict(w_intra=w_intra, w_inter=w_inter, qb=qb, kb=kb, vb=vb, cb=cb, s=s, qc=qc, num=num, qn=qn, den=den,
                e=e, nn=nn, m_new=m_new, w_k=w_k, decay=decay)


def _chunks_per_step(nc):
    return max(c for c in (3, 2, 1) if nc % c == 0)


def _mlstm_fwd(qk3, p3, pg3):
    bsz, lp, _ = p3.shape
    lc = M_CHUNK
    nc = lp // lc
    dk, dv = 128, 256
    scale = dk ** -0.5

    cps = _chunks_per_step(nc)
    rows = cps * lc

    def body(q_ref, k_ref, v_ref, g_ref, h_ref, cs_ref, ns_ref, ms_ref, c_sc, n_sc, m_sc):
        st = pl.program_id(1)

        @pl.when(st == 0)
        def _():
            c_sc[...] = jnp.zeros_like(c_sc)
            n_sc[...] = jnp.zeros_like(n_sc)
            m_sc[...] = jnp.zeros_like(m_sc)

        for j in range(cps):
            rs = slice(j * lc, (j + 1) * lc)
            g = g_ref[rs, :]
            for hh in range(M_HEADS):
                c_st, n_st, m_all = c_sc[hh], n_sc[hh], m_sc[hh]
                cs_ref[hh, j] = c_st
                ns_ref[hh, j] = n_st
                ms_ref[hh, j] = m_all
                m_st = m_all[:, 0:1]
                q = q_ref[rs, hh * dk:(hh + 1) * dk]
                ks = k_ref[rs, hh * dk:(hh + 1) * dk] * scale
                v = v_ref[rs, hh * dv:(hh + 1) * dv]
                gq = _mlstm_gates(g, hh, st * cps + j, lc)
                f = _mlstm_chunk(q, ks, v, gq, c_st, n_st, m_st, lc)
                h_ref[rs, hh * dv:(hh + 1) * dv] = _bf(f["num"] / f["nn"])
                kw = ks * f["w_k"]
                c_sc[hh] = f["decay"] * c_st + _dot(_bf(kw), f["vb"], TN)
                n_sc[hh] = f["decay"] * n_st + _colsum(kw)
                m_sc[hh] = jnp.broadcast_to(f["m_new"], (1, 128))

    sd = jax.ShapeDtypeStruct
    nh = M_HEADS
    return _pcall(
        body, name="mlstm_fwd", grid=(bsz, nc // cps),
        in_specs=[pl.BlockSpec((None, rows, nh * dk), lambda b, c: (b, c, 0)),
                  pl.BlockSpec((None, rows, nh * dk), lambda b, c: (b, c, 1)),
                  pl.BlockSpec((None, rows, nh * dv), lambda b, c: (b, c, V_OFF // (nh * dv))),
                  pl.BlockSpec((None, rows, 128), lambda b, c: (b, c, 0))],
        out_specs=[pl.BlockSpec((None, rows, nh * dv), lambda b, c: (b, c, 0)),
                   pl.BlockSpec((None, nh, cps, dk, dv), lambda b, c: (b, 0, c, 0, 0)),
                   pl.BlockSpec((None, nh, cps, 1, dk), lambda b, c: (b, 0, c, 0, 0)),
                   pl.BlockSpec((None, nh, cps, 1, 128), lambda b, c: (b, 0, c, 0, 0))],
        out_shape=[sd((bsz, lp, nh * dv), BF16), sd((bsz, nh, nc, dk, dv), F32),
                   sd((bsz, nh, nc, 1, dk), F32), sd((bsz, nh, nc, 1, 128), F32)],
        scratch_shapes=[pltpu.VMEM((nh, dk, dv), F32), pltpu.VMEM((nh, 1, dk), F32), pltpu.VMEM((nh, 1, 128), F32)],
        compiler_params=_cp(("parallel", "arbitrary")),
    )(qk3, qk3, p3, pg3)


def _mlstm_bwd(dp3, qk3, p3, pg3, dh3, cs, ns, ms):
    bsz, lp, _ = p3.shape
    lc = M_CHUNK
    nc = lp // lc
    dk, dv = 128, 256
    scale = dk ** -0.5

    cps = _chunks_per_step(nc)
    nst = nc // cps
    rows = cps * lc

    def body(dp_any, q_ref, k_ref, v_ref, g_ref, dh_ref, cs_ref, ns_ref, ms_ref,
             dv_ref, dqk_ref, dg_ref, dc_sc, dn_sc):
        t = pl.program_id(1)
        st = nst - 1 - t

        @pl.when(t == 0)
        def _():
            dc_sc[...] = jnp.zeros_like(dc_sc)
            dn_sc[...] = jnp.zeros_like(dn_sc)

        lane = lax.broadcasted_iota(jnp.int32, (lc, 128), 1)
        for j in reversed(range(cps)):
            rs = slice(j * lc, (j + 1) * lc)
            g = g_ref[rs, :]
            dgate = jnp.zeros((lc, 128), F32)
            for hh in range(M_HEADS):
                dgate = head(hh, j, rs, st * cps + j, g, lane, dgate, q_ref, k_ref, v_ref, dh_ref, cs_ref, ns_ref,
                             ms_ref, dv_ref, dqk_ref, dc_sc, dn_sc)
            dg_ref[rs, :] = dgate.astype(dg_ref.dtype)

    def head(hh, j, sl, c, g, lane, dgate, q_ref, k_ref, v_ref, dh_ref, cs_ref, ns_ref, ms_ref, dv_ref, dqk_ref,
             dc_sc, dn_sc):
        c_st, n_st = cs_ref[hh, j], ns_ref[hh, j]
        m_st = ms_ref[hh, j][:, 0:1]
        q = q_ref[sl, hh * dk:(hh + 1) * dk]
        ks = k_ref[sl, hh * dk:(hh + 1) * dk] * scale
        v = v_ref[sl, hh * dv:(hh + 1) * dv]
        dh = dh_ref[sl, hh * dv:(hh + 1) * dv]
        gq = _mlstm_gates(g, hh, c, lc)
        f = _mlstm_chunk(q, ks, v, gq, c_st, n_st, m_st, lc)
        eye, r2, c2, row, valid = gq["eye"], gq["r2"], gq["c2"], gq["row"], gq["valid"]
        w_intra, w_inter, s, nn, den = f["w_intra"], f["w_inter"], f["s"], f["nn"], f["den"]
        qb, kb, vb, cb, w_k, decay = f["qb"], f["kb"], f["vb"], f["cb"], f["w_k"], f["decay"]
        d_c, d_n = dc_sc[hh], dn_sc[hh]
        d_cb = _bf(d_c)

        hout = f["num"] / nn
        dnum = dh / nn
        d_nn = -jnp.sum(dh * hout, axis=1, keepdims=True) / nn
        dden = jnp.where(jnp.abs(den) > f["e"], d_nn * jnp.sign(den), 0.0)
        wdnum = w_inter * dnum
        wdden = w_inter * dden
        ds = _dot(_bf(dnum), vb, NT) + dden
        dsw = _bf(ds * w_intra)
        dq = _dot(dsw, kb) + _dot(_bf(wdnum), cb, NT) + wdden * n_st
        dkw = _dot(vb, d_cb, NT) + d_n
        dks = _dot(dsw, qb, TN) + dkw * w_k
        kw = ks * w_k
        dvv = _dot(_bf(s), _bf(dnum), TN) + _dot(_bf(kw), d_cb)
        dd = ds * s
        rs = jnp.sum(dd, axis=1, keepdims=True)
        cs_col = jnp.sum(jnp.where(eye, jnp.sum(dd, axis=0, keepdims=True), 0.0), axis=1, keepdims=True)
        dwi = jnp.sum(dnum * f["qc"], axis=1, keepdims=True) + dden * f["qn"]
        db = rs - cs_col + dwi * w_inter
        dli = cs_col
        ddecay = jnp.sum(jnp.sum(d_c * c_st, axis=1, keepdims=True), axis=0, keepdims=True) \
            + jnp.sum(d_n * n_st, axis=1, keepdims=True)
        dgl = jnp.sum(dkw * ks, axis=1, keepdims=True) * w_k
        dblast = ddecay * decay + jnp.sum(dgl, axis=0, keepdims=True)
        db = db - dgl + jnp.where(row == lc - 1, dblast, 0.0)
        dli = dli + dgl
        db_row = gq["to_row"](db)
        dlf = jnp.sum(jnp.where(c2 >= r2, db_row, 0.0), axis=1, keepdims=True)
        dlf = jnp.where(valid, dlf, 0.0)
        dgate = jnp.where(lane == hh, jnp.where(valid, dli, 0.0), dgate)
        dgate = jnp.where(lane == M_HEADS + hh, dlf / (1.0 + jnp.exp(gq["f_col"])), dgate)
        dqk_ref[sl, hh * dk:(hh + 1) * dk] = dq
        dqk_ref[sl, (M_HEADS + hh) * dk:(M_HEADS + hh + 1) * dk] = dks * scale
        dv_ref[sl, hh * dv:(hh + 1) * dv] = dvv.astype(dv_ref.dtype)
        dc_sc[hh] = decay * d_c + _dot(qb, _bf(wdnum), TN)
        dn_sc[hh] = decay * d_n + _colsum(q * wdden)
        return dgate

    sd = jax.ShapeDtypeStruct
    nh = M_HEADS
    rc = lambda c: nst - 1 - c
    return _pcall(
        body, name="mlstm_bwd", grid=(bsz, nst),
        in_specs=[pl.BlockSpec(memory_space=pl.ANY),
                  pl.BlockSpec((None, rows, nh * dk), lambda b, c: (b, rc(c), 0)),
                  pl.BlockSpec((None, rows, nh * dk), lambda b, c: (b, rc(c), 1)),
                  pl.BlockSpec((None, rows, nh * dv), lambda b, c: (b, rc(c), V_OFF // (nh * dv))),
                  pl.BlockSpec((None, rows, 128), lambda b, c: (b, rc(c), 0)),
                  pl.BlockSpec((None, rows, nh * dv), lambda b, c: (b, rc(c), 0)),
                  pl.BlockSpec((None, nh, cps, dk, dv), lambda b, c: (b, 0, rc(c), 0, 0)),
                  pl.BlockSpec((None, nh, cps, 1, dk), lambda b, c: (b, 0, rc(c), 0, 0)),
                  pl.BlockSpec((None, nh, cps, 1, 128), lambda b, c: (b, 0, rc(c), 0, 0))],
        out_specs=[pl.BlockSpec((None, rows, nh * dv), lambda b, c: (b, rc(c), V_OFF // (nh * dv))),
                   pl.BlockSpec((None, rows, 2 * nh * dk), lambda b, c: (b, rc(c), 0)),
                   pl.BlockSpec((None, rows, 128), lambda b, c: (b, rc(c), 0))],
        out_shape=[sd(dp3.shape, dp3.dtype), sd((bsz, lp, 2 * nh * dk), F32), sd((bsz, lp, 128), dp3.dtype)],
        scratch_shapes=[pltpu.VMEM((nh, dk, dv), F32), pltpu.VMEM((nh, 1, dk), F32)],
        input_output_aliases={0: 0},
        compiler_params=_cp(("arbitrary", "arbitrary")),
    )(dp3, qk3, qk3, p3, pg3, dh3, cs, ns, ms)


def _headnorm(x):
    dv = x.shape[1] // M_HEADS
    xh, rs = [], []
    for h in range(M_HEADS):
        xx = x[:, h * dv:(h + 1) * dv]
        mu = jnp.mean(xx, axis=-1, keepdims=True)
        xc = xx - mu
        rstd = lax.rsqrt(jnp.mean(xc * xc, axis=-1, keepdims=True) + LN_EPS)
        xh.append(xc * rstd)
        rs.append(rstd)
    return jnp.concatenate(xh, axis=1), rs


def _mix_fwd(hm, p, ys5g, h0, gn, wmo_bf, wo_bf, lp):
    r, d = hm.shape
    tm = _row_tile(lp, 384)

    def body(hm_ref, o_ref, gs_ref, gm_ref, ys_ref, h0_ref, gn_ref, wmo_ref, wo_ref,
             ymin_ref, mix_ref, r1_ref):
        xhat, _ = _headnorm(hm_ref[...].astype(F32))
        ymin = _bf(_sig(o_ref[...].astype(F32)) * (xhat * gn_ref[...]))
        ymin_ref[...] = ymin
        ym = _dot(ymin, wmo_ref[...])
        mix = _bf(_sig(gs_ref[...].astype(F32)) * ys_ref[...].astype(F32) + _sig(gm_ref[...].astype(F32)) * ym)
        mix_ref[...] = mix
        r1_ref[...] = ALPHA * h0_ref[...] + _dot(mix, wo_ref[...])

    sd = jax.ShapeDtypeStruct
    row = pl.BlockSpec((tm, d), lambda i: (i, 0))
    return _pcall(
        body, name="mix_fwd", grid=(r // tm,),
        in_specs=[row, pl.BlockSpec((tm, d), lambda i: (i, O_OFF // d)), pl.BlockSpec((tm, d), lambda i: (i, GS_OFF // d)),
                  pl.BlockSpec((tm, d), lambda i: (i, GM_OFF // d)), row, row, _const((1, d)),
                  _resident((d, d)), _resident((d, d))],
        out_specs=[row] * 3,
        out_shape=[sd((r, d), BF16), sd((r, d), BF16), sd((r, d), F32)],
        compiler_params=_cp(("parallel",), 48),
    )(hm, p, p, p, ys5g, h0, gn, wmo_bf, wo_bf)


def _mix_bwd(dr1, wo_bf, wmo_bf, p, ys5g, ymin, hm, gn, lp):
    r, d = hm.shape
    tm = _row_tile(lp, 384)
    dv = d // M_HEADS

    def body(dr1_ref, wo_ref, wmo_ref, o_ref, gs_ref, gm_ref, ys_ref, ym_ref, hm_ref, gn_ref,
             dp_ref, dys_ref, dym_ref, dhm_ref, dgn_ref):
        i = pl.program_id(0)

        @pl.when(i == 0)
        def _():
            dgn_ref[...] = jnp.zeros_like(dgn_ref)

        dmix = _dot(_bf(dr1_ref[...]), wo_ref[...], NT)
        sgs, sgm, so = (_sig(gs_ref[...].astype(F32)), _sig(gm_ref[...].astype(F32)), _sig(o_ref[...].astype(F32)))
        dys_ref[...] = _bf(dmix * sgs)
        dp_ref[:, d:2 * d] = _bf(dmix * ys_ref[...].astype(F32) * sgs * (1.0 - sgs))
        dym = dmix * sgm
        dym_ref[...] = _bf(dym)
        ym = _dot(ym_ref[...], wmo_ref[...])
        dp_ref[:, 2 * d:3 * d] = _bf(dmix * ym * sgm * (1.0 - sgm))
        dymin = _dot(_bf(dym), wmo_ref[...], NT)
        xhat, rs = _headnorm(hm_ref[...].astype(F32))
        gn_ = gn_ref[...]
        dp_ref[:, 0:d] = _bf(dymin * (xhat * gn_) * so * (1.0 - so))
        dhn = dymin * so
        dgn_ref[...] += _colsum(dhn * xhat)
        dxh = dhn * gn_
        for h in range(M_HEADS):
            sl = slice(h * dv, (h + 1) * dv)
            a, xh = dxh[:, sl], xhat[:, sl]
            m1 = jnp.mean(a, axis=-1, keepdims=True)
            m2 = jnp.mean(a * xh, axis=-1, keepdims=True)
            dhm_ref[:, sl] = rs[h] * (a - m1 - xh * m2)

    sd = jax.ShapeDtypeStruct
    row = pl.BlockSpec((tm, d), lambda i: (i, 0))
    vec = _const((1, d))
    return _pcall(
        body, name="mix_bwd", grid=(r // tm,),
        in_specs=[row, _resident((d, d)), _resident((d, d)),
                  pl.BlockSpec((tm, d), lambda i: (i, O_OFF // d)), pl.BlockSpec((tm, d), lambda i: (i, GS_OFF // d)),
                  pl.BlockSpec((tm, d), lambda i: (i, GM_OFF // d)), row, row, row, vec],
        out_specs=[pl.BlockSpec((tm, 3 * d), lambda i: (i, 0)), row, row, row, vec],
        out_shape=[sd((r, NP), BF16), sd((r, d), BF16), sd((r, d), BF16), sd((r, d), F32), sd((1, d), F32)],
        compiler_params=_cp(("arbitrary",), 56),
    )(dr1, wo_bf, wmo_bf, p, p, p, ys5g, ymin, hm, gn)


def _mlp_fwd(r1, tgt, g1, b1, wup_g, wdn_bf, bup, g2, b2, lp):
    r, d = r1.shape
    tm = _row_tile(lp, 352)
    tps = lp // tm
    nf = wup_g.shape[0]

    def body(r1_ref, t_ref, g1_ref, b1_ref, wup_ref, wdn_ref, bup_ref, g2_ref, b2_ref,
             dr2_ref, h1b_ref, act_ref, loss_ref, dg2_ref, db2_ref):
        i = pl.program_id(0)

        @pl.when(i == 0)
        def _():
            loss_ref[...] = jnp.zeros_like(loss_ref)
            dg2_ref[...] = jnp.zeros_like(dg2_ref)
            db2_ref[...] = jnp.zeros_like(db2_ref)

        h1, _, _ = _ln_fwd(r1_ref[...], g1_ref[...], b1_ref[...])
        h1b = _bf(h1)
        h1b_ref[...] = h1b
        ff = jnp.zeros((tm, d), F32)
        for s in range(nf):
            up = _dot(h1b, wup_ref[s]) + bup_ref[:, s * d:(s + 1) * d]
            a = jnp.maximum(up, 0.0)
            a = _bf(a * a)
            act_ref[:, s * d:(s + 1) * d] = a
            ff = ff + _dot(a, wdn_ref[s * d:(s + 1) * d, :])
        r2 = ALPHA * h1 + ff
        g2 = g2_ref[...]
        y, xhat, rstd = _ln_fwd(r2, g2, b2_ref[...])
        t = (i % tps) * tm + lax.broadcasted_iota(jnp.int32, (tm, 1), 0)
        diff = jnp.where(t >= PAD + N_META, y - t_ref[...], 0.0)
        loss_ref[...] += 0.5 / d * jnp.sum(jnp.sum(diff * diff, axis=1, keepdims=True), axis=0, keepdims=True)
        dy = diff * (1.0 / d)
        dg2_ref[...] += _colsum(dy * xhat)
        db2_ref[...] += _colsum(dy)
        dr2_ref[...] = _ln_bwd(dy, xhat, rstd, g2)

    sd = jax.ShapeDtypeStruct
    row = pl.BlockSpec((tm, d), lambda i: (i, 0))
    vec = _const((1, d))
    return _pcall(
        body, name="mlp_fwd", grid=(r // tm,),
        in_specs=[row, row, vec, vec, _resident(wup_g.shape), _resident(wdn_bf.shape), _const((1, nf * d)), vec, vec],
        out_specs=[row, row, pl.BlockSpec((tm, nf * d), lambda i: (i, 0)), _const((1, 128)), vec, vec],
        out_shape=[sd((r, d), F32), sd((r, d), BF16), sd((r, nf * d), BF16), sd((1, 128), F32), sd((1, d), F32),
                   sd((1, d), F32)],
        compiler_params=_cp(("arbitrary",), 56),
    )(r1, tgt, g1, b1, wup_g, wdn_bf, bup, g2, b2)


def _mlp_bwd(h1b, dr2, r1, g1, wup_g, wdn_bf, bup, lp):
    r, d = h1b.shape
    tm = _row_tile(lp, 352)
    nf = wup_g.shape[0]

    def body(h1_ref, dr2_ref, r1_ref, g1_ref, wup_ref, wdn_ref, bup_ref, dr1_ref, dup_ref, dbup_ref, dg1_ref, db1_ref):
        i = pl.program_id(0)

        @pl.when(i == 0)
        def _():
            dbup_ref[...] = jnp.zeros_like(dbup_ref)
            dg1_ref[...] = jnp.zeros_like(dg1_ref)
            db1_ref[...] = jnp.zeros_like(db1_ref)

        h1b = h1_ref[...]
        dr2 = dr2_ref[...]
        dr2b = _bf(dr2)
        acc = ALPHA * dr2
        for s in range(nf):
            up = _dot(h1b, wup_ref[s]) + bup_ref[:, s * d:(s + 1) * d]
            dact = _dot(dr2b, wdn_ref[s * d:(s + 1) * d, :], NT)
            dup = dact * (2.0 * jnp.maximum(up, 0.0))
            dbup_ref[:, s * d:(s + 1) * d] += _colsum(dup)
            dupb = _bf(dup)
            dup_ref[:, s * d:(s + 1) * d] = dupb
            acc = acc + _dot(dupb, wup_ref[s], NT)
        g1 = g1_ref[...]
        _, xhat1, rstd1 = _ln_fwd(r1_ref[...], g1, 0.0)
        dr1_ref[...] = _ln_bwd(acc, xhat1, rstd1, g1)
        dg1_ref[...] += _colsum(acc * xhat1)
        db1_ref[...] += _colsum(acc)

    sd = jax.ShapeDtypeStruct
    row = pl.BlockSpec((tm, d), lambda i: (i, 0))
    vec = _const((1, d))
    return _pcall(
        body, name="mlp_bwd", grid=(r // tm,),
        in_specs=[row, row, row, vec, _resident(wup_g.shape), _resident(wdn_bf.shape), _const((1, nf * d))],
        out_specs=[row, pl.BlockSpec((tm, nf * d), lambda i: (i, 0)), _const((1, nf * d)), vec, vec],
        out_shape=[sd((r, d), F32), sd((r, nf * d), BF16), sd((1, nf * d), F32), sd((1, d), F32), sd((1, d), F32)],
        compiler_params=_cp(("arbitrary",), 56),
    )(h1b, dr2, r1, g1, wup_g, wdn_bf, bup)


def _s5_block_mats(bb_re_t, bb_im_t, c_re, c_im, ap_re, ap_im):
    ng = c_re.shape[0]
    gl = ng // S5_KCH
    eye = jnp.eye(gl, dtype=F32)

    def bmat(bt):
        bb = jnp.transpose(bt, (1, 0, 2)).reshape(S5_KCH, gl, S5_GROUP, S5_STATE)
        return jnp.einsum("kghp,gj->kghjp", bb, eye).reshape(S5_KCH, gl * S5_GROUP, gl * S5_STATE)

    def cmat(c):
        cc = c.reshape(S5_KCH, gl, S5_GROUP, S5_STATE)
        return jnp.einsum("kghp,gj->kjpgh", cc, eye).reshape(S5_KCH, gl * S5_STATE, gl * S5_GROUP)

    def pw(a):
        return jnp.transpose(a.reshape(8, S5_KCH, gl * S5_STATE), (1, 0, 2))

    bk = jnp.concatenate([bmat(bb_re_t), bmat(bb_im_t)], axis=-1)
    apow = jnp.concatenate([pw(ap_re), pw(ap_im)], axis=-1)
    return _bf(bk), _bf(cmat(c_re)), _bf(cmat(c_im)), apow


def _s5_block_grads(dbk, dcre, dcim, da):
    gl = dbk.shape[1] // S5_GROUP
    ng = gl * S5_KCH
    eye = jnp.eye(gl, dtype=F32)
    hw = gl * S5_STATE

    def bpart(x):
        x = x.reshape(S5_KCH, gl, S5_GROUP, gl, S5_STATE)
        x = jnp.einsum("kghjp,gj->kghp", x, eye).reshape(ng, S5_GROUP, S5_STATE)
        return jnp.transpose(x, (1, 0, 2))

    def cpart(x):
        x = x.reshape(S5_KCH, gl, S5_STATE, gl, S5_GROUP)
        return jnp.einsum("kjpgh,gj->kghp", x, eye).reshape(ng, S5_GROUP, S5_STATE)

    return (bpart(dbk[..., :hw]), bpart(dbk[..., hw:]), cpart(dcre), cpart(dcim),
            da[:, 0, :hw].reshape(ng, S5_STATE), da[:, 0, hw:].reshape(ng, S5_STATE))


def _tie(a, tok):
    return a if tok is None else a + tok[0, 0]


def _local_step(x, tgt, w, early=None, late=None, ready=None):
    ready = ready or (lambda names, g: None)
    bsz, seq, d = x.shape
    lp = PAD + N_META + seq
    r = bsz * lp
    tgtp = jnp.concatenate([jnp.zeros((bsz, PAD + N_META, d), F32), tgt], axis=1).reshape(r, d)

    h0, h0b = _ln0_fwd(x, w["meta_tokens"], w["ln0_g"], w["ln0_b"])
    if early is not None:
        w = {**w, **early((h0, tgtp))}
    p, pg = _inproj(h0b, w["w_in"], w["b_in"], lp)
    p3 = p.reshape(bsz, lp, NP)
    pg3 = pg.reshape(bsz, lp, 128)

    b_re_t = jnp.transpose(w["s5_b_re"], (2, 0, 1))
    b_im_t = jnp.transpose(w["s5_b_im"], (2, 0, 1))
    ap_re, ap_im, bb_re_t, bb_im_t = _s5_prep(w["s5_lambda_re"], w["s5_lambda_im"], w["s5_log_dt"], b_re_t, b_im_t)
    bk, cre, cim, apow = _s5_block_mats(bb_re_t, bb_im_t, w["s5_c_re"], w["s5_c_im"], ap_re, ap_im)
    y_s5, xs = _s5_fwd(p3, bk, cre, cim, apow, w["s5_d"])
    sw = y_s5.shape[-1]
    if late is not None:
        w = {**w, **late(y_s5)}
    gy, z, ys5g = _glu_fwd(y_s5.reshape(r, sw), w["s5_w_glu"], lp)

    pre3, qk3 = _conv_fwd(p3, w["qk_conv_w"], w["qk_conv_b"])
    hm3, cs, ns, ms = _mlstm_fwd(qk3, p3, pg3)
    hm = hm3.reshape(r, d)
    ymin, mix, r1 = _mix_fwd(hm, p, ys5g, h0, w["m_norm_g"], w["m_w_out"], w["w_o"], lp)
    dr2, h1b, act, loss, dg2, db2 = _mlp_fwd(r1, tgtp, w["ln1_g"], w["ln1_b"], w["w_up"], w["w_down"], w["b_up"],
                                             w["ln2_g"], w["ln2_b"], lp)

    g = {"ln2_g": dg2, "ln2_b": db2}
    dr1, dup, g["b_up"], g["ln1_g"], g["ln1_b"] = _mlp_bwd(h1b, dr2, r1, w["ln1_g"], w["w_up"], w["w_down"], w["b_up"], lp)
    g["w_down"] = _mm_tn(act, dr2, name="dw_down")
    g["w_up"] = _mm_tn(h1b, dup, name="dw_up", split=w["w_up"].shape[0])
    tok = ready(("w_down", "w_up"), g)
    dp, dys5g, dym, dhm, g["m_norm_g"] = _mix_bwd(
        dr1, w["w_o"], w["m_w_out"], p, ys5g, ymin, hm, _tie(w["m_norm_g"], tok), lp)
    g["w_o"] = _mm_tn(mix, dr1, name="dw_o")
    g["m_w_out"] = _mm_tn(ymin, dym, name="dw_mout")

    dp3 = dp.reshape(bsz, lp, NP)
    dp3, dqk3, dgate = _mlstm_bwd(dp3, qk3, p3, pg3, dhm.reshape(bsz, lp, d), cs, ns, ms)
    dp3, g["qk_conv_w"], g["qk_conv_b"] = _conv_bwd(dp3, p3, dqk3, pre3, w["qk_conv_w"])
    dz, dys5 = _glu_bwd(dys5g, z, y_s5.reshape(r, sw), w["s5_w_glu"], lp)
    g["s5_w_glu"] = _mm_tn(gy, dz, name="dw_glu", split=w["s5_w_glu"].shape[0])
    tok = ready(("s5_w_glu", "m_w_out", "w_o"), g)
    apow_rev = jnp.flip(apow, axis=1)
    dp3, dbk, dcre, dcim, da, g["s5_d"] = _s5_bwd(dp3, p3, dys5.reshape(bsz, lp, sw), xs, bk, cre, cim, apow_rev,
                                                 _tie(w["s5_d"], tok))
    dbb_re_t, dbb_im_t, g["s5_c_re"], g["s5_c_im"], da_re, da_im = _s5_block_grads(dbk, dcre, dcim, da)
    g["s5_lambda_re"], g["s5_lambda_im"], g["s5_log_dt"], gb_re_t, gb_im_t = _s5_prep_bwd(
        w["s5_lambda_re"], w["s5_lambda_im"], w["s5_log_dt"], b_re_t, b_im_t, da_re, da_im, dbb_re_t, dbb_im_t)
    g["s5_b_re"] = jnp.transpose(gb_re_t, (1, 2, 0))
    g["s5_b_im"] = jnp.transpose(gb_im_t, (1, 2, 0))

    dp3 = lax.dynamic_update_slice(dp3, dgate, (0, 0, G_OFF))
    dp = dp3.reshape(r, NP)
    g["w_in"], g["b_in"] = _mm_tn(h0b, dp, name="dw_in", colsum=True)
    tok = ready(("w_in",), g)
    dpw = _mm_nt(dp, w["w_in"], lp, name="dh0", dep=tok)
    grad_x, g["ln0_g"], g["ln0_b"], g["meta_tokens"] = _ln0_bwd(x, w["meta_tokens"], dr1, dpw, w["ln0_g"])
    return loss, grad_x, g


_ANY = pl.BlockSpec(memory_space=pl.ANY)
_MESH = pl.DeviceIdType.MESH


def _place():
    return lax.axis_index("x"), lax.axis_index("y"), lax.axis_index("c")


def _gather_chips(shards):
    n = len(shards)

    def body(*refs):
        ins, outs = refs[:n], refs[n:2 * n]
        send, recv, loc = refs[2 * n:]
        x, y, c = _place()
        me = 2 * x + y
        peers = [(1 - x, y), (x, 1 - y), (1 - x, 1 - y)]

        def rc(a, k, slot):
            px, py = peers[k]
            return pltpu.make_async_remote_copy(src_ref=ins[a], dst_ref=outs[a].at[slot], send_sem=send.at[a, k],
                                                recv_sem=recv.at[a, k], device_id=(px, py, c), device_id_type=_MESH)

        own = [pltpu.make_async_copy(ins[a], outs[a].at[me], loc.at[a]) for a in range(n)]
        for cp in own:
            cp.start()
        out = [rc(a, k, me) for a in range(n) for k in range(3)]
        for cp in out:
            cp.start()
        for a in range(n):
            for k in range(3):
                rc(a, k, 2 * peers[k][0] + peers[k][1]).wait_recv()
        for cp in out:
            cp.wait_send()
        for cp in own:
            cp.wait()

    return _pcall(
        body, name="gather_chips", in_specs=[_ANY] * n, out_specs=[_ANY] * n,
        out_shape=[jax.ShapeDtypeStruct((4,) + s.shape, s.dtype) for s in shards],
        scratch_shapes=[pltpu.SemaphoreType.DMA((n, 3)), pltpu.SemaphoreType.DMA((n, 3)), pltpu.SemaphoreType.DMA((n,))],
    )(*shards)


_HBM = pl.BlockSpec(memory_space=pltpu.HBM)
_SEM = pl.BlockSpec(memory_space=pltpu.SEMAPHORE)
_EFFECT = pltpu.SideEffectType.DATAFLOW_SIDE_EFFECTING


def _xchg_copies(srcs, lands, send, recv, scatter):
    x, y, c = _place()
    me = 2 * x + y
    peers = [(1 - x, y), (x, 1 - y), (1 - x, 1 - y)]
    out = []
    for a in range(len(srcs)):
        for k, (px, py) in enumerate(peers):
            src = srcs[a].at[2 * px + py] if scatter else srcs[a]
            dst = lands[a].at[k] if scatter else lands[a].at[me]
            out.append(pltpu.make_async_remote_copy(src_ref=src, dst_ref=dst, send_sem=send.at[3 * a + k],
                                                    recv_sem=recv.at[3 * a + k], device_id=(px, py, c),
                                                    device_id_type=_MESH))
    return out


def _xchg_start(srcs, lands, *, name, scatter, dep=None):
    n = len(srcs)
    deps = [] if dep is None else [dep]
    nd = len(deps)

    def body(*refs):
        send, recv = refs[2 * n + nd], refs[2 * n + nd + 1]
        for cp in _xchg_copies(refs[:n], refs[n:2 * n], send, recv, scatter):
            cp.start()
        refs[-1][...] = jnp.zeros_like(refs[-1])

    hbm = lambda a: pltpu.HBM(a.shape, a.dtype)
    con = lambda a: pltpu.with_memory_space_constraint(a, pltpu.HBM)
    res = _pcall(
        body, name=name, in_specs=[_HBM] * (2 * n) + [_ANY] * nd,
        out_specs=[_SEM, _SEM] + [_HBM] * (2 * n) + [pl.BlockSpec(memory_space=pltpu.VMEM)],
        out_shape=[pltpu.SemaphoreType.DMA((3 * n,)), pltpu.SemaphoreType.DMA((3 * n,))]
        + [hbm(a) for a in srcs] + [hbm(a) for a in lands] + [jax.ShapeDtypeStruct((8, 128), F32)],
        input_output_aliases={i: 2 + i for i in range(2 * n)},
        compiler_params=pltpu.CompilerParams(has_side_effects=_EFFECT),
    )(*[con(a) for a in srcs], *[con(a) for a in lands], *deps)
    return res[0], res[1], list(res[2:2 + n]), list(res[2 + n:2 + 2 * n]), res[-1]


def _xchg_wait(send, recv, srcs, lands, after, *, name, scatter):
    n = len(srcs)
    afters = list(after) if isinstance(after, (list, tuple)) else [after]

    def body(*refs):
        s_ref, r_ref = refs[2 * n], refs[2 * n + 1]
        for cp in _xchg_copies(refs[:n], refs[n:2 * n], s_ref, r_ref, scatter):
            cp.wait_send()
            cp.wait_recv()

    hbm = lambda a: pltpu.HBM(a.shape, a.dtype)
    res = _pcall(
        body, name=name, in_specs=[_HBM] * (2 * n) + [_SEM, _SEM] + [_ANY] * len(afters),
        out_specs=[_HBM] * (2 * n),
        out_shape=[hbm(a) for a in srcs] + [hbm(a) for a in lands],
        input_output_aliases={i: i for i in range(2 * n)},
        compiler_params=pltpu.CompilerParams(has_side_effects=_EFFECT),
    )(*srcs, *lands, send, recv, *afters)
    return list(res[:n]), list(res[n:])


def _swap_cores(arrs, name="swap_cores"):
    n = len(arrs)

    def body(*refs):
        ins, outs = refs[:n], refs[n:2 * n]
        send, recv = refs[2 * n:]
        x, y, c = _place()
        cps = [pltpu.make_async_remote_copy(src_ref=ins[a], dst_ref=outs[a], send_sem=send.at[a], recv_sem=recv.at[a],
                                            device_id=(x, y, 1 - c), device_id_type=_MESH) for a in range(n)]
        for cp in cps:
            cp.start()
        for cp in cps:
            cp.wait_recv()
        for cp in cps:
            cp.wait_send()

    return _pcall(
        body, name=name, in_specs=[_ANY] * n, out_specs=[_ANY] * n,
        out_shape=[jax.ShapeDtypeStruct(s.shape, s.dtype) for s in arrs],
        scratch_shapes=[pltpu.SemaphoreType.DMA((n,)), pltpu.SemaphoreType.DMA((n,))],
    )(*arrs)


def _allreduce_small(v, dep=None):
    rows = v.shape[0]
    half = rows // 2
    assert half % 8 == 0 and 2 * half == rows
    deps = [] if dep is None else [dep]

    def body(v_ref, *rest):
        out_ref, sib_ref, pair_ref, slots_ref, send, recv = rest[len(deps):]
        x, y, c = _place()
        chip = 2 * x + y
        sibling = (x, y, 1 - c)
        peers = [(1 - x, y), (x, 1 - y), (1 - x, 1 - y)]
        mine = pl.ds(pl.multiple_of(c * half, 8), half)

        first = pltpu.make_async_remote_copy(src_ref=v_ref, dst_ref=sib_ref, send_sem=send.at[0], recv_sem=recv.at[0],
                                             device_id=sibling, device_id_type=_MESH)
        first.start()
        first.wait_recv()
        pair_ref[...] = v_ref[...] + sib_ref[...]
        slots_ref[chip] = pair_ref[mine, :]
        cross = [pltpu.make_async_remote_copy(src_ref=pair_ref.at[mine], dst_ref=slots_ref.at[chip],
                                              send_sem=send.at[1 + k], recv_sem=recv.at[1 + k],
                                              device_id=(px, py, c), device_id_type=_MESH)
                 for k, (px, py) in enumerate(peers)]
        for cp in cross:
            cp.start()
        for cp in cross:
            cp.wait_recv()
        out_ref[mine, :] = ((slots_ref[0] + slots_ref[1]) + slots_ref[2]) + slots_ref[3]
        last = pltpu.make_async_remote_copy(src_ref=out_ref.at[mine], dst_ref=out_ref.at[mine], send_sem=send.at[4],
                                            recv_sem=recv.at[4], device_id=sibling, device_id_type=_MESH)
        last.start()
        last.wait_recv()
        first.wait_send()
        for cp in cross:
            cp.wait_send()
        last.wait_send()

    vm = pl.BlockSpec(memory_space=pltpu.VMEM)
    return _pcall(
        body, name="allreduce_small", in_specs=[vm] + [_ANY] * len(deps), out_specs=vm,
        out_shape=jax.ShapeDtypeStruct((rows, 128), F32),
        scratch_shapes=[pltpu.VMEM((rows, 128), F32), pltpu.VMEM((rows, 128), F32), pltpu.VMEM((4, half, 128), F32),
                        pltpu.SemaphoreType.DMA((5,)), pltpu.SemaphoreType.DMA((5,))],
        compiler_params=_cp(None, 40),
    )(v, *deps)


def _sum_slots(own, land):
    ns, rows, cols = land.shape
    tm = _row_tile(rows, 256, 8)

    def body(own_ref, a_ref, o_ref):
        o_ref[...] = ((own_ref[...] + a_ref[0]) + a_ref[1]) + a_ref[2]

    return _pcall(
        body, name="sum_slots", grid=(rows // tm,),
        in_specs=[pl.BlockSpec((tm, cols), lambda i: (i, 0)), pl.BlockSpec((ns, tm, cols), lambda i: (0, i, 0))],
        out_specs=pl.BlockSpec((tm, cols), lambda i: (i, 0)),
        out_shape=jax.ShapeDtypeStruct((rows, cols), F32),
        compiler_params=_cp(("parallel",), 40),
    )(own, land)


def _adamw(w, m, v, g0, g1=None):
    rows, cols = w.shape[-2:]
    lead = w.ndim == 3
    tm = _row_tile(rows, max(8, (1 << 20) // (4 * cols)), 8)
    c1 = 1.0 - ADAM_B1 ** ADAM_STEP
    c2 = 1.0 - ADAM_B2 ** ADAM_STEP
    two = g1 is not None

    def body(*refs):
        w_ref, m_ref, v_ref, g0_ref = refs[:4]
        g_ref, d_ref, nm_ref, nv_ref = refs[-4:]
        g = g0_ref[...]
        if two:
            g = g + refs[4][...]
        nm = ADAM_B1 * m_ref[...] + (1.0 - ADAM_B1) * g
        nv = ADAM_B2 * v_ref[...] + (1.0 - ADAM_B2) * (g * g)
        g_ref[...] = g
        nm_ref[...] = nm
        nv_ref[...] = nv
        d_ref[...] = -ADAM_LR * ((nm / c1) / (jnp.sqrt(nv / c2) + ADAM_EPS) + ADAM_WD * w_ref[...])

    blk = pl.BlockSpec((tm, cols), lambda i: (i, 0))
    wblk = pl.BlockSpec((None, tm, cols), lambda i: (0, i, 0)) if lead else blk
    ins = [w, m, v, g0] + ([g1] if two else [])
    return _pcall(
        body, name="adamw", grid=(rows // tm,), in_specs=[wblk] * 3 + [blk] * (len(ins) - 3), out_specs=[wblk] * 4,
        out_shape=[jax.ShapeDtypeStruct(w.shape, F32)] * 4,
        compiler_params=_cp(("parallel",), 40),
    )(*ins)


_BIG = ("w_in", "s5_w_glu", "m_w_out", "w_o", "w_up", "w_down")
_SMALL = ("ln0_g", "ln0_b", "b_in", "qk_conv_b", "s5_lambda_re", "s5_lambda_im", "s5_log_dt", "s5_b_re", "s5_b_im",
          "s5_c_re", "s5_c_im", "s5_d", "m_norm_g", "ln1_g", "ln1_b", "b_up", "ln2_g", "ln2_b")
_SMALL_SHARDED = ("meta_tokens", "qk_conv_w")
_ORDER = ("meta_tokens", "ln0_g", "ln0_b", "w_in", "b_in", "qk_conv_w", "qk_conv_b", "s5_lambda_re", "s5_lambda_im",
          "s5_log_dt", "s5_b_re", "s5_b_im", "s5_c_re", "s5_c_im", "s5_d", "s5_w_glu", "m_norm_g", "m_w_out", "w_o",
          "ln1_g", "ln1_b", "w_up", "b_up", "w_down", "ln2_g", "ln2_b")


def _pack(arrs):
    flat = jnp.concatenate([a.reshape(-1) for a in arrs])
    n = flat.shape[0]
    rows = -(-n // 2048) * 16
    return jnp.pad(flat, (0, rows * 128 - n)).reshape(rows, 128)


def _unpack(packed, shapes):
    flat = packed.reshape(-1)
    out, off = [], 0
    for s in shapes:
        n = math.prod(s)
        out.append(flat[off:off + n].reshape(s))
        off += n
    return out


def kernel(x, meta_tokens, ln0_g, ln0_b, w_in, b_in, qk_conv_w, qk_conv_b, s5_lambda_re, s5_lambda_im, s5_log_dt, s5_b_re, s5_b_im, s5_c_re, s5_c_im, s5_d, s5_w_glu, m_norm_g, m_w_out, w_o, ln1_g, ln1_b, w_up, b_up, w_down, ln2_g, ln2_b, loss_target, m_meta_tokens, m_ln0_g, m_ln0_b, m_w_in, m_b_in, m_qk_conv_w, m_qk_conv_b, m_s5_lambda_re, m_s5_lambda_im, m_s5_log_dt, m_s5_b_re, m_s5_b_im, m_s5_c_re, m_s5_c_im, m_s5_d, m_s5_w_glu, m_m_norm_g, m_m_w_out, m_w_o, m_ln1_g, m_ln1_b, m_w_up, m_b_up, m_w_down, m_ln2_g, m_ln2_b, v_meta_tokens, v_ln0_g, v_ln0_b, v_w_in, v_b_in, v_qk_conv_w, v_qk_conv_b, v_s5_lambda_re, v_s5_lambda_im, v_s5_log_dt, v_s5_b_re, v_s5_b_im, v_s5_c_re, v_s5_c_im, v_s5_d, v_s5_w_glu, v_m_norm_g, v_m_w_out, v_w_o, v_ln1_g, v_ln1_b, v_w_up, v_b_up, v_w_down, v_ln2_g, v_ln2_b):
    wts = dict(meta_tokens=meta_tokens, ln0_g=ln0_g, ln0_b=ln0_b, w_in=w_in, b_in=b_in, qk_conv_w=qk_conv_w,
               qk_conv_b=qk_conv_b, s5_lambda_re=s5_lambda_re, s5_lambda_im=s5_lambda_im, s5_log_dt=s5_log_dt,
               s5_b_re=s5_b_re, s5_b_im=s5_b_im, s5_c_re=s5_c_re, s5_c_im=s5_c_im, s5_d=s5_d, s5_w_glu=s5_w_glu,
               m_norm_g=m_norm_g, m_w_out=m_w_out, w_o=w_o, ln1_g=ln1_g, ln1_b=ln1_b, w_up=w_up, b_up=b_up,
               w_down=w_down, ln2_g=ln2_g, ln2_b=ln2_b)
    mom = dict(meta_tokens=m_meta_tokens, ln0_g=m_ln0_g, ln0_b=m_ln0_b, w_in=m_w_in, b_in=m_b_in, qk_conv_w=m_qk_conv_w,
               qk_conv_b=m_qk_conv_b, s5_lambda_re=m_s5_lambda_re, s5_lambda_im=m_s5_lambda_im, s5_log_dt=m_s5_log_dt,
               s5_b_re=m_s5_b_re, s5_b_im=m_s5_b_im, s5_c_re=m_s5_c_re, s5_c_im=m_s5_c_im, s5_d=m_s5_d,
               s5_w_glu=m_s5_w_glu, m_norm_g=m_m_norm_g, m_w_out=m_m_w_out, w_o=m_w_o, ln1_g=m_ln1_g, ln1_b=m_ln1_b,
               w_up=m_w_up, b_up=m_b_up, w_down=m_w_down, ln2_g=m_ln2_g, ln2_b=m_ln2_b)
    var = dict(meta_tokens=v_meta_tokens, ln0_g=v_ln0_g, ln0_b=v_ln0_b, w_in=v_w_in, b_in=v_b_in, qk_conv_w=v_qk_conv_w,
               qk_conv_b=v_qk_conv_b, s5_lambda_re=v_s5_lambda_re, s5_lambda_im=v_s5_lambda_im, s5_log_dt=v_s5_log_dt,
               s5_b_re=v_s5_b_re, s5_b_im=v_s5_b_im, s5_c_re=v_s5_c_re, s5_c_im=v_s5_c_im, s5_d=v_s5_d,
               s5_w_glu=v_s5_w_glu, m_norm_g=v_m_norm_g, m_w_out=v_m_w_out, w_o=v_w_o, ln1_g=v_ln1_g, ln1_b=v_ln1_b,
               w_up=v_w_up, b_up=v_b_up, w_down=v_w_down, ln2_g=v_ln2_g, ln2_b=v_ln2_b)
    d = x.shape[-1]
    chip = 2 * lax.axis_index("x") + lax.axis_index("y")

    gw = dict(zip(_SMALL_SHARDED, _gather_chips([meta_tokens, qk_conv_w[0]])))
    own_w_in = _bf(w_in[0])
    fsend, frecv, fsrc, fland, ftok = _xchg_start([own_w_in], [lax.empty((4,) + own_w_in.shape, BF16)],
                                                  name="gather_w_in_start", scatter=False, dep=gw["qk_conv_w"])
    late_names = tuple(n for n in _BIG if n != "w_in")
    cat = lambda a: jnp.transpose(a, (1, 0, 2)).reshape(a.shape[1], 4 * a.shape[2])
    w = dict(
        meta_tokens=cat(gw["meta_tokens"]), ln0_g=ln0_g[None], ln0_b=_tie(ln0_b[None], ftok),
        qk_conv_w=cat(gw["qk_conv_w"]), qk_conv_b=qk_conv_b,
        s5_lambda_re=s5_lambda_re[0], s5_lambda_im=s5_lambda_im[0], s5_log_dt=s5_log_dt[0][:, None],
        s5_b_re=s5_b_re[0], s5_b_im=s5_b_im[0], s5_c_re=s5_c_re[0], s5_c_im=s5_c_im[0], s5_d=s5_d,
        m_norm_g=m_norm_g, ln1_g=ln1_g, ln1_b=ln1_b, b_up=b_up, ln2_g=ln2_g, ln2_b=ln2_b)
    in_flight = {}

    def place_own(src, land):
        return lax.dynamic_update_slice(land, src[None], (chip,) + (0,) * src.ndim)

    small_names = _SMALL + _SMALL_SHARDED

    def view(n, a):
        return jnp.swapaxes(a, -1, -2) if n in ("s5_b_re", "s5_b_im") else a

    small_wmv = [_pack([view(n, dct[n]) for n in small_names]) for dct in (wts, mom, var)]

    def early(after):
        src, land = _xchg_wait(fsend, frecv, fsrc, fland, tuple(after) + tuple(small_wmv), name="gather_w_in_wait",
                               scatter=False)
        late_src = [_bf(wts[n][0]) for n in late_names]
        st = _xchg_start(late_src, [lax.empty((4,) + a.shape, a.dtype) for a in late_src], name="gather_late_start",
                         scatter=False, dep=src[0])
        in_flight["late"] = st[:4]
        return dict(w_in=_w_in_from_slots(place_own(src[0], land[0]), IN_CHUNK), b_in=_tie(_to_pad_cols(b_in), st[4]))

    def late(after):
        src, land = _xchg_wait(*in_flight["late"], after, name="gather_late_wait", scatter=False)
        full = {n: place_own(s, ld) for n, s, ld in zip(late_names, src, land)}
        return dict(s5_w_glu=full["s5_w_glu"], m_w_out=full["m_w_out"].reshape(d, d), w_o=full["w_o"].reshape(d, d),
                    w_up=full["w_up"], w_down=full["w_down"].reshape(4 * d, d))

    flying = []

    def ready(names, g):
        parts = dict(
            w_in=lambda: _slots_from_w_in(g["w_in"][0]), s5_w_glu=lambda: g["s5_w_glu"],
            m_w_out=lambda: g["m_w_out"].reshape(4, d // 4, d), w_o=lambda: g["w_o"].reshape(4, d // 4, d),
            w_up=lambda: g["w_up"], w_down=lambda: g["w_down"].reshape(4, d, d))
        src = [parts[n]() for n in names]
        land = [lax.empty((3,) + a.shape[1:], a.dtype) for a in src]
        st = _xchg_start(src, land, name="scatter_" + names[0] + "_start", scatter=True)
        flying.append((names,) + st[:4])
        return st[4]

    loss, grad_x, g = _local_step(x, loss_target, w, early, late, ready)
    g["b_in"] = _from_pad_cols(g["b_in"])

    res = {}

    def flat(a):
        return jnp.swapaxes(a, -1, -2).reshape(a.shape[:-2] + (-1, 128))

    def unflat(y, shape):
        return jnp.swapaxes(y.reshape(shape[:-2] + (shape[-1], shape[-2])), -1, -2)

    def finish(groups, after, tag):
        mine = {}
        for names, send, recv, src, land in groups:
            src, land = _xchg_wait(send, recv, src, land, after, name="scatter_" + names[0] + "_wait", scatter=True)
            for n, s, ld in zip(names, src, land):
                mine[n] = _sum_slots(lax.dynamic_index_in_dim(s, chip, 0, keepdims=False), ld)
        theirs = _swap_cores(list(mine.values()), name="swap_cores_" + tag)
        for n, t in zip(mine, theirs):
            if n == "w_in":
                res[n] = [unflat(r, wts[n].shape) for r in _adamw(flat(wts[n]), flat(mom[n]), flat(var[n]),
                                                                  flat(mine[n]), flat(t))]
            else:
                res[n] = _adamw(wts[n], mom[n], var[n], mine[n], t)

    finish(flying[:-1], g["ln0_g"], "a")

    small_shapes = [(1, 128)] + [view(n, wts[n]).shape for n in _SMALL] + [g[n].shape for n in _SMALL_SHARDED]
    packed = _pack([loss] + [view(n, g[n]) for n in _SMALL] + [g[n] for n in _SMALL_SHARDED])
    tot = _unpack(_allreduce_small(packed, dep=res["w_o"][3]), small_shapes)
    loss_out = tot[0][0, 0]
    gsm = dict(zip(_SMALL + _SMALL_SHARDED, tot[1:]))
    for n in _SMALL_SHARDED:
        cols = wts[n].shape[-1]
        gsm[n] = lax.dynamic_slice_in_dim(gsm[n], chip * cols, cols, axis=1).reshape(wts[n].shape)

    names = small_names
    shapes = [view(n, wts[n]).shape for n in names]
    small_out = _adamw(*small_wmv, _pack([gsm[n] for n in names]))
    small_res = [_unpack(r, shapes) for r in small_out]
    for j, n in enumerate(names):
        res[n] = [view(n, small_res[q][j]) for q in range(4)]
    finish(flying[-1:], small_out[0], "b")

    return (loss_out, grad_x, *[res[n][0] for n in _ORDER], *[res[n][1] for n in _ORDER],
            *[res[n][2] for n in _ORDER], *[res[n][3] for n in _ORDER])
```

```python
import functools
import math

import jax
import jax.numpy as jnp
from jax import lax
from jax.experimental import pallas as pl
from jax.experimental.pallas import tpu as pltpu

F32 = jnp.float32
BF16 = jnp.bfloat16
HI = lax.Precision.HIGHEST

N_META = 16
M_HEADS = 4
M_CHUNK = 128
PAD = M_CHUNK - N_META
CONV_W = 4
HALO_ROWS = 16
S5_GROUP = 16
S5_STATE = 64
S5_KCH = 4
LN_EPS = 1e-5
ALPHA = 2.0 ** 0.25
NEG = -1e30
ADAM_LR, ADAM_B1, ADAM_B2, ADAM_EPS, ADAM_WD, ADAM_STEP = 0.001, 0.9, 0.999, 1e-08, 0.01, 10

O_OFF, GS_OFF, GM_OFF, V_OFF, Q_OFF, K_OFF, U_OFF, G_OFF, NP = 0, 1024, 2048, 3072, 4096, 4608, 5120, 5632, 5760

NN = ((1,), (0,))
NT = ((1,), (1,))
TN = ((0,), (0,))


def _dot(a, b, dims=NN, prec=None):
    return lax.dot_general(a, b, (dims, ((), ())), preferred_element_type=F32, precision=prec)


def _bf(x):
    return x.astype(BF16)


def _sig(x):
    return 0.5 * jnp.tanh(0.5 * x) + 0.5


def _pcall(body, **kw):
    return pl.pallas_call(body, **kw)


def _cp(sem=None, vmem_mb=None):
    kw = {}
    if sem is not None:
        kw["dimension_semantics"] = sem
    if vmem_mb is not None:
        kw["vmem_limit_bytes"] = vmem_mb << 20
    return pltpu.CompilerParams(**kw)


def _row_tile(n, want, mult=16):
    best = None
    for t in range(mult, want + 1, mult):
        if n % t == 0:
            best = t
    assert best is not None, (n, want)
    return best


def _resident(shape):
    nd = len(shape)
    return pl.BlockSpec(shape, lambda *_: (0,) * nd, pipeline_mode=pl.Buffered(1))


def _const(shape):
    nd = len(shape)
    return pl.BlockSpec(shape, lambda *_: (0,) * nd)


def _ln_fwd(x, g, b):
    mu = jnp.mean(x, axis=-1, keepdims=True)
    xc = x - mu
    var = jnp.mean(xc * xc, axis=-1, keepdims=True)
    rstd = lax.rsqrt(var + LN_EPS)
    xhat = xc * rstd
    return xhat * g + b, xhat, rstd


def _ln_bwd(dy, xhat, rstd, g):
    dxh = dy * g
    m1 = jnp.mean(dxh, axis=-1, keepdims=True)
    m2 = jnp.mean(dxh * xhat, axis=-1, keepdims=True)
    return rstd * (dxh - m1 - xhat * m2)


def _colsum(x):
    return jnp.sum(x, axis=0, keepdims=True)


def _to_pad_cols(w):
    u, q, k, v, o, gi, gf, gs, gm = (w[..., 0:512], w[..., 512:1024], w[..., 1024:1536], w[..., 1536:2560],
                                     w[..., 2560:3584], w[..., 3584:3588], w[..., 3588:3592], w[..., 3592:4616],
                                     w[..., 4616:5640])
    z = jnp.zeros(w.shape[:-1] + (NP - G_OFF - 8,), w.dtype)
    return jnp.concatenate([o, gs, gm, v, q, k, u, gi, gf, z], axis=-1)


def _from_pad_cols(w):
    o, gs, gm, v, q, k, u = (w[..., O_OFF:GS_OFF], w[..., GS_OFF:GM_OFF], w[..., GM_OFF:V_OFF], w[..., V_OFF:Q_OFF],
                             w[..., Q_OFF:K_OFF], w[..., K_OFF:U_OFF], w[..., U_OFF:G_OFF])
    gi, gf = w[..., G_OFF:G_OFF + 4], w[..., G_OFF + 4:G_OFF + 8]
    return jnp.concatenate([u, q, k, v, o, gi, gf, gs, gm], axis=-1)


_IN_REF = (("u", 512), ("q", 512), ("k", 512), ("v", 1024), ("o", 1024), ("i", 4), ("f", 4), ("gs", 1024), ("gm", 1024))
_IN_PAD = (("o", O_OFF), ("gs", GS_OFF), ("gm", GM_OFF), ("v", V_OFF), ("q", Q_OFF), ("k", K_OFF), ("u", U_OFF),
           ("i", G_OFF), ("f", G_OFF + 4))


def _in_ref_ranges():
    out, off = {}, 0
    for n, s in _IN_REF:
        out[n] = (off, off + s)
        off += s
    return out, off


def _w_in_from_slots(g, chunk=None):
    rng, total = _in_ref_ranges()
    width = total // g.shape[0]
    cols = []
    for n, _ in _IN_PAD:
        a, b = rng[n]
        while a < b:
            s = a // width
            e = min(b, (s + 1) * width)
            cols.append(g[s][:, a - s * width:e - s * width])
            a = e
    cols.append(jnp.zeros((g.shape[1], NP - G_OFF - 8), g.dtype))
    if chunk is None:
        return jnp.concatenate(cols, axis=1)
    chunks, cur, room = [], [], chunk
    for c in cols:
        while c.shape[1] > 0:
            take = min(room, c.shape[1])
            cur.append(c[:, :take])
            c, room = c[:, take:], room - take
            if room == 0:
                chunks.append(jnp.concatenate(cur, axis=1))
                cur, room = [], chunk
    assert not cur
    return jnp.stack(chunks, axis=0)


def _slots_from_w_in(wp, nslot=4):
    rng, total = _in_ref_ranges()
    width = total // nslot
    pad_off = dict(_IN_PAD)
    slots = []
    for s in range(nslot):
        lo, hi = s * width, (s + 1) * width
        cols = []
        for n, _ in _IN_REF:
            a, b = rng[n]
            x0, x1 = max(a, lo), min(b, hi)
            if x0 < x1:
                cols.append(wp[:, pad_off[n] + x0 - a:pad_off[n] + x1 - a])
        slots.append(jnp.concatenate(cols, axis=1))
    return jnp.stack(slots, axis=0)


HEAD = PAD + N_META


def _ln0_in(j, x_ref, meta_ref):
    first = jnp.concatenate([jnp.zeros((PAD, meta_ref.shape[1]), F32), meta_ref[...]], axis=0)
    return jnp.where(j == 0, first[None], x_ref[...])


def _ln0_fwd(x, meta, g, b):
    bsz, seq, d = x.shape
    nb = seq // HEAD + 1

    def body(x_ref, m_ref, g_ref, b_ref, o_ref, ob_ref):
        y, _, _ = _ln_fwd(_ln0_in(pl.program_id(0), x_ref, m_ref), g_ref[...], b_ref[...])
        o_ref[...] = y
        ob_ref[...] = _bf(y)

    row = pl.BlockSpec((bsz, HEAD, d), lambda j: (0, j, 0))
    h0, h0b = _pcall(
        body, name="ln0_fwd", grid=(nb,),
        in_specs=[pl.BlockSpec((bsz, HEAD, d), lambda j: (0, jnp.maximum(j - 1, 0), 0)), _const((N_META, d)),
                  _const((1, d)), _const((1, d))],
        out_specs=[row, row],
        out_shape=[jax.ShapeDtypeStruct((bsz, nb * HEAD, d), F32), jax.ShapeDtypeStruct((bsz, nb * HEAD, d), BF16)],
        compiler_params=_cp(("arbitrary",)),
    )(x, meta, g, b)
    return h0.reshape(-1, d), h0b.reshape(-1, d)


def _ln0_bwd(x, meta, dr1, dpw, g):
    bsz, seq, d = x.shape
    nb = seq // HEAD + 1

    def body(x_ref, m_ref, a_ref, c_ref, g_ref, o_ref, dg_ref, db_ref, dm_ref):
        j = pl.program_id(0)

        @pl.when(j == 0)
        def _():
            dg_ref[...] = jnp.zeros_like(dg_ref)
            db_ref[...] = jnp.zeros_like(db_ref)
            dm_ref[...] = jnp.zeros_like(dm_ref)

        dy = ALPHA * a_ref[...] + c_ref[...]
        _, xhat, rstd = _ln_fwd(_ln0_in(j, x_ref, m_ref), g_ref[...], 0.0)
        dx = _ln_bwd(dy, xhat, rstd, g_ref[...])
        o_ref[...] = dx
        dg_ref[...] += _colsum((dy * xhat).reshape(bsz * HEAD, d))
        db_ref[...] += _colsum(dy.reshape(bsz * HEAD, d))

        @pl.when(j == 0)
        def _():
            dm_ref[...] += jnp.sum(dx[:, PAD:, :], axis=0)

    row = pl.BlockSpec((bsz, HEAD, d), lambda j: (0, j, 0))
    tok = pl.BlockSpec((bsz, HEAD, d), lambda j: (0, jnp.maximum(j - 1, 0), 0))
    lp = nb * HEAD
    return _pcall(
        body, name="ln0_bwd", grid=(nb,),
        in_specs=[tok, _const((N_META, d)), row, row, _const((1, d))],
        out_specs=[tok, _const((1, d)), _const((1, d)), _const((N_META, d))],
        out_shape=[jax.ShapeDtypeStruct((bsz, seq, d), F32), jax.ShapeDtypeStruct((1, d), F32),
                   jax.ShapeDtypeStruct((1, d), F32), jax.ShapeDtypeStruct((N_META, d), F32)],
        compiler_params=_cp(("arbitrary",)),
    )(x, meta, dr1.reshape(bsz, lp, d), dpw.reshape(bsz, lp, d), g)


IN_CHUNK = 1152


def _chunk_cols(w):
    k, n = w.shape
    return jnp.transpose(w.reshape(k, n // IN_CHUNK, IN_CHUNK), (1, 0, 2))


def _inproj(h0b, w3, bias, lp):
    r, d = h0b.shape
    nj, _, tn = w3.shape
    tm = _row_tile(lp, 1056)
    tps = lp // tm

    def body(a_ref, w_ref, b_ref, o_ref, gate_ref):
        i = pl.program_id(0)
        j = pl.program_id(1)
        acc = _dot(a_ref[...], w_ref[j]) + b_ref[...]
        t = (i % tps) * tm + lax.broadcasted_iota(jnp.int32, (tm, 1), 0)
        acc = jnp.where(t >= PAD, acc, 0.0)
        o_ref[...] = _bf(acc)

        @pl.when(j == nj - 1)
        def _():
            gate_ref[...] = acc[:, tn - 128:]

    return _pcall(
        body, name="inproj", grid=(r // tm, nj),
        in_specs=[pl.BlockSpec((tm, d), lambda i, j: (i, 0)), _resident(w3.shape),
                  pl.BlockSpec((1, tn), lambda i, j: (0, j))],
        out_specs=[pl.BlockSpec((tm, tn), lambda i, j: (i, j)), pl.BlockSpec((tm, 128), lambda i, j: (i, 0))],
        out_shape=[jax.ShapeDtypeStruct((r, nj * tn), BF16), jax.ShapeDtypeStruct((r, 128), F32)],
        compiler_params=_cp(("parallel", "arbitrary"), 48),
    )(h0b, w3, bias)


def _mm_tn(a, b, *, name, split=1, colsum=False, tk_want=2112):
    r, m = a.shape
    n = b.shape[1]
    tk = _row_tile(r, tk_want)
    tm = min(m, 1024)
    ns = n // split
    tn = ns
    for cand in (1024, 1152, 640, 512, 128):
        if ns % cand == 0 and cand <= ns:
            tn = cand
            break
    nb = ns // tn
    nk = r // tk

    def body(a_ref, b_ref, o_ref, *rest):
        acc = rest[-1]
        k = pl.program_id(2)

        @pl.when(k == 0)
        def _():
            acc[...] = jnp.zeros_like(acc)

        bt = b_ref[...]
        acc[...] += _dot(_bf(a_ref[...]), _bf(bt), TN)

        @pl.when(k == nk - 1)
        def _():
            o_ref[...] = acc[...]

        if colsum:
            cs_ref = rest[0]

            @pl.when(k == 0)
            def _():
                cs_ref[...] = jnp.zeros_like(cs_ref)

            cs_ref[...] += _colsum(bt.astype(F32))

    out_specs = [pl.BlockSpec((None, tm, tn), lambda i, j, k: (j // nb, i, j % nb))]
    out_shape = [jax.ShapeDtypeStruct((split, m, ns), F32)]
    if colsum:
        assert m == tm
        out_specs.append(pl.BlockSpec((1, tn), lambda i, j, k: (0, j)))
        out_shape.append(jax.ShapeDtypeStruct((1, n), F32))
    res = _pcall(
        body, name=name, grid=(m // tm, n // tn, nk),
        in_specs=[pl.BlockSpec((tk, tm), lambda i, j, k: (k, i)), pl.BlockSpec((tk, tn), lambda i, j, k: (k, j))],
        out_specs=out_specs, out_shape=out_shape,
        scratch_shapes=[pltpu.VMEM((tm, tn), F32)],
        compiler_params=_cp(("parallel", "parallel", "arbitrary"), 56),
    )(a, b)
    return res if colsum else res[0]


def _mm_nt(a, w3, lp, *, name, dep=None):
    r, kdim = a.shape
    nk, n, tk = w3.shape
    assert nk * tk == kdim
    tm = _row_tile(lp, 1056)
    deps = [] if dep is None else [dep]

    def body(a_ref, w_ref, *rest):
        o_ref, acc = rest[-2:]
        k = pl.program_id(1)

        @pl.when(k == 0)
        def _():
            acc[...] = jnp.zeros_like(acc)

        acc[...] += _dot(_bf(a_ref[...]), w_ref[k], NT)

        @pl.when(k == nk - 1)
        def _():
            o_ref[...] = acc[...]

    return _pcall(
        body, name=name, grid=(r // tm, nk),
        in_specs=[pl.BlockSpec((tm, tk), lambda i, k: (i, k)), _resident(w3.shape)]
        + [_const(dp_.shape) for dp_ in deps],
        out_specs=pl.BlockSpec((tm, n), lambda i, k: (i, 0)),
        out_shape=jax.ShapeDtypeStruct((r, n), F32),
        scratch_shapes=[pltpu.VMEM((tm, n), F32)],
        compiler_params=_cp(("parallel", "arbitrary"), 48),
    )(a, w3, *deps)


def _s5_prep(lam_re, lam_im, log_dt, b_re_t, b_im_t):
    g, p = lam_re.shape
    h = b_re_t.shape[0]

    def body(lr_ref, li_ref, ldt_ref, br_ref, bi_ref, pr_ref, pi_ref, bbr_ref, bbi_ref):
        lr, li = lr_ref[...], li_ref[...]
        dt = jnp.exp(ldt_ref[...])
        e = jnp.exp(lr * dt)
        ar, ai = e * jnp.cos(li * dt), e * jnp.sin(li * dt)
        den = lr * lr + li * li
        cr = ((ar - 1.0) * lr + ai * li) / den
        ci = (ai * lr - (ar - 1.0) * li) / den
        br, bi = br_ref[...], bi_ref[...]
        bbr_ref[...] = cr[None] * br - ci[None] * bi
        bbi_ref[...] = cr[None] * bi + ci[None] * br
        xr, xi = ar, ai
        pr_ref[0] = xr
        pi_ref[0] = xi
        for t in range(1, 8):
            xr, xi = xr * ar - xi * ai, xr * ai + xi * ar
            pr_ref[t] = xr
            pi_ref[t] = xi

    sd = jax.ShapeDtypeStruct
    return _pcall(body, name="s5_prep",
                  out_shape=[sd((8, g, p), F32), sd((8, g, p), F32), sd((h, g, p), F32), sd((h, g, p), F32)])(
        lam_re, lam_im, log_dt, b_re_t, b_im_t)


def _s5_prep_bwd(lam_re, lam_im, log_dt, b_re_t, b_im_t, da_re, da_im, dbb_re_t, dbb_im_t):
    g, p = lam_re.shape
    h = b_re_t.shape[0]

    def body(lr_ref, li_ref, ldt_ref, br_ref, bi_ref, dar_ref, dai_ref, dbr_ref, dbi_ref,
             glr_ref, gli_ref, gdt_ref, gbr_ref, gbi_ref):
        lr, li = lr_ref[...], li_ref[...]
        dt = jnp.exp(ldt_ref[...])
        e = jnp.exp(lr * dt)
        ar, ai = e * jnp.cos(li * dt), e * jnp.sin(li * dt)
        den = lr * lr + li * li
        cr = ((ar - 1.0) * lr + ai * li) / den
        ci = (ai * lr - (ar - 1.0) * li) / den
        br, bi = br_ref[...], bi_ref[...]
        gr, gi = dbr_ref[...], dbi_ref[...]
        gbr_ref[...] = gr * cr[None] + gi * ci[None]
        gbi_ref[...] = gi * cr[None] - gr * ci[None]
        gcr = jnp.sum(gr * br + gi * bi, axis=0)
        gci = jnp.sum(gi * br - gr * bi, axis=0)
        ilr, ili = lr / den, -li / den
        gar = dar_ref[...] + gcr * ilr + gci * ili
        gai = dai_ref[...] + gci * ilr - gcr * ili
        qr, qi = cr * ilr - ci * ili, cr * ili + ci * ilr
        glr = -(gcr * qr + gci * qi)
        gli = -(gci * qr - gcr * qi)
        gzr = gar * ar + gai * ai
        gzi = gai * ar - gar * ai
        glr_ref[...] = glr + gzr * dt
        gli_ref[...] = gli + gzi * dt
        gdt_ref[...] = jnp.sum(gzr * lr + gzi * li, axis=1, keepdims=True) * dt

    sd = jax.ShapeDtypeStruct
    return _pcall(body, name="s5_prep_bwd",
                  out_shape=[sd((g, p), F32), sd((g, p), F32), sd((g, 1), F32), sd((h, g, p), F32), sd((h, g, p), F32)])(
        lam_re, lam_im, log_dt, b_re_t, b_im_t, da_re, da_im, dbb_re_t, dbb_im_t)


def _cmul(xr, xi, yr, yi):
    return xr * yr - xi * yi, xr * yi + xi * yr


def _dot5(a, b, dims=NN):
    return _dot(_bf(a), _bf(b), dims)


def _s5_fwd(p3, bk, cre, cim, apow, dskip):
    bsz, lp, _ = p3.shape
    tt = _row_tile(lp, 528, 8)
    nt = lp // tt
    nblk = tt // 8
    hw = 512

    def body(u_ref, bk_ref, cre_ref, cim_ref, ap_ref, d_ref, y_ref, xs_ref, car_ref):
        t = pl.program_id(2)

        @pl.when(t == 0)
        def _():
            car_ref[...] = jnp.zeros_like(car_ref)

        u = u_ref[...].astype(F32)
        xs_ref[...] = _dot5(u, bk_ref[...])
        ap = ap_ref[...]
        apr, api = ap[:, :hw], ap[:, hw:]
        rows = lax.broadcasted_iota(jnp.int32, (8, hw), 0)
        lev = [(d, jnp.where(rows < d, 0.0, jnp.broadcast_to(apr[d - 1:d, :], (8, hw))),
                jnp.where(rows < d, 0.0, jnp.broadcast_to(api[d - 1:d, :], (8, hw)))) for d in (1, 2, 4)]

        def blk(i, carry):
            cr, ci = carry
            off = pl.multiple_of(i * 8, 8)
            x = xs_ref[pl.ds(off, 8), :]
            xr, xi = x[:, :hw], x[:, hw:]
            for d, lr, li in lev:
                mr, mi = _cmul(pltpu.roll(xr, d, 0), pltpu.roll(xi, d, 0), lr, li)
                xr, xi = xr + mr, xi + mi
            mr, mi = _cmul(apr, api, cr, ci)
            xr, xi = xr + mr, xi + mi
            xs_ref[pl.ds(off, 8), :] = jnp.concatenate([xr, xi], axis=1)
            return xr[7:8, :], xi[7:8, :]

        c0 = car_ref[...]
        cr, ci = lax.fori_loop(0, nblk, blk, (c0[0:1, :hw], c0[0:1, hw:]))
        car_ref[...] = jnp.broadcast_to(jnp.concatenate([cr, ci], axis=1), car_ref.shape)
        xs = xs_ref[...]
        y_ref[...] = (_dot5(xs[:, :hw], cre_ref[...]) - _dot5(xs[:, hw:], cim_ref[...])
                      + d_ref[...] * u)

    ub = U_OFF // 128
    return _pcall(
        body, name="s5_fwd", grid=(S5_KCH, bsz, nt),
        in_specs=[pl.BlockSpec((None, tt, 128), lambda k, b, t: (b, t, ub + k)),
                  pl.BlockSpec((None, 128, 2 * hw), lambda k, b, t: (k, 0, 0)),
                  pl.BlockSpec((None, hw, 128), lambda k, b, t: (k, 0, 0)),
                  pl.BlockSpec((None, hw, 128), lambda k, b, t: (k, 0, 0)),
                  pl.BlockSpec((None, 8, 2 * hw), lambda k, b, t: (k, 0, 0)),
                  pl.BlockSpec((1, 128), lambda k, b, t: (0, k))],
        out_specs=[pl.BlockSpec((None, tt, 128), lambda k, b, t: (b, t, k)),
                   pl.BlockSpec((None, None, tt, 2 * hw), lambda k, b, t: (b, k, t, 0))],
        out_shape=[jax.ShapeDtypeStruct((bsz, lp, S5_KCH * 128), F32),
                   jax.ShapeDtypeStruct((bsz, S5_KCH, lp, 2 * hw), F32)],
        scratch_shapes=[pltpu.VMEM((8, 2 * hw), F32)],
        compiler_params=_cp(("parallel", "parallel", "arbitrary"), 40),
    )(p3, bk, cre, cim, apow, dskip)


def _s5_bwd(dp3, p3, dy3, xs, bk, cre, cim, apow_rev, dskip):
    bsz, lp, _ = p3.shape
    tt = _row_tile(lp, 528, 8)
    nt = lp // tt
    nblk = tt // 8
    hw = 512
    tb = tt // 8

    def body(dp_any, u_ref, dy_ref, xs_ref, halo_ref, bkt_ref, cre_ref, cim_ref, ap_ref, d_ref,
             du_ref, dbk_ref, dcre_ref, dcim_ref, da_ref, dd_ref, g_ref, ext_ref, car_ref):
        b = pl.program_id(1)
        t = pl.program_id(2)
        tidx = nt - 1 - t

        @pl.when(t == 0)
        def _():
            car_ref[...] = jnp.zeros_like(car_ref)

        @pl.when((b == 0) & (t == 0))
        def _():
            dbk_ref[...] = jnp.zeros_like(dbk_ref)
            dcre_ref[...] = jnp.zeros_like(dcre_ref)
            dcim_ref[...] = jnp.zeros_like(dcim_ref)
            da_ref[...] = jnp.zeros_like(da_ref)
            dd_ref[...] = jnp.zeros_like(dd_ref)

        u = u_ref[...].astype(F32)
        dy = dy_ref[...]
        g_ref[:, :hw] = _dot5(dy, cre_ref[...])
        g_ref[:, hw:] = -_dot5(dy, cim_ref[...])
        ap = ap_ref[...]
        apr, api = ap[:, :hw], -ap[:, hw:]
        rows = lax.broadcasted_iota(jnp.int32, (8, hw), 0)
        lev = [(d, jnp.where(rows >= 8 - d, 0.0, jnp.broadcast_to(apr[8 - d:9 - d, :], (8, hw))),
                jnp.where(rows >= 8 - d, 0.0, jnp.broadcast_to(api[8 - d:9 - d, :], (8, hw)))) for d in (1, 2, 4)]

        def blk(i, carry):
            cr, ci = carry
            off = pl.multiple_of((nblk - 1 - i) * 8, 8)
            x = g_ref[pl.ds(off, 8), :]
            xr, xi = x[:, :hw], x[:, hw:]
            for d, lr, li in lev:
                mr, mi = _cmul(pltpu.roll(xr, 8 - d, 0), pltpu.roll(xi, 8 - d, 0), lr, li)
                xr, xi = xr + mr, xi + mi
            mr, mi = _cmul(apr, api, cr, ci)
            xr, xi = xr + mr, xi + mi
            g_ref[pl.ds(off, 8), :] = jnp.concatenate([xr, xi], axis=1)
            return xr[0:1, :], xi[0:1, :]

        c0 = car_ref[...]
        cr, ci = lax.fori_loop(0, nblk, blk, (c0[0:1, :hw], c0[0:1, hw:]))
        car_ref[...] = jnp.broadcast_to(jnp.concatenate([cr, ci], axis=1), car_ref.shape)

        gg = g_ref[...]
        du = _dot5(gg, bkt_ref[...]) + d_ref[...] * dy
        trow = tidx * tt + lax.broadcasted_iota(jnp.int32, (tt, 1), 0)
        du_ref[...] = jnp.where(trow >= PAD, du, 0.0).astype(du_ref.dtype)
        dbk_ref[...] += _dot5(u, gg, TN)
        xsv = xs_ref[...]
        dcre_ref[...] += _dot5(dy, xsv[:, :hw], TN)
        dcim_ref[...] -= _dot5(dy, xsv[:, hw:], TN)
        dd_ref[...] += _colsum(dy * u)
        ext_ref[0:8, :] = jnp.where(tidx == 0, 0.0, halo_ref[...])
        ext_ref[8:, :] = xsv
        xp = ext_ref[pl.ds(7, tt), :]
        gr, gi, pr, pi = gg[:, :hw], gg[:, hw:], xp[:, :hw], xp[:, hw:]
        da_ref[:, :hw] += _colsum(gr * pr + gi * pi)
        da_ref[:, hw:] += _colsum(gi * pr - gr * pi)

    ub = U_OFF // 128
    sd = jax.ShapeDtypeStruct
    rt = lambda t: nt - 1 - t
    tr = lambda a: jnp.swapaxes(a, 1, 2)
    res = _pcall(
        body, name="s5_bwd", grid=(S5_KCH, bsz, nt),
        in_specs=[pl.BlockSpec(memory_space=pl.ANY),
                  pl.BlockSpec((None, tt, 128), lambda k, b, t: (b, rt(t), ub + k)),
                  pl.BlockSpec((None, tt, 128), lambda k, b, t: (b, rt(t), k)),
                  pl.BlockSpec((None, None, tt, 2 * hw), lambda k, b, t: (b, k, rt(t), 0)),
                  pl.BlockSpec((None, None, 8, 2 * hw), lambda k, b, t: (b, k, jnp.maximum(rt(t) * tb - 1, 0), 0)),
                  pl.BlockSpec((None, 2 * hw, 128), lambda k, b, t: (k, 0, 0)),
                  pl.BlockSpec((None, 128, hw), lambda k, b, t: (k, 0, 0)),
                  pl.BlockSpec((None, 128, hw), lambda k, b, t: (k, 0, 0)),
                  pl.BlockSpec((None, 8, 2 * hw), lambda k, b, t: (k, 0, 0)),
                  pl.BlockSpec((1, 128), lambda k, b, t: (0, k))],
        out_specs=[pl.BlockSpec((None, tt, 128), lambda k, b, t: (b, rt(t), ub + k)),
                   pl.BlockSpec((None, 128, 2 * hw), lambda k, b, t: (k, 0, 0)),
                   pl.BlockSpec((None, 128, hw), lambda k, b, t: (k, 0, 0)),
                   pl.BlockSpec((None, 128, hw), lambda k, b, t: (k, 0, 0)),
                   pl.BlockSpec((None, 1, 2 * hw), lambda k, b, t: (k, 0, 0)),
                   pl.BlockSpec((1, 128), lambda k, b, t: (0, k))],
        out_shape=[sd(dp3.shape, dp3.dtype), sd((S5_KCH, 128, 2 * hw), F32), sd((S5_KCH, 128, hw), F32),
                   sd((S5_KCH, 128, hw), F32), sd((S5_KCH, 1, 2 * hw), F32), sd((1, S5_KCH * 128), F32)],
        scratch_shapes=[pltpu.VMEM((tt, 2 * hw), F32), pltpu.VMEM((tt + 8, 2 * hw), F32), pltpu.VMEM((8, 2 * hw), F32)],
        input_output_aliases={0: 0},
        compiler_params=_cp(("arbitrary", "arbitrary", "arbitrary"), 48),
    )(dp3, p3, dy3, xs, xs, tr(bk), tr(cre), tr(cim), apow_rev, dskip)
    return res[0], res[1], tr(res[2]), tr(res[3]), res[4], res[5]


_G0 = math.sqrt(2.0 / math.pi)
_G1 = 0.044715


def _gelu(y):
    return 0.5 * y * (1.0 + jnp.tanh(_G0 * (y + _G1 * y * y * y)))


def _gelu_grad(y):
    th = jnp.tanh(_G0 * (y + _G1 * y * y * y))
    return 0.5 * (1.0 + th) + 0.5 * y * (1.0 - th * th) * _G0 * (1.0 + 3.0 * _G1 * y * y)


def _glu_fwd(y_s5, wglu_g, lp):
    r, w = y_s5.shape
    tm = _row_tile(lp, 416)
    cw = wglu_g.shape[2]

    def body(y_ref, w_ref, gy_ref, z_ref, o_ref):
        gy = _bf(_gelu(y_ref[...]))
        gy_ref[...] = gy
        zs = [_dot(gy, w_ref[s]) for s in range(4)]
        for s in range(4):
            z_ref[:, s * cw:(s + 1) * cw] = _bf(zs[s])
        o_ref[:, :cw] = _bf(zs[0] * _sig(zs[2]))
        o_ref[:, cw:] = _bf(zs[1] * _sig(zs[3]))

    sd = jax.ShapeDtypeStruct
    return _pcall(
        body, name="glu_fwd", grid=(r // tm,),
        in_specs=[pl.BlockSpec((tm, w), lambda i: (i, 0)), _resident(wglu_g.shape)],
        out_specs=[pl.BlockSpec((tm, w), lambda i: (i, 0)), pl.BlockSpec((tm, 4 * cw), lambda i: (i, 0)),
                   pl.BlockSpec((tm, 2 * cw), lambda i: (i, 0))],
        out_shape=[sd((r, w), BF16), sd((r, 4 * cw), BF16), sd((r, 2 * cw), BF16)],
        compiler_params=_cp(("parallel",), 40),
    )(y_s5, wglu_g)


def _glu_bwd(dyg, z, y_s5, wglu_g, lp):
    r, w = y_s5.shape
    tm = _row_tile(lp, 416)
    cw = wglu_g.shape[2]

    def body(d_ref, z_ref, y_ref, w_ref, dz_ref, dy_ref):
        d = d_ref[...].astype(F32)
        zz = z_ref[...].astype(F32)
        acc = jnp.zeros((tm, w), F32)
        for s in range(2):
            z1 = zz[:, s * cw:(s + 1) * cw]
            sg = _sig(zz[:, (2 + s) * cw:(3 + s) * cw])
            dd = d[:, s * cw:(s + 1) * cw]
            dz1 = _bf(dd * sg)
            dz2 = _bf(dd * z1 * sg * (1.0 - sg))
            dz_ref[:, s * cw:(s + 1) * cw] = dz1
            dz_ref[:, (2 + s) * cw:(3 + s) * cw] = dz2
            acc += _dot(dz1, w_ref[s], NT) + _dot(dz2, w_ref[2 + s], NT)
        dy_ref[...] = acc * _gelu_grad(y_ref[...])

    sd = jax.ShapeDtypeStruct
    return _pcall(
        body, name="glu_bwd", grid=(r // tm,),
        in_specs=[pl.BlockSpec((tm, 2 * cw), lambda i: (i, 0)), pl.BlockSpec((tm, 4 * cw), lambda i: (i, 0)),
                  pl.BlockSpec((tm, w), lambda i: (i, 0)), _resident(wglu_g.shape)],
        out_specs=[pl.BlockSpec((tm, 4 * cw), lambda i: (i, 0)), pl.BlockSpec((tm, w), lambda i: (i, 0))],
        out_shape=[sd((r, 4 * cw), BF16), sd((r, w), F32)],
        compiler_params=_cp(("parallel",), 40),
    )(dyg, z, y_s5, wglu_g)


def _conv_fwd(p3, cw, cb):
    bsz, lp, _ = p3.shape
    tt = _row_tile(lp, 416)
    nt = lp // tt
    tb = tt // 8
    c = cw.shape[1]
    qb = Q_OFF // c

    hr = HALO_ROWS
    off = hr - (CONV_W - 1)

    def body(x_ref, halo_ref, w_ref, b_ref, pre_ref, act_ref, ext_ref):
        t = pl.program_id(1)
        ext_ref[0:hr, :] = jnp.where(t == 0, 0.0, halo_ref[...].astype(F32))
        ext_ref[hr:, :] = x_ref[...].astype(F32)
        w = w_ref[...]
        acc = b_ref[...] + w[0:1, :] * ext_ref[pl.ds(off, tt), :]
        for j in range(1, CONV_W):
            acc = acc + w[j:j + 1, :] * ext_ref[pl.ds(off + j, tt), :]
        pre_ref[...] = _bf(acc)
        act_ref[...] = _bf(acc * _sig(acc))

    sd = jax.ShapeDtypeStruct
    return _pcall(
        body, name="conv_fwd", grid=(bsz, nt),
        in_specs=[pl.BlockSpec((None, tt, c), lambda b, t: (b, t, qb)),
                  pl.BlockSpec((None, hr, c), lambda b, t: (b, jnp.maximum(t * (tt // hr) - 1, 0), qb)),
                  _const((CONV_W, c)), _const((1, c))],
        out_specs=[pl.BlockSpec((None, tt, c), lambda b, t: (b, t, 0))] * 2,
        out_shape=[sd((bsz, lp, c), BF16)] * 2,
        scratch_shapes=[pltpu.VMEM((tt + hr, c), F32)],
        compiler_params=_cp(("parallel", "parallel")),
    )(p3, p3, cw, cb)


def _conv_bwd(dp3, p3, dact3, pre3, cw):
    bsz, lp, _ = p3.shape
    tt = _row_tile(lp, 416)
    nt = lp // tt
    tb = tt // 8
    c = cw.shape[1]
    qb = Q_OFF // c

    hr = HALO_ROWS
    off = hr - (CONV_W - 1)

    def silu_grad(x):
        s = _sig(x)
        return s * (1.0 + x * (1.0 - s))

    def body(dp_any, x_ref, xh_ref, d_ref, dh_ref, pre_ref, preh_ref, w_ref, o_ref, dw_ref, db_ref, ext_ref, dext_ref):
        b = pl.program_id(0)
        t = pl.program_id(1)

        @pl.when((b == 0) & (t == 0))
        def _():
            dw_ref[...] = jnp.zeros_like(dw_ref)
            db_ref[...] = jnp.zeros_like(db_ref)

        dc = d_ref[...].astype(F32) * silu_grad(pre_ref[...].astype(F32))
        dch = jnp.where(t == nt - 1, 0.0, dh_ref[...].astype(F32) * silu_grad(preh_ref[...].astype(F32)))
        dext_ref[0:tt, :] = dc
        dext_ref[tt:, :] = dch
        ext_ref[0:hr, :] = jnp.where(t == 0, 0.0, xh_ref[...].astype(F32))
        ext_ref[hr:, :] = x_ref[...].astype(F32)
        w = w_ref[...]
        acc = w[CONV_W - 1:CONV_W, :] * dc
        for j in range(CONV_W - 1):
            acc = acc + w[j:j + 1, :] * dext_ref[pl.ds(CONV_W - 1 - j, tt), :]
        trow = t * tt + lax.broadcasted_iota(jnp.int32, (tt, 1), 0)
        o_ref[...] = jnp.where(trow >= PAD, acc, 0.0).astype(o_ref.dtype)
        db_ref[...] += _colsum(dc)
        for j in range(CONV_W):
            dw_ref[j:j + 1, :] += _colsum(dc * ext_ref[pl.ds(off + j, tt), :])

    sd = jax.ShapeDtypeStruct
    nxt = lambda t: jnp.minimum((t + 1) * (tt // hr), lp // hr - 1)
    return _pcall(
        body, name="conv_bwd", grid=(bsz, nt),
        in_specs=[pl.BlockSpec(memory_space=pl.ANY),
                  pl.BlockSpec((None, tt, c), lambda b, t: (b, t, qb)),
                  pl.BlockSpec((None, hr, c), lambda b, t: (b, jnp.maximum(t * (tt // hr) - 1, 0), qb)),
                  pl.BlockSpec((None, tt, c), lambda b, t: (b, t, 0)),
                  pl.BlockSpec((None, hr, c), lambda b, t: (b, nxt(t), 0)),
                  pl.BlockSpec((None, tt, c), lambda b, t: (b, t, 0)),
                  pl.BlockSpec((None, hr, c), lambda b, t: (b, nxt(t), 0)),
                  _const((CONV_W, c))],
        out_specs=[pl.BlockSpec((None, tt, c), lambda b, t: (b, t, qb)), _const((CONV_W, c)), _const((1, c))],
        out_shape=[sd(dp3.shape, dp3.dtype), sd((CONV_W, c), F32), sd((1, c), F32)],
        scratch_shapes=[pltpu.VMEM((tt + hr, c), F32), pltpu.VMEM((tt + hr, c), F32)],
        input_output_aliases={0: 0},
        compiler_params=_cp(("arbitrary", "arbitrary")),
    )(dp3, p3, p3, dact3, dact3, pre3, pre3, cw)


def _mlstm_gates(g, h_idx, c_idx, lc):
    lane = lax.broadcasted_iota(jnp.int32, g.shape, 1)
    i_col = jnp.sum(jnp.where(lane == h_idx, g, 0.0), axis=1, keepdims=True)
    f_col = jnp.sum(jnp.where(lane == M_HEADS + h_idx, g, 0.0), axis=1, keepdims=True)
    row = lax.broadcasted_iota(jnp.int32, (lc, 1), 0)
    valid = (c_idx * lc + row) >= PAD
    li = jnp.where(valid, i_col, NEG)
    lf = jnp.where(valid, jnp.minimum(f_col, 0.0) - jnp.log(1.0 + jnp.exp(-jnp.abs(f_col))), 0.0)
    r2 = lax.broadcasted_iota(jnp.int32, (lc, lc), 0)
    c2 = lax.broadcasted_iota(jnp.int32, (lc, lc), 1)
    eye = r2 == c2
    tril = r2 >= c2
    to_row = lambda col: jnp.sum(jnp.where(eye, col, 0.0), axis=0, keepdims=True)
    lf_row = to_row(lf)
    b_col = jnp.sum(jnp.where(tril, lf_row, 0.0), axis=1, keepdims=True)
    b_row = to_row(b_col)
    li_row = to_row(li)
    d_mat = jnp.where(tril, b_col - b_row + li_row, NEG)
    return dict(f_col=f_col, valid=valid, li=li, b_col=b_col, d_mat=d_mat, eye=eye, r2=r2, c2=c2, row=row,
                to_row=to_row)


def _mlstm_chunk(q, ks, v, gq, c_st, n_st, m_st, lc):
    b_col, d_mat = gq["b_col"], gq["d_mat"]
    m_inter = b_col + m_st
    m_row = jnp.maximum(m_inter, jnp.max(d_mat, axis=1, keepdims=True))
    w_intra = jnp.exp(d_mat - m_row)
    w_inter = jnp.exp(m_inter - m_row)
    qb, kb, vb, cb = _bf(q), _bf(ks), _bf(v), _bf(c_st)
    s = _dot(qb, kb, NT) * w_intra
    qc = _dot(qb, cb)
    num = _dot(_bf(s), vb) + w_inter * qc
    qn = jnp.sum(q * n_st, axis=1, keepdims=True)
    den = jnp.sum(s, axis=1, keepdims=True) + w_inter * qn
    e = jnp.exp(-m_row)
    nn = jnp.maximum(jnp.abs(den), e)
    b_last = b_col[lc - 1:lc, :]
    g_log = b_last - b_col + gq["li"]
    m_new = jnp.maximum(b_last + m_st, jnp.max(g_log, axis=0, keepdims=True))
    w_k = jnp.exp(g_log - m_new)
    decay = jnp.exp(b_last + m_st - m_new)
    return dict(w_intra=w_intra, w_inter=w_inter, qb=qb, kb=kb, vb=vb, cb=cb, s=s, qc=qc, num=num, qn=qn, den=den,
                e=e, nn=nn, m_new=m_new, w_k=w_k, decay=decay)


def _chunks_per_step(nc):
    return max(c for c in (3, 2, 1) if nc % c == 0)


def _mlstm_fwd(qk3, p3, pg3):
    bsz, lp, _ = p3.shape
    lc = M_CHUNK
    nc = lp // lc
    dk, dv = 128, 256
    scale = dk ** -0.5

    cps = _chunks_per_step(nc)
    rows = cps * lc

    def body(q_ref, k_ref, v_ref, g_ref, h_ref, cs_ref, ns_ref, ms_ref, c_sc, n_sc, m_sc):
        st = pl.program_id(1)

        @pl.when(st == 0)
        def _():
            c_sc[...] = jnp.zeros_like(c_sc)
            n_sc[...] = jnp.zeros_like(n_sc)
            m_sc[...] = jnp.zeros_like(m_sc)

        for j in range(cps):
            rs = slice(j * lc, (j + 1) * lc)
            g = g_ref[rs, :]
            for hh in range(M_HEADS):
                c_st, n_st, m_all = c_sc[hh], n_sc[hh], m_sc[hh]
                cs_ref[hh, j] = c_st
                ns_ref[hh, j] = n_st
                ms_ref[hh, j] = m_all
                m_st = m_all[:, 0:1]
                q = q_ref[rs, hh * dk:(hh + 1) * dk].astype(F32)
                ks = k_ref[rs, hh * dk:(hh + 1) * dk].astype(F32) * scale
                v = v_ref[rs, hh * dv:(hh + 1) * dv]
                gq = _mlstm_gates(g, hh, st * cps + j, lc)
                f = _mlstm_chunk(q, ks, v, gq, c_st, n_st, m_st, lc)
                h_ref[rs, hh * dv:(hh + 1) * dv] = _bf(f["num"] / f["nn"])
                kw = ks * f["w_k"]
                c_sc[hh] = f["decay"] * c_st + _dot(_bf(kw), f["vb"], TN)
                n_sc[hh] = f["decay"] * n_st + _colsum(kw)
                m_sc[hh] = jnp.broadcast_to(f["m_new"], (1, 128))

    sd = jax.ShapeDtypeStruct
    nh = M_HEADS
    return _pcall(
        body, name="mlstm_fwd", grid=(bsz, nc // cps),
        in_specs=[pl.BlockSpec((None, rows, nh * dk), lambda b, c: (b, c, 0)),
                  pl.BlockSpec((None, rows, nh * dk), lambda b, c: (b, c, 1)),
                  pl.BlockSpec((None, rows, nh * dv), lambda b, c: (b, c, V_OFF // (nh * dv))),
                  pl.BlockSpec((None, rows, 128), lambda b, c: (b, c, 0))],
        out_specs=[pl.BlockSpec((None, rows, nh * dv), lambda b, c: (b, c, 0)),
                   pl.BlockSpec((None, nh, cps, dk, dv), lambda b, c: (b, 0, c, 0, 0)),
                   pl.BlockSpec((None, nh, cps, 1, dk), lambda b, c: (b, 0, c, 0, 0)),
                   pl.BlockSpec((None, nh, cps, 1, 128), lambda b, c: (b, 0, c, 0, 0))],
        out_shape=[sd((bsz, lp, nh * dv), BF16), sd((bsz, nh, nc, dk, dv), F32),
                   sd((bsz, nh, nc, 1, dk), F32), sd((bsz, nh, nc, 1, 128), F32)],
        scratch_shapes=[pltpu.VMEM((nh, dk, dv), F32), pltpu.VMEM((nh, 1, dk), F32), pltpu.VMEM((nh, 1, 128), F32)],
        compiler_params=_cp(("parallel", "arbitrary")),
    )(qk3, qk3, p3, pg3)


def _mlstm_bwd(dp3, qk3, p3, pg3, dh3, cs, ns, ms):
    bsz, lp, _ = p3.shape
    lc = M_CHUNK
    nc = lp // lc
    dk, dv = 128, 256
    scale = dk ** -0.5

    cps = _chunks_per_step(nc)
    nst = nc // cps
    rows = cps * lc

    def body(dp_any, q_ref, k_ref, v_ref, g_ref, dh_ref, cs_ref, ns_ref, ms_ref,
             dv_ref, dqk_ref, dg_ref, dc_sc, dn_sc):
        t = pl.program_id(1)
        st = nst - 1 - t

        @pl.when(t == 0)
        def _():
            dc_sc[...] = jnp.zeros_like(dc_sc)
            dn_sc[...] = jnp.zeros_like(dn_sc)

        lane = lax.broadcasted_iota(jnp.int32, (lc, 128), 1)
        for j in reversed(range(cps)):
            rs = slice(j * lc, (j + 1) * lc)
            g = g_ref[rs, :]
            dgate = jnp.zeros((lc, 128), F32)
            for hh in range(M_HEADS):
                dgate = head(hh, j, rs, st * cps + j, g, lane, dgate, q_ref, k_ref, v_ref, dh_ref, cs_ref, ns_ref,
                             ms_ref, dv_ref, dqk_ref, dc_sc, dn_sc)
            dg_ref[rs, :] = dgate.astype(dg_ref.dtype)

    def head(hh, j, sl, c, g, lane, dgate, q_ref, k_ref, v_ref, dh_ref, cs_ref, ns_ref, ms_ref, dv_ref, dqk_ref,
             dc_sc, dn_sc):
        c_st, n_st = cs_ref[hh, j], ns_ref[hh, j]
        m_st = ms_ref[hh, j][:, 0:1]
        q = q_ref[sl, hh * dk:(hh + 1) * dk].astype(F32)
        ks = k_ref[sl, hh * dk:(hh + 1) * dk].astype(F32) * scale
        v = v_ref[sl, hh * dv:(hh + 1) * dv]
        dh = dh_ref[sl, hh * dv:(hh + 1) * dv].astype(F32)
        gq = _mlstm_gates(g, hh, c, lc)
        f = _mlstm_chunk(q, ks, v, gq, c_st, n_st, m_st, lc)
        eye, r2, c2, row, valid = gq["eye"], gq["r2"], gq["c2"], gq["row"], gq["valid"]
        w_intra, w_inter, s, nn, den = f["w_intra"], f["w_inter"], f["s"], f["nn"], f["den"]
        qb, kb, vb, cb, w_k, decay = f["qb"], f["kb"], f["vb"], f["cb"], f["w_k"], f["decay"]
        d_c, d_n = dc_sc[hh], dn_sc[hh]
        d_cb = _bf(d_c)

        hout = f["num"] / nn
        dnum = dh / nn
        d_nn = -jnp.sum(dh * hout, axis=1, keepdims=True) / nn
        dden = jnp.where(jnp.abs(den) > f["e"], d_nn * jnp.sign(den), 0.0)
        wdnum = w_inter * dnum
        wdden = w_inter * dden
        ds = _dot(_bf(dnum), vb, NT) + dden
        dsw = _bf(ds * w_intra)
        dq = _dot(dsw, kb) + _dot(_bf(wdnum), cb, NT) + wdden * n_st
        dkw = _dot(vb, d_cb, NT) + d_n
        dks = _dot(dsw, qb, TN) + dkw * w_k
        kw = ks * w_k
        dvv = _dot(_bf(s), _bf(dnum), TN) + _dot(_bf(kw), d_cb)
        dd = ds * s
        rs = jnp.sum(dd, axis=1, keepdims=True)
        cs_col = jnp.sum(jnp.where(eye, jnp.sum(dd, axis=0, keepdims=True), 0.0), axis=1, keepdims=True)
        dwi = jnp.sum(dnum * f["qc"], axis=1, keepdims=True) + dden * f["qn"]
        db = rs - cs_col + dwi * w_inter
        dli = cs_col
        ddecay = jnp.sum(jnp.sum(d_c * c_st, axis=1, keepdims=True), axis=0, keepdims=True) \
            + jnp.sum(d_n * n_st, axis=1, keepdims=True)
        dgl = jnp.sum(dkw * ks, axis=1, keepdims=True) * w_k
        dblast = ddecay * decay + jnp.sum(dgl, axis=0, keepdims=True)
        db = db - dgl + jnp.where(row == lc - 1, dblast, 0.0)
        dli = dli + dgl
        db_row = gq["to_row"](db)
        dlf = jnp.sum(jnp.where(c2 >= r2, db_row, 0.0), axis=1, keepdims=True)
        dlf = jnp.where(valid, dlf, 0.0)
        dgate = jnp.where(lane == hh, jnp.where(valid, dli, 0.0), dgate)
        dgate = jnp.where(lane == M_HEADS + hh, dlf / (1.0 + jnp.exp(gq["f_col"])), dgate)
        dqk_ref[sl, hh * dk:(hh + 1) * dk] = _bf(dq)
        dqk_ref[sl, (M_HEADS + hh) * dk:(M_HEADS + hh + 1) * dk] = _bf(dks * scale)
        dv_ref[sl, hh * dv:(hh + 1) * dv] = dvv.astype(dv_ref.dtype)
        dc_sc[hh] = decay * d_c + _dot(qb, _bf(wdnum), TN)
        dn_sc[hh] = decay * d_n + _colsum(q * wdden)
        return dgate

    sd = jax.ShapeDtypeStruct
    nh = M_HEADS
    rc = lambda c: nst - 1 - c
    return _pcall(
        body, name="mlstm_bwd", grid=(bsz, nst),
        in_specs=[pl.BlockSpec(memory_space=pl.ANY),
                  pl.BlockSpec((None, rows, nh * dk), lambda b, c: (b, rc(c), 0)),
                  pl.BlockSpec((None, rows, nh * dk), lambda b, c: (b, rc(c), 1)),
                  pl.BlockSpec((None, rows, nh * dv), lambda b, c: (b, rc(c), V_OFF // (nh * dv))),
                  pl.BlockSpec((None, rows, 128), lambda b, c: (b, rc(c), 0)),
                  pl.BlockSpec((None, rows, nh * dv), lambda b, c: (b, rc(c), 0)),
                  pl.BlockSpec((None, nh, cps, dk, dv), lambda b, c: (b, 0, rc(c), 0, 0)),
                  pl.BlockSpec((None, nh, cps, 1, dk), lambda b, c: (b, 0, rc(c), 0, 0)),
                  pl.BlockSpec((None, nh, cps, 1, 128), lambda b, c: (b, 0, rc(c), 0, 0))],
        out_specs=[pl.BlockSpec((None, rows, nh * dv), lambda b, c: (b, rc(c), V_OFF // (nh * dv))),
                   pl.BlockSpec((None, rows, 2 * nh * dk), lambda b, c: (b, rc(c), 0)),
                   pl.BlockSpec((None, rows, 128), lambda b, c: (b, rc(c), 0))],
        out_shape=[sd(dp3.shape, dp3.dtype), sd((bsz, lp, 2 * nh * dk), BF16), sd((bsz, lp, 128), dp3.dtype)],
        scratch_shapes=[pltpu.VMEM((nh, dk, dv), F32), pltpu.VMEM((nh, 1, dk), F32)],
        input_output_aliases={0: 0},
        compiler_params=_cp(("arbitrary", "arbitrary")),
    )(dp3, qk3, qk3, p3, pg3, dh3, cs, ns, ms)


def _headnorm(x):
    dv = x.shape[1] // M_HEADS
    xh, rs = [], []
    for h in range(M_HEADS):
        xx = x[:, h * dv:(h + 1) * dv]
        mu = jnp.mean(xx, axis=-1, keepdims=True)
        xc = xx - mu
        rstd = lax.rsqrt(jnp.mean(xc * xc, axis=-1, keepdims=True) + LN_EPS)
        xh.append(xc * rstd)
        rs.append(rstd)
    return jnp.concatenate(xh, axis=1), rs


def _mix_fwd(hm, p, ys5g, h0, gn, wmo_bf, wo_bf, lp):
    r, d = hm.shape
    tm = _row_tile(lp, 384)

    def body(hm_ref, o_ref, gs_ref, gm_ref, ys_ref, h0_ref, gn_ref, wmo_ref, wo_ref,
             ymin_ref, mix_ref, r1_ref):
        xhat, _ = _headnorm(hm_ref[...].astype(F32))
        ymin = _bf(_sig(o_ref[...].astype(F32)) * (xhat * gn_ref[...]))
        ymin_ref[...] = ymin
        ym = _dot(ymin, wmo_ref[...])
        mix = _bf(_sig(gs_ref[...].astype(F32)) * ys_ref[...].astype(F32) + _sig(gm_ref[...].astype(F32)) * ym)
        mix_ref[...] = mix
        r1_ref[...] = ALPHA * h0_ref[...] + _dot(mix, wo_ref[...])

    sd = jax.ShapeDtypeStruct
    row = pl.BlockSpec((tm, d), lambda i: (i, 0))
    return _pcall(
        body, name="mix_fwd", grid=(r // tm,),
        in_specs=[row, pl.BlockSpec((tm, d), lambda i: (i, O_OFF // d)), pl.BlockSpec((tm, d), lambda i: (i, GS_OFF // d)),
                  pl.BlockSpec((tm, d), lambda i: (i, GM_OFF // d)), row, row, _const((1, d)),
                  _resident((d, d)), _resident((d, d))],
        out_specs=[row] * 3,
        out_shape=[sd((r, d), BF16), sd((r, d), BF16), sd((r, d), F32)],
        compiler_params=_cp(("parallel",), 48),
    )(hm, p, p, p, ys5g, h0, gn, wmo_bf, wo_bf)


def _mix_bwd(dr1, wo_bf, wmo_bf, p, ys5g, ymin, hm, gn, lp):
    r, d = hm.shape
    tm = _row_tile(lp, 384)
    dv = d // M_HEADS

    def body(dr1_ref, wo_ref, wmo_ref, o_ref, gs_ref, gm_ref, ys_ref, ym_ref, hm_ref, gn_ref,
             dp_ref, dys_ref, dym_ref, dhm_ref, dgn_ref):
        i = pl.program_id(0)

        @pl.when(i == 0)
        def _():
            dgn_ref[...] = jnp.zeros_like(dgn_ref)

        dmix = _dot(_bf(dr1_ref[...]), wo_ref[...], NT)
        sgs, sgm, so = (_sig(gs_ref[...].astype(F32)), _sig(gm_ref[...].astype(F32)), _sig(o_ref[...].astype(F32)))
        dys_ref[...] = _bf(dmix * sgs)
        dp_ref[:, d:2 * d] = _bf(dmix * ys_ref[...].astype(F32) * sgs * (1.0 - sgs))
        dym = dmix * sgm
        dym_ref[...] = _bf(dym)
        ym = _dot(ym_ref[...], wmo_ref[...])
        dp_ref[:, 2 * d:3 * d] = _bf(dmix * ym * sgm * (1.0 - sgm))
        dymin = _dot(_bf(dym), wmo_ref[...], NT)
        xhat, rs = _headnorm(hm_ref[...].astype(F32))
        gn_ = gn_ref[...]
        dp_ref[:, 0:d] = _bf(dymin * (xhat * gn_) * so * (1.0 - so))
        dhn = dymin * so
        dgn_ref[...] += _colsum(dhn * xhat)
        dxh = dhn * gn_
        for h in range(M_HEADS):
            sl = slice(h * dv, (h + 1) * dv)
            a, xh = dxh[:, sl], xhat[:, sl]
            m1 = jnp.mean(a, axis=-1, keepdims=True)
            m2 = jnp.mean(a * xh, axis=-1, keepdims=True)
            dhm_ref[:, sl] = _bf(rs[h] * (a - m1 - xh * m2))

    sd = jax.ShapeDtypeStruct
    row = pl.BlockSpec((tm, d), lambda i: (i, 0))
    vec = _const((1, d))
    return _pcall(
        body, name="mix_bwd", grid=(r // tm,),
        in_specs=[row, _resident((d, d)), _resident((d, d)),
                  pl.BlockSpec((tm, d), lambda i: (i, O_OFF // d)), pl.BlockSpec((tm, d), lambda i: (i, GS_OFF // d)),
                  pl.BlockSpec((tm, d), lambda i: (i, GM_OFF // d)), row, row, row, vec],
        out_specs=[pl.BlockSpec((tm, 3 * d), lambda i: (i, 0)), row, row, row, vec],
        out_shape=[sd((r, NP), BF16), sd((r, d), BF16), sd((r, d), BF16), sd((r, d), BF16), sd((1, d), F32)],
        compiler_params=_cp(("arbitrary",), 56),
    )(dr1, wo_bf, wmo_bf, p, p, p, ys5g, ymin, hm, gn)


def _mlp_fwd(r1, tgt, g1, b1, wup_g, wdn_bf, bup, g2, b2, lp):
    r, d = r1.shape
    tm = _row_tile(lp, 352)
    tps = lp // tm
    nf = wup_g.shape[0]

    def body(r1_ref, t_ref, g1_ref, b1_ref, wup_ref, wdn_ref, bup_ref, g2_ref, b2_ref,
             dr2_ref, h1b_ref, act_ref, loss_ref, dg2_ref, db2_ref):
        i = pl.program_id(0)

        @pl.when(i == 0)
        def _():
            loss_ref[...] = jnp.zeros_like(loss_ref)
            dg2_ref[...] = jnp.zeros_like(dg2_ref)
            db2_ref[...] = jnp.zeros_like(db2_ref)

        h1, _, _ = _ln_fwd(r1_ref[...], g1_ref[...], b1_ref[...])
        h1b = _bf(h1)
        h1b_ref[...] = h1b
        ff = jnp.zeros((tm, d), F32)
        for s in range(nf):
            up = _dot(h1b, wup_ref[s]) + bup_ref[:, s * d:(s + 1) * d]
            a = jnp.maximum(up, 0.0)
            a = _bf(a * a)
            act_ref[:, s * d:(s + 1) * d] = a
            ff = ff + _dot(a, wdn_ref[s * d:(s + 1) * d, :])
        r2 = ALPHA * h1 + ff
        g2 = g2_ref[...]
        y, xhat, rstd = _ln_fwd(r2, g2, b2_ref[...])
        t = (i % tps) * tm + lax.broadcasted_iota(jnp.int32, (tm, 1), 0)
        diff = jnp.where(t >= PAD + N_META, y - t_ref[...], 0.0)
        loss_ref[...] += 0.5 / d * jnp.sum(jnp.sum(diff * diff, axis=1, keepdims=True), axis=0, keepdims=True)
        dy = diff * (1.0 / d)
        dg2_ref[...] += _colsum(dy * xhat)
        db2_ref[...] += _colsum(dy)
        dr2_ref[...] = _ln_bwd(dy, xhat, rstd, g2)

    sd = jax.ShapeDtypeStruct
    row = pl.BlockSpec((tm, d), lambda i: (i, 0))
    vec = _const((1, d))
    return _pcall(
        body, name="mlp_fwd", grid=(r // tm,),
        in_specs=[row, row, vec, vec, _resident(wup_g.shape), _resident(wdn_bf.shape), _const((1, nf * d)), vec, vec],
        out_specs=[row, row, pl.BlockSpec((tm, nf * d), lambda i: (i, 0)), _const((1, 128)), vec, vec],
        out_shape=[sd((r, d), F32), sd((r, d), BF16), sd((r, nf * d), BF16), sd((1, 128), F32), sd((1, d), F32),
                   sd((1, d), F32)],
        compiler_params=_cp(("arbitrary",), 56),
    )(r1, tgt, g1, b1, wup_g, wdn_bf, bup, g2, b2)


def _mlp_bwd(h1b, dr2, r1, g1, wup_g, wdn_bf, bup, lp):
    r, d = h1b.shape
    tm = _row_tile(lp, 352)
    nf = wup_g.shape[0]

    def body(h1_ref, dr2_ref, r1_ref, g1_ref, wup_ref, wdn_ref, bup_ref, dr1_ref, dup_ref, dbup_ref, dg1_ref, db1_ref):
        i = pl.program_id(0)

        @pl.when(i == 0)
        def _():
            dbup_ref[...] = jnp.zeros_like(dbup_ref)
            dg1_ref[...] = jnp.zeros_like(dg1_ref)
            db1_ref[...] = jnp.zeros_like(db1_ref)

        h1b = h1_ref[...]
        dr2 = dr2_ref[...]
        dr2b = _bf(dr2)
        acc = ALPHA * dr2
        for s in range(nf):
            up = _dot(h1b, wup_ref[s]) + bup_ref[:, s * d:(s + 1) * d]
            dact = _dot(dr2b, wdn_ref[s * d:(s + 1) * d, :], NT)
            dup = dact * (2.0 * jnp.maximum(up, 0.0))
            dbup_ref[:, s * d:(s + 1) * d] += _colsum(dup)
            dupb = _bf(dup)
            dup_ref[:, s * d:(s + 1) * d] = dupb
            acc = acc + _dot(dupb, wup_ref[s], NT)
        g1 = g1_ref[...]
        _, xhat1, rstd1 = _ln_fwd(r1_ref[...], g1, 0.0)
        dr1_ref[...] = _ln_bwd(acc, xhat1, rstd1, g1)
        dg1_ref[...] += _colsum(acc * xhat1)
        db1_ref[...] += _colsum(acc)

    sd = jax.ShapeDtypeStruct
    row = pl.BlockSpec((tm, d), lambda i: (i, 0))
    vec = _const((1, d))
    return _pcall(
        body, name="mlp_bwd", grid=(r // tm,),
        in_specs=[row, row, row, vec, _resident(wup_g.shape), _resident(wdn_bf.shape), _const((1, nf * d))],
        out_specs=[row, pl.BlockSpec((tm, nf * d), lambda i: (i, 0)), _const((1, nf * d)), vec, vec],
        out_shape=[sd((r, d), F32), sd((r, nf * d), BF16), sd((1, nf * d), F32), sd((1, d), F32), sd((1, d), F32)],
        compiler_params=_cp(("arbitrary",), 56),
    )(h1b, dr2, r1, g1, wup_g, wdn_bf, bup)


def _s5_block_mats(bb_re_t, bb_im_t, c_re, c_im, ap_re, ap_im):
    ng = c_re.shape[0]
    gl = ng // S5_KCH
    eye = jnp.eye(gl, dtype=F32)

    def bmat(bt):
        bb = jnp.transpose(bt, (1, 0, 2)).reshape(S5_KCH, gl, S5_GROUP, S5_STATE)
        return jnp.einsum("kghp,gj->kghjp", bb, eye).reshape(S5_KCH, gl * S5_GROUP, gl * S5_STATE)

    def cmat(c):
        cc = c.reshape(S5_KCH, gl, S5_GROUP, S5_STATE)
        return jnp.einsum("kghp,gj->kjpgh", cc, eye).reshape(S5_KCH, gl * S5_STATE, gl * S5_GROUP)

    def pw(a):
        return jnp.transpose(a.reshape(8, S5_KCH, gl * S5_STATE), (1, 0, 2))

    bk = jnp.concatenate([bmat(bb_re_t), bmat(bb_im_t)], axis=-1)
    apow = jnp.concatenate([pw(ap_re), pw(ap_im)], axis=-1)
    return _bf(bk), _bf(cmat(c_re)), _bf(cmat(c_im)), apow


def _s5_block_grads(dbk, dcre, dcim, da):
    gl = dbk.shape[1] // S5_GROUP
    ng = gl * S5_KCH
    eye = jnp.eye(gl, dtype=F32)
    hw = gl * S5_STATE

    def bpart(x):
        x = x.reshape(S5_KCH, gl, S5_GROUP, gl, S5_STATE)
        x = jnp.einsum("kghjp,gj->kghp", x, eye).reshape(ng, S5_GROUP, S5_STATE)
        return jnp.transpose(x, (1, 0, 2))

    def cpart(x):
        x = x.reshape(S5_KCH, gl, S5_STATE, gl, S5_GROUP)
        return jnp.einsum("kjpgh,gj->kghp", x, eye).reshape(ng, S5_GROUP, S5_STATE)

    return (bpart(dbk[..., :hw]), bpart(dbk[..., hw:]), cpart(dcre), cpart(dcim),
            da[:, 0, :hw].reshape(ng, S5_STATE), da[:, 0, hw:].reshape(ng, S5_STATE))


def _tie(a, tok):
    return a if tok is None else a + tok[0, 0]


def _local_step(x, tgt, w, early=None, late=None, ready=None):
    ready = ready or (lambda names, g: None)
    bsz, seq, d = x.shape
    lp = PAD + N_META + seq
    r = bsz * lp
    tgtp = jnp.concatenate([jnp.zeros((bsz, PAD + N_META, d), F32), tgt], axis=1).reshape(r, d)

    h0, h0b = _ln0_fwd(x, w["meta_tokens"], w["ln0_g"], w["ln0_b"])
    if early is not None:
        w = {**w, **early((h0, tgtp))}
    p, pg = _inproj(h0b, w["w_in"], w["b_in"], lp)
    p3 = p.reshape(bsz, lp, NP)
    pg3 = pg.reshape(bsz, lp, 128)

    b_re_t = jnp.transpose(w["s5_b_re"], (2, 0, 1))
    b_im_t = jnp.transpose(w["s5_b_im"], (2, 0, 1))
    ap_re, ap_im, bb_re_t, bb_im_t = _s5_prep(w["s5_lambda_re"], w["s5_lambda_im"], w["s5_log_dt"], b_re_t, b_im_t)
    bk, cre, cim, apow = _s5_block_mats(bb_re_t, bb_im_t, w["s5_c_re"], w["s5_c_im"], ap_re, ap_im)
    y_s5, xs = _s5_fwd(p3, bk, cre, cim, apow, w["s5_d"])
    sw = y_s5.shape[-1]
    if late is not None:
        w = {**w, **late(y_s5)}
    gy, z, ys5g = _glu_fwd(y_s5.reshape(r, sw), w["s5_w_glu"], lp)

    pre3, qk3 = _conv_fwd(p3, w["qk_conv_w"], w["qk_conv_b"])
    hm3, cs, ns, ms = _mlstm_fwd(qk3, p3, pg3)
    hm = hm3.reshape(r, d)
    ymin, mix, r1 = _mix_fwd(hm, p, ys5g, h0, w["m_norm_g"], w["m_w_out"], w["w_o"], lp)
    dr2, h1b, act, loss, dg2, db2 = _mlp_fwd(r1, tgtp, w["ln1_g"], w["ln1_b"], w["w_up"], w["w_down"], w["b_up"],
                                             w["ln2_g"], w["ln2_b"], lp)

    g = {"ln2_g": dg2, "ln2_b": db2}
    dr1, dup, g["b_up"], g["ln1_g"], g["ln1_b"] = _mlp_bwd(h1b, dr2, r1, w["ln1_g"], w["w_up"], w["w_down"], w["b_up"], lp)
    g["w_down"] = _mm_tn(act, dr2, name="dw_down")
    g["w_up"] = _mm_tn(h1b, dup, name="dw_up", split=w["w_up"].shape[0])
    tok = ready(("w_down", "w_up"), g)
    dp, dys5g, dym, dhm, g["m_norm_g"] = _mix_bwd(
        dr1, w["w_o"], w["m_w_out"], p, ys5g, ymin, hm, _tie(w["m_norm_g"], tok), lp)
    g["w_o"] = _mm_tn(mix, dr1, name="dw_o")
    g["m_w_out"] = _mm_tn(ymin, dym, name="dw_mout")

    dp3 = dp.reshape(bsz, lp, NP)
    dp3, dqk3, dgate = _mlstm_bwd(dp3, qk3, p3, pg3, dhm.reshape(bsz, lp, d), cs, ns, ms)
    dp3, g["qk_conv_w"], g["qk_conv_b"] = _conv_bwd(dp3, p3, dqk3, pre3, w["qk_conv_w"])
    dz, dys5 = _glu_bwd(dys5g, z, y_s5.reshape(r, sw), w["s5_w_glu"], lp)
    g["s5_w_glu"] = _mm_tn(gy, dz, name="dw_glu", split=w["s5_w_glu"].shape[0])
    tok = ready(("s5_w_glu", "m_w_out", "w_o"), g)
    apow_rev = jnp.flip(apow, axis=1)
    dp3, dbk, dcre, dcim, da, g["s5_d"] = _s5_bwd(dp3, p3, dys5.reshape(bsz, lp, sw), xs, bk, cre, cim, apow_rev,
                                                 _tie(w["s5_d"], tok))
    dbb_re_t, dbb_im_t, g["s5_c_re"], g["s5_c_im"], da_re, da_im = _s5_block_grads(dbk, dcre, dcim, da)
    g["s5_lambda_re"], g["s5_lambda_im"], g["s5_log_dt"], gb_re_t, gb_im_t = _s5_prep_bwd(
        w["s5_lambda_re"], w["s5_lambda_im"], w["s5_log_dt"], b_re_t, b_im_t, da_re, da_im, dbb_re_t, dbb_im_t)
    g["s5_b_re"] = jnp.transpose(gb_re_t, (1, 2, 0))
    g["s5_b_im"] = jnp.transpose(gb_im_t, (1, 2, 0))

    dp3 = lax.dynamic_update_slice(dp3, dgate, (0, 0, G_OFF))
    dp = dp3.reshape(r, NP)
    g["w_in"], g["b_in"] = _mm_tn(h0b, dp, name="dw_in", colsum=True)
    tok = ready(("w_in",), g)
    dpw = _mm_nt(dp, w["w_in"], lp, name="dh0", dep=tok)
    grad_x, g["ln0_g"], g["ln0_b"], g["meta_tokens"] = _ln0_bwd(x, w["meta_tokens"], dr1, dpw, w["ln0_g"])
    return loss, grad_x, g


_ANY = pl.BlockSpec(memory_space=pl.ANY)
_MESH = pl.DeviceIdType.MESH


def _place():
    return lax.axis_index("x"), lax.axis_index("y"), lax.axis_index("c")


def _gather_chips(shards):
    n = len(shards)

    def body(*refs):
        ins, outs = refs[:n], refs[n:2 * n]
        send, recv, loc = refs[2 * n:]
        x, y, c = _place()
        me = 2 * x + y
        peers = [(1 - x, y), (x, 1 - y), (1 - x, 1 - y)]

        def rc(a, k, slot):
            px, py = peers[k]
            return pltpu.make_async_remote_copy(src_ref=ins[a], dst_ref=outs[a].at[slot], send_sem=send.at[a, k],
                                                recv_sem=recv.at[a, k], device_id=(px, py, c), device_id_type=_MESH)

        own = [pltpu.make_async_copy(ins[a], outs[a].at[me], loc.at[a]) for a in range(n)]
        for cp in own:
            cp.start()
        out = [rc(a, k, me) for a in range(n) for k in range(3)]
        for cp in out:
            cp.start()
        for a in range(n):
            for k in range(3):
                rc(a, k, 2 * peers[k][0] + peers[k][1]).wait_recv()
        for cp in out:
            cp.wait_send()
        for cp in own:
            cp.wait()

    return _pcall(
        body, name="gather_chips", in_specs=[_ANY] * n, out_specs=[_ANY] * n,
        out_shape=[jax.ShapeDtypeStruct((4,) + s.shape, s.dtype) for s in shards],
        scratch_shapes=[pltpu.SemaphoreType.DMA((n, 3)), pltpu.SemaphoreType.DMA((n, 3)), pltpu.SemaphoreType.DMA((n,))],
    )(*shards)


_HBM = pl.BlockSpec(memory_space=pltpu.HBM)
_SEM = pl.BlockSpec(memory_space=pltpu.SEMAPHORE)
_EFFECT = pltpu.SideEffectType.DATAFLOW_SIDE_EFFECTING


def _xchg_copies(srcs, lands, send, recv, scatter):
    x, y, c = _place()
    me = 2 * x + y
    peers = [(1 - x, y), (x, 1 - y), (1 - x, 1 - y)]
    out = []
    for a in range(len(srcs)):
        for k, (px, py) in enumerate(peers):
            src = srcs[a].at[2 * px + py] if scatter else srcs[a]
            dst = lands[a].at[k] if scatter else lands[a].at[me]
            out.append(pltpu.make_async_remote_copy(src_ref=src, dst_ref=dst, send_sem=send.at[3 * a + k],
                                                    recv_sem=recv.at[3 * a + k], device_id=(px, py, c),
                                                    device_id_type=_MESH))
    return out


def _xchg_start(srcs, lands, *, name, scatter, dep=None):
    n = len(srcs)
    deps = [] if dep is None else [dep]
    nd = len(deps)

    def body(*refs):
        send, recv = refs[2 * n + nd], refs[2 * n + nd + 1]
        for cp in _xchg_copies(refs[:n], refs[n:2 * n], send, recv, scatter):
            cp.start()
        refs[-1][...] = jnp.zeros_like(refs[-1])

    hbm = lambda a: pltpu.HBM(a.shape, a.dtype)
    con = lambda a: pltpu.with_memory_space_constraint(a, pltpu.HBM)
    res = _pcall(
        body, name=name, in_specs=[_HBM] * (2 * n) + [_ANY] * nd,
        out_specs=[_SEM, _SEM] + [_HBM] * (2 * n) + [pl.BlockSpec(memory_space=pltpu.VMEM)],
        out_shape=[pltpu.SemaphoreType.DMA((3 * n,)), pltpu.SemaphoreType.DMA((3 * n,))]
        + [hbm(a) for a in srcs] + [hbm(a) for a in lands] + [jax.ShapeDtypeStruct((8, 128), F32)],
        input_output_aliases={i: 2 + i for i in range(2 * n)},
        compiler_params=pltpu.CompilerParams(has_side_effects=_EFFECT),
    )(*[con(a) for a in srcs], *[con(a) for a in lands], *deps)
    return res[0], res[1], list(res[2:2 + n]), list(res[2 + n:2 + 2 * n]), res[-1]


def _xchg_wait(send, recv, srcs, lands, after, *, name, scatter):
    n = len(srcs)
    afters = list(after) if isinstance(after, (list, tuple)) else [after]

    def body(*refs):
        s_ref, r_ref = refs[2 * n], refs[2 * n + 1]
        for cp in _xchg_copies(refs[:n], refs[n:2 * n], s_ref, r_ref, scatter):
            cp.wait_send()
            cp.wait_recv()

    hbm = lambda a: pltpu.HBM(a.shape, a.dtype)
    res = _pcall(
        body, name=name, in_specs=[_HBM] * (2 * n) + [_SEM, _SEM] + [_ANY] * len(afters),
        out_specs=[_HBM] * (2 * n),
        out_shape=[hbm(a) for a in srcs] + [hbm(a) for a in lands],
        input_output_aliases={i: i for i in range(2 * n)},
        compiler_params=pltpu.CompilerParams(has_side_effects=_EFFECT),
    )(*srcs, *lands, send, recv, *afters)
    return list(res[:n]), list(res[n:])


def _swap_cores(arrs, name="swap_cores"):
    n = len(arrs)

    def body(*refs):
        ins, outs = refs[:n], refs[n:2 * n]
        send, recv = refs[2 * n:]
        x, y, c = _place()
        cps = [pltpu.make_async_remote_copy(src_ref=ins[a], dst_ref=outs[a], send_sem=send.at[a], recv_sem=recv.at[a],
                                            device_id=(x, y, 1 - c), device_id_type=_MESH) for a in range(n)]
        for cp in cps:
            cp.start()
        for cp in cps:
            cp.wait_recv()
        for cp in cps:
            cp.wait_send()

    return _pcall(
        body, name=name, in_specs=[_ANY] * n, out_specs=[_ANY] * n,
        out_shape=[jax.ShapeDtypeStruct(s.shape, s.dtype) for s in arrs],
        scratch_shapes=[pltpu.SemaphoreType.DMA((n,)), pltpu.SemaphoreType.DMA((n,))],
    )(*arrs)


def _allreduce_small(v, dep=None):
    rows = v.shape[0]
    half = rows // 2
    assert half % 8 == 0 and 2 * half == rows
    deps = [] if dep is None else [dep]

    def body(v_ref, *rest):
        out_ref, sib_ref, pair_ref, slots_ref, send, recv = rest[len(deps):]
        x, y, c = _place()
        chip = 2 * x + y
        sibling = (x, y, 1 - c)
        peers = [(1 - x, y), (x, 1 - y), (1 - x, 1 - y)]
        mine = pl.ds(pl.multiple_of(c * half, 8), half)

        first = pltpu.make_async_remote_copy(src_ref=v_ref, dst_ref=sib_ref, send_sem=send.at[0], recv_sem=recv.at[0],
                                             device_id=sibling, device_id_type=_MESH)
        first.start()
        first.wait_recv()
        pair_ref[...] = v_ref[...] + sib_ref[...]
        slots_ref[chip] = pair_ref[mine, :]
        cross = [pltpu.make_async_remote_copy(src_ref=pair_ref.at[mine], dst_ref=slots_ref.at[chip],
                                              send_sem=send.at[1 + k], recv_sem=recv.at[1 + k],
                                              device_id=(px, py, c), device_id_type=_MESH)
                 for k, (px, py) in enumerate(peers)]
        for cp in cross:
            cp.start()
        for cp in cross:
            cp.wait_recv()
        out_ref[mine, :] = ((slots_ref[0] + slots_ref[1]) + slots_ref[2]) + slots_ref[3]
        last = pltpu.make_async_remote_copy(src_ref=out_ref.at[mine], dst_ref=out_ref.at[mine], send_sem=send.at[4],
                                            recv_sem=recv.at[4], device_id=sibling, device_id_type=_MESH)
        last.start()
        last.wait_recv()
        first.wait_send()
        for cp in cross:
            cp.wait_send()
        last.wait_send()

    vm = pl.BlockSpec(memory_space=pltpu.VMEM)
    return _pcall(
        body, name="allreduce_small", in_specs=[vm] + [_ANY] * len(deps), out_specs=vm,
        out_shape=jax.ShapeDtypeStruct((rows, 128), F32),
        scratch_shapes=[pltpu.VMEM((rows, 128), F32), pltpu.VMEM((rows, 128), F32), pltpu.VMEM((4, half, 128), F32),
                        pltpu.SemaphoreType.DMA((5,)), pltpu.SemaphoreType.DMA((5,))],
        compiler_params=_cp(None, 40),
    )(v, *deps)


def _sum_slots(own, land):
    ns, rows, cols = land.shape
    tm = _row_tile(rows, 256, 8)

    def body(own_ref, a_ref, o_ref):
        o_ref[...] = ((own_ref[...] + a_ref[0]) + a_ref[1]) + a_ref[2]

    return _pcall(
        body, name="sum_slots", grid=(rows // tm,),
        in_specs=[pl.BlockSpec((tm, cols), lambda i: (i, 0)), pl.BlockSpec((ns, tm, cols), lambda i: (0, i, 0))],
        out_specs=pl.BlockSpec((tm, cols), lambda i: (i, 0)),
        out_shape=jax.ShapeDtypeStruct((rows, cols), F32),
        compiler_params=_cp(("parallel",), 40),
    )(own, land)


def _adamw(w, m, v, g0, g1=None):
    rows, cols = w.shape[-2:]
    lead = w.ndim == 3
    tm = _row_tile(rows, max(8, (1 << 20) // (4 * cols)), 8)
    c1 = 1.0 - ADAM_B1 ** ADAM_STEP
    c2 = 1.0 - ADAM_B2 ** ADAM_STEP
    two = g1 is not None

    def body(*refs):
        w_ref, m_ref, v_ref, g0_ref = refs[:4]
        g_ref, d_ref, nm_ref, nv_ref = refs[-4:]
        g = g0_ref[...]
        if two:
            g = g + refs[4][...]
        nm = ADAM_B1 * m_ref[...] + (1.0 - ADAM_B1) * g
        nv = ADAM_B2 * v_ref[...] + (1.0 - ADAM_B2) * (g * g)
        g_ref[...] = g
        nm_ref[...] = nm
        nv_ref[...] = nv
        d_ref[...] = -ADAM_LR * ((nm / c1) / (jnp.sqrt(nv / c2) + ADAM_EPS) + ADAM_WD * w_ref[...])

    blk = pl.BlockSpec((tm, cols), lambda i: (i, 0))
    wblk = pl.BlockSpec((None, tm, cols), lambda i: (0, i, 0)) if lead else blk
    ins = [w, m, v, g0] + ([g1] if two else [])
    return _pcall(
        body, name="adamw", grid=(rows // tm,), in_specs=[wblk] * 3 + [blk] * (len(ins) - 3), out_specs=[wblk] * 4,
        out_shape=[jax.ShapeDtypeStruct(w.shape, F32)] * 4,
        compiler_params=_cp(("parallel",), 40),
    )(*ins)


_BIG = ("w_in", "s5_w_glu", "m_w_out", "w_o", "w_up", "w_down")
_SMALL = ("ln0_g", "ln0_b", "b_in", "qk_conv_b", "s5_lambda_re", "s5_lambda_im", "s5_log_dt", "s5_b_re", "s5_b_im",
          "s5_c_re", "s5_c_im", "s5_d", "m_norm_g", "ln1_g", "ln1_b", "b_up", "ln2_g", "ln2_b")
_SMALL_SHARDED = ("meta_tokens", "qk_conv_w")
_ORDER = ("meta_tokens", "ln0_g", "ln0_b", "w_in", "b_in", "qk_conv_w", "qk_conv_b", "s5_lambda_re", "s5_lambda_im",
          "s5_log_dt", "s5_b_re", "s5_b_im", "s5_c_re", "s5_c_im", "s5_d", "s5_w_glu", "m_norm_g", "m_w_out", "w_o",
          "ln1_g", "ln1_b", "w_up", "b_up", "w_down", "ln2_g", "ln2_b")


def _pack(arrs):
    flat = jnp.concatenate([a.reshape(-1) for a in arrs])
    n = flat.shape[0]
    rows = -(-n // 2048) * 16
    return jnp.pad(flat, (0, rows * 128 - n)).reshape(rows, 128)


def _unpack(packed, shapes):
    flat = packed.reshape(-1)
    out, off = [], 0
    for s in shapes:
        n = math.prod(s)
        out.append(flat[off:off + n].reshape(s))
        off += n
    return out


def kernel(x, meta_tokens, ln0_g, ln0_b, w_in, b_in, qk_conv_w, qk_conv_b, s5_lambda_re, s5_lambda_im, s5_log_dt, s5_b_re, s5_b_im, s5_c_re, s5_c_im, s5_d, s5_w_glu, m_norm_g, m_w_out, w_o, ln1_g, ln1_b, w_up, b_up, w_down, ln2_g, ln2_b, loss_target, m_meta_tokens, m_ln0_g, m_ln0_b, m_w_in, m_b_in, m_qk_conv_w, m_qk_conv_b, m_s5_lambda_re, m_s5_lambda_im, m_s5_log_dt, m_s5_b_re, m_s5_b_im, m_s5_c_re, m_s5_c_im, m_s5_d, m_s5_w_glu, m_m_norm_g, m_m_w_out, m_w_o, m_ln1_g, m_ln1_b, m_w_up, m_b_up, m_w_down, m_ln2_g, m_ln2_b, v_meta_tokens, v_ln0_g, v_ln0_b, v_w_in, v_b_in, v_qk_conv_w, v_qk_conv_b, v_s5_lambda_re, v_s5_lambda_im, v_s5_log_dt, v_s5_b_re, v_s5_b_im, v_s5_c_re, v_s5_c_im, v_s5_d, v_s5_w_glu, v_m_norm_g, v_m_w_out, v_w_o, v_ln1_g, v_ln1_b, v_w_up, v_b_up, v_w_down, v_ln2_g, v_ln2_b):
    wts = dict(meta_tokens=meta_tokens, ln0_g=ln0_g, ln0_b=ln0_b, w_in=w_in, b_in=b_in, qk_conv_w=qk_conv_w,
               qk_conv_b=qk_conv_b, s5_lambda_re=s5_lambda_re, s5_lambda_im=s5_lambda_im, s5_log_dt=s5_log_dt,
               s5_b_re=s5_b_re, s5_b_im=s5_b_im, s5_c_re=s5_c_re, s5_c_im=s5_c_im, s5_d=s5_d, s5_w_glu=s5_w_glu,
               m_norm_g=m_norm_g, m_w_out=m_w_out, w_o=w_o, ln1_g=ln1_g, ln1_b=ln1_b, w_up=w_up, b_up=b_up,
               w_down=w_down, ln2_g=ln2_g, ln2_b=ln2_b)
    mom = dict(meta_tokens=m_meta_tokens, ln0_g=m_ln0_g, ln0_b=m_ln0_b, w_in=m_w_in, b_in=m_b_in, qk_conv_w=m_qk_conv_w,
               qk_conv_b=m_qk_conv_b, s5_lambda_re=m_s5_lambda_re, s5_lambda_im=m_s5_lambda_im, s5_log_dt=m_s5_log_dt,
               s5_b_re=m_s5_b_re, s5_b_im=m_s5_b_im, s5_c_re=m_s5_c_re, s5_c_im=m_s5_c_im, s5_d=m_s5_d,
               s5_w_glu=m_s5_w_glu, m_norm_g=m_m_norm_g, m_w_out=m_m_w_out, w_o=m_w_o, ln1_g=m_ln1_g, ln1_b=m_ln1_b,
               w_up=m_w_up, b_up=m_b_up, w_down=m_w_down, ln2_g=m_ln2_g, ln2_b=m_ln2_b)
    var = dict(meta_tokens=v_meta_tokens, ln0_g=v_ln0_g, ln0_b=v_ln0_b, w_in=v_w_in, b_in=v_b_in, qk_conv_w=v_qk_conv_w,
               qk_conv_b=v_qk_conv_b, s5_lambda_re=v_s5_lambda_re, s5_lambda_im=v_s5_lambda_im, s5_log_dt=v_s5_log_dt,
               s5_b_re=v_s5_b_re, s5_b_im=v_s5_b_im, s5_c_re=v_s5_c_re, s5_c_im=v_s5_c_im, s5_d=v_s5_d,
               s5_w_glu=v_s5_w_glu, m_norm_g=v_m_norm_g, m_w_out=v_m_w_out, w_o=v_w_o, ln1_g=v_ln1_g, ln1_b=v_ln1_b,
               w_up=v_w_up, b_up=v_b_up, w_down=v_w_down, ln2_g=v_ln2_g, ln2_b=v_ln2_b)
    d = x.shape[-1]
    chip = 2 * lax.axis_index("x") + lax.axis_index("y")

    gw = dict(zip(_SMALL_SHARDED, _gather_chips([meta_tokens, qk_conv_w[0]])))
    own_w_in = _bf(w_in[0])
    fsend, frecv, fsrc, fland, ftok = _xchg_start([own_w_in], [lax.empty((4,) + own_w_in.shape, BF16)],
                                                  name="gather_w_in_start", scatter=False, dep=gw["qk_conv_w"])
    late_names = tuple(n for n in _BIG if n != "w_in")
    cat = lambda a: jnp.transpose(a, (1, 0, 2)).reshape(a.shape[1], 4 * a.shape[2])
    w = dict(
        meta_tokens=cat(gw["meta_tokens"]), ln0_g=ln0_g[None], ln0_b=_tie(ln0_b[None], ftok),
        qk_conv_w=cat(gw["qk_conv_w"]), qk_conv_b=qk_conv_b,
        s5_lambda_re=s5_lambda_re[0], s5_lambda_im=s5_lambda_im[0], s5_log_dt=s5_log_dt[0][:, None],
        s5_b_re=s5_b_re[0], s5_b_im=s5_b_im[0], s5_c_re=s5_c_re[0], s5_c_im=s5_c_im[0], s5_d=s5_d,
        m_norm_g=m_norm_g, ln1_g=ln1_g, ln1_b=ln1_b, b_up=b_up, ln2_g=ln2_g, ln2_b=ln2_b)
    in_flight = {}

    def place_own(src, land):
        return lax.dynamic_update_slice(land, src[None], (chip,) + (0,) * src.ndim)

    small_names = _SMALL + _SMALL_SHARDED

    def view(n, a):
        return jnp.swapaxes(a, -1, -2) if n in ("s5_b_re", "s5_b_im") else a

    small_wmv = [_pack([view(n, dct[n]) for n in small_names]) for dct in (wts, mom, var)]

    def early(after):
        src, land = _xchg_wait(fsend, frecv, fsrc, fland, tuple(after) + tuple(small_wmv), name="gather_w_in_wait",
                               scatter=False)
        late_src = [_bf(wts[n][0]) for n in late_names]
        st = _xchg_start(late_src, [lax.empty((4,) + a.shape, a.dtype) for a in late_src], name="gather_late_start",
                         scatter=False, dep=src[0])
        in_flight["late"] = st[:4]
        return dict(w_in=_w_in_from_slots(place_own(src[0], land[0]), IN_CHUNK), b_in=_tie(_to_pad_cols(b_in), st[4]))

    def late(after):
        src, land = _xchg_wait(*in_flight["late"], after, name="gather_late_wait", scatter=False)
        full = {n: place_own(s, ld) for n, s, ld in zip(late_names, src, land)}
        return dict(s5_w_glu=full["s5_w_glu"], m_w_out=full["m_w_out"].reshape(d, d), w_o=full["w_o"].reshape(d, d),
                    w_up=full["w_up"], w_down=full["w_down"].reshape(4 * d, d))

    flying = []

    def ready(names, g):
        parts = dict(
            w_in=lambda: _slots_from_w_in(g["w_in"][0]), s5_w_glu=lambda: g["s5_w_glu"],
            m_w_out=lambda: g["m_w_out"].reshape(4, d // 4, d), w_o=lambda: g["w_o"].reshape(4, d // 4, d),
            w_up=lambda: g["w_up"], w_down=lambda: g["w_down"].reshape(4, d, d))
        src = [parts[n]() for n in names]
        land = [lax.empty((3,) + a.shape[1:], a.dtype) for a in src]
        st = _xchg_start(src, land, name="scatter_" + names[0] + "_start", scatter=True)
        flying.append((names,) + st[:4])
        return st[4]

    loss, grad_x, g = _local_step(x, loss_target, w, early, late, ready)
    g["b_in"] = _from_pad_cols(g["b_in"])

    res = {}

    def flat(a):
        return jnp.swapaxes(a, -1, -2).reshape(a.shape[:-2] + (-1, 128))

    def unflat(y, shape):
        return jnp.swapaxes(y.reshape(shape[:-2] + (shape[-1], shape[-2])), -1, -2)

    def finish(groups, after, tag):
        mine = {}
        for names, send, recv, src, land in groups:
            src, land = _xchg_wait(send, recv, src, land, after, name="scatter_" + names[0] + "_wait", scatter=True)
            for n, s, ld in zip(names, src, land):
                mine[n] = _sum_slots(lax.dynamic_index_in_dim(s, chip, 0, keepdims=False), ld)
        theirs = _swap_cores(list(mine.values()), name="swap_cores_" + tag)
        for n, t in zip(mine, theirs):
            if n == "w_in":
                res[n] = [unflat(r, wts[n].shape) for r in _adamw(flat(wts[n]), flat(mom[n]), flat(var[n]),
                                                                  flat(mine[n]), flat(t))]
            else:
                res[n] = _adamw(wts[n], mom[n], var[n], mine[n], t)

    finish(flying[:-1], g["ln0_g"], "a")

    small_shapes = [(1, 128)] + [view(n, wts[n]).shape for n in _SMALL] + [g[n].shape for n in _SMALL_SHARDED]
    packed = _pack([loss] + [view(n, g[n]) for n in _SMALL] + [g[n] for n in _SMALL_SHARDED])
    tot = _unpack(_allreduce_small(packed, dep=res["w_o"][3]), small_shapes)
    loss_out = tot[0][0, 0]
    gsm = dict(zip(_SMALL + _SMALL_SHARDED, tot[1:]))
    for n in _SMALL_SHARDED:
        cols = wts[n].shape[-1]
        gsm[n] = lax.dynamic_slice_in_dim(gsm[n], chip * cols, cols, axis=1).reshape(wts[n].shape)

    names = small_names
    shapes = [view(n, wts[n]).shape for n in names]
    small_out = _adamw(*small_wmv, _pack([gsm[n] for n in names]))
    small_res = [_unpack(r, shapes) for r in small_out]
    for j, n in enumerate(names):
        res[n] = [view(n, small_res[q][j]) for q in range(4)]
    finish(flying[-1:], small_out[0], "b")

    return (loss_out, grad_x, *[res[n][0] for n in _ORDER], *[res[n][1] for n in _ORDER],
            *[res[n][2] for n in _ORDER], *[res[n][3] for n in _ORDER])
```

```python
import functools
import math

import jax
import jax.numpy as jnp
from jax import lax
from jax.experimental import pallas as pl
from jax.experimental.pallas import tpu as pltpu

F32 = jnp.float32
BF16 = jnp.bfloat16
HI = lax.Precision.HIGHEST

N_META = 16
M_HEADS = 4
M_CHUNK = 128
PAD = M_CHUNK - N_META
CONV_W = 4
HALO_ROWS = 16
S5_GROUP = 16
S5_STATE = 64
S5_KCH = 4
LN_EPS = 1e-5
ALPHA = 2.0 ** 0.25
NEG = -1e30
ADAM_LR, ADAM_B1, ADAM_B2, ADAM_EPS, ADAM_WD, ADAM_STEP = 0.001, 0.9, 0.999, 1e-08, 0.01, 10

O_OFF, GS_OFF, GM_OFF, V_OFF, Q_OFF, K_OFF, U_OFF, G_OFF, NP = 0, 1024, 2048, 3072, 4096, 4608, 5120, 5632, 5760

NN = ((1,), (0,))
NT = ((1,), (1,))
TN = ((0,), (0,))


def _dot(a, b, dims=NN, prec=None):
    return lax.dot_general(a, b, (dims, ((), ())), preferred_element_type=F32, precision=prec)


def _bf(x):
    return x.astype(BF16)


def _sig(x):
    return 0.5 * jnp.tanh(0.5 * x) + 0.5


def _pcall(body, **kw):
    return pl.pallas_call(body, **kw)


def _cp(sem=None, vmem_mb=None):
    kw = {}
    if sem is not None:
        kw["dimension_semantics"] = sem
    if vmem_mb is not None:
        kw["vmem_limit_bytes"] = vmem_mb << 20
    return pltpu.CompilerParams(**kw)


def _row_tile(n, want, mult=16):
    best = None
    for t in range(mult, want + 1, mult):
        if n % t == 0:
            best = t
    assert best is not None, (n, want)
    return best


def _resident(shape):
    nd = len(shape)
    return pl.BlockSpec(shape, lambda *_: (0,) * nd, pipeline_mode=pl.Buffered(1))


def _const(shape):
    nd = len(shape)
    return pl.BlockSpec(shape, lambda *_: (0,) * nd)


def _ln_fwd(x, g, b):
    mu = jnp.mean(x, axis=-1, keepdims=True)
    xc = x - mu
    var = jnp.mean(xc * xc, axis=-1, keepdims=True)
    rstd = lax.rsqrt(var + LN_EPS)
    xhat = xc * rstd
    return xhat * g + b, xhat, rstd


def _ln_bwd(dy, xhat, rstd, g):
    dxh = dy * g
    m1 = jnp.mean(dxh, axis=-1, keepdims=True)
    m2 = jnp.mean(dxh * xhat, axis=-1, keepdims=True)
    return rstd * (dxh - m1 - xhat * m2)


def _colsum(x):
    return jnp.sum(x, axis=0, keepdims=True)


def _to_pad_cols(w):
    u, q, k, v, o, gi, gf, gs, gm = (w[..., 0:512], w[..., 512:1024], w[..., 1024:1536], w[..., 1536:2560],
                                     w[..., 2560:3584], w[..., 3584:3588], w[..., 3588:3592], w[..., 3592:4616],
                                     w[..., 4616:5640])
    z = jnp.zeros(w.shape[:-1] + (NP - G_OFF - 8,), w.dtype)
    return jnp.concatenate([o, gs, gm, v, q, k, u, gi, gf, z], axis=-1)


def _from_pad_cols(w):
    o, gs, gm, v, q, k, u = (w[..., O_OFF:GS_OFF], w[..., GS_OFF:GM_OFF], w[..., GM_OFF:V_OFF], w[..., V_OFF:Q_OFF],
                             w[..., Q_OFF:K_OFF], w[..., K_OFF:U_OFF], w[..., U_OFF:G_OFF])
    gi, gf = w[..., G_OFF:G_OFF + 4], w[..., G_OFF + 4:G_OFF + 8]
    return jnp.concatenate([u, q, k, v, o, gi, gf, gs, gm], axis=-1)


_IN_REF = (("u", 512), ("q", 512), ("k", 512), ("v", 1024), ("o", 1024), ("i", 4), ("f", 4), ("gs", 1024), ("gm", 1024))
_IN_PAD = (("o", O_OFF), ("gs", GS_OFF), ("gm", GM_OFF), ("v", V_OFF), ("q", Q_OFF), ("k", K_OFF), ("u", U_OFF),
           ("i", G_OFF), ("f", G_OFF + 4))


def _in_ref_ranges():
    out, off = {}, 0
    for n, s in _IN_REF:
        out[n] = (off, off + s)
        off += s
    return out, off


def _w_in_from_slots(g, chunk=None):
    rng, total = _in_ref_ranges()
    width = total // g.shape[0]
    cols = []
    for n, _ in _IN_PAD:
        a, b = rng[n]
        while a < b:
            s = a // width
            e = min(b, (s + 1) * width)
            cols.append(g[s][:, a - s * width:e - s * width])
            a = e
    cols.append(jnp.zeros((g.shape[1], NP - G_OFF - 8), g.dtype))
    if chunk is None:
        return jnp.concatenate(cols, axis=1)
    chunks, cur, room = [], [], chunk
    for c in cols:
        while c.shape[1] > 0:
            take = min(room, c.shape[1])
            cur.append(c[:, :take])
            c, room = c[:, take:], room - take
            if room == 0:
                chunks.append(jnp.concatenate(cur, axis=1))
                cur, room = [], chunk
    assert not cur
    return jnp.stack(chunks, axis=0)


def _slots_from_w_in(wp, nslot=4):
    rng, total = _in_ref_ranges()
    width = total // nslot
    pad_off = dict(_IN_PAD)
    slots = []
    for s in range(nslot):
        lo, hi = s * width, (s + 1) * width
        cols = []
        for n, _ in _IN_REF:
            a, b = rng[n]
            x0, x1 = max(a, lo), min(b, hi)
            if x0 < x1:
                cols.append(wp[:, pad_off[n] + x0 - a:pad_off[n] + x1 - a])
        slots.append(jnp.concatenate(cols, axis=1))
    return jnp.stack(slots, axis=0)


HEAD = PAD + N_META


def _ln0_in(j, x_ref, meta_ref):
    first = jnp.concatenate([jnp.zeros((PAD, meta_ref.shape[1]), F32), meta_ref[...]], axis=0)
    return jnp.where(j == 0, first[None], x_ref[...])


def _ln0_fwd(x, meta, g, b):
    bsz, seq, d = x.shape
    nb = seq // HEAD + 1

    def body(x_ref, m_ref, g_ref, b_ref, o_ref, ob_ref):
        y, _, _ = _ln_fwd(_ln0_in(pl.program_id(0), x_ref, m_ref), g_ref[...], b_ref[...])
        o_ref[...] = y
        ob_ref[...] = _bf(y)

    row = pl.BlockSpec((bsz, HEAD, d), lambda j: (0, j, 0))
    h0, h0b = _pcall(
        body, name="ln0_fwd", grid=(nb,),
        in_specs=[pl.BlockSpec((bsz, HEAD, d), lambda j: (0, jnp.maximum(j - 1, 0), 0)), _const((N_META, d)),
                  _const((1, d)), _const((1, d))],
        out_specs=[row, row],
        out_shape=[jax.ShapeDtypeStruct((bsz, nb * HEAD, d), F32), jax.ShapeDtypeStruct((bsz, nb * HEAD, d), BF16)],
        compiler_params=_cp(("arbitrary",)),
    )(x, meta, g, b)
    return h0.reshape(-1, d), h0b.reshape(-1, d)


def _ln0_bwd(x, meta, dr1, dpw, g):
    bsz, seq, d = x.shape
    nb = seq // HEAD + 1

    def body(x_ref, m_ref, a_ref, c_ref, g_ref, o_ref, dg_ref, db_ref, dm_ref):
        j = pl.program_id(0)

        @pl.when(j == 0)
        def _():
            dg_ref[...] = jnp.zeros_like(dg_ref)
            db_ref[...] = jnp.zeros_like(db_ref)
            dm_ref[...] = jnp.zeros_like(dm_ref)

        dy = ALPHA * a_ref[...] + c_ref[...]
        _, xhat, rstd = _ln_fwd(_ln0_in(j, x_ref, m_ref), g_ref[...], 0.0)
        dx = _ln_bwd(dy, xhat, rstd, g_ref[...])
        o_ref[...] = dx
        dg_ref[...] += _colsum((dy * xhat).reshape(bsz * HEAD, d))
        db_ref[...] += _colsum(dy.reshape(bsz * HEAD, d))

        @pl.when(j == 0)
        def _():
            dm_ref[...] += jnp.sum(dx[:, PAD:, :], axis=0)

    row = pl.BlockSpec((bsz, HEAD, d), lambda j: (0, j, 0))
    tok = pl.BlockSpec((bsz, HEAD, d), lambda j: (0, jnp.maximum(j - 1, 0), 0))
    lp = nb * HEAD
    return _pcall(
        body, name="ln0_bwd", grid=(nb,),
        in_specs=[tok, _const((N_META, d)), row, row, _const((1, d))],
        out_specs=[tok, _const((1, d)), _const((1, d)), _const((N_META, d))],
        out_shape=[jax.ShapeDtypeStruct((bsz, seq, d), F32), jax.ShapeDtypeStruct((1, d), F32),
                   jax.ShapeDtypeStruct((1, d), F32), jax.ShapeDtypeStruct((N_META, d), F32)],
        compiler_params=_cp(("arbitrary",)),
    )(x, meta, dr1.reshape(bsz, lp, d), dpw.reshape(bsz, lp, d), g)


IN_CHUNK = 1152


def _chunk_cols(w):
    k, n = w.shape
    return jnp.transpose(w.reshape(k, n // IN_CHUNK, IN_CHUNK), (1, 0, 2))


def _inproj(h0b, w3, bias, lp):
    r, d = h0b.shape
    nj, _, tn = w3.shape
    tm = _row_tile(lp, 1056)
    tps = lp // tm

    def body(a_ref, w_ref, b_ref, o_ref, gate_ref):
        i = pl.program_id(0)
        j = pl.program_id(1)
        acc = _dot(a_ref[...], w_ref[j]) + b_ref[...]
        t = (i % tps) * tm + lax.broadcasted_iota(jnp.int32, (tm, 1), 0)
        acc = jnp.where(t >= PAD, acc, 0.0)
        o_ref[...] = _bf(acc)

        @pl.when(j == nj - 1)
        def _():
            gate_ref[...] = acc[:, tn - 128:]

    return _pcall(
        body, name="inproj", grid=(r // tm, nj),
        in_specs=[pl.BlockSpec((tm, d), lambda i, j: (i, 0)), _resident(w3.shape),
                  pl.BlockSpec((1, tn), lambda i, j: (0, j))],
        out_specs=[pl.BlockSpec((tm, tn), lambda i, j: (i, j)), pl.BlockSpec((tm, 128), lambda i, j: (i, 0))],
        out_shape=[jax.ShapeDtypeStruct((r, nj * tn), BF16), jax.ShapeDtypeStruct((r, 128), F32)],
        compiler_params=_cp(("parallel", "arbitrary"), 48),
    )(h0b, w3, bias)


def _mm_tn(a, b, *, name, split=1, colsum=False, tk_want=2112):
    r, m = a.shape
    n = b.shape[1]
    tk = _row_tile(r, tk_want)
    tm = min(m, 1024)
    ns = n // split
    tn = ns
    for cand in (1024, 1152, 640, 512, 128):
        if ns % cand == 0 and cand <= ns:
            tn = cand
            break
    nb = ns // tn
    nk = r // tk

    def body(a_ref, b_ref, o_ref, *rest):
        acc = rest[-1]
        k = pl.program_id(2)

        @pl.when(k == 0)
        def _():
            acc[...] = jnp.zeros_like(acc)

        bt = b_ref[...]
        acc[...] += _dot(_bf(a_ref[...]), _bf(bt), TN)

        @pl.when(k == nk - 1)
        def _():
            o_ref[...] = acc[...]

        if colsum:
            cs_ref = rest[0]

            @pl.when(k == 0)
            def _():
                cs_ref[...] = jnp.zeros_like(cs_ref)

            cs_ref[...] += _colsum(bt.astype(F32))

    out_specs = [pl.BlockSpec((None, tm, tn), lambda i, j, k: (j // nb, i, j % nb))]
    out_shape = [jax.ShapeDtypeStruct((split, m, ns), F32)]
    if colsum:
        assert m == tm
        out_specs.append(pl.BlockSpec((1, tn), lambda i, j, k: (0, j)))
        out_shape.append(jax.ShapeDtypeStruct((1, n), F32))
    res = _pcall(
        body, name=name, grid=(m // tm, n // tn, nk),
        in_specs=[pl.BlockSpec((tk, tm), lambda i, j, k: (k, i)), pl.BlockSpec((tk, tn), lambda i, j, k: (k, j))],
        out_specs=out_specs, out_shape=out_shape,
        scratch_shapes=[pltpu.VMEM((tm, tn), F32)],
        compiler_params=_cp(("parallel", "parallel", "arbitrary"), 56),
    )(a, b)
    return res if colsum else res[0]


def _mm_nt(a, w3, lp, *, name, dep=None):
    r, kdim = a.shape
    nk, n, tk = w3.shape
    assert nk * tk == kdim
    tm = _row_tile(lp, 1056)
    deps = [] if dep is None else [dep]

    def body(a_ref, w_ref, *rest):
        o_ref, acc = rest[-2:]
        k = pl.program_id(1)

        @pl.when(k == 0)
        def _():
            acc[...] = jnp.zeros_like(acc)

        acc[...] += _dot(_bf(a_ref[...]), w_ref[k], NT)

        @pl.when(k == nk - 1)
        def _():
            o_ref[...] = acc[...]

    return _pcall(
        body, name=name, grid=(r // tm, nk),
        in_specs=[pl.BlockSpec((tm, tk), lambda i, k: (i, k)), _resident(w3.shape)]
        + [_const(dp_.shape) for dp_ in deps],
        out_specs=pl.BlockSpec((tm, n), lambda i, k: (i, 0)),
        out_shape=jax.ShapeDtypeStruct((r, n), F32),
        scratch_shapes=[pltpu.VMEM((tm, n), F32)],
        compiler_params=_cp(("parallel", "arbitrary"), 48),
    )(a, w3, *deps)


def _s5_prep(lam_re, lam_im, log_dt, b_re_t, b_im_t):
    g, p = lam_re.shape
    h = b_re_t.shape[0]

    def body(lr_ref, li_ref, ldt_ref, br_ref, bi_ref, pr_ref, pi_ref, bbr_ref, bbi_ref):
        lr, li = lr_ref[...], li_ref[...]
        dt = jnp.exp(ldt_ref[...])
        e = jnp.exp(lr * dt)
        ar, ai = e * jnp.cos(li * dt), e * jnp.sin(li * dt)
        den = lr * lr + li * li
        cr = ((ar - 1.0) * lr + ai * li) / den
        ci = (ai * lr - (ar - 1.0) * li) / den
        br, bi = br_ref[...], bi_ref[...]
        bbr_ref[...] = cr[None] * br - ci[None] * bi
        bbi_ref[...] = cr[None] * bi + ci[None] * br
        xr, xi = ar, ai
        pr_ref[0] = xr
        pi_ref[0] = xi
        for t in range(1, 8):
            xr, xi = xr * ar - xi * ai, xr * ai + xi * ar
            pr_ref[t] = xr
            pi_ref[t] = xi

    sd = jax.ShapeDtypeStruct
    return _pcall(body, name="s5_prep",
                  out_shape=[sd((8, g, p), F32), sd((8, g, p), F32), sd((h, g, p), F32), sd((h, g, p), F32)])(
        lam_re, lam_im, log_dt, b_re_t, b_im_t)


def _s5_prep_bwd(lam_re, lam_im, log_dt, b_re_t, b_im_t, da_re, da_im, dbb_re_t, dbb_im_t):
    g, p = lam_re.shape
    h = b_re_t.shape[0]

    def body(lr_ref, li_ref, ldt_ref, br_ref, bi_ref, dar_ref, dai_ref, dbr_ref, dbi_ref,
             glr_ref, gli_ref, gdt_ref, gbr_ref, gbi_ref):
        lr, li = lr_ref[...], li_ref[...]
        dt = jnp.exp(ldt_ref[...])
        e = jnp.exp(lr * dt)
        ar, ai = e * jnp.cos(li * dt), e * jnp.sin(li * dt)
        den = lr * lr + li * li
        cr = ((ar - 1.0) * lr + ai * li) / den
        ci = (ai * lr - (ar - 1.0) * li) / den
        br, bi = br_ref[...], bi_ref[...]
        gr, gi = dbr_ref[...], dbi_ref[...]
        gbr_ref[...] = gr * cr[None] + gi * ci[None]
        gbi_ref[...] = gi * cr[None] - gr * ci[None]
        gcr = jnp.sum(gr * br + gi * bi, axis=0)
        gci = jnp.sum(gi * br - gr * bi, axis=0)
        ilr, ili = lr / den, -li / den
        gar = dar_ref[...] + gcr * ilr + gci * ili
        gai = dai_ref[...] + gci * ilr - gcr * ili
        qr, qi = cr * ilr - ci * ili, cr * ili + ci * ilr
        glr = -(gcr * qr + gci * qi)
        gli = -(gci * qr - gcr * qi)
        gzr = gar * ar + gai * ai
        gzi = gai * ar - gar * ai
        glr_ref[...] = glr + gzr * dt
        gli_ref[...] = gli + gzi * dt
        gdt_ref[...] = jnp.sum(gzr * lr + gzi * li, axis=1, keepdims=True) * dt

    sd = jax.ShapeDtypeStruct
    return _pcall(body, name="s5_prep_bwd",
                  out_shape=[sd((g, p), F32), sd((g, p), F32), sd((g, 1), F32), sd((h, g, p), F32), sd((h, g, p), F32)])(
        lam_re, lam_im, log_dt, b_re_t, b_im_t, da_re, da_im, dbb_re_t, dbb_im_t)


def _cmul(xr, xi, yr, yi):
    return xr * yr - xi * yi, xr * yi + xi * yr


def _dot5(a, b, dims=NN):
    return _dot(_bf(a), _bf(b), dims)


def _s5_fwd(p3, bk, cre, cim, apow, dskip):
    bsz, lp, _ = p3.shape
    tt = _row_tile(lp, 528, 8)
    nt = lp // tt
    nblk = tt // 8
    hw = 512

    def body(u_ref, bk_ref, cre_ref, cim_ref, ap_ref, d_ref, y_ref, xs_ref, car_ref):
        t = pl.program_id(2)

        @pl.when(t == 0)
        def _():
            car_ref[...] = jnp.zeros_like(car_ref)

        u = u_ref[...].astype(F32)
        xs_ref[...] = _dot5(u, bk_ref[...])
        ap = ap_ref[...]
        apr, api = ap[:, :hw], ap[:, hw:]
        rows = lax.broadcasted_iota(jnp.int32, (8, hw), 0)
        lev = [(d, jnp.where(rows < d, 0.0, jnp.broadcast_to(apr[d - 1:d, :], (8, hw))),
                jnp.where(rows < d, 0.0, jnp.broadcast_to(api[d - 1:d, :], (8, hw)))) for d in (1, 2, 4)]

        def blk(i, carry):
            cr, ci = carry
            off = pl.multiple_of(i * 8, 8)
            x = xs_ref[pl.ds(off, 8), :]
            xr, xi = x[:, :hw], x[:, hw:]
            for d, lr, li in lev:
                mr, mi = _cmul(pltpu.roll(xr, d, 0), pltpu.roll(xi, d, 0), lr, li)
                xr, xi = xr + mr, xi + mi
            mr, mi = _cmul(apr, api, cr, ci)
            xr, xi = xr + mr, xi + mi
            xs_ref[pl.ds(off, 8), :] = jnp.concatenate([xr, xi], axis=1)
            return xr[7:8, :], xi[7:8, :]

        c0 = car_ref[...]
        cr, ci = lax.fori_loop(0, nblk, blk, (c0[0:1, :hw], c0[0:1, hw:]))
        car_ref[...] = jnp.broadcast_to(jnp.concatenate([cr, ci], axis=1), car_ref.shape)
        xs = xs_ref[...]
        y_ref[...] = (_dot5(xs[:, :hw], cre_ref[...]) - _dot5(xs[:, hw:], cim_ref[...])
                      + d_ref[...] * u)

    ub = U_OFF // 128
    return _pcall(
        body, name="s5_fwd", grid=(S5_KCH, bsz, nt),
        in_specs=[pl.BlockSpec((None, tt, 128), lambda k, b, t: (b, t, ub + k)),
                  pl.BlockSpec((None, 128, 2 * hw), lambda k, b, t: (k, 0, 0)),
                  pl.BlockSpec((None, hw, 128), lambda k, b, t: (k, 0, 0)),
                  pl.BlockSpec((None, hw, 128), lambda k, b, t: (k, 0, 0)),
                  pl.BlockSpec((None, 8, 2 * hw), lambda k, b, t: (k, 0, 0)),
                  pl.BlockSpec((1, 128), lambda k, b, t: (0, k))],
        out_specs=[pl.BlockSpec((None, tt, 128), lambda k, b, t: (b, t, k)),
                   pl.BlockSpec((None, None, tt, 2 * hw), lambda k, b, t: (b, k, t, 0))],
        out_shape=[jax.ShapeDtypeStruct((bsz, lp, S5_KCH * 128), F32),
                   jax.ShapeDtypeStruct((bsz, S5_KCH, lp, 2 * hw), F32)],
        scratch_shapes=[pltpu.VMEM((8, 2 * hw), F32)],
        compiler_params=_cp(("parallel", "parallel", "arbitrary"), 40),
    )(p3, bk, cre, cim, apow, dskip)


def _s5_bwd(dp3, p3, dy3, xs, bk, cre, cim, apow_rev, dskip):
    bsz, lp, _ = p3.shape
    tt = _row_tile(lp, 528, 8)
    nt = lp // tt
    nblk = tt // 8
    hw = 512
    tb = tt // 8

    def body(dp_any, u_ref, dy_ref, xs_ref, halo_ref, bkt_ref, cre_ref, cim_ref, ap_ref, d_ref,
             du_ref, dbk_ref, dcre_ref, dcim_ref, da_ref, dd_ref, g_ref, ext_ref, car_ref):
        b = pl.program_id(1)
        t = pl.program_id(2)
        tidx = nt - 1 - t

        @pl.when(t == 0)
        def _():
            car_ref[...] = jnp.zeros_like(car_ref)

        @pl.when((b == 0) & (t == 0))
        def _():
            dbk_ref[...] = jnp.zeros_like(dbk_ref)
            dcre_ref[...] = jnp.zeros_like(dcre_ref)
            dcim_ref[...] = jnp.zeros_like(dcim_ref)
            da_ref[...] = jnp.zeros_like(da_ref)
            dd_ref[...] = jnp.zeros_like(dd_ref)

        u = u_ref[...].astype(F32)
        dy = dy_ref[...]
        g_ref[:, :hw] = _dot5(dy, cre_ref[...])
        g_ref[:, hw:] = -_dot5(dy, cim_ref[...])
        ap = ap_ref[...]
        apr, api = ap[:, :hw], -ap[:, hw:]
        rows = lax.broadcasted_iota(jnp.int32, (8, hw), 0)
        lev = [(d, jnp.where(rows >= 8 - d, 0.0, jnp.broadcast_to(apr[8 - d:9 - d, :], (8, hw))),
                jnp.where(rows >= 8 - d, 0.0, jnp.broadcast_to(api[8 - d:9 - d, :], (8, hw)))) for d in (1, 2, 4)]

        def blk(i, carry):
            cr, ci = carry
            off = pl.multiple_of((nblk - 1 - i) * 8, 8)
            x = g_ref[pl.ds(off, 8), :]
            xr, xi = x[:, :hw], x[:, hw:]
            for d, lr, li in lev:
                mr, mi = _cmul(pltpu.roll(xr, 8 - d, 0), pltpu.roll(xi, 8 - d, 0), lr, li)
                xr, xi = xr + mr, xi + mi
            mr, mi = _cmul(apr, api, cr, ci)
            xr, xi = xr + mr, xi + mi
            g_ref[pl.ds(off, 8), :] = jnp.concatenate([xr, xi], axis=1)
            return xr[0:1, :], xi[0:1, :]

        c0 = car_ref[...]
        cr, ci = lax.fori_loop(0, nblk, blk, (c0[0:1, :hw], c0[0:1, hw:]))
        car_ref[...] = jnp.broadcast_to(jnp.concatenate([cr, ci], axis=1), car_ref.shape)

        gg = g_ref[...]
        du = _dot5(gg, bkt_ref[...]) + d_ref[...] * dy
        trow = tidx * tt + lax.broadcasted_iota(jnp.int32, (tt, 1), 0)
        du_ref[...] = jnp.where(trow >= PAD, du, 0.0).astype(du_ref.dtype)
        dbk_ref[...] += _dot5(u, gg, TN)
        xsv = xs_ref[...]
        dcre_ref[...] += _dot5(dy, xsv[:, :hw], TN)
        dcim_ref[...] -= _dot5(dy, xsv[:, hw:], TN)
        dd_ref[...] += _colsum(dy * u)
        ext_ref[0:8, :] = jnp.where(tidx == 0, 0.0, halo_ref[...])
        ext_ref[8:, :] = xsv
        xp = ext_ref[pl.ds(7, tt), :]
        gr, gi, pr, pi = gg[:, :hw], gg[:, hw:], xp[:, :hw], xp[:, hw:]
        da_ref[:, :hw] += _colsum(gr * pr + gi * pi)
        da_ref[:, hw:] += _colsum(gi * pr - gr * pi)

    ub = U_OFF // 128
    sd = jax.ShapeDtypeStruct
    rt = lambda t: nt - 1 - t
    tr = lambda a: jnp.swapaxes(a, 1, 2)
    res = _pcall(
        body, name="s5_bwd", grid=(S5_KCH, bsz, nt),
        in_specs=[pl.BlockSpec(memory_space=pl.ANY),
                  pl.BlockSpec((None, tt, 128), lambda k, b, t: (b, rt(t), ub + k)),
                  pl.BlockSpec((None, tt, 128), lambda k, b, t: (b, rt(t), k)),
                  pl.BlockSpec((None, None, tt, 2 * hw), lambda k, b, t: (b, k, rt(t), 0)),
                  pl.BlockSpec((None, None, 8, 2 * hw), lambda k, b, t: (b, k, jnp.maximum(rt(t) * tb - 1, 0), 0)),
                  pl.BlockSpec((None, 2 * hw, 128), lambda k, b, t: (k, 0, 0)),
                  pl.BlockSpec((None, 128, hw), lambda k, b, t: (k, 0, 0)),
                  pl.BlockSpec((None, 128, hw), lambda k, b, t: (k, 0, 0)),
                  pl.BlockSpec((None, 8, 2 * hw), lambda k, b, t: (k, 0, 0)),
                  pl.BlockSpec((1, 128), lambda k, b, t: (0, k))],
        out_specs=[pl.BlockSpec((None, tt, 128), lambda k, b, t: (b, rt(t), ub + k)),
                   pl.BlockSpec((None, 128, 2 * hw), lambda k, b, t: (k, 0, 0)),
                   pl.BlockSpec((None, 128, hw), lambda k, b, t: (k, 0, 0)),
                   pl.BlockSpec((None, 128, hw), lambda k, b, t: (k, 0, 0)),
                   pl.BlockSpec((None, 1, 2 * hw), lambda k, b, t: (k, 0, 0)),
                   pl.BlockSpec((1, 128), lambda k, b, t: (0, k))],
        out_shape=[sd(dp3.shape, dp3.dtype), sd((S5_KCH, 128, 2 * hw), F32), sd((S5_KCH, 128, hw), F32),
                   sd((S5_KCH, 128, hw), F32), sd((S5_KCH, 1, 2 * hw), F32), sd((1, S5_KCH * 128), F32)],
        scratch_shapes=[pltpu.VMEM((tt, 2 * hw), F32), pltpu.VMEM((tt + 8, 2 * hw), F32), pltpu.VMEM((8, 2 * hw), F32)],
        input_output_aliases={0: 0},
        compiler_params=_cp(("arbitrary", "arbitrary", "arbitrary"), 48),
    )(dp3, p3, dy3, xs, xs, tr(bk), tr(cre), tr(cim), apow_rev, dskip)
    return res[0], res[1], tr(res[2]), tr(res[3]), res[4], res[5]


_G0 = math.sqrt(2.0 / math.pi)
_G1 = 0.044715


def _gelu(y):
    return 0.5 * y * (1.0 + jnp.tanh(_G0 * (y + _G1 * y * y * y)))


def _gelu_grad(y):
    th = jnp.tanh(_G0 * (y + _G1 * y * y * y))
    return 0.5 * (1.0 + th) + 0.5 * y * (1.0 - th * th) * _G0 * (1.0 + 3.0 * _G1 * y * y)


def _glu_fwd(y_s5, wglu_g, lp):
    r, w = y_s5.shape
    tm = _row_tile(lp, 416)
    cw = wglu_g.shape[2]

    def body(y_ref, w_ref, gy_ref, z_ref, o_ref):
        gy = _bf(_gelu(y_ref[...]))
        gy_ref[...] = gy
        zs = [_dot(gy, w_ref[s]) for s in range(4)]
        for s in range(4):
            z_ref[:, s * cw:(s + 1) * cw] = _bf(zs[s])
        o_ref[:, :cw] = _bf(zs[0] * _sig(zs[2]))
        o_ref[:, cw:] = _bf(zs[1] * _sig(zs[3]))

    sd = jax.ShapeDtypeStruct
    return _pcall(
        body, name="glu_fwd", grid=(r // tm,),
        in_specs=[pl.BlockSpec((tm, w), lambda i: (i, 0)), _resident(wglu_g.shape)],
        out_specs=[pl.BlockSpec((tm, w), lambda i: (i, 0)), pl.BlockSpec((tm, 4 * cw), lambda i: (i, 0)),
                   pl.BlockSpec((tm, 2 * cw), lambda i: (i, 0))],
        out_shape=[sd((r, w), BF16), sd((r, 4 * cw), BF16), sd((r, 2 * cw), BF16)],
        compiler_params=_cp(("parallel",), 40),
    )(y_s5, wglu_g)


def _glu_bwd(dyg, z, y_s5, wglu_g, lp):
    r, w = y_s5.shape
    tm = _row_tile(lp, 416)
    cw = wglu_g.shape[2]

    def body(d_ref, z_ref, y_ref, w_ref, dz_ref, dy_ref):
        d = d_ref[...].astype(F32)
        zz = z_ref[...].astype(F32)
        acc = jnp.zeros((tm, w), F32)
        for s in range(2):
            z1 = zz[:, s * cw:(s + 1) * cw]
            sg = _sig(zz[:, (2 + s) * cw:(3 + s) * cw])
            dd = d[:, s * cw:(s + 1) * cw]
            dz1 = _bf(dd * sg)
            dz2 = _bf(dd * z1 * sg * (1.0 - sg))
            dz_ref[:, s * cw:(s + 1) * cw] = dz1
            dz_ref[:, (2 + s) * cw:(3 + s) * cw] = dz2
            acc += _dot(dz1, w_ref[s], NT) + _dot(dz2, w_ref[2 + s], NT)
        dy_ref[...] = acc * _gelu_grad(y_ref[...])

    sd = jax.ShapeDtypeStruct
    return _pcall(
        body, name="glu_bwd", grid=(r // tm,),
        in_specs=[pl.BlockSpec((tm, 2 * cw), lambda i: (i, 0)), pl.BlockSpec((tm, 4 * cw), lambda i: (i, 0)),
                  pl.BlockSpec((tm, w), lambda i: (i, 0)), _resident(wglu_g.shape)],
        out_specs=[pl.BlockSpec((tm, 4 * cw), lambda i: (i, 0)), pl.BlockSpec((tm, w), lambda i: (i, 0))],
        out_shape=[sd((r, 4 * cw), BF16), sd((r, w), F32)],
        compiler_params=_cp(("parallel",), 40),
    )(dyg, z, y_s5, wglu_g)


def _conv_fwd(p3, cw, cb):
    bsz, lp, _ = p3.shape
    tt = _row_tile(lp, 416)
    nt = lp // tt
    tb = tt // 8
    c = cw.shape[1]
    qb = Q_OFF // c

    hr = HALO_ROWS
    off = hr - (CONV_W - 1)

    def body(x_ref, halo_ref, w_ref, b_ref, pre_ref, act_ref, ext_ref):
        t = pl.program_id(1)
        ext_ref[0:hr, :] = jnp.where(t == 0, 0.0, halo_ref[...].astype(F32))
        ext_ref[hr:, :] = x_ref[...].astype(F32)
        w = w_ref[...]
        acc = b_ref[...] + w[0:1, :] * ext_ref[pl.ds(off, tt), :]
        for j in range(1, CONV_W):
            acc = acc + w[j:j + 1, :] * ext_ref[pl.ds(off + j, tt), :]
        pre_ref[...] = _bf(acc)
        act_ref[...] = _bf(acc * _sig(acc))

    sd = jax.ShapeDtypeStruct
    return _pcall(
        body, name="conv_fwd", grid=(bsz, nt),
        in_specs=[pl.BlockSpec((None, tt, c), lambda b, t: (b, t, qb)),
                  pl.BlockSpec((None, hr, c), lambda b, t: (b, jnp.maximum(t * (tt // hr) - 1, 0), qb)),
                  _const((CONV_W, c)), _const((1, c))],
        out_specs=[pl.BlockSpec((None, tt, c), lambda b, t: (b, t, 0))] * 2,
        out_shape=[sd((bsz, lp, c), BF16)] * 2,
        scratch_shapes=[pltpu.VMEM((tt + hr, c), F32)],
        compiler_params=_cp(("parallel", "parallel")),
    )(p3, p3, cw, cb)


def _conv_bwd(dp3, p3, dact3, pre3, cw):
    bsz, lp, _ = p3.shape
    tt = _row_tile(lp, 416)
    nt = lp // tt
    tb = tt // 8
    c = cw.shape[1]
    qb = Q_OFF // c

    hr = HALO_ROWS
    off = hr - (CONV_W - 1)

    def silu_grad(x):
        s = _sig(x)
        return s * (1.0 + x * (1.0 - s))

    def body(dp_any, x_ref, xh_ref, d_ref, dh_ref, pre_ref, preh_ref, w_ref, o_ref, dw_ref, db_ref, ext_ref, dext_ref):
        b = pl.program_id(0)
        t = pl.program_id(1)

        @pl.when((b == 0) & (t == 0))
        def _():
            dw_ref[...] = jnp.zeros_like(dw_ref)
            db_ref[...] = jnp.zeros_like(db_ref)

        dc = d_ref[...].astype(F32) * silu_grad(pre_ref[...].astype(F32))
        dch = jnp.where(t == nt - 1, 0.0, dh_ref[...].astype(F32) * silu_grad(preh_ref[...].astype(F32)))
        dext_ref[0:tt, :] = dc
        dext_ref[tt:, :] = dch
        ext_ref[0:hr, :] = jnp.where(t == 0, 0.0, xh_ref[...].astype(F32))
        ext_ref[hr:, :] = x_ref[...].astype(F32)
        w = w_ref[...]
        acc = w[CONV_W - 1:CONV_W, :] * dc
        for j in range(CONV_W - 1):
            acc = acc + w[j:j + 1, :] * dext_ref[pl.ds(CONV_W - 1 - j, tt), :]
        trow = t * tt + lax.broadcasted_iota(jnp.int32, (tt, 1), 0)
        o_ref[...] = jnp.where(trow >= PAD, acc, 0.0).astype(o_ref.dtype)
        db_ref[...] += _colsum(dc)
        for j in range(CONV_W):
            dw_ref[j:j + 1, :] += _colsum(dc * ext_ref[pl.ds(off + j, tt), :])

    sd = jax.ShapeDtypeStruct
    nxt = lambda t: jnp.minimum((t + 1) * (tt // hr), lp // hr - 1)
    return _pcall(
        body, name="conv_bwd", grid=(bsz, nt),
        in_specs=[pl.BlockSpec(memory_space=pl.ANY),
                  pl.BlockSpec((None, tt, c), lambda b, t: (b, t, qb)),
                  pl.BlockSpec((None, hr, c), lambda b, t: (b, jnp.maximum(t * (tt // hr) - 1, 0), qb)),
                  pl.BlockSpec((None, tt, c), lambda b, t: (b, t, 0)),
                  pl.BlockSpec((None, hr, c), lambda b, t: (b, nxt(t), 0)),
                  pl.BlockSpec((None, tt, c), lambda b, t: (b, t, 0)),
                  pl.BlockSpec((None, hr, c), lambda b, t: (b, nxt(t), 0)),
                  _const((CONV_W, c))],
        out_specs=[pl.BlockSpec((None, tt, c), lambda b, t: (b, t, qb)), _const((CONV_W, c)), _const((1, c))],
        out_shape=[sd(dp3.shape, dp3.dtype), sd((CONV_W, c), F32), sd((1, c), F32)],
        scratch_shapes=[pltpu.VMEM((tt + hr, c), F32), pltpu.VMEM((tt + hr, c), F32)],
        input_output_aliases={0: 0},
        compiler_params=_cp(("arbitrary", "arbitrary")),
    )(dp3, p3, p3, dact3, dact3, pre3, pre3, cw)


def _mlstm_gates(g, h_idx, c_idx, lc):
    lane = lax.broadcasted_iota(jnp.int32, g.shape, 1)
    i_col = jnp.sum(jnp.where(lane == h_idx, g, 0.0), axis=1, keepdims=True)
    f_col = jnp.sum(jnp.where(lane == M_HEADS + h_idx, g, 0.0), axis=1, keepdims=True)
    row = lax.broadcasted_iota(jnp.int32, (lc, 1), 0)
    valid = (c_idx * lc + row) >= PAD
    li = jnp.where(valid, i_col, NEG)
    lf = jnp.where(valid, jnp.minimum(f_col, 0.0) - jnp.log(1.0 + jnp.exp(-jnp.abs(f_col))), 0.0)
    r2 = lax.broadcasted_iota(jnp.int32, (lc, lc), 0)
    c2 = lax.broadcasted_iota(jnp.int32, (lc, lc), 1)
    eye = r2 == c2
    tril = r2 >= c2
    to_row = lambda col: jnp.sum(jnp.where(eye, col, 0.0), axis=0, keepdims=True)
    lf_row = to_row(lf)
    b_col = jnp.sum(jnp.where(tril, lf_row, 0.0), axis=1, keepdims=True)
    b_row = to_row(b_col)
    li_row = to_row(li)
    d_mat = jnp.where(tril, b_col - b_row + li_row, NEG)
    return dict(f_col=f_col, valid=valid, li=li, b_col=b_col, d_mat=d_mat, eye=eye, r2=r2, c2=c2, row=row,
                to_row=to_row)


def _mlstm_chunk(q, ks, v, gq, c_st, n_st, m_st, lc):
    b_col, d_mat = gq["b_col"], gq["d_mat"]
    m_inter = b_col + m_st
    m_row = jnp.maximum(m_inter, jnp.max(d_mat, axis=1, keepdims=True))
    w_intra = jnp.exp(d_mat - m_row)
    w_inter = jnp.exp(m_inter - m_row)
    qb, kb, vb, cb = _bf(q), _bf(ks), _bf(v), _bf(c_st)
    s = _dot(qb, kb, NT) * w_intra
    qc = _dot(qb, cb)
    num = _dot(_bf(s), vb) + w_inter * qc
    qn = jnp.sum(q * n_st, axis=1, keepdims=True)
    den = jnp.sum(s, axis=1, keepdims=True) + w_inter * qn
    e = jnp.exp(-m_row)
    nn = jnp.maximum(jnp.abs(den), e)
    b_last = b_col[lc - 1:lc, :]
    g_log = b_last - b_col + gq["li"]
    m_new = jnp.maximum(b_last + m_st, jnp.max(g_log, axis=0, keepdims=True))
    w_k = jnp.exp(g_log - m_new)
    decay = jnp.exp(b_last + m_st - m_new)
    return dict(w_intra=w_intra, w_inter=w_inter, qb=qb, kb=kb, vb=vb, cb=cb, s=s, qc=qc, num=num, qn=qn, den=den,
                e=e, nn=nn, m_new=m_new, w_k=w_k, decay=decay)


def _chunks_per_step(nc):
    return max(c for c in (3, 2, 1) if nc % c == 0)


def _mlstm_fwd(qk3, p3, pg3):
    bsz, lp, _ = p3.shape
    lc = M_CHUNK
    nc = lp // lc
    dk, dv = 128, 256
    scale = dk ** -0.5

    cps = _chunks_per_step(nc)
    rows = cps * lc

    def body(q_ref, k_ref, v_ref, g_ref, h_ref, cs_ref, ns_ref, ms_ref, c_sc, n_sc, m_sc):
        st = pl.program_id(1)

        @pl.when(st == 0)
        def _():
            c_sc[...] = jnp.zeros_like(c_sc)
            n_sc[...] = jnp.zeros_like(n_sc)
            m_sc[...] = jnp.zeros_like(m_sc)

        for j in range(cps):
            rs = slice(j * lc, (j + 1) * lc)
            g = g_ref[rs, :]
            for hh in range(M_HEADS):
                c_st, n_st, m_all = c_sc[hh], n_sc[hh], m_sc[hh]
                cs_ref[hh, j] = c_st
                ns_ref[hh, j] = n_st
                ms_ref[hh, j] = m_all
                m_st = m_all[:, 0:1]
                q = q_ref[rs, hh * dk:(hh + 1) * dk].astype(F32)
                ks = k_ref[rs, hh * dk:(hh + 1) * dk].astype(F32) * scale
                v = v_ref[rs, hh * dv:(hh + 1) * dv]
                gq = _mlstm_gates(g, hh, st * cps + j, lc)
                f = _mlstm_chunk(q, ks, v, gq, c_st, n_st, m_st, lc)
                h_ref[rs, hh * dv:(hh + 1) * dv] = _bf(f["num"] / f["nn"])
                kw = ks * f["w_k"]
                c_sc[hh] = f["decay"] * c_st + _dot(_bf(kw), f["vb"], TN)
                n_sc[hh] = f["decay"] * n_st + _colsum(kw)
                m_sc[hh] = jnp.broadcast_to(f["m_new"], (1, 128))

    sd = jax.ShapeDtypeStruct
    nh = M_HEADS
    return _pcall(
        body, name="mlstm_fwd", grid=(bsz, nc // cps),
        in_specs=[pl.BlockSpec((None, rows, nh * dk), lambda b, c: (b, c, 0)),
                  pl.BlockSpec((None, rows, nh * dk), lambda b, c: (b, c, 1)),
                  pl.BlockSpec((None, rows, nh * dv), lambda b, c: (b, c, V_OFF // (nh * dv))),
                  pl.BlockSpec((None, rows, 128), lambda b, c: (b, c, 0))],
        out_specs=[pl.BlockSpec((None, rows, nh * dv), lambda b, c: (b, c, 0)),
                   pl.BlockSpec((None, nh, cps, dk, dv), lambda b, c: (b, 0, c, 0, 0)),
                   pl.BlockSpec((None, nh, cps, 1, dk), lambda b, c: (b, 0, c, 0, 0)),
                   pl.BlockSpec((None, nh, cps, 1, 128), lambda b, c: (b, 0, c, 0, 0))],
        out_shape=[sd((bsz, lp, nh * dv), BF16), sd((bsz, nh, nc, dk, dv), F32),
                   sd((bsz, nh, nc, 1, dk), F32), sd((bsz, nh, nc, 1, 128), F32)],
        scratch_shapes=[pltpu.VMEM((nh, dk, dv), F32), pltpu.VMEM((nh, 1, dk), F32), pltpu.VMEM((nh, 1, 128), F32)],
        compiler_params=_cp(("parallel", "arbitrary")),
    )(qk3, qk3, p3, pg3)


def _mlstm_bwd(dp3, qk3, p3, pg3, dh3, cs, ns, ms):
    bsz, lp, _ = p3.shape
    lc = M_CHUNK
    nc = lp // lc
    dk, dv = 128, 256
    scale = dk ** -0.5

    cps = _chunks_per_step(nc)
    nst = nc // cps
    rows = cps * lc

    def body(dp_any, q_ref, k_ref, v_ref, g_ref, dh_ref, cs_ref, ns_ref, ms_ref,
             dv_ref, dqk_ref, dg_ref, dc_sc, dn_sc):
        t = pl.program_id(1)
        st = nst - 1 - t

        @pl.when(t == 0)
        def _():
            dc_sc[...] = jnp.zeros_like(dc_sc)
            dn_sc[...] = jnp.zeros_like(dn_sc)

        lane = lax.broadcasted_iota(jnp.int32, (lc, 128), 1)
        for j in reversed(range(cps)):
            rs = slice(j * lc, (j + 1) * lc)
            g = g_ref[rs, :]
            dgate = jnp.zeros((lc, 128), F32)
            for hh in range(M_HEADS):
                dgate = head(hh, j, rs, st * cps + j, g, lane, dgate, q_ref, k_ref, v_ref, dh_ref, cs_ref, ns_ref,
                             ms_ref, dv_ref, dqk_ref, dc_sc, dn_sc)
            dg_ref[rs, :] = dgate.astype(dg_ref.dtype)

    def head(hh, j, sl, c, g, lane, dgate, q_ref, k_ref, v_ref, dh_ref, cs_ref, ns_ref, ms_ref, dv_ref, dqk_ref,
             dc_sc, dn_sc):
        c_st, n_st = cs_ref[hh, j], ns_ref[hh, j]
        m_st = ms_ref[hh, j][:, 0:1]
        q = q_ref[sl, hh * dk:(hh + 1) * dk].astype(F32)
        ks = k_ref[sl, hh * dk:(hh + 1) * dk].astype(F32) * scale
        v = v_ref[sl, hh * dv:(hh + 1) * dv]
        dh = dh_ref[sl, hh * dv:(hh + 1) * dv].astype(F32)
        gq = _mlstm_gates(g, hh, c, lc)
        f = _mlstm_chunk(q, ks, v, gq, c_st, n_st, m_st, lc)
        eye, r2, c2, row, valid = gq["eye"], gq["r2"], gq["c2"], gq["row"], gq["valid"]
        w_intra, w_inter, s, nn, den = f["w_intra"], f["w_inter"], f["s"], f["nn"], f["den"]
        qb, kb, vb, cb, w_k, decay = f["qb"], f["kb"], f["vb"], f["cb"], f["w_k"], f["decay"]
        d_c, d_n = dc_sc[hh], dn_sc[hh]
        d_cb = _bf(d_c)

        hout = f["num"] / nn
        dnum = dh / nn
        d_nn = -jnp.sum(dh * hout, axis=1, keepdims=True) / nn
        dden = jnp.where(jnp.abs(den) > f["e"], d_nn * jnp.sign(den), 0.0)
        wdnum = w_inter * dnum
        wdden = w_inter * dden
        ds = _dot(_bf(dnum), vb, NT) + dden
        dsw = _bf(ds * w_intra)
        dq = _dot(dsw, kb) + _dot(_bf(wdnum), cb, NT) + wdden * n_st
        dkw = _dot(vb, d_cb, NT) + d_n
        dks = _dot(dsw, qb, TN) + dkw * w_k
        kw = ks * w_k
        dvv = _dot(_bf(s), _bf(dnum), TN) + _dot(_bf(kw), d_cb)
        dd = ds * s
        rs = jnp.sum(dd, axis=1, keepdims=True)
        cs_col = jnp.sum(jnp.where(eye, jnp.sum(dd, axis=0, keepdims=True), 0.0), axis=1, keepdims=True)
        dwi = jnp.sum(dnum * f["qc"], axis=1, keepdims=True) + dden * f["qn"]
        db = rs - cs_col + dwi * w_inter
        dli = cs_col
        ddecay = jnp.sum(jnp.sum(d_c * c_st, axis=1, keepdims=True), axis=0, keepdims=True) \
            + jnp.sum(d_n * n_st, axis=1, keepdims=True)
        dgl = jnp.sum(dkw * ks, axis=1, keepdims=True) * w_k
        dblast = ddecay * decay + jnp.sum(dgl, axis=0, keepdims=True)
        db = db - dgl + jnp.where(row == lc - 1, dblast, 0.0)
        dli = dli + dgl
        db_row = gq["to_row"](db)
        dlf = jnp.sum(jnp.where(c2 >= r2, db_row, 0.0), axis=1, keepdims=True)
        dlf = jnp.where(valid, dlf, 0.0)
        dgate = jnp.where(lane == hh, jnp.where(valid, dli, 0.0), dgate)
        dgate = jnp.where(lane == M_HEADS + hh, dlf / (1.0 + jnp.exp(gq["f_col"])), dgate)
        dqk_ref[sl, hh * dk:(hh + 1) * dk] = _bf(dq)
        dqk_ref[sl, (M_HEADS + hh) * dk:(M_HEADS + hh + 1) * dk] = _bf(dks * scale)
        dv_ref[sl, hh * dv:(hh + 1) * dv] = dvv.astype(dv_ref.dtype)
        dc_sc[hh] = decay * d_c + _dot(qb, _bf(wdnum), TN)
        dn_sc[hh] = decay * d_n + _colsum(q * wdden)
        return dgate

    sd = jax.ShapeDtypeStruct
    nh = M_HEADS
    rc = lambda c: nst - 1 - c
    return _pcall(
        body, name="mlstm_bwd", grid=(bsz, nst),
        in_specs=[pl.BlockSpec(memory_space=pl.ANY),
                  pl.BlockSpec((None, rows, nh * dk), lambda b, c: (b, rc(c), 0)),
                  pl.BlockSpec((None, rows, nh * dk), lambda b, c: (b, rc(c), 1)),
                  pl.BlockSpec((None, rows, nh * dv), lambda b, c: (b, rc(c), V_OFF // (nh * dv))),
                  pl.BlockSpec((None, rows, 128), lambda b, c: (b, rc(c), 0)),
                  pl.BlockSpec((None, rows, nh * dv), lambda b, c: (b, rc(c), 0)),
                  pl.BlockSpec((None, nh, cps, dk, dv), lambda b, c: (b, 0, rc(c), 0, 0)),
                  pl.BlockSpec((None, nh, cps, 1, dk), lambda b, c: (b, 0, rc(c), 0, 0)),
                  pl.BlockSpec((None, nh, cps, 1, 128), lambda b, c: (b, 0, rc(c), 0, 0))],
        out_specs=[pl.BlockSpec((None, rows, nh * dv), lambda b, c: (b, rc(c), V_OFF // (nh * dv))),
                   pl.BlockSpec((None, rows, 2 * nh * dk), lambda b, c: (b, rc(c), 0)),
                   pl.BlockSpec((None, rows, 128), lambda b, c: (b, rc(c), 0))],
        out_shape=[sd(dp3.shape, dp3.dtype), sd((bsz, lp, 2 * nh * dk), BF16), sd((bsz, lp, 128), dp3.dtype)],
        scratch_shapes=[pltpu.VMEM((nh, dk, dv), F32), pltpu.VMEM((nh, 1, dk), F32)],
        input_output_aliases={0: 0},
        compiler_params=_cp(("arbitrary", "arbitrary")),
    )(dp3, qk3, qk3, p3, pg3, dh3, cs, ns, ms)


def _headnorm(x):
    dv = x.shape[1] // M_HEADS
    xh, rs = [], []
    for h in range(M_HEADS):
        xx = x[:, h * dv:(h + 1) * dv]
        mu = jnp.mean(xx, axis=-1, keepdims=True)
        xc = xx - mu
        rstd = lax.rsqrt(jnp.mean(xc * xc, axis=-1, keepdims=True) + LN_EPS)
        xh.append(xc * rstd)
        rs.append(rstd)
    return jnp.concatenate(xh, axis=1), rs


def _mix_fwd(hm, p, ys5g, h0, gn, wmo_bf, wo_bf, lp):
    r, d = hm.shape
    tm = _row_tile(lp, 384)

    def body(hm_ref, o_ref, gs_ref, gm_ref, ys_ref, h0_ref, gn_ref, wmo_ref, wo_ref,
             ymin_ref, mix_ref, r1_ref):
        xhat, _ = _headnorm(hm_ref[...].astype(F32))
        ymin = _bf(_sig(o_ref[...].astype(F32)) * (xhat * gn_ref[...]))
        ymin_ref[...] = ymin
        ym = _dot(ymin, wmo_ref[...])
        mix = _bf(_sig(gs_ref[...].astype(F32)) * ys_ref[...].astype(F32) + _sig(gm_ref[...].astype(F32)) * ym)
        mix_ref[...] = mix
        r1_ref[...] = ALPHA * h0_ref[...] + _dot(mix, wo_ref[...])

    sd = jax.ShapeDtypeStruct
    row = pl.BlockSpec((tm, d), lambda i: (i, 0))
    return _pcall(
        body, name="mix_fwd", grid=(r // tm,),
        in_specs=[row, pl.BlockSpec((tm, d), lambda i: (i, O_OFF // d)), pl.BlockSpec((tm, d), lambda i: (i, GS_OFF // d)),
                  pl.BlockSpec((tm, d), lambda i: (i, GM_OFF // d)), row, row, _const((1, d)),
                  _resident((d, d)), _resident((d, d))],
        out_specs=[row] * 3,
        out_shape=[sd((r, d), BF16), sd((r, d), BF16), sd((r, d), F32)],
        compiler_params=_cp(("parallel",), 48),
    )(hm, p, p, p, ys5g, h0, gn, wmo_bf, wo_bf)


def _mix_bwd(dr1, wo_bf, wmo_bf, p, ys5g, ymin, hm, gn, lp):
    r, d = hm.shape
    tm = _row_tile(lp, 384)
    dv = d // M_HEADS

    def body(dr1_ref, wo_ref, wmo_ref, o_ref, gs_ref, gm_ref, ys_ref, ym_ref, hm_ref, gn_ref,
             dp_ref, dys_ref, dym_ref, dhm_ref, dgn_ref):
        i = pl.program_id(0)

        @pl.when(i == 0)
        def _():
            dgn_ref[...] = jnp.zeros_like(dgn_ref)

        dmix = _dot(_bf(dr1_ref[...]), wo_ref[...], NT)
        sgs, sgm, so = (_sig(gs_ref[...].astype(F32)), _sig(gm_ref[...].astype(F32)), _sig(o_ref[...].astype(F32)))
        dys_ref[...] = _bf(dmix * sgs)
        dp_ref[:, d:2 * d] = _bf(dmix * ys_ref[...].astype(F32) * sgs * (1.0 - sgs))
        dym = dmix * sgm
        dym_ref[...] = _bf(dym)
        ym = _dot(ym_ref[...], wmo_ref[...])
        dp_ref[:, 2 * d:3 * d] = _bf(dmix * ym * sgm * (1.0 - sgm))
        dymin = _dot(_bf(dym), wmo_ref[...], NT)
        xhat, rs = _headnorm(hm_ref[...].astype(F32))
        gn_ = gn_ref[...]
        dp_ref[:, 0:d] = _bf(dymin * (xhat * gn_) * so * (1.0 - so))
        dhn = dymin * so
        dgn_ref[...] += _colsum(dhn * xhat)
        dxh = dhn * gn_
        for h in range(M_HEADS):
            sl = slice(h * dv, (h + 1) * dv)
            a, xh = dxh[:, sl], xhat[:, sl]
            m1 = jnp.mean(a, axis=-1, keepdims=True)
            m2 = jnp.mean(a * xh, axis=-1, keepdims=True)
            dhm_ref[:, sl] = _bf(rs[h] * (a - m1 - xh * m2))

    sd = jax.ShapeDtypeStruct
    row = pl.BlockSpec((tm, d), lambda i: (i, 0))
    vec = _const((1, d))
    return _pcall(
        body, name="mix_bwd", grid=(r // tm,),
        in_specs=[row, _resident((d, d)), _resident((d, d)),
                  pl.BlockSpec((tm, d), lambda i: (i, O_OFF // d)), pl.BlockSpec((tm, d), lambda i: (i, GS_OFF // d)),
                  pl.BlockSpec((tm, d), lambda i: (i, GM_OFF // d)), row, row, row, vec],
        out_specs=[pl.BlockSpec((tm, 3 * d), lambda i: (i, 0)), row, row, row, vec],
        out_shape=[sd((r, NP), BF16), sd((r, d), BF16), sd((r, d), BF16), sd((r, d), BF16), sd((1, d), F32)],
        compiler_params=_cp(("arbitrary",), 56),
    )(dr1, wo_bf, wmo_bf, p, p, p, ys5g, ymin, hm, gn)


def _mlp_fwd(r1, tgt, g1, b1, wup_g, wdn_bf, bup, g2, b2, lp):
    r, d = r1.shape
    tm = _row_tile(lp, 352)
    tps = lp // tm
    nf = wup_g.shape[0]

    def body(r1_ref, t_ref, g1_ref, b1_ref, wup_ref, wdn_ref, bup_ref, g2_ref, b2_ref,
             dr2_ref, h1b_ref, act_ref, loss_ref, dg2_ref, db2_ref):
        i = pl.program_id(0)

        @pl.when(i == 0)
        def _():
            loss_ref[...] = jnp.zeros_like(loss_ref)
            dg2_ref[...] = jnp.zeros_like(dg2_ref)
            db2_ref[...] = jnp.zeros_like(db2_ref)

        h1, _, _ = _ln_fwd(r1_ref[...], g1_ref[...], b1_ref[...])
        h1b = _bf(h1)
        h1b_ref[...] = h1b
        ff = jnp.zeros((tm, d), F32)
        for s in range(nf):
            up = _dot(h1b, wup_ref[s]) + bup_ref[:, s * d:(s + 1) * d]
            a = jnp.maximum(up, 0.0)
            a = _bf(a * a)
            act_ref[:, s * d:(s + 1) * d] = a
            ff = ff + _dot(a, wdn_ref[s * d:(s + 1) * d, :])
        r2 = ALPHA * h1 + ff
        g2 = g2_ref[...]
        y, xhat, rstd = _ln_fwd(r2, g2, b2_ref[...])
        t = (i % tps) * tm + lax.broadcasted_iota(jnp.int32, (tm, 1), 0)
        diff = jnp.where(t >= PAD + N_META, y - t_ref[...], 0.0)
        loss_ref[...] += 0.5 / d * jnp.sum(jnp.sum(diff * diff, axis=1, keepdims=True), axis=0, keepdims=True)
        dy = diff * (1.0 / d)
        dg2_ref[...] += _colsum(dy * xhat)
        db2_ref[...] += _colsum(dy)
        dr2_ref[...] = _ln_bwd(dy, xhat, rstd, g2)

    sd = jax.ShapeDtypeStruct
    row = pl.BlockSpec((tm, d), lambda i: (i, 0))
    vec = _const((1, d))
    return _pcall(
        body, name="mlp_fwd", grid=(r // tm,),
        in_specs=[row, row, vec, vec, _resident(wup_g.shape), _resident(wdn_bf.shape), _const((1, nf * d)), vec, vec],
        out_specs=[row, row, pl.BlockSpec((tm, nf * d), lambda i: (i, 0)), _const((1, 128)), vec, vec],
        out_shape=[sd((r, d), F32), sd((r, d), BF16), sd((r, nf * d), BF16), sd((1, 128), F32), sd((1, d), F32),
                   sd((1, d), F32)],
        compiler_params=_cp(("arbitrary",), 56),
    )(r1, tgt, g1, b1, wup_g, wdn_bf, bup, g2, b2)


def _mlp_bwd(h1b, dr2, r1, g1, wup_g, wdn_bf, bup, lp):
    r, d = h1b.shape
    tm = _row_tile(lp, 352)
    nf = wup_g.shape[0]

    def body(h1_ref, dr2_ref, r1_ref, g1_ref, wup_ref, wdn_ref, bup_ref, dr1_ref, dup_ref, dbup_ref, dg1_ref, db1_ref):
        i = pl.program_id(0)

        @pl.when(i == 0)
        def _():
            dbup_ref[...] = jnp.zeros_like(dbup_ref)
            dg1_ref[...] = jnp.zeros_like(dg1_ref)
            db1_ref[...] = jnp.zeros_like(db1_ref)

        h1b = h1_ref[...]
        dr2 = dr2_ref[...]
        dr2b = _bf(dr2)
        acc = ALPHA * dr2
        for s in range(nf):
            up = _dot(h1b, wup_ref[s]) + bup_ref[:, s * d:(s + 1) * d]
            dact = _dot(dr2b, wdn_ref[s * d:(s + 1) * d, :], NT)
            dup = dact * (2.0 * jnp.maximum(up, 0.0))
            dbup_ref[:, s * d:(s + 1) * d] += _colsum(dup)
            dupb = _bf(dup)
            dup_ref[:, s * d:(s + 1) * d] = dupb
            acc = acc + _dot(dupb, wup_ref[s], NT)
        g1 = g1_ref[...]
        _, xhat1, rstd1 = _ln_fwd(r1_ref[...], g1, 0.0)
        dr1_ref[...] = _ln_bwd(acc, xhat1, rstd1, g1)
        dg1_ref[...] += _colsum(acc * xhat1)
        db1_ref[...] += _colsum(acc)

    sd = jax.ShapeDtypeStruct
    row = pl.BlockSpec((tm, d), lambda i: (i, 0))
    vec = _const((1, d))
    return _pcall(
        body, name="mlp_bwd", grid=(r // tm,),
        in_specs=[row, row, row, vec, _resident(wup_g.shape), _resident(wdn_bf.shape), _const((1, nf * d))],
        out_specs=[row, pl.BlockSpec((tm, nf * d), lambda i: (i, 0)), _const((1, nf * d)), vec, vec],
        out_shape=[sd((r, d), F32), sd((r, nf * d), BF16), sd((1, nf * d), F32), sd((1, d), F32), sd((1, d), F32)],
        compiler_params=_cp(("arbitrary",), 56),
    )(h1b, dr2, r1, g1, wup_g, wdn_bf, bup)


def _s5_block_mats(bb_re_t, bb_im_t, c_re, c_im, ap_re, ap_im):
    ng = c_re.shape[0]
    gl = ng // S5_KCH
    eye = jnp.eye(gl, dtype=F32)

    def bmat(bt):
        bb = jnp.transpose(bt, (1, 0, 2)).reshape(S5_KCH, gl, S5_GROUP, S5_STATE)
        return jnp.einsum("kghp,gj->kghjp", bb, eye).reshape(S5_KCH, gl * S5_GROUP, gl * S5_STATE)

    def cmat(c):
        cc = c.reshape(S5_KCH, gl, S5_GROUP, S5_STATE)
        return jnp.einsum("kghp,gj->kjpgh", cc, eye).reshape(S5_KCH, gl * S5_STATE, gl * S5_GROUP)

    def pw(a):
        return jnp.transpose(a.reshape(8, S5_KCH, gl * S5_STATE), (1, 0, 2))

    bk = jnp.concatenate([bmat(bb_re_t), bmat(bb_im_t)], axis=-1)
    apow = jnp.concatenate([pw(ap_re), pw(ap_im)], axis=-1)
    return _bf(bk), _bf(cmat(c_re)), _bf(cmat(c_im)), apow


def _s5_block_grads(dbk, dcre, dcim, da):
    gl = dbk.shape[1] // S5_GROUP
    ng = gl * S5_KCH
    eye = jnp.eye(gl, dtype=F32)
    hw = gl * S5_STATE

    def bpart(x):
        x = x.reshape(S5_KCH, gl, S5_GROUP, gl, S5_STATE)
        x = jnp.einsum("kghjp,gj->kghp", x, eye).reshape(ng, S5_GROUP, S5_STATE)
        return jnp.transpose(x, (1, 0, 2))

    def cpart(x):
        x = x.reshape(S5_KCH, gl, S5_STATE, gl, S5_GROUP)
        return jnp.einsum("kjpgh,gj->kghp", x, eye).reshape(ng, S5_GROUP, S5_STATE)

    return (bpart(dbk[..., :hw]), bpart(dbk[..., hw:]), cpart(dcre), cpart(dcim),
            da[:, 0, :hw].reshape(ng, S5_STATE), da[:, 0, hw:].reshape(ng, S5_STATE))


def _tie(a, tok):
    return a if tok is None else a + tok[0, 0]


def _local_step(x, tgt, w, early=None, late=None, ready=None):
    ready = ready or (lambda names, g: None)
    bsz, seq, d = x.shape
    lp = PAD + N_META + seq
    r = bsz * lp
    tgtp = jnp.concatenate([jnp.zeros((bsz, PAD + N_META, d), F32), tgt], axis=1).reshape(r, d)

    h0, h0b = _ln0_fwd(x, w["meta_tokens"], w["ln0_g"], w["ln0_b"])
    b_re_t = jnp.transpose(w["s5_b_re"], (2, 0, 1))
    b_im_t = jnp.transpose(w["s5_b_im"], (2, 0, 1))
    ap_re, ap_im, bb_re_t, bb_im_t = _s5_prep(w["s5_lambda_re"], w["s5_lambda_im"], w["s5_log_dt"], b_re_t, b_im_t)
    bk, cre, cim, apow = _s5_block_mats(bb_re_t, bb_im_t, w["s5_c_re"], w["s5_c_im"], ap_re, ap_im)
    apow_rev = jnp.flip(apow, axis=1)
    if early is not None:
        w = {**w, **early((h0, tgtp, bk, cre, cim, apow_rev))}
    p, pg = _inproj(h0b, w["w_in"], w["b_in"], lp)
    p3 = p.reshape(bsz, lp, NP)
    pg3 = pg.reshape(bsz, lp, 128)

    y_s5, xs = _s5_fwd(p3, bk, cre, cim, apow, w["s5_d"])
    sw = y_s5.shape[-1]
    if late is not None:
        w = {**w, **late(y_s5)}
    gy, z, ys5g = _glu_fwd(y_s5.reshape(r, sw), w["s5_w_glu"], lp)

    pre3, qk3 = _conv_fwd(p3, w["qk_conv_w"], w["qk_conv_b"])
    hm3, cs, ns, ms = _mlstm_fwd(qk3, p3, pg3)
    hm = hm3.reshape(r, d)
    ymin, mix, r1 = _mix_fwd(hm, p, ys5g, h0, w["m_norm_g"], w["m_w_out"], w["w_o"], lp)
    dr2, h1b, act, loss, dg2, db2 = _mlp_fwd(r1, tgtp, w["ln1_g"], w["ln1_b"], w["w_up"], w["w_down"], w["b_up"],
                                             w["ln2_g"], w["ln2_b"], lp)

    g = {"ln2_g": dg2, "ln2_b": db2}
    dr1, dup, g["b_up"], g["ln1_g"], g["ln1_b"] = _mlp_bwd(h1b, dr2, r1, w["ln1_g"], w["w_up"], w["w_down"], w["b_up"], lp)
    g["w_down"] = _mm_tn(act, dr2, name="dw_down")
    g["w_up"] = _mm_tn(h1b, dup, name="dw_up", split=w["w_up"].shape[0])
    tok = ready(("w_down", "w_up"), g)
    dp, dys5g, dym, dhm, g["m_norm_g"] = _mix_bwd(
        dr1, w["w_o"], w["m_w_out"], p, ys5g, ymin, hm, _tie(w["m_norm_g"], tok), lp)
    g["w_o"] = _mm_tn(mix, dr1, name="dw_o")
    g["m_w_out"] = _mm_tn(ymin, dym, name="dw_mout")

    dp3 = dp.reshape(bsz, lp, NP)
    dp3, dqk3, dgate = _mlstm_bwd(dp3, qk3, p3, pg3, dhm.reshape(bsz, lp, d), cs, ns, ms)
    dp3, g["qk_conv_w"], g["qk_conv_b"] = _conv_bwd(dp3, p3, dqk3, pre3, w["qk_conv_w"])
    dz, dys5 = _glu_bwd(dys5g, z, y_s5.reshape(r, sw), w["s5_w_glu"], lp)
    g["s5_w_glu"] = _mm_tn(gy, dz, name="dw_glu", split=w["s5_w_glu"].shape[0])
    tok = ready(("s5_w_glu", "m_w_out", "w_o"), g)
    dp3, dbk, dcre, dcim, da, g["s5_d"] = _s5_bwd(dp3, p3, dys5.reshape(bsz, lp, sw), xs, bk, cre, cim, apow_rev,
                                                 _tie(w["s5_d"], tok))
    dbb_re_t, dbb_im_t, g["s5_c_re"], g["s5_c_im"], da_re, da_im = _s5_block_grads(dbk, dcre, dcim, da)
    g["s5_lambda_re"], g["s5_lambda_im"], g["s5_log_dt"], gb_re_t, gb_im_t = _s5_prep_bwd(
        w["s5_lambda_re"], w["s5_lambda_im"], w["s5_log_dt"], b_re_t, b_im_t, da_re, da_im, dbb_re_t, dbb_im_t)
    g["s5_b_re"] = jnp.transpose(gb_re_t, (1, 2, 0))
    g["s5_b_im"] = jnp.transpose(gb_im_t, (1, 2, 0))

    dp3 = lax.dynamic_update_slice(dp3, dgate, (0, 0, G_OFF))
    dp = dp3.reshape(r, NP)
    g["w_in"], g["b_in"] = _mm_tn(h0b, dp, name="dw_in", colsum=True)
    tok = ready(("w_in",), g)
    dpw = _mm_nt(dp, w["w_in"], lp, name="dh0", dep=tok)
    grad_x, g["ln0_g"], g["ln0_b"], g["meta_tokens"] = _ln0_bwd(x, w["meta_tokens"], dr1, dpw, w["ln0_g"])
    return loss, grad_x, g


_ANY = pl.BlockSpec(memory_space=pl.ANY)
_MESH = pl.DeviceIdType.MESH


def _place():
    return lax.axis_index("x"), lax.axis_index("y"), lax.axis_index("c")


def _gather_chips(shards):
    n = len(shards)

    def body(*refs):
        ins, outs = refs[:n], refs[n:2 * n]
        send, recv, loc = refs[2 * n:]
        x, y, c = _place()
        me = 2 * x + y
        peers = [(1 - x, y), (x, 1 - y), (1 - x, 1 - y)]

        def rc(a, k, slot):
            px, py = peers[k]
            return pltpu.make_async_remote_copy(src_ref=ins[a], dst_ref=outs[a].at[slot], send_sem=send.at[a, k],
                                                recv_sem=recv.at[a, k], device_id=(px, py, c), device_id_type=_MESH)

        own = [pltpu.make_async_copy(ins[a], outs[a].at[me], loc.at[a]) for a in range(n)]
        for cp in own:
            cp.start()
        out = [rc(a, k, me) for a in range(n) for k in range(3)]
        for cp in out:
            cp.start()
        for a in range(n):
            for k in range(3):
                rc(a, k, 2 * peers[k][0] + peers[k][1]).wait_recv()
        for cp in out:
            cp.wait_send()
        for cp in own:
            cp.wait()

    return _pcall(
        body, name="gather_chips", in_specs=[_ANY] * n, out_specs=[_ANY] * n,
        out_shape=[jax.ShapeDtypeStruct((4,) + s.shape, s.dtype) for s in shards],
        scratch_shapes=[pltpu.SemaphoreType.DMA((n, 3)), pltpu.SemaphoreType.DMA((n, 3)), pltpu.SemaphoreType.DMA((n,))],
    )(*shards)


_HBM = pl.BlockSpec(memory_space=pltpu.HBM)
_SEM = pl.BlockSpec(memory_space=pltpu.SEMAPHORE)
_EFFECT = pltpu.SideEffectType.DATAFLOW_SIDE_EFFECTING


def _xchg_copies(srcs, lands, send, recv, scatter):
    x, y, c = _place()
    me = 2 * x + y
    peers = [(1 - x, y), (x, 1 - y), (1 - x, 1 - y)]
    out = []
    for a in range(len(srcs)):
        for k, (px, py) in enumerate(peers):
            src = srcs[a].at[2 * px + py] if scatter else srcs[a]
            dst = lands[a].at[k] if scatter else lands[a].at[me]
            out.append(pltpu.make_async_remote_copy(src_ref=src, dst_ref=dst, send_sem=send.at[3 * a + k],
                                                    recv_sem=recv.at[3 * a + k], device_id=(px, py, c),
                                                    device_id_type=_MESH))
    return out


def _xchg_start(srcs, lands, *, name, scatter, dep=None):
    n = len(srcs)
    deps = [] if dep is None else [dep]
    nd = len(deps)

    def body(*refs):
        send, recv = refs[2 * n + nd], refs[2 * n + nd + 1]
        for cp in _xchg_copies(refs[:n], refs[n:2 * n], send, recv, scatter):
            cp.start()
        refs[-1][...] = jnp.zeros_like(refs[-1])

    hbm = lambda a: pltpu.HBM(a.shape, a.dtype)
    con = lambda a: pltpu.with_memory_space_constraint(a, pltpu.HBM)
    res = _pcall(
        body, name=name, in_specs=[_HBM] * (2 * n) + [_ANY] * nd,
        out_specs=[_SEM, _SEM] + [_HBM] * (2 * n) + [pl.BlockSpec(memory_space=pltpu.VMEM)],
        out_shape=[pltpu.SemaphoreType.DMA((3 * n,)), pltpu.SemaphoreType.DMA((3 * n,))]
        + [hbm(a) for a in srcs] + [hbm(a) for a in lands] + [jax.ShapeDtypeStruct((8, 128), F32)],
        input_output_aliases={i: 2 + i for i in range(2 * n)},
        compiler_params=pltpu.CompilerParams(has_side_effects=_EFFECT),
    )(*[con(a) for a in srcs], *[con(a) for a in lands], *deps)
    return res[0], res[1], list(res[2:2 + n]), list(res[2 + n:2 + 2 * n]), res[-1]


def _xchg_wait(send, recv, srcs, lands, after, *, name, scatter):
    n = len(srcs)
    afters = list(after) if isinstance(after, (list, tuple)) else [after]

    def body(*refs):
        s_ref, r_ref = refs[2 * n], refs[2 * n + 1]
        for cp in _xchg_copies(refs[:n], refs[n:2 * n], s_ref, r_ref, scatter):
            cp.wait_send()
            cp.wait_recv()

    hbm = lambda a: pltpu.HBM(a.shape, a.dtype)
    res = _pcall(
        body, name=name, in_specs=[_HBM] * (2 * n) + [_SEM, _SEM] + [_ANY] * len(afters),
        out_specs=[_HBM] * (2 * n),
        out_shape=[hbm(a) for a in srcs] + [hbm(a) for a in lands],
        input_output_aliases={i: i for i in range(2 * n)},
        compiler_params=pltpu.CompilerParams(has_side_effects=_EFFECT),
    )(*srcs, *lands, send, recv, *afters)
    return list(res[:n]), list(res[n:])


def _swap_cores(arrs, name="swap_cores"):
    n = len(arrs)

    def body(*refs):
        ins, outs = refs[:n], refs[n:2 * n]
        send, recv = refs[2 * n:]
        x, y, c = _place()
        cps = [pltpu.make_async_remote_copy(src_ref=ins[a], dst_ref=outs[a], send_sem=send.at[a], recv_sem=recv.at[a],
                                            device_id=(x, y, 1 - c), device_id_type=_MESH) for a in range(n)]
        for cp in cps:
            cp.start()
        for cp in cps:
            cp.wait_recv()
        for cp in cps:
            cp.wait_send()

    return _pcall(
        body, name=name, in_specs=[_ANY] * n, out_specs=[_ANY] * n,
        out_shape=[jax.ShapeDtypeStruct(s.shape, s.dtype) for s in arrs],
        scratch_shapes=[pltpu.SemaphoreType.DMA((n,)), pltpu.SemaphoreType.DMA((n,))],
    )(*arrs)


def _allreduce_small(v, dep=None):
    rows = v.shape[0]
    half = rows // 2
    assert half % 8 == 0 and 2 * half == rows
    deps = [] if dep is None else [dep]

    def body(v_ref, *rest):
        out_ref, sib_ref, pair_ref, slots_ref, send, recv = rest[len(deps):]
        x, y, c = _place()
        chip = 2 * x + y
        sibling = (x, y, 1 - c)
        peers = [(1 - x, y), (x, 1 - y), (1 - x, 1 - y)]
        mine = pl.ds(pl.multiple_of(c * half, 8), half)

        first = pltpu.make_async_remote_copy(src_ref=v_ref, dst_ref=sib_ref, send_sem=send.at[0], recv_sem=recv.at[0],
                                             device_id=sibling, device_id_type=_MESH)
        first.start()
        first.wait_recv()
        pair_ref[...] = v_ref[...] + sib_ref[...]
        slots_ref[chip] = pair_ref[mine, :]
        cross = [pltpu.make_async_remote_copy(src_ref=pair_ref.at[mine], dst_ref=slots_ref.at[chip],
                                              send_sem=send.at[1 + k], recv_sem=recv.at[1 + k],
                                              device_id=(px, py, c), device_id_type=_MESH)
                 for k, (px, py) in enumerate(peers)]
        for cp in cross:
            cp.start()
        for cp in cross:
            cp.wait_recv()
        out_ref[mine, :] = ((slots_ref[0] + slots_ref[1]) + slots_ref[2]) + slots_ref[3]
        last = pltpu.make_async_remote_copy(src_ref=out_ref.at[mine], dst_ref=out_ref.at[mine], send_sem=send.at[4],
                                            recv_sem=recv.at[4], device_id=sibling, device_id_type=_MESH)
        last.start()
        last.wait_recv()
        first.wait_send()
        for cp in cross:
            cp.wait_send()
        last.wait_send()

    vm = pl.BlockSpec(memory_space=pltpu.VMEM)
    return _pcall(
        body, name="allreduce_small", in_specs=[vm] + [_ANY] * len(deps), out_specs=vm,
        out_shape=jax.ShapeDtypeStruct((rows, 128), F32),
        scratch_shapes=[pltpu.VMEM((rows, 128), F32), pltpu.VMEM((rows, 128), F32), pltpu.VMEM((4, half, 128), F32),
                        pltpu.SemaphoreType.DMA((5,)), pltpu.SemaphoreType.DMA((5,))],
        compiler_params=_cp(None, 40),
    )(v, *deps)


def _sum_slots(parts, land, chip):
    ns, rows, cols = land.shape
    tm = _row_tile(rows, 256, 8)

    def body(chip_ref, own_ref, a_ref, o_ref):
        o_ref[...] = ((own_ref[...] + a_ref[0]) + a_ref[1]) + a_ref[2]

    return _pcall(
        body, name="sum_slots",
        grid_spec=pltpu.PrefetchScalarGridSpec(
            num_scalar_prefetch=1, grid=(rows // tm,),
            in_specs=[pl.BlockSpec((None, tm, cols), lambda i, c: (c[0], i, 0)),
                      pl.BlockSpec((ns, tm, cols), lambda i, c: (0, i, 0))],
            out_specs=pl.BlockSpec((tm, cols), lambda i, c: (i, 0))),
        out_shape=jax.ShapeDtypeStruct((rows, cols), F32),
        compiler_params=_cp(("parallel",), 40),
    )(jnp.reshape(chip, (1,)).astype(jnp.int32), parts, land)


def _adamw(w, m, v, g0, g1=None):
    rows, cols = w.shape[-2:]
    lead = w.ndim == 3
    tm = _row_tile(rows, max(8, (1 << 20) // (4 * cols)), 8)
    c1 = 1.0 - ADAM_B1 ** ADAM_STEP
    c2 = 1.0 - ADAM_B2 ** ADAM_STEP
    two = g1 is not None

    def body(*refs):
        w_ref, m_ref, v_ref, g0_ref = refs[:4]
        g_ref, d_ref, nm_ref, nv_ref = refs[-4:]
        g = g0_ref[...]
        if two:
            g = g + refs[4][...]
        nm = ADAM_B1 * m_ref[...] + (1.0 - ADAM_B1) * g
        nv = ADAM_B2 * v_ref[...] + (1.0 - ADAM_B2) * (g * g)
        g_ref[...] = g
        nm_ref[...] = nm
        nv_ref[...] = nv
        d_ref[...] = -ADAM_LR * ((nm / c1) / (jnp.sqrt(nv / c2) + ADAM_EPS) + ADAM_WD * w_ref[...])

    blk = pl.BlockSpec((tm, cols), lambda i: (i, 0))
    wblk = pl.BlockSpec((None, tm, cols), lambda i: (0, i, 0)) if lead else blk
    ins = [w, m, v, g0] + ([g1] if two else [])
    return _pcall(
        body, name="adamw", grid=(rows // tm,), in_specs=[wblk] * 3 + [blk] * (len(ins) - 3), out_specs=[wblk] * 4,
        out_shape=[jax.ShapeDtypeStruct(w.shape, F32)] * 4,
        compiler_params=_cp(("parallel",), 40),
    )(*ins)


_BIG = ("w_in", "s5_w_glu", "m_w_out", "w_o", "w_up", "w_down")
_SMALL = ("ln0_g", "ln0_b", "b_in", "qk_conv_b", "s5_lambda_re", "s5_lambda_im", "s5_log_dt", "s5_b_re", "s5_b_im",
          "s5_c_re", "s5_c_im", "s5_d", "m_norm_g", "ln1_g", "ln1_b", "b_up", "ln2_g", "ln2_b")
_SMALL_SHARDED = ("meta_tokens", "qk_conv_w")
_ORDER = ("meta_tokens", "ln0_g", "ln0_b", "w_in", "b_in", "qk_conv_w", "qk_conv_b", "s5_lambda_re", "s5_lambda_im",
          "s5_log_dt", "s5_b_re", "s5_b_im", "s5_c_re", "s5_c_im", "s5_d", "s5_w_glu", "m_norm_g", "m_w_out", "w_o",
          "ln1_g", "ln1_b", "w_up", "b_up", "w_down", "ln2_g", "ln2_b")


def _pack(arrs):
    flat = jnp.concatenate([a.reshape(-1) for a in arrs])
    n = flat.shape[0]
    rows = -(-n // 2048) * 16
    return jnp.pad(flat, (0, rows * 128 - n)).reshape(rows, 128)


def _unpack(packed, shapes):
    flat = packed.reshape(-1)
    out, off = [], 0
    for s in shapes:
        n = math.prod(s)
        out.append(flat[off:off + n].reshape(s))
        off += n
    return out


def kernel(x, meta_tokens, ln0_g, ln0_b, w_in, b_in, qk_conv_w, qk_conv_b, s5_lambda_re, s5_lambda_im, s5_log_dt, s5_b_re, s5_b_im, s5_c_re, s5_c_im, s5_d, s5_w_glu, m_norm_g, m_w_out, w_o, ln1_g, ln1_b, w_up, b_up, w_down, ln2_g, ln2_b, loss_target, m_meta_tokens, m_ln0_g, m_ln0_b, m_w_in, m_b_in, m_qk_conv_w, m_qk_conv_b, m_s5_lambda_re, m_s5_lambda_im, m_s5_log_dt, m_s5_b_re, m_s5_b_im, m_s5_c_re, m_s5_c_im, m_s5_d, m_s5_w_glu, m_m_norm_g, m_m_w_out, m_w_o, m_ln1_g, m_ln1_b, m_w_up, m_b_up, m_w_down, m_ln2_g, m_ln2_b, v_meta_tokens, v_ln0_g, v_ln0_b, v_w_in, v_b_in, v_qk_conv_w, v_qk_conv_b, v_s5_lambda_re, v_s5_lambda_im, v_s5_log_dt, v_s5_b_re, v_s5_b_im, v_s5_c_re, v_s5_c_im, v_s5_d, v_s5_w_glu, v_m_norm_g, v_m_w_out, v_w_o, v_ln1_g, v_ln1_b, v_w_up, v_b_up, v_w_down, v_ln2_g, v_ln2_b):
    wts = dict(meta_tokens=meta_tokens, ln0_g=ln0_g, ln0_b=ln0_b, w_in=w_in, b_in=b_in, qk_conv_w=qk_conv_w,
               qk_conv_b=qk_conv_b, s5_lambda_re=s5_lambda_re, s5_lambda_im=s5_lambda_im, s5_log_dt=s5_log_dt,
               s5_b_re=s5_b_re, s5_b_im=s5_b_im, s5_c_re=s5_c_re, s5_c_im=s5_c_im, s5_d=s5_d, s5_w_glu=s5_w_glu,
               m_norm_g=m_norm_g, m_w_out=m_w_out, w_o=w_o, ln1_g=ln1_g, ln1_b=ln1_b, w_up=w_up, b_up=b_up,
               w_down=w_down, ln2_g=ln2_g, ln2_b=ln2_b)
    mom = dict(meta_tokens=m_meta_tokens, ln0_g=m_ln0_g, ln0_b=m_ln0_b, w_in=m_w_in, b_in=m_b_in, qk_conv_w=m_qk_conv_w,
               qk_conv_b=m_qk_conv_b, s5_lambda_re=m_s5_lambda_re, s5_lambda_im=m_s5_lambda_im, s5_log_dt=m_s5_log_dt,
               s5_b_re=m_s5_b_re, s5_b_im=m_s5_b_im, s5_c_re=m_s5_c_re, s5_c_im=m_s5_c_im, s5_d=m_s5_d,
               s5_w_glu=m_s5_w_glu, m_norm_g=m_m_norm_g, m_w_out=m_m_w_out, w_o=m_w_o, ln1_g=m_ln1_g, ln1_b=m_ln1_b,
               w_up=m_w_up, b_up=m_b_up, w_down=m_w_down, ln2_g=m_ln2_g, ln2_b=m_ln2_b)
    var = dict(meta_tokens=v_meta_tokens, ln0_g=v_ln0_g, ln0_b=v_ln0_b, w_in=v_w_in, b_in=v_b_in, qk_conv_w=v_qk_conv_w,
               qk_conv_b=v_qk_conv_b, s5_lambda_re=v_s5_lambda_re, s5_lambda_im=v_s5_lambda_im, s5_log_dt=v_s5_log_dt,
               s5_b_re=v_s5_b_re, s5_b_im=v_s5_b_im, s5_c_re=v_s5_c_re, s5_c_im=v_s5_c_im, s5_d=v_s5_d,
               s5_w_glu=v_s5_w_glu, m_norm_g=v_m_norm_g, m_w_out=v_m_w_out, w_o=v_w_o, ln1_g=v_ln1_g, ln1_b=v_ln1_b,
               w_up=v_w_up, b_up=v_b_up, w_down=v_w_down, ln2_g=v_ln2_g, ln2_b=v_ln2_b)
    d = x.shape[-1]
    chip = 2 * lax.axis_index("x") + lax.axis_index("y")

    gw = dict(zip(_SMALL_SHARDED, _gather_chips([meta_tokens, qk_conv_w[0]])))
    own_w_in = _bf(w_in[0])
    fsend, frecv, fsrc, fland, ftok = _xchg_start([own_w_in], [lax.empty((4,) + own_w_in.shape, BF16)],
                                                  name="gather_w_in_start", scatter=False, dep=gw["qk_conv_w"])
    late_names = tuple(n for n in _BIG if n != "w_in")
    cat = lambda a: jnp.transpose(a, (1, 0, 2)).reshape(a.shape[1], 4 * a.shape[2])
    w = dict(
        meta_tokens=cat(gw["meta_tokens"]), ln0_g=ln0_g[None], ln0_b=_tie(ln0_b[None], ftok),
        qk_conv_w=cat(gw["qk_conv_w"]), qk_conv_b=qk_conv_b,
        s5_lambda_re=s5_lambda_re[0], s5_lambda_im=s5_lambda_im[0], s5_log_dt=s5_log_dt[0][:, None],
        s5_b_re=s5_b_re[0], s5_b_im=s5_b_im[0], s5_c_re=s5_c_re[0], s5_c_im=s5_c_im[0], s5_d=s5_d,
        m_norm_g=m_norm_g, ln1_g=ln1_g, ln1_b=ln1_b, b_up=b_up, ln2_g=ln2_g, ln2_b=ln2_b)
    in_flight = {}

    def place_own(src, land):
        return lax.dynamic_update_slice(land, src[None], (chip,) + (0,) * src.ndim)

    small_names = _SMALL + _SMALL_SHARDED

    def view(n, a):
        return jnp.swapaxes(a, -1, -2) if n in ("s5_b_re", "s5_b_im") else a

    small_wmv = [_pack([view(n, dct[n]) for n in small_names]) for dct in (wts, mom, var)]

    def early(after):
        src, land = _xchg_wait(fsend, frecv, fsrc, fland, tuple(after) + tuple(small_wmv), name="gather_w_in_wait",
                               scatter=False)
        late_src = [_bf(wts[n][0]) for n in late_names]
        st = _xchg_start(late_src, [lax.empty((4,) + a.shape, a.dtype) for a in late_src], name="gather_late_start",
                         scatter=False, dep=src[0])
        in_flight["late"] = st[:4]
        return dict(w_in=_w_in_from_slots(place_own(src[0], land[0]), IN_CHUNK), b_in=_tie(_to_pad_cols(b_in), st[4]))

    def late(after):
        src, land = _xchg_wait(*in_flight["late"], after, name="gather_late_wait", scatter=False)
        full = {n: place_own(s, ld) for n, s, ld in zip(late_names, src, land)}
        return dict(s5_w_glu=full["s5_w_glu"], m_w_out=full["m_w_out"].reshape(d, d), w_o=full["w_o"].reshape(d, d),
                    w_up=full["w_up"], w_down=full["w_down"].reshape(4 * d, d))

    flying = []

    def ready(names, g):
        parts = dict(
            w_in=lambda: _slots_from_w_in(g["w_in"][0]), s5_w_glu=lambda: g["s5_w_glu"],
            m_w_out=lambda: g["m_w_out"].reshape(4, d // 4, d), w_o=lambda: g["w_o"].reshape(4, d // 4, d),
            w_up=lambda: g["w_up"], w_down=lambda: g["w_down"].reshape(4, d, d))
        src = [parts[n]() for n in names]
        land = [lax.empty((3,) + a.shape[1:], a.dtype) for a in src]
        st = _xchg_start(src, land, name="scatter_" + names[0] + "_start", scatter=True)
        flying.append((names,) + st[:4])
        return st[4]

    loss, grad_x, g = _local_step(x, loss_target, w, early, late, ready)
    g["b_in"] = _from_pad_cols(g["b_in"])

    res = {}

    def flat(a):
        return jnp.swapaxes(a, -1, -2).reshape(a.shape[:-2] + (-1, 128))

    def unflat(y, shape):
        return jnp.swapaxes(y.reshape(shape[:-2] + (shape[-1], shape[-2])), -1, -2)

    def finish(groups, after, tag):
        mine = {}
        for names, send, recv, src, land in groups:
            src, land = _xchg_wait(send, recv, src, land, after, name="scatter_" + names[0] + "_wait", scatter=True)
            for n, s, ld in zip(names, src, land):
                mine[n] = _sum_slots(s, ld, chip)
        theirs = _swap_cores(list(mine.values()), name="swap_cores_" + tag)
        for n, t in zip(mine, theirs):
            if n == "w_in":
                res[n] = [unflat(r, wts[n].shape) for r in _adamw(flat(wts[n]), flat(mom[n]), flat(var[n]),
                                                                  flat(mine[n]), flat(t))]
            else:
                res[n] = _adamw(wts[n], mom[n], var[n], mine[n], t)

    finish(flying[:-1], g["ln0_g"], "a")

    small_shapes = [(1, 128)] + [view(n, wts[n]).shape for n in _SMALL] + [g[n].shape for n in _SMALL_SHARDED]
    packed = _pack([loss] + [view(n, g[n]) for n in _SMALL] + [g[n] for n in _SMALL_SHARDED])
    tot = _unpack(_allreduce_small(packed, dep=res["w_o"][3]), small_shapes)
    loss_out = tot[0][0, 0]
    gsm = dict(zip(_SMALL + _SMALL_SHARDED, tot[1:]))
    for n in _SMALL_SHARDED:
        cols = wts[n].shape[-1]
        gsm[n] = lax.dynamic_slice_in_dim(gsm[n], chip * cols, cols, axis=1).reshape(wts[n].shape)

    names = small_names
    shapes = [view(n, wts[n]).shape for n in names]
    small_out = _adamw(*small_wmv, _pack([gsm[n] for n in names]))
    small_res = [_unpack(r, shapes) for r in small_out]
    for j, n in enumerate(names):
        res[n] = [view(n, small_res[q][j]) for q in range(4)]
    finish(flying[-1:], small_out[0], "b")

    return (loss_out, grad_x, *[res[n][0] for n in _ORDER], *[res[n][1] for n in _ORDER],
            *[res[n][2] for n in _ORDER], *[res[n][3] for n in _ORDER])
```

```python
import functools
import math

import jax
import jax.numpy as jnp
from jax import lax
from jax.experimental import pallas as pl
from jax.experimental.pallas import tpu as pltpu

F32 = jnp.float32
BF16 = jnp.bfloat16
HI = lax.Precision.HIGHEST

N_META = 16
M_HEADS = 4
M_CHUNK = 128
PAD = M_CHUNK - N_META
CONV_W = 4
HALO_ROWS = 16
S5_GROUP = 16
S5_STATE = 64
S5_KCH = 4
LN_EPS = 1e-5
ALPHA = 2.0 ** 0.25
NEG = -1e30
ADAM_LR, ADAM_B1, ADAM_B2, ADAM_EPS, ADAM_WD, ADAM_STEP = 0.001, 0.9, 0.999, 1e-08, 0.01, 10

O_OFF, GS_OFF, GM_OFF, V_OFF, Q_OFF, K_OFF, U_OFF, G_OFF, NP = 0, 1024, 2048, 3072, 4096, 4608, 5120, 5632, 5760

NN = ((1,), (0,))
NT = ((1,), (1,))
TN = ((0,), (0,))


def _dot(a, b, dims=NN, prec=None):
    return lax.dot_general(a, b, (dims, ((), ())), preferred_element_type=F32, precision=prec)


def _bf(x):
    return x.astype(BF16)


def _sig(x):
    return 0.5 * jnp.tanh(0.5 * x) + 0.5


def _pcall(body, **kw):
    return pl.pallas_call(body, **kw)


def _cp(sem=None, vmem_mb=None):
    kw = {}
    if sem is not None:
        kw["dimension_semantics"] = sem
    if vmem_mb is not None:
        kw["vmem_limit_bytes"] = vmem_mb << 20
    return pltpu.CompilerParams(**kw)


def _row_tile(n, want, mult=16):
    best = None
    for t in range(mult, want + 1, mult):
        if n % t == 0:
            best = t
    assert best is not None, (n, want)
    return best


def _resident(shape):
    nd = len(shape)
    return pl.BlockSpec(shape, lambda *_: (0,) * nd, pipeline_mode=pl.Buffered(1))


def _const(shape):
    nd = len(shape)
    return pl.BlockSpec(shape, lambda *_: (0,) * nd)


def _ln_fwd(x, g, b):
    mu = jnp.mean(x, axis=-1, keepdims=True)
    xc = x - mu
    var = jnp.mean(xc * xc, axis=-1, keepdims=True)
    rstd = lax.rsqrt(var + LN_EPS)
    xhat = xc * rstd
    return xhat * g + b, xhat, rstd


def _ln_bwd(dy, xhat, rstd, g):
    dxh = dy * g
    m1 = jnp.mean(dxh, axis=-1, keepdims=True)
    m2 = jnp.mean(dxh * xhat, axis=-1, keepdims=True)
    return rstd * (dxh - m1 - xhat * m2)


def _colsum(x):
    return jnp.sum(x, axis=0, keepdims=True)


def _to_pad_cols(w):
    u, q, k, v, o, gi, gf, gs, gm = (w[..., 0:512], w[..., 512:1024], w[..., 1024:1536], w[..., 1536:2560],
                                     w[..., 2560:3584], w[..., 3584:3588], w[..., 3588:3592], w[..., 3592:4616],
                                     w[..., 4616:5640])
    z = jnp.zeros(w.shape[:-1] + (NP - G_OFF - 8,), w.dtype)
    return jnp.concatenate([o, gs, gm, v, q, k, u, gi, gf, z], axis=-1)


def _from_pad_cols(w):
    o, gs, gm, v, q, k, u = (w[..., O_OFF:GS_OFF], w[..., GS_OFF:GM_OFF], w[..., GM_OFF:V_OFF], w[..., V_OFF:Q_OFF],
                             w[..., Q_OFF:K_OFF], w[..., K_OFF:U_OFF], w[..., U_OFF:G_OFF])
    gi, gf = w[..., G_OFF:G_OFF + 4], w[..., G_OFF + 4:G_OFF + 8]
    return jnp.concatenate([u, q, k, v, o, gi, gf, gs, gm], axis=-1)


_IN_REF = (("u", 512), ("q", 512), ("k", 512), ("v", 1024), ("o", 1024), ("i", 4), ("f", 4), ("gs", 1024), ("gm", 1024))
_IN_PAD = (("o", O_OFF), ("gs", GS_OFF), ("gm", GM_OFF), ("v", V_OFF), ("q", Q_OFF), ("k", K_OFF), ("u", U_OFF),
           ("i", G_OFF), ("f", G_OFF + 4))


def _in_ref_ranges():
    out, off = {}, 0
    for n, s in _IN_REF:
        out[n] = (off, off + s)
        off += s
    return out, off


def _w_in_from_slots(g, chunk=None):
    rng, total = _in_ref_ranges()
    width = total // g.shape[0]
    cols = []
    for n, _ in _IN_PAD:
        a, b = rng[n]
        while a < b:
            s = a // width
            e = min(b, (s + 1) * width)
            cols.append(g[s][:, a - s * width:e - s * width])
            a = e
    cols.append(jnp.zeros((g.shape[1], NP - G_OFF - 8), g.dtype))
    if chunk is None:
        return jnp.concatenate(cols, axis=1)
    chunks, cur, room = [], [], chunk
    for c in cols:
        while c.shape[1] > 0:
            take = min(room, c.shape[1])
            cur.append(c[:, :take])
            c, room = c[:, take:], room - take
            if room == 0:
                chunks.append(jnp.concatenate(cur, axis=1))
                cur, room = [], chunk
    assert not cur
    return jnp.stack(chunks, axis=0)


def _slots_from_w_in(wp, nslot=4):
    rng, total = _in_ref_ranges()
    width = total // nslot
    pad_off = dict(_IN_PAD)
    slots = []
    for s in range(nslot):
        lo, hi = s * width, (s + 1) * width
        cols = []
        for n, _ in _IN_REF:
            a, b = rng[n]
            x0, x1 = max(a, lo), min(b, hi)
            if x0 < x1:
                cols.append(wp[:, pad_off[n] + x0 - a:pad_off[n] + x1 - a])
        slots.append(jnp.concatenate(cols, axis=1))
    return jnp.stack(slots, axis=0)


HEAD = PAD + N_META


def _ln0_in(j, x_ref, meta_ref):
    first = jnp.concatenate([jnp.zeros((PAD, meta_ref.shape[1]), F32), meta_ref[...]], axis=0)
    return jnp.where(j == 0, first[None], x_ref[...])


def _ln0_fwd(x, meta, g, b):
    bsz, seq, d = x.shape
    nb = seq // HEAD + 1

    def body(x_ref, m_ref, g_ref, b_ref, o_ref, ob_ref):
        y, _, _ = _ln_fwd(_ln0_in(pl.program_id(0), x_ref, m_ref), g_ref[...], b_ref[...])
        o_ref[...] = y
        ob_ref[...] = _bf(y)

    row = pl.BlockSpec((bsz, HEAD, d), lambda j: (0, j, 0))
    h0, h0b = _pcall(
        body, name="ln0_fwd", grid=(nb,),
        in_specs=[pl.BlockSpec((bsz, HEAD, d), lambda j: (0, jnp.maximum(j - 1, 0), 0)), _const((N_META, d)),
                  _const((1, d)), _const((1, d))],
        out_specs=[row, row],
        out_shape=[jax.ShapeDtypeStruct((bsz, nb * HEAD, d), F32), jax.ShapeDtypeStruct((bsz, nb * HEAD, d), BF16)],
        compiler_params=_cp(("arbitrary",)),
    )(x, meta, g, b)
    return h0.reshape(-1, d), h0b.reshape(-1, d)


def _ln0_bwd(x, meta, dr1, dpw, g):
    bsz, seq, d = x.shape
    nb = seq // HEAD + 1

    def body(x_ref, m_ref, a_ref, c_ref, g_ref, o_ref, dg_ref, db_ref, dm_ref):
        j = pl.program_id(0)

        @pl.when(j == 0)
        def _():
            dg_ref[...] = jnp.zeros_like(dg_ref)
            db_ref[...] = jnp.zeros_like(db_ref)
            dm_ref[...] = jnp.zeros_like(dm_ref)

        dy = ALPHA * a_ref[...] + c_ref[...]
        _, xhat, rstd = _ln_fwd(_ln0_in(j, x_ref, m_ref), g_ref[...], 0.0)
        dx = _ln_bwd(dy, xhat, rstd, g_ref[...])
        o_ref[...] = dx
        dg_ref[...] += _colsum((dy * xhat).reshape(bsz * HEAD, d))
        db_ref[...] += _colsum(dy.reshape(bsz * HEAD, d))

        @pl.when(j == 0)
        def _():
            dm_ref[...] += jnp.sum(dx[:, PAD:, :], axis=0)

    row = pl.BlockSpec((bsz, HEAD, d), lambda j: (0, j, 0))
    tok = pl.BlockSpec((bsz, HEAD, d), lambda j: (0, jnp.maximum(j - 1, 0), 0))
    lp = nb * HEAD
    return _pcall(
        body, name="ln0_bwd", grid=(nb,),
        in_specs=[tok, _const((N_META, d)), row, row, _const((1, d))],
        out_specs=[tok, _const((1, d)), _const((1, d)), _const((N_META, d))],
        out_shape=[jax.ShapeDtypeStruct((bsz, seq, d), F32), jax.ShapeDtypeStruct((1, d), F32),
                   jax.ShapeDtypeStruct((1, d), F32), jax.ShapeDtypeStruct((N_META, d), F32)],
        compiler_params=_cp(("arbitrary",)),
    )(x, meta, dr1.reshape(bsz, lp, d), dpw.reshape(bsz, lp, d), g)


IN_CHUNK = 1152


def _chunk_cols(w):
    k, n = w.shape
    return jnp.transpose(w.reshape(k, n // IN_CHUNK, IN_CHUNK), (1, 0, 2))


def _inproj(h0b, w3, bias, lp):
    r, d = h0b.shape
    nj, _, tn = w3.shape
    tm = _row_tile(lp, 1056)
    tps = lp // tm

    def body(a_ref, w_ref, b_ref, o_ref, gate_ref):
        i = pl.program_id(0)
        j = pl.program_id(1)
        acc = _dot(a_ref[...], w_ref[j]) + b_ref[...]
        t = (i % tps) * tm + lax.broadcasted_iota(jnp.int32, (tm, 1), 0)
        acc = jnp.where(t >= PAD, acc, 0.0)
        o_ref[...] = _bf(acc)

        @pl.when(j == nj - 1)
        def _():
            gate_ref[...] = acc[:, tn - 128:]

    return _pcall(
        body, name="inproj", grid=(r // tm, nj),
        in_specs=[pl.BlockSpec((tm, d), lambda i, j: (i, 0)), _resident(w3.shape),
                  pl.BlockSpec((1, tn), lambda i, j: (0, j))],
        out_specs=[pl.BlockSpec((tm, tn), lambda i, j: (i, j)), pl.BlockSpec((tm, 128), lambda i, j: (i, 0))],
        out_shape=[jax.ShapeDtypeStruct((r, nj * tn), BF16), jax.ShapeDtypeStruct((r, 128), F32)],
        compiler_params=_cp(("parallel", "arbitrary"), 48),
    )(h0b, w3, bias)


def _mm_tn(a, b, *, name, split=1, colsum=False, tk_want=2112):
    r, m = a.shape
    n = b.shape[1]
    tk = _row_tile(r, tk_want)
    tm = min(m, 1024)
    ns = n // split
    tn = ns
    for cand in (1024, 1152, 640, 512, 128):
        if ns % cand == 0 and cand <= ns:
            tn = cand
            break
    nb = ns // tn
    nk = r // tk

    def body(a_ref, b_ref, o_ref, *rest):
        acc = rest[-1]
        k = pl.program_id(2)

        @pl.when(k == 0)
        def _():
            acc[...] = jnp.zeros_like(acc)

        bt = b_ref[...]
        acc[...] += _dot(_bf(a_ref[...]), _bf(bt), TN)

        @pl.when(k == nk - 1)
        def _():
            o_ref[...] = acc[...]

        if colsum:
            cs_ref = rest[0]

            @pl.when(k == 0)
            def _():
                cs_ref[...] = jnp.zeros_like(cs_ref)

            cs_ref[...] += _colsum(bt.astype(F32))

    out_specs = [pl.BlockSpec((None, tm, tn), lambda i, j, k: (j // nb, i, j % nb))]
    out_shape = [jax.ShapeDtypeStruct((split, m, ns), F32)]
    if colsum:
        assert m == tm
        out_specs.append(pl.BlockSpec((1, tn), lambda i, j, k: (0, j)))
        out_shape.append(jax.ShapeDtypeStruct((1, n), F32))
    res = _pcall(
        body, name=name, grid=(m // tm, n // tn, nk),
        in_specs=[pl.BlockSpec((tk, tm), lambda i, j, k: (k, i)), pl.BlockSpec((tk, tn), lambda i, j, k: (k, j))],
        out_specs=out_specs, out_shape=out_shape,
        scratch_shapes=[pltpu.VMEM((tm, tn), F32)],
        compiler_params=_cp(("parallel", "parallel", "arbitrary"), 56),
    )(a, b)
    return res if colsum else res[0]


def _mm_nt(a, w3, lp, *, name, dep=None):
    r, kdim = a.shape
    nk, n, tk = w3.shape
    assert nk * tk == kdim
    tm = _row_tile(lp, 1056)
    deps = [] if dep is None else [dep]

    def body(a_ref, w_ref, *rest):
        o_ref, acc = rest[-2:]
        k = pl.program_id(1)

        @pl.when(k == 0)
        def _():
            acc[...] = jnp.zeros_like(acc)

        acc[...] += _dot(_bf(a_ref[...]), w_ref[k], NT)

        @pl.when(k == nk - 1)
        def _():
            o_ref[...] = acc[...]

    return _pcall(
        body, name=name, grid=(r // tm, nk),
        in_specs=[pl.BlockSpec((tm, tk), lambda i, k: (i, k)), _resident(w3.shape)]
        + [_const(dp_.shape) for dp_ in deps],
        out_specs=pl.BlockSpec((tm, n), lambda i, k: (i, 0)),
        out_shape=jax.ShapeDtypeStruct((r, n), F32),
        scratch_shapes=[pltpu.VMEM((tm, n), F32)],
        compiler_params=_cp(("parallel", "arbitrary"), 48),
    )(a, w3, *deps)


def _s5_prep(lam_re, lam_im, log_dt, b_re_t, b_im_t):
    g, p = lam_re.shape
    h = b_re_t.shape[0]

    def body(lr_ref, li_ref, ldt_ref, br_ref, bi_ref, pr_ref, pi_ref, bbr_ref, bbi_ref):
        lr, li = lr_ref[...], li_ref[...]
        dt = jnp.exp(ldt_ref[...])
        e = jnp.exp(lr * dt)
        ar, ai = e * jnp.cos(li * dt), e * jnp.sin(li * dt)
        den = lr * lr + li * li
        cr = ((ar - 1.0) * lr + ai * li) / den
        ci = (ai * lr - (ar - 1.0) * li) / den
        br, bi = br_ref[...], bi_ref[...]
        bbr_ref[...] = cr[None] * br - ci[None] * bi
        bbi_ref[...] = cr[None] * bi + ci[None] * br
        xr, xi = ar, ai
        pr_ref[0] = xr
        pi_ref[0] = xi
        for t in range(1, 8):
            xr, xi = xr * ar - xi * ai, xr * ai + xi * ar
            pr_ref[t] = xr
            pi_ref[t] = xi

    sd = jax.ShapeDtypeStruct
    return _pcall(body, name="s5_prep",
                  out_shape=[sd((8, g, p), F32), sd((8, g, p), F32), sd((h, g, p), F32), sd((h, g, p), F32)])(
        lam_re, lam_im, log_dt, b_re_t, b_im_t)


def _s5_prep_bwd(lam_re, lam_im, log_dt, b_re_t, b_im_t, da_re, da_im, dbb_re_t, dbb_im_t):
    g, p = lam_re.shape
    h = b_re_t.shape[0]

    def body(lr_ref, li_ref, ldt_ref, br_ref, bi_ref, dar_ref, dai_ref, dbr_ref, dbi_ref,
             glr_ref, gli_ref, gdt_ref, gbr_ref, gbi_ref):
        lr, li = lr_ref[...], li_ref[...]
        dt = jnp.exp(ldt_ref[...])
        e = jnp.exp(lr * dt)
        ar, ai = e * jnp.cos(li * dt), e * jnp.sin(li * dt)
        den = lr * lr + li * li
        cr = ((ar - 1.0) * lr + ai * li) / den
        ci = (ai * lr - (ar - 1.0) * li) / den
        br, bi = br_ref[...], bi_ref[...]
        gr, gi = dbr_ref[...], dbi_ref[...]
        gbr_ref[...] = gr * cr[None] + gi * ci[None]
        gbi_ref[...] = gi * cr[None] - gr * ci[None]
        gcr = jnp.sum(gr * br + gi * bi, axis=0)
        gci = jnp.sum(gi * br - gr * bi, axis=0)
        ilr, ili = lr / den, -li / den
        gar = dar_ref[...] + gcr * ilr + gci * ili
        gai = dai_ref[...] + gci * ilr - gcr * ili
        qr, qi = cr * ilr - ci * ili, cr * ili + ci * ilr
        glr = -(gcr * qr + gci * qi)
        gli = -(gci * qr - gcr * qi)
        gzr = gar * ar + gai * ai
        gzi = gai * ar - gar * ai
        glr_ref[...] = glr + gzr * dt
        gli_ref[...] = gli + gzi * dt
        gdt_ref[...] = jnp.sum(gzr * lr + gzi * li, axis=1, keepdims=True) * dt

    sd = jax.ShapeDtypeStruct
    return _pcall(body, name="s5_prep_bwd",
                  out_shape=[sd((g, p), F32), sd((g, p), F32), sd((g, 1), F32), sd((h, g, p), F32), sd((h, g, p), F32)])(
        lam_re, lam_im, log_dt, b_re_t, b_im_t, da_re, da_im, dbb_re_t, dbb_im_t)


def _cmul(xr, xi, yr, yi):
    return xr * yr - xi * yi, xr * yi + xi * yr


def _dot5(a, b, dims=NN):
    return _dot(_bf(a), _bf(b), dims)


def _s5_fwd(p3, bk, cre, cim, apow, dskip):
    bsz, lp, _ = p3.shape
    tt = _row_tile(lp, 528, 8)
    nt = lp // tt
    nblk = tt // 8
    hw = 512

    def body(u_ref, bk_ref, cre_ref, cim_ref, ap_ref, d_ref, y_ref, xs_ref, car_ref):
        t = pl.program_id(2)

        @pl.when(t == 0)
        def _():
            car_ref[...] = jnp.zeros_like(car_ref)

        u = u_ref[...].astype(F32)
        xs_ref[...] = _dot5(u, bk_ref[...])
        ap = ap_ref[...]
        apr, api = ap[:, :hw], ap[:, hw:]
        rows = lax.broadcasted_iota(jnp.int32, (8, hw), 0)
        lev = [(d, jnp.where(rows < d, 0.0, jnp.broadcast_to(apr[d - 1:d, :], (8, hw))),
                jnp.where(rows < d, 0.0, jnp.broadcast_to(api[d - 1:d, :], (8, hw)))) for d in (1, 2, 4)]

        def blk(i, carry):
            cr, ci = carry
            off = pl.multiple_of(i * 8, 8)
            x = xs_ref[pl.ds(off, 8), :]
            xr, xi = x[:, :hw], x[:, hw:]
            for d, lr, li in lev:
                mr, mi = _cmul(pltpu.roll(xr, d, 0), pltpu.roll(xi, d, 0), lr, li)
                xr, xi = xr + mr, xi + mi
            mr, mi = _cmul(apr, api, cr, ci)
            xr, xi = xr + mr, xi + mi
            xs_ref[pl.ds(off, 8), :] = jnp.concatenate([xr, xi], axis=1)
            return xr[7:8, :], xi[7:8, :]

        c0 = car_ref[...]
        cr, ci = lax.fori_loop(0, nblk, blk, (c0[0:1, :hw], c0[0:1, hw:]))
        car_ref[...] = jnp.broadcast_to(jnp.concatenate([cr, ci], axis=1), car_ref.shape)
        xs = xs_ref[...]
        y_ref[...] = (_dot5(xs[:, :hw], cre_ref[...]) - _dot5(xs[:, hw:], cim_ref[...])
                      + d_ref[...] * u)

    ub = U_OFF // 128
    return _pcall(
        body, name="s5_fwd", grid=(S5_KCH, bsz, nt),
        in_specs=[pl.BlockSpec((None, tt, 128), lambda k, b, t: (b, t, ub + k)),
                  pl.BlockSpec((None, 128, 2 * hw), lambda k, b, t: (k, 0, 0)),
                  pl.BlockSpec((None, hw, 128), lambda k, b, t: (k, 0, 0)),
                  pl.BlockSpec((None, hw, 128), lambda k, b, t: (k, 0, 0)),
                  pl.BlockSpec((None, 8, 2 * hw), lambda k, b, t: (k, 0, 0)),
                  pl.BlockSpec((1, 128), lambda k, b, t: (0, k))],
        out_specs=[pl.BlockSpec((None, tt, 128), lambda k, b, t: (b, t, k)),
                   pl.BlockSpec((None, None, tt, 2 * hw), lambda k, b, t: (b, k, t, 0))],
        out_shape=[jax.ShapeDtypeStruct((bsz, lp, S5_KCH * 128), F32),
                   jax.ShapeDtypeStruct((bsz, S5_KCH, lp, 2 * hw), F32)],
        scratch_shapes=[pltpu.VMEM((8, 2 * hw), F32)],
        compiler_params=_cp(("parallel", "parallel", "arbitrary"), 40),
    )(p3, bk, cre, cim, apow, dskip)


def _s5_bwd(dp3, p3, dy3, xs, bk, cre, cim, apow_rev, dskip):
    bsz, lp, _ = p3.shape
    tt = _row_tile(lp, 528, 8)
    nt = lp // tt
    nblk = tt // 8
    hw = 512
    tb = tt // 8

    def body(dp_any, u_ref, dy_ref, xs_ref, halo_ref, bkt_ref, cre_ref, cim_ref, ap_ref, d_ref,
             du_ref, dbk_ref, dcre_ref, dcim_ref, da_ref, dd_ref, g_ref, ext_ref, car_ref):
        b = pl.program_id(1)
        t = pl.program_id(2)
        tidx = nt - 1 - t

        @pl.when(t == 0)
        def _():
            car_ref[...] = jnp.zeros_like(car_ref)

        @pl.when((b == 0) & (t == 0))
        def _():
            dbk_ref[...] = jnp.zeros_like(dbk_ref)
            dcre_ref[...] = jnp.zeros_like(dcre_ref)
            dcim_ref[...] = jnp.zeros_like(dcim_ref)
            da_ref[...] = jnp.zeros_like(da_ref)
            dd_ref[...] = jnp.zeros_like(dd_ref)

        u = u_ref[...].astype(F32)
        dy = dy_ref[...]
        g_ref[:, :hw] = _dot5(dy, cre_ref[...])
        g_ref[:, hw:] = -_dot5(dy, cim_ref[...])
        ap = ap_ref[...]
        apr, api = ap[:, :hw], -ap[:, hw:]
        rows = lax.broadcasted_iota(jnp.int32, (8, hw), 0)
        lev = [(d, jnp.where(rows >= 8 - d, 0.0, jnp.broadcast_to(apr[8 - d:9 - d, :], (8, hw))),
                jnp.where(rows >= 8 - d, 0.0, jnp.broadcast_to(api[8 - d:9 - d, :], (8, hw)))) for d in (1, 2, 4)]

        def blk(i, carry):
            cr, ci = carry
            off = pl.multiple_of((nblk - 1 - i) * 8, 8)
            x = g_ref[pl.ds(off, 8), :]
            xr, xi = x[:, :hw], x[:, hw:]
            for d, lr, li in lev:
                mr, mi = _cmul(pltpu.roll(xr, 8 - d, 0), pltpu.roll(xi, 8 - d, 0), lr, li)
                xr, xi = xr + mr, xi + mi
            mr, mi = _cmul(apr, api, cr, ci)
            xr, xi = xr + mr, xi + mi
            g_ref[pl.ds(off, 8), :] = jnp.concatenate([xr, xi], axis=1)
            return xr[0:1, :], xi[0:1, :]

        c0 = car_ref[...]
        cr, ci = lax.fori_loop(0, nblk, blk, (c0[0:1, :hw], c0[0:1, hw:]))
        car_ref[...] = jnp.broadcast_to(jnp.concatenate([cr, ci], axis=1), car_ref.shape)

        gg = g_ref[...]
        du = _dot5(gg, bkt_ref[...]) + d_ref[...] * dy
        trow = tidx * tt + lax.broadcasted_iota(jnp.int32, (tt, 1), 0)
        du_ref[...] = jnp.where(trow >= PAD, du, 0.0).astype(du_ref.dtype)
        dbk_ref[...] += _dot5(u, gg, TN)
        xsv = xs_ref[...]
        dcre_ref[...] += _dot5(dy, xsv[:, :hw], TN)
        dcim_ref[...] -= _dot5(dy, xsv[:, hw:], TN)
        dd_ref[...] += _colsum(dy * u)
        ext_ref[0:8, :] = jnp.where(tidx == 0, 0.0, halo_ref[...])
        ext_ref[8:, :] = xsv
        xp = ext_ref[pl.ds(7, tt), :]
        gr, gi, pr, pi = gg[:, :hw], gg[:, hw:], xp[:, :hw], xp[:, hw:]
        da_ref[:, :hw] += _colsum(gr * pr + gi * pi)
        da_ref[:, hw:] += _colsum(gi * pr - gr * pi)

    ub = U_OFF // 128
    sd = jax.ShapeDtypeStruct
    rt = lambda t: nt - 1 - t
    tr = lambda a: jnp.swapaxes(a, 1, 2)
    res = _pcall(
        body, name="s5_bwd", grid=(S5_KCH, bsz, nt),
        in_specs=[pl.BlockSpec(memory_space=pl.ANY),
                  pl.BlockSpec((None, tt, 128), lambda k, b, t: (b, rt(t), ub + k)),
                  pl.BlockSpec((None, tt, 128), lambda k, b, t: (b, rt(t), k)),
                  pl.BlockSpec((None, None, tt, 2 * hw), lambda k, b, t: (b, k, rt(t), 0)),
                  pl.BlockSpec((None, None, 8, 2 * hw), lambda k, b, t: (b, k, jnp.maximum(rt(t) * tb - 1, 0), 0)),
                  pl.BlockSpec((None, 2 * hw, 128), lambda k, b, t: (k, 0, 0)),
                  pl.BlockSpec((None, 128, hw), lambda k, b, t: (k, 0, 0)),
                  pl.BlockSpec((None, 128, hw), lambda k, b, t: (k, 0, 0)),
                  pl.BlockSpec((None, 8, 2 * hw), lambda k, b, t: (k, 0, 0)),
                  pl.BlockSpec((1, 128), lambda k, b, t: (0, k))],
        out_specs=[pl.BlockSpec((None, tt, 128), lambda k, b, t: (b, rt(t), ub + k)),
                   pl.BlockSpec((None, 128, 2 * hw), lambda k, b, t: (k, 0, 0)),
                   pl.BlockSpec((None, 128, hw), lambda k, b, t: (k, 0, 0)),
                   pl.BlockSpec((None, 128, hw), lambda k, b, t: (k, 0, 0)),
                   pl.BlockSpec((None, 1, 2 * hw), lambda k, b, t: (k, 0, 0)),
                   pl.BlockSpec((1, 128), lambda k, b, t: (0, k))],
        out_shape=[sd(dp3.shape, dp3.dtype), sd((S5_KCH, 128, 2 * hw), F32), sd((S5_KCH, 128, hw), F32),
                   sd((S5_KCH, 128, hw), F32), sd((S5_KCH, 1, 2 * hw), F32), sd((1, S5_KCH * 128), F32)],
        scratch_shapes=[pltpu.VMEM((tt, 2 * hw), F32), pltpu.VMEM((tt + 8, 2 * hw), F32), pltpu.VMEM((8, 2 * hw), F32)],
        input_output_aliases={0: 0},
        compiler_params=_cp(("arbitrary", "arbitrary", "arbitrary"), 48),
    )(dp3, p3, dy3, xs, xs, tr(bk), tr(cre), tr(cim), apow_rev, dskip)
    return res[0], res[1], tr(res[2]), tr(res[3]), res[4], res[5]


_G0 = math.sqrt(2.0 / math.pi)
_G1 = 0.044715


def _gelu(y):
    return 0.5 * y * (1.0 + jnp.tanh(_G0 * (y + _G1 * y * y * y)))


def _gelu_grad(y):
    th = jnp.tanh(_G0 * (y + _G1 * y * y * y))
    return 0.5 * (1.0 + th) + 0.5 * y * (1.0 - th * th) * _G0 * (1.0 + 3.0 * _G1 * y * y)


def _glu_fwd(y_s5, wglu_g, lp):
    r, w = y_s5.shape
    tm = _row_tile(lp, 416)
    cw = wglu_g.shape[2]

    def body(y_ref, w_ref, gy_ref, z_ref, o_ref):
        gy = _bf(_gelu(y_ref[...]))
        gy_ref[...] = gy
        zs = [_dot(gy, w_ref[s]) for s in range(4)]
        for s in range(4):
            z_ref[:, s * cw:(s + 1) * cw] = _bf(zs[s])
        o_ref[:, :cw] = _bf(zs[0] * _sig(zs[2]))
        o_ref[:, cw:] = _bf(zs[1] * _sig(zs[3]))

    sd = jax.ShapeDtypeStruct
    return _pcall(
        body, name="glu_fwd", grid=(r // tm,),
        in_specs=[pl.BlockSpec((tm, w), lambda i: (i, 0)), _resident(wglu_g.shape)],
        out_specs=[pl.BlockSpec((tm, w), lambda i: (i, 0)), pl.BlockSpec((tm, 4 * cw), lambda i: (i, 0)),
                   pl.BlockSpec((tm, 2 * cw), lambda i: (i, 0))],
        out_shape=[sd((r, w), BF16), sd((r, 4 * cw), BF16), sd((r, 2 * cw), BF16)],
        compiler_params=_cp(("parallel",), 40),
    )(y_s5, wglu_g)


def _glu_bwd(dyg, z, y_s5, wglu_g, lp):
    r, w = y_s5.shape
    tm = _row_tile(lp, 416)
    cw = wglu_g.shape[2]

    def body(d_ref, z_ref, y_ref, w_ref, dz_ref, dy_ref):
        d = d_ref[...].astype(F32)
        zz = z_ref[...].astype(F32)
        acc = jnp.zeros((tm, w), F32)
        for s in range(2):
            z1 = zz[:, s * cw:(s + 1) * cw]
            sg = _sig(zz[:, (2 + s) * cw:(3 + s) * cw])
            dd = d[:, s * cw:(s + 1) * cw]
            dz1 = _bf(dd * sg)
            dz2 = _bf(dd * z1 * sg * (1.0 - sg))
            dz_ref[:, s * cw:(s + 1) * cw] = dz1
            dz_ref[:, (2 + s) * cw:(3 + s) * cw] = dz2
            acc += _dot(dz1, w_ref[s], NT) + _dot(dz2, w_ref[2 + s], NT)
        dy_ref[...] = acc * _gelu_grad(y_ref[...])

    sd = jax.ShapeDtypeStruct
    return _pcall(
        body, name="glu_bwd", grid=(r // tm,),
        in_specs=[pl.BlockSpec((tm, 2 * cw), lambda i: (i, 0)), pl.BlockSpec((tm, 4 * cw), lambda i: (i, 0)),
                  pl.BlockSpec((tm, w), lambda i: (i, 0)), _resident(wglu_g.shape)],
        out_specs=[pl.BlockSpec((tm, 4 * cw), lambda i: (i, 0)), pl.BlockSpec((tm, w), lambda i: (i, 0))],
        out_shape=[sd((r, 4 * cw), BF16), sd((r, w), F32)],
        compiler_params=_cp(("parallel",), 40),
    )(dyg, z, y_s5, wglu_g)


def _conv_fwd(p3, cw, cb):
    bsz, lp, _ = p3.shape
    tt = _row_tile(lp, 416)
    nt = lp // tt
    tb = tt // 8
    c = cw.shape[1]
    qb = Q_OFF // c

    hr = HALO_ROWS
    off = hr - (CONV_W - 1)

    def body(x_ref, halo_ref, w_ref, b_ref, pre_ref, act_ref, ext_ref):
        t = pl.program_id(1)
        ext_ref[0:hr, :] = jnp.where(t == 0, 0.0, halo_ref[...].astype(F32))
        ext_ref[hr:, :] = x_ref[...].astype(F32)
        w = w_ref[...]
        acc = b_ref[...] + w[0:1, :] * ext_ref[pl.ds(off, tt), :]
        for j in range(1, CONV_W):
            acc = acc + w[j:j + 1, :] * ext_ref[pl.ds(off + j, tt), :]
        pre_ref[...] = _bf(acc)
        act_ref[...] = _bf(acc * _sig(acc))

    sd = jax.ShapeDtypeStruct
    return _pcall(
        body, name="conv_fwd", grid=(bsz, nt),
        in_specs=[pl.BlockSpec((None, tt, c), lambda b, t: (b, t, qb)),
                  pl.BlockSpec((None, hr, c), lambda b, t: (b, jnp.maximum(t * (tt // hr) - 1, 0), qb)),
                  _const((CONV_W, c)), _const((1, c))],
        out_specs=[pl.BlockSpec((None, tt, c), lambda b, t: (b, t, 0))] * 2,
        out_shape=[sd((bsz, lp, c), BF16)] * 2,
        scratch_shapes=[pltpu.VMEM((tt + hr, c), F32)],
        compiler_params=_cp(("parallel", "parallel")),
    )(p3, p3, cw, cb)


def _conv_bwd(dp3, p3, dact3, pre3, cw):
    bsz, lp, _ = p3.shape
    tt = _row_tile(lp, 416)
    nt = lp // tt
    tb = tt // 8
    c = cw.shape[1]
    qb = Q_OFF // c

    hr = HALO_ROWS
    off = hr - (CONV_W - 1)

    def silu_grad(x):
        s = _sig(x)
        return s * (1.0 + x * (1.0 - s))

    def body(dp_any, x_ref, xh_ref, d_ref, dh_ref, pre_ref, preh_ref, w_ref, o_ref, dw_ref, db_ref, ext_ref, dext_ref):
        b = pl.program_id(0)
        t = pl.program_id(1)

        @pl.when((b == 0) & (t == 0))
        def _():
            dw_ref[...] = jnp.zeros_like(dw_ref)
            db_ref[...] = jnp.zeros_like(db_ref)

        dc = d_ref[...].astype(F32) * silu_grad(pre_ref[...].astype(F32))
        dch = jnp.where(t == nt - 1, 0.0, dh_ref[...].astype(F32) * silu_grad(preh_ref[...].astype(F32)))
        dext_ref[0:tt, :] = dc
        dext_ref[tt:, :] = dch
        ext_ref[0:hr, :] = jnp.where(t == 0, 0.0, xh_ref[...].astype(F32))
        ext_ref[hr:, :] = x_ref[...].astype(F32)
        w = w_ref[...]
        acc = w[CONV_W - 1:CONV_W, :] * dc
        for j in range(CONV_W - 1):
            acc = acc + w[j:j + 1, :] * dext_ref[pl.ds(CONV_W - 1 - j, tt), :]
        trow = t * tt + lax.broadcasted_iota(jnp.int32, (tt, 1), 0)
        o_ref[...] = jnp.where(trow >= PAD, acc, 0.0).astype(o_ref.dtype)
        db_ref[...] += _colsum(dc)
        for j in range(CONV_W):
            dw_ref[j:j + 1, :] += _colsum(dc * ext_ref[pl.ds(off + j, tt), :])

    sd = jax.ShapeDtypeStruct
    nxt = lambda t: jnp.minimum((t + 1) * (tt // hr), lp // hr - 1)
    return _pcall(
        body, name="conv_bwd", grid=(bsz, nt),
        in_specs=[pl.BlockSpec(memory_space=pl.ANY),
                  pl.BlockSpec((None, tt, c), lambda b, t: (b, t, qb)),
                  pl.BlockSpec((None, hr, c), lambda b, t: (b, jnp.maximum(t * (tt // hr) - 1, 0), qb)),
                  pl.BlockSpec((None, tt, c), lambda b, t: (b, t, 0)),
                  pl.BlockSpec((None, hr, c), lambda b, t: (b, nxt(t), 0)),
                  pl.BlockSpec((None, tt, c), lambda b, t: (b, t, 0)),
                  pl.BlockSpec((None, hr, c), lambda b, t: (b, nxt(t), 0)),
                  _const((CONV_W, c))],
        out_specs=[pl.BlockSpec((None, tt, c), lambda b, t: (b, t, qb)), _const((CONV_W, c)), _const((1, c))],
        out_shape=[sd(dp3.shape, dp3.dtype), sd((CONV_W, c), F32), sd((1, c), F32)],
        scratch_shapes=[pltpu.VMEM((tt + hr, c), F32), pltpu.VMEM((tt + hr, c), F32)],
        input_output_aliases={0: 0},
        compiler_params=_cp(("arbitrary", "arbitrary")),
    )(dp3, p3, p3, dact3, dact3, pre3, pre3, cw)


def _mlstm_gates(g, h_idx, c_idx, lc):
    lane = lax.broadcasted_iota(jnp.int32, g.shape, 1)
    i_col = jnp.sum(jnp.where(lane == h_idx, g, 0.0), axis=1, keepdims=True)
    f_col = jnp.sum(jnp.where(lane == M_HEADS + h_idx, g, 0.0), axis=1, keepdims=True)
    row = lax.broadcasted_iota(jnp.int32, (lc, 1), 0)
    valid = (c_idx * lc + row) >= PAD
    li = jnp.where(valid, i_col, NEG)
    lf = jnp.where(valid, jnp.minimum(f_col, 0.0) - jnp.log(1.0 + jnp.exp(-jnp.abs(f_col))), 0.0)
    r2 = lax.broadcasted_iota(jnp.int32, (lc, lc), 0)
    c2 = lax.broadcasted_iota(jnp.int32, (lc, lc), 1)
    eye = r2 == c2
    tril = r2 >= c2
    to_row = lambda col: jnp.sum(jnp.where(eye, col, 0.0), axis=0, keepdims=True)
    lf_row = to_row(lf)
    b_col = jnp.sum(jnp.where(tril, lf_row, 0.0), axis=1, keepdims=True)
    b_row = to_row(b_col)
    li_row = to_row(li)
    d_mat = jnp.where(tril, b_col - b_row + li_row, NEG)
    return dict(f_col=f_col, valid=valid, li=li, b_col=b_col, d_mat=d_mat, eye=eye, r2=r2, c2=c2, row=row,
                to_row=to_row)


def _mlstm_chunk(q, ks, v, gq, c_st, n_st, m_st, lc):
    b_col, d_mat = gq["b_col"], gq["d_mat"]
    m_inter = b_col + m_st
    m_row = jnp.maximum(m_inter, jnp.max(d_mat, axis=1, keepdims=True))
    w_intra = jnp.exp(d_mat - m_row)
    w_inter = jnp.exp(m_inter - m_row)
    qb, kb, vb, cb = _bf(q), _bf(ks), _bf(v), _bf(c_st)
    s = _dot(qb, kb, NT) * w_intra
    qc = _dot(qb, cb)
    num = _dot(_bf(s), vb) + w_inter * qc
    qn = jnp.sum(q * n_st, axis=1, keepdims=True)
    den = jnp.sum(s, axis=1, keepdims=True) + w_inter * qn
    e = jnp.exp(-m_row)
    nn = jnp.maximum(jnp.abs(den), e)
    b_last = b_col[lc - 1:lc, :]
    g_log = b_last - b_col + gq["li"]
    m_new = jnp.maximum(b_last + m_st, jnp.max(g_log, axis=0, keepdims=True))
    w_k = jnp.exp(g_log - m_new)
    decay = jnp.exp(b_last + m_st - m_new)
    return dict(w_intra=w_intra, w_inter=w_inter, qb=qb, kb=kb, vb=vb, cb=cb, s=s, qc=qc, num=num, qn=qn, den=den,
                e=e, nn=nn, m_new=m_new, w_k=w_k, decay=decay)


def _chunks_per_step(nc):
    return max(c for c in (3, 2, 1) if nc % c == 0)


def _mlstm_fwd(qk3, p3, pg3):
    bsz, lp, _ = p3.shape
    lc = M_CHUNK
    nc = lp // lc
    dk, dv = 128, 256
    scale = dk ** -0.5

    cps = _chunks_per_step(nc)
    rows = cps * lc

    def body(q_ref, k_ref, v_ref, g_ref, h_ref, cs_ref, ns_ref, ms_ref, c_sc, n_sc, m_sc):
        st = pl.program_id(1)

        @pl.when(st == 0)
        def _():
            c_sc[...] = jnp.zeros_like(c_sc)
            n_sc[...] = jnp.zeros_like(n_sc)
            m_sc[...] = jnp.zeros_like(m_sc)

        for j in range(cps):
            rs = slice(j * lc, (j + 1) * lc)
            g = g_ref[rs, :]
            for hh in range(M_HEADS):
                c_st, n_st, m_all = c_sc[hh], n_sc[hh], m_sc[hh]
                cs_ref[hh, j] = c_st
                ns_ref[hh, j] = n_st
                ms_ref[hh, j] = m_all
                m_st = m_all[:, 0:1]
                q = q_ref[rs, hh * dk:(hh + 1) * dk].astype(F32)
                ks = k_ref[rs, hh * dk:(hh + 1) * dk].astype(F32) * scale
                v = v_ref[rs, hh * dv:(hh + 1) * dv]
                gq = _mlstm_gates(g, hh, st * cps + j, lc)
                f = _mlstm_chunk(q, ks, v, gq, c_st, n_st, m_st, lc)
                h_ref[rs, hh * dv:(hh + 1) * dv] = _bf(f["num"] / f["nn"])
                kw = ks * f["w_k"]
                c_sc[hh] = f["decay"] * c_st + _dot(_bf(kw), f["vb"], TN)
                n_sc[hh] = f["decay"] * n_st + _colsum(kw)
                m_sc[hh] = jnp.broadcast_to(f["m_new"], (1, 128))

    sd = jax.ShapeDtypeStruct
    nh = M_HEADS
    return _pcall(
        body, name="mlstm_fwd", grid=(bsz, nc // cps),
        in_specs=[pl.BlockSpec((None, rows, nh * dk), lambda b, c: (b, c, 0)),
                  pl.BlockSpec((None, rows, nh * dk), lambda b, c: (b, c, 1)),
                  pl.BlockSpec((None, rows, nh * dv), lambda b, c: (b, c, V_OFF // (nh * dv))),
                  pl.BlockSpec((None, rows, 128), lambda b, c: (b, c, 0))],
        out_specs=[pl.BlockSpec((None, rows, nh * dv), lambda b, c: (b, c, 0)),
                   pl.BlockSpec((None, nh, cps, dk, dv), lambda b, c: (b, 0, c, 0, 0)),
                   pl.BlockSpec((None, nh, cps, 1, dk), lambda b, c: (b, 0, c, 0, 0)),
                   pl.BlockSpec((None, nh, cps, 1, 128), lambda b, c: (b, 0, c, 0, 0))],
        out_shape=[sd((bsz, lp, nh * dv), BF16), sd((bsz, nh, nc, dk, dv), F32),
                   sd((bsz, nh, nc, 1, dk), F32), sd((bsz, nh, nc, 1, 128), F32)],
        scratch_shapes=[pltpu.VMEM((nh, dk, dv), F32), pltpu.VMEM((nh, 1, dk), F32), pltpu.VMEM((nh, 1, 128), F32)],
        compiler_params=_cp(("parallel", "arbitrary")),
    )(qk3, qk3, p3, pg3)


def _mlstm_bwd(dp3, qk3, p3, pg3, dh3, cs, ns, ms):
    bsz, lp, _ = p3.shape
    lc = M_CHUNK
    nc = lp // lc
    dk, dv = 128, 256
    scale = dk ** -0.5

    cps = _chunks_per_step(nc)
    nst = nc // cps
    rows = cps * lc

    def body(dp_any, q_ref, k_ref, v_ref, g_ref, dh_ref, cs_ref, ns_ref, ms_ref,
             dv_ref, dqk_ref, dg_ref, dc_sc, dn_sc):
        t = pl.program_id(1)
        st = nst - 1 - t

        @pl.when(t == 0)
        def _():
            dc_sc[...] = jnp.zeros_like(dc_sc)
            dn_sc[...] = jnp.zeros_like(dn_sc)

        lane = lax.broadcasted_iota(jnp.int32, (lc, 128), 1)
        for j in reversed(range(cps)):
            rs = slice(j * lc, (j + 1) * lc)
            g = g_ref[rs, :]
            dgate = jnp.zeros((lc, 128), F32)
            for hh in range(M_HEADS):
                dgate = head(hh, j, rs, st * cps + j, g, lane, dgate, q_ref, k_ref, v_ref, dh_ref, cs_ref, ns_ref,
                             ms_ref, dv_ref, dqk_ref, dc_sc, dn_sc)
            dg_ref[rs, :] = dgate.astype(dg_ref.dtype)

    def head(hh, j, sl, c, g, lane, dgate, q_ref, k_ref, v_ref, dh_ref, cs_ref, ns_ref, ms_ref, dv_ref, dqk_ref,
             dc_sc, dn_sc):
        c_st, n_st = cs_ref[hh, j], ns_ref[hh, j]
        m_st = ms_ref[hh, j][:, 0:1]
        q = q_ref[sl, hh * dk:(hh + 1) * dk].astype(F32)
        ks = k_ref[sl, hh * dk:(hh + 1) * dk].astype(F32) * scale
        v = v_ref[sl, hh * dv:(hh + 1) * dv]
        dh = dh_ref[sl, hh * dv:(hh + 1) * dv].astype(F32)
        gq = _mlstm_gates(g, hh, c, lc)
        f = _mlstm_chunk(q, ks, v, gq, c_st, n_st, m_st, lc)
        eye, r2, c2, row, valid = gq["eye"], gq["r2"], gq["c2"], gq["row"], gq["valid"]
        w_intra, w_inter, s, nn, den = f["w_intra"], f["w_inter"], f["s"], f["nn"], f["den"]
        qb, kb, vb, cb, w_k, decay = f["qb"], f["kb"], f["vb"], f["cb"], f["w_k"], f["decay"]
        d_c, d_n = dc_sc[hh], dn_sc[hh]
        d_cb = _bf(d_c)

        hout = f["num"] / nn
        dnum = dh / nn
        d_nn = -jnp.sum(dh * hout, axis=1, keepdims=True) / nn
        dden = jnp.where(jnp.abs(den) > f["e"], d_nn * jnp.sign(den), 0.0)
        wdnum = w_inter * dnum
        wdden = w_inter * dden
        ds = _dot(_bf(dnum), vb, NT) + dden
        dsw = _bf(ds * w_intra)
        dq = _dot(dsw, kb) + _dot(_bf(wdnum), cb, NT) + wdden * n_st
        dkw = _dot(vb, d_cb, NT) + d_n
        dks = _dot(dsw, qb, TN) + dkw * w_k
        kw = ks * w_k
        dvv = _dot(_bf(s), _bf(dnum), TN) + _dot(_bf(kw), d_cb)
        dd = ds * s
        rs = jnp.sum(dd, axis=1, keepdims=True)
        cs_col = jnp.sum(jnp.where(eye, jnp.sum(dd, axis=0, keepdims=True), 0.0), axis=1, keepdims=True)
        dwi = jnp.sum(dnum * f["qc"], axis=1, keepdims=True) + dden * f["qn"]
        db = rs - cs_col + dwi * w_inter
        dli = cs_col
        ddecay = jnp.sum(jnp.sum(d_c * c_st, axis=1, keepdims=True), axis=0, keepdims=True) \
            + jnp.sum(d_n * n_st, axis=1, keepdims=True)
        dgl = jnp.sum(dkw * ks, axis=1, keepdims=True) * w_k
        dblast = ddecay * decay + jnp.sum(dgl, axis=0, keepdims=True)
        db = db - dgl + jnp.where(row == lc - 1, dblast, 0.0)
        dli = dli + dgl
        db_row = gq["to_row"](db)
        dlf = jnp.sum(jnp.where(c2 >= r2, db_row, 0.0), axis=1, keepdims=True)
        dlf = jnp.where(valid, dlf, 0.0)
        dgate = jnp.where(lane == hh, jnp.where(valid, dli, 0.0), dgate)
        dgate = jnp.where(lane == M_HEADS + hh, dlf / (1.0 + jnp.exp(gq["f_col"])), dgate)
        dqk_ref[sl, hh * dk:(hh + 1) * dk] = _bf(dq)
        dqk_ref[sl, (M_HEADS + hh) * dk:(M_HEADS + hh + 1) * dk] = _bf(dks * scale)
        dv_ref[sl, hh * dv:(hh + 1) * dv] = dvv.astype(dv_ref.dtype)
        dc_sc[hh] = decay * d_c + _dot(qb, _bf(wdnum), TN)
        dn_sc[hh] = decay * d_n + _colsum(q * wdden)
        return dgate

    sd = jax.ShapeDtypeStruct
    nh = M_HEADS
    rc = lambda c: nst - 1 - c
    return _pcall(
        body, name="mlstm_bwd", grid=(bsz, nst),
        in_specs=[pl.BlockSpec(memory_space=pl.ANY),
                  pl.BlockSpec((None, rows, nh * dk), lambda b, c: (b, rc(c), 0)),
                  pl.BlockSpec((None, rows, nh * dk), lambda b, c: (b, rc(c), 1)),
                  pl.BlockSpec((None, rows, nh * dv), lambda b, c: (b, rc(c), V_OFF // (nh * dv))),
                  pl.BlockSpec((None, rows, 128), lambda b, c: (b, rc(c), 0)),
                  pl.BlockSpec((None, rows, nh * dv), lambda b, c: (b, rc(c), 0)),
                  pl.BlockSpec((None, nh, cps, dk, dv), lambda b, c: (b, 0, rc(c), 0, 0)),
                  pl.BlockSpec((None, nh, cps, 1, dk), lambda b, c: (b, 0, rc(c), 0, 0)),
                  pl.BlockSpec((None, nh, cps, 1, 128), lambda b, c: (b, 0, rc(c), 0, 0))],
        out_specs=[pl.BlockSpec((None, rows, nh * dv), lambda b, c: (b, rc(c), V_OFF // (nh * dv))),
                   pl.BlockSpec((None, rows, 2 * nh * dk), lambda b, c: (b, rc(c), 0)),
                   pl.BlockSpec((None, rows, 128), lambda b, c: (b, rc(c), 0))],
        out_shape=[sd(dp3.shape, dp3.dtype), sd((bsz, lp, 2 * nh * dk), BF16), sd((bsz, lp, 128), dp3.dtype)],
        scratch_shapes=[pltpu.VMEM((nh, dk, dv), F32), pltpu.VMEM((nh, 1, dk), F32)],
        input_output_aliases={0: 0},
        compiler_params=_cp(("arbitrary", "arbitrary")),
    )(dp3, qk3, qk3, p3, pg3, dh3, cs, ns, ms)


def _headnorm(x):
    dv = x.shape[1] // M_HEADS
    xh, rs = [], []
    for h in range(M_HEADS):
        xx = x[:, h * dv:(h + 1) * dv]
        mu = jnp.mean(xx, axis=-1, keepdims=True)
        xc = xx - mu
        rstd = lax.rsqrt(jnp.mean(xc * xc, axis=-1, keepdims=True) + LN_EPS)
        xh.append(xc * rstd)
        rs.append(rstd)
    return jnp.concatenate(xh, axis=1), rs


def _mix_fwd(hm, p, ys5g, h0, gn, wmo_bf, wo_bf, lp):
    r, d = hm.shape
    tm = _row_tile(lp, 384)

    def body(hm_ref, o_ref, gs_ref, gm_ref, ys_ref, h0_ref, gn_ref, wmo_ref, wo_ref,
             ymin_ref, mix_ref, r1_ref):
        xhat, _ = _headnorm(hm_ref[...].astype(F32))
        ymin = _bf(_sig(o_ref[...].astype(F32)) * (xhat * gn_ref[...]))
        ymin_ref[...] = ymin
        ym = _dot(ymin, wmo_ref[...])
        mix = _bf(_sig(gs_ref[...].astype(F32)) * ys_ref[...].astype(F32) + _sig(gm_ref[...].astype(F32)) * ym)
        mix_ref[...] = mix
        r1_ref[...] = ALPHA * h0_ref[...] + _dot(mix, wo_ref[...])

    sd = jax.ShapeDtypeStruct
    row = pl.BlockSpec((tm, d), lambda i: (i, 0))
    return _pcall(
        body, name="mix_fwd", grid=(r // tm,),
        in_specs=[row, pl.BlockSpec((tm, d), lambda i: (i, O_OFF // d)), pl.BlockSpec((tm, d), lambda i: (i, GS_OFF // d)),
                  pl.BlockSpec((tm, d), lambda i: (i, GM_OFF // d)), row, row, _const((1, d)),
                  _resident((d, d)), _resident((d, d))],
        out_specs=[row] * 3,
        out_shape=[sd((r, d), BF16), sd((r, d), BF16), sd((r, d), F32)],
        compiler_params=_cp(("parallel",), 48),
    )(hm, p, p, p, ys5g, h0, gn, wmo_bf, wo_bf)


def _mix_bwd(dr1, wo_bf, wmo_bf, p, ys5g, ymin, hm, gn, lp):
    r, d = hm.shape
    tm = _row_tile(lp, 384)
    dv = d // M_HEADS

    def body(dr1_ref, wo_ref, wmo_ref, o_ref, gs_ref, gm_ref, ys_ref, ym_ref, hm_ref, gn_ref,
             dp_ref, dys_ref, dym_ref, dhm_ref, dgn_ref):
        i = pl.program_id(0)

        @pl.when(i == 0)
        def _():
            dgn_ref[...] = jnp.zeros_like(dgn_ref)

        dmix = _dot(_bf(dr1_ref[...]), wo_ref[...], NT)
        sgs, sgm, so = (_sig(gs_ref[...].astype(F32)), _sig(gm_ref[...].astype(F32)), _sig(o_ref[...].astype(F32)))
        dys_ref[...] = _bf(dmix * sgs)
        dp_ref[:, d:2 * d] = _bf(dmix * ys_ref[...].astype(F32) * sgs * (1.0 - sgs))
        dym = dmix * sgm
        dym_ref[...] = _bf(dym)
        ym = _dot(ym_ref[...], wmo_ref[...])
        dp_ref[:, 2 * d:3 * d] = _bf(dmix * ym * sgm * (1.0 - sgm))
        dymin = _dot(_bf(dym), wmo_ref[...], NT)
        xhat, rs = _headnorm(hm_ref[...].astype(F32))
        gn_ = gn_ref[...]
        dp_ref[:, 0:d] = _bf(dymin * (xhat * gn_) * so * (1.0 - so))
        dhn = dymin * so
        dgn_ref[...] += _colsum(dhn * xhat)
        dxh = dhn * gn_
        for h in range(M_HEADS):
            sl = slice(h * dv, (h + 1) * dv)
            a, xh = dxh[:, sl], xhat[:, sl]
            m1 = jnp.mean(a, axis=-1, keepdims=True)
            m2 = jnp.mean(a * xh, axis=-1, keepdims=True)
            dhm_ref[:, sl] = _bf(rs[h] * (a - m1 - xh * m2))

    sd = jax.ShapeDtypeStruct
    row = pl.BlockSpec((tm, d), lambda i: (i, 0))
    vec = _const((1, d))
    return _pcall(
        body, name="mix_bwd", grid=(r // tm,),
        in_specs=[row, _resident((d, d)), _resident((d, d)),
                  pl.BlockSpec((tm, d), lambda i: (i, O_OFF // d)), pl.BlockSpec((tm, d), lambda i: (i, GS_OFF // d)),
                  pl.BlockSpec((tm, d), lambda i: (i, GM_OFF // d)), row, row, row, vec],
        out_specs=[pl.BlockSpec((tm, 3 * d), lambda i: (i, 0)), row, row, row, vec],
        out_shape=[sd((r, NP), BF16), sd((r, d), BF16), sd((r, d), BF16), sd((r, d), BF16), sd((1, d), F32)],
        compiler_params=_cp(("arbitrary",), 56),
    )(dr1, wo_bf, wmo_bf, p, p, p, ys5g, ymin, hm, gn)


def _mlp_fwd(r1, tgt, g1, b1, wup_g, wdn_bf, bup, g2, b2, lp):
    r, d = r1.shape
    tm = _row_tile(lp, 352)
    tps = lp // tm
    nf = wup_g.shape[0]

    def body(r1_ref, t_ref, g1_ref, b1_ref, wup_ref, wdn_ref, bup_ref, g2_ref, b2_ref,
             dr2_ref, h1b_ref, act_ref, loss_ref, dg2_ref, db2_ref):
        i = pl.program_id(0)

        @pl.when(i == 0)
        def _():
            loss_ref[...] = jnp.zeros_like(loss_ref)
            dg2_ref[...] = jnp.zeros_like(dg2_ref)
            db2_ref[...] = jnp.zeros_like(db2_ref)

        h1, _, _ = _ln_fwd(r1_ref[...], g1_ref[...], b1_ref[...])
        h1b = _bf(h1)
        h1b_ref[...] = h1b
        ff = jnp.zeros((tm, d), F32)
        for s in range(nf):
            up = _dot(h1b, wup_ref[s]) + bup_ref[:, s * d:(s + 1) * d]
            a = jnp.maximum(up, 0.0)
            a = _bf(a * a)
            act_ref[:, s * d:(s + 1) * d] = a
            ff = ff + _dot(a, wdn_ref[s * d:(s + 1) * d, :])
        r2 = ALPHA * h1 + ff
        g2 = g2_ref[...]
        y, xhat, rstd = _ln_fwd(r2, g2, b2_ref[...])
        t = (i % tps) * tm + lax.broadcasted_iota(jnp.int32, (tm, 1), 0)
        diff = jnp.where(t >= PAD + N_META, y - t_ref[...], 0.0)
        loss_ref[...] += 0.5 / d * jnp.sum(jnp.sum(diff * diff, axis=1, keepdims=True), axis=0, keepdims=True)
        dy = diff * (1.0 / d)
        dg2_ref[...] += _colsum(dy * xhat)
        db2_ref[...] += _colsum(dy)
        dr2_ref[...] = _ln_bwd(dy, xhat, rstd, g2)

    sd = jax.ShapeDtypeStruct
    row = pl.BlockSpec((tm, d), lambda i: (i, 0))
    vec = _const((1, d))
    return _pcall(
        body, name="mlp_fwd", grid=(r // tm,),
        in_specs=[row, row, vec, vec, _resident(wup_g.shape), _resident(wdn_bf.shape), _const((1, nf * d)), vec, vec],
        out_specs=[row, row, pl.BlockSpec((tm, nf * d), lambda i: (i, 0)), _const((1, 128)), vec, vec],
        out_shape=[sd((r, d), F32), sd((r, d), BF16), sd((r, nf * d), BF16), sd((1, 128), F32), sd((1, d), F32),
                   sd((1, d), F32)],
        compiler_params=_cp(("arbitrary",), 56),
    )(r1, tgt, g1, b1, wup_g, wdn_bf, bup, g2, b2)


def _mlp_bwd(act, dr2, r1, g1, wup_g, wdn_bf, lp):
    r, d = dr2.shape
    tm = _row_tile(lp, 352)
    nf = wup_g.shape[0]

    def body(act_ref, dr2_ref, r1_ref, g1_ref, wup_ref, wdn_ref, dr1_ref, dup_ref, dbup_ref, dg1_ref, db1_ref):
        i = pl.program_id(0)

        @pl.when(i == 0)
        def _():
            dbup_ref[...] = jnp.zeros_like(dbup_ref)
            dg1_ref[...] = jnp.zeros_like(dg1_ref)
            db1_ref[...] = jnp.zeros_like(db1_ref)

        dr2 = dr2_ref[...]
        dr2b = _bf(dr2)
        acc = ALPHA * dr2
        for s in range(nf):
            dact = _dot(dr2b, wdn_ref[s * d:(s + 1) * d, :], NT)
            dup = dact * (2.0 * jnp.sqrt(act_ref[:, s * d:(s + 1) * d].astype(F32)))
            dbup_ref[:, s * d:(s + 1) * d] += _colsum(dup)
            dupb = _bf(dup)
            dup_ref[:, s * d:(s + 1) * d] = dupb
            acc = acc + _dot(dupb, wup_ref[s], NT)
        g1 = g1_ref[...]
        _, xhat1, rstd1 = _ln_fwd(r1_ref[...], g1, 0.0)
        dr1_ref[...] = _ln_bwd(acc, xhat1, rstd1, g1)
        dg1_ref[...] += _colsum(acc * xhat1)
        db1_ref[...] += _colsum(acc)

    sd = jax.ShapeDtypeStruct
    row = pl.BlockSpec((tm, d), lambda i: (i, 0))
    vec = _const((1, d))
    return _pcall(
        body, name="mlp_bwd", grid=(r // tm,),
        in_specs=[pl.BlockSpec((tm, nf * d), lambda i: (i, 0)), row, row, vec, _resident(wup_g.shape),
                  _resident(wdn_bf.shape)],
        out_specs=[row, pl.BlockSpec((tm, nf * d), lambda i: (i, 0)), _const((1, nf * d)), vec, vec],
        out_shape=[sd((r, d), F32), sd((r, nf * d), BF16), sd((1, nf * d), F32), sd((1, d), F32), sd((1, d), F32)],
        compiler_params=_cp(("arbitrary",), 56),
    )(act, dr2, r1, g1, wup_g, wdn_bf)


def _s5_block_mats(bb_re_t, bb_im_t, c_re, c_im, ap_re, ap_im):
    ng = c_re.shape[0]
    gl = ng // S5_KCH
    eye = jnp.eye(gl, dtype=F32)

    def bmat(bt):
        bb = jnp.transpose(bt, (1, 0, 2)).reshape(S5_KCH, gl, S5_GROUP, S5_STATE)
        return jnp.einsum("kghp,gj->kghjp", bb, eye).reshape(S5_KCH, gl * S5_GROUP, gl * S5_STATE)

    def cmat(c):
        cc = c.reshape(S5_KCH, gl, S5_GROUP, S5_STATE)
        return jnp.einsum("kghp,gj->kjpgh", cc, eye).reshape(S5_KCH, gl * S5_STATE, gl * S5_GROUP)

    def pw(a):
        return jnp.transpose(a.reshape(8, S5_KCH, gl * S5_STATE), (1, 0, 2))

    bk = jnp.concatenate([bmat(bb_re_t), bmat(bb_im_t)], axis=-1)
    apow = jnp.concatenate([pw(ap_re), pw(ap_im)], axis=-1)
    return _bf(bk), _bf(cmat(c_re)), _bf(cmat(c_im)), apow


def _s5_block_grads(dbk, dcre, dcim, da):
    gl = dbk.shape[1] // S5_GROUP
    ng = gl * S5_KCH
    eye = jnp.eye(gl, dtype=F32)
    hw = gl * S5_STATE

    def bpart(x):
        x = x.reshape(S5_KCH, gl, S5_GROUP, gl, S5_STATE)
        x = jnp.einsum("kghjp,gj->kghp", x, eye).reshape(ng, S5_GROUP, S5_STATE)
        return jnp.transpose(x, (1, 0, 2))

    def cpart(x):
        x = x.reshape(S5_KCH, gl, S5_STATE, gl, S5_GROUP)
        return jnp.einsum("kjpgh,gj->kghp", x, eye).reshape(ng, S5_GROUP, S5_STATE)

    return (bpart(dbk[..., :hw]), bpart(dbk[..., hw:]), cpart(dcre), cpart(dcim),
            da[:, 0, :hw].reshape(ng, S5_STATE), da[:, 0, hw:].reshape(ng, S5_STATE))


def _tie(a, tok):
    return a if tok is None else a + tok[0, 0]


def _local_step(x, tgt, w, early=None, late=None, ready=None):
    ready = ready or (lambda names, g: None)
    bsz, seq, d = x.shape
    lp = PAD + N_META + seq
    r = bsz * lp
    tgtp = jnp.concatenate([jnp.zeros((bsz, PAD + N_META, d), F32), tgt], axis=1).reshape(r, d)

    h0, h0b = _ln0_fwd(x, w["meta_tokens"], w["ln0_g"], w["ln0_b"])
    b_re_t = jnp.transpose(w["s5_b_re"], (2, 0, 1))
    b_im_t = jnp.transpose(w["s5_b_im"], (2, 0, 1))
    ap_re, ap_im, bb_re_t, bb_im_t = _s5_prep(w["s5_lambda_re"], w["s5_lambda_im"], w["s5_log_dt"], b_re_t, b_im_t)
    bk, cre, cim, apow = _s5_block_mats(bb_re_t, bb_im_t, w["s5_c_re"], w["s5_c_im"], ap_re, ap_im)
    apow_rev = jnp.flip(apow, axis=1)
    if early is not None:
        w = {**w, **early((h0, tgtp, bk, cre, cim, apow_rev))}
    p, pg = _inproj(h0b, w["w_in"], w["b_in"], lp)
    p3 = p.reshape(bsz, lp, NP)
    pg3 = pg.reshape(bsz, lp, 128)

    y_s5, xs = _s5_fwd(p3, bk, cre, cim, apow, w["s5_d"])
    sw = y_s5.shape[-1]
    if late is not None:
        w = {**w, **late(y_s5)}
    gy, z, ys5g = _glu_fwd(y_s5.reshape(r, sw), w["s5_w_glu"], lp)

    pre3, qk3 = _conv_fwd(p3, w["qk_conv_w"], w["qk_conv_b"])
    hm3, cs, ns, ms = _mlstm_fwd(qk3, p3, pg3)
    hm = hm3.reshape(r, d)
    ymin, mix, r1 = _mix_fwd(hm, p, ys5g, h0, w["m_norm_g"], w["m_w_out"], w["w_o"], lp)
    dr2, h1b, act, loss, dg2, db2 = _mlp_fwd(r1, tgtp, w["ln1_g"], w["ln1_b"], w["w_up"], w["w_down"], w["b_up"],
                                             w["ln2_g"], w["ln2_b"], lp)

    g = {"ln2_g": dg2, "ln2_b": db2}
    dr1, dup, g["b_up"], g["ln1_g"], g["ln1_b"] = _mlp_bwd(act, dr2, r1, w["ln1_g"], w["w_up"], w["w_down"], lp)
    g["w_down"] = _mm_tn(act, dr2, name="dw_down")
    g["w_up"] = _mm_tn(h1b, dup, name="dw_up", split=w["w_up"].shape[0])
    tok = ready(("w_down", "w_up"), g)
    dp, dys5g, dym, dhm, g["m_norm_g"] = _mix_bwd(
        dr1, w["w_o"], w["m_w_out"], p, ys5g, ymin, hm, _tie(w["m_norm_g"], tok), lp)
    g["w_o"] = _mm_tn(mix, dr1, name="dw_o")
    g["m_w_out"] = _mm_tn(ymin, dym, name="dw_mout")

    dp3 = dp.reshape(bsz, lp, NP)
    dp3, dqk3, dgate = _mlstm_bwd(dp3, qk3, p3, pg3, dhm.reshape(bsz, lp, d), cs, ns, ms)
    dp3, g["qk_conv_w"], g["qk_conv_b"] = _conv_bwd(dp3, p3, dqk3, pre3, w["qk_conv_w"])
    dz, dys5 = _glu_bwd(dys5g, z, y_s5.reshape(r, sw), w["s5_w_glu"], lp)
    g["s5_w_glu"] = _mm_tn(gy, dz, name="dw_glu", split=w["s5_w_glu"].shape[0])
    tok = ready(("s5_w_glu", "m_w_out", "w_o"), g)
    dp3, dbk, dcre, dcim, da, g["s5_d"] = _s5_bwd(dp3, p3, dys5.reshape(bsz, lp, sw), xs, bk, cre, cim, apow_rev,
                                                 _tie(w["s5_d"], tok))
    dbb_re_t, dbb_im_t, g["s5_c_re"], g["s5_c_im"], da_re, da_im = _s5_block_grads(dbk, dcre, dcim, da)
    g["s5_lambda_re"], g["s5_lambda_im"], g["s5_log_dt"], gb_re_t, gb_im_t = _s5_prep_bwd(
        w["s5_lambda_re"], w["s5_lambda_im"], w["s5_log_dt"], b_re_t, b_im_t, da_re, da_im, dbb_re_t, dbb_im_t)
    g["s5_b_re"] = jnp.transpose(gb_re_t, (1, 2, 0))
    g["s5_b_im"] = jnp.transpose(gb_im_t, (1, 2, 0))

    dp3 = lax.dynamic_update_slice(dp3, dgate, (0, 0, G_OFF))
    dp = dp3.reshape(r, NP)
    g["w_in"], g["b_in"] = _mm_tn(h0b, dp, name="dw_in", colsum=True)
    tok = ready(("w_in",), g)
    dpw = _mm_nt(dp, w["w_in"], lp, name="dh0", dep=tok)
    grad_x, g["ln0_g"], g["ln0_b"], g["meta_tokens"] = _ln0_bwd(x, w["meta_tokens"], dr1, dpw, w["ln0_g"])
    return loss, grad_x, g


_ANY = pl.BlockSpec(memory_space=pl.ANY)
_MESH = pl.DeviceIdType.MESH


def _place():
    return lax.axis_index("x"), lax.axis_index("y"), lax.axis_index("c")


def _gather_chips(shards):
    n = len(shards)

    def body(*refs):
        ins, outs = refs[:n], refs[n:2 * n]
        send, recv, loc = refs[2 * n:]
        x, y, c = _place()
        me = 2 * x + y
        peers = [(1 - x, y), (x, 1 - y), (1 - x, 1 - y)]

        def rc(a, k, slot):
            px, py = peers[k]
            return pltpu.make_async_remote_copy(src_ref=ins[a], dst_ref=outs[a].at[slot], send_sem=send.at[a, k],
                                                recv_sem=recv.at[a, k], device_id=(px, py, c), device_id_type=_MESH)

        own = [pltpu.make_async_copy(ins[a], outs[a].at[me], loc.at[a]) for a in range(n)]
        for cp in own:
            cp.start()
        out = [rc(a, k, me) for a in range(n) for k in range(3)]
        for cp in out:
            cp.start()
        for a in range(n):
            for k in range(3):
                rc(a, k, 2 * peers[k][0] + peers[k][1]).wait_recv()
        for cp in out:
            cp.wait_send()
        for cp in own:
            cp.wait()

    return _pcall(
        body, name="gather_chips", in_specs=[_ANY] * n, out_specs=[_ANY] * n,
        out_shape=[jax.ShapeDtypeStruct((4,) + s.shape, s.dtype) for s in shards],
        scratch_shapes=[pltpu.SemaphoreType.DMA((n, 3)), pltpu.SemaphoreType.DMA((n, 3)), pltpu.SemaphoreType.DMA((n,))],
    )(*shards)


_HBM = pl.BlockSpec(memory_space=pltpu.HBM)
_SEM = pl.BlockSpec(memory_space=pltpu.SEMAPHORE)
_EFFECT = pltpu.SideEffectType.DATAFLOW_SIDE_EFFECTING


def _xchg_copies(srcs, lands, send, recv, scatter):
    x, y, c = _place()
    me = 2 * x + y
    peers = [(1 - x, y), (x, 1 - y), (1 - x, 1 - y)]
    out = []
    for a in range(len(srcs)):
        for k, (px, py) in enumerate(peers):
            src = srcs[a].at[2 * px + py] if scatter else srcs[a]
            dst = lands[a].at[k] if scatter else lands[a].at[me]
            out.append(pltpu.make_async_remote_copy(src_ref=src, dst_ref=dst, send_sem=send.at[3 * a + k],
                                                    recv_sem=recv.at[3 * a + k], device_id=(px, py, c),
                                                    device_id_type=_MESH))
    return out


def _xchg_start(srcs, lands, *, name, scatter, dep=None):
    n = len(srcs)
    deps = [] if dep is None else [dep]
    nd = len(deps)

    def body(*refs):
        send, recv = refs[2 * n + nd], refs[2 * n + nd + 1]
        for cp in _xchg_copies(refs[:n], refs[n:2 * n], send, recv, scatter):
            cp.start()
        refs[-1][...] = jnp.zeros_like(refs[-1])

    hbm = lambda a: pltpu.HBM(a.shape, a.dtype)
    con = lambda a: pltpu.with_memory_space_constraint(a, pltpu.HBM)
    res = _pcall(
        body, name=name, in_specs=[_HBM] * (2 * n) + [_ANY] * nd,
        out_specs=[_SEM, _SEM] + [_HBM] * (2 * n) + [pl.BlockSpec(memory_space=pltpu.VMEM)],
        out_shape=[pltpu.SemaphoreType.DMA((3 * n,)), pltpu.SemaphoreType.DMA((3 * n,))]
        + [hbm(a) for a in srcs] + [hbm(a) for a in lands] + [jax.ShapeDtypeStruct((8, 128), F32)],
        input_output_aliases={i: 2 + i for i in range(2 * n)},
        compiler_params=pltpu.CompilerParams(has_side_effects=_EFFECT),
    )(*[con(a) for a in srcs], *[con(a) for a in lands], *deps)
    return res[0], res[1], list(res[2:2 + n]), list(res[2 + n:2 + 2 * n]), res[-1]


def _xchg_wait(send, recv, srcs, lands, after, *, name, scatter):
    n = len(srcs)
    afters = list(after) if isinstance(after, (list, tuple)) else [after]

    def body(*refs):
        s_ref, r_ref = refs[2 * n], refs[2 * n + 1]
        for cp in _xchg_copies(refs[:n], refs[n:2 * n], s_ref, r_ref, scatter):
            cp.wait_send()
            cp.wait_recv()

    hbm = lambda a: pltpu.HBM(a.shape, a.dtype)
    res = _pcall(
        body, name=name, in_specs=[_HBM] * (2 * n) + [_SEM, _SEM] + [_ANY] * len(afters),
        out_specs=[_HBM] * (2 * n),
        out_shape=[hbm(a) for a in srcs] + [hbm(a) for a in lands],
        input_output_aliases={i: i for i in range(2 * n)},
        compiler_params=pltpu.CompilerParams(has_side_effects=_EFFECT),
    )(*srcs, *lands, send, recv, *afters)
    return list(res[:n]), list(res[n:])


def _swap_cores(arrs, name="swap_cores"):
    n = len(arrs)

    def body(*refs):
        ins, outs = refs[:n], refs[n:2 * n]
        send, recv = refs[2 * n:]
        x, y, c = _place()
        cps = [pltpu.make_async_remote_copy(src_ref=ins[a], dst_ref=outs[a], send_sem=send.at[a], recv_sem=recv.at[a],
                                            device_id=(x, y, 1 - c), device_id_type=_MESH) for a in range(n)]
        for cp in cps:
            cp.start()
        for cp in cps:
            cp.wait_recv()
        for cp in cps:
            cp.wait_send()

    return _pcall(
        body, name=name, in_specs=[_ANY] * n, out_specs=[_ANY] * n,
        out_shape=[jax.ShapeDtypeStruct(s.shape, s.dtype) for s in arrs],
        scratch_shapes=[pltpu.SemaphoreType.DMA((n,)), pltpu.SemaphoreType.DMA((n,))],
    )(*arrs)


def _allreduce_small(v, dep=None):
    rows = v.shape[0]
    half = rows // 2
    assert half % 8 == 0 and 2 * half == rows
    deps = [] if dep is None else [dep]

    def body(v_ref, *rest):
        out_ref, sib_ref, pair_ref, slots_ref, send, recv = rest[len(deps):]
        x, y, c = _place()
        chip = 2 * x + y
        sibling = (x, y, 1 - c)
        peers = [(1 - x, y), (x, 1 - y), (1 - x, 1 - y)]
        mine = pl.ds(pl.multiple_of(c * half, 8), half)

        first = pltpu.make_async_remote_copy(src_ref=v_ref, dst_ref=sib_ref, send_sem=send.at[0], recv_sem=recv.at[0],
                                             device_id=sibling, device_id_type=_MESH)
        first.start()
        first.wait_recv()
        pair_ref[...] = v_ref[...] + sib_ref[...]
        slots_ref[chip] = pair_ref[mine, :]
        cross = [pltpu.make_async_remote_copy(src_ref=pair_ref.at[mine], dst_ref=slots_ref.at[chip],
                                              send_sem=send.at[1 + k], recv_sem=recv.at[1 + k],
                                              device_id=(px, py, c), device_id_type=_MESH)
                 for k, (px, py) in enumerate(peers)]
        for cp in cross:
            cp.start()
        for cp in cross:
            cp.wait_recv()
        out_ref[mine, :] = ((slots_ref[0] + slots_ref[1]) + slots_ref[2]) + slots_ref[3]
        last = pltpu.make_async_remote_copy(src_ref=out_ref.at[mine], dst_ref=out_ref.at[mine], send_sem=send.at[4],
                                            recv_sem=recv.at[4], device_id=sibling, device_id_type=_MESH)
        last.start()
        last.wait_recv()
        first.wait_send()
        for cp in cross:
            cp.wait_send()
        last.wait_send()

    vm = pl.BlockSpec(memory_space=pltpu.VMEM)
    return _pcall(
        body, name="allreduce_small", in_specs=[vm] + [_ANY] * len(deps), out_specs=vm,
        out_shape=jax.ShapeDtypeStruct((rows, 128), F32),
        scratch_shapes=[pltpu.VMEM((rows, 128), F32), pltpu.VMEM((rows, 128), F32), pltpu.VMEM((4, half, 128), F32),
                        pltpu.SemaphoreType.DMA((5,)), pltpu.SemaphoreType.DMA((5,))],
        compiler_params=_cp(None, 40),
    )(v, *deps)


def _sum_slots(parts, land, chip):
    ns, rows, cols = land.shape
    tm = _row_tile(rows, 256, 8)

    def body(chip_ref, own_ref, a_ref, o_ref):
        o_ref[...] = ((own_ref[...] + a_ref[0]) + a_ref[1]) + a_ref[2]

    return _pcall(
        body, name="sum_slots",
        grid_spec=pltpu.PrefetchScalarGridSpec(
            num_scalar_prefetch=1, grid=(rows // tm,),
            in_specs=[pl.BlockSpec((None, tm, cols), lambda i, c: (c[0], i, 0)),
                      pl.BlockSpec((ns, tm, cols), lambda i, c: (0, i, 0))],
            out_specs=pl.BlockSpec((tm, cols), lambda i, c: (i, 0))),
        out_shape=jax.ShapeDtypeStruct((rows, cols), F32),
        compiler_params=_cp(("parallel",), 40),
    )(jnp.reshape(chip, (1,)).astype(jnp.int32), parts, land)


def _adamw(w, m, v, g0, g1=None):
    rows, cols = w.shape[-2:]
    lead = w.ndim == 3
    tm = _row_tile(rows, max(8, (1 << 20) // (4 * cols)), 8)
    c1 = 1.0 - ADAM_B1 ** ADAM_STEP
    c2 = 1.0 - ADAM_B2 ** ADAM_STEP
    two = g1 is not None

    def body(*refs):
        w_ref, m_ref, v_ref, g0_ref = refs[:4]
        g_ref, d_ref, nm_ref, nv_ref = refs[-4:]
        g = g0_ref[...]
        if two:
            g = g + refs[4][...]
        nm = ADAM_B1 * m_ref[...] + (1.0 - ADAM_B1) * g
        nv = ADAM_B2 * v_ref[...] + (1.0 - ADAM_B2) * (g * g)
        g_ref[...] = g
        nm_ref[...] = nm
        nv_ref[...] = nv
        d_ref[...] = -ADAM_LR * ((nm / c1) / (jnp.sqrt(nv / c2) + ADAM_EPS) + ADAM_WD * w_ref[...])

    blk = pl.BlockSpec((tm, cols), lambda i: (i, 0))
    wblk = pl.BlockSpec((None, tm, cols), lambda i: (0, i, 0)) if lead else blk
    ins = [w, m, v, g0] + ([g1] if two else [])
    return _pcall(
        body, name="adamw", grid=(rows // tm,), in_specs=[wblk] * 3 + [blk] * (len(ins) - 3), out_specs=[wblk] * 4,
        out_shape=[jax.ShapeDtypeStruct(w.shape, F32)] * 4,
        compiler_params=_cp(("parallel",), 40),
    )(*ins)


_BIG = ("w_in", "s5_w_glu", "m_w_out", "w_o", "w_up", "w_down")
_SMALL = ("ln0_g", "ln0_b", "b_in", "qk_conv_b", "s5_lambda_re", "s5_lambda_im", "s5_log_dt", "s5_b_re", "s5_b_im",
          "s5_c_re", "s5_c_im", "s5_d", "m_norm_g", "ln1_g", "ln1_b", "b_up", "ln2_g", "ln2_b")
_SMALL_SHARDED = ("meta_tokens", "qk_conv_w")
_ORDER = ("meta_tokens", "ln0_g", "ln0_b", "w_in", "b_in", "qk_conv_w", "qk_conv_b", "s5_lambda_re", "s5_lambda_im",
          "s5_log_dt", "s5_b_re", "s5_b_im", "s5_c_re", "s5_c_im", "s5_d", "s5_w_glu", "m_norm_g", "m_w_out", "w_o",
          "ln1_g", "ln1_b", "w_up", "b_up", "w_down", "ln2_g", "ln2_b")


def _pack(arrs):
    flat = jnp.concatenate([a.reshape(-1) for a in arrs])
    n = flat.shape[0]
    rows = -(-n // 2048) * 16
    return jnp.pad(flat, (0, rows * 128 - n)).reshape(rows, 128)


def _unpack(packed, shapes):
    flat = packed.reshape(-1)
    out, off = [], 0
    for s in shapes:
        n = math.prod(s)
        out.append(flat[off:off + n].reshape(s))
        off += n
    return out


def kernel(x, meta_tokens, ln0_g, ln0_b, w_in, b_in, qk_conv_w, qk_conv_b, s5_lambda_re, s5_lambda_im, s5_log_dt, s5_b_re, s5_b_im, s5_c_re, s5_c_im, s5_d, s5_w_glu, m_norm_g, m_w_out, w_o, ln1_g, ln1_b, w_up, b_up, w_down, ln2_g, ln2_b, loss_target, m_meta_tokens, m_ln0_g, m_ln0_b, m_w_in, m_b_in, m_qk_conv_w, m_qk_conv_b, m_s5_lambda_re, m_s5_lambda_im, m_s5_log_dt, m_s5_b_re, m_s5_b_im, m_s5_c_re, m_s5_c_im, m_s5_d, m_s5_w_glu, m_m_norm_g, m_m_w_out, m_w_o, m_ln1_g, m_ln1_b, m_w_up, m_b_up, m_w_down, m_ln2_g, m_ln2_b, v_meta_tokens, v_ln0_g, v_ln0_b, v_w_in, v_b_in, v_qk_conv_w, v_qk_conv_b, v_s5_lambda_re, v_s5_lambda_im, v_s5_log_dt, v_s5_b_re, v_s5_b_im, v_s5_c_re, v_s5_c_im, v_s5_d, v_s5_w_glu, v_m_norm_g, v_m_w_out, v_w_o, v_ln1_g, v_ln1_b, v_w_up, v_b_up, v_w_down, v_ln2_g, v_ln2_b):
    wts = dict(meta_tokens=meta_tokens, ln0_g=ln0_g, ln0_b=ln0_b, w_in=w_in, b_in=b_in, qk_conv_w=qk_conv_w,
               qk_conv_b=qk_conv_b, s5_lambda_re=s5_lambda_re, s5_lambda_im=s5_lambda_im, s5_log_dt=s5_log_dt,
               s5_b_re=s5_b_re, s5_b_im=s5_b_im, s5_c_re=s5_c_re, s5_c_im=s5_c_im, s5_d=s5_d, s5_w_glu=s5_w_glu,
               m_norm_g=m_norm_g, m_w_out=m_w_out, w_o=w_o, ln1_g=ln1_g, ln1_b=ln1_b, w_up=w_up, b_up=b_up,
               w_down=w_down, ln2_g=ln2_g, ln2_b=ln2_b)
    mom = dict(meta_tokens=m_meta_tokens, ln0_g=m_ln0_g, ln0_b=m_ln0_b, w_in=m_w_in, b_in=m_b_in, qk_conv_w=m_qk_conv_w,
               qk_conv_b=m_qk_conv_b, s5_lambda_re=m_s5_lambda_re, s5_lambda_im=m_s5_lambda_im, s5_log_dt=m_s5_log_dt,
               s5_b_re=m_s5_b_re, s5_b_im=m_s5_b_im, s5_c_re=m_s5_c_re, s5_c_im=m_s5_c_im, s5_d=m_s5_d,
               s5_w_glu=m_s5_w_glu, m_norm_g=m_m_norm_g, m_w_out=m_m_w_out, w_o=m_w_o, ln1_g=m_ln1_g, ln1_b=m_ln1_b,
               w_up=m_w_up, b_up=m_b_up, w_down=m_w_down, ln2_g=m_ln2_g, ln2_b=m_ln2_b)
    var = dict(meta_tokens=v_meta_tokens, ln0_g=v_ln0_g, ln0_b=v_ln0_b, w_in=v_w_in, b_in=v_b_in, qk_conv_w=v_qk_conv_w,
               qk_conv_b=v_qk_conv_b, s5_lambda_re=v_s5_lambda_re, s5_lambda_im=v_s5_lambda_im, s5_log_dt=v_s5_log_dt,
               s5_b_re=v_s5_b_re, s5_b_im=v_s5_b_im, s5_c_re=v_s5_c_re, s5_c_im=v_s5_c_im, s5_d=v_s5_d,
               s5_w_glu=v_s5_w_glu, m_norm_g=v_m_norm_g, m_w_out=v_m_w_out, w_o=v_w_o, ln1_g=v_ln1_g, ln1_b=v_ln1_b,
               w_up=v_w_up, b_up=v_b_up, w_down=v_w_down, ln2_g=v_ln2_g, ln2_b=v_ln2_b)
    d = x.shape[-1]
    chip = 2 * lax.axis_index("x") + lax.axis_index("y")

    gw = dict(zip(_SMALL_SHARDED, _gather_chips([meta_tokens, qk_conv_w[0]])))
    own_w_in = _bf(w_in[0])
    fsend, frecv, fsrc, fland, ftok = _xchg_start([own_w_in], [lax.empty((4,) + own_w_in.shape, BF16)],
                                                  name="gather_w_in_start", scatter=False, dep=gw["qk_conv_w"])
    late_names = tuple(n for n in _BIG if n != "w_in")
    cat = lambda a: jnp.transpose(a, (1, 0, 2)).reshape(a.shape[1], 4 * a.shape[2])
    w = dict(
        meta_tokens=cat(gw["meta_tokens"]), ln0_g=ln0_g[None], ln0_b=_tie(ln0_b[None], ftok),
        qk_conv_w=cat(gw["qk_conv_w"]), qk_conv_b=qk_conv_b,
        s5_lambda_re=s5_lambda_re[0], s5_lambda_im=s5_lambda_im[0], s5_log_dt=s5_log_dt[0][:, None],
        s5_b_re=s5_b_re[0], s5_b_im=s5_b_im[0], s5_c_re=s5_c_re[0], s5_c_im=s5_c_im[0], s5_d=s5_d,
        m_norm_g=m_norm_g, ln1_g=ln1_g, ln1_b=ln1_b, b_up=b_up, ln2_g=ln2_g, ln2_b=ln2_b)
    in_flight = {}

    def place_own(src, land):
        return lax.dynamic_update_slice(land, src[None], (chip,) + (0,) * src.ndim)

    small_names = _SMALL + _SMALL_SHARDED

    def view(n, a):
        return jnp.swapaxes(a, -1, -2) if n in ("s5_b_re", "s5_b_im") else a

    small_wmv = [_pack([view(n, dct[n]) for n in small_names]) for dct in (wts, mom, var)]

    def early(after):
        src, land = _xchg_wait(fsend, frecv, fsrc, fland, tuple(after) + tuple(small_wmv), name="gather_w_in_wait",
                               scatter=False)
        late_src = [_bf(wts[n][0]) for n in late_names]
        st = _xchg_start(late_src, [lax.empty((4,) + a.shape, a.dtype) for a in late_src], name="gather_late_start",
                         scatter=False, dep=src[0])
        in_flight["late"] = st[:4]
        return dict(w_in=_w_in_from_slots(place_own(src[0], land[0]), IN_CHUNK), b_in=_tie(_to_pad_cols(b_in), st[4]))

    def late(after):
        src, land = _xchg_wait(*in_flight["late"], after, name="gather_late_wait", scatter=False)
        full = {n: place_own(s, ld) for n, s, ld in zip(late_names, src, land)}
        return dict(s5_w_glu=full["s5_w_glu"], m_w_out=full["m_w_out"].reshape(d, d), w_o=full["w_o"].reshape(d, d),
                    w_up=full["w_up"], w_down=full["w_down"].reshape(4 * d, d))

    flying = []

    def ready(names, g):
        parts = dict(
            w_in=lambda: _slots_from_w_in(g["w_in"][0]), s5_w_glu=lambda: g["s5_w_glu"],
            m_w_out=lambda: g["m_w_out"].reshape(4, d // 4, d), w_o=lambda: g["w_o"].reshape(4, d // 4, d),
            w_up=lambda: g["w_up"], w_down=lambda: g["w_down"].reshape(4, d, d))
        src = [parts[n]() for n in names]
        land = [lax.empty((3,) + a.shape[1:], a.dtype) for a in src]
        st = _xchg_start(src, land, name="scatter_" + names[0] + "_start", scatter=True)
        flying.append((names,) + st[:4])
        return st[4]

    loss, grad_x, g = _local_step(x, loss_target, w, early, late, ready)
    g["b_in"] = _from_pad_cols(g["b_in"])

    res = {}

    def flat(a):
        return jnp.swapaxes(a, -1, -2).reshape(a.shape[:-2] + (-1, 128))

    def unflat(y, shape):
        return jnp.swapaxes(y.reshape(shape[:-2] + (shape[-1], shape[-2])), -1, -2)

    def finish(groups, after, tag):
        mine = {}
        for names, send, recv, src, land in groups:
            src, land = _xchg_wait(send, recv, src, land, after, name="scatter_" + names[0] + "_wait", scatter=True)
            for n, s, ld in zip(names, src, land):
                mine[n] = _sum_slots(s, ld, chip)
        theirs = _swap_cores(list(mine.values()), name="swap_cores_" + tag)
        for n, t in zip(mine, theirs):
            if n == "w_in":
                res[n] = [unflat(r, wts[n].shape) for r in _adamw(flat(wts[n]), flat(mom[n]), flat(var[n]),
                                                                  flat(mine[n]), flat(t))]
            else:
                res[n] = _adamw(wts[n], mom[n], var[n], mine[n], t)

    finish(flying[:-1], g["ln0_g"], "a")

    small_shapes = [(1, 128)] + [view(n, wts[n]).shape for n in _SMALL] + [g[n].shape for n in _SMALL_SHARDED]
    packed = _pack([loss] + [view(n, g[n]) for n in _SMALL] + [g[n] for n in _SMALL_SHARDED])
    tot = _unpack(_allreduce_small(packed, dep=res["w_o"][3]), small_shapes)
    loss_out = tot[0][0, 0]
    gsm = dict(zip(_SMALL + _SMALL_SHARDED, tot[1:]))
    for n in _SMALL_SHARDED:
        cols = wts[n].shape[-1]
        gsm[n] = lax.dynamic_slice_in_dim(gsm[n], chip * cols, cols, axis=1).reshape(wts[n].shape)

    names = small_names
    shapes = [view(n, wts[n]).shape for n in names]
    small_out = _adamw(*small_wmv, _pack([gsm[n] for n in names]))
    small_res = [_unpack(r, shapes) for r in small_out]
    for j, n in enumerate(names):
        res[n] = [view(n, small_res[q][j]) for q in range(4)]
    finish(flying[-1:], small_out[0], "b")

    return (loss_out, grad_x, *[res[n][0] for n in _ORDER], *[res[n][1] for n in _ORDER],
            *[res[n][2] for n in _ORDER], *[res[n][3] for n in _ORDER])
```

```python
import functools
import math

import jax
import jax.numpy as jnp
from jax import lax
from jax.experimental import pallas as pl
from jax.experimental.pallas import tpu as pltpu

F32 = jnp.float32
BF16 = jnp.bfloat16
HI = lax.Precision.HIGHEST

N_META = 16
M_HEADS = 4
M_CHUNK = 128
PAD = M_CHUNK - N_META
CONV_W = 4
HALO_ROWS = 16
S5_GROUP = 16
S5_STATE = 64
S5_KCH = 4
LN_EPS = 1e-5
ALPHA = 2.0 ** 0.25
NEG = -1e30
ADAM_LR, ADAM_B1, ADAM_B2, ADAM_EPS, ADAM_WD, ADAM_STEP = 0.001, 0.9, 0.999, 1e-08, 0.01, 10

O_OFF, GS_OFF, GM_OFF, V_OFF, Q_OFF, K_OFF, U_OFF, G_OFF, NP = 0, 1024, 2048, 3072, 4096, 4608, 5120, 5632, 5760

NN = ((1,), (0,))
NT = ((1,), (1,))
TN = ((0,), (0,))


def _dot(a, b, dims=NN, prec=None):
    return lax.dot_general(a, b, (dims, ((), ())), preferred_element_type=F32, precision=prec)


def _bf(x):
    return x.astype(BF16)


def _sig(x):
    return 0.5 * jnp.tanh(0.5 * x) + 0.5


def _pcall(body, **kw):
    return pl.pallas_call(body, **kw)


def _cp(sem=None, vmem_mb=None):
    kw = {}
    if sem is not None:
        kw["dimension_semantics"] = sem
    if vmem_mb is not None:
        kw["vmem_limit_bytes"] = vmem_mb << 20
    return pltpu.CompilerParams(**kw)


def _row_tile(n, want, mult=16):
    best = None
    for t in range(mult, want + 1, mult):
        if n % t == 0:
            best = t
    assert best is not None, (n, want)
    return best


def _resident(shape):
    nd = len(shape)
    return pl.BlockSpec(shape, lambda *_: (0,) * nd, pipeline_mode=pl.Buffered(1))


def _const(shape):
    nd = len(shape)
    return pl.BlockSpec(shape, lambda *_: (0,) * nd)


def _ln_fwd(x, g, b):
    mu = jnp.mean(x, axis=-1, keepdims=True)
    xc = x - mu
    var = jnp.mean(xc * xc, axis=-1, keepdims=True)
    rstd = lax.rsqrt(var + LN_EPS)
    xhat = xc * rstd
    return xhat * g + b, xhat, rstd


def _ln_bwd(dy, xhat, rstd, g):
    dxh = dy * g
    m1 = jnp.mean(dxh, axis=-1, keepdims=True)
    m2 = jnp.mean(dxh * xhat, axis=-1, keepdims=True)
    return rstd * (dxh - m1 - xhat * m2)


def _colsum(x):
    return jnp.sum(x, axis=0, keepdims=True)


def _to_pad_cols(w):
    u, q, k, v, o, gi, gf, gs, gm = (w[..., 0:512], w[..., 512:1024], w[..., 1024:1536], w[..., 1536:2560],
                                     w[..., 2560:3584], w[..., 3584:3588], w[..., 3588:3592], w[..., 3592:4616],
                                     w[..., 4616:5640])
    z = jnp.zeros(w.shape[:-1] + (NP - G_OFF - 8,), w.dtype)
    return jnp.concatenate([o, gs, gm, v, q, k, u, gi, gf, z], axis=-1)


def _from_pad_cols(w):
    o, gs, gm, v, q, k, u = (w[..., O_OFF:GS_OFF], w[..., GS_OFF:GM_OFF], w[..., GM_OFF:V_OFF], w[..., V_OFF:Q_OFF],
                             w[..., Q_OFF:K_OFF], w[..., K_OFF:U_OFF], w[..., U_OFF:G_OFF])
    gi, gf = w[..., G_OFF:G_OFF + 4], w[..., G_OFF + 4:G_OFF + 8]
    return jnp.concatenate([u, q, k, v, o, gi, gf, gs, gm], axis=-1)


_IN_REF = (("u", 512), ("q", 512), ("k", 512), ("v", 1024), ("o", 1024), ("i", 4), ("f", 4), ("gs", 1024), ("gm", 1024))
_IN_PAD = (("o", O_OFF), ("gs", GS_OFF), ("gm", GM_OFF), ("v", V_OFF), ("q", Q_OFF), ("k", K_OFF), ("u", U_OFF),
           ("i", G_OFF), ("f", G_OFF + 4))


def _in_ref_ranges():
    out, off = {}, 0
    for n, s in _IN_REF:
        out[n] = (off, off + s)
        off += s
    return out, off


def _w_in_from_slots(g, chunk=None):
    rng, total = _in_ref_ranges()
    width = total // g.shape[0]
    cols = []
    for n, _ in _IN_PAD:
        a, b = rng[n]
        while a < b:
            s = a // width
            e = min(b, (s + 1) * width)
            cols.append(g[s][:, a - s * width:e - s * width])
            a = e
    cols.append(jnp.zeros((g.shape[1], NP - G_OFF - 8), g.dtype))
    if chunk is None:
        return jnp.concatenate(cols, axis=1)
    chunks, cur, room = [], [], chunk
    for c in cols:
        while c.shape[1] > 0:
            take = min(room, c.shape[1])
            cur.append(c[:, :take])
            c, room = c[:, take:], room - take
            if room == 0:
                chunks.append(jnp.concatenate(cur, axis=1))
                cur, room = [], chunk
    assert not cur
    return jnp.stack(chunks, axis=0)


def _slots_from_w_in(wp, nslot=4):
    rng, total = _in_ref_ranges()
    width = total // nslot
    pad_off = dict(_IN_PAD)
    slots = []
    for s in range(nslot):
        lo, hi = s * width, (s + 1) * width
        cols = []
        for n, _ in _IN_REF:
            a, b = rng[n]
            x0, x1 = max(a, lo), min(b, hi)
            if x0 < x1:
                cols.append(wp[:, pad_off[n] + x0 - a:pad_off[n] + x1 - a])
        slots.append(jnp.concatenate(cols, axis=1))
    return jnp.stack(slots, axis=0)


HEAD = PAD + N_META


def _ln0_in(j, x_ref, meta_ref):
    first = jnp.concatenate([jnp.zeros((PAD, meta_ref.shape[1]), F32), meta_ref[...]], axis=0)
    return jnp.where(j == 0, first[None], x_ref[...])


def _ln0_fwd(x, meta, g, b):
    bsz, seq, d = x.shape
    nb = seq // HEAD + 1

    def body(x_ref, m_ref, g_ref, b_ref, o_ref, ob_ref):
        y, _, _ = _ln_fwd(_ln0_in(pl.program_id(0), x_ref, m_ref), g_ref[...], b_ref[...])
        o_ref[...] = y
        ob_ref[...] = _bf(y)

    row = pl.BlockSpec((bsz, HEAD, d), lambda j: (0, j, 0))
    h0, h0b = _pcall(
        body, name="ln0_fwd", grid=(nb,),
        in_specs=[pl.BlockSpec((bsz, HEAD, d), lambda j: (0, jnp.maximum(j - 1, 0), 0)), _const((N_META, d)),
                  _const((1, d)), _const((1, d))],
        out_specs=[row, row],
        out_shape=[jax.ShapeDtypeStruct((bsz, nb * HEAD, d), F32), jax.ShapeDtypeStruct((bsz, nb * HEAD, d), BF16)],
        compiler_params=_cp(("arbitrary",)),
    )(x, meta, g, b)
    return h0.reshape(-1, d), h0b.reshape(-1, d)


def _ln0_bwd(x, meta, dr1, dpw, g):
    bsz, seq, d = x.shape
    nb = seq // HEAD + 1

    def body(x_ref, m_ref, a_ref, c_ref, g_ref, o_ref, dg_ref, db_ref, dm_ref):
        j = pl.program_id(0)

        @pl.when(j == 0)
        def _():
            dg_ref[...] = jnp.zeros_like(dg_ref)
            db_ref[...] = jnp.zeros_like(db_ref)
            dm_ref[...] = jnp.zeros_like(dm_ref)

        dy = ALPHA * a_ref[...] + c_ref[...]
        _, xhat, rstd = _ln_fwd(_ln0_in(j, x_ref, m_ref), g_ref[...], 0.0)
        dx = _ln_bwd(dy, xhat, rstd, g_ref[...])
        o_ref[...] = dx
        dg_ref[...] += _colsum((dy * xhat).reshape(bsz * HEAD, d))
        db_ref[...] += _colsum(dy.reshape(bsz * HEAD, d))

        @pl.when(j == 0)
        def _():
            dm_ref[...] += jnp.sum(dx[:, PAD:, :], axis=0)

    row = pl.BlockSpec((bsz, HEAD, d), lambda j: (0, j, 0))
    tok = pl.BlockSpec((bsz, HEAD, d), lambda j: (0, jnp.maximum(j - 1, 0), 0))
    lp = nb * HEAD
    return _pcall(
        body, name="ln0_bwd", grid=(nb,),
        in_specs=[tok, _const((N_META, d)), row, row, _const((1, d))],
        out_specs=[tok, _const((1, d)), _const((1, d)), _const((N_META, d))],
        out_shape=[jax.ShapeDtypeStruct((bsz, seq, d), F32), jax.ShapeDtypeStruct((1, d), F32),
                   jax.ShapeDtypeStruct((1, d), F32), jax.ShapeDtypeStruct((N_META, d), F32)],
        compiler_params=_cp(("arbitrary",)),
    )(x, meta, dr1.reshape(bsz, lp, d), dpw.reshape(bsz, lp, d), g)


IN_CHUNK = 1152


def _chunk_cols(w):
    k, n = w.shape
    return jnp.transpose(w.reshape(k, n // IN_CHUNK, IN_CHUNK), (1, 0, 2))


def _inproj(h0b, w3, bias, lp):
    r, d = h0b.shape
    nj, _, tn = w3.shape
    tm = _row_tile(lp, 1056)
    tps = lp // tm

    def body(a_ref, w_ref, b_ref, o_ref, gate_ref):
        i = pl.program_id(0)
        j = pl.program_id(1)
        acc = _dot(a_ref[...], w_ref[j]) + b_ref[...]
        t = (i % tps) * tm + lax.broadcasted_iota(jnp.int32, (tm, 1), 0)
        acc = jnp.where(t >= PAD, acc, 0.0)
        o_ref[...] = _bf(acc)

        @pl.when(j == nj - 1)
        def _():
            gate_ref[...] = acc[:, tn - 128:]

    return _pcall(
        body, name="inproj", grid=(r // tm, nj),
        in_specs=[pl.BlockSpec((tm, d), lambda i, j: (i, 0)), _resident(w3.shape),
                  pl.BlockSpec((1, tn), lambda i, j: (0, j))],
        out_specs=[pl.BlockSpec((tm, tn), lambda i, j: (i, j)), pl.BlockSpec((tm, 128), lambda i, j: (i, 0))],
        out_shape=[jax.ShapeDtypeStruct((r, nj * tn), BF16), jax.ShapeDtypeStruct((r, 128), F32)],
        compiler_params=_cp(("parallel", "arbitrary"), 48),
    )(h0b, w3, bias)


def _mm_tn(a, b, *, name, split=1, colsum=False, tk_want=2112):
    r, m = a.shape
    n = b.shape[1]
    tk = _row_tile(r, tk_want)
    tm = min(m, 1024)
    ns = n // split
    tn = ns
    for cand in (1024, 1152, 640, 512, 128):
        if ns % cand == 0 and cand <= ns:
            tn = cand
            break
    nb = ns // tn
    nk = r // tk

    def body(a_ref, b_ref, o_ref, *rest):
        acc = rest[-1]
        k = pl.program_id(2)

        @pl.when(k == 0)
        def _():
            acc[...] = jnp.zeros_like(acc)

        bt = b_ref[...]
        acc[...] += _dot(_bf(a_ref[...]), _bf(bt), TN)

        @pl.when(k == nk - 1)
        def _():
            o_ref[...] = acc[...]

        if colsum:
            cs_ref = rest[0]

            @pl.when(k == 0)
            def _():
                cs_ref[...] = jnp.zeros_like(cs_ref)

            cs_ref[...] += _colsum(bt.astype(F32))

    out_specs = [pl.BlockSpec((None, tm, tn), lambda i, j, k: (j // nb, i, j % nb))]
    out_shape = [jax.ShapeDtypeStruct((split, m, ns), F32)]
    if colsum:
        assert m == tm
        out_specs.append(pl.BlockSpec((1, tn), lambda i, j, k: (0, j)))
        out_shape.append(jax.ShapeDtypeStruct((1, n), F32))
    res = _pcall(
        body, name=name, grid=(m // tm, n // tn, nk),
        in_specs=[pl.BlockSpec((tk, tm), lambda i, j, k: (k, i)), pl.BlockSpec((tk, tn), lambda i, j, k: (k, j))],
        out_specs=out_specs, out_shape=out_shape,
        scratch_shapes=[pltpu.VMEM((tm, tn), F32)],
        compiler_params=_cp(("parallel", "parallel", "arbitrary"), 56),
    )(a, b)
    return res if colsum else res[0]


def _mm_nt(a, w3, lp, *, name, dep=None):
    r, kdim = a.shape
    nk, n, tk = w3.shape
    assert nk * tk == kdim
    tm = _row_tile(lp, 1056)
    deps = [] if dep is None else [dep]

    def body(a_ref, w_ref, *rest):
        o_ref, acc = rest[-2:]
        k = pl.program_id(1)

        @pl.when(k == 0)
        def _():
            acc[...] = jnp.zeros_like(acc)

        acc[...] += _dot(_bf(a_ref[...]), w_ref[k], NT)

        @pl.when(k == nk - 1)
        def _():
            o_ref[...] = acc[...]

    return _pcall(
        body, name=name, grid=(r // tm, nk),
        in_specs=[pl.BlockSpec((tm, tk), lambda i, k: (i, k)), _resident(w3.shape)]
        + [_const(dp_.shape) for dp_ in deps],
        out_specs=pl.BlockSpec((tm, n), lambda i, k: (i, 0)),
        out_shape=jax.ShapeDtypeStruct((r, n), F32),
        scratch_shapes=[pltpu.VMEM((tm, n), F32)],
        compiler_params=_cp(("parallel", "arbitrary"), 48),
    )(a, w3, *deps)


def _s5_prep(lam_re, lam_im, log_dt, b_re_t, b_im_t):
    g, p = lam_re.shape
    h = b_re_t.shape[0]

    def body(lr_ref, li_ref, ldt_ref, br_ref, bi_ref, pr_ref, pi_ref, bbr_ref, bbi_ref):
        lr, li = lr_ref[...], li_ref[...]
        dt = jnp.exp(ldt_ref[...])
        e = jnp.exp(lr * dt)
        ar, ai = e * jnp.cos(li * dt), e * jnp.sin(li * dt)
        den = lr * lr + li * li
        cr = ((ar - 1.0) * lr + ai * li) / den
        ci = (ai * lr - (ar - 1.0) * li) / den
        br, bi = br_ref[...], bi_ref[...]
        bbr_ref[...] = cr[None] * br - ci[None] * bi
        bbi_ref[...] = cr[None] * bi + ci[None] * br
        xr, xi = ar, ai
        pr_ref[0] = xr
        pi_ref[0] = xi
        for t in range(1, 8):
            xr, xi = xr * ar - xi * ai, xr * ai + xi * ar
            pr_ref[t] = xr
            pi_ref[t] = xi

    sd = jax.ShapeDtypeStruct
    return _pcall(body, name="s5_prep",
                  out_shape=[sd((8, g, p), F32), sd((8, g, p), F32), sd((h, g, p), F32), sd((h, g, p), F32)])(
        lam_re, lam_im, log_dt, b_re_t, b_im_t)


def _s5_prep_bwd(lam_re, lam_im, log_dt, b_re_t, b_im_t, da_re, da_im, dbb_re_t, dbb_im_t):
    g, p = lam_re.shape
    h = b_re_t.shape[0]

    def body(lr_ref, li_ref, ldt_ref, br_ref, bi_ref, dar_ref, dai_ref, dbr_ref, dbi_ref,
             glr_ref, gli_ref, gdt_ref, gbr_ref, gbi_ref):
        lr, li = lr_ref[...], li_ref[...]
        dt = jnp.exp(ldt_ref[...])
        e = jnp.exp(lr * dt)
        ar, ai = e * jnp.cos(li * dt), e * jnp.sin(li * dt)
        den = lr * lr + li * li
        cr = ((ar - 1.0) * lr + ai * li) / den
        ci = (ai * lr - (ar - 1.0) * li) / den
        br, bi = br_ref[...], bi_ref[...]
        gr, gi = dbr_ref[...], dbi_ref[...]
        gbr_ref[...] = gr * cr[None] + gi * ci[None]
        gbi_ref[...] = gi * cr[None] - gr * ci[None]
        gcr = jnp.sum(gr * br + gi * bi, axis=0)
        gci = jnp.sum(gi * br - gr * bi, axis=0)
        ilr, ili = lr / den, -li / den
        gar = dar_ref[...] + gcr * ilr + gci * ili
        gai = dai_ref[...] + gci * ilr - gcr * ili
        qr, qi = cr * ilr - ci * ili, cr * ili + ci * ilr
        glr = -(gcr * qr + gci * qi)
        gli = -(gci * qr - gcr * qi)
        gzr = gar * ar + gai * ai
        gzi = gai * ar - gar * ai
        glr_ref[...] = glr + gzr * dt
        gli_ref[...] = gli + gzi * dt
        gdt_ref[...] = jnp.sum(gzr * lr + gzi * li, axis=1, keepdims=True) * dt

    sd = jax.ShapeDtypeStruct
    return _pcall(body, name="s5_prep_bwd",
                  out_shape=[sd((g, p), F32), sd((g, p), F32), sd((g, 1), F32), sd((h, g, p), F32), sd((h, g, p), F32)])(
        lam_re, lam_im, log_dt, b_re_t, b_im_t, da_re, da_im, dbb_re_t, dbb_im_t)


def _cmul(xr, xi, yr, yi):
    return xr * yr - xi * yi, xr * yi + xi * yr


def _dot5(a, b, dims=NN):
    return _dot(_bf(a), _bf(b), dims)


def _s5_fwd(p3, bk, cre, cim, apow, dskip):
    bsz, lp, _ = p3.shape
    tt = _row_tile(lp, 528, 8)
    nt = lp // tt
    nblk = tt // 8
    hw = 512

    def body(u_ref, bk_ref, cre_ref, cim_ref, ap_ref, d_ref, y_ref, xs_ref, car_ref):
        t = pl.program_id(2)

        @pl.when(t == 0)
        def _():
            car_ref[...] = jnp.zeros_like(car_ref)

        u = u_ref[...].astype(F32)
        xs_ref[...] = _dot5(u, bk_ref[...])
        ap = ap_ref[...]
        apr, api = ap[:, :hw], ap[:, hw:]
        rows = lax.broadcasted_iota(jnp.int32, (8, hw), 0)
        lev = [(d, jnp.where(rows < d, 0.0, jnp.broadcast_to(apr[d - 1:d, :], (8, hw))),
                jnp.where(rows < d, 0.0, jnp.broadcast_to(api[d - 1:d, :], (8, hw)))) for d in (1, 2, 4)]

        def blk(i, carry):
            cr, ci = carry
            off = pl.multiple_of(i * 8, 8)
            x = xs_ref[pl.ds(off, 8), :]
            xr, xi = x[:, :hw], x[:, hw:]
            for d, lr, li in lev:
                mr, mi = _cmul(pltpu.roll(xr, d, 0), pltpu.roll(xi, d, 0), lr, li)
                xr, xi = xr + mr, xi + mi
            mr, mi = _cmul(apr, api, cr, ci)
            xr, xi = xr + mr, xi + mi
            xs_ref[pl.ds(off, 8), :] = jnp.concatenate([xr, xi], axis=1)
            return xr[7:8, :], xi[7:8, :]

        c0 = car_ref[...]
        cr, ci = lax.fori_loop(0, nblk, blk, (c0[0:1, :hw], c0[0:1, hw:]), unroll=2)
        car_ref[...] = jnp.broadcast_to(jnp.concatenate([cr, ci], axis=1), car_ref.shape)
        xs = xs_ref[...]
        y_ref[...] = (_dot5(xs[:, :hw], cre_ref[...]) - _dot5(xs[:, hw:], cim_ref[...])
                      + d_ref[...] * u)

    ub = U_OFF // 128
    return _pcall(
        body, name="s5_fwd", grid=(S5_KCH, bsz, nt),
        in_specs=[pl.BlockSpec((None, tt, 128), lambda k, b, t: (b, t, ub + k)),
                  pl.BlockSpec((None, 128, 2 * hw), lambda k, b, t: (k, 0, 0)),
                  pl.BlockSpec((None, hw, 128), lambda k, b, t: (k, 0, 0)),
                  pl.BlockSpec((None, hw, 128), lambda k, b, t: (k, 0, 0)),
                  pl.BlockSpec((None, 8, 2 * hw), lambda k, b, t: (k, 0, 0)),
                  pl.BlockSpec((1, 128), lambda k, b, t: (0, k))],
        out_specs=[pl.BlockSpec((None, tt, 128), lambda k, b, t: (b, t, k)),
                   pl.BlockSpec((None, None, tt, 2 * hw), lambda k, b, t: (b, k, t, 0))],
        out_shape=[jax.ShapeDtypeStruct((bsz, lp, S5_KCH * 128), F32),
                   jax.ShapeDtypeStruct((bsz, S5_KCH, lp, 2 * hw), F32)],
        scratch_shapes=[pltpu.VMEM((8, 2 * hw), F32)],
        compiler_params=_cp(("parallel", "parallel", "arbitrary"), 40),
    )(p3, bk, cre, cim, apow, dskip)


def _s5_bwd(dp3, p3, dy3, xs, bk, cre, cim, apow_rev, dskip):
    bsz, lp, _ = p3.shape
    tt = _row_tile(lp, 528, 8)
    nt = lp // tt
    nblk = tt // 8
    hw = 512
    tb = tt // 8

    def body(dp_any, u_ref, dy_ref, xs_ref, halo_ref, bkt_ref, cre_ref, cim_ref, ap_ref, d_ref,
             du_ref, dbk_ref, dcre_ref, dcim_ref, da_ref, dd_ref, g_ref, ext_ref, car_ref):
        b = pl.program_id(1)
        t = pl.program_id(2)
        tidx = nt - 1 - t

        @pl.when(t == 0)
        def _():
            car_ref[...] = jnp.zeros_like(car_ref)

        @pl.when((b == 0) & (t == 0))
        def _():
            dbk_ref[...] = jnp.zeros_like(dbk_ref)
            dcre_ref[...] = jnp.zeros_like(dcre_ref)
            dcim_ref[...] = jnp.zeros_like(dcim_ref)
            da_ref[...] = jnp.zeros_like(da_ref)
            dd_ref[...] = jnp.zeros_like(dd_ref)

        u = u_ref[...].astype(F32)
        dy = dy_ref[...]
        g_ref[:, :hw] = _dot5(dy, cre_ref[...])
        g_ref[:, hw:] = -_dot5(dy, cim_ref[...])
        ap = ap_ref[...]
        apr, api = ap[:, :hw], -ap[:, hw:]
        rows = lax.broadcasted_iota(jnp.int32, (8, hw), 0)
        lev = [(d, jnp.where(rows >= 8 - d, 0.0, jnp.broadcast_to(apr[8 - d:9 - d, :], (8, hw))),
                jnp.where(rows >= 8 - d, 0.0, jnp.broadcast_to(api[8 - d:9 - d, :], (8, hw)))) for d in (1, 2, 4)]

        def blk(i, carry):
            cr, ci = carry
            off = pl.multiple_of((nblk - 1 - i) * 8, 8)
            x = g_ref[pl.ds(off, 8), :]
            xr, xi = x[:, :hw], x[:, hw:]
            for d, lr, li in lev:
                mr, mi = _cmul(pltpu.roll(xr, 8 - d, 0), pltpu.roll(xi, 8 - d, 0), lr, li)
                xr, xi = xr + mr, xi + mi
            mr, mi = _cmul(apr, api, cr, ci)
            xr, xi = xr + mr, xi + mi
            g_ref[pl.ds(off, 8), :] = jnp.concatenate([xr, xi], axis=1)
            return xr[0:1, :], xi[0:1, :]

        c0 = car_ref[...]
        cr, ci = lax.fori_loop(0, nblk, blk, (c0[0:1, :hw], c0[0:1, hw:]), unroll=2)
        car_ref[...] = jnp.broadcast_to(jnp.concatenate([cr, ci], axis=1), car_ref.shape)

        gg = g_ref[...]
        du = _dot5(gg, bkt_ref[...]) + d_ref[...] * dy
        trow = tidx * tt + lax.broadcasted_iota(jnp.int32, (tt, 1), 0)
        du_ref[...] = jnp.where(trow >= PAD, du, 0.0).astype(du_ref.dtype)
        dbk_ref[...] += _dot5(u, gg, TN)
        xsv = xs_ref[...]
        dcre_ref[...] += _dot5(dy, xsv[:, :hw], TN)
        dcim_ref[...] -= _dot5(dy, xsv[:, hw:], TN)
        dd_ref[...] += _colsum(dy * u)
        ext_ref[0:8, :] = jnp.where(tidx == 0, 0.0, halo_ref[...])
        ext_ref[8:, :] = xsv
        xp = ext_ref[pl.ds(7, tt), :]
        gr, gi, pr, pi = gg[:, :hw], gg[:, hw:], xp[:, :hw], xp[:, hw:]
        da_ref[:, :hw] += _colsum(gr * pr + gi * pi)
        da_ref[:, hw:] += _colsum(gi * pr - gr * pi)

    ub = U_OFF // 128
    sd = jax.ShapeDtypeStruct
    rt = lambda t: nt - 1 - t
    tr = lambda a: jnp.swapaxes(a, 1, 2)
    res = _pcall(
        body, name="s5_bwd", grid=(S5_KCH, bsz, nt),
        in_specs=[pl.BlockSpec(memory_space=pl.ANY),
                  pl.BlockSpec((None, tt, 128), lambda k, b, t: (b, rt(t), ub + k)),
                  pl.BlockSpec((None, tt, 128), lambda k, b, t: (b, rt(t), k)),
                  pl.BlockSpec((None, None, tt, 2 * hw), lambda k, b, t: (b, k, rt(t), 0)),
                  pl.BlockSpec((None, None, 8, 2 * hw), lambda k, b, t: (b, k, jnp.maximum(rt(t) * tb - 1, 0), 0)),
                  pl.BlockSpec((None, 2 * hw, 128), lambda k, b, t: (k, 0, 0)),
                  pl.BlockSpec((None, 128, hw), lambda k, b, t: (k, 0, 0)),
                  pl.BlockSpec((None, 128, hw), lambda k, b, t: (k, 0, 0)),
                  pl.BlockSpec((None, 8, 2 * hw), lambda k, b, t: (k, 0, 0)),
                  pl.BlockSpec((1, 128), lambda k, b, t: (0, k))],
        out_specs=[pl.BlockSpec((None, tt, 128), lambda k, b, t: (b, rt(t), ub + k)),
                   pl.BlockSpec((None, 128, 2 * hw), lambda k, b, t: (k, 0, 0)),
                   pl.BlockSpec((None, 128, hw), lambda k, b, t: (k, 0, 0)),
                   pl.BlockSpec((None, 128, hw), lambda k, b, t: (k, 0, 0)),
                   pl.BlockSpec((None, 1, 2 * hw), lambda k, b, t: (k, 0, 0)),
                   pl.BlockSpec((1, 128), lambda k, b, t: (0, k))],
        out_shape=[sd(dp3.shape, dp3.dtype), sd((S5_KCH, 128, 2 * hw), F32), sd((S5_KCH, 128, hw), F32),
                   sd((S5_KCH, 128, hw), F32), sd((S5_KCH, 1, 2 * hw), F32), sd((1, S5_KCH * 128), F32)],
        scratch_shapes=[pltpu.VMEM((tt, 2 * hw), F32), pltpu.VMEM((tt + 8, 2 * hw), F32), pltpu.VMEM((8, 2 * hw), F32)],
        input_output_aliases={0: 0},
        compiler_params=_cp(("arbitrary", "arbitrary", "arbitrary"), 48),
    )(dp3, p3, dy3, xs, xs, tr(bk), tr(cre), tr(cim), apow_rev, dskip)
    return res[0], res[1], tr(res[2]), tr(res[3]), res[4], res[5]


_G0 = math.sqrt(2.0 / math.pi)
_G1 = 0.044715


def _gelu(y):
    return 0.5 * y * (1.0 + jnp.tanh(_G0 * (y + _G1 * y * y * y)))


def _gelu_grad(y):
    th = jnp.tanh(_G0 * (y + _G1 * y * y * y))
    return 0.5 * (1.0 + th) + 0.5 * y * (1.0 - th * th) * _G0 * (1.0 + 3.0 * _G1 * y * y)


def _glu_fwd(y_s5, wglu_g, lp):
    r, w = y_s5.shape
    tm = _row_tile(lp, 416)
    cw = wglu_g.shape[2]

    def body(y_ref, w_ref, gy_ref, z_ref, o_ref):
        gy = _bf(_gelu(y_ref[...]))
        gy_ref[...] = gy
        zs = [_dot(gy, w_ref[s]) for s in range(4)]
        for s in range(4):
            z_ref[:, s * cw:(s + 1) * cw] = _bf(zs[s])
        o_ref[:, :cw] = _bf(zs[0] * _sig(zs[2]))
        o_ref[:, cw:] = _bf(zs[1] * _sig(zs[3]))

    sd = jax.ShapeDtypeStruct
    return _pcall(
        body, name="glu_fwd", grid=(r // tm,),
        in_specs=[pl.BlockSpec((tm, w), lambda i: (i, 0)), _resident(wglu_g.shape)],
        out_specs=[pl.BlockSpec((tm, w), lambda i: (i, 0)), pl.BlockSpec((tm, 4 * cw), lambda i: (i, 0)),
                   pl.BlockSpec((tm, 2 * cw), lambda i: (i, 0))],
        out_shape=[sd((r, w), BF16), sd((r, 4 * cw), BF16), sd((r, 2 * cw), BF16)],
        compiler_params=_cp(("parallel",), 40),
    )(y_s5, wglu_g)


def _glu_bwd(dyg, z, y_s5, wglu_g, lp):
    r, w = y_s5.shape
    tm = _row_tile(lp, 416)
    cw = wglu_g.shape[2]

    def body(d_ref, z_ref, y_ref, w_ref, dz_ref, dy_ref):
        d = d_ref[...].astype(F32)
        zz = z_ref[...].astype(F32)
        acc = jnp.zeros((tm, w), F32)
        for s in range(2):
            z1 = zz[:, s * cw:(s + 1) * cw]
            sg = _sig(zz[:, (2 + s) * cw:(3 + s) * cw])
            dd = d[:, s * cw:(s + 1) * cw]
            dz1 = _bf(dd * sg)
            dz2 = _bf(dd * z1 * sg * (1.0 - sg))
            dz_ref[:, s * cw:(s + 1) * cw] = dz1
            dz_ref[:, (2 + s) * cw:(3 + s) * cw] = dz2
            acc += _dot(dz1, w_ref[s], NT) + _dot(dz2, w_ref[2 + s], NT)
        dy_ref[...] = acc * _gelu_grad(y_ref[...])

    sd = jax.ShapeDtypeStruct
    return _pcall(
        body, name="glu_bwd", grid=(r // tm,),
        in_specs=[pl.BlockSpec((tm, 2 * cw), lambda i: (i, 0)), pl.BlockSpec((tm, 4 * cw), lambda i: (i, 0)),
                  pl.BlockSpec((tm, w), lambda i: (i, 0)), _resident(wglu_g.shape)],
        out_specs=[pl.BlockSpec((tm, 4 * cw), lambda i: (i, 0)), pl.BlockSpec((tm, w), lambda i: (i, 0))],
        out_shape=[sd((r, 4 * cw), BF16), sd((r, w), F32)],
        compiler_params=_cp(("parallel",), 40),
    )(dyg, z, y_s5, wglu_g)


def _conv_fwd(p3, cw, cb):
    bsz, lp, _ = p3.shape
    tt = _row_tile(lp, 416)
    nt = lp // tt
    tb = tt // 8
    c = cw.shape[1]
    qb = Q_OFF // c

    hr = HALO_ROWS
    off = hr - (CONV_W - 1)

    def body(x_ref, halo_ref, w_ref, b_ref, pre_ref, act_ref, ext_ref):
        t = pl.program_id(1)
        ext_ref[0:hr, :] = jnp.where(t == 0, 0.0, halo_ref[...].astype(F32))
        ext_ref[hr:, :] = x_ref[...].astype(F32)
        w = w_ref[...]
        acc = b_ref[...] + w[0:1, :] * ext_ref[pl.ds(off, tt), :]
        for j in range(1, CONV_W):
            acc = acc + w[j:j + 1, :] * ext_ref[pl.ds(off + j, tt), :]
        pre_ref[...] = _bf(acc)
        act_ref[...] = _bf(acc * _sig(acc))

    sd = jax.ShapeDtypeStruct
    return _pcall(
        body, name="conv_fwd", grid=(bsz, nt),
        in_specs=[pl.BlockSpec((None, tt, c), lambda b, t: (b, t, qb)),
                  pl.BlockSpec((None, hr, c), lambda b, t: (b, jnp.maximum(t * (tt // hr) - 1, 0), qb)),
                  _const((CONV_W, c)), _const((1, c))],
        out_specs=[pl.BlockSpec((None, tt, c), lambda b, t: (b, t, 0))] * 2,
        out_shape=[sd((bsz, lp, c), BF16)] * 2,
        scratch_shapes=[pltpu.VMEM((tt + hr, c), F32)],
        compiler_params=_cp(("parallel", "parallel")),
    )(p3, p3, cw, cb)


def _conv_bwd(dp3, p3, dact3, pre3, cw):
    bsz, lp, _ = p3.shape
    tt = _row_tile(lp, 416)
    nt = lp // tt
    tb = tt // 8
    c = cw.shape[1]
    qb = Q_OFF // c

    hr = HALO_ROWS
    off = hr - (CONV_W - 1)

    def silu_grad(x):
        s = _sig(x)
        return s * (1.0 + x * (1.0 - s))

    def body(dp_any, x_ref, xh_ref, d_ref, dh_ref, pre_ref, preh_ref, w_ref, o_ref, dw_ref, db_ref, ext_ref, dext_ref):
        b = pl.program_id(0)
        t = pl.program_id(1)

        @pl.when((b == 0) & (t == 0))
        def _():
            dw_ref[...] = jnp.zeros_like(dw_ref)
            db_ref[...] = jnp.zeros_like(db_ref)

        dc = d_ref[...].astype(F32) * silu_grad(pre_ref[...].astype(F32))
        dch = jnp.where(t == nt - 1, 0.0, dh_ref[...].astype(F32) * silu_grad(preh_ref[...].astype(F32)))
        dext_ref[0:tt, :] = dc
        dext_ref[tt:, :] = dch
        ext_ref[0:hr, :] = jnp.where(t == 0, 0.0, xh_ref[...].astype(F32))
        ext_ref[hr:, :] = x_ref[...].astype(F32)
        w = w_ref[...]
        acc = w[CONV_W - 1:CONV_W, :] * dc
        for j in range(CONV_W - 1):
            acc = acc + w[j:j + 1, :] * dext_ref[pl.ds(CONV_W - 1 - j, tt), :]
        trow = t * tt + lax.broadcasted_iota(jnp.int32, (tt, 1), 0)
        o_ref[...] = jnp.where(trow >= PAD, acc, 0.0).astype(o_ref.dtype)
        db_ref[...] += _colsum(dc)
        for j in range(CONV_W):
            dw_ref[j:j + 1, :] += _colsum(dc * ext_ref[pl.ds(off + j, tt), :])

    sd = jax.ShapeDtypeStruct
    nxt = lambda t: jnp.minimum((t + 1) * (tt // hr), lp // hr - 1)
    return _pcall(
        body, name="conv_bwd", grid=(bsz, nt),
        in_specs=[pl.BlockSpec(memory_space=pl.ANY),
                  pl.BlockSpec((None, tt, c), lambda b, t: (b, t, qb)),
                  pl.BlockSpec((None, hr, c), lambda b, t: (b, jnp.maximum(t * (tt // hr) - 1, 0), qb)),
                  pl.BlockSpec((None, tt, c), lambda b, t: (b, t, 0)),
                  pl.BlockSpec((None, hr, c), lambda b, t: (b, nxt(t), 0)),
                  pl.BlockSpec((None, tt, c), lambda b, t: (b, t, 0)),
                  pl.BlockSpec((None, hr, c), lambda b, t: (b, nxt(t), 0)),
                  _const((CONV_W, c))],
        out_specs=[pl.BlockSpec((None, tt, c), lambda b, t: (b, t, qb)), _const((CONV_W, c)), _const((1, c))],
        out_shape=[sd(dp3.shape, dp3.dtype), sd((CONV_W, c), F32), sd((1, c), F32)],
        scratch_shapes=[pltpu.VMEM((tt + hr, c), F32), pltpu.VMEM((tt + hr, c), F32)],
        input_output_aliases={0: 0},
        compiler_params=_cp(("arbitrary", "arbitrary")),
    )(dp3, p3, p3, dact3, dact3, pre3, pre3, cw)


def _mlstm_gates(g, h_idx, c_idx, lc):
    lane = lax.broadcasted_iota(jnp.int32, g.shape, 1)
    i_col = jnp.sum(jnp.where(lane == h_idx, g, 0.0), axis=1, keepdims=True)
    f_col = jnp.sum(jnp.where(lane == M_HEADS + h_idx, g, 0.0), axis=1, keepdims=True)
    row = lax.broadcasted_iota(jnp.int32, (lc, 1), 0)
    valid = (c_idx * lc + row) >= PAD
    li = jnp.where(valid, i_col, NEG)
    lf = jnp.where(valid, jnp.minimum(f_col, 0.0) - jnp.log(1.0 + jnp.exp(-jnp.abs(f_col))), 0.0)
    r2 = lax.broadcasted_iota(jnp.int32, (lc, lc), 0)
    c2 = lax.broadcasted_iota(jnp.int32, (lc, lc), 1)
    eye = r2 == c2
    tril = r2 >= c2
    to_row = lambda col: jnp.sum(jnp.where(eye, col, 0.0), axis=0, keepdims=True)
    lf_row = to_row(lf)
    b_col = jnp.sum(jnp.where(tril, lf_row, 0.0), axis=1, keepdims=True)
    b_row = to_row(b_col)
    li_row = to_row(li)
    d_mat = jnp.where(tril, b_col - b_row + li_row, NEG)
    return dict(f_col=f_col, valid=valid, li=li, b_col=b_col, d_mat=d_mat, eye=eye, r2=r2, c2=c2, row=row,
                to_row=to_row)


def _mlstm_chunk(q, ks, v, gq, c_st, n_st, m_st, lc):
    b_col, d_mat = gq["b_col"], gq["d_mat"]
    m_inter = b_col + m_st
    m_row = jnp.maximum(m_inter, jnp.max(d_mat, axis=1, keepdims=True))
    w_intra = jnp.exp(d_mat - m_row)
    w_inter = jnp.exp(m_inter - m_row)
    qb, kb, vb, cb = _bf(q), _bf(ks), _bf(v), _bf(c_st)
    s = _dot(qb, kb, NT) * w_intra
    qc = _dot(qb, cb)
    num = _dot(_bf(s), vb) + w_inter * qc
    qn = jnp.sum(q * n_st, axis=1, keepdims=True)
    den = jnp.sum(s, axis=1, keepdims=True) + w_inter * qn
    e = jnp.exp(-m_row)
    nn = jnp.maximum(jnp.abs(den), e)
    b_last = b_col[lc - 1:lc, :]
    g_log = b_last - b_col + gq["li"]
    m_new = jnp.maximum(b_last + m_st, jnp.max(g_log, axis=0, keepdims=True))
    w_k = jnp.exp(g_log - m_new)
    decay = jnp.exp(b_last + m_st - m_new)
    return dict(w_intra=w_intra, w_inter=w_inter, qb=qb, kb=kb, vb=vb, cb=cb, s=s, qc=qc, num=num, qn=qn, den=den,
                e=e, nn=nn, m_new=m_new, w_k=w_k, decay=decay)


def _chunks_per_step(nc):
    return max(c for c in (3, 2, 1) if nc % c == 0)


def _mlstm_fwd(qk3, p3, pg3):
    bsz, lp, _ = p3.shape
    lc = M_CHUNK
    nc = lp // lc
    dk, dv = 128, 256
    scale = dk ** -0.5

    cps = _chunks_per_step(nc)
    rows = cps * lc

    def body(q_ref, k_ref, v_ref, g_ref, h_ref, cs_ref, ns_ref, ms_ref, c_sc, n_sc, m_sc):
        st = pl.program_id(1)

        @pl.when(st == 0)
        def _():
            c_sc[...] = jnp.zeros_like(c_sc)
            n_sc[...] = jnp.zeros_like(n_sc)
            m_sc[...] = jnp.zeros_like(m_sc)

        for j in range(cps):
            rs = slice(j * lc, (j + 1) * lc)
            g = g_ref[rs, :]
            for hh in range(M_HEADS):
                c_st, n_st, m_all = c_sc[hh], n_sc[hh], m_sc[hh]
                cs_ref[hh, j] = c_st
                ns_ref[hh, j] = n_st
                ms_ref[hh, j] = m_all
                m_st = m_all[:, 0:1]
                q = q_ref[rs, hh * dk:(hh + 1) * dk].astype(F32)
                ks = k_ref[rs, hh * dk:(hh + 1) * dk].astype(F32) * scale
                v = v_ref[rs, hh * dv:(hh + 1) * dv]
                gq = _mlstm_gates(g, hh, st * cps + j, lc)
                f = _mlstm_chunk(q, ks, v, gq, c_st, n_st, m_st, lc)
                h_ref[rs, hh * dv:(hh + 1) * dv] = _bf(f["num"] / f["nn"])
                kw = ks * f["w_k"]
                c_sc[hh] = f["decay"] * c_st + _dot(_bf(kw), f["vb"], TN)
                n_sc[hh] = f["decay"] * n_st + _colsum(kw)
                m_sc[hh] = jnp.broadcast_to(f["m_new"], (1, 128))

    sd = jax.ShapeDtypeStruct
    nh = M_HEADS
    return _pcall(
        body, name="mlstm_fwd", grid=(bsz, nc // cps),
        in_specs=[pl.BlockSpec((None, rows, nh * dk), lambda b, c: (b, c, 0)),
                  pl.BlockSpec((None, rows, nh * dk), lambda b, c: (b, c, 1)),
                  pl.BlockSpec((None, rows, nh * dv), lambda b, c: (b, c, V_OFF // (nh * dv))),
                  pl.BlockSpec((None, rows, 128), lambda b, c: (b, c, 0))],
        out_specs=[pl.BlockSpec((None, rows, nh * dv), lambda b, c: (b, c, 0)),
                   pl.BlockSpec((None, nh, cps, dk, dv), lambda b, c: (b, 0, c, 0, 0)),
                   pl.BlockSpec((None, nh, cps, 1, dk), lambda b, c: (b, 0, c, 0, 0)),
                   pl.BlockSpec((None, nh, cps, 1, 128), lambda b, c: (b, 0, c, 0, 0))],
        out_shape=[sd((bsz, lp, nh * dv), BF16), sd((bsz, nh, nc, dk, dv), F32),
                   sd((bsz, nh, nc, 1, dk), F32), sd((bsz, nh, nc, 1, 128), F32)],
        scratch_shapes=[pltpu.VMEM((nh, dk, dv), F32), pltpu.VMEM((nh, 1, dk), F32), pltpu.VMEM((nh, 1, 128), F32)],
        compiler_params=_cp(("parallel", "arbitrary")),
    )(qk3, qk3, p3, pg3)


def _mlstm_bwd(dp3, qk3, p3, pg3, dh3, cs, ns, ms):
    bsz, lp, _ = p3.shape
    lc = M_CHUNK
    nc = lp // lc
    dk, dv = 128, 256
    scale = dk ** -0.5

    cps = _chunks_per_step(nc)
    nst = nc // cps
    rows = cps * lc

    def body(dp_any, q_ref, k_ref, v_ref, g_ref, dh_ref, cs_ref, ns_ref, ms_ref,
             dv_ref, dqk_ref, dg_ref, dc_sc, dn_sc):
        t = pl.program_id(1)
        st = nst - 1 - t

        @pl.when(t == 0)
        def _():
            dc_sc[...] = jnp.zeros_like(dc_sc)
            dn_sc[...] = jnp.zeros_like(dn_sc)

        lane = lax.broadcasted_iota(jnp.int32, (lc, 128), 1)
        for j in reversed(range(cps)):
            rs = slice(j * lc, (j + 1) * lc)
            g = g_ref[rs, :]
            dgate = jnp.zeros((lc, 128), F32)
            for hh in range(M_HEADS):
                dgate = head(hh, j, rs, st * cps + j, g, lane, dgate, q_ref, k_ref, v_ref, dh_ref, cs_ref, ns_ref,
                             ms_ref, dv_ref, dqk_ref, dc_sc, dn_sc)
            dg_ref[rs, :] = dgate.astype(dg_ref.dtype)

    def head(hh, j, sl, c, g, lane, dgate, q_ref, k_ref, v_ref, dh_ref, cs_ref, ns_ref, ms_ref, dv_ref, dqk_ref,
             dc_sc, dn_sc):
        c_st, n_st = cs_ref[hh, j], ns_ref[hh, j]
        m_st = ms_ref[hh, j][:, 0:1]
        q = q_ref[sl, hh * dk:(hh + 1) * dk].astype(F32)
        ks = k_ref[sl, hh * dk:(hh + 1) * dk].astype(F32) * scale
        v = v_ref[sl, hh * dv:(hh + 1) * dv]
        dh = dh_ref[sl, hh * dv:(hh + 1) * dv].astype(F32)
        gq = _mlstm_gates(g, hh, c, lc)
        f = _mlstm_chunk(q, ks, v, gq, c_st, n_st, m_st, lc)
        eye, r2, c2, row, valid = gq["eye"], gq["r2"], gq["c2"], gq["row"], gq["valid"]
        w_intra, w_inter, s, nn, den = f["w_intra"], f["w_inter"], f["s"], f["nn"], f["den"]
        qb, kb, vb, cb, w_k, decay = f["qb"], f["kb"], f["vb"], f["cb"], f["w_k"], f["decay"]
        d_c, d_n = dc_sc[hh], dn_sc[hh]
        d_cb = _bf(d_c)

        hout = f["num"] / nn
        dnum = dh / nn
        d_nn = -jnp.sum(dh * hout, axis=1, keepdims=True) / nn
        dden = jnp.where(jnp.abs(den) > f["e"], d_nn * jnp.sign(den), 0.0)
        wdnum = w_inter * dnum
        wdden = w_inter * dden
        ds = _dot(_bf(dnum), vb, NT) + dden
        dsw = _bf(ds * w_intra)
        dq = _dot(dsw, kb) + _dot(_bf(wdnum), cb, NT) + wdden * n_st
        dkw = _dot(vb, d_cb, NT) + d_n
        dks = _dot(dsw, qb, TN) + dkw * w_k
        kw = ks * w_k
        dvv = _dot(_bf(s), _bf(dnum), TN) + _dot(_bf(kw), d_cb)
        dd = ds * s
        rs = jnp.sum(dd, axis=1, keepdims=True)
        cs_col = jnp.sum(jnp.where(eye, jnp.sum(dd, axis=0, keepdims=True), 0.0), axis=1, keepdims=True)
        dwi = jnp.sum(dnum * f["qc"], axis=1, keepdims=True) + dden * f["qn"]
        db = rs - cs_col + dwi * w_inter
        dli = cs_col
        ddecay = jnp.sum(jnp.sum(d_c * c_st, axis=1, keepdims=True), axis=0, keepdims=True) \
            + jnp.sum(d_n * n_st, axis=1, keepdims=True)
        dgl = jnp.sum(dkw * ks, axis=1, keepdims=True) * w_k
        dblast = ddecay * decay + jnp.sum(dgl, axis=0, keepdims=True)
        db = db - dgl + jnp.where(row == lc - 1, dblast, 0.0)
        dli = dli + dgl
        db_row = gq["to_row"](db)
        dlf = jnp.sum(jnp.where(c2 >= r2, db_row, 0.0), axis=1, keepdims=True)
        dlf = jnp.where(valid, dlf, 0.0)
        dgate = jnp.where(lane == hh, jnp.where(valid, dli, 0.0), dgate)
        dgate = jnp.where(lane == M_HEADS + hh, dlf / (1.0 + jnp.exp(gq["f_col"])), dgate)
        dqk_ref[sl, hh * dk:(hh + 1) * dk] = _bf(dq)
        dqk_ref[sl, (M_HEADS + hh) * dk:(M_HEADS + hh + 1) * dk] = _bf(dks * scale)
        dv_ref[sl, hh * dv:(hh + 1) * dv] = dvv.astype(dv_ref.dtype)
        dc_sc[hh] = decay * d_c + _dot(qb, _bf(wdnum), TN)
        dn_sc[hh] = decay * d_n + _colsum(q * wdden)
        return dgate

    sd = jax.ShapeDtypeStruct
    nh = M_HEADS
    rc = lambda c: nst - 1 - c
    return _pcall(
        body, name="mlstm_bwd", grid=(bsz, nst),
        in_specs=[pl.BlockSpec(memory_space=pl.ANY),
                  pl.BlockSpec((None, rows, nh * dk), lambda b, c: (b, rc(c), 0)),
                  pl.BlockSpec((None, rows, nh * dk), lambda b, c: (b, rc(c), 1)),
                  pl.BlockSpec((None, rows, nh * dv), lambda b, c: (b, rc(c), V_OFF // (nh * dv))),
                  pl.BlockSpec((None, rows, 128), lambda b, c: (b, rc(c), 0)),
                  pl.BlockSpec((None, rows, nh * dv), lambda b, c: (b, rc(c), 0)),
                  pl.BlockSpec((None, nh, cps, dk, dv), lambda b, c: (b, 0, rc(c), 0, 0)),
                  pl.BlockSpec((None, nh, cps, 1, dk), lambda b, c: (b, 0, rc(c), 0, 0)),
                  pl.BlockSpec((None, nh, cps, 1, 128), lambda b, c: (b, 0, rc(c), 0, 0))],
        out_specs=[pl.BlockSpec((None, rows, nh * dv), lambda b, c: (b, rc(c), V_OFF // (nh * dv))),
                   pl.BlockSpec((None, rows, 2 * nh * dk), lambda b, c: (b, rc(c), 0)),
                   pl.BlockSpec((None, rows, 128), lambda b, c: (b, rc(c), 0))],
        out_shape=[sd(dp3.shape, dp3.dtype), sd((bsz, lp, 2 * nh * dk), BF16), sd((bsz, lp, 128), dp3.dtype)],
        scratch_shapes=[pltpu.VMEM((nh, dk, dv), F32), pltpu.VMEM((nh, 1, dk), F32)],
        input_output_aliases={0: 0},
        compiler_params=_cp(("arbitrary", "arbitrary")),
    )(dp3, qk3, qk3, p3, pg3, dh3, cs, ns, ms)


def _headnorm(x):
    dv = x.shape[1] // M_HEADS
    xh, rs = [], []
    for h in range(M_HEADS):
        xx = x[:, h * dv:(h + 1) * dv]
        mu = jnp.mean(xx, axis=-1, keepdims=True)
        xc = xx - mu
        rstd = lax.rsqrt(jnp.mean(xc * xc, axis=-1, keepdims=True) + LN_EPS)
        xh.append(xc * rstd)
        rs.append(rstd)
    return jnp.concatenate(xh, axis=1), rs


def _mix_fwd(hm, p, ys5g, h0, gn, wmo_bf, wo_bf, lp):
    r, d = hm.shape
    tm = _row_tile(lp, 384)

    def body(hm_ref, o_ref, gs_ref, gm_ref, ys_ref, h0_ref, gn_ref, wmo_ref, wo_ref,
             ymin_ref, mix_ref, r1_ref):
        xhat, _ = _headnorm(hm_ref[...].astype(F32))
        ymin = _bf(_sig(o_ref[...].astype(F32)) * (xhat * gn_ref[...]))
        ymin_ref[...] = ymin
        ym = _dot(ymin, wmo_ref[...])
        mix = _bf(_sig(gs_ref[...].astype(F32)) * ys_ref[...].astype(F32) + _sig(gm_ref[...].astype(F32)) * ym)
        mix_ref[...] = mix
        r1_ref[...] = ALPHA * h0_ref[...] + _dot(mix, wo_ref[...])

    sd = jax.ShapeDtypeStruct
    row = pl.BlockSpec((tm, d), lambda i: (i, 0))
    return _pcall(
        body, name="mix_fwd", grid=(r // tm,),
        in_specs=[row, pl.BlockSpec((tm, d), lambda i: (i, O_OFF // d)), pl.BlockSpec((tm, d), lambda i: (i, GS_OFF // d)),
                  pl.BlockSpec((tm, d), lambda i: (i, GM_OFF // d)), row, row, _const((1, d)),
                  _resident((d, d)), _resident((d, d))],
        out_specs=[row] * 3,
        out_shape=[sd((r, d), BF16), sd((r, d), BF16), sd((r, d), F32)],
        compiler_params=_cp(("parallel",), 48),
    )(hm, p, p, p, ys5g, h0, gn, wmo_bf, wo_bf)


def _mix_bwd(dr1, wo_bf, wmo_bf, p, ys5g, ymin, hm, gn, lp):
    r, d = hm.shape
    tm = _row_tile(lp, 384)
    dv = d // M_HEADS

    def body(dr1_ref, wo_ref, wmo_ref, o_ref, gs_ref, gm_ref, ys_ref, ym_ref, hm_ref, gn_ref,
             dp_ref, dys_ref, dym_ref, dhm_ref, dgn_ref):
        i = pl.program_id(0)

        @pl.when(i == 0)
        def _():
            dgn_ref[...] = jnp.zeros_like(dgn_ref)

        dmix = _dot(_bf(dr1_ref[...]), wo_ref[...], NT)
        sgs, sgm, so = (_sig(gs_ref[...].astype(F32)), _sig(gm_ref[...].astype(F32)), _sig(o_ref[...].astype(F32)))
        dys_ref[...] = _bf(dmix * sgs)
        dp_ref[:, d:2 * d] = _bf(dmix * ys_ref[...].astype(F32) * sgs * (1.0 - sgs))
        dym = dmix * sgm
        dym_ref[...] = _bf(dym)
        ym = _dot(ym_ref[...], wmo_ref[...])
        dp_ref[:, 2 * d:3 * d] = _bf(dmix * ym * sgm * (1.0 - sgm))
        dymin = _dot(_bf(dym), wmo_ref[...], NT)
        xhat, rs = _headnorm(hm_ref[...].astype(F32))
        gn_ = gn_ref[...]
        dp_ref[:, 0:d] = _bf(dymin * (xhat * gn_) * so * (1.0 - so))
        dhn = dymin * so
        dgn_ref[...] += _colsum(dhn * xhat)
        dxh = dhn * gn_
        for h in range(M_HEADS):
            sl = slice(h * dv, (h + 1) * dv)
            a, xh = dxh[:, sl], xhat[:, sl]
            m1 = jnp.mean(a, axis=-1, keepdims=True)
            m2 = jnp.mean(a * xh, axis=-1, keepdims=True)
            dhm_ref[:, sl] = _bf(rs[h] * (a - m1 - xh * m2))

    sd = jax.ShapeDtypeStruct
    row = pl.BlockSpec((tm, d), lambda i: (i, 0))
    vec = _const((1, d))
    return _pcall(
        body, name="mix_bwd", grid=(r // tm,),
        in_specs=[row, _resident((d, d)), _resident((d, d)),
                  pl.BlockSpec((tm, d), lambda i: (i, O_OFF // d)), pl.BlockSpec((tm, d), lambda i: (i, GS_OFF // d)),
                  pl.BlockSpec((tm, d), lambda i: (i, GM_OFF // d)), row, row, row, vec],
        out_specs=[pl.BlockSpec((tm, 3 * d), lambda i: (i, 0)), row, row, row, vec],
        out_shape=[sd((r, NP), BF16), sd((r, d), BF16), sd((r, d), BF16), sd((r, d), BF16), sd((1, d), F32)],
        compiler_params=_cp(("arbitrary",), 56),
    )(dr1, wo_bf, wmo_bf, p, p, p, ys5g, ymin, hm, gn)


def _mlp_fwd(r1, tgt, g1, b1, wup_g, wdn_bf, bup, g2, b2, lp):
    r, d = r1.shape
    tm = _row_tile(lp, 384)
    tps = lp // tm
    nf = wup_g.shape[0]

    def body(r1_ref, t_ref, g1_ref, b1_ref, wup_ref, wdn_ref, bup_ref, g2_ref, b2_ref,
             dr2_ref, h1b_ref, act_ref, loss_ref, dg2_ref, db2_ref):
        i = pl.program_id(0)

        @pl.when(i == 0)
        def _():
            loss_ref[...] = jnp.zeros_like(loss_ref)
            dg2_ref[...] = jnp.zeros_like(dg2_ref)
            db2_ref[...] = jnp.zeros_like(db2_ref)

        h1, _, _ = _ln_fwd(r1_ref[...], g1_ref[...], b1_ref[...])
        h1b = _bf(h1)
        h1b_ref[...] = h1b
        ff = jnp.zeros((tm, d), F32)
        for s in range(nf):
            up = _dot(h1b, wup_ref[s]) + bup_ref[:, s * d:(s + 1) * d]
            a = jnp.maximum(up, 0.0)
            a = _bf(a * a)
            act_ref[:, s * d:(s + 1) * d] = a
            ff = ff + _dot(a, wdn_ref[s * d:(s + 1) * d, :])
        r2 = ALPHA * h1 + ff
        g2 = g2_ref[...]
        y, xhat, rstd = _ln_fwd(r2, g2, b2_ref[...])
        t = (i % tps) * tm + lax.broadcasted_iota(jnp.int32, (tm, 1), 0)
        diff = jnp.where(t >= PAD + N_META, y - t_ref[...], 0.0)
        loss_ref[...] += 0.5 / d * jnp.sum(jnp.sum(diff * diff, axis=1, keepdims=True), axis=0, keepdims=True)
        dy = diff * (1.0 / d)
        dg2_ref[...] += _colsum(dy * xhat)
        db2_ref[...] += _colsum(dy)
        dr2_ref[...] = _ln_bwd(dy, xhat, rstd, g2)

    sd = jax.ShapeDtypeStruct
    row = pl.BlockSpec((tm, d), lambda i: (i, 0))
    vec = _const((1, d))
    return _pcall(
        body, name="mlp_fwd", grid=(r // tm,),
        in_specs=[row, row, vec, vec, _resident(wup_g.shape), _resident(wdn_bf.shape), _const((1, nf * d)), vec, vec],
        out_specs=[row, row, pl.BlockSpec((tm, nf * d), lambda i: (i, 0)), _const((1, 128)), vec, vec],
        out_shape=[sd((r, d), F32), sd((r, d), BF16), sd((r, nf * d), BF16), sd((1, 128), F32), sd((1, d), F32),
                   sd((1, d), F32)],
        compiler_params=_cp(("arbitrary",), 56),
    )(r1, tgt, g1, b1, wup_g, wdn_bf, bup, g2, b2)


def _mlp_bwd(act, dr2, r1, g1, wup_g, wdn_bf, lp):
    r, d = dr2.shape
    tm = _row_tile(lp, 384)
    nf = wup_g.shape[0]

    def body(act_ref, dr2_ref, r1_ref, g1_ref, wup_ref, wdn_ref, dr1_ref, dup_ref, dbup_ref, dg1_ref, db1_ref):
        i = pl.program_id(0)

        @pl.when(i == 0)
        def _():
            dbup_ref[...] = jnp.zeros_like(dbup_ref)
            dg1_ref[...] = jnp.zeros_like(dg1_ref)
            db1_ref[...] = jnp.zeros_like(db1_ref)

        dr2 = dr2_ref[...]
        dr2b = _bf(dr2)
        acc = ALPHA * dr2
        for s in range(nf):
            dact = _dot(dr2b, wdn_ref[s * d:(s + 1) * d, :], NT)
            dup = dact * (2.0 * jnp.sqrt(act_ref[:, s * d:(s + 1) * d].astype(F32)))
            dbup_ref[:, s * d:(s + 1) * d] += _colsum(dup)
            dupb = _bf(dup)
            dup_ref[:, s * d:(s + 1) * d] = dupb
            acc = acc + _dot(dupb, wup_ref[s], NT)
        g1 = g1_ref[...]
        _, xhat1, rstd1 = _ln_fwd(r1_ref[...], g1, 0.0)
        dr1_ref[...] = _ln_bwd(acc, xhat1, rstd1, g1)
        dg1_ref[...] += _colsum(acc * xhat1)
        db1_ref[...] += _colsum(acc)

    sd = jax.ShapeDtypeStruct
    row = pl.BlockSpec((tm, d), lambda i: (i, 0))
    vec = _const((1, d))
    return _pcall(
        body, name="mlp_bwd", grid=(r // tm,),
        in_specs=[pl.BlockSpec((tm, nf * d), lambda i: (i, 0)), row, row, vec, _resident(wup_g.shape),
                  _resident(wdn_bf.shape)],
        out_specs=[row, pl.BlockSpec((tm, nf * d), lambda i: (i, 0)), _const((1, nf * d)), vec, vec],
        out_shape=[sd((r, d), F32), sd((r, nf * d), BF16), sd((1, nf * d), F32), sd((1, d), F32), sd((1, d), F32)],
        compiler_params=_cp(("arbitrary",), 56),
    )(act, dr2, r1, g1, wup_g, wdn_bf)


def _s5_block_mats(bb_re_t, bb_im_t, c_re, c_im, ap_re, ap_im):
    ng = c_re.shape[0]
    gl = ng // S5_KCH
    eye = jnp.eye(gl, dtype=F32)

    def bmat(bt):
        bb = jnp.transpose(bt, (1, 0, 2)).reshape(S5_KCH, gl, S5_GROUP, S5_STATE)
        return jnp.einsum("kghp,gj->kghjp", bb, eye).reshape(S5_KCH, gl * S5_GROUP, gl * S5_STATE)

    def cmat(c):
        cc = c.reshape(S5_KCH, gl, S5_GROUP, S5_STATE)
        return jnp.einsum("kghp,gj->kjpgh", cc, eye).reshape(S5_KCH, gl * S5_STATE, gl * S5_GROUP)

    def pw(a):
        return jnp.transpose(a.reshape(8, S5_KCH, gl * S5_STATE), (1, 0, 2))

    bk = jnp.concatenate([bmat(bb_re_t), bmat(bb_im_t)], axis=-1)
    apow = jnp.concatenate([pw(ap_re), pw(ap_im)], axis=-1)
    return _bf(bk), _bf(cmat(c_re)), _bf(cmat(c_im)), apow


def _s5_block_grads(dbk, dcre, dcim, da):
    gl = dbk.shape[1] // S5_GROUP
    ng = gl * S5_KCH
    eye = jnp.eye(gl, dtype=F32)
    hw = gl * S5_STATE

    def bpart(x):
        x = x.reshape(S5_KCH, gl, S5_GROUP, gl, S5_STATE)
        x = jnp.einsum("kghjp,gj->kghp", x, eye).reshape(ng, S5_GROUP, S5_STATE)
        return jnp.transpose(x, (1, 0, 2))

    def cpart(x):
        x = x.reshape(S5_KCH, gl, S5_STATE, gl, S5_GROUP)
        return jnp.einsum("kjpgh,gj->kghp", x, eye).reshape(ng, S5_GROUP, S5_STATE)

    return (bpart(dbk[..., :hw]), bpart(dbk[..., hw:]), cpart(dcre), cpart(dcim),
            da[:, 0, :hw].reshape(ng, S5_STATE), da[:, 0, hw:].reshape(ng, S5_STATE))


def _tie(a, tok):
    return a if tok is None else a + tok[0, 0]


def _local_step(x, tgt, w, early=None, late=None, ready=None):
    ready = ready or (lambda names, g: None)
    bsz, seq, d = x.shape
    lp = PAD + N_META + seq
    r = bsz * lp
    tgtp = jnp.concatenate([jnp.zeros((bsz, PAD + N_META, d), F32), tgt], axis=1).reshape(r, d)

    h0, h0b = _ln0_fwd(x, w["meta_tokens"], w["ln0_g"], w["ln0_b"])
    b_re_t = jnp.transpose(w["s5_b_re"], (2, 0, 1))
    b_im_t = jnp.transpose(w["s5_b_im"], (2, 0, 1))
    ap_re, ap_im, bb_re_t, bb_im_t = _s5_prep(w["s5_lambda_re"], w["s5_lambda_im"], w["s5_log_dt"], b_re_t, b_im_t)
    bk, cre, cim, apow = _s5_block_mats(bb_re_t, bb_im_t, w["s5_c_re"], w["s5_c_im"], ap_re, ap_im)
    apow_rev = jnp.flip(apow, axis=1)
    if early is not None:
        w = {**w, **early((h0, tgtp, bk, cre, cim, apow_rev))}
    p, pg = _inproj(h0b, w["w_in"], w["b_in"], lp)
    p3 = p.reshape(bsz, lp, NP)
    pg3 = pg.reshape(bsz, lp, 128)

    y_s5, xs = _s5_fwd(p3, bk, cre, cim, apow, w["s5_d"])
    sw = y_s5.shape[-1]
    if late is not None:
        w = {**w, **late(y_s5)}
    gy, z, ys5g = _glu_fwd(y_s5.reshape(r, sw), w["s5_w_glu"], lp)

    pre3, qk3 = _conv_fwd(p3, w["qk_conv_w"], w["qk_conv_b"])
    hm3, cs, ns, ms = _mlstm_fwd(qk3, p3, pg3)
    hm = hm3.reshape(r, d)
    ymin, mix, r1 = _mix_fwd(hm, p, ys5g, h0, w["m_norm_g"], w["m_w_out"], w["w_o"], lp)
    dr2, h1b, act, loss, dg2, db2 = _mlp_fwd(r1, tgtp, w["ln1_g"], w["ln1_b"], w["w_up"], w["w_down"], w["b_up"],
                                             w["ln2_g"], w["ln2_b"], lp)

    g = {"ln2_g": dg2, "ln2_b": db2}
    dr1, dup, g["b_up"], g["ln1_g"], g["ln1_b"] = _mlp_bwd(act, dr2, r1, w["ln1_g"], w["w_up"], w["w_down"], lp)
    g["w_down"] = _mm_tn(act, dr2, name="dw_down")
    g["w_up"] = _mm_tn(h1b, dup, name="dw_up", split=w["w_up"].shape[0])
    tok = ready(("w_down", "w_up"), g)
    dp, dys5g, dym, dhm, g["m_norm_g"] = _mix_bwd(
        dr1, w["w_o"], w["m_w_out"], p, ys5g, ymin, hm, _tie(w["m_norm_g"], tok), lp)
    g["w_o"] = _mm_tn(mix, dr1, name="dw_o")
    g["m_w_out"] = _mm_tn(ymin, dym, name="dw_mout")

    dp3 = dp.reshape(bsz, lp, NP)
    dp3, dqk3, dgate = _mlstm_bwd(dp3, qk3, p3, pg3, dhm.reshape(bsz, lp, d), cs, ns, ms)
    dp3, g["qk_conv_w"], g["qk_conv_b"] = _conv_bwd(dp3, p3, dqk3, pre3, w["qk_conv_w"])
    dz, dys5 = _glu_bwd(dys5g, z, y_s5.reshape(r, sw), w["s5_w_glu"], lp)
    g["s5_w_glu"] = _mm_tn(gy, dz, name="dw_glu", split=w["s5_w_glu"].shape[0])
    tok = ready(("s5_w_glu", "m_w_out", "w_o"), g)
    dp3, dbk, dcre, dcim, da, g["s5_d"] = _s5_bwd(dp3, p3, dys5.reshape(bsz, lp, sw), xs, bk, cre, cim, apow_rev,
                                                 _tie(w["s5_d"], tok))
    dbb_re_t, dbb_im_t, g["s5_c_re"], g["s5_c_im"], da_re, da_im = _s5_block_grads(dbk, dcre, dcim, da)
    g["s5_lambda_re"], g["s5_lambda_im"], g["s5_log_dt"], gb_re_t, gb_im_t = _s5_prep_bwd(
        w["s5_lambda_re"], w["s5_lambda_im"], w["s5_log_dt"], b_re_t, b_im_t, da_re, da_im, dbb_re_t, dbb_im_t)
    g["s5_b_re"] = jnp.transpose(gb_re_t, (1, 2, 0))
    g["s5_b_im"] = jnp.transpose(gb_im_t, (1, 2, 0))

    dp3 = lax.dynamic_update_slice(dp3, dgate, (0, 0, G_OFF))
    dp = dp3.reshape(r, NP)
    g["w_in"], g["b_in"] = _mm_tn(h0b, dp, name="dw_in", colsum=True)
    tok = ready(("w_in",), g)
    dpw = _mm_nt(dp, w["w_in"], lp, name="dh0", dep=tok)
    grad_x, g["ln0_g"], g["ln0_b"], g["meta_tokens"] = _ln0_bwd(x, w["meta_tokens"], dr1, dpw, w["ln0_g"])
    return loss, grad_x, g


_ANY = pl.BlockSpec(memory_space=pl.ANY)
_MESH = pl.DeviceIdType.MESH


def _place():
    return lax.axis_index("x"), lax.axis_index("y"), lax.axis_index("c")


def _gather_chips(shards):
    n = len(shards)

    def body(*refs):
        ins, outs = refs[:n], refs[n:2 * n]
        send, recv, loc = refs[2 * n:]
        x, y, c = _place()
        me = 2 * x + y
        peers = [(1 - x, y), (x, 1 - y), (1 - x, 1 - y)]

        def rc(a, k, slot):
            px, py = peers[k]
            return pltpu.make_async_remote_copy(src_ref=ins[a], dst_ref=outs[a].at[slot], send_sem=send.at[a, k],
                                                recv_sem=recv.at[a, k], device_id=(px, py, c), device_id_type=_MESH)

        own = [pltpu.make_async_copy(ins[a], outs[a].at[me], loc.at[a]) for a in range(n)]
        for cp in own:
            cp.start()
        out = [rc(a, k, me) for a in range(n) for k in range(3)]
        for cp in out:
            cp.start()
        for a in range(n):
            for k in range(3):
                rc(a, k, 2 * peers[k][0] + peers[k][1]).wait_recv()
        for cp in out:
            cp.wait_send()
        for cp in own:
            cp.wait()

    return _pcall(
        body, name="gather_chips", in_specs=[_ANY] * n, out_specs=[_ANY] * n,
        out_shape=[jax.ShapeDtypeStruct((4,) + s.shape, s.dtype) for s in shards],
        scratch_shapes=[pltpu.SemaphoreType.DMA((n, 3)), pltpu.SemaphoreType.DMA((n, 3)), pltpu.SemaphoreType.DMA((n,))],
    )(*shards)


_HBM = pl.BlockSpec(memory_space=pltpu.HBM)
_SEM = pl.BlockSpec(memory_space=pltpu.SEMAPHORE)
_EFFECT = pltpu.SideEffectType.DATAFLOW_SIDE_EFFECTING


def _xchg_copies(srcs, lands, send, recv, scatter):
    x, y, c = _place()
    me = 2 * x + y
    peers = [(1 - x, y), (x, 1 - y), (1 - x, 1 - y)]
    out = []
    for a in range(len(srcs)):
        for k, (px, py) in enumerate(peers):
            src = srcs[a].at[2 * px + py] if scatter else srcs[a]
            dst = lands[a].at[k] if scatter else lands[a].at[me]
            out.append(pltpu.make_async_remote_copy(src_ref=src, dst_ref=dst, send_sem=send.at[3 * a + k],
                                                    recv_sem=recv.at[3 * a + k], device_id=(px, py, c),
                                                    device_id_type=_MESH))
    return out


def _xchg_start(srcs, lands, *, name, scatter, dep=None):
    n = len(srcs)
    deps = [] if dep is None else [dep]
    nd = len(deps)

    def body(*refs):
        send, recv = refs[2 * n + nd], refs[2 * n + nd + 1]
        for cp in _xchg_copies(refs[:n], refs[n:2 * n], send, recv, scatter):
            cp.start()
        refs[-1][...] = jnp.zeros_like(refs[-1])

    hbm = lambda a: pltpu.HBM(a.shape, a.dtype)
    con = lambda a: pltpu.with_memory_space_constraint(a, pltpu.HBM)
    res = _pcall(
        body, name=name, in_specs=[_HBM] * (2 * n) + [_ANY] * nd,
        out_specs=[_SEM, _SEM] + [_HBM] * (2 * n) + [pl.BlockSpec(memory_space=pltpu.VMEM)],
        out_shape=[pltpu.SemaphoreType.DMA((3 * n,)), pltpu.SemaphoreType.DMA((3 * n,))]
        + [hbm(a) for a in srcs] + [hbm(a) for a in lands] + [jax.ShapeDtypeStruct((8, 128), F32)],
        input_output_aliases={i: 2 + i for i in range(2 * n)},
        compiler_params=pltpu.CompilerParams(has_side_effects=_EFFECT),
    )(*[con(a) for a in srcs], *[con(a) for a in lands], *deps)
    return res[0], res[1], list(res[2:2 + n]), list(res[2 + n:2 + 2 * n]), res[-1]


def _xchg_wait(send, recv, srcs, lands, after, *, name, scatter):
    n = len(srcs)
    afters = list(after) if isinstance(after, (list, tuple)) else [after]

    def body(*refs):
        s_ref, r_ref = refs[2 * n], refs[2 * n + 1]
        for cp in _xchg_copies(refs[:n], refs[n:2 * n], s_ref, r_ref, scatter):
            cp.wait_send()
            cp.wait_recv()

    hbm = lambda a: pltpu.HBM(a.shape, a.dtype)
    res = _pcall(
        body, name=name, in_specs=[_HBM] * (2 * n) + [_SEM, _SEM] + [_ANY] * len(afters),
        out_specs=[_HBM] * (2 * n),
        out_shape=[hbm(a) for a in srcs] + [hbm(a) for a in lands],
        input_output_aliases={i: i for i in range(2 * n)},
        compiler_params=pltpu.CompilerParams(has_side_effects=_EFFECT),
    )(*srcs, *lands, send, recv, *afters)
    return list(res[:n]), list(res[n:])


def _swap_cores(arrs, name="swap_cores"):
    n = len(arrs)

    def body(*refs):
        ins, outs = refs[:n], refs[n:2 * n]
        send, recv = refs[2 * n:]
        x, y, c = _place()
        cps = [pltpu.make_async_remote_copy(src_ref=ins[a], dst_ref=outs[a], send_sem=send.at[a], recv_sem=recv.at[a],
                                            device_id=(x, y, 1 - c), device_id_type=_MESH) for a in range(n)]
        for cp in cps:
            cp.start()
        for cp in cps:
            cp.wait_recv()
        for cp in cps:
            cp.wait_send()

    return _pcall(
        body, name=name, in_specs=[_ANY] * n, out_specs=[_ANY] * n,
        out_shape=[jax.ShapeDtypeStruct(s.shape, s.dtype) for s in arrs],
        scratch_shapes=[pltpu.SemaphoreType.DMA((n,)), pltpu.SemaphoreType.DMA((n,))],
    )(*arrs)


def _allreduce_small(v, dep=None):
    rows = v.shape[0]
    half = rows // 2
    assert half % 8 == 0 and 2 * half == rows
    deps = [] if dep is None else [dep]

    def body(v_ref, *rest):
        out_ref, sib_ref, pair_ref, slots_ref, send, recv = rest[len(deps):]
        x, y, c = _place()
        chip = 2 * x + y
        sibling = (x, y, 1 - c)
        peers = [(1 - x, y), (x, 1 - y), (1 - x, 1 - y)]
        mine = pl.ds(pl.multiple_of(c * half, 8), half)

        first = pltpu.make_async_remote_copy(src_ref=v_ref, dst_ref=sib_ref, send_sem=send.at[0], recv_sem=recv.at[0],
                                             device_id=sibling, device_id_type=_MESH)
        first.start()
        first.wait_recv()
        pair_ref[...] = v_ref[...] + sib_ref[...]
        slots_ref[chip] = pair_ref[mine, :]
        cross = [pltpu.make_async_remote_copy(src_ref=pair_ref.at[mine], dst_ref=slots_ref.at[chip],
                                              send_sem=send.at[1 + k], recv_sem=recv.at[1 + k],
                                              device_id=(px, py, c), device_id_type=_MESH)
                 for k, (px, py) in enumerate(peers)]
        for cp in cross:
            cp.start()
        for cp in cross:
            cp.wait_recv()
        out_ref[mine, :] = ((slots_ref[0] + slots_ref[1]) + slots_ref[2]) + slots_ref[3]
        last = pltpu.make_async_remote_copy(src_ref=out_ref.at[mine], dst_ref=out_ref.at[mine], send_sem=send.at[4],
                                            recv_sem=recv.at[4], device_id=sibling, device_id_type=_MESH)
        last.start()
        last.wait_recv()
        first.wait_send()
        for cp in cross:
            cp.wait_send()
        last.wait_send()

    vm = pl.BlockSpec(memory_space=pltpu.VMEM)
    return _pcall(
        body, name="allreduce_small", in_specs=[vm] + [_ANY] * len(deps), out_specs=vm,
        out_shape=jax.ShapeDtypeStruct((rows, 128), F32),
        scratch_shapes=[pltpu.VMEM((rows, 128), F32), pltpu.VMEM((rows, 128), F32), pltpu.VMEM((4, half, 128), F32),
                        pltpu.SemaphoreType.DMA((5,)), pltpu.SemaphoreType.DMA((5,))],
        compiler_params=_cp(None, 40),
    )(v, *deps)


def _sum_slots(parts, land, chip):
    ns, rows, cols = land.shape
    tm = _row_tile(rows, 256, 8)

    def body(chip_ref, own_ref, a_ref, o_ref):
        o_ref[...] = ((own_ref[...] + a_ref[0]) + a_ref[1]) + a_ref[2]

    return _pcall(
        body, name="sum_slots",
        grid_spec=pltpu.PrefetchScalarGridSpec(
            num_scalar_prefetch=1, grid=(rows // tm,),
            in_specs=[pl.BlockSpec((None, tm, cols), lambda i, c: (c[0], i, 0)),
                      pl.BlockSpec((ns, tm, cols), lambda i, c: (0, i, 0))],
            out_specs=pl.BlockSpec((tm, cols), lambda i, c: (i, 0))),
        out_shape=jax.ShapeDtypeStruct((rows, cols), F32),
        compiler_params=_cp(("parallel",), 40),
    )(jnp.reshape(chip, (1,)).astype(jnp.int32), parts, land)


def _adamw(w, m, v, g0, g1=None):
    rows, cols = w.shape[-2:]
    lead = w.ndim == 3
    tm = _row_tile(rows, max(8, (1 << 20) // (4 * cols)), 8)
    c1 = 1.0 - ADAM_B1 ** ADAM_STEP
    c2 = 1.0 - ADAM_B2 ** ADAM_STEP
    two = g1 is not None

    def body(*refs):
        w_ref, m_ref, v_ref, g0_ref = refs[:4]
        g_ref, d_ref, nm_ref, nv_ref = refs[-4:]
        g = g0_ref[...]
        if two:
            g = g + refs[4][...]
        nm = ADAM_B1 * m_ref[...] + (1.0 - ADAM_B1) * g
        nv = ADAM_B2 * v_ref[...] + (1.0 - ADAM_B2) * (g * g)
        g_ref[...] = g
        nm_ref[...] = nm
        nv_ref[...] = nv
        d_ref[...] = -ADAM_LR * ((nm / c1) / (jnp.sqrt(nv / c2) + ADAM_EPS) + ADAM_WD * w_ref[...])

    blk = pl.BlockSpec((tm, cols), lambda i: (i, 0))
    wblk = pl.BlockSpec((None, tm, cols), lambda i: (0, i, 0)) if lead else blk
    ins = [w, m, v, g0] + ([g1] if two else [])
    return _pcall(
        body, name="adamw", grid=(rows // tm,), in_specs=[wblk] * 3 + [blk] * (len(ins) - 3), out_specs=[wblk] * 4,
        out_shape=[jax.ShapeDtypeStruct(w.shape, F32)] * 4,
        compiler_params=_cp(("parallel",), 40),
    )(*ins)


_BIG = ("w_in", "s5_w_glu", "m_w_out", "w_o", "w_up", "w_down")
_SMALL = ("ln0_g", "ln0_b", "b_in", "qk_conv_b", "s5_lambda_re", "s5_lambda_im", "s5_log_dt", "s5_b_re", "s5_b_im",
          "s5_c_re", "s5_c_im", "s5_d", "m_norm_g", "ln1_g", "ln1_b", "b_up", "ln2_g", "ln2_b")
_SMALL_SHARDED = ("meta_tokens", "qk_conv_w")
_ORDER = ("meta_tokens", "ln0_g", "ln0_b", "w_in", "b_in", "qk_conv_w", "qk_conv_b", "s5_lambda_re", "s5_lambda_im",
          "s5_log_dt", "s5_b_re", "s5_b_im", "s5_c_re", "s5_c_im", "s5_d", "s5_w_glu", "m_norm_g", "m_w_out", "w_o",
          "ln1_g", "ln1_b", "w_up", "b_up", "w_down", "ln2_g", "ln2_b")


def _pack(arrs):
    flat = jnp.concatenate([a.reshape(-1) for a in arrs])
    n = flat.shape[0]
    rows = -(-n // 2048) * 16
    return jnp.pad(flat, (0, rows * 128 - n)).reshape(rows, 128)


def _unpack(packed, shapes):
    flat = packed.reshape(-1)
    out, off = [], 0
    for s in shapes:
        n = math.prod(s)
        out.append(flat[off:off + n].reshape(s))
        off += n
    return out


def kernel(x, meta_tokens, ln0_g, ln0_b, w_in, b_in, qk_conv_w, qk_conv_b, s5_lambda_re, s5_lambda_im, s5_log_dt, s5_b_re, s5_b_im, s5_c_re, s5_c_im, s5_d, s5_w_glu, m_norm_g, m_w_out, w_o, ln1_g, ln1_b, w_up, b_up, w_down, ln2_g, ln2_b, loss_target, m_meta_tokens, m_ln0_g, m_ln0_b, m_w_in, m_b_in, m_qk_conv_w, m_qk_conv_b, m_s5_lambda_re, m_s5_lambda_im, m_s5_log_dt, m_s5_b_re, m_s5_b_im, m_s5_c_re, m_s5_c_im, m_s5_d, m_s5_w_glu, m_m_norm_g, m_m_w_out, m_w_o, m_ln1_g, m_ln1_b, m_w_up, m_b_up, m_w_down, m_ln2_g, m_ln2_b, v_meta_tokens, v_ln0_g, v_ln0_b, v_w_in, v_b_in, v_qk_conv_w, v_qk_conv_b, v_s5_lambda_re, v_s5_lambda_im, v_s5_log_dt, v_s5_b_re, v_s5_b_im, v_s5_c_re, v_s5_c_im, v_s5_d, v_s5_w_glu, v_m_norm_g, v_m_w_out, v_w_o, v_ln1_g, v_ln1_b, v_w_up, v_b_up, v_w_down, v_ln2_g, v_ln2_b):
    wts = dict(meta_tokens=meta_tokens, ln0_g=ln0_g, ln0_b=ln0_b, w_in=w_in, b_in=b_in, qk_conv_w=qk_conv_w,
               qk_conv_b=qk_conv_b, s5_lambda_re=s5_lambda_re, s5_lambda_im=s5_lambda_im, s5_log_dt=s5_log_dt,
               s5_b_re=s5_b_re, s5_b_im=s5_b_im, s5_c_re=s5_c_re, s5_c_im=s5_c_im, s5_d=s5_d, s5_w_glu=s5_w_glu,
               m_norm_g=m_norm_g, m_w_out=m_w_out, w_o=w_o, ln1_g=ln1_g, ln1_b=ln1_b, w_up=w_up, b_up=b_up,
               w_down=w_down, ln2_g=ln2_g, ln2_b=ln2_b)
    mom = dict(meta_tokens=m_meta_tokens, ln0_g=m_ln0_g, ln0_b=m_ln0_b, w_in=m_w_in, b_in=m_b_in, qk_conv_w=m_qk_conv_w,
               qk_conv_b=m_qk_conv_b, s5_lambda_re=m_s5_lambda_re, s5_lambda_im=m_s5_lambda_im, s5_log_dt=m_s5_log_dt,
               s5_b_re=m_s5_b_re, s5_b_im=m_s5_b_im, s5_c_re=m_s5_c_re, s5_c_im=m_s5_c_im, s5_d=m_s5_d,
               s5_w_glu=m_s5_w_glu, m_norm_g=m_m_norm_g, m_w_out=m_m_w_out, w_o=m_w_o, ln1_g=m_ln1_g, ln1_b=m_ln1_b,
               w_up=m_w_up, b_up=m_b_up, w_down=m_w_down, ln2_g=m_ln2_g, ln2_b=m_ln2_b)
    var = dict(meta_tokens=v_meta_tokens, ln0_g=v_ln0_g, ln0_b=v_ln0_b, w_in=v_w_in, b_in=v_b_in, qk_conv_w=v_qk_conv_w,
               qk_conv_b=v_qk_conv_b, s5_lambda_re=v_s5_lambda_re, s5_lambda_im=v_s5_lambda_im, s5_log_dt=v_s5_log_dt,
               s5_b_re=v_s5_b_re, s5_b_im=v_s5_b_im, s5_c_re=v_s5_c_re, s5_c_im=v_s5_c_im, s5_d=v_s5_d,
               s5_w_glu=v_s5_w_glu, m_norm_g=v_m_norm_g, m_w_out=v_m_w_out, w_o=v_w_o, ln1_g=v_ln1_g, ln1_b=v_ln1_b,
               w_up=v_w_up, b_up=v_b_up, w_down=v_w_down, ln2_g=v_ln2_g, ln2_b=v_ln2_b)
    d = x.shape[-1]
    chip = 2 * lax.axis_index("x") + lax.axis_index("y")

    gw = dict(zip(_SMALL_SHARDED, _gather_chips([meta_tokens, qk_conv_w[0]])))
    own_w_in = _bf(w_in[0])
    fsend, frecv, fsrc, fland, ftok = _xchg_start([own_w_in], [lax.empty((4,) + own_w_in.shape, BF16)],
                                                  name="gather_w_in_start", scatter=False, dep=gw["qk_conv_w"])
    late_names = tuple(n for n in _BIG if n != "w_in")
    cat = lambda a: jnp.transpose(a, (1, 0, 2)).reshape(a.shape[1], 4 * a.shape[2])
    w = dict(
        meta_tokens=cat(gw["meta_tokens"]), ln0_g=ln0_g[None], ln0_b=_tie(ln0_b[None], ftok),
        qk_conv_w=cat(gw["qk_conv_w"]), qk_conv_b=qk_conv_b,
        s5_lambda_re=s5_lambda_re[0], s5_lambda_im=s5_lambda_im[0], s5_log_dt=s5_log_dt[0][:, None],
        s5_b_re=s5_b_re[0], s5_b_im=s5_b_im[0], s5_c_re=s5_c_re[0], s5_c_im=s5_c_im[0], s5_d=s5_d,
        m_norm_g=m_norm_g, ln1_g=ln1_g, ln1_b=ln1_b, b_up=b_up, ln2_g=ln2_g, ln2_b=ln2_b)
    in_flight = {}

    def place_own(src, land):
        return lax.dynamic_update_slice(land, src[None], (chip,) + (0,) * src.ndim)

    small_names = _SMALL + _SMALL_SHARDED

    def view(n, a):
        return jnp.swapaxes(a, -1, -2) if n in ("s5_b_re", "s5_b_im") else a

    small_wmv = [_pack([view(n, dct[n]) for n in small_names]) for dct in (wts, mom, var)]

    def early(after):
        src, land = _xchg_wait(fsend, frecv, fsrc, fland, tuple(after) + tuple(small_wmv), name="gather_w_in_wait",
                               scatter=False)
        late_src = [_bf(wts[n][0]) for n in late_names]
        st = _xchg_start(late_src, [lax.empty((4,) + a.shape, a.dtype) for a in late_src], name="gather_late_start",
                         scatter=False, dep=src[0])
        in_flight["late"] = st[:4]
        return dict(w_in=_w_in_from_slots(place_own(src[0], land[0]), IN_CHUNK), b_in=_tie(_to_pad_cols(b_in), st[4]))

    def late(after):
        src, land = _xchg_wait(*in_flight["late"], after, name="gather_late_wait", scatter=False)
        full = {n: place_own(s, ld) for n, s, ld in zip(late_names, src, land)}
        return dict(s5_w_glu=full["s5_w_glu"], m_w_out=full["m_w_out"].reshape(d, d), w_o=full["w_o"].reshape(d, d),
                    w_up=full["w_up"], w_down=full["w_down"].reshape(4 * d, d))

    flying = []

    def ready(names, g):
        parts = dict(
            w_in=lambda: _slots_from_w_in(g["w_in"][0]), s5_w_glu=lambda: g["s5_w_glu"],
            m_w_out=lambda: g["m_w_out"].reshape(4, d // 4, d), w_o=lambda: g["w_o"].reshape(4, d // 4, d),
            w_up=lambda: g["w_up"], w_down=lambda: g["w_down"].reshape(4, d, d))
        src = [parts[n]() for n in names]
        land = [lax.empty((3,) + a.shape[1:], a.dtype) for a in src]
        st = _xchg_start(src, land, name="scatter_" + names[0] + "_start", scatter=True)
        flying.append((names,) + st[:4])
        return st[4]

    loss, grad_x, g = _local_step(x, loss_target, w, early, late, ready)
    g["b_in"] = _from_pad_cols(g["b_in"])

    res = {}

    def flat(a):
        return jnp.swapaxes(a, -1, -2).reshape(a.shape[:-2] + (-1, 128))

    def unflat(y, shape):
        return jnp.swapaxes(y.reshape(shape[:-2] + (shape[-1], shape[-2])), -1, -2)

    def finish(groups, after, tag):
        mine = {}
        for names, send, recv, src, land in groups:
            src, land = _xchg_wait(send, recv, src, land, after, name="scatter_" + names[0] + "_wait", scatter=True)
            for n, s, ld in zip(names, src, land):
                mine[n] = _sum_slots(s, ld, chip)
        theirs = _swap_cores(list(mine.values()), name="swap_cores_" + tag)
        for n, t in zip(mine, theirs):
            if n == "w_in":
                res[n] = [unflat(r, wts[n].shape) for r in _adamw(flat(wts[n]), flat(mom[n]), flat(var[n]),
                                                                  flat(mine[n]), flat(t))]
            else:
                res[n] = _adamw(wts[n], mom[n], var[n], mine[n], t)

    finish(flying[:-1], g["ln0_g"], "a")

    small_shapes = [(1, 128)] + [view(n, wts[n]).shape for n in _SMALL] + [g[n].shape for n in _SMALL_SHARDED]
    packed = _pack([loss] + [view(n, g[n]) for n in _SMALL] + [g[n] for n in _SMALL_SHARDED])
    tot = _unpack(_allreduce_small(packed, dep=res["w_o"][3]), small_shapes)
    loss_out = tot[0][0, 0]
    gsm = dict(zip(_SMALL + _SMALL_SHARDED, tot[1:]))
    for n in _SMALL_SHARDED:
        cols = wts[n].shape[-1]
        gsm[n] = lax.dynamic_slice_in_dim(gsm[n], chip * cols, cols, axis=1).reshape(wts[n].shape)

    names = small_names
    shapes = [view(n, wts[n]).shape for n in names]
    small_out = _adamw(*small_wmv, _pack([gsm[n] for n in names]))
    small_res = [_unpack(r, shapes) for r in small_out]
    for j, n in enumerate(names):
        res[n] = [view(n, small_res[q][j]) for q in range(4)]
    finish(flying[-1:], small_out[0], "b")

    return (loss_out, grad_x, *[res[n][0] for n in _ORDER], *[res[n][1] for n in _ORDER],
            *[res[n][2] for n in _ORDER], *[res[n][3] for n in _ORDER])
```

```python
import functools
import math

import jax
import jax.numpy as jnp
from jax import lax
from jax.experimental import pallas as pl
from jax.experimental.pallas import tpu as pltpu

F32 = jnp.float32
BF16 = jnp.bfloat16
HI = lax.Precision.HIGHEST

N_META = 16
M_HEADS = 4
M_CHUNK = 128
PAD = M_CHUNK - N_META
CONV_W = 4
HALO_ROWS = 16
S5_GROUP = 16
S5_STATE = 64
S5_KCH = 4
LN_EPS = 1e-5
ALPHA = 2.0 ** 0.25
NEG = -1e30
ADAM_LR, ADAM_B1, ADAM_B2, ADAM_EPS, ADAM_WD, ADAM_STEP = 0.001, 0.9, 0.999, 1e-08, 0.01, 10

O_OFF, GS_OFF, GM_OFF, V_OFF, Q_OFF, K_OFF, U_OFF, G_OFF, NP = 0, 1024, 2048, 3072, 4096, 4608, 5120, 5632, 5760

NN = ((1,), (0,))
NT = ((1,), (1,))
TN = ((0,), (0,))


def _dot(a, b, dims=NN, prec=None):
    return lax.dot_general(a, b, (dims, ((), ())), preferred_element_type=F32, precision=prec)


def _bf(x):
    return x.astype(BF16)


def _sig(x):
    return 0.5 * jnp.tanh(0.5 * x) + 0.5


def _pcall(body, **kw):
    return pl.pallas_call(body, **kw)


def _cp(sem=None, vmem_mb=None):
    kw = {}
    if sem is not None:
        kw["dimension_semantics"] = sem
    if vmem_mb is not None:
        kw["vmem_limit_bytes"] = vmem_mb << 20
    return pltpu.CompilerParams(**kw)


def _row_tile(n, want, mult=16):
    best = None
    for t in range(mult, want + 1, mult):
        if n % t == 0:
            best = t
    assert best is not None, (n, want)
    return best


def _resident(shape):
    nd = len(shape)
    return pl.BlockSpec(shape, lambda *_: (0,) * nd, pipeline_mode=pl.Buffered(1))


def _const(shape):
    nd = len(shape)
    return pl.BlockSpec(shape, lambda *_: (0,) * nd)


def _ln_fwd(x, g, b):
    mu = jnp.mean(x, axis=-1, keepdims=True)
    xc = x - mu
    var = jnp.mean(xc * xc, axis=-1, keepdims=True)
    rstd = lax.rsqrt(var + LN_EPS)
    xhat = xc * rstd
    return xhat * g + b, xhat, rstd


def _ln_bwd(dy, xhat, rstd, g):
    dxh = dy * g
    m1 = jnp.mean(dxh, axis=-1, keepdims=True)
    m2 = jnp.mean(dxh * xhat, axis=-1, keepdims=True)
    return rstd * (dxh - m1 - xhat * m2)


def _colsum(x):
    return jnp.sum(x, axis=0, keepdims=True)


def _to_pad_cols(w):
    u, q, k, v, o, gi, gf, gs, gm = (w[..., 0:512], w[..., 512:1024], w[..., 1024:1536], w[..., 1536:2560],
                                     w[..., 2560:3584], w[..., 3584:3588], w[..., 3588:3592], w[..., 3592:4616],
                                     w[..., 4616:5640])
    z = jnp.zeros(w.shape[:-1] + (NP - G_OFF - 8,), w.dtype)
    return jnp.concatenate([o, gs, gm, v, q, k, u, gi, gf, z], axis=-1)


def _from_pad_cols(w):
    o, gs, gm, v, q, k, u = (w[..., O_OFF:GS_OFF], w[..., GS_OFF:GM_OFF], w[..., GM_OFF:V_OFF], w[..., V_OFF:Q_OFF],
                             w[..., Q_OFF:K_OFF], w[..., K_OFF:U_OFF], w[..., U_OFF:G_OFF])
    gi, gf = w[..., G_OFF:G_OFF + 4], w[..., G_OFF + 4:G_OFF + 8]
    return jnp.concatenate([u, q, k, v, o, gi, gf, gs, gm], axis=-1)


_IN_REF = (("u", 512), ("q", 512), ("k", 512), ("v", 1024), ("o", 1024), ("i", 4), ("f", 4), ("gs", 1024), ("gm", 1024))
_IN_PAD = (("o", O_OFF), ("gs", GS_OFF), ("gm", GM_OFF), ("v", V_OFF), ("q", Q_OFF), ("k", K_OFF), ("u", U_OFF),
           ("i", G_OFF), ("f", G_OFF + 4))


def _in_ref_ranges():
    out, off = {}, 0
    for n, s in _IN_REF:
        out[n] = (off, off + s)
        off += s
    return out, off


def _w_in_from_slots(g, chunk=None):
    rng, total = _in_ref_ranges()
    width = total // g.shape[0]
    cols = []
    for n, _ in _IN_PAD:
        a, b = rng[n]
        while a < b:
            s = a // width
            e = min(b, (s + 1) * width)
            cols.append(g[s][:, a - s * width:e - s * width])
            a = e
    cols.append(jnp.zeros((g.shape[1], NP - G_OFF - 8), g.dtype))
    if chunk is None:
        return jnp.concatenate(cols, axis=1)
    chunks, cur, room = [], [], chunk
    for c in cols:
        while c.shape[1] > 0:
            take = min(room, c.shape[1])
            cur.append(c[:, :take])
            c, room = c[:, take:], room - take
            if room == 0:
                chunks.append(jnp.concatenate(cur, axis=1))
                cur, room = [], chunk
    assert not cur
    return jnp.stack(chunks, axis=0)


def _slots_from_w_in(wp, nslot=4):
    rng, total = _in_ref_ranges()
    width = total // nslot
    pad_off = dict(_IN_PAD)
    slots = []
    for s in range(nslot):
        lo, hi = s * width, (s + 1) * width
        cols = []
        for n, _ in _IN_REF:
            a, b = rng[n]
            x0, x1 = max(a, lo), min(b, hi)
            if x0 < x1:
                cols.append(wp[:, pad_off[n] + x0 - a:pad_off[n] + x1 - a])
        slots.append(jnp.concatenate(cols, axis=1))
    return jnp.stack(slots, axis=0)


HEAD = PAD + N_META


def _ln0_in(j, x_ref, meta_ref):
    first = jnp.concatenate([jnp.zeros((PAD, meta_ref.shape[1]), F32), meta_ref[...]], axis=0)
    return jnp.where(j == 0, first[None], x_ref[...])


def _ln0_fwd(x, meta, g, b):
    bsz, seq, d = x.shape
    nb = seq // HEAD + 1

    def body(x_ref, m_ref, g_ref, b_ref, o_ref, ob_ref):
        y, _, _ = _ln_fwd(_ln0_in(pl.program_id(0), x_ref, m_ref), g_ref[...], b_ref[...])
        o_ref[...] = y
        ob_ref[...] = _bf(y)

    row = pl.BlockSpec((bsz, HEAD, d), lambda j: (0, j, 0))
    h0, h0b = _pcall(
        body, name="ln0_fwd", grid=(nb,),
        in_specs=[pl.BlockSpec((bsz, HEAD, d), lambda j: (0, jnp.maximum(j - 1, 0), 0)), _const((N_META, d)),
                  _const((1, d)), _const((1, d))],
        out_specs=[row, row],
        out_shape=[jax.ShapeDtypeStruct((bsz, nb * HEAD, d), F32), jax.ShapeDtypeStruct((bsz, nb * HEAD, d), BF16)],
        compiler_params=_cp(("arbitrary",)),
    )(x, meta, g, b)
    return h0.reshape(-1, d), h0b.reshape(-1, d)


def _ln0_bwd(x, meta, dr1, dpw, g):
    bsz, seq, d = x.shape
    nb = seq // HEAD + 1

    def body(x_ref, m_ref, a_ref, c_ref, g_ref, o_ref, dg_ref, db_ref, dm_ref):
        j = pl.program_id(0)

        @pl.when(j == 0)
        def _():
            dg_ref[...] = jnp.zeros_like(dg_ref)
            db_ref[...] = jnp.zeros_like(db_ref)
            dm_ref[...] = jnp.zeros_like(dm_ref)

        dy = ALPHA * a_ref[...] + c_ref[...]
        _, xhat, rstd = _ln_fwd(_ln0_in(j, x_ref, m_ref), g_ref[...], 0.0)
        dx = _ln_bwd(dy, xhat, rstd, g_ref[...])
        o_ref[...] = dx
        dg_ref[...] += _colsum((dy * xhat).reshape(bsz * HEAD, d))
        db_ref[...] += _colsum(dy.reshape(bsz * HEAD, d))

        @pl.when(j == 0)
        def _():
            dm_ref[...] += jnp.sum(dx[:, PAD:, :], axis=0)

    row = pl.BlockSpec((bsz, HEAD, d), lambda j: (0, j, 0))
    tok = pl.BlockSpec((bsz, HEAD, d), lambda j: (0, jnp.maximum(j - 1, 0), 0))
    lp = nb * HEAD
    return _pcall(
        body, name="ln0_bwd", grid=(nb,),
        in_specs=[tok, _const((N_META, d)), row, row, _const((1, d))],
        out_specs=[tok, _const((1, d)), _const((1, d)), _const((N_META, d))],
        out_shape=[jax.ShapeDtypeStruct((bsz, seq, d), F32), jax.ShapeDtypeStruct((1, d), F32),
                   jax.ShapeDtypeStruct((1, d), F32), jax.ShapeDtypeStruct((N_META, d), F32)],
        compiler_params=_cp(("arbitrary",)),
    )(x, meta, dr1.reshape(bsz, lp, d), dpw.reshape(bsz, lp, d), g)


IN_CHUNK = 1152


def _chunk_cols(w):
    k, n = w.shape
    return jnp.transpose(w.reshape(k, n // IN_CHUNK, IN_CHUNK), (1, 0, 2))


def _inproj(h0b, w3, bias, lp):
    r, d = h0b.shape
    nj, _, tn = w3.shape
    tm = _row_tile(lp, 2112)
    tps = lp // tm

    def body(a_ref, w_ref, b_ref, o_ref, gate_ref):
        i = pl.program_id(0)
        j = pl.program_id(1)
        acc = _dot(a_ref[...], w_ref[j]) + b_ref[...]
        t = (i % tps) * tm + lax.broadcasted_iota(jnp.int32, (tm, 1), 0)
        acc = jnp.where(t >= PAD, acc, 0.0)
        o_ref[...] = _bf(acc)

        @pl.when(j == nj - 1)
        def _():
            gate_ref[...] = acc[:, tn - 128:]

    return _pcall(
        body, name="inproj", grid=(r // tm, nj),
        in_specs=[pl.BlockSpec((tm, d), lambda i, j: (i, 0)), _resident(w3.shape),
                  pl.BlockSpec((1, tn), lambda i, j: (0, j))],
        out_specs=[pl.BlockSpec((tm, tn), lambda i, j: (i, j)), pl.BlockSpec((tm, 128), lambda i, j: (i, 0))],
        out_shape=[jax.ShapeDtypeStruct((r, nj * tn), BF16), jax.ShapeDtypeStruct((r, 128), F32)],
        compiler_params=_cp(("parallel", "arbitrary"), 48),
    )(h0b, w3, bias)


def _mm_tn(a, b, *, name, split=1, colsum=False, tk_want=2112):
    r, m = a.shape
    n = b.shape[1]
    tk = _row_tile(r, tk_want)
    tm = min(m, 1024)
    ns = n // split
    tn = ns
    for cand in (1024, 1152, 640, 512, 128):
        if ns % cand == 0 and cand <= ns:
            tn = cand
            break
    nb = ns // tn
    nk = r // tk

    def body(a_ref, b_ref, o_ref, *rest):
        acc = rest[-1]
        k = pl.program_id(2)

        @pl.when(k == 0)
        def _():
            acc[...] = jnp.zeros_like(acc)

        bt = b_ref[...]
        acc[...] += _dot(_bf(a_ref[...]), _bf(bt), TN)

        @pl.when(k == nk - 1)
        def _():
            o_ref[...] = acc[...]

        if colsum:
            cs_ref = rest[0]

            @pl.when(k == 0)
            def _():
                cs_ref[...] = jnp.zeros_like(cs_ref)

            cs_ref[...] += _colsum(bt.astype(F32))

    out_specs = [pl.BlockSpec((None, tm, tn), lambda i, j, k: (j // nb, i, j % nb))]
    out_shape = [jax.ShapeDtypeStruct((split, m, ns), F32)]
    if colsum:
        assert m == tm
        out_specs.append(pl.BlockSpec((1, tn), lambda i, j, k: (0, j)))
        out_shape.append(jax.ShapeDtypeStruct((1, n), F32))
    res = _pcall(
        body, name=name, grid=(m // tm, n // tn, nk),
        in_specs=[pl.BlockSpec((tk, tm), lambda i, j, k: (k, i)), pl.BlockSpec((tk, tn), lambda i, j, k: (k, j))],
        out_specs=out_specs, out_shape=out_shape,
        scratch_shapes=[pltpu.VMEM((tm, tn), F32)],
        compiler_params=_cp(("parallel", "parallel", "arbitrary"), 56),
    )(a, b)
    return res if colsum else res[0]


def _mm_nt(a, w3, lp, *, name, dep=None):
    r, kdim = a.shape
    nk, n, tk = w3.shape
    assert nk * tk == kdim
    tm = _row_tile(lp, 1056)
    deps = [] if dep is None else [dep]

    def body(a_ref, w_ref, *rest):
        o_ref, acc = rest[-2:]
        k = pl.program_id(1)

        @pl.when(k == 0)
        def _():
            acc[...] = jnp.zeros_like(acc)

        acc[...] += _dot(_bf(a_ref[...]), w_ref[k], NT)

        @pl.when(k == nk - 1)
        def _():
            o_ref[...] = acc[...]

    return _pcall(
        body, name=name, grid=(r // tm, nk),
        in_specs=[pl.BlockSpec((tm, tk), lambda i, k: (i, k)), _resident(w3.shape)]
        + [_const(dp_.shape) for dp_ in deps],
        out_specs=pl.BlockSpec((tm, n), lambda i, k: (i, 0)),
        out_shape=jax.ShapeDtypeStruct((r, n), F32),
        scratch_shapes=[pltpu.VMEM((tm, n), F32)],
        compiler_params=_cp(("parallel", "arbitrary"), 48),
    )(a, w3, *deps)


def _s5_prep(lam_re, lam_im, log_dt, b_re_t, b_im_t):
    g, p = lam_re.shape
    h = b_re_t.shape[0]

    def body(lr_ref, li_ref, ldt_ref, br_ref, bi_ref, pr_ref, pi_ref, bbr_ref, bbi_ref):
        lr, li = lr_ref[...], li_ref[...]
        dt = jnp.exp(ldt_ref[...])
        e = jnp.exp(lr * dt)
        ar, ai = e * jnp.cos(li * dt), e * jnp.sin(li * dt)
        den = lr * lr + li * li
        cr = ((ar - 1.0) * lr + ai * li) / den
        ci = (ai * lr - (ar - 1.0) * li) / den
        br, bi = br_ref[...], bi_ref[...]
        bbr_ref[...] = cr[None] * br - ci[None] * bi
        bbi_ref[...] = cr[None] * bi + ci[None] * br
        xr, xi = ar, ai
        pr_ref[0] = xr
        pi_ref[0] = xi
        for t in range(1, 8):
            xr, xi = xr * ar - xi * ai, xr * ai + xi * ar
            pr_ref[t] = xr
            pi_ref[t] = xi

    sd = jax.ShapeDtypeStruct
    return _pcall(body, name="s5_prep",
                  out_shape=[sd((8, g, p), F32), sd((8, g, p), F32), sd((h, g, p), F32), sd((h, g, p), F32)])(
        lam_re, lam_im, log_dt, b_re_t, b_im_t)


def _s5_prep_bwd(lam_re, lam_im, log_dt, b_re_t, b_im_t, da_re, da_im, dbb_re_t, dbb_im_t):
    g, p = lam_re.shape
    h = b_re_t.shape[0]

    def body(lr_ref, li_ref, ldt_ref, br_ref, bi_ref, dar_ref, dai_ref, dbr_ref, dbi_ref,
             glr_ref, gli_ref, gdt_ref, gbr_ref, gbi_ref):
        lr, li = lr_ref[...], li_ref[...]
        dt = jnp.exp(ldt_ref[...])
        e = jnp.exp(lr * dt)
        ar, ai = e * jnp.cos(li * dt), e * jnp.sin(li * dt)
        den = lr * lr + li * li
        cr = ((ar - 1.0) * lr + ai * li) / den
        ci = (ai * lr - (ar - 1.0) * li) / den
        br, bi = br_ref[...], bi_ref[...]
        gr, gi = dbr_ref[...], dbi_ref[...]
        gbr_ref[...] = gr * cr[None] + gi * ci[None]
        gbi_ref[...] = gi * cr[None] - gr * ci[None]
        gcr = jnp.sum(gr * br + gi * bi, axis=0)
        gci = jnp.sum(gi * br - gr * bi, axis=0)
        ilr, ili = lr / den, -li / den
        gar = dar_ref[...] + gcr * ilr + gci * ili
        gai = dai_ref[...] + gci * ilr - gcr * ili
        qr, qi = cr * ilr - ci * ili, cr * ili + ci * ilr
        glr = -(gcr * qr + gci * qi)
        gli = -(gci * qr - gcr * qi)
        gzr = gar * ar + gai * ai
        gzi = gai * ar - gar * ai
        glr_ref[...] = glr + gzr * dt
        gli_ref[...] = gli + gzi * dt
        gdt_ref[...] = jnp.sum(gzr * lr + gzi * li, axis=1, keepdims=True) * dt

    sd = jax.ShapeDtypeStruct
    return _pcall(body, name="s5_prep_bwd",
                  out_shape=[sd((g, p), F32), sd((g, p), F32), sd((g, 1), F32), sd((h, g, p), F32), sd((h, g, p), F32)])(
        lam_re, lam_im, log_dt, b_re_t, b_im_t, da_re, da_im, dbb_re_t, dbb_im_t)


def _cmul(xr, xi, yr, yi):
    return xr * yr - xi * yi, xr * yi + xi * yr


def _dot5(a, b, dims=NN):
    return _dot(_bf(a), _bf(b), dims)


def _s5_fwd(p3, bk, cre, cim, apow, dskip):
    bsz, lp, _ = p3.shape
    tt = _row_tile(lp, 528, 8)
    nt = lp // tt
    nblk = tt // 8
    hw = 512

    def body(u_ref, bk_ref, cre_ref, cim_ref, ap_ref, d_ref, y_ref, xs_ref, car_ref):
        t = pl.program_id(2)

        @pl.when(t == 0)
        def _():
            car_ref[...] = jnp.zeros_like(car_ref)

        u = u_ref[...].astype(F32)
        xs_ref[...] = _dot5(u, bk_ref[...])
        ap = ap_ref[...]
        apr, api = ap[:, :hw], ap[:, hw:]
        rows = lax.broadcasted_iota(jnp.int32, (8, hw), 0)
        lev = [(d, jnp.where(rows < d, 0.0, jnp.broadcast_to(apr[d - 1:d, :], (8, hw))),
                jnp.where(rows < d, 0.0, jnp.broadcast_to(api[d - 1:d, :], (8, hw)))) for d in (1, 2, 4)]

        def blk(i, carry):
            cr, ci = carry
            off = pl.multiple_of(i * 8, 8)
            x = xs_ref[pl.ds(off, 8), :]
            xr, xi = x[:, :hw], x[:, hw:]
            for d, lr, li in lev:
                mr, mi = _cmul(pltpu.roll(xr, d, 0), pltpu.roll(xi, d, 0), lr, li)
                xr, xi = xr + mr, xi + mi
            mr, mi = _cmul(apr, api, cr, ci)
            xr, xi = xr + mr, xi + mi
            xs_ref[pl.ds(off, 8), :] = jnp.concatenate([xr, xi], axis=1)
            return xr[7:8, :], xi[7:8, :]

        c0 = car_ref[...]
        cr, ci = lax.fori_loop(0, nblk, blk, (c0[0:1, :hw], c0[0:1, hw:]), unroll=2)
        car_ref[...] = jnp.broadcast_to(jnp.concatenate([cr, ci], axis=1), car_ref.shape)
        xs = xs_ref[...]
        y_ref[...] = (_dot5(xs[:, :hw], cre_ref[...]) - _dot5(xs[:, hw:], cim_ref[...])
                      + d_ref[...] * u)

    ub = U_OFF // 128
    return _pcall(
        body, name="s5_fwd", grid=(S5_KCH, bsz, nt),
        in_specs=[pl.BlockSpec((None, tt, 128), lambda k, b, t: (b, t, ub + k)),
                  pl.BlockSpec((None, 128, 2 * hw), lambda k, b, t: (k, 0, 0)),
                  pl.BlockSpec((None, hw, 128), lambda k, b, t: (k, 0, 0)),
                  pl.BlockSpec((None, hw, 128), lambda k, b, t: (k, 0, 0)),
                  pl.BlockSpec((None, 8, 2 * hw), lambda k, b, t: (k, 0, 0)),
                  pl.BlockSpec((1, 128), lambda k, b, t: (0, k))],
        out_specs=[pl.BlockSpec((None, tt, 128), lambda k, b, t: (b, t, k)),
                   pl.BlockSpec((None, None, tt, 2 * hw), lambda k, b, t: (b, k, t, 0))],
        out_shape=[jax.ShapeDtypeStruct((bsz, lp, S5_KCH * 128), F32),
                   jax.ShapeDtypeStruct((bsz, S5_KCH, lp, 2 * hw), F32)],
        scratch_shapes=[pltpu.VMEM((8, 2 * hw), F32)],
        compiler_params=_cp(("parallel", "parallel", "arbitrary"), 40),
    )(p3, bk, cre, cim, apow, dskip)


def _s5_bwd(dp3, p3, dy3, xs, bk, cre, cim, apow_rev, dskip):
    bsz, lp, _ = p3.shape
    tt = _row_tile(lp, 528, 8)
    nt = lp // tt
    nblk = tt // 8
    hw = 512
    tb = tt // 8

    def body(dp_any, u_ref, dy_ref, xs_ref, halo_ref, bkt_ref, cre_ref, cim_ref, ap_ref, d_ref,
             du_ref, dbk_ref, dcre_ref, dcim_ref, da_ref, dd_ref, g_ref, ext_ref, car_ref):
        b = pl.program_id(1)
        t = pl.program_id(2)
        tidx = nt - 1 - t

        @pl.when(t == 0)
        def _():
            car_ref[...] = jnp.zeros_like(car_ref)

        @pl.when((b == 0) & (t == 0))
        def _():
            dbk_ref[...] = jnp.zeros_like(dbk_ref)
            dcre_ref[...] = jnp.zeros_like(dcre_ref)
            dcim_ref[...] = jnp.zeros_like(dcim_ref)
            da_ref[...] = jnp.zeros_like(da_ref)
            dd_ref[...] = jnp.zeros_like(dd_ref)

        u = u_ref[...].astype(F32)
        dy = dy_ref[...]
        g_ref[:, :hw] = _dot5(dy, cre_ref[...])
        g_ref[:, hw:] = -_dot5(dy, cim_ref[...])
        ap = ap_ref[...]
        apr, api = ap[:, :hw], -ap[:, hw:]
        rows = lax.broadcasted_iota(jnp.int32, (8, hw), 0)
        lev = [(d, jnp.where(rows >= 8 - d, 0.0, jnp.broadcast_to(apr[8 - d:9 - d, :], (8, hw))),
                jnp.where(rows >= 8 - d, 0.0, jnp.broadcast_to(api[8 - d:9 - d, :], (8, hw)))) for d in (1, 2, 4)]

        def blk(i, carry):
            cr, ci = carry
            off = pl.multiple_of((nblk - 1 - i) * 8, 8)
            x = g_ref[pl.ds(off, 8), :]
            xr, xi = x[:, :hw], x[:, hw:]
            for d, lr, li in lev:
                mr, mi = _cmul(pltpu.roll(xr, 8 - d, 0), pltpu.roll(xi, 8 - d, 0), lr, li)
                xr, xi = xr + mr, xi + mi
            mr, mi = _cmul(apr, api, cr, ci)
            xr, xi = xr + mr, xi + mi
            g_ref[pl.ds(off, 8), :] = jnp.concatenate([xr, xi], axis=1)
            return xr[0:1, :], xi[0:1, :]

        c0 = car_ref[...]
        cr, ci = lax.fori_loop(0, nblk, blk, (c0[0:1, :hw], c0[0:1, hw:]), unroll=2)
        car_ref[...] = jnp.broadcast_to(jnp.concatenate([cr, ci], axis=1), car_ref.shape)

        gg = g_ref[...]
        du = _dot5(gg, bkt_ref[...]) + d_ref[...] * dy
        trow = tidx * tt + lax.broadcasted_iota(jnp.int32, (tt, 1), 0)
        du_ref[...] = jnp.where(trow >= PAD, du, 0.0).astype(du_ref.dtype)
        dbk_ref[...] += _dot5(u, gg, TN)
        xsv = xs_ref[...]
        dcre_ref[...] += _dot5(dy, xsv[:, :hw], TN)
        dcim_ref[...] -= _dot5(dy, xsv[:, hw:], TN)
        dd_ref[...] += _colsum(dy * u)
        ext_ref[0:8, :] = jnp.where(tidx == 0, 0.0, halo_ref[...])
        ext_ref[8:, :] = xsv
        xp = ext_ref[pl.ds(7, tt), :]
        gr, gi, pr, pi = gg[:, :hw], gg[:, hw:], xp[:, :hw], xp[:, hw:]
        da_ref[:, :hw] += _colsum(gr * pr + gi * pi)
        da_ref[:, hw:] += _colsum(gi * pr - gr * pi)

    ub = U_OFF // 128
    sd = jax.ShapeDtypeStruct
    rt = lambda t: nt - 1 - t
    tr = lambda a: jnp.swapaxes(a, 1, 2)
    res = _pcall(
        body, name="s5_bwd", grid=(S5_KCH, bsz, nt),
        in_specs=[pl.BlockSpec(memory_space=pl.ANY),
                  pl.BlockSpec((None, tt, 128), lambda k, b, t: (b, rt(t), ub + k)),
                  pl.BlockSpec((None, tt, 128), lambda k, b, t: (b, rt(t), k)),
                  pl.BlockSpec((None, None, tt, 2 * hw), lambda k, b, t: (b, k, rt(t), 0)),
                  pl.BlockSpec((None, None, 8, 2 * hw), lambda k, b, t: (b, k, jnp.maximum(rt(t) * tb - 1, 0), 0)),
                  pl.BlockSpec((None, 2 * hw, 128), lambda k, b, t: (k, 0, 0)),
                  pl.BlockSpec((None, 128, hw), lambda k, b, t: (k, 0, 0)),
                  pl.BlockSpec((None, 128, hw), lambda k, b, t: (k, 0, 0)),
                  pl.BlockSpec((None, 8, 2 * hw), lambda k, b, t: (k, 0, 0)),
                  pl.BlockSpec((1, 128), lambda k, b, t: (0, k))],
        out_specs=[pl.BlockSpec((None, tt, 128), lambda k, b, t: (b, rt(t), ub + k)),
                   pl.BlockSpec((None, 128, 2 * hw), lambda k, b, t: (k, 0, 0)),
                   pl.BlockSpec((None, 128, hw), lambda k, b, t: (k, 0, 0)),
                   pl.BlockSpec((None, 128, hw), lambda k, b, t: (k, 0, 0)),
                   pl.BlockSpec((None, 1, 2 * hw), lambda k, b, t: (k, 0, 0)),
                   pl.BlockSpec((1, 128), lambda k, b, t: (0, k))],
        out_shape=[sd(dp3.shape, dp3.dtype), sd((S5_KCH, 128, 2 * hw), F32), sd((S5_KCH, 128, hw), F32),
                   sd((S5_KCH, 128, hw), F32), sd((S5_KCH, 1, 2 * hw), F32), sd((1, S5_KCH * 128), F32)],
        scratch_shapes=[pltpu.VMEM((tt, 2 * hw), F32), pltpu.VMEM((tt + 8, 2 * hw), F32), pltpu.VMEM((8, 2 * hw), F32)],
        input_output_aliases={0: 0},
        compiler_params=_cp(("arbitrary", "arbitrary", "arbitrary"), 48),
    )(dp3, p3, dy3, xs, xs, tr(bk), tr(cre), tr(cim), apow_rev, dskip)
    return res[0], res[1], tr(res[2]), tr(res[3]), res[4], res[5]


_G0 = math.sqrt(2.0 / math.pi)
_G1 = 0.044715


def _gelu(y):
    return 0.5 * y * (1.0 + jnp.tanh(_G0 * (y + _G1 * y * y * y)))


def _gelu_grad(y):
    th = jnp.tanh(_G0 * (y + _G1 * y * y * y))
    return 0.5 * (1.0 + th) + 0.5 * y * (1.0 - th * th) * _G0 * (1.0 + 3.0 * _G1 * y * y)


def _glu_fwd(y_s5, wglu_g, lp):
    r, w = y_s5.shape
    tm = _row_tile(lp, 416)
    cw = wglu_g.shape[2]

    def body(y_ref, w_ref, gy_ref, z_ref, o_ref):
        gy = _bf(_gelu(y_ref[...]))
        gy_ref[...] = gy
        zs = [_dot(gy, w_ref[s]) for s in range(4)]
        for s in range(4):
            z_ref[:, s * cw:(s + 1) * cw] = _bf(zs[s])
        o_ref[:, :cw] = _bf(zs[0] * _sig(zs[2]))
        o_ref[:, cw:] = _bf(zs[1] * _sig(zs[3]))

    sd = jax.ShapeDtypeStruct
    return _pcall(
        body, name="glu_fwd", grid=(r // tm,),
        in_specs=[pl.BlockSpec((tm, w), lambda i: (i, 0)), _resident(wglu_g.shape)],
        out_specs=[pl.BlockSpec((tm, w), lambda i: (i, 0)), pl.BlockSpec((tm, 4 * cw), lambda i: (i, 0)),
                   pl.BlockSpec((tm, 2 * cw), lambda i: (i, 0))],
        out_shape=[sd((r, w), BF16), sd((r, 4 * cw), BF16), sd((r, 2 * cw), BF16)],
        compiler_params=_cp(("parallel",), 40),
    )(y_s5, wglu_g)


def _glu_bwd(dyg, z, y_s5, wglu_g, lp):
    r, w = y_s5.shape
    tm = _row_tile(lp, 416)
    cw = wglu_g.shape[2]

    def body(d_ref, z_ref, y_ref, w_ref, dz_ref, dy_ref):
        d = d_ref[...].astype(F32)
        zz = z_ref[...].astype(F32)
        acc = jnp.zeros((tm, w), F32)
        for s in range(2):
            z1 = zz[:, s * cw:(s + 1) * cw]
            sg = _sig(zz[:, (2 + s) * cw:(3 + s) * cw])
            dd = d[:, s * cw:(s + 1) * cw]
            dz1 = _bf(dd * sg)
            dz2 = _bf(dd * z1 * sg * (1.0 - sg))
            dz_ref[:, s * cw:(s + 1) * cw] = dz1
            dz_ref[:, (2 + s) * cw:(3 + s) * cw] = dz2
            acc += _dot(dz1, w_ref[s], NT) + _dot(dz2, w_ref[2 + s], NT)
        dy_ref[...] = acc * _gelu_grad(y_ref[...])

    sd = jax.ShapeDtypeStruct
    return _pcall(
        body, name="glu_bwd", grid=(r // tm,),
        in_specs=[pl.BlockSpec((tm, 2 * cw), lambda i: (i, 0)), pl.BlockSpec((tm, 4 * cw), lambda i: (i, 0)),
                  pl.BlockSpec((tm, w), lambda i: (i, 0)), _resident(wglu_g.shape)],
        out_specs=[pl.BlockSpec((tm, 4 * cw), lambda i: (i, 0)), pl.BlockSpec((tm, w), lambda i: (i, 0))],
        out_shape=[sd((r, 4 * cw), BF16), sd((r, w), F32)],
        compiler_params=_cp(("parallel",), 40),
    )(dyg, z, y_s5, wglu_g)


def _conv_fwd(p3, cw, cb):
    bsz, lp, _ = p3.shape
    tt = _row_tile(lp, 704)
    nt = lp // tt
    tb = tt // 8
    c = cw.shape[1]
    qb = Q_OFF // c

    hr = HALO_ROWS
    off = hr - (CONV_W - 1)

    def body(x_ref, halo_ref, w_ref, b_ref, pre_ref, act_ref, ext_ref):
        t = pl.program_id(1)
        ext_ref[0:hr, :] = jnp.where(t == 0, 0.0, halo_ref[...].astype(F32))
        ext_ref[hr:, :] = x_ref[...].astype(F32)
        w = w_ref[...]
        acc = b_ref[...] + w[0:1, :] * ext_ref[pl.ds(off, tt), :]
        for j in range(1, CONV_W):
            acc = acc + w[j:j + 1, :] * ext_ref[pl.ds(off + j, tt), :]
        pre_ref[...] = _bf(acc)
        act_ref[...] = _bf(acc * _sig(acc))

    sd = jax.ShapeDtypeStruct
    return _pcall(
        body, name="conv_fwd", grid=(bsz, nt),
        in_specs=[pl.BlockSpec((None, tt, c), lambda b, t: (b, t, qb)),
                  pl.BlockSpec((None, hr, c), lambda b, t: (b, jnp.maximum(t * (tt // hr) - 1, 0), qb)),
                  _const((CONV_W, c)), _const((1, c))],
        out_specs=[pl.BlockSpec((None, tt, c), lambda b, t: (b, t, 0))] * 2,
        out_shape=[sd((bsz, lp, c), BF16)] * 2,
        scratch_shapes=[pltpu.VMEM((tt + hr, c), F32)],
        compiler_params=_cp(("parallel", "parallel")),
    )(p3, p3, cw, cb)


def _conv_bwd(dp3, p3, dact3, pre3, cw):
    bsz, lp, _ = p3.shape
    tt = _row_tile(lp, 704)
    nt = lp // tt
    tb = tt // 8
    c = cw.shape[1]
    qb = Q_OFF // c

    hr = HALO_ROWS
    off = hr - (CONV_W - 1)

    def silu_grad(x):
        s = _sig(x)
        return s * (1.0 + x * (1.0 - s))

    def body(dp_any, x_ref, xh_ref, d_ref, dh_ref, pre_ref, preh_ref, w_ref, o_ref, dw_ref, db_ref, ext_ref, dext_ref):
        b = pl.program_id(0)
        t = pl.program_id(1)

        @pl.when((b == 0) & (t == 0))
        def _():
            dw_ref[...] = jnp.zeros_like(dw_ref)
            db_ref[...] = jnp.zeros_like(db_ref)

        dc = d_ref[...].astype(F32) * silu_grad(pre_ref[...].astype(F32))
        dch = jnp.where(t == nt - 1, 0.0, dh_ref[...].astype(F32) * silu_grad(preh_ref[...].astype(F32)))
        dext_ref[0:tt, :] = dc
        dext_ref[tt:, :] = dch
        ext_ref[0:hr, :] = jnp.where(t == 0, 0.0, xh_ref[...].astype(F32))
        ext_ref[hr:, :] = x_ref[...].astype(F32)
        w = w_ref[...]
        acc = w[CONV_W - 1:CONV_W, :] * dc
        for j in range(CONV_W - 1):
            acc = acc + w[j:j + 1, :] * dext_ref[pl.ds(CONV_W - 1 - j, tt), :]
        trow = t * tt + lax.broadcasted_iota(jnp.int32, (tt, 1), 0)
        o_ref[...] = jnp.where(trow >= PAD, acc, 0.0).astype(o_ref.dtype)
        db_ref[...] += _colsum(dc)
        for j in range(CONV_W):
            dw_ref[j:j + 1, :] += _colsum(dc * ext_ref[pl.ds(off + j, tt), :])

    sd = jax.ShapeDtypeStruct
    nxt = lambda t: jnp.minimum((t + 1) * (tt // hr), lp // hr - 1)
    return _pcall(
        body, name="conv_bwd", grid=(bsz, nt),
        in_specs=[pl.BlockSpec(memory_space=pl.ANY),
                  pl.BlockSpec((None, tt, c), lambda b, t: (b, t, qb)),
                  pl.BlockSpec((None, hr, c), lambda b, t: (b, jnp.maximum(t * (tt // hr) - 1, 0), qb)),
                  pl.BlockSpec((None, tt, c), lambda b, t: (b, t, 0)),
                  pl.BlockSpec((None, hr, c), lambda b, t: (b, nxt(t), 0)),
                  pl.BlockSpec((None, tt, c), lambda b, t: (b, t, 0)),
                  pl.BlockSpec((None, hr, c), lambda b, t: (b, nxt(t), 0)),
                  _const((CONV_W, c))],
        out_specs=[pl.BlockSpec((None, tt, c), lambda b, t: (b, t, qb)), _const((CONV_W, c)), _const((1, c))],
        out_shape=[sd(dp3.shape, dp3.dtype), sd((CONV_W, c), F32), sd((1, c), F32)],
        scratch_shapes=[pltpu.VMEM((tt + hr, c), F32), pltpu.VMEM((tt + hr, c), F32)],
        input_output_aliases={0: 0},
        compiler_params=_cp(("arbitrary", "arbitrary")),
    )(dp3, p3, p3, dact3, dact3, pre3, pre3, cw)


def _mlstm_gates(g, h_idx, c_idx, lc):
    lane = lax.broadcasted_iota(jnp.int32, g.shape, 1)
    i_col = jnp.sum(jnp.where(lane == h_idx, g, 0.0), axis=1, keepdims=True)
    f_col = jnp.sum(jnp.where(lane == M_HEADS + h_idx, g, 0.0), axis=1, keepdims=True)
    row = lax.broadcasted_iota(jnp.int32, (lc, 1), 0)
    valid = (c_idx * lc + row) >= PAD
    li = jnp.where(valid, i_col, NEG)
    lf = jnp.where(valid, jnp.minimum(f_col, 0.0) - jnp.log(1.0 + jnp.exp(-jnp.abs(f_col))), 0.0)
    r2 = lax.broadcasted_iota(jnp.int32, (lc, lc), 0)
    c2 = lax.broadcasted_iota(jnp.int32, (lc, lc), 1)
    eye = r2 == c2
    tril = r2 >= c2
    to_row = lambda col: jnp.sum(jnp.where(eye, col, 0.0), axis=0, keepdims=True)
    lf_row = to_row(lf)
    b_col = jnp.sum(jnp.where(tril, lf_row, 0.0), axis=1, keepdims=True)
    b_row = to_row(b_col)
    li_row = to_row(li)
    d_mat = jnp.where(tril, b_col - b_row + li_row, NEG)
    return dict(f_col=f_col, valid=valid, li=li, b_col=b_col, d_mat=d_mat, eye=eye, r2=r2, c2=c2, row=row,
                to_row=to_row)


def _mlstm_chunk(q, ks, v, gq, c_st, n_st, m_st, lc):
    b_col, d_mat = gq["b_col"], gq["d_mat"]
    m_inter = b_col + m_st
    m_row = jnp.maximum(m_inter, jnp.max(d_mat, axis=1, keepdims=True))
    w_intra = jnp.exp(d_mat - m_row)
    w_inter = jnp.exp(m_inter - m_row)
    qb, kb, vb, cb = _bf(q), _bf(ks), _bf(v), _bf(c_st)
    s = _dot(qb, kb, NT) * w_intra
    qc = _dot(qb, cb)
    num = _dot(_bf(s), vb) + w_inter * qc
    qn = jnp.sum(q * n_st, axis=1, keepdims=True)
    den = jnp.sum(s, axis=1, keepdims=True) + w_inter * qn
    e = jnp.exp(-m_row)
    nn = jnp.maximum(jnp.abs(den), e)
    b_last = b_col[lc - 1:lc, :]
    g_log = b_last - b_col + gq["li"]
    m_new = jnp.maximum(b_last + m_st, jnp.max(g_log, axis=0, keepdims=True))
    w_k = jnp.exp(g_log - m_new)
    decay = jnp.exp(b_last + m_st - m_new)
    return dict(w_intra=w_intra, w_inter=w_inter, qb=qb, kb=kb, vb=vb, cb=cb, s=s, qc=qc, num=num, qn=qn, den=den,
                e=e, nn=nn, m_new=m_new, w_k=w_k, decay=decay)


def _chunks_per_step(nc):
    return max(c for c in (3, 2, 1) if nc % c == 0)


def _mlstm_fwd(qk3, p3, pg3):
    bsz, lp, _ = p3.shape
    lc = M_CHUNK
    nc = lp // lc
    dk, dv = 128, 256
    scale = dk ** -0.5

    cps = _chunks_per_step(nc)
    rows = cps * lc

    def body(q_ref, k_ref, v_ref, g_ref, h_ref, cs_ref, ns_ref, ms_ref, c_sc, n_sc, m_sc):
        st = pl.program_id(1)

        @pl.when(st == 0)
        def _():
            c_sc[...] = jnp.zeros_like(c_sc)
            n_sc[...] = jnp.zeros_like(n_sc)
            m_sc[...] = jnp.zeros_like(m_sc)

        for j in range(cps):
            rs = slice(j * lc, (j + 1) * lc)
            g = g_ref[rs, :]
            for hh in range(M_HEADS):
                c_st, n_st, m_all = c_sc[hh], n_sc[hh], m_sc[hh]
                cs_ref[hh, j] = c_st
                ns_ref[hh, j] = n_st
                ms_ref[hh, j] = m_all
                m_st = m_all[:, 0:1]
                q = q_ref[rs, hh * dk:(hh + 1) * dk].astype(F32)
                ks = k_ref[rs, hh * dk:(hh + 1) * dk].astype(F32) * scale
                v = v_ref[rs, hh * dv:(hh + 1) * dv]
                gq = _mlstm_gates(g, hh, st * cps + j, lc)
                f = _mlstm_chunk(q, ks, v, gq, c_st, n_st, m_st, lc)
                h_ref[rs, hh * dv:(hh + 1) * dv] = _bf(f["num"] / f["nn"])
                kw = ks * f["w_k"]
                c_sc[hh] = f["decay"] * c_st + _dot(_bf(kw), f["vb"], TN)
                n_sc[hh] = f["decay"] * n_st + _colsum(kw)
                m_sc[hh] = jnp.broadcast_to(f["m_new"], (1, 128))

    sd = jax.ShapeDtypeStruct
    nh = M_HEADS
    return _pcall(
        body, name="mlstm_fwd", grid=(bsz, nc // cps),
        in_specs=[pl.BlockSpec((None, rows, nh * dk), lambda b, c: (b, c, 0)),
                  pl.BlockSpec((None, rows, nh * dk), lambda b, c: (b, c, 1)),
                  pl.BlockSpec((None, rows, nh * dv), lambda b, c: (b, c, V_OFF // (nh * dv))),
                  pl.BlockSpec((None, rows, 128), lambda b, c: (b, c, 0))],
        out_specs=[pl.BlockSpec((None, rows, nh * dv), lambda b, c: (b, c, 0)),
                   pl.BlockSpec((None, nh, cps, dk, dv), lambda b, c: (b, 0, c, 0, 0)),
                   pl.BlockSpec((None, nh, cps, 1, dk), lambda b, c: (b, 0, c, 0, 0)),
                   pl.BlockSpec((None, nh, cps, 1, 128), lambda b, c: (b, 0, c, 0, 0))],
        out_shape=[sd((bsz, lp, nh * dv), BF16), sd((bsz, nh, nc, dk, dv), F32),
                   sd((bsz, nh, nc, 1, dk), F32), sd((bsz, nh, nc, 1, 128), F32)],
        scratch_shapes=[pltpu.VMEM((nh, dk, dv), F32), pltpu.VMEM((nh, 1, dk), F32), pltpu.VMEM((nh, 1, 128), F32)],
        compiler_params=_cp(("parallel", "arbitrary")),
    )(qk3, qk3, p3, pg3)


def _mlstm_bwd(dp3, qk3, p3, pg3, dh3, cs, ns, ms):
    bsz, lp, _ = p3.shape
    lc = M_CHUNK
    nc = lp // lc
    dk, dv = 128, 256
    scale = dk ** -0.5

    cps = _chunks_per_step(nc)
    nst = nc // cps
    rows = cps * lc

    def body(dp_any, q_ref, k_ref, v_ref, g_ref, dh_ref, cs_ref, ns_ref, ms_ref,
             dv_ref, dqk_ref, dg_ref, dc_sc, dn_sc):
        t = pl.program_id(1)
        st = nst - 1 - t

        @pl.when(t == 0)
        def _():
            dc_sc[...] = jnp.zeros_like(dc_sc)
            dn_sc[...] = jnp.zeros_like(dn_sc)

        lane = lax.broadcasted_iota(jnp.int32, (lc, 128), 1)
        for j in reversed(range(cps)):
            rs = slice(j * lc, (j + 1) * lc)
            g = g_ref[rs, :]
            dgate = jnp.zeros((lc, 128), F32)
            for hh in range(M_HEADS):
                dgate = head(hh, j, rs, st * cps + j, g, lane, dgate, q_ref, k_ref, v_ref, dh_ref, cs_ref, ns_ref,
                             ms_ref, dv_ref, dqk_ref, dc_sc, dn_sc)
            dg_ref[rs, :] = dgate.astype(dg_ref.dtype)

    def head(hh, j, sl, c, g, lane, dgate, q_ref, k_ref, v_ref, dh_ref, cs_ref, ns_ref, ms_ref, dv_ref, dqk_ref,
             dc_sc, dn_sc):
        c_st, n_st = cs_ref[hh, j], ns_ref[hh, j]
        m_st = ms_ref[hh, j][:, 0:1]
        q = q_ref[sl, hh * dk:(hh + 1) * dk].astype(F32)
        ks = k_ref[sl, hh * dk:(hh + 1) * dk].astype(F32) * scale
        v = v_ref[sl, hh * dv:(hh + 1) * dv]
        dh = dh_ref[sl, hh * dv:(hh + 1) * dv].astype(F32)
        gq = _mlstm_gates(g, hh, c, lc)
        f = _mlstm_chunk(q, ks, v, gq, c_st, n_st, m_st, lc)
        eye, r2, c2, row, valid = gq["eye"], gq["r2"], gq["c2"], gq["row"], gq["valid"]
        w_intra, w_inter, s, nn, den = f["w_intra"], f["w_inter"], f["s"], f["nn"], f["den"]
        qb, kb, vb, cb, w_k, decay = f["qb"], f["kb"], f["vb"], f["cb"], f["w_k"], f["decay"]
        d_c, d_n = dc_sc[hh], dn_sc[hh]
        d_cb = _bf(d_c)

        hout = f["num"] / nn
        dnum = dh / nn
        d_nn = -jnp.sum(dh * hout, axis=1, keepdims=True) / nn
        dden = jnp.where(jnp.abs(den) > f["e"], d_nn * jnp.sign(den), 0.0)
        wdnum = w_inter * dnum
        wdden = w_inter * dden
        ds = _dot(_bf(dnum), vb, NT) + dden
        dsw = _bf(ds * w_intra)
        dq = _dot(dsw, kb) + _dot(_bf(wdnum), cb, NT) + wdden * n_st
        dkw = _dot(vb, d_cb, NT) + d_n
        dks = _dot(dsw, qb, TN) + dkw * w_k
        kw = ks * w_k
        dvv = _dot(_bf(s), _bf(dnum), TN) + _dot(_bf(kw), d_cb)
        dd = ds * s
        rs = jnp.sum(dd, axis=1, keepdims=True)
        cs_col = jnp.sum(jnp.where(eye, jnp.sum(dd, axis=0, keepdims=True), 0.0), axis=1, keepdims=True)
        dwi = jnp.sum(dnum * f["qc"], axis=1, keepdims=True) + dden * f["qn"]
        db = rs - cs_col + dwi * w_inter
        dli = cs_col
        ddecay = jnp.sum(jnp.sum(d_c * c_st, axis=1, keepdims=True), axis=0, keepdims=True) \
            + jnp.sum(d_n * n_st, axis=1, keepdims=True)
        dgl = jnp.sum(dkw * ks, axis=1, keepdims=True) * w_k
        dblast = ddecay * decay + jnp.sum(dgl, axis=0, keepdims=True)
        db = db - dgl + jnp.where(row == lc - 1, dblast, 0.0)
        dli = dli + dgl
        db_row = gq["to_row"](db)
        dlf = jnp.sum(jnp.where(c2 >= r2, db_row, 0.0), axis=1, keepdims=True)
        dlf = jnp.where(valid, dlf, 0.0)
        dgate = jnp.where(lane == hh, jnp.where(valid, dli, 0.0), dgate)
        dgate = jnp.where(lane == M_HEADS + hh, dlf / (1.0 + jnp.exp(gq["f_col"])), dgate)
        dqk_ref[sl, hh * dk:(hh + 1) * dk] = _bf(dq)
        dqk_ref[sl, (M_HEADS + hh) * dk:(M_HEADS + hh + 1) * dk] = _bf(dks * scale)
        dv_ref[sl, hh * dv:(hh + 1) * dv] = dvv.astype(dv_ref.dtype)
        dc_sc[hh] = decay * d_c + _dot(qb, _bf(wdnum), TN)
        dn_sc[hh] = decay * d_n + _colsum(q * wdden)
        return dgate

    sd = jax.ShapeDtypeStruct
    nh = M_HEADS
    rc = lambda c: nst - 1 - c
    return _pcall(
        body, name="mlstm_bwd", grid=(bsz, nst),
        in_specs=[pl.BlockSpec(memory_space=pl.ANY),
                  pl.BlockSpec((None, rows, nh * dk), lambda b, c: (b, rc(c), 0)),
                  pl.BlockSpec((None, rows, nh * dk), lambda b, c: (b, rc(c), 1)),
                  pl.BlockSpec((None, rows, nh * dv), lambda b, c: (b, rc(c), V_OFF // (nh * dv))),
                  pl.BlockSpec((None, rows, 128), lambda b, c: (b, rc(c), 0)),
                  pl.BlockSpec((None, rows, nh * dv), lambda b, c: (b, rc(c), 0)),
                  pl.BlockSpec((None, nh, cps, dk, dv), lambda b, c: (b, 0, rc(c), 0, 0)),
                  pl.BlockSpec((None, nh, cps, 1, dk), lambda b, c: (b, 0, rc(c), 0, 0)),
                  pl.BlockSpec((None, nh, cps, 1, 128), lambda b, c: (b, 0, rc(c), 0, 0))],
        out_specs=[pl.BlockSpec((None, rows, nh * dv), lambda b, c: (b, rc(c), V_OFF // (nh * dv))),
                   pl.BlockSpec((None, rows, 2 * nh * dk), lambda b, c: (b, rc(c), 0)),
                   pl.BlockSpec((None, rows, 128), lambda b, c: (b, rc(c), 0))],
        out_shape=[sd(dp3.shape, dp3.dtype), sd((bsz, lp, 2 * nh * dk), BF16), sd((bsz, lp, 128), dp3.dtype)],
        scratch_shapes=[pltpu.VMEM((nh, dk, dv), F32), pltpu.VMEM((nh, 1, dk), F32)],
        input_output_aliases={0: 0},
        compiler_params=_cp(("arbitrary", "arbitrary")),
    )(dp3, qk3, qk3, p3, pg3, dh3, cs, ns, ms)


def _headnorm(x):
    dv = x.shape[1] // M_HEADS
    xh, rs = [], []
    for h in range(M_HEADS):
        xx = x[:, h * dv:(h + 1) * dv]
        mu = jnp.mean(xx, axis=-1, keepdims=True)
        xc = xx - mu
        rstd = lax.rsqrt(jnp.mean(xc * xc, axis=-1, keepdims=True) + LN_EPS)
        xh.append(xc * rstd)
        rs.append(rstd)
    return jnp.concatenate(xh, axis=1), rs


def _mix_fwd(hm, p, ys5g, h0, gn, wmo_bf, wo_bf, lp):
    r, d = hm.shape
    tm = _row_tile(lp, 384)

    def body(hm_ref, o_ref, gs_ref, gm_ref, ys_ref, h0_ref, gn_ref, wmo_ref, wo_ref,
             ymin_ref, mix_ref, r1_ref):
        xhat, _ = _headnorm(hm_ref[...].astype(F32))
        ymin = _bf(_sig(o_ref[...].astype(F32)) * (xhat * gn_ref[...]))
        ymin_ref[...] = ymin
        ym = _dot(ymin, wmo_ref[...])
        mix = _bf(_sig(gs_ref[...].astype(F32)) * ys_ref[...].astype(F32) + _sig(gm_ref[...].astype(F32)) * ym)
        mix_ref[...] = mix
        r1_ref[...] = ALPHA * h0_ref[...] + _dot(mix, wo_ref[...])

    sd = jax.ShapeDtypeStruct
    row = pl.BlockSpec((tm, d), lambda i: (i, 0))
    return _pcall(
        body, name="mix_fwd", grid=(r // tm,),
        in_specs=[row, pl.BlockSpec((tm, d), lambda i: (i, O_OFF // d)), pl.BlockSpec((tm, d), lambda i: (i, GS_OFF // d)),
                  pl.BlockSpec((tm, d), lambda i: (i, GM_OFF // d)), row, row, _const((1, d)),
                  _resident((d, d)), _resident((d, d))],
        out_specs=[row] * 3,
        out_shape=[sd((r, d), BF16), sd((r, d), BF16), sd((r, d), F32)],
        compiler_params=_cp(("parallel",), 48),
    )(hm, p, p, p, ys5g, h0, gn, wmo_bf, wo_bf)


def _mix_bwd(dr1, wo_bf, wmo_bf, p, ys5g, ymin, hm, gn, lp):
    r, d = hm.shape
    tm = _row_tile(lp, 384)
    dv = d // M_HEADS

    def body(dr1_ref, wo_ref, wmo_ref, o_ref, gs_ref, gm_ref, ys_ref, ym_ref, hm_ref, gn_ref,
             dp_ref, dys_ref, dym_ref, dhm_ref, dgn_ref):
        i = pl.program_id(0)

        @pl.when(i == 0)
        def _():
            dgn_ref[...] = jnp.zeros_like(dgn_ref)

        dmix = _dot(_bf(dr1_ref[...]), wo_ref[...], NT)
        sgs, sgm, so = (_sig(gs_ref[...].astype(F32)), _sig(gm_ref[...].astype(F32)), _sig(o_ref[...].astype(F32)))
        dys_ref[...] = _bf(dmix * sgs)
        dp_ref[:, d:2 * d] = _bf(dmix * ys_ref[...].astype(F32) * sgs * (1.0 - sgs))
        dym = dmix * sgm
        dym_ref[...] = _bf(dym)
        ym = _dot(ym_ref[...], wmo_ref[...])
        dp_ref[:, 2 * d:3 * d] = _bf(dmix * ym * sgm * (1.0 - sgm))
        dymin = _dot(_bf(dym), wmo_ref[...], NT)
        xhat, rs = _headnorm(hm_ref[...].astype(F32))
        gn_ = gn_ref[...]
        dp_ref[:, 0:d] = _bf(dymin * (xhat * gn_) * so * (1.0 - so))
        dhn = dymin * so
        dgn_ref[...] += _colsum(dhn * xhat)
        dxh = dhn * gn_
        for h in range(M_HEADS):
            sl = slice(h * dv, (h + 1) * dv)
            a, xh = dxh[:, sl], xhat[:, sl]
            m1 = jnp.mean(a, axis=-1, keepdims=True)
            m2 = jnp.mean(a * xh, axis=-1, keepdims=True)
            dhm_ref[:, sl] = _bf(rs[h] * (a - m1 - xh * m2))

    sd = jax.ShapeDtypeStruct
    row = pl.BlockSpec((tm, d), lambda i: (i, 0))
    vec = _const((1, d))
    return _pcall(
        body, name="mix_bwd", grid=(r // tm,),
        in_specs=[row, _resident((d, d)), _resident((d, d)),
                  pl.BlockSpec((tm, d), lambda i: (i, O_OFF // d)), pl.BlockSpec((tm, d), lambda i: (i, GS_OFF // d)),
                  pl.BlockSpec((tm, d), lambda i: (i, GM_OFF // d)), row, row, row, vec],
        out_specs=[pl.BlockSpec((tm, 3 * d), lambda i: (i, 0)), row, row, row, vec],
        out_shape=[sd((r, NP), BF16), sd((r, d), BF16), sd((r, d), BF16), sd((r, d), BF16), sd((1, d), F32)],
        compiler_params=_cp(("arbitrary",), 56),
    )(dr1, wo_bf, wmo_bf, p, p, p, ys5g, ymin, hm, gn)


def _mlp_fwd(r1, tgt, g1, b1, wup_g, wdn_bf, bup, g2, b2, lp):
    r, d = r1.shape
    tm = _row_tile(lp, 384)
    tps = lp // tm
    nf = wup_g.shape[0]

    def body(r1_ref, t_ref, g1_ref, b1_ref, wup_ref, wdn_ref, bup_ref, g2_ref, b2_ref,
             dr2_ref, h1b_ref, act_ref, loss_ref, dg2_ref, db2_ref):
        i = pl.program_id(0)

        @pl.when(i == 0)
        def _():
            loss_ref[...] = jnp.zeros_like(loss_ref)
            dg2_ref[...] = jnp.zeros_like(dg2_ref)
            db2_ref[...] = jnp.zeros_like(db2_ref)

        h1, _, _ = _ln_fwd(r1_ref[...], g1_ref[...], b1_ref[...])
        h1b = _bf(h1)
        h1b_ref[...] = h1b
        ff = jnp.zeros((tm, d), F32)
        for s in range(nf):
            up = _dot(h1b, wup_ref[s]) + bup_ref[:, s * d:(s + 1) * d]
            a = jnp.maximum(up, 0.0)
            a = _bf(a * a)
            act_ref[:, s * d:(s + 1) * d] = a
            ff = ff + _dot(a, wdn_ref[s * d:(s + 1) * d, :])
        r2 = ALPHA * h1 + ff
        g2 = g2_ref[...]
        y, xhat, rstd = _ln_fwd(r2, g2, b2_ref[...])
        t = (i % tps) * tm + lax.broadcasted_iota(jnp.int32, (tm, 1), 0)
        diff = jnp.where(t >= PAD + N_META, y - t_ref[...], 0.0)
        loss_ref[...] += 0.5 / d * jnp.sum(jnp.sum(diff * diff, axis=1, keepdims=True), axis=0, keepdims=True)
        dy = diff * (1.0 / d)
        dg2_ref[...] += _colsum(dy * xhat)
        db2_ref[...] += _colsum(dy)
        dr2_ref[...] = _ln_bwd(dy, xhat, rstd, g2)

    sd = jax.ShapeDtypeStruct
    row = pl.BlockSpec((tm, d), lambda i: (i, 0))
    vec = _const((1, d))
    return _pcall(
        body, name="mlp_fwd", grid=(r // tm,),
        in_specs=[row, row, vec, vec, _resident(wup_g.shape), _resident(wdn_bf.shape), _const((1, nf * d)), vec, vec],
        out_specs=[row, row, pl.BlockSpec((tm, nf * d), lambda i: (i, 0)), _const((1, 128)), vec, vec],
        out_shape=[sd((r, d), F32), sd((r, d), BF16), sd((r, nf * d), BF16), sd((1, 128), F32), sd((1, d), F32),
                   sd((1, d), F32)],
        compiler_params=_cp(("arbitrary",), 56),
    )(r1, tgt, g1, b1, wup_g, wdn_bf, bup, g2, b2)


def _mlp_bwd(act, dr2, r1, g1, wup_g, wdn_bf, lp):
    r, d = dr2.shape
    tm = _row_tile(lp, 384)
    nf = wup_g.shape[0]

    def body(act_ref, dr2_ref, r1_ref, g1_ref, wup_ref, wdn_ref, dr1_ref, dup_ref, dbup_ref, dg1_ref, db1_ref):
        i = pl.program_id(0)

        @pl.when(i == 0)
        def _():
            dbup_ref[...] = jnp.zeros_like(dbup_ref)
            dg1_ref[...] = jnp.zeros_like(dg1_ref)
            db1_ref[...] = jnp.zeros_like(db1_ref)

        dr2 = dr2_ref[...]
        dr2b = _bf(dr2)
        acc = ALPHA * dr2
        for s in range(nf):
            dact = _dot(dr2b, wdn_ref[s * d:(s + 1) * d, :], NT)
            dup = dact * (2.0 * jnp.sqrt(act_ref[:, s * d:(s + 1) * d].astype(F32)))
            dbup_ref[:, s * d:(s + 1) * d] += _colsum(dup)
            dupb = _bf(dup)
            dup_ref[:, s * d:(s + 1) * d] = dupb
            acc = acc + _dot(dupb, wup_ref[s], NT)
        g1 = g1_ref[...]
        _, xhat1, rstd1 = _ln_fwd(r1_ref[...], g1, 0.0)
        dr1_ref[...] = _ln_bwd(acc, xhat1, rstd1, g1)
        dg1_ref[...] += _colsum(acc * xhat1)
        db1_ref[...] += _colsum(acc)

    sd = jax.ShapeDtypeStruct
    row = pl.BlockSpec((tm, d), lambda i: (i, 0))
    vec = _const((1, d))
    return _pcall(
        body, name="mlp_bwd", grid=(r // tm,),
        in_specs=[pl.BlockSpec((tm, nf * d), lambda i: (i, 0)), row, row, vec, _resident(wup_g.shape),
                  _resident(wdn_bf.shape)],
        out_specs=[row, pl.BlockSpec((tm, nf * d), lambda i: (i, 0)), _const((1, nf * d)), vec, vec],
        out_shape=[sd((r, d), F32), sd((r, nf * d), BF16), sd((1, nf * d), F32), sd((1, d), F32), sd((1, d), F32)],
        compiler_params=_cp(("arbitrary",), 56),
    )(act, dr2, r1, g1, wup_g, wdn_bf)


def _s5_block_mats(bb_re_t, bb_im_t, c_re, c_im, ap_re, ap_im):
    ng = c_re.shape[0]
    gl = ng // S5_KCH
    eye = jnp.eye(gl, dtype=F32)

    def bmat(bt):
        bb = jnp.transpose(bt, (1, 0, 2)).reshape(S5_KCH, gl, S5_GROUP, S5_STATE)
        return jnp.einsum("kghp,gj->kghjp", bb, eye).reshape(S5_KCH, gl * S5_GROUP, gl * S5_STATE)

    def cmat(c):
        cc = c.reshape(S5_KCH, gl, S5_GROUP, S5_STATE)
        return jnp.einsum("kghp,gj->kjpgh", cc, eye).reshape(S5_KCH, gl * S5_STATE, gl * S5_GROUP)

    def pw(a):
        return jnp.transpose(a.reshape(8, S5_KCH, gl * S5_STATE), (1, 0, 2))

    bk = jnp.concatenate([bmat(bb_re_t), bmat(bb_im_t)], axis=-1)
    apow = jnp.concatenate([pw(ap_re), pw(ap_im)], axis=-1)
    return _bf(bk), _bf(cmat(c_re)), _bf(cmat(c_im)), apow


def _s5_block_grads(dbk, dcre, dcim, da):
    gl = dbk.shape[1] // S5_GROUP
    ng = gl * S5_KCH
    eye = jnp.eye(gl, dtype=F32)
    hw = gl * S5_STATE

    def bpart(x):
        x = x.reshape(S5_KCH, gl, S5_GROUP, gl, S5_STATE)
        x = jnp.einsum("kghjp,gj->kghp", x, eye).reshape(ng, S5_GROUP, S5_STATE)
        return jnp.transpose(x, (1, 0, 2))

    def cpart(x):
        x = x.reshape(S5_KCH, gl, S5_STATE, gl, S5_GROUP)
        return jnp.einsum("kjpgh,gj->kghp", x, eye).reshape(ng, S5_GROUP, S5_STATE)

    return (bpart(dbk[..., :hw]), bpart(dbk[..., hw:]), cpart(dcre), cpart(dcim),
            da[:, 0, :hw].reshape(ng, S5_STATE), da[:, 0, hw:].reshape(ng, S5_STATE))


def _tie(a, tok):
    return a if tok is None else a + tok[0, 0]


def _local_step(x, tgt, w, early=None, late=None, ready=None):
    ready = ready or (lambda names, g: None)
    bsz, seq, d = x.shape
    lp = PAD + N_META + seq
    r = bsz * lp
    tgtp = jnp.concatenate([jnp.zeros((bsz, PAD + N_META, d), F32), tgt], axis=1).reshape(r, d)

    h0, h0b = _ln0_fwd(x, w["meta_tokens"], w["ln0_g"], w["ln0_b"])
    b_re_t = jnp.transpose(w["s5_b_re"], (2, 0, 1))
    b_im_t = jnp.transpose(w["s5_b_im"], (2, 0, 1))
    ap_re, ap_im, bb_re_t, bb_im_t = _s5_prep(w["s5_lambda_re"], w["s5_lambda_im"], w["s5_log_dt"], b_re_t, b_im_t)
    bk, cre, cim, apow = _s5_block_mats(bb_re_t, bb_im_t, w["s5_c_re"], w["s5_c_im"], ap_re, ap_im)
    apow_rev = jnp.flip(apow, axis=1)
    if early is not None:
        w = {**w, **early((h0, tgtp, bk, cre, cim, apow_rev))}
    p, pg = _inproj(h0b, w["w_in"], w["b_in"], lp)
    p3 = p.reshape(bsz, lp, NP)
    pg3 = pg.reshape(bsz, lp, 128)

    y_s5, xs = _s5_fwd(p3, bk, cre, cim, apow, w["s5_d"])
    sw = y_s5.shape[-1]
    if late is not None:
        w = {**w, **late(y_s5)}
    gy, z, ys5g = _glu_fwd(y_s5.reshape(r, sw), w["s5_w_glu"], lp)

    pre3, qk3 = _conv_fwd(p3, w["qk_conv_w"], w["qk_conv_b"])
    hm3, cs, ns, ms = _mlstm_fwd(qk3, p3, pg3)
    hm = hm3.reshape(r, d)
    ymin, mix, r1 = _mix_fwd(hm, p, ys5g, h0, w["m_norm_g"], w["m_w_out"], w["w_o"], lp)
    dr2, h1b, act, loss, dg2, db2 = _mlp_fwd(r1, tgtp, w["ln1_g"], w["ln1_b"], w["w_up"], w["w_down"], w["b_up"],
                                             w["ln2_g"], w["ln2_b"], lp)

    g = {"ln2_g": dg2, "ln2_b": db2}
    dr1, dup, g["b_up"], g["ln1_g"], g["ln1_b"] = _mlp_bwd(act, dr2, r1, w["ln1_g"], w["w_up"], w["w_down"], lp)
    g["w_down"] = _mm_tn(act, dr2, name="dw_down")
    g["w_up"] = _mm_tn(h1b, dup, name="dw_up", split=w["w_up"].shape[0])
    tok = ready(("w_down", "w_up"), g)
    dp, dys5g, dym, dhm, g["m_norm_g"] = _mix_bwd(
        dr1, w["w_o"], w["m_w_out"], p, ys5g, ymin, hm, _tie(w["m_norm_g"], tok), lp)
    g["w_o"] = _mm_tn(mix, dr1, name="dw_o")
    g["m_w_out"] = _mm_tn(ymin, dym, name="dw_mout")

    dp3 = dp.reshape(bsz, lp, NP)
    dp3, dqk3, dgate = _mlstm_bwd(dp3, qk3, p3, pg3, dhm.reshape(bsz, lp, d), cs, ns, ms)
    dp3, g["qk_conv_w"], g["qk_conv_b"] = _conv_bwd(dp3, p3, dqk3, pre3, w["qk_conv_w"])
    dz, dys5 = _glu_bwd(dys5g, z, y_s5.reshape(r, sw), w["s5_w_glu"], lp)
    g["s5_w_glu"] = _mm_tn(gy, dz, name="dw_glu", split=w["s5_w_glu"].shape[0])
    tok = ready(("s5_w_glu", "m_w_out", "w_o"), g)
    dp3, dbk, dcre, dcim, da, g["s5_d"] = _s5_bwd(dp3, p3, dys5.reshape(bsz, lp, sw), xs, bk, cre, cim, apow_rev,
                                                 _tie(w["s5_d"], tok))
    dbb_re_t, dbb_im_t, g["s5_c_re"], g["s5_c_im"], da_re, da_im = _s5_block_grads(dbk, dcre, dcim, da)
    g["s5_lambda_re"], g["s5_lambda_im"], g["s5_log_dt"], gb_re_t, gb_im_t = _s5_prep_bwd(
        w["s5_lambda_re"], w["s5_lambda_im"], w["s5_log_dt"], b_re_t, b_im_t, da_re, da_im, dbb_re_t, dbb_im_t)
    g["s5_b_re"] = jnp.transpose(gb_re_t, (1, 2, 0))
    g["s5_b_im"] = jnp.transpose(gb_im_t, (1, 2, 0))

    dp3 = lax.dynamic_update_slice(dp3, dgate, (0, 0, G_OFF))
    dp = dp3.reshape(r, NP)
    g["w_in"], g["b_in"] = _mm_tn(h0b, dp, name="dw_in", colsum=True)
    tok = ready(("w_in",), g)
    dpw = _mm_nt(dp, w["w_in"], lp, name="dh0", dep=tok)
    grad_x, g["ln0_g"], g["ln0_b"], g["meta_tokens"] = _ln0_bwd(x, w["meta_tokens"], dr1, dpw, w["ln0_g"])
    return loss, grad_x, g


_ANY = pl.BlockSpec(memory_space=pl.ANY)
_MESH = pl.DeviceIdType.MESH


def _place():
    return lax.axis_index("x"), lax.axis_index("y"), lax.axis_index("c")


def _gather_chips(shards):
    n = len(shards)

    def body(*refs):
        ins, outs = refs[:n], refs[n:2 * n]
        send, recv, loc = refs[2 * n:]
        x, y, c = _place()
        me = 2 * x + y
        peers = [(1 - x, y), (x, 1 - y), (1 - x, 1 - y)]

        def rc(a, k, slot):
            px, py = peers[k]
            return pltpu.make_async_remote_copy(src_ref=ins[a], dst_ref=outs[a].at[slot], send_sem=send.at[a, k],
                                                recv_sem=recv.at[a, k], device_id=(px, py, c), device_id_type=_MESH)

        own = [pltpu.make_async_copy(ins[a], outs[a].at[me], loc.at[a]) for a in range(n)]
        for cp in own:
            cp.start()
        out = [rc(a, k, me) for a in range(n) for k in range(3)]
        for cp in out:
            cp.start()
        for a in range(n):
            for k in range(3):
                rc(a, k, 2 * peers[k][0] + peers[k][1]).wait_recv()
        for cp in out:
            cp.wait_send()
        for cp in own:
            cp.wait()

    return _pcall(
        body, name="gather_chips", in_specs=[_ANY] * n, out_specs=[_ANY] * n,
        out_shape=[jax.ShapeDtypeStruct((4,) + s.shape, s.dtype) for s in shards],
        scratch_shapes=[pltpu.SemaphoreType.DMA((n, 3)), pltpu.SemaphoreType.DMA((n, 3)), pltpu.SemaphoreType.DMA((n,))],
    )(*shards)


_HBM = pl.BlockSpec(memory_space=pltpu.HBM)
_SEM = pl.BlockSpec(memory_space=pltpu.SEMAPHORE)
_EFFECT = pltpu.SideEffectType.DATAFLOW_SIDE_EFFECTING


def _xchg_copies(srcs, lands, send, recv, scatter):
    x, y, c = _place()
    me = 2 * x + y
    peers = [(1 - x, y), (x, 1 - y), (1 - x, 1 - y)]
    out = []
    for a in range(len(srcs)):
        for k, (px, py) in enumerate(peers):
            src = srcs[a].at[2 * px + py] if scatter else srcs[a]
            dst = lands[a].at[k] if scatter else lands[a].at[me]
            out.append(pltpu.make_async_remote_copy(src_ref=src, dst_ref=dst, send_sem=send.at[3 * a + k],
                                                    recv_sem=recv.at[3 * a + k], device_id=(px, py, c),
                                                    device_id_type=_MESH))
    return out


def _xchg_start(srcs, lands, *, name, scatter, dep=None):
    n = len(srcs)
    deps = [] if dep is None else [dep]
    nd = len(deps)

    def body(*refs):
        send, recv = refs[2 * n + nd], refs[2 * n + nd + 1]
        for cp in _xchg_copies(refs[:n], refs[n:2 * n], send, recv, scatter):
            cp.start()
        refs[-1][...] = jnp.zeros_like(refs[-1])

    hbm = lambda a: pltpu.HBM(a.shape, a.dtype)
    con = lambda a: pltpu.with_memory_space_constraint(a, pltpu.HBM)
    res = _pcall(
        body, name=name, in_specs=[_HBM] * (2 * n) + [_ANY] * nd,
        out_specs=[_SEM, _SEM] + [_HBM] * (2 * n) + [pl.BlockSpec(memory_space=pltpu.VMEM)],
        out_shape=[pltpu.SemaphoreType.DMA((3 * n,)), pltpu.SemaphoreType.DMA((3 * n,))]
        + [hbm(a) for a in srcs] + [hbm(a) for a in lands] + [jax.ShapeDtypeStruct((8, 128), F32)],
        input_output_aliases={i: 2 + i for i in range(2 * n)},
        compiler_params=pltpu.CompilerParams(has_side_effects=_EFFECT),
    )(*[con(a) for a in srcs], *[con(a) for a in lands], *deps)
    return res[0], res[1], list(res[2:2 + n]), list(res[2 + n:2 + 2 * n]), res[-1]


def _xchg_wait(send, recv, srcs, lands, after, *, name, scatter):
    n = len(srcs)
    afters = list(after) if isinstance(after, (list, tuple)) else [after]

    def body(*refs):
        s_ref, r_ref = refs[2 * n], refs[2 * n + 1]
        for cp in _xchg_copies(refs[:n], refs[n:2 * n], s_ref, r_ref, scatter):
            cp.wait_send()
            cp.wait_recv()

    hbm = lambda a: pltpu.HBM(a.shape, a.dtype)
    res = _pcall(
        body, name=name, in_specs=[_HBM] * (2 * n) + [_SEM, _SEM] + [_ANY] * len(afters),
        out_specs=[_HBM] * (2 * n),
        out_shape=[hbm(a) for a in srcs] + [hbm(a) for a in lands],
        input_output_aliases={i: i for i in range(2 * n)},
        compiler_params=pltpu.CompilerParams(has_side_effects=_EFFECT),
    )(*srcs, *lands, send, recv, *afters)
    return list(res[:n]), list(res[n:])


def _swap_cores(arrs, name="swap_cores"):
    n = len(arrs)

    def body(*refs):
        ins, outs = refs[:n], refs[n:2 * n]
        send, recv = refs[2 * n:]
        x, y, c = _place()
        cps = [pltpu.make_async_remote_copy(src_ref=ins[a], dst_ref=outs[a], send_sem=send.at[a], recv_sem=recv.at[a],
                                            device_id=(x, y, 1 - c), device_id_type=_MESH) for a in range(n)]
        for cp in cps:
            cp.start()
        for cp in cps:
            cp.wait_recv()
        for cp in cps:
            cp.wait_send()

    return _pcall(
        body, name=name, in_specs=[_ANY] * n, out_specs=[_ANY] * n,
        out_shape=[jax.ShapeDtypeStruct(s.shape, s.dtype) for s in arrs],
        scratch_shapes=[pltpu.SemaphoreType.DMA((n,)), pltpu.SemaphoreType.DMA((n,))],
    )(*arrs)


def _allreduce_small(v, dep=None):
    rows = v.shape[0]
    half = rows // 2
    assert half % 8 == 0 and 2 * half == rows
    deps = [] if dep is None else [dep]

    def body(v_ref, *rest):
        out_ref, sib_ref, pair_ref, slots_ref, send, recv = rest[len(deps):]
        x, y, c = _place()
        chip = 2 * x + y
        sibling = (x, y, 1 - c)
        peers = [(1 - x, y), (x, 1 - y), (1 - x, 1 - y)]
        mine = pl.ds(pl.multiple_of(c * half, 8), half)

        first = pltpu.make_async_remote_copy(src_ref=v_ref, dst_ref=sib_ref, send_sem=send.at[0], recv_sem=recv.at[0],
                                             device_id=sibling, device_id_type=_MESH)
        first.start()
        first.wait_recv()
        pair_ref[...] = v_ref[...] + sib_ref[...]
        slots_ref[chip] = pair_ref[mine, :]
        cross = [pltpu.make_async_remote_copy(src_ref=pair_ref.at[mine], dst_ref=slots_ref.at[chip],
                                              send_sem=send.at[1 + k], recv_sem=recv.at[1 + k],
                                              device_id=(px, py, c), device_id_type=_MESH)
                 for k, (px, py) in enumerate(peers)]
        for cp in cross:
            cp.start()
        for cp in cross:
            cp.wait_recv()
        out_ref[mine, :] = ((slots_ref[0] + slots_ref[1]) + slots_ref[2]) + slots_ref[3]
        last = pltpu.make_async_remote_copy(src_ref=out_ref.at[mine], dst_ref=out_ref.at[mine], send_sem=send.at[4],
                                            recv_sem=recv.at[4], device_id=sibling, device_id_type=_MESH)
        last.start()
        last.wait_recv()
        first.wait_send()
        for cp in cross:
            cp.wait_send()
        last.wait_send()

    vm = pl.BlockSpec(memory_space=pltpu.VMEM)
    return _pcall(
        body, name="allreduce_small", in_specs=[vm] + [_ANY] * len(deps), out_specs=vm,
        out_shape=jax.ShapeDtypeStruct((rows, 128), F32),
        scratch_shapes=[pltpu.VMEM((rows, 128), F32), pltpu.VMEM((rows, 128), F32), pltpu.VMEM((4, half, 128), F32),
                        pltpu.SemaphoreType.DMA((5,)), pltpu.SemaphoreType.DMA((5,))],
        compiler_params=_cp(None, 40),
    )(v, *deps)


def _sum_slots(parts, land, chip):
    ns, rows, cols = land.shape
    tm = _row_tile(rows, 256, 8)

    def body(chip_ref, own_ref, a_ref, o_ref):
        o_ref[...] = ((own_ref[...] + a_ref[0]) + a_ref[1]) + a_ref[2]

    return _pcall(
        body, name="sum_slots",
        grid_spec=pltpu.PrefetchScalarGridSpec(
            num_scalar_prefetch=1, grid=(rows // tm,),
            in_specs=[pl.BlockSpec((None, tm, cols), lambda i, c: (c[0], i, 0)),
                      pl.BlockSpec((ns, tm, cols), lambda i, c: (0, i, 0))],
            out_specs=pl.BlockSpec((tm, cols), lambda i, c: (i, 0))),
        out_shape=jax.ShapeDtypeStruct((rows, cols), F32),
        compiler_params=_cp(("parallel",), 40),
    )(jnp.reshape(chip, (1,)).astype(jnp.int32), parts, land)


def _adamw(w, m, v, g0, g1=None):
    rows, cols = w.shape[-2:]
    lead = w.ndim == 3
    tm = _row_tile(rows, max(8, (1 << 20) // (4 * cols)), 8)
    c1 = 1.0 - ADAM_B1 ** ADAM_STEP
    c2 = 1.0 - ADAM_B2 ** ADAM_STEP
    two = g1 is not None

    def body(*refs):
        w_ref, m_ref, v_ref, g0_ref = refs[:4]
        g_ref, d_ref, nm_ref, nv_ref = refs[-4:]
        g = g0_ref[...]
        if two:
            g = g + refs[4][...]
        nm = ADAM_B1 * m_ref[...] + (1.0 - ADAM_B1) * g
        nv = ADAM_B2 * v_ref[...] + (1.0 - ADAM_B2) * (g * g)
        g_ref[...] = g
        nm_ref[...] = nm
        nv_ref[...] = nv
        d_ref[...] = -ADAM_LR * ((nm / c1) / (jnp.sqrt(nv / c2) + ADAM_EPS) + ADAM_WD * w_ref[...])

    blk = pl.BlockSpec((tm, cols), lambda i: (i, 0))
    wblk = pl.BlockSpec((None, tm, cols), lambda i: (0, i, 0)) if lead else blk
    ins = [w, m, v, g0] + ([g1] if two else [])
    return _pcall(
        body, name="adamw", grid=(rows // tm,), in_specs=[wblk] * 3 + [blk] * (len(ins) - 3), out_specs=[wblk] * 4,
        out_shape=[jax.ShapeDtypeStruct(w.shape, F32)] * 4,
        compiler_params=_cp(("parallel",), 40),
    )(*ins)


_BIG = ("w_in", "s5_w_glu", "m_w_out", "w_o", "w_up", "w_down")
_SMALL = ("ln0_g", "ln0_b", "b_in", "qk_conv_b", "s5_lambda_re", "s5_lambda_im", "s5_log_dt", "s5_b_re", "s5_b_im",
          "s5_c_re", "s5_c_im", "s5_d", "m_norm_g", "ln1_g", "ln1_b", "b_up", "ln2_g", "ln2_b")
_SMALL_SHARDED = ("meta_tokens", "qk_conv_w")
_ORDER = ("meta_tokens", "ln0_g", "ln0_b", "w_in", "b_in", "qk_conv_w", "qk_conv_b", "s5_lambda_re", "s5_lambda_im",
          "s5_log_dt", "s5_b_re", "s5_b_im", "s5_c_re", "s5_c_im", "s5_d", "s5_w_glu", "m_norm_g", "m_w_out", "w_o",
          "ln1_g", "ln1_b", "w_up", "b_up", "w_down", "ln2_g", "ln2_b")


def _pack(arrs):
    flat = jnp.concatenate([a.reshape(-1) for a in arrs])
    n = flat.shape[0]
    rows = -(-n // 2048) * 16
    return jnp.pad(flat, (0, rows * 128 - n)).reshape(rows, 128)


def _unpack(packed, shapes):
    flat = packed.reshape(-1)
    out, off = [], 0
    for s in shapes:
        n = math.prod(s)
        out.append(flat[off:off + n].reshape(s))
        off += n
    return out


def kernel(x, meta_tokens, ln0_g, ln0_b, w_in, b_in, qk_conv_w, qk_conv_b, s5_lambda_re, s5_lambda_im, s5_log_dt, s5_b_re, s5_b_im, s5_c_re, s5_c_im, s5_d, s5_w_glu, m_norm_g, m_w_out, w_o, ln1_g, ln1_b, w_up, b_up, w_down, ln2_g, ln2_b, loss_target, m_meta_tokens, m_ln0_g, m_ln0_b, m_w_in, m_b_in, m_qk_conv_w, m_qk_conv_b, m_s5_lambda_re, m_s5_lambda_im, m_s5_log_dt, m_s5_b_re, m_s5_b_im, m_s5_c_re, m_s5_c_im, m_s5_d, m_s5_w_glu, m_m_norm_g, m_m_w_out, m_w_o, m_ln1_g, m_ln1_b, m_w_up, m_b_up, m_w_down, m_ln2_g, m_ln2_b, v_meta_tokens, v_ln0_g, v_ln0_b, v_w_in, v_b_in, v_qk_conv_w, v_qk_conv_b, v_s5_lambda_re, v_s5_lambda_im, v_s5_log_dt, v_s5_b_re, v_s5_b_im, v_s5_c_re, v_s5_c_im, v_s5_d, v_s5_w_glu, v_m_norm_g, v_m_w_out, v_w_o, v_ln1_g, v_ln1_b, v_w_up, v_b_up, v_w_down, v_ln2_g, v_ln2_b):
    wts = dict(meta_tokens=meta_tokens, ln0_g=ln0_g, ln0_b=ln0_b, w_in=w_in, b_in=b_in, qk_conv_w=qk_conv_w,
               qk_conv_b=qk_conv_b, s5_lambda_re=s5_lambda_re, s5_lambda_im=s5_lambda_im, s5_log_dt=s5_log_dt,
               s5_b_re=s5_b_re, s5_b_im=s5_b_im, s5_c_re=s5_c_re, s5_c_im=s5_c_im, s5_d=s5_d, s5_w_glu=s5_w_glu,
               m_norm_g=m_norm_g, m_w_out=m_w_out, w_o=w_o, ln1_g=ln1_g, ln1_b=ln1_b, w_up=w_up, b_up=b_up,
               w_down=w_down, ln2_g=ln2_g, ln2_b=ln2_b)
    mom = dict(meta_tokens=m_meta_tokens, ln0_g=m_ln0_g, ln0_b=m_ln0_b, w_in=m_w_in, b_in=m_b_in, qk_conv_w=m_qk_conv_w,
               qk_conv_b=m_qk_conv_b, s5_lambda_re=m_s5_lambda_re, s5_lambda_im=m_s5_lambda_im, s5_log_dt=m_s5_log_dt,
               s5_b_re=m_s5_b_re, s5_b_im=m_s5_b_im, s5_c_re=m_s5_c_re, s5_c_im=m_s5_c_im, s5_d=m_s5_d,
               s5_w_glu=m_s5_w_glu, m_norm_g=m_m_norm_g, m_w_out=m_m_w_out, w_o=m_w_o, ln1_g=m_ln1_g, ln1_b=m_ln1_b,
               w_up=m_w_up, b_up=m_b_up, w_down=m_w_down, ln2_g=m_ln2_g, ln2_b=m_ln2_b)
    var = dict(meta_tokens=v_meta_tokens, ln0_g=v_ln0_g, ln0_b=v_ln0_b, w_in=v_w_in, b_in=v_b_in, qk_conv_w=v_qk_conv_w,
               qk_conv_b=v_qk_conv_b, s5_lambda_re=v_s5_lambda_re, s5_lambda_im=v_s5_lambda_im, s5_log_dt=v_s5_log_dt,
               s5_b_re=v_s5_b_re, s5_b_im=v_s5_b_im, s5_c_re=v_s5_c_re, s5_c_im=v_s5_c_im, s5_d=v_s5_d,
               s5_w_glu=v_s5_w_glu, m_norm_g=v_m_norm_g, m_w_out=v_m_w_out, w_o=v_w_o, ln1_g=v_ln1_g, ln1_b=v_ln1_b,
               w_up=v_w_up, b_up=v_b_up, w_down=v_w_down, ln2_g=v_ln2_g, ln2_b=v_ln2_b)
    d = x.shape[-1]
    chip = 2 * lax.axis_index("x") + lax.axis_index("y")

    gw = dict(zip(_SMALL_SHARDED, _gather_chips([meta_tokens, qk_conv_w[0]])))
    own_w_in = _bf(w_in[0])
    fsend, frecv, fsrc, fland, ftok = _xchg_start([own_w_in], [lax.empty((4,) + own_w_in.shape, BF16)],
                                                  name="gather_w_in_start", scatter=False, dep=gw["qk_conv_w"])
    late_names = tuple(n for n in _BIG if n != "w_in")
    cat = lambda a: jnp.transpose(a, (1, 0, 2)).reshape(a.shape[1], 4 * a.shape[2])
    w = dict(
        meta_tokens=cat(gw["meta_tokens"]), ln0_g=ln0_g[None], ln0_b=_tie(ln0_b[None], ftok),
        qk_conv_w=cat(gw["qk_conv_w"]), qk_conv_b=qk_conv_b,
        s5_lambda_re=s5_lambda_re[0], s5_lambda_im=s5_lambda_im[0], s5_log_dt=s5_log_dt[0][:, None],
        s5_b_re=s5_b_re[0], s5_b_im=s5_b_im[0], s5_c_re=s5_c_re[0], s5_c_im=s5_c_im[0], s5_d=s5_d,
        m_norm_g=m_norm_g, ln1_g=ln1_g, ln1_b=ln1_b, b_up=b_up, ln2_g=ln2_g, ln2_b=ln2_b)
    in_flight = {}

    def place_own(src, land):
        return lax.dynamic_update_slice(land, src[None], (chip,) + (0,) * src.ndim)

    small_names = _SMALL + _SMALL_SHARDED

    def view(n, a):
        return jnp.swapaxes(a, -1, -2) if n in ("s5_b_re", "s5_b_im") else a

    small_wmv = [_pack([view(n, dct[n]) for n in small_names]) for dct in (wts, mom, var)]

    def early(after):
        src, land = _xchg_wait(fsend, frecv, fsrc, fland, tuple(after) + tuple(small_wmv), name="gather_w_in_wait",
                               scatter=False)
        late_src = [_bf(wts[n][0]) for n in late_names]
        st = _xchg_start(late_src, [lax.empty((4,) + a.shape, a.dtype) for a in late_src], name="gather_late_start",
                         scatter=False, dep=src[0])
        in_flight["late"] = st[:4]
        return dict(w_in=_w_in_from_slots(place_own(src[0], land[0]), IN_CHUNK), b_in=_tie(_to_pad_cols(b_in), st[4]))

    def late(after):
        src, land = _xchg_wait(*in_flight["late"], after, name="gather_late_wait", scatter=False)
        full = {n: place_own(s, ld) for n, s, ld in zip(late_names, src, land)}
        return dict(s5_w_glu=full["s5_w_glu"], m_w_out=full["m_w_out"].reshape(d, d), w_o=full["w_o"].reshape(d, d),
                    w_up=full["w_up"], w_down=full["w_down"].reshape(4 * d, d))

    flying = []

    def ready(names, g):
        parts = dict(
            w_in=lambda: _slots_from_w_in(g["w_in"][0]), s5_w_glu=lambda: g["s5_w_glu"],
            m_w_out=lambda: g["m_w_out"].reshape(4, d // 4, d), w_o=lambda: g["w_o"].reshape(4, d // 4, d),
            w_up=lambda: g["w_up"], w_down=lambda: g["w_down"].reshape(4, d, d))
        src = [parts[n]() for n in names]
        land = [lax.empty((3,) + a.shape[1:], a.dtype) for a in src]
        st = _xchg_start(src, land, name="scatter_" + names[0] + "_start", scatter=True)
        flying.append((names,) + st[:4])
        return st[4]

    loss, grad_x, g = _local_step(x, loss_target, w, early, late, ready)
    g["b_in"] = _from_pad_cols(g["b_in"])

    res = {}

    def flat(a):
        return jnp.swapaxes(a, -1, -2).reshape(a.shape[:-2] + (-1, 128))

    def unflat(y, shape):
        return jnp.swapaxes(y.reshape(shape[:-2] + (shape[-1], shape[-2])), -1, -2)

    def finish(groups, after, tag):
        mine = {}
        for names, send, recv, src, land in groups:
            src, land = _xchg_wait(send, recv, src, land, after, name="scatter_" + names[0] + "_wait", scatter=True)
            for n, s, ld in zip(names, src, land):
                mine[n] = _sum_slots(s, ld, chip)
        theirs = _swap_cores(list(mine.values()), name="swap_cores_" + tag)
        for n, t in zip(mine, theirs):
            if n == "w_in":
                res[n] = [unflat(r, wts[n].shape) for r in _adamw(flat(wts[n]), flat(mom[n]), flat(var[n]),
                                                                  flat(mine[n]), flat(t))]
            else:
                res[n] = _adamw(wts[n], mom[n], var[n], mine[n], t)

    finish(flying[:-1], g["ln0_g"], "a")

    small_shapes = [(1, 128)] + [view(n, wts[n]).shape for n in _SMALL] + [g[n].shape for n in _SMALL_SHARDED]
    packed = _pack([loss] + [view(n, g[n]) for n in _SMALL] + [g[n] for n in _SMALL_SHARDED])
    tot = _unpack(_allreduce_small(packed, dep=res["w_o"][3]), small_shapes)
    loss_out = tot[0][0, 0]
    gsm = dict(zip(_SMALL + _SMALL_SHARDED, tot[1:]))
    for n in _SMALL_SHARDED:
        cols = wts[n].shape[-1]
        gsm[n] = lax.dynamic_slice_in_dim(gsm[n], chip * cols, cols, axis=1).reshape(wts[n].shape)

    names = small_names
    shapes = [view(n, wts[n]).shape for n in names]
    small_out = _adamw(*small_wmv, _pack([gsm[n] for n in names]))
    small_res = [_unpack(r, shapes) for r in small_out]
    for j, n in enumerate(names):
        res[n] = [view(n, small_res[q][j]) for q in range(4)]
    finish(flying[-1:], small_out[0], "b")

    return (loss_out, grad_x, *[res[n][0] for n in _ORDER], *[res[n][1] for n in _ORDER],
            *[res[n][2] for n in _ORDER], *[res[n][3] for n in _ORDER])
```

```python
import functools
import math

import jax
import jax.numpy as jnp
from jax import lax
from jax.experimental import pallas as pl
from jax.experimental.pallas import tpu as pltpu

F32 = jnp.float32
BF16 = jnp.bfloat16
HI = lax.Precision.HIGHEST

N_META = 16
M_HEADS = 4
M_CHUNK = 128
PAD = M_CHUNK - N_META
CONV_W = 4
HALO_ROWS = 16
S5_GROUP = 16
S5_STATE = 64
S5_KCH = 4
LN_EPS = 1e-5
ALPHA = 2.0 ** 0.25
NEG = -1e30
ADAM_LR, ADAM_B1, ADAM_B2, ADAM_EPS, ADAM_WD, ADAM_STEP = 0.001, 0.9, 0.999, 1e-08, 0.01, 10

O_OFF, GS_OFF, GM_OFF, V_OFF, Q_OFF, K_OFF, U_OFF, G_OFF, NP = 0, 1024, 2048, 3072, 4096, 4608, 5120, 5632, 5760

NN = ((1,), (0,))
NT = ((1,), (1,))
TN = ((0,), (0,))


def _dot(a, b, dims=NN, prec=None):
    return lax.dot_general(a, b, (dims, ((), ())), preferred_element_type=F32, precision=prec)


def _bf(x):
    return x.astype(BF16)


def _sig(x):
    return 0.5 * jnp.tanh(0.5 * x) + 0.5


def _pcall(body, **kw):
    return pl.pallas_call(body, **kw)


def _cp(sem=None, vmem_mb=None):
    kw = {}
    if sem is not None:
        kw["dimension_semantics"] = sem
    if vmem_mb is not None:
        kw["vmem_limit_bytes"] = vmem_mb << 20
    return pltpu.CompilerParams(**kw)


def _row_tile(n, want, mult=16):
    best = None
    for t in range(mult, want + 1, mult):
        if n % t == 0:
            best = t
    assert best is not None, (n, want)
    return best


def _resident(shape):
    nd = len(shape)
    return pl.BlockSpec(shape, lambda *_: (0,) * nd, pipeline_mode=pl.Buffered(1))


def _const(shape):
    nd = len(shape)
    return pl.BlockSpec(shape, lambda *_: (0,) * nd)


def _ln_fwd(x, g, b):
    mu = jnp.mean(x, axis=-1, keepdims=True)
    xc = x - mu
    var = jnp.mean(xc * xc, axis=-1, keepdims=True)
    rstd = lax.rsqrt(var + LN_EPS)
    xhat = xc * rstd
    return xhat * g + b, xhat, rstd


def _ln_bwd(dy, xhat, rstd, g):
    dxh = dy * g
    m1 = jnp.mean(dxh, axis=-1, keepdims=True)
    m2 = jnp.mean(dxh * xhat, axis=-1, keepdims=True)
    return rstd * (dxh - m1 - xhat * m2)


def _colsum(x):
    return jnp.sum(x, axis=0, keepdims=True)


def _to_pad_cols(w):
    u, q, k, v, o, gi, gf, gs, gm = (w[..., 0:512], w[..., 512:1024], w[..., 1024:1536], w[..., 1536:2560],
                                     w[..., 2560:3584], w[..., 3584:3588], w[..., 3588:3592], w[..., 3592:4616],
                                     w[..., 4616:5640])
    z = jnp.zeros(w.shape[:-1] + (NP - G_OFF - 8,), w.dtype)
    return jnp.concatenate([o, gs, gm, v, q, k, u, gi, gf, z], axis=-1)


def _from_pad_cols(w):
    o, gs, gm, v, q, k, u = (w[..., O_OFF:GS_OFF], w[..., GS_OFF:GM_OFF], w[..., GM_OFF:V_OFF], w[..., V_OFF:Q_OFF],
                             w[..., Q_OFF:K_OFF], w[..., K_OFF:U_OFF], w[..., U_OFF:G_OFF])
    gi, gf = w[..., G_OFF:G_OFF + 4], w[..., G_OFF + 4:G_OFF + 8]
    return jnp.concatenate([u, q, k, v, o, gi, gf, gs, gm], axis=-1)


_IN_REF = (("u", 512), ("q", 512), ("k", 512), ("v", 1024), ("o", 1024), ("i", 4), ("f", 4), ("gs", 1024), ("gm", 1024))
_IN_PAD = (("o", O_OFF), ("gs", GS_OFF), ("gm", GM_OFF), ("v", V_OFF), ("q", Q_OFF), ("k", K_OFF), ("u", U_OFF),
           ("i", G_OFF), ("f", G_OFF + 4))


def _in_ref_ranges():
    out, off = {}, 0
    for n, s in _IN_REF:
        out[n] = (off, off + s)
        off += s
    return out, off


def _w_in_from_slots(g, chunk=None):
    rng, total = _in_ref_ranges()
    width = total // g.shape[0]
    cols = []
    for n, _ in _IN_PAD:
        a, b = rng[n]
        while a < b:
            s = a // width
            e = min(b, (s + 1) * width)
            cols.append(g[s][:, a - s * width:e - s * width])
            a = e
    cols.append(jnp.zeros((g.shape[1], NP - G_OFF - 8), g.dtype))
    if chunk is None:
        return jnp.concatenate(cols, axis=1)
    chunks, cur, room = [], [], chunk
    for c in cols:
        while c.shape[1] > 0:
            take = min(room, c.shape[1])
            cur.append(c[:, :take])
            c, room = c[:, take:], room - take
            if room == 0:
                chunks.append(jnp.concatenate(cur, axis=1))
                cur, room = [], chunk
    assert not cur
    return jnp.stack(chunks, axis=0)


def _slots_from_w_in(wp, nslot=4):
    rng, total = _in_ref_ranges()
    width = total // nslot
    pad_off = dict(_IN_PAD)
    slots = []
    for s in range(nslot):
        lo, hi = s * width, (s + 1) * width
        cols = []
        for n, _ in _IN_REF:
            a, b = rng[n]
            x0, x1 = max(a, lo), min(b, hi)
            if x0 < x1:
                cols.append(wp[:, pad_off[n] + x0 - a:pad_off[n] + x1 - a])
        slots.append(jnp.concatenate(cols, axis=1))
    return jnp.stack(slots, axis=0)


HEAD = PAD + N_META


def _ln0_in(j, x_ref, meta_ref):
    first = jnp.concatenate([jnp.zeros((PAD, meta_ref.shape[1]), F32), meta_ref[...]], axis=0)
    return jnp.where(j == 0, first[None], x_ref[...])


def _ln0_fwd(x, meta, g, b):
    bsz, seq, d = x.shape
    nb = seq // HEAD + 1

    def body(x_ref, m_ref, g_ref, b_ref, o_ref, ob_ref):
        y, _, _ = _ln_fwd(_ln0_in(pl.program_id(0), x_ref, m_ref), g_ref[...], b_ref[...])
        o_ref[...] = y
        ob_ref[...] = _bf(y)

    row = pl.BlockSpec((bsz, HEAD, d), lambda j: (0, j, 0))
    h0, h0b = _pcall(
        body, name="ln0_fwd", grid=(nb,),
        in_specs=[pl.BlockSpec((bsz, HEAD, d), lambda j: (0, jnp.maximum(j - 1, 0), 0)), _const((N_META, d)),
                  _const((1, d)), _const((1, d))],
        out_specs=[row, row],
        out_shape=[jax.ShapeDtypeStruct((bsz, nb * HEAD, d), F32), jax.ShapeDtypeStruct((bsz, nb * HEAD, d), BF16)],
        compiler_params=_cp(("arbitrary",)),
    )(x, meta, g, b)
    return h0.reshape(-1, d), h0b.reshape(-1, d)


def _ln0_bwd(x, meta, dr1, dpw, g):
    bsz, seq, d = x.shape
    nb = seq // HEAD + 1

    def body(x_ref, m_ref, a_ref, c_ref, g_ref, o_ref, dg_ref, db_ref, dm_ref):
        j = pl.program_id(0)

        @pl.when(j == 0)
        def _():
            dg_ref[...] = jnp.zeros_like(dg_ref)
            db_ref[...] = jnp.zeros_like(db_ref)
            dm_ref[...] = jnp.zeros_like(dm_ref)

        dy = ALPHA * a_ref[...] + c_ref[...]
        _, xhat, rstd = _ln_fwd(_ln0_in(j, x_ref, m_ref), g_ref[...], 0.0)
        dx = _ln_bwd(dy, xhat, rstd, g_ref[...])
        o_ref[...] = dx
        dg_ref[...] += _colsum((dy * xhat).reshape(bsz * HEAD, d))
        db_ref[...] += _colsum(dy.reshape(bsz * HEAD, d))

        @pl.when(j == 0)
        def _():
            dm_ref[...] += jnp.sum(dx[:, PAD:, :], axis=0)

    row = pl.BlockSpec((bsz, HEAD, d), lambda j: (0, j, 0))
    tok = pl.BlockSpec((bsz, HEAD, d), lambda j: (0, jnp.maximum(j - 1, 0), 0))
    lp = nb * HEAD
    return _pcall(
        body, name="ln0_bwd", grid=(nb,),
        in_specs=[tok, _const((N_META, d)), row, row, _const((1, d))],
        out_specs=[tok, _const((1, d)), _const((1, d)), _const((N_META, d))],
        out_shape=[jax.ShapeDtypeStruct((bsz, seq, d), F32), jax.ShapeDtypeStruct((1, d), F32),
                   jax.ShapeDtypeStruct((1, d), F32), jax.ShapeDtypeStruct((N_META, d), F32)],
        compiler_params=_cp(("arbitrary",)),
    )(x, meta, dr1.reshape(bsz, lp, d), dpw.reshape(bsz, lp, d), g)


IN_CHUNK = 1152


def _chunk_cols(w):
    k, n = w.shape
    return jnp.transpose(w.reshape(k, n // IN_CHUNK, IN_CHUNK), (1, 0, 2))


def _inproj(h0b, w3, bias, lp):
    r, d = h0b.shape
    nj, _, tn = w3.shape
    tm = _row_tile(lp, 2112)
    tps = lp // tm

    def body(a_ref, w_ref, b_ref, o_ref, gate_ref):
        i = pl.program_id(0)
        j = pl.program_id(1)
        acc = _dot(a_ref[...], w_ref[j]) + b_ref[...]
        t = (i % tps) * tm + lax.broadcasted_iota(jnp.int32, (tm, 1), 0)
        acc = jnp.where(t >= PAD, acc, 0.0)
        o_ref[...] = _bf(acc)

        @pl.when(j == nj - 1)
        def _():
            gate_ref[...] = acc[:, tn - 128:]

    return _pcall(
        body, name="inproj", grid=(r // tm, nj),
        in_specs=[pl.BlockSpec((tm, d), lambda i, j: (i, 0)), _resident(w3.shape),
                  pl.BlockSpec((1, tn), lambda i, j: (0, j))],
        out_specs=[pl.BlockSpec((tm, tn), lambda i, j: (i, j)), pl.BlockSpec((tm, 128), lambda i, j: (i, 0))],
        out_shape=[jax.ShapeDtypeStruct((r, nj * tn), BF16), jax.ShapeDtypeStruct((r, 128), F32)],
        compiler_params=_cp(("parallel", "arbitrary"), 48),
    )(h0b, w3, bias)


def _mm_tn(a, b, *, name, split=1, colsum=False, tk_want=2112):
    r, m = a.shape
    n = b.shape[1]
    tk = _row_tile(r, tk_want)
    tm = min(m, 1024)
    ns = n // split
    tn = ns
    for cand in (1024, 1152, 640, 512, 128):
        if ns % cand == 0 and cand <= ns:
            tn = cand
            break
    nb = ns // tn
    nk = r // tk

    def body(a_ref, b_ref, o_ref, *rest):
        acc = rest[-1]
        k = pl.program_id(2)

        @pl.when(k == 0)
        def _():
            acc[...] = jnp.zeros_like(acc)

        bt = b_ref[...]
        acc[...] += _dot(_bf(a_ref[...]), _bf(bt), TN)

        @pl.when(k == nk - 1)
        def _():
            o_ref[...] = acc[...]

        if colsum:
            cs_ref = rest[0]

            @pl.when(k == 0)
            def _():
                cs_ref[...] = jnp.zeros_like(cs_ref)

            cs_ref[...] += _colsum(bt.astype(F32))

    out_specs = [pl.BlockSpec((None, tm, tn), lambda i, j, k: (j // nb, i, j % nb))]
    out_shape = [jax.ShapeDtypeStruct((split, m, ns), F32)]
    if colsum:
        assert m == tm
        out_specs.append(pl.BlockSpec((1, tn), lambda i, j, k: (0, j)))
        out_shape.append(jax.ShapeDtypeStruct((1, n), F32))
    res = _pcall(
        body, name=name, grid=(m // tm, n // tn, nk),
        in_specs=[pl.BlockSpec((tk, tm), lambda i, j, k: (k, i)), pl.BlockSpec((tk, tn), lambda i, j, k: (k, j))],
        out_specs=out_specs, out_shape=out_shape,
        scratch_shapes=[pltpu.VMEM((tm, tn), F32)],
        compiler_params=_cp(("parallel", "parallel", "arbitrary"), 56),
    )(a, b)
    return res if colsum else res[0]


def _mm_nt(a, w3, lp, *, name, dep=None):
    r, kdim = a.shape
    nk, n, tk = w3.shape
    assert nk * tk == kdim
    tm = _row_tile(lp, 1056)
    deps = [] if dep is None else [dep]

    def body(a_ref, w_ref, *rest):
        o_ref, acc = rest[-2:]
        k = pl.program_id(1)

        @pl.when(k == 0)
        def _():
            acc[...] = jnp.zeros_like(acc)

        acc[...] += _dot(_bf(a_ref[...]), w_ref[k], NT)

        @pl.when(k == nk - 1)
        def _():
            o_ref[...] = acc[...]

    return _pcall(
        body, name=name, grid=(r // tm, nk),
        in_specs=[pl.BlockSpec((tm, tk), lambda i, k: (i, k)), _resident(w3.shape)]
        + [_const(dp_.shape) for dp_ in deps],
        out_specs=pl.BlockSpec((tm, n), lambda i, k: (i, 0)),
        out_shape=jax.ShapeDtypeStruct((r, n), F32),
        scratch_shapes=[pltpu.VMEM((tm, n), F32)],
        compiler_params=_cp(("parallel", "arbitrary"), 48),
    )(a, w3, *deps)


def _s5_prep(lam_re, lam_im, log_dt, b_re_t, b_im_t):
    g, p = lam_re.shape
    h = b_re_t.shape[0]

    def body(lr_ref, li_ref, ldt_ref, br_ref, bi_ref, pr_ref, pi_ref, bbr_ref, bbi_ref):
        lr, li = lr_ref[...], li_ref[...]
        dt = jnp.exp(ldt_ref[...])
        e = jnp.exp(lr * dt)
        ar, ai = e * jnp.cos(li * dt), e * jnp.sin(li * dt)
        den = lr * lr + li * li
        cr = ((ar - 1.0) * lr + ai * li) / den
        ci = (ai * lr - (ar - 1.0) * li) / den
        br, bi = br_ref[...], bi_ref[...]
        bbr_ref[...] = cr[None] * br - ci[None] * bi
        bbi_ref[...] = cr[None] * bi + ci[None] * br
        xr, xi = ar, ai
        pr_ref[0] = xr
        pi_ref[0] = xi
        for t in range(1, 8):
            xr, xi = xr * ar - xi * ai, xr * ai + xi * ar
            pr_ref[t] = xr
            pi_ref[t] = xi

    sd = jax.ShapeDtypeStruct
    return _pcall(body, name="s5_prep",
                  out_shape=[sd((8, g, p), F32), sd((8, g, p), F32), sd((h, g, p), F32), sd((h, g, p), F32)])(
        lam_re, lam_im, log_dt, b_re_t, b_im_t)


def _s5_prep_bwd(lam_re, lam_im, log_dt, b_re_t, b_im_t, da_re, da_im, dbb_re_t, dbb_im_t):
    g, p = lam_re.shape
    h = b_re_t.shape[0]

    def body(lr_ref, li_ref, ldt_ref, br_ref, bi_ref, dar_ref, dai_ref, dbr_ref, dbi_ref,
             glr_ref, gli_ref, gdt_ref, gbr_ref, gbi_ref):
        lr, li = lr_ref[...], li_ref[...]
        dt = jnp.exp(ldt_ref[...])
        e = jnp.exp(lr * dt)
        ar, ai = e * jnp.cos(li * dt), e * jnp.sin(li * dt)
        den = lr * lr + li * li
        cr = ((ar - 1.0) * lr + ai * li) / den
        ci = (ai * lr - (ar - 1.0) * li) / den
        br, bi = br_ref[...], bi_ref[...]
        gr, gi = dbr_ref[...], dbi_ref[...]
        gbr_ref[...] = gr * cr[None] + gi * ci[None]
        gbi_ref[...] = gi * cr[None] - gr * ci[None]
        gcr = jnp.sum(gr * br + gi * bi, axis=0)
        gci = jnp.sum(gi * br - gr * bi, axis=0)
        ilr, ili = lr / den, -li / den
        gar = dar_ref[...] + gcr * ilr + gci * ili
        gai = dai_ref[...] + gci * ilr - gcr * ili
        qr, qi = cr * ilr - ci * ili, cr * ili + ci * ilr
        glr = -(gcr * qr + gci * qi)
        gli = -(gci * qr - gcr * qi)
        gzr = gar * ar + gai * ai
        gzi = gai * ar - gar * ai
        glr_ref[...] = glr + gzr * dt
        gli_ref[...] = gli + gzi * dt
        gdt_ref[...] = jnp.sum(gzr * lr + gzi * li, axis=1, keepdims=True) * dt

    sd = jax.ShapeDtypeStruct
    return _pcall(body, name="s5_prep_bwd",
                  out_shape=[sd((g, p), F32), sd((g, p), F32), sd((g, 1), F32), sd((h, g, p), F32), sd((h, g, p), F32)])(
        lam_re, lam_im, log_dt, b_re_t, b_im_t, da_re, da_im, dbb_re_t, dbb_im_t)


def _cmul(xr, xi, yr, yi):
    return xr * yr - xi * yi, xr * yi + xi * yr


def _dot5(a, b, dims=NN):
    return _dot(_bf(a), _bf(b), dims)


def _s5_fwd(p3, bk, cre, cim, apow, dskip):
    bsz, lp, _ = p3.shape
    tt = _row_tile(lp, 528, 8)
    nt = lp // tt
    nblk = tt // 8
    hw = 512

    def body(u_ref, bk_ref, cre_ref, cim_ref, ap_ref, d_ref, y_ref, xs_ref, car_ref):
        t = pl.program_id(2)

        @pl.when(t == 0)
        def _():
            car_ref[...] = jnp.zeros_like(car_ref)

        u = u_ref[...].astype(F32)
        xs_ref[...] = _dot5(u, bk_ref[...])
        ap = ap_ref[...]
        apr, api = ap[:, :hw], ap[:, hw:]
        rows = lax.broadcasted_iota(jnp.int32, (8, hw), 0)
        lev = [(d, jnp.where(rows < d, 0.0, jnp.broadcast_to(apr[d - 1:d, :], (8, hw))),
                jnp.where(rows < d, 0.0, jnp.broadcast_to(api[d - 1:d, :], (8, hw)))) for d in (1, 2, 4)]

        def blk(i, carry):
            cr, ci = carry
            off = pl.multiple_of(i * 8, 8)
            x = xs_ref[pl.ds(off, 8), :]
            xr, xi = x[:, :hw], x[:, hw:]
            for d, lr, li in lev:
                mr, mi = _cmul(pltpu.roll(xr, d, 0), pltpu.roll(xi, d, 0), lr, li)
                xr, xi = xr + mr, xi + mi
            mr, mi = _cmul(apr, api, cr, ci)
            xr, xi = xr + mr, xi + mi
            xs_ref[pl.ds(off, 8), :] = jnp.concatenate([xr, xi], axis=1)
            return xr[7:8, :], xi[7:8, :]

        c0 = car_ref[...]
        cr, ci = lax.fori_loop(0, nblk, blk, (c0[0:1, :hw], c0[0:1, hw:]), unroll=2)
        car_ref[...] = jnp.broadcast_to(jnp.concatenate([cr, ci], axis=1), car_ref.shape)
        xs = xs_ref[...]
        y_ref[...] = (_dot5(xs[:, :hw], cre_ref[...]) - _dot5(xs[:, hw:], cim_ref[...])
                      + d_ref[...] * u)

    ub = U_OFF // 128
    return _pcall(
        body, name="s5_fwd", grid=(S5_KCH, bsz, nt),
        in_specs=[pl.BlockSpec((None, tt, 128), lambda k, b, t: (b, t, ub + k)),
                  pl.BlockSpec((None, 128, 2 * hw), lambda k, b, t: (k, 0, 0)),
                  pl.BlockSpec((None, hw, 128), lambda k, b, t: (k, 0, 0)),
                  pl.BlockSpec((None, hw, 128), lambda k, b, t: (k, 0, 0)),
                  pl.BlockSpec((None, 8, 2 * hw), lambda k, b, t: (k, 0, 0)),
                  pl.BlockSpec((1, 128), lambda k, b, t: (0, k))],
        out_specs=[pl.BlockSpec((None, tt, 128), lambda k, b, t: (b, t, k)),
                   pl.BlockSpec((None, None, tt, 2 * hw), lambda k, b, t: (b, k, t, 0))],
        out_shape=[jax.ShapeDtypeStruct((bsz, lp, S5_KCH * 128), F32),
                   jax.ShapeDtypeStruct((bsz, S5_KCH, lp, 2 * hw), F32)],
        scratch_shapes=[pltpu.VMEM((8, 2 * hw), F32)],
        compiler_params=_cp(("parallel", "parallel", "arbitrary"), 40),
    )(p3, bk, cre, cim, apow, dskip)


def _s5_bwd(dp3, p3, dy3, xs, bk, cre, cim, apow_rev, dskip):
    bsz, lp, _ = p3.shape
    tt = _row_tile(lp, 528, 8)
    nt = lp // tt
    nblk = tt // 8
    hw = 512
    tb = tt // 8

    def body(dp_any, u_ref, dy_ref, xs_ref, halo_ref, bkt_ref, cre_ref, cim_ref, ap_ref, d_ref,
             du_ref, dbk_ref, dcre_ref, dcim_ref, da_ref, dd_ref, g_ref, ext_ref, car_ref):
        b = pl.program_id(1)
        t = pl.program_id(2)
        tidx = nt - 1 - t

        @pl.when(t == 0)
        def _():
            car_ref[...] = jnp.zeros_like(car_ref)

        @pl.when((b == 0) & (t == 0))
        def _():
            dbk_ref[...] = jnp.zeros_like(dbk_ref)
            dcre_ref[...] = jnp.zeros_like(dcre_ref)
            dcim_ref[...] = jnp.zeros_like(dcim_ref)
            da_ref[...] = jnp.zeros_like(da_ref)
            dd_ref[...] = jnp.zeros_like(dd_ref)

        u = u_ref[...].astype(F32)
        dy = dy_ref[...]
        g_ref[:, :hw] = _dot5(dy, cre_ref[...])
        g_ref[:, hw:] = -_dot5(dy, cim_ref[...])
        ap = ap_ref[...]
        apr, api = ap[:, :hw], -ap[:, hw:]
        rows = lax.broadcasted_iota(jnp.int32, (8, hw), 0)
        lev = [(d, jnp.where(rows >= 8 - d, 0.0, jnp.broadcast_to(apr[8 - d:9 - d, :], (8, hw))),
                jnp.where(rows >= 8 - d, 0.0, jnp.broadcast_to(api[8 - d:9 - d, :], (8, hw)))) for d in (1, 2, 4)]

        def blk(i, carry):
            cr, ci = carry
            off = pl.multiple_of((nblk - 1 - i) * 8, 8)
            x = g_ref[pl.ds(off, 8), :]
            xr, xi = x[:, :hw], x[:, hw:]
            for d, lr, li in lev:
                mr, mi = _cmul(pltpu.roll(xr, 8 - d, 0), pltpu.roll(xi, 8 - d, 0), lr, li)
                xr, xi = xr + mr, xi + mi
            mr, mi = _cmul(apr, api, cr, ci)
            xr, xi = xr + mr, xi + mi
            g_ref[pl.ds(off, 8), :] = jnp.concatenate([xr, xi], axis=1)
            return xr[0:1, :], xi[0:1, :]

        c0 = car_ref[...]
        cr, ci = lax.fori_loop(0, nblk, blk, (c0[0:1, :hw], c0[0:1, hw:]), unroll=2)
        car_ref[...] = jnp.broadcast_to(jnp.concatenate([cr, ci], axis=1), car_ref.shape)

        gg = g_ref[...]
        du = _dot5(gg, bkt_ref[...]) + d_ref[...] * dy
        trow = tidx * tt + lax.broadcasted_iota(jnp.int32, (tt, 1), 0)
        du_ref[...] = jnp.where(trow >= PAD, du, 0.0).astype(du_ref.dtype)
        dbk_ref[...] += _dot5(u, gg, TN)
        xsv = xs_ref[...]
        dcre_ref[...] += _dot5(dy, xsv[:, :hw], TN)
        dcim_ref[...] -= _dot5(dy, xsv[:, hw:], TN)
        dd_ref[...] += _colsum(dy * u)
        ext_ref[0:8, :] = jnp.where(tidx == 0, 0.0, halo_ref[...])
        ext_ref[8:, :] = xsv
        xp = ext_ref[pl.ds(7, tt), :]
        gr, gi, pr, pi = gg[:, :hw], gg[:, hw:], xp[:, :hw], xp[:, hw:]
        da_ref[:, :hw] += _colsum(gr * pr + gi * pi)
        da_ref[:, hw:] += _colsum(gi * pr - gr * pi)

    ub = U_OFF // 128
    sd = jax.ShapeDtypeStruct
    rt = lambda t: nt - 1 - t
    tr = lambda a: jnp.swapaxes(a, 1, 2)
    res = _pcall(
        body, name="s5_bwd", grid=(S5_KCH, bsz, nt),
        in_specs=[pl.BlockSpec(memory_space=pl.ANY),
                  pl.BlockSpec((None, tt, 128), lambda k, b, t: (b, rt(t), ub + k)),
                  pl.BlockSpec((None, tt, 128), lambda k, b, t: (b, rt(t), k)),
                  pl.BlockSpec((None, None, tt, 2 * hw), lambda k, b, t: (b, k, rt(t), 0)),
                  pl.BlockSpec((None, None, 8, 2 * hw), lambda k, b, t: (b, k, jnp.maximum(rt(t) * tb - 1, 0), 0)),
                  pl.BlockSpec((None, 2 * hw, 128), lambda k, b, t: (k, 0, 0)),
                  pl.BlockSpec((None, 128, hw), lambda k, b, t: (k, 0, 0)),
                  pl.BlockSpec((None, 128, hw), lambda k, b, t: (k, 0, 0)),
                  pl.BlockSpec((None, 8, 2 * hw), lambda k, b, t: (k, 0, 0)),
                  pl.BlockSpec((1, 128), lambda k, b, t: (0, k))],
        out_specs=[pl.BlockSpec((None, tt, 128), lambda k, b, t: (b, rt(t), ub + k)),
                   pl.BlockSpec((None, 128, 2 * hw), lambda k, b, t: (k, 0, 0)),
                   pl.BlockSpec((None, 128, hw), lambda k, b, t: (k, 0, 0)),
                   pl.BlockSpec((None, 128, hw), lambda k, b, t: (k, 0, 0)),
                   pl.BlockSpec((None, 1, 2 * hw), lambda k, b, t: (k, 0, 0)),
                   pl.BlockSpec((1, 128), lambda k, b, t: (0, k))],
        out_shape=[sd(dp3.shape, dp3.dtype), sd((S5_KCH, 128, 2 * hw), F32), sd((S5_KCH, 128, hw), F32),
                   sd((S5_KCH, 128, hw), F32), sd((S5_KCH, 1, 2 * hw), F32), sd((1, S5_KCH * 128), F32)],
        scratch_shapes=[pltpu.VMEM((tt, 2 * hw), F32), pltpu.VMEM((tt + 8, 2 * hw), F32), pltpu.VMEM((8, 2 * hw), F32)],
        input_output_aliases={0: 0},
        compiler_params=_cp(("arbitrary", "arbitrary", "arbitrary"), 48),
    )(dp3, p3, dy3, xs, xs, tr(bk), tr(cre), tr(cim), apow_rev, dskip)
    return res[0], res[1], tr(res[2]), tr(res[3]), res[4], res[5]


_G0 = math.sqrt(2.0 / math.pi)
_G1 = 0.044715


def _gelu(y):
    return 0.5 * y * (1.0 + jnp.tanh(_G0 * (y + _G1 * y * y * y)))


def _gelu_grad(y):
    th = jnp.tanh(_G0 * (y + _G1 * y * y * y))
    return 0.5 * (1.0 + th) + 0.5 * y * (1.0 - th * th) * _G0 * (1.0 + 3.0 * _G1 * y * y)


def _glu_fwd(y_s5, wglu_g, lp):
    r, w = y_s5.shape
    tm = _row_tile(lp, 416)
    cw = wglu_g.shape[2]

    def body(y_ref, w_ref, gy_ref, z_ref, o_ref):
        gy = _bf(_gelu(y_ref[...]))
        gy_ref[...] = gy
        zs = [_dot(gy, w_ref[s]) for s in range(4)]
        for s in range(4):
            z_ref[:, s * cw:(s + 1) * cw] = _bf(zs[s])
        o_ref[:, :cw] = _bf(zs[0] * _sig(zs[2]))
        o_ref[:, cw:] = _bf(zs[1] * _sig(zs[3]))

    sd = jax.ShapeDtypeStruct
    return _pcall(
        body, name="glu_fwd", grid=(r // tm,),
        in_specs=[pl.BlockSpec((tm, w), lambda i: (i, 0)), _resident(wglu_g.shape)],
        out_specs=[pl.BlockSpec((tm, w), lambda i: (i, 0)), pl.BlockSpec((tm, 4 * cw), lambda i: (i, 0)),
                   pl.BlockSpec((tm, 2 * cw), lambda i: (i, 0))],
        out_shape=[sd((r, w), BF16), sd((r, 4 * cw), BF16), sd((r, 2 * cw), BF16)],
        compiler_params=_cp(("parallel",), 40),
    )(y_s5, wglu_g)


def _glu_bwd(dyg, z, y_s5, wglu_g, lp):
    r, w = y_s5.shape
    tm = _row_tile(lp, 416)
    cw = wglu_g.shape[2]

    def body(d_ref, z_ref, y_ref, w_ref, dz_ref, dy_ref):
        d = d_ref[...].astype(F32)
        zz = z_ref[...].astype(F32)
        acc = jnp.zeros((tm, w), F32)
        for s in range(2):
            z1 = zz[:, s * cw:(s + 1) * cw]
            sg = _sig(zz[:, (2 + s) * cw:(3 + s) * cw])
            dd = d[:, s * cw:(s + 1) * cw]
            dz1 = _bf(dd * sg)
            dz2 = _bf(dd * z1 * sg * (1.0 - sg))
            dz_ref[:, s * cw:(s + 1) * cw] = dz1
            dz_ref[:, (2 + s) * cw:(3 + s) * cw] = dz2
            acc += _dot(dz1, w_ref[s], NT) + _dot(dz2, w_ref[2 + s], NT)
        dy_ref[...] = acc * _gelu_grad(y_ref[...])

    sd = jax.ShapeDtypeStruct
    return _pcall(
        body, name="glu_bwd", grid=(r // tm,),
        in_specs=[pl.BlockSpec((tm, 2 * cw), lambda i: (i, 0)), pl.BlockSpec((tm, 4 * cw), lambda i: (i, 0)),
                  pl.BlockSpec((tm, w), lambda i: (i, 0)), _resident(wglu_g.shape)],
        out_specs=[pl.BlockSpec((tm, 4 * cw), lambda i: (i, 0)), pl.BlockSpec((tm, w), lambda i: (i, 0))],
        out_shape=[sd((r, 4 * cw), BF16), sd((r, w), F32)],
        compiler_params=_cp(("parallel",), 40),
    )(dyg, z, y_s5, wglu_g)


def _conv_fwd(p3, cw, cb):
    bsz, lp, _ = p3.shape
    tt = _row_tile(lp, 704)
    nt = lp // tt
    tb = tt // 8
    c = cw.shape[1]
    qb = Q_OFF // c

    hr = HALO_ROWS
    off = hr - (CONV_W - 1)

    def body(x_ref, halo_ref, w_ref, b_ref, pre_ref, act_ref, ext_ref):
        t = pl.program_id(1)
        ext_ref[0:hr, :] = jnp.where(t == 0, 0.0, halo_ref[...].astype(F32))
        ext_ref[hr:, :] = x_ref[...].astype(F32)
        w = w_ref[...]
        acc = b_ref[...] + w[0:1, :] * ext_ref[pl.ds(off, tt), :]
        for j in range(1, CONV_W):
            acc = acc + w[j:j + 1, :] * ext_ref[pl.ds(off + j, tt), :]
        pre_ref[...] = _bf(acc)
        act_ref[...] = _bf(acc * _sig(acc))

    sd = jax.ShapeDtypeStruct
    return _pcall(
        body, name="conv_fwd", grid=(bsz, nt),
        in_specs=[pl.BlockSpec((None, tt, c), lambda b, t: (b, t, qb)),
                  pl.BlockSpec((None, hr, c), lambda b, t: (b, jnp.maximum(t * (tt // hr) - 1, 0), qb)),
                  _const((CONV_W, c)), _const((1, c))],
        out_specs=[pl.BlockSpec((None, tt, c), lambda b, t: (b, t, 0))] * 2,
        out_shape=[sd((bsz, lp, c), BF16)] * 2,
        scratch_shapes=[pltpu.VMEM((tt + hr, c), F32)],
        compiler_params=_cp(("parallel", "parallel")),
    )(p3, p3, cw, cb)


def _conv_bwd(dp3, p3, dact3, pre3, cw):
    bsz, lp, _ = p3.shape
    tt = _row_tile(lp, 704)
    nt = lp // tt
    tb = tt // 8
    c = cw.shape[1]
    qb = Q_OFF // c

    hr = HALO_ROWS
    off = hr - (CONV_W - 1)

    def silu_grad(x):
        s = _sig(x)
        return s * (1.0 + x * (1.0 - s))

    def body(dp_any, x_ref, xh_ref, d_ref, dh_ref, pre_ref, preh_ref, w_ref, o_ref, dw_ref, db_ref, ext_ref, dext_ref):
        b = pl.program_id(0)
        t = pl.program_id(1)

        @pl.when((b == 0) & (t == 0))
        def _():
            dw_ref[...] = jnp.zeros_like(dw_ref)
            db_ref[...] = jnp.zeros_like(db_ref)

        dc = d_ref[...].astype(F32) * silu_grad(pre_ref[...].astype(F32))
        dch = jnp.where(t == nt - 1, 0.0, dh_ref[...].astype(F32) * silu_grad(preh_ref[...].astype(F32)))
        dext_ref[0:tt, :] = dc
        dext_ref[tt:, :] = dch
        ext_ref[0:hr, :] = jnp.where(t == 0, 0.0, xh_ref[...].astype(F32))
        ext_ref[hr:, :] = x_ref[...].astype(F32)
        w = w_ref[...]
        acc = w[CONV_W - 1:CONV_W, :] * dc
        for j in range(CONV_W - 1):
            acc = acc + w[j:j + 1, :] * dext_ref[pl.ds(CONV_W - 1 - j, tt), :]
        trow = t * tt + lax.broadcasted_iota(jnp.int32, (tt, 1), 0)
        o_ref[...] = jnp.where(trow >= PAD, acc, 0.0).astype(o_ref.dtype)
        db_ref[...] += _colsum(dc)
        for j in range(CONV_W):
            dw_ref[j:j + 1, :] += _colsum(dc * ext_ref[pl.ds(off + j, tt), :])

    sd = jax.ShapeDtypeStruct
    nxt = lambda t: jnp.minimum((t + 1) * (tt // hr), lp // hr - 1)
    return _pcall(
        body, name="conv_bwd", grid=(bsz, nt),
        in_specs=[pl.BlockSpec(memory_space=pl.ANY),
                  pl.BlockSpec((None, tt, c), lambda b, t: (b, t, qb)),
                  pl.BlockSpec((None, hr, c), lambda b, t: (b, jnp.maximum(t * (tt // hr) - 1, 0), qb)),
                  pl.BlockSpec((None, tt, c), lambda b, t: (b, t, 0)),
                  pl.BlockSpec((None, hr, c), lambda b, t: (b, nxt(t), 0)),
                  pl.BlockSpec((None, tt, c), lambda b, t: (b, t, 0)),
                  pl.BlockSpec((None, hr, c), lambda b, t: (b, nxt(t), 0)),
                  _const((CONV_W, c))],
        out_specs=[pl.BlockSpec((None, tt, c), lambda b, t: (b, t, qb)), _const((CONV_W, c)), _const((1, c))],
        out_shape=[sd(dp3.shape, dp3.dtype), sd((CONV_W, c), F32), sd((1, c), F32)],
        scratch_shapes=[pltpu.VMEM((tt + hr, c), F32), pltpu.VMEM((tt + hr, c), F32)],
        input_output_aliases={0: 0},
        compiler_params=_cp(("arbitrary", "arbitrary")),
    )(dp3, p3, p3, dact3, dact3, pre3, pre3, cw)


def _mlstm_gates(g, h_idx, c_idx, lc):
    lane = lax.broadcasted_iota(jnp.int32, g.shape, 1)
    i_col = jnp.sum(jnp.where(lane == h_idx, g, 0.0), axis=1, keepdims=True)
    f_col = jnp.sum(jnp.where(lane == M_HEADS + h_idx, g, 0.0), axis=1, keepdims=True)
    row = lax.broadcasted_iota(jnp.int32, (lc, 1), 0)
    valid = (c_idx * lc + row) >= PAD
    li = jnp.where(valid, i_col, NEG)
    lf = jnp.where(valid, jnp.minimum(f_col, 0.0) - jnp.log(1.0 + jnp.exp(-jnp.abs(f_col))), 0.0)
    r2 = lax.broadcasted_iota(jnp.int32, (lc, lc), 0)
    c2 = lax.broadcasted_iota(jnp.int32, (lc, lc), 1)
    eye = r2 == c2
    tril = r2 >= c2
    to_row = lambda col: jnp.sum(jnp.where(eye, col, 0.0), axis=0, keepdims=True)
    lf_row = to_row(lf)
    b_col = jnp.sum(jnp.where(tril, lf_row, 0.0), axis=1, keepdims=True)
    b_row = to_row(b_col)
    li_row = to_row(li)
    d_mat = jnp.where(tril, b_col - b_row + li_row, NEG)
    return dict(f_col=f_col, valid=valid, li=li, b_col=b_col, d_mat=d_mat, eye=eye, r2=r2, c2=c2, row=row,
                to_row=to_row)


def _mlstm_chunk(q, ks, v, gq, c_st, n_st, m_st, lc):
    b_col, d_mat = gq["b_col"], gq["d_mat"]
    m_inter = b_col + m_st
    m_row = jnp.maximum(m_inter, jnp.max(d_mat, axis=1, keepdims=True))
    w_intra = jnp.exp(d_mat - m_row)
    w_inter = jnp.exp(m_inter - m_row)
    qb, kb, vb, cb = _bf(q), _bf(ks), _bf(v), _bf(c_st)
    s = _dot(qb, kb, NT) * w_intra
    qc = _dot(qb, cb)
    num = _dot(_bf(s), vb) + w_inter * qc
    qn = jnp.sum(q * n_st, axis=1, keepdims=True)
    den = jnp.sum(s, axis=1, keepdims=True) + w_inter * qn
    e = jnp.exp(-m_row)
    nn = jnp.maximum(jnp.abs(den), e)
    b_last = b_col[lc - 1:lc, :]
    g_log = b_last - b_col + gq["li"]
    m_new = jnp.maximum(b_last + m_st, jnp.max(g_log, axis=0, keepdims=True))
    w_k = jnp.exp(g_log - m_new)
    decay = jnp.exp(b_last + m_st - m_new)
    return dict(w_intra=w_intra, w_inter=w_inter, qb=qb, kb=kb, vb=vb, cb=cb, s=s, qc=qc, num=num, qn=qn, den=den,
                e=e, nn=nn, m_new=m_new, w_k=w_k, decay=decay)


def _chunks_per_step(nc):
    return max(c for c in (3, 2, 1) if nc % c == 0)


def _mlstm_fwd(qk3, p3, pg3):
    bsz, lp, _ = p3.shape
    lc = M_CHUNK
    nc = lp // lc
    dk, dv = 128, 256
    scale = dk ** -0.5

    cps = _chunks_per_step(nc)
    rows = cps * lc

    def body(q_ref, k_ref, v_ref, g_ref, h_ref, cs_ref, ns_ref, ms_ref, c_sc, n_sc, m_sc):
        st = pl.program_id(1)

        @pl.when(st == 0)
        def _():
            c_sc[...] = jnp.zeros_like(c_sc)
            n_sc[...] = jnp.zeros_like(n_sc)
            m_sc[...] = jnp.zeros_like(m_sc)

        for j in range(cps):
            rs = slice(j * lc, (j + 1) * lc)
            g = g_ref[rs, :]
            for hh in range(M_HEADS):
                c_st, n_st, m_all = c_sc[hh], n_sc[hh], m_sc[hh]
                cs_ref[hh, j] = c_st
                ns_ref[hh, j] = n_st
                ms_ref[hh, j] = m_all
                m_st = m_all[:, 0:1]
                q = q_ref[rs, hh * dk:(hh + 1) * dk].astype(F32)
                ks = k_ref[rs, hh * dk:(hh + 1) * dk].astype(F32) * scale
                v = v_ref[rs, hh * dv:(hh + 1) * dv]
                gq = _mlstm_gates(g, hh, st * cps + j, lc)
                f = _mlstm_chunk(q, ks, v, gq, c_st, n_st, m_st, lc)
                h_ref[rs, hh * dv:(hh + 1) * dv] = _bf(f["num"] / f["nn"])
                kw = ks * f["w_k"]
                c_sc[hh] = f["decay"] * c_st + _dot(_bf(kw), f["vb"], TN)
                n_sc[hh] = f["decay"] * n_st + _colsum(kw)
                m_sc[hh] = jnp.broadcast_to(f["m_new"], (1, 128))

    sd = jax.ShapeDtypeStruct
    nh = M_HEADS
    return _pcall(
        body, name="mlstm_fwd", grid=(bsz, nc // cps),
        in_specs=[pl.BlockSpec((None, rows, nh * dk), lambda b, c: (b, c, 0)),
                  pl.BlockSpec((None, rows, nh * dk), lambda b, c: (b, c, 1)),
                  pl.BlockSpec((None, rows, nh * dv), lambda b, c: (b, c, V_OFF // (nh * dv))),
                  pl.BlockSpec((None, rows, 128), lambda b, c: (b, c, 0))],
        out_specs=[pl.BlockSpec((None, rows, nh * dv), lambda b, c: (b, c, 0)),
                   pl.BlockSpec((None, nh, cps, dk, dv), lambda b, c: (b, 0, c, 0, 0)),
                   pl.BlockSpec((None, nh, cps, 1, dk), lambda b, c: (b, 0, c, 0, 0)),
                   pl.BlockSpec((None, nh, cps, 1, 128), lambda b, c: (b, 0, c, 0, 0))],
        out_shape=[sd((bsz, lp, nh * dv), BF16), sd((bsz, nh, nc, dk, dv), F32),
                   sd((bsz, nh, nc, 1, dk), F32), sd((bsz, nh, nc, 1, 128), F32)],
        scratch_shapes=[pltpu.VMEM((nh, dk, dv), F32), pltpu.VMEM((nh, 1, dk), F32), pltpu.VMEM((nh, 1, 128), F32)],
        compiler_params=_cp(("parallel", "arbitrary")),
    )(qk3, qk3, p3, pg3)


def _mlstm_bwd(dp3, qk3, p3, pg3, dh3, cs, ns, ms):
    bsz, lp, _ = p3.shape
    lc = M_CHUNK
    nc = lp // lc
    dk, dv = 128, 256
    scale = dk ** -0.5

    cps = _chunks_per_step(nc)
    nst = nc // cps
    rows = cps * lc

    def body(dp_any, q_ref, k_ref, v_ref, g_ref, dh_ref, cs_ref, ns_ref, ms_ref,
             dv_ref, dqk_ref, dg_ref, dc_sc, dn_sc):
        t = pl.program_id(1)
        st = nst - 1 - t

        @pl.when(t == 0)
        def _():
            dc_sc[...] = jnp.zeros_like(dc_sc)
            dn_sc[...] = jnp.zeros_like(dn_sc)

        lane = lax.broadcasted_iota(jnp.int32, (lc, 128), 1)
        for j in reversed(range(cps)):
            rs = slice(j * lc, (j + 1) * lc)
            g = g_ref[rs, :]
            dgate = jnp.zeros((lc, 128), F32)
            for hh in range(M_HEADS):
                dgate = head(hh, j, rs, st * cps + j, g, lane, dgate, q_ref, k_ref, v_ref, dh_ref, cs_ref, ns_ref,
                             ms_ref, dv_ref, dqk_ref, dc_sc, dn_sc)
            dg_ref[rs, :] = dgate.astype(dg_ref.dtype)

    def head(hh, j, sl, c, g, lane, dgate, q_ref, k_ref, v_ref, dh_ref, cs_ref, ns_ref, ms_ref, dv_ref, dqk_ref,
             dc_sc, dn_sc):
        c_st, n_st = cs_ref[hh, j], ns_ref[hh, j]
        m_st = ms_ref[hh, j][:, 0:1]
        q = q_ref[sl, hh * dk:(hh + 1) * dk].astype(F32)
        ks = k_ref[sl, hh * dk:(hh + 1) * dk].astype(F32) * scale
        v = v_ref[sl, hh * dv:(hh + 1) * dv]
        dh = dh_ref[sl, hh * dv:(hh + 1) * dv].astype(F32)
        gq = _mlstm_gates(g, hh, c, lc)
        f = _mlstm_chunk(q, ks, v, gq, c_st, n_st, m_st, lc)
        eye, r2, c2, row, valid = gq["eye"], gq["r2"], gq["c2"], gq["row"], gq["valid"]
        w_intra, w_inter, s, nn, den = f["w_intra"], f["w_inter"], f["s"], f["nn"], f["den"]
        qb, kb, vb, cb, w_k, decay = f["qb"], f["kb"], f["vb"], f["cb"], f["w_k"], f["decay"]
        d_c, d_n = dc_sc[hh], dn_sc[hh]
        d_cb = _bf(d_c)

        hout = f["num"] / nn
        dnum = dh / nn
        d_nn = -jnp.sum(dh * hout, axis=1, keepdims=True) / nn
        dden = jnp.where(jnp.abs(den) > f["e"], d_nn * jnp.sign(den), 0.0)
        wdnum = w_inter * dnum
        wdden = w_inter * dden
        ds = _dot(_bf(dnum), vb, NT) + dden
        dsw = _bf(ds * w_intra)
        dq = _dot(dsw, kb) + _dot(_bf(wdnum), cb, NT) + wdden * n_st
        dkw = _dot(vb, d_cb, NT) + d_n
        dks = _dot(dsw, qb, TN) + dkw * w_k
        kw = ks * w_k
        dvv = _dot(_bf(s), _bf(dnum), TN) + _dot(_bf(kw), d_cb)
        dd = ds * s
        rs = jnp.sum(dd, axis=1, keepdims=True)
        cs_col = jnp.sum(jnp.where(eye, jnp.sum(dd, axis=0, keepdims=True), 0.0), axis=1, keepdims=True)
        dwi = jnp.sum(dnum * f["qc"], axis=1, keepdims=True) + dden * f["qn"]
        db = rs - cs_col + dwi * w_inter
        dli = cs_col
        ddecay = jnp.sum(jnp.sum(d_c * c_st, axis=1, keepdims=True), axis=0, keepdims=True) \
            + jnp.sum(d_n * n_st, axis=1, keepdims=True)
        dgl = jnp.sum(dkw * ks, axis=1, keepdims=True) * w_k
        dblast = ddecay * decay + jnp.sum(dgl, axis=0, keepdims=True)
        db = db - dgl + jnp.where(row == lc - 1, dblast, 0.0)
        dli = dli + dgl
        db_row = gq["to_row"](db)
        dlf = jnp.sum(jnp.where(c2 >= r2, db_row, 0.0), axis=1, keepdims=True)
        dlf = jnp.where(valid, dlf, 0.0)
        dgate = jnp.where(lane == hh, jnp.where(valid, dli, 0.0), dgate)
        dgate = jnp.where(lane == M_HEADS + hh, dlf / (1.0 + jnp.exp(gq["f_col"])), dgate)
        dqk_ref[sl, hh * dk:(hh + 1) * dk] = _bf(dq)
        dqk_ref[sl, (M_HEADS + hh) * dk:(M_HEADS + hh + 1) * dk] = _bf(dks * scale)
        dv_ref[sl, hh * dv:(hh + 1) * dv] = dvv.astype(dv_ref.dtype)
        dc_sc[hh] = decay * d_c + _dot(qb, _bf(wdnum), TN)
        dn_sc[hh] = decay * d_n + _colsum(q * wdden)
        return dgate

    sd = jax.ShapeDtypeStruct
    nh = M_HEADS
    rc = lambda c: nst - 1 - c
    return _pcall(
        body, name="mlstm_bwd", grid=(bsz, nst),
        in_specs=[pl.BlockSpec(memory_space=pl.ANY),
                  pl.BlockSpec((None, rows, nh * dk), lambda b, c: (b, rc(c), 0)),
                  pl.BlockSpec((None, rows, nh * dk), lambda b, c: (b, rc(c), 1)),
                  pl.BlockSpec((None, rows, nh * dv), lambda b, c: (b, rc(c), V_OFF // (nh * dv))),
                  pl.BlockSpec((None, rows, 128), lambda b, c: (b, rc(c), 0)),
                  pl.BlockSpec((None, rows, nh * dv), lambda b, c: (b, rc(c), 0)),
                  pl.BlockSpec((None, nh, cps, dk, dv), lambda b, c: (b, 0, rc(c), 0, 0)),
                  pl.BlockSpec((None, nh, cps, 1, dk), lambda b, c: (b, 0, rc(c), 0, 0)),
                  pl.BlockSpec((None, nh, cps, 1, 128), lambda b, c: (b, 0, rc(c), 0, 0))],
        out_specs=[pl.BlockSpec((None, rows, nh * dv), lambda b, c: (b, rc(c), V_OFF // (nh * dv))),
                   pl.BlockSpec((None, rows, 2 * nh * dk), lambda b, c: (b, rc(c), 0)),
                   pl.BlockSpec((None, rows, 128), lambda b, c: (b, rc(c), 0))],
        out_shape=[sd(dp3.shape, dp3.dtype), sd((bsz, lp, 2 * nh * dk), BF16), sd((bsz, lp, 128), dp3.dtype)],
        scratch_shapes=[pltpu.VMEM((nh, dk, dv), F32), pltpu.VMEM((nh, 1, dk), F32)],
        input_output_aliases={0: 0},
        compiler_params=_cp(("arbitrary", "arbitrary")),
    )(dp3, qk3, qk3, p3, pg3, dh3, cs, ns, ms)


def _headnorm(x):
    dv = x.shape[1] // M_HEADS
    xh, rs = [], []
    for h in range(M_HEADS):
        xx = x[:, h * dv:(h + 1) * dv]
        mu = jnp.mean(xx, axis=-1, keepdims=True)
        xc = xx - mu
        rstd = lax.rsqrt(jnp.mean(xc * xc, axis=-1, keepdims=True) + LN_EPS)
        xh.append(xc * rstd)
        rs.append(rstd)
    return jnp.concatenate(xh, axis=1), rs


def _mix_fwd(hm, p, ys5g, h0, gn, wmo_bf, wo_bf, lp):
    r, d = hm.shape
    tm = _row_tile(lp, 384)

    def body(hm_ref, o_ref, gs_ref, gm_ref, ys_ref, h0_ref, gn_ref, wmo_ref, wo_ref,
             ymin_ref, mix_ref, r1_ref):
        xhat, _ = _headnorm(hm_ref[...].astype(F32))
        ymin = _bf(_sig(o_ref[...].astype(F32)) * (xhat * gn_ref[...]))
        ymin_ref[...] = ymin
        ym = _dot(ymin, wmo_ref[...])
        mix = _bf(_sig(gs_ref[...].astype(F32)) * ys_ref[...].astype(F32) + _sig(gm_ref[...].astype(F32)) * ym)
        mix_ref[...] = mix
        r1_ref[...] = ALPHA * h0_ref[...] + _dot(mix, wo_ref[...])

    sd = jax.ShapeDtypeStruct
    row = pl.BlockSpec((tm, d), lambda i: (i, 0))
    return _pcall(
        body, name="mix_fwd", grid=(r // tm,),
        in_specs=[row, pl.BlockSpec((tm, d), lambda i: (i, O_OFF // d)), pl.BlockSpec((tm, d), lambda i: (i, GS_OFF // d)),
                  pl.BlockSpec((tm, d), lambda i: (i, GM_OFF // d)), row, row, _const((1, d)),
                  _resident((d, d)), _resident((d, d))],
        out_specs=[row] * 3,
        out_shape=[sd((r, d), BF16), sd((r, d), BF16), sd((r, d), F32)],
        compiler_params=_cp(("parallel",), 48),
    )(hm, p, p, p, ys5g, h0, gn, wmo_bf, wo_bf)


def _mix_bwd(dr1, wo_bf, wmo_bf, p, ys5g, ymin, hm, gn, lp):
    r, d = hm.shape
    tm = _row_tile(lp, 384)
    dv = d // M_HEADS

    def body(dr1_ref, wo_ref, wmo_ref, o_ref, gs_ref, gm_ref, ys_ref, ym_ref, hm_ref, gn_ref,
             dp_ref, dys_ref, dym_ref, dhm_ref, dgn_ref):
        i = pl.program_id(0)

        @pl.when(i == 0)
        def _():
            dgn_ref[...] = jnp.zeros_like(dgn_ref)

        dmix = _dot(_bf(dr1_ref[...]), wo_ref[...], NT)
        sgs, sgm, so = (_sig(gs_ref[...].astype(F32)), _sig(gm_ref[...].astype(F32)), _sig(o_ref[...].astype(F32)))
        dys_ref[...] = _bf(dmix * sgs)
        dp_ref[:, d:2 * d] = _bf(dmix * ys_ref[...].astype(F32) * sgs * (1.0 - sgs))
        dym = dmix * sgm
        dym_ref[...] = _bf(dym)
        ym = _dot(ym_ref[...], wmo_ref[...])
        dp_ref[:, 2 * d:3 * d] = _bf(dmix * ym * sgm * (1.0 - sgm))
        dymin = _dot(_bf(dym), wmo_ref[...], NT)
        xhat, rs = _headnorm(hm_ref[...].astype(F32))
        gn_ = gn_ref[...]
        dp_ref[:, 0:d] = _bf(dymin * (xhat * gn_) * so * (1.0 - so))
        dhn = dymin * so
        dgn_ref[...] += _colsum(dhn * xhat)
        dxh = dhn * gn_
        for h in range(M_HEADS):
            sl = slice(h * dv, (h + 1) * dv)
            a, xh = dxh[:, sl], xhat[:, sl]
            m1 = jnp.mean(a, axis=-1, keepdims=True)
            m2 = jnp.mean(a * xh, axis=-1, keepdims=True)
            dhm_ref[:, sl] = _bf(rs[h] * (a - m1 - xh * m2))

    sd = jax.ShapeDtypeStruct
    row = pl.BlockSpec((tm, d), lambda i: (i, 0))
    vec = _const((1, d))
    return _pcall(
        body, name="mix_bwd", grid=(r // tm,),
        in_specs=[row, _resident((d, d)), _resident((d, d)),
                  pl.BlockSpec((tm, d), lambda i: (i, O_OFF // d)), pl.BlockSpec((tm, d), lambda i: (i, GS_OFF // d)),
                  pl.BlockSpec((tm, d), lambda i: (i, GM_OFF // d)), row, row, row, vec],
        out_specs=[pl.BlockSpec((tm, 3 * d), lambda i: (i, 0)), row, row, row, vec],
        out_shape=[sd((r, NP), BF16), sd((r, d), BF16), sd((r, d), BF16), sd((r, d), BF16), sd((1, d), F32)],
        compiler_params=_cp(("arbitrary",), 56),
    )(dr1, wo_bf, wmo_bf, p, p, p, ys5g, ymin, hm, gn)


def _mlp_fwd(r1, tgt, g1, b1, wup_g, wdn_bf, bup, g2, b2, lp):
    r, d = r1.shape
    tm = _row_tile(lp, 384)
    tps = lp // tm
    nf = wup_g.shape[0]

    def body(r1_ref, t_ref, g1_ref, b1_ref, wup_ref, wdn_ref, bup_ref, g2_ref, b2_ref,
             dr2_ref, h1b_ref, act_ref, loss_ref, dg2_ref, db2_ref):
        i = pl.program_id(0)

        @pl.when(i == 0)
        def _():
            loss_ref[...] = jnp.zeros_like(loss_ref)
            dg2_ref[...] = jnp.zeros_like(dg2_ref)
            db2_ref[...] = jnp.zeros_like(db2_ref)

        h1, _, _ = _ln_fwd(r1_ref[...], g1_ref[...], b1_ref[...])
        h1b = _bf(h1)
        h1b_ref[...] = h1b
        ff = jnp.zeros((tm, d), F32)
        for s in range(nf):
            up = _dot(h1b, wup_ref[s]) + bup_ref[:, s * d:(s + 1) * d]
            a = jnp.maximum(up, 0.0)
            a = _bf(a * a)
            act_ref[:, s * d:(s + 1) * d] = a
            ff = ff + _dot(a, wdn_ref[s * d:(s + 1) * d, :])
        r2 = ALPHA * h1 + ff
        g2 = g2_ref[...]
        y, xhat, rstd = _ln_fwd(r2, g2, b2_ref[...])
        t = (i % tps) * tm + lax.broadcasted_iota(jnp.int32, (tm, 1), 0)
        diff = jnp.where(t >= PAD + N_META, y - t_ref[...], 0.0)
        loss_ref[...] += 0.5 / d * jnp.sum(jnp.sum(diff * diff, axis=1, keepdims=True), axis=0, keepdims=True)
        dy = diff * (1.0 / d)
        dg2_ref[...] += _colsum(dy * xhat)
        db2_ref[...] += _colsum(dy)
        dr2_ref[...] = _ln_bwd(dy, xhat, rstd, g2)

    sd = jax.ShapeDtypeStruct
    row = pl.BlockSpec((tm, d), lambda i: (i, 0))
    vec = _const((1, d))
    return _pcall(
        body, name="mlp_fwd", grid=(r // tm,),
        in_specs=[row, row, vec, vec, _resident(wup_g.shape), _resident(wdn_bf.shape), _const((1, nf * d)), vec, vec],
        out_specs=[row, row, pl.BlockSpec((tm, nf * d), lambda i: (i, 0)), _const((1, 128)), vec, vec],
        out_shape=[sd((r, d), F32), sd((r, d), BF16), sd((r, nf * d), BF16), sd((1, 128), F32), sd((1, d), F32),
                   sd((1, d), F32)],
        compiler_params=_cp(("arbitrary",), 56),
    )(r1, tgt, g1, b1, wup_g, wdn_bf, bup, g2, b2)


def _mlp_bwd(act, dr2, r1, g1, wup_g, wdn_bf, lp):
    r, d = dr2.shape
    tm = _row_tile(lp, 384)
    nf = wup_g.shape[0]

    def body(act_ref, dr2_ref, r1_ref, g1_ref, wup_ref, wdn_ref, dr1_ref, dup_ref, dbup_ref, dg1_ref, db1_ref):
        i = pl.program_id(0)

        @pl.when(i == 0)
        def _():
            dbup_ref[...] = jnp.zeros_like(dbup_ref)
            dg1_ref[...] = jnp.zeros_like(dg1_ref)
            db1_ref[...] = jnp.zeros_like(db1_ref)

        dr2 = dr2_ref[...]
        dr2b = _bf(dr2)
        acc = ALPHA * dr2
        for s in range(nf):
            dact = _dot(dr2b, wdn_ref[s * d:(s + 1) * d, :], NT)
            dup = dact * (2.0 * jnp.sqrt(act_ref[:, s * d:(s + 1) * d].astype(F32)))
            dbup_ref[:, s * d:(s + 1) * d] += _colsum(dup)
            dupb = _bf(dup)
            dup_ref[:, s * d:(s + 1) * d] = dupb
            acc = acc + _dot(dupb, wup_ref[s], NT)
        g1 = g1_ref[...]
        _, xhat1, rstd1 = _ln_fwd(r1_ref[...], g1, 0.0)
        dr1_ref[...] = _ln_bwd(acc, xhat1, rstd1, g1)
        dg1_ref[...] += _colsum(acc * xhat1)
        db1_ref[...] += _colsum(acc)

    sd = jax.ShapeDtypeStruct
    row = pl.BlockSpec((tm, d), lambda i: (i, 0))
    vec = _const((1, d))
    return _pcall(
        body, name="mlp_bwd", grid=(r // tm,),
        in_specs=[pl.BlockSpec((tm, nf * d), lambda i: (i, 0)), row, row, vec, _resident(wup_g.shape),
                  _resident(wdn_bf.shape)],
        out_specs=[row, pl.BlockSpec((tm, nf * d), lambda i: (i, 0)), _const((1, nf * d)), vec, vec],
        out_shape=[sd((r, d), F32), sd((r, nf * d), BF16), sd((1, nf * d), F32), sd((1, d), F32), sd((1, d), F32)],
        compiler_params=_cp(("arbitrary",), 56),
    )(act, dr2, r1, g1, wup_g, wdn_bf)


def _s5_block_mats(bb_re_t, bb_im_t, c_re, c_im, ap_re, ap_im):
    ng = c_re.shape[0]
    gl = ng // S5_KCH
    eye = jnp.eye(gl, dtype=F32)

    def bmat(bt):
        bb = jnp.transpose(bt, (1, 0, 2)).reshape(S5_KCH, gl, S5_GROUP, S5_STATE)
        return jnp.einsum("kghp,gj->kghjp", bb, eye).reshape(S5_KCH, gl * S5_GROUP, gl * S5_STATE)

    def cmat(c):
        cc = c.reshape(S5_KCH, gl, S5_GROUP, S5_STATE)
        return jnp.einsum("kghp,gj->kjpgh", cc, eye).reshape(S5_KCH, gl * S5_STATE, gl * S5_GROUP)

    def pw(a):
        return jnp.transpose(a.reshape(8, S5_KCH, gl * S5_STATE), (1, 0, 2))

    bk = jnp.concatenate([bmat(bb_re_t), bmat(bb_im_t)], axis=-1)
    apow = jnp.concatenate([pw(ap_re), pw(ap_im)], axis=-1)
    return _bf(bk), _bf(cmat(c_re)), _bf(cmat(c_im)), apow


def _s5_block_grads(dbk, dcre, dcim, da):
    gl = dbk.shape[1] // S5_GROUP
    ng = gl * S5_KCH
    eye = jnp.eye(gl, dtype=F32)
    hw = gl * S5_STATE

    def bpart(x):
        x = x.reshape(S5_KCH, gl, S5_GROUP, gl, S5_STATE)
        x = jnp.einsum("kghjp,gj->kghp", x, eye).reshape(ng, S5_GROUP, S5_STATE)
        return jnp.transpose(x, (1, 0, 2))

    def cpart(x):
        x = x.reshape(S5_KCH, gl, S5_STATE, gl, S5_GROUP)
        return jnp.einsum("kjpgh,gj->kghp", x, eye).reshape(ng, S5_GROUP, S5_STATE)

    return (bpart(dbk[..., :hw]), bpart(dbk[..., hw:]), cpart(dcre), cpart(dcim),
            da[:, 0, :hw].reshape(ng, S5_STATE), da[:, 0, hw:].reshape(ng, S5_STATE))


def _tie(a, tok):
    return a if tok is None else a + tok[0, 0]


def _local_step(x, tgt, w, early=None, late=None, ready=None):
    ready = ready or (lambda names, g: None)
    bsz, seq, d = x.shape
    lp = PAD + N_META + seq
    r = bsz * lp
    tgtp = jnp.concatenate([jnp.zeros((bsz, PAD + N_META, d), F32), tgt], axis=1).reshape(r, d)

    h0, h0b = _ln0_fwd(x, w["meta_tokens"], w["ln0_g"], w["ln0_b"])
    b_re_t = jnp.transpose(w["s5_b_re"], (2, 0, 1))
    b_im_t = jnp.transpose(w["s5_b_im"], (2, 0, 1))
    ap_re, ap_im, bb_re_t, bb_im_t = _s5_prep(w["s5_lambda_re"], w["s5_lambda_im"], w["s5_log_dt"], b_re_t, b_im_t)
    bk, cre, cim, apow = _s5_block_mats(bb_re_t, bb_im_t, w["s5_c_re"], w["s5_c_im"], ap_re, ap_im)
    apow_rev = jnp.flip(apow, axis=1)
    if early is not None:
        w = {**w, **early((h0, tgtp, bk, cre, cim, apow_rev))}
    p, pg = _inproj(h0b, w["w_in"], w["b_in"], lp)
    p3 = p.reshape(bsz, lp, NP)
    pg3 = pg.reshape(bsz, lp, 128)

    y_s5, xs = _s5_fwd(p3, bk, cre, cim, apow, w["s5_d"])
    sw = y_s5.shape[-1]
    if late is not None:
        w = {**w, **late(y_s5)}
    gy, z, ys5g = _glu_fwd(y_s5.reshape(r, sw), w["s5_w_glu"], lp)

    pre3, qk3 = _conv_fwd(p3, w["qk_conv_w"], w["qk_conv_b"])
    hm3, cs, ns, ms = _mlstm_fwd(qk3, p3, pg3)
    hm = hm3.reshape(r, d)
    ymin, mix, r1 = _mix_fwd(hm, p, ys5g, h0, w["m_norm_g"], w["m_w_out"], w["w_o"], lp)
    dr2, h1b, act, loss, dg2, db2 = _mlp_fwd(r1, tgtp, w["ln1_g"], w["ln1_b"], w["w_up"], w["w_down"], w["b_up"],
                                             w["ln2_g"], w["ln2_b"], lp)

    g = {"ln2_g": dg2, "ln2_b": db2}
    dr1, dup, g["b_up"], g["ln1_g"], g["ln1_b"] = _mlp_bwd(act, dr2, r1, w["ln1_g"], w["w_up"], w["w_down"], lp)
    g["w_down"] = _mm_tn(act, dr2, name="dw_down")
    g["w_up"] = _mm_tn(h1b, dup, name="dw_up", split=w["w_up"].shape[0])
    tok = ready(("w_down", "w_up"), g)
    dp, dys5g, dym, dhm, g["m_norm_g"] = _mix_bwd(
        dr1, w["w_o"], w["m_w_out"], p, ys5g, ymin, hm, _tie(w["m_norm_g"], tok), lp)
    g["w_o"] = _mm_tn(mix, dr1, name="dw_o")
    g["m_w_out"] = _mm_tn(ymin, dym, name="dw_mout")

    dp3 = dp.reshape(bsz, lp, NP)
    dp3, dqk3, dgate = _mlstm_bwd(dp3, qk3, p3, pg3, dhm.reshape(bsz, lp, d), cs, ns, ms)
    dp3, g["qk_conv_w"], g["qk_conv_b"] = _conv_bwd(dp3, p3, dqk3, pre3, w["qk_conv_w"])
    dz, dys5 = _glu_bwd(dys5g, z, y_s5.reshape(r, sw), w["s5_w_glu"], lp)
    g["s5_w_glu"] = _mm_tn(gy, dz, name="dw_glu", split=w["s5_w_glu"].shape[0])
    tok = ready(("s5_w_glu", "m_w_out", "w_o"), g)
    dp3, dbk, dcre, dcim, da, g["s5_d"] = _s5_bwd(dp3, p3, dys5.reshape(bsz, lp, sw), xs, bk, cre, cim, apow_rev,
                                                 _tie(w["s5_d"], tok))
    dbb_re_t, dbb_im_t, g["s5_c_re"], g["s5_c_im"], da_re, da_im = _s5_block_grads(dbk, dcre, dcim, da)
    g["s5_lambda_re"], g["s5_lambda_im"], g["s5_log_dt"], gb_re_t, gb_im_t = _s5_prep_bwd(
        w["s5_lambda_re"], w["s5_lambda_im"], w["s5_log_dt"], b_re_t, b_im_t, da_re, da_im, dbb_re_t, dbb_im_t)
    g["s5_b_re"] = jnp.transpose(gb_re_t, (1, 2, 0))
    g["s5_b_im"] = jnp.transpose(gb_im_t, (1, 2, 0))

    dp3 = lax.dynamic_update_slice(dp3, dgate, (0, 0, G_OFF))
    dp = dp3.reshape(r, NP)
    g["w_in"], g["b_in"] = _mm_tn(h0b, dp, name="dw_in", colsum=True)
    tok = ready(("w_in",), g)
    dpw = _mm_nt(dp, w["w_in"], lp, name="dh0", dep=tok)
    grad_x, g["ln0_g"], g["ln0_b"], g["meta_tokens"] = _ln0_bwd(x, w["meta_tokens"], dr1, dpw, w["ln0_g"])
    return loss, grad_x, g


_ANY = pl.BlockSpec(memory_space=pl.ANY)
_MESH = pl.DeviceIdType.MESH


def _place():
    return lax.axis_index("x"), lax.axis_index("y"), lax.axis_index("c")


def _gather_chips(shards):
    n = len(shards)

    def body(*refs):
        ins, outs = refs[:n], refs[n:2 * n]
        send, recv, loc = refs[2 * n:]
        x, y, c = _place()
        me = 2 * x + y
        peers = [(1 - x, y), (x, 1 - y), (1 - x, 1 - y)]

        def rc(a, k, slot):
            px, py = peers[k]
            return pltpu.make_async_remote_copy(src_ref=ins[a], dst_ref=outs[a].at[slot], send_sem=send.at[a, k],
                                                recv_sem=recv.at[a, k], device_id=(px, py, c), device_id_type=_MESH)

        own = [pltpu.make_async_copy(ins[a], outs[a].at[me], loc.at[a]) for a in range(n)]
        for cp in own:
            cp.start()
        out = [rc(a, k, me) for a in range(n) for k in range(3)]
        for cp in out:
            cp.start()
        for a in range(n):
            for k in range(3):
                rc(a, k, 2 * peers[k][0] + peers[k][1]).wait_recv()
        for cp in out:
            cp.wait_send()
        for cp in own:
            cp.wait()

    return _pcall(
        body, name="gather_chips", in_specs=[_ANY] * n, out_specs=[_ANY] * n,
        out_shape=[jax.ShapeDtypeStruct((4,) + s.shape, s.dtype) for s in shards],
        scratch_shapes=[pltpu.SemaphoreType.DMA((n, 3)), pltpu.SemaphoreType.DMA((n, 3)), pltpu.SemaphoreType.DMA((n,))],
    )(*shards)


_HBM = pl.BlockSpec(memory_space=pltpu.HBM)
_SEM = pl.BlockSpec(memory_space=pltpu.SEMAPHORE)
_EFFECT = pltpu.SideEffectType.DATAFLOW_SIDE_EFFECTING


def _xchg_copies(srcs, lands, send, recv, scatter):
    x, y, c = _place()
    me = 2 * x + y
    peers = [(1 - x, y), (x, 1 - y), (1 - x, 1 - y)]
    out = []
    for a in range(len(srcs)):
        for k, (px, py) in enumerate(peers):
            src = srcs[a].at[2 * px + py] if scatter else srcs[a]
            dst = lands[a].at[k] if scatter else lands[a].at[me]
            out.append(pltpu.make_async_remote_copy(src_ref=src, dst_ref=dst, send_sem=send.at[3 * a + k],
                                                    recv_sem=recv.at[3 * a + k], device_id=(px, py, c),
                                                    device_id_type=_MESH))
    return out


def _xchg_start(srcs, lands, *, name, scatter, dep=None):
    n = len(srcs)
    deps = [] if dep is None else [dep]
    nd = len(deps)

    def body(*refs):
        send, recv = refs[2 * n + nd], refs[2 * n + nd + 1]
        for cp in _xchg_copies(refs[:n], refs[n:2 * n], send, recv, scatter):
            cp.start()
        refs[-1][...] = jnp.zeros_like(refs[-1])

    hbm = lambda a: pltpu.HBM(a.shape, a.dtype)
    con = lambda a: pltpu.with_memory_space_constraint(a, pltpu.HBM)
    res = _pcall(
        body, name=name, in_specs=[_HBM] * (2 * n) + [_ANY] * nd,
        out_specs=[_SEM, _SEM] + [_HBM] * (2 * n) + [pl.BlockSpec(memory_space=pltpu.VMEM)],
        out_shape=[pltpu.SemaphoreType.DMA((3 * n,)), pltpu.SemaphoreType.DMA((3 * n,))]
        + [hbm(a) for a in srcs] + [hbm(a) for a in lands] + [jax.ShapeDtypeStruct((8, 128), F32)],
        input_output_aliases={i: 2 + i for i in range(2 * n)},
        compiler_params=pltpu.CompilerParams(has_side_effects=_EFFECT),
    )(*[con(a) for a in srcs], *[con(a) for a in lands], *deps)
    return res[0], res[1], list(res[2:2 + n]), list(res[2 + n:2 + 2 * n]), res[-1]


def _xchg_wait(send, recv, srcs, lands, after, *, name, scatter):
    n = len(srcs)
    afters = list(after) if isinstance(after, (list, tuple)) else [after]

    def body(*refs):
        s_ref, r_ref = refs[2 * n], refs[2 * n + 1]
        for cp in _xchg_copies(refs[:n], refs[n:2 * n], s_ref, r_ref, scatter):
            cp.wait_send()
            cp.wait_recv()

    hbm = lambda a: pltpu.HBM(a.shape, a.dtype)
    res = _pcall(
        body, name=name, in_specs=[_HBM] * (2 * n) + [_SEM, _SEM] + [_ANY] * len(afters),
        out_specs=[_HBM] * (2 * n),
        out_shape=[hbm(a) for a in srcs] + [hbm(a) for a in lands],
        input_output_aliases={i: i for i in range(2 * n)},
        compiler_params=pltpu.CompilerParams(has_side_effects=_EFFECT),
    )(*srcs, *lands, send, recv, *afters)
    return list(res[:n]), list(res[n:])


def _swap_cores(arrs, name="swap_cores"):
    n = len(arrs)

    def body(*refs):
        ins, outs = refs[:n], refs[n:2 * n]
        send, recv = refs[2 * n:]
        x, y, c = _place()
        cps = [pltpu.make_async_remote_copy(src_ref=ins[a], dst_ref=outs[a], send_sem=send.at[a], recv_sem=recv.at[a],
                                            device_id=(x, y, 1 - c), device_id_type=_MESH) for a in range(n)]
        for cp in cps:
            cp.start()
        for cp in cps:
            cp.wait_recv()
        for cp in cps:
            cp.wait_send()

    return _pcall(
        body, name=name, in_specs=[_ANY] * n, out_specs=[_ANY] * n,
        out_shape=[jax.ShapeDtypeStruct(s.shape, s.dtype) for s in arrs],
        scratch_shapes=[pltpu.SemaphoreType.DMA((n,)), pltpu.SemaphoreType.DMA((n,))],
    )(*arrs)


def _allreduce_small(v, dep=None):
    rows = v.shape[0]
    half = rows // 2
    assert half % 8 == 0 and 2 * half == rows
    deps = [] if dep is None else [dep]

    def body(v_ref, *rest):
        out_ref, sib_ref, pair_ref, slots_ref, send, recv = rest[len(deps):]
        x, y, c = _place()
        chip = 2 * x + y
        sibling = (x, y, 1 - c)
        peers = [(1 - x, y), (x, 1 - y), (1 - x, 1 - y)]
        mine = pl.ds(pl.multiple_of(c * half, 8), half)

        first = pltpu.make_async_remote_copy(src_ref=v_ref, dst_ref=sib_ref, send_sem=send.at[0], recv_sem=recv.at[0],
                                             device_id=sibling, device_id_type=_MESH)
        first.start()
        first.wait_recv()
        pair_ref[...] = v_ref[...] + sib_ref[...]
        slots_ref[chip] = pair_ref[mine, :]
        cross = [pltpu.make_async_remote_copy(src_ref=pair_ref.at[mine], dst_ref=slots_ref.at[chip],
                                              send_sem=send.at[1 + k], recv_sem=recv.at[1 + k],
                                              device_id=(px, py, c), device_id_type=_MESH)
                 for k, (px, py) in enumerate(peers)]
        for cp in cross:
            cp.start()
        for cp in cross:
            cp.wait_recv()
        out_ref[mine, :] = ((slots_ref[0] + slots_ref[1]) + slots_ref[2]) + slots_ref[3]
        last = pltpu.make_async_remote_copy(src_ref=out_ref.at[mine], dst_ref=out_ref.at[mine], send_sem=send.at[4],
                                            recv_sem=recv.at[4], device_id=sibling, device_id_type=_MESH)
        last.start()
        last.wait_recv()
        first.wait_send()
        for cp in cross:
            cp.wait_send()
        last.wait_send()

    vm = pl.BlockSpec(memory_space=pltpu.VMEM)
    return _pcall(
        body, name="allreduce_small", in_specs=[vm] + [_ANY] * len(deps), out_specs=vm,
        out_shape=jax.ShapeDtypeStruct((rows, 128), F32),
        scratch_shapes=[pltpu.VMEM((rows, 128), F32), pltpu.VMEM((rows, 128), F32), pltpu.VMEM((4, half, 128), F32),
                        pltpu.SemaphoreType.DMA((5,)), pltpu.SemaphoreType.DMA((5,))],
        compiler_params=_cp(None, 40),
    )(v, *deps)


def _sum_slots(parts, land, chip):
    ns, rows, cols = land.shape
    tm = _row_tile(rows, 256, 8)

    def body(chip_ref, own_ref, a_ref, o_ref):
        o_ref[...] = ((own_ref[...] + a_ref[0]) + a_ref[1]) + a_ref[2]

    return _pcall(
        body, name="sum_slots",
        grid_spec=pltpu.PrefetchScalarGridSpec(
            num_scalar_prefetch=1, grid=(rows // tm,),
            in_specs=[pl.BlockSpec((None, tm, cols), lambda i, c: (c[0], i, 0)),
                      pl.BlockSpec((ns, tm, cols), lambda i, c: (0, i, 0))],
            out_specs=pl.BlockSpec((tm, cols), lambda i, c: (i, 0))),
        out_shape=jax.ShapeDtypeStruct((rows, cols), F32),
        compiler_params=_cp(("parallel",), 40),
    )(jnp.reshape(chip, (1,)).astype(jnp.int32), parts, land)


def _adamw(w, m, v, g0, g1=None):
    rows, cols = w.shape[-2:]
    lead = w.ndim == 3
    tm = _row_tile(rows, max(8, (1 << 20) // (4 * cols)), 8)
    c1 = 1.0 - ADAM_B1 ** ADAM_STEP
    c2 = 1.0 - ADAM_B2 ** ADAM_STEP
    two = g1 is not None

    def body(*refs):
        w_ref, m_ref, v_ref, g0_ref = refs[:4]
        g_ref, d_ref, nm_ref, nv_ref = refs[-4:]
        g = g0_ref[...]
        if two:
            g = g + refs[4][...]
        nm = ADAM_B1 * m_ref[...] + (1.0 - ADAM_B1) * g
        nv = ADAM_B2 * v_ref[...] + (1.0 - ADAM_B2) * (g * g)
        g_ref[...] = g
        nm_ref[...] = nm
        nv_ref[...] = nv
        d_ref[...] = -ADAM_LR * ((nm / c1) / (jnp.sqrt(nv / c2) + ADAM_EPS) + ADAM_WD * w_ref[...])

    blk = pl.BlockSpec((tm, cols), lambda i: (i, 0))
    wblk = pl.BlockSpec((None, tm, cols), lambda i: (0, i, 0)) if lead else blk
    ins = [w, m, v, g0] + ([g1] if two else [])
    return _pcall(
        body, name="adamw", grid=(rows // tm,), in_specs=[wblk] * 3 + [blk] * (len(ins) - 3), out_specs=[wblk] * 4,
        out_shape=[jax.ShapeDtypeStruct(w.shape, F32)] * 4,
        compiler_params=_cp(("parallel",), 40),
    )(*ins)


_BIG = ("w_in", "s5_w_glu", "m_w_out", "w_o", "w_up", "w_down")
_SMALL = ("ln0_g", "ln0_b", "b_in", "qk_conv_b", "s5_lambda_re", "s5_lambda_im", "s5_log_dt", "s5_b_re", "s5_b_im",
          "s5_c_re", "s5_c_im", "s5_d", "m_norm_g", "ln1_g", "ln1_b", "b_up", "ln2_g", "ln2_b")
_SMALL_SHARDED = ("meta_tokens", "qk_conv_w")
_ORDER = ("meta_tokens", "ln0_g", "ln0_b", "w_in", "b_in", "qk_conv_w", "qk_conv_b", "s5_lambda_re", "s5_lambda_im",
          "s5_log_dt", "s5_b_re", "s5_b_im", "s5_c_re", "s5_c_im", "s5_d", "s5_w_glu", "m_norm_g", "m_w_out", "w_o",
          "ln1_g", "ln1_b", "w_up", "b_up", "w_down", "ln2_g", "ln2_b")


def _pack(arrs):
    flat = jnp.concatenate([a.reshape(-1) for a in arrs])
    n = flat.shape[0]
    rows = -(-n // 2048) * 16
    return jnp.pad(flat, (0, rows * 128 - n)).reshape(rows, 128)


def _unpack(packed, shapes):
    flat = packed.reshape(-1)
    out, off = [], 0
    for s in shapes:
        n = math.prod(s)
        out.append(flat[off:off + n].reshape(s))
        off += n
    return out


def kernel(x, meta_tokens, ln0_g, ln0_b, w_in, b_in, qk_conv_w, qk_conv_b, s5_lambda_re, s5_lambda_im, s5_log_dt, s5_b_re, s5_b_im, s5_c_re, s5_c_im, s5_d, s5_w_glu, m_norm_g, m_w_out, w_o, ln1_g, ln1_b, w_up, b_up, w_down, ln2_g, ln2_b, loss_target, m_meta_tokens, m_ln0_g, m_ln0_b, m_w_in, m_b_in, m_qk_conv_w, m_qk_conv_b, m_s5_lambda_re, m_s5_lambda_im, m_s5_log_dt, m_s5_b_re, m_s5_b_im, m_s5_c_re, m_s5_c_im, m_s5_d, m_s5_w_glu, m_m_norm_g, m_m_w_out, m_w_o, m_ln1_g, m_ln1_b, m_w_up, m_b_up, m_w_down, m_ln2_g, m_ln2_b, v_meta_tokens, v_ln0_g, v_ln0_b, v_w_in, v_b_in, v_qk_conv_w, v_qk_conv_b, v_s5_lambda_re, v_s5_lambda_im, v_s5_log_dt, v_s5_b_re, v_s5_b_im, v_s5_c_re, v_s5_c_im, v_s5_d, v_s5_w_glu, v_m_norm_g, v_m_w_out, v_w_o, v_ln1_g, v_ln1_b, v_w_up, v_b_up, v_w_down, v_ln2_g, v_ln2_b):
    wts = dict(meta_tokens=meta_tokens, ln0_g=ln0_g, ln0_b=ln0_b, w_in=w_in, b_in=b_in, qk_conv_w=qk_conv_w,
               qk_conv_b=qk_conv_b, s5_lambda_re=s5_lambda_re, s5_lambda_im=s5_lambda_im, s5_log_dt=s5_log_dt,
               s5_b_re=s5_b_re, s5_b_im=s5_b_im, s5_c_re=s5_c_re, s5_c_im=s5_c_im, s5_d=s5_d, s5_w_glu=s5_w_glu,
               m_norm_g=m_norm_g, m_w_out=m_w_out, w_o=w_o, ln1_g=ln1_g, ln1_b=ln1_b, w_up=w_up, b_up=b_up,
               w_down=w_down, ln2_g=ln2_g, ln2_b=ln2_b)
    mom = dict(meta_tokens=m_meta_tokens, ln0_g=m_ln0_g, ln0_b=m_ln0_b, w_in=m_w_in, b_in=m_b_in, qk_conv_w=m_qk_conv_w,
               qk_conv_b=m_qk_conv_b, s5_lambda_re=m_s5_lambda_re, s5_lambda_im=m_s5_lambda_im, s5_log_dt=m_s5_log_dt,
               s5_b_re=m_s5_b_re, s5_b_im=m_s5_b_im, s5_c_re=m_s5_c_re, s5_c_im=m_s5_c_im, s5_d=m_s5_d,
               s5_w_glu=m_s5_w_glu, m_norm_g=m_m_norm_g, m_w_out=m_m_w_out, w_o=m_w_o, ln1_g=m_ln1_g, ln1_b=m_ln1_b,
               w_up=m_w_up, b_up=m_b_up, w_down=m_w_down, ln2_g=m_ln2_g, ln2_b=m_ln2_b)
    var = dict(meta_tokens=v_meta_tokens, ln0_g=v_ln0_g, ln0_b=v_ln0_b, w_in=v_w_in, b_in=v_b_in, qk_conv_w=v_qk_conv_w,
               qk_conv_b=v_qk_conv_b, s5_lambda_re=v_s5_lambda_re, s5_lambda_im=v_s5_lambda_im, s5_log_dt=v_s5_log_dt,
               s5_b_re=v_s5_b_re, s5_b_im=v_s5_b_im, s5_c_re=v_s5_c_re, s5_c_im=v_s5_c_im, s5_d=v_s5_d,
               s5_w_glu=v_s5_w_glu, m_norm_g=v_m_norm_g, m_w_out=v_m_w_out, w_o=v_w_o, ln1_g=v_ln1_g, ln1_b=v_ln1_b,
               w_up=v_w_up, b_up=v_b_up, w_down=v_w_down, ln2_g=v_ln2_g, ln2_b=v_ln2_b)
    d = x.shape[-1]
    chip = 2 * lax.axis_index("x") + lax.axis_index("y")

    gw = dict(zip(_SMALL_SHARDED, _gather_chips([meta_tokens, qk_conv_w[0]])))
    core = lax.axis_index("c")
    half_rows = w_in.shape[1] // 2
    own_w_in = lax.dynamic_slice_in_dim(_bf(w_in[0]), core * half_rows, half_rows, axis=0)
    fsend, frecv, fsrc, fland, ftok = _xchg_start([own_w_in], [lax.empty((4,) + own_w_in.shape, BF16)],
                                                  name="gather_w_in_start", scatter=False, dep=gw["qk_conv_w"])
    late_names = tuple(n for n in _BIG if n != "w_in")
    cat = lambda a: jnp.transpose(a, (1, 0, 2)).reshape(a.shape[1], 4 * a.shape[2])
    w = dict(
        meta_tokens=cat(gw["meta_tokens"]), ln0_g=ln0_g[None], ln0_b=_tie(ln0_b[None], ftok),
        qk_conv_w=cat(gw["qk_conv_w"]), qk_conv_b=qk_conv_b,
        s5_lambda_re=s5_lambda_re[0], s5_lambda_im=s5_lambda_im[0], s5_log_dt=s5_log_dt[0][:, None],
        s5_b_re=s5_b_re[0], s5_b_im=s5_b_im[0], s5_c_re=s5_c_re[0], s5_c_im=s5_c_im[0], s5_d=s5_d,
        m_norm_g=m_norm_g, ln1_g=ln1_g, ln1_b=ln1_b, b_up=b_up, ln2_g=ln2_g, ln2_b=ln2_b)
    in_flight = {}

    def place_own(src, land):
        return lax.dynamic_update_slice(land, src[None], (chip,) + (0,) * src.ndim)

    small_names = _SMALL + _SMALL_SHARDED

    def view(n, a):
        return jnp.swapaxes(a, -1, -2) if n in ("s5_b_re", "s5_b_im") else a

    small_wmv = [_pack([view(n, dct[n]) for n in small_names]) for dct in (wts, mom, var)]

    def early(after):
        src, land = _xchg_wait(fsend, frecv, fsrc, fland, tuple(after) + tuple(small_wmv), name="gather_w_in_wait",
                               scatter=False)
        late_src = [_bf(wts[n][0]) for n in late_names]
        st = _xchg_start(late_src, [lax.empty((4,) + a.shape, a.dtype) for a in late_src], name="gather_late_start",
                         scatter=False, dep=src[0])
        in_flight["late"] = st[:4]
        mine = place_own(src[0], land[0])
        theirs = _swap_cores([mine], name="swap_w_in")[0]
        top, bottom = jnp.where(core == 0, mine, theirs), jnp.where(core == 0, theirs, mine)
        full = jnp.concatenate([top, bottom], axis=1)
        return dict(w_in=_w_in_from_slots(full, IN_CHUNK), b_in=_tie(_to_pad_cols(b_in), st[4]))

    def late(after):
        src, land = _xchg_wait(*in_flight["late"], after, name="gather_late_wait", scatter=False)
        full = {n: place_own(s, ld) for n, s, ld in zip(late_names, src, land)}
        return dict(s5_w_glu=full["s5_w_glu"], m_w_out=full["m_w_out"].reshape(d, d), w_o=full["w_o"].reshape(d, d),
                    w_up=full["w_up"], w_down=full["w_down"].reshape(4 * d, d))

    flying = []

    def ready(names, g):
        parts = dict(
            w_in=lambda: _slots_from_w_in(g["w_in"][0]), s5_w_glu=lambda: g["s5_w_glu"],
            m_w_out=lambda: g["m_w_out"].reshape(4, d // 4, d), w_o=lambda: g["w_o"].reshape(4, d // 4, d),
            w_up=lambda: g["w_up"], w_down=lambda: g["w_down"].reshape(4, d, d))
        src = [parts[n]() for n in names]
        land = [lax.empty((3,) + a.shape[1:], a.dtype) for a in src]
        st = _xchg_start(src, land, name="scatter_" + names[0] + "_start", scatter=True)
        flying.append((names,) + st[:4])
        return st[4]

    loss, grad_x, g = _local_step(x, loss_target, w, early, late, ready)
    g["b_in"] = _from_pad_cols(g["b_in"])

    res = {}

    def flat(a):
        return jnp.swapaxes(a, -1, -2).reshape(a.shape[:-2] + (-1, 128))

    def unflat(y, shape):
        return jnp.swapaxes(y.reshape(shape[:-2] + (shape[-1], shape[-2])), -1, -2)

    def finish(groups, after, tag):
        mine = {}
        for names, send, recv, src, land in groups:
            src, land = _xchg_wait(send, recv, src, land, after, name="scatter_" + names[0] + "_wait", scatter=True)
            for n, s, ld in zip(names, src, land):
                mine[n] = _sum_slots(s, ld, chip)
        theirs = _swap_cores(list(mine.values()), name="swap_cores_" + tag)
        for n, t in zip(mine, theirs):
            if n == "w_in":
                res[n] = [unflat(r, wts[n].shape) for r in _adamw(flat(wts[n]), flat(mom[n]), flat(var[n]),
                                                                  flat(mine[n]), flat(t))]
            else:
                res[n] = _adamw(wts[n], mom[n], var[n], mine[n], t)

    finish(flying[:-1], g["ln0_g"], "a")

    small_shapes = [(1, 128)] + [view(n, wts[n]).shape for n in _SMALL] + [g[n].shape for n in _SMALL_SHARDED]
    packed = _pack([loss] + [view(n, g[n]) for n in _SMALL] + [g[n] for n in _SMALL_SHARDED])
    tot = _unpack(_allreduce_small(packed, dep=res["w_o"][3]), small_shapes)
    loss_out = tot[0][0, 0]
    gsm = dict(zip(_SMALL + _SMALL_SHARDED, tot[1:]))
    for n in _SMALL_SHARDED:
        cols = wts[n].shape[-1]
        gsm[n] = lax.dynamic_slice_in_dim(gsm[n], chip * cols, cols, axis=1).reshape(wts[n].shape)

    names = small_names
    shapes = [view(n, wts[n]).shape for n in names]
    small_out = _adamw(*small_wmv, _pack([gsm[n] for n in names]))
    small_res = [_unpack(r, shapes) for r in small_out]
    for j, n in enumerate(names):
        res[n] = [view(n, small_res[q][j]) for q in range(4)]
    finish(flying[-1:], small_out[0], "b")

    return (loss_out, grad_x, *[res[n][0] for n in _ORDER], *[res[n][1] for n in _ORDER],
            *[res[n][2] for n in _ORDER], *[res[n][3] for n in _ORDER])
```

```python
import functools
import math

import jax
import jax.numpy as jnp
from jax import lax
from jax.experimental import pallas as pl
from jax.experimental.pallas import tpu as pltpu

F32 = jnp.float32
BF16 = jnp.bfloat16
HI = lax.Precision.HIGHEST

N_META = 16
M_HEADS = 4
M_CHUNK = 128
PAD = M_CHUNK - N_META
CONV_W = 4
HALO_ROWS = 16
S5_GROUP = 16
S5_STATE = 64
S5_KCH = 4
LN_EPS = 1e-5
ALPHA = 2.0 ** 0.25
NEG = -1e30
ADAM_LR, ADAM_B1, ADAM_B2, ADAM_EPS, ADAM_WD, ADAM_STEP = 0.001, 0.9, 0.999, 1e-08, 0.01, 10

O_OFF, GS_OFF, GM_OFF, V_OFF, Q_OFF, K_OFF, U_OFF, G_OFF, NP = 0, 1024, 2048, 3072, 4096, 4608, 5120, 5632, 5760

NN = ((1,), (0,))
NT = ((1,), (1,))
TN = ((0,), (0,))


def _dot(a, b, dims=NN, prec=None):
    return lax.dot_general(a, b, (dims, ((), ())), preferred_element_type=F32, precision=prec)


def _bf(x):
    return x.astype(BF16)


def _sig(x):
    return 0.5 * jnp.tanh(0.5 * x) + 0.5


def _pcall(body, **kw):
    return pl.pallas_call(body, **kw)


def _cp(sem=None, vmem_mb=None):
    kw = {}
    if sem is not None:
        kw["dimension_semantics"] = sem
    if vmem_mb is not None:
        kw["vmem_limit_bytes"] = vmem_mb << 20
    return pltpu.CompilerParams(**kw)


def _row_tile(n, want, mult=16):
    best = None
    for t in range(mult, want + 1, mult):
        if n % t == 0:
            best = t
    assert best is not None, (n, want)
    return best


def _resident(shape):
    nd = len(shape)
    return pl.BlockSpec(shape, lambda *_: (0,) * nd, pipeline_mode=pl.Buffered(1))


def _const(shape):
    nd = len(shape)
    return pl.BlockSpec(shape, lambda *_: (0,) * nd)


def _ln_fwd(x, g, b):
    mu = jnp.mean(x, axis=-1, keepdims=True)
    xc = x - mu
    var = jnp.mean(xc * xc, axis=-1, keepdims=True)
    rstd = lax.rsqrt(var + LN_EPS)
    xhat = xc * rstd
    return xhat * g + b, xhat, rstd


def _ln_bwd(dy, xhat, rstd, g):
    dxh = dy * g
    m1 = jnp.mean(dxh, axis=-1, keepdims=True)
    m2 = jnp.mean(dxh * xhat, axis=-1, keepdims=True)
    return rstd * (dxh - m1 - xhat * m2)


def _colsum(x):
    return jnp.sum(x, axis=0, keepdims=True)


def _to_pad_cols(w):
    u, q, k, v, o, gi, gf, gs, gm = (w[..., 0:512], w[..., 512:1024], w[..., 1024:1536], w[..., 1536:2560],
                                     w[..., 2560:3584], w[..., 3584:3588], w[..., 3588:3592], w[..., 3592:4616],
                                     w[..., 4616:5640])
    z = jnp.zeros(w.shape[:-1] + (NP - G_OFF - 8,), w.dtype)
    return jnp.concatenate([o, gs, gm, v, q, k, u, gi, gf, z], axis=-1)


def _from_pad_cols(w):
    o, gs, gm, v, q, k, u = (w[..., O_OFF:GS_OFF], w[..., GS_OFF:GM_OFF], w[..., GM_OFF:V_OFF], w[..., V_OFF:Q_OFF],
                             w[..., Q_OFF:K_OFF], w[..., K_OFF:U_OFF], w[..., U_OFF:G_OFF])
    gi, gf = w[..., G_OFF:G_OFF + 4], w[..., G_OFF + 4:G_OFF + 8]
    return jnp.concatenate([u, q, k, v, o, gi, gf, gs, gm], axis=-1)


_IN_REF = (("u", 512), ("q", 512), ("k", 512), ("v", 1024), ("o", 1024), ("i", 4), ("f", 4), ("gs", 1024), ("gm", 1024))
_IN_PAD = (("o", O_OFF), ("gs", GS_OFF), ("gm", GM_OFF), ("v", V_OFF), ("q", Q_OFF), ("k", K_OFF), ("u", U_OFF),
           ("i", G_OFF), ("f", G_OFF + 4))


def _in_ref_ranges():
    out, off = {}, 0
    for n, s in _IN_REF:
        out[n] = (off, off + s)
        off += s
    return out, off


def _w_in_from_slots(g, chunk=None):
    rng, total = _in_ref_ranges()
    width = total // g.shape[0]
    cols = []
    for n, _ in _IN_PAD:
        a, b = rng[n]
        while a < b:
            s = a // width
            e = min(b, (s + 1) * width)
            cols.append(g[s][:, a - s * width:e - s * width])
            a = e
    cols.append(jnp.zeros((g.shape[1], NP - G_OFF - 8), g.dtype))
    if chunk is None:
        return jnp.concatenate(cols, axis=1)
    chunks, cur, room = [], [], chunk
    for c in cols:
        while c.shape[1] > 0:
            take = min(room, c.shape[1])
            cur.append(c[:, :take])
            c, room = c[:, take:], room - take
            if room == 0:
                chunks.append(jnp.concatenate(cur, axis=1))
                cur, room = [], chunk
    assert not cur
    return jnp.stack(chunks, axis=0)


def _slots_from_w_in(wp, nslot=4):
    rng, total = _in_ref_ranges()
    width = total // nslot
    pad_off = dict(_IN_PAD)
    slots = []
    for s in range(nslot):
        lo, hi = s * width, (s + 1) * width
        cols = []
        for n, _ in _IN_REF:
            a, b = rng[n]
            x0, x1 = max(a, lo), min(b, hi)
            if x0 < x1:
                cols.append(wp[:, pad_off[n] + x0 - a:pad_off[n] + x1 - a])
        slots.append(jnp.concatenate(cols, axis=1))
    return jnp.stack(slots, axis=0)


HEAD = PAD + N_META


def _ln0_in(j, x_ref, meta_ref):
    first = jnp.concatenate([jnp.zeros((PAD, meta_ref.shape[1]), F32), meta_ref[...]], axis=0)
    return jnp.where(j == 0, first[None], x_ref[...])


def _ln0_fwd(x, meta, g, b):
    bsz, seq, d = x.shape
    nb = seq // HEAD + 1

    def body(x_ref, m_ref, g_ref, b_ref, o_ref, ob_ref):
        y, _, _ = _ln_fwd(_ln0_in(pl.program_id(0), x_ref, m_ref), g_ref[...], b_ref[...])
        o_ref[...] = y
        ob_ref[...] = _bf(y)

    row = pl.BlockSpec((bsz, HEAD, d), lambda j: (0, j, 0))
    h0, h0b = _pcall(
        body, name="ln0_fwd", grid=(nb,),
        in_specs=[pl.BlockSpec((bsz, HEAD, d), lambda j: (0, jnp.maximum(j - 1, 0), 0)), _const((N_META, d)),
                  _const((1, d)), _const((1, d))],
        out_specs=[row, row],
        out_shape=[jax.ShapeDtypeStruct((bsz, nb * HEAD, d), F32), jax.ShapeDtypeStruct((bsz, nb * HEAD, d), BF16)],
        compiler_params=_cp(("arbitrary",)),
    )(x, meta, g, b)
    return h0.reshape(-1, d), h0b.reshape(-1, d)


def _ln0_bwd(x, meta, dr1, dpw, g):
    bsz, seq, d = x.shape
    nb = seq // HEAD + 1

    def body(x_ref, m_ref, a_ref, c_ref, g_ref, o_ref, dg_ref, db_ref, dm_ref):
        j = pl.program_id(0)

        @pl.when(j == 0)
        def _():
            dg_ref[...] = jnp.zeros_like(dg_ref)
            db_ref[...] = jnp.zeros_like(db_ref)
            dm_ref[...] = jnp.zeros_like(dm_ref)

        dy = ALPHA * a_ref[...] + c_ref[...]
        _, xhat, rstd = _ln_fwd(_ln0_in(j, x_ref, m_ref), g_ref[...], 0.0)
        dx = _ln_bwd(dy, xhat, rstd, g_ref[...])
        o_ref[...] = dx
        dg_ref[...] += _colsum((dy * xhat).reshape(bsz * HEAD, d))
        db_ref[...] += _colsum(dy.reshape(bsz * HEAD, d))

        @pl.when(j == 0)
        def _():
            dm_ref[...] += jnp.sum(dx[:, PAD:, :], axis=0)

    row = pl.BlockSpec((bsz, HEAD, d), lambda j: (0, j, 0))
    tok = pl.BlockSpec((bsz, HEAD, d), lambda j: (0, jnp.maximum(j - 1, 0), 0))
    lp = nb * HEAD
    return _pcall(
        body, name="ln0_bwd", grid=(nb,),
        in_specs=[tok, _const((N_META, d)), row, row, _const((1, d))],
        out_specs=[tok, _const((1, d)), _const((1, d)), _const((N_META, d))],
        out_shape=[jax.ShapeDtypeStruct((bsz, seq, d), F32), jax.ShapeDtypeStruct((1, d), F32),
                   jax.ShapeDtypeStruct((1, d), F32), jax.ShapeDtypeStruct((N_META, d), F32)],
        compiler_params=_cp(("arbitrary",)),
    )(x, meta, dr1.reshape(bsz, lp, d), dpw.reshape(bsz, lp, d), g)


IN_CHUNK = 1152


def _chunk_cols(w):
    k, n = w.shape
    return jnp.transpose(w.reshape(k, n // IN_CHUNK, IN_CHUNK), (1, 0, 2))


def _inproj(h0b, w3, bias, lp):
    r, d = h0b.shape
    nj, _, tn = w3.shape
    tm = _row_tile(lp, 2112)
    tps = lp // tm

    def body(a_ref, w_ref, b_ref, o_ref, gate_ref):
        i = pl.program_id(0)
        j = pl.program_id(1)
        acc = _dot(a_ref[...], w_ref[j]) + b_ref[...]
        t = (i % tps) * tm + lax.broadcasted_iota(jnp.int32, (tm, 1), 0)
        acc = jnp.where(t >= PAD, acc, 0.0)
        o_ref[...] = _bf(acc)

        @pl.when(j == nj - 1)
        def _():
            gate_ref[...] = acc[:, tn - 128:]

    return _pcall(
        body, name="inproj", grid=(r // tm, nj),
        in_specs=[pl.BlockSpec((tm, d), lambda i, j: (i, 0)), _resident(w3.shape),
                  pl.BlockSpec((1, tn), lambda i, j: (0, j))],
        out_specs=[pl.BlockSpec((tm, tn), lambda i, j: (i, j)), pl.BlockSpec((tm, 128), lambda i, j: (i, 0))],
        out_shape=[jax.ShapeDtypeStruct((r, nj * tn), BF16), jax.ShapeDtypeStruct((r, 128), F32)],
        compiler_params=_cp(("parallel", "arbitrary"), 48),
    )(h0b, w3, bias)


def _mm_tn(a, b, *, name, split=1, colsum=False, tk_want=2112):
    r, m = a.shape
    n = b.shape[1]
    tk = _row_tile(r, tk_want)
    tm = min(m, 1024)
    ns = n // split
    tn = ns
    for cand in (1024, 1152, 640, 512, 128):
        if ns % cand == 0 and cand <= ns:
            tn = cand
            break
    nb = ns // tn
    nk = r // tk

    def body(a_ref, b_ref, o_ref, *rest):
        acc = rest[-1]
        k = pl.program_id(2)

        @pl.when(k == 0)
        def _():
            acc[...] = jnp.zeros_like(acc)

        bt = b_ref[...]
        acc[...] += _dot(_bf(a_ref[...]), _bf(bt), TN)

        @pl.when(k == nk - 1)
        def _():
            o_ref[...] = acc[...]

        if colsum:
            cs_ref = rest[0]

            @pl.when(k == 0)
            def _():
                cs_ref[...] = jnp.zeros_like(cs_ref)

            cs_ref[...] += _colsum(bt.astype(F32))

    out_specs = [pl.BlockSpec((None, tm, tn), lambda i, j, k: (j // nb, i, j % nb))]
    out_shape = [jax.ShapeDtypeStruct((split, m, ns), F32)]
    if colsum:
        assert m == tm
        out_specs.append(pl.BlockSpec((1, tn), lambda i, j, k: (0, j)))
        out_shape.append(jax.ShapeDtypeStruct((1, n), F32))
    res = _pcall(
        body, name=name, grid=(m // tm, n // tn, nk),
        in_specs=[pl.BlockSpec((tk, tm), lambda i, j, k: (k, i)), pl.BlockSpec((tk, tn), lambda i, j, k: (k, j))],
        out_specs=out_specs, out_shape=out_shape,
        scratch_shapes=[pltpu.VMEM((tm, tn), F32)],
        compiler_params=_cp(("parallel", "parallel", "arbitrary"), 56),
    )(a, b)
    return res if colsum else res[0]


def _mm_nt(a, w3, lp, *, name, dep=None):
    r, kdim = a.shape
    nk, n, tk = w3.shape
    assert nk * tk == kdim
    tm = _row_tile(lp, 1056)
    deps = [] if dep is None else [dep]

    def body(a_ref, w_ref, *rest):
        o_ref, acc = rest[-2:]
        k = pl.program_id(1)

        @pl.when(k == 0)
        def _():
            acc[...] = jnp.zeros_like(acc)

        acc[...] += _dot(_bf(a_ref[...]), w_ref[k], NT)

        @pl.when(k == nk - 1)
        def _():
            o_ref[...] = acc[...]

    return _pcall(
        body, name=name, grid=(r // tm, nk),
        in_specs=[pl.BlockSpec((tm, tk), lambda i, k: (i, k)), _resident(w3.shape)]
        + [_const(dp_.shape) for dp_ in deps],
        out_specs=pl.BlockSpec((tm, n), lambda i, k: (i, 0)),
        out_shape=jax.ShapeDtypeStruct((r, n), F32),
        scratch_shapes=[pltpu.VMEM((tm, n), F32)],
        compiler_params=_cp(("parallel", "arbitrary"), 48),
    )(a, w3, *deps)


def _s5_prep(lam_re, lam_im, log_dt, b_re_t, b_im_t):
    g, p = lam_re.shape
    h = b_re_t.shape[0]

    def body(lr_ref, li_ref, ldt_ref, br_ref, bi_ref, pr_ref, pi_ref, bbr_ref, bbi_ref):
        lr, li = lr_ref[...], li_ref[...]
        dt = jnp.exp(ldt_ref[...])
        e = jnp.exp(lr * dt)
        ar, ai = e * jnp.cos(li * dt), e * jnp.sin(li * dt)
        den = lr * lr + li * li
        cr = ((ar - 1.0) * lr + ai * li) / den
        ci = (ai * lr - (ar - 1.0) * li) / den
        br, bi = br_ref[...], bi_ref[...]
        bbr_ref[...] = cr[None] * br - ci[None] * bi
        bbi_ref[...] = cr[None] * bi + ci[None] * br
        xr, xi = ar, ai
        pr_ref[0] = xr
        pi_ref[0] = xi
        for t in range(1, 8):
            xr, xi = xr * ar - xi * ai, xr * ai + xi * ar
            pr_ref[t] = xr
            pi_ref[t] = xi

    sd = jax.ShapeDtypeStruct
    return _pcall(body, name="s5_prep",
                  out_shape=[sd((8, g, p), F32), sd((8, g, p), F32), sd((h, g, p), F32), sd((h, g, p), F32)])(
        lam_re, lam_im, log_dt, b_re_t, b_im_t)


def _s5_prep_bwd(lam_re, lam_im, log_dt, b_re_t, b_im_t, da_re, da_im, dbb_re_t, dbb_im_t):
    g, p = lam_re.shape
    h = b_re_t.shape[0]

    def body(lr_ref, li_ref, ldt_ref, br_ref, bi_ref, dar_ref, dai_ref, dbr_ref, dbi_ref,
             glr_ref, gli_ref, gdt_ref, gbr_ref, gbi_ref):
        lr, li = lr_ref[...], li_ref[...]
        dt = jnp.exp(ldt_ref[...])
        e = jnp.exp(lr * dt)
        ar, ai = e * jnp.cos(li * dt), e * jnp.sin(li * dt)
        den = lr * lr + li * li
        cr = ((ar - 1.0) * lr + ai * li) / den
        ci = (ai * lr - (ar - 1.0) * li) / den
        br, bi = br_ref[...], bi_ref[...]
        gr, gi = dbr_ref[...], dbi_ref[...]
        gbr_ref[...] = gr * cr[None] + gi * ci[None]
        gbi_ref[...] = gi * cr[None] - gr * ci[None]
        gcr = jnp.sum(gr * br + gi * bi, axis=0)
        gci = jnp.sum(gi * br - gr * bi, axis=0)
        ilr, ili = lr / den, -li / den
        gar = dar_ref[...] + gcr * ilr + gci * ili
        gai = dai_ref[...] + gci * ilr - gcr * ili
        qr, qi = cr * ilr - ci * ili, cr * ili + ci * ilr
        glr = -(gcr * qr + gci * qi)
        gli = -(gci * qr - gcr * qi)
        gzr = gar * ar + gai * ai
        gzi = gai * ar - gar * ai
        glr_ref[...] = glr + gzr * dt
        gli_ref[...] = gli + gzi * dt
        gdt_ref[...] = jnp.sum(gzr * lr + gzi * li, axis=1, keepdims=True) * dt

    sd = jax.ShapeDtypeStruct
    return _pcall(body, name="s5_prep_bwd",
                  out_shape=[sd((g, p), F32), sd((g, p), F32), sd((g, 1), F32), sd((h, g, p), F32), sd((h, g, p), F32)])(
        lam_re, lam_im, log_dt, b_re_t, b_im_t, da_re, da_im, dbb_re_t, dbb_im_t)


def _cmul(xr, xi, yr, yi):
    return xr * yr - xi * yi, xr * yi + xi * yr


def _dot5(a, b, dims=NN):
    return _dot(_bf(a), _bf(b), dims)


def _s5_fwd(p3, bk, cre, cim, apow, dskip):
    bsz, lp, _ = p3.shape
    tt = _row_tile(lp, 528, 8)
    nt = lp // tt
    nblk = tt // 8
    hw = 512

    def body(u_ref, bk_ref, cre_ref, cim_ref, ap_ref, d_ref, y_ref, xs_ref, car_ref):
        t = pl.program_id(2)

        @pl.when(t == 0)
        def _():
            car_ref[...] = jnp.zeros_like(car_ref)

        u = u_ref[...].astype(F32)
        xs_ref[...] = _dot5(u, bk_ref[...])
        ap = ap_ref[...]
        apr, api = ap[:, :hw], ap[:, hw:]
        rows = lax.broadcasted_iota(jnp.int32, (8, hw), 0)
        lev = [(d, jnp.where(rows < d, 0.0, jnp.broadcast_to(apr[d - 1:d, :], (8, hw))),
                jnp.where(rows < d, 0.0, jnp.broadcast_to(api[d - 1:d, :], (8, hw)))) for d in (1, 2, 4)]

        def blk(i, carry):
            cr, ci = carry
            off = pl.multiple_of(i * 8, 8)
            x = xs_ref[pl.ds(off, 8), :]
            xr, xi = x[:, :hw], x[:, hw:]
            for d, lr, li in lev:
                mr, mi = _cmul(pltpu.roll(xr, d, 0), pltpu.roll(xi, d, 0), lr, li)
                xr, xi = xr + mr, xi + mi
            mr, mi = _cmul(apr, api, cr, ci)
            xr, xi = xr + mr, xi + mi
            xs_ref[pl.ds(off, 8), :] = jnp.concatenate([xr, xi], axis=1)
            return xr[7:8, :], xi[7:8, :]

        c0 = car_ref[...]
        cr, ci = lax.fori_loop(0, nblk, blk, (c0[0:1, :hw], c0[0:1, hw:]), unroll=2)
        car_ref[...] = jnp.broadcast_to(jnp.concatenate([cr, ci], axis=1), car_ref.shape)
        xs = xs_ref[...]
        y_ref[...] = (_dot5(xs[:, :hw], cre_ref[...]) - _dot5(xs[:, hw:], cim_ref[...])
                      + d_ref[...] * u)

    ub = U_OFF // 128
    return _pcall(
        body, name="s5_fwd", grid=(S5_KCH, bsz, nt),
        in_specs=[pl.BlockSpec((None, tt, 128), lambda k, b, t: (b, t, ub + k)),
                  pl.BlockSpec((None, 128, 2 * hw), lambda k, b, t: (k, 0, 0)),
                  pl.BlockSpec((None, hw, 128), lambda k, b, t: (k, 0, 0)),
                  pl.BlockSpec((None, hw, 128), lambda k, b, t: (k, 0, 0)),
                  pl.BlockSpec((None, 8, 2 * hw), lambda k, b, t: (k, 0, 0)),
                  pl.BlockSpec((1, 128), lambda k, b, t: (0, k))],
        out_specs=[pl.BlockSpec((None, tt, 128), lambda k, b, t: (b, t, k)),
                   pl.BlockSpec((None, None, tt, 2 * hw), lambda k, b, t: (b, k, t, 0))],
        out_shape=[jax.ShapeDtypeStruct((bsz, lp, S5_KCH * 128), F32),
                   jax.ShapeDtypeStruct((bsz, S5_KCH, lp, 2 * hw), F32)],
        scratch_shapes=[pltpu.VMEM((8, 2 * hw), F32)],
        compiler_params=_cp(("parallel", "parallel", "arbitrary"), 40),
    )(p3, bk, cre, cim, apow, dskip)


def _s5_bwd(dp3, p3, dy3, xs, bk, cre, cim, apow_rev, dskip):
    bsz, lp, _ = p3.shape
    tt = _row_tile(lp, 528, 8)
    nt = lp // tt
    nblk = tt // 8
    hw = 512
    tb = tt // 8

    def body(dp_any, u_ref, dy_ref, xs_ref, halo_ref, bkt_ref, cre_ref, cim_ref, ap_ref, d_ref,
             du_ref, dbk_ref, dcre_ref, dcim_ref, da_ref, dd_ref, g_ref, ext_ref, car_ref):
        b = pl.program_id(1)
        t = pl.program_id(2)
        tidx = nt - 1 - t

        @pl.when(t == 0)
        def _():
            car_ref[...] = jnp.zeros_like(car_ref)

        @pl.when((b == 0) & (t == 0))
        def _():
            dbk_ref[...] = jnp.zeros_like(dbk_ref)
            dcre_ref[...] = jnp.zeros_like(dcre_ref)
            dcim_ref[...] = jnp.zeros_like(dcim_ref)
            da_ref[...] = jnp.zeros_like(da_ref)
            dd_ref[...] = jnp.zeros_like(dd_ref)

        u = u_ref[...].astype(F32)
        dy = dy_ref[...]
        g_ref[:, :hw] = _dot5(dy, cre_ref[...])
        g_ref[:, hw:] = -_dot5(dy, cim_ref[...])
        ap = ap_ref[...]
        apr, api = ap[:, :hw], -ap[:, hw:]
        rows = lax.broadcasted_iota(jnp.int32, (8, hw), 0)
        lev = [(d, jnp.where(rows >= 8 - d, 0.0, jnp.broadcast_to(apr[8 - d:9 - d, :], (8, hw))),
                jnp.where(rows >= 8 - d, 0.0, jnp.broadcast_to(api[8 - d:9 - d, :], (8, hw)))) for d in (1, 2, 4)]

        def blk(i, carry):
            cr, ci = carry
            off = pl.multiple_of((nblk - 1 - i) * 8, 8)
            x = g_ref[pl.ds(off, 8), :]
            xr, xi = x[:, :hw], x[:, hw:]
            for d, lr, li in lev:
                mr, mi = _cmul(pltpu.roll(xr, 8 - d, 0), pltpu.roll(xi, 8 - d, 0), lr, li)
                xr, xi = xr + mr, xi + mi
            mr, mi = _cmul(apr, api, cr, ci)
            xr, xi = xr + mr, xi + mi
            g_ref[pl.ds(off, 8), :] = jnp.concatenate([xr, xi], axis=1)
            return xr[0:1, :], xi[0:1, :]

        c0 = car_ref[...]
        cr, ci = lax.fori_loop(0, nblk, blk, (c0[0:1, :hw], c0[0:1, hw:]), unroll=2)
        car_ref[...] = jnp.broadcast_to(jnp.concatenate([cr, ci], axis=1), car_ref.shape)

        gg = g_ref[...]
        du = _dot5(gg, bkt_ref[...]) + d_ref[...] * dy
        trow = tidx * tt + lax.broadcasted_iota(jnp.int32, (tt, 1), 0)
        du_ref[...] = jnp.where(trow >= PAD, du, 0.0).astype(du_ref.dtype)
        dbk_ref[...] += _dot5(u, gg, TN)
        xsv = xs_ref[...]
        dcre_ref[...] += _dot5(dy, xsv[:, :hw], TN)
        dcim_ref[...] -= _dot5(dy, xsv[:, hw:], TN)
        dd_ref[...] += _colsum(dy * u)
        ext_ref[0:8, :] = jnp.where(tidx == 0, 0.0, halo_ref[...])
        ext_ref[8:, :] = xsv
        xp = ext_ref[pl.ds(7, tt), :]
        gr, gi, pr, pi = gg[:, :hw], gg[:, hw:], xp[:, :hw], xp[:, hw:]
        da_ref[:, :hw] += _colsum(gr * pr + gi * pi)
        da_ref[:, hw:] += _colsum(gi * pr - gr * pi)

    ub = U_OFF // 128
    sd = jax.ShapeDtypeStruct
    rt = lambda t: nt - 1 - t
    tr = lambda a: jnp.swapaxes(a, 1, 2)
    res = _pcall(
        body, name="s5_bwd", grid=(S5_KCH, bsz, nt),
        in_specs=[pl.BlockSpec(memory_space=pl.ANY),
                  pl.BlockSpec((None, tt, 128), lambda k, b, t: (b, rt(t), ub + k)),
                  pl.BlockSpec((None, tt, 128), lambda k, b, t: (b, rt(t), k)),
                  pl.BlockSpec((None, None, tt, 2 * hw), lambda k, b, t: (b, k, rt(t), 0)),
                  pl.BlockSpec((None, None, 8, 2 * hw), lambda k, b, t: (b, k, jnp.maximum(rt(t) * tb - 1, 0), 0)),
                  pl.BlockSpec((None, 2 * hw, 128), lambda k, b, t: (k, 0, 0)),
                  pl.BlockSpec((None, 128, hw), lambda k, b, t: (k, 0, 0)),
                  pl.BlockSpec((None, 128, hw), lambda k, b, t: (k, 0, 0)),
                  pl.BlockSpec((None, 8, 2 * hw), lambda k, b, t: (k, 0, 0)),
                  pl.BlockSpec((1, 128), lambda k, b, t: (0, k))],
        out_specs=[pl.BlockSpec((None, tt, 128), lambda k, b, t: (b, rt(t), ub + k)),
                   pl.BlockSpec((None, 128, 2 * hw), lambda k, b, t: (k, 0, 0)),
                   pl.BlockSpec((None, 128, hw), lambda k, b, t: (k, 0, 0)),
                   pl.BlockSpec((None, 128, hw), lambda k, b, t: (k, 0, 0)),
                   pl.BlockSpec((None, 1, 2 * hw), lambda k, b, t: (k, 0, 0)),
                   pl.BlockSpec((1, 128), lambda k, b, t: (0, k))],
        out_shape=[sd(dp3.shape, dp3.dtype), sd((S5_KCH, 128, 2 * hw), F32), sd((S5_KCH, 128, hw), F32),
                   sd((S5_KCH, 128, hw), F32), sd((S5_KCH, 1, 2 * hw), F32), sd((1, S5_KCH * 128), F32)],
        scratch_shapes=[pltpu.VMEM((tt, 2 * hw), F32), pltpu.VMEM((tt + 8, 2 * hw), F32), pltpu.VMEM((8, 2 * hw), F32)],
        input_output_aliases={0: 0},
        compiler_params=_cp(("arbitrary", "arbitrary", "arbitrary"), 48),
    )(dp3, p3, dy3, xs, xs, tr(bk), tr(cre), tr(cim), apow_rev, dskip)
    return res[0], res[1], tr(res[2]), tr(res[3]), res[4], res[5]


_G0 = math.sqrt(2.0 / math.pi)
_G1 = 0.044715


def _gelu(y):
    return 0.5 * y * (1.0 + jnp.tanh(_G0 * (y + _G1 * y * y * y)))


def _gelu_grad(y):
    th = jnp.tanh(_G0 * (y + _G1 * y * y * y))
    return 0.5 * (1.0 + th) + 0.5 * y * (1.0 - th * th) * _G0 * (1.0 + 3.0 * _G1 * y * y)


def _glu_fwd(y_s5, wglu_g, lp):
    r, w = y_s5.shape
    tm = _row_tile(lp, 416)
    cw = wglu_g.shape[2]

    def body(y_ref, w_ref, gy_ref, z_ref, o_ref):
        gy = _bf(_gelu(y_ref[...]))
        gy_ref[...] = gy
        zs = [_dot(gy, w_ref[s]) for s in range(4)]
        for s in range(4):
            z_ref[:, s * cw:(s + 1) * cw] = _bf(zs[s])
        o_ref[:, :cw] = _bf(zs[0] * _sig(zs[2]))
        o_ref[:, cw:] = _bf(zs[1] * _sig(zs[3]))

    sd = jax.ShapeDtypeStruct
    return _pcall(
        body, name="glu_fwd", grid=(r // tm,),
        in_specs=[pl.BlockSpec((tm, w), lambda i: (i, 0)), _resident(wglu_g.shape)],
        out_specs=[pl.BlockSpec((tm, w), lambda i: (i, 0)), pl.BlockSpec((tm, 4 * cw), lambda i: (i, 0)),
                   pl.BlockSpec((tm, 2 * cw), lambda i: (i, 0))],
        out_shape=[sd((r, w), BF16), sd((r, 4 * cw), BF16), sd((r, 2 * cw), BF16)],
        compiler_params=_cp(("parallel",), 40),
    )(y_s5, wglu_g)


def _glu_bwd(dyg, z, y_s5, wglu_g, lp):
    r, w = y_s5.shape
    tm = _row_tile(lp, 416)
    cw = wglu_g.shape[2]

    def body(d_ref, z_ref, y_ref, w_ref, dz_ref, dy_ref):
        d = d_ref[...].astype(F32)
        zz = z_ref[...].astype(F32)
        acc = jnp.zeros((tm, w), F32)
        for s in range(2):
            z1 = zz[:, s * cw:(s + 1) * cw]
            sg = _sig(zz[:, (2 + s) * cw:(3 + s) * cw])
            dd = d[:, s * cw:(s + 1) * cw]
            dz1 = _bf(dd * sg)
            dz2 = _bf(dd * z1 * sg * (1.0 - sg))
            dz_ref[:, s * cw:(s + 1) * cw] = dz1
            dz_ref[:, (2 + s) * cw:(3 + s) * cw] = dz2
            acc += _dot(dz1, w_ref[s], NT) + _dot(dz2, w_ref[2 + s], NT)
        dy_ref[...] = acc * _gelu_grad(y_ref[...])

    sd = jax.ShapeDtypeStruct
    return _pcall(
        body, name="glu_bwd", grid=(r // tm,),
        in_specs=[pl.BlockSpec((tm, 2 * cw), lambda i: (i, 0)), pl.BlockSpec((tm, 4 * cw), lambda i: (i, 0)),
                  pl.BlockSpec((tm, w), lambda i: (i, 0)), _resident(wglu_g.shape)],
        out_specs=[pl.BlockSpec((tm, 4 * cw), lambda i: (i, 0)), pl.BlockSpec((tm, w), lambda i: (i, 0))],
        out_shape=[sd((r, 4 * cw), BF16), sd((r, w), F32)],
        compiler_params=_cp(("parallel",), 40),
    )(dyg, z, y_s5, wglu_g)


def _conv_fwd(p3, cw, cb):
    bsz, lp, _ = p3.shape
    tt = _row_tile(lp, 704)
    nt = lp // tt
    tb = tt // 8
    c = cw.shape[1]
    qb = Q_OFF // c

    hr = HALO_ROWS
    off = hr - (CONV_W - 1)

    def body(x_ref, halo_ref, w_ref, b_ref, pre_ref, act_ref, ext_ref):
        t = pl.program_id(1)
        ext_ref[0:hr, :] = jnp.where(t == 0, 0.0, halo_ref[...].astype(F32))
        ext_ref[hr:, :] = x_ref[...].astype(F32)
        w = w_ref[...]
        acc = b_ref[...] + w[0:1, :] * ext_ref[pl.ds(off, tt), :]
        for j in range(1, CONV_W):
            acc = acc + w[j:j + 1, :] * ext_ref[pl.ds(off + j, tt), :]
        pre_ref[...] = _bf(acc)
        act_ref[...] = _bf(acc * _sig(acc))

    sd = jax.ShapeDtypeStruct
    return _pcall(
        body, name="conv_fwd", grid=(bsz, nt),
        in_specs=[pl.BlockSpec((None, tt, c), lambda b, t: (b, t, qb)),
                  pl.BlockSpec((None, hr, c), lambda b, t: (b, jnp.maximum(t * (tt // hr) - 1, 0), qb)),
                  _const((CONV_W, c)), _const((1, c))],
        out_specs=[pl.BlockSpec((None, tt, c), lambda b, t: (b, t, 0))] * 2,
        out_shape=[sd((bsz, lp, c), BF16)] * 2,
        scratch_shapes=[pltpu.VMEM((tt + hr, c), F32)],
        compiler_params=_cp(("parallel", "parallel")),
    )(p3, p3, cw, cb)


def _conv_bwd(dp3, p3, dact3, pre3, cw):
    bsz, lp, _ = p3.shape
    tt = _row_tile(lp, 704)
    nt = lp // tt
    tb = tt // 8
    c = cw.shape[1]
    qb = Q_OFF // c

    hr = HALO_ROWS
    off = hr - (CONV_W - 1)

    def silu_grad(x):
        s = _sig(x)
        return s * (1.0 + x * (1.0 - s))

    def body(dp_any, x_ref, xh_ref, d_ref, dh_ref, pre_ref, preh_ref, w_ref, o_ref, dw_ref, db_ref, ext_ref, dext_ref):
        b = pl.program_id(0)
        t = pl.program_id(1)

        @pl.when((b == 0) & (t == 0))
        def _():
            dw_ref[...] = jnp.zeros_like(dw_ref)
            db_ref[...] = jnp.zeros_like(db_ref)

        dc = d_ref[...].astype(F32) * silu_grad(pre_ref[...].astype(F32))
        dch = jnp.where(t == nt - 1, 0.0, dh_ref[...].astype(F32) * silu_grad(preh_ref[...].astype(F32)))
        dext_ref[0:tt, :] = dc
        dext_ref[tt:, :] = dch
        ext_ref[0:hr, :] = jnp.where(t == 0, 0.0, xh_ref[...].astype(F32))
        ext_ref[hr:, :] = x_ref[...].astype(F32)
        w = w_ref[...]
        acc = w[CONV_W - 1:CONV_W, :] * dc
        for j in range(CONV_W - 1):
            acc = acc + w[j:j + 1, :] * dext_ref[pl.ds(CONV_W - 1 - j, tt), :]
        trow = t * tt + lax.broadcasted_iota(jnp.int32, (tt, 1), 0)
        o_ref[...] = jnp.where(trow >= PAD, acc, 0.0).astype(o_ref.dtype)
        db_ref[...] += _colsum(dc)
        for j in range(CONV_W):
            dw_ref[j:j + 1, :] += _colsum(dc * ext_ref[pl.ds(off + j, tt), :])

    sd = jax.ShapeDtypeStruct
    nxt = lambda t: jnp.minimum((t + 1) * (tt // hr), lp // hr - 1)
    return _pcall(
        body, name="conv_bwd", grid=(bsz, nt),
        in_specs=[pl.BlockSpec(memory_space=pl.ANY),
                  pl.BlockSpec((None, tt, c), lambda b, t: (b, t, qb)),
                  pl.BlockSpec((None, hr, c), lambda b, t: (b, jnp.maximum(t * (tt // hr) - 1, 0), qb)),
                  pl.BlockSpec((None, tt, c), lambda b, t: (b, t, 0)),
                  pl.BlockSpec((None, hr, c), lambda b, t: (b, nxt(t), 0)),
                  pl.BlockSpec((None, tt, c), lambda b, t: (b, t, 0)),
                  pl.BlockSpec((None, hr, c), lambda b, t: (b, nxt(t), 0)),
                  _const((CONV_W, c))],
        out_specs=[pl.BlockSpec((None, tt, c), lambda b, t: (b, t, qb)), _const((CONV_W, c)), _const((1, c))],
        out_shape=[sd(dp3.shape, dp3.dtype), sd((CONV_W, c), F32), sd((1, c), F32)],
        scratch_shapes=[pltpu.VMEM((tt + hr, c), F32), pltpu.VMEM((tt + hr, c), F32)],
        input_output_aliases={0: 0},
        compiler_params=_cp(("arbitrary", "arbitrary")),
    )(dp3, p3, p3, dact3, dact3, pre3, pre3, cw)


def _mlstm_gates(g, h_idx, c_idx, lc):
    lane = lax.broadcasted_iota(jnp.int32, g.shape, 1)
    i_col = jnp.sum(jnp.where(lane == h_idx, g, 0.0), axis=1, keepdims=True)
    f_col = jnp.sum(jnp.where(lane == M_HEADS + h_idx, g, 0.0), axis=1, keepdims=True)
    row = lax.broadcasted_iota(jnp.int32, (lc, 1), 0)
    valid = (c_idx * lc + row) >= PAD
    li = jnp.where(valid, i_col, NEG)
    lf = jnp.where(valid, jnp.minimum(f_col, 0.0) - jnp.log(1.0 + jnp.exp(-jnp.abs(f_col))), 0.0)
    r2 = lax.broadcasted_iota(jnp.int32, (lc, lc), 0)
    c2 = lax.broadcasted_iota(jnp.int32, (lc, lc), 1)
    eye = r2 == c2
    tril = r2 >= c2
    to_row = lambda col: jnp.sum(jnp.where(eye, col, 0.0), axis=0, keepdims=True)
    lf_row = to_row(lf)
    b_col = jnp.sum(jnp.where(tril, lf_row, 0.0), axis=1, keepdims=True)
    b_row = to_row(b_col)
    li_row = to_row(li)
    d_mat = jnp.where(tril, b_col - b_row + li_row, NEG)
    return dict(f_col=f_col, valid=valid, li=li, b_col=b_col, d_mat=d_mat, eye=eye, r2=r2, c2=c2, row=row,
                to_row=to_row)


def _mlstm_chunk(q, ks, v, gq, c_st, n_st, m_st, lc):
    b_col, d_mat = gq["b_col"], gq["d_mat"]
    m_inter = b_col + m_st
    m_row = jnp.maximum(m_inter, jnp.max(d_mat, axis=1, keepdims=True))
    w_intra = jnp.exp(d_mat - m_row)
    w_inter = jnp.exp(m_inter - m_row)
    qb, kb, vb, cb = _bf(q), _bf(ks), _bf(v), _bf(c_st)
    s = _dot(qb, kb, NT) * w_intra
    qc = _dot(qb, cb)
    num = _dot(_bf(s), vb) + w_inter * qc
    qn = jnp.sum(q * n_st, axis=1, keepdims=True)
    den = jnp.sum(s, axis=1, keepdims=True) + w_inter * qn
    e = jnp.exp(-m_row)
    nn = jnp.maximum(jnp.abs(den), e)
    b_last = b_col[lc - 1:lc, :]
    g_log = b_last - b_col + gq["li"]
    m_new = jnp.maximum(b_last + m_st, jnp.max(g_log, axis=0, keepdims=True))
    w_k = jnp.exp(g_log - m_new)
    decay = jnp.exp(b_last + m_st - m_new)
    return dict(w_intra=w_intra, w_inter=w_inter, qb=qb, kb=kb, vb=vb, cb=cb, s=s, qc=qc, num=num, qn=qn, den=den,
                e=e, nn=nn, m_new=m_new, w_k=w_k, decay=decay)


def _chunks_per_step(nc):
    return max(c for c in (3, 2, 1) if nc % c == 0)


def _mlstm_fwd(qk3, p3, pg3):
    bsz, lp, _ = p3.shape
    lc = M_CHUNK
    nc = lp // lc
    dk, dv = 128, 256
    scale = dk ** -0.5

    cps = _chunks_per_step(nc)
    rows = cps * lc

    def body(q_ref, k_ref, v_ref, g_ref, h_ref, cs_ref, ns_ref, ms_ref, c_sc, n_sc, m_sc):
        st = pl.program_id(1)

        @pl.when(st == 0)
        def _():
            c_sc[...] = jnp.zeros_like(c_sc)
            n_sc[...] = jnp.zeros_like(n_sc)
            m_sc[...] = jnp.zeros_like(m_sc)

        for j in range(cps):
            rs = slice(j * lc, (j + 1) * lc)
            g = g_ref[rs, :]
            for hh in range(M_HEADS):
                c_st, n_st, m_all = c_sc[hh], n_sc[hh], m_sc[hh]
                cs_ref[hh, j] = c_st
                ns_ref[hh, j] = n_st
                ms_ref[hh, j] = m_all
                m_st = m_all[:, 0:1]
                q = q_ref[rs, hh * dk:(hh + 1) * dk].astype(F32)
                ks = k_ref[rs, hh * dk:(hh + 1) * dk].astype(F32) * scale
                v = v_ref[rs, hh * dv:(hh + 1) * dv]
                gq = _mlstm_gates(g, hh, st * cps + j, lc)
                f = _mlstm_chunk(q, ks, v, gq, c_st, n_st, m_st, lc)
                h_ref[rs, hh * dv:(hh + 1) * dv] = _bf(f["num"] / f["nn"])
                kw = ks * f["w_k"]
                c_sc[hh] = f["decay"] * c_st + _dot(_bf(kw), f["vb"], TN)
                n_sc[hh] = f["decay"] * n_st + _colsum(kw)
                m_sc[hh] = jnp.broadcast_to(f["m_new"], (1, 128))

    sd = jax.ShapeDtypeStruct
    nh = M_HEADS
    return _pcall(
        body, name="mlstm_fwd", grid=(bsz, nc // cps),
        in_specs=[pl.BlockSpec((None, rows, nh * dk), lambda b, c: (b, c, 0)),
                  pl.BlockSpec((None, rows, nh * dk), lambda b, c: (b, c, 1)),
                  pl.BlockSpec((None, rows, nh * dv), lambda b, c: (b, c, V_OFF // (nh * dv))),
                  pl.BlockSpec((None, rows, 128), lambda b, c: (b, c, 0))],
        out_specs=[pl.BlockSpec((None, rows, nh * dv), lambda b, c: (b, c, 0)),
                   pl.BlockSpec((None, nh, cps, dk, dv), lambda b, c: (b, 0, c, 0, 0)),
                   pl.BlockSpec((None, nh, cps, 1, dk), lambda b, c: (b, 0, c, 0, 0)),
                   pl.BlockSpec((None, nh, cps, 1, 128), lambda b, c: (b, 0, c, 0, 0))],
        out_shape=[sd((bsz, lp, nh * dv), BF16), sd((bsz, nh, nc, dk, dv), F32),
                   sd((bsz, nh, nc, 1, dk), F32), sd((bsz, nh, nc, 1, 128), F32)],
        scratch_shapes=[pltpu.VMEM((nh, dk, dv), F32), pltpu.VMEM((nh, 1, dk), F32), pltpu.VMEM((nh, 1, 128), F32)],
        compiler_params=_cp(("parallel", "arbitrary")),
    )(qk3, qk3, p3, pg3)


def _mlstm_bwd(dp3, qk3, p3, pg3, dh3, cs, ns, ms):
    bsz, lp, _ = p3.shape
    lc = M_CHUNK
    nc = lp // lc
    dk, dv = 128, 256
    scale = dk ** -0.5

    cps = _chunks_per_step(nc)
    nst = nc // cps
    rows = cps * lc

    def body(dp_any, q_ref, k_ref, v_ref, g_ref, dh_ref, cs_ref, ns_ref, ms_ref,
             dv_ref, dqk_ref, dg_ref, dc_sc, dn_sc):
        t = pl.program_id(1)
        st = nst - 1 - t

        @pl.when(t == 0)
        def _():
            dc_sc[...] = jnp.zeros_like(dc_sc)
            dn_sc[...] = jnp.zeros_like(dn_sc)

        lane = lax.broadcasted_iota(jnp.int32, (lc, 128), 1)
        for j in reversed(range(cps)):
            rs = slice(j * lc, (j + 1) * lc)
            g = g_ref[rs, :]
            dgate = jnp.zeros((lc, 128), F32)
            for hh in range(M_HEADS):
                dgate = head(hh, j, rs, st * cps + j, g, lane, dgate, q_ref, k_ref, v_ref, dh_ref, cs_ref, ns_ref,
                             ms_ref, dv_ref, dqk_ref, dc_sc, dn_sc)
            dg_ref[rs, :] = dgate.astype(dg_ref.dtype)

    def head(hh, j, sl, c, g, lane, dgate, q_ref, k_ref, v_ref, dh_ref, cs_ref, ns_ref, ms_ref, dv_ref, dqk_ref,
             dc_sc, dn_sc):
        c_st, n_st = cs_ref[hh, j], ns_ref[hh, j]
        m_st = ms_ref[hh, j][:, 0:1]
        q = q_ref[sl, hh * dk:(hh + 1) * dk].astype(F32)
        ks = k_ref[sl, hh * dk:(hh + 1) * dk].astype(F32) * scale
        v = v_ref[sl, hh * dv:(hh + 1) * dv]
        dh = dh_ref[sl, hh * dv:(hh + 1) * dv].astype(F32)
        gq = _mlstm_gates(g, hh, c, lc)
        f = _mlstm_chunk(q, ks, v, gq, c_st, n_st, m_st, lc)
        eye, r2, c2, row, valid = gq["eye"], gq["r2"], gq["c2"], gq["row"], gq["valid"]
        w_intra, w_inter, s, nn, den = f["w_intra"], f["w_inter"], f["s"], f["nn"], f["den"]
        qb, kb, vb, cb, w_k, decay = f["qb"], f["kb"], f["vb"], f["cb"], f["w_k"], f["decay"]
        d_c, d_n = dc_sc[hh], dn_sc[hh]
        d_cb = _bf(d_c)

        hout = f["num"] / nn
        dnum = dh / nn
        d_nn = -jnp.sum(dh * hout, axis=1, keepdims=True) / nn
        dden = jnp.where(jnp.abs(den) > f["e"], d_nn * jnp.sign(den), 0.0)
        wdnum = w_inter * dnum
        wdden = w_inter * dden
        ds = _dot(_bf(dnum), vb, NT) + dden
        dsw = _bf(ds * w_intra)
        dq = _dot(dsw, kb) + _dot(_bf(wdnum), cb, NT) + wdden * n_st
        dkw = _dot(vb, d_cb, NT) + d_n
        dks = _dot(dsw, qb, TN) + dkw * w_k
        kw = ks * w_k
        dvv = _dot(_bf(s), _bf(dnum), TN) + _dot(_bf(kw), d_cb)
        dd = ds * s
        rs = jnp.sum(dd, axis=1, keepdims=True)
        cs_col = jnp.sum(jnp.where(eye, jnp.sum(dd, axis=0, keepdims=True), 0.0), axis=1, keepdims=True)
        dwi = jnp.sum(dnum * f["qc"], axis=1, keepdims=True) + dden * f["qn"]
        db = rs - cs_col + dwi * w_inter
        dli = cs_col
        ddecay = jnp.sum(jnp.sum(d_c * c_st, axis=1, keepdims=True), axis=0, keepdims=True) \
            + jnp.sum(d_n * n_st, axis=1, keepdims=True)
        dgl = jnp.sum(dkw * ks, axis=1, keepdims=True) * w_k
        dblast = ddecay * decay + jnp.sum(dgl, axis=0, keepdims=True)
        db = db - dgl + jnp.where(row == lc - 1, dblast, 0.0)
        dli = dli + dgl
        db_row = gq["to_row"](db)
        dlf = jnp.sum(jnp.where(c2 >= r2, db_row, 0.0), axis=1, keepdims=True)
        dlf = jnp.where(valid, dlf, 0.0)
        dgate = jnp.where(lane == hh, jnp.where(valid, dli, 0.0), dgate)
        dgate = jnp.where(lane == M_HEADS + hh, dlf / (1.0 + jnp.exp(gq["f_col"])), dgate)
        dqk_ref[sl, hh * dk:(hh + 1) * dk] = _bf(dq)
        dqk_ref[sl, (M_HEADS + hh) * dk:(M_HEADS + hh + 1) * dk] = _bf(dks * scale)
        dv_ref[sl, hh * dv:(hh + 1) * dv] = dvv.astype(dv_ref.dtype)
        dc_sc[hh] = decay * d_c + _dot(qb, _bf(wdnum), TN)
        dn_sc[hh] = decay * d_n + _colsum(q * wdden)
        return dgate

    sd = jax.ShapeDtypeStruct
    nh = M_HEADS
    rc = lambda c: nst - 1 - c
    return _pcall(
        body, name="mlstm_bwd", grid=(bsz, nst),
        in_specs=[pl.BlockSpec(memory_space=pl.ANY),
                  pl.BlockSpec((None, rows, nh * dk), lambda b, c: (b, rc(c), 0)),
                  pl.BlockSpec((None, rows, nh * dk), lambda b, c: (b, rc(c), 1)),
                  pl.BlockSpec((None, rows, nh * dv), lambda b, c: (b, rc(c), V_OFF // (nh * dv))),
                  pl.BlockSpec((None, rows, 128), lambda b, c: (b, rc(c), 0)),
                  pl.BlockSpec((None, rows, nh * dv), lambda b, c: (b, rc(c), 0)),
                  pl.BlockSpec((None, nh, cps, dk, dv), lambda b, c: (b, 0, rc(c), 0, 0)),
                  pl.BlockSpec((None, nh, cps, 1, dk), lambda b, c: (b, 0, rc(c), 0, 0)),
                  pl.BlockSpec((None, nh, cps, 1, 128), lambda b, c: (b, 0, rc(c), 0, 0))],
        out_specs=[pl.BlockSpec((None, rows, nh * dv), lambda b, c: (b, rc(c), V_OFF // (nh * dv))),
                   pl.BlockSpec((None, rows, 2 * nh * dk), lambda b, c: (b, rc(c), 0)),
                   pl.BlockSpec((None, rows, 128), lambda b, c: (b, rc(c), 0))],
        out_shape=[sd(dp3.shape, dp3.dtype), sd((bsz, lp, 2 * nh * dk), BF16), sd((bsz, lp, 128), dp3.dtype)],
        scratch_shapes=[pltpu.VMEM((nh, dk, dv), F32), pltpu.VMEM((nh, 1, dk), F32)],
        input_output_aliases={0: 0},
        compiler_params=_cp(("arbitrary", "arbitrary")),
    )(dp3, qk3, qk3, p3, pg3, dh3, cs, ns, ms)


def _headnorm(x):
    dv = x.shape[1] // M_HEADS
    xh, rs = [], []
    for h in range(M_HEADS):
        xx = x[:, h * dv:(h + 1) * dv]
        mu = jnp.mean(xx, axis=-1, keepdims=True)
        xc = xx - mu
        rstd = lax.rsqrt(jnp.mean(xc * xc, axis=-1, keepdims=True) + LN_EPS)
        xh.append(xc * rstd)
        rs.append(rstd)
    return jnp.concatenate(xh, axis=1), rs


def _mix_fwd(hm, p, ys5g, h0, gn, wmo_bf, wo_bf, lp):
    r, d = hm.shape
    tm = _row_tile(lp, 384)

    def body(hm_ref, o_ref, gs_ref, gm_ref, ys_ref, h0_ref, gn_ref, wmo_ref, wo_ref,
             ymin_ref, mix_ref, r1_ref):
        xhat, _ = _headnorm(hm_ref[...].astype(F32))
        ymin = _bf(_sig(o_ref[...].astype(F32)) * (xhat * gn_ref[...]))
        ymin_ref[...] = ymin
        ym = _dot(ymin, wmo_ref[...])
        mix = _bf(_sig(gs_ref[...].astype(F32)) * ys_ref[...].astype(F32) + _sig(gm_ref[...].astype(F32)) * ym)
        mix_ref[...] = mix
        r1_ref[...] = ALPHA * h0_ref[...] + _dot(mix, wo_ref[...])

    sd = jax.ShapeDtypeStruct
    row = pl.BlockSpec((tm, d), lambda i: (i, 0))
    return _pcall(
        body, name="mix_fwd", grid=(r // tm,),
        in_specs=[row, pl.BlockSpec((tm, d), lambda i: (i, O_OFF // d)), pl.BlockSpec((tm, d), lambda i: (i, GS_OFF // d)),
                  pl.BlockSpec((tm, d), lambda i: (i, GM_OFF // d)), row, row, _const((1, d)),
                  _resident((d, d)), _resident((d, d))],
        out_specs=[row] * 3,
        out_shape=[sd((r, d), BF16), sd((r, d), BF16), sd((r, d), F32)],
        compiler_params=_cp(("parallel",), 48),
    )(hm, p, p, p, ys5g, h0, gn, wmo_bf, wo_bf)


def _mix_bwd(dr1, wo_bf, wmo_bf, p, ys5g, ymin, hm, gn, lp):
    r, d = hm.shape
    tm = _row_tile(lp, 384)
    dv = d // M_HEADS

    def body(dr1_ref, wo_ref, wmo_ref, o_ref, gs_ref, gm_ref, ys_ref, ym_ref, hm_ref, gn_ref,
             dp_ref, dys_ref, dym_ref, dhm_ref, dgn_ref):
        i = pl.program_id(0)

        @pl.when(i == 0)
        def _():
            dgn_ref[...] = jnp.zeros_like(dgn_ref)

        dmix = _dot(_bf(dr1_ref[...]), wo_ref[...], NT)
        sgs, sgm, so = (_sig(gs_ref[...].astype(F32)), _sig(gm_ref[...].astype(F32)), _sig(o_ref[...].astype(F32)))
        dys_ref[...] = _bf(dmix * sgs)
        dp_ref[:, d:2 * d] = _bf(dmix * ys_ref[...].astype(F32) * sgs * (1.0 - sgs))
        dym = dmix * sgm
        dym_ref[...] = _bf(dym)
        ym = _dot(ym_ref[...], wmo_ref[...])
        dp_ref[:, 2 * d:3 * d] = _bf(dmix * ym * sgm * (1.0 - sgm))
        dymin = _dot(_bf(dym), wmo_ref[...], NT)
        xhat, rs = _headnorm(hm_ref[...].astype(F32))
        gn_ = gn_ref[...]
        dp_ref[:, 0:d] = _bf(dymin * (xhat * gn_) * so * (1.0 - so))
        dhn = dymin * so
        dgn_ref[...] += _colsum(dhn * xhat)
        dxh = dhn * gn_
        for h in range(M_HEADS):
            sl = slice(h * dv, (h + 1) * dv)
            a, xh = dxh[:, sl], xhat[:, sl]
            m1 = jnp.mean(a, axis=-1, keepdims=True)
            m2 = jnp.mean(a * xh, axis=-1, keepdims=True)
            dhm_ref[:, sl] = _bf(rs[h] * (a - m1 - xh * m2))

    sd = jax.ShapeDtypeStruct
    row = pl.BlockSpec((tm, d), lambda i: (i, 0))
    vec = _const((1, d))
    return _pcall(
        body, name="mix_bwd", grid=(r // tm,),
        in_specs=[row, _resident((d, d)), _resident((d, d)),
                  pl.BlockSpec((tm, d), lambda i: (i, O_OFF // d)), pl.BlockSpec((tm, d), lambda i: (i, GS_OFF // d)),
                  pl.BlockSpec((tm, d), lambda i: (i, GM_OFF // d)), row, row, row, vec],
        out_specs=[pl.BlockSpec((tm, 3 * d), lambda i: (i, 0)), row, row, row, vec],
        out_shape=[sd((r, NP), BF16), sd((r, d), BF16), sd((r, d), BF16), sd((r, d), BF16), sd((1, d), F32)],
        compiler_params=_cp(("arbitrary",), 56),
    )(dr1, wo_bf, wmo_bf, p, p, p, ys5g, ymin, hm, gn)


def _mlp_fwd(r1, tgt, g1, b1, wup_g, wdn_bf, bup, g2, b2, lp):
    r, d = r1.shape
    tm = _row_tile(lp, 384)
    tps = lp // tm
    nf = wup_g.shape[0]

    def body(r1_ref, t_ref, g1_ref, b1_ref, wup_ref, wdn_ref, bup_ref, g2_ref, b2_ref,
             dr2_ref, h1b_ref, act_ref, loss_ref, dg2_ref, db2_ref):
        i = pl.program_id(0)

        @pl.when(i == 0)
        def _():
            loss_ref[...] = jnp.zeros_like(loss_ref)
            dg2_ref[...] = jnp.zeros_like(dg2_ref)
            db2_ref[...] = jnp.zeros_like(db2_ref)

        h1, _, _ = _ln_fwd(r1_ref[...], g1_ref[...], b1_ref[...])
        h1b = _bf(h1)
        h1b_ref[...] = h1b
        ff = jnp.zeros((tm, d), F32)
        for s in range(nf):
            up = _dot(h1b, wup_ref[s]) + bup_ref[:, s * d:(s + 1) * d]
            a = jnp.maximum(up, 0.0)
            a = _bf(a * a)
            act_ref[:, s * d:(s + 1) * d] = a
            ff = ff + _dot(a, wdn_ref[s * d:(s + 1) * d, :])
        r2 = ALPHA * h1 + ff
        g2 = g2_ref[...]
        y, xhat, rstd = _ln_fwd(r2, g2, b2_ref[...])
        t = (i % tps) * tm + lax.broadcasted_iota(jnp.int32, (tm, 1), 0)
        diff = jnp.where(t >= PAD + N_META, y - t_ref[...], 0.0)
        loss_ref[...] += 0.5 / d * jnp.sum(jnp.sum(diff * diff, axis=1, keepdims=True), axis=0, keepdims=True)
        dy = diff * (1.0 / d)
        dg2_ref[...] += _colsum(dy * xhat)
        db2_ref[...] += _colsum(dy)
        dr2_ref[...] = _ln_bwd(dy, xhat, rstd, g2)

    sd = jax.ShapeDtypeStruct
    row = pl.BlockSpec((tm, d), lambda i: (i, 0))
    vec = _const((1, d))
    return _pcall(
        body, name="mlp_fwd", grid=(r // tm,),
        in_specs=[row, row, vec, vec, _resident(wup_g.shape), _resident(wdn_bf.shape), _const((1, nf * d)), vec, vec],
        out_specs=[row, row, pl.BlockSpec((tm, nf * d), lambda i: (i, 0)), _const((1, 128)), vec, vec],
        out_shape=[sd((r, d), F32), sd((r, d), BF16), sd((r, nf * d), BF16), sd((1, 128), F32), sd((1, d), F32),
                   sd((1, d), F32)],
        compiler_params=_cp(("arbitrary",), 56),
    )(r1, tgt, g1, b1, wup_g, wdn_bf, bup, g2, b2)


def _mlp_bwd(act, dr2, r1, g1, wup_g, wdn_bf, lp):
    r, d = dr2.shape
    tm = _row_tile(lp, 384)
    nf = wup_g.shape[0]

    def body(act_ref, dr2_ref, r1_ref, g1_ref, wup_ref, wdn_ref, dr1_ref, dup_ref, dbup_ref, dg1_ref, db1_ref):
        i = pl.program_id(0)

        @pl.when(i == 0)
        def _():
            dbup_ref[...] = jnp.zeros_like(dbup_ref)
            dg1_ref[...] = jnp.zeros_like(dg1_ref)
            db1_ref[...] = jnp.zeros_like(db1_ref)

        dr2 = dr2_ref[...]
        dr2b = _bf(dr2)
        acc = ALPHA * dr2
        for s in range(nf):
            dact = _dot(dr2b, wdn_ref[s * d:(s + 1) * d, :], NT)
            dup = dact * (2.0 * jnp.sqrt(act_ref[:, s * d:(s + 1) * d].astype(F32)))
            dbup_ref[:, s * d:(s + 1) * d] += _colsum(dup)
            dupb = _bf(dup)
            dup_ref[:, s * d:(s + 1) * d] = dupb
            acc = acc + _dot(dupb, wup_ref[s], NT)
        g1 = g1_ref[...]
        _, xhat1, rstd1 = _ln_fwd(r1_ref[...], g1, 0.0)
        dr1_ref[...] = _ln_bwd(acc, xhat1, rstd1, g1)
        dg1_ref[...] += _colsum(acc * xhat1)
        db1_ref[...] += _colsum(acc)

    sd = jax.ShapeDtypeStruct
    row = pl.BlockSpec((tm, d), lambda i: (i, 0))
    vec = _const((1, d))
    return _pcall(
        body, name="mlp_bwd", grid=(r // tm,),
        in_specs=[pl.BlockSpec((tm, nf * d), lambda i: (i, 0)), row, row, vec, _resident(wup_g.shape),
                  _resident(wdn_bf.shape)],
        out_specs=[row, pl.BlockSpec((tm, nf * d), lambda i: (i, 0)), _const((1, nf * d)), vec, vec],
        out_shape=[sd((r, d), F32), sd((r, nf * d), BF16), sd((1, nf * d), F32), sd((1, d), F32), sd((1, d), F32)],
        compiler_params=_cp(("arbitrary",), 56),
    )(act, dr2, r1, g1, wup_g, wdn_bf)


def _s5_block_mats(bb_re_t, bb_im_t, c_re, c_im, ap_re, ap_im):
    ng = c_re.shape[0]
    gl = ng // S5_KCH
    eye = jnp.eye(gl, dtype=F32)

    def bmat(bt):
        bb = jnp.transpose(bt, (1, 0, 2)).reshape(S5_KCH, gl, S5_GROUP, S5_STATE)
        return jnp.einsum("kghp,gj->kghjp", bb, eye).reshape(S5_KCH, gl * S5_GROUP, gl * S5_STATE)

    def cmat(c):
        cc = c.reshape(S5_KCH, gl, S5_GROUP, S5_STATE)
        return jnp.einsum("kghp,gj->kjpgh", cc, eye).reshape(S5_KCH, gl * S5_STATE, gl * S5_GROUP)

    def pw(a):
        return jnp.transpose(a.reshape(8, S5_KCH, gl * S5_STATE), (1, 0, 2))

    bk = jnp.concatenate([bmat(bb_re_t), bmat(bb_im_t)], axis=-1)
    apow = jnp.concatenate([pw(ap_re), pw(ap_im)], axis=-1)
    return _bf(bk), _bf(cmat(c_re)), _bf(cmat(c_im)), apow


def _s5_block_grads(dbk, dcre, dcim, da):
    gl = dbk.shape[1] // S5_GROUP
    ng = gl * S5_KCH
    eye = jnp.eye(gl, dtype=F32)
    hw = gl * S5_STATE

    def bpart(x):
        x = x.reshape(S5_KCH, gl, S5_GROUP, gl, S5_STATE)
        x = jnp.einsum("kghjp,gj->kghp", x, eye).reshape(ng, S5_GROUP, S5_STATE)
        return jnp.transpose(x, (1, 0, 2))

    def cpart(x):
        x = x.reshape(S5_KCH, gl, S5_STATE, gl, S5_GROUP)
        return jnp.einsum("kjpgh,gj->kghp", x, eye).reshape(ng, S5_GROUP, S5_STATE)

    return (bpart(dbk[..., :hw]), bpart(dbk[..., hw:]), cpart(dcre), cpart(dcim),
            da[:, 0, :hw].reshape(ng, S5_STATE), da[:, 0, hw:].reshape(ng, S5_STATE))


def _tie(a, tok):
    return a if tok is None else a + tok[0, 0]


def _local_step(x, tgt, w, early=None, late=None, ready=None):
    ready = ready or (lambda names, g: None)
    bsz, seq, d = x.shape
    lp = PAD + N_META + seq
    r = bsz * lp
    tgtp = jnp.concatenate([jnp.zeros((bsz, PAD + N_META, d), F32), tgt], axis=1).reshape(r, d)

    h0, h0b = _ln0_fwd(x, w["meta_tokens"], w["ln0_g"], w["ln0_b"])
    b_re_t = jnp.transpose(w["s5_b_re"], (2, 0, 1))
    b_im_t = jnp.transpose(w["s5_b_im"], (2, 0, 1))
    ap_re, ap_im, bb_re_t, bb_im_t = _s5_prep(w["s5_lambda_re"], w["s5_lambda_im"], w["s5_log_dt"], b_re_t, b_im_t)
    bk, cre, cim, apow = _s5_block_mats(bb_re_t, bb_im_t, w["s5_c_re"], w["s5_c_im"], ap_re, ap_im)
    apow_rev = jnp.flip(apow, axis=1)
    if early is not None:
        w = {**w, **early((h0, tgtp, bk, cre, cim, apow_rev))}
    p, pg = _inproj(h0b, w["w_in"], w["b_in"], lp)
    p3 = p.reshape(bsz, lp, NP)
    pg3 = pg.reshape(bsz, lp, 128)

    y_s5, xs = _s5_fwd(p3, bk, cre, cim, apow, w["s5_d"])
    sw = y_s5.shape[-1]
    if late is not None:
        w = {**w, **late(y_s5)}
    gy, z, ys5g = _glu_fwd(y_s5.reshape(r, sw), w["s5_w_glu"], lp)

    pre3, qk3 = _conv_fwd(p3, w["qk_conv_w"], w["qk_conv_b"])
    hm3, cs, ns, ms = _mlstm_fwd(qk3, p3, pg3)
    hm = hm3.reshape(r, d)
    ymin, mix, r1 = _mix_fwd(hm, p, ys5g, h0, w["m_norm_g"], w["m_w_out"], w["w_o"], lp)
    dr2, h1b, act, loss, dg2, db2 = _mlp_fwd(r1, tgtp, w["ln1_g"], w["ln1_b"], w["w_up"], w["w_down"], w["b_up"],
                                             w["ln2_g"], w["ln2_b"], lp)

    g = {"ln2_g": dg2, "ln2_b": db2}
    dr1, dup, g["b_up"], g["ln1_g"], g["ln1_b"] = _mlp_bwd(act, dr2, r1, w["ln1_g"], w["w_up"], w["w_down"], lp)
    g["w_down"] = _mm_tn(act, dr2, name="dw_down")
    g["w_up"] = _mm_tn(h1b, dup, name="dw_up", split=w["w_up"].shape[0])
    tok = ready(("w_down", "w_up"), g)
    dp, dys5g, dym, dhm, g["m_norm_g"] = _mix_bwd(
        dr1, w["w_o"], w["m_w_out"], p, ys5g, ymin, hm, _tie(w["m_norm_g"], tok), lp)
    g["w_o"] = _mm_tn(mix, dr1, name="dw_o")
    g["m_w_out"] = _mm_tn(ymin, dym, name="dw_mout")

    dp3 = dp.reshape(bsz, lp, NP)
    dp3, dqk3, dgate = _mlstm_bwd(dp3, qk3, p3, pg3, dhm.reshape(bsz, lp, d), cs, ns, ms)
    dp3, g["qk_conv_w"], g["qk_conv_b"] = _conv_bwd(dp3, p3, dqk3, pre3, w["qk_conv_w"])
    dz, dys5 = _glu_bwd(dys5g, z, y_s5.reshape(r, sw), w["s5_w_glu"], lp)
    g["s5_w_glu"] = _mm_tn(gy, dz, name="dw_glu", split=w["s5_w_glu"].shape[0])
    tok = ready(("s5_w_glu", "m_w_out", "w_o"), g)
    dp3, dbk, dcre, dcim, da, g["s5_d"] = _s5_bwd(dp3, p3, dys5.reshape(bsz, lp, sw), xs, bk, cre, cim, apow_rev,
                                                 _tie(w["s5_d"], tok))
    dbb_re_t, dbb_im_t, g["s5_c_re"], g["s5_c_im"], da_re, da_im = _s5_block_grads(dbk, dcre, dcim, da)
    g["s5_lambda_re"], g["s5_lambda_im"], g["s5_log_dt"], gb_re_t, gb_im_t = _s5_prep_bwd(
        w["s5_lambda_re"], w["s5_lambda_im"], w["s5_log_dt"], b_re_t, b_im_t, da_re, da_im, dbb_re_t, dbb_im_t)
    g["s5_b_re"] = jnp.transpose(gb_re_t, (1, 2, 0))
    g["s5_b_im"] = jnp.transpose(gb_im_t, (1, 2, 0))

    dp3 = lax.dynamic_update_slice(dp3, dgate, (0, 0, G_OFF))
    dp = dp3.reshape(r, NP)
    g["w_in"], g["b_in"] = _mm_tn(h0b, dp, name="dw_in", colsum=True)
    tok = ready(("w_in",), g)
    dpw = _mm_nt(dp, w["w_in"], lp, name="dh0", dep=tok)
    grad_x, g["ln0_g"], g["ln0_b"], g["meta_tokens"] = _ln0_bwd(x, w["meta_tokens"], dr1, dpw, w["ln0_g"])
    return loss, grad_x, g


_ANY = pl.BlockSpec(memory_space=pl.ANY)
_MESH = pl.DeviceIdType.MESH


def _place():
    return lax.axis_index("x"), lax.axis_index("y"), lax.axis_index("c")


def _gather_chips(shards):
    n = len(shards)

    def body(*refs):
        ins, outs = refs[:n], refs[n:2 * n]
        send, recv, loc = refs[2 * n:]
        x, y, c = _place()
        me = 2 * x + y
        peers = [(1 - x, y), (x, 1 - y), (1 - x, 1 - y)]

        def rc(a, k, slot):
            px, py = peers[k]
            return pltpu.make_async_remote_copy(src_ref=ins[a], dst_ref=outs[a].at[slot], send_sem=send.at[a, k],
                                                recv_sem=recv.at[a, k], device_id=(px, py, c), device_id_type=_MESH)

        own = [pltpu.make_async_copy(ins[a], outs[a].at[me], loc.at[a]) for a in range(n)]
        for cp in own:
            cp.start()
        out = [rc(a, k, me) for a in range(n) for k in range(3)]
        for cp in out:
            cp.start()
        for a in range(n):
            for k in range(3):
                rc(a, k, 2 * peers[k][0] + peers[k][1]).wait_recv()
        for cp in out:
            cp.wait_send()
        for cp in own:
            cp.wait()

    return _pcall(
        body, name="gather_chips", in_specs=[_ANY] * n, out_specs=[_ANY] * n,
        out_shape=[jax.ShapeDtypeStruct((4,) + s.shape, s.dtype) for s in shards],
        scratch_shapes=[pltpu.SemaphoreType.DMA((n, 3)), pltpu.SemaphoreType.DMA((n, 3)), pltpu.SemaphoreType.DMA((n,))],
    )(*shards)


_HBM = pl.BlockSpec(memory_space=pltpu.HBM)
_SEM = pl.BlockSpec(memory_space=pltpu.SEMAPHORE)
_EFFECT = pltpu.SideEffectType.DATAFLOW_SIDE_EFFECTING


def _xchg_copies(srcs, lands, send, recv, scatter):
    x, y, c = _place()
    me = 2 * x + y
    if scatter == "pair":
        return [pltpu.make_async_remote_copy(src_ref=srcs[a], dst_ref=lands[a], send_sem=send.at[3 * a],
                                             recv_sem=recv.at[3 * a], device_id=(x, y, 1 - c), device_id_type=_MESH)
                for a in range(len(srcs))]
    peers = [(1 - x, y), (x, 1 - y), (1 - x, 1 - y)]
    out = []
    for a in range(len(srcs)):
        for k, (px, py) in enumerate(peers):
            src = srcs[a].at[2 * px + py] if scatter else srcs[a]
            dst = lands[a].at[k] if scatter else lands[a].at[me]
            out.append(pltpu.make_async_remote_copy(src_ref=src, dst_ref=dst, send_sem=send.at[3 * a + k],
                                                    recv_sem=recv.at[3 * a + k], device_id=(px, py, c),
                                                    device_id_type=_MESH))
    return out


def _xchg_start(srcs, lands, *, name, scatter, dep=None):
    n = len(srcs)
    deps = [] if dep is None else [dep]
    nd = len(deps)

    def body(*refs):
        send, recv = refs[2 * n + nd], refs[2 * n + nd + 1]
        for cp in _xchg_copies(refs[:n], refs[n:2 * n], send, recv, scatter):
            cp.start()
        refs[-1][...] = jnp.zeros_like(refs[-1])

    hbm = lambda a: pltpu.HBM(a.shape, a.dtype)
    con = lambda a: pltpu.with_memory_space_constraint(a, pltpu.HBM)
    res = _pcall(
        body, name=name, in_specs=[_HBM] * (2 * n) + [_ANY] * nd,
        out_specs=[_SEM, _SEM] + [_HBM] * (2 * n) + [pl.BlockSpec(memory_space=pltpu.VMEM)],
        out_shape=[pltpu.SemaphoreType.DMA((3 * n,)), pltpu.SemaphoreType.DMA((3 * n,))]
        + [hbm(a) for a in srcs] + [hbm(a) for a in lands] + [jax.ShapeDtypeStruct((8, 128), F32)],
        input_output_aliases={i: 2 + i for i in range(2 * n)},
        compiler_params=pltpu.CompilerParams(has_side_effects=_EFFECT),
    )(*[con(a) for a in srcs], *[con(a) for a in lands], *deps)
    return res[0], res[1], list(res[2:2 + n]), list(res[2 + n:2 + 2 * n]), res[-1]


def _xchg_wait(send, recv, srcs, lands, after, *, name, scatter):
    n = len(srcs)
    afters = list(after) if isinstance(after, (list, tuple)) else [after]

    def body(*refs):
        s_ref, r_ref = refs[2 * n], refs[2 * n + 1]
        for cp in _xchg_copies(refs[:n], refs[n:2 * n], s_ref, r_ref, scatter):
            cp.wait_send()
            cp.wait_recv()

    hbm = lambda a: pltpu.HBM(a.shape, a.dtype)
    res = _pcall(
        body, name=name, in_specs=[_HBM] * (2 * n) + [_SEM, _SEM] + [_ANY] * len(afters),
        out_specs=[_HBM] * (2 * n),
        out_shape=[hbm(a) for a in srcs] + [hbm(a) for a in lands],
        input_output_aliases={i: i for i in range(2 * n)},
        compiler_params=pltpu.CompilerParams(has_side_effects=_EFFECT),
    )(*srcs, *lands, send, recv, *afters)
    return list(res[:n]), list(res[n:])


def _swap_cores(arrs, name="swap_cores"):
    n = len(arrs)

    def body(*refs):
        ins, outs = refs[:n], refs[n:2 * n]
        send, recv = refs[2 * n:]
        x, y, c = _place()
        cps = [pltpu.make_async_remote_copy(src_ref=ins[a], dst_ref=outs[a], send_sem=send.at[a], recv_sem=recv.at[a],
                                            device_id=(x, y, 1 - c), device_id_type=_MESH) for a in range(n)]
        for cp in cps:
            cp.start()
        for cp in cps:
            cp.wait_recv()
        for cp in cps:
            cp.wait_send()

    return _pcall(
        body, name=name, in_specs=[_ANY] * n, out_specs=[_ANY] * n,
        out_shape=[jax.ShapeDtypeStruct(s.shape, s.dtype) for s in arrs],
        scratch_shapes=[pltpu.SemaphoreType.DMA((n,)), pltpu.SemaphoreType.DMA((n,))],
    )(*arrs)


def _allreduce_small(v, dep=None):
    rows = v.shape[0]
    half = rows // 2
    assert half % 8 == 0 and 2 * half == rows
    deps = [] if dep is None else [dep]

    def body(v_ref, *rest):
        out_ref, sib_ref, pair_ref, slots_ref, send, recv = rest[len(deps):]
        x, y, c = _place()
        chip = 2 * x + y
        sibling = (x, y, 1 - c)
        peers = [(1 - x, y), (x, 1 - y), (1 - x, 1 - y)]
        mine = pl.ds(pl.multiple_of(c * half, 8), half)

        first = pltpu.make_async_remote_copy(src_ref=v_ref, dst_ref=sib_ref, send_sem=send.at[0], recv_sem=recv.at[0],
                                             device_id=sibling, device_id_type=_MESH)
        first.start()
        first.wait_recv()
        pair_ref[...] = v_ref[...] + sib_ref[...]
        slots_ref[chip] = pair_ref[mine, :]
        cross = [pltpu.make_async_remote_copy(src_ref=pair_ref.at[mine], dst_ref=slots_ref.at[chip],
                                              send_sem=send.at[1 + k], recv_sem=recv.at[1 + k],
                                              device_id=(px, py, c), device_id_type=_MESH)
                 for k, (px, py) in enumerate(peers)]
        for cp in cross:
            cp.start()
        for cp in cross:
            cp.wait_recv()
        out_ref[mine, :] = ((slots_ref[0] + slots_ref[1]) + slots_ref[2]) + slots_ref[3]
        last = pltpu.make_async_remote_copy(src_ref=out_ref.at[mine], dst_ref=out_ref.at[mine], send_sem=send.at[4],
                                            recv_sem=recv.at[4], device_id=sibling, device_id_type=_MESH)
        last.start()
        last.wait_recv()
        first.wait_send()
        for cp in cross:
            cp.wait_send()
        last.wait_send()

    vm = pl.BlockSpec(memory_space=pltpu.VMEM)
    return _pcall(
        body, name="allreduce_small", in_specs=[vm] + [_ANY] * len(deps), out_specs=vm,
        out_shape=jax.ShapeDtypeStruct((rows, 128), F32),
        scratch_shapes=[pltpu.VMEM((rows, 128), F32), pltpu.VMEM((rows, 128), F32), pltpu.VMEM((4, half, 128), F32),
                        pltpu.SemaphoreType.DMA((5,)), pltpu.SemaphoreType.DMA((5,))],
        compiler_params=_cp(None, 40),
    )(v, *deps)


def _sum_slots(parts, land, chip):
    ns, rows, cols = land.shape
    tm = _row_tile(rows, 256, 8)

    def body(chip_ref, own_ref, a_ref, o_ref):
        o_ref[...] = ((own_ref[...] + a_ref[0]) + a_ref[1]) + a_ref[2]

    return _pcall(
        body, name="sum_slots",
        grid_spec=pltpu.PrefetchScalarGridSpec(
            num_scalar_prefetch=1, grid=(rows // tm,),
            in_specs=[pl.BlockSpec((None, tm, cols), lambda i, c: (c[0], i, 0)),
                      pl.BlockSpec((ns, tm, cols), lambda i, c: (0, i, 0))],
            out_specs=pl.BlockSpec((tm, cols), lambda i, c: (i, 0))),
        out_shape=jax.ShapeDtypeStruct((rows, cols), F32),
        compiler_params=_cp(("parallel",), 40),
    )(jnp.reshape(chip, (1,)).astype(jnp.int32), parts, land)


def _adamw(w, m, v, g0, g1=None):
    rows, cols = w.shape[-2:]
    lead = w.ndim == 3
    tm = _row_tile(rows, max(8, (1 << 20) // (4 * cols)), 8)
    c1 = 1.0 - ADAM_B1 ** ADAM_STEP
    c2 = 1.0 - ADAM_B2 ** ADAM_STEP
    two = g1 is not None

    def body(*refs):
        w_ref, m_ref, v_ref, g0_ref = refs[:4]
        g_ref, d_ref, nm_ref, nv_ref = refs[-4:]
        g = g0_ref[...]
        if two:
            g = g + refs[4][...]
        nm = ADAM_B1 * m_ref[...] + (1.0 - ADAM_B1) * g
        nv = ADAM_B2 * v_ref[...] + (1.0 - ADAM_B2) * (g * g)
        g_ref[...] = g
        nm_ref[...] = nm
        nv_ref[...] = nv
        d_ref[...] = -ADAM_LR * ((nm / c1) / (jnp.sqrt(nv / c2) + ADAM_EPS) + ADAM_WD * w_ref[...])

    blk = pl.BlockSpec((tm, cols), lambda i: (i, 0))
    wblk = pl.BlockSpec((None, tm, cols), lambda i: (0, i, 0)) if lead else blk
    ins = [w, m, v, g0] + ([g1] if two else [])
    return _pcall(
        body, name="adamw", grid=(rows // tm,), in_specs=[wblk] * 3 + [blk] * (len(ins) - 3), out_specs=[wblk] * 4,
        out_shape=[jax.ShapeDtypeStruct(w.shape, F32)] * 4,
        compiler_params=_cp(("parallel",), 40),
    )(*ins)


_BIG = ("w_in", "s5_w_glu", "m_w_out", "w_o", "w_up", "w_down")
_SMALL = ("ln0_g", "ln0_b", "b_in", "qk_conv_b", "s5_lambda_re", "s5_lambda_im", "s5_log_dt", "s5_b_re", "s5_b_im",
          "s5_c_re", "s5_c_im", "s5_d", "m_norm_g", "ln1_g", "ln1_b", "b_up", "ln2_g", "ln2_b")
_SMALL_SHARDED = ("meta_tokens", "qk_conv_w")
_ORDER = ("meta_tokens", "ln0_g", "ln0_b", "w_in", "b_in", "qk_conv_w", "qk_conv_b", "s5_lambda_re", "s5_lambda_im",
          "s5_log_dt", "s5_b_re", "s5_b_im", "s5_c_re", "s5_c_im", "s5_d", "s5_w_glu", "m_norm_g", "m_w_out", "w_o",
          "ln1_g", "ln1_b", "w_up", "b_up", "w_down", "ln2_g", "ln2_b")


def _pack(arrs):
    flat = jnp.concatenate([a.reshape(-1) for a in arrs])
    n = flat.shape[0]
    rows = -(-n // 2048) * 16
    return jnp.pad(flat, (0, rows * 128 - n)).reshape(rows, 128)


def _unpack(packed, shapes):
    flat = packed.reshape(-1)
    out, off = [], 0
    for s in shapes:
        n = math.prod(s)
        out.append(flat[off:off + n].reshape(s))
        off += n
    return out


def kernel(x, meta_tokens, ln0_g, ln0_b, w_in, b_in, qk_conv_w, qk_conv_b, s5_lambda_re, s5_lambda_im, s5_log_dt, s5_b_re, s5_b_im, s5_c_re, s5_c_im, s5_d, s5_w_glu, m_norm_g, m_w_out, w_o, ln1_g, ln1_b, w_up, b_up, w_down, ln2_g, ln2_b, loss_target, m_meta_tokens, m_ln0_g, m_ln0_b, m_w_in, m_b_in, m_qk_conv_w, m_qk_conv_b, m_s5_lambda_re, m_s5_lambda_im, m_s5_log_dt, m_s5_b_re, m_s5_b_im, m_s5_c_re, m_s5_c_im, m_s5_d, m_s5_w_glu, m_m_norm_g, m_m_w_out, m_w_o, m_ln1_g, m_ln1_b, m_w_up, m_b_up, m_w_down, m_ln2_g, m_ln2_b, v_meta_tokens, v_ln0_g, v_ln0_b, v_w_in, v_b_in, v_qk_conv_w, v_qk_conv_b, v_s5_lambda_re, v_s5_lambda_im, v_s5_log_dt, v_s5_b_re, v_s5_b_im, v_s5_c_re, v_s5_c_im, v_s5_d, v_s5_w_glu, v_m_norm_g, v_m_w_out, v_w_o, v_ln1_g, v_ln1_b, v_w_up, v_b_up, v_w_down, v_ln2_g, v_ln2_b):
    wts = dict(meta_tokens=meta_tokens, ln0_g=ln0_g, ln0_b=ln0_b, w_in=w_in, b_in=b_in, qk_conv_w=qk_conv_w,
               qk_conv_b=qk_conv_b, s5_lambda_re=s5_lambda_re, s5_lambda_im=s5_lambda_im, s5_log_dt=s5_log_dt,
               s5_b_re=s5_b_re, s5_b_im=s5_b_im, s5_c_re=s5_c_re, s5_c_im=s5_c_im, s5_d=s5_d, s5_w_glu=s5_w_glu,
               m_norm_g=m_norm_g, m_w_out=m_w_out, w_o=w_o, ln1_g=ln1_g, ln1_b=ln1_b, w_up=w_up, b_up=b_up,
               w_down=w_down, ln2_g=ln2_g, ln2_b=ln2_b)
    mom = dict(meta_tokens=m_meta_tokens, ln0_g=m_ln0_g, ln0_b=m_ln0_b, w_in=m_w_in, b_in=m_b_in, qk_conv_w=m_qk_conv_w,
               qk_conv_b=m_qk_conv_b, s5_lambda_re=m_s5_lambda_re, s5_lambda_im=m_s5_lambda_im, s5_log_dt=m_s5_log_dt,
               s5_b_re=m_s5_b_re, s5_b_im=m_s5_b_im, s5_c_re=m_s5_c_re, s5_c_im=m_s5_c_im, s5_d=m_s5_d,
               s5_w_glu=m_s5_w_glu, m_norm_g=m_m_norm_g, m_w_out=m_m_w_out, w_o=m_w_o, ln1_g=m_ln1_g, ln1_b=m_ln1_b,
               w_up=m_w_up, b_up=m_b_up, w_down=m_w_down, ln2_g=m_ln2_g, ln2_b=m_ln2_b)
    var = dict(meta_tokens=v_meta_tokens, ln0_g=v_ln0_g, ln0_b=v_ln0_b, w_in=v_w_in, b_in=v_b_in, qk_conv_w=v_qk_conv_w,
               qk_conv_b=v_qk_conv_b, s5_lambda_re=v_s5_lambda_re, s5_lambda_im=v_s5_lambda_im, s5_log_dt=v_s5_log_dt,
               s5_b_re=v_s5_b_re, s5_b_im=v_s5_b_im, s5_c_re=v_s5_c_re, s5_c_im=v_s5_c_im, s5_d=v_s5_d,
               s5_w_glu=v_s5_w_glu, m_norm_g=v_m_norm_g, m_w_out=v_m_w_out, w_o=v_w_o, ln1_g=v_ln1_g, ln1_b=v_ln1_b,
               w_up=v_w_up, b_up=v_b_up, w_down=v_w_down, ln2_g=v_ln2_g, ln2_b=v_ln2_b)
    d = x.shape[-1]
    chip = 2 * lax.axis_index("x") + lax.axis_index("y")

    gw = dict(zip(_SMALL_SHARDED, _gather_chips([meta_tokens, qk_conv_w[0]])))
    core = lax.axis_index("c")
    half_rows = w_in.shape[1] // 2
    own_w_in = lax.dynamic_slice_in_dim(_bf(w_in[0]), core * half_rows, half_rows, axis=0)
    fsend, frecv, fsrc, fland, ftok = _xchg_start([own_w_in], [lax.empty((4,) + own_w_in.shape, BF16)],
                                                  name="gather_w_in_start", scatter=False, dep=gw["qk_conv_w"])
    late_names = tuple(n for n in _BIG if n != "w_in")
    cat = lambda a: jnp.transpose(a, (1, 0, 2)).reshape(a.shape[1], 4 * a.shape[2])
    w = dict(
        meta_tokens=cat(gw["meta_tokens"]), ln0_g=ln0_g[None], ln0_b=_tie(ln0_b[None], ftok),
        qk_conv_w=cat(gw["qk_conv_w"]), qk_conv_b=qk_conv_b,
        s5_lambda_re=s5_lambda_re[0], s5_lambda_im=s5_lambda_im[0], s5_log_dt=s5_log_dt[0][:, None],
        s5_b_re=s5_b_re[0], s5_b_im=s5_b_im[0], s5_c_re=s5_c_re[0], s5_c_im=s5_c_im[0], s5_d=s5_d,
        m_norm_g=m_norm_g, ln1_g=ln1_g, ln1_b=ln1_b, b_up=b_up, ln2_g=ln2_g, ln2_b=ln2_b)
    in_flight = {}

    def place_own(src, land):
        return lax.dynamic_update_slice(land, src[None], (chip,) + (0,) * src.ndim)

    small_names = _SMALL + _SMALL_SHARDED

    def view(n, a):
        return jnp.swapaxes(a, -1, -2) if n in ("s5_b_re", "s5_b_im") else a

    small_wmv = [_pack([view(n, dct[n]) for n in small_names]) for dct in (wts, mom, var)]

    def early(after):
        src, land = _xchg_wait(fsend, frecv, fsrc, fland, tuple(after) + tuple(small_wmv), name="gather_w_in_wait",
                               scatter=False)
        late_src = [_bf(wts[n][0]) for n in late_names]
        st = _xchg_start(late_src, [lax.empty((4,) + a.shape, a.dtype) for a in late_src], name="gather_late_start",
                         scatter=False, dep=src[0])
        in_flight["late"] = st[:4]
        mine = place_own(src[0], land[0])
        theirs = _swap_cores([mine], name="swap_w_in")[0]
        top, bottom = jnp.where(core == 0, mine, theirs), jnp.where(core == 0, theirs, mine)
        full = jnp.concatenate([top, bottom], axis=1)
        return dict(w_in=_w_in_from_slots(full, IN_CHUNK), b_in=_tie(_to_pad_cols(b_in), st[4]))

    def late(after):
        src, land = _xchg_wait(*in_flight["late"], after, name="gather_late_wait", scatter=False)
        full = {n: place_own(s, ld) for n, s, ld in zip(late_names, src, land)}
        return dict(s5_w_glu=full["s5_w_glu"], m_w_out=full["m_w_out"].reshape(d, d), w_o=full["w_o"].reshape(d, d),
                    w_up=full["w_up"], w_down=full["w_down"].reshape(4 * d, d))

    flying = []

    def ready(names, g):
        parts = dict(
            w_in=lambda: _slots_from_w_in(g["w_in"][0]), s5_w_glu=lambda: g["s5_w_glu"],
            m_w_out=lambda: g["m_w_out"].reshape(4, d // 4, d), w_o=lambda: g["w_o"].reshape(4, d // 4, d),
            w_up=lambda: g["w_up"], w_down=lambda: g["w_down"].reshape(4, d, d))
        src = [parts[n]() for n in names]
        land = [lax.empty((3,) + a.shape[1:], a.dtype) for a in src]
        st = _xchg_start(src, land, name="scatter_" + names[0] + "_start", scatter=True)
        flying.append((names,) + st[:4])
        return st[4]

    loss, grad_x, g = _local_step(x, loss_target, w, early, late, ready)
    g["b_in"] = _from_pad_cols(g["b_in"])

    res = {}

    def flat(a):
        return jnp.swapaxes(a, -1, -2).reshape(a.shape[:-2] + (-1, 128))

    def unflat(y, shape):
        return jnp.swapaxes(y.reshape(shape[:-2] + (shape[-1], shape[-2])), -1, -2)

    def finish(groups, after, tag):
        mine = {}
        for names, send, recv, src, land in groups:
            src, land = _xchg_wait(send, recv, src, land, after, name="scatter_" + names[0] + "_wait", scatter=True)
            for n, s, ld in zip(names, src, land):
                mine[n] = _sum_slots(s, ld, chip)
        theirs = _swap_cores(list(mine.values()), name="swap_cores_" + tag)
        for n, t in zip(mine, theirs):
            if n == "w_in":
                res[n] = [unflat(r, wts[n].shape) for r in _adamw(flat(wts[n]), flat(mom[n]), flat(var[n]),
                                                                  flat(mine[n]), flat(t))]
            else:
                res[n] = _adamw(wts[n], mom[n], var[n], mine[n], t)

    a_mine = {}
    for names, send, recv, src, land in flying[:-1]:
        src, land = _xchg_wait(send, recv, src, land, g["ln0_g"], name="scatter_" + names[0] + "_wait", scatter=True)
        for n, s, ld in zip(names, src, land):
            a_mine[n] = _sum_slots(s, ld, chip)
    a_src = list(a_mine.values())
    a_st = _xchg_start(a_src, [lax.empty(a.shape, a.dtype) for a in a_src], name="swap_a_start", scatter="pair")

    small_shapes = [(1, 128)] + [view(n, wts[n]).shape for n in _SMALL] + [g[n].shape for n in _SMALL_SHARDED]
    packed = _pack([loss] + [view(n, g[n]) for n in _SMALL] + [g[n] for n in _SMALL_SHARDED])
    tot = _unpack(_allreduce_small(packed, dep=a_st[4]), small_shapes)
    loss_out = tot[0][0, 0]
    gsm = dict(zip(_SMALL + _SMALL_SHARDED, tot[1:]))
    for n in _SMALL_SHARDED:
        cols = wts[n].shape[-1]
        gsm[n] = lax.dynamic_slice_in_dim(gsm[n], chip * cols, cols, axis=1).reshape(wts[n].shape)

    names = small_names
    shapes = [view(n, wts[n]).shape for n in names]
    small_out = _adamw(*small_wmv, _pack([gsm[n] for n in names]))
    small_res = [_unpack(r, shapes) for r in small_out]
    for j, n in enumerate(names):
        res[n] = [view(n, small_res[q][j]) for q in range(4)]
    a_src, a_land = _xchg_wait(a_st[0], a_st[1], a_st[2], a_st[3], small_out[0], name="swap_a_wait", scatter="pair")
    for n, m_, t_ in zip(a_mine, a_src, a_land):
        res[n] = _adamw(wts[n], mom[n], var[n], m_, t_)
    finish(flying[-1:], small_out[0], "b")

    return (loss_out, grad_x, *[res[n][0] for n in _ORDER], *[res[n][1] for n in _ORDER],
            *[res[n][2] for n in _ORDER], *[res[n][3] for n in _ORDER])
```

```python
import functools
import math

import jax
import jax.numpy as jnp
from jax import lax
from jax.experimental import pallas as pl
from jax.experimental.pallas import tpu as pltpu

F32 = jnp.float32
BF16 = jnp.bfloat16
HI = lax.Precision.HIGHEST

N_META = 16
M_HEADS = 4
M_CHUNK = 128
PAD = M_CHUNK - N_META
CONV_W = 4
HALO_ROWS = 16
S5_GROUP = 16
S5_STATE = 64
S5_KCH = 4
LN_EPS = 1e-5
ALPHA = 2.0 ** 0.25
NEG = -1e30
ADAM_LR, ADAM_B1, ADAM_B2, ADAM_EPS, ADAM_WD, ADAM_STEP = 0.001, 0.9, 0.999, 1e-08, 0.01, 10

O_OFF, GS_OFF, GM_OFF, V_OFF, Q_OFF, K_OFF, U_OFF, G_OFF, NP = 0, 1024, 2048, 3072, 4096, 4608, 5120, 5632, 5760

NN = ((1,), (0,))
NT = ((1,), (1,))
TN = ((0,), (0,))


def _dot(a, b, dims=NN, prec=None):
    return lax.dot_general(a, b, (dims, ((), ())), preferred_element_type=F32, precision=prec)


def _bf(x):
    return x.astype(BF16)


def _sig(x):
    return 0.5 * jnp.tanh(0.5 * x) + 0.5


def _pcall(body, **kw):
    return pl.pallas_call(body, **kw)


def _cp(sem=None, vmem_mb=None):
    kw = {}
    if sem is not None:
        kw["dimension_semantics"] = sem
    if vmem_mb is not None:
        kw["vmem_limit_bytes"] = vmem_mb << 20
    return pltpu.CompilerParams(**kw)


def _row_tile(n, want, mult=16):
    best = None
    for t in range(mult, want + 1, mult):
        if n % t == 0:
            best = t
    assert best is not None, (n, want)
    return best


def _resident(shape):
    nd = len(shape)
    return pl.BlockSpec(shape, lambda *_: (0,) * nd, pipeline_mode=pl.Buffered(1))


def _const(shape):
    nd = len(shape)
    return pl.BlockSpec(shape, lambda *_: (0,) * nd)


def _ln_fwd(x, g, b):
    mu = jnp.mean(x, axis=-1, keepdims=True)
    xc = x - mu
    var = jnp.mean(xc * xc, axis=-1, keepdims=True)
    rstd = lax.rsqrt(var + LN_EPS)
    xhat = xc * rstd
    return xhat * g + b, xhat, rstd


def _ln_bwd(dy, xhat, rstd, g):
    dxh = dy * g
    m1 = jnp.mean(dxh, axis=-1, keepdims=True)
    m2 = jnp.mean(dxh * xhat, axis=-1, keepdims=True)
    return rstd * (dxh - m1 - xhat * m2)


def _colsum(x):
    return jnp.sum(x, axis=0, keepdims=True)


def _to_pad_cols(w):
    u, q, k, v, o, gi, gf, gs, gm = (w[..., 0:512], w[..., 512:1024], w[..., 1024:1536], w[..., 1536:2560],
                                     w[..., 2560:3584], w[..., 3584:3588], w[..., 3588:3592], w[..., 3592:4616],
                                     w[..., 4616:5640])
    z = jnp.zeros(w.shape[:-1] + (NP - G_OFF - 8,), w.dtype)
    return jnp.concatenate([o, gs, gm, v, q, k, u, gi, gf, z], axis=-1)


def _from_pad_cols(w):
    o, gs, gm, v, q, k, u = (w[..., O_OFF:GS_OFF], w[..., GS_OFF:GM_OFF], w[..., GM_OFF:V_OFF], w[..., V_OFF:Q_OFF],
                             w[..., Q_OFF:K_OFF], w[..., K_OFF:U_OFF], w[..., U_OFF:G_OFF])
    gi, gf = w[..., G_OFF:G_OFF + 4], w[..., G_OFF + 4:G_OFF + 8]
    return jnp.concatenate([u, q, k, v, o, gi, gf, gs, gm], axis=-1)


_IN_REF = (("u", 512), ("q", 512), ("k", 512), ("v", 1024), ("o", 1024), ("i", 4), ("f", 4), ("gs", 1024), ("gm", 1024))
_IN_PAD = (("o", O_OFF), ("gs", GS_OFF), ("gm", GM_OFF), ("v", V_OFF), ("q", Q_OFF), ("k", K_OFF), ("u", U_OFF),
           ("i", G_OFF), ("f", G_OFF + 4))


def _in_ref_ranges():
    out, off = {}, 0
    for n, s in _IN_REF:
        out[n] = (off, off + s)
        off += s
    return out, off


def _w_in_from_slots(g, chunk=None):
    rng, total = _in_ref_ranges()
    width = total // g.shape[0]
    cols = []
    for n, _ in _IN_PAD:
        a, b = rng[n]
        while a < b:
            s = a // width
            e = min(b, (s + 1) * width)
            cols.append(g[s][:, a - s * width:e - s * width])
            a = e
    cols.append(jnp.zeros((g.shape[1], NP - G_OFF - 8), g.dtype))
    if chunk is None:
        return jnp.concatenate(cols, axis=1)
    chunks, cur, room = [], [], chunk
    for c in cols:
        while c.shape[1] > 0:
            take = min(room, c.shape[1])
            cur.append(c[:, :take])
            c, room = c[:, take:], room - take
            if room == 0:
                chunks.append(jnp.concatenate(cur, axis=1))
                cur, room = [], chunk
    assert not cur
    return jnp.stack(chunks, axis=0)


def _slots_from_w_in(wp, nslot=4):
    rng, total = _in_ref_ranges()
    width = total // nslot
    pad_off = dict(_IN_PAD)
    slots = []
    for s in range(nslot):
        lo, hi = s * width, (s + 1) * width
        cols = []
        for n, _ in _IN_REF:
            a, b = rng[n]
            x0, x1 = max(a, lo), min(b, hi)
            if x0 < x1:
                cols.append(wp[:, pad_off[n] + x0 - a:pad_off[n] + x1 - a])
        slots.append(jnp.concatenate(cols, axis=1))
    return jnp.stack(slots, axis=0)


HEAD = PAD + N_META


def _ln0_in(j, x_ref, meta_ref):
    first = jnp.concatenate([jnp.zeros((PAD, meta_ref.shape[1]), F32), meta_ref[...]], axis=0)
    return jnp.where(j == 0, first[None], x_ref[...])


def _ln0_fwd(x, meta, g, b):
    bsz, seq, d = x.shape
    nb = seq // HEAD + 1

    def body(x_ref, m_ref, g_ref, b_ref, o_ref, ob_ref):
        y, _, _ = _ln_fwd(_ln0_in(pl.program_id(0), x_ref, m_ref), g_ref[...], b_ref[...])
        o_ref[...] = y
        ob_ref[...] = _bf(y)

    row = pl.BlockSpec((bsz, HEAD, d), lambda j: (0, j, 0))
    h0, h0b = _pcall(
        body, name="ln0_fwd", grid=(nb,),
        in_specs=[pl.BlockSpec((bsz, HEAD, d), lambda j: (0, jnp.maximum(j - 1, 0), 0)), _const((N_META, d)),
                  _const((1, d)), _const((1, d))],
        out_specs=[row, row],
        out_shape=[jax.ShapeDtypeStruct((bsz, nb * HEAD, d), F32), jax.ShapeDtypeStruct((bsz, nb * HEAD, d), BF16)],
        compiler_params=_cp(("arbitrary",)),
    )(x, meta, g, b)
    return h0.reshape(-1, d), h0b.reshape(-1, d)


def _ln0_bwd(x, meta, dr1, dpw, g):
    bsz, seq, d = x.shape
    nb = seq // HEAD + 1

    def body(x_ref, m_ref, a_ref, c_ref, g_ref, o_ref, dg_ref, db_ref, dm_ref):
        j = pl.program_id(0)

        @pl.when(j == 0)
        def _():
            dg_ref[...] = jnp.zeros_like(dg_ref)
            db_ref[...] = jnp.zeros_like(db_ref)
            dm_ref[...] = jnp.zeros_like(dm_ref)

        dy = ALPHA * a_ref[...] + c_ref[...]
        _, xhat, rstd = _ln_fwd(_ln0_in(j, x_ref, m_ref), g_ref[...], 0.0)
        dx = _ln_bwd(dy, xhat, rstd, g_ref[...])
        o_ref[...] = dx
        dg_ref[...] += _colsum((dy * xhat).reshape(bsz * HEAD, d))
        db_ref[...] += _colsum(dy.reshape(bsz * HEAD, d))

        @pl.when(j == 0)
        def _():
            dm_ref[...] += jnp.sum(dx[:, PAD:, :], axis=0)

    row = pl.BlockSpec((bsz, HEAD, d), lambda j: (0, j, 0))
    tok = pl.BlockSpec((bsz, HEAD, d), lambda j: (0, jnp.maximum(j - 1, 0), 0))
    lp = nb * HEAD
    return _pcall(
        body, name="ln0_bwd", grid=(nb,),
        in_specs=[tok, _const((N_META, d)), row, row, _const((1, d))],
        out_specs=[tok, _const((1, d)), _const((1, d)), _const((N_META, d))],
        out_shape=[jax.ShapeDtypeStruct((bsz, seq, d), F32), jax.ShapeDtypeStruct((1, d), F32),
                   jax.ShapeDtypeStruct((1, d), F32), jax.ShapeDtypeStruct((N_META, d), F32)],
        compiler_params=_cp(("arbitrary",)),
    )(x, meta, dr1.reshape(bsz, lp, d), dpw.reshape(bsz, lp, d), g)


IN_CHUNK = 1152


def _chunk_cols(w):
    k, n = w.shape
    return jnp.transpose(w.reshape(k, n // IN_CHUNK, IN_CHUNK), (1, 0, 2))


def _inproj(h0b, w3, bias, lp):
    r, d = h0b.shape
    nj, _, tn = w3.shape
    tm = _row_tile(lp, 2112)
    tps = lp // tm

    def body(a_ref, w_ref, b_ref, o_ref, gate_ref):
        i = pl.program_id(0)
        j = pl.program_id(1)
        acc = _dot(a_ref[...], w_ref[j]) + b_ref[...]
        t = (i % tps) * tm + lax.broadcasted_iota(jnp.int32, (tm, 1), 0)
        acc = jnp.where(t >= PAD, acc, 0.0)
        o_ref[...] = _bf(acc)

        @pl.when(j == nj - 1)
        def _():
            gate_ref[...] = acc[:, tn - 128:]

    return _pcall(
        body, name="inproj", grid=(r // tm, nj),
        in_specs=[pl.BlockSpec((tm, d), lambda i, j: (i, 0)), _resident(w3.shape),
                  pl.BlockSpec((1, tn), lambda i, j: (0, j))],
        out_specs=[pl.BlockSpec((tm, tn), lambda i, j: (i, j)), pl.BlockSpec((tm, 128), lambda i, j: (i, 0))],
        out_shape=[jax.ShapeDtypeStruct((r, nj * tn), BF16), jax.ShapeDtypeStruct((r, 128), F32)],
        compiler_params=_cp(("parallel", "arbitrary"), 48),
    )(h0b, w3, bias)


def _mm_tn(a, b, *, name, split=1, colsum=False, tk_want=2112):
    r, m = a.shape
    n = b.shape[1]
    tk = _row_tile(r, tk_want)
    tm = min(m, 1024)
    ns = n // split
    tn = ns
    for cand in (1024, 1152, 640, 512, 128):
        if ns % cand == 0 and cand <= ns:
            tn = cand
            break
    nb = ns // tn
    nk = r // tk

    def body(a_ref, b_ref, o_ref, *rest):
        acc = rest[-1]
        k = pl.program_id(2)

        @pl.when(k == 0)
        def _():
            acc[...] = jnp.zeros_like(acc)

        bt = b_ref[...]
        acc[...] += _dot(_bf(a_ref[...]), _bf(bt), TN)

        @pl.when(k == nk - 1)
        def _():
            o_ref[...] = acc[...]

        if colsum:
            cs_ref = rest[0]

            @pl.when(k == 0)
            def _():
                cs_ref[...] = jnp.zeros_like(cs_ref)

            cs_ref[...] += _colsum(bt.astype(F32))

    out_specs = [pl.BlockSpec((None, tm, tn), lambda i, j, k: (j // nb, i, j % nb))]
    out_shape = [jax.ShapeDtypeStruct((split, m, ns), F32)]
    if colsum:
        assert m == tm
        out_specs.append(pl.BlockSpec((1, tn), lambda i, j, k: (0, j)))
        out_shape.append(jax.ShapeDtypeStruct((1, n), F32))
    res = _pcall(
        body, name=name, grid=(m // tm, n // tn, nk),
        in_specs=[pl.BlockSpec((tk, tm), lambda i, j, k: (k, i)), pl.BlockSpec((tk, tn), lambda i, j, k: (k, j))],
        out_specs=out_specs, out_shape=out_shape,
        scratch_shapes=[pltpu.VMEM((tm, tn), F32)],
        compiler_params=_cp(("parallel", "parallel", "arbitrary"), 56),
    )(a, b)
    return res if colsum else res[0]


def _mm_nt(a, w3, lp, *, name, dep=None):
    r, kdim = a.shape
    nk, n, tk = w3.shape
    assert nk * tk == kdim
    tm = _row_tile(lp, 1056)
    deps = [] if dep is None else [dep]

    def body(a_ref, w_ref, *rest):
        o_ref, acc = rest[-2:]
        k = pl.program_id(1)

        @pl.when(k == 0)
        def _():
            acc[...] = jnp.zeros_like(acc)

        acc[...] += _dot(_bf(a_ref[...]), w_ref[k], NT)

        @pl.when(k == nk - 1)
        def _():
            o_ref[...] = acc[...]

    return _pcall(
        body, name=name, grid=(r // tm, nk),
        in_specs=[pl.BlockSpec((tm, tk), lambda i, k: (i, k)), _resident(w3.shape)]
        + [_const(dp_.shape) for dp_ in deps],
        out_specs=pl.BlockSpec((tm, n), lambda i, k: (i, 0)),
        out_shape=jax.ShapeDtypeStruct((r, n), F32),
        scratch_shapes=[pltpu.VMEM((tm, n), F32)],
        compiler_params=_cp(("parallel", "arbitrary"), 48),
    )(a, w3, *deps)


def _s5_prep(lam_re, lam_im, log_dt, b_re_t, b_im_t):
    g, p = lam_re.shape
    h = b_re_t.shape[0]

    def body(lr_ref, li_ref, ldt_ref, br_ref, bi_ref, pr_ref, pi_ref, bbr_ref, bbi_ref):
        lr, li = lr_ref[...], li_ref[...]
        dt = jnp.exp(ldt_ref[...])
        e = jnp.exp(lr * dt)
        ar, ai = e * jnp.cos(li * dt), e * jnp.sin(li * dt)
        den = lr * lr + li * li
        cr = ((ar - 1.0) * lr + ai * li) / den
        ci = (ai * lr - (ar - 1.0) * li) / den
        br, bi = br_ref[...], bi_ref[...]
        bbr_ref[...] = cr[None] * br - ci[None] * bi
        bbi_ref[...] = cr[None] * bi + ci[None] * br
        xr, xi = ar, ai
        pr_ref[0] = xr
        pi_ref[0] = xi
        for t in range(1, 8):
            xr, xi = xr * ar - xi * ai, xr * ai + xi * ar
            pr_ref[t] = xr
            pi_ref[t] = xi

    sd = jax.ShapeDtypeStruct
    return _pcall(body, name="s5_prep",
                  out_shape=[sd((8, g, p), F32), sd((8, g, p), F32), sd((h, g, p), F32), sd((h, g, p), F32)])(
        lam_re, lam_im, log_dt, b_re_t, b_im_t)


def _s5_prep_bwd(lam_re, lam_im, log_dt, b_re_t, b_im_t, da_re, da_im, dbb_re_t, dbb_im_t):
    g, p = lam_re.shape
    h = b_re_t.shape[0]

    def body(lr_ref, li_ref, ldt_ref, br_ref, bi_ref, dar_ref, dai_ref, dbr_ref, dbi_ref,
             glr_ref, gli_ref, gdt_ref, gbr_ref, gbi_ref):
        lr, li = lr_ref[...], li_ref[...]
        dt = jnp.exp(ldt_ref[...])
        e = jnp.exp(lr * dt)
        ar, ai = e * jnp.cos(li * dt), e * jnp.sin(li * dt)
        den = lr * lr + li * li
        cr = ((ar - 1.0) * lr + ai * li) / den
        ci = (ai * lr - (ar - 1.0) * li) / den
        br, bi = br_ref[...], bi_ref[...]
        gr, gi = dbr_ref[...], dbi_ref[...]
        gbr_ref[...] = gr * cr[None] + gi * ci[None]
        gbi_ref[...] = gi * cr[None] - gr * ci[None]
        gcr = jnp.sum(gr * br + gi * bi, axis=0)
        gci = jnp.sum(gi * br - gr * bi, axis=0)
        ilr, ili = lr / den, -li / den
        gar = dar_ref[...] + gcr * ilr + gci * ili
        gai = dai_ref[...] + gci * ilr - gcr * ili
        qr, qi = cr * ilr - ci * ili, cr * ili + ci * ilr
        glr = -(gcr * qr + gci * qi)
        gli = -(gci * qr - gcr * qi)
        gzr = gar * ar + gai * ai
        gzi = gai * ar - gar * ai
        glr_ref[...] = glr + gzr * dt
        gli_ref[...] = gli + gzi * dt
        gdt_ref[...] = jnp.sum(gzr * lr + gzi * li, axis=1, keepdims=True) * dt

    sd = jax.ShapeDtypeStruct
    return _pcall(body, name="s5_prep_bwd",
                  out_shape=[sd((g, p), F32), sd((g, p), F32), sd((g, 1), F32), sd((h, g, p), F32), sd((h, g, p), F32)])(
        lam_re, lam_im, log_dt, b_re_t, b_im_t, da_re, da_im, dbb_re_t, dbb_im_t)


def _cmul(xr, xi, yr, yi):
    return xr * yr - xi * yi, xr * yi + xi * yr


def _dot5(a, b, dims=NN):
    return _dot(_bf(a), _bf(b), dims)


def _s5_fwd(p3, bk, cre, cim, apow, dskip):
    bsz, lp, _ = p3.shape
    tt = _row_tile(lp, 528, 8)
    nt = lp // tt
    nblk = tt // 8
    hw = 512

    def body(u_ref, bk_ref, cre_ref, cim_ref, ap_ref, d_ref, y_ref, xs_ref, car_ref):
        t = pl.program_id(2)

        @pl.when(t == 0)
        def _():
            car_ref[...] = jnp.zeros_like(car_ref)

        u = u_ref[...].astype(F32)
        xs_ref[...] = _dot5(u, bk_ref[...])
        ap = ap_ref[...]
        apr, api = ap[:, :hw], ap[:, hw:]
        rows = lax.broadcasted_iota(jnp.int32, (8, hw), 0)
        lev = [(d, jnp.where(rows < d, 0.0, jnp.broadcast_to(apr[d - 1:d, :], (8, hw))),
                jnp.where(rows < d, 0.0, jnp.broadcast_to(api[d - 1:d, :], (8, hw)))) for d in (1, 2, 4)]

        def blk(i, carry):
            cr, ci = carry
            off = pl.multiple_of(i * 8, 8)
            x = xs_ref[pl.ds(off, 8), :]
            xr, xi = x[:, :hw], x[:, hw:]
            for d, lr, li in lev:
                mr, mi = _cmul(pltpu.roll(xr, d, 0), pltpu.roll(xi, d, 0), lr, li)
                xr, xi = xr + mr, xi + mi
            mr, mi = _cmul(apr, api, cr, ci)
            xr, xi = xr + mr, xi + mi
            xs_ref[pl.ds(off, 8), :] = jnp.concatenate([xr, xi], axis=1)
            return xr[7:8, :], xi[7:8, :]

        c0 = car_ref[...]
        cr, ci = lax.fori_loop(0, nblk, blk, (c0[0:1, :hw], c0[0:1, hw:]), unroll=2)
        car_ref[...] = jnp.broadcast_to(jnp.concatenate([cr, ci], axis=1), car_ref.shape)
        xs = xs_ref[...]
        y_ref[...] = (_dot5(xs[:, :hw], cre_ref[...]) - _dot5(xs[:, hw:], cim_ref[...])
                      + d_ref[...] * u)

    ub = U_OFF // 128
    return _pcall(
        body, name="s5_fwd", grid=(S5_KCH, bsz, nt),
        in_specs=[pl.BlockSpec((None, tt, 128), lambda k, b, t: (b, t, ub + k)),
                  pl.BlockSpec((None, 128, 2 * hw), lambda k, b, t: (k, 0, 0)),
                  pl.BlockSpec((None, hw, 128), lambda k, b, t: (k, 0, 0)),
                  pl.BlockSpec((None, hw, 128), lambda k, b, t: (k, 0, 0)),
                  pl.BlockSpec((None, 8, 2 * hw), lambda k, b, t: (k, 0, 0)),
                  pl.BlockSpec((1, 128), lambda k, b, t: (0, k))],
        out_specs=[pl.BlockSpec((None, tt, 128), lambda k, b, t: (b, t, k)),
                   pl.BlockSpec((None, None, tt, 2 * hw), lambda k, b, t: (b, k, t, 0))],
        out_shape=[jax.ShapeDtypeStruct((bsz, lp, S5_KCH * 128), F32),
                   jax.ShapeDtypeStruct((bsz, S5_KCH, lp, 2 * hw), F32)],
        scratch_shapes=[pltpu.VMEM((8, 2 * hw), F32)],
        compiler_params=_cp(("parallel", "parallel", "arbitrary"), 40),
    )(p3, bk, cre, cim, apow, dskip)


def _s5_bwd(dp3, p3, dy3, xs, bk, cre, cim, apow_rev, dskip):
    bsz, lp, _ = p3.shape
    tt = _row_tile(lp, 528, 8)
    nt = lp // tt
    nblk = tt // 8
    hw = 512
    tb = tt // 8

    def body(dp_any, u_ref, dy_ref, xs_ref, halo_ref, bkt_ref, cre_ref, cim_ref, ap_ref, d_ref,
             du_ref, dbk_ref, dcre_ref, dcim_ref, da_ref, dd_ref, g_ref, ext_ref, car_ref):
        b = pl.program_id(1)
        t = pl.program_id(2)
        tidx = nt - 1 - t

        @pl.when(t == 0)
        def _():
            car_ref[...] = jnp.zeros_like(car_ref)

        @pl.when((b == 0) & (t == 0))
        def _():
            dbk_ref[...] = jnp.zeros_like(dbk_ref)
            dcre_ref[...] = jnp.zeros_like(dcre_ref)
            dcim_ref[...] = jnp.zeros_like(dcim_ref)
            da_ref[...] = jnp.zeros_like(da_ref)
            dd_ref[...] = jnp.zeros_like(dd_ref)

        u = u_ref[...].astype(F32)
        dy = dy_ref[...]
        g_ref[:, :hw] = _dot5(dy, cre_ref[...])
        g_ref[:, hw:] = -_dot5(dy, cim_ref[...])
        ap = ap_ref[...]
        apr, api = ap[:, :hw], -ap[:, hw:]
        rows = lax.broadcasted_iota(jnp.int32, (8, hw), 0)
        lev = [(d, jnp.where(rows >= 8 - d, 0.0, jnp.broadcast_to(apr[8 - d:9 - d, :], (8, hw))),
                jnp.where(rows >= 8 - d, 0.0, jnp.broadcast_to(api[8 - d:9 - d, :], (8, hw)))) for d in (1, 2, 4)]

        def blk(i, carry):
            cr, ci = carry
            off = pl.multiple_of((nblk - 1 - i) * 8, 8)
            x = g_ref[pl.ds(off, 8), :]
            xr, xi = x[:, :hw], x[:, hw:]
            for d, lr, li in lev:
                mr, mi = _cmul(pltpu.roll(xr, 8 - d, 0), pltpu.roll(xi, 8 - d, 0), lr, li)
                xr, xi = xr + mr, xi + mi
            mr, mi = _cmul(apr, api, cr, ci)
            xr, xi = xr + mr, xi + mi
            g_ref[pl.ds(off, 8), :] = jnp.concatenate([xr, xi], axis=1)
            return xr[0:1, :], xi[0:1, :]

        c0 = car_ref[...]
        cr, ci = lax.fori_loop(0, nblk, blk, (c0[0:1, :hw], c0[0:1, hw:]), unroll=2)
        car_ref[...] = jnp.broadcast_to(jnp.concatenate([cr, ci], axis=1), car_ref.shape)

        gg = g_ref[...]
        du = _dot5(gg, bkt_ref[...]) + d_ref[...] * dy
        trow = tidx * tt + lax.broadcasted_iota(jnp.int32, (tt, 1), 0)
        du_ref[...] = jnp.where(trow >= PAD, du, 0.0).astype(du_ref.dtype)
        dbk_ref[...] += _dot5(u, gg, TN)
        xsv = xs_ref[...]
        dcre_ref[...] += _dot5(dy, xsv[:, :hw], TN)
        dcim_ref[...] -= _dot5(dy, xsv[:, hw:], TN)
        dd_ref[...] += _colsum(dy * u)
        ext_ref[0:8, :] = jnp.where(tidx == 0, 0.0, halo_ref[...])
        ext_ref[8:, :] = xsv
        xp = ext_ref[pl.ds(7, tt), :]
        gr, gi, pr, pi = gg[:, :hw], gg[:, hw:], xp[:, :hw], xp[:, hw:]
        da_ref[:, :hw] += _colsum(gr * pr + gi * pi)
        da_ref[:, hw:] += _colsum(gi * pr - gr * pi)

    ub = U_OFF // 128
    sd = jax.ShapeDtypeStruct
    rt = lambda t: nt - 1 - t
    tr = lambda a: jnp.swapaxes(a, 1, 2)
    res = _pcall(
        body, name="s5_bwd", grid=(S5_KCH, bsz, nt),
        in_specs=[pl.BlockSpec(memory_space=pl.ANY),
                  pl.BlockSpec((None, tt, 128), lambda k, b, t: (b, rt(t), ub + k)),
                  pl.BlockSpec((None, tt, 128), lambda k, b, t: (b, rt(t), k)),
                  pl.BlockSpec((None, None, tt, 2 * hw), lambda k, b, t: (b, k, rt(t), 0)),
                  pl.BlockSpec((None, None, 8, 2 * hw), lambda k, b, t: (b, k, jnp.maximum(rt(t) * tb - 1, 0), 0)),
                  pl.BlockSpec((None, 2 * hw, 128), lambda k, b, t: (k, 0, 0)),
                  pl.BlockSpec((None, 128, hw), lambda k, b, t: (k, 0, 0)),
                  pl.BlockSpec((None, 128, hw), lambda k, b, t: (k, 0, 0)),
                  pl.BlockSpec((None, 8, 2 * hw), lambda k, b, t: (k, 0, 0)),
                  pl.BlockSpec((1, 128), lambda k, b, t: (0, k))],
        out_specs=[pl.BlockSpec((None, tt, 128), lambda k, b, t: (b, rt(t), ub + k)),
                   pl.BlockSpec((None, 128, 2 * hw), lambda k, b, t: (k, 0, 0)),
                   pl.BlockSpec((None, 128, hw), lambda k, b, t: (k, 0, 0)),
                   pl.BlockSpec((None, 128, hw), lambda k, b, t: (k, 0, 0)),
                   pl.BlockSpec((None, 1, 2 * hw), lambda k, b, t: (k, 0, 0)),
                   pl.BlockSpec((1, 128), lambda k, b, t: (0, k))],
        out_shape=[sd(dp3.shape, dp3.dtype), sd((S5_KCH, 128, 2 * hw), F32), sd((S5_KCH, 128, hw), F32),
                   sd((S5_KCH, 128, hw), F32), sd((S5_KCH, 1, 2 * hw), F32), sd((1, S5_KCH * 128), F32)],
        scratch_shapes=[pltpu.VMEM((tt, 2 * hw), F32), pltpu.VMEM((tt + 8, 2 * hw), F32), pltpu.VMEM((8, 2 * hw), F32)],
        input_output_aliases={0: 0},
        compiler_params=_cp(("arbitrary", "arbitrary", "arbitrary"), 48),
    )(dp3, p3, dy3, xs, xs, tr(bk), tr(cre), tr(cim), apow_rev, dskip)
    return res[0], res[1], tr(res[2]), tr(res[3]), res[4], res[5]


_G0 = math.sqrt(2.0 / math.pi)
_G1 = 0.044715


def _gelu(y):
    return 0.5 * y * (1.0 + jnp.tanh(_G0 * (y + _G1 * y * y * y)))


def _gelu_grad(y):
    th = jnp.tanh(_G0 * (y + _G1 * y * y * y))
    return 0.5 * (1.0 + th) + 0.5 * y * (1.0 - th * th) * _G0 * (1.0 + 3.0 * _G1 * y * y)


def _glu_fwd(y_s5, wglu_g, lp):
    r, w = y_s5.shape
    tm = _row_tile(lp, 416)
    cw = wglu_g.shape[2]

    def body(y_ref, w_ref, gy_ref, z_ref, o_ref):
        gy = _bf(_gelu(y_ref[...]))
        gy_ref[...] = gy
        zs = [_dot(gy, w_ref[s]) for s in range(4)]
        for s in range(4):
            z_ref[:, s * cw:(s + 1) * cw] = _bf(zs[s])
        o_ref[:, :cw] = _bf(zs[0] * _sig(zs[2]))
        o_ref[:, cw:] = _bf(zs[1] * _sig(zs[3]))

    sd = jax.ShapeDtypeStruct
    return _pcall(
        body, name="glu_fwd", grid=(r // tm,),
        in_specs=[pl.BlockSpec((tm, w), lambda i: (i, 0)), _resident(wglu_g.shape)],
        out_specs=[pl.BlockSpec((tm, w), lambda i: (i, 0)), pl.BlockSpec((tm, 4 * cw), lambda i: (i, 0)),
                   pl.BlockSpec((tm, 2 * cw), lambda i: (i, 0))],
        out_shape=[sd((r, w), BF16), sd((r, 4 * cw), BF16), sd((r, 2 * cw), BF16)],
        compiler_params=_cp(("parallel",), 40),
    )(y_s5, wglu_g)


def _glu_bwd(dyg, z, y_s5, wglu_g, lp):
    r, w = y_s5.shape
    tm = _row_tile(lp, 416)
    cw = wglu_g.shape[2]

    def body(d_ref, z_ref, y_ref, w_ref, dz_ref, dy_ref):
        d = d_ref[...].astype(F32)
        zz = z_ref[...].astype(F32)
        acc = jnp.zeros((tm, w), F32)
        for s in range(2):
            z1 = zz[:, s * cw:(s + 1) * cw]
            sg = _sig(zz[:, (2 + s) * cw:(3 + s) * cw])
            dd = d[:, s * cw:(s + 1) * cw]
            dz1 = _bf(dd * sg)
            dz2 = _bf(dd * z1 * sg * (1.0 - sg))
            dz_ref[:, s * cw:(s + 1) * cw] = dz1
            dz_ref[:, (2 + s) * cw:(3 + s) * cw] = dz2
            acc += _dot(dz1, w_ref[s], NT) + _dot(dz2, w_ref[2 + s], NT)
        dy_ref[...] = acc * _gelu_grad(y_ref[...])

    sd = jax.ShapeDtypeStruct
    return _pcall(
        body, name="glu_bwd", grid=(r // tm,),
        in_specs=[pl.BlockSpec((tm, 2 * cw), lambda i: (i, 0)), pl.BlockSpec((tm, 4 * cw), lambda i: (i, 0)),
                  pl.BlockSpec((tm, w), lambda i: (i, 0)), _resident(wglu_g.shape)],
        out_specs=[pl.BlockSpec((tm, 4 * cw), lambda i: (i, 0)), pl.BlockSpec((tm, w), lambda i: (i, 0))],
        out_shape=[sd((r, 4 * cw), BF16), sd((r, w), F32)],
        compiler_params=_cp(("parallel",), 40),
    )(dyg, z, y_s5, wglu_g)


def _conv_fwd(p3, cw, cb):
    bsz, lp, _ = p3.shape
    tt = _row_tile(lp, 704)
    nt = lp // tt
    tb = tt // 8
    c = cw.shape[1]
    qb = Q_OFF // c

    hr = HALO_ROWS
    off = hr - (CONV_W - 1)

    def body(x_ref, halo_ref, w_ref, b_ref, pre_ref, act_ref, ext_ref):
        t = pl.program_id(1)
        ext_ref[0:hr, :] = jnp.where(t == 0, 0.0, halo_ref[...].astype(F32))
        ext_ref[hr:, :] = x_ref[...].astype(F32)
        w = w_ref[...]
        acc = b_ref[...] + w[0:1, :] * ext_ref[pl.ds(off, tt), :]
        for j in range(1, CONV_W):
            acc = acc + w[j:j + 1, :] * ext_ref[pl.ds(off + j, tt), :]
        pre_ref[...] = _bf(acc)
        act_ref[...] = _bf(acc * _sig(acc))

    sd = jax.ShapeDtypeStruct
    return _pcall(
        body, name="conv_fwd", grid=(bsz, nt),
        in_specs=[pl.BlockSpec((None, tt, c), lambda b, t: (b, t, qb)),
                  pl.BlockSpec((None, hr, c), lambda b, t: (b, jnp.maximum(t * (tt // hr) - 1, 0), qb)),
                  _const((CONV_W, c)), _const((1, c))],
        out_specs=[pl.BlockSpec((None, tt, c), lambda b, t: (b, t, 0))] * 2,
        out_shape=[sd((bsz, lp, c), BF16)] * 2,
        scratch_shapes=[pltpu.VMEM((tt + hr, c), F32)],
        compiler_params=_cp(("parallel", "parallel")),
    )(p3, p3, cw, cb)


def _conv_bwd(dp3, p3, dact3, pre3, cw):
    bsz, lp, _ = p3.shape
    tt = _row_tile(lp, 704)
    nt = lp // tt
    tb = tt // 8
    c = cw.shape[1]
    qb = Q_OFF // c

    hr = HALO_ROWS
    off = hr - (CONV_W - 1)

    def silu_grad(x):
        s = _sig(x)
        return s * (1.0 + x * (1.0 - s))

    def body(dp_any, x_ref, xh_ref, d_ref, dh_ref, pre_ref, preh_ref, w_ref, o_ref, dw_ref, db_ref, ext_ref, dext_ref):
        b = pl.program_id(0)
        t = pl.program_id(1)

        @pl.when((b == 0) & (t == 0))
        def _():
            dw_ref[...] = jnp.zeros_like(dw_ref)
            db_ref[...] = jnp.zeros_like(db_ref)

        dc = d_ref[...].astype(F32) * silu_grad(pre_ref[...].astype(F32))
        dch = jnp.where(t == nt - 1, 0.0, dh_ref[...].astype(F32) * silu_grad(preh_ref[...].astype(F32)))
        dext_ref[0:tt, :] = dc
        dext_ref[tt:, :] = dch
        ext_ref[0:hr, :] = jnp.where(t == 0, 0.0, xh_ref[...].astype(F32))
        ext_ref[hr:, :] = x_ref[...].astype(F32)
        w = w_ref[...]
        acc = w[CONV_W - 1:CONV_W, :] * dc
        for j in range(CONV_W - 1):
            acc = acc + w[j:j + 1, :] * dext_ref[pl.ds(CONV_W - 1 - j, tt), :]
        trow = t * tt + lax.broadcasted_iota(jnp.int32, (tt, 1), 0)
        o_ref[...] = jnp.where(trow >= PAD, acc, 0.0).astype(o_ref.dtype)
        db_ref[...] += _colsum(dc)
        for j in range(CONV_W):
            dw_ref[j:j + 1, :] += _colsum(dc * ext_ref[pl.ds(off + j, tt), :])

    sd = jax.ShapeDtypeStruct
    nxt = lambda t: jnp.minimum((t + 1) * (tt // hr), lp // hr - 1)
    return _pcall(
        body, name="conv_bwd", grid=(bsz, nt),
        in_specs=[pl.BlockSpec(memory_space=pl.ANY),
                  pl.BlockSpec((None, tt, c), lambda b, t: (b, t, qb)),
                  pl.BlockSpec((None, hr, c), lambda b, t: (b, jnp.maximum(t * (tt // hr) - 1, 0), qb)),
                  pl.BlockSpec((None, tt, c), lambda b, t: (b, t, 0)),
                  pl.BlockSpec((None, hr, c), lambda b, t: (b, nxt(t), 0)),
                  pl.BlockSpec((None, tt, c), lambda b, t: (b, t, 0)),
                  pl.BlockSpec((None, hr, c), lambda b, t: (b, nxt(t), 0)),
                  _const((CONV_W, c))],
        out_specs=[pl.BlockSpec((None, tt, c), lambda b, t: (b, t, qb)), _const((CONV_W, c)), _const((1, c))],
        out_shape=[sd(dp3.shape, dp3.dtype), sd((CONV_W, c), F32), sd((1, c), F32)],
        scratch_shapes=[pltpu.VMEM((tt + hr, c), F32), pltpu.VMEM((tt + hr, c), F32)],
        input_output_aliases={0: 0},
        compiler_params=_cp(("arbitrary", "arbitrary")),
    )(dp3, p3, p3, dact3, dact3, pre3, pre3, cw)


def _mlstm_gates(g, h_idx, c_idx, lc):
    lane = lax.broadcasted_iota(jnp.int32, g.shape, 1)
    i_col = jnp.sum(jnp.where(lane == h_idx, g, 0.0), axis=1, keepdims=True)
    f_col = jnp.sum(jnp.where(lane == M_HEADS + h_idx, g, 0.0), axis=1, keepdims=True)
    row = lax.broadcasted_iota(jnp.int32, (lc, 1), 0)
    valid = (c_idx * lc + row) >= PAD
    li = jnp.where(valid, i_col, NEG)
    lf = jnp.where(valid, jnp.minimum(f_col, 0.0) - jnp.log(1.0 + jnp.exp(-jnp.abs(f_col))), 0.0)
    r2 = lax.broadcasted_iota(jnp.int32, (lc, lc), 0)
    c2 = lax.broadcasted_iota(jnp.int32, (lc, lc), 1)
    eye = r2 == c2
    tril = r2 >= c2
    to_row = lambda col: jnp.sum(jnp.where(eye, col, 0.0), axis=0, keepdims=True)
    lf_row = to_row(lf)
    b_col = jnp.sum(jnp.where(tril, lf_row, 0.0), axis=1, keepdims=True)
    b_row = to_row(b_col)
    li_row = to_row(li)
    d_mat = jnp.where(tril, b_col - b_row + li_row, NEG)
    return dict(f_col=f_col, valid=valid, li=li, b_col=b_col, d_mat=d_mat, eye=eye, r2=r2, c2=c2, row=row,
                to_row=to_row)


def _mlstm_chunk(q, ks, v, gq, c_st, n_st, m_st, lc):
    b_col, d_mat = gq["b_col"], gq["d_mat"]
    m_inter = b_col + m_st
    m_row = jnp.maximum(m_inter, jnp.max(d_mat, axis=1, keepdims=True))
    w_intra = jnp.exp(d_mat - m_row)
    w_inter = jnp.exp(m_inter - m_row)
    qb, kb, vb, cb = _bf(q), _bf(ks), _bf(v), _bf(c_st)
    s = _dot(qb, kb, NT) * w_intra
    qc = _dot(qb, cb)
    num = _dot(_bf(s), vb) + w_inter * qc
    qn = jnp.sum(q * n_st, axis=1, keepdims=True)
    den = jnp.sum(s, axis=1, keepdims=True) + w_inter * qn
    e = jnp.exp(-m_row)
    nn = jnp.maximum(jnp.abs(den), e)
    b_last = b_col[lc - 1:lc, :]
    g_log = b_last - b_col + gq["li"]
    m_new = jnp.maximum(b_last + m_st, jnp.max(g_log, axis=0, keepdims=True))
    w_k = jnp.exp(g_log - m_new)
    decay = jnp.exp(b_last + m_st - m_new)
    return dict(w_intra=w_intra, w_inter=w_inter, qb=qb, kb=kb, vb=vb, cb=cb, s=s, qc=qc, num=num, qn=qn, den=den,
                e=e, nn=nn, m_new=m_new, w_k=w_k, decay=decay)


def _chunks_per_step(nc):
    return max(c for c in (3, 2, 1) if nc % c == 0)


def _mlstm_fwd(qk3, p3, pg3):
    bsz, lp, _ = p3.shape
    lc = M_CHUNK
    nc = lp // lc
    dk, dv = 128, 256
    scale = dk ** -0.5

    cps = _chunks_per_step(nc)
    rows = cps * lc

    def body(q_ref, k_ref, v_ref, g_ref, h_ref, cs_ref, ns_ref, ms_ref, c_sc, n_sc, m_sc):
        st = pl.program_id(1)

        @pl.when(st == 0)
        def _():
            c_sc[...] = jnp.zeros_like(c_sc)
            n_sc[...] = jnp.zeros_like(n_sc)
            m_sc[...] = jnp.zeros_like(m_sc)

        for j in range(cps):
            rs = slice(j * lc, (j + 1) * lc)
            g = g_ref[rs, :]
            for hh in range(M_HEADS):
                c_st, n_st, m_all = c_sc[hh], n_sc[hh], m_sc[hh]
                cs_ref[hh, j] = c_st
                ns_ref[hh, j] = n_st
                ms_ref[hh, j] = m_all
                m_st = m_all[:, 0:1]
                q = q_ref[rs, hh * dk:(hh + 1) * dk].astype(F32)
                ks = k_ref[rs, hh * dk:(hh + 1) * dk].astype(F32) * scale
                v = v_ref[rs, hh * dv:(hh + 1) * dv]
                gq = _mlstm_gates(g, hh, st * cps + j, lc)
                f = _mlstm_chunk(q, ks, v, gq, c_st, n_st, m_st, lc)
                h_ref[rs, hh * dv:(hh + 1) * dv] = _bf(f["num"] / f["nn"])
                kw = ks * f["w_k"]
                c_sc[hh] = f["decay"] * c_st + _dot(_bf(kw), f["vb"], TN)
                n_sc[hh] = f["decay"] * n_st + _colsum(kw)
                m_sc[hh] = jnp.broadcast_to(f["m_new"], (1, 128))

    sd = jax.ShapeDtypeStruct
    nh = M_HEADS
    return _pcall(
        body, name="mlstm_fwd", grid=(bsz, nc // cps),
        in_specs=[pl.BlockSpec((None, rows, nh * dk), lambda b, c: (b, c, 0)),
                  pl.BlockSpec((None, rows, nh * dk), lambda b, c: (b, c, 1)),
                  pl.BlockSpec((None, rows, nh * dv), lambda b, c: (b, c, V_OFF // (nh * dv))),
                  pl.BlockSpec((None, rows, 128), lambda b, c: (b, c, 0))],
        out_specs=[pl.BlockSpec((None, rows, nh * dv), lambda b, c: (b, c, 0)),
                   pl.BlockSpec((None, nh, cps, dk, dv), lambda b, c: (b, 0, c, 0, 0)),
                   pl.BlockSpec((None, nh, cps, 1, dk), lambda b, c: (b, 0, c, 0, 0)),
                   pl.BlockSpec((None, nh, cps, 1, 128), lambda b, c: (b, 0, c, 0, 0))],
        out_shape=[sd((bsz, lp, nh * dv), BF16), sd((bsz, nh, nc, dk, dv), F32),
                   sd((bsz, nh, nc, 1, dk), F32), sd((bsz, nh, nc, 1, 128), F32)],
        scratch_shapes=[pltpu.VMEM((nh, dk, dv), F32), pltpu.VMEM((nh, 1, dk), F32), pltpu.VMEM((nh, 1, 128), F32)],
        compiler_params=_cp(("parallel", "arbitrary")),
    )(qk3, qk3, p3, pg3)


def _mlstm_bwd(dp3, qk3, p3, pg3, dh3, cs, ns, ms):
    bsz, lp, _ = p3.shape
    lc = M_CHUNK
    nc = lp // lc
    dk, dv = 128, 256
    scale = dk ** -0.5

    cps = _chunks_per_step(nc)
    nst = nc // cps
    rows = cps * lc

    def body(dp_any, q_ref, k_ref, v_ref, g_ref, dh_ref, cs_ref, ns_ref, ms_ref,
             dv_ref, dqk_ref, dg_ref, dc_sc, dn_sc):
        t = pl.program_id(1)
        st = nst - 1 - t

        @pl.when(t == 0)
        def _():
            dc_sc[...] = jnp.zeros_like(dc_sc)
            dn_sc[...] = jnp.zeros_like(dn_sc)

        lane = lax.broadcasted_iota(jnp.int32, (lc, 128), 1)
        for j in reversed(range(cps)):
            rs = slice(j * lc, (j + 1) * lc)
            g = g_ref[rs, :]
            dgate = jnp.zeros((lc, 128), F32)
            for hh in range(M_HEADS):
                dgate = head(hh, j, rs, st * cps + j, g, lane, dgate, q_ref, k_ref, v_ref, dh_ref, cs_ref, ns_ref,
                             ms_ref, dv_ref, dqk_ref, dc_sc, dn_sc)
            dg_ref[rs, :] = dgate.astype(dg_ref.dtype)

    def head(hh, j, sl, c, g, lane, dgate, q_ref, k_ref, v_ref, dh_ref, cs_ref, ns_ref, ms_ref, dv_ref, dqk_ref,
             dc_sc, dn_sc):
        c_st, n_st = cs_ref[hh, j], ns_ref[hh, j]
        m_st = ms_ref[hh, j][:, 0:1]
        q = q_ref[sl, hh * dk:(hh + 1) * dk].astype(F32)
        ks = k_ref[sl, hh * dk:(hh + 1) * dk].astype(F32) * scale
        v = v_ref[sl, hh * dv:(hh + 1) * dv]
        dh = dh_ref[sl, hh * dv:(hh + 1) * dv].astype(F32)
        gq = _mlstm_gates(g, hh, c, lc)
        f = _mlstm_chunk(q, ks, v, gq, c_st, n_st, m_st, lc)
        eye, r2, c2, row, valid = gq["eye"], gq["r2"], gq["c2"], gq["row"], gq["valid"]
        w_intra, w_inter, s, nn, den = f["w_intra"], f["w_inter"], f["s"], f["nn"], f["den"]
        qb, kb, vb, cb, w_k, decay = f["qb"], f["kb"], f["vb"], f["cb"], f["w_k"], f["decay"]
        d_c, d_n = dc_sc[hh], dn_sc[hh]
        d_cb = _bf(d_c)

        hout = f["num"] / nn
        dnum = dh / nn
        d_nn = -jnp.sum(dh * hout, axis=1, keepdims=True) / nn
        dden = jnp.where(jnp.abs(den) > f["e"], d_nn * jnp.sign(den), 0.0)
        wdnum = w_inter * dnum
        wdden = w_inter * dden
        ds = _dot(_bf(dnum), vb, NT) + dden
        dsw = _bf(ds * w_intra)
        dq = _dot(dsw, kb) + _dot(_bf(wdnum), cb, NT) + wdden * n_st
        dkw = _dot(vb, d_cb, NT) + d_n
        dks = _dot(dsw, qb, TN) + dkw * w_k
        kw = ks * w_k
        dvv = _dot(_bf(s), _bf(dnum), TN) + _dot(_bf(kw), d_cb)
        dd = ds * s
        rs = jnp.sum(dd, axis=1, keepdims=True)
        cs_col = jnp.sum(jnp.where(eye, jnp.sum(dd, axis=0, keepdims=True), 0.0), axis=1, keepdims=True)
        dwi = jnp.sum(dnum * f["qc"], axis=1, keepdims=True) + dden * f["qn"]
        db = rs - cs_col + dwi * w_inter
        dli = cs_col
        ddecay = jnp.sum(jnp.sum(d_c * c_st, axis=1, keepdims=True), axis=0, keepdims=True) \
            + jnp.sum(d_n * n_st, axis=1, keepdims=True)
        dgl = jnp.sum(dkw * ks, axis=1, keepdims=True) * w_k
        dblast = ddecay * decay + jnp.sum(dgl, axis=0, keepdims=True)
        db = db - dgl + jnp.where(row == lc - 1, dblast, 0.0)
        dli = dli + dgl
        db_row = gq["to_row"](db)
        dlf = jnp.sum(jnp.where(c2 >= r2, db_row, 0.0), axis=1, keepdims=True)
        dlf = jnp.where(valid, dlf, 0.0)
        dgate = jnp.where(lane == hh, jnp.where(valid, dli, 0.0), dgate)
        dgate = jnp.where(lane == M_HEADS + hh, dlf / (1.0 + jnp.exp(gq["f_col"])), dgate)
        dqk_ref[sl, hh * dk:(hh + 1) * dk] = _bf(dq)
        dqk_ref[sl, (M_HEADS + hh) * dk:(M_HEADS + hh + 1) * dk] = _bf(dks * scale)
        dv_ref[sl, hh * dv:(hh + 1) * dv] = dvv.astype(dv_ref.dtype)
        dc_sc[hh] = decay * d_c + _dot(qb, _bf(wdnum), TN)
        dn_sc[hh] = decay * d_n + _colsum(q * wdden)
        return dgate

    sd = jax.ShapeDtypeStruct
    nh = M_HEADS
    rc = lambda c: nst - 1 - c
    return _pcall(
        body, name="mlstm_bwd", grid=(bsz, nst),
        in_specs=[pl.BlockSpec(memory_space=pl.ANY),
                  pl.BlockSpec((None, rows, nh * dk), lambda b, c: (b, rc(c), 0)),
                  pl.BlockSpec((None, rows, nh * dk), lambda b, c: (b, rc(c), 1)),
                  pl.BlockSpec((None, rows, nh * dv), lambda b, c: (b, rc(c), V_OFF // (nh * dv))),
                  pl.BlockSpec((None, rows, 128), lambda b, c: (b, rc(c), 0)),
                  pl.BlockSpec((None, rows, nh * dv), lambda b, c: (b, rc(c), 0)),
                  pl.BlockSpec((None, nh, cps, dk, dv), lambda b, c: (b, 0, rc(c), 0, 0)),
                  pl.BlockSpec((None, nh, cps, 1, dk), lambda b, c: (b, 0, rc(c), 0, 0)),
                  pl.BlockSpec((None, nh, cps, 1, 128), lambda b, c: (b, 0, rc(c), 0, 0))],
        out_specs=[pl.BlockSpec((None, rows, nh * dv), lambda b, c: (b, rc(c), V_OFF // (nh * dv))),
                   pl.BlockSpec((None, rows, 2 * nh * dk), lambda b, c: (b, rc(c), 0)),
                   pl.BlockSpec((None, rows, 128), lambda b, c: (b, rc(c), 0))],
        out_shape=[sd(dp3.shape, dp3.dtype), sd((bsz, lp, 2 * nh * dk), BF16), sd((bsz, lp, 128), dp3.dtype)],
        scratch_shapes=[pltpu.VMEM((nh, dk, dv), F32), pltpu.VMEM((nh, 1, dk), F32)],
        input_output_aliases={0: 0},
        compiler_params=_cp(("arbitrary", "arbitrary")),
    )(dp3, qk3, qk3, p3, pg3, dh3, cs, ns, ms)


def _headnorm(x):
    dv = x.shape[1] // M_HEADS
    xh, rs = [], []
    for h in range(M_HEADS):
        xx = x[:, h * dv:(h + 1) * dv]
        mu = jnp.mean(xx, axis=-1, keepdims=True)
        xc = xx - mu
        rstd = lax.rsqrt(jnp.mean(xc * xc, axis=-1, keepdims=True) + LN_EPS)
        xh.append(xc * rstd)
        rs.append(rstd)
    return jnp.concatenate(xh, axis=1), rs


def _mix_fwd(hm, p, ys5g, h0, gn, wmo_bf, wo_bf, lp):
    r, d = hm.shape
    tm = _row_tile(lp, 384)

    def body(hm_ref, o_ref, gs_ref, gm_ref, ys_ref, h0_ref, gn_ref, wmo_ref, wo_ref,
             ymin_ref, mix_ref, r1_ref):
        xhat, _ = _headnorm(hm_ref[...].astype(F32))
        ymin = _bf(_sig(o_ref[...].astype(F32)) * (xhat * gn_ref[...]))
        ymin_ref[...] = ymin
        ym = _dot(ymin, wmo_ref[...])
        mix = _bf(_sig(gs_ref[...].astype(F32)) * ys_ref[...].astype(F32) + _sig(gm_ref[...].astype(F32)) * ym)
        mix_ref[...] = mix
        r1_ref[...] = ALPHA * h0_ref[...] + _dot(mix, wo_ref[...])

    sd = jax.ShapeDtypeStruct
    row = pl.BlockSpec((tm, d), lambda i: (i, 0))
    return _pcall(
        body, name="mix_fwd", grid=(r // tm,),
        in_specs=[row, pl.BlockSpec((tm, d), lambda i: (i, O_OFF // d)), pl.BlockSpec((tm, d), lambda i: (i, GS_OFF // d)),
                  pl.BlockSpec((tm, d), lambda i: (i, GM_OFF // d)), row, row, _const((1, d)),
                  _resident((d, d)), _resident((d, d))],
        out_specs=[row] * 3,
        out_shape=[sd((r, d), BF16), sd((r, d), BF16), sd((r, d), F32)],
        compiler_params=_cp(("parallel",), 48),
    )(hm, p, p, p, ys5g, h0, gn, wmo_bf, wo_bf)


def _mix_bwd(dr1, wo_bf, wmo_bf, p, ys5g, ymin, hm, gn, lp):
    r, d = hm.shape
    tm = _row_tile(lp, 384)
    dv = d // M_HEADS

    def body(dr1_ref, wo_ref, wmo_ref, o_ref, gs_ref, gm_ref, ys_ref, ym_ref, hm_ref, gn_ref,
             dp_ref, dys_ref, dym_ref, dhm_ref, dgn_ref):
        i = pl.program_id(0)

        @pl.when(i == 0)
        def _():
            dgn_ref[...] = jnp.zeros_like(dgn_ref)

        dmix = _dot(_bf(dr1_ref[...]), wo_ref[...], NT)
        sgs, sgm, so = (_sig(gs_ref[...].astype(F32)), _sig(gm_ref[...].astype(F32)), _sig(o_ref[...].astype(F32)))
        dys_ref[...] = _bf(dmix * sgs)
        dp_ref[:, d:2 * d] = _bf(dmix * ys_ref[...].astype(F32) * sgs * (1.0 - sgs))
        dym = dmix * sgm
        dym_ref[...] = _bf(dym)
        ym = _dot(ym_ref[...], wmo_ref[...])
        dp_ref[:, 2 * d:3 * d] = _bf(dmix * ym * sgm * (1.0 - sgm))
        dymin = _dot(_bf(dym), wmo_ref[...], NT)
        xhat, rs = _headnorm(hm_ref[...].astype(F32))
        gn_ = gn_ref[...]
        dp_ref[:, 0:d] = _bf(dymin * (xhat * gn_) * so * (1.0 - so))
        dhn = dymin * so
        dgn_ref[...] += _colsum(dhn * xhat)
        dxh = dhn * gn_
        for h in range(M_HEADS):
            sl = slice(h * dv, (h + 1) * dv)
            a, xh = dxh[:, sl], xhat[:, sl]
            m1 = jnp.mean(a, axis=-1, keepdims=True)
            m2 = jnp.mean(a * xh, axis=-1, keepdims=True)
            dhm_ref[:, sl] = _bf(rs[h] * (a - m1 - xh * m2))

    sd = jax.ShapeDtypeStruct
    row = pl.BlockSpec((tm, d), lambda i: (i, 0))
    vec = _const((1, d))
    return _pcall(
        body, name="mix_bwd", grid=(r // tm,),
        in_specs=[row, _resident((d, d)), _resident((d, d)),
                  pl.BlockSpec((tm, d), lambda i: (i, O_OFF // d)), pl.BlockSpec((tm, d), lambda i: (i, GS_OFF // d)),
                  pl.BlockSpec((tm, d), lambda i: (i, GM_OFF // d)), row, row, row, vec],
        out_specs=[pl.BlockSpec((tm, 3 * d), lambda i: (i, 0)), row, row, row, vec],
        out_shape=[sd((r, NP), BF16), sd((r, d), BF16), sd((r, d), BF16), sd((r, d), BF16), sd((1, d), F32)],
        compiler_params=_cp(("arbitrary",), 56),
    )(dr1, wo_bf, wmo_bf, p, p, p, ys5g, ymin, hm, gn)


def _mlp_fwd(r1, tgt, g1, b1, wup_g, wdn_bf, bup, g2, b2, lp):
    r, d = r1.shape
    tm = _row_tile(lp, 384)
    tps = lp // tm
    nf = wup_g.shape[0]

    def body(r1_ref, t_ref, g1_ref, b1_ref, wup_ref, wdn_ref, bup_ref, g2_ref, b2_ref,
             dr2_ref, h1b_ref, act_ref, loss_ref, dg2_ref, db2_ref):
        i = pl.program_id(0)

        @pl.when(i == 0)
        def _():
            loss_ref[...] = jnp.zeros_like(loss_ref)
            dg2_ref[...] = jnp.zeros_like(dg2_ref)
            db2_ref[...] = jnp.zeros_like(db2_ref)

        h1, _, _ = _ln_fwd(r1_ref[...], g1_ref[...], b1_ref[...])
        h1b = _bf(h1)
        h1b_ref[...] = h1b
        ff = jnp.zeros((tm, d), F32)
        for s in range(nf):
            up = _dot(h1b, wup_ref[s]) + bup_ref[:, s * d:(s + 1) * d]
            a = jnp.maximum(up, 0.0)
            a = _bf(a * a)
            act_ref[:, s * d:(s + 1) * d] = a
            ff = ff + _dot(a, wdn_ref[s * d:(s + 1) * d, :])
        r2 = ALPHA * h1 + ff
        g2 = g2_ref[...]
        y, xhat, rstd = _ln_fwd(r2, g2, b2_ref[...])
        t = (i % tps) * tm + lax.broadcasted_iota(jnp.int32, (tm, 1), 0)
        diff = jnp.where(t >= PAD + N_META, y - t_ref[...], 0.0)
        loss_ref[...] += 0.5 / d * jnp.sum(jnp.sum(diff * diff, axis=1, keepdims=True), axis=0, keepdims=True)
        dy = diff * (1.0 / d)
        dg2_ref[...] += _colsum(dy * xhat)
        db2_ref[...] += _colsum(dy)
        dr2_ref[...] = _ln_bwd(dy, xhat, rstd, g2)

    sd = jax.ShapeDtypeStruct
    row = pl.BlockSpec((tm, d), lambda i: (i, 0))
    vec = _const((1, d))
    return _pcall(
        body, name="mlp_fwd", grid=(r // tm,),
        in_specs=[row, row, vec, vec, _resident(wup_g.shape), _resident(wdn_bf.shape), _const((1, nf * d)), vec, vec],
        out_specs=[row, row, pl.BlockSpec((tm, nf * d), lambda i: (i, 0)), _const((1, 128)), vec, vec],
        out_shape=[sd((r, d), F32), sd((r, d), BF16), sd((r, nf * d), BF16), sd((1, 128), F32), sd((1, d), F32),
                   sd((1, d), F32)],
        compiler_params=_cp(("arbitrary",), 56),
    )(r1, tgt, g1, b1, wup_g, wdn_bf, bup, g2, b2)


def _mlp_bwd(act, dr2, r1, g1, wup_g, wdn_bf, lp):
    r, d = dr2.shape
    tm = _row_tile(lp, 384)
    nf = wup_g.shape[0]

    def body(act_ref, dr2_ref, r1_ref, g1_ref, wup_ref, wdn_ref, dr1_ref, dup_ref, dbup_ref, dg1_ref, db1_ref):
        i = pl.program_id(0)

        @pl.when(i == 0)
        def _():
            dbup_ref[...] = jnp.zeros_like(dbup_ref)
            dg1_ref[...] = jnp.zeros_like(dg1_ref)
            db1_ref[...] = jnp.zeros_like(db1_ref)

        dr2 = dr2_ref[...]
        dr2b = _bf(dr2)
        acc = ALPHA * dr2
        for s in range(nf):
            dact = _dot(dr2b, wdn_ref[s * d:(s + 1) * d, :], NT)
            dup = dact * (2.0 * jnp.sqrt(act_ref[:, s * d:(s + 1) * d].astype(F32)))
            dbup_ref[:, s * d:(s + 1) * d] += _colsum(dup)
            dupb = _bf(dup)
            dup_ref[:, s * d:(s + 1) * d] = dupb
            acc = acc + _dot(dupb, wup_ref[s], NT)
        g1 = g1_ref[...]
        _, xhat1, rstd1 = _ln_fwd(r1_ref[...], g1, 0.0)
        dr1_ref[...] = _ln_bwd(acc, xhat1, rstd1, g1)
        dg1_ref[...] += _colsum(acc * xhat1)
        db1_ref[...] += _colsum(acc)

    sd = jax.ShapeDtypeStruct
    row = pl.BlockSpec((tm, d), lambda i: (i, 0))
    vec = _const((1, d))
    return _pcall(
        body, name="mlp_bwd", grid=(r // tm,),
        in_specs=[pl.BlockSpec((tm, nf * d), lambda i: (i, 0)), row, row, vec, _resident(wup_g.shape),
                  _resident(wdn_bf.shape)],
        out_specs=[row, pl.BlockSpec((tm, nf * d), lambda i: (i, 0)), _const((1, nf * d)), vec, vec],
        out_shape=[sd((r, d), F32), sd((r, nf * d), BF16), sd((1, nf * d), F32), sd((1, d), F32), sd((1, d), F32)],
        compiler_params=_cp(("arbitrary",), 56),
    )(act, dr2, r1, g1, wup_g, wdn_bf)


def _s5_block_mats(bb_re_t, bb_im_t, c_re, c_im, ap_re, ap_im):
    ng = c_re.shape[0]
    gl = ng // S5_KCH
    eye = jnp.eye(gl, dtype=F32)

    def bmat(bt):
        bb = jnp.transpose(bt, (1, 0, 2)).reshape(S5_KCH, gl, S5_GROUP, S5_STATE)
        return jnp.einsum("kghp,gj->kghjp", bb, eye).reshape(S5_KCH, gl * S5_GROUP, gl * S5_STATE)

    def cmat(c):
        cc = c.reshape(S5_KCH, gl, S5_GROUP, S5_STATE)
        return jnp.einsum("kghp,gj->kjpgh", cc, eye).reshape(S5_KCH, gl * S5_STATE, gl * S5_GROUP)

    def pw(a):
        return jnp.transpose(a.reshape(8, S5_KCH, gl * S5_STATE), (1, 0, 2))

    bk = jnp.concatenate([bmat(bb_re_t), bmat(bb_im_t)], axis=-1)
    apow = jnp.concatenate([pw(ap_re), pw(ap_im)], axis=-1)
    return _bf(bk), _bf(cmat(c_re)), _bf(cmat(c_im)), apow


def _s5_block_grads(dbk, dcre, dcim, da):
    gl = dbk.shape[1] // S5_GROUP
    ng = gl * S5_KCH
    eye = jnp.eye(gl, dtype=F32)
    hw = gl * S5_STATE

    def bpart(x):
        x = x.reshape(S5_KCH, gl, S5_GROUP, gl, S5_STATE)
        x = jnp.einsum("kghjp,gj->kghp", x, eye).reshape(ng, S5_GROUP, S5_STATE)
        return jnp.transpose(x, (1, 0, 2))

    def cpart(x):
        x = x.reshape(S5_KCH, gl, S5_STATE, gl, S5_GROUP)
        return jnp.einsum("kjpgh,gj->kghp", x, eye).reshape(ng, S5_GROUP, S5_STATE)

    return (bpart(dbk[..., :hw]), bpart(dbk[..., hw:]), cpart(dcre), cpart(dcim),
            da[:, 0, :hw].reshape(ng, S5_STATE), da[:, 0, hw:].reshape(ng, S5_STATE))


def _tie(a, tok):
    return a if tok is None else a + tok[0, 0]


def _local_step(x, tgt, w, early=None, late=None, ready=None):
    ready = ready or (lambda names, g: None)
    bsz, seq, d = x.shape
    lp = PAD + N_META + seq
    r = bsz * lp
    tgtp = jnp.concatenate([jnp.zeros((bsz, PAD + N_META, d), F32), tgt], axis=1).reshape(r, d)

    h0, h0b = _ln0_fwd(x, w["meta_tokens"], w["ln0_g"], w["ln0_b"])
    b_re_t = jnp.transpose(w["s5_b_re"], (2, 0, 1))
    b_im_t = jnp.transpose(w["s5_b_im"], (2, 0, 1))
    ap_re, ap_im, bb_re_t, bb_im_t = _s5_prep(w["s5_lambda_re"], w["s5_lambda_im"], w["s5_log_dt"], b_re_t, b_im_t)
    bk, cre, cim, apow = _s5_block_mats(bb_re_t, bb_im_t, w["s5_c_re"], w["s5_c_im"], ap_re, ap_im)
    apow_rev = jnp.flip(apow, axis=1)
    if early is not None:
        w = {**w, **early((h0, tgtp, bk, cre, cim, apow_rev))}
    p, pg = _inproj(h0b, w["w_in"], w["b_in"], lp)
    p3 = p.reshape(bsz, lp, NP)
    pg3 = pg.reshape(bsz, lp, 128)

    y_s5, xs = _s5_fwd(p3, bk, cre, cim, apow, w["s5_d"])
    sw = y_s5.shape[-1]
    if late is not None:
        w = {**w, **late(y_s5)}
    gy, z, ys5g = _glu_fwd(y_s5.reshape(r, sw), w["s5_w_glu"], lp)

    pre3, qk3 = _conv_fwd(p3, w["qk_conv_w"], w["qk_conv_b"])
    hm3, cs, ns, ms = _mlstm_fwd(qk3, p3, pg3)
    hm = hm3.reshape(r, d)
    ymin, mix, r1 = _mix_fwd(hm, p, ys5g, h0, w["m_norm_g"], w["m_w_out"], w["w_o"], lp)
    dr2, h1b, act, loss, dg2, db2 = _mlp_fwd(r1, tgtp, w["ln1_g"], w["ln1_b"], w["w_up"], w["w_down"], w["b_up"],
                                             w["ln2_g"], w["ln2_b"], lp)

    g = {"ln2_g": dg2, "ln2_b": db2}
    dr1, dup, g["b_up"], g["ln1_g"], g["ln1_b"] = _mlp_bwd(act, dr2, r1, w["ln1_g"], w["w_up"], w["w_down"], lp)
    g["w_down"] = _mm_tn(act, dr2, name="dw_down")
    g["w_up"] = _mm_tn(h1b, dup, name="dw_up", split=w["w_up"].shape[0])
    tok = ready(("w_down", "w_up"), g)
    dp, dys5g, dym, dhm, g["m_norm_g"] = _mix_bwd(
        dr1, w["w_o"], w["m_w_out"], p, ys5g, ymin, hm, _tie(w["m_norm_g"], tok), lp)
    g["w_o"] = _mm_tn(mix, dr1, name="dw_o")
    g["m_w_out"] = _mm_tn(ymin, dym, name="dw_mout")

    dp3 = dp.reshape(bsz, lp, NP)
    dp3, dqk3, dgate = _mlstm_bwd(dp3, qk3, p3, pg3, dhm.reshape(bsz, lp, d), cs, ns, ms)
    dp3, g["qk_conv_w"], g["qk_conv_b"] = _conv_bwd(dp3, p3, dqk3, pre3, w["qk_conv_w"])
    dz, dys5 = _glu_bwd(dys5g, z, y_s5.reshape(r, sw), w["s5_w_glu"], lp)
    g["s5_w_glu"] = _mm_tn(gy, dz, name="dw_glu", split=w["s5_w_glu"].shape[0])
    tok = ready(("s5_w_glu", "m_w_out", "w_o"), g)
    dp3, dbk, dcre, dcim, da, g["s5_d"] = _s5_bwd(dp3, p3, dys5.reshape(bsz, lp, sw), xs, bk, cre, cim, apow_rev,
                                                 _tie(w["s5_d"], tok))
    dbb_re_t, dbb_im_t, g["s5_c_re"], g["s5_c_im"], da_re, da_im = _s5_block_grads(dbk, dcre, dcim, da)
    g["s5_lambda_re"], g["s5_lambda_im"], g["s5_log_dt"], gb_re_t, gb_im_t = _s5_prep_bwd(
        w["s5_lambda_re"], w["s5_lambda_im"], w["s5_log_dt"], b_re_t, b_im_t, da_re, da_im, dbb_re_t, dbb_im_t)
    g["s5_b_re"] = jnp.transpose(gb_re_t, (1, 2, 0))
    g["s5_b_im"] = jnp.transpose(gb_im_t, (1, 2, 0))

    dp3 = lax.dynamic_update_slice(dp3, dgate, (0, 0, G_OFF))
    dp = dp3.reshape(r, NP)
    g["w_in"], g["b_in"] = _mm_tn(h0b, dp, name="dw_in", colsum=True)
    tok = ready(("w_in",), g)
    dpw = _mm_nt(dp, w["w_in"], lp, name="dh0", dep=tok)
    grad_x, g["ln0_g"], g["ln0_b"], g["meta_tokens"] = _ln0_bwd(x, w["meta_tokens"], dr1, dpw, w["ln0_g"])
    return loss, grad_x, g


_ANY = pl.BlockSpec(memory_space=pl.ANY)
_MESH = pl.DeviceIdType.MESH


def _place():
    return lax.axis_index("x"), lax.axis_index("y"), lax.axis_index("c")


def _gather_chips(shards):
    n = len(shards)

    def body(*refs):
        ins, outs = refs[:n], refs[n:2 * n]
        send, recv, loc = refs[2 * n:]
        x, y, c = _place()
        me = 2 * x + y
        peers = [(1 - x, y), (x, 1 - y), (1 - x, 1 - y)]

        def rc(a, k, slot):
            px, py = peers[k]
            return pltpu.make_async_remote_copy(src_ref=ins[a], dst_ref=outs[a].at[slot], send_sem=send.at[a, k],
                                                recv_sem=recv.at[a, k], device_id=(px, py, c), device_id_type=_MESH)

        own = [pltpu.make_async_copy(ins[a], outs[a].at[me], loc.at[a]) for a in range(n)]
        for cp in own:
            cp.start()
        out = [rc(a, k, me) for a in range(n) for k in range(3)]
        for cp in out:
            cp.start()
        for a in range(n):
            for k in range(3):
                rc(a, k, 2 * peers[k][0] + peers[k][1]).wait_recv()
        for cp in out:
            cp.wait_send()
        for cp in own:
            cp.wait()

    return _pcall(
        body, name="gather_chips", in_specs=[_ANY] * n, out_specs=[_ANY] * n,
        out_shape=[jax.ShapeDtypeStruct((4,) + s.shape, s.dtype) for s in shards],
        scratch_shapes=[pltpu.SemaphoreType.DMA((n, 3)), pltpu.SemaphoreType.DMA((n, 3)), pltpu.SemaphoreType.DMA((n,))],
    )(*shards)


_HBM = pl.BlockSpec(memory_space=pltpu.HBM)
_SEM = pl.BlockSpec(memory_space=pltpu.SEMAPHORE)
_EFFECT = pltpu.SideEffectType.DATAFLOW_SIDE_EFFECTING


def _xchg_copies(srcs, lands, send, recv, scatter):
    x, y, c = _place()
    me = 2 * x + y
    if scatter == "pair":
        return [pltpu.make_async_remote_copy(src_ref=srcs[a], dst_ref=lands[a], send_sem=send.at[3 * a],
                                             recv_sem=recv.at[3 * a], device_id=(x, y, 1 - c), device_id_type=_MESH)
                for a in range(len(srcs))]
    peers = [(1 - x, y), (x, 1 - y), (1 - x, 1 - y)]
    out = []
    for a in range(len(srcs)):
        for k, (px, py) in enumerate(peers):
            src = srcs[a].at[2 * px + py] if scatter else srcs[a]
            dst = lands[a].at[k] if scatter else lands[a].at[me]
            out.append(pltpu.make_async_remote_copy(src_ref=src, dst_ref=dst, send_sem=send.at[3 * a + k],
                                                    recv_sem=recv.at[3 * a + k], device_id=(px, py, c),
                                                    device_id_type=_MESH))
    return out


def _xchg_start(srcs, lands, *, name, scatter, dep=None):
    n = len(srcs)
    deps = [] if dep is None else [dep]
    nd = len(deps)

    def body(*refs):
        send, recv = refs[2 * n + nd], refs[2 * n + nd + 1]
        for cp in _xchg_copies(refs[:n], refs[n:2 * n], send, recv, scatter):
            cp.start()
        refs[-1][...] = jnp.zeros_like(refs[-1])

    hbm = lambda a: pltpu.HBM(a.shape, a.dtype)
    con = lambda a: pltpu.with_memory_space_constraint(a, pltpu.HBM)
    res = _pcall(
        body, name=name, in_specs=[_HBM] * (2 * n) + [_ANY] * nd,
        out_specs=[_SEM, _SEM] + [_HBM] * (2 * n) + [pl.BlockSpec(memory_space=pltpu.VMEM)],
        out_shape=[pltpu.SemaphoreType.DMA((3 * n,)), pltpu.SemaphoreType.DMA((3 * n,))]
        + [hbm(a) for a in srcs] + [hbm(a) for a in lands] + [jax.ShapeDtypeStruct((8, 128), F32)],
        input_output_aliases={i: 2 + i for i in range(2 * n)},
        compiler_params=pltpu.CompilerParams(has_side_effects=_EFFECT),
    )(*[con(a) for a in srcs], *[con(a) for a in lands], *deps)
    return res[0], res[1], list(res[2:2 + n]), list(res[2 + n:2 + 2 * n]), res[-1]


def _xchg_wait(send, recv, srcs, lands, after, *, name, scatter):
    n = len(srcs)
    afters = list(after) if isinstance(after, (list, tuple)) else [after]

    def body(*refs):
        s_ref, r_ref = refs[2 * n], refs[2 * n + 1]
        for cp in _xchg_copies(refs[:n], refs[n:2 * n], s_ref, r_ref, scatter):
            cp.wait_send()
            cp.wait_recv()

    hbm = lambda a: pltpu.HBM(a.shape, a.dtype)
    res = _pcall(
        body, name=name, in_specs=[_HBM] * (2 * n) + [_SEM, _SEM] + [_ANY] * len(afters),
        out_specs=[_HBM] * (2 * n),
        out_shape=[hbm(a) for a in srcs] + [hbm(a) for a in lands],
        input_output_aliases={i: i for i in range(2 * n)},
        compiler_params=pltpu.CompilerParams(has_side_effects=_EFFECT),
    )(*srcs, *lands, send, recv, *afters)
    return list(res[:n]), list(res[n:])


def _swap_cores(arrs, name="swap_cores"):
    n = len(arrs)

    def body(*refs):
        ins, outs = refs[:n], refs[n:2 * n]
        send, recv = refs[2 * n:]
        x, y, c = _place()
        cps = [pltpu.make_async_remote_copy(src_ref=ins[a], dst_ref=outs[a], send_sem=send.at[a], recv_sem=recv.at[a],
                                            device_id=(x, y, 1 - c), device_id_type=_MESH) for a in range(n)]
        for cp in cps:
            cp.start()
        for cp in cps:
            cp.wait_recv()
        for cp in cps:
            cp.wait_send()

    return _pcall(
        body, name=name, in_specs=[_ANY] * n, out_specs=[_ANY] * n,
        out_shape=[jax.ShapeDtypeStruct(s.shape, s.dtype) for s in arrs],
        scratch_shapes=[pltpu.SemaphoreType.DMA((n,)), pltpu.SemaphoreType.DMA((n,))],
    )(*arrs)


def _allreduce_small(v, dep=None):
    rows = v.shape[0]
    half = rows // 2
    assert half % 8 == 0 and 2 * half == rows
    deps = [] if dep is None else [dep]

    def body(v_ref, *rest):
        out_ref, sib_ref, pair_ref, slots_ref, send, recv = rest[len(deps):]
        x, y, c = _place()
        chip = 2 * x + y
        sibling = (x, y, 1 - c)
        peers = [(1 - x, y), (x, 1 - y), (1 - x, 1 - y)]
        mine = pl.ds(pl.multiple_of(c * half, 8), half)

        first = pltpu.make_async_remote_copy(src_ref=v_ref, dst_ref=sib_ref, send_sem=send.at[0], recv_sem=recv.at[0],
                                             device_id=sibling, device_id_type=_MESH)
        first.start()
        first.wait_recv()
        pair_ref[...] = v_ref[...] + sib_ref[...]
        slots_ref[chip] = pair_ref[mine, :]
        cross = [pltpu.make_async_remote_copy(src_ref=pair_ref.at[mine], dst_ref=slots_ref.at[chip],
                                              send_sem=send.at[1 + k], recv_sem=recv.at[1 + k],
                                              device_id=(px, py, c), device_id_type=_MESH)
                 for k, (px, py) in enumerate(peers)]
        for cp in cross:
            cp.start()
        for cp in cross:
            cp.wait_recv()
        out_ref[mine, :] = ((slots_ref[0] + slots_ref[1]) + slots_ref[2]) + slots_ref[3]
        last = pltpu.make_async_remote_copy(src_ref=out_ref.at[mine], dst_ref=out_ref.at[mine], send_sem=send.at[4],
                                            recv_sem=recv.at[4], device_id=sibling, device_id_type=_MESH)
        last.start()
        last.wait_recv()
        first.wait_send()
        for cp in cross:
            cp.wait_send()
        last.wait_send()

    vm = pl.BlockSpec(memory_space=pltpu.VMEM)
    return _pcall(
        body, name="allreduce_small", in_specs=[vm] + [_ANY] * len(deps), out_specs=vm,
        out_shape=jax.ShapeDtypeStruct((rows, 128), F32),
        scratch_shapes=[pltpu.VMEM((rows, 128), F32), pltpu.VMEM((rows, 128), F32), pltpu.VMEM((4, half, 128), F32),
                        pltpu.SemaphoreType.DMA((5,)), pltpu.SemaphoreType.DMA((5,))],
        compiler_params=_cp(None, 40),
    )(v, *deps)


def _sum_slots(parts, land, chip):
    ns, rows, cols = land.shape
    tm = _row_tile(rows, 256, 8)

    def body(chip_ref, own_ref, a_ref, o_ref):
        o_ref[...] = ((own_ref[...] + a_ref[0]) + a_ref[1]) + a_ref[2]

    return _pcall(
        body, name="sum_slots",
        grid_spec=pltpu.PrefetchScalarGridSpec(
            num_scalar_prefetch=1, grid=(rows // tm,),
            in_specs=[pl.BlockSpec((None, tm, cols), lambda i, c: (c[0], i, 0)),
                      pl.BlockSpec((ns, tm, cols), lambda i, c: (0, i, 0))],
            out_specs=pl.BlockSpec((tm, cols), lambda i, c: (i, 0))),
        out_shape=jax.ShapeDtypeStruct((rows, cols), F32),
        compiler_params=_cp(("parallel",), 40),
    )(jnp.reshape(chip, (1,)).astype(jnp.int32), parts, land)


def _adamw(w, m, v, g0, g1=None):
    rows, cols = w.shape[-2:]
    lead = w.ndim == 3
    tm = _row_tile(rows, max(8, (1 << 20) // (4 * cols)), 8)
    c1 = 1.0 - ADAM_B1 ** ADAM_STEP
    c2 = 1.0 - ADAM_B2 ** ADAM_STEP
    two = g1 is not None

    def body(*refs):
        w_ref, m_ref, v_ref, g0_ref = refs[:4]
        g_ref, d_ref, nm_ref, nv_ref = refs[-4:]
        g = g0_ref[...]
        if two:
            g = g + refs[4][...]
        nm = ADAM_B1 * m_ref[...] + (1.0 - ADAM_B1) * g
        nv = ADAM_B2 * v_ref[...] + (1.0 - ADAM_B2) * (g * g)
        g_ref[...] = g
        nm_ref[...] = nm
        nv_ref[...] = nv
        d_ref[...] = -ADAM_LR * ((nm / c1) / (jnp.sqrt(nv / c2) + ADAM_EPS) + ADAM_WD * w_ref[...])

    blk = pl.BlockSpec((tm, cols), lambda i: (i, 0))
    wblk = pl.BlockSpec((None, tm, cols), lambda i: (0, i, 0)) if lead else blk
    ins = [w, m, v, g0] + ([g1] if two else [])
    return _pcall(
        body, name="adamw", grid=(rows // tm,), in_specs=[wblk] * 3 + [blk] * (len(ins) - 3), out_specs=[wblk] * 4,
        out_shape=[jax.ShapeDtypeStruct(w.shape, F32)] * 4,
        compiler_params=_cp(("parallel",), 40),
    )(*ins)


_BIG = ("w_in", "s5_w_glu", "m_w_out", "w_o", "w_up", "w_down")
_SMALL = ("ln0_g", "ln0_b", "b_in", "qk_conv_b", "s5_lambda_re", "s5_lambda_im", "s5_log_dt", "s5_b_re", "s5_b_im",
          "s5_c_re", "s5_c_im", "s5_d", "m_norm_g", "ln1_g", "ln1_b", "b_up", "ln2_g", "ln2_b")
_SMALL_SHARDED = ("meta_tokens", "qk_conv_w")
_ORDER = ("meta_tokens", "ln0_g", "ln0_b", "w_in", "b_in", "qk_conv_w", "qk_conv_b", "s5_lambda_re", "s5_lambda_im",
          "s5_log_dt", "s5_b_re", "s5_b_im", "s5_c_re", "s5_c_im", "s5_d", "s5_w_glu", "m_norm_g", "m_w_out", "w_o",
          "ln1_g", "ln1_b", "w_up", "b_up", "w_down", "ln2_g", "ln2_b")


def _pack(arrs):
    flat = jnp.concatenate([a.reshape(-1) for a in arrs])
    n = flat.shape[0]
    rows = -(-n // 2048) * 16
    return jnp.pad(flat, (0, rows * 128 - n)).reshape(rows, 128)


def _unpack(packed, shapes):
    flat = packed.reshape(-1)
    out, off = [], 0
    for s in shapes:
        n = math.prod(s)
        out.append(flat[off:off + n].reshape(s))
        off += n
    return out


def kernel(x, meta_tokens, ln0_g, ln0_b, w_in, b_in, qk_conv_w, qk_conv_b, s5_lambda_re, s5_lambda_im, s5_log_dt, s5_b_re, s5_b_im, s5_c_re, s5_c_im, s5_d, s5_w_glu, m_norm_g, m_w_out, w_o, ln1_g, ln1_b, w_up, b_up, w_down, ln2_g, ln2_b, loss_target, m_meta_tokens, m_ln0_g, m_ln0_b, m_w_in, m_b_in, m_qk_conv_w, m_qk_conv_b, m_s5_lambda_re, m_s5_lambda_im, m_s5_log_dt, m_s5_b_re, m_s5_b_im, m_s5_c_re, m_s5_c_im, m_s5_d, m_s5_w_glu, m_m_norm_g, m_m_w_out, m_w_o, m_ln1_g, m_ln1_b, m_w_up, m_b_up, m_w_down, m_ln2_g, m_ln2_b, v_meta_tokens, v_ln0_g, v_ln0_b, v_w_in, v_b_in, v_qk_conv_w, v_qk_conv_b, v_s5_lambda_re, v_s5_lambda_im, v_s5_log_dt, v_s5_b_re, v_s5_b_im, v_s5_c_re, v_s5_c_im, v_s5_d, v_s5_w_glu, v_m_norm_g, v_m_w_out, v_w_o, v_ln1_g, v_ln1_b, v_w_up, v_b_up, v_w_down, v_ln2_g, v_ln2_b):
    wts = dict(meta_tokens=meta_tokens, ln0_g=ln0_g, ln0_b=ln0_b, w_in=w_in, b_in=b_in, qk_conv_w=qk_conv_w,
               qk_conv_b=qk_conv_b, s5_lambda_re=s5_lambda_re, s5_lambda_im=s5_lambda_im, s5_log_dt=s5_log_dt,
               s5_b_re=s5_b_re, s5_b_im=s5_b_im, s5_c_re=s5_c_re, s5_c_im=s5_c_im, s5_d=s5_d, s5_w_glu=s5_w_glu,
               m_norm_g=m_norm_g, m_w_out=m_w_out, w_o=w_o, ln1_g=ln1_g, ln1_b=ln1_b, w_up=w_up, b_up=b_up,
               w_down=w_down, ln2_g=ln2_g, ln2_b=ln2_b)
    mom = dict(meta_tokens=m_meta_tokens, ln0_g=m_ln0_g, ln0_b=m_ln0_b, w_in=m_w_in, b_in=m_b_in, qk_conv_w=m_qk_conv_w,
               qk_conv_b=m_qk_conv_b, s5_lambda_re=m_s5_lambda_re, s5_lambda_im=m_s5_lambda_im, s5_log_dt=m_s5_log_dt,
               s5_b_re=m_s5_b_re, s5_b_im=m_s5_b_im, s5_c_re=m_s5_c_re, s5_c_im=m_s5_c_im, s5_d=m_s5_d,
               s5_w_glu=m_s5_w_glu, m_norm_g=m_m_norm_g, m_w_out=m_m_w_out, w_o=m_w_o, ln1_g=m_ln1_g, ln1_b=m_ln1_b,
               w_up=m_w_up, b_up=m_b_up, w_down=m_w_down, ln2_g=m_ln2_g, ln2_b=m_ln2_b)
    var = dict(meta_tokens=v_meta_tokens, ln0_g=v_ln0_g, ln0_b=v_ln0_b, w_in=v_w_in, b_in=v_b_in, qk_conv_w=v_qk_conv_w,
               qk_conv_b=v_qk_conv_b, s5_lambda_re=v_s5_lambda_re, s5_lambda_im=v_s5_lambda_im, s5_log_dt=v_s5_log_dt,
               s5_b_re=v_s5_b_re, s5_b_im=v_s5_b_im, s5_c_re=v_s5_c_re, s5_c_im=v_s5_c_im, s5_d=v_s5_d,
               s5_w_glu=v_s5_w_glu, m_norm_g=v_m_norm_g, m_w_out=v_m_w_out, w_o=v_w_o, ln1_g=v_ln1_g, ln1_b=v_ln1_b,
               w_up=v_w_up, b_up=v_b_up, w_down=v_w_down, ln2_g=v_ln2_g, ln2_b=v_ln2_b)
    d = x.shape[-1]
    chip = 2 * lax.axis_index("x") + lax.axis_index("y")

    gw = dict(zip(_SMALL_SHARDED, _gather_chips([meta_tokens, qk_conv_w[0]])))
    core = lax.axis_index("c")
    half_rows = w_in.shape[1] // 2
    own_w_in = lax.dynamic_slice_in_dim(_bf(w_in[0]), core * half_rows, half_rows, axis=0)
    fsend, frecv, fsrc, fland, ftok = _xchg_start([own_w_in], [lax.empty((4,) + own_w_in.shape, BF16)],
                                                  name="gather_w_in_start", scatter=False, dep=gw["qk_conv_w"])
    late_names = tuple(n for n in _BIG if n != "w_in")
    cat = lambda a: jnp.transpose(a, (1, 0, 2)).reshape(a.shape[1], 4 * a.shape[2])
    w = dict(
        meta_tokens=cat(gw["meta_tokens"]), ln0_g=ln0_g[None], ln0_b=_tie(ln0_b[None], ftok),
        qk_conv_w=cat(gw["qk_conv_w"]), qk_conv_b=qk_conv_b,
        s5_lambda_re=s5_lambda_re[0], s5_lambda_im=s5_lambda_im[0], s5_log_dt=s5_log_dt[0][:, None],
        s5_b_re=s5_b_re[0], s5_b_im=s5_b_im[0], s5_c_re=s5_c_re[0], s5_c_im=s5_c_im[0], s5_d=s5_d,
        m_norm_g=m_norm_g, ln1_g=ln1_g, ln1_b=ln1_b, b_up=b_up, ln2_g=ln2_g, ln2_b=ln2_b)
    in_flight = {}

    def place_own(src, land):
        return lax.dynamic_update_slice(land, src[None], (chip,) + (0,) * src.ndim)

    small_names = _SMALL + _SMALL_SHARDED

    def view(n, a):
        return jnp.swapaxes(a, -1, -2) if n in ("s5_b_re", "s5_b_im") else a

    small_wmv = [_pack([view(n, dct[n]) for n in small_names]) for dct in (wts, mom, var)]

    def early(after):
        src, land = _xchg_wait(fsend, frecv, fsrc, fland, tuple(after) + tuple(small_wmv), name="gather_w_in_wait",
                               scatter=False)
        late_src = [_bf(wts[n][0]) for n in late_names]
        st = _xchg_start(late_src, [lax.empty((4,) + a.shape, a.dtype) for a in late_src], name="gather_late_start",
                         scatter=False, dep=src[0])
        in_flight["late"] = st[:4]
        mine = place_own(src[0], land[0])
        theirs = _swap_cores([mine], name="swap_w_in")[0]
        top, bottom = jnp.where(core == 0, mine, theirs), jnp.where(core == 0, theirs, mine)
        full = jnp.concatenate([top, bottom], axis=1)
        return dict(w_in=_w_in_from_slots(full, IN_CHUNK), b_in=_tie(_to_pad_cols(b_in), st[4]))

    def late(after):
        src, land = _xchg_wait(*in_flight["late"], after, name="gather_late_wait", scatter=False)
        full = {n: place_own(s, ld) for n, s, ld in zip(late_names, src, land)}
        return dict(s5_w_glu=full["s5_w_glu"], m_w_out=full["m_w_out"].reshape(d, d), w_o=full["w_o"].reshape(d, d),
                    w_up=full["w_up"], w_down=full["w_down"].reshape(4 * d, d))

    flying = []

    def ready(names, g):
        parts = dict(
            w_in=lambda: _slots_from_w_in(g["w_in"][0]), s5_w_glu=lambda: g["s5_w_glu"],
            m_w_out=lambda: g["m_w_out"].reshape(4, d // 4, d), w_o=lambda: g["w_o"].reshape(4, d // 4, d),
            w_up=lambda: g["w_up"], w_down=lambda: g["w_down"].reshape(4, d, d))
        src = [parts[n]() for n in names]
        land = [lax.empty((3,) + a.shape[1:], a.dtype) for a in src]
        st = _xchg_start(src, land, name="scatter_" + names[0] + "_start", scatter=True)
        flying.append((names,) + st[:4])
        return st[4]

    loss, grad_x, g = _local_step(x, loss_target, w, early, late, ready)
    g["b_in"] = _from_pad_cols(g["b_in"])

    res = {}

    def flat(a):
        return jnp.swapaxes(a, -1, -2).reshape(a.shape[:-2] + (-1, 128))

    def unflat(y, shape):
        return jnp.swapaxes(y.reshape(shape[:-2] + (shape[-1], shape[-2])), -1, -2)

    def finish(groups, after, tag):
        mine = {}
        for names, send, recv, src, land in groups:
            src, land = _xchg_wait(send, recv, src, land, after, name="scatter_" + names[0] + "_wait", scatter=True)
            for n, s, ld in zip(names, src, land):
                mine[n] = _sum_slots(s, ld, chip)
        theirs = _swap_cores(list(mine.values()), name="swap_cores_" + tag)
        for n, t in zip(mine, theirs):
            if n == "w_in":
                res[n] = [unflat(r, wts[n].shape) for r in _adamw(flat(wts[n]), flat(mom[n]), flat(var[n]),
                                                                  flat(mine[n]), flat(t))]
            else:
                res[n] = _adamw(wts[n], mom[n], var[n], mine[n], t)

    a_mine = {}
    for names, send, recv, src, land in flying[:-1]:
        src, land = _xchg_wait(send, recv, src, land, g["ln0_g"], name="scatter_" + names[0] + "_wait", scatter=True)
        for n, s, ld in zip(names, src, land):
            a_mine[n] = _sum_slots(s, ld, chip)
    a_src = list(a_mine.values())
    a_st = _xchg_start(a_src, [lax.empty(a.shape, a.dtype) for a in a_src], name="swap_a_start", scatter="pair")

    small_shapes = [(1, 128)] + [view(n, wts[n]).shape for n in _SMALL] + [g[n].shape for n in _SMALL_SHARDED]
    packed = _pack([loss] + [view(n, g[n]) for n in _SMALL] + [g[n] for n in _SMALL_SHARDED])
    tot = _unpack(_allreduce_small(packed, dep=a_st[4]), small_shapes)
    loss_out = tot[0][0, 0]
    gsm = dict(zip(_SMALL + _SMALL_SHARDED, tot[1:]))
    for n in _SMALL_SHARDED:
        cols = wts[n].shape[-1]
        gsm[n] = lax.dynamic_slice_in_dim(gsm[n], chip * cols, cols, axis=1).reshape(wts[n].shape)

    names = small_names
    shapes = [view(n, wts[n]).shape for n in names]
    small_out = _adamw(*small_wmv, _pack([gsm[n] for n in names]))
    small_res = [_unpack(r, shapes) for r in small_out]
    for j, n in enumerate(names):
        res[n] = [view(n, small_res[q][j]) for q in range(4)]
    names, send, recv, src, land = flying[-1]
    src, land = _xchg_wait(send, recv, src, land, small_out[0], name="scatter_w_in_wait", scatter=True)
    b_src = [_sum_slots(src[0], land[0], chip)]
    b_st = _xchg_start(b_src, [lax.empty(b_src[0].shape, F32)], name="swap_b_start", scatter="pair")
    a_src, a_land = _xchg_wait(a_st[0], a_st[1], a_st[2], a_st[3], b_st[4], name="swap_a_wait", scatter="pair")
    for n, m_, t_ in zip(a_mine, a_src, a_land):
        res[n] = _adamw(wts[n], mom[n], var[n], m_, t_)
    b_src, b_land = _xchg_wait(b_st[0], b_st[1], b_st[2], b_st[3], res["w_o"][3], name="swap_b_wait", scatter="pair")
    res["w_in"] = [unflat(r, w_in.shape) for r in _adamw(flat(w_in), flat(m_w_in), flat(v_w_in), flat(b_src[0]),
                                                         flat(b_land[0]))]

    return (loss_out, grad_x, *[res[n][0] for n in _ORDER], *[res[n][1] for n in _ORDER],
            *[res[n][2] for n in _ORDER], *[res[n][3] for n in _ORDER])
```
